```python
import jax, jax.numpy as jnp
from jax import lax
import numpy as np

D_MODEL = 1024
BATCH = 8
SEQ = 4096
DEPTH = 1

CHUNK = 64
Q_BLOCK = 128
MLA_HEADS = 8
QK_NOPE_DIM = 64
QK_ROPE_DIM = 32
V_HEAD_DIM = 64
Q_LORA_RANK = 384
KV_LORA_RANK = 256
MLA_WIDTH = MLA_HEADS * V_HEAD_DIM
ROPE_THETA = 10000.0
POOL_WINDOWS = (2, 4, 8, 16)
POOL_GROUPS = len(POOL_WINDOWS)
POOL_WIDTH = D_MODEL // 2
POOL_GROUP_DIM = POOL_WIDTH // POOL_GROUPS
N_BRANCHES = 2
EPS = 1e-6

IN_SIZES = (Q_LORA_RANK, KV_LORA_RANK, QK_ROPE_DIM, MLA_WIDTH, POOL_WIDTH, POOL_WIDTH, N_BRANCHES * D_MODEL)
IN_SPLITS = tuple(int(s) for s in np.cumsum(IN_SIZES)[:-1])
IN_TOTAL = int(sum(IN_SIZES))

kernel_name = "hybrid_mla_pool_gated_block"


def rms_norm(x, g):
    xf = x.astype(jnp.float32)
    y = xf * lax.rsqrt(jnp.mean(xf * xf, axis=-1, keepdims=True) + EPS)
    return (y * g.astype(jnp.float32)).astype(x.dtype)


def rope_tables(seq):
    half = QK_ROPE_DIM // 2
    inv_freq = ROPE_THETA ** (-jnp.arange(half, dtype=jnp.float32) / half)
    ang = jnp.arange(seq, dtype=jnp.float32)[:, None] * inv_freq[None, :]
    return jnp.cos(ang), jnp.sin(ang)


def apply_rope(x, cos, sin):
    xf = x.astype(jnp.float32)
    x1, x2 = jnp.split(xf, 2, axis=-1)
    out = jnp.concatenate([x1 * cos - x2 * sin, x1 * sin + x2 * cos], axis=-1)
    return out.astype(x.dtype)


def mla_attention(zq, zkv, zkr, q_norm, w_uq, kv_norm, w_ukv):
    b, s, _ = zq.shape
    cos, sin = rope_tables(s)
    c_q = rms_norm(zq, q_norm)
    q = jnp.einsum('bsr,rhd->bshd', c_q, w_uq)
    q_nope = q[..., :QK_NOPE_DIM]
    q_rope = apply_rope(q[..., QK_NOPE_DIM:], cos[None, :, None, :], sin[None, :, None, :])
    c_kv = rms_norm(zkv, kv_norm)
    kv = jnp.einsum('bsr,rhd->bshd', c_kv, w_ukv)
    k_nope = kv[..., :QK_NOPE_DIM]
    v = kv[..., QK_NOPE_DIM:]
    k_rope = apply_rope(zkr, cos[None], sin[None])
    scale = (QK_NOPE_DIM + QK_ROPE_DIM) ** -0.5
    key_chunk = jnp.arange(s) // CHUNK
    n_blocks = s // Q_BLOCK

    def one_block(i):
        start = i * Q_BLOCK
        qn = lax.dynamic_slice_in_dim(q_nope, start, Q_BLOCK, axis=1)
        qr = lax.dynamic_slice_in_dim(q_rope, start, Q_BLOCK, axis=1)
        sc = (jnp.einsum('bqhd,bkhd->bhqk', qn, k_nope).astype(jnp.float32)
              + jnp.einsum('bqhr,bkr->bhqk', qr, k_rope).astype(jnp.float32)) * scale
        q_chunk = (start + jnp.arange(Q_BLOCK)) // CHUNK
        mask = key_chunk[None, :] <= q_chunk[:, None]
        sc = jnp.where(mask[None, None], sc, -jnp.inf)
        p = jax.nn.softmax(sc, axis=-1).astype(v.dtype)
        return jnp.einsum('bhqk,bkhd->bqhd', p, v)

    o = lax.map(one_block, jnp.arange(n_blocks))
    o = jnp.transpose(o, (1, 0, 2, 3, 4)).reshape(b, s, MLA_WIDTH)
    return o


def multiscale_pool(u, pool_w, pool_scale):
    b, s, _ = u.shape
    uf = u.astype(jnp.float32)
    csum = jnp.cumsum(uf, axis=1)
    t = jnp.arange(s)
    outs = []
    for gi, w in enumerate(POOL_WINDOWS):
        sl = slice(gi * POOL_GROUP_DIM, (gi + 1) * POOL_GROUP_DIM)
        cg = csum[..., sl]
        shifted = jnp.pad(cg, ((0, 0), (w, 0), (0, 0)))[:, :s]
        count = jnp.minimum(t + 1, w).astype(jnp.float32)[None, :, None]
        d = (cg - shifted) / count - uf[..., sl]
        outs.append(jnp.einsum('bsc,cd->bsd', d, pool_w[gi].astype(jnp.float32)))
    y = jnp.concatenate(outs, axis=-1) * pool_scale.astype(jnp.float32)
    return y.astype(u.dtype)


def _fwd_setup_inputs(seed: int = 0) -> dict:
    key = jax.random.key(seed)
    ks = jax.random.split(key, 14)
    f = jnp.float32
    def nrm(k, shape, fan_in):
        return jax.random.normal(k, shape, f) * (fan_in ** -0.5)
    return {
        "x": jax.random.normal(ks[0], (BATCH, SEQ, D_MODEL), f),
        "norm_in": 1.0 + 0.02 * jax.random.normal(ks[1], (D_MODEL,), f),
        "w_in": nrm(ks[2], (D_MODEL, IN_TOTAL), D_MODEL),
        "q_norm": 1.0 + 0.02 * jax.random.normal(ks[3], (Q_LORA_RANK,), f),
        "w_uq": nrm(ks[4], (Q_LORA_RANK, MLA_HEADS, QK_NOPE_DIM + QK_ROPE_DIM), Q_LORA_RANK),
        "kv_norm": 1.0 + 0.02 * jax.random.normal(ks[5], (KV_LORA_RANK,), f),
        "w_ukv": nrm(ks[6], (KV_LORA_RANK, MLA_HEADS, QK_NOPE_DIM + V_HEAD_DIM), KV_LORA_RANK),
        "pool_w": nrm(ks[7], (POOL_GROUPS, POOL_GROUP_DIM, POOL_GROUP_DIM), POOL_GROUP_DIM),
        "pool_scale": 1.0 + 0.02 * jax.random.normal(ks[8], (POOL_WIDTH,), f),
        "w_branch_attn": nrm(ks[9], (MLA_WIDTH, D_MODEL), MLA_WIDTH),
        "w_branch_pool": nrm(ks[10], (POOL_WIDTH, D_MODEL), POOL_WIDTH),
        "w_out": nrm(ks[11], (D_MODEL, D_MODEL), D_MODEL),
        "norm_final": 1.0 + 0.02 * jax.random.normal(ks[12], (D_MODEL,), f),
    }


def _fwd_reference(x, norm_in, w_in, q_norm, w_uq, kv_norm, w_ukv, pool_w, pool_scale,
              w_branch_attn, w_branch_pool, w_out, norm_final):
    h = x
    for _ in range(DEPTH):
        hn = rms_norm(h, norm_in)
        z = jnp.einsum('bsd,de->bse', hn, w_in)
        zq, zkv, zkr, g_attn, u_pool, g_pool, g_merge = jnp.split(z, IN_SPLITS, axis=-1)
        y_attn = mla_attention(zq, zkv, zkr, q_norm, w_uq, kv_norm, w_ukv) * jax.nn.silu(g_attn)
        y_pool = multiscale_pool(u_pool, pool_w, pool_scale) * jax.nn.silu(g_pool)
        a = jnp.einsum('bsc,cd->bsd', y_attn, w_branch_attn)
        p = jnp.einsum('bsc,cd->bsd', y_pool, w_branch_pool)
        gate_a, gate_p = jnp.split(jax.nn.sigmoid(g_merge.astype(jnp.float32)).astype(h.dtype), 2, axis=-1)
        merged = gate_a * a + gate_p * p
        h = h + jnp.einsum('bsd,de->bse', merged, w_out)
    return rms_norm(h, norm_final)


import jax as _jax
import jax.numpy as _jnp

TWIN_FORMAT = 'train_step'
FWD_PARAMS = ['x', 'norm_in', 'w_in', 'q_norm', 'w_uq', 'kv_norm', 'w_ukv', 'pool_w', 'pool_scale', 'w_branch_attn', 'w_branch_pool', 'w_out', 'norm_final']
TWIN_WEIGHTS = ['norm_in', 'w_in', 'q_norm', 'w_uq', 'kv_norm', 'w_ukv', 'pool_w', 'pool_scale', 'w_branch_attn', 'w_branch_pool', 'w_out', 'norm_final']
TWIN_DIFF_INPUT = 'x'
TWIN_INPUTS = ['x', 'norm_in', 'w_in', 'q_norm', 'w_uq', 'kv_norm', 'w_ukv', 'pool_w', 'pool_scale', 'w_branch_attn', 'w_branch_pool', 'w_out', 'norm_final', 'loss_target', 'm_norm_in', 'm_w_in', 'm_q_norm', 'm_w_uq', 'm_kv_norm', 'm_w_ukv', 'm_pool_w', 'm_pool_scale', 'm_w_branch_attn', 'm_w_branch_pool', 'm_w_out', 'm_norm_final', 'v_norm_in', 'v_w_in', 'v_q_norm', 'v_w_uq', 'v_kv_norm', 'v_w_ukv', 'v_pool_w', 'v_pool_scale', 'v_w_branch_attn', 'v_w_branch_pool', 'v_w_out', 'v_norm_final']
TWIN_OUTPUTS = ['loss', 'grad_x', 'grad_norm_in', 'grad_w_in', 'grad_q_norm', 'grad_w_uq', 'grad_kv_norm', 'grad_w_ukv', 'grad_pool_w', 'grad_pool_scale', 'grad_w_branch_attn', 'grad_w_branch_pool', 'grad_w_out', 'grad_norm_final', 'delta_norm_in', 'delta_w_in', 'delta_q_norm', 'delta_w_uq', 'delta_kv_norm', 'delta_w_ukv', 'delta_pool_w', 'delta_pool_scale', 'delta_w_branch_attn', 'delta_w_branch_pool', 'delta_w_out', 'delta_norm_final', 'new_m_norm_in', 'new_m_w_in', 'new_m_q_norm', 'new_m_w_uq', 'new_m_kv_norm', 'new_m_w_ukv', 'new_m_pool_w', 'new_m_pool_scale', 'new_m_w_branch_attn', 'new_m_w_branch_pool', 'new_m_w_out', 'new_m_norm_final', 'new_v_norm_in', 'new_v_w_in', 'new_v_q_norm', 'new_v_w_uq', 'new_v_kv_norm', 'new_v_w_ukv', 'new_v_pool_w', 'new_v_pool_scale', 'new_v_w_branch_attn', 'new_v_w_branch_pool', 'new_v_w_out', 'new_v_norm_final']
TWIN_LEAF_KINDS = {'loss': 'loss', 'grad_x': 'grad_x', 'grad_norm_in': 'grad_w', 'grad_w_in': 'grad_w', 'grad_q_norm': 'grad_w', 'grad_w_uq': 'grad_w', 'grad_kv_norm': 'grad_w', 'grad_w_ukv': 'grad_w', 'grad_pool_w': 'grad_w', 'grad_pool_scale': 'grad_w', 'grad_w_branch_attn': 'grad_w', 'grad_w_branch_pool': 'grad_w', 'grad_w_out': 'grad_w', 'grad_norm_final': 'grad_w', 'delta_norm_in': 'delta_w', 'delta_w_in': 'delta_w', 'delta_q_norm': 'delta_w', 'delta_w_uq': 'delta_w', 'delta_kv_norm': 'delta_w', 'delta_w_ukv': 'delta_w', 'delta_pool_w': 'delta_w', 'delta_pool_scale': 'delta_w', 'delta_w_branch_attn': 'delta_w', 'delta_w_branch_pool': 'delta_w', 'delta_w_out': 'delta_w', 'delta_norm_final': 'delta_w', 'new_m_norm_in': 'new_m', 'new_m_w_in': 'new_m', 'new_m_q_norm': 'new_m', 'new_m_w_uq': 'new_m', 'new_m_kv_norm': 'new_m', 'new_m_w_ukv': 'new_m', 'new_m_pool_w': 'new_m', 'new_m_pool_scale': 'new_m', 'new_m_w_branch_attn': 'new_m', 'new_m_w_branch_pool': 'new_m', 'new_m_w_out': 'new_m', 'new_m_norm_final': 'new_m', 'new_v_norm_in': 'new_v', 'new_v_w_in': 'new_v', 'new_v_q_norm': 'new_v', 'new_v_w_uq': 'new_v', 'new_v_kv_norm': 'new_v', 'new_v_w_ukv': 'new_v', 'new_v_pool_w': 'new_v', 'new_v_pool_scale': 'new_v', 'new_v_w_branch_attn': 'new_v', 'new_v_w_branch_pool': 'new_v', 'new_v_w_out': 'new_v', 'new_v_norm_final': 'new_v'}


def _forward(args):
    return _fwd_reference(*[args[k] for k in FWD_PARAMS])


def _output_shape():
    out = _jax.eval_shape(lambda: _forward(_fwd_setup_inputs(0)))
    return out.shape, out.dtype

N_MICROBATCH = 1
ADAM_LR = 0.001
ADAM_B1 = 0.9
ADAM_B2 = 0.999
ADAM_EPS = 1e-08
ADAM_WD = 0.01
ADAM_STEP = 10
PER_EXAMPLE_BATCH_AXIS = {'x': 0, 'loss_target': 0}
SHARED_INPUTS = []
_WEIGHT_DTYPES = {'norm_in': _jnp.float32, 'w_in': _jnp.float32, 'q_norm': _jnp.float32, 'w_uq': _jnp.float32, 'kv_norm': _jnp.float32, 'w_ukv': _jnp.float32, 'pool_w': _jnp.float32, 'pool_scale': _jnp.float32, 'w_branch_attn': _jnp.float32, 'w_branch_pool': _jnp.float32, 'w_out': _jnp.float32, 'norm_final': _jnp.float32}
MOMENT_SCALE = {'norm_in': 7.620964e-02, 'w_in': 3.597904e-02, 'q_norm': 1.337516e-02, 'w_uq': 9.464471e-03, 'kv_norm': 2.303644e-02, 'w_ukv': 1.101585e-02, 'pool_w': 6.837753e-02, 'pool_scale': 7.005910e-02, 'w_branch_attn': 8.551660e-03, 'w_branch_pool': 4.825563e-02, 'w_out': 4.857604e-02, 'norm_final': 3.201303e+01}


def _to_microbatches(a, axis):
    t = _jnp.moveaxis(a, axis, 0)
    t = t.reshape((N_MICROBATCH, t.shape[0] // N_MICROBATCH) + t.shape[1:])
    return _jnp.moveaxis(t, 1, axis + 1)


def setup_inputs(seed: int = 0) -> dict:
    inp = _fwd_setup_inputs(seed)
    key = _jax.random.fold_in(_jax.random.key(seed), 7919)
    shape, _ = _output_shape()
    out = dict(inp)
    out["loss_target"] = _jax.random.normal(_jax.random.fold_in(key, 0), shape, _jnp.float32)
    for i, name in enumerate(TWIN_WEIGHTS):
        w = inp[name].astype(_jnp.float32)
        if MOMENT_SCALE is None:
            s = _jnp.sqrt(_jnp.mean(_jnp.square(w)) + 1e-30)
        else:
            s = MOMENT_SCALE[name]
        km, kv = _jax.random.split(_jax.random.fold_in(key, i + 1))
        out[name] = w
        out["m_" + name] = s * _jax.random.normal(km, w.shape, _jnp.float32)
        out["v_" + name] = (s * s) * _jax.random.uniform(kv, w.shape, _jnp.float32, 0.5, 1.5)
    if N_MICROBATCH > 1:
        for name, axis in PER_EXAMPLE_BATCH_AXIS.items():
            out[name] = _to_microbatches(out[name], axis)
    return {'x': out['x'], 'norm_in': out['norm_in'], 'w_in': out['w_in'], 'q_norm': out['q_norm'], 'w_uq': out['w_uq'], 'kv_norm': out['kv_norm'], 'w_ukv': out['w_ukv'], 'pool_w': out['pool_w'], 'pool_scale': out['pool_scale'], 'w_branch_attn': out['w_branch_attn'], 'w_branch_pool': out['w_branch_pool'], 'w_out': out['w_out'], 'norm_final': out['norm_final'], 'loss_target': out['loss_target'], 'm_norm_in': out['m_norm_in'], 'm_w_in': out['m_w_in'], 'm_q_norm': out['m_q_norm'], 'm_w_uq': out['m_w_uq'], 'm_kv_norm': out['m_kv_norm'], 'm_w_ukv': out['m_w_ukv'], 'm_pool_w': out['m_pool_w'], 'm_pool_scale': out['m_pool_scale'], 'm_w_branch_attn': out['m_w_branch_attn'], 'm_w_branch_pool': out['m_w_branch_pool'], 'm_w_out': out['m_w_out'], 'm_norm_final': out['m_norm_final'], 'v_norm_in': out['v_norm_in'], 'v_w_in': out['v_w_in'], 'v_q_norm': out['v_q_norm'], 'v_w_uq': out['v_w_uq'], 'v_kv_norm': out['v_kv_norm'], 'v_w_ukv': out['v_w_ukv'], 'v_pool_w': out['v_pool_w'], 'v_pool_scale': out['v_pool_scale'], 'v_w_branch_attn': out['v_w_branch_attn'], 'v_w_branch_pool': out['v_w_branch_pool'], 'v_w_out': out['v_w_out'], 'v_norm_final': out['v_norm_final']}


def _loss(weights, diff, rest, loss_target):
    with _jax.named_scope("forward"):
        args = {**rest, TWIN_DIFF_INPUT: diff, **{k: w.astype(_WEIGHT_DTYPES[k]) for k, w in weights.items()}}
        y = _forward(args)
    with _jax.named_scope("loss_head"):
        err = _jnp.square(y.astype(_jnp.float32) - loss_target)
        return 0.5 * _jnp.sum(_jnp.mean(err, axis=-1)) if err.ndim else 0.5 * err


def _adamw(w, g, m, v):
    m = ADAM_B1 * m + (1.0 - ADAM_B1) * g
    v = ADAM_B2 * v + (1.0 - ADAM_B2) * _jnp.square(g)
    m_hat = m / (1.0 - ADAM_B1 ** ADAM_STEP)
    v_hat = v / (1.0 - ADAM_B2 ** ADAM_STEP)
    delta = -ADAM_LR * (m_hat / (_jnp.sqrt(v_hat) + ADAM_EPS) + ADAM_WD * w)
    return delta, m, v


def reference(x, norm_in, w_in, q_norm, w_uq, kv_norm, w_ukv, pool_w, pool_scale, w_branch_attn, w_branch_pool, w_out, norm_final, loss_target, m_norm_in, m_w_in, m_q_norm, m_w_uq, m_kv_norm, m_w_ukv, m_pool_w, m_pool_scale, m_w_branch_attn, m_w_branch_pool, m_w_out, m_norm_final, v_norm_in, v_w_in, v_q_norm, v_w_uq, v_kv_norm, v_w_ukv, v_pool_w, v_pool_scale, v_w_branch_attn, v_w_branch_pool, v_w_out, v_norm_final):
    given = dict(x=x, norm_in=norm_in, w_in=w_in, q_norm=q_norm, w_uq=w_uq, kv_norm=kv_norm, w_ukv=w_ukv, pool_w=pool_w, pool_scale=pool_scale, w_branch_attn=w_branch_attn, w_branch_pool=w_branch_pool, w_out=w_out, norm_final=norm_final, loss_target=loss_target, m_norm_in=m_norm_in, m_w_in=m_w_in, m_q_norm=m_q_norm, m_w_uq=m_w_uq, m_kv_norm=m_kv_norm, m_w_ukv=m_w_ukv, m_pool_w=m_pool_w, m_pool_scale=m_pool_scale, m_w_branch_attn=m_w_branch_attn, m_w_branch_pool=m_w_branch_pool, m_w_out=m_w_out, m_norm_final=m_norm_final, v_norm_in=v_norm_in, v_w_in=v_w_in, v_q_norm=v_q_norm, v_w_uq=v_w_uq, v_kv_norm=v_kv_norm, v_w_ukv=v_w_ukv, v_pool_w=v_pool_w, v_pool_scale=v_pool_scale, v_w_branch_attn=v_w_branch_attn, v_w_branch_pool=v_w_branch_pool, v_w_out=v_w_out, v_norm_final=v_norm_final)
    weights = {n: given[n] for n in TWIN_WEIGHTS}
    shared = {n: given[n] for n in SHARED_INPUTS}
    per_example = {n: given[n] for n in ['x']}
    grad_fn = _jax.value_and_grad(_loss, argnums=(0, 1))

    def one_microbatch(ex, loss_target):
        ex = dict(ex)
        diff = ex.pop(TWIN_DIFF_INPUT)
        return grad_fn(weights, diff, {**shared, **ex}, loss_target)

    if N_MICROBATCH == 1:
        loss, (grad_w, grad_x) = one_microbatch(per_example, given["loss_target"])
    else:
        def body(carry, xs):
            loss_sum, grad_sum = carry
            l_k, (gw_k, gx_k) = one_microbatch(xs[0], xs[1])
            with _jax.named_scope("update"):
                return (loss_sum + l_k, _jax.tree.map(_jnp.add, grad_sum, gw_k)), gx_k

        init = (_jnp.zeros((), _jnp.float32), _jax.tree.map(_jnp.zeros_like, weights))
        (loss, grad_w), grad_x = _jax.lax.scan(body, init, (per_example, given["loss_target"]))
    with _jax.named_scope("update"):
        delta_w, new_m, new_v = {}, {}, {}
        for n in TWIN_WEIGHTS:
            delta_w[n], new_m[n], new_v[n] = _adamw(weights[n], grad_w[n], given["m_" + n], given["v_" + n])
    return (loss, grad_x, *[grad_w[n] for n in TWIN_WEIGHTS], *[delta_w[n] for n in TWIN_WEIGHTS],
            *[new_m[n] for n in TWIN_WEIGHTS], *[new_v[n] for n in TWIN_WEIGHTS])
```

```python
import functools

import jax
import jax.numpy as jnp
import numpy as np
from jax import lax
from jax.experimental import pallas as pl
from jax.experimental.pallas import tpu as pltpu

F32 = jnp.float32
BF16 = jnp.bfloat16
MESH = pl.DeviceIdType.MESH

D_MODEL = 1024
CHUNK = 64
MLA_HEADS = 8
QK_NOPE_DIM = 64
QK_ROPE_DIM = 32
V_HEAD_DIM = 64
Q_LORA_RANK = 384
KV_LORA_RANK = 256
MLA_WIDTH = MLA_HEADS * V_HEAD_DIM
ROPE_THETA = 10000.0
POOL_WINDOWS = (2, 4, 8, 16)
POOL_WIDTH = 512
POOL_GROUP_DIM = 128
POOL_HALO = 16
EPS = 1e-6
IN_TOTAL = 4256
HEAD_PAD = 128
ATT_SCALE = (QK_NOPE_DIM + QK_ROPE_DIM) ** -0.5

ADAM_LR = 0.001
ADAM_B1 = 0.9
ADAM_B2 = 0.999
ADAM_EPS = 1e-08
ADAM_WD = 0.01
ADAM_STEP = 10

N_CHIPS = 4
N_DEV = 8
LANES = 128
VMEM_LIMIT = 60 * 1024 * 1024

SEG_ZQ = (0, 384)
SEG_ZKV = (384, 640)
SEG_ZKR = (640, 768)
SEG_GATTN = (768, 1280)
SEG_U = (1280, 1792)
SEG_GPOOL = (1792, 2304)
SEG_GMERGE = (2304, 4352)
IN_PAD_TOTAL = 4352
ZKR_END = 672

SHARD_SHAPES = (
    ("w_in", (1024, 1064)),
    ("w_uq", (96, 8, 96)),
    ("w_ukv", (64, 8, 128)),
    ("w_branch_attn", (512, 256)),
    ("w_branch_pool", (512, 256)),
    ("w_out", (256, 1024)),
)
SHARD_ELEMS = sum(int(np.prod(s)) for _, s in SHARD_SHAPES)
PACK_CHUNK = 512
PACK_ROWS = -(-SHARD_ELEMS // (LANES * 2 * PACK_CHUNK)) * 2 * PACK_CHUNK
PACK_HALF = PACK_ROWS // 2

SMALL_SHAPES = (
    ("norm_in", (1024,)),
    ("q_norm", (384,)),
    ("kv_norm", (256,)),
    ("pool_scale", (512,)),
    ("norm_final", (1024,)),
    ("pool_w", (4, 128, 128)),
)
SMALL_ELEMS = sum(int(np.prod(s)) for _, s in SMALL_SHAPES)
SMALL_ROWS = -(-SMALL_ELEMS // (LANES * 8)) * 8


def _dot(a, b):
    return jnp.dot(a, b, preferred_element_type=F32)


def _dot_nt(a, b):
    return lax.dot_general(a, b, (((1,), (1,)), ((), ())), preferred_element_type=F32)


def _dot_tn(a, b):
    return lax.dot_general(a, b, (((0,), (0,)), ((), ())), preferred_element_type=F32)


def _sigmoid(x):
    return 1.0 / (1.0 + jnp.exp(-x))


def _colsum(x):
    return jnp.sum(x, axis=0, keepdims=True)


def _rms_fwd(x, g):
    r = lax.rsqrt(jnp.mean(x * x, axis=-1, keepdims=True) + EPS)
    xhat = x * r
    return xhat * g, xhat, r


def _rms_bwd(dy, xhat, r, g):
    dxhat = dy * g
    return r * (dxhat - xhat * jnp.mean(dxhat * xhat, axis=-1, keepdims=True))


def _rope(v, c, sa, sb):
    return v * c + pltpu.roll(v, 112, 1) * sa + pltpu.roll(v, 16, 1) * sb


def _unrope(d, c, sa, sb):
    return d * c + pltpu.roll(d * sa, 16, 1) + pltpu.roll(d * sb, 112, 1)


def _row_spec(tm, n):
    return pl.BlockSpec((tm, n), lambda i: (i, 0))


def _full_spec(shape):
    nd = len(shape)
    return pl.BlockSpec(shape, lambda i: (0,) * nd)


def _tiles(s):
    t_att = 512 if s >= 2048 else 128
    t_row = 256 if s >= 1024 else 128
    return t_att, t_row


def _inproj_fwd(x, norm_in, w_in_p, tm):
    s = x.shape[0]
    segs = (SEG_ZQ, SEG_ZKV, SEG_ZKR, SEG_GATTN, SEG_U, SEG_GPOOL, SEG_GMERGE)

    def body(x_ref, g_ref, w_ref, hn_ref, *z_refs):
        hn, _, _ = _rms_fwd(x_ref[...], g_ref[...])
        hn = hn.astype(BF16)
        hn_ref[...] = hn
        for (a, b), z_ref in zip(segs, z_refs):
            z_ref[...] = _dot(hn, w_ref[:, a:b])

    out_shape = [jax.ShapeDtypeStruct((s, D_MODEL), BF16)]
    out_specs = [_row_spec(tm, D_MODEL)]
    for a, b in segs:
        out_shape.append(jax.ShapeDtypeStruct((s, b - a), F32))
        out_specs.append(_row_spec(tm, b - a))
    return pl.pallas_call(
        body,
        name="inproj_fwd",
        grid=(s // tm,),
        in_specs=[_row_spec(tm, D_MODEL), _full_spec((1, D_MODEL)), _full_spec((D_MODEL, IN_PAD_TOTAL))],
        out_specs=out_specs,
        out_shape=out_shape,
        compiler_params=pltpu.CompilerParams(dimension_semantics=("parallel",), vmem_limit_bytes=VMEM_LIMIT),
    )(x, norm_in, w_in_p)


def _qkv_fwd(zq, zkv, zkr, q_norm, kv_norm, wuq_p, wk_p, wv, rc, rsa, rsb, tm):
    s = zq.shape[0]
    hw = MLA_HEADS * HEAD_PAD

    def body(zq_ref, zkv_ref, zkr_ref, gq_ref, gkv_ref, wuq_ref, wk_ref, wv_ref, c_ref, sa_ref, sb_ref,
             q_ref, k_ref, v_ref):
        c, sa, sb = c_ref[...], sa_ref[...], sb_ref[...]
        cq, _, _ = _rms_fwd(zq_ref[...], gq_ref[...])
        qf = _dot(cq.astype(BF16), wuq_ref[...])
        ckv, _, _ = _rms_fwd(zkv_ref[...], gkv_ref[...])
        ckv = ckv.astype(BF16)
        kn = _dot(ckv, wk_ref[...])
        kr = _rope(pltpu.roll(zkr_ref[...], 64, 1), c, sa, sb)
        for h in range(MLA_HEADS):
            cols = slice(h * HEAD_PAD, (h + 1) * HEAD_PAD)
            q_ref[:, cols] = _rope(qf[:, cols], c, sa, sb).astype(BF16)
            k_ref[:, cols] = (kn[:, cols] + kr).astype(BF16)
        v_ref[...] = _dot(ckv, wv_ref[...]).astype(BF16)

    return pl.pallas_call(
        body,
        name="qkv_fwd",
        grid=(s // tm,),
        in_specs=[
            _row_spec(tm, Q_LORA_RANK), _row_spec(tm, KV_LORA_RANK), _row_spec(tm, HEAD_PAD),
            _full_spec((1, Q_LORA_RANK)), _full_spec((1, KV_LORA_RANK)),
            _full_spec((Q_LORA_RANK, hw)), _full_spec((KV_LORA_RANK, hw)), _full_spec((KV_LORA_RANK, MLA_WIDTH)),
            _row_spec(tm, HEAD_PAD), _row_spec(tm, HEAD_PAD), _row_spec(tm, HEAD_PAD),
        ],
        out_specs=[_row_spec(tm, hw), _row_spec(tm, hw), _row_spec(tm, MLA_WIDTH)],
        out_shape=[jax.ShapeDtypeStruct((s, hw), BF16), jax.ShapeDtypeStruct((s, hw), BF16),
                   jax.ShapeDtypeStruct((s, MLA_WIDTH), BF16)],
        compiler_params=pltpu.CompilerParams(dimension_semantics=("parallel",), vmem_limit_bytes=VMEM_LIMIT),
    )(zq, zkv, zkr, q_norm, kv_norm, wuq_p, wk_p, wv, rc, rsa, rsb)


def _chunk_mask(t):
    rows = lax.broadcasted_iota(jnp.int32, (t, t), 0) // CHUNK
    cols = lax.broadcasted_iota(jnp.int32, (t, t), 1) // CHUNK
    return cols <= rows


def _attn_fwd(q, k, v, t):
    s = q.shape[0]
    pairs = MLA_HEADS // 2

    def body(q_ref, k_ref, v_ref, o_ref, lse_ref):
        i = pl.program_id(1)
        mask = _chunk_mask(t)
        for hh in range(2):
            qc = slice(hh * HEAD_PAD, (hh + 1) * HEAD_PAD)
            vc = slice(hh * V_HEAD_DIM, (hh + 1) * V_HEAD_DIM)
            qh = q_ref[:, qc]

            def step(j, carry, masked):
                m, l, acc = carry
                rows = pl.ds(pl.multiple_of(j * t, t), t)
                sc = _dot_nt(qh, k_ref[rows, qc]) * ATT_SCALE
                if masked:
                    sc = jnp.where(mask, sc, -jnp.inf)
                m_new = jnp.maximum(m, jnp.max(sc, axis=1, keepdims=True))
                alpha = jnp.exp(m - m_new)
                p = jnp.exp(sc - m_new)
                l = alpha * l + jnp.sum(p, axis=1, keepdims=True)
                acc = alpha * acc + _dot(p.astype(BF16), v_ref[rows, vc])
                return m_new, l, acc

            init = (jnp.full((t, 1), -jnp.inf, F32), jnp.zeros((t, 1), F32), jnp.zeros((t, V_HEAD_DIM), F32))
            carry = lax.fori_loop(0, i, functools.partial(step, masked=False), init)
            m, l, acc = step(i, carry, True)
            o_ref[:, vc] = acc / l
            lse_ref[:, qc] = jnp.broadcast_to(m + jnp.log(l), (t, HEAD_PAD))

    return pl.pallas_call(
        body,
        name="attn_fwd",
        grid=(pairs, s // t),
        in_specs=[
            pl.BlockSpec((t, 2 * HEAD_PAD), lambda p, i: (i, p)),
            pl.BlockSpec((s, 2 * HEAD_PAD), lambda p, i: (0, p)),
            pl.BlockSpec((s, 2 * V_HEAD_DIM), lambda p, i: (0, p)),
        ],
        out_specs=[
            pl.BlockSpec((t, 2 * V_HEAD_DIM), lambda p, i: (i, p)),
            pl.BlockSpec((t, 2 * HEAD_PAD), lambda p, i: (i, p)),
        ],
        out_shape=[jax.ShapeDtypeStruct((s, MLA_WIDTH), F32), jax.ShapeDtypeStruct((s, MLA_HEADS * HEAD_PAD), F32)],
        compiler_params=pltpu.CompilerParams(dimension_semantics=("parallel", "parallel"),
                                             vmem_limit_bytes=VMEM_LIMIT),
    )(q, k, v)


def _mid(o, gattn, u, gpool, gmerge, x, target, pool_w, pool_scale, w_ba, w_bp, w_out, norm_final, tm):
    s = x.shape[0]
    n_tiles = s // tm
    halo_per_tile = tm // POOL_HALO

    def body(o_ref, ga_ref, u_ref, uh_ref, gp_ref, gm_ref, x_ref, t_ref, pw_ref, ps_ref, wba_ref, wbp_ref,
             wout_ref, gf_ref,
             do_ref, dl_ref, dga_ref, dgp_ref, dgm_ref, ddc_ref, dh_ref,
             loss_ref, dwout_ref, dwba_ref, dwbp_ref, dpw_ref, dps_ref, dgf_ref,
             ubuf):
        i = pl.program_id(0)

        @pl.when(i == 0)
        def _():
            loss_ref[...] = jnp.zeros_like(loss_ref)
            dwout_ref[...] = jnp.zeros_like(dwout_ref)
            dwba_ref[...] = jnp.zeros_like(dwba_ref)
            dwbp_ref[...] = jnp.zeros_like(dwbp_ref)
            dpw_ref[...] = jnp.zeros_like(dpw_ref)
            dps_ref[...] = jnp.zeros_like(dps_ref)
            dgf_ref[...] = jnp.zeros_like(dgf_ref)

        o = o_ref[...]
        ga = ga_ref[...]
        sga = _sigmoid(ga)
        silu_a = ga * sga
        y_attn = (o * silu_a).astype(BF16)

        ubuf[0:POOL_HALO, :] = jnp.where(i > 0, uh_ref[...], 0.0)
        ubuf[POOL_HALO:, :] = u_ref[...]
        row = lax.broadcasted_iota(jnp.int32, (tm, POOL_GROUP_DIM), 0) + i * tm
        ps = ps_ref[...]
        gp = gp_ref[...]
        sgp = _sigmoid(gp)
        silu_p = gp * sgp
        d_bf, dm, inv_cnt = [], [], []
        for g, w in enumerate(POOL_WINDOWS):
            cols = slice(g * POOL_GROUP_DIM, (g + 1) * POOL_GROUP_DIM)
            wsum = ubuf[POOL_HALO:, cols]
            for kk in range(1, w):
                wsum = wsum + ubuf[POOL_HALO - kk:POOL_HALO - kk + tm, cols]
            inv = 1.0 / jnp.minimum(row + 1, w).astype(F32)
            dg = (wsum * inv - ubuf[POOL_HALO:, cols]).astype(BF16)
            d_bf.append(dg)
            inv_cnt.append(inv)
            dm.append(_dot(dg, pw_ref[g]))
        dm = jnp.concatenate(dm, axis=1)
        yp = dm * ps
        y_pool = (yp * silu_p).astype(BF16)

        a = _dot(y_attn, wba_ref[...])
        p = _dot(y_pool, wbp_ref[...])
        gate_a = _sigmoid(gm_ref[:, :D_MODEL])
        gate_p = _sigmoid(gm_ref[:, D_MODEL:])
        merged = (gate_a * a + gate_p * p).astype(BF16)
        h = x_ref[...] + _dot(merged, wout_ref[...])
        gf = gf_ref[...]
        y, xhat, r = _rms_fwd(h, gf)
        err = y - t_ref[...]
        e2 = err * err
        e2 = jnp.sum(e2.reshape(tm // 8, 8, D_MODEL), axis=0)
        acc = e2[:, 0:LANES]
        for cidx in range(1, D_MODEL // LANES):
            acc = acc + e2[:, cidx * LANES:(cidx + 1) * LANES]
        loss_ref[...] += acc

        dy = err * (1.0 / D_MODEL)
        dgf_ref[...] += _colsum(dy * xhat)
        dh = _rms_bwd(dy, xhat, r, gf)
        dh_ref[...] = dh
        dh_bf = dh.astype(BF16)
        dwout_ref[...] += _dot_tn(merged, dh_bf)
        dmerged = _dot_nt(dh_bf, wout_ref[...])
        da = (dmerged * gate_a).astype(BF16)
        dp = (dmerged * gate_p).astype(BF16)
        dgm_ref[:, :D_MODEL] = (dmerged * a * gate_a * (1.0 - gate_a)).astype(BF16)
        dgm_ref[:, D_MODEL:] = (dmerged * p * gate_p * (1.0 - gate_p)).astype(BF16)
        dwba_ref[...] += _dot_tn(y_attn, da)
        dwbp_ref[...] += _dot_tn(y_pool, dp)
        dy_attn = _dot_nt(da, wba_ref[...])
        dy_pool = _dot_nt(dp, wbp_ref[...])

        do = dy_attn * silu_a
        do_ref[...] = do
        dga_ref[...] = (dy_attn * o * (sga * (1.0 + ga * (1.0 - sga)))).astype(BF16)
        doo = do * o
        for hd in range(MLA_HEADS):
            dl = jnp.sum(doo[:, hd * V_HEAD_DIM:(hd + 1) * V_HEAD_DIM], axis=1, keepdims=True)
            dl_ref[:, hd * HEAD_PAD:(hd + 1) * HEAD_PAD] = jnp.broadcast_to(dl, (tm, HEAD_PAD))

        dyp = dy_pool * silu_p
        dgp_ref[...] = (dy_pool * yp * (sgp * (1.0 + gp * (1.0 - sgp)))).astype(BF16)
        dps_ref[...] += _colsum(dyp * dm)
        dmm = (dyp * ps).astype(BF16)
        for g in range(len(POOL_WINDOWS)):
            cols = slice(g * POOL_GROUP_DIM, (g + 1) * POOL_GROUP_DIM)
            dpw_ref[g] += _dot_tn(d_bf[g], dmm[:, cols])
            ddc_ref[:, cols] = _dot_nt(dmm[:, cols], pw_ref[g]) * inv_cnt[g]

    row_in = lambda n: _row_spec(tm, n)
    in_specs = [
        row_in(MLA_WIDTH), row_in(MLA_WIDTH), row_in(POOL_WIDTH),
        pl.BlockSpec((POOL_HALO, POOL_WIDTH), lambda i: (jnp.maximum(i * halo_per_tile - 1, 0), 0)),
        row_in(POOL_WIDTH), row_in(2 * D_MODEL), row_in(D_MODEL), row_in(D_MODEL),
        _full_spec((4, POOL_GROUP_DIM, POOL_GROUP_DIM)), _full_spec((1, POOL_WIDTH)),
        _full_spec((MLA_WIDTH, D_MODEL)), _full_spec((POOL_WIDTH, D_MODEL)), _full_spec((D_MODEL, D_MODEL)),
        _full_spec((1, D_MODEL)),
    ]
    out_shape = [
        jax.ShapeDtypeStruct((s, MLA_WIDTH), F32),
        jax.ShapeDtypeStruct((s, MLA_HEADS * HEAD_PAD), F32),
        jax.ShapeDtypeStruct((s, MLA_WIDTH), BF16),
        jax.ShapeDtypeStruct((s, POOL_WIDTH), BF16),
        jax.ShapeDtypeStruct((s, 2 * D_MODEL), BF16),
        jax.ShapeDtypeStruct((s, POOL_WIDTH), F32),
        jax.ShapeDtypeStruct((s, D_MODEL), F32),
        jax.ShapeDtypeStruct((8, LANES), F32),
        jax.ShapeDtypeStruct((D_MODEL, D_MODEL), F32),
        jax.ShapeDtypeStruct((MLA_WIDTH, D_MODEL), F32),
        jax.ShapeDtypeStruct((POOL_WIDTH, D_MODEL), F32),
        jax.ShapeDtypeStruct((4, POOL_GROUP_DIM, POOL_GROUP_DIM), F32),
        jax.ShapeDtypeStruct((1, POOL_WIDTH), F32),
        jax.ShapeDtypeStruct((1, D_MODEL), F32),
    ]
    out_specs = [
        row_in(MLA_WIDTH), row_in(MLA_HEADS * HEAD_PAD), row_in(MLA_WIDTH), row_in(POOL_WIDTH),
        row_in(2 * D_MODEL), row_in(POOL_WIDTH), row_in(D_MODEL),
        _full_spec((8, LANES)), _full_spec((D_MODEL, D_MODEL)), _full_spec((MLA_WIDTH, D_MODEL)),
        _full_spec((POOL_WIDTH, D_MODEL)), _full_spec((4, POOL_GROUP_DIM, POOL_GROUP_DIM)),
        _full_spec((1, POOL_WIDTH)), _full_spec((1, D_MODEL)),
    ]
    return pl.pallas_call(
        body,
        name="mid",
        grid=(n_tiles,),
        in_specs=in_specs,
        out_specs=out_specs,
        out_shape=out_shape,
        scratch_shapes=[pltpu.VMEM((tm + POOL_HALO, POOL_WIDTH), F32)],
        compiler_params=pltpu.CompilerParams(dimension_semantics=("arbitrary",), vmem_limit_bytes=VMEM_LIMIT),
    )(o, gattn, u, u, gpool, gmerge, x, target, pool_w, pool_scale, w_ba, w_bp, w_out, norm_final)


def _attn_bwd(q, k, v, do, lse, delta, t):
    s = q.shape[0]
    pairs = MLA_HEADS // 2

    def body(q_ref, do_ref, lse_ref, dl_ref, k_ref, v_ref, dq_ref, dk_ref, dv_ref):
        i = pl.program_id(1)

        @pl.when(i == 0)
        def _():
            dk_ref[...] = jnp.zeros_like(dk_ref)
            dv_ref[...] = jnp.zeros_like(dv_ref)

        mask = _chunk_mask(t)
        for hh in range(2):
            qc = slice(hh * HEAD_PAD, (hh + 1) * HEAD_PAD)
            vc = slice(hh * V_HEAD_DIM, (hh + 1) * V_HEAD_DIM)
            qh = q_ref[:, qc]
            doh = do_ref[:, vc].astype(BF16)
            lse_c = lse_ref[:, hh * HEAD_PAD:hh * HEAD_PAD + 1]
            dl_c = dl_ref[:, hh * HEAD_PAD:hh * HEAD_PAD + 1]

            def step(j, dq, masked):
                rows = pl.ds(pl.multiple_of(j * t, t), t)
                kj = k_ref[rows, qc]
                vj = v_ref[rows, vc]
                sc = _dot_nt(qh, kj) * ATT_SCALE
                p = jnp.exp(sc - lse_c)
                if masked:
                    p = jnp.where(mask, p, 0.0)
                dp = _dot_nt(doh, vj)
                ds = (p * (dp - dl_c) * ATT_SCALE).astype(BF16)
                dv_ref[rows, vc] += _dot_tn(p.astype(BF16), doh)
                dk_ref[rows, qc] += _dot_tn(ds, qh)
                return dq + _dot(ds, kj)

            dq = lax.fori_loop(0, i, functools.partial(step, masked=False), jnp.zeros((t, HEAD_PAD), F32))
            dq_ref[:, qc] = step(i, dq, True)

    hw = MLA_HEADS * HEAD_PAD
    return pl.pallas_call(
        body,
        name="attn_bwd",
        grid=(pairs, s // t),
        in_specs=[
            pl.BlockSpec((t, 2 * HEAD_PAD), lambda p, i: (i, p)),
            pl.BlockSpec((t, 2 * V_HEAD_DIM), lambda p, i: (i, p)),
            pl.BlockSpec((t, 2 * HEAD_PAD), lambda p, i: (i, p)),
            pl.BlockSpec((t, 2 * HEAD_PAD), lambda p, i: (i, p)),
            pl.BlockSpec((s, 2 * HEAD_PAD), lambda p, i: (0, p)),
            pl.BlockSpec((s, 2 * V_HEAD_DIM), lambda p, i: (0, p)),
        ],
        out_specs=[
            pl.BlockSpec((t, 2 * HEAD_PAD), lambda p, i: (i, p)),
            pl.BlockSpec((s, 2 * HEAD_PAD), lambda p, i: (0, p)),
            pl.BlockSpec((s, 2 * V_HEAD_DIM), lambda p, i: (0, p)),
        ],
        out_shape=[jax.ShapeDtypeStruct((s, hw), F32), jax.ShapeDtypeStruct((s, hw), F32),
                   jax.ShapeDtypeStruct((s, MLA_WIDTH), F32)],
        compiler_params=pltpu.CompilerParams(dimension_semantics=("parallel", "arbitrary"),
                                             vmem_limit_bytes=VMEM_LIMIT),
    )(q, do, lse, delta, k, v)


def _qkv_bwd(dq, dk, dv, zq, zkv, q_norm, kv_norm, wuq_p, wk_p, wv, rc, rsa, rsb, tm):
    s = zq.shape[0]
    hw = MLA_HEADS * HEAD_PAD

    def body(dq_ref, dk_ref, dv_ref, zq_ref, zkv_ref, gq_ref, gkv_ref, wuq_ref, wk_ref, wv_ref,
             c_ref, sa_ref, sb_ref,
             dzq_ref, dzkv_ref, dzkr_ref, dwuq_ref, dwk_ref, dwv_ref, dgq_ref, dgkv_ref):
        i = pl.program_id(0)

        @pl.when(i == 0)
        def _():
            dwuq_ref[...] = jnp.zeros_like(dwuq_ref)
            dwk_ref[...] = jnp.zeros_like(dwk_ref)
            dwv_ref[...] = jnp.zeros_like(dwv_ref)
            dgq_ref[...] = jnp.zeros_like(dgq_ref)
            dgkv_ref[...] = jnp.zeros_like(dgkv_ref)

        c, sa, sb = c_ref[...], sa_ref[...], sb_ref[...]
        gq, gkv = gq_ref[...], gkv_ref[...]

        cq, xq, rq = _rms_fwd(zq_ref[...], gq)
        dqp = jnp.concatenate(
            [_unrope(dq_ref[:, h * HEAD_PAD:(h + 1) * HEAD_PAD], c, sa, sb) for h in range(MLA_HEADS)],
            axis=1).astype(BF16)
        dwuq_ref[...] += _dot_tn(cq.astype(BF16), dqp)
        dcq = _dot_nt(dqp, wuq_ref[...])
        dgq_ref[...] += _colsum(dcq * xq)
        dzq_ref[...] = _rms_bwd(dcq, xq, rq, gq).astype(BF16)

        ckv, xkv, rkv = _rms_fwd(zkv_ref[...], gkv)
        ckv = ckv.astype(BF16)
        dkf = dk_ref[...]
        dk_bf = dkf.astype(BF16)
        dv_bf = dv_ref[...].astype(BF16)
        dwk_ref[...] += _dot_tn(ckv, dk_bf)
        dwv_ref[...] += _dot_tn(ckv, dv_bf)
        dckv = _dot_nt(dk_bf, wk_ref[...]) + _dot_nt(dv_bf, wv_ref[...])
        dgkv_ref[...] += _colsum(dckv * xkv)
        dzkv_ref[...] = _rms_bwd(dckv, xkv, rkv, gkv).astype(BF16)

        dkr = dkf[:, 0:HEAD_PAD]
        for h in range(1, MLA_HEADS):
            dkr = dkr + dkf[:, h * HEAD_PAD:(h + 1) * HEAD_PAD]
        dkr = pltpu.roll(_unrope(dkr, c, sa, sb), 64, 1)
        lane = lax.broadcasted_iota(jnp.int32, (tm, HEAD_PAD), 1)
        dzkr_ref[...] = jnp.where(lane < QK_ROPE_DIM, dkr, 0.0).astype(BF16)

    return pl.pallas_call(
        body,
        name="qkv_bwd",
        grid=(s // tm,),
        in_specs=[
            _row_spec(tm, hw), _row_spec(tm, hw), _row_spec(tm, MLA_WIDTH),
            _row_spec(tm, Q_LORA_RANK), _row_spec(tm, KV_LORA_RANK),
            _full_spec((1, Q_LORA_RANK)), _full_spec((1, KV_LORA_RANK)),
            _full_spec((Q_LORA_RANK, hw)), _full_spec((KV_LORA_RANK, hw)), _full_spec((KV_LORA_RANK, MLA_WIDTH)),
            _row_spec(tm, HEAD_PAD), _row_spec(tm, HEAD_PAD), _row_spec(tm, HEAD_PAD),
        ],
        out_specs=[
            _row_spec(tm, Q_LORA_RANK), _row_spec(tm, KV_LORA_RANK), _row_spec(tm, HEAD_PAD),
            _full_spec((Q_LORA_RANK, hw)), _full_spec((KV_LORA_RANK, hw)), _full_spec((KV_LORA_RANK, MLA_WIDTH)),
            _full_spec((1, Q_LORA_RANK)), _full_spec((1, KV_LORA_RANK)),
        ],
        out_shape=[
            jax.ShapeDtypeStruct((s, Q_LORA_RANK), BF16), jax.ShapeDtypeStruct((s, KV_LORA_RANK), BF16),
            jax.ShapeDtypeStruct((s, HEAD_PAD), BF16),
            jax.ShapeDtypeStruct((Q_LORA_RANK, hw), F32), jax.ShapeDtypeStruct((KV_LORA_RANK, hw), F32),
            jax.ShapeDtypeStruct((KV_LORA_RANK, MLA_WIDTH), F32),
            jax.ShapeDtypeStruct((1, Q_LORA_RANK), F32), jax.ShapeDtypeStruct((1, KV_LORA_RANK), F32),
        ],
        compiler_params=pltpu.CompilerParams(dimension_semantics=("arbitrary",), vmem_limit_bytes=VMEM_LIMIT),
    )(dq, dk, dv, zq, zkv, q_norm, kv_norm, wuq_p, wk_p, wv, rc, rsa, rsb)


def _inproj_bwd(dzq, dzkv, dzkr, dgattn, ddc, dgpool, dgmerge, hn, x, dh, norm_in, w_in_p, tm):
    s = x.shape[0]
    n_tiles = s // tm
    halo_per_tile = tm // POOL_HALO
    n_halo = s // POOL_HALO
    segs = (SEG_ZQ, SEG_ZKV, SEG_ZKR, SEG_GATTN, SEG_U, SEG_GPOOL, SEG_GMERGE)

    def body(dzq_ref, dzkv_ref, dzkr_ref, dga_ref, ddc_ref, ddn_ref, dgp_ref, dgm_ref, hn_ref, x_ref, dh_ref,
             g_ref, w_hbm, gx_ref, dgin_ref, dw_hbm, w_vmem, dw_acc, dbuf, sem):
        i = pl.program_id(0)

        @pl.when(i == 0)
        def _():
            cp = pltpu.make_async_copy(w_hbm, w_vmem, sem)
            cp.start()
            dw_acc[...] = jnp.zeros_like(dw_acc)
            dgin_ref[...] = jnp.zeros_like(dgin_ref)
            cp.wait()

        dbuf[0:tm, :] = ddc_ref[...]
        dbuf[tm:, :] = jnp.where(i < n_tiles - 1, ddn_ref[...], 0.0)
        row = lax.broadcasted_iota(jnp.int32, (tm, POOL_GROUP_DIM), 0) + i * tm
        du = []
        for g, w in enumerate(POOL_WINDOWS):
            cols = slice(g * POOL_GROUP_DIM, (g + 1) * POOL_GROUP_DIM)
            fsum = dbuf[0:tm, cols]
            for kk in range(1, w):
                fsum = fsum + dbuf[kk:kk + tm, cols]
            du.append(fsum - dbuf[0:tm, cols] * jnp.minimum(row + 1, w).astype(F32))
        du = jnp.concatenate(du, axis=1).astype(BF16)

        dz = (dzq_ref[...], dzkv_ref[...], dzkr_ref[...], dga_ref[...], du, dgp_ref[...], dgm_ref[...])
        hn = hn_ref[...]
        dhn = None
        for (a, b), dzs in zip(segs, dz):
            part = _dot_nt(dzs, w_vmem[:, a:b])
            dhn = part if dhn is None else dhn + part
            dw_acc[:, a:b] += _dot_tn(hn, dzs)

        g = g_ref[...]
        _, xhat, r = _rms_fwd(x_ref[...], g)
        dgin_ref[...] += _colsum(dhn * xhat)
        gx_ref[...] = dh_ref[...] + _rms_bwd(dhn, xhat, r, g)

        @pl.when(i == n_tiles - 1)
        def _():
            cp = pltpu.make_async_copy(dw_acc, dw_hbm, sem)
            cp.start()
            cp.wait()

    any_spec = pl.BlockSpec(memory_space=pl.ANY)
    return pl.pallas_call(
        body,
        name="inproj_bwd",
        grid=(n_tiles,),
        in_specs=[
            _row_spec(tm, Q_LORA_RANK), _row_spec(tm, KV_LORA_RANK), _row_spec(tm, HEAD_PAD),
            _row_spec(tm, MLA_WIDTH), _row_spec(tm, POOL_WIDTH),
            pl.BlockSpec((POOL_HALO, POOL_WIDTH), lambda i: (jnp.minimum((i + 1) * halo_per_tile, n_halo - 1), 0)),
            _row_spec(tm, POOL_WIDTH), _row_spec(tm, 2 * D_MODEL),
            _row_spec(tm, D_MODEL), _row_spec(tm, D_MODEL), _row_spec(tm, D_MODEL),
            _full_spec((1, D_MODEL)), any_spec,
        ],
        out_specs=[_row_spec(tm, D_MODEL), _full_spec((1, D_MODEL)), any_spec],
        out_shape=[jax.ShapeDtypeStruct((s, D_MODEL), F32), jax.ShapeDtypeStruct((1, D_MODEL), F32),
                   jax.ShapeDtypeStruct((D_MODEL, IN_PAD_TOTAL), F32)],
        scratch_shapes=[
            pltpu.VMEM((D_MODEL, IN_PAD_TOTAL), BF16),
            pltpu.VMEM((D_MODEL, IN_PAD_TOTAL), F32),
            pltpu.VMEM((tm + POOL_HALO, POOL_WIDTH), F32),
            pltpu.SemaphoreType.DMA,
        ],
        compiler_params=pltpu.CompilerParams(dimension_semantics=("arbitrary",), vmem_limit_bytes=VMEM_LIMIT),
    )(dzq, dzkv, dzkr, dgattn, ddc, ddc, dgpool, dgmerge, hn, x, dh, norm_in, w_in_p)


def _other_chips(x, y):
    return ((1 - x, y), (x, 1 - y), (1 - x, 1 - y))


def _weight_gather(wp):
    def body(wp_ref, wall_ref, send_sems, recv_sems, local_sem):
        x, y, c = lax.axis_index("x"), lax.axis_index("y"), lax.axis_index("c")
        k = 2 * x + y
        chips = _other_chips(x, y)
        mine = pl.ds(pl.multiple_of(c * PACK_HALF, PACK_CHUNK), PACK_HALF)
        theirs = pl.ds(pl.multiple_of((1 - c) * PACK_HALF, PACK_CHUNK), PACK_HALF)

        def copy(sem, src, dst, to):
            return pltpu.make_async_remote_copy(src_ref=src, dst_ref=dst, send_sem=send_sems.at[sem],
                                                recv_sem=recv_sems.at[sem], device_id=to, device_id_type=MESH)

        local = pltpu.make_async_copy(wp_ref, wall_ref.at[k], local_sem)
        local.start()
        first = [copy(j, wp_ref.at[mine], wall_ref.at[k, mine], (cx, cy, c)) for j, (cx, cy) in enumerate(chips)]
        for cp in first:
            cp.start()
        passed = []
        for j, (cx, cy) in enumerate(chips):
            kj = 2 * cx + cy
            copy(j, wall_ref.at[kj, mine], wall_ref.at[kj, mine], (x, y, c)).wait_recv()
            fwd = copy(3 + j, wall_ref.at[kj, mine], wall_ref.at[kj, mine], (x, y, 1 - c))
            fwd.start()
            passed.append(fwd)
        for j, (cx, cy) in enumerate(chips):
            kj = 2 * cx + cy
            copy(3 + j, wall_ref.at[kj, theirs], wall_ref.at[kj, theirs], (x, y, c)).wait_recv()
        for cp in first + passed:
            cp.wait_send()
        local.wait()

    any_spec = pl.BlockSpec(memory_space=pl.ANY)
    return pl.pallas_call(
        body,
        name="weight_gather",
        in_specs=[any_spec],
        out_specs=any_spec,
        out_shape=jax.ShapeDtypeStruct((N_CHIPS, PACK_ROWS, LANES), BF16),
        scratch_shapes=[pltpu.SemaphoreType.DMA((6,)), pltpu.SemaphoreType.DMA((6,)), pltpu.SemaphoreType.DMA],
    )(wp)


def _grad_reduce(gp, gs):
    n_chunks = PACK_HALF // PACK_CHUNK

    def body(gp_ref, gs_ref, g_ref, gsum_ref, pm, a_buf, b_buf, r_buf, s_buf, send_sems, recv_sems, local_sems):
        x, y, c = lax.axis_index("x"), lax.axis_index("y"), lax.axis_index("c")
        k = 2 * x + y
        me = 4 * x + 2 * y + c
        chips = _other_chips(x, y)
        mine = pl.ds(pl.multiple_of(c * PACK_HALF, PACK_CHUNK), PACK_HALF)
        theirs = pl.ds(pl.multiple_of((1 - c) * PACK_HALF, PACK_CHUNK), PACK_HALF)

        def copy(sem, src, dst, to):
            return pltpu.make_async_remote_copy(src_ref=src, dst_ref=dst, send_sem=send_sems.at[sem],
                                                recv_sem=recv_sems.at[sem], device_id=to, device_id_type=MESH)

        load = pltpu.make_async_copy(gp_ref.at[:, mine], pm, local_sems.at[0])
        load.start()
        to_sib = copy(0, gp_ref.at[:, theirs], a_buf, (x, y, 1 - c))
        to_sib.start()
        flips = [(fx, fy, fc) for fx in (0, 1) for fy in (0, 1) for fc in (0, 1)][1:]
        small = []
        for f, (fx, fy, fc) in enumerate(flips, start=1):
            peer = (1 - x if fx else x, 1 - y if fy else y, 1 - c if fc else c)
            cp = copy(4 + f, gs_ref, s_buf.at[f], peer)
            cp.start()
            small.append(cp)
        s_buf[0] = gs_ref[...]

        load.wait()
        copy(0, a_buf, a_buf, (x, y, c)).wait_recv()

        def rows_of(i):
            return pl.ds(pl.multiple_of(i * PACK_CHUNK, PACK_CHUNK), PACK_CHUNK)

        sends = []
        for j, (cx, cy) in enumerate(chips):
            kj = 2 * cx + cy

            def pair_sum(i, carry, kj=kj):
                rows = rows_of(i)
                pm[kj, rows, :] = (pm[kj, rows, :].astype(F32) + a_buf[kj, rows, :].astype(F32)).astype(BF16)
                return carry

            lax.fori_loop(0, n_chunks, pair_sum, 0)
            cp = copy(1 + j, pm.at[kj], b_buf.at[j], (cx, cy, c))
            cp.start()
            sends.append(cp)

        def own_sum(i, carry):
            rows = rows_of(i)
            r_buf[rows, :] = pm[k, rows, :].astype(F32) + a_buf[k, rows, :].astype(F32)
            return carry

        lax.fori_loop(0, n_chunks, own_sum, 0)
        for j in range(3):
            copy(1 + j, b_buf.at[j], b_buf.at[j], (x, y, c)).wait_recv()

            def add_remote(i, carry, j=j):
                rows = rows_of(i)
                r_buf[rows, :] = r_buf[rows, :] + b_buf[j, rows, :].astype(F32)
                return carry

            lax.fori_loop(0, n_chunks, add_remote, 0)

        store = pltpu.make_async_copy(r_buf, g_ref.at[mine], local_sems.at[1])
        store.start()
        to_sib_r = copy(4, r_buf, g_ref.at[mine], (x, y, 1 - c))
        to_sib_r.start()

        for f in range(1, N_DEV):
            copy(4 + f, s_buf.at[f], s_buf.at[f], (x, y, c)).wait_recv()
        total = s_buf[jnp.bitwise_xor(me, 0)]
        for d in range(1, N_DEV):
            total = total + s_buf[jnp.bitwise_xor(me, d)]
        gsum_ref[...] = total

        copy(4, g_ref.at[theirs], g_ref.at[theirs], (x, y, c)).wait_recv()
        store.wait()
        for cp in [to_sib, to_sib_r] + sends + small:
            cp.wait_send()

    any_spec = pl.BlockSpec(memory_space=pl.ANY)
    vmem_spec = pl.BlockSpec(memory_space=pltpu.VMEM)
    return pl.pallas_call(
        body,
        name="grad_reduce",
        in_specs=[any_spec, vmem_spec],
        out_specs=[any_spec, vmem_spec],
        out_shape=[jax.ShapeDtypeStruct((PACK_ROWS, LANES), F32), jax.ShapeDtypeStruct((SMALL_ROWS, LANES), F32)],
        scratch_shapes=[
            pltpu.VMEM((N_CHIPS, PACK_HALF, LANES), BF16),
            pltpu.VMEM((N_CHIPS, PACK_HALF, LANES), BF16),
            pltpu.VMEM((3, PACK_HALF, LANES), BF16),
            pltpu.VMEM((PACK_HALF, LANES), F32),
            pltpu.VMEM((N_DEV, SMALL_ROWS, LANES), F32),
            pltpu.SemaphoreType.DMA((12,)), pltpu.SemaphoreType.DMA((12,)), pltpu.SemaphoreType.DMA((2,)),
        ],
        compiler_params=pltpu.CompilerParams(vmem_limit_bytes=VMEM_LIMIT),
    )(gp, gs)


def _adamw_math(w, g, m, v):
    m = ADAM_B1 * m + (1.0 - ADAM_B1) * g
    v = ADAM_B2 * v + (1.0 - ADAM_B2) * (g * g)
    m_hat = m / (1.0 - ADAM_B1 ** ADAM_STEP)
    v_hat = v / (1.0 - ADAM_B2 ** ADAM_STEP)
    delta = -ADAM_LR * (m_hat / (jnp.sqrt(v_hat) + ADAM_EPS) + ADAM_WD * w)
    return delta, m, v


def _adamw_tiled(w, g, m, v, tm):
    rows, cols = w.shape

    def body(w_ref, g_ref, m_ref, v_ref, d_ref, nm_ref, nv_ref):
        d_ref[...], nm_ref[...], nv_ref[...] = _adamw_math(w_ref[...], g_ref[...], m_ref[...], v_ref[...])

    spec = _row_spec(tm, cols)
    return pl.pallas_call(
        body,
        name="adamw_w_in",
        grid=(rows // tm,),
        in_specs=[spec] * 4,
        out_specs=[spec] * 3,
        out_shape=[jax.ShapeDtypeStruct(w.shape, F32)] * 3,
        compiler_params=pltpu.CompilerParams(dimension_semantics=("parallel",), vmem_limit_bytes=VMEM_LIMIT),
    )(w, g, m, v)


def _adamw_many(ws, gs, ms, vs):
    n = len(ws)

    def body(*refs):
        ins, outs = refs[:4 * n], refs[4 * n:]
        for i in range(n):
            d, nm, nv = _adamw_math(ins[i][...], ins[n + i][...], ins[2 * n + i][...], ins[3 * n + i][...])
            outs[i][...] = d
            outs[n + i][...] = nm
            outs[2 * n + i][...] = nv

    vmem_spec = pl.BlockSpec(memory_space=pltpu.VMEM)
    shapes = [jax.ShapeDtypeStruct(w.shape, F32) for w in ws]
    out = pl.pallas_call(
        body,
        name="adamw_small",
        in_specs=[vmem_spec] * (4 * n),
        out_specs=[vmem_spec] * (3 * n),
        out_shape=shapes * 3,
        compiler_params=pltpu.CompilerParams(vmem_limit_bytes=VMEM_LIMIT),
    )(*ws, *gs, *ms, *vs)
    return out[:n], out[n:2 * n], out[2 * n:]


def _pack_rows(parts, rows, dtype):
    flat = jnp.concatenate([p.reshape(-1).astype(dtype) for p in parts])
    flat = jnp.concatenate([flat, jnp.zeros((rows * LANES - flat.shape[0],), dtype)])
    return flat.reshape(rows, LANES)


def _unpack_rows(packed, shapes):
    flat = packed.reshape(-1)
    out, off = [], 0
    for _, shp in shapes:
        n = int(np.prod(shp))
        out.append(flat[off:off + n].reshape(shp))
        off += n
    return out


def _rope_tables(s):
    half = QK_ROPE_DIM // 2
    inv_freq = ROPE_THETA ** (-jnp.arange(half, dtype=F32) / half)
    ang = jnp.arange(s, dtype=F32)[:, None] * inv_freq[None, :]
    cos, sin = jnp.cos(ang), jnp.sin(ang)
    z16 = jnp.zeros((s, half), F32)
    z32 = jnp.zeros((s, HEAD_PAD - QK_NOPE_DIM - QK_ROPE_DIM), F32)
    z64 = jnp.zeros((s, QK_NOPE_DIM), F32)
    rc = jnp.concatenate([jnp.ones((s, QK_NOPE_DIM), F32), cos, cos, z32], axis=1)
    rsa = jnp.concatenate([z64, -sin, z16, z32], axis=1)
    rsb = jnp.concatenate([z64, z16, sin, z32], axis=1)
    return rc, rsa, rsb


def kernel(x, norm_in, w_in, q_norm, w_uq, kv_norm, w_ukv, pool_w, pool_scale, w_branch_attn, w_branch_pool, w_out, norm_final, loss_target, m_norm_in, m_w_in, m_q_norm, m_w_uq, m_kv_norm, m_w_ukv, m_pool_w, m_pool_scale, m_w_branch_attn, m_w_branch_pool, m_w_out, m_norm_final, v_norm_in, v_w_in, v_q_norm, v_w_uq, v_kv_norm, v_w_ukv, v_pool_w, v_pool_scale, v_w_branch_attn, v_w_branch_pool, v_w_out, v_norm_final):
    s = x.shape[1]
    t_att, t_row = _tiles(s)
    x2 = x.reshape(s, D_MODEL)
    tgt = loss_target.reshape(s, D_MODEL)

    shard = dict(w_in=w_in, w_uq=w_uq, w_ukv=w_ukv, w_branch_attn=w_branch_attn, w_branch_pool=w_branch_pool,
                 w_out=w_out)
    wall = _weight_gather(_pack_rows([shard[n] for n, _ in SHARD_SHAPES], PACK_ROWS, BF16))
    per_chip = [_unpack_rows(wall[j], SHARD_SHAPES) for j in range(N_CHIPS)]
    w_in_f = jnp.concatenate([p[0] for p in per_chip], axis=1)
    w_uq_f = jnp.concatenate([p[1] for p in per_chip], axis=0)
    w_ukv_f = jnp.concatenate([p[2] for p in per_chip], axis=0)
    w_ba_f = jnp.concatenate([p[3] for p in per_chip], axis=1)
    w_bp_f = jnp.concatenate([p[4] for p in per_chip], axis=1)
    w_out_f = jnp.concatenate([p[5] for p in per_chip], axis=0)

    w_in_p = jnp.concatenate([w_in_f[:, :ZKR_END], jnp.zeros((D_MODEL, SEG_ZKR[1] - ZKR_END), BF16),
                              w_in_f[:, ZKR_END:]], axis=1)
    hw = MLA_HEADS * HEAD_PAD
    wuq_p = jnp.pad(w_uq_f, ((0, 0), (0, 0), (0, HEAD_PAD - QK_NOPE_DIM - QK_ROPE_DIM))).reshape(Q_LORA_RANK, hw)
    wk_p = jnp.pad(w_ukv_f[:, :, :QK_NOPE_DIM], ((0, 0), (0, 0), (0, HEAD_PAD - QK_NOPE_DIM))).reshape(KV_LORA_RANK, hw)
    wv = w_ukv_f[:, :, QK_NOPE_DIM:].reshape(KV_LORA_RANK, MLA_WIDTH)
    rc, rsa, rsb = _rope_tables(s)
    g_in = norm_in.reshape(1, -1)
    g_q = q_norm.reshape(1, -1)
    g_kv = kv_norm.reshape(1, -1)
    g_f = norm_final.reshape(1, -1)
    ps = pool_scale.reshape(1, -1)
    pw_bf = pool_w.astype(BF16)

    hn, zq, zkv, zkr, gattn, u, gpool, gmerge = _inproj_fwd(x2, g_in, w_in_p, t_row)
    q, k, v = _qkv_fwd(zq, zkv, zkr, g_q, g_kv, wuq_p, wk_p, wv, rc, rsa, rsb, t_row)
    o, lse = _attn_fwd(q, k, v, t_att)

    (do, delta, dgattn, dgpool, dgmerge, ddc, dh, sq_err, d_w_out, d_w_ba, d_w_bp, d_pool_w, d_pool_scale,
     d_norm_final) = _mid(o, gattn, u, gpool, gmerge, x2, tgt, pw_bf, ps, w_ba_f, w_bp_f, w_out_f, g_f, t_row)

    dq, dk, dv = _attn_bwd(q, k, v, do, lse, delta, t_att)
    dzq, dzkv, dzkr, d_wuq_p, d_wk_p, d_wv, d_q_norm, d_kv_norm = _qkv_bwd(
        dq, dk, dv, zq, zkv, g_q, g_kv, wuq_p, wk_p, wv, rc, rsa, rsb, t_row)
    grad_x, d_norm_in, d_w_in_p = _inproj_bwd(dzq, dzkv, dzkr, dgattn, ddc, dgpool, dgmerge, hn, x2, dh, g_in,
                                              w_in_p, t_row)

    d_w_in = jnp.concatenate([d_w_in_p[:, :ZKR_END], d_w_in_p[:, SEG_ZKR[1]:]], axis=1)
    d_w_uq = d_wuq_p.reshape(Q_LORA_RANK, MLA_HEADS, HEAD_PAD)[:, :, :QK_NOPE_DIM + QK_ROPE_DIM]
    d_w_ukv = jnp.concatenate([d_wk_p.reshape(KV_LORA_RANK, MLA_HEADS, HEAD_PAD)[:, :, :QK_NOPE_DIM],
                               d_wv.reshape(KV_LORA_RANK, MLA_HEADS, V_HEAD_DIM)], axis=2)

    def shard_of(j):
        return [
            d_w_in[:, j * 1064:(j + 1) * 1064], d_w_uq[j * 96:(j + 1) * 96], d_w_ukv[j * 64:(j + 1) * 64],
            d_w_ba[:, j * 256:(j + 1) * 256], d_w_bp[:, j * 256:(j + 1) * 256], d_w_out[j * 256:(j + 1) * 256],
        ]

    gp = jnp.stack([_pack_rows(shard_of(j), PACK_ROWS, BF16) for j in range(N_CHIPS)])
    small = dict(norm_in=d_norm_in, q_norm=d_q_norm, kv_norm=d_kv_norm, pool_scale=d_pool_scale,
                 norm_final=d_norm_final, pool_w=d_pool_w)
    gs = _pack_rows([small[n] for n, _ in SMALL_SHAPES], SMALL_ROWS, F32)

    g_shard, g_small = _grad_reduce(gp, gs)
    g_w_in, g_w_uq, g_w_ukv, g_w_ba, g_w_bp, g_w_out = _unpack_rows(g_shard, SHARD_SHAPES)
    g_norm_in, g_q_norm, g_kv_norm, g_pool_scale, g_norm_final, g_pool_w = _unpack_rows(g_small, SMALL_SHAPES)

    dl_w_in, nm_w_in, nv_w_in = _adamw_tiled(w_in, g_w_in, m_w_in, v_w_in, 256)

    def two_d(a):
        return a.reshape(1, -1) if a.ndim == 1 else a.reshape(a.shape[0], -1)

    names = ["norm_in", "q_norm", "w_uq", "kv_norm", "w_ukv", "pool_w", "pool_scale", "w_branch_attn",
             "w_branch_pool", "w_out", "norm_final"]
    ws = dict(norm_in=norm_in, q_norm=q_norm, w_uq=w_uq, kv_norm=kv_norm, w_ukv=w_ukv, pool_w=pool_w,
              pool_scale=pool_scale, w_branch_attn=w_branch_attn, w_branch_pool=w_branch_pool, w_out=w_out,
              norm_final=norm_final)
    gsd = dict(norm_in=g_norm_in, q_norm=g_q_norm, w_uq=g_w_uq, kv_norm=g_kv_norm, w_ukv=g_w_ukv, pool_w=g_pool_w,
               pool_scale=g_pool_scale, w_branch_attn=g_w_ba, w_branch_pool=g_w_bp, w_out=g_w_out,
               norm_final=g_norm_final)
    msd = dict(norm_in=m_norm_in, q_norm=m_q_norm, w_uq=m_w_uq, kv_norm=m_kv_norm, w_ukv=m_w_ukv, pool_w=m_pool_w,
               pool_scale=m_pool_scale, w_branch_attn=m_w_branch_attn, w_branch_pool=m_w_branch_pool, w_out=m_w_out,
               norm_final=m_norm_final)
    vsd = dict(norm_in=v_norm_in, q_norm=v_q_norm, w_uq=v_w_uq, kv_norm=v_kv_norm, w_ukv=v_w_ukv, pool_w=v_pool_w,
               pool_scale=v_pool_scale, w_branch_attn=v_w_branch_attn, w_branch_pool=v_w_branch_pool, w_out=v_w_out,
               norm_final=v_norm_final)
    dls, nms, nvs = _adamw_many([two_d(ws[n]) for n in names], [two_d(gsd[n]) for n in names],
                                [two_d(msd[n]) for n in names], [two_d(vsd[n]) for n in names])

    grads = dict(gsd)
    grads["w_in"] = g_w_in
    delta_w = {n: d.reshape(ws[n].shape) for n, d in zip(names, dls)}
    new_m = {n: d.reshape(ws[n].shape) for n, d in zip(names, nms)}
    new_v = {n: d.reshape(ws[n].shape) for n, d in zip(names, nvs)}
    delta_w["w_in"], new_m["w_in"], new_v["w_in"] = dl_w_in, nm_w_in, nv_w_in
    ws["w_in"] = w_in

    order = ["norm_in", "w_in", "q_norm", "w_uq", "kv_norm", "w_ukv", "pool_w", "pool_scale", "w_branch_attn",
             "w_branch_pool", "w_out", "norm_final"]
    loss = lax.psum(0.5 * jnp.sum(sq_err) / D_MODEL, ("x", "y", "c"))
    return (loss, grad_x.reshape(x.shape),
            *[grads[n].reshape(ws[n].shape) for n in order],
            *[delta_w[n] for n in order], *[new_m[n] for n in order], *[new_v[n] for n in order])
```

```python
import functools

import jax
import jax.numpy as jnp
import numpy as np
from jax import lax
from jax.experimental import pallas as pl
from jax.experimental.pallas import tpu as pltpu

F32 = jnp.float32
BF16 = jnp.bfloat16
MESH = pl.DeviceIdType.MESH

D_MODEL = 1024
CHUNK = 64
MLA_HEADS = 8
QK_NOPE_DIM = 64
QK_ROPE_DIM = 32
V_HEAD_DIM = 64
Q_LORA_RANK = 384
KV_LORA_RANK = 256
MLA_WIDTH = MLA_HEADS * V_HEAD_DIM
ROPE_THETA = 10000.0
POOL_WINDOWS = (2, 4, 8, 16)
POOL_WIDTH = 512
POOL_GROUP_DIM = 128
BRANCH_COLS = D_MODEL // 4
POOL_HALO = 16
EPS = 1e-6
IN_TOTAL = 4256
HEAD_PAD = 128
ATT_SCALE = (QK_NOPE_DIM + QK_ROPE_DIM) ** -0.5

ADAM_LR = 0.001
ADAM_B1 = 0.9
ADAM_B2 = 0.999
ADAM_EPS = 1e-08
ADAM_WD = 0.01
ADAM_STEP = 10

N_CHIPS = 4
N_DEV = 8
LANES = 128
VMEM_LIMIT = 60 * 1024 * 1024

IN_SEGMENTS = ((384, 384), (256, 256), (32, HEAD_PAD), (512, 512), (512, 512), (512, 512), (2048, 2048))
SHARD_COLS = IN_TOTAL // N_CHIPS


def _shard_pieces():
    bounds, off = [], 0
    for w, _ in IN_SEGMENTS:
        bounds.append((off, off + w))
        off += w
    out = []
    for j in range(N_CHIPS):
        lo, hi = SHARD_COLS * j, SHARD_COLS * (j + 1)
        out.append([(i, max(lo, a) - a, min(hi, b) - a, max(lo, a) - lo)
                    for i, (a, b) in enumerate(bounds) if max(lo, a) < min(hi, b)])
    return out


SHARD_PIECES = _shard_pieces()

COMM_PARAMS = (
    ("w_in", SHARD_COLS, D_MODEL, 1, 512),
    ("w_uq", 96, 768, 0, 48),
    ("w_ukv", 64, 1024, 0, 32),
    ("w_branch_attn", 512, 256, 0, 256),
    ("w_branch_pool", 512, 256, 0, 256),
    ("w_out", 256, 1024, 0, 128),
)

SMALL_SHAPES = (
    ("norm_in", (1024,)),
    ("q_norm", (384,)),
    ("kv_norm", (256,)),
    ("pool_scale", (512,)),
    ("norm_final", (1024,)),
    ("pool_w", (4, 128, 128)),
)
SMALL_ELEMS = sum(int(np.prod(s)) for _, s in SMALL_SHAPES)
SMALL_ROWS = -(-SMALL_ELEMS // (LANES * 8)) * 8


def _dot(a, b):
    return jnp.dot(a, b, preferred_element_type=F32)


def _dot_nt(a, b):
    return lax.dot_general(a, b, (((1,), (1,)), ((), ())), preferred_element_type=F32)


def _dot_tn(a, b):
    return lax.dot_general(a, b, (((0,), (0,)), ((), ())), preferred_element_type=F32)


def _sigmoid(x):
    return 1.0 / (1.0 + jnp.exp(-x))


def _colsum(x):
    return jnp.sum(x, axis=0, keepdims=True)


def _rms_fwd(x, g):
    r = lax.rsqrt(jnp.mean(x * x, axis=-1, keepdims=True) + EPS)
    xhat = x * r
    return xhat * g, xhat, r


def _rms_bwd(dy, xhat, r, g):
    dxhat = dy * g
    return r * (dxhat - xhat * jnp.mean(dxhat * xhat, axis=-1, keepdims=True))


def _rope(v, c, sa, sb):
    return v * c + pltpu.roll(v, 112, 1) * sa + pltpu.roll(v, 16, 1) * sb


def _unrope(d, c, sa, sb):
    return d * c + pltpu.roll(d * sa, 16, 1) + pltpu.roll(d * sb, 112, 1)


def _row_spec(tm, n):
    return pl.BlockSpec((tm, n), lambda i: (i, 0))


def _full_spec(shape):
    nd = len(shape)
    return pl.BlockSpec(shape, lambda i: (0,) * nd)


def _tiles(s):
    t_att = 512 if s >= 2048 else 128
    t_row = 256 if s >= 1024 else 128
    return t_att, t_row


def _inproj_fwd(x, norm_in, w_in_t, tm):
    s = x.shape[0]

    def body(x_ref, g_ref, w_ref, hn_ref, *z_refs):
        hn, _, _ = _rms_fwd(x_ref[...], g_ref[...])
        hn = hn.astype(BF16)
        hn_ref[...] = hn
        for j, pieces in enumerate(SHARD_PIECES):
            zj = _dot_nt(hn, w_ref[j])
            for seg, lo, hi, col in pieces:
                z_refs[seg][:, lo:hi] = zj[:, col:col + hi - lo]
        for seg, (w, wide) in enumerate(IN_SEGMENTS):
            if wide > w:
                z_refs[seg][:, w:wide] = jnp.zeros((tm, wide - w), F32)

    out_shape = [jax.ShapeDtypeStruct((s, D_MODEL), BF16)]
    out_specs = [_row_spec(tm, D_MODEL)]
    for _, wide in IN_SEGMENTS:
        out_shape.append(jax.ShapeDtypeStruct((s, wide), F32))
        out_specs.append(_row_spec(tm, wide))
    return pl.pallas_call(
        body,
        name="inproj_fwd",
        grid=(s // tm,),
        in_specs=[_row_spec(tm, D_MODEL), _full_spec((1, D_MODEL)),
                  pl.BlockSpec((N_CHIPS, SHARD_COLS, D_MODEL), lambda i: (0, 0, 0), pipeline_mode=pl.Buffered(1))],
        out_specs=out_specs,
        out_shape=out_shape,
        compiler_params=pltpu.CompilerParams(dimension_semantics=("parallel",), vmem_limit_bytes=VMEM_LIMIT),
    )(x, norm_in, w_in_t)


def _qkv_fwd(zq, zkv, zkr, q_norm, kv_norm, wuq_p, wk_p, wv, rc, rsa, rsb, tm):
    s = zq.shape[0]
    hw = MLA_HEADS * HEAD_PAD

    def body(zq_ref, zkv_ref, zkr_ref, gq_ref, gkv_ref, wuq_ref, wk_ref, wv_ref, c_ref, sa_ref, sb_ref,
             q_ref, k_ref, v_ref):
        c, sa, sb = c_ref[...], sa_ref[...], sb_ref[...]
        cq, _, _ = _rms_fwd(zq_ref[...], gq_ref[...])
        qf = _dot(cq.astype(BF16), wuq_ref[...])
        ckv, _, _ = _rms_fwd(zkv_ref[...], gkv_ref[...])
        ckv = ckv.astype(BF16)
        kn = _dot(ckv, wk_ref[...])
        kr = _rope(pltpu.roll(zkr_ref[...], 64, 1), c, sa, sb)
        for h in range(MLA_HEADS):
            cols = slice(h * HEAD_PAD, (h + 1) * HEAD_PAD)
            q_ref[:, cols] = _rope(qf[:, cols], c, sa, sb).astype(BF16)
            k_ref[:, cols] = (kn[:, cols] + kr).astype(BF16)
        v_ref[...] = _dot(ckv, wv_ref[...]).astype(BF16)

    return pl.pallas_call(
        body,
        name="qkv_fwd",
        grid=(s // tm,),
        in_specs=[
            _row_spec(tm, Q_LORA_RANK), _row_spec(tm, KV_LORA_RANK), _row_spec(tm, HEAD_PAD),
            _full_spec((1, Q_LORA_RANK)), _full_spec((1, KV_LORA_RANK)),
            _full_spec((Q_LORA_RANK, hw)), _full_spec((KV_LORA_RANK, hw)), _full_spec((KV_LORA_RANK, MLA_WIDTH)),
            _row_spec(tm, HEAD_PAD), _row_spec(tm, HEAD_PAD), _row_spec(tm, HEAD_PAD),
        ],
        out_specs=[_row_spec(tm, hw), _row_spec(tm, hw), _row_spec(tm, MLA_WIDTH)],
        out_shape=[jax.ShapeDtypeStruct((s, hw), BF16), jax.ShapeDtypeStruct((s, hw), BF16),
                   jax.ShapeDtypeStruct((s, MLA_WIDTH), BF16)],
        compiler_params=pltpu.CompilerParams(dimension_semantics=("parallel",), vmem_limit_bytes=VMEM_LIMIT),
    )(zq, zkv, zkr, q_norm, kv_norm, wuq_p, wk_p, wv, rc, rsa, rsb)


def _chunk_mask(t):
    rows = lax.broadcasted_iota(jnp.int32, (t, t), 0) // CHUNK
    cols = lax.broadcasted_iota(jnp.int32, (t, t), 1) // CHUNK
    return cols <= rows


def _attn_fwd(q, k, v, t):
    s = q.shape[0]
    pairs = MLA_HEADS // 2

    def body(q_ref, k_ref, v_ref, o_ref, lse_ref):
        i = pl.program_id(1)
        mask = _chunk_mask(t)
        for hh in range(2):
            qc = slice(hh * HEAD_PAD, (hh + 1) * HEAD_PAD)
            vc = slice(hh * V_HEAD_DIM, (hh + 1) * V_HEAD_DIM)
            qh = q_ref[:, qc]

            def step(j, carry, masked):
                m, l, acc = carry
                rows = pl.ds(pl.multiple_of(j * t, t), t)
                sc = _dot_nt(qh, k_ref[rows, qc]) * ATT_SCALE
                if masked:
                    sc = jnp.where(mask, sc, -jnp.inf)
                m_new = jnp.maximum(m, jnp.max(sc, axis=1, keepdims=True))
                alpha = jnp.exp(m - m_new)
                p = jnp.exp(sc - m_new)
                l = alpha * l + jnp.sum(p, axis=1, keepdims=True)
                acc = alpha * acc + _dot(p.astype(BF16), v_ref[rows, vc])
                return m_new, l, acc

            init = (jnp.full((t, 1), -jnp.inf, F32), jnp.zeros((t, 1), F32), jnp.zeros((t, V_HEAD_DIM), F32))
            carry = lax.fori_loop(0, i, functools.partial(step, masked=False), init)
            m, l, acc = step(i, carry, True)
            o_ref[:, vc] = acc / l
            lse_ref[:, qc] = jnp.broadcast_to(m + jnp.log(l), (t, HEAD_PAD))

    return pl.pallas_call(
        body,
        name="attn_fwd",
        grid=(pairs, s // t),
        in_specs=[
            pl.BlockSpec((t, 2 * HEAD_PAD), lambda p, i: (i, p)),
            pl.BlockSpec((s, 2 * HEAD_PAD), lambda p, i: (0, p)),
            pl.BlockSpec((s, 2 * V_HEAD_DIM), lambda p, i: (0, p)),
        ],
        out_specs=[
            pl.BlockSpec((t, 2 * V_HEAD_DIM), lambda p, i: (i, p)),
            pl.BlockSpec((t, 2 * HEAD_PAD), lambda p, i: (i, p)),
        ],
        out_shape=[jax.ShapeDtypeStruct((s, MLA_WIDTH), F32), jax.ShapeDtypeStruct((s, MLA_HEADS * HEAD_PAD), F32)],
        compiler_params=pltpu.CompilerParams(dimension_semantics=("parallel", "parallel"),
                                             vmem_limit_bytes=VMEM_LIMIT),
    )(q, k, v)


def _mid(o, gattn, u, gpool, gmerge, x, target, pool_w, pool_scale, w_ba, w_bp, w_out, norm_final, tm):
    s = x.shape[0]
    n_tiles = s // tm
    halo_per_tile = tm // POOL_HALO

    def body(o_ref, ga_ref, u_ref, uh_ref, gp_ref, gm_ref, x_ref, t_ref, pw_ref, ps_ref, wba_ref, wbp_ref,
             wout_ref, gf_ref,
             do_ref, dl_ref, dga_ref, dgp_ref, dgm_ref, ddc_ref, dh_ref,
             loss_ref, dwout_out, dwba_out, dwbp_out, dpw_ref, dps_ref, dgf_ref,
             ubuf, dwout_ref, dwba_ref, dwbp_ref):
        i = pl.program_id(0)

        @pl.when(i == 0)
        def _():
            loss_ref[...] = jnp.zeros_like(loss_ref)
            dwout_ref[...] = jnp.zeros_like(dwout_ref)
            dwba_ref[...] = jnp.zeros_like(dwba_ref)
            dwbp_ref[...] = jnp.zeros_like(dwbp_ref)
            dpw_ref[...] = jnp.zeros_like(dpw_ref)
            dps_ref[...] = jnp.zeros_like(dps_ref)
            dgf_ref[...] = jnp.zeros_like(dgf_ref)

        o = o_ref[...]
        ga = ga_ref[...]
        sga = _sigmoid(ga)
        silu_a = ga * sga
        y_attn = (o * silu_a).astype(BF16)

        ubuf[0:POOL_HALO, :] = jnp.where(i > 0, uh_ref[...], 0.0)
        ubuf[POOL_HALO:, :] = u_ref[...]
        row = lax.broadcasted_iota(jnp.int32, (tm, POOL_GROUP_DIM), 0) + i * tm
        ps = ps_ref[...]
        gp = gp_ref[...]
        sgp = _sigmoid(gp)
        silu_p = gp * sgp
        d_bf, dm, inv_cnt = [], [], []
        for g, w in enumerate(POOL_WINDOWS):
            cols = slice(g * POOL_GROUP_DIM, (g + 1) * POOL_GROUP_DIM)
            wsum = ubuf[POOL_HALO:, cols]
            for kk in range(1, w):
                wsum = wsum + ubuf[POOL_HALO - kk:POOL_HALO - kk + tm, cols]
            inv = 1.0 / jnp.minimum(row + 1, w).astype(F32)
            dg = (wsum * inv - ubuf[POOL_HALO:, cols]).astype(BF16)
            d_bf.append(dg)
            inv_cnt.append(inv)
            dm.append(_dot(dg, pw_ref[g]))
        dm = jnp.concatenate(dm, axis=1)
        yp = dm * ps
        y_pool = (yp * silu_p).astype(BF16)

        a = jnp.concatenate([_dot(y_attn, wba_ref[j]) for j in range(N_CHIPS)], axis=1)
        p = jnp.concatenate([_dot(y_pool, wbp_ref[j]) for j in range(N_CHIPS)], axis=1)
        gate_a = _sigmoid(gm_ref[:, :D_MODEL])
        gate_p = _sigmoid(gm_ref[:, D_MODEL:])
        merged = (gate_a * a + gate_p * p).astype(BF16)
        h = x_ref[...] + _dot(merged, wout_ref[...])
        gf = gf_ref[...]
        y, xhat, r = _rms_fwd(h, gf)
        err = y - t_ref[...]
        e2 = err * err
        e2 = jnp.sum(e2.reshape(tm // 8, 8, D_MODEL), axis=0)
        acc = e2[:, 0:LANES]
        for cidx in range(1, D_MODEL // LANES):
            acc = acc + e2[:, cidx * LANES:(cidx + 1) * LANES]
        loss_ref[...] += acc

        dy = err * (1.0 / D_MODEL)
        dgf_ref[...] += _colsum(dy * xhat)
        dh = _rms_bwd(dy, xhat, r, gf)
        dh_ref[...] = dh
        dh_bf = dh.astype(BF16)
        dwout_ref[...] += _dot_tn(merged, dh_bf)
        dmerged = _dot_nt(dh_bf, wout_ref[...])
        da = (dmerged * gate_a).astype(BF16)
        dp = (dmerged * gate_p).astype(BF16)
        dgm_ref[:, :D_MODEL] = (dmerged * a * gate_a * (1.0 - gate_a)).astype(BF16)
        dgm_ref[:, D_MODEL:] = (dmerged * p * gate_p * (1.0 - gate_p)).astype(BF16)
        dy_attn = dy_pool = None
        for j in range(N_CHIPS):
            cols = slice(j * BRANCH_COLS, (j + 1) * BRANCH_COLS)
            dwba_ref[j] += _dot_tn(y_attn, da[:, cols])
            dwbp_ref[j] += _dot_tn(y_pool, dp[:, cols])
            pa = _dot_nt(da[:, cols], wba_ref[j])
            pp = _dot_nt(dp[:, cols], wbp_ref[j])
            dy_attn = pa if dy_attn is None else dy_attn + pa
            dy_pool = pp if dy_pool is None else dy_pool + pp

        do = dy_attn * silu_a
        do_ref[...] = do
        dga_ref[...] = (dy_attn * o * (sga * (1.0 + ga * (1.0 - sga)))).astype(BF16)
        doo = do * o
        for hd in range(MLA_HEADS):
            dl = jnp.sum(doo[:, hd * V_HEAD_DIM:(hd + 1) * V_HEAD_DIM], axis=1, keepdims=True)
            dl_ref[:, hd * HEAD_PAD:(hd + 1) * HEAD_PAD] = jnp.broadcast_to(dl, (tm, HEAD_PAD))

        dyp = dy_pool * silu_p
        dgp_ref[...] = (dy_pool * yp * (sgp * (1.0 + gp * (1.0 - sgp)))).astype(BF16)
        dps_ref[...] += _colsum(dyp * dm)
        dmm = (dyp * ps).astype(BF16)
        for g in range(len(POOL_WINDOWS)):
            cols = slice(g * POOL_GROUP_DIM, (g + 1) * POOL_GROUP_DIM)
            dpw_ref[g] += _dot_tn(d_bf[g], dmm[:, cols])
            ddc_ref[:, cols] = _dot_nt(dmm[:, cols], pw_ref[g]) * inv_cnt[g]

        @pl.when(i == n_tiles - 1)
        def _():
            dwout_out[...] = dwout_ref[...].astype(BF16)
            dwba_out[...] = dwba_ref[...].astype(BF16)
            dwbp_out[...] = dwbp_ref[...].astype(BF16)

    row_in = lambda n: _row_spec(tm, n)
    in_specs = [
        row_in(MLA_WIDTH), row_in(MLA_WIDTH), row_in(POOL_WIDTH),
        pl.BlockSpec((POOL_HALO, POOL_WIDTH), lambda i: (jnp.maximum(i * halo_per_tile - 1, 0), 0)),
        row_in(POOL_WIDTH), row_in(2 * D_MODEL), row_in(D_MODEL), row_in(D_MODEL),
        _full_spec((4, POOL_GROUP_DIM, POOL_GROUP_DIM)), _full_spec((1, POOL_WIDTH)),
        _full_spec((N_CHIPS, MLA_WIDTH, BRANCH_COLS)), _full_spec((N_CHIPS, POOL_WIDTH, BRANCH_COLS)),
        _full_spec((D_MODEL, D_MODEL)), _full_spec((1, D_MODEL)),
    ]
    out_shape = [
        jax.ShapeDtypeStruct((s, MLA_WIDTH), F32),
        jax.ShapeDtypeStruct((s, MLA_HEADS * HEAD_PAD), F32),
        jax.ShapeDtypeStruct((s, MLA_WIDTH), BF16),
        jax.ShapeDtypeStruct((s, POOL_WIDTH), BF16),
        jax.ShapeDtypeStruct((s, 2 * D_MODEL), BF16),
        jax.ShapeDtypeStruct((s, POOL_WIDTH), F32),
        jax.ShapeDtypeStruct((s, D_MODEL), F32),
        jax.ShapeDtypeStruct((8, LANES), F32),
        jax.ShapeDtypeStruct((D_MODEL, D_MODEL), BF16),
        jax.ShapeDtypeStruct((N_CHIPS, MLA_WIDTH, BRANCH_COLS), BF16),
        jax.ShapeDtypeStruct((N_CHIPS, POOL_WIDTH, BRANCH_COLS), BF16),
        jax.ShapeDtypeStruct((4, POOL_GROUP_DIM, POOL_GROUP_DIM), F32),
        jax.ShapeDtypeStruct((1, POOL_WIDTH), F32),
        jax.ShapeDtypeStruct((1, D_MODEL), F32),
    ]
    out_specs = [
        row_in(MLA_WIDTH), row_in(MLA_HEADS * HEAD_PAD), row_in(MLA_WIDTH), row_in(POOL_WIDTH),
        row_in(2 * D_MODEL), row_in(POOL_WIDTH), row_in(D_MODEL),
        _full_spec((8, LANES)), _full_spec((D_MODEL, D_MODEL)), _full_spec((N_CHIPS, MLA_WIDTH, BRANCH_COLS)),
        _full_spec((N_CHIPS, POOL_WIDTH, BRANCH_COLS)), _full_spec((4, POOL_GROUP_DIM, POOL_GROUP_DIM)),
        _full_spec((1, POOL_WIDTH)), _full_spec((1, D_MODEL)),
    ]
    return pl.pallas_call(
        body,
        name="mid",
        grid=(n_tiles,),
        in_specs=in_specs,
        out_specs=out_specs,
        out_shape=out_shape,
        scratch_shapes=[
            pltpu.VMEM((tm + POOL_HALO, POOL_WIDTH), F32),
            pltpu.VMEM((D_MODEL, D_MODEL), F32),
            pltpu.VMEM((N_CHIPS, MLA_WIDTH, BRANCH_COLS), F32),
            pltpu.VMEM((N_CHIPS, POOL_WIDTH, BRANCH_COLS), F32),
        ],
        compiler_params=pltpu.CompilerParams(dimension_semantics=("arbitrary",), vmem_limit_bytes=VMEM_LIMIT),
    )(o, gattn, u, u, gpool, gmerge, x, target, pool_w, pool_scale, w_ba, w_bp, w_out, norm_final)


def _attn_bwd(q, k, v, do, lse, delta, t):
    s = q.shape[0]
    pairs = MLA_HEADS // 2

    def body(q_ref, do_ref, lse_ref, dl_ref, k_ref, v_ref, dq_ref, dk_ref, dv_ref):
        i = pl.program_id(1)

        @pl.when(i == 0)
        def _():
            dk_ref[...] = jnp.zeros_like(dk_ref)
            dv_ref[...] = jnp.zeros_like(dv_ref)

        mask = _chunk_mask(t)
        for hh in range(2):
            qc = slice(hh * HEAD_PAD, (hh + 1) * HEAD_PAD)
            vc = slice(hh * V_HEAD_DIM, (hh + 1) * V_HEAD_DIM)
            qh = q_ref[:, qc]
            doh = do_ref[:, vc].astype(BF16)
            lse_c = lse_ref[:, hh * HEAD_PAD:hh * HEAD_PAD + 1]
            dl_c = dl_ref[:, hh * HEAD_PAD:hh * HEAD_PAD + 1]

            def step(j, dq, masked):
                rows = pl.ds(pl.multiple_of(j * t, t), t)
                kj = k_ref[rows, qc]
                vj = v_ref[rows, vc]
                sc = _dot_nt(qh, kj) * ATT_SCALE
                p = jnp.exp(sc - lse_c)
                if masked:
                    p = jnp.where(mask, p, 0.0)
                dp = _dot_nt(doh, vj)
                ds = (p * (dp - dl_c) * ATT_SCALE).astype(BF16)
                dv_ref[rows, vc] += _dot_tn(p.astype(BF16), doh)
                dk_ref[rows, qc] += _dot_tn(ds, qh)
                return dq + _dot(ds, kj)

            dq = lax.fori_loop(0, i, functools.partial(step, masked=False), jnp.zeros((t, HEAD_PAD), F32))
            dq_ref[:, qc] = step(i, dq, True)

    hw = MLA_HEADS * HEAD_PAD
    return pl.pallas_call(
        body,
        name="attn_bwd",
        grid=(pairs, s // t),
        in_specs=[
            pl.BlockSpec((t, 2 * HEAD_PAD), lambda p, i: (i, p)),
            pl.BlockSpec((t, 2 * V_HEAD_DIM), lambda p, i: (i, p)),
            pl.BlockSpec((t, 2 * HEAD_PAD), lambda p, i: (i, p)),
            pl.BlockSpec((t, 2 * HEAD_PAD), lambda p, i: (i, p)),
            pl.BlockSpec((s, 2 * HEAD_PAD), lambda p, i: (0, p)),
            pl.BlockSpec((s, 2 * V_HEAD_DIM), lambda p, i: (0, p)),
        ],
        out_specs=[
            pl.BlockSpec((t, 2 * HEAD_PAD), lambda p, i: (i, p)),
            pl.BlockSpec((s, 2 * HEAD_PAD), lambda p, i: (0, p)),
            pl.BlockSpec((s, 2 * V_HEAD_DIM), lambda p, i: (0, p)),
        ],
        out_shape=[jax.ShapeDtypeStruct((s, hw), F32), jax.ShapeDtypeStruct((s, hw), F32),
                   jax.ShapeDtypeStruct((s, MLA_WIDTH), F32)],
        compiler_params=pltpu.CompilerParams(dimension_semantics=("parallel", "arbitrary"),
                                             vmem_limit_bytes=VMEM_LIMIT),
    )(q, do, lse, delta, k, v)


def _qkv_bwd(dq, dk, dv, zq, zkv, q_norm, kv_norm, wuq_p, wk_p, wv, rc, rsa, rsb, tm):
    s = zq.shape[0]
    hw = MLA_HEADS * HEAD_PAD

    def body(dq_ref, dk_ref, dv_ref, zq_ref, zkv_ref, gq_ref, gkv_ref, wuq_ref, wk_ref, wv_ref,
             c_ref, sa_ref, sb_ref,
             dzq_ref, dzkv_ref, dzkr_ref, dwuq_ref, dwk_ref, dwv_ref, dgq_ref, dgkv_ref):
        i = pl.program_id(0)

        @pl.when(i == 0)
        def _():
            dwuq_ref[...] = jnp.zeros_like(dwuq_ref)
            dwk_ref[...] = jnp.zeros_like(dwk_ref)
            dwv_ref[...] = jnp.zeros_like(dwv_ref)
            dgq_ref[...] = jnp.zeros_like(dgq_ref)
            dgkv_ref[...] = jnp.zeros_like(dgkv_ref)

        c, sa, sb = c_ref[...], sa_ref[...], sb_ref[...]
        gq, gkv = gq_ref[...], gkv_ref[...]

        cq, xq, rq = _rms_fwd(zq_ref[...], gq)
        dqp = jnp.concatenate(
            [_unrope(dq_ref[:, h * HEAD_PAD:(h + 1) * HEAD_PAD], c, sa, sb) for h in range(MLA_HEADS)],
            axis=1).astype(BF16)
        dwuq_ref[...] += _dot_tn(cq.astype(BF16), dqp)
        dcq = _dot_nt(dqp, wuq_ref[...])
        dgq_ref[...] += _colsum(dcq * xq)
        dzq_ref[...] = _rms_bwd(dcq, xq, rq, gq).astype(BF16)

        ckv, xkv, rkv = _rms_fwd(zkv_ref[...], gkv)
        ckv = ckv.astype(BF16)
        dkf = dk_ref[...]
        dk_bf = dkf.astype(BF16)
        dv_bf = dv_ref[...].astype(BF16)
        dwk_ref[...] += _dot_tn(ckv, dk_bf)
        dwv_ref[...] += _dot_tn(ckv, dv_bf)
        dckv = _dot_nt(dk_bf, wk_ref[...]) + _dot_nt(dv_bf, wv_ref[...])
        dgkv_ref[...] += _colsum(dckv * xkv)
        dzkv_ref[...] = _rms_bwd(dckv, xkv, rkv, gkv).astype(BF16)

        dkr = dkf[:, 0:HEAD_PAD]
        for h in range(1, MLA_HEADS):
            dkr = dkr + dkf[:, h * HEAD_PAD:(h + 1) * HEAD_PAD]
        dkr = pltpu.roll(_unrope(dkr, c, sa, sb), 64, 1)
        lane = lax.broadcasted_iota(jnp.int32, (tm, HEAD_PAD), 1)
        dzkr_ref[...] = jnp.where(lane < QK_ROPE_DIM, dkr, 0.0).astype(BF16)

    return pl.pallas_call(
        body,
        name="qkv_bwd",
        grid=(s // tm,),
        in_specs=[
            _row_spec(tm, hw), _row_spec(tm, hw), _row_spec(tm, MLA_WIDTH),
            _row_spec(tm, Q_LORA_RANK), _row_spec(tm, KV_LORA_RANK),
            _full_spec((1, Q_LORA_RANK)), _full_spec((1, KV_LORA_RANK)),
            _full_spec((Q_LORA_RANK, hw)), _full_spec((KV_LORA_RANK, hw)), _full_spec((KV_LORA_RANK, MLA_WIDTH)),
            _row_spec(tm, HEAD_PAD), _row_spec(tm, HEAD_PAD), _row_spec(tm, HEAD_PAD),
        ],
        out_specs=[
            _row_spec(tm, Q_LORA_RANK), _row_spec(tm, KV_LORA_RANK), _row_spec(tm, HEAD_PAD),
            _full_spec((Q_LORA_RANK, hw)), _full_spec((KV_LORA_RANK, hw)), _full_spec((KV_LORA_RANK, MLA_WIDTH)),
            _full_spec((1, Q_LORA_RANK)), _full_spec((1, KV_LORA_RANK)),
        ],
        out_shape=[
            jax.ShapeDtypeStruct((s, Q_LORA_RANK), BF16), jax.ShapeDtypeStruct((s, KV_LORA_RANK), BF16),
            jax.ShapeDtypeStruct((s, HEAD_PAD), BF16),
            jax.ShapeDtypeStruct((Q_LORA_RANK, hw), F32), jax.ShapeDtypeStruct((KV_LORA_RANK, hw), F32),
            jax.ShapeDtypeStruct((KV_LORA_RANK, MLA_WIDTH), F32),
            jax.ShapeDtypeStruct((1, Q_LORA_RANK), F32), jax.ShapeDtypeStruct((1, KV_LORA_RANK), F32),
        ],
        compiler_params=pltpu.CompilerParams(dimension_semantics=("arbitrary",), vmem_limit_bytes=VMEM_LIMIT),
    )(dq, dk, dv, zq, zkv, q_norm, kv_norm, wuq_p, wk_p, wv, rc, rsa, rsb)


def _inproj_bwd(dzq, dzkv, dzkr, dgattn, ddc, dgpool, dgmerge, hn, x, dh, norm_in, w_in_t, tm):
    s = x.shape[0]
    n_tiles = s // tm
    halo_per_tile = tm // POOL_HALO
    n_halo = s // POOL_HALO
    u_seg = 4

    def body(dzq_ref, dzkv_ref, dzkr_ref, dga_ref, ddc_ref, ddn_ref, dgp_ref, dgm_ref, hn_ref, x_ref, dh_ref,
             g_ref, w_hbm, gx_ref, dgin_ref, dw_hbm, w_vmem, dw_acc, dbuf, sem):
        i = pl.program_id(0)

        @pl.when(i == 0)
        def _():
            cp = pltpu.make_async_copy(w_hbm, w_vmem, sem)
            cp.start()
            dw_acc[...] = jnp.zeros_like(dw_acc)
            dgin_ref[...] = jnp.zeros_like(dgin_ref)
            cp.wait()

        dbuf[0:tm, :] = ddc_ref[...]
        dbuf[tm:, :] = jnp.where(i < n_tiles - 1, ddn_ref[...], 0.0)
        row = lax.broadcasted_iota(jnp.int32, (tm, POOL_GROUP_DIM), 0) + i * tm
        du = []
        for g, w in enumerate(POOL_WINDOWS):
            cols = slice(g * POOL_GROUP_DIM, (g + 1) * POOL_GROUP_DIM)
            fsum = dbuf[0:tm, cols]
            for kk in range(1, w):
                fsum = fsum + dbuf[kk:kk + tm, cols]
            du.append(fsum - dbuf[0:tm, cols] * jnp.minimum(row + 1, w).astype(F32))
        du = jnp.concatenate(du, axis=1).astype(BF16)

        dz = [dzq_ref[...], dzkv_ref[...], dzkr_ref[...], dga_ref[...], du, dgp_ref[...], dgm_ref[...]]
        hn = hn_ref[...]
        dhn = None
        for j, pieces in enumerate(SHARD_PIECES):
            parts = [dz[seg][:, lo:hi] for seg, lo, hi, _ in pieces]
            dzj = parts[0] if len(parts) == 1 else jnp.concatenate(parts, axis=1)
            part = _dot(dzj, w_vmem[j])
            dhn = part if dhn is None else dhn + part
            dw_acc[j] += _dot_tn(dzj, hn)

        g = g_ref[...]
        _, xhat, r = _rms_fwd(x_ref[...], g)
        dgin_ref[...] += _colsum(dhn * xhat)
        gx_ref[...] = dh_ref[...] + _rms_bwd(dhn, xhat, r, g)

        @pl.when(i == n_tiles - 1)
        def _():
            for j in range(N_CHIPS):
                w_vmem[j] = dw_acc[j].astype(BF16)
            cp = pltpu.make_async_copy(w_vmem, dw_hbm, sem)
            cp.start()
            cp.wait()

    any_spec = pl.BlockSpec(memory_space=pl.ANY)
    seg_w = [wide for _, wide in IN_SEGMENTS]
    return pl.pallas_call(
        body,
        name="inproj_bwd",
        grid=(n_tiles,),
        in_specs=[
            _row_spec(tm, seg_w[0]), _row_spec(tm, seg_w[1]), _row_spec(tm, seg_w[2]),
            _row_spec(tm, seg_w[3]), _row_spec(tm, seg_w[u_seg]),
            pl.BlockSpec((POOL_HALO, POOL_WIDTH), lambda i: (jnp.minimum((i + 1) * halo_per_tile, n_halo - 1), 0)),
            _row_spec(tm, seg_w[5]), _row_spec(tm, seg_w[6]),
            _row_spec(tm, D_MODEL), _row_spec(tm, D_MODEL), _row_spec(tm, D_MODEL),
            _full_spec((1, D_MODEL)), any_spec,
        ],
        out_specs=[_row_spec(tm, D_MODEL), _full_spec((1, D_MODEL)), any_spec],
        out_shape=[jax.ShapeDtypeStruct((s, D_MODEL), F32), jax.ShapeDtypeStruct((1, D_MODEL), F32),
                   jax.ShapeDtypeStruct((N_CHIPS, SHARD_COLS, D_MODEL), BF16)],
        scratch_shapes=[
            pltpu.VMEM((N_CHIPS, SHARD_COLS, D_MODEL), BF16),
            pltpu.VMEM((N_CHIPS, SHARD_COLS, D_MODEL), F32),
            pltpu.VMEM((tm + POOL_HALO, POOL_WIDTH), F32),
            pltpu.SemaphoreType.DMA,
        ],
        compiler_params=pltpu.CompilerParams(dimension_semantics=("arbitrary",), vmem_limit_bytes=VMEM_LIMIT),
    )(dzq, dzkv, dzkr, dgattn, ddc, ddc, dgpool, dgmerge, hn, x, dh, norm_in, w_in_t)


def _other_chips(x, y):
    return ((1 - x, y), (x, 1 - y), (1 - x, 1 - y))


def _half(ref, axis, size, c, lead=()):
    window = pl.ds(pl.multiple_of(c * size, size), size)
    if axis == 0:
        return ref.at[(*lead, window, slice(None))]
    return ref.at[(*lead, slice(None), window)]


def _half_shape(rows, cols, axis, size):
    return (size, cols) if axis == 0 else (rows, size)


def _weight_gather(shards):
    n = len(COMM_PARAMS)

    def body(*refs):
        ins, outs = refs[:n], refs[n:2 * n]
        send_sems, recv_sems, local_sems = refs[2 * n:]
        x, y, c = lax.axis_index("x"), lax.axis_index("y"), lax.axis_index("c")
        k = 2 * x + y
        chips = _other_chips(x, y)

        def copy(sem, src, dst, to):
            return pltpu.make_async_remote_copy(src_ref=src, dst_ref=dst, send_sem=send_sems.at[sem],
                                                recv_sem=recv_sems.at[sem], device_id=to, device_id_type=MESH)

        local, first, passed = [], [], []
        for p in range(n):
            cp = pltpu.make_async_copy(ins[p], outs[p].at[k], local_sems.at[p])
            cp.start()
            local.append(cp)
        for p, (_, _, _, axis, size) in enumerate(COMM_PARAMS):
            for j, (cx, cy) in enumerate(chips):
                cp = copy(6 * p + j, _half(ins[p], axis, size, c), _half(outs[p], axis, size, c, (k,)), (cx, cy, c))
                cp.start()
                first.append(cp)
        for j, (cx, cy) in enumerate(chips):
            kj = 2 * cx + cy
            for p, (_, _, _, axis, size) in enumerate(COMM_PARAMS):
                landed = _half(outs[p], axis, size, c, (kj,))
                copy(6 * p + j, landed, landed, (x, y, c)).wait_recv()
                cp = copy(6 * p + 3 + j, landed, landed, (x, y, 1 - c))
                cp.start()
                passed.append(cp)
        for j, (cx, cy) in enumerate(chips):
            kj = 2 * cx + cy
            for p, (_, _, _, axis, size) in enumerate(COMM_PARAMS):
                theirs = _half(outs[p], axis, size, 1 - c, (kj,))
                copy(6 * p + 3 + j, theirs, theirs, (x, y, c)).wait_recv()
        for cp in first + passed:
            cp.wait_send()
        for cp in local:
            cp.wait()

    any_spec = pl.BlockSpec(memory_space=pl.ANY)
    return pl.pallas_call(
        body,
        name="weight_gather",
        in_specs=[any_spec] * n,
        out_specs=[any_spec] * n,
        out_shape=[jax.ShapeDtypeStruct((N_CHIPS, r, cc), BF16) for _, r, cc, _, _ in COMM_PARAMS],
        scratch_shapes=[pltpu.SemaphoreType.DMA((6 * n,)), pltpu.SemaphoreType.DMA((6 * n,)),
                        pltpu.SemaphoreType.DMA((n,))],
    )(*shards)


def _grad_reduce(grads, gs):
    n = len(COMM_PARAMS)
    n_small = N_DEV - 1

    def body(*refs):
        g_in, gs_ref = refs[:n], refs[n]
        g_out, gsum_ref = refs[n + 1:2 * n + 1], refs[2 * n + 1]
        scratch = refs[2 * n + 2:]
        pm, a_buf, b_buf, r_buf = scratch[0:n], scratch[n:2 * n], scratch[2 * n:3 * n], scratch[3 * n:4 * n]
        s_buf, send_sems, recv_sems, local_sems = scratch[4 * n:]
        x, y, c = lax.axis_index("x"), lax.axis_index("y"), lax.axis_index("c")
        k = 2 * x + y
        me = 4 * x + 2 * y + c
        chips = _other_chips(x, y)

        def copy(sem, src, dst, to):
            return pltpu.make_async_remote_copy(src_ref=src, dst_ref=dst, send_sem=send_sems.at[sem],
                                                recv_sem=recv_sems.at[sem], device_id=to, device_id_type=MESH)

        loads, sends = [], []
        for p, (_, _, _, axis, size) in enumerate(COMM_PARAMS):
            ld = pltpu.make_async_copy(_half(g_in[p], axis, size, c, (slice(None),)), pm[p], local_sems.at[p])
            ld.start()
            loads.append(ld)
            cp = copy(5 * p, _half(g_in[p], axis, size, 1 - c, (slice(None),)), a_buf[p], (x, y, 1 - c))
            cp.start()
            sends.append(cp)
        flips = [(fx, fy, fc) for fx in (0, 1) for fy in (0, 1) for fc in (0, 1)][1:]
        for f, (fx, fy, fc) in enumerate(flips, start=1):
            peer = (1 - x if fx else x, 1 - y if fy else y, 1 - c if fc else c)
            cp = copy(5 * n + f - 1, gs_ref, s_buf.at[f], peer)
            cp.start()
            sends.append(cp)
        s_buf[0] = gs_ref[...]

        for p in range(n):
            loads[p].wait()
            copy(5 * p, a_buf[p], a_buf[p], (x, y, c)).wait_recv()
            for j, (cx, cy) in enumerate(chips):
                kj = 2 * cx + cy
                pm[p][kj] = (pm[p][kj].astype(F32) + a_buf[p][kj].astype(F32)).astype(BF16)
                cp = copy(5 * p + 1 + j, pm[p].at[kj], b_buf[p].at[j], (cx, cy, c))
                cp.start()
                sends.append(cp)
            r_buf[p][...] = pm[p][k].astype(F32) + a_buf[p][k].astype(F32)

        stores = []
        for p, (_, _, _, axis, size) in enumerate(COMM_PARAMS):
            for j in range(3):
                copy(5 * p + 1 + j, b_buf[p].at[j], b_buf[p].at[j], (x, y, c)).wait_recv()
                r_buf[p][...] = r_buf[p][...] + b_buf[p][j].astype(F32)
            st = pltpu.make_async_copy(r_buf[p], _half(g_out[p], axis, size, c), local_sems.at[n + p])
            st.start()
            stores.append(st)
            cp = copy(5 * p + 4, r_buf[p], _half(g_out[p], axis, size, c), (x, y, 1 - c))
            cp.start()
            sends.append(cp)

        for f in range(1, N_DEV):
            copy(5 * n + f - 1, s_buf.at[f], s_buf.at[f], (x, y, c)).wait_recv()
        total = s_buf[me]
        for d in range(1, N_DEV):
            total = total + s_buf[jnp.bitwise_xor(me, d)]
        gsum_ref[...] = total

        for p, (_, _, _, axis, size) in enumerate(COMM_PARAMS):
            theirs = _half(g_out[p], axis, size, 1 - c)
            copy(5 * p + 4, theirs, theirs, (x, y, c)).wait_recv()
            stores[p].wait()
        for cp in sends:
            cp.wait_send()

    any_spec = pl.BlockSpec(memory_space=pl.ANY)
    vmem_spec = pl.BlockSpec(memory_space=pltpu.VMEM)
    halves = [_half_shape(r, cc, axis, size) for _, r, cc, axis, size in COMM_PARAMS]
    scratch = ([pltpu.VMEM((N_CHIPS, *h), BF16) for h in halves]
               + [pltpu.VMEM((N_CHIPS, *h), BF16) for h in halves]
               + [pltpu.VMEM((3, *h), BF16) for h in halves]
               + [pltpu.VMEM(h, F32) for h in halves]
               + [pltpu.VMEM((N_DEV, SMALL_ROWS, LANES), F32),
                  pltpu.SemaphoreType.DMA((5 * n + n_small,)), pltpu.SemaphoreType.DMA((5 * n + n_small,)),
                  pltpu.SemaphoreType.DMA((2 * n,))])
    out = pl.pallas_call(
        body,
        name="grad_reduce",
        in_specs=[any_spec] * n + [vmem_spec],
        out_specs=[any_spec] * n + [vmem_spec],
        out_shape=[jax.ShapeDtypeStruct((r, cc), F32) for _, r, cc, _, _ in COMM_PARAMS]
        + [jax.ShapeDtypeStruct((SMALL_ROWS, LANES), F32)],
        scratch_shapes=scratch,
        compiler_params=pltpu.CompilerParams(vmem_limit_bytes=VMEM_LIMIT),
    )(*grads, gs)
    return out[:n], out[n]


def _adamw_math(w, g, m, v):
    m = ADAM_B1 * m + (1.0 - ADAM_B1) * g
    v = ADAM_B2 * v + (1.0 - ADAM_B2) * (g * g)
    m_hat = m / (1.0 - ADAM_B1 ** ADAM_STEP)
    v_hat = v / (1.0 - ADAM_B2 ** ADAM_STEP)
    delta = -ADAM_LR * (m_hat / (jnp.sqrt(v_hat) + ADAM_EPS) + ADAM_WD * w)
    return delta, m, v


def _adamw_tiled(w, g, m, v, tm):
    rows, cols = w.shape

    def body(w_ref, g_ref, m_ref, v_ref, d_ref, nm_ref, nv_ref):
        d_ref[...], nm_ref[...], nv_ref[...] = _adamw_math(w_ref[...], g_ref[...], m_ref[...], v_ref[...])

    spec = _row_spec(tm, cols)
    return pl.pallas_call(
        body,
        name="adamw_w_in",
        grid=(rows // tm,),
        in_specs=[spec] * 4,
        out_specs=[spec] * 3,
        out_shape=[jax.ShapeDtypeStruct(w.shape, F32)] * 3,
        compiler_params=pltpu.CompilerParams(dimension_semantics=("parallel",), vmem_limit_bytes=VMEM_LIMIT),
    )(w, g, m, v)


def _adamw_many(ws, gs, ms, vs):
    n = len(ws)

    def body(*refs):
        ins, outs = refs[:4 * n], refs[4 * n:]
        for i in range(n):
            d, nm, nv = _adamw_math(ins[i][...], ins[n + i][...], ins[2 * n + i][...], ins[3 * n + i][...])
            outs[i][...] = d
            outs[n + i][...] = nm
            outs[2 * n + i][...] = nv

    vmem_spec = pl.BlockSpec(memory_space=pltpu.VMEM)
    shapes = [jax.ShapeDtypeStruct(w.shape, F32) for w in ws]
    out = pl.pallas_call(
        body,
        name="adamw_small",
        in_specs=[vmem_spec] * (4 * n),
        out_specs=[vmem_spec] * (3 * n),
        out_shape=shapes * 3,
        compiler_params=pltpu.CompilerParams(vmem_limit_bytes=VMEM_LIMIT),
    )(*ws, *gs, *ms, *vs)
    return out[:n], out[n:2 * n], out[2 * n:]


def _pack_rows(parts, rows, dtype):
    flat = jnp.concatenate([p.reshape(-1).astype(dtype) for p in parts])
    flat = jnp.concatenate([flat, jnp.zeros((rows * LANES - flat.shape[0],), dtype)])
    return flat.reshape(rows, LANES)


def _unpack_rows(packed, shapes):
    flat = packed.reshape(-1)
    out, off = [], 0
    for _, shp in shapes:
        n = int(np.prod(shp))
        out.append(flat[off:off + n].reshape(shp))
        off += n
    return out


def _rope_tables(s):
    half = QK_ROPE_DIM // 2
    inv_freq = ROPE_THETA ** (-jnp.arange(half, dtype=F32) / half)
    ang = jnp.arange(s, dtype=F32)[:, None] * inv_freq[None, :]
    cos, sin = jnp.cos(ang), jnp.sin(ang)
    z16 = jnp.zeros((s, half), F32)
    z32 = jnp.zeros((s, HEAD_PAD - QK_NOPE_DIM - QK_ROPE_DIM), F32)
    z64 = jnp.zeros((s, QK_NOPE_DIM), F32)
    rc = jnp.concatenate([jnp.ones((s, QK_NOPE_DIM), F32), cos, cos, z32], axis=1)
    rsa = jnp.concatenate([z64, -sin, z16, z32], axis=1)
    rsb = jnp.concatenate([z64, z16, sin, z32], axis=1)
    return rc, rsa, rsb


def kernel(x, norm_in, w_in, q_norm, w_uq, kv_norm, w_ukv, pool_w, pool_scale, w_branch_attn, w_branch_pool, w_out, norm_final, loss_target, m_norm_in, m_w_in, m_q_norm, m_w_uq, m_kv_norm, m_w_ukv, m_pool_w, m_pool_scale, m_w_branch_attn, m_w_branch_pool, m_w_out, m_norm_final, v_norm_in, v_w_in, v_q_norm, v_w_uq, v_kv_norm, v_w_ukv, v_pool_w, v_pool_scale, v_w_branch_attn, v_w_branch_pool, v_w_out, v_norm_final):
    s = x.shape[1]
    t_att, t_row = _tiles(s)
    x2 = x.reshape(s, D_MODEL)
    tgt = loss_target.reshape(s, D_MODEL)

    local = [w_in.T, w_uq.reshape(96, 768), w_ukv.reshape(64, 1024), w_branch_attn, w_branch_pool, w_out]
    w_in_t, w_uq_all, w_ukv_all, w_ba_all, w_bp_all, w_out_all = _weight_gather([a.astype(BF16) for a in local])
    w_uq_f = w_uq_all.reshape(Q_LORA_RANK, MLA_HEADS, QK_NOPE_DIM + QK_ROPE_DIM)
    w_ukv_f = w_ukv_all.reshape(KV_LORA_RANK, MLA_HEADS, QK_NOPE_DIM + V_HEAD_DIM)
    w_out_f = w_out_all.reshape(D_MODEL, D_MODEL)
    hw = MLA_HEADS * HEAD_PAD
    wuq_p = jnp.pad(w_uq_f, ((0, 0), (0, 0), (0, HEAD_PAD - QK_NOPE_DIM - QK_ROPE_DIM))).reshape(Q_LORA_RANK, hw)
    wk_p = jnp.pad(w_ukv_f[:, :, :QK_NOPE_DIM], ((0, 0), (0, 0), (0, HEAD_PAD - QK_NOPE_DIM))).reshape(KV_LORA_RANK, hw)
    wv = w_ukv_f[:, :, QK_NOPE_DIM:].reshape(KV_LORA_RANK, MLA_WIDTH)
    rc, rsa, rsb = _rope_tables(s)
    g_in = norm_in.reshape(1, -1)
    g_q = q_norm.reshape(1, -1)
    g_kv = kv_norm.reshape(1, -1)
    g_f = norm_final.reshape(1, -1)
    ps = pool_scale.reshape(1, -1)
    pw_bf = pool_w.astype(BF16)

    hn, zq, zkv, zkr, gattn, u, gpool, gmerge = _inproj_fwd(x2, g_in, w_in_t, t_row)
    q, k, v = _qkv_fwd(zq, zkv, zkr, g_q, g_kv, wuq_p, wk_p, wv, rc, rsa, rsb, t_row)
    o, lse = _attn_fwd(q, k, v, t_att)

    (do, delta, dgattn, dgpool, dgmerge, ddc, dh, sq_err, d_w_out, d_w_ba, d_w_bp, d_pool_w, d_pool_scale,
     d_norm_final) = _mid(o, gattn, u, gpool, gmerge, x2, tgt, pw_bf, ps, w_ba_all, w_bp_all, w_out_f, g_f, t_row)

    dq, dk, dv = _attn_bwd(q, k, v, do, lse, delta, t_att)
    dzq, dzkv, dzkr, d_wuq_p, d_wk_p, d_wv, d_q_norm, d_kv_norm = _qkv_bwd(
        dq, dk, dv, zq, zkv, g_q, g_kv, wuq_p, wk_p, wv, rc, rsa, rsb, t_row)
    grad_x, d_norm_in, d_w_in_t = _inproj_bwd(dzq, dzkv, dzkr, dgattn, ddc, dgpool, dgmerge, hn, x2, dh, g_in,
                                              w_in_t, t_row)

    d_w_uq = d_wuq_p.reshape(Q_LORA_RANK, MLA_HEADS, HEAD_PAD)[:, :, :QK_NOPE_DIM + QK_ROPE_DIM]
    d_w_ukv = jnp.concatenate([d_wk_p.reshape(KV_LORA_RANK, MLA_HEADS, HEAD_PAD)[:, :, :QK_NOPE_DIM],
                               d_wv.reshape(KV_LORA_RANK, MLA_HEADS, V_HEAD_DIM)], axis=2)
    grads_local = [d_w_in_t, d_w_uq.reshape(N_CHIPS, 96, 768).astype(BF16),
                   d_w_ukv.reshape(N_CHIPS, 64, 1024).astype(BF16), d_w_ba, d_w_bp,
                   d_w_out.reshape(N_CHIPS, 256, D_MODEL)]
    small = dict(norm_in=d_norm_in, q_norm=d_q_norm, kv_norm=d_kv_norm, pool_scale=d_pool_scale,
                 norm_final=d_norm_final, pool_w=d_pool_w)
    gs = _pack_rows([small[n] for n, _ in SMALL_SHAPES], SMALL_ROWS, F32)
    (g_w_in_t, g_w_uq, g_w_ukv, g_w_ba, g_w_bp, g_w_out), g_small = _grad_reduce(grads_local, gs)
    g_norm_in, g_q_norm, g_kv_norm, g_pool_scale, g_norm_final, g_pool_w = _unpack_rows(g_small, SMALL_SHAPES)

    dl_w_in, nm_w_in, nv_w_in = (a.T for a in _adamw_tiled(w_in.T, g_w_in_t, m_w_in.T, v_w_in.T, 152))

    def two_d(a):
        return a.reshape(1, -1) if a.ndim == 1 else a.reshape(a.shape[0], -1)

    names = ["norm_in", "q_norm", "w_uq", "kv_norm", "w_ukv", "pool_w", "pool_scale", "w_branch_attn",
             "w_branch_pool", "w_out", "norm_final"]
    ws = dict(norm_in=norm_in, q_norm=q_norm, w_uq=w_uq, kv_norm=kv_norm, w_ukv=w_ukv, pool_w=pool_w,
              pool_scale=pool_scale, w_branch_attn=w_branch_attn, w_branch_pool=w_branch_pool, w_out=w_out,
              norm_final=norm_final)
    gsd = dict(norm_in=g_norm_in, q_norm=g_q_norm, w_uq=g_w_uq, kv_norm=g_kv_norm, w_ukv=g_w_ukv, pool_w=g_pool_w,
               pool_scale=g_pool_scale, w_branch_attn=g_w_ba, w_branch_pool=g_w_bp, w_out=g_w_out,
               norm_final=g_norm_final)
    msd = dict(norm_in=m_norm_in, q_norm=m_q_norm, w_uq=m_w_uq, kv_norm=m_kv_norm, w_ukv=m_w_ukv, pool_w=m_pool_w,
               pool_scale=m_pool_scale, w_branch_attn=m_w_branch_attn, w_branch_pool=m_w_branch_pool, w_out=m_w_out,
               norm_final=m_norm_final)
    vsd = dict(norm_in=v_norm_in, q_norm=v_q_norm, w_uq=v_w_uq, kv_norm=v_kv_norm, w_ukv=v_w_ukv, pool_w=v_pool_w,
               pool_scale=v_pool_scale, w_branch_attn=v_w_branch_attn, w_branch_pool=v_w_branch_pool, w_out=v_w_out,
               norm_final=v_norm_final)
    dls, nms, nvs = _adamw_many([two_d(ws[n]) for n in names], [two_d(gsd[n]) for n in names],
                                [two_d(msd[n]) for n in names], [two_d(vsd[n]) for n in names])

    grads = dict(gsd)
    grads["w_in"] = g_w_in_t.T
    delta_w = {n: d.reshape(ws[n].shape) for n, d in zip(names, dls)}
    new_m = {n: d.reshape(ws[n].shape) for n, d in zip(names, nms)}
    new_v = {n: d.reshape(ws[n].shape) for n, d in zip(names, nvs)}
    delta_w["w_in"], new_m["w_in"], new_v["w_in"] = dl_w_in, nm_w_in, nv_w_in
    ws["w_in"] = w_in

    order = ["norm_in", "w_in", "q_norm", "w_uq", "kv_norm", "w_ukv", "pool_w", "pool_scale", "w_branch_attn",
             "w_branch_pool", "w_out", "norm_final"]
    loss = lax.psum(0.5 * jnp.sum(sq_err) / D_MODEL, ("x", "y", "c"))
    return (loss, grad_x.reshape(x.shape),
            *[grads[n].reshape(ws[n].shape) for n in order],
            *[delta_w[n] for n in order], *[new_m[n] for n in order], *[new_v[n] for n in order])
```

```python
import functools

import jax
import jax.numpy as jnp
import numpy as np
from jax import lax
from jax.experimental import pallas as pl
from jax.experimental.pallas import tpu as pltpu

F32 = jnp.float32
BF16 = jnp.bfloat16
MESH = pl.DeviceIdType.MESH

D_MODEL = 1024
CHUNK = 64
MLA_HEADS = 8
QK_NOPE_DIM = 64
QK_ROPE_DIM = 32
V_HEAD_DIM = 64
Q_LORA_RANK = 384
KV_LORA_RANK = 256
MLA_WIDTH = MLA_HEADS * V_HEAD_DIM
ROPE_THETA = 10000.0
POOL_WINDOWS = (2, 4, 8, 16)
POOL_WIDTH = 512
POOL_GROUP_DIM = 128
BRANCH_COLS = D_MODEL // 4
POOL_HALO = 16
EPS = 1e-6
IN_TOTAL = 4256
HEAD_PAD = 128
ATT_SCALE = (QK_NOPE_DIM + QK_ROPE_DIM) ** -0.5
ATT_SCALE_LOG2E = ATT_SCALE * 1.4426950408889634

ADAM_LR = 0.001
ADAM_B1 = 0.9
ADAM_B2 = 0.999
ADAM_EPS = 1e-08
ADAM_WD = 0.01
ADAM_STEP = 10

N_CHIPS = 4
N_DEV = 8
LANES = 128
VMEM_LIMIT = 60 * 1024 * 1024

IN_SEGMENTS = ((384, 384), (256, 256), (32, HEAD_PAD), (512, 512), (512, 512), (512, 512), (2048, 2048))
SHARD_COLS = IN_TOTAL // N_CHIPS


def _shard_pieces():
    bounds, off = [], 0
    for w, _ in IN_SEGMENTS:
        bounds.append((off, off + w))
        off += w
    out = []
    for j in range(N_CHIPS):
        lo, hi = SHARD_COLS * j, SHARD_COLS * (j + 1)
        out.append([(i, max(lo, a) - a, min(hi, b) - a, max(lo, a) - lo)
                    for i, (a, b) in enumerate(bounds) if max(lo, a) < min(hi, b)])
    return out


SHARD_PIECES = _shard_pieces()

COMM_PARAMS = (
    ("w_in", SHARD_COLS, D_MODEL, 1, 512),
    ("w_uq", 96, 768, 0, 48),
    ("w_ukv", 64, 1024, 0, 32),
    ("w_branch_attn", 512, 256, 0, 256),
    ("w_branch_pool", 512, 256, 0, 256),
    ("w_out", 256, 1024, 0, 128),
)

SMALL_SHAPES = (
    ("norm_in", (1024,)),
    ("q_norm", (384,)),
    ("kv_norm", (256,)),
    ("pool_scale", (512,)),
    ("norm_final", (1024,)),
    ("pool_w", (4, 128, 128)),
)
SMALL_ELEMS = sum(int(np.prod(s)) for _, s in SMALL_SHAPES)
SMALL_ROWS = -(-SMALL_ELEMS // (LANES * 8)) * 8


def _dot(a, b):
    return jnp.dot(a, b, preferred_element_type=F32)


def _dot_nt(a, b):
    return lax.dot_general(a, b, (((1,), (1,)), ((), ())), preferred_element_type=F32)


def _dot_tn(a, b):
    return lax.dot_general(a, b, (((0,), (0,)), ((), ())), preferred_element_type=F32)


def _sigmoid(x):
    return 1.0 / (1.0 + jnp.exp(-x))


def _colsum(x):
    return jnp.sum(x, axis=0, keepdims=True)


def _rms_fwd(x, g):
    r = lax.rsqrt(jnp.mean(x * x, axis=-1, keepdims=True) + EPS)
    xhat = x * r
    return xhat * g, xhat, r


def _rms_bwd(dy, xhat, r, g):
    dxhat = dy * g
    return r * (dxhat - xhat * jnp.mean(dxhat * xhat, axis=-1, keepdims=True))


def _rope(v, c, sa, sb):
    return v * c + pltpu.roll(v, 112, 1) * sa + pltpu.roll(v, 16, 1) * sb


def _unrope(d, c, sa, sb):
    return d * c + pltpu.roll(d * sa, 16, 1) + pltpu.roll(d * sb, 112, 1)


def _row_spec(tm, n):
    return pl.BlockSpec((tm, n), lambda i: (i, 0))


def _full_spec(shape):
    nd = len(shape)
    return pl.BlockSpec(shape, lambda i: (0,) * nd)


def _tiles(s):
    t_att = 512 if s >= 2048 else 128
    t_row = 256 if s >= 1024 else 128
    return t_att, t_row


def _inproj_fwd(x, norm_in, w_in_t, tm):
    s = x.shape[0]

    def body(x_ref, g_ref, w_ref, hn_ref, *z_refs):
        hn, _, _ = _rms_fwd(x_ref[...], g_ref[...])
        hn = hn.astype(BF16)
        hn_ref[...] = hn
        for j, pieces in enumerate(SHARD_PIECES):
            zj = _dot_nt(hn, w_ref[j])
            for seg, lo, hi, col in pieces:
                z_refs[seg][:, lo:hi] = zj[:, col:col + hi - lo]
        for seg, (w, wide) in enumerate(IN_SEGMENTS):
            if wide > w:
                z_refs[seg][:, w:wide] = jnp.zeros((tm, wide - w), F32)

    out_shape = [jax.ShapeDtypeStruct((s, D_MODEL), BF16)]
    out_specs = [_row_spec(tm, D_MODEL)]
    for _, wide in IN_SEGMENTS:
        out_shape.append(jax.ShapeDtypeStruct((s, wide), F32))
        out_specs.append(_row_spec(tm, wide))
    return pl.pallas_call(
        body,
        name="inproj_fwd",
        grid=(s // tm,),
        in_specs=[_row_spec(tm, D_MODEL), _full_spec((1, D_MODEL)),
                  pl.BlockSpec((N_CHIPS, SHARD_COLS, D_MODEL), lambda i: (0, 0, 0), pipeline_mode=pl.Buffered(1))],
        out_specs=out_specs,
        out_shape=out_shape,
        compiler_params=pltpu.CompilerParams(dimension_semantics=("parallel",), vmem_limit_bytes=VMEM_LIMIT),
    )(x, norm_in, w_in_t)


def _qkv_fwd(zq, zkv, zkr, q_norm, kv_norm, wuq_p, wk_p, wv, rc, rsa, rsb, tm):
    s = zq.shape[0]
    hw = MLA_HEADS * HEAD_PAD

    def body(zq_ref, zkv_ref, zkr_ref, gq_ref, gkv_ref, wuq_ref, wk_ref, wv_ref, c_ref, sa_ref, sb_ref,
             q_ref, k_ref, v_ref, qt_ref, vt_ref):
        c, sa, sb = c_ref[...], sa_ref[...], sb_ref[...]
        cq, _, _ = _rms_fwd(zq_ref[...], gq_ref[...])
        qf = _dot(cq.astype(BF16), wuq_ref[...])
        ckv, _, _ = _rms_fwd(zkv_ref[...], gkv_ref[...])
        ckv = ckv.astype(BF16)
        kn = _dot(ckv, wk_ref[...])
        kr = _rope(pltpu.roll(zkr_ref[...], 64, 1), c, sa, sb)
        for h in range(MLA_HEADS):
            cols = slice(h * HEAD_PAD, (h + 1) * HEAD_PAD)
            qh = _rope(qf[:, cols], c, sa, sb)
            q_ref[:, cols] = qh.astype(BF16)
            qt_ref[cols, :] = qh.T.astype(BF16)
            k_ref[:, cols] = (kn[:, cols] + kr).astype(BF16)
        vf = _dot(ckv, wv_ref[...])
        v_ref[...] = vf.astype(BF16)
        vt_ref[...] = vf.T.astype(BF16)

    return pl.pallas_call(
        body,
        name="qkv_fwd",
        grid=(s // tm,),
        in_specs=[
            _row_spec(tm, Q_LORA_RANK), _row_spec(tm, KV_LORA_RANK), _row_spec(tm, HEAD_PAD),
            _full_spec((1, Q_LORA_RANK)), _full_spec((1, KV_LORA_RANK)),
            _full_spec((Q_LORA_RANK, hw)), _full_spec((KV_LORA_RANK, hw)), _full_spec((KV_LORA_RANK, MLA_WIDTH)),
            _row_spec(tm, HEAD_PAD), _row_spec(tm, HEAD_PAD), _row_spec(tm, HEAD_PAD),
        ],
        out_specs=[_row_spec(tm, hw), _row_spec(tm, hw), _row_spec(tm, MLA_WIDTH),
                   pl.BlockSpec((hw, tm), lambda i: (0, i)), pl.BlockSpec((MLA_WIDTH, tm), lambda i: (0, i))],
        out_shape=[jax.ShapeDtypeStruct((s, hw), BF16), jax.ShapeDtypeStruct((s, hw), BF16),
                   jax.ShapeDtypeStruct((s, MLA_WIDTH), BF16),
                   jax.ShapeDtypeStruct((hw, s), BF16), jax.ShapeDtypeStruct((MLA_WIDTH, s), BF16)],
        compiler_params=pltpu.CompilerParams(dimension_semantics=("parallel",), vmem_limit_bytes=VMEM_LIMIT),
    )(zq, zkv, zkr, q_norm, kv_norm, wuq_p, wk_p, wv, rc, rsa, rsb)


def _chunk_mask(t, keys_on_rows):
    rows = lax.broadcasted_iota(jnp.int32, (t, t), 0) // CHUNK
    cols = lax.broadcasted_iota(jnp.int32, (t, t), 1) // CHUNK
    return rows <= cols if keys_on_rows else cols <= rows


def _attn_fwd(q_t, k, v_t, t):
    s = k.shape[0]
    pairs = MLA_HEADS // 2

    def body(qt_ref, k_ref, k2_ref, vt_ref, o_ref, lse_ref):
        i = pl.program_id(1)
        mask = _chunk_mask(t, True)
        qcs = [slice(hh * HEAD_PAD, (hh + 1) * HEAD_PAD) for hh in range(2)]
        vcs = [slice(hh * V_HEAD_DIM, (hh + 1) * V_HEAD_DIM) for hh in range(2)]
        qts = [qt_ref[qc, :] for qc in qcs]

        def step(j, carry, masked):
            keys = pl.ds(pl.multiple_of(j * t, t), t)
            out = []
            for hh in range(2):
                m, l, acc = carry[hh]
                sc = _dot(k_ref[keys, qcs[hh]], qts[hh])
                if masked:
                    sc = jnp.where(mask, sc, -jnp.inf)
                m_new = jnp.maximum(m, jnp.max(sc, axis=0, keepdims=True))
                alpha = jnp.exp2((m - m_new) * ATT_SCALE_LOG2E)
                p = jnp.exp2((_dot(k2_ref[keys, qcs[hh]], qts[hh]) - m_new) * ATT_SCALE_LOG2E)
                if masked:
                    p = jnp.where(mask, p, 0.0)
                l = alpha * l + jnp.sum(p, axis=0, keepdims=True)
                acc = alpha * acc + _dot(vt_ref[vcs[hh], keys], p.astype(BF16))
                out.append((m_new, l, acc))
            return tuple(out)

        one = (jnp.full((1, t), -jnp.inf, F32), jnp.zeros((1, t), F32), jnp.zeros((V_HEAD_DIM, t), F32))
        carry = lax.fori_loop(0, i, functools.partial(step, masked=False), (one, one))
        carry = step(i, carry, True)
        o_ref[...] = jnp.concatenate([carry[hh][2] / carry[hh][1] for hh in range(2)], axis=0).T
        for hh in range(2):
            m, l, _ = carry[hh]
            lse_ref[:, qcs[hh]] = jnp.broadcast_to(m * ATT_SCALE_LOG2E + jnp.log2(l), (HEAD_PAD, t)).T

    return pl.pallas_call(
        body,
        name="attn_fwd",
        grid=(pairs, s // t),
        in_specs=[
            pl.BlockSpec((2 * HEAD_PAD, t), lambda p, i: (p, i)),
            pl.BlockSpec((s, 2 * HEAD_PAD), lambda p, i: (0, p)),
            pl.BlockSpec((s, 2 * HEAD_PAD), lambda p, i: (0, p)),
            pl.BlockSpec((2 * V_HEAD_DIM, s), lambda p, i: (p, 0)),
        ],
        out_specs=[
            pl.BlockSpec((t, 2 * V_HEAD_DIM), lambda p, i: (i, p)),
            pl.BlockSpec((t, 2 * HEAD_PAD), lambda p, i: (i, p)),
        ],
        out_shape=[jax.ShapeDtypeStruct((s, MLA_WIDTH), F32), jax.ShapeDtypeStruct((s, MLA_HEADS * HEAD_PAD), F32)],
        compiler_params=pltpu.CompilerParams(dimension_semantics=("parallel", "parallel"),
                                             vmem_limit_bytes=VMEM_LIMIT),
    )(q_t, k, k, v_t)


def _mid(o, gattn, u, gpool, gmerge, x, target, pool_w, pool_scale, w_ba, w_bp, w_out, norm_final, tm):
    s = x.shape[0]
    n_tiles = s // tm
    halo_per_tile = tm // POOL_HALO

    def body(o_ref, ga_ref, u_ref, uh_ref, gp_ref, gm_ref, x_ref, t_ref, pw_ref, ps_ref, wba_ref, wbp_ref,
             wout_ref, gf_ref,
             do_ref, dl_ref, dga_ref, dgp_ref, dgm_ref, ddc_ref, dh_ref,
             loss_ref, dwout_out, dwba_out, dwbp_out, dpw_ref, dps_ref, dgf_ref,
             ubuf, dwout_ref, dwba_ref, dwbp_ref):
        i = pl.program_id(0)

        @pl.when(i == 0)
        def _():
            loss_ref[...] = jnp.zeros_like(loss_ref)
            dwout_ref[...] = jnp.zeros_like(dwout_ref)
            dwba_ref[...] = jnp.zeros_like(dwba_ref)
            dwbp_ref[...] = jnp.zeros_like(dwbp_ref)
            dpw_ref[...] = jnp.zeros_like(dpw_ref)
            dps_ref[...] = jnp.zeros_like(dps_ref)
            dgf_ref[...] = jnp.zeros_like(dgf_ref)

        o = o_ref[...]
        ga = ga_ref[...]
        sga = _sigmoid(ga)
        silu_a = ga * sga
        y_attn = (o * silu_a).astype(BF16)

        ubuf[0:POOL_HALO, :] = jnp.where(i > 0, uh_ref[...], 0.0)
        ubuf[POOL_HALO:, :] = u_ref[...]
        row = lax.broadcasted_iota(jnp.int32, (tm, POOL_GROUP_DIM), 0) + i * tm
        ps = ps_ref[...]
        gp = gp_ref[...]
        sgp = _sigmoid(gp)
        silu_p = gp * sgp
        d_bf, dm, inv_cnt = [], [], []
        for g, w in enumerate(POOL_WINDOWS):
            cols = slice(g * POOL_GROUP_DIM, (g + 1) * POOL_GROUP_DIM)
            wsum = ubuf[POOL_HALO:, cols]
            for kk in range(1, w):
                wsum = wsum + ubuf[POOL_HALO - kk:POOL_HALO - kk + tm, cols]
            inv = 1.0 / jnp.minimum(row + 1, w).astype(F32)
            dg = (wsum * inv - ubuf[POOL_HALO:, cols]).astype(BF16)
            d_bf.append(dg)
            inv_cnt.append(inv)
            dm.append(_dot(dg, pw_ref[g]))
        dm = jnp.concatenate(dm, axis=1)
        yp = dm * ps
        y_pool = (yp * silu_p).astype(BF16)

        a = jnp.concatenate([_dot(y_attn, wba_ref[j]) for j in range(N_CHIPS)], axis=1)
        p = jnp.concatenate([_dot(y_pool, wbp_ref[j]) for j in range(N_CHIPS)], axis=1)
        gate_a = _sigmoid(gm_ref[:, :D_MODEL])
        gate_p = _sigmoid(gm_ref[:, D_MODEL:])
        merged = (gate_a * a + gate_p * p).astype(BF16)
        h = x_ref[...] + _dot(merged, wout_ref[...])
        gf = gf_ref[...]
        y, xhat, r = _rms_fwd(h, gf)
        err = y - t_ref[...]
        e2 = err * err
        e2 = jnp.sum(e2.reshape(tm // 8, 8, D_MODEL), axis=0)
        acc = e2[:, 0:LANES]
        for cidx in range(1, D_MODEL // LANES):
            acc = acc + e2[:, cidx * LANES:(cidx + 1) * LANES]
        loss_ref[...] += acc

        dy = err * (1.0 / D_MODEL)
        dgf_ref[...] += _colsum(dy * xhat)
        dh = _rms_bwd(dy, xhat, r, gf)
        dh_ref[...] = dh
        dh_bf = dh.astype(BF16)
        dwout_ref[...] += _dot_tn(merged, dh_bf)
        dmerged = _dot_nt(dh_bf, wout_ref[...])
        da = (dmerged * gate_a).astype(BF16)
        dp = (dmerged * gate_p).astype(BF16)
        dgm_ref[:, :D_MODEL] = (dmerged * a * gate_a * (1.0 - gate_a)).astype(BF16)
        dgm_ref[:, D_MODEL:] = (dmerged * p * gate_p * (1.0 - gate_p)).astype(BF16)
        dy_attn = dy_pool = None
        for j in range(N_CHIPS):
            cols = slice(j * BRANCH_COLS, (j + 1) * BRANCH_COLS)
            dwba_ref[j] += _dot_tn(y_attn, da[:, cols])
            dwbp_ref[j] += _dot_tn(y_pool, dp[:, cols])
            pa = _dot_nt(da[:, cols], wba_ref[j])
            pp = _dot_nt(dp[:, cols], wbp_ref[j])
            dy_attn = pa if dy_attn is None else dy_attn + pa
            dy_pool = pp if dy_pool is None else dy_pool + pp

        do = dy_attn * silu_a
        do_ref[...] = do
        dga_ref[...] = (dy_attn * o * (sga * (1.0 + ga * (1.0 - sga)))).astype(BF16)
        doo = do * o
        for hd in range(MLA_HEADS):
            dl = jnp.sum(doo[:, hd * V_HEAD_DIM:(hd + 1) * V_HEAD_DIM], axis=1, keepdims=True)
            dl_ref[:, hd * HEAD_PAD:(hd + 1) * HEAD_PAD] = jnp.broadcast_to(dl, (tm, HEAD_PAD))

        dyp = dy_pool * silu_p
        dgp_ref[...] = (dy_pool * yp * (sgp * (1.0 + gp * (1.0 - sgp)))).astype(BF16)
        dps_ref[...] += _colsum(dyp * dm)
        dmm = (dyp * ps).astype(BF16)
        for g in range(len(POOL_WINDOWS)):
            cols = slice(g * POOL_GROUP_DIM, (g + 1) * POOL_GROUP_DIM)
            dpw_ref[g] += _dot_tn(d_bf[g], dmm[:, cols])
            ddc_ref[:, cols] = _dot_nt(dmm[:, cols], pw_ref[g]) * inv_cnt[g]

        @pl.when(i == n_tiles - 1)
        def _():
            dwout_out[...] = dwout_ref[...].astype(BF16)
            dwba_out[...] = dwba_ref[...].astype(BF16)
            dwbp_out[...] = dwbp_ref[...].astype(BF16)

    row_in = lambda n: _row_spec(tm, n)
    in_specs = [
        row_in(MLA_WIDTH), row_in(MLA_WIDTH), row_in(POOL_WIDTH),
        pl.BlockSpec((POOL_HALO, POOL_WIDTH), lambda i: (jnp.maximum(i * halo_per_tile - 1, 0), 0)),
        row_in(POOL_WIDTH), row_in(2 * D_MODEL), row_in(D_MODEL), row_in(D_MODEL),
        _full_spec((4, POOL_GROUP_DIM, POOL_GROUP_DIM)), _full_spec((1, POOL_WIDTH)),
        _full_spec((N_CHIPS, MLA_WIDTH, BRANCH_COLS)), _full_spec((N_CHIPS, POOL_WIDTH, BRANCH_COLS)),
        _full_spec((D_MODEL, D_MODEL)), _full_spec((1, D_MODEL)),
    ]
    out_shape = [
        jax.ShapeDtypeStruct((s, MLA_WIDTH), F32),
        jax.ShapeDtypeStruct((s, MLA_HEADS * HEAD_PAD), F32),
        jax.ShapeDtypeStruct((s, MLA_WIDTH), BF16),
        jax.ShapeDtypeStruct((s, POOL_WIDTH), BF16),
        jax.ShapeDtypeStruct((s, 2 * D_MODEL), BF16),
        jax.ShapeDtypeStruct((s, POOL_WIDTH), F32),
        jax.ShapeDtypeStruct((s, D_MODEL), F32),
        jax.ShapeDtypeStruct((8, LANES), F32),
        jax.ShapeDtypeStruct((D_MODEL, D_MODEL), BF16),
        jax.ShapeDtypeStruct((N_CHIPS, MLA_WIDTH, BRANCH_COLS), BF16),
        jax.ShapeDtypeStruct((N_CHIPS, POOL_WIDTH, BRANCH_COLS), BF16),
        jax.ShapeDtypeStruct((4, POOL_GROUP_DIM, POOL_GROUP_DIM), F32),
        jax.ShapeDtypeStruct((1, POOL_WIDTH), F32),
        jax.ShapeDtypeStruct((1, D_MODEL), F32),
    ]
    out_specs = [
        row_in(MLA_WIDTH), row_in(MLA_HEADS * HEAD_PAD), row_in(MLA_WIDTH), row_in(POOL_WIDTH),
        row_in(2 * D_MODEL), row_in(POOL_WIDTH), row_in(D_MODEL),
        _full_spec((8, LANES)), _full_spec((D_MODEL, D_MODEL)), _full_spec((N_CHIPS, MLA_WIDTH, BRANCH_COLS)),
        _full_spec((N_CHIPS, POOL_WIDTH, BRANCH_COLS)), _full_spec((4, POOL_GROUP_DIM, POOL_GROUP_DIM)),
        _full_spec((1, POOL_WIDTH)), _full_spec((1, D_MODEL)),
    ]
    return pl.pallas_call(
        body,
        name="mid",
        grid=(n_tiles,),
        in_specs=in_specs,
        out_specs=out_specs,
        out_shape=out_shape,
        scratch_shapes=[
            pltpu.VMEM((tm + POOL_HALO, POOL_WIDTH), F32),
            pltpu.VMEM((D_MODEL, D_MODEL), F32),
            pltpu.VMEM((N_CHIPS, MLA_WIDTH, BRANCH_COLS), F32),
            pltpu.VMEM((N_CHIPS, POOL_WIDTH, BRANCH_COLS), F32),
        ],
        compiler_params=pltpu.CompilerParams(dimension_semantics=("arbitrary",), vmem_limit_bytes=VMEM_LIMIT),
    )(o, gattn, u, u, gpool, gmerge, x, target, pool_w, pool_scale, w_ba, w_bp, w_out, norm_final)


def _attn_bwd(q, k, v, do, lse, delta, t):
    s = q.shape[0]
    pairs = MLA_HEADS // 2

    def body(q_ref, do_ref, lse_ref, dl_ref, k_ref, v_ref, dq_ref, dk_ref, dv_ref):
        i = pl.program_id(1)

        @pl.when(i == 0)
        def _():
            dk_ref[...] = jnp.zeros_like(dk_ref)
            dv_ref[...] = jnp.zeros_like(dv_ref)

        mask = _chunk_mask(t, False)
        qcs = [slice(hh * HEAD_PAD, (hh + 1) * HEAD_PAD) for hh in range(2)]
        vcs = [slice(hh * V_HEAD_DIM, (hh + 1) * V_HEAD_DIM) for hh in range(2)]
        qhs = [q_ref[:, qc] for qc in qcs]
        dohs = [do_ref[:, vc].astype(BF16) for vc in vcs]
        lses = [lse_ref[:, hh * HEAD_PAD:hh * HEAD_PAD + 1] for hh in range(2)]
        dls = [dl_ref[:, hh * HEAD_PAD:hh * HEAD_PAD + 1] for hh in range(2)]

        def step(j, dqs, masked):
            keys = pl.ds(pl.multiple_of(j * t, t), t)
            out = []
            for hh in range(2):
                kj = k_ref[keys, qcs[hh]]
                vj = v_ref[keys, vcs[hh]]
                p = jnp.exp2(_dot_nt(qhs[hh], kj) * ATT_SCALE_LOG2E - lses[hh])
                if masked:
                    p = jnp.where(mask, p, 0.0)
                ds = (p * (_dot_nt(dohs[hh], vj) - dls[hh])).astype(BF16)
                dv_ref[keys, vcs[hh]] += _dot_tn(p.astype(BF16), dohs[hh])
                dk_ref[keys, qcs[hh]] += _dot_tn(ds, qhs[hh]) * ATT_SCALE
                out.append(dqs[hh] + _dot(ds, kj))
            return tuple(out)

        zero = jnp.zeros((t, HEAD_PAD), F32)
        dqs = lax.fori_loop(0, i, functools.partial(step, masked=False), (zero, zero))
        dqs = step(i, dqs, True)
        for hh in range(2):
            dq_ref[:, qcs[hh]] = dqs[hh] * ATT_SCALE

    hw = MLA_HEADS * HEAD_PAD
    return pl.pallas_call(
        body,
        name="attn_bwd",
        grid=(pairs, s // t),
        in_specs=[
            pl.BlockSpec((t, 2 * HEAD_PAD), lambda p, i: (i, p)),
            pl.BlockSpec((t, 2 * V_HEAD_DIM), lambda p, i: (i, p)),
            pl.BlockSpec((t, 2 * HEAD_PAD), lambda p, i: (i, p)),
            pl.BlockSpec((t, 2 * HEAD_PAD), lambda p, i: (i, p)),
            pl.BlockSpec((s, 2 * HEAD_PAD), lambda p, i: (0, p)),
            pl.BlockSpec((s, 2 * V_HEAD_DIM), lambda p, i: (0, p)),
        ],
        out_specs=[
            pl.BlockSpec((t, 2 * HEAD_PAD), lambda p, i: (i, p)),
            pl.BlockSpec((s, 2 * HEAD_PAD), lambda p, i: (0, p)),
            pl.BlockSpec((s, 2 * V_HEAD_DIM), lambda p, i: (0, p)),
        ],
        out_shape=[jax.ShapeDtypeStruct((s, hw), F32), jax.ShapeDtypeStruct((s, hw), F32),
                   jax.ShapeDtypeStruct((s, MLA_WIDTH), F32)],
        compiler_params=pltpu.CompilerParams(dimension_semantics=("parallel", "arbitrary"),
                                             vmem_limit_bytes=VMEM_LIMIT),
    )(q, do, lse, delta, k, v)


def _qkv_bwd(dq, dk, dv, zq, zkv, q_norm, kv_norm, wuq_p, wk_p, wv, rc, rsa, rsb, tm):
    s = zq.shape[0]
    hw = MLA_HEADS * HEAD_PAD

    def body(dq_ref, dk_ref, dv_ref, zq_ref, zkv_ref, gq_ref, gkv_ref, wuq_ref, wk_ref, wv_ref,
             c_ref, sa_ref, sb_ref,
             dzq_ref, dzkv_ref, dzkr_ref, dwuq_ref, dwk_ref, dwv_ref, dgq_ref, dgkv_ref):
        i = pl.program_id(0)

        @pl.when(i == 0)
        def _():
            dwuq_ref[...] = jnp.zeros_like(dwuq_ref)
            dwk_ref[...] = jnp.zeros_like(dwk_ref)
            dwv_ref[...] = jnp.zeros_like(dwv_ref)
            dgq_ref[...] = jnp.zeros_like(dgq_ref)
            dgkv_ref[...] = jnp.zeros_like(dgkv_ref)

        c, sa, sb = c_ref[...], sa_ref[...], sb_ref[...]
        gq, gkv = gq_ref[...], gkv_ref[...]

        cq, xq, rq = _rms_fwd(zq_ref[...], gq)
        dqp = jnp.concatenate(
            [_unrope(dq_ref[:, h * HEAD_PAD:(h + 1) * HEAD_PAD], c, sa, sb) for h in range(MLA_HEADS)],
            axis=1).astype(BF16)
        dwuq_ref[...] += _dot_tn(cq.astype(BF16), dqp)
        dcq = _dot_nt(dqp, wuq_ref[...])
        dgq_ref[...] += _colsum(dcq * xq)
        dzq_ref[...] = _rms_bwd(dcq, xq, rq, gq).astype(BF16)

        ckv, xkv, rkv = _rms_fwd(zkv_ref[...], gkv)
        ckv = ckv.astype(BF16)
        dkf = dk_ref[...]
        dk_bf = dkf.astype(BF16)
        dv_bf = dv_ref[...].astype(BF16)
        dwk_ref[...] += _dot_tn(ckv, dk_bf)
        dwv_ref[...] += _dot_tn(ckv, dv_bf)
        dckv = _dot_nt(dk_bf, wk_ref[...]) + _dot_nt(dv_bf, wv_ref[...])
        dgkv_ref[...] += _colsum(dckv * xkv)
        dzkv_ref[...] = _rms_bwd(dckv, xkv, rkv, gkv).astype(BF16)

        dkr = dkf[:, 0:HEAD_PAD]
        for h in range(1, MLA_HEADS):
            dkr = dkr + dkf[:, h * HEAD_PAD:(h + 1) * HEAD_PAD]
        dkr = pltpu.roll(_unrope(dkr, c, sa, sb), 64, 1)
        lane = lax.broadcasted_iota(jnp.int32, (tm, HEAD_PAD), 1)
        dzkr_ref[...] = jnp.where(lane < QK_ROPE_DIM, dkr, 0.0).astype(BF16)

    return pl.pallas_call(
        body,
        name="qkv_bwd",
        grid=(s // tm,),
        in_specs=[
            _row_spec(tm, hw), _row_spec(tm, hw), _row_spec(tm, MLA_WIDTH),
            _row_spec(tm, Q_LORA_RANK), _row_spec(tm, KV_LORA_RANK),
            _full_spec((1, Q_LORA_RANK)), _full_spec((1, KV_LORA_RANK)),
            _full_spec((Q_LORA_RANK, hw)), _full_spec((KV_LORA_RANK, hw)), _full_spec((KV_LORA_RANK, MLA_WIDTH)),
            _row_spec(tm, HEAD_PAD), _row_spec(tm, HEAD_PAD), _row_spec(tm, HEAD_PAD),
        ],
        out_specs=[
            _row_spec(tm, Q_LORA_RANK), _row_spec(tm, KV_LORA_RANK), _row_spec(tm, HEAD_PAD),
            _full_spec((Q_LORA_RANK, hw)), _full_spec((KV_LORA_RANK, hw)), _full_spec((KV_LORA_RANK, MLA_WIDTH)),
            _full_spec((1, Q_LORA_RANK)), _full_spec((1, KV_LORA_RANK)),
        ],
        out_shape=[
            jax.ShapeDtypeStruct((s, Q_LORA_RANK), BF16), jax.ShapeDtypeStruct((s, KV_LORA_RANK), BF16),
            jax.ShapeDtypeStruct((s, HEAD_PAD), BF16),
            jax.ShapeDtypeStruct((Q_LORA_RANK, hw), F32), jax.ShapeDtypeStruct((KV_LORA_RANK, hw), F32),
            jax.ShapeDtypeStruct((KV_LORA_RANK, MLA_WIDTH), F32),
            jax.ShapeDtypeStruct((1, Q_LORA_RANK), F32), jax.ShapeDtypeStruct((1, KV_LORA_RANK), F32),
        ],
        compiler_params=pltpu.CompilerParams(dimension_semantics=("arbitrary",), vmem_limit_bytes=VMEM_LIMIT),
    )(dq, dk, dv, zq, zkv, q_norm, kv_norm, wuq_p, wk_p, wv, rc, rsa, rsb)


def _inproj_bwd(dzq, dzkv, dzkr, dgattn, ddc, dgpool, dgmerge, hn, x, dh, norm_in, w_in_t, tm):
    s = x.shape[0]
    n_tiles = s // tm
    halo_per_tile = tm // POOL_HALO
    n_halo = s // POOL_HALO
    u_seg = 4

    def body(dzq_ref, dzkv_ref, dzkr_ref, dga_ref, ddc_ref, ddn_ref, dgp_ref, dgm_ref, hn_ref, x_ref, dh_ref,
             g_ref, w_hbm, gx_ref, dgin_ref, dw_hbm, w_vmem, dw_acc, dbuf, sem):
        i = pl.program_id(0)

        @pl.when(i == 0)
        def _():
            cp = pltpu.make_async_copy(w_hbm, w_vmem, sem)
            cp.start()
            dw_acc[...] = jnp.zeros_like(dw_acc)
            dgin_ref[...] = jnp.zeros_like(dgin_ref)
            cp.wait()

        dbuf[0:tm, :] = ddc_ref[...]
        dbuf[tm:, :] = jnp.where(i < n_tiles - 1, ddn_ref[...], 0.0)
        row = lax.broadcasted_iota(jnp.int32, (tm, POOL_GROUP_DIM), 0) + i * tm
        du = []
        for g, w in enumerate(POOL_WINDOWS):
            cols = slice(g * POOL_GROUP_DIM, (g + 1) * POOL_GROUP_DIM)
            fsum = dbuf[0:tm, cols]
            for kk in range(1, w):
                fsum = fsum + dbuf[kk:kk + tm, cols]
            du.append(fsum - dbuf[0:tm, cols] * jnp.minimum(row + 1, w).astype(F32))
        du = jnp.concatenate(du, axis=1).astype(BF16)

        dz = [dzq_ref[...], dzkv_ref[...], dzkr_ref[...], dga_ref[...], du, dgp_ref[...], dgm_ref[...]]
        hn = hn_ref[...]
        dhn = None
        for j, pieces in enumerate(SHARD_PIECES):
            parts = [dz[seg][:, lo:hi] for seg, lo, hi, _ in pieces]
            dzj = parts[0] if len(parts) == 1 else jnp.concatenate(parts, axis=1)
            part = _dot(dzj, w_vmem[j])
            dhn = part if dhn is None else dhn + part
            dw_acc[j] += _dot_tn(dzj, hn)

        g = g_ref[...]
        _, xhat, r = _rms_fwd(x_ref[...], g)
        dgin_ref[...] += _colsum(dhn * xhat)
        gx_ref[...] = dh_ref[...] + _rms_bwd(dhn, xhat, r, g)

        @pl.when(i == n_tiles - 1)
        def _():
            for j in range(N_CHIPS):
                w_vmem[j] = dw_acc[j].astype(BF16)
            cp = pltpu.make_async_copy(w_vmem, dw_hbm, sem)
            cp.start()
            cp.wait()

    any_spec = pl.BlockSpec(memory_space=pl.ANY)
    seg_w = [wide for _, wide in IN_SEGMENTS]
    return pl.pallas_call(
        body,
        name="inproj_bwd",
        grid=(n_tiles,),
        in_specs=[
            _row_spec(tm, seg_w[0]), _row_spec(tm, seg_w[1]), _row_spec(tm, seg_w[2]),
            _row_spec(tm, seg_w[3]), _row_spec(tm, seg_w[u_seg]),
            pl.BlockSpec((POOL_HALO, POOL_WIDTH), lambda i: (jnp.minimum((i + 1) * halo_per_tile, n_halo - 1), 0)),
            _row_spec(tm, seg_w[5]), _row_spec(tm, seg_w[6]),
            _row_spec(tm, D_MODEL), _row_spec(tm, D_MODEL), _row_spec(tm, D_MODEL),
            _full_spec((1, D_MODEL)), any_spec,
        ],
        out_specs=[_row_spec(tm, D_MODEL), _full_spec((1, D_MODEL)), any_spec],
        out_shape=[jax.ShapeDtypeStruct((s, D_MODEL), F32), jax.ShapeDtypeStruct((1, D_MODEL), F32),
                   jax.ShapeDtypeStruct((N_CHIPS, SHARD_COLS, D_MODEL), BF16)],
        scratch_shapes=[
            pltpu.VMEM((N_CHIPS, SHARD_COLS, D_MODEL), BF16),
            pltpu.VMEM((N_CHIPS, SHARD_COLS, D_MODEL), F32),
            pltpu.VMEM((tm + POOL_HALO, POOL_WIDTH), F32),
            pltpu.SemaphoreType.DMA,
        ],
        compiler_params=pltpu.CompilerParams(dimension_semantics=("arbitrary",), vmem_limit_bytes=VMEM_LIMIT),
    )(dzq, dzkv, dzkr, dgattn, ddc, ddc, dgpool, dgmerge, hn, x, dh, norm_in, w_in_t)


def _other_chips(x, y):
    return ((1 - x, y), (x, 1 - y), (1 - x, 1 - y))


def _half(ref, axis, size, c, lead=()):
    window = pl.ds(pl.multiple_of(c * size, size), size)
    if axis == 0:
        return ref.at[(*lead, window, slice(None))]
    return ref.at[(*lead, slice(None), window)]


def _half_shape(rows, cols, axis, size):
    return (size, cols) if axis == 0 else (rows, size)


def _weight_gather(shards):
    n = len(COMM_PARAMS)

    def body(*refs):
        ins, outs = refs[:n], refs[n:2 * n]
        send_sems, recv_sems, local_sems = refs[2 * n:]
        x, y, c = lax.axis_index("x"), lax.axis_index("y"), lax.axis_index("c")
        k = 2 * x + y
        chips = _other_chips(x, y)

        def copy(sem, src, dst, to):
            return pltpu.make_async_remote_copy(src_ref=src, dst_ref=dst, send_sem=send_sems.at[sem],
                                                recv_sem=recv_sems.at[sem], device_id=to, device_id_type=MESH)

        local, first, passed = [], [], []
        for p in range(n):
            cp = pltpu.make_async_copy(ins[p], outs[p].at[k], local_sems.at[p])
            cp.start()
            local.append(cp)
        for p, (_, _, _, axis, size) in enumerate(COMM_PARAMS):
            for j, (cx, cy) in enumerate(chips):
                cp = copy(6 * p + j, _half(ins[p], axis, size, c), _half(outs[p], axis, size, c, (k,)), (cx, cy, c))
                cp.start()
                first.append(cp)
        for j, (cx, cy) in enumerate(chips):
            kj = 2 * cx + cy
            for p, (_, _, _, axis, size) in enumerate(COMM_PARAMS):
                landed = _half(outs[p], axis, size, c, (kj,))
                copy(6 * p + j, landed, landed, (x, y, c)).wait_recv()
                cp = copy(6 * p + 3 + j, landed, landed, (x, y, 1 - c))
                cp.start()
                passed.append(cp)
        for j, (cx, cy) in enumerate(chips):
            kj = 2 * cx + cy
            for p, (_, _, _, axis, size) in enumerate(COMM_PARAMS):
                theirs = _half(outs[p], axis, size, 1 - c, (kj,))
                copy(6 * p + 3 + j, theirs, theirs, (x, y, c)).wait_recv()
        for cp in first + passed:
            cp.wait_send()
        for cp in local:
            cp.wait()

    any_spec = pl.BlockSpec(memory_space=pl.ANY)
    return pl.pallas_call(
        body,
        name="weight_gather",
        in_specs=[any_spec] * n,
        out_specs=[any_spec] * n,
        out_shape=[jax.ShapeDtypeStruct((N_CHIPS, r, cc), BF16) for _, r, cc, _, _ in COMM_PARAMS],
        scratch_shapes=[pltpu.SemaphoreType.DMA((6 * n,)), pltpu.SemaphoreType.DMA((6 * n,)),
                        pltpu.SemaphoreType.DMA((n,))],
    )(*shards)


def _grad_reduce(grads, gs):
    n = len(COMM_PARAMS)
    n_small = N_DEV - 1

    def body(*refs):
        g_in, gs_ref = refs[:n], refs[n]
        g_out, gsum_ref = refs[n + 1:2 * n + 1], refs[2 * n + 1]
        scratch = refs[2 * n + 2:]
        pm, a_buf, b_buf, r_buf = scratch[0:n], scratch[n:2 * n], scratch[2 * n:3 * n], scratch[3 * n:4 * n]
        s_buf, send_sems, recv_sems, local_sems = scratch[4 * n:]
        x, y, c = lax.axis_index("x"), lax.axis_index("y"), lax.axis_index("c")
        k = 2 * x + y
        me = 4 * x + 2 * y + c
        chips = _other_chips(x, y)

        def copy(sem, src, dst, to):
            return pltpu.make_async_remote_copy(src_ref=src, dst_ref=dst, send_sem=send_sems.at[sem],
                                                recv_sem=recv_sems.at[sem], device_id=to, device_id_type=MESH)

        loads, sends = [], []
        for p, (_, _, _, axis, size) in enumerate(COMM_PARAMS):
            ld = pltpu.make_async_copy(_half(g_in[p], axis, size, c, (slice(None),)), pm[p], local_sems.at[p])
            ld.start()
            loads.append(ld)
            cp = copy(5 * p, _half(g_in[p], axis, size, 1 - c, (slice(None),)), a_buf[p], (x, y, 1 - c))
            cp.start()
            sends.append(cp)
        flips = [(fx, fy, fc) for fx in (0, 1) for fy in (0, 1) for fc in (0, 1)][1:]
        for f, (fx, fy, fc) in enumerate(flips, start=1):
            peer = (1 - x if fx else x, 1 - y if fy else y, 1 - c if fc else c)
            cp = copy(5 * n + f - 1, gs_ref, s_buf.at[f], peer)
            cp.start()
            sends.append(cp)
        s_buf[0] = gs_ref[...]

        for p in range(n):
            loads[p].wait()
            copy(5 * p, a_buf[p], a_buf[p], (x, y, c)).wait_recv()
            for j, (cx, cy) in enumerate(chips):
                kj = 2 * cx + cy
                pm[p][kj] = (pm[p][kj].astype(F32) + a_buf[p][kj].astype(F32)).astype(BF16)
                cp = copy(5 * p + 1 + j, pm[p].at[kj], b_buf[p].at[j], (cx, cy, c))
                cp.start()
                sends.append(cp)
            r_buf[p][...] = pm[p][k].astype(F32) + a_buf[p][k].astype(F32)

        stores = []
        for p, (_, _, _, axis, size) in enumerate(COMM_PARAMS):
            for j in range(3):
                copy(5 * p + 1 + j, b_buf[p].at[j], b_buf[p].at[j], (x, y, c)).wait_recv()
                r_buf[p][...] = r_buf[p][...] + b_buf[p][j].astype(F32)
            st = pltpu.make_async_copy(r_buf[p], _half(g_out[p], axis, size, c), local_sems.at[n + p])
            st.start()
            stores.append(st)
            cp = copy(5 * p + 4, r_buf[p], _half(g_out[p], axis, size, c), (x, y, 1 - c))
            cp.start()
            sends.append(cp)

        for f in range(1, N_DEV):
            copy(5 * n + f - 1, s_buf.at[f], s_buf.at[f], (x, y, c)).wait_recv()
        total = s_buf[me]
        for d in range(1, N_DEV):
            total = total + s_buf[jnp.bitwise_xor(me, d)]
        gsum_ref[...] = total

        for p, (_, _, _, axis, size) in enumerate(COMM_PARAMS):
            theirs = _half(g_out[p], axis, size, 1 - c)
            copy(5 * p + 4, theirs, theirs, (x, y, c)).wait_recv()
            stores[p].wait()
        for cp in sends:
            cp.wait_send()

    any_spec = pl.BlockSpec(memory_space=pl.ANY)
    vmem_spec = pl.BlockSpec(memory_space=pltpu.VMEM)
    halves = [_half_shape(r, cc, axis, size) for _, r, cc, axis, size in COMM_PARAMS]
    scratch = ([pltpu.VMEM((N_CHIPS, *h), BF16) for h in halves]
               + [pltpu.VMEM((N_CHIPS, *h), BF16) for h in halves]
               + [pltpu.VMEM((3, *h), BF16) for h in halves]
               + [pltpu.VMEM(h, F32) for h in halves]
               + [pltpu.VMEM((N_DEV, SMALL_ROWS, LANES), F32),
                  pltpu.SemaphoreType.DMA((5 * n + n_small,)), pltpu.SemaphoreType.DMA((5 * n + n_small,)),
                  pltpu.SemaphoreType.DMA((2 * n,))])
    out = pl.pallas_call(
        body,
        name="grad_reduce",
        in_specs=[any_spec] * n + [vmem_spec],
        out_specs=[any_spec] * n + [vmem_spec],
        out_shape=[jax.ShapeDtypeStruct((r, cc), F32) for _, r, cc, _, _ in COMM_PARAMS]
        + [jax.ShapeDtypeStruct((SMALL_ROWS, LANES), F32)],
        scratch_shapes=scratch,
        compiler_params=pltpu.CompilerParams(vmem_limit_bytes=VMEM_LIMIT),
    )(*grads, gs)
    return out[:n], out[n]


def _adamw_math(w, g, m, v):
    m = ADAM_B1 * m + (1.0 - ADAM_B1) * g
    v = ADAM_B2 * v + (1.0 - ADAM_B2) * (g * g)
    m_hat = m / (1.0 - ADAM_B1 ** ADAM_STEP)
    v_hat = v / (1.0 - ADAM_B2 ** ADAM_STEP)
    delta = -ADAM_LR * (m_hat / (jnp.sqrt(v_hat) + ADAM_EPS) + ADAM_WD * w)
    return delta, m, v


def _adamw_tiled(w, g, m, v, tm):
    rows, cols = w.shape

    def body(w_ref, g_ref, m_ref, v_ref, d_ref, nm_ref, nv_ref):
        d_ref[...], nm_ref[...], nv_ref[...] = _adamw_math(w_ref[...], g_ref[...], m_ref[...], v_ref[...])

    spec = _row_spec(tm, cols)
    return pl.pallas_call(
        body,
        name="adamw_w_in",
        grid=(rows // tm,),
        in_specs=[spec] * 4,
        out_specs=[spec] * 3,
        out_shape=[jax.ShapeDtypeStruct(w.shape, F32)] * 3,
        compiler_params=pltpu.CompilerParams(dimension_semantics=("parallel",), vmem_limit_bytes=VMEM_LIMIT),
    )(w, g, m, v)


def _adamw_many(ws, gs, ms, vs):
    n = len(ws)

    def body(*refs):
        ins, outs = refs[:4 * n], refs[4 * n:]
        for i in range(n):
            d, nm, nv = _adamw_math(ins[i][...], ins[n + i][...], ins[2 * n + i][...], ins[3 * n + i][...])
            outs[i][...] = d
            outs[n + i][...] = nm
            outs[2 * n + i][...] = nv

    vmem_spec = pl.BlockSpec(memory_space=pltpu.VMEM)
    shapes = [jax.ShapeDtypeStruct(w.shape, F32) for w in ws]
    out = pl.pallas_call(
        body,
        name="adamw_small",
        in_specs=[vmem_spec] * (4 * n),
        out_specs=[vmem_spec] * (3 * n),
        out_shape=shapes * 3,
        compiler_params=pltpu.CompilerParams(vmem_limit_bytes=VMEM_LIMIT),
    )(*ws, *gs, *ms, *vs)
    return out[:n], out[n:2 * n], out[2 * n:]


def _pack_rows(parts, rows, dtype):
    flat = jnp.concatenate([p.reshape(-1).astype(dtype) for p in parts])
    flat = jnp.concatenate([flat, jnp.zeros((rows * LANES - flat.shape[0],), dtype)])
    return flat.reshape(rows, LANES)


def _unpack_rows(packed, shapes):
    flat = packed.reshape(-1)
    out, off = [], 0
    for _, shp in shapes:
        n = int(np.prod(shp))
        out.append(flat[off:off + n].reshape(shp))
        off += n
    return out


def _rope_tables(s):
    half = QK_ROPE_DIM // 2
    inv_freq = ROPE_THETA ** (-jnp.arange(half, dtype=F32) / half)
    ang = jnp.arange(s, dtype=F32)[:, None] * inv_freq[None, :]
    cos, sin = jnp.cos(ang), jnp.sin(ang)
    z16 = jnp.zeros((s, half), F32)
    z32 = jnp.zeros((s, HEAD_PAD - QK_NOPE_DIM - QK_ROPE_DIM), F32)
    z64 = jnp.zeros((s, QK_NOPE_DIM), F32)
    rc = jnp.concatenate([jnp.ones((s, QK_NOPE_DIM), F32), cos, cos, z32], axis=1)
    rsa = jnp.concatenate([z64, -sin, z16, z32], axis=1)
    rsb = jnp.concatenate([z64, z16, sin, z32], axis=1)
    return rc, rsa, rsb


def kernel(x, norm_in, w_in, q_norm, w_uq, kv_norm, w_ukv, pool_w, pool_scale, w_branch_attn, w_branch_pool, w_out, norm_final, loss_target, m_norm_in, m_w_in, m_q_norm, m_w_uq, m_kv_norm, m_w_ukv, m_pool_w, m_pool_scale, m_w_branch_attn, m_w_branch_pool, m_w_out, m_norm_final, v_norm_in, v_w_in, v_q_norm, v_w_uq, v_kv_norm, v_w_ukv, v_pool_w, v_pool_scale, v_w_branch_attn, v_w_branch_pool, v_w_out, v_norm_final):
    s = x.shape[1]
    t_att, t_row = _tiles(s)
    x2 = x.reshape(s, D_MODEL)
    tgt = loss_target.reshape(s, D_MODEL)

    local = [w_in.T, w_uq.reshape(96, 768), w_ukv.reshape(64, 1024), w_branch_attn, w_branch_pool, w_out]
    w_in_t, w_uq_all, w_ukv_all, w_ba_all, w_bp_all, w_out_all = _weight_gather([a.astype(BF16) for a in local])
    w_uq_f = w_uq_all.reshape(Q_LORA_RANK, MLA_HEADS, QK_NOPE_DIM + QK_ROPE_DIM)
    w_ukv_f = w_ukv_all.reshape(KV_LORA_RANK, MLA_HEADS, QK_NOPE_DIM + V_HEAD_DIM)
    w_out_f = w_out_all.reshape(D_MODEL, D_MODEL)
    hw = MLA_HEADS * HEAD_PAD
    wuq_p = jnp.pad(w_uq_f, ((0, 0), (0, 0), (0, HEAD_PAD - QK_NOPE_DIM - QK_ROPE_DIM))).reshape(Q_LORA_RANK, hw)
    wk_p = jnp.pad(w_ukv_f[:, :, :QK_NOPE_DIM], ((0, 0), (0, 0), (0, HEAD_PAD - QK_NOPE_DIM))).reshape(KV_LORA_RANK, hw)
    wv = w_ukv_f[:, :, QK_NOPE_DIM:].reshape(KV_LORA_RANK, MLA_WIDTH)
    rc, rsa, rsb = _rope_tables(s)
    g_in = norm_in.reshape(1, -1)
    g_q = q_norm.reshape(1, -1)
    g_kv = kv_norm.reshape(1, -1)
    g_f = norm_final.reshape(1, -1)
    ps = pool_scale.reshape(1, -1)
    pw_bf = pool_w.astype(BF16)

    hn, zq, zkv, zkr, gattn, u, gpool, gmerge = _inproj_fwd(x2, g_in, w_in_t, t_row)
    q, k, v, q_t, v_t = _qkv_fwd(zq, zkv, zkr, g_q, g_kv, wuq_p, wk_p, wv, rc, rsa, rsb, t_row)
    o, lse = _attn_fwd(q_t, k, v_t, t_att)

    (do, delta, dgattn, dgpool, dgmerge, ddc, dh, sq_err, d_w_out, d_w_ba, d_w_bp, d_pool_w, d_pool_scale,
     d_norm_final) = _mid(o, gattn, u, gpool, gmerge, x2, tgt, pw_bf, ps, w_ba_all, w_bp_all, w_out_f, g_f, t_row)

    dq, dk, dv = _attn_bwd(q, k, v, do, lse, delta, t_att)
    dzq, dzkv, dzkr, d_wuq_p, d_wk_p, d_wv, d_q_norm, d_kv_norm = _qkv_bwd(
        dq, dk, dv, zq, zkv, g_q, g_kv, wuq_p, wk_p, wv, rc, rsa, rsb, t_row)
    grad_x, d_norm_in, d_w_in_t = _inproj_bwd(dzq, dzkv, dzkr, dgattn, ddc, dgpool, dgmerge, hn, x2, dh, g_in,
                                              w_in_t, t_row)

    d_w_uq = d_wuq_p.reshape(Q_LORA_RANK, MLA_HEADS, HEAD_PAD)[:, :, :QK_NOPE_DIM + QK_ROPE_DIM]
    d_w_ukv = jnp.concatenate([d_wk_p.reshape(KV_LORA_RANK, MLA_HEADS, HEAD_PAD)[:, :, :QK_NOPE_DIM],
                               d_wv.reshape(KV_LORA_RANK, MLA_HEADS, V_HEAD_DIM)], axis=2)
    grads_local = [d_w_in_t, d_w_uq.reshape(N_CHIPS, 96, 768).astype(BF16),
                   d_w_ukv.reshape(N_CHIPS, 64, 1024).astype(BF16), d_w_ba, d_w_bp,
                   d_w_out.reshape(N_CHIPS, 256, D_MODEL)]
    small = dict(norm_in=d_norm_in, q_norm=d_q_norm, kv_norm=d_kv_norm, pool_scale=d_pool_scale,
                 norm_final=d_norm_final, pool_w=d_pool_w)
    gs = _pack_rows([small[n] for n, _ in SMALL_SHAPES], SMALL_ROWS, F32)
    (g_w_in_t, g_w_uq, g_w_ukv, g_w_ba, g_w_bp, g_w_out), g_small = _grad_reduce(grads_local, gs)
    g_norm_in, g_q_norm, g_kv_norm, g_pool_scale, g_norm_final, g_pool_w = _unpack_rows(g_small, SMALL_SHAPES)

    dl_w_in, nm_w_in, nv_w_in = (a.T for a in _adamw_tiled(w_in.T, g_w_in_t, m_w_in.T, v_w_in.T, 152))

    def two_d(a):
        return a.reshape(1, -1) if a.ndim == 1 else a.reshape(a.shape[0], -1)

    names = ["norm_in", "q_norm", "w_uq", "kv_norm", "w_ukv", "pool_w", "pool_scale", "w_branch_attn",
             "w_branch_pool", "w_out", "norm_final"]
    ws = dict(norm_in=norm_in, q_norm=q_norm, w_uq=w_uq, kv_norm=kv_norm, w_ukv=w_ukv, pool_w=pool_w,
              pool_scale=pool_scale, w_branch_attn=w_branch_attn, w_branch_pool=w_branch_pool, w_out=w_out,
              norm_final=norm_final)
    gsd = dict(norm_in=g_norm_in, q_norm=g_q_norm, w_uq=g_w_uq, kv_norm=g_kv_norm, w_ukv=g_w_ukv, pool_w=g_pool_w,
               pool_scale=g_pool_scale, w_branch_attn=g_w_ba, w_branch_pool=g_w_bp, w_out=g_w_out,
               norm_final=g_norm_final)
    msd = dict(norm_in=m_norm_in, q_norm=m_q_norm, w_uq=m_w_uq, kv_norm=m_kv_norm, w_ukv=m_w_ukv, pool_w=m_pool_w,
               pool_scale=m_pool_scale, w_branch_attn=m_w_branch_attn, w_branch_pool=m_w_branch_pool, w_out=m_w_out,
               norm_final=m_norm_final)
    vsd = dict(norm_in=v_norm_in, q_norm=v_q_norm, w_uq=v_w_uq, kv_norm=v_kv_norm, w_ukv=v_w_ukv, pool_w=v_pool_w,
               pool_scale=v_pool_scale, w_branch_attn=v_w_branch_attn, w_branch_pool=v_w_branch_pool, w_out=v_w_out,
               norm_final=v_norm_final)
    dls, nms, nvs = _adamw_many([two_d(ws[n]) for n in names], [two_d(gsd[n]) for n in names],
                                [two_d(msd[n]) for n in names], [two_d(vsd[n]) for n in names])

    grads = dict(gsd)
    grads["w_in"] = g_w_in_t.T
    delta_w = {n: d.reshape(ws[n].shape) for n, d in zip(names, dls)}
    new_m = {n: d.reshape(ws[n].shape) for n, d in zip(names, nms)}
    new_v = {n: d.reshape(ws[n].shape) for n, d in zip(names, nvs)}
    delta_w["w_in"], new_m["w_in"], new_v["w_in"] = dl_w_in, nm_w_in, nv_w_in
    ws["w_in"] = w_in

    order = ["norm_in", "w_in", "q_norm", "w_uq", "kv_norm", "w_ukv", "pool_w", "pool_scale", "w_branch_attn",
             "w_branch_pool", "w_out", "norm_final"]
    loss = lax.psum(0.5 * jnp.sum(sq_err) / D_MODEL, ("x", "y", "c"))
    return (loss, grad_x.reshape(x.shape),
            *[grads[n].reshape(ws[n].shape) for n in order],
            *[delta_w[n] for n in order], *[new_m[n] for n in order], *[new_v[n] for n in order])
```

```python
import functools

import jax
import jax.numpy as jnp
import numpy as np
from jax import lax
from jax.experimental import pallas as pl
from jax.experimental.pallas import tpu as pltpu

F32 = jnp.float32
BF16 = jnp.bfloat16
MESH = pl.DeviceIdType.MESH

D_MODEL = 1024
CHUNK = 64
MLA_HEADS = 8
QK_NOPE_DIM = 64
QK_ROPE_DIM = 32
V_HEAD_DIM = 64
Q_LORA_RANK = 384
KV_LORA_RANK = 256
MLA_WIDTH = MLA_HEADS * V_HEAD_DIM
ROPE_THETA = 10000.0
POOL_WINDOWS = (2, 4, 8, 16)
POOL_WIDTH = 512
POOL_GROUP_DIM = 128
BRANCH_COLS = D_MODEL // 4
POOL_HALO = 16
EPS = 1e-6
IN_TOTAL = 4256
HEAD_PAD = 128
ATT_SCALE = (QK_NOPE_DIM + QK_ROPE_DIM) ** -0.5
ATT_SCALE_LOG2E = ATT_SCALE * 1.4426950408889634

ADAM_LR = 0.001
ADAM_B1 = 0.9
ADAM_B2 = 0.999
ADAM_EPS = 1e-08
ADAM_WD = 0.01
ADAM_STEP = 10

N_CHIPS = 4
N_DEV = 8
LANES = 128
VMEM_LIMIT = 60 * 1024 * 1024

IN_SEGMENTS = ((384, 384), (256, 256), (32, HEAD_PAD), (512, 512), (512, 512), (512, 512), (2048, 2048))
SHARD_COLS = IN_TOTAL // N_CHIPS


def _shard_pieces():
    bounds, off = [], 0
    for w, _ in IN_SEGMENTS:
        bounds.append((off, off + w))
        off += w
    out = []
    for j in range(N_CHIPS):
        lo, hi = SHARD_COLS * j, SHARD_COLS * (j + 1)
        out.append([(i, max(lo, a) - a, min(hi, b) - a, max(lo, a) - lo)
                    for i, (a, b) in enumerate(bounds) if max(lo, a) < min(hi, b)])
    return out


SHARD_PIECES = _shard_pieces()

COMM_PARAMS = (
    ("w_in", SHARD_COLS, D_MODEL, 1, 512),
    ("w_uq", 96, 768, 0, 48),
    ("w_ukv", 64, 1024, 0, 32),
    ("w_branch_attn", 512, 256, 0, 256),
    ("w_branch_pool", 512, 256, 0, 256),
    ("w_out", 256, 1024, 0, 128),
)

SMALL_SHAPES = (
    ("norm_in", (1024,)),
    ("q_norm", (384,)),
    ("kv_norm", (256,)),
    ("pool_scale", (512,)),
    ("norm_final", (1024,)),
    ("pool_w", (4, 128, 128)),
)
SMALL_ELEMS = sum(int(np.prod(s)) for _, s in SMALL_SHAPES)
SMALL_ROWS = -(-SMALL_ELEMS // (LANES * 8)) * 8


def _dot(a, b):
    return jnp.dot(a, b, preferred_element_type=F32)


def _dot_nt(a, b):
    return lax.dot_general(a, b, (((1,), (1,)), ((), ())), preferred_element_type=F32)


def _dot_tn(a, b):
    return lax.dot_general(a, b, (((0,), (0,)), ((), ())), preferred_element_type=F32)


def _sigmoid(x):
    return 1.0 / (1.0 + jnp.exp(-x))


def _colsum(x):
    return jnp.sum(x, axis=0, keepdims=True)


def _rms_fwd(x, g):
    r = lax.rsqrt(jnp.mean(x * x, axis=-1, keepdims=True) + EPS)
    xhat = x * r
    return xhat * g, xhat, r


def _rms_bwd(dy, xhat, r, g):
    dxhat = dy * g
    return r * (dxhat - xhat * jnp.mean(dxhat * xhat, axis=-1, keepdims=True))


def _rope(v, c, sa, sb):
    return v * c + pltpu.roll(v, 112, 1) * sa + pltpu.roll(v, 16, 1) * sb


def _unrope(d, c, sa, sb):
    return d * c + pltpu.roll(d * sa, 16, 1) + pltpu.roll(d * sb, 112, 1)


def _row_spec(tm, n):
    return pl.BlockSpec((tm, n), lambda i: (i, 0))


def _full_spec(shape):
    nd = len(shape)
    return pl.BlockSpec(shape, lambda i: (0,) * nd)


def _tiles(s):
    t_att = 512 if s >= 2048 else 128
    t_row = 256 if s >= 1024 else 128
    return t_att, t_row


def _inproj_fwd(x, norm_in, w_in_t, tm):
    s = x.shape[0]

    def body(x_ref, g_ref, w_ref, hn_ref, *z_refs):
        hn, _, _ = _rms_fwd(x_ref[...], g_ref[...])
        hn = hn.astype(BF16)
        hn_ref[...] = hn
        for j, pieces in enumerate(SHARD_PIECES):
            zj = _dot_nt(hn, w_ref[j])
            for seg, lo, hi, col in pieces:
                z_refs[seg][:, lo:hi] = zj[:, col:col + hi - lo]
        for seg, (w, wide) in enumerate(IN_SEGMENTS):
            if wide > w:
                z_refs[seg][:, w:wide] = jnp.zeros((tm, wide - w), F32)

    out_shape = [jax.ShapeDtypeStruct((s, D_MODEL), BF16)]
    out_specs = [_row_spec(tm, D_MODEL)]
    for _, wide in IN_SEGMENTS:
        out_shape.append(jax.ShapeDtypeStruct((s, wide), F32))
        out_specs.append(_row_spec(tm, wide))
    return pl.pallas_call(
        body,
        name="inproj_fwd",
        grid=(s // tm,),
        in_specs=[_row_spec(tm, D_MODEL), _full_spec((1, D_MODEL)),
                  pl.BlockSpec((N_CHIPS, SHARD_COLS, D_MODEL), lambda i: (0, 0, 0), pipeline_mode=pl.Buffered(1))],
        out_specs=out_specs,
        out_shape=out_shape,
        compiler_params=pltpu.CompilerParams(dimension_semantics=("parallel",), vmem_limit_bytes=VMEM_LIMIT),
    )(x, norm_in, w_in_t)


def _qkv_fwd(zq, zkv, zkr, q_norm, kv_norm, wuq_p, wk_p, wv, rc, rsa, rsb, tm):
    s = zq.shape[0]
    hw = MLA_HEADS * HEAD_PAD

    def body(zq_ref, zkv_ref, zkr_ref, gq_ref, gkv_ref, wuq_ref, wk_ref, wv_ref, c_ref, sa_ref, sb_ref,
             q_ref, k_ref, v_ref, qt_ref, vt_ref):
        c, sa, sb = c_ref[...], sa_ref[...], sb_ref[...]
        cq, _, _ = _rms_fwd(zq_ref[...], gq_ref[...])
        qf = _dot(cq.astype(BF16), wuq_ref[...])
        ckv, _, _ = _rms_fwd(zkv_ref[...], gkv_ref[...])
        ckv = ckv.astype(BF16)
        kn = _dot(ckv, wk_ref[...])
        kr = _rope(pltpu.roll(zkr_ref[...], 64, 1), c, sa, sb)
        for h in range(MLA_HEADS):
            cols = slice(h * HEAD_PAD, (h + 1) * HEAD_PAD)
            qh = _rope(qf[:, cols], c, sa, sb)
            q_ref[:, cols] = qh.astype(BF16)
            qt_ref[cols, :] = qh.T.astype(BF16)
            k_ref[:, cols] = (kn[:, cols] + kr).astype(BF16)
        vf = _dot(ckv, wv_ref[...])
        v_ref[...] = vf.astype(BF16)
        vt_ref[...] = vf.T.astype(BF16)

    return pl.pallas_call(
        body,
        name="qkv_fwd",
        grid=(s // tm,),
        in_specs=[
            _row_spec(tm, Q_LORA_RANK), _row_spec(tm, KV_LORA_RANK), _row_spec(tm, HEAD_PAD),
            _full_spec((1, Q_LORA_RANK)), _full_spec((1, KV_LORA_RANK)),
            _full_spec((Q_LORA_RANK, hw)), _full_spec((KV_LORA_RANK, hw)), _full_spec((KV_LORA_RANK, MLA_WIDTH)),
            _row_spec(tm, HEAD_PAD), _row_spec(tm, HEAD_PAD), _row_spec(tm, HEAD_PAD),
        ],
        out_specs=[_row_spec(tm, hw), _row_spec(tm, hw), _row_spec(tm, MLA_WIDTH),
                   pl.BlockSpec((hw, tm), lambda i: (0, i)), pl.BlockSpec((MLA_WIDTH, tm), lambda i: (0, i))],
        out_shape=[jax.ShapeDtypeStruct((s, hw), BF16), jax.ShapeDtypeStruct((s, hw), BF16),
                   jax.ShapeDtypeStruct((s, MLA_WIDTH), BF16),
                   jax.ShapeDtypeStruct((hw, s), BF16), jax.ShapeDtypeStruct((MLA_WIDTH, s), BF16)],
        compiler_params=pltpu.CompilerParams(dimension_semantics=("parallel",), vmem_limit_bytes=VMEM_LIMIT),
    )(zq, zkv, zkr, q_norm, kv_norm, wuq_p, wk_p, wv, rc, rsa, rsb)


def _chunk_mask(t, keys_on_rows):
    rows = lax.broadcasted_iota(jnp.int32, (t, t), 0) // CHUNK
    cols = lax.broadcasted_iota(jnp.int32, (t, t), 1) // CHUNK
    return rows <= cols if keys_on_rows else cols <= rows


def _attn_fwd(q_t, k, v_t, late_shards, t):
    s = k.shape[0]
    pairs = MLA_HEADS // 2
    n_q = s // t
    gat = _Gather(COMM_PARAMS[3:])
    n_w = len(gat.params)

    def body(qt_ref, k_ref, k2_ref, vt_ref, *rest):
        w_in, (o_ref, lse_ref), w_out = rest[:n_w], rest[n_w:n_w + 2], rest[n_w + 2:2 * n_w + 2]
        gat.bind(w_in, w_out, rest[2 * n_w + 2:])
        i = pl.program_id(1)
        step_no = pl.program_id(0) * n_q + i
        pl.when(step_no == 0)(gat.start)
        pl.when(step_no == n_q)(gat.relay)
        mask = _chunk_mask(t, True)
        qcs = [slice(hh * HEAD_PAD, (hh + 1) * HEAD_PAD) for hh in range(2)]
        vcs = [slice(hh * V_HEAD_DIM, (hh + 1) * V_HEAD_DIM) for hh in range(2)]
        qts = [qt_ref[qc, :] for qc in qcs]

        def step(j, carry, masked):
            keys = pl.ds(pl.multiple_of(j * t, t), t)
            out = []
            for hh in range(2):
                m, l, acc = carry[hh]
                sc = _dot(k_ref[keys, qcs[hh]], qts[hh])
                if masked:
                    sc = jnp.where(mask, sc, -jnp.inf)
                m_new = jnp.maximum(m, jnp.max(sc, axis=0, keepdims=True))
                alpha = jnp.exp2((m - m_new) * ATT_SCALE_LOG2E)
                p = jnp.exp2((_dot(k2_ref[keys, qcs[hh]], qts[hh]) - m_new) * ATT_SCALE_LOG2E)
                if masked:
                    p = jnp.where(mask, p, 0.0)
                l = alpha * l + jnp.sum(p, axis=0, keepdims=True)
                acc = alpha * acc + _dot(vt_ref[vcs[hh], keys], p.astype(BF16))
                out.append((m_new, l, acc))
            return tuple(out)

        one = (jnp.full((1, t), -jnp.inf, F32), jnp.zeros((1, t), F32), jnp.zeros((V_HEAD_DIM, t), F32))
        carry = lax.fori_loop(0, i, functools.partial(step, masked=False), (one, one))
        carry = step(i, carry, True)
        o_ref[...] = jnp.concatenate([carry[hh][2] / carry[hh][1] for hh in range(2)], axis=0).T
        for hh in range(2):
            m, l, _ = carry[hh]
            lse_ref[:, qcs[hh]] = jnp.broadcast_to(m * ATT_SCALE_LOG2E + jnp.log2(l), (HEAD_PAD, t)).T
        pl.when(step_no == pairs * n_q - 1)(gat.finish)

    any_spec = pl.BlockSpec(memory_space=pl.ANY)
    out = pl.pallas_call(
        body,
        name="attn_fwd",
        grid=(pairs, n_q),
        in_specs=[
            pl.BlockSpec((2 * HEAD_PAD, t), lambda p, i: (p, i)),
            pl.BlockSpec((s, 2 * HEAD_PAD), lambda p, i: (0, p)),
            pl.BlockSpec((s, 2 * HEAD_PAD), lambda p, i: (0, p)),
            pl.BlockSpec((2 * V_HEAD_DIM, s), lambda p, i: (p, 0)),
        ] + [any_spec] * n_w,
        out_specs=[
            pl.BlockSpec((t, 2 * V_HEAD_DIM), lambda p, i: (i, p)),
            pl.BlockSpec((t, 2 * HEAD_PAD), lambda p, i: (i, p)),
        ] + [any_spec] * n_w,
        out_shape=[jax.ShapeDtypeStruct((s, MLA_WIDTH), F32), jax.ShapeDtypeStruct((s, MLA_HEADS * HEAD_PAD), F32)]
        + gat.out_shape,
        scratch_shapes=gat.scratch,
        compiler_params=pltpu.CompilerParams(dimension_semantics=("arbitrary", "arbitrary"),
                                             vmem_limit_bytes=VMEM_LIMIT),
    )(q_t, k, k, v_t, *late_shards)
    return out[0], out[1], out[2:]


def _mid(o, gattn, u, gpool, gmerge, x, target, pool_w, pool_scale, w_ba, w_bp, w_out, norm_final, tm):
    s = x.shape[0]
    n_tiles = s // tm
    halo_per_tile = tm // POOL_HALO

    def body(o_ref, ga_ref, u_ref, uh_ref, gp_ref, gm_ref, x_ref, t_ref, pw_ref, ps_ref, wba_ref, wbp_ref,
             wout_ref, gf_ref,
             do_ref, dl_ref, dga_ref, dgp_ref, dgm_ref, ddc_ref, dh_ref,
             loss_ref, dwout_out, dwba_out, dwbp_out, dpw_ref, dps_ref, dgf_ref,
             ubuf, dwout_ref, dwba_ref, dwbp_ref):
        i = pl.program_id(0)

        @pl.when(i == 0)
        def _():
            loss_ref[...] = jnp.zeros_like(loss_ref)
            dwout_ref[...] = jnp.zeros_like(dwout_ref)
            dwba_ref[...] = jnp.zeros_like(dwba_ref)
            dwbp_ref[...] = jnp.zeros_like(dwbp_ref)
            dpw_ref[...] = jnp.zeros_like(dpw_ref)
            dps_ref[...] = jnp.zeros_like(dps_ref)
            dgf_ref[...] = jnp.zeros_like(dgf_ref)

        o = o_ref[...]
        ga = ga_ref[...]
        sga = _sigmoid(ga)
        silu_a = ga * sga
        y_attn = (o * silu_a).astype(BF16)

        ubuf[0:POOL_HALO, :] = jnp.where(i > 0, uh_ref[...], 0.0)
        ubuf[POOL_HALO:, :] = u_ref[...]
        row = lax.broadcasted_iota(jnp.int32, (tm, POOL_GROUP_DIM), 0) + i * tm
        ps = ps_ref[...]
        gp = gp_ref[...]
        sgp = _sigmoid(gp)
        silu_p = gp * sgp
        d_bf, dm, inv_cnt = [], [], []
        for g, w in enumerate(POOL_WINDOWS):
            cols = slice(g * POOL_GROUP_DIM, (g + 1) * POOL_GROUP_DIM)
            wsum = ubuf[POOL_HALO:, cols]
            for kk in range(1, w):
                wsum = wsum + ubuf[POOL_HALO - kk:POOL_HALO - kk + tm, cols]
            inv = 1.0 / jnp.minimum(row + 1, w).astype(F32)
            dg = (wsum * inv - ubuf[POOL_HALO:, cols]).astype(BF16)
            d_bf.append(dg)
            inv_cnt.append(inv)
            dm.append(_dot(dg, pw_ref[g]))
        dm = jnp.concatenate(dm, axis=1)
        yp = dm * ps
        y_pool = (yp * silu_p).astype(BF16)

        a = jnp.concatenate([_dot(y_attn, wba_ref[j]) for j in range(N_CHIPS)], axis=1)
        p = jnp.concatenate([_dot(y_pool, wbp_ref[j]) for j in range(N_CHIPS)], axis=1)
        gate_a = _sigmoid(gm_ref[:, :D_MODEL])
        gate_p = _sigmoid(gm_ref[:, D_MODEL:])
        merged = (gate_a * a + gate_p * p).astype(BF16)
        h = x_ref[...] + _dot(merged, wout_ref[...])
        gf = gf_ref[...]
        y, xhat, r = _rms_fwd(h, gf)
        err = y - t_ref[...]
        e2 = err * err
        e2 = jnp.sum(e2.reshape(tm // 8, 8, D_MODEL), axis=0)
        acc = e2[:, 0:LANES]
        for cidx in range(1, D_MODEL // LANES):
            acc = acc + e2[:, cidx * LANES:(cidx + 1) * LANES]
        loss_ref[...] += acc

        dy = err * (1.0 / D_MODEL)
        dgf_ref[...] += _colsum(dy * xhat)
        dh = _rms_bwd(dy, xhat, r, gf)
        dh_ref[...] = dh
        dh_bf = dh.astype(BF16)
        dwout_ref[...] += _dot_tn(merged, dh_bf)
        dmerged = _dot_nt(dh_bf, wout_ref[...])
        da = (dmerged * gate_a).astype(BF16)
        dp = (dmerged * gate_p).astype(BF16)
        dgm_ref[:, :D_MODEL] = (dmerged * a * gate_a * (1.0 - gate_a)).astype(BF16)
        dgm_ref[:, D_MODEL:] = (dmerged * p * gate_p * (1.0 - gate_p)).astype(BF16)
        dy_attn = dy_pool = None
        for j in range(N_CHIPS):
            cols = slice(j * BRANCH_COLS, (j + 1) * BRANCH_COLS)
            dwba_ref[j] += _dot_tn(y_attn, da[:, cols])
            dwbp_ref[j] += _dot_tn(y_pool, dp[:, cols])
            pa = _dot_nt(da[:, cols], wba_ref[j])
            pp = _dot_nt(dp[:, cols], wbp_ref[j])
            dy_attn = pa if dy_attn is None else dy_attn + pa
            dy_pool = pp if dy_pool is None else dy_pool + pp

        do = dy_attn * silu_a
        do_ref[...] = do
        dga_ref[...] = (dy_attn * o * (sga * (1.0 + ga * (1.0 - sga)))).astype(BF16)
        doo = do * o
        for hd in range(MLA_HEADS):
            dl = jnp.sum(doo[:, hd * V_HEAD_DIM:(hd + 1) * V_HEAD_DIM], axis=1, keepdims=True)
            dl_ref[:, hd * HEAD_PAD:(hd + 1) * HEAD_PAD] = jnp.broadcast_to(dl, (tm, HEAD_PAD))

        dyp = dy_pool * silu_p
        dgp_ref[...] = (dy_pool * yp * (sgp * (1.0 + gp * (1.0 - sgp)))).astype(BF16)
        dps_ref[...] += _colsum(dyp * dm)
        dmm = (dyp * ps).astype(BF16)
        for g in range(len(POOL_WINDOWS)):
            cols = slice(g * POOL_GROUP_DIM, (g + 1) * POOL_GROUP_DIM)
            dpw_ref[g] += _dot_tn(d_bf[g], dmm[:, cols])
            ddc_ref[:, cols] = _dot_nt(dmm[:, cols], pw_ref[g]) * inv_cnt[g]

        @pl.when(i == n_tiles - 1)
        def _():
            dwout_out[...] = dwout_ref[...].astype(BF16)
            dwba_out[...] = dwba_ref[...].astype(BF16)
            dwbp_out[...] = dwbp_ref[...].astype(BF16)

    row_in = lambda n: _row_spec(tm, n)
    in_specs = [
        row_in(MLA_WIDTH), row_in(MLA_WIDTH), row_in(POOL_WIDTH),
        pl.BlockSpec((POOL_HALO, POOL_WIDTH), lambda i: (jnp.maximum(i * halo_per_tile - 1, 0), 0)),
        row_in(POOL_WIDTH), row_in(2 * D_MODEL), row_in(D_MODEL), row_in(D_MODEL),
        _full_spec((4, POOL_GROUP_DIM, POOL_GROUP_DIM)), _full_spec((1, POOL_WIDTH)),
        _full_spec((N_CHIPS, MLA_WIDTH, BRANCH_COLS)), _full_spec((N_CHIPS, POOL_WIDTH, BRANCH_COLS)),
        _full_spec((D_MODEL, D_MODEL)), _full_spec((1, D_MODEL)),
    ]
    out_shape = [
        jax.ShapeDtypeStruct((s, MLA_WIDTH), F32),
        jax.ShapeDtypeStruct((s, MLA_HEADS * HEAD_PAD), F32),
        jax.ShapeDtypeStruct((s, MLA_WIDTH), BF16),
        jax.ShapeDtypeStruct((s, POOL_WIDTH), BF16),
        jax.ShapeDtypeStruct((s, 2 * D_MODEL), BF16),
        jax.ShapeDtypeStruct((s, POOL_WIDTH), F32),
        jax.ShapeDtypeStruct((s, D_MODEL), F32),
        jax.ShapeDtypeStruct((8, LANES), F32),
        jax.ShapeDtypeStruct((D_MODEL, D_MODEL), BF16),
        jax.ShapeDtypeStruct((N_CHIPS, MLA_WIDTH, BRANCH_COLS), BF16),
        jax.ShapeDtypeStruct((N_CHIPS, POOL_WIDTH, BRANCH_COLS), BF16),
        jax.ShapeDtypeStruct((4, POOL_GROUP_DIM, POOL_GROUP_DIM), F32),
        jax.ShapeDtypeStruct((1, POOL_WIDTH), F32),
        jax.ShapeDtypeStruct((1, D_MODEL), F32),
    ]
    out_specs = [
        row_in(MLA_WIDTH), row_in(MLA_HEADS * HEAD_PAD), row_in(MLA_WIDTH), row_in(POOL_WIDTH),
        row_in(2 * D_MODEL), row_in(POOL_WIDTH), row_in(D_MODEL),
        _full_spec((8, LANES)), _full_spec((D_MODEL, D_MODEL)), _full_spec((N_CHIPS, MLA_WIDTH, BRANCH_COLS)),
        _full_spec((N_CHIPS, POOL_WIDTH, BRANCH_COLS)), _full_spec((4, POOL_GROUP_DIM, POOL_GROUP_DIM)),
        _full_spec((1, POOL_WIDTH)), _full_spec((1, D_MODEL)),
    ]
    return pl.pallas_call(
        body,
        name="mid",
        grid=(n_tiles,),
        in_specs=in_specs,
        out_specs=out_specs,
        out_shape=out_shape,
        scratch_shapes=[
            pltpu.VMEM((tm + POOL_HALO, POOL_WIDTH), F32),
            pltpu.VMEM((D_MODEL, D_MODEL), F32),
            pltpu.VMEM((N_CHIPS, MLA_WIDTH, BRANCH_COLS), F32),
            pltpu.VMEM((N_CHIPS, POOL_WIDTH, BRANCH_COLS), F32),
        ],
        compiler_params=pltpu.CompilerParams(dimension_semantics=("arbitrary",), vmem_limit_bytes=VMEM_LIMIT),
    )(o, gattn, u, u, gpool, gmerge, x, target, pool_w, pool_scale, w_ba, w_bp, w_out, norm_final)


def _attn_bwd(q, k, v, do, lse, delta, late_grads, t):
    s = q.shape[0]
    pairs = MLA_HEADS // 2
    n_q = s // t
    red = _Reduce(COMM_PARAMS[3:])
    n_w = len(red.params)

    def body(q_ref, do_ref, lse_ref, dl_ref, k_ref, v_ref, *rest):
        g_in, (dq_ref, dk_ref, dv_ref), g_out = rest[:n_w], rest[n_w:n_w + 3], rest[n_w + 3:2 * n_w + 3]
        red.bind(g_in, g_out, rest[2 * n_w + 3:])
        i = pl.program_id(1)
        step_no = pl.program_id(0) * n_q + i
        pl.when(step_no == 0)(red.start)
        pl.when(step_no == n_q)(red.exchange)

        @pl.when(i == 0)
        def _():
            dk_ref[...] = jnp.zeros_like(dk_ref)
            dv_ref[...] = jnp.zeros_like(dv_ref)

        mask = _chunk_mask(t, False)
        qcs = [slice(hh * HEAD_PAD, (hh + 1) * HEAD_PAD) for hh in range(2)]
        vcs = [slice(hh * V_HEAD_DIM, (hh + 1) * V_HEAD_DIM) for hh in range(2)]
        qhs = [q_ref[:, qc] for qc in qcs]
        dohs = [do_ref[:, vc].astype(BF16) for vc in vcs]
        lses = [lse_ref[:, hh * HEAD_PAD:hh * HEAD_PAD + 1] for hh in range(2)]
        dls = [dl_ref[:, hh * HEAD_PAD:hh * HEAD_PAD + 1] for hh in range(2)]

        def step(j, dqs, masked):
            keys = pl.ds(pl.multiple_of(j * t, t), t)
            out = []
            for hh in range(2):
                kj = k_ref[keys, qcs[hh]]
                vj = v_ref[keys, vcs[hh]]
                p = jnp.exp2(_dot_nt(qhs[hh], kj) * ATT_SCALE_LOG2E - lses[hh])
                if masked:
                    p = jnp.where(mask, p, 0.0)
                ds = (p * (_dot_nt(dohs[hh], vj) - dls[hh])).astype(BF16)
                dv_ref[keys, vcs[hh]] += _dot_tn(p.astype(BF16), dohs[hh])
                dk_ref[keys, qcs[hh]] += _dot_tn(ds, qhs[hh]) * ATT_SCALE
                out.append(dqs[hh] + _dot(ds, kj))
            return tuple(out)

        zero = jnp.zeros((t, HEAD_PAD), F32)
        dqs = lax.fori_loop(0, i, functools.partial(step, masked=False), (zero, zero))
        dqs = step(i, dqs, True)
        for hh in range(2):
            dq_ref[:, qcs[hh]] = dqs[hh] * ATT_SCALE
        pl.when(step_no == pairs * n_q - 1)(red.finish)

    hw = MLA_HEADS * HEAD_PAD
    any_spec = pl.BlockSpec(memory_space=pl.ANY)
    out = pl.pallas_call(
        body,
        name="attn_bwd",
        grid=(pairs, n_q),
        in_specs=[
            pl.BlockSpec((t, 2 * HEAD_PAD), lambda p, i: (i, p)),
            pl.BlockSpec((t, 2 * V_HEAD_DIM), lambda p, i: (i, p)),
            pl.BlockSpec((t, 2 * HEAD_PAD), lambda p, i: (i, p)),
            pl.BlockSpec((t, 2 * HEAD_PAD), lambda p, i: (i, p)),
            pl.BlockSpec((s, 2 * HEAD_PAD), lambda p, i: (0, p)),
            pl.BlockSpec((s, 2 * V_HEAD_DIM), lambda p, i: (0, p)),
        ] + [any_spec] * n_w,
        out_specs=[
            pl.BlockSpec((t, 2 * HEAD_PAD), lambda p, i: (i, p)),
            pl.BlockSpec((s, 2 * HEAD_PAD), lambda p, i: (0, p)),
            pl.BlockSpec((s, 2 * V_HEAD_DIM), lambda p, i: (0, p)),
        ] + [any_spec] * n_w,
        out_shape=[jax.ShapeDtypeStruct((s, hw), F32), jax.ShapeDtypeStruct((s, hw), F32),
                   jax.ShapeDtypeStruct((s, MLA_WIDTH), F32)] + red.out_shape,
        scratch_shapes=red.scratch,
        compiler_params=pltpu.CompilerParams(dimension_semantics=("arbitrary", "arbitrary"),
                                             vmem_limit_bytes=VMEM_LIMIT),
    )(q, do, lse, delta, k, v, *late_grads)
    return out[0], out[1], out[2], out[3:]


def _qkv_bwd(dq, dk, dv, zq, zkv, q_norm, kv_norm, wuq_p, wk_p, wv, rc, rsa, rsb, tm):
    s = zq.shape[0]
    hw = MLA_HEADS * HEAD_PAD

    def body(dq_ref, dk_ref, dv_ref, zq_ref, zkv_ref, gq_ref, gkv_ref, wuq_ref, wk_ref, wv_ref,
             c_ref, sa_ref, sb_ref,
             dzq_ref, dzkv_ref, dzkr_ref, dwuq_ref, dwk_ref, dwv_ref, dgq_ref, dgkv_ref):
        i = pl.program_id(0)

        @pl.when(i == 0)
        def _():
            dwuq_ref[...] = jnp.zeros_like(dwuq_ref)
            dwk_ref[...] = jnp.zeros_like(dwk_ref)
            dwv_ref[...] = jnp.zeros_like(dwv_ref)
            dgq_ref[...] = jnp.zeros_like(dgq_ref)
            dgkv_ref[...] = jnp.zeros_like(dgkv_ref)

        c, sa, sb = c_ref[...], sa_ref[...], sb_ref[...]
        gq, gkv = gq_ref[...], gkv_ref[...]

        cq, xq, rq = _rms_fwd(zq_ref[...], gq)
        dqp = jnp.concatenate(
            [_unrope(dq_ref[:, h * HEAD_PAD:(h + 1) * HEAD_PAD], c, sa, sb) for h in range(MLA_HEADS)],
            axis=1).astype(BF16)
        dwuq_ref[...] += _dot_tn(cq.astype(BF16), dqp)
        dcq = _dot_nt(dqp, wuq_ref[...])
        dgq_ref[...] += _colsum(dcq * xq)
        dzq_ref[...] = _rms_bwd(dcq, xq, rq, gq).astype(BF16)

        ckv, xkv, rkv = _rms_fwd(zkv_ref[...], gkv)
        ckv = ckv.astype(BF16)
        dkf = dk_ref[...]
        dk_bf = dkf.astype(BF16)
        dv_bf = dv_ref[...].astype(BF16)
        dwk_ref[...] += _dot_tn(ckv, dk_bf)
        dwv_ref[...] += _dot_tn(ckv, dv_bf)
        dckv = _dot_nt(dk_bf, wk_ref[...]) + _dot_nt(dv_bf, wv_ref[...])
        dgkv_ref[...] += _colsum(dckv * xkv)
        dzkv_ref[...] = _rms_bwd(dckv, xkv, rkv, gkv).astype(BF16)

        dkr = dkf[:, 0:HEAD_PAD]
        for h in range(1, MLA_HEADS):
            dkr = dkr + dkf[:, h * HEAD_PAD:(h + 1) * HEAD_PAD]
        dkr = pltpu.roll(_unrope(dkr, c, sa, sb), 64, 1)
        lane = lax.broadcasted_iota(jnp.int32, (tm, HEAD_PAD), 1)
        dzkr_ref[...] = jnp.where(lane < QK_ROPE_DIM, dkr, 0.0).astype(BF16)

    return pl.pallas_call(
        body,
        name="qkv_bwd",
        grid=(s // tm,),
        in_specs=[
            _row_spec(tm, hw), _row_spec(tm, hw), _row_spec(tm, MLA_WIDTH),
            _row_spec(tm, Q_LORA_RANK), _row_spec(tm, KV_LORA_RANK),
            _full_spec((1, Q_LORA_RANK)), _full_spec((1, KV_LORA_RANK)),
            _full_spec((Q_LORA_RANK, hw)), _full_spec((KV_LORA_RANK, hw)), _full_spec((KV_LORA_RANK, MLA_WIDTH)),
            _row_spec(tm, HEAD_PAD), _row_spec(tm, HEAD_PAD), _row_spec(tm, HEAD_PAD),
        ],
        out_specs=[
            _row_spec(tm, Q_LORA_RANK), _row_spec(tm, KV_LORA_RANK), _row_spec(tm, HEAD_PAD),
            _full_spec((Q_LORA_RANK, hw)), _full_spec((KV_LORA_RANK, hw)), _full_spec((KV_LORA_RANK, MLA_WIDTH)),
            _full_spec((1, Q_LORA_RANK)), _full_spec((1, KV_LORA_RANK)),
        ],
        out_shape=[
            jax.ShapeDtypeStruct((s, Q_LORA_RANK), BF16), jax.ShapeDtypeStruct((s, KV_LORA_RANK), BF16),
            jax.ShapeDtypeStruct((s, HEAD_PAD), BF16),
            jax.ShapeDtypeStruct((Q_LORA_RANK, hw), F32), jax.ShapeDtypeStruct((KV_LORA_RANK, hw), F32),
            jax.ShapeDtypeStruct((KV_LORA_RANK, MLA_WIDTH), F32),
            jax.ShapeDtypeStruct((1, Q_LORA_RANK), F32), jax.ShapeDtypeStruct((1, KV_LORA_RANK), F32),
        ],
        compiler_params=pltpu.CompilerParams(dimension_semantics=("arbitrary",), vmem_limit_bytes=VMEM_LIMIT),
    )(dq, dk, dv, zq, zkv, q_norm, kv_norm, wuq_p, wk_p, wv, rc, rsa, rsb)


def _inproj_bwd_x(dzq, dzkv, dzkr, dgattn, ddc, dgpool, dgmerge, x, dh, norm_in, w_in_t, tm):
    s = x.shape[0]
    n_tiles = s // tm
    halo_per_tile = tm // POOL_HALO
    n_halo = s // POOL_HALO
    u_seg = 4

    def body(dzq_ref, dzkv_ref, dzkr_ref, dga_ref, ddc_ref, ddn_ref, dgp_ref, dgm_ref, x_ref, dh_ref,
             g_ref, w_hbm, gx_ref, dgin_ref, dzs_ref, w_vmem, dbuf, sem):
        i = pl.program_id(0)

        @pl.when(i == 0)
        def _():
            cp = pltpu.make_async_copy(w_hbm, w_vmem, sem)
            cp.start()
            dgin_ref[...] = jnp.zeros_like(dgin_ref)
            cp.wait()

        dbuf[0:tm, :] = ddc_ref[...]
        dbuf[tm:, :] = jnp.where(i < n_tiles - 1, ddn_ref[...], 0.0)
        row = lax.broadcasted_iota(jnp.int32, (tm, POOL_GROUP_DIM), 0) + i * tm
        du = []
        for g, w in enumerate(POOL_WINDOWS):
            cols = slice(g * POOL_GROUP_DIM, (g + 1) * POOL_GROUP_DIM)
            fsum = dbuf[0:tm, cols]
            for kk in range(1, w):
                fsum = fsum + dbuf[kk:kk + tm, cols]
            du.append(fsum - dbuf[0:tm, cols] * jnp.minimum(row + 1, w).astype(F32))
        du = jnp.concatenate(du, axis=1).astype(BF16)

        dz = [dzq_ref[...], dzkv_ref[...], dzkr_ref[...], dga_ref[...], du, dgp_ref[...], dgm_ref[...]]
        dhn = None
        for j, pieces in enumerate(SHARD_PIECES):
            parts = [dz[seg][:, lo:hi] for seg, lo, hi, _ in pieces]
            dzj = parts[0] if len(parts) == 1 else jnp.concatenate(parts, axis=1)
            dzs_ref[j] = dzj
            part = _dot(dzj, w_vmem[j])
            dhn = part if dhn is None else dhn + part

        g = g_ref[...]
        _, xhat, r = _rms_fwd(x_ref[...], g)
        dgin_ref[...] += _colsum(dhn * xhat)
        gx_ref[...] = dh_ref[...] + _rms_bwd(dhn, xhat, r, g)

    any_spec = pl.BlockSpec(memory_space=pl.ANY)
    seg_w = [wide for _, wide in IN_SEGMENTS]
    return pl.pallas_call(
        body,
        name="inproj_bwd_x",
        grid=(n_tiles,),
        in_specs=[
            _row_spec(tm, seg_w[0]), _row_spec(tm, seg_w[1]), _row_spec(tm, seg_w[2]),
            _row_spec(tm, seg_w[3]), _row_spec(tm, seg_w[u_seg]),
            pl.BlockSpec((POOL_HALO, POOL_WIDTH), lambda i: (jnp.minimum((i + 1) * halo_per_tile, n_halo - 1), 0)),
            _row_spec(tm, seg_w[5]), _row_spec(tm, seg_w[6]),
            _row_spec(tm, D_MODEL), _row_spec(tm, D_MODEL),
            _full_spec((1, D_MODEL)), any_spec,
        ],
        out_specs=[_row_spec(tm, D_MODEL), _full_spec((1, D_MODEL)),
                   pl.BlockSpec((N_CHIPS, tm, SHARD_COLS), lambda i: (0, i, 0))],
        out_shape=[jax.ShapeDtypeStruct((s, D_MODEL), F32), jax.ShapeDtypeStruct((1, D_MODEL), F32),
                   jax.ShapeDtypeStruct((N_CHIPS, s, SHARD_COLS), BF16)],
        scratch_shapes=[
            pltpu.VMEM((N_CHIPS, SHARD_COLS, D_MODEL), BF16),
            pltpu.VMEM((tm + POOL_HALO, POOL_WIDTH), F32),
            pltpu.SemaphoreType.DMA,
        ],
        compiler_params=pltpu.CompilerParams(dimension_semantics=("arbitrary",), vmem_limit_bytes=VMEM_LIMIT),
    )(dzq, dzkv, dzkr, dgattn, ddc, ddc, dgpool, dgmerge, x, dh, norm_in, w_in_t)


def _inproj_bwd_w(order, dz_sh, hn, g_uq, g_ukv, gs, tm):
    s = hn.shape[0]
    n_tiles = s // tm
    mid = n_tiles // 2
    hc = D_MODEL // 2
    red = _Reduce(COMM_PARAMS[1:3])
    n_red = len(red.scratch)

    def body(order_ref, dz_ref, hn_ref, guq_hbm, gukv_hbm, gs_ref, gw_hbm, guq_out, gukv_out, gsum_ref,
             acc, pm_w, a_w, b_w, r_w, s_buf, w_send, w_recv, w_local, *red_scratch):
        ph, i = pl.program_id(0), pl.program_id(1)
        x, y, c = lax.axis_index("x"), lax.axis_index("y"), lax.axis_index("c")
        k = 2 * x + y
        me, sibling = (x, y, c), (x, y, 1 - c)
        chips = _other_chips(x, y)
        shard_of_phase = [2 * cx + cy for cx, cy in chips] + [k]
        copy = _remote_copier(w_send, w_recv)
        red.bind([guq_hbm, gukv_hbm], [guq_out, gukv_out], red_scratch)
        mine = pl.ds(pl.multiple_of(c * hc, hc), hc)
        theirs = pl.ds(pl.multiple_of((1 - c) * hc, hc), hc)
        flips = [(fx, fy, fc) for fx in (0, 1) for fy in (0, 1) for fc in (0, 1)][1:]

        def to_sibling(f):
            j = shard_of_phase[f]
            return copy(f, pm_w.at[j, 1 - c], a_w.at[j], sibling)

        def pair_sum(f):
            cx, cy = chips[f]
            return copy(4 + f, pm_w.at[shard_of_phase[f], c], b_w.at[f], (cx, cy, c))

        def small(f):
            fx, fy, fc = flips[f - 1]
            peer = (1 - x if fx else x, 1 - y if fy else y, 1 - c if fc else c)
            return copy(7 + f, gs_ref, s_buf.at[f], peer)

        def finished():
            return copy(7, r_w, gw_hbm.at[:, mine], sibling)

        @pl.when(jnp.logical_and(ph == 0, i == 0))
        def _():
            red.start()
            for f in range(1, N_DEV):
                small(f).start()
            s_buf[0] = gs_ref[...]

        part = _dot_tn(dz_ref[0], hn_ref[...])

        @pl.when(i == 0)
        def _():
            acc[...] = part

        @pl.when(i > 0)
        def _():
            acc[...] += part

        for f in range(3):
            @pl.when(jnp.logical_and(ph == f + 1, i == mid))
            def _(f=f):
                j = shard_of_phase[f]
                copy(f, a_w.at[j], a_w.at[j], me).wait_recv()
                pm_w[j, c] = (pm_w[j, c].astype(F32) + a_w[j].astype(F32)).astype(BF16)
                pair_sum(f).start()
                if f == 0:
                    red.exchange()

        for f in range(4):
            @pl.when(jnp.logical_and(ph == f, i == n_tiles - 1))
            def _(f=f):
                j = shard_of_phase[f]
                pm_w[j, 0] = acc[:, :hc].astype(BF16)
                pm_w[j, 1] = acc[:, hc:].astype(BF16)
                to_sibling(f).start()
                if f < 3:
                    return
                copy(3, a_w.at[k], a_w.at[k], me).wait_recv()
                r_w[...] = pm_w[k, c].astype(F32) + a_w[k].astype(F32)
                for g in range(3):
                    copy(4 + g, b_w.at[g], b_w.at[g], me).wait_recv()
                    r_w[...] = r_w[...] + b_w[g].astype(F32)
                store = pltpu.make_async_copy(r_w, gw_hbm.at[:, mine], w_local)
                store.start()
                finished().start()
                red.finish()
                for g in range(1, N_DEV):
                    copy(7 + g, s_buf.at[g], s_buf.at[g], me).wait_recv()
                dev = 4 * x + 2 * y + c
                total = s_buf[dev]
                for d in range(1, N_DEV):
                    total = total + s_buf[jnp.bitwise_xor(dev, d)]
                gsum_ref[...] = total
                copy(7, gw_hbm.at[:, theirs], gw_hbm.at[:, theirs], me).wait_recv()
                store.wait()
                for g in range(4):
                    to_sibling(g).wait_send()
                for g in range(3):
                    pair_sum(g).wait_send()
                finished().wait_send()
                for g in range(1, N_DEV):
                    small(g).wait_send()

    any_spec = pl.BlockSpec(memory_space=pl.ANY)
    n_sem = 8 + N_DEV - 1
    grid_spec = pltpu.PrefetchScalarGridSpec(
        num_scalar_prefetch=1,
        grid=(N_CHIPS, n_tiles),
        in_specs=[
            pl.BlockSpec((1, tm, SHARD_COLS), lambda ph, i, order: (order[ph], i, 0)),
            pl.BlockSpec((tm, D_MODEL), lambda ph, i, order: (i, 0)),
            any_spec, any_spec,
            pl.BlockSpec((SMALL_ROWS, LANES), lambda ph, i, order: (0, 0)),
        ],
        out_specs=[any_spec, any_spec, any_spec, pl.BlockSpec((SMALL_ROWS, LANES), lambda ph, i, order: (0, 0))],
        scratch_shapes=[
            pltpu.VMEM((SHARD_COLS, D_MODEL), F32),
            pltpu.VMEM((N_CHIPS, 2, SHARD_COLS, hc), BF16),
            pltpu.VMEM((N_CHIPS, SHARD_COLS, hc), BF16),
            pltpu.VMEM((3, SHARD_COLS, hc), BF16),
            pltpu.VMEM((SHARD_COLS, hc), F32),
            pltpu.VMEM((N_DEV, SMALL_ROWS, LANES), F32),
            pltpu.SemaphoreType.DMA((n_sem,)), pltpu.SemaphoreType.DMA((n_sem,)), pltpu.SemaphoreType.DMA,
        ] + red.scratch,
    )
    out = pl.pallas_call(
        body,
        name="inproj_bwd_w",
        grid_spec=grid_spec,
        out_shape=[jax.ShapeDtypeStruct((SHARD_COLS, D_MODEL), F32)] + red.out_shape
        + [jax.ShapeDtypeStruct((SMALL_ROWS, LANES), F32)],
        compiler_params=pltpu.CompilerParams(dimension_semantics=("arbitrary", "arbitrary"),
                                             vmem_limit_bytes=VMEM_LIMIT),
    )(order, dz_sh, hn, g_uq, g_ukv, gs)
    return out[0], out[1], out[2], out[3]


def _other_chips(x, y):
    return ((1 - x, y), (x, 1 - y), (1 - x, 1 - y))


def _half(ref, axis, size, c, lead=()):
    window = pl.ds(pl.multiple_of(c * size, size), size)
    if axis == 0:
        return ref.at[(*lead, window, slice(None))]
    return ref.at[(*lead, slice(None), window)]


def _half_shape(rows, cols, axis, size):
    return (size, cols) if axis == 0 else (rows, size)


def _remote_copier(send_sems, recv_sems):
    def copy(sem, src, dst, to):
        return pltpu.make_async_remote_copy(src_ref=src, dst_ref=dst, send_sem=send_sems.at[sem],
                                            recv_sem=recv_sems.at[sem], device_id=to, device_id_type=MESH)
    return copy


class _Gather:
    def __init__(self, params):
        self.params = params
        n = len(params)
        self.scratch = [pltpu.SemaphoreType.DMA((6 * n,)), pltpu.SemaphoreType.DMA((6 * n,)),
                        pltpu.SemaphoreType.DMA((n,))]
        self.out_shape = [jax.ShapeDtypeStruct((N_CHIPS, r, cc), BF16) for _, r, cc, _, _ in params]

    def bind(self, ins, outs, scratch):
        self.ins, self.outs = ins, outs
        send_sems, recv_sems, self.local_sems = scratch
        self.copy = _remote_copier(send_sems, recv_sems)
        self.x, self.y, self.c = lax.axis_index("x"), lax.axis_index("y"), lax.axis_index("c")
        self.k = 2 * self.x + self.y
        self.chips = _other_chips(self.x, self.y)

    def _local(self, p):
        return pltpu.make_async_copy(self.ins[p], self.outs[p].at[self.k], self.local_sems.at[p])

    def _first(self, p, j):
        _, _, _, axis, size = self.params[p]
        cx, cy = self.chips[j]
        return self.copy(6 * p + j, _half(self.ins[p], axis, size, self.c),
                         _half(self.outs[p], axis, size, self.c, (self.k,)), (cx, cy, self.c))

    def _relay(self, p, j, half_of):
        _, _, _, axis, size = self.params[p]
        cx, cy = self.chips[j]
        block = _half(self.outs[p], axis, size, half_of, (2 * cx + cy,))
        return self.copy(6 * p + 3 + j, block, block, (self.x, self.y, 1 - self.c))

    def start(self):
        for p in range(len(self.params)):
            self._local(p).start()
            for j in range(3):
                self._first(p, j).start()

    def relay(self):
        for j in range(3):
            for p, (_, _, _, axis, size) in enumerate(self.params):
                cx, cy = self.chips[j]
                landed = _half(self.outs[p], axis, size, self.c, (2 * cx + cy,))
                self.copy(6 * p + j, landed, landed, (self.x, self.y, self.c)).wait_recv()
                self._relay(p, j, self.c).start()

    def finish(self):
        for j in range(3):
            for p in range(len(self.params)):
                self._relay(p, j, 1 - self.c).wait_recv()
        for p in range(len(self.params)):
            for j in range(3):
                self._first(p, j).wait_send()
                self._relay(p, j, self.c).wait_send()
            self._local(p).wait()


class _Reduce:
    def __init__(self, params):
        self.params = params
        n = len(params)
        halves = [_half_shape(r, cc, axis, size) for _, r, cc, axis, size in params]
        self.scratch = ([pltpu.VMEM((N_CHIPS, *h), BF16) for h in halves]
                        + [pltpu.VMEM((N_CHIPS, *h), BF16) for h in halves]
                        + [pltpu.VMEM((3, *h), BF16) for h in halves]
                        + [pltpu.VMEM(h, F32) for h in halves]
                        + [pltpu.SemaphoreType.DMA((5 * n,)), pltpu.SemaphoreType.DMA((5 * n,)),
                           pltpu.SemaphoreType.DMA((2 * n,))])
        self.out_shape = [jax.ShapeDtypeStruct((r, cc), F32) for _, r, cc, _, _ in params]

    def bind(self, g_in, g_out, scratch):
        n = len(self.params)
        self.g_in, self.g_out = g_in, g_out
        self.pm, self.a_buf = scratch[0:n], scratch[n:2 * n]
        self.b_buf, self.r_buf = scratch[2 * n:3 * n], scratch[3 * n:4 * n]
        send_sems, recv_sems, self.local_sems = scratch[4 * n:]
        self.copy = _remote_copier(send_sems, recv_sems)
        self.x, self.y, self.c = lax.axis_index("x"), lax.axis_index("y"), lax.axis_index("c")
        self.k = 2 * self.x + self.y
        self.chips = _other_chips(self.x, self.y)
        self.me = (self.x, self.y, self.c)
        self.sibling = (self.x, self.y, 1 - self.c)

    def _load(self, p):
        _, _, _, axis, size = self.params[p]
        return pltpu.make_async_copy(_half(self.g_in[p], axis, size, self.c, (slice(None),)), self.pm[p],
                                     self.local_sems.at[p])

    def _to_sibling(self, p):
        _, _, _, axis, size = self.params[p]
        return self.copy(5 * p, _half(self.g_in[p], axis, size, 1 - self.c, (slice(None),)), self.a_buf[p],
                         self.sibling)

    def _pair_sum(self, p, j):
        cx, cy = self.chips[j]
        return self.copy(5 * p + 1 + j, self.pm[p].at[2 * cx + cy], self.b_buf[p].at[j], (cx, cy, self.c))

    def _store(self, p):
        _, _, _, axis, size = self.params[p]
        n = len(self.params)
        return pltpu.make_async_copy(self.r_buf[p], _half(self.g_out[p], axis, size, self.c),
                                     self.local_sems.at[n + p])

    def _finished(self, p):
        _, _, _, axis, size = self.params[p]
        return self.copy(5 * p + 4, self.r_buf[p], _half(self.g_out[p], axis, size, self.c), self.sibling)

    def start(self):
        for p in range(len(self.params)):
            self._load(p).start()
            self._to_sibling(p).start()

    def exchange(self):
        for p in range(len(self.params)):
            self._load(p).wait()
            self.copy(5 * p, self.a_buf[p], self.a_buf[p], self.me).wait_recv()
            for j, (cx, cy) in enumerate(self.chips):
                kj = 2 * cx + cy
                self.pm[p][kj] = (self.pm[p][kj].astype(F32) + self.a_buf[p][kj].astype(F32)).astype(BF16)
                self._pair_sum(p, j).start()
            self.r_buf[p][...] = self.pm[p][self.k].astype(F32) + self.a_buf[p][self.k].astype(F32)

    def finish(self):
        for p, (_, _, _, axis, size) in enumerate(self.params):
            for j in range(3):
                self.copy(5 * p + 1 + j, self.b_buf[p].at[j], self.b_buf[p].at[j], self.me).wait_recv()
                self.r_buf[p][...] = self.r_buf[p][...] + self.b_buf[p][j].astype(F32)
            self._store(p).start()
            self._finished(p).start()
        for p, (_, _, _, axis, size) in enumerate(self.params):
            theirs = _half(self.g_out[p], axis, size, 1 - self.c)
            self.copy(5 * p + 4, theirs, theirs, self.me).wait_recv()
            self._store(p).wait()
            self._to_sibling(p).wait_send()
            for j in range(3):
                self._pair_sum(p, j).wait_send()
            self._finished(p).wait_send()


def _weight_gather(shards):
    gat = _Gather(COMM_PARAMS[:3])
    n = len(gat.params)

    def body(*refs):
        gat.bind(refs[:n], refs[n:2 * n], refs[2 * n:])
        gat.start()
        gat.relay()
        gat.finish()

    any_spec = pl.BlockSpec(memory_space=pl.ANY)
    return pl.pallas_call(
        body,
        name="weight_gather",
        in_specs=[any_spec] * n,
        out_specs=[any_spec] * n,
        out_shape=gat.out_shape,
        scratch_shapes=gat.scratch,
    )(*shards)


def _adamw_math(w, g, m, v):
    m = ADAM_B1 * m + (1.0 - ADAM_B1) * g
    v = ADAM_B2 * v + (1.0 - ADAM_B2) * (g * g)
    m_hat = m / (1.0 - ADAM_B1 ** ADAM_STEP)
    v_hat = v / (1.0 - ADAM_B2 ** ADAM_STEP)
    delta = -ADAM_LR * (m_hat / (jnp.sqrt(v_hat) + ADAM_EPS) + ADAM_WD * w)
    return delta, m, v


def _adamw_tiled(w, g, m, v, tm):
    rows, cols = w.shape

    def body(w_ref, g_ref, m_ref, v_ref, d_ref, nm_ref, nv_ref):
        d_ref[...], nm_ref[...], nv_ref[...] = _adamw_math(w_ref[...], g_ref[...], m_ref[...], v_ref[...])

    spec = _row_spec(tm, cols)
    return pl.pallas_call(
        body,
        name="adamw_w_in",
        grid=(rows // tm,),
        in_specs=[spec] * 4,
        out_specs=[spec] * 3,
        out_shape=[jax.ShapeDtypeStruct(w.shape, F32)] * 3,
        compiler_params=pltpu.CompilerParams(dimension_semantics=("parallel",), vmem_limit_bytes=VMEM_LIMIT),
    )(w, g, m, v)


def _adamw_many(ws, gs, ms, vs):
    n = len(ws)

    def body(*refs):
        ins, outs = refs[:4 * n], refs[4 * n:]
        for i in range(n):
            d, nm, nv = _adamw_math(ins[i][...], ins[n + i][...], ins[2 * n + i][...], ins[3 * n + i][...])
            outs[i][...] = d
            outs[n + i][...] = nm
            outs[2 * n + i][...] = nv

    vmem_spec = pl.BlockSpec(memory_space=pltpu.VMEM)
    shapes = [jax.ShapeDtypeStruct(w.shape, F32) for w in ws]
    out = pl.pallas_call(
        body,
        name="adamw_small",
        in_specs=[vmem_spec] * (4 * n),
        out_specs=[vmem_spec] * (3 * n),
        out_shape=shapes * 3,
        compiler_params=pltpu.CompilerParams(vmem_limit_bytes=VMEM_LIMIT),
    )(*ws, *gs, *ms, *vs)
    return out[:n], out[n:2 * n], out[2 * n:]


def _pack_rows(parts, rows, dtype):
    flat = jnp.concatenate([p.reshape(-1).astype(dtype) for p in parts])
    flat = jnp.concatenate([flat, jnp.zeros((rows * LANES - flat.shape[0],), dtype)])
    return flat.reshape(rows, LANES)


def _unpack_rows(packed, shapes):
    flat = packed.reshape(-1)
    out, off = [], 0
    for _, shp in shapes:
        n = int(np.prod(shp))
        out.append(flat[off:off + n].reshape(shp))
        off += n
    return out


def _rope_tables(s):
    half = QK_ROPE_DIM // 2
    inv_freq = ROPE_THETA ** (-jnp.arange(half, dtype=F32) / half)
    ang = jnp.arange(s, dtype=F32)[:, None] * inv_freq[None, :]
    cos, sin = jnp.cos(ang), jnp.sin(ang)
    z16 = jnp.zeros((s, half), F32)
    z32 = jnp.zeros((s, HEAD_PAD - QK_NOPE_DIM - QK_ROPE_DIM), F32)
    z64 = jnp.zeros((s, QK_NOPE_DIM), F32)
    rc = jnp.concatenate([jnp.ones((s, QK_NOPE_DIM), F32), cos, cos, z32], axis=1)
    rsa = jnp.concatenate([z64, -sin, z16, z32], axis=1)
    rsb = jnp.concatenate([z64, z16, sin, z32], axis=1)
    return rc, rsa, rsb


def kernel(x, norm_in, w_in, q_norm, w_uq, kv_norm, w_ukv, pool_w, pool_scale, w_branch_attn, w_branch_pool, w_out, norm_final, loss_target, m_norm_in, m_w_in, m_q_norm, m_w_uq, m_kv_norm, m_w_ukv, m_pool_w, m_pool_scale, m_w_branch_attn, m_w_branch_pool, m_w_out, m_norm_final, v_norm_in, v_w_in, v_q_norm, v_w_uq, v_kv_norm, v_w_ukv, v_pool_w, v_pool_scale, v_w_branch_attn, v_w_branch_pool, v_w_out, v_norm_final):
    s = x.shape[1]
    t_att, t_row = _tiles(s)
    x2 = x.reshape(s, D_MODEL)
    tgt = loss_target.reshape(s, D_MODEL)

    local = [w_in.T, w_uq.reshape(96, 768), w_ukv.reshape(64, 1024), w_branch_attn, w_branch_pool, w_out]
    local = [a.astype(BF16) for a in local]
    w_in_t, w_uq_all, w_ukv_all = _weight_gather(local[:3])
    w_uq_f = w_uq_all.reshape(Q_LORA_RANK, MLA_HEADS, QK_NOPE_DIM + QK_ROPE_DIM)
    w_ukv_f = w_ukv_all.reshape(KV_LORA_RANK, MLA_HEADS, QK_NOPE_DIM + V_HEAD_DIM)
    hw = MLA_HEADS * HEAD_PAD
    wuq_p = jnp.pad(w_uq_f, ((0, 0), (0, 0), (0, HEAD_PAD - QK_NOPE_DIM - QK_ROPE_DIM))).reshape(Q_LORA_RANK, hw)
    wk_p = jnp.pad(w_ukv_f[:, :, :QK_NOPE_DIM], ((0, 0), (0, 0), (0, HEAD_PAD - QK_NOPE_DIM))).reshape(KV_LORA_RANK, hw)
    wv = w_ukv_f[:, :, QK_NOPE_DIM:].reshape(KV_LORA_RANK, MLA_WIDTH)
    rc, rsa, rsb = _rope_tables(s)
    g_in = norm_in.reshape(1, -1)
    g_q = q_norm.reshape(1, -1)
    g_kv = kv_norm.reshape(1, -1)
    g_f = norm_final.reshape(1, -1)
    ps = pool_scale.reshape(1, -1)
    pw_bf = pool_w.astype(BF16)

    hn, zq, zkv, zkr, gattn, u, gpool, gmerge = _inproj_fwd(x2, g_in, w_in_t, t_row)
    q, k, v, q_t, v_t = _qkv_fwd(zq, zkv, zkr, g_q, g_kv, wuq_p, wk_p, wv, rc, rsa, rsb, t_row)
    o, lse, (w_ba_all, w_bp_all, w_out_all) = _attn_fwd(q_t, k, v_t, local[3:], t_att)
    w_out_f = w_out_all.reshape(D_MODEL, D_MODEL)

    (do, delta, dgattn, dgpool, dgmerge, ddc, dh, sq_err, d_w_out, d_w_ba, d_w_bp, d_pool_w, d_pool_scale,
     d_norm_final) = _mid(o, gattn, u, gpool, gmerge, x2, tgt, pw_bf, ps, w_ba_all, w_bp_all, w_out_f, g_f, t_row)

    late_grads = [d_w_ba, d_w_bp, d_w_out.reshape(N_CHIPS, 256, D_MODEL)]
    dq, dk, dv, (g_w_ba, g_w_bp, g_w_out) = _attn_bwd(q, k, v, do, lse, delta, late_grads, t_att)
    dzq, dzkv, dzkr, d_wuq_p, d_wk_p, d_wv, d_q_norm, d_kv_norm = _qkv_bwd(
        dq, dk, dv, zq, zkv, g_q, g_kv, wuq_p, wk_p, wv, rc, rsa, rsb, t_row)
    grad_x, d_norm_in, dz_sh = _inproj_bwd_x(dzq, dzkv, dzkr, dgattn, ddc, dgpool, dgmerge, x2, dh, g_in, w_in_t,
                                             t_row)

    d_w_uq = d_wuq_p.reshape(Q_LORA_RANK, MLA_HEADS, HEAD_PAD)[:, :, :QK_NOPE_DIM + QK_ROPE_DIM]
    d_w_ukv = jnp.concatenate([d_wk_p.reshape(KV_LORA_RANK, MLA_HEADS, HEAD_PAD)[:, :, :QK_NOPE_DIM],
                               d_wv.reshape(KV_LORA_RANK, MLA_HEADS, V_HEAD_DIM)], axis=2)
    small = dict(norm_in=d_norm_in, q_norm=d_q_norm, kv_norm=d_kv_norm, pool_scale=d_pool_scale,
                 norm_final=d_norm_final, pool_w=d_pool_w)
    gs = _pack_rows([small[n] for n, _ in SMALL_SHAPES], SMALL_ROWS, F32)
    cx, cy = lax.axis_index("x"), lax.axis_index("y")
    order = jnp.stack([2 * (1 - cx) + cy, 2 * cx + (1 - cy), 2 * (1 - cx) + (1 - cy), 2 * cx + cy]).astype(jnp.int32)
    g_w_in_t, g_w_uq, g_w_ukv, g_small = _inproj_bwd_w(
        order, dz_sh, hn, d_w_uq.reshape(N_CHIPS, 96, 768).astype(BF16),
        d_w_ukv.reshape(N_CHIPS, 64, 1024).astype(BF16), gs, 2 * t_row)
    g_norm_in, g_q_norm, g_kv_norm, g_pool_scale, g_norm_final, g_pool_w = _unpack_rows(g_small, SMALL_SHAPES)

    dl_w_in, nm_w_in, nv_w_in = (a.T for a in _adamw_tiled(w_in.T, g_w_in_t, m_w_in.T, v_w_in.T, 152))

    def two_d(a):
        return a.reshape(1, -1) if a.ndim == 1 else a.reshape(a.shape[0], -1)

    names = ["norm_in", "q_norm", "w_uq", "kv_norm", "w_ukv", "pool_w", "pool_scale", "w_branch_attn",
             "w_branch_pool", "w_out", "norm_final"]
    ws = dict(norm_in=norm_in, q_norm=q_norm, w_uq=w_uq, kv_norm=kv_norm, w_ukv=w_ukv, pool_w=pool_w,
              pool_scale=pool_scale, w_branch_attn=w_branch_attn, w_branch_pool=w_branch_pool, w_out=w_out,
              norm_final=norm_final)
    gsd = dict(norm_in=g_norm_in, q_norm=g_q_norm, w_uq=g_w_uq, kv_norm=g_kv_norm, w_ukv=g_w_ukv, pool_w=g_pool_w,
               pool_scale=g_pool_scale, w_branch_attn=g_w_ba, w_branch_pool=g_w_bp, w_out=g_w_out,
               norm_final=g_norm_final)
    msd = dict(norm_in=m_norm_in, q_norm=m_q_norm, w_uq=m_w_uq, kv_norm=m_kv_norm, w_ukv=m_w_ukv, pool_w=m_pool_w,
               pool_scale=m_pool_scale, w_branch_attn=m_w_branch_attn, w_branch_pool=m_w_branch_pool, w_out=m_w_out,
               norm_final=m_norm_final)
    vsd = dict(norm_in=v_norm_in, q_norm=v_q_norm, w_uq=v_w_uq, kv_norm=v_kv_norm, w_ukv=v_w_ukv, pool_w=v_pool_w,
               pool_scale=v_pool_scale, w_branch_attn=v_w_branch_attn, w_branch_pool=v_w_branch_pool, w_out=v_w_out,
               norm_final=v_norm_final)
    dls, nms, nvs = _adamw_many([two_d(ws[n]) for n in names], [two_d(gsd[n]) for n in names],
                                [two_d(msd[n]) for n in names], [two_d(vsd[n]) for n in names])

    grads = dict(gsd)
    grads["w_in"] = g_w_in_t.T
    delta_w = {n: d.reshape(ws[n].shape) for n, d in zip(names, dls)}
    new_m = {n: d.reshape(ws[n].shape) for n, d in zip(names, nms)}
    new_v = {n: d.reshape(ws[n].shape) for n, d in zip(names, nvs)}
    delta_w["w_in"], new_m["w_in"], new_v["w_in"] = dl_w_in, nm_w_in, nv_w_in
    ws["w_in"] = w_in

    order = ["norm_in", "w_in", "q_norm", "w_uq", "kv_norm", "w_ukv", "pool_w", "pool_scale", "w_branch_attn",
             "w_branch_pool", "w_out", "norm_final"]
    loss = lax.psum(0.5 * jnp.sum(sq_err) / D_MODEL, ("x", "y", "c"))
    return (loss, grad_x.reshape(x.shape),
            *[grads[n].reshape(ws[n].shape) for n in order],
            *[delta_w[n] for n in order], *[new_m[n] for n in order], *[new_v[n] for n in order])
```

```python
import functools

import jax
import jax.numpy as jnp
import numpy as np
from jax import lax
from jax.experimental import pallas as pl
from jax.experimental.pallas import tpu as pltpu

F32 = jnp.float32
BF16 = jnp.bfloat16
MESH = pl.DeviceIdType.MESH

D_MODEL = 1024
CHUNK = 64
MLA_HEADS = 8
QK_NOPE_DIM = 64
QK_ROPE_DIM = 32
V_HEAD_DIM = 64
Q_LORA_RANK = 384
KV_LORA_RANK = 256
MLA_WIDTH = MLA_HEADS * V_HEAD_DIM
ROPE_THETA = 10000.0
POOL_WINDOWS = (2, 4, 8, 16)
POOL_WIDTH = 512
POOL_GROUP_DIM = 128
BRANCH_COLS = D_MODEL // 4
POOL_HALO = 16
EPS = 1e-6
IN_TOTAL = 4256
HEAD_PAD = 128
ATT_SCALE = (QK_NOPE_DIM + QK_ROPE_DIM) ** -0.5
ATT_SCALE_LOG2E = ATT_SCALE * 1.4426950408889634

ADAM_LR = 0.001
ADAM_B1 = 0.9
ADAM_B2 = 0.999
ADAM_EPS = 1e-08
ADAM_WD = 0.01
ADAM_STEP = 10

N_CHIPS = 4
N_DEV = 8
LANES = 128
VMEM_LIMIT = 60 * 1024 * 1024

IN_SEGMENTS = ((384, 384), (256, 256), (32, HEAD_PAD), (512, 512), (512, 512), (512, 512), (2048, 2048))
SHARD_COLS = IN_TOTAL // N_CHIPS


def _shard_pieces():
    bounds, off = [], 0
    for w, _ in IN_SEGMENTS:
        bounds.append((off, off + w))
        off += w
    out = []
    for j in range(N_CHIPS):
        lo, hi = SHARD_COLS * j, SHARD_COLS * (j + 1)
        out.append([(i, max(lo, a) - a, min(hi, b) - a, max(lo, a) - lo)
                    for i, (a, b) in enumerate(bounds) if max(lo, a) < min(hi, b)])
    return out


SHARD_PIECES = _shard_pieces()

COMM_PARAMS = (
    ("w_in", SHARD_COLS, D_MODEL, 1, 512),
    ("w_uq", 96, 768, 0, 48),
    ("w_ukv", 64, 1024, 0, 32),
    ("w_branch_attn", 512, 256, 0, 256),
    ("w_branch_pool", 512, 256, 0, 256),
    ("w_out", 256, 1024, 0, 128),
)

SMALL_SHAPES = (
    ("norm_in", (1024,)),
    ("q_norm", (384,)),
    ("kv_norm", (256,)),
    ("pool_scale", (512,)),
    ("norm_final", (1024,)),
    ("pool_w", (4, 128, 128)),
)
SMALL_ELEMS = sum(int(np.prod(s)) for _, s in SMALL_SHAPES)
SMALL_ROWS = -(-SMALL_ELEMS // (LANES * 8)) * 8


def _dot(a, b):
    return jnp.dot(a, b, preferred_element_type=F32)


def _dot_nt(a, b):
    return lax.dot_general(a, b, (((1,), (1,)), ((), ())), preferred_element_type=F32)


def _dot_tn(a, b):
    return lax.dot_general(a, b, (((0,), (0,)), ((), ())), preferred_element_type=F32)


def _sigmoid(x):
    return 1.0 / (1.0 + jnp.exp(-x))


def _colsum(x):
    return jnp.sum(x, axis=0, keepdims=True)


def _rms_fwd(x, g):
    r = lax.rsqrt(jnp.mean(x * x, axis=-1, keepdims=True) + EPS)
    xhat = x * r
    return xhat * g, xhat, r


def _rms_bwd(dy, xhat, r, g):
    dxhat = dy * g
    return r * (dxhat - xhat * jnp.mean(dxhat * xhat, axis=-1, keepdims=True))


def _rope(v, c, sa, sb):
    return v * c + pltpu.roll(v, 112, 1) * sa + pltpu.roll(v, 16, 1) * sb


def _unrope(d, c, sa, sb):
    return d * c + pltpu.roll(d * sa, 16, 1) + pltpu.roll(d * sb, 112, 1)


def _row_spec(tm, n):
    return pl.BlockSpec((tm, n), lambda i: (i, 0))


def _full_spec(shape):
    nd = len(shape)
    return pl.BlockSpec(shape, lambda i: (0,) * nd)


def _tiles(s):
    t_att = 512 if s >= 2048 else 128
    t_row = 256 if s >= 1024 else 128
    return t_att, t_row


def _inproj_fwd(x, norm_in, w_in_t, tm):
    s = x.shape[0]

    def body(x_ref, g_ref, w_ref, hn_ref, *z_refs):
        hn, _, _ = _rms_fwd(x_ref[...], g_ref[...])
        hn = hn.astype(BF16)
        hn_ref[...] = hn
        for j, pieces in enumerate(SHARD_PIECES):
            zj = _dot_nt(hn, w_ref[j])
            for seg, lo, hi, col in pieces:
                z_refs[seg][:, lo:hi] = zj[:, col:col + hi - lo]
        for seg, (w, wide) in enumerate(IN_SEGMENTS):
            if wide > w:
                z_refs[seg][:, w:wide] = jnp.zeros((tm, wide - w), F32)

    out_shape = [jax.ShapeDtypeStruct((s, D_MODEL), BF16)]
    out_specs = [_row_spec(tm, D_MODEL)]
    for _, wide in IN_SEGMENTS:
        out_shape.append(jax.ShapeDtypeStruct((s, wide), F32))
        out_specs.append(_row_spec(tm, wide))
    return pl.pallas_call(
        body,
        name="inproj_fwd",
        grid=(s // tm,),
        in_specs=[_row_spec(tm, D_MODEL), _full_spec((1, D_MODEL)),
                  pl.BlockSpec((N_CHIPS, SHARD_COLS, D_MODEL), lambda i: (0, 0, 0), pipeline_mode=pl.Buffered(1))],
        out_specs=out_specs,
        out_shape=out_shape,
        compiler_params=pltpu.CompilerParams(dimension_semantics=("parallel",), vmem_limit_bytes=VMEM_LIMIT),
    )(x, norm_in, w_in_t)


def _qkv_fwd(zq, zkv, zkr, q_norm, kv_norm, wuq_p, wk_p, wv, rc, rsa, rsb, tm):
    s = zq.shape[0]
    hw = MLA_HEADS * HEAD_PAD

    def body(zq_ref, zkv_ref, zkr_ref, gq_ref, gkv_ref, wuq_ref, wk_ref, wv_ref, c_ref, sa_ref, sb_ref,
             q_ref, k_ref, v_ref, qt_ref, vt_ref):
        c, sa, sb = c_ref[...], sa_ref[...], sb_ref[...]
        cq, _, _ = _rms_fwd(zq_ref[...], gq_ref[...])
        qf = _dot(cq.astype(BF16), wuq_ref[...])
        ckv, _, _ = _rms_fwd(zkv_ref[...], gkv_ref[...])
        ckv = ckv.astype(BF16)
        kn = _dot(ckv, wk_ref[...])
        kr = _rope(pltpu.roll(zkr_ref[...], 64, 1), c, sa, sb)
        for h in range(MLA_HEADS):
            cols = slice(h * HEAD_PAD, (h + 1) * HEAD_PAD)
            qh = _rope(qf[:, cols], c, sa, sb)
            q_ref[:, cols] = qh.astype(BF16)
            qt_ref[cols, :] = qh.T.astype(BF16)
            k_ref[:, cols] = (kn[:, cols] + kr).astype(BF16)
        vf = _dot(ckv, wv_ref[...])
        v_ref[...] = vf.astype(BF16)
        vt_ref[...] = vf.T.astype(BF16)

    return pl.pallas_call(
        body,
        name="qkv_fwd",
        grid=(s // tm,),
        in_specs=[
            _row_spec(tm, Q_LORA_RANK), _row_spec(tm, KV_LORA_RANK), _row_spec(tm, HEAD_PAD),
            _full_spec((1, Q_LORA_RANK)), _full_spec((1, KV_LORA_RANK)),
            _full_spec((Q_LORA_RANK, hw)), _full_spec((KV_LORA_RANK, hw)), _full_spec((KV_LORA_RANK, MLA_WIDTH)),
            _row_spec(tm, HEAD_PAD), _row_spec(tm, HEAD_PAD), _row_spec(tm, HEAD_PAD),
        ],
        out_specs=[_row_spec(tm, hw), _row_spec(tm, hw), _row_spec(tm, MLA_WIDTH),
                   pl.BlockSpec((hw, tm), lambda i: (0, i)), pl.BlockSpec((MLA_WIDTH, tm), lambda i: (0, i))],
        out_shape=[jax.ShapeDtypeStruct((s, hw), BF16), jax.ShapeDtypeStruct((s, hw), BF16),
                   jax.ShapeDtypeStruct((s, MLA_WIDTH), BF16),
                   jax.ShapeDtypeStruct((hw, s), BF16), jax.ShapeDtypeStruct((MLA_WIDTH, s), BF16)],
        compiler_params=pltpu.CompilerParams(dimension_semantics=("parallel",), vmem_limit_bytes=VMEM_LIMIT),
    )(zq, zkv, zkr, q_norm, kv_norm, wuq_p, wk_p, wv, rc, rsa, rsb)


def _chunk_mask(t, keys_on_rows):
    rows = lax.broadcasted_iota(jnp.int32, (t, t), 0) // CHUNK
    cols = lax.broadcasted_iota(jnp.int32, (t, t), 1) // CHUNK
    return rows <= cols if keys_on_rows else cols <= rows


def _attn_fwd(q_t, k, v_t, late_shards, t):
    s = k.shape[0]
    pairs = MLA_HEADS // 2
    n_q = s // t
    gat = _Gather(COMM_PARAMS[3:])
    n_w = len(gat.params)

    def body(qt_ref, k_ref, k2_ref, vt_ref, *rest):
        w_in, (o_ref, lse_ref), w_out = rest[:n_w], rest[n_w:n_w + 2], rest[n_w + 2:2 * n_w + 2]
        gat.bind(w_in, w_out, rest[2 * n_w + 2:])
        i = pl.program_id(1)
        step_no = pl.program_id(0) * n_q + i
        pl.when(step_no == 0)(gat.start)
        pl.when(step_no == n_q)(gat.relay)
        mask = _chunk_mask(t, True)
        qcs = [slice(hh * HEAD_PAD, (hh + 1) * HEAD_PAD) for hh in range(2)]
        vcs = [slice(hh * V_HEAD_DIM, (hh + 1) * V_HEAD_DIM) for hh in range(2)]
        qts = [qt_ref[qc, :] for qc in qcs]

        def step(j, carry, masked):
            keys = pl.ds(pl.multiple_of(j * t, t), t)
            out = []
            for hh in range(2):
                m, l, acc = carry[hh]
                sc = _dot(k_ref[keys, qcs[hh]], qts[hh])
                if masked:
                    sc = jnp.where(mask, sc, -jnp.inf)
                m_new = jnp.maximum(m, jnp.max(sc, axis=0, keepdims=True))
                alpha = jnp.exp2((m - m_new) * ATT_SCALE_LOG2E)
                p = jnp.exp2((_dot(k2_ref[keys, qcs[hh]], qts[hh]) - m_new) * ATT_SCALE_LOG2E)
                if masked:
                    p = jnp.where(mask, p, 0.0)
                l = alpha * l + jnp.sum(p, axis=0, keepdims=True)
                acc = alpha * acc + _dot(vt_ref[vcs[hh], keys], p.astype(BF16))
                out.append((m_new, l, acc))
            return tuple(out)

        one = (jnp.full((1, t), -jnp.inf, F32), jnp.zeros((1, t), F32), jnp.zeros((V_HEAD_DIM, t), F32))
        carry = lax.fori_loop(0, i, functools.partial(step, masked=False), (one, one))
        carry = step(i, carry, True)
        o_ref[...] = jnp.concatenate([carry[hh][2] / carry[hh][1] for hh in range(2)], axis=0).T
        for hh in range(2):
            m, l, _ = carry[hh]
            lse_ref[:, qcs[hh]] = jnp.broadcast_to(m * ATT_SCALE_LOG2E + jnp.log2(l), (HEAD_PAD, t)).T
        pl.when(step_no == pairs * n_q - 1)(gat.finish)

    any_spec = pl.BlockSpec(memory_space=pl.ANY)
    out = pl.pallas_call(
        body,
        name="attn_fwd",
        grid=(pairs, n_q),
        in_specs=[
            pl.BlockSpec((2 * HEAD_PAD, t), lambda p, i: (p, i)),
            pl.BlockSpec((s, 2 * HEAD_PAD), lambda p, i: (0, p)),
            pl.BlockSpec((s, 2 * HEAD_PAD), lambda p, i: (0, p)),
            pl.BlockSpec((2 * V_HEAD_DIM, s), lambda p, i: (p, 0)),
        ] + [any_spec] * n_w,
        out_specs=[
            pl.BlockSpec((t, 2 * V_HEAD_DIM), lambda p, i: (i, p)),
            pl.BlockSpec((t, 2 * HEAD_PAD), lambda p, i: (i, p)),
        ] + [any_spec] * n_w,
        out_shape=[jax.ShapeDtypeStruct((s, MLA_WIDTH), F32), jax.ShapeDtypeStruct((s, MLA_HEADS * HEAD_PAD), F32)]
        + gat.out_shape,
        scratch_shapes=gat.scratch,
        compiler_params=pltpu.CompilerParams(dimension_semantics=("arbitrary", "arbitrary"),
                                             vmem_limit_bytes=VMEM_LIMIT),
    )(q_t, k, k, v_t, *late_shards)
    return out[0], out[1], out[2:]


def _mid(o, gattn, u, gpool, gmerge, x, target, pool_w, pool_scale, w_ba, w_bp, w_out, norm_final, tm):
    s = x.shape[0]
    n_tiles = s // tm
    halo_per_tile = tm // POOL_HALO

    def body(o_ref, ga_ref, u_ref, uh_ref, gp_ref, gm_ref, x_ref, t_ref, pw_ref, ps_ref, wba_ref, wbp_ref,
             wout_ref, gf_ref,
             do_ref, dl_ref, dga_ref, dgp_ref, dgm_ref, ddc_ref, dh_ref,
             loss_ref, dwout_out, dwba_out, dwbp_out, dpw_ref, dps_ref, dgf_ref,
             ubuf, dwout_ref, dwba_ref, dwbp_ref):
        i = pl.program_id(0)

        @pl.when(i == 0)
        def _():
            loss_ref[...] = jnp.zeros_like(loss_ref)
            dwout_ref[...] = jnp.zeros_like(dwout_ref)
            dwba_ref[...] = jnp.zeros_like(dwba_ref)
            dwbp_ref[...] = jnp.zeros_like(dwbp_ref)
            dpw_ref[...] = jnp.zeros_like(dpw_ref)
            dps_ref[...] = jnp.zeros_like(dps_ref)
            dgf_ref[...] = jnp.zeros_like(dgf_ref)

        o = o_ref[...]
        ga = ga_ref[...]
        sga = _sigmoid(ga)
        silu_a = ga * sga
        y_attn = (o * silu_a).astype(BF16)

        ubuf[0:POOL_HALO, :] = jnp.where(i > 0, uh_ref[...], 0.0)
        ubuf[POOL_HALO:, :] = u_ref[...]
        row = lax.broadcasted_iota(jnp.int32, (tm, POOL_GROUP_DIM), 0) + i * tm
        ps = ps_ref[...]
        gp = gp_ref[...]
        sgp = _sigmoid(gp)
        silu_p = gp * sgp
        d_bf, dm, inv_cnt = [], [], []
        for g, w in enumerate(POOL_WINDOWS):
            cols = slice(g * POOL_GROUP_DIM, (g + 1) * POOL_GROUP_DIM)
            wsum = ubuf[POOL_HALO:, cols]
            for kk in range(1, w):
                wsum = wsum + ubuf[POOL_HALO - kk:POOL_HALO - kk + tm, cols]
            inv = 1.0 / jnp.minimum(row + 1, w).astype(F32)
            dg = (wsum * inv - ubuf[POOL_HALO:, cols]).astype(BF16)
            d_bf.append(dg)
            inv_cnt.append(inv)
            dm.append(_dot(dg, pw_ref[g]))
        dm = jnp.concatenate(dm, axis=1)
        yp = dm * ps
        y_pool = (yp * silu_p).astype(BF16)

        a = jnp.concatenate([_dot(y_attn, wba_ref[j]) for j in range(N_CHIPS)], axis=1)
        p = jnp.concatenate([_dot(y_pool, wbp_ref[j]) for j in range(N_CHIPS)], axis=1)
        gate_a = _sigmoid(gm_ref[:, :D_MODEL])
        gate_p = _sigmoid(gm_ref[:, D_MODEL:])
        merged = (gate_a * a + gate_p * p).astype(BF16)
        h = x_ref[...] + _dot(merged, wout_ref[...])
        gf = gf_ref[...]
        y, xhat, r = _rms_fwd(h, gf)
        err = y - t_ref[...]
        e2 = err * err
        e2 = jnp.sum(e2.reshape(tm // 8, 8, D_MODEL), axis=0)
        acc = e2[:, 0:LANES]
        for cidx in range(1, D_MODEL // LANES):
            acc = acc + e2[:, cidx * LANES:(cidx + 1) * LANES]
        loss_ref[...] += acc

        dy = err * (1.0 / D_MODEL)
        dgf_ref[...] += _colsum(dy * xhat)
        dh = _rms_bwd(dy, xhat, r, gf)
        dh_ref[...] = dh
        dh_bf = dh.astype(BF16)
        dwout_ref[...] += _dot_tn(merged, dh_bf)
        dmerged = _dot_nt(dh_bf, wout_ref[...])
        da = (dmerged * gate_a).astype(BF16)
        dp = (dmerged * gate_p).astype(BF16)
        dgm_ref[:, :D_MODEL] = (dmerged * a * gate_a * (1.0 - gate_a)).astype(BF16)
        dgm_ref[:, D_MODEL:] = (dmerged * p * gate_p * (1.0 - gate_p)).astype(BF16)
        dy_attn = dy_pool = None
        for j in range(N_CHIPS):
            cols = slice(j * BRANCH_COLS, (j + 1) * BRANCH_COLS)
            dwba_ref[j] += _dot_tn(y_attn, da[:, cols])
            dwbp_ref[j] += _dot_tn(y_pool, dp[:, cols])
            pa = _dot_nt(da[:, cols], wba_ref[j])
            pp = _dot_nt(dp[:, cols], wbp_ref[j])
            dy_attn = pa if dy_attn is None else dy_attn + pa
            dy_pool = pp if dy_pool is None else dy_pool + pp

        do = dy_attn * silu_a
        do_ref[...] = do
        dga_ref[...] = (dy_attn * o * (sga * (1.0 + ga * (1.0 - sga)))).astype(BF16)
        doo = do * o
        for hd in range(MLA_HEADS):
            dl = jnp.sum(doo[:, hd * V_HEAD_DIM:(hd + 1) * V_HEAD_DIM], axis=1, keepdims=True)
            dl_ref[:, hd * HEAD_PAD:(hd + 1) * HEAD_PAD] = jnp.broadcast_to(dl, (tm, HEAD_PAD))

        dyp = dy_pool * silu_p
        dgp_ref[...] = (dy_pool * yp * (sgp * (1.0 + gp * (1.0 - sgp)))).astype(BF16)
        dps_ref[...] += _colsum(dyp * dm)
        dmm = (dyp * ps).astype(BF16)
        for g in range(len(POOL_WINDOWS)):
            cols = slice(g * POOL_GROUP_DIM, (g + 1) * POOL_GROUP_DIM)
            dpw_ref[g] += _dot_tn(d_bf[g], dmm[:, cols])
            ddc_ref[:, cols] = _dot_nt(dmm[:, cols], pw_ref[g]) * inv_cnt[g]

        @pl.when(i == n_tiles - 1)
        def _():
            dwout_out[...] = dwout_ref[...].astype(BF16)
            dwba_out[...] = dwba_ref[...].astype(BF16)
            dwbp_out[...] = dwbp_ref[...].astype(BF16)

    row_in = lambda n: _row_spec(tm, n)
    in_specs = [
        row_in(MLA_WIDTH), row_in(MLA_WIDTH), row_in(POOL_WIDTH),
        pl.BlockSpec((POOL_HALO, POOL_WIDTH), lambda i: (jnp.maximum(i * halo_per_tile - 1, 0), 0)),
        row_in(POOL_WIDTH), row_in(2 * D_MODEL), row_in(D_MODEL), row_in(D_MODEL),
        _full_spec((4, POOL_GROUP_DIM, POOL_GROUP_DIM)), _full_spec((1, POOL_WIDTH)),
        _full_spec((N_CHIPS, MLA_WIDTH, BRANCH_COLS)), _full_spec((N_CHIPS, POOL_WIDTH, BRANCH_COLS)),
        _full_spec((D_MODEL, D_MODEL)), _full_spec((1, D_MODEL)),
    ]
    out_shape = [
        jax.ShapeDtypeStruct((s, MLA_WIDTH), F32),
        jax.ShapeDtypeStruct((s, MLA_HEADS * HEAD_PAD), F32),
        jax.ShapeDtypeStruct((s, MLA_WIDTH), BF16),
        jax.ShapeDtypeStruct((s, POOL_WIDTH), BF16),
        jax.ShapeDtypeStruct((s, 2 * D_MODEL), BF16),
        jax.ShapeDtypeStruct((s, POOL_WIDTH), F32),
        jax.ShapeDtypeStruct((s, D_MODEL), F32),
        jax.ShapeDtypeStruct((8, LANES), F32),
        jax.ShapeDtypeStruct((D_MODEL, D_MODEL), BF16),
        jax.ShapeDtypeStruct((N_CHIPS, MLA_WIDTH, BRANCH_COLS), BF16),
        jax.ShapeDtypeStruct((N_CHIPS, POOL_WIDTH, BRANCH_COLS), BF16),
        jax.ShapeDtypeStruct((4, POOL_GROUP_DIM, POOL_GROUP_DIM), F32),
        jax.ShapeDtypeStruct((1, POOL_WIDTH), F32),
        jax.ShapeDtypeStruct((1, D_MODEL), F32),
    ]
    out_specs = [
        row_in(MLA_WIDTH), row_in(MLA_HEADS * HEAD_PAD), row_in(MLA_WIDTH), row_in(POOL_WIDTH),
        row_in(2 * D_MODEL), row_in(POOL_WIDTH), row_in(D_MODEL),
        _full_spec((8, LANES)), _full_spec((D_MODEL, D_MODEL)), _full_spec((N_CHIPS, MLA_WIDTH, BRANCH_COLS)),
        _full_spec((N_CHIPS, POOL_WIDTH, BRANCH_COLS)), _full_spec((4, POOL_GROUP_DIM, POOL_GROUP_DIM)),
        _full_spec((1, POOL_WIDTH)), _full_spec((1, D_MODEL)),
    ]
    return pl.pallas_call(
        body,
        name="mid",
        grid=(n_tiles,),
        in_specs=in_specs,
        out_specs=out_specs,
        out_shape=out_shape,
        scratch_shapes=[
            pltpu.VMEM((tm + POOL_HALO, POOL_WIDTH), F32),
            pltpu.VMEM((D_MODEL, D_MODEL), F32),
            pltpu.VMEM((N_CHIPS, MLA_WIDTH, BRANCH_COLS), F32),
            pltpu.VMEM((N_CHIPS, POOL_WIDTH, BRANCH_COLS), F32),
        ],
        compiler_params=pltpu.CompilerParams(dimension_semantics=("arbitrary",), vmem_limit_bytes=VMEM_LIMIT),
    )(o, gattn, u, u, gpool, gmerge, x, target, pool_w, pool_scale, w_ba, w_bp, w_out, norm_final)


def _attn_bwd(q, k, v, do, lse, delta, late_grads, t):
    s = q.shape[0]
    pairs = MLA_HEADS // 2
    n_q = s // t
    red = _Reduce(COMM_PARAMS[3:])
    n_w = len(red.params)

    def body(q_ref, do_ref, lse_ref, dl_ref, k_ref, v_ref, *rest):
        g_in, (dq_ref, dk_ref, dv_ref), g_out = rest[:n_w], rest[n_w:n_w + 3], rest[n_w + 3:2 * n_w + 3]
        red.bind(g_in, g_out, rest[2 * n_w + 3:])
        i = pl.program_id(1)
        step_no = pl.program_id(0) * n_q + i
        pl.when(step_no == 0)(red.start)
        pl.when(step_no == n_q)(red.exchange)

        @pl.when(i == 0)
        def _():
            dk_ref[...] = jnp.zeros_like(dk_ref)
            dv_ref[...] = jnp.zeros_like(dv_ref)

        mask = _chunk_mask(t, False)
        qcs = [slice(hh * HEAD_PAD, (hh + 1) * HEAD_PAD) for hh in range(2)]
        vcs = [slice(hh * V_HEAD_DIM, (hh + 1) * V_HEAD_DIM) for hh in range(2)]
        qhs = [q_ref[:, qc] for qc in qcs]
        dohs = [do_ref[:, vc].astype(BF16) for vc in vcs]
        lses = [lse_ref[:, hh * HEAD_PAD:hh * HEAD_PAD + 1] for hh in range(2)]
        dls = [dl_ref[:, hh * HEAD_PAD:hh * HEAD_PAD + 1] for hh in range(2)]

        def step(j, dqs, masked):
            keys = pl.ds(pl.multiple_of(j * t, t), t)
            out = []
            for hh in range(2):
                kj = k_ref[keys, qcs[hh]]
                vj = v_ref[keys, vcs[hh]]
                p = jnp.exp2(_dot_nt(qhs[hh], kj) * ATT_SCALE_LOG2E - lses[hh])
                if masked:
                    p = jnp.where(mask, p, 0.0)
                ds = (p * (_dot_nt(dohs[hh], vj) - dls[hh])).astype(BF16)
                dv_ref[keys, vcs[hh]] += _dot_tn(p.astype(BF16), dohs[hh])
                dk_ref[keys, qcs[hh]] += _dot_tn(ds, qhs[hh]) * ATT_SCALE
                out.append(dqs[hh] + _dot(ds, kj))
            return tuple(out)

        zero = jnp.zeros((t, HEAD_PAD), F32)
        dqs = lax.fori_loop(0, i, functools.partial(step, masked=False), (zero, zero))
        dqs = step(i, dqs, True)
        for hh in range(2):
            dq_ref[:, qcs[hh]] = dqs[hh] * ATT_SCALE
        pl.when(step_no == pairs * n_q - 1)(red.finish)

    hw = MLA_HEADS * HEAD_PAD
    any_spec = pl.BlockSpec(memory_space=pl.ANY)
    out = pl.pallas_call(
        body,
        name="attn_bwd",
        grid=(pairs, n_q),
        in_specs=[
            pl.BlockSpec((t, 2 * HEAD_PAD), lambda p, i: (i, p)),
            pl.BlockSpec((t, 2 * V_HEAD_DIM), lambda p, i: (i, p)),
            pl.BlockSpec((t, 2 * HEAD_PAD), lambda p, i: (i, p)),
            pl.BlockSpec((t, 2 * HEAD_PAD), lambda p, i: (i, p)),
            pl.BlockSpec((s, 2 * HEAD_PAD), lambda p, i: (0, p)),
            pl.BlockSpec((s, 2 * V_HEAD_DIM), lambda p, i: (0, p)),
        ] + [any_spec] * n_w,
        out_specs=[
            pl.BlockSpec((t, 2 * HEAD_PAD), lambda p, i: (i, p)),
            pl.BlockSpec((s, 2 * HEAD_PAD), lambda p, i: (0, p)),
            pl.BlockSpec((s, 2 * V_HEAD_DIM), lambda p, i: (0, p)),
        ] + [any_spec] * n_w,
        out_shape=[jax.ShapeDtypeStruct((s, hw), F32), jax.ShapeDtypeStruct((s, hw), F32),
                   jax.ShapeDtypeStruct((s, MLA_WIDTH), F32)] + red.out_shape,
        scratch_shapes=red.scratch,
        compiler_params=pltpu.CompilerParams(dimension_semantics=("arbitrary", "arbitrary"),
                                             vmem_limit_bytes=VMEM_LIMIT),
    )(q, do, lse, delta, k, v, *late_grads)
    return out[0], out[1], out[2], out[3:]


def _qkv_bwd(dq, dk, dv, zq, zkv, q_norm, kv_norm, wuq_p, wk_p, wv, rc, rsa, rsb, tm):
    s = zq.shape[0]
    hw = MLA_HEADS * HEAD_PAD

    def body(dq_ref, dk_ref, dv_ref, zq_ref, zkv_ref, gq_ref, gkv_ref, wuq_ref, wk_ref, wv_ref,
             c_ref, sa_ref, sb_ref,
             dzq_ref, dzkv_ref, dzkr_ref, dwuq_ref, dwk_ref, dwv_ref, dgq_ref, dgkv_ref):
        i = pl.program_id(0)

        @pl.when(i == 0)
        def _():
            dwuq_ref[...] = jnp.zeros_like(dwuq_ref)
            dwk_ref[...] = jnp.zeros_like(dwk_ref)
            dwv_ref[...] = jnp.zeros_like(dwv_ref)
            dgq_ref[...] = jnp.zeros_like(dgq_ref)
            dgkv_ref[...] = jnp.zeros_like(dgkv_ref)

        c, sa, sb = c_ref[...], sa_ref[...], sb_ref[...]
        gq, gkv = gq_ref[...], gkv_ref[...]

        cq, xq, rq = _rms_fwd(zq_ref[...], gq)
        dqp = jnp.concatenate(
            [_unrope(dq_ref[:, h * HEAD_PAD:(h + 1) * HEAD_PAD], c, sa, sb) for h in range(MLA_HEADS)],
            axis=1).astype(BF16)
        dwuq_ref[...] += _dot_tn(cq.astype(BF16), dqp)
        dcq = _dot_nt(dqp, wuq_ref[...])
        dgq_ref[...] += _colsum(dcq * xq)
        dzq_ref[...] = _rms_bwd(dcq, xq, rq, gq).astype(BF16)

        ckv, xkv, rkv = _rms_fwd(zkv_ref[...], gkv)
        ckv = ckv.astype(BF16)
        dkf = dk_ref[...]
        dk_bf = dkf.astype(BF16)
        dv_bf = dv_ref[...].astype(BF16)
        dwk_ref[...] += _dot_tn(ckv, dk_bf)
        dwv_ref[...] += _dot_tn(ckv, dv_bf)
        dckv = _dot_nt(dk_bf, wk_ref[...]) + _dot_nt(dv_bf, wv_ref[...])
        dgkv_ref[...] += _colsum(dckv * xkv)
        dzkv_ref[...] = _rms_bwd(dckv, xkv, rkv, gkv).astype(BF16)

        dkr = dkf[:, 0:HEAD_PAD]
        for h in range(1, MLA_HEADS):
            dkr = dkr + dkf[:, h * HEAD_PAD:(h + 1) * HEAD_PAD]
        dkr = pltpu.roll(_unrope(dkr, c, sa, sb), 64, 1)
        lane = lax.broadcasted_iota(jnp.int32, (tm, HEAD_PAD), 1)
        dzkr_ref[...] = jnp.where(lane < QK_ROPE_DIM, dkr, 0.0).astype(BF16)

    return pl.pallas_call(
        body,
        name="qkv_bwd",
        grid=(s // tm,),
        in_specs=[
            _row_spec(tm, hw), _row_spec(tm, hw), _row_spec(tm, MLA_WIDTH),
            _row_spec(tm, Q_LORA_RANK), _row_spec(tm, KV_LORA_RANK),
            _full_spec((1, Q_LORA_RANK)), _full_spec((1, KV_LORA_RANK)),
            _full_spec((Q_LORA_RANK, hw)), _full_spec((KV_LORA_RANK, hw)), _full_spec((KV_LORA_RANK, MLA_WIDTH)),
            _row_spec(tm, HEAD_PAD), _row_spec(tm, HEAD_PAD), _row_spec(tm, HEAD_PAD),
        ],
        out_specs=[
            _row_spec(tm, Q_LORA_RANK), _row_spec(tm, KV_LORA_RANK), _row_spec(tm, HEAD_PAD),
            _full_spec((Q_LORA_RANK, hw)), _full_spec((KV_LORA_RANK, hw)), _full_spec((KV_LORA_RANK, MLA_WIDTH)),
            _full_spec((1, Q_LORA_RANK)), _full_spec((1, KV_LORA_RANK)),
        ],
        out_shape=[
            jax.ShapeDtypeStruct((s, Q_LORA_RANK), BF16), jax.ShapeDtypeStruct((s, KV_LORA_RANK), BF16),
            jax.ShapeDtypeStruct((s, HEAD_PAD), BF16),
            jax.ShapeDtypeStruct((Q_LORA_RANK, hw), F32), jax.ShapeDtypeStruct((KV_LORA_RANK, hw), F32),
            jax.ShapeDtypeStruct((KV_LORA_RANK, MLA_WIDTH), F32),
            jax.ShapeDtypeStruct((1, Q_LORA_RANK), F32), jax.ShapeDtypeStruct((1, KV_LORA_RANK), F32),
        ],
        compiler_params=pltpu.CompilerParams(dimension_semantics=("arbitrary",), vmem_limit_bytes=VMEM_LIMIT),
    )(dq, dk, dv, zq, zkv, q_norm, kv_norm, wuq_p, wk_p, wv, rc, rsa, rsb)


def _inproj_bwd_x(dzq, dzkv, dzkr, dgattn, ddc, dgpool, dgmerge, x, dh, norm_in, w_in_t, tm):
    s = x.shape[0]
    n_tiles = s // tm
    halo_per_tile = tm // POOL_HALO
    n_halo = s // POOL_HALO
    u_seg = 4

    def body(dzq_ref, dzkv_ref, dzkr_ref, dga_ref, ddc_ref, ddn_ref, dgp_ref, dgm_ref, x_ref, dh_ref,
             g_ref, w_hbm, gx_ref, dgin_ref, dzs_ref, w_vmem, dbuf, sem):
        i = pl.program_id(0)

        @pl.when(i == 0)
        def _():
            cp = pltpu.make_async_copy(w_hbm, w_vmem, sem)
            cp.start()
            dgin_ref[...] = jnp.zeros_like(dgin_ref)
            cp.wait()

        dbuf[0:tm, :] = ddc_ref[...]
        dbuf[tm:, :] = jnp.where(i < n_tiles - 1, ddn_ref[...], 0.0)
        row = lax.broadcasted_iota(jnp.int32, (tm, POOL_GROUP_DIM), 0) + i * tm
        du = []
        for g, w in enumerate(POOL_WINDOWS):
            cols = slice(g * POOL_GROUP_DIM, (g + 1) * POOL_GROUP_DIM)
            fsum = dbuf[0:tm, cols]
            for kk in range(1, w):
                fsum = fsum + dbuf[kk:kk + tm, cols]
            du.append(fsum - dbuf[0:tm, cols] * jnp.minimum(row + 1, w).astype(F32))
        du = jnp.concatenate(du, axis=1).astype(BF16)

        dz = [dzq_ref[...], dzkv_ref[...], dzkr_ref[...], dga_ref[...], du, dgp_ref[...], dgm_ref[...]]
        dhn = None
        for j, pieces in enumerate(SHARD_PIECES):
            parts = [dz[seg][:, lo:hi] for seg, lo, hi, _ in pieces]
            dzj = parts[0] if len(parts) == 1 else jnp.concatenate(parts, axis=1)
            dzs_ref[j] = dzj.T
            part = _dot(dzj, w_vmem[j])
            dhn = part if dhn is None else dhn + part

        g = g_ref[...]
        _, xhat, r = _rms_fwd(x_ref[...], g)
        dgin_ref[...] += _colsum(dhn * xhat)
        gx_ref[...] = dh_ref[...] + _rms_bwd(dhn, xhat, r, g)

    any_spec = pl.BlockSpec(memory_space=pl.ANY)
    seg_w = [wide for _, wide in IN_SEGMENTS]
    return pl.pallas_call(
        body,
        name="inproj_bwd_x",
        grid=(n_tiles,),
        in_specs=[
            _row_spec(tm, seg_w[0]), _row_spec(tm, seg_w[1]), _row_spec(tm, seg_w[2]),
            _row_spec(tm, seg_w[3]), _row_spec(tm, seg_w[u_seg]),
            pl.BlockSpec((POOL_HALO, POOL_WIDTH), lambda i: (jnp.minimum((i + 1) * halo_per_tile, n_halo - 1), 0)),
            _row_spec(tm, seg_w[5]), _row_spec(tm, seg_w[6]),
            _row_spec(tm, D_MODEL), _row_spec(tm, D_MODEL),
            _full_spec((1, D_MODEL)), any_spec,
        ],
        out_specs=[_row_spec(tm, D_MODEL), _full_spec((1, D_MODEL)),
                   pl.BlockSpec((N_CHIPS, SHARD_COLS, tm), lambda i: (0, 0, i))],
        out_shape=[jax.ShapeDtypeStruct((s, D_MODEL), F32), jax.ShapeDtypeStruct((1, D_MODEL), F32),
                   jax.ShapeDtypeStruct((N_CHIPS, SHARD_COLS, s), BF16)],
        scratch_shapes=[
            pltpu.VMEM((N_CHIPS, SHARD_COLS, D_MODEL), BF16),
            pltpu.VMEM((tm + POOL_HALO, POOL_WIDTH), F32),
            pltpu.SemaphoreType.DMA,
        ],
        compiler_params=pltpu.CompilerParams(dimension_semantics=("arbitrary",), vmem_limit_bytes=VMEM_LIMIT),
    )(dzq, dzkv, dzkr, dgattn, ddc, ddc, dgpool, dgmerge, x, dh, norm_in, w_in_t)


def _inproj_bwd_w(order, dz_sh, hn, g_uq, g_ukv, gs, tm):
    s = hn.shape[0]
    n_tiles = s // tm
    mid = n_tiles // 2
    hc = D_MODEL // 2
    red = _Reduce(COMM_PARAMS[1:3])
    n_red = len(red.scratch)

    def body(order_ref, dz_ref, hn_ref, guq_hbm, gukv_hbm, gs_ref, gw_hbm, guq_out, gukv_out, gsum_ref,
             acc, pm_w, a_w, b_w, r_w, s_buf, w_send, w_recv, w_local, *red_scratch):
        ph, i = pl.program_id(0), pl.program_id(1)
        x, y, c = lax.axis_index("x"), lax.axis_index("y"), lax.axis_index("c")
        k = 2 * x + y
        me, sibling = (x, y, c), (x, y, 1 - c)
        chips = _other_chips(x, y)
        shard_of_phase = [2 * cx + cy for cx, cy in chips] + [k]
        copy = _remote_copier(w_send, w_recv)
        red.bind([guq_hbm, gukv_hbm], [guq_out, gukv_out], red_scratch)
        mine = pl.ds(pl.multiple_of(c * hc, hc), hc)
        theirs = pl.ds(pl.multiple_of((1 - c) * hc, hc), hc)
        flips = [(fx, fy, fc) for fx in (0, 1) for fy in (0, 1) for fc in (0, 1)][1:]

        def to_sibling(f):
            j = shard_of_phase[f]
            return copy(f, pm_w.at[j, 1 - c], a_w.at[j], sibling)

        def pair_sum(f):
            cx, cy = chips[f]
            return copy(4 + f, pm_w.at[shard_of_phase[f], c], b_w.at[f], (cx, cy, c))

        def small(f):
            fx, fy, fc = flips[f - 1]
            peer = (1 - x if fx else x, 1 - y if fy else y, 1 - c if fc else c)
            return copy(7 + f, gs_ref, s_buf.at[f], peer)

        def finished():
            return copy(7, r_w, gw_hbm.at[:, mine], sibling)

        @pl.when(jnp.logical_and(ph == 0, i == 0))
        def _():
            red.start()
            for f in range(1, N_DEV):
                small(f).start()
            s_buf[0] = gs_ref[...]

        part = _dot(dz_ref[0], hn_ref[...])

        @pl.when(i == 0)
        def _():
            acc[...] = part

        @pl.when(i > 0)
        def _():
            acc[...] += part

        for f in range(3):
            @pl.when(jnp.logical_and(ph == f + 1, i == mid))
            def _(f=f):
                j = shard_of_phase[f]
                copy(f, a_w.at[j], a_w.at[j], me).wait_recv()
                pm_w[j, c] = (pm_w[j, c].astype(F32) + a_w[j].astype(F32)).astype(BF16)
                pair_sum(f).start()
                if f == 0:
                    red.exchange()

        for f in range(4):
            @pl.when(jnp.logical_and(ph == f, i == n_tiles - 1))
            def _(f=f):
                j = shard_of_phase[f]
                pm_w[j, 0] = acc[:, :hc].astype(BF16)
                pm_w[j, 1] = acc[:, hc:].astype(BF16)
                to_sibling(f).start()
                if f < 3:
                    return
                copy(3, a_w.at[k], a_w.at[k], me).wait_recv()
                r_w[...] = pm_w[k, c].astype(F32) + a_w[k].astype(F32)
                for g in range(3):
                    copy(4 + g, b_w.at[g], b_w.at[g], me).wait_recv()
                    r_w[...] = r_w[...] + b_w[g].astype(F32)
                store = pltpu.make_async_copy(r_w, gw_hbm.at[:, mine], w_local)
                store.start()
                finished().start()
                red.finish()
                for g in range(1, N_DEV):
                    copy(7 + g, s_buf.at[g], s_buf.at[g], me).wait_recv()
                dev = 4 * x + 2 * y + c
                total = s_buf[dev]
                for d in range(1, N_DEV):
                    total = total + s_buf[jnp.bitwise_xor(dev, d)]
                gsum_ref[...] = total
                copy(7, gw_hbm.at[:, theirs], gw_hbm.at[:, theirs], me).wait_recv()
                store.wait()
                for g in range(4):
                    to_sibling(g).wait_send()
                for g in range(3):
                    pair_sum(g).wait_send()
                finished().wait_send()
                for g in range(1, N_DEV):
                    small(g).wait_send()

    any_spec = pl.BlockSpec(memory_space=pl.ANY)
    n_sem = 8 + N_DEV - 1
    grid_spec = pltpu.PrefetchScalarGridSpec(
        num_scalar_prefetch=1,
        grid=(N_CHIPS, n_tiles),
        in_specs=[
            pl.BlockSpec((1, SHARD_COLS, tm), lambda ph, i, order: (order[ph], 0, i)),
            pl.BlockSpec((tm, D_MODEL), lambda ph, i, order: (i, 0)),
            any_spec, any_spec,
            pl.BlockSpec((SMALL_ROWS, LANES), lambda ph, i, order: (0, 0)),
        ],
        out_specs=[any_spec, any_spec, any_spec, pl.BlockSpec((SMALL_ROWS, LANES), lambda ph, i, order: (0, 0))],
        scratch_shapes=[
            pltpu.VMEM((SHARD_COLS, D_MODEL), F32),
            pltpu.VMEM((N_CHIPS, 2, SHARD_COLS, hc), BF16),
            pltpu.VMEM((N_CHIPS, SHARD_COLS, hc), BF16),
            pltpu.VMEM((3, SHARD_COLS, hc), BF16),
            pltpu.VMEM((SHARD_COLS, hc), F32),
            pltpu.VMEM((N_DEV, SMALL_ROWS, LANES), F32),
            pltpu.SemaphoreType.DMA((n_sem,)), pltpu.SemaphoreType.DMA((n_sem,)), pltpu.SemaphoreType.DMA,
        ] + red.scratch,
    )
    out = pl.pallas_call(
        body,
        name="inproj_bwd_w",
        grid_spec=grid_spec,
        out_shape=[jax.ShapeDtypeStruct((SHARD_COLS, D_MODEL), F32)] + red.out_shape
        + [jax.ShapeDtypeStruct((SMALL_ROWS, LANES), F32)],
        compiler_params=pltpu.CompilerParams(dimension_semantics=("arbitrary", "arbitrary"),
                                             vmem_limit_bytes=VMEM_LIMIT),
    )(order, dz_sh, hn, g_uq, g_ukv, gs)
    return out[0], out[1], out[2], out[3]


def _other_chips(x, y):
    return ((1 - x, 1 - y), (1 - x, y), (x, 1 - y))


def _half(ref, axis, size, c, lead=()):
    window = pl.ds(pl.multiple_of(c * size, size), size)
    if axis == 0:
        return ref.at[(*lead, window, slice(None))]
    return ref.at[(*lead, slice(None), window)]


def _half_shape(rows, cols, axis, size):
    return (size, cols) if axis == 0 else (rows, size)


def _remote_copier(send_sems, recv_sems):
    def copy(sem, src, dst, to):
        return pltpu.make_async_remote_copy(src_ref=src, dst_ref=dst, send_sem=send_sems.at[sem],
                                            recv_sem=recv_sems.at[sem], device_id=to, device_id_type=MESH)
    return copy


class _Gather:
    def __init__(self, params):
        self.params = params
        n = len(params)
        self.scratch = [pltpu.SemaphoreType.DMA((6 * n,)), pltpu.SemaphoreType.DMA((6 * n,)),
                        pltpu.SemaphoreType.DMA((n,))]
        self.out_shape = [jax.ShapeDtypeStruct((N_CHIPS, r, cc), BF16) for _, r, cc, _, _ in params]

    def bind(self, ins, outs, scratch):
        self.ins, self.outs = ins, outs
        send_sems, recv_sems, self.local_sems = scratch
        self.copy = _remote_copier(send_sems, recv_sems)
        self.x, self.y, self.c = lax.axis_index("x"), lax.axis_index("y"), lax.axis_index("c")
        self.k = 2 * self.x + self.y
        self.chips = _other_chips(self.x, self.y)

    def _local(self, p):
        return pltpu.make_async_copy(self.ins[p], self.outs[p].at[self.k], self.local_sems.at[p])

    def _first(self, p, j):
        _, _, _, axis, size = self.params[p]
        cx, cy = self.chips[j]
        return self.copy(6 * p + j, _half(self.ins[p], axis, size, self.c),
                         _half(self.outs[p], axis, size, self.c, (self.k,)), (cx, cy, self.c))

    def _relay(self, p, j, half_of):
        _, _, _, axis, size = self.params[p]
        cx, cy = self.chips[j]
        block = _half(self.outs[p], axis, size, half_of, (2 * cx + cy,))
        return self.copy(6 * p + 3 + j, block, block, (self.x, self.y, 1 - self.c))

    def start(self):
        for p in range(len(self.params)):
            self._local(p).start()
            for j in range(3):
                self._first(p, j).start()

    def relay(self):
        for j in range(3):
            for p, (_, _, _, axis, size) in enumerate(self.params):
                cx, cy = self.chips[j]
                landed = _half(self.outs[p], axis, size, self.c, (2 * cx + cy,))
                self.copy(6 * p + j, landed, landed, (self.x, self.y, self.c)).wait_recv()
                self._relay(p, j, self.c).start()

    def finish(self):
        for j in range(3):
            for p in range(len(self.params)):
                self._relay(p, j, 1 - self.c).wait_recv()
        for p in range(len(self.params)):
            for j in range(3):
                self._first(p, j).wait_send()
                self._relay(p, j, self.c).wait_send()
            self._local(p).wait()


class _Reduce:
    def __init__(self, params):
        self.params = params
        n = len(params)
        halves = [_half_shape(r, cc, axis, size) for _, r, cc, axis, size in params]
        self.scratch = ([pltpu.VMEM((N_CHIPS, *h), BF16) for h in halves]
                        + [pltpu.VMEM((N_CHIPS, *h), BF16) for h in halves]
                        + [pltpu.VMEM((3, *h), BF16) for h in halves]
                        + [pltpu.VMEM(h, F32) for h in halves]
                        + [pltpu.SemaphoreType.DMA((5 * n,)), pltpu.SemaphoreType.DMA((5 * n,)),
                           pltpu.SemaphoreType.DMA((2 * n,))])
        self.out_shape = [jax.ShapeDtypeStruct((r, cc), F32) for _, r, cc, _, _ in params]

    def bind(self, g_in, g_out, scratch):
        n = len(self.params)
        self.g_in, self.g_out = g_in, g_out
        self.pm, self.a_buf = scratch[0:n], scratch[n:2 * n]
        self.b_buf, self.r_buf = scratch[2 * n:3 * n], scratch[3 * n:4 * n]
        send_sems, recv_sems, self.local_sems = scratch[4 * n:]
        self.copy = _remote_copier(send_sems, recv_sems)
        self.x, self.y, self.c = lax.axis_index("x"), lax.axis_index("y"), lax.axis_index("c")
        self.k = 2 * self.x + self.y
        self.chips = _other_chips(self.x, self.y)
        self.me = (self.x, self.y, self.c)
        self.sibling = (self.x, self.y, 1 - self.c)

    def _load(self, p):
        _, _, _, axis, size = self.params[p]
        return pltpu.make_async_copy(_half(self.g_in[p], axis, size, self.c, (slice(None),)), self.pm[p],
                                     self.local_sems.at[p])

    def _to_sibling(self, p):
        _, _, _, axis, size = self.params[p]
        return self.copy(5 * p, _half(self.g_in[p], axis, size, 1 - self.c, (slice(None),)), self.a_buf[p],
                         self.sibling)

    def _pair_sum(self, p, j):
        cx, cy = self.chips[j]
        return self.copy(5 * p + 1 + j, self.pm[p].at[2 * cx + cy], self.b_buf[p].at[j], (cx, cy, self.c))

    def _store(self, p):
        _, _, _, axis, size = self.params[p]
        n = len(self.params)
        return pltpu.make_async_copy(self.r_buf[p], _half(self.g_out[p], axis, size, self.c),
                                     self.local_sems.at[n + p])

    def _finished(self, p):
        _, _, _, axis, size = self.params[p]
        return self.copy(5 * p + 4, self.r_buf[p], _half(self.g_out[p], axis, size, self.c), self.sibling)

    def start(self):
        for p in range(len(self.params)):
            self._load(p).start()
            self._to_sibling(p).start()

    def exchange(self):
        for p in range(len(self.params)):
            self._load(p).wait()
            self.copy(5 * p, self.a_buf[p], self.a_buf[p], self.me).wait_recv()
            for j, (cx, cy) in enumerate(self.chips):
                kj = 2 * cx + cy
                self.pm[p][kj] = (self.pm[p][kj].astype(F32) + self.a_buf[p][kj].astype(F32)).astype(BF16)
                self._pair_sum(p, j).start()
            self.r_buf[p][...] = self.pm[p][self.k].astype(F32) + self.a_buf[p][self.k].astype(F32)

    def finish(self):
        for p, (_, _, _, axis, size) in enumerate(self.params):
            for j in range(3):
                self.copy(5 * p + 1 + j, self.b_buf[p].at[j], self.b_buf[p].at[j], self.me).wait_recv()
                self.r_buf[p][...] = self.r_buf[p][...] + self.b_buf[p][j].astype(F32)
            self._store(p).start()
            self._finished(p).start()
        for p, (_, _, _, axis, size) in enumerate(self.params):
            theirs = _half(self.g_out[p], axis, size, 1 - self.c)
            self.copy(5 * p + 4, theirs, theirs, self.me).wait_recv()
            self._store(p).wait()
            self._to_sibling(p).wait_send()
            for j in range(3):
                self._pair_sum(p, j).wait_send()
            self._finished(p).wait_send()


def _weight_gather(shards):
    gat = _Gather(COMM_PARAMS[:3])
    n = len(gat.params)

    def body(*refs):
        gat.bind(refs[:n], refs[n:2 * n], refs[2 * n:])
        gat.start()
        gat.relay()
        gat.finish()

    any_spec = pl.BlockSpec(memory_space=pl.ANY)
    return pl.pallas_call(
        body,
        name="weight_gather",
        in_specs=[any_spec] * n,
        out_specs=[any_spec] * n,
        out_shape=gat.out_shape,
        scratch_shapes=gat.scratch,
    )(*shards)


def _adamw_math(w, g, m, v):
    m = ADAM_B1 * m + (1.0 - ADAM_B1) * g
    v = ADAM_B2 * v + (1.0 - ADAM_B2) * (g * g)
    m_hat = m / (1.0 - ADAM_B1 ** ADAM_STEP)
    v_hat = v / (1.0 - ADAM_B2 ** ADAM_STEP)
    delta = -ADAM_LR * (m_hat / (jnp.sqrt(v_hat) + ADAM_EPS) + ADAM_WD * w)
    return delta, m, v


def _adamw_tiled(w, g, m, v, tm):
    rows, cols = w.shape

    def body(w_ref, g_ref, m_ref, v_ref, d_ref, nm_ref, nv_ref):
        d_ref[...], nm_ref[...], nv_ref[...] = _adamw_math(w_ref[...], g_ref[...], m_ref[...], v_ref[...])

    spec = _row_spec(tm, cols)
    return pl.pallas_call(
        body,
        name="adamw_w_in",
        grid=(rows // tm,),
        in_specs=[spec] * 4,
        out_specs=[spec] * 3,
        out_shape=[jax.ShapeDtypeStruct(w.shape, F32)] * 3,
        compiler_params=pltpu.CompilerParams(dimension_semantics=("parallel",), vmem_limit_bytes=VMEM_LIMIT),
    )(w, g, m, v)


def _adamw_many(ws, gs, ms, vs):
    n = len(ws)

    def body(*refs):
        ins, outs = refs[:4 * n], refs[4 * n:]
        for i in range(n):
            d, nm, nv = _adamw_math(ins[i][...], ins[n + i][...], ins[2 * n + i][...], ins[3 * n + i][...])
            outs[i][...] = d
            outs[n + i][...] = nm
            outs[2 * n + i][...] = nv

    vmem_spec = pl.BlockSpec(memory_space=pltpu.VMEM)
    shapes = [jax.ShapeDtypeStruct(w.shape, F32) for w in ws]
    out = pl.pallas_call(
        body,
        name="adamw_small",
        in_specs=[vmem_spec] * (4 * n),
        out_specs=[vmem_spec] * (3 * n),
        out_shape=shapes * 3,
        compiler_params=pltpu.CompilerParams(vmem_limit_bytes=VMEM_LIMIT),
    )(*ws, *gs, *ms, *vs)
    return out[:n], out[n:2 * n], out[2 * n:]


def _pack_rows(parts, rows, dtype):
    flat = jnp.concatenate([p.reshape(-1).astype(dtype) for p in parts])
    flat = jnp.concatenate([flat, jnp.zeros((rows * LANES - flat.shape[0],), dtype)])
    return flat.reshape(rows, LANES)


def _unpack_rows(packed, shapes):
    flat = packed.reshape(-1)
    out, off = [], 0
    for _, shp in shapes:
        n = int(np.prod(shp))
        out.append(flat[off:off + n].reshape(shp))
        off += n
    return out


def _rope_tables(s):
    half = QK_ROPE_DIM // 2
    inv_freq = ROPE_THETA ** (-jnp.arange(half, dtype=F32) / half)
    ang = jnp.arange(s, dtype=F32)[:, None] * inv_freq[None, :]
    cos, sin = jnp.cos(ang), jnp.sin(ang)
    z16 = jnp.zeros((s, half), F32)
    z32 = jnp.zeros((s, HEAD_PAD - QK_NOPE_DIM - QK_ROPE_DIM), F32)
    z64 = jnp.zeros((s, QK_NOPE_DIM), F32)
    rc = jnp.concatenate([jnp.ones((s, QK_NOPE_DIM), F32), cos, cos, z32], axis=1)
    rsa = jnp.concatenate([z64, -sin, z16, z32], axis=1)
    rsb = jnp.concatenate([z64, z16, sin, z32], axis=1)
    return rc, rsa, rsb


def kernel(x, norm_in, w_in, q_norm, w_uq, kv_norm, w_ukv, pool_w, pool_scale, w_branch_attn, w_branch_pool, w_out, norm_final, loss_target, m_norm_in, m_w_in, m_q_norm, m_w_uq, m_kv_norm, m_w_ukv, m_pool_w, m_pool_scale, m_w_branch_attn, m_w_branch_pool, m_w_out, m_norm_final, v_norm_in, v_w_in, v_q_norm, v_w_uq, v_kv_norm, v_w_ukv, v_pool_w, v_pool_scale, v_w_branch_attn, v_w_branch_pool, v_w_out, v_norm_final):
    s = x.shape[1]
    t_att, t_row = _tiles(s)
    x2 = x.reshape(s, D_MODEL)
    tgt = loss_target.reshape(s, D_MODEL)

    local = [w_in.T, w_uq.reshape(96, 768), w_ukv.reshape(64, 1024), w_branch_attn, w_branch_pool, w_out]
    local = [a.astype(BF16) for a in local]
    w_in_t, w_uq_all, w_ukv_all = _weight_gather(local[:3])
    w_uq_f = w_uq_all.reshape(Q_LORA_RANK, MLA_HEADS, QK_NOPE_DIM + QK_ROPE_DIM)
    w_ukv_f = w_ukv_all.reshape(KV_LORA_RANK, MLA_HEADS, QK_NOPE_DIM + V_HEAD_DIM)
    hw = MLA_HEADS * HEAD_PAD
    wuq_p = jnp.pad(w_uq_f, ((0, 0), (0, 0), (0, HEAD_PAD - QK_NOPE_DIM - QK_ROPE_DIM))).reshape(Q_LORA_RANK, hw)
    wk_p = jnp.pad(w_ukv_f[:, :, :QK_NOPE_DIM], ((0, 0), (0, 0), (0, HEAD_PAD - QK_NOPE_DIM))).reshape(KV_LORA_RANK, hw)
    wv = w_ukv_f[:, :, QK_NOPE_DIM:].reshape(KV_LORA_RANK, MLA_WIDTH)
    rc, rsa, rsb = _rope_tables(s)
    g_in = norm_in.reshape(1, -1)
    g_q = q_norm.reshape(1, -1)
    g_kv = kv_norm.reshape(1, -1)
    g_f = norm_final.reshape(1, -1)
    ps = pool_scale.reshape(1, -1)
    pw_bf = pool_w.astype(BF16)

    hn, zq, zkv, zkr, gattn, u, gpool, gmerge = _inproj_fwd(x2, g_in, w_in_t, t_row)
    q, k, v, q_t, v_t = _qkv_fwd(zq, zkv, zkr, g_q, g_kv, wuq_p, wk_p, wv, rc, rsa, rsb, t_row)
    o, lse, (w_ba_all, w_bp_all, w_out_all) = _attn_fwd(q_t, k, v_t, local[3:], t_att)
    w_out_f = w_out_all.reshape(D_MODEL, D_MODEL)

    (do, delta, dgattn, dgpool, dgmerge, ddc, dh, sq_err, d_w_out, d_w_ba, d_w_bp, d_pool_w, d_pool_scale,
     d_norm_final) = _mid(o, gattn, u, gpool, gmerge, x2, tgt, pw_bf, ps, w_ba_all, w_bp_all, w_out_f, g_f, t_row)

    late_grads = [d_w_ba, d_w_bp, d_w_out.reshape(N_CHIPS, 256, D_MODEL)]
    dq, dk, dv, (g_w_ba, g_w_bp, g_w_out) = _attn_bwd(q, k, v, do, lse, delta, late_grads, t_att)
    dzq, dzkv, dzkr, d_wuq_p, d_wk_p, d_wv, d_q_norm, d_kv_norm = _qkv_bwd(
        dq, dk, dv, zq, zkv, g_q, g_kv, wuq_p, wk_p, wv, rc, rsa, rsb, t_row)
    grad_x, d_norm_in, dz_sh = _inproj_bwd_x(dzq, dzkv, dzkr, dgattn, ddc, dgpool, dgmerge, x2, dh, g_in, w_in_t,
                                             t_row)

    d_w_uq = d_wuq_p.reshape(Q_LORA_RANK, MLA_HEADS, HEAD_PAD)[:, :, :QK_NOPE_DIM + QK_ROPE_DIM]
    d_w_ukv = jnp.concatenate([d_wk_p.reshape(KV_LORA_RANK, MLA_HEADS, HEAD_PAD)[:, :, :QK_NOPE_DIM],
                               d_wv.reshape(KV_LORA_RANK, MLA_HEADS, V_HEAD_DIM)], axis=2)
    small = dict(norm_in=d_norm_in, q_norm=d_q_norm, kv_norm=d_kv_norm, pool_scale=d_pool_scale,
                 norm_final=d_norm_final, pool_w=d_pool_w)
    gs = _pack_rows([small[n] for n, _ in SMALL_SHAPES], SMALL_ROWS, F32)
    cx, cy = lax.axis_index("x"), lax.axis_index("y")
    order = jnp.stack([2 * ox + oy for ox, oy in _other_chips(cx, cy)] + [2 * cx + cy]).astype(jnp.int32)
    g_w_in_t, g_w_uq, g_w_ukv, g_small = _inproj_bwd_w(
        order, dz_sh, hn, d_w_uq.reshape(N_CHIPS, 96, 768).astype(BF16),
        d_w_ukv.reshape(N_CHIPS, 64, 1024).astype(BF16), gs, 4 * t_row)
    g_norm_in, g_q_norm, g_kv_norm, g_pool_scale, g_norm_final, g_pool_w = _unpack_rows(g_small, SMALL_SHAPES)

    dl_w_in, nm_w_in, nv_w_in = (a.T for a in _adamw_tiled(w_in.T, g_w_in_t, m_w_in.T, v_w_in.T, 152))

    def two_d(a):
        return a.reshape(1, -1) if a.ndim == 1 else a.reshape(a.shape[0], -1)

    names = ["norm_in", "q_norm", "w_uq", "kv_norm", "w_ukv", "pool_w", "pool_scale", "w_branch_attn",
             "w_branch_pool", "w_out", "norm_final"]
    ws = dict(norm_in=norm_in, q_norm=q_norm, w_uq=w_uq, kv_norm=kv_norm, w_ukv=w_ukv, pool_w=pool_w,
              pool_scale=pool_scale, w_branch_attn=w_branch_attn, w_branch_pool=w_branch_pool, w_out=w_out,
              norm_final=norm_final)
    gsd = dict(norm_in=g_norm_in, q_norm=g_q_norm, w_uq=g_w_uq, kv_norm=g_kv_norm, w_ukv=g_w_ukv, pool_w=g_pool_w,
               pool_scale=g_pool_scale, w_branch_attn=g_w_ba, w_branch_pool=g_w_bp, w_out=g_w_out,
               norm_final=g_norm_final)
    msd = dict(norm_in=m_norm_in, q_norm=m_q_norm, w_uq=m_w_uq, kv_norm=m_kv_norm, w_ukv=m_w_ukv, pool_w=m_pool_w,
               pool_scale=m_pool_scale, w_branch_attn=m_w_branch_attn, w_branch_pool=m_w_branch_pool, w_out=m_w_out,
               norm_final=m_norm_final)
    vsd = dict(norm_in=v_norm_in, q_norm=v_q_norm, w_uq=v_w_uq, kv_norm=v_kv_norm, w_ukv=v_w_ukv, pool_w=v_pool_w,
               pool_scale=v_pool_scale, w_branch_attn=v_w_branch_attn, w_branch_pool=v_w_branch_pool, w_out=v_w_out,
               norm_final=v_norm_final)
    dls, nms, nvs = _adamw_many([two_d(ws[n]) for n in names], [two_d(gsd[n]) for n in names],
                                [two_d(msd[n]) for n in names], [two_d(vsd[n]) for n in names])

    grads = dict(gsd)
    grads["w_in"] = g_w_in_t.T
    delta_w = {n: d.reshape(ws[n].shape) for n, d in zip(names, dls)}
    new_m = {n: d.reshape(ws[n].shape) for n, d in zip(names, nms)}
    new_v = {n: d.reshape(ws[n].shape) for n, d in zip(names, nvs)}
    delta_w["w_in"], new_m["w_in"], new_v["w_in"] = dl_w_in, nm_w_in, nv_w_in
    ws["w_in"] = w_in

    order = ["norm_in", "w_in", "q_norm", "w_uq", "kv_norm", "w_ukv", "pool_w", "pool_scale", "w_branch_attn",
             "w_branch_pool", "w_out", "norm_final"]
    loss = lax.psum(0.5 * jnp.sum(sq_err) / D_MODEL, ("x", "y", "c"))
    return (loss, grad_x.reshape(x.shape),
            *[grads[n].reshape(ws[n].shape) for n in order],
            *[delta_w[n] for n in order], *[new_m[n] for n in order], *[new_v[n] for n in order])
```

```python
import functools

import jax
import jax.numpy as jnp
import numpy as np
from jax import lax
from jax.experimental import pallas as pl
from jax.experimental.pallas import tpu as pltpu

F32 = jnp.float32
BF16 = jnp.bfloat16
MESH = pl.DeviceIdType.MESH

D_MODEL = 1024
CHUNK = 64
MLA_HEADS = 8
QK_NOPE_DIM = 64
QK_ROPE_DIM = 32
V_HEAD_DIM = 64
Q_LORA_RANK = 384
KV_LORA_RANK = 256
MLA_WIDTH = MLA_HEADS * V_HEAD_DIM
ROPE_THETA = 10000.0
POOL_WINDOWS = (2, 4, 8, 16)
POOL_WIDTH = 512
POOL_GROUP_DIM = 128
BRANCH_COLS = D_MODEL // 4
POOL_HALO = 16
EPS = 1e-6
IN_TOTAL = 4256
HEAD_PAD = 128
ATT_SCALE = (QK_NOPE_DIM + QK_ROPE_DIM) ** -0.5
ATT_SCALE_LOG2E = ATT_SCALE * 1.4426950408889634

ADAM_LR = 0.001
ADAM_B1 = 0.9
ADAM_B2 = 0.999
ADAM_EPS = 1e-08
ADAM_WD = 0.01
ADAM_STEP = 10

N_CHIPS = 4
N_DEV = 8
LANES = 128
VMEM_LIMIT = 60 * 1024 * 1024

IN_SEGMENTS = ((384, 384), (256, 256), (32, HEAD_PAD), (512, 512), (512, 512), (512, 512), (2048, 2048))
SHARD_COLS = IN_TOTAL // N_CHIPS


def _shard_pieces():
    bounds, off = [], 0
    for w, _ in IN_SEGMENTS:
        bounds.append((off, off + w))
        off += w
    out = []
    for j in range(N_CHIPS):
        lo, hi = SHARD_COLS * j, SHARD_COLS * (j + 1)
        out.append([(i, max(lo, a) - a, min(hi, b) - a, max(lo, a) - lo)
                    for i, (a, b) in enumerate(bounds) if max(lo, a) < min(hi, b)])
    return out


SHARD_PIECES = _shard_pieces()

COMM_PARAMS = (
    ("w_in", SHARD_COLS, D_MODEL, 1, 512),
    ("w_uq", 96, 768, 0, 48),
    ("w_ukv", 64, 1024, 0, 32),
    ("w_branch_attn", 512, 256, 0, 256),
    ("w_branch_pool", 512, 256, 0, 256),
    ("w_out", 256, 1024, 0, 128),
)

SMALL_SHAPES = (
    ("norm_in", (1024,)),
    ("q_norm", (384,)),
    ("kv_norm", (256,)),
    ("pool_scale", (512,)),
    ("norm_final", (1024,)),
    ("pool_w", (4, 128, 128)),
    ("sq_err", (8, 128)),
)
SMALL_ELEMS = sum(int(np.prod(s)) for _, s in SMALL_SHAPES)
SMALL_ROWS = -(-SMALL_ELEMS // (LANES * 8)) * 8


def _dot(a, b):
    return jnp.dot(a, b, preferred_element_type=F32)


def _dot_nt(a, b):
    return lax.dot_general(a, b, (((1,), (1,)), ((), ())), preferred_element_type=F32)


def _dot_tn(a, b):
    return lax.dot_general(a, b, (((0,), (0,)), ((), ())), preferred_element_type=F32)


def _sigmoid(x):
    return 1.0 / (1.0 + jnp.exp(-x))


def _colsum(x):
    return jnp.sum(x, axis=0, keepdims=True)


def _rms_fwd(x, g):
    r = lax.rsqrt(jnp.mean(x * x, axis=-1, keepdims=True) + EPS)
    xhat = x * r
    return xhat * g, xhat, r


def _rms_bwd(dy, xhat, r, g):
    dxhat = dy * g
    return r * (dxhat - xhat * jnp.mean(dxhat * xhat, axis=-1, keepdims=True))


def _rope(v, c, sa, sb):
    return v * c + pltpu.roll(v, 112, 1) * sa + pltpu.roll(v, 16, 1) * sb


def _unrope(d, c, sa, sb):
    return d * c + pltpu.roll(d * sa, 16, 1) + pltpu.roll(d * sb, 112, 1)


def _row_spec(tm, n):
    return pl.BlockSpec((tm, n), lambda i: (i, 0))


def _full_spec(shape):
    nd = len(shape)
    return pl.BlockSpec(shape, lambda i: (0,) * nd)


def _tiles(s):
    t_att = 512 if s >= 2048 else 128
    t_row = 256 if s >= 1024 else 128
    return t_att, t_row


def _inproj_fwd(x, norm_in, w_in_t, tm):
    s = x.shape[0]

    def body(x_ref, g_ref, w_ref, hn_ref, *z_refs):
        hn, _, _ = _rms_fwd(x_ref[...], g_ref[...])
        hn = hn.astype(BF16)
        hn_ref[...] = hn
        for j, pieces in enumerate(SHARD_PIECES):
            zj = _dot_nt(hn, w_ref[j])
            for seg, lo, hi, col in pieces:
                z_refs[seg][:, lo:hi] = zj[:, col:col + hi - lo]
        for seg, (w, wide) in enumerate(IN_SEGMENTS):
            if wide > w:
                z_refs[seg][:, w:wide] = jnp.zeros((tm, wide - w), F32)

    out_shape = [jax.ShapeDtypeStruct((s, D_MODEL), BF16)]
    out_specs = [_row_spec(tm, D_MODEL)]
    for _, wide in IN_SEGMENTS:
        out_shape.append(jax.ShapeDtypeStruct((s, wide), F32))
        out_specs.append(_row_spec(tm, wide))
    return pl.pallas_call(
        body,
        name="inproj_fwd",
        grid=(s // tm,),
        in_specs=[_row_spec(tm, D_MODEL), _full_spec((1, D_MODEL)),
                  pl.BlockSpec((N_CHIPS, SHARD_COLS, D_MODEL), lambda i: (0, 0, 0), pipeline_mode=pl.Buffered(1))],
        out_specs=out_specs,
        out_shape=out_shape,
        compiler_params=pltpu.CompilerParams(dimension_semantics=("parallel",), vmem_limit_bytes=VMEM_LIMIT),
    )(x, norm_in, w_in_t)


def _qkv_fwd(zq, zkv, zkr, q_norm, kv_norm, wuq_p, wk_p, wv, rc, rsa, rsb, tm):
    s = zq.shape[0]
    hw = MLA_HEADS * HEAD_PAD

    def body(zq_ref, zkv_ref, zkr_ref, gq_ref, gkv_ref, wuq_ref, wk_ref, wv_ref, c_ref, sa_ref, sb_ref,
             q_ref, k_ref, v_ref, qt_ref, vt_ref):
        c, sa, sb = c_ref[...], sa_ref[...], sb_ref[...]
        cq, _, _ = _rms_fwd(zq_ref[...], gq_ref[...])
        qf = _dot(cq.astype(BF16), wuq_ref[...])
        ckv, _, _ = _rms_fwd(zkv_ref[...], gkv_ref[...])
        ckv = ckv.astype(BF16)
        kn = _dot(ckv, wk_ref[...])
        kr = _rope(pltpu.roll(zkr_ref[...], 64, 1), c, sa, sb)
        for h in range(MLA_HEADS):
            cols = slice(h * HEAD_PAD, (h + 1) * HEAD_PAD)
            qh = _rope(qf[:, cols], c, sa, sb)
            q_ref[:, cols] = qh.astype(BF16)
            qt_ref[cols, :] = qh.T.astype(BF16)
            k_ref[:, cols] = (kn[:, cols] + kr).astype(BF16)
        vf = _dot(ckv, wv_ref[...])
        v_ref[...] = vf.astype(BF16)
        vt_ref[...] = vf.T.astype(BF16)

    return pl.pallas_call(
        body,
        name="qkv_fwd",
        grid=(s // tm,),
        in_specs=[
            _row_spec(tm, Q_LORA_RANK), _row_spec(tm, KV_LORA_RANK), _row_spec(tm, HEAD_PAD),
            _full_spec((1, Q_LORA_RANK)), _full_spec((1, KV_LORA_RANK)),
            _full_spec((Q_LORA_RANK, hw)), _full_spec((KV_LORA_RANK, hw)), _full_spec((KV_LORA_RANK, MLA_WIDTH)),
            _row_spec(tm, HEAD_PAD), _row_spec(tm, HEAD_PAD), _row_spec(tm, HEAD_PAD),
        ],
        out_specs=[_row_spec(tm, hw), _row_spec(tm, hw), _row_spec(tm, MLA_WIDTH),
                   pl.BlockSpec((hw, tm), lambda i: (0, i)), pl.BlockSpec((MLA_WIDTH, tm), lambda i: (0, i))],
        out_shape=[jax.ShapeDtypeStruct((s, hw), BF16), jax.ShapeDtypeStruct((s, hw), BF16),
                   jax.ShapeDtypeStruct((s, MLA_WIDTH), BF16),
                   jax.ShapeDtypeStruct((hw, s), BF16), jax.ShapeDtypeStruct((MLA_WIDTH, s), BF16)],
        compiler_params=pltpu.CompilerParams(dimension_semantics=("parallel",), vmem_limit_bytes=VMEM_LIMIT),
    )(zq, zkv, zkr, q_norm, kv_norm, wuq_p, wk_p, wv, rc, rsa, rsb)


def _chunk_mask(t, keys_on_rows):
    rows = lax.broadcasted_iota(jnp.int32, (t, t), 0) // CHUNK
    cols = lax.broadcasted_iota(jnp.int32, (t, t), 1) // CHUNK
    return rows <= cols if keys_on_rows else cols <= rows


def _attn_fwd(q_t, k, v_t, late_shards, t):
    s = k.shape[0]
    pairs = MLA_HEADS // 2
    n_q = s // t
    gat = _Gather(COMM_PARAMS[3:])
    n_w = len(gat.params)

    def body(qt_ref, k_ref, k2_ref, vt_ref, *rest):
        w_in, (o_ref, lse_ref), w_out = rest[:n_w], rest[n_w:n_w + 2], rest[n_w + 2:2 * n_w + 2]
        gat.bind(w_in, w_out, rest[2 * n_w + 2:])
        i = pl.program_id(1)
        step_no = pl.program_id(0) * n_q + i
        pl.when(step_no == 0)(gat.start)
        pl.when(step_no == n_q)(gat.relay)
        mask = _chunk_mask(t, True)
        qcs = [slice(hh * HEAD_PAD, (hh + 1) * HEAD_PAD) for hh in range(2)]
        vcs = [slice(hh * V_HEAD_DIM, (hh + 1) * V_HEAD_DIM) for hh in range(2)]
        qts = [qt_ref[qc, :] for qc in qcs]

        def step(j, carry, masked):
            keys = pl.ds(pl.multiple_of(j * t, t), t)
            out = []
            for hh in range(2):
                m, l, acc = carry[hh]
                sc = _dot(k_ref[keys, qcs[hh]], qts[hh])
                if masked:
                    sc = jnp.where(mask, sc, -jnp.inf)
                m_new = jnp.maximum(m, jnp.max(sc, axis=0, keepdims=True))
                alpha = jnp.exp2((m - m_new) * ATT_SCALE_LOG2E)
                p = jnp.exp2((_dot(k2_ref[keys, qcs[hh]], qts[hh]) - m_new) * ATT_SCALE_LOG2E)
                if masked:
                    p = jnp.where(mask, p, 0.0)
                l = alpha * l + jnp.sum(p, axis=0, keepdims=True)
                acc = alpha * acc + _dot(vt_ref[vcs[hh], keys], p.astype(BF16))
                out.append((m_new, l, acc))
            return tuple(out)

        one = (jnp.full((1, t), -jnp.inf, F32), jnp.zeros((1, t), F32), jnp.zeros((V_HEAD_DIM, t), F32))
        carry = lax.fori_loop(0, i, functools.partial(step, masked=False), (one, one))
        carry = step(i, carry, True)
        o_ref[...] = jnp.concatenate([carry[hh][2] / carry[hh][1] for hh in range(2)], axis=0).T
        for hh in range(2):
            m, l, _ = carry[hh]
            lse_ref[:, qcs[hh]] = jnp.broadcast_to(m * ATT_SCALE_LOG2E + jnp.log2(l), (HEAD_PAD, t)).T
        pl.when(step_no == pairs * n_q - 1)(gat.finish)

    any_spec = pl.BlockSpec(memory_space=pl.ANY)
    out = pl.pallas_call(
        body,
        name="attn_fwd",
        grid=(pairs, n_q),
        in_specs=[
            pl.BlockSpec((2 * HEAD_PAD, t), lambda p, i: (p, i)),
            pl.BlockSpec((s, 2 * HEAD_PAD), lambda p, i: (0, p)),
            pl.BlockSpec((s, 2 * HEAD_PAD), lambda p, i: (0, p)),
            pl.BlockSpec((2 * V_HEAD_DIM, s), lambda p, i: (p, 0)),
        ] + [any_spec] * n_w,
        out_specs=[
            pl.BlockSpec((t, 2 * V_HEAD_DIM), lambda p, i: (i, p)),
            pl.BlockSpec((t, 2 * HEAD_PAD), lambda p, i: (i, p)),
        ] + [any_spec] * n_w,
        out_shape=[jax.ShapeDtypeStruct((s, MLA_WIDTH), F32), jax.ShapeDtypeStruct((s, MLA_HEADS * HEAD_PAD), F32)]
        + gat.out_shape,
        scratch_shapes=gat.scratch,
        compiler_params=pltpu.CompilerParams(dimension_semantics=("arbitrary", "arbitrary"),
                                             vmem_limit_bytes=VMEM_LIMIT),
    )(q_t, k, k, v_t, *late_shards)
    return out[0], out[1], out[2:]


def _mid(o, gattn, u, gpool, gmerge, x, target, pool_w, pool_scale, w_ba, w_bp, w_out, norm_final, tm):
    s = x.shape[0]
    n_tiles = s // tm
    halo_per_tile = tm // POOL_HALO

    def body(o_ref, ga_ref, u_ref, uh_ref, gp_ref, gm_ref, x_ref, t_ref, pw_ref, ps_ref, wba_ref, wbp_ref,
             wout_ref, gf_ref,
             do_ref, dl_ref, dga_ref, dgp_ref, dgm_ref, ddc_ref, dh_ref,
             loss_ref, dwout_out, dwba_out, dwbp_out, dpw_ref, dps_ref, dgf_ref,
             ubuf, dwout_ref, dwba_ref, dwbp_ref):
        i = pl.program_id(0)

        @pl.when(i == 0)
        def _():
            loss_ref[...] = jnp.zeros_like(loss_ref)
            dwout_ref[...] = jnp.zeros_like(dwout_ref)
            dwba_ref[...] = jnp.zeros_like(dwba_ref)
            dwbp_ref[...] = jnp.zeros_like(dwbp_ref)
            dpw_ref[...] = jnp.zeros_like(dpw_ref)
            dps_ref[...] = jnp.zeros_like(dps_ref)
            dgf_ref[...] = jnp.zeros_like(dgf_ref)

        o = o_ref[...]
        ga = ga_ref[...]
        sga = _sigmoid(ga)
        silu_a = ga * sga
        y_attn = (o * silu_a).astype(BF16)

        ubuf[0:POOL_HALO, :] = jnp.where(i > 0, uh_ref[...], 0.0)
        ubuf[POOL_HALO:, :] = u_ref[...]
        row = lax.broadcasted_iota(jnp.int32, (tm, POOL_GROUP_DIM), 0) + i * tm
        ps = ps_ref[...]
        gp = gp_ref[...]
        sgp = _sigmoid(gp)
        silu_p = gp * sgp
        d_bf, dm, inv_cnt = [], [], []
        for g, w in enumerate(POOL_WINDOWS):
            cols = slice(g * POOL_GROUP_DIM, (g + 1) * POOL_GROUP_DIM)
            wsum = ubuf[POOL_HALO:, cols]
            for kk in range(1, w):
                wsum = wsum + ubuf[POOL_HALO - kk:POOL_HALO - kk + tm, cols]
            inv = 1.0 / jnp.minimum(row + 1, w).astype(F32)
            dg = (wsum * inv - ubuf[POOL_HALO:, cols]).astype(BF16)
            d_bf.append(dg)
            inv_cnt.append(inv)
            dm.append(_dot(dg, pw_ref[g]))
        dm = jnp.concatenate(dm, axis=1)
        yp = dm * ps
        y_pool = (yp * silu_p).astype(BF16)

        a = jnp.concatenate([_dot(y_attn, wba_ref[j]) for j in range(N_CHIPS)], axis=1)
        p = jnp.concatenate([_dot(y_pool, wbp_ref[j]) for j in range(N_CHIPS)], axis=1)
        gate_a = _sigmoid(gm_ref[:, :D_MODEL])
        gate_p = _sigmoid(gm_ref[:, D_MODEL:])
        merged = (gate_a * a + gate_p * p).astype(BF16)
        h = x_ref[...] + _dot(merged, wout_ref[...])
        gf = gf_ref[...]
        y, xhat, r = _rms_fwd(h, gf)
        err = y - t_ref[...]
        e2 = err * err
        e2 = jnp.sum(e2.reshape(tm // 8, 8, D_MODEL), axis=0)
        acc = e2[:, 0:LANES]
        for cidx in range(1, D_MODEL // LANES):
            acc = acc + e2[:, cidx * LANES:(cidx + 1) * LANES]
        loss_ref[...] += acc

        dy = err * (1.0 / D_MODEL)
        dgf_ref[...] += _colsum(dy * xhat)
        dh = _rms_bwd(dy, xhat, r, gf)
        dh_ref[...] = dh
        dh_bf = dh.astype(BF16)
        dwout_ref[...] += _dot_tn(merged, dh_bf)
        dmerged = _dot_nt(dh_bf, wout_ref[...])
        da = (dmerged * gate_a).astype(BF16)
        dp = (dmerged * gate_p).astype(BF16)
        dgm_ref[:, :D_MODEL] = (dmerged * a * gate_a * (1.0 - gate_a)).astype(BF16)
        dgm_ref[:, D_MODEL:] = (dmerged * p * gate_p * (1.0 - gate_p)).astype(BF16)
        dy_attn = dy_pool = None
        for j in range(N_CHIPS):
            cols = slice(j * BRANCH_COLS, (j + 1) * BRANCH_COLS)
            dwba_ref[j] += _dot_tn(y_attn, da[:, cols])
            dwbp_ref[j] += _dot_tn(y_pool, dp[:, cols])
            pa = _dot_nt(da[:, cols], wba_ref[j])
            pp = _dot_nt(dp[:, cols], wbp_ref[j])
            dy_attn = pa if dy_attn is None else dy_attn + pa
            dy_pool = pp if dy_pool is None else dy_pool + pp

        do = dy_attn * silu_a
        do_ref[...] = do
        dga_ref[...] = (dy_attn * o * (sga * (1.0 + ga * (1.0 - sga)))).astype(BF16)
        doo = do * o
        for hd in range(MLA_HEADS):
            dl = jnp.sum(doo[:, hd * V_HEAD_DIM:(hd + 1) * V_HEAD_DIM], axis=1, keepdims=True)
            dl_ref[:, hd * HEAD_PAD:(hd + 1) * HEAD_PAD] = jnp.broadcast_to(dl, (tm, HEAD_PAD))

        dyp = dy_pool * silu_p
        dgp_ref[...] = (dy_pool * yp * (sgp * (1.0 + gp * (1.0 - sgp)))).astype(BF16)
        dps_ref[...] += _colsum(dyp * dm)
        dmm = (dyp * ps).astype(BF16)
        for g in range(len(POOL_WINDOWS)):
            cols = slice(g * POOL_GROUP_DIM, (g + 1) * POOL_GROUP_DIM)
            dpw_ref[g] += _dot_tn(d_bf[g], dmm[:, cols])
            ddc_ref[:, cols] = _dot_nt(dmm[:, cols], pw_ref[g]) * inv_cnt[g]

        @pl.when(i == n_tiles - 1)
        def _():
            dwout_out[...] = dwout_ref[...].astype(BF16)
            dwba_out[...] = dwba_ref[...].astype(BF16)
            dwbp_out[...] = dwbp_ref[...].astype(BF16)

    row_in = lambda n: _row_spec(tm, n)
    in_specs = [
        row_in(MLA_WIDTH), row_in(MLA_WIDTH), row_in(POOL_WIDTH),
        pl.BlockSpec((POOL_HALO, POOL_WIDTH), lambda i: (jnp.maximum(i * halo_per_tile - 1, 0), 0)),
        row_in(POOL_WIDTH), row_in(2 * D_MODEL), row_in(D_MODEL), row_in(D_MODEL),
        _full_spec((4, POOL_GROUP_DIM, POOL_GROUP_DIM)), _full_spec((1, POOL_WIDTH)),
        _full_spec((N_CHIPS, MLA_WIDTH, BRANCH_COLS)), _full_spec((N_CHIPS, POOL_WIDTH, BRANCH_COLS)),
        _full_spec((D_MODEL, D_MODEL)), _full_spec((1, D_MODEL)),
    ]
    out_shape = [
        jax.ShapeDtypeStruct((s, MLA_WIDTH), F32),
        jax.ShapeDtypeStruct((s, MLA_HEADS * HEAD_PAD), F32),
        jax.ShapeDtypeStruct((s, MLA_WIDTH), BF16),
        jax.ShapeDtypeStruct((s, POOL_WIDTH), BF16),
        jax.ShapeDtypeStruct((s, 2 * D_MODEL), BF16),
        jax.ShapeDtypeStruct((s, POOL_WIDTH), F32),
        jax.ShapeDtypeStruct((s, D_MODEL), F32),
        jax.ShapeDtypeStruct((8, LANES), F32),
        jax.ShapeDtypeStruct((D_MODEL, D_MODEL), BF16),
        jax.ShapeDtypeStruct((N_CHIPS, MLA_WIDTH, BRANCH_COLS), BF16),
        jax.ShapeDtypeStruct((N_CHIPS, POOL_WIDTH, BRANCH_COLS), BF16),
        jax.ShapeDtypeStruct((4, POOL_GROUP_DIM, POOL_GROUP_DIM), F32),
        jax.ShapeDtypeStruct((1, POOL_WIDTH), F32),
        jax.ShapeDtypeStruct((1, D_MODEL), F32),
    ]
    out_specs = [
        row_in(MLA_WIDTH), row_in(MLA_HEADS * HEAD_PAD), row_in(MLA_WIDTH), row_in(POOL_WIDTH),
        row_in(2 * D_MODEL), row_in(POOL_WIDTH), row_in(D_MODEL),
        _full_spec((8, LANES)), _full_spec((D_MODEL, D_MODEL)), _full_spec((N_CHIPS, MLA_WIDTH, BRANCH_COLS)),
        _full_spec((N_CHIPS, POOL_WIDTH, BRANCH_COLS)), _full_spec((4, POOL_GROUP_DIM, POOL_GROUP_DIM)),
        _full_spec((1, POOL_WIDTH)), _full_spec((1, D_MODEL)),
    ]
    return pl.pallas_call(
        body,
        name="mid",
        grid=(n_tiles,),
        in_specs=in_specs,
        out_specs=out_specs,
        out_shape=out_shape,
        scratch_shapes=[
            pltpu.VMEM((tm + POOL_HALO, POOL_WIDTH), F32),
            pltpu.VMEM((D_MODEL, D_MODEL), F32),
            pltpu.VMEM((N_CHIPS, MLA_WIDTH, BRANCH_COLS), F32),
            pltpu.VMEM((N_CHIPS, POOL_WIDTH, BRANCH_COLS), F32),
        ],
        compiler_params=pltpu.CompilerParams(dimension_semantics=("arbitrary",), vmem_limit_bytes=VMEM_LIMIT),
    )(o, gattn, u, u, gpool, gmerge, x, target, pool_w, pool_scale, w_ba, w_bp, w_out, norm_final)


def _attn_bwd(q, k, v, do, lse, delta, late_grads, t):
    s = q.shape[0]
    pairs = MLA_HEADS // 2
    n_q = s // t
    red = _Reduce(COMM_PARAMS[3:])
    n_w = len(red.params)

    def body(q_ref, do_ref, lse_ref, dl_ref, k_ref, v_ref, *rest):
        g_in, (dq_ref, dk_ref, dv_ref), g_out = rest[:n_w], rest[n_w:n_w + 3], rest[n_w + 3:2 * n_w + 3]
        red.bind(g_in, g_out, rest[2 * n_w + 3:])
        i = pl.program_id(1)
        step_no = pl.program_id(0) * n_q + i
        pl.when(step_no == 0)(red.start)
        pl.when(step_no == n_q)(red.exchange)

        @pl.when(i == 0)
        def _():
            dk_ref[...] = jnp.zeros_like(dk_ref)
            dv_ref[...] = jnp.zeros_like(dv_ref)

        mask = _chunk_mask(t, False)
        qcs = [slice(hh * HEAD_PAD, (hh + 1) * HEAD_PAD) for hh in range(2)]
        vcs = [slice(hh * V_HEAD_DIM, (hh + 1) * V_HEAD_DIM) for hh in range(2)]
        qhs = [q_ref[:, qc] for qc in qcs]
        dohs = [do_ref[:, vc].astype(BF16) for vc in vcs]
        lses = [lse_ref[:, hh * HEAD_PAD:hh * HEAD_PAD + 1] for hh in range(2)]
        dls = [dl_ref[:, hh * HEAD_PAD:hh * HEAD_PAD + 1] for hh in range(2)]

        def step(j, dqs, masked):
            keys = pl.ds(pl.multiple_of(j * t, t), t)
            out = []
            for hh in range(2):
                kj = k_ref[keys, qcs[hh]]
                vj = v_ref[keys, vcs[hh]]
                p = jnp.exp2(_dot_nt(qhs[hh], kj) * ATT_SCALE_LOG2E - lses[hh])
                if masked:
                    p = jnp.where(mask, p, 0.0)
                ds = (p * (_dot_nt(dohs[hh], vj) - dls[hh])).astype(BF16)
                dv_ref[keys, vcs[hh]] += _dot_tn(p.astype(BF16), dohs[hh])
                dk_ref[keys, qcs[hh]] += _dot_tn(ds, qhs[hh]) * ATT_SCALE
                out.append(dqs[hh] + _dot(ds, kj))
            return tuple(out)

        zero = jnp.zeros((t, HEAD_PAD), F32)
        dqs = lax.fori_loop(0, i, functools.partial(step, masked=False), (zero, zero))
        dqs = step(i, dqs, True)
        for hh in range(2):
            dq_ref[:, qcs[hh]] = dqs[hh] * ATT_SCALE
        pl.when(step_no == pairs * n_q - 1)(red.finish)

    hw = MLA_HEADS * HEAD_PAD
    any_spec = pl.BlockSpec(memory_space=pl.ANY)
    out = pl.pallas_call(
        body,
        name="attn_bwd",
        grid=(pairs, n_q),
        in_specs=[
            pl.BlockSpec((t, 2 * HEAD_PAD), lambda p, i: (i, p)),
            pl.BlockSpec((t, 2 * V_HEAD_DIM), lambda p, i: (i, p)),
            pl.BlockSpec((t, 2 * HEAD_PAD), lambda p, i: (i, p)),
            pl.BlockSpec((t, 2 * HEAD_PAD), lambda p, i: (i, p)),
            pl.BlockSpec((s, 2 * HEAD_PAD), lambda p, i: (0, p)),
            pl.BlockSpec((s, 2 * V_HEAD_DIM), lambda p, i: (0, p)),
        ] + [any_spec] * n_w,
        out_specs=[
            pl.BlockSpec((t, 2 * HEAD_PAD), lambda p, i: (i, p)),
            pl.BlockSpec((s, 2 * HEAD_PAD), lambda p, i: (0, p)),
            pl.BlockSpec((s, 2 * V_HEAD_DIM), lambda p, i: (0, p)),
        ] + [any_spec] * n_w,
        out_shape=[jax.ShapeDtypeStruct((s, hw), F32), jax.ShapeDtypeStruct((s, hw), F32),
                   jax.ShapeDtypeStruct((s, MLA_WIDTH), F32)] + red.out_shape,
        scratch_shapes=red.scratch,
        compiler_params=pltpu.CompilerParams(dimension_semantics=("arbitrary", "arbitrary"),
                                             vmem_limit_bytes=VMEM_LIMIT),
    )(q, do, lse, delta, k, v, *late_grads)
    return out[0], out[1], out[2], out[3:]


def _qkv_bwd(dq, dk, dv, zq, zkv, q_norm, kv_norm, wuq_p, wk_p, wv, rc, rsa, rsb, tm):
    s = zq.shape[0]
    hw = MLA_HEADS * HEAD_PAD

    def body(dq_ref, dk_ref, dv_ref, zq_ref, zkv_ref, gq_ref, gkv_ref, wuq_ref, wk_ref, wv_ref,
             c_ref, sa_ref, sb_ref,
             dzq_ref, dzkv_ref, dzkr_ref, dwuq_ref, dwk_ref, dwv_ref, dgq_ref, dgkv_ref):
        i = pl.program_id(0)

        @pl.when(i == 0)
        def _():
            dwuq_ref[...] = jnp.zeros_like(dwuq_ref)
            dwk_ref[...] = jnp.zeros_like(dwk_ref)
            dwv_ref[...] = jnp.zeros_like(dwv_ref)
            dgq_ref[...] = jnp.zeros_like(dgq_ref)
            dgkv_ref[...] = jnp.zeros_like(dgkv_ref)

        c, sa, sb = c_ref[...], sa_ref[...], sb_ref[...]
        gq, gkv = gq_ref[...], gkv_ref[...]

        cq, xq, rq = _rms_fwd(zq_ref[...], gq)
        dqp = jnp.concatenate(
            [_unrope(dq_ref[:, h * HEAD_PAD:(h + 1) * HEAD_PAD], c, sa, sb) for h in range(MLA_HEADS)],
            axis=1).astype(BF16)
        dwuq_ref[...] += _dot_tn(cq.astype(BF16), dqp)
        dcq = _dot_nt(dqp, wuq_ref[...])
        dgq_ref[...] += _colsum(dcq * xq)
        dzq_ref[...] = _rms_bwd(dcq, xq, rq, gq).astype(BF16)

        ckv, xkv, rkv = _rms_fwd(zkv_ref[...], gkv)
        ckv = ckv.astype(BF16)
        dkf = dk_ref[...]
        dk_bf = dkf.astype(BF16)
        dv_bf = dv_ref[...].astype(BF16)
        dwk_ref[...] += _dot_tn(ckv, dk_bf)
        dwv_ref[...] += _dot_tn(ckv, dv_bf)
        dckv = _dot_nt(dk_bf, wk_ref[...]) + _dot_nt(dv_bf, wv_ref[...])
        dgkv_ref[...] += _colsum(dckv * xkv)
        dzkv_ref[...] = _rms_bwd(dckv, xkv, rkv, gkv).astype(BF16)

        dkr = dkf[:, 0:HEAD_PAD]
        for h in range(1, MLA_HEADS):
            dkr = dkr + dkf[:, h * HEAD_PAD:(h + 1) * HEAD_PAD]
        dkr = pltpu.roll(_unrope(dkr, c, sa, sb), 64, 1)
        lane = lax.broadcasted_iota(jnp.int32, (tm, HEAD_PAD), 1)
        dzkr_ref[...] = jnp.where(lane < QK_ROPE_DIM, dkr, 0.0).astype(BF16)

    return pl.pallas_call(
        body,
        name="qkv_bwd",
        grid=(s // tm,),
        in_specs=[
            _row_spec(tm, hw), _row_spec(tm, hw), _row_spec(tm, MLA_WIDTH),
            _row_spec(tm, Q_LORA_RANK), _row_spec(tm, KV_LORA_RANK),
            _full_spec((1, Q_LORA_RANK)), _full_spec((1, KV_LORA_RANK)),
            _full_spec((Q_LORA_RANK, hw)), _full_spec((KV_LORA_RANK, hw)), _full_spec((KV_LORA_RANK, MLA_WIDTH)),
            _row_spec(tm, HEAD_PAD), _row_spec(tm, HEAD_PAD), _row_spec(tm, HEAD_PAD),
        ],
        out_specs=[
            _row_spec(tm, Q_LORA_RANK), _row_spec(tm, KV_LORA_RANK), _row_spec(tm, HEAD_PAD),
            _full_spec((Q_LORA_RANK, hw)), _full_spec((KV_LORA_RANK, hw)), _full_spec((KV_LORA_RANK, MLA_WIDTH)),
            _full_spec((1, Q_LORA_RANK)), _full_spec((1, KV_LORA_RANK)),
        ],
        out_shape=[
            jax.ShapeDtypeStruct((s, Q_LORA_RANK), BF16), jax.ShapeDtypeStruct((s, KV_LORA_RANK), BF16),
            jax.ShapeDtypeStruct((s, HEAD_PAD), BF16),
            jax.ShapeDtypeStruct((Q_LORA_RANK, hw), F32), jax.ShapeDtypeStruct((KV_LORA_RANK, hw), F32),
            jax.ShapeDtypeStruct((KV_LORA_RANK, MLA_WIDTH), F32),
            jax.ShapeDtypeStruct((1, Q_LORA_RANK), F32), jax.ShapeDtypeStruct((1, KV_LORA_RANK), F32),
        ],
        compiler_params=pltpu.CompilerParams(dimension_semantics=("arbitrary",), vmem_limit_bytes=VMEM_LIMIT),
    )(dq, dk, dv, zq, zkv, q_norm, kv_norm, wuq_p, wk_p, wv, rc, rsa, rsb)


def _inproj_bwd_x(dzq, dzkv, dzkr, dgattn, ddc, dgpool, dgmerge, x, dh, norm_in, w_in_t, tm):
    s = x.shape[0]
    n_tiles = s // tm
    halo_per_tile = tm // POOL_HALO
    n_halo = s // POOL_HALO
    u_seg = 4

    def body(dzq_ref, dzkv_ref, dzkr_ref, dga_ref, ddc_ref, ddn_ref, dgp_ref, dgm_ref, x_ref, dh_ref,
             g_ref, w_hbm, gx_ref, dgin_ref, dzs_ref, w_vmem, dbuf, sem):
        i = pl.program_id(0)

        @pl.when(i == 0)
        def _():
            cp = pltpu.make_async_copy(w_hbm, w_vmem, sem)
            cp.start()
            dgin_ref[...] = jnp.zeros_like(dgin_ref)
            cp.wait()

        dbuf[0:tm, :] = ddc_ref[...]
        dbuf[tm:, :] = jnp.where(i < n_tiles - 1, ddn_ref[...], 0.0)
        row = lax.broadcasted_iota(jnp.int32, (tm, POOL_GROUP_DIM), 0) + i * tm
        du = []
        for g, w in enumerate(POOL_WINDOWS):
            cols = slice(g * POOL_GROUP_DIM, (g + 1) * POOL_GROUP_DIM)
            fsum = dbuf[0:tm, cols]
            for kk in range(1, w):
                fsum = fsum + dbuf[kk:kk + tm, cols]
            du.append(fsum - dbuf[0:tm, cols] * jnp.minimum(row + 1, w).astype(F32))
        du = jnp.concatenate(du, axis=1).astype(BF16)

        dz = [dzq_ref[...], dzkv_ref[...], dzkr_ref[...], dga_ref[...], du, dgp_ref[...], dgm_ref[...]]
        dhn = None
        for j, pieces in enumerate(SHARD_PIECES):
            parts = [dz[seg][:, lo:hi] for seg, lo, hi, _ in pieces]
            dzj = parts[0] if len(parts) == 1 else jnp.concatenate(parts, axis=1)
            dzs_ref[j] = dzj.T
            part = _dot(dzj, w_vmem[j])
            dhn = part if dhn is None else dhn + part

        g = g_ref[...]
        _, xhat, r = _rms_fwd(x_ref[...], g)
        dgin_ref[...] += _colsum(dhn * xhat)
        gx_ref[...] = dh_ref[...] + _rms_bwd(dhn, xhat, r, g)

    any_spec = pl.BlockSpec(memory_space=pl.ANY)
    seg_w = [wide for _, wide in IN_SEGMENTS]
    return pl.pallas_call(
        body,
        name="inproj_bwd_x",
        grid=(n_tiles,),
        in_specs=[
            _row_spec(tm, seg_w[0]), _row_spec(tm, seg_w[1]), _row_spec(tm, seg_w[2]),
            _row_spec(tm, seg_w[3]), _row_spec(tm, seg_w[u_seg]),
            pl.BlockSpec((POOL_HALO, POOL_WIDTH), lambda i: (jnp.minimum((i + 1) * halo_per_tile, n_halo - 1), 0)),
            _row_spec(tm, seg_w[5]), _row_spec(tm, seg_w[6]),
            _row_spec(tm, D_MODEL), _row_spec(tm, D_MODEL),
            _full_spec((1, D_MODEL)), any_spec,
        ],
        out_specs=[_row_spec(tm, D_MODEL), _full_spec((1, D_MODEL)),
                   pl.BlockSpec((N_CHIPS, SHARD_COLS, tm), lambda i: (0, 0, i))],
        out_shape=[jax.ShapeDtypeStruct((s, D_MODEL), F32), jax.ShapeDtypeStruct((1, D_MODEL), F32),
                   jax.ShapeDtypeStruct((N_CHIPS, SHARD_COLS, s), BF16)],
        scratch_shapes=[
            pltpu.VMEM((N_CHIPS, SHARD_COLS, D_MODEL), BF16),
            pltpu.VMEM((tm + POOL_HALO, POOL_WIDTH), F32),
            pltpu.SemaphoreType.DMA,
        ],
        compiler_params=pltpu.CompilerParams(dimension_semantics=("arbitrary",), vmem_limit_bytes=VMEM_LIMIT),
    )(dzq, dzkv, dzkr, dgattn, ddc, ddc, dgpool, dgmerge, x, dh, norm_in, w_in_t)


def _inproj_bwd_w(order, dz_sh, hn, g_uq, g_ukv, gs, tm):
    s = hn.shape[0]
    n_tiles = s // tm
    mid = n_tiles // 2
    hc = D_MODEL // 2
    red = _Reduce(COMM_PARAMS[1:3])

    def body(order_ref, dz_ref, hn_ref, guq_hbm, gukv_hbm, gs_ref, gw_hbm, guq_out, gukv_out, gsum_ref,
             acc, pm_w, a_w, b_w, r_w, s_buf, w_send, w_recv, w_local, *red_scratch):
        ph, i = pl.program_id(0), pl.program_id(1)
        x, y, c = lax.axis_index("x"), lax.axis_index("y"), lax.axis_index("c")
        k = 2 * x + y
        me, sibling = (x, y, c), (x, y, 1 - c)
        chips = _other_chips(x, y)
        shard_of_phase = [2 * cx + cy for cx, cy in chips] + [k]
        copy = _remote_copier(w_send, w_recv)
        red.bind([guq_hbm, gukv_hbm], [guq_out, gukv_out], red_scratch)
        mine = pl.ds(pl.multiple_of(c * hc, hc), hc)
        theirs = pl.ds(pl.multiple_of((1 - c) * hc, hc), hc)
        flips = [(fx, fy, fc) for fx in (0, 1) for fy in (0, 1) for fc in (0, 1)][1:]

        def to_sibling(f):
            j = shard_of_phase[f]
            return copy(f, pm_w.at[j, 1 - c], a_w.at[j], sibling)

        def pair_sum(f):
            cx, cy = chips[f]
            return copy(4 + f, pm_w.at[shard_of_phase[f], c], b_w.at[f], (cx, cy, c))

        def small(f):
            fx, fy, fc = flips[f - 1]
            peer = (1 - x if fx else x, 1 - y if fy else y, 1 - c if fc else c)
            return copy(7 + f, gs_ref, s_buf.at[f], peer)

        def finished():
            return copy(7, r_w, gw_hbm.at[:, mine], sibling)

        @pl.when(jnp.logical_and(ph == 0, i == 0))
        def _():
            red.start()
            for f in range(1, N_DEV):
                small(f).start()
            s_buf[0] = gs_ref[...]

        part = _dot(dz_ref[0], hn_ref[...])

        @pl.when(i == 0)
        def _():
            acc[...] = part

        @pl.when(i > 0)
        def _():
            acc[...] += part

        for f in range(3):
            @pl.when(jnp.logical_and(ph == f + 1, i == mid))
            def _(f=f):
                j = shard_of_phase[f]
                copy(f, a_w.at[j], a_w.at[j], me).wait_recv()
                pm_w[j, c] = (pm_w[j, c].astype(F32) + a_w[j].astype(F32)).astype(BF16)
                pair_sum(f).start()
                if f == 0:
                    red.exchange()

        for f in range(4):
            @pl.when(jnp.logical_and(ph == f, i == n_tiles - 1))
            def _(f=f):
                j = shard_of_phase[f]
                pm_w[j, 0] = acc[:, :hc].astype(BF16)
                pm_w[j, 1] = acc[:, hc:].astype(BF16)
                to_sibling(f).start()
                if f < 3:
                    return
                copy(3, a_w.at[k], a_w.at[k], me).wait_recv()
                r_w[...] = pm_w[k, c].astype(F32) + a_w[k].astype(F32)
                for g in range(3):
                    copy(4 + g, b_w.at[g], b_w.at[g], me).wait_recv()
                    r_w[...] = r_w[...] + b_w[g].astype(F32)
                store = pltpu.make_async_copy(r_w, gw_hbm.at[:, mine], w_local)
                store.start()
                finished().start()
                red.finish()
                for g in range(1, N_DEV):
                    copy(7 + g, s_buf.at[g], s_buf.at[g], me).wait_recv()
                dev = 4 * x + 2 * y + c
                total = s_buf[dev]
                for d in range(1, N_DEV):
                    total = total + s_buf[jnp.bitwise_xor(dev, d)]
                gsum_ref[...] = total
                copy(7, gw_hbm.at[:, theirs], gw_hbm.at[:, theirs], me).wait_recv()
                store.wait()
                for g in range(4):
                    to_sibling(g).wait_send()
                for g in range(3):
                    pair_sum(g).wait_send()
                finished().wait_send()
                for g in range(1, N_DEV):
                    small(g).wait_send()

    any_spec = pl.BlockSpec(memory_space=pl.ANY)
    n_sem = 8 + N_DEV - 1
    grid_spec = pltpu.PrefetchScalarGridSpec(
        num_scalar_prefetch=1,
        grid=(N_CHIPS, n_tiles),
        in_specs=[
            pl.BlockSpec((1, SHARD_COLS, tm), lambda ph, i, order: (order[ph], 0, i)),
            pl.BlockSpec((tm, D_MODEL), lambda ph, i, order: (i, 0)),
            any_spec, any_spec,
            pl.BlockSpec((SMALL_ROWS, LANES), lambda ph, i, order: (0, 0)),
        ],
        out_specs=[any_spec, any_spec, any_spec, pl.BlockSpec((SMALL_ROWS, LANES), lambda ph, i, order: (0, 0))],
        scratch_shapes=[
            pltpu.VMEM((SHARD_COLS, D_MODEL), F32),
            pltpu.VMEM((N_CHIPS, 2, SHARD_COLS, hc), BF16),
            pltpu.VMEM((N_CHIPS, SHARD_COLS, hc), BF16),
            pltpu.VMEM((3, SHARD_COLS, hc), BF16),
            pltpu.VMEM((SHARD_COLS, hc), F32),
            pltpu.VMEM((N_DEV, SMALL_ROWS, LANES), F32),
            pltpu.SemaphoreType.DMA((n_sem,)), pltpu.SemaphoreType.DMA((n_sem,)), pltpu.SemaphoreType.DMA,
        ] + red.scratch,
    )
    out = pl.pallas_call(
        body,
        name="inproj_bwd_w",
        grid_spec=grid_spec,
        out_shape=[jax.ShapeDtypeStruct((SHARD_COLS, D_MODEL), F32)] + red.out_shape
        + [jax.ShapeDtypeStruct((SMALL_ROWS, LANES), F32)],
        compiler_params=pltpu.CompilerParams(dimension_semantics=("arbitrary", "arbitrary"),
                                             vmem_limit_bytes=VMEM_LIMIT),
    )(order, dz_sh, hn, g_uq, g_ukv, gs)
    return out[0], out[1], out[2], out[3]


def _other_chips(x, y):
    return ((1 - x, 1 - y), (1 - x, y), (x, 1 - y))


def _half(ref, axis, size, c, lead=()):
    window = pl.ds(pl.multiple_of(c * size, size), size)
    if axis == 0:
        return ref.at[(*lead, window, slice(None))]
    return ref.at[(*lead, slice(None), window)]


def _half_shape(rows, cols, axis, size):
    return (size, cols) if axis == 0 else (rows, size)


def _remote_copier(send_sems, recv_sems):
    def copy(sem, src, dst, to):
        return pltpu.make_async_remote_copy(src_ref=src, dst_ref=dst, send_sem=send_sems.at[sem],
                                            recv_sem=recv_sems.at[sem], device_id=to, device_id_type=MESH)
    return copy


class _Gather:
    def __init__(self, params):
        self.params = params
        n = len(params)
        self.scratch = [pltpu.SemaphoreType.DMA((6 * n,)), pltpu.SemaphoreType.DMA((6 * n,)),
                        pltpu.SemaphoreType.DMA((n,))]
        self.out_shape = [jax.ShapeDtypeStruct((N_CHIPS, r, cc), BF16) for _, r, cc, _, _ in params]

    def bind(self, ins, outs, scratch):
        self.ins, self.outs = ins, outs
        send_sems, recv_sems, self.local_sems = scratch
        self.copy = _remote_copier(send_sems, recv_sems)
        self.x, self.y, self.c = lax.axis_index("x"), lax.axis_index("y"), lax.axis_index("c")
        self.k = 2 * self.x + self.y
        self.chips = _other_chips(self.x, self.y)

    def _local(self, p):
        return pltpu.make_async_copy(self.ins[p], self.outs[p].at[self.k], self.local_sems.at[p])

    def _first(self, p, j):
        _, _, _, axis, size = self.params[p]
        cx, cy = self.chips[j]
        return self.copy(6 * p + j, _half(self.ins[p], axis, size, self.c),
                         _half(self.outs[p], axis, size, self.c, (self.k,)), (cx, cy, self.c))

    def _relay(self, p, j, half_of):
        _, _, _, axis, size = self.params[p]
        cx, cy = self.chips[j]
        block = _half(self.outs[p], axis, size, half_of, (2 * cx + cy,))
        return self.copy(6 * p + 3 + j, block, block, (self.x, self.y, 1 - self.c))

    def start(self):
        for p in range(len(self.params)):
            self._local(p).start()
            for j in range(3):
                self._first(p, j).start()

    def relay(self):
        for j in range(3):
            for p, (_, _, _, axis, size) in enumerate(self.params):
                cx, cy = self.chips[j]
                landed = _half(self.outs[p], axis, size, self.c, (2 * cx + cy,))
                self.copy(6 * p + j, landed, landed, (self.x, self.y, self.c)).wait_recv()
                self._relay(p, j, self.c).start()

    def finish(self):
        for j in range(3):
            for p in range(len(self.params)):
                self._relay(p, j, 1 - self.c).wait_recv()
        for p in range(len(self.params)):
            for j in range(3):
                self._first(p, j).wait_send()
                self._relay(p, j, self.c).wait_send()
            self._local(p).wait()


class _Reduce:
    def __init__(self, params):
        self.params = params
        n = len(params)
        halves = [_half_shape(r, cc, axis, size) for _, r, cc, axis, size in params]
        self.scratch = ([pltpu.VMEM((N_CHIPS, *h), BF16) for h in halves]
                        + [pltpu.VMEM((N_CHIPS, *h), BF16) for h in halves]
                        + [pltpu.VMEM((3, *h), BF16) for h in halves]
                        + [pltpu.VMEM(h, F32) for h in halves]
                        + [pltpu.SemaphoreType.DMA((5 * n,)), pltpu.SemaphoreType.DMA((5 * n,)),
                           pltpu.SemaphoreType.DMA((2 * n,))])
        self.out_shape = [jax.ShapeDtypeStruct((r, cc), F32) for _, r, cc, _, _ in params]

    def bind(self, g_in, g_out, scratch):
        n = len(self.params)
        self.g_in, self.g_out = g_in, g_out
        self.pm, self.a_buf = scratch[0:n], scratch[n:2 * n]
        self.b_buf, self.r_buf = scratch[2 * n:3 * n], scratch[3 * n:4 * n]
        send_sems, recv_sems, self.local_sems = scratch[4 * n:]
        self.copy = _remote_copier(send_sems, recv_sems)
        self.x, self.y, self.c = lax.axis_index("x"), lax.axis_index("y"), lax.axis_index("c")
        self.k = 2 * self.x + self.y
        self.chips = _other_chips(self.x, self.y)
        self.me = (self.x, self.y, self.c)
        self.sibling = (self.x, self.y, 1 - self.c)

    def _load(self, p):
        _, _, _, axis, size = self.params[p]
        return pltpu.make_async_copy(_half(self.g_in[p], axis, size, self.c, (slice(None),)), self.pm[p],
                                     self.local_sems.at[p])

    def _to_sibling(self, p):
        _, _, _, axis, size = self.params[p]
        return self.copy(5 * p, _half(self.g_in[p], axis, size, 1 - self.c, (slice(None),)), self.a_buf[p],
                         self.sibling)

    def _pair_sum(self, p, j):
        cx, cy = self.chips[j]
        return self.copy(5 * p + 1 + j, self.pm[p].at[2 * cx + cy], self.b_buf[p].at[j], (cx, cy, self.c))

    def _store(self, p):
        _, _, _, axis, size = self.params[p]
        n = len(self.params)
        return pltpu.make_async_copy(self.r_buf[p], _half(self.g_out[p], axis, size, self.c),
                                     self.local_sems.at[n + p])

    def _finished(self, p):
        _, _, _, axis, size = self.params[p]
        return self.copy(5 * p + 4, self.r_buf[p], _half(self.g_out[p], axis, size, self.c), self.sibling)

    def start(self):
        for p in range(len(self.params)):
            self._load(p).start()
            self._to_sibling(p).start()

    def exchange(self):
        for p in range(len(self.params)):
            self._load(p).wait()
            self.copy(5 * p, self.a_buf[p], self.a_buf[p], self.me).wait_recv()
            for j, (cx, cy) in enumerate(self.chips):
                kj = 2 * cx + cy
                self.pm[p][kj] = (self.pm[p][kj].astype(F32) + self.a_buf[p][kj].astype(F32)).astype(BF16)
                self._pair_sum(p, j).start()
            self.r_buf[p][...] = self.pm[p][self.k].astype(F32) + self.a_buf[p][self.k].astype(F32)

    def finish(self):
        for p, (_, _, _, axis, size) in enumerate(self.params):
            for j in range(3):
                self.copy(5 * p + 1 + j, self.b_buf[p].at[j], self.b_buf[p].at[j], self.me).wait_recv()
                self.r_buf[p][...] = self.r_buf[p][...] + self.b_buf[p][j].astype(F32)
            self._store(p).start()
            self._finished(p).start()
        for p, (_, _, _, axis, size) in enumerate(self.params):
            theirs = _half(self.g_out[p], axis, size, 1 - self.c)
            self.copy(5 * p + 4, theirs, theirs, self.me).wait_recv()
            self._store(p).wait()
            self._to_sibling(p).wait_send()
            for j in range(3):
                self._pair_sum(p, j).wait_send()
            self._finished(p).wait_send()


def _weight_gather(shards):
    gat = _Gather(COMM_PARAMS[:3])
    n = len(gat.params)

    def body(*refs):
        gat.bind(refs[:n], refs[n:2 * n], refs[2 * n:])
        gat.start()
        gat.relay()
        gat.finish()

    any_spec = pl.BlockSpec(memory_space=pl.ANY)
    return pl.pallas_call(
        body,
        name="weight_gather",
        in_specs=[any_spec] * n,
        out_specs=[any_spec] * n,
        out_shape=gat.out_shape,
        scratch_shapes=gat.scratch,
    )(*shards)


def _adamw_math(w, g, m, v):
    m = ADAM_B1 * m + (1.0 - ADAM_B1) * g
    v = ADAM_B2 * v + (1.0 - ADAM_B2) * (g * g)
    m_hat = m / (1.0 - ADAM_B1 ** ADAM_STEP)
    v_hat = v / (1.0 - ADAM_B2 ** ADAM_STEP)
    delta = -ADAM_LR * (m_hat / (jnp.sqrt(v_hat) + ADAM_EPS) + ADAM_WD * w)
    return delta, m, v


def _adamw_tiled(w, g, m, v, tm):
    rows, cols = w.shape

    def body(w_ref, g_ref, m_ref, v_ref, d_ref, nm_ref, nv_ref):
        d_ref[...], nm_ref[...], nv_ref[...] = _adamw_math(w_ref[...], g_ref[...], m_ref[...], v_ref[...])

    spec = _row_spec(tm, cols)
    return pl.pallas_call(
        body,
        name="adamw_w_in",
        grid=(rows // tm,),
        in_specs=[spec] * 4,
        out_specs=[spec] * 3,
        out_shape=[jax.ShapeDtypeStruct(w.shape, F32)] * 3,
        compiler_params=pltpu.CompilerParams(dimension_semantics=("parallel",), vmem_limit_bytes=VMEM_LIMIT),
    )(w, g, m, v)


def _adamw_many(ws, gs, ms, vs):
    n = len(ws)

    def body(*refs):
        ins, outs = refs[:4 * n], refs[4 * n:]
        for i in range(n):
            d, nm, nv = _adamw_math(ins[i][...], ins[n + i][...], ins[2 * n + i][...], ins[3 * n + i][...])
            outs[i][...] = d
            outs[n + i][...] = nm
            outs[2 * n + i][...] = nv

    vmem_spec = pl.BlockSpec(memory_space=pltpu.VMEM)
    shapes = [jax.ShapeDtypeStruct(w.shape, F32) for w in ws]
    out = pl.pallas_call(
        body,
        name="adamw_small",
        in_specs=[vmem_spec] * (4 * n),
        out_specs=[vmem_spec] * (3 * n),
        out_shape=shapes * 3,
        compiler_params=pltpu.CompilerParams(vmem_limit_bytes=VMEM_LIMIT),
    )(*ws, *gs, *ms, *vs)
    return out[:n], out[n:2 * n], out[2 * n:]


def _pack_rows(parts, rows, dtype):
    flat = jnp.concatenate([p.reshape(-1).astype(dtype) for p in parts])
    flat = jnp.concatenate([flat, jnp.zeros((rows * LANES - flat.shape[0],), dtype)])
    return flat.reshape(rows, LANES)


def _unpack_rows(packed, shapes):
    flat = packed.reshape(-1)
    out, off = [], 0
    for _, shp in shapes:
        n = int(np.prod(shp))
        out.append(flat[off:off + n].reshape(shp))
        off += n
    return out


def _rope_tables(s):
    half = QK_ROPE_DIM // 2
    inv_freq = np.float32(ROPE_THETA) ** (-np.arange(half, dtype=np.float32) / np.float32(half))
    ang = (np.arange(s, dtype=np.float32)[:, None] * inv_freq[None, :]).astype(np.float32)
    cos, sin = np.cos(ang.astype(np.float64)).astype(np.float32), np.sin(ang.astype(np.float64)).astype(np.float32)
    z16 = np.zeros((s, half), np.float32)
    z32 = np.zeros((s, HEAD_PAD - QK_NOPE_DIM - QK_ROPE_DIM), np.float32)
    z64 = np.zeros((s, QK_NOPE_DIM), np.float32)
    rc = np.concatenate([np.ones((s, QK_NOPE_DIM), np.float32), cos, cos, z32], axis=1)
    rsa = np.concatenate([z64, -sin, z16, z32], axis=1)
    rsb = np.concatenate([z64, z16, sin, z32], axis=1)
    return jnp.asarray(rc), jnp.asarray(rsa), jnp.asarray(rsb)


def kernel(x, norm_in, w_in, q_norm, w_uq, kv_norm, w_ukv, pool_w, pool_scale, w_branch_attn, w_branch_pool, w_out, norm_final, loss_target, m_norm_in, m_w_in, m_q_norm, m_w_uq, m_kv_norm, m_w_ukv, m_pool_w, m_pool_scale, m_w_branch_attn, m_w_branch_pool, m_w_out, m_norm_final, v_norm_in, v_w_in, v_q_norm, v_w_uq, v_kv_norm, v_w_ukv, v_pool_w, v_pool_scale, v_w_branch_attn, v_w_branch_pool, v_w_out, v_norm_final):
    s = x.shape[1]
    t_att, t_row = _tiles(s)
    x2 = x.reshape(s, D_MODEL)
    tgt = loss_target.reshape(s, D_MODEL)

    local = [w_in.T, w_uq.reshape(96, 768), w_ukv.reshape(64, 1024), w_branch_attn, w_branch_pool, w_out]
    local = [a.astype(BF16) for a in local]
    w_in_t, w_uq_all, w_ukv_all = _weight_gather(local[:3])
    w_uq_f = w_uq_all.reshape(Q_LORA_RANK, MLA_HEADS, QK_NOPE_DIM + QK_ROPE_DIM)
    w_ukv_f = w_ukv_all.reshape(KV_LORA_RANK, MLA_HEADS, QK_NOPE_DIM + V_HEAD_DIM)
    hw = MLA_HEADS * HEAD_PAD
    wuq_p = jnp.pad(w_uq_f, ((0, 0), (0, 0), (0, HEAD_PAD - QK_NOPE_DIM - QK_ROPE_DIM))).reshape(Q_LORA_RANK, hw)
    wk_p = jnp.pad(w_ukv_f[:, :, :QK_NOPE_DIM], ((0, 0), (0, 0), (0, HEAD_PAD - QK_NOPE_DIM))).reshape(KV_LORA_RANK, hw)
    wv = w_ukv_f[:, :, QK_NOPE_DIM:].reshape(KV_LORA_RANK, MLA_WIDTH)
    rc, rsa, rsb = _rope_tables(s)
    g_in = norm_in.reshape(1, -1)
    g_q = q_norm.reshape(1, -1)
    g_kv = kv_norm.reshape(1, -1)
    g_f = norm_final.reshape(1, -1)
    ps = pool_scale.reshape(1, -1)
    pw_bf = pool_w.astype(BF16)

    hn, zq, zkv, zkr, gattn, u, gpool, gmerge = _inproj_fwd(x2, g_in, w_in_t, t_row)
    q, k, v, q_t, v_t = _qkv_fwd(zq, zkv, zkr, g_q, g_kv, wuq_p, wk_p, wv, rc, rsa, rsb, t_row)
    o, lse, (w_ba_all, w_bp_all, w_out_all) = _attn_fwd(q_t, k, v_t, local[3:], t_att)
    w_out_f = w_out_all.reshape(D_MODEL, D_MODEL)

    (do, delta, dgattn, dgpool, dgmerge, ddc, dh, sq_err, d_w_out, d_w_ba, d_w_bp, d_pool_w, d_pool_scale,
     d_norm_final) = _mid(o, gattn, u, gpool, gmerge, x2, tgt, pw_bf, ps, w_ba_all, w_bp_all, w_out_f, g_f, t_row)

    late_grads = [d_w_ba, d_w_bp, d_w_out.reshape(N_CHIPS, 256, D_MODEL)]
    dq, dk, dv, (g_w_ba, g_w_bp, g_w_out) = _attn_bwd(q, k, v, do, lse, delta, late_grads, t_att)
    dzq, dzkv, dzkr, d_wuq_p, d_wk_p, d_wv, d_q_norm, d_kv_norm = _qkv_bwd(
        dq, dk, dv, zq, zkv, g_q, g_kv, wuq_p, wk_p, wv, rc, rsa, rsb, t_row)
    grad_x, d_norm_in, dz_sh = _inproj_bwd_x(dzq, dzkv, dzkr, dgattn, ddc, dgpool, dgmerge, x2, dh, g_in, w_in_t,
                                             t_row)

    d_w_uq = d_wuq_p.reshape(Q_LORA_RANK, MLA_HEADS, HEAD_PAD)[:, :, :QK_NOPE_DIM + QK_ROPE_DIM]
    d_w_ukv = jnp.concatenate([d_wk_p.reshape(KV_LORA_RANK, MLA_HEADS, HEAD_PAD)[:, :, :QK_NOPE_DIM],
                               d_wv.reshape(KV_LORA_RANK, MLA_HEADS, V_HEAD_DIM)], axis=2)
    small = dict(norm_in=d_norm_in, q_norm=d_q_norm, kv_norm=d_kv_norm, pool_scale=d_pool_scale,
                 norm_final=d_norm_final, pool_w=d_pool_w, sq_err=sq_err)
    gs = _pack_rows([small[n] for n, _ in SMALL_SHAPES], SMALL_ROWS, F32)
    cx, cy = lax.axis_index("x"), lax.axis_index("y")
    order = jnp.stack([2 * ox + oy for ox, oy in _other_chips(cx, cy)] + [2 * cx + cy]).astype(jnp.int32)
    g_w_in_t, g_w_uq, g_w_ukv, g_small = _inproj_bwd_w(
        order, dz_sh, hn, d_w_uq.reshape(N_CHIPS, 96, 768).astype(BF16),
        d_w_ukv.reshape(N_CHIPS, 64, 1024).astype(BF16), gs, 4 * t_row)
    (g_norm_in, g_q_norm, g_kv_norm, g_pool_scale, g_norm_final, g_pool_w,
     sq_err_all) = _unpack_rows(g_small, SMALL_SHAPES)
    g_w_uq = g_w_uq.reshape(w_uq.shape)
    g_w_ukv = g_w_ukv.reshape(w_ukv.shape)

    dl_w_in, nm_w_in, nv_w_in = (a.T for a in _adamw_tiled(w_in.T, g_w_in_t, m_w_in.T, v_w_in.T, 152))

    def two_d(a):
        return a.reshape(1, -1) if a.ndim == 1 else a

    names = ["norm_in", "q_norm", "w_uq", "kv_norm", "w_ukv", "pool_w", "pool_scale", "w_branch_attn",
             "w_branch_pool", "w_out", "norm_final"]
    ws = dict(norm_in=norm_in, q_norm=q_norm, w_uq=w_uq, kv_norm=kv_norm, w_ukv=w_ukv, pool_w=pool_w,
              pool_scale=pool_scale, w_branch_attn=w_branch_attn, w_branch_pool=w_branch_pool, w_out=w_out,
              norm_final=norm_final)
    gsd = dict(norm_in=g_norm_in, q_norm=g_q_norm, w_uq=g_w_uq, kv_norm=g_kv_norm, w_ukv=g_w_ukv, pool_w=g_pool_w,
               pool_scale=g_pool_scale, w_branch_attn=g_w_ba, w_branch_pool=g_w_bp, w_out=g_w_out,
               norm_final=g_norm_final)
    msd = dict(norm_in=m_norm_in, q_norm=m_q_norm, w_uq=m_w_uq, kv_norm=m_kv_norm, w_ukv=m_w_ukv, pool_w=m_pool_w,
               pool_scale=m_pool_scale, w_branch_attn=m_w_branch_attn, w_branch_pool=m_w_branch_pool, w_out=m_w_out,
               norm_final=m_norm_final)
    vsd = dict(norm_in=v_norm_in, q_norm=v_q_norm, w_uq=v_w_uq, kv_norm=v_kv_norm, w_ukv=v_w_ukv, pool_w=v_pool_w,
               pool_scale=v_pool_scale, w_branch_attn=v_w_branch_attn, w_branch_pool=v_w_branch_pool, w_out=v_w_out,
               norm_final=v_norm_final)
    dls, nms, nvs = _adamw_many([two_d(ws[n]) for n in names], [two_d(gsd[n]) for n in names],
                                [two_d(msd[n]) for n in names], [two_d(vsd[n]) for n in names])

    grads = dict(gsd)
    grads["w_in"] = g_w_in_t.T
    delta_w = {n: d.reshape(ws[n].shape) for n, d in zip(names, dls)}
    new_m = {n: d.reshape(ws[n].shape) for n, d in zip(names, nms)}
    new_v = {n: d.reshape(ws[n].shape) for n, d in zip(names, nvs)}
    delta_w["w_in"], new_m["w_in"], new_v["w_in"] = dl_w_in, nm_w_in, nv_w_in
    ws["w_in"] = w_in

    order = ["norm_in", "w_in", "q_norm", "w_uq", "kv_norm", "w_ukv", "pool_w", "pool_scale", "w_branch_attn",
             "w_branch_pool", "w_out", "norm_final"]
    loss = 0.5 * jnp.sum(sq_err_all) / D_MODEL
    return (loss, grad_x.reshape(x.shape),
            *[grads[n].reshape(ws[n].shape) for n in order],
            *[delta_w[n] for n in order], *[new_m[n] for n in order], *[new_v[n] for n in order])
```

```python
import functools

import jax
import jax.numpy as jnp
import numpy as np
from jax import lax
from jax.experimental import pallas as pl
from jax.experimental.pallas import tpu as pltpu

F32 = jnp.float32
BF16 = jnp.bfloat16
MESH = pl.DeviceIdType.MESH

D_MODEL = 1024
CHUNK = 64
MLA_HEADS = 8
QK_NOPE_DIM = 64
QK_ROPE_DIM = 32
V_HEAD_DIM = 64
Q_LORA_RANK = 384
KV_LORA_RANK = 256
MLA_WIDTH = MLA_HEADS * V_HEAD_DIM
ROPE_THETA = 10000.0
POOL_WINDOWS = (2, 4, 8, 16)
POOL_WIDTH = 512
POOL_GROUP_DIM = 128
BRANCH_COLS = D_MODEL // 4
POOL_HALO = 16
EPS = 1e-6
IN_TOTAL = 4256
HEAD_PAD = 128
ATT_SCALE = (QK_NOPE_DIM + QK_ROPE_DIM) ** -0.5
ATT_SCALE_LOG2E = ATT_SCALE * 1.4426950408889634

ADAM_LR = 0.001
ADAM_B1 = 0.9
ADAM_B2 = 0.999
ADAM_EPS = 1e-08
ADAM_WD = 0.01
ADAM_STEP = 10

N_CHIPS = 4
N_DEV = 8
LANES = 128
VMEM_LIMIT = 60 * 1024 * 1024

IN_SEGMENTS = ((384, 384), (256, 256), (32, HEAD_PAD), (512, 512), (512, 512), (512, 512), (2048, 2048))
SHARD_COLS = IN_TOTAL // N_CHIPS


def _shard_pieces():
    bounds, off = [], 0
    for w, _ in IN_SEGMENTS:
        bounds.append((off, off + w))
        off += w
    out = []
    for j in range(N_CHIPS):
        lo, hi = SHARD_COLS * j, SHARD_COLS * (j + 1)
        out.append([(i, max(lo, a) - a, min(hi, b) - a, max(lo, a) - lo)
                    for i, (a, b) in enumerate(bounds) if max(lo, a) < min(hi, b)])
    return out


SHARD_PIECES = _shard_pieces()

COMM_PARAMS = (
    ("w_in", SHARD_COLS, D_MODEL, 1, 512),
    ("w_uq", 96, 768, 0, 48),
    ("w_ukv", 64, 1024, 0, 32),
    ("w_branch_attn", 512, 256, 0, 256),
    ("w_branch_pool", 512, 256, 0, 256),
    ("w_out", 256, 1024, 0, 128),
)

SMALL_SHAPES = (
    ("norm_in", (1024,)),
    ("q_norm", (384,)),
    ("kv_norm", (256,)),
    ("pool_scale", (512,)),
    ("norm_final", (1024,)),
    ("pool_w", (4, 128, 128)),
    ("sq_err", (8, 128)),
)
SMALL_ELEMS = sum(int(np.prod(s)) for _, s in SMALL_SHAPES)
SMALL_ROWS = -(-SMALL_ELEMS // (LANES * 8)) * 8


def _dot(a, b):
    return jnp.dot(a, b, preferred_element_type=F32)


def _dot_nt(a, b):
    return lax.dot_general(a, b, (((1,), (1,)), ((), ())), preferred_element_type=F32)


def _dot_tn(a, b):
    return lax.dot_general(a, b, (((0,), (0,)), ((), ())), preferred_element_type=F32)


def _sigmoid(x):
    return 1.0 / (1.0 + jnp.exp(-x))


def _colsum(x):
    return jnp.sum(x, axis=0, keepdims=True)


def _rms_fwd(x, g):
    r = lax.rsqrt(jnp.mean(x * x, axis=-1, keepdims=True) + EPS)
    xhat = x * r
    return xhat * g, xhat, r


def _rms_bwd(dy, xhat, r, g):
    dxhat = dy * g
    return r * (dxhat - xhat * jnp.mean(dxhat * xhat, axis=-1, keepdims=True))


def _rope(v, c, sa, sb):
    return v * c + pltpu.roll(v, 112, 1) * sa + pltpu.roll(v, 16, 1) * sb


def _unrope(d, c, sa, sb):
    return d * c + pltpu.roll(d * sa, 16, 1) + pltpu.roll(d * sb, 112, 1)


def _row_spec(tm, n):
    return pl.BlockSpec((tm, n), lambda i: (i, 0))


def _full_spec(shape):
    nd = len(shape)
    return pl.BlockSpec(shape, lambda i: (0,) * nd)


def _tiles(s):
    t_att = 512 if s >= 2048 else 128
    t_row = 256 if s >= 1024 else 128
    return t_att, t_row


def _inproj_fwd(x, norm_in, w_in_t, tm):
    s = x.shape[0]

    def body(x_ref, g_ref, w_ref, hn_ref, *z_refs):
        hn, _, _ = _rms_fwd(x_ref[...], g_ref[...])
        hn = hn.astype(BF16)
        hn_ref[...] = hn
        for j, pieces in enumerate(SHARD_PIECES):
            zj = _dot_nt(hn, w_ref[j])
            for seg, lo, hi, col in pieces:
                z_refs[seg][:, lo:hi] = zj[:, col:col + hi - lo]
        for seg, (w, wide) in enumerate(IN_SEGMENTS):
            if wide > w:
                z_refs[seg][:, w:wide] = jnp.zeros((tm, wide - w), F32)

    out_shape = [jax.ShapeDtypeStruct((s, D_MODEL), BF16)]
    out_specs = [_row_spec(tm, D_MODEL)]
    for _, wide in IN_SEGMENTS:
        out_shape.append(jax.ShapeDtypeStruct((s, wide), F32))
        out_specs.append(_row_spec(tm, wide))
    return pl.pallas_call(
        body,
        name="inproj_fwd",
        grid=(s // tm,),
        in_specs=[_row_spec(tm, D_MODEL), _full_spec((1, D_MODEL)),
                  pl.BlockSpec((N_CHIPS, SHARD_COLS, D_MODEL), lambda i: (0, 0, 0), pipeline_mode=pl.Buffered(1))],
        out_specs=out_specs,
        out_shape=out_shape,
        compiler_params=pltpu.CompilerParams(dimension_semantics=("parallel",), vmem_limit_bytes=VMEM_LIMIT),
    )(x, norm_in, w_in_t)


def _qkv_fwd(zq, zkv, zkr, q_norm, kv_norm, wuq_p, wk_p, wv, rc, rsa, rsb, tm):
    s = zq.shape[0]
    hw = MLA_HEADS * HEAD_PAD

    def body(zq_ref, zkv_ref, zkr_ref, gq_ref, gkv_ref, wuq_ref, wk_ref, wv_ref, c_ref, sa_ref, sb_ref,
             q_ref, k_ref, v_ref, qt_ref, vt_ref):
        c, sa, sb = c_ref[...], sa_ref[...], sb_ref[...]
        cq, _, _ = _rms_fwd(zq_ref[...], gq_ref[...])
        qf = _dot(cq.astype(BF16), wuq_ref[...])
        ckv, _, _ = _rms_fwd(zkv_ref[...], gkv_ref[...])
        ckv = ckv.astype(BF16)
        kn = _dot(ckv, wk_ref[...])
        kr = _rope(pltpu.roll(zkr_ref[...], 64, 1), c, sa, sb)
        for h in range(MLA_HEADS):
            cols = slice(h * HEAD_PAD, (h + 1) * HEAD_PAD)
            qh = _rope(qf[:, cols], c, sa, sb)
            q_ref[:, cols] = qh.astype(BF16)
            qt_ref[cols, :] = qh.T.astype(BF16)
            k_ref[:, cols] = (kn[:, cols] + kr).astype(BF16)
        vf = _dot(ckv, wv_ref[...])
        v_ref[...] = vf.astype(BF16)
        vt_ref[...] = vf.T.astype(BF16)

    return pl.pallas_call(
        body,
        name="qkv_fwd",
        grid=(s // tm,),
        in_specs=[
            _row_spec(tm, Q_LORA_RANK), _row_spec(tm, KV_LORA_RANK), _row_spec(tm, HEAD_PAD),
            _full_spec((1, Q_LORA_RANK)), _full_spec((1, KV_LORA_RANK)),
            _full_spec((Q_LORA_RANK, hw)), _full_spec((KV_LORA_RANK, hw)), _full_spec((KV_LORA_RANK, MLA_WIDTH)),
            _row_spec(tm, HEAD_PAD), _row_spec(tm, HEAD_PAD), _row_spec(tm, HEAD_PAD),
        ],
        out_specs=[_row_spec(tm, hw), _row_spec(tm, hw), _row_spec(tm, MLA_WIDTH),
                   pl.BlockSpec((hw, tm), lambda i: (0, i)), pl.BlockSpec((MLA_WIDTH, tm), lambda i: (0, i))],
        out_shape=[jax.ShapeDtypeStruct((s, hw), BF16), jax.ShapeDtypeStruct((s, hw), BF16),
                   jax.ShapeDtypeStruct((s, MLA_WIDTH), BF16),
                   jax.ShapeDtypeStruct((hw, s), BF16), jax.ShapeDtypeStruct((MLA_WIDTH, s), BF16)],
        compiler_params=pltpu.CompilerParams(dimension_semantics=("parallel",), vmem_limit_bytes=VMEM_LIMIT),
    )(zq, zkv, zkr, q_norm, kv_norm, wuq_p, wk_p, wv, rc, rsa, rsb)


def _chunk_mask(t, keys_on_rows):
    rows = lax.broadcasted_iota(jnp.int32, (t, t), 0) // CHUNK
    cols = lax.broadcasted_iota(jnp.int32, (t, t), 1) // CHUNK
    return rows <= cols if keys_on_rows else cols <= rows


def _attn_fwd(q_t, k, v_t, late_shards, t):
    s = k.shape[0]
    pairs = MLA_HEADS // 2
    n_q = s // t
    gat = _Gather(COMM_PARAMS[3:])
    n_w = len(gat.params)

    def body(qt_ref, k_ref, k2_ref, vt_ref, *rest):
        w_in, (o_ref, lse_ref), w_out = rest[:n_w], rest[n_w:n_w + 2], rest[n_w + 2:2 * n_w + 2]
        gat.bind(w_in, w_out, rest[2 * n_w + 2:])
        i = pl.program_id(1)
        step_no = pl.program_id(0) * n_q + i
        pl.when(step_no == 0)(gat.start)
        pl.when(step_no == n_q)(gat.relay)
        mask = _chunk_mask(t, True)
        qcs = [slice(hh * HEAD_PAD, (hh + 1) * HEAD_PAD) for hh in range(2)]
        vcs = [slice(hh * V_HEAD_DIM, (hh + 1) * V_HEAD_DIM) for hh in range(2)]
        qts = [qt_ref[qc, :] for qc in qcs]

        def step(j, carry, masked):
            keys = pl.ds(pl.multiple_of(j * t, t), t)
            out = []
            for hh in range(2):
                m, l, acc = carry[hh]
                sc = _dot(k_ref[keys, qcs[hh]], qts[hh])
                if masked:
                    sc = jnp.where(mask, sc, -jnp.inf)
                m_new = jnp.maximum(m, jnp.max(sc, axis=0, keepdims=True))
                alpha = jnp.exp2((m - m_new) * ATT_SCALE_LOG2E)
                p = jnp.exp2((_dot(k2_ref[keys, qcs[hh]], qts[hh]) - m_new) * ATT_SCALE_LOG2E)
                if masked:
                    p = jnp.where(mask, p, 0.0)
                l = alpha * l + jnp.sum(p, axis=0, keepdims=True)
                acc = alpha * acc + _dot(vt_ref[vcs[hh], keys], p.astype(BF16))
                out.append((m_new, l, acc))
            return tuple(out)

        one = (jnp.full((1, t), -jnp.inf, F32), jnp.zeros((1, t), F32), jnp.zeros((V_HEAD_DIM, t), F32))
        carry = lax.fori_loop(0, i, functools.partial(step, masked=False), (one, one))
        carry = step(i, carry, True)
        o_ref[...] = jnp.concatenate([carry[hh][2] / carry[hh][1] for hh in range(2)], axis=0).T
        for hh in range(2):
            m, l, _ = carry[hh]
            lse_ref[:, qcs[hh]] = jnp.broadcast_to(m * ATT_SCALE_LOG2E + jnp.log2(l), (HEAD_PAD, t)).T
        pl.when(step_no == pairs * n_q - 1)(gat.finish)

    any_spec = pl.BlockSpec(memory_space=pl.ANY)
    out = pl.pallas_call(
        body,
        name="attn_fwd",
        grid=(pairs, n_q),
        in_specs=[
            pl.BlockSpec((2 * HEAD_PAD, t), lambda p, i: (p, i)),
            pl.BlockSpec((s, 2 * HEAD_PAD), lambda p, i: (0, p)),
            pl.BlockSpec((s, 2 * HEAD_PAD), lambda p, i: (0, p)),
            pl.BlockSpec((2 * V_HEAD_DIM, s), lambda p, i: (p, 0)),
        ] + [any_spec] * n_w,
        out_specs=[
            pl.BlockSpec((t, 2 * V_HEAD_DIM), lambda p, i: (i, p)),
            pl.BlockSpec((t, 2 * HEAD_PAD), lambda p, i: (i, p)),
        ] + [any_spec] * n_w,
        out_shape=[jax.ShapeDtypeStruct((s, MLA_WIDTH), F32), jax.ShapeDtypeStruct((s, MLA_HEADS * HEAD_PAD), F32)]
        + gat.out_shape,
        scratch_shapes=gat.scratch,
        compiler_params=pltpu.CompilerParams(dimension_semantics=("arbitrary", "arbitrary"),
                                             vmem_limit_bytes=VMEM_LIMIT),
    )(q_t, k, k, v_t, *late_shards)
    return out[0], out[1], out[2:]


def _mid(o, gattn, u, gpool, gmerge, x, target, pool_w, pool_scale, w_ba, w_bp, w_out, norm_final, tm):
    s = x.shape[0]
    n_tiles = s // tm
    halo_per_tile = tm // POOL_HALO

    def body(o_ref, ga_ref, u_ref, uh_ref, gp_ref, gm_ref, x_ref, t_ref, pw_ref, ps_ref, wba_ref, wbp_ref,
             wout_ref, gf_ref,
             do_ref, dl_ref, dga_ref, dgp_ref, dgm_ref, ddc_ref, dh_ref,
             loss_ref, dwout_out, dwba_out, dwbp_out, dpw_ref, dps_ref, dgf_ref,
             ubuf, dwout_ref, dwba_ref, dwbp_ref):
        i = pl.program_id(0)

        @pl.when(i == 0)
        def _():
            loss_ref[...] = jnp.zeros_like(loss_ref)
            dwout_ref[...] = jnp.zeros_like(dwout_ref)
            dwba_ref[...] = jnp.zeros_like(dwba_ref)
            dwbp_ref[...] = jnp.zeros_like(dwbp_ref)
            dpw_ref[...] = jnp.zeros_like(dpw_ref)
            dps_ref[...] = jnp.zeros_like(dps_ref)
            dgf_ref[...] = jnp.zeros_like(dgf_ref)

        o = o_ref[...]
        ga = ga_ref[...]
        sga = _sigmoid(ga)
        silu_a = ga * sga
        y_attn = (o * silu_a).astype(BF16)

        ubuf[0:POOL_HALO, :] = jnp.where(i > 0, uh_ref[...], 0.0)
        ubuf[POOL_HALO:, :] = u_ref[...]
        row = lax.broadcasted_iota(jnp.int32, (tm, POOL_GROUP_DIM), 0) + i * tm
        ps = ps_ref[...]
        gp = gp_ref[...]
        sgp = _sigmoid(gp)
        silu_p = gp * sgp
        d_bf, dm, inv_cnt = [], [], []
        for g, w in enumerate(POOL_WINDOWS):
            cols = slice(g * POOL_GROUP_DIM, (g + 1) * POOL_GROUP_DIM)
            wsum = ubuf[POOL_HALO:, cols]
            for kk in range(1, w):
                wsum = wsum + ubuf[POOL_HALO - kk:POOL_HALO - kk + tm, cols]
            inv = 1.0 / jnp.minimum(row + 1, w).astype(F32)
            dg = (wsum * inv - ubuf[POOL_HALO:, cols]).astype(BF16)
            d_bf.append(dg)
            inv_cnt.append(inv)
            dm.append(_dot(dg, pw_ref[g]))
        dm = jnp.concatenate(dm, axis=1)
        yp = dm * ps
        y_pool = (yp * silu_p).astype(BF16)

        a = jnp.concatenate([_dot(y_attn, wba_ref[j]) for j in range(N_CHIPS)], axis=1)
        p = jnp.concatenate([_dot(y_pool, wbp_ref[j]) for j in range(N_CHIPS)], axis=1)
        gate_a = _sigmoid(gm_ref[:, :D_MODEL])
        gate_p = _sigmoid(gm_ref[:, D_MODEL:])
        merged = (gate_a * a + gate_p * p).astype(BF16)
        h = x_ref[...] + _dot(merged, wout_ref[...])
        gf = gf_ref[...]
        y, xhat, r = _rms_fwd(h, gf)
        err = y - t_ref[...]
        e2 = err * err
        e2 = jnp.sum(e2.reshape(tm // 8, 8, D_MODEL), axis=0)
        acc = e2[:, 0:LANES]
        for cidx in range(1, D_MODEL // LANES):
            acc = acc + e2[:, cidx * LANES:(cidx + 1) * LANES]
        loss_ref[...] += acc

        dy = err * (1.0 / D_MODEL)
        dgf_ref[...] += _colsum(dy * xhat)
        dh = _rms_bwd(dy, xhat, r, gf)
        dh_ref[...] = dh
        dh_bf = dh.astype(BF16)
        dwout_ref[...] += _dot_tn(merged, dh_bf)
        dmerged = _dot_nt(dh_bf, wout_ref[...])
        da = (dmerged * gate_a).astype(BF16)
        dp = (dmerged * gate_p).astype(BF16)
        dgm_ref[:, :D_MODEL] = (dmerged * a * gate_a * (1.0 - gate_a)).astype(BF16)
        dgm_ref[:, D_MODEL:] = (dmerged * p * gate_p * (1.0 - gate_p)).astype(BF16)
        dy_attn = dy_pool = None
        for j in range(N_CHIPS):
            cols = slice(j * BRANCH_COLS, (j + 1) * BRANCH_COLS)
            dwba_ref[j] += _dot_tn(y_attn, da[:, cols])
            dwbp_ref[j] += _dot_tn(y_pool, dp[:, cols])
            pa = _dot_nt(da[:, cols], wba_ref[j])
            pp = _dot_nt(dp[:, cols], wbp_ref[j])
            dy_attn = pa if dy_attn is None else dy_attn + pa
            dy_pool = pp if dy_pool is None else dy_pool + pp

        do = dy_attn * silu_a
        do_ref[...] = do
        dga_ref[...] = (dy_attn * o * (sga * (1.0 + ga * (1.0 - sga)))).astype(BF16)
        doo = do * o
        for hd in range(MLA_HEADS):
            dl = jnp.sum(doo[:, hd * V_HEAD_DIM:(hd + 1) * V_HEAD_DIM], axis=1, keepdims=True)
            dl_ref[:, hd * HEAD_PAD:(hd + 1) * HEAD_PAD] = jnp.broadcast_to(dl, (tm, HEAD_PAD))

        dyp = dy_pool * silu_p
        dgp_ref[...] = (dy_pool * yp * (sgp * (1.0 + gp * (1.0 - sgp)))).astype(BF16)
        dps_ref[...] += _colsum(dyp * dm)
        dmm = (dyp * ps).astype(BF16)
        for g in range(len(POOL_WINDOWS)):
            cols = slice(g * POOL_GROUP_DIM, (g + 1) * POOL_GROUP_DIM)
            dpw_ref[g] += _dot_tn(d_bf[g], dmm[:, cols])
            ddc_ref[:, cols] = _dot_nt(dmm[:, cols], pw_ref[g]) * inv_cnt[g]

        @pl.when(i == n_tiles - 1)
        def _():
            dwout_out[...] = dwout_ref[...].astype(BF16)
            dwba_out[...] = dwba_ref[...].astype(BF16)
            dwbp_out[...] = dwbp_ref[...].astype(BF16)

    row_in = lambda n: _row_spec(tm, n)
    in_specs = [
        row_in(MLA_WIDTH), row_in(MLA_WIDTH), row_in(POOL_WIDTH),
        pl.BlockSpec((POOL_HALO, POOL_WIDTH), lambda i: (jnp.maximum(i * halo_per_tile - 1, 0), 0)),
        row_in(POOL_WIDTH), row_in(2 * D_MODEL), row_in(D_MODEL), row_in(D_MODEL),
        _full_spec((4, POOL_GROUP_DIM, POOL_GROUP_DIM)), _full_spec((1, POOL_WIDTH)),
        _full_spec((N_CHIPS, MLA_WIDTH, BRANCH_COLS)), _full_spec((N_CHIPS, POOL_WIDTH, BRANCH_COLS)),
        _full_spec((D_MODEL, D_MODEL)), _full_spec((1, D_MODEL)),
    ]
    out_shape = [
        jax.ShapeDtypeStruct((s, MLA_WIDTH), F32),
        jax.ShapeDtypeStruct((s, MLA_HEADS * HEAD_PAD), F32),
        jax.ShapeDtypeStruct((s, MLA_WIDTH), BF16),
        jax.ShapeDtypeStruct((s, POOL_WIDTH), BF16),
        jax.ShapeDtypeStruct((s, 2 * D_MODEL), BF16),
        jax.ShapeDtypeStruct((s, POOL_WIDTH), F32),
        jax.ShapeDtypeStruct((s, D_MODEL), F32),
        jax.ShapeDtypeStruct((8, LANES), F32),
        jax.ShapeDtypeStruct((D_MODEL, D_MODEL), BF16),
        jax.ShapeDtypeStruct((N_CHIPS, MLA_WIDTH, BRANCH_COLS), BF16),
        jax.ShapeDtypeStruct((N_CHIPS, POOL_WIDTH, BRANCH_COLS), BF16),
        jax.ShapeDtypeStruct((4, POOL_GROUP_DIM, POOL_GROUP_DIM), F32),
        jax.ShapeDtypeStruct((1, POOL_WIDTH), F32),
        jax.ShapeDtypeStruct((1, D_MODEL), F32),
    ]
    out_specs = [
        row_in(MLA_WIDTH), row_in(MLA_HEADS * HEAD_PAD), row_in(MLA_WIDTH), row_in(POOL_WIDTH),
        row_in(2 * D_MODEL), row_in(POOL_WIDTH), row_in(D_MODEL),
        _full_spec((8, LANES)), _full_spec((D_MODEL, D_MODEL)), _full_spec((N_CHIPS, MLA_WIDTH, BRANCH_COLS)),
        _full_spec((N_CHIPS, POOL_WIDTH, BRANCH_COLS)), _full_spec((4, POOL_GROUP_DIM, POOL_GROUP_DIM)),
        _full_spec((1, POOL_WIDTH)), _full_spec((1, D_MODEL)),
    ]
    return pl.pallas_call(
        body,
        name="mid",
        grid=(n_tiles,),
        in_specs=in_specs,
        out_specs=out_specs,
        out_shape=out_shape,
        scratch_shapes=[
            pltpu.VMEM((tm + POOL_HALO, POOL_WIDTH), F32),
            pltpu.VMEM((D_MODEL, D_MODEL), F32),
            pltpu.VMEM((N_CHIPS, MLA_WIDTH, BRANCH_COLS), F32),
            pltpu.VMEM((N_CHIPS, POOL_WIDTH, BRANCH_COLS), F32),
        ],
        compiler_params=pltpu.CompilerParams(dimension_semantics=("arbitrary",), vmem_limit_bytes=VMEM_LIMIT),
    )(o, gattn, u, u, gpool, gmerge, x, target, pool_w, pool_scale, w_ba, w_bp, w_out, norm_final)


def _attn_bwd(q, q_t, k, v, do, lse, delta, late_grads, t):
    s = q.shape[0]
    pairs = MLA_HEADS // 2
    n_q = s // t
    red = _Reduce(COMM_PARAMS[3:])
    n_w = len(red.params)

    def body(q_ref, qt_ref, do_ref, lse_ref, dl_ref, k_ref, v_ref, *rest):
        g_in, (dq_ref, dk_ref, dv_ref), g_out = rest[:n_w], rest[n_w:n_w + 3], rest[n_w + 3:2 * n_w + 3]
        red.bind(g_in, g_out, rest[2 * n_w + 3:])
        i = pl.program_id(1)
        step_no = pl.program_id(0) * n_q + i
        pl.when(step_no == 0)(red.start)
        pl.when(step_no == n_q)(red.exchange)

        @pl.when(i == 0)
        def _():
            dk_ref[...] = jnp.zeros_like(dk_ref)
            dv_ref[...] = jnp.zeros_like(dv_ref)

        mask = _chunk_mask(t, False)
        qcs = [slice(hh * HEAD_PAD, (hh + 1) * HEAD_PAD) for hh in range(2)]
        vcs = [slice(hh * V_HEAD_DIM, (hh + 1) * V_HEAD_DIM) for hh in range(2)]
        qhs = [q_ref[:, qc] for qc in qcs]
        qts = [qt_ref[qc, :] for qc in qcs]
        dohs = [do_ref[:, vc].astype(BF16) for vc in vcs]
        do_t = do_ref[...].T.astype(BF16)
        dots = [do_t[vc, :] for vc in vcs]
        lses = [lse_ref[:, hh * HEAD_PAD:hh * HEAD_PAD + 1] for hh in range(2)]
        dls = [dl_ref[:, hh * HEAD_PAD:hh * HEAD_PAD + 1] for hh in range(2)]

        def step(j, dqs, masked):
            keys = pl.ds(pl.multiple_of(j * t, t), t)
            out = []
            for hh in range(2):
                kj = k_ref[keys, qcs[hh]]
                vj = v_ref[keys, vcs[hh]]
                p = jnp.exp2(_dot_nt(qhs[hh], kj) * ATT_SCALE_LOG2E - lses[hh])
                if masked:
                    p = jnp.where(mask, p, 0.0)
                ds = (p * (_dot_nt(dohs[hh], vj) - dls[hh])).astype(BF16)
                dv_ref[vcs[hh], keys] += _dot(dots[hh], p.astype(BF16))
                dk_ref[qcs[hh], keys] += _dot(qts[hh], ds) * ATT_SCALE
                out.append(dqs[hh] + _dot(ds, kj))
            return tuple(out)

        zero = jnp.zeros((t, HEAD_PAD), F32)
        dqs = lax.fori_loop(0, i, functools.partial(step, masked=False), (zero, zero))
        dqs = step(i, dqs, True)
        for hh in range(2):
            dq_ref[:, qcs[hh]] = dqs[hh] * ATT_SCALE
        pl.when(step_no == pairs * n_q - 1)(red.finish)

    hw = MLA_HEADS * HEAD_PAD
    any_spec = pl.BlockSpec(memory_space=pl.ANY)
    out = pl.pallas_call(
        body,
        name="attn_bwd",
        grid=(pairs, n_q),
        in_specs=[
            pl.BlockSpec((t, 2 * HEAD_PAD), lambda p, i: (i, p)),
            pl.BlockSpec((2 * HEAD_PAD, t), lambda p, i: (p, i)),
            pl.BlockSpec((t, 2 * V_HEAD_DIM), lambda p, i: (i, p)),
            pl.BlockSpec((t, 2 * HEAD_PAD), lambda p, i: (i, p)),
            pl.BlockSpec((t, 2 * HEAD_PAD), lambda p, i: (i, p)),
            pl.BlockSpec((s, 2 * HEAD_PAD), lambda p, i: (0, p)),
            pl.BlockSpec((s, 2 * V_HEAD_DIM), lambda p, i: (0, p)),
        ] + [any_spec] * n_w,
        out_specs=[
            pl.BlockSpec((t, 2 * HEAD_PAD), lambda p, i: (i, p)),
            pl.BlockSpec((2 * HEAD_PAD, s), lambda p, i: (p, 0)),
            pl.BlockSpec((2 * V_HEAD_DIM, s), lambda p, i: (p, 0)),
        ] + [any_spec] * n_w,
        out_shape=[jax.ShapeDtypeStruct((s, hw), F32), jax.ShapeDtypeStruct((hw, s), F32),
                   jax.ShapeDtypeStruct((MLA_WIDTH, s), F32)] + red.out_shape,
        scratch_shapes=red.scratch,
        compiler_params=pltpu.CompilerParams(dimension_semantics=("arbitrary", "arbitrary"),
                                             vmem_limit_bytes=VMEM_LIMIT),
    )(q, q_t, do, lse, delta, k, v, *late_grads)
    return out[0], out[1], out[2], out[3:]


def _qkv_bwd(dq, dk_t, dv_t, zq, zkv, q_norm, kv_norm, wuq_p, wk_p, wv, rc, rsa, rsb, tm):
    s = zq.shape[0]
    hw = MLA_HEADS * HEAD_PAD

    def body(dq_ref, dk_ref, dv_ref, zq_ref, zkv_ref, gq_ref, gkv_ref, wuq_ref, wk_ref, wv_ref,
             c_ref, sa_ref, sb_ref,
             dzq_ref, dzkv_ref, dzkr_ref, dwuq_ref, dwk_ref, dwv_ref, dgq_ref, dgkv_ref):
        i = pl.program_id(0)

        @pl.when(i == 0)
        def _():
            dwuq_ref[...] = jnp.zeros_like(dwuq_ref)
            dwk_ref[...] = jnp.zeros_like(dwk_ref)
            dwv_ref[...] = jnp.zeros_like(dwv_ref)
            dgq_ref[...] = jnp.zeros_like(dgq_ref)
            dgkv_ref[...] = jnp.zeros_like(dgkv_ref)

        c, sa, sb = c_ref[...], sa_ref[...], sb_ref[...]
        gq, gkv = gq_ref[...], gkv_ref[...]

        cq, xq, rq = _rms_fwd(zq_ref[...], gq)
        dqp = jnp.concatenate(
            [_unrope(dq_ref[:, h * HEAD_PAD:(h + 1) * HEAD_PAD], c, sa, sb) for h in range(MLA_HEADS)],
            axis=1).astype(BF16)
        dwuq_ref[...] += _dot_tn(cq.astype(BF16), dqp)
        dcq = _dot_nt(dqp, wuq_ref[...])
        dgq_ref[...] += _colsum(dcq * xq)
        dzq_ref[...] = _rms_bwd(dcq, xq, rq, gq).astype(BF16)

        ckv, xkv, rkv = _rms_fwd(zkv_ref[...], gkv)
        ckv = ckv.astype(BF16)
        dkf = dk_ref[...].T
        dk_bf = dkf.astype(BF16)
        dv_bf = dv_ref[...].T.astype(BF16)
        dwk_ref[...] += _dot_tn(ckv, dk_bf)
        dwv_ref[...] += _dot_tn(ckv, dv_bf)
        dckv = _dot_nt(dk_bf, wk_ref[...]) + _dot_nt(dv_bf, wv_ref[...])
        dgkv_ref[...] += _colsum(dckv * xkv)
        dzkv_ref[...] = _rms_bwd(dckv, xkv, rkv, gkv).astype(BF16)

        dkr = dkf[:, 0:HEAD_PAD]
        for h in range(1, MLA_HEADS):
            dkr = dkr + dkf[:, h * HEAD_PAD:(h + 1) * HEAD_PAD]
        dkr = pltpu.roll(_unrope(dkr, c, sa, sb), 64, 1)
        lane = lax.broadcasted_iota(jnp.int32, (tm, HEAD_PAD), 1)
        dzkr_ref[...] = jnp.where(lane < QK_ROPE_DIM, dkr, 0.0).astype(BF16)

    return pl.pallas_call(
        body,
        name="qkv_bwd",
        grid=(s // tm,),
        in_specs=[
            _row_spec(tm, hw), pl.BlockSpec((hw, tm), lambda i: (0, i)), pl.BlockSpec((MLA_WIDTH, tm), lambda i: (0, i)),
            _row_spec(tm, Q_LORA_RANK), _row_spec(tm, KV_LORA_RANK),
            _full_spec((1, Q_LORA_RANK)), _full_spec((1, KV_LORA_RANK)),
            _full_spec((Q_LORA_RANK, hw)), _full_spec((KV_LORA_RANK, hw)), _full_spec((KV_LORA_RANK, MLA_WIDTH)),
            _row_spec(tm, HEAD_PAD), _row_spec(tm, HEAD_PAD), _row_spec(tm, HEAD_PAD),
        ],
        out_specs=[
            _row_spec(tm, Q_LORA_RANK), _row_spec(tm, KV_LORA_RANK), _row_spec(tm, HEAD_PAD),
            _full_spec((Q_LORA_RANK, hw)), _full_spec((KV_LORA_RANK, hw)), _full_spec((KV_LORA_RANK, MLA_WIDTH)),
            _full_spec((1, Q_LORA_RANK)), _full_spec((1, KV_LORA_RANK)),
        ],
        out_shape=[
            jax.ShapeDtypeStruct((s, Q_LORA_RANK), BF16), jax.ShapeDtypeStruct((s, KV_LORA_RANK), BF16),
            jax.ShapeDtypeStruct((s, HEAD_PAD), BF16),
            jax.ShapeDtypeStruct((Q_LORA_RANK, hw), F32), jax.ShapeDtypeStruct((KV_LORA_RANK, hw), F32),
            jax.ShapeDtypeStruct((KV_LORA_RANK, MLA_WIDTH), F32),
            jax.ShapeDtypeStruct((1, Q_LORA_RANK), F32), jax.ShapeDtypeStruct((1, KV_LORA_RANK), F32),
        ],
        compiler_params=pltpu.CompilerParams(dimension_semantics=("arbitrary",), vmem_limit_bytes=VMEM_LIMIT),
    )(dq, dk_t, dv_t, zq, zkv, q_norm, kv_norm, wuq_p, wk_p, wv, rc, rsa, rsb)


def _inproj_bwd_x(dzq, dzkv, dzkr, dgattn, ddc, dgpool, dgmerge, x, dh, norm_in, w_in_t, tm):
    s = x.shape[0]
    n_tiles = s // tm
    halo_per_tile = tm // POOL_HALO
    n_halo = s // POOL_HALO
    u_seg = 4

    def body(dzq_ref, dzkv_ref, dzkr_ref, dga_ref, ddc_ref, ddn_ref, dgp_ref, dgm_ref, x_ref, dh_ref,
             g_ref, w_hbm, gx_ref, dgin_ref, dzs_ref, w_vmem, dbuf, sem):
        i = pl.program_id(0)

        @pl.when(i == 0)
        def _():
            cp = pltpu.make_async_copy(w_hbm, w_vmem, sem)
            cp.start()
            dgin_ref[...] = jnp.zeros_like(dgin_ref)
            cp.wait()

        dbuf[0:tm, :] = ddc_ref[...]
        dbuf[tm:, :] = jnp.where(i < n_tiles - 1, ddn_ref[...], 0.0)
        row = lax.broadcasted_iota(jnp.int32, (tm, POOL_GROUP_DIM), 0) + i * tm
        du = []
        for g, w in enumerate(POOL_WINDOWS):
            cols = slice(g * POOL_GROUP_DIM, (g + 1) * POOL_GROUP_DIM)
            fsum = dbuf[0:tm, cols]
            for kk in range(1, w):
                fsum = fsum + dbuf[kk:kk + tm, cols]
            du.append(fsum - dbuf[0:tm, cols] * jnp.minimum(row + 1, w).astype(F32))
        du = jnp.concatenate(du, axis=1).astype(BF16)

        dz = [dzq_ref[...], dzkv_ref[...], dzkr_ref[...], dga_ref[...], du, dgp_ref[...], dgm_ref[...]]
        dhn = None
        for j, pieces in enumerate(SHARD_PIECES):
            parts = [dz[seg][:, lo:hi] for seg, lo, hi, _ in pieces]
            dzj = parts[0] if len(parts) == 1 else jnp.concatenate(parts, axis=1)
            dzs_ref[j] = dzj.T
            part = _dot(dzj, w_vmem[j])
            dhn = part if dhn is None else dhn + part

        g = g_ref[...]
        _, xhat, r = _rms_fwd(x_ref[...], g)
        dgin_ref[...] += _colsum(dhn * xhat)
        gx_ref[...] = dh_ref[...] + _rms_bwd(dhn, xhat, r, g)

    any_spec = pl.BlockSpec(memory_space=pl.ANY)
    seg_w = [wide for _, wide in IN_SEGMENTS]
    return pl.pallas_call(
        body,
        name="inproj_bwd_x",
        grid=(n_tiles,),
        in_specs=[
            _row_spec(tm, seg_w[0]), _row_spec(tm, seg_w[1]), _row_spec(tm, seg_w[2]),
            _row_spec(tm, seg_w[3]), _row_spec(tm, seg_w[u_seg]),
            pl.BlockSpec((POOL_HALO, POOL_WIDTH), lambda i: (jnp.minimum((i + 1) * halo_per_tile, n_halo - 1), 0)),
            _row_spec(tm, seg_w[5]), _row_spec(tm, seg_w[6]),
            _row_spec(tm, D_MODEL), _row_spec(tm, D_MODEL),
            _full_spec((1, D_MODEL)), any_spec,
        ],
        out_specs=[_row_spec(tm, D_MODEL), _full_spec((1, D_MODEL)),
                   pl.BlockSpec((N_CHIPS, SHARD_COLS, tm), lambda i: (0, 0, i))],
        out_shape=[jax.ShapeDtypeStruct((s, D_MODEL), F32), jax.ShapeDtypeStruct((1, D_MODEL), F32),
                   jax.ShapeDtypeStruct((N_CHIPS, SHARD_COLS, s), BF16)],
        scratch_shapes=[
            pltpu.VMEM((N_CHIPS, SHARD_COLS, D_MODEL), BF16),
            pltpu.VMEM((tm + POOL_HALO, POOL_WIDTH), F32),
            pltpu.SemaphoreType.DMA,
        ],
        compiler_params=pltpu.CompilerParams(dimension_semantics=("arbitrary",), vmem_limit_bytes=VMEM_LIMIT),
    )(dzq, dzkv, dzkr, dgattn, ddc, ddc, dgpool, dgmerge, x, dh, norm_in, w_in_t)


def _inproj_bwd_w(order, dz_sh, hn, g_uq, g_ukv, gs, tm):
    s = hn.shape[0]
    n_tiles = s // tm
    mid = n_tiles // 2
    hc = D_MODEL // 2
    red = _Reduce(COMM_PARAMS[1:3])

    def body(order_ref, dz_ref, hn_ref, guq_hbm, gukv_hbm, gs_ref, gw_hbm, guq_out, gukv_out, gsum_ref,
             acc, pm_w, a_w, b_w, r_w, s_buf, w_send, w_recv, w_local, *red_scratch):
        ph, i = pl.program_id(0), pl.program_id(1)
        x, y, c = lax.axis_index("x"), lax.axis_index("y"), lax.axis_index("c")
        k = 2 * x + y
        me, sibling = (x, y, c), (x, y, 1 - c)
        chips = _other_chips(x, y)
        shard_of_phase = [2 * cx + cy for cx, cy in chips] + [k]
        copy = _remote_copier(w_send, w_recv)
        red.bind([guq_hbm, gukv_hbm], [guq_out, gukv_out], red_scratch)
        mine = pl.ds(pl.multiple_of(c * hc, hc), hc)
        theirs = pl.ds(pl.multiple_of((1 - c) * hc, hc), hc)
        flips = [(fx, fy, fc) for fx in (0, 1) for fy in (0, 1) for fc in (0, 1)][1:]

        def to_sibling(f):
            j = shard_of_phase[f]
            return copy(f, pm_w.at[j, 1 - c], a_w.at[j], sibling)

        def pair_sum(f):
            cx, cy = chips[f]
            return copy(4 + f, pm_w.at[shard_of_phase[f], c], b_w.at[f], (cx, cy, c))

        def small(f):
            fx, fy, fc = flips[f - 1]
            peer = (1 - x if fx else x, 1 - y if fy else y, 1 - c if fc else c)
            return copy(7 + f, gs_ref, s_buf.at[f], peer)

        def finished():
            return copy(7, r_w, gw_hbm.at[:, mine], sibling)

        @pl.when(jnp.logical_and(ph == 0, i == 0))
        def _():
            red.start()
            for f in range(1, N_DEV):
                small(f).start()
            s_buf[0] = gs_ref[...]

        part = _dot(dz_ref[0], hn_ref[...])

        @pl.when(i == 0)
        def _():
            acc[...] = part

        @pl.when(i > 0)
        def _():
            acc[...] += part

        for f in range(3):
            @pl.when(jnp.logical_and(ph == f + 1, i == mid))
            def _(f=f):
                j = shard_of_phase[f]
                copy(f, a_w.at[j], a_w.at[j], me).wait_recv()
                pm_w[j, c] = (pm_w[j, c].astype(F32) + a_w[j].astype(F32)).astype(BF16)
                pair_sum(f).start()
                if f == 0:
                    red.exchange()

        for f in range(4):
            @pl.when(jnp.logical_and(ph == f, i == n_tiles - 1))
            def _(f=f):
                j = shard_of_phase[f]
                pm_w[j, 0] = acc[:, :hc].astype(BF16)
                pm_w[j, 1] = acc[:, hc:].astype(BF16)
                to_sibling(f).start()
                if f < 3:
                    return
                copy(3, a_w.at[k], a_w.at[k], me).wait_recv()
                r_w[...] = pm_w[k, c].astype(F32) + a_w[k].astype(F32)
                for g in range(3):
                    copy(4 + g, b_w.at[g], b_w.at[g], me).wait_recv()
                    r_w[...] = r_w[...] + b_w[g].astype(F32)
                store = pltpu.make_async_copy(r_w, gw_hbm.at[:, mine], w_local)
                store.start()
                finished().start()
                red.finish()
                for g in range(1, N_DEV):
                    copy(7 + g, s_buf.at[g], s_buf.at[g], me).wait_recv()
                dev = 4 * x + 2 * y + c
                total = s_buf[dev]
                for d in range(1, N_DEV):
                    total = total + s_buf[jnp.bitwise_xor(dev, d)]
                gsum_ref[...] = total
                copy(7, gw_hbm.at[:, theirs], gw_hbm.at[:, theirs], me).wait_recv()
                store.wait()
                for g in range(4):
                    to_sibling(g).wait_send()
                for g in range(3):
                    pair_sum(g).wait_send()
                finished().wait_send()
                for g in range(1, N_DEV):
                    small(g).wait_send()

    any_spec = pl.BlockSpec(memory_space=pl.ANY)
    n_sem = 8 + N_DEV - 1
    grid_spec = pltpu.PrefetchScalarGridSpec(
        num_scalar_prefetch=1,
        grid=(N_CHIPS, n_tiles),
        in_specs=[
            pl.BlockSpec((1, SHARD_COLS, tm), lambda ph, i, order: (order[ph], 0, i)),
            pl.BlockSpec((tm, D_MODEL), lambda ph, i, order: (i, 0)),
            any_spec, any_spec,
            pl.BlockSpec((SMALL_ROWS, LANES), lambda ph, i, order: (0, 0)),
        ],
        out_specs=[any_spec, any_spec, any_spec, pl.BlockSpec((SMALL_ROWS, LANES), lambda ph, i, order: (0, 0))],
        scratch_shapes=[
            pltpu.VMEM((SHARD_COLS, D_MODEL), F32),
            pltpu.VMEM((N_CHIPS, 2, SHARD_COLS, hc), BF16),
            pltpu.VMEM((N_CHIPS, SHARD_COLS, hc), BF16),
            pltpu.VMEM((3, SHARD_COLS, hc), BF16),
            pltpu.VMEM((SHARD_COLS, hc), F32),
            pltpu.VMEM((N_DEV, SMALL_ROWS, LANES), F32),
            pltpu.SemaphoreType.DMA((n_sem,)), pltpu.SemaphoreType.DMA((n_sem,)), pltpu.SemaphoreType.DMA,
        ] + red.scratch,
    )
    out = pl.pallas_call(
        body,
        name="inproj_bwd_w",
        grid_spec=grid_spec,
        out_shape=[jax.ShapeDtypeStruct((SHARD_COLS, D_MODEL), F32)] + red.out_shape
        + [jax.ShapeDtypeStruct((SMALL_ROWS, LANES), F32)],
        compiler_params=pltpu.CompilerParams(dimension_semantics=("arbitrary", "arbitrary"),
                                             vmem_limit_bytes=VMEM_LIMIT),
    )(order, dz_sh, hn, g_uq, g_ukv, gs)
    return out[0], out[1], out[2], out[3]


def _other_chips(x, y):
    return ((1 - x, 1 - y), (1 - x, y), (x, 1 - y))


def _half(ref, axis, size, c, lead=()):
    window = pl.ds(pl.multiple_of(c * size, size), size)
    if axis == 0:
        return ref.at[(*lead, window, slice(None))]
    return ref.at[(*lead, slice(None), window)]


def _half_shape(rows, cols, axis, size):
    return (size, cols) if axis == 0 else (rows, size)


def _remote_copier(send_sems, recv_sems):
    def copy(sem, src, dst, to):
        return pltpu.make_async_remote_copy(src_ref=src, dst_ref=dst, send_sem=send_sems.at[sem],
                                            recv_sem=recv_sems.at[sem], device_id=to, device_id_type=MESH)
    return copy


class _Gather:
    def __init__(self, params):
        self.params = params
        n = len(params)
        self.scratch = [pltpu.SemaphoreType.DMA((6 * n,)), pltpu.SemaphoreType.DMA((6 * n,)),
                        pltpu.SemaphoreType.DMA((n,))]
        self.out_shape = [jax.ShapeDtypeStruct((N_CHIPS, r, cc), BF16) for _, r, cc, _, _ in params]

    def bind(self, ins, outs, scratch):
        self.ins, self.outs = ins, outs
        send_sems, recv_sems, self.local_sems = scratch
        self.copy = _remote_copier(send_sems, recv_sems)
        self.x, self.y, self.c = lax.axis_index("x"), lax.axis_index("y"), lax.axis_index("c")
        self.k = 2 * self.x + self.y
        self.chips = _other_chips(self.x, self.y)

    def _local(self, p):
        return pltpu.make_async_copy(self.ins[p], self.outs[p].at[self.k], self.local_sems.at[p])

    def _first(self, p, j):
        _, _, _, axis, size = self.params[p]
        cx, cy = self.chips[j]
        return self.copy(6 * p + j, _half(self.ins[p], axis, size, self.c),
                         _half(self.outs[p], axis, size, self.c, (self.k,)), (cx, cy, self.c))

    def _relay(self, p, j, half_of):
        _, _, _, axis, size = self.params[p]
        cx, cy = self.chips[j]
        block = _half(self.outs[p], axis, size, half_of, (2 * cx + cy,))
        return self.copy(6 * p + 3 + j, block, block, (self.x, self.y, 1 - self.c))

    def start(self):
        for p in range(len(self.params)):
            self._local(p).start()
            for j in range(3):
                self._first(p, j).start()

    def relay(self):
        for j in range(3):
            for p, (_, _, _, axis, size) in enumerate(self.params):
                cx, cy = self.chips[j]
                landed = _half(self.outs[p], axis, size, self.c, (2 * cx + cy,))
                self.copy(6 * p + j, landed, landed, (self.x, self.y, self.c)).wait_recv()
                self._relay(p, j, self.c).start()

    def finish(self):
        for j in range(3):
            for p in range(len(self.params)):
                self._relay(p, j, 1 - self.c).wait_recv()
        for p in range(len(self.params)):
            for j in range(3):
                self._first(p, j).wait_send()
                self._relay(p, j, self.c).wait_send()
            self._local(p).wait()


class _Reduce:
    def __init__(self, params):
        self.params = params
        n = len(params)
        halves = [_half_shape(r, cc, axis, size) for _, r, cc, axis, size in params]
        self.scratch = ([pltpu.VMEM((N_CHIPS, *h), BF16) for h in halves]
                        + [pltpu.VMEM((N_CHIPS, *h), BF16) for h in halves]
                        + [pltpu.VMEM((3, *h), BF16) for h in halves]
                        + [pltpu.VMEM(h, F32) for h in halves]
                        + [pltpu.SemaphoreType.DMA((5 * n,)), pltpu.SemaphoreType.DMA((5 * n,)),
                           pltpu.SemaphoreType.DMA((2 * n,))])
        self.out_shape = [jax.ShapeDtypeStruct((r, cc), F32) for _, r, cc, _, _ in params]

    def bind(self, g_in, g_out, scratch):
        n = len(self.params)
        self.g_in, self.g_out = g_in, g_out
        self.pm, self.a_buf = scratch[0:n], scratch[n:2 * n]
        self.b_buf, self.r_buf = scratch[2 * n:3 * n], scratch[3 * n:4 * n]
        send_sems, recv_sems, self.local_sems = scratch[4 * n:]
        self.copy = _remote_copier(send_sems, recv_sems)
        self.x, self.y, self.c = lax.axis_index("x"), lax.axis_index("y"), lax.axis_index("c")
        self.k = 2 * self.x + self.y
        self.chips = _other_chips(self.x, self.y)
        self.me = (self.x, self.y, self.c)
        self.sibling = (self.x, self.y, 1 - self.c)

    def _load(self, p):
        _, _, _, axis, size = self.params[p]
        return pltpu.make_async_copy(_half(self.g_in[p], axis, size, self.c, (slice(None),)), self.pm[p],
                                     self.local_sems.at[p])

    def _to_sibling(self, p):
        _, _, _, axis, size = self.params[p]
        return self.copy(5 * p, _half(self.g_in[p], axis, size, 1 - self.c, (slice(None),)), self.a_buf[p],
                         self.sibling)

    def _pair_sum(self, p, j):
        cx, cy = self.chips[j]
        return self.copy(5 * p + 1 + j, self.pm[p].at[2 * cx + cy], self.b_buf[p].at[j], (cx, cy, self.c))

    def _store(self, p):
        _, _, _, axis, size = self.params[p]
        n = len(self.params)
        return pltpu.make_async_copy(self.r_buf[p], _half(self.g_out[p], axis, size, self.c),
                                     self.local_sems.at[n + p])

    def _finished(self, p):
        _, _, _, axis, size = self.params[p]
        return self.copy(5 * p + 4, self.r_buf[p], _half(self.g_out[p], axis, size, self.c), self.sibling)

    def start(self):
        for p in range(len(self.params)):
            self._load(p).start()
            self._to_sibling(p).start()

    def exchange(self):
        for p in range(len(self.params)):
            self._load(p).wait()
            self.copy(5 * p, self.a_buf[p], self.a_buf[p], self.me).wait_recv()
            for j, (cx, cy) in enumerate(self.chips):
                kj = 2 * cx + cy
                self.pm[p][kj] = (self.pm[p][kj].astype(F32) + self.a_buf[p][kj].astype(F32)).astype(BF16)
                self._pair_sum(p, j).start()
            self.r_buf[p][...] = self.pm[p][self.k].astype(F32) + self.a_buf[p][self.k].astype(F32)

    def finish(self):
        for p, (_, _, _, axis, size) in enumerate(self.params):
            for j in range(3):
                self.copy(5 * p + 1 + j, self.b_buf[p].at[j], self.b_buf[p].at[j], self.me).wait_recv()
                self.r_buf[p][...] = self.r_buf[p][...] + self.b_buf[p][j].astype(F32)
            self._store(p).start()
            self._finished(p).start()
        for p, (_, _, _, axis, size) in enumerate(self.params):
            theirs = _half(self.g_out[p], axis, size, 1 - self.c)
            self.copy(5 * p + 4, theirs, theirs, self.me).wait_recv()
            self._store(p).wait()
            self._to_sibling(p).wait_send()
            for j in range(3):
                self._pair_sum(p, j).wait_send()
            self._finished(p).wait_send()


def _weight_gather(shards):
    gat = _Gather(COMM_PARAMS[:3])
    n = len(gat.params)

    def body(*refs):
        gat.bind(refs[:n], refs[n:2 * n], refs[2 * n:])
        gat.start()
        gat.relay()
        gat.finish()

    any_spec = pl.BlockSpec(memory_space=pl.ANY)
    return pl.pallas_call(
        body,
        name="weight_gather",
        in_specs=[any_spec] * n,
        out_specs=[any_spec] * n,
        out_shape=gat.out_shape,
        scratch_shapes=gat.scratch,
    )(*shards)


def _adamw_math(w, g, m, v):
    m = ADAM_B1 * m + (1.0 - ADAM_B1) * g
    v = ADAM_B2 * v + (1.0 - ADAM_B2) * (g * g)
    m_hat = m / (1.0 - ADAM_B1 ** ADAM_STEP)
    v_hat = v / (1.0 - ADAM_B2 ** ADAM_STEP)
    delta = -ADAM_LR * (m_hat / (jnp.sqrt(v_hat) + ADAM_EPS) + ADAM_WD * w)
    return delta, m, v


def _adamw_tiled(w, g, m, v, tm):
    rows, cols = w.shape

    def body(w_ref, g_ref, m_ref, v_ref, d_ref, nm_ref, nv_ref):
        d_ref[...], nm_ref[...], nv_ref[...] = _adamw_math(w_ref[...], g_ref[...], m_ref[...], v_ref[...])

    spec = _row_spec(tm, cols)
    return pl.pallas_call(
        body,
        name="adamw_w_in",
        grid=(rows // tm,),
        in_specs=[spec] * 4,
        out_specs=[spec] * 3,
        out_shape=[jax.ShapeDtypeStruct(w.shape, F32)] * 3,
        compiler_params=pltpu.CompilerParams(dimension_semantics=("parallel",), vmem_limit_bytes=VMEM_LIMIT),
    )(w, g, m, v)


def _adamw_many(ws, gs, ms, vs):
    n = len(ws)

    def body(*refs):
        ins, outs = refs[:4 * n], refs[4 * n:]
        for i in range(n):
            d, nm, nv = _adamw_math(ins[i][...], ins[n + i][...], ins[2 * n + i][...], ins[3 * n + i][...])
            outs[i][...] = d
            outs[n + i][...] = nm
            outs[2 * n + i][...] = nv

    vmem_spec = pl.BlockSpec(memory_space=pltpu.VMEM)
    shapes = [jax.ShapeDtypeStruct(w.shape, F32) for w in ws]
    out = pl.pallas_call(
        body,
        name="adamw_small",
        in_specs=[vmem_spec] * (4 * n),
        out_specs=[vmem_spec] * (3 * n),
        out_shape=shapes * 3,
        compiler_params=pltpu.CompilerParams(vmem_limit_bytes=VMEM_LIMIT),
    )(*ws, *gs, *ms, *vs)
    return out[:n], out[n:2 * n], out[2 * n:]


def _pack_rows(parts, rows, dtype):
    flat = jnp.concatenate([p.reshape(-1).astype(dtype) for p in parts])
    flat = jnp.concatenate([flat, jnp.zeros((rows * LANES - flat.shape[0],), dtype)])
    return flat.reshape(rows, LANES)


def _unpack_rows(packed, shapes):
    flat = packed.reshape(-1)
    out, off = [], 0
    for _, shp in shapes:
        n = int(np.prod(shp))
        out.append(flat[off:off + n].reshape(shp))
        off += n
    return out


def _rope_tables(s):
    half = QK_ROPE_DIM // 2
    inv_freq = np.float32(ROPE_THETA) ** (-np.arange(half, dtype=np.float32) / np.float32(half))
    ang = (np.arange(s, dtype=np.float32)[:, None] * inv_freq[None, :]).astype(np.float32)
    cos, sin = np.cos(ang.astype(np.float64)).astype(np.float32), np.sin(ang.astype(np.float64)).astype(np.float32)
    z16 = np.zeros((s, half), np.float32)
    z32 = np.zeros((s, HEAD_PAD - QK_NOPE_DIM - QK_ROPE_DIM), np.float32)
    z64 = np.zeros((s, QK_NOPE_DIM), np.float32)
    rc = np.concatenate([np.ones((s, QK_NOPE_DIM), np.float32), cos, cos, z32], axis=1)
    rsa = np.concatenate([z64, -sin, z16, z32], axis=1)
    rsb = np.concatenate([z64, z16, sin, z32], axis=1)
    return jnp.asarray(rc), jnp.asarray(rsa), jnp.asarray(rsb)


def kernel(x, norm_in, w_in, q_norm, w_uq, kv_norm, w_ukv, pool_w, pool_scale, w_branch_attn, w_branch_pool, w_out, norm_final, loss_target, m_norm_in, m_w_in, m_q_norm, m_w_uq, m_kv_norm, m_w_ukv, m_pool_w, m_pool_scale, m_w_branch_attn, m_w_branch_pool, m_w_out, m_norm_final, v_norm_in, v_w_in, v_q_norm, v_w_uq, v_kv_norm, v_w_ukv, v_pool_w, v_pool_scale, v_w_branch_attn, v_w_branch_pool, v_w_out, v_norm_final):
    s = x.shape[1]
    t_att, t_row = _tiles(s)
    x2 = x.reshape(s, D_MODEL)
    tgt = loss_target.reshape(s, D_MODEL)

    local = [w_in.T, w_uq.reshape(96, 768), w_ukv.reshape(64, 1024), w_branch_attn, w_branch_pool, w_out]
    local = [a.astype(BF16) for a in local]
    w_in_t, w_uq_all, w_ukv_all = _weight_gather(local[:3])
    w_uq_f = w_uq_all.reshape(Q_LORA_RANK, MLA_HEADS, QK_NOPE_DIM + QK_ROPE_DIM)
    w_ukv_f = w_ukv_all.reshape(KV_LORA_RANK, MLA_HEADS, QK_NOPE_DIM + V_HEAD_DIM)
    hw = MLA_HEADS * HEAD_PAD
    wuq_p = jnp.pad(w_uq_f, ((0, 0), (0, 0), (0, HEAD_PAD - QK_NOPE_DIM - QK_ROPE_DIM))).reshape(Q_LORA_RANK, hw)
    wk_p = jnp.pad(w_ukv_f[:, :, :QK_NOPE_DIM], ((0, 0), (0, 0), (0, HEAD_PAD - QK_NOPE_DIM))).reshape(KV_LORA_RANK, hw)
    wv = w_ukv_f[:, :, QK_NOPE_DIM:].reshape(KV_LORA_RANK, MLA_WIDTH)
    rc, rsa, rsb = _rope_tables(s)
    g_in = norm_in.reshape(1, -1)
    g_q = q_norm.reshape(1, -1)
    g_kv = kv_norm.reshape(1, -1)
    g_f = norm_final.reshape(1, -1)
    ps = pool_scale.reshape(1, -1)
    pw_bf = pool_w.astype(BF16)

    hn, zq, zkv, zkr, gattn, u, gpool, gmerge = _inproj_fwd(x2, g_in, w_in_t, t_row)
    q, k, v, q_t, v_t = _qkv_fwd(zq, zkv, zkr, g_q, g_kv, wuq_p, wk_p, wv, rc, rsa, rsb, t_row)
    o, lse, (w_ba_all, w_bp_all, w_out_all) = _attn_fwd(q_t, k, v_t, local[3:], t_att)
    w_out_f = w_out_all.reshape(D_MODEL, D_MODEL)

    (do, delta, dgattn, dgpool, dgmerge, ddc, dh, sq_err, d_w_out, d_w_ba, d_w_bp, d_pool_w, d_pool_scale,
     d_norm_final) = _mid(o, gattn, u, gpool, gmerge, x2, tgt, pw_bf, ps, w_ba_all, w_bp_all, w_out_f, g_f, t_row)

    late_grads = [d_w_ba, d_w_bp, d_w_out.reshape(N_CHIPS, 256, D_MODEL)]
    dq, dk_t, dv_t, (g_w_ba, g_w_bp, g_w_out) = _attn_bwd(q, q_t, k, v, do, lse, delta, late_grads, t_att)
    dzq, dzkv, dzkr, d_wuq_p, d_wk_p, d_wv, d_q_norm, d_kv_norm = _qkv_bwd(
        dq, dk_t, dv_t, zq, zkv, g_q, g_kv, wuq_p, wk_p, wv, rc, rsa, rsb, t_row)
    grad_x, d_norm_in, dz_sh = _inproj_bwd_x(dzq, dzkv, dzkr, dgattn, ddc, dgpool, dgmerge, x2, dh, g_in, w_in_t,
                                             t_row)

    d_w_uq = d_wuq_p.reshape(Q_LORA_RANK, MLA_HEADS, HEAD_PAD)[:, :, :QK_NOPE_DIM + QK_ROPE_DIM]
    d_w_ukv = jnp.concatenate([d_wk_p.reshape(KV_LORA_RANK, MLA_HEADS, HEAD_PAD)[:, :, :QK_NOPE_DIM],
                               d_wv.reshape(KV_LORA_RANK, MLA_HEADS, V_HEAD_DIM)], axis=2)
    small = dict(norm_in=d_norm_in, q_norm=d_q_norm, kv_norm=d_kv_norm, pool_scale=d_pool_scale,
                 norm_final=d_norm_final, pool_w=d_pool_w, sq_err=sq_err)
    gs = _pack_rows([small[n] for n, _ in SMALL_SHAPES], SMALL_ROWS, F32)
    cx, cy = lax.axis_index("x"), lax.axis_index("y")
    order = jnp.stack([2 * ox + oy for ox, oy in _other_chips(cx, cy)] + [2 * cx + cy]).astype(jnp.int32)
    g_w_in_t, g_w_uq, g_w_ukv, g_small = _inproj_bwd_w(
        order, dz_sh, hn, d_w_uq.reshape(N_CHIPS, 96, 768).astype(BF16),
        d_w_ukv.reshape(N_CHIPS, 64, 1024).astype(BF16), gs, 4 * t_row)
    (g_norm_in, g_q_norm, g_kv_norm, g_pool_scale, g_norm_final, g_pool_w,
     sq_err_all) = _unpack_rows(g_small, SMALL_SHAPES)
    g_w_uq = g_w_uq.reshape(w_uq.shape)
    g_w_ukv = g_w_ukv.reshape(w_ukv.shape)

    dl_w_in, nm_w_in, nv_w_in = (a.T for a in _adamw_tiled(w_in.T, g_w_in_t, m_w_in.T, v_w_in.T, 152))

    def two_d(a):
        return a.reshape(1, -1) if a.ndim == 1 else a

    names = ["norm_in", "q_norm", "w_uq", "kv_norm", "w_ukv", "pool_w", "pool_scale", "w_branch_attn",
             "w_branch_pool", "w_out", "norm_final"]
    ws = dict(norm_in=norm_in, q_norm=q_norm, w_uq=w_uq, kv_norm=kv_norm, w_ukv=w_ukv, pool_w=pool_w,
              pool_scale=pool_scale, w_branch_attn=w_branch_attn, w_branch_pool=w_branch_pool, w_out=w_out,
              norm_final=norm_final)
    gsd = dict(norm_in=g_norm_in, q_norm=g_q_norm, w_uq=g_w_uq, kv_norm=g_kv_norm, w_ukv=g_w_ukv, pool_w=g_pool_w,
               pool_scale=g_pool_scale, w_branch_attn=g_w_ba, w_branch_pool=g_w_bp, w_out=g_w_out,
               norm_final=g_norm_final)
    msd = dict(norm_in=m_norm_in, q_norm=m_q_norm, w_uq=m_w_uq, kv_norm=m_kv_norm, w_ukv=m_w_ukv, pool_w=m_pool_w,
               pool_scale=m_pool_scale, w_branch_attn=m_w_branch_attn, w_branch_pool=m_w_branch_pool, w_out=m_w_out,
               norm_final=m_norm_final)
    vsd = dict(norm_in=v_norm_in, q_norm=v_q_norm, w_uq=v_w_uq, kv_norm=v_kv_norm, w_ukv=v_w_ukv, pool_w=v_pool_w,
               pool_scale=v_pool_scale, w_branch_attn=v_w_branch_attn, w_branch_pool=v_w_branch_pool, w_out=v_w_out,
               norm_final=v_norm_final)
    dls, nms, nvs = _adamw_many([two_d(ws[n]) for n in names], [two_d(gsd[n]) for n in names],
                                [two_d(msd[n]) for n in names], [two_d(vsd[n]) for n in names])

    grads = dict(gsd)
    grads["w_in"] = g_w_in_t.T
    delta_w = {n: d.reshape(ws[n].shape) for n, d in zip(names, dls)}
    new_m = {n: d.reshape(ws[n].shape) for n, d in zip(names, nms)}
    new_v = {n: d.reshape(ws[n].shape) for n, d in zip(names, nvs)}
    delta_w["w_in"], new_m["w_in"], new_v["w_in"] = dl_w_in, nm_w_in, nv_w_in
    ws["w_in"] = w_in

    order = ["norm_in", "w_in", "q_norm", "w_uq", "kv_norm", "w_ukv", "pool_w", "pool_scale", "w_branch_attn",
             "w_branch_pool", "w_out", "norm_final"]
    loss = 0.5 * jnp.sum(sq_err_all) / D_MODEL
    return (loss, grad_x.reshape(x.shape),
            *[grads[n].reshape(ws[n].shape) for n in order],
            *[delta_w[n] for n in order], *[new_m[n] for n in order], *[new_v[n] for n in order])
```

```python
import functools

import jax
import jax.numpy as jnp
import numpy as np
from jax import lax
from jax.experimental import pallas as pl
from jax.experimental.pallas import tpu as pltpu

F32 = jnp.float32
BF16 = jnp.bfloat16
MESH = pl.DeviceIdType.MESH

D_MODEL = 1024
CHUNK = 64
MLA_HEADS = 8
QK_NOPE_DIM = 64
QK_ROPE_DIM = 32
V_HEAD_DIM = 64
Q_LORA_RANK = 384
KV_LORA_RANK = 256
MLA_WIDTH = MLA_HEADS * V_HEAD_DIM
ROPE_THETA = 10000.0
POOL_WINDOWS = (2, 4, 8, 16)
POOL_WIDTH = 512
POOL_GROUP_DIM = 128
BRANCH_COLS = D_MODEL // 4
POOL_HALO = 16
EPS = 1e-6
IN_TOTAL = 4256
HEAD_PAD = 128
ATT_SCALE = (QK_NOPE_DIM + QK_ROPE_DIM) ** -0.5
ATT_SCALE_LOG2E = ATT_SCALE * 1.4426950408889634

ADAM_LR = 0.001
ADAM_B1 = 0.9
ADAM_B2 = 0.999
ADAM_EPS = 1e-08
ADAM_WD = 0.01
ADAM_STEP = 10

N_CHIPS = 4
N_DEV = 8
LANES = 128
VMEM_LIMIT = 60 * 1024 * 1024

IN_SEGMENTS = ((384, 384), (256, 256), (32, HEAD_PAD), (512, 512), (512, 512), (512, 512), (2048, 2048))
SHARD_COLS = IN_TOTAL // N_CHIPS
ZQ_COLS = slice(0, 384)
ZKV_COLS = slice(384, 640)
ZKR_TILE = slice(640, 768)


def _shard_pieces():
    bounds, off = [], 0
    for w, _ in IN_SEGMENTS:
        bounds.append((off, off + w))
        off += w
    out = []
    for j in range(N_CHIPS):
        lo, hi = SHARD_COLS * j, SHARD_COLS * (j + 1)
        out.append([(i, max(lo, a) - a, min(hi, b) - a, max(lo, a) - lo)
                    for i, (a, b) in enumerate(bounds) if max(lo, a) < min(hi, b)])
    return out


SHARD_PIECES = _shard_pieces()


def _segment(z_blocks, seg):
    parts = [z_blocks[j][:, col:col + hi - lo]
             for j, pieces in enumerate(SHARD_PIECES) for sg, lo, hi, col in pieces if sg == seg]
    return parts[0] if len(parts) == 1 else jnp.concatenate(parts, axis=1)

COMM_PARAMS = (
    ("w_in", SHARD_COLS, D_MODEL, 1, 512),
    ("w_uq", 96, 768, 0, 48),
    ("w_ukv", 64, 1024, 0, 32),
    ("w_branch_attn", 512, 256, 0, 256),
    ("w_branch_pool", 512, 256, 0, 256),
    ("w_out", 256, 1024, 0, 128),
)

SMALL_SHAPES = (
    ("norm_in", (1024,)),
    ("q_norm", (384,)),
    ("kv_norm", (256,)),
    ("pool_scale", (512,)),
    ("norm_final", (1024,)),
    ("pool_w", (4, 128, 128)),
    ("sq_err", (8, 128)),
)
SMALL_ELEMS = sum(int(np.prod(s)) for _, s in SMALL_SHAPES)
SMALL_ROWS = -(-SMALL_ELEMS // (LANES * 8)) * 8


def _dot(a, b):
    return jnp.dot(a, b, preferred_element_type=F32)


def _dot_nt(a, b):
    return lax.dot_general(a, b, (((1,), (1,)), ((), ())), preferred_element_type=F32)


def _dot_tn(a, b):
    return lax.dot_general(a, b, (((0,), (0,)), ((), ())), preferred_element_type=F32)


def _sigmoid(x):
    return 1.0 / (1.0 + jnp.exp(-x))


def _colsum(x):
    return jnp.sum(x, axis=0, keepdims=True)


def _rms_fwd(x, g):
    r = lax.rsqrt(jnp.mean(x * x, axis=-1, keepdims=True) + EPS)
    xhat = x * r
    return xhat * g, xhat, r


def _rms_bwd(dy, xhat, r, g):
    dxhat = dy * g
    return r * (dxhat - xhat * jnp.mean(dxhat * xhat, axis=-1, keepdims=True))


def _rope(v, c, sa, sb):
    return v * c + pltpu.roll(v, 112, 1) * sa + pltpu.roll(v, 16, 1) * sb


def _unrope(d, c, sa, sb):
    return d * c + pltpu.roll(d * sa, 16, 1) + pltpu.roll(d * sb, 112, 1)


def _row_spec(tm, n):
    return pl.BlockSpec((tm, n), lambda i: (i, 0))


def _full_spec(shape):
    nd = len(shape)
    return pl.BlockSpec(shape, lambda i: (0,) * nd)


def _tiles(s):
    t_att = 512 if s >= 2048 else 128
    t_row = 256 if s >= 1024 else 128
    return t_att, t_row


def _inproj_fwd(order, x, norm_in, early_shards, tm):
    s = x.shape[0]
    n_tiles = s // tm
    gat = _Gather(COMM_PARAMS[:3])
    n_w = len(gat.params)
    arrival = (1, 2, 0)

    def body(order_ref, x_ref, g_ref, *rest):
        w_loc, (hn_ref, z_ref), w_all = rest[:n_w], rest[n_w:n_w + 2], rest[n_w + 2:2 * n_w + 2]
        w_vmem, hn_all, w_sem = rest[2 * n_w + 2:2 * n_w + 5]
        gat.bind(w_loc, w_all, rest[2 * n_w + 5:])
        ph, i = pl.program_id(0), pl.program_id(1)
        pl.when(jnp.logical_and(ph == 0, i == 0))(gat.start)

        @pl.when(jnp.logical_and(ph == 0, i == 0))
        def _():
            cp = pltpu.make_async_copy(w_loc[0], w_vmem, w_sem)
            cp.start()
            cp.wait()

        for f in range(3):
            @pl.when(jnp.logical_and(ph == f + 1, i == 0))
            def _(f=f):
                gat.relay_one(0, arrival[f])
                gat.await_one(0, arrival[f])
                cp = pltpu.make_async_copy(w_all[0].at[order_ref[ph]], w_vmem, w_sem)
                cp.start()
                cp.wait()

        rows = pl.ds(pl.multiple_of(i * tm, tm), tm)

        @pl.when(ph == 0)
        def _():
            hn, _, _ = _rms_fwd(x_ref[...], g_ref[...])
            hn = hn.astype(BF16)
            hn_ref[...] = hn
            hn_all[rows, :] = hn

        z_ref[0] = _dot_nt(hn_all[rows, :], w_vmem[...])

        @pl.when(jnp.logical_and(ph == N_CHIPS - 1, i == n_tiles - 1))
        def _():
            for p in range(1, n_w):
                for j in range(3):
                    gat.relay_one(p, j)
            for p in range(1, n_w):
                for j in range(3):
                    gat.await_one(p, j)
            gat.wait_sends()

    def tile_in_phase0(ph, i, order):
        return (jnp.where(ph == 0, i, n_tiles - 1), 0)

    any_spec = pl.BlockSpec(memory_space=pl.ANY)
    grid_spec = pltpu.PrefetchScalarGridSpec(
        num_scalar_prefetch=1,
        grid=(N_CHIPS, n_tiles),
        in_specs=[pl.BlockSpec((tm, D_MODEL), tile_in_phase0),
                  pl.BlockSpec((1, D_MODEL), lambda ph, i, order: (0, 0))] + [any_spec] * n_w,
        out_specs=[pl.BlockSpec((tm, D_MODEL), tile_in_phase0),
                   pl.BlockSpec((1, tm, SHARD_COLS), lambda ph, i, order: (order[ph], i, 0))] + [any_spec] * n_w,
        scratch_shapes=[pltpu.VMEM((SHARD_COLS, D_MODEL), BF16), pltpu.VMEM((s, D_MODEL), BF16),
                        pltpu.SemaphoreType.DMA] + gat.scratch,
    )
    out = pl.pallas_call(
        body,
        name="inproj_fwd",
        grid_spec=grid_spec,
        out_shape=[jax.ShapeDtypeStruct((s, D_MODEL), BF16), jax.ShapeDtypeStruct((N_CHIPS, s, SHARD_COLS), F32)]
        + gat.out_shape,
        compiler_params=pltpu.CompilerParams(dimension_semantics=("arbitrary", "arbitrary"),
                                             vmem_limit_bytes=VMEM_LIMIT),
    )(order, x, norm_in, *early_shards)
    return out[0], out[1], out[2:]


def _qkv_fwd(z_sh, q_norm, kv_norm, wuq_p, wk_p, wv, rc, rsa, rsb, tm):
    s = z_sh.shape[1]
    hw = MLA_HEADS * HEAD_PAD

    def body(z_ref, gq_ref, gkv_ref, wuq_ref, wk_ref, wv_ref, c_ref, sa_ref, sb_ref,
             q_ref, k_ref, v_ref, qt_ref, vt_ref):
        c, sa, sb = c_ref[...], sa_ref[...], sb_ref[...]
        z0 = z_ref[0]
        cq, _, _ = _rms_fwd(z0[:, ZQ_COLS], gq_ref[...])
        qf = _dot(cq.astype(BF16), wuq_ref[...])
        ckv, _, _ = _rms_fwd(z0[:, ZKV_COLS], gkv_ref[...])
        ckv = ckv.astype(BF16)
        kn = _dot(ckv, wk_ref[...])
        lane = lax.broadcasted_iota(jnp.int32, (tm, HEAD_PAD), 1)
        zkr = jnp.where(lane < QK_ROPE_DIM, z0[:, ZKR_TILE], 0.0)
        kr = _rope(pltpu.roll(zkr, 64, 1), c, sa, sb)
        for h in range(MLA_HEADS):
            cols = slice(h * HEAD_PAD, (h + 1) * HEAD_PAD)
            qh = _rope(qf[:, cols], c, sa, sb)
            q_ref[:, cols] = qh.astype(BF16)
            qt_ref[cols, :] = qh.T.astype(BF16)
            k_ref[:, cols] = (kn[:, cols] + kr).astype(BF16)
        vf = _dot(ckv, wv_ref[...])
        v_ref[...] = vf.astype(BF16)
        vt_ref[...] = vf.T.astype(BF16)

    return pl.pallas_call(
        body,
        name="qkv_fwd",
        grid=(s // tm,),
        in_specs=[
            pl.BlockSpec((1, tm, SHARD_COLS), lambda i: (0, i, 0)),
            _full_spec((1, Q_LORA_RANK)), _full_spec((1, KV_LORA_RANK)),
            _full_spec((Q_LORA_RANK, hw)), _full_spec((KV_LORA_RANK, hw)), _full_spec((KV_LORA_RANK, MLA_WIDTH)),
            _row_spec(tm, HEAD_PAD), _row_spec(tm, HEAD_PAD), _row_spec(tm, HEAD_PAD),
        ],
        out_specs=[_row_spec(tm, hw), _row_spec(tm, hw), _row_spec(tm, MLA_WIDTH),
                   pl.BlockSpec((hw, tm), lambda i: (0, i)), pl.BlockSpec((MLA_WIDTH, tm), lambda i: (0, i))],
        out_shape=[jax.ShapeDtypeStruct((s, hw), BF16), jax.ShapeDtypeStruct((s, hw), BF16),
                   jax.ShapeDtypeStruct((s, MLA_WIDTH), BF16),
                   jax.ShapeDtypeStruct((hw, s), BF16), jax.ShapeDtypeStruct((MLA_WIDTH, s), BF16)],
        compiler_params=pltpu.CompilerParams(dimension_semantics=("parallel",), vmem_limit_bytes=VMEM_LIMIT),
    )(z_sh, q_norm, kv_norm, wuq_p, wk_p, wv, rc, rsa, rsb)


def _chunk_mask(t, keys_on_rows):
    rows = lax.broadcasted_iota(jnp.int32, (t, t), 0) // CHUNK
    cols = lax.broadcasted_iota(jnp.int32, (t, t), 1) // CHUNK
    return rows <= cols if keys_on_rows else cols <= rows


def _attn_fwd(q_t, k, v_t, late_shards, t):
    s = k.shape[0]
    pairs = MLA_HEADS // 2
    n_q = s // t
    gat = _Gather(COMM_PARAMS[3:])
    n_w = len(gat.params)

    def body(qt_ref, k_ref, k2_ref, vt_ref, *rest):
        w_in, (o_ref, lse_ref), w_out = rest[:n_w], rest[n_w:n_w + 2], rest[n_w + 2:2 * n_w + 2]
        gat.bind(w_in, w_out, rest[2 * n_w + 2:])
        i = pl.program_id(1)
        step_no = pl.program_id(0) * n_q + i
        pl.when(step_no == 0)(gat.start)
        pl.when(step_no == n_q)(gat.relay)
        mask = _chunk_mask(t, True)
        qcs = [slice(hh * HEAD_PAD, (hh + 1) * HEAD_PAD) for hh in range(2)]
        vcs = [slice(hh * V_HEAD_DIM, (hh + 1) * V_HEAD_DIM) for hh in range(2)]
        qts = [qt_ref[qc, :] for qc in qcs]

        def step(j, carry, masked):
            keys = pl.ds(pl.multiple_of(j * t, t), t)
            out = []
            for hh in range(2):
                m, l, acc = carry[hh]
                sc = _dot(k_ref[keys, qcs[hh]], qts[hh])
                if masked:
                    sc = jnp.where(mask, sc, -jnp.inf)
                m_new = jnp.maximum(m, jnp.max(sc, axis=0, keepdims=True))
                alpha = jnp.exp2((m - m_new) * ATT_SCALE_LOG2E)
                p = jnp.exp2((_dot(k2_ref[keys, qcs[hh]], qts[hh]) - m_new) * ATT_SCALE_LOG2E)
                if masked:
                    p = jnp.where(mask, p, 0.0)
                l = alpha * l + jnp.sum(p, axis=0, keepdims=True)
                acc = alpha * acc + _dot(vt_ref[vcs[hh], keys], p.astype(BF16))
                out.append((m_new, l, acc))
            return tuple(out)

        one = (jnp.full((1, t), -jnp.inf, F32), jnp.zeros((1, t), F32), jnp.zeros((V_HEAD_DIM, t), F32))
        carry = lax.fori_loop(0, i, functools.partial(step, masked=False), (one, one))
        carry = step(i, carry, True)
        o_ref[...] = jnp.concatenate([carry[hh][2] / carry[hh][1] for hh in range(2)], axis=0).T
        for hh in range(2):
            m, l, _ = carry[hh]
            lse_ref[:, qcs[hh]] = jnp.broadcast_to(m * ATT_SCALE_LOG2E + jnp.log2(l), (HEAD_PAD, t)).T
        pl.when(step_no == pairs * n_q - 1)(gat.finish)

    any_spec = pl.BlockSpec(memory_space=pl.ANY)
    out = pl.pallas_call(
        body,
        name="attn_fwd",
        grid=(pairs, n_q),
        in_specs=[
            pl.BlockSpec((2 * HEAD_PAD, t), lambda p, i: (p, i)),
            pl.BlockSpec((s, 2 * HEAD_PAD), lambda p, i: (0, p)),
            pl.BlockSpec((s, 2 * HEAD_PAD), lambda p, i: (0, p)),
            pl.BlockSpec((2 * V_HEAD_DIM, s), lambda p, i: (p, 0)),
        ] + [any_spec] * n_w,
        out_specs=[
            pl.BlockSpec((t, 2 * V_HEAD_DIM), lambda p, i: (i, p)),
            pl.BlockSpec((t, 2 * HEAD_PAD), lambda p, i: (i, p)),
        ] + [any_spec] * n_w,
        out_shape=[jax.ShapeDtypeStruct((s, MLA_WIDTH), F32), jax.ShapeDtypeStruct((s, MLA_HEADS * HEAD_PAD), F32)]
        + gat.out_shape,
        scratch_shapes=gat.scratch,
        compiler_params=pltpu.CompilerParams(dimension_semantics=("arbitrary", "arbitrary"),
                                             vmem_limit_bytes=VMEM_LIMIT),
    )(q_t, k, k, v_t, *late_shards)
    return out[0], out[1], out[2:]


def _mid(o, z_sh, x, target, pool_w, pool_scale, w_ba, w_bp, w_out, norm_final, tm):
    s = x.shape[0]
    n_tiles = s // tm
    halo_per_tile = tm // POOL_HALO

    def body(o_ref, z0_ref, z1_ref, z1h_ref, z2_ref, z3_ref, x_ref, t_ref, pw_ref, ps_ref, wba_ref, wbp_ref,
             wout_ref, gf_ref,
             do_ref, dl_ref, dga_ref, dgp_ref, dgm_ref, ddc_ref, dh_ref,
             loss_ref, dwout_out, dwba_out, dwbp_out, dpw_ref, dps_ref, dgf_ref,
             ubuf, dwout_ref, dwba_ref, dwbp_ref):
        i = pl.program_id(0)

        @pl.when(i == 0)
        def _():
            loss_ref[...] = jnp.zeros_like(loss_ref)
            dwout_ref[...] = jnp.zeros_like(dwout_ref)
            dwba_ref[...] = jnp.zeros_like(dwba_ref)
            dwbp_ref[...] = jnp.zeros_like(dwbp_ref)
            dpw_ref[...] = jnp.zeros_like(dpw_ref)
            dps_ref[...] = jnp.zeros_like(dps_ref)
            dgf_ref[...] = jnp.zeros_like(dgf_ref)

        zs = [z0_ref[0], z1_ref[0], z2_ref[0], z3_ref[0]]
        o = o_ref[...]
        ga = _segment(zs, 3)
        sga = _sigmoid(ga)
        silu_a = ga * sga
        y_attn = (o * silu_a).astype(BF16)

        ubuf[0:POOL_HALO, :] = jnp.where(i > 0, _segment([None, z1h_ref[0]], 4), 0.0)
        ubuf[POOL_HALO:, :] = _segment(zs, 4)
        row = lax.broadcasted_iota(jnp.int32, (tm, POOL_GROUP_DIM), 0) + i * tm
        ps = ps_ref[...]
        gp = _segment(zs, 5)
        sgp = _sigmoid(gp)
        silu_p = gp * sgp
        d_bf, dm, inv_cnt = [], [], []
        for g, w in enumerate(POOL_WINDOWS):
            cols = slice(g * POOL_GROUP_DIM, (g + 1) * POOL_GROUP_DIM)
            wsum = ubuf[POOL_HALO:, cols]
            for kk in range(1, w):
                wsum = wsum + ubuf[POOL_HALO - kk:POOL_HALO - kk + tm, cols]
            inv = 1.0 / jnp.minimum(row + 1, w).astype(F32)
            dg = (wsum * inv - ubuf[POOL_HALO:, cols]).astype(BF16)
            d_bf.append(dg)
            inv_cnt.append(inv)
            dm.append(_dot(dg, pw_ref[g]))
        dm = jnp.concatenate(dm, axis=1)
        yp = dm * ps
        y_pool = (yp * silu_p).astype(BF16)

        a = jnp.concatenate([_dot(y_attn, wba_ref[j]) for j in range(N_CHIPS)], axis=1)
        p = jnp.concatenate([_dot(y_pool, wbp_ref[j]) for j in range(N_CHIPS)], axis=1)
        gm = _segment(zs, 6)
        gate_a = _sigmoid(gm[:, :D_MODEL])
        gate_p = _sigmoid(gm[:, D_MODEL:])
        merged = (gate_a * a + gate_p * p).astype(BF16)
        h = x_ref[...] + _dot(merged, wout_ref[...])
        gf = gf_ref[...]
        y, xhat, r = _rms_fwd(h, gf)
        err = y - t_ref[...]
        e2 = err * err
        e2 = jnp.sum(e2.reshape(tm // 8, 8, D_MODEL), axis=0)
        acc = e2[:, 0:LANES]
        for cidx in range(1, D_MODEL // LANES):
            acc = acc + e2[:, cidx * LANES:(cidx + 1) * LANES]
        loss_ref[...] += acc

        dy = err * (1.0 / D_MODEL)
        dgf_ref[...] += _colsum(dy * xhat)
        dh = _rms_bwd(dy, xhat, r, gf)
        dh_ref[...] = dh
        dh_bf = dh.astype(BF16)
        dwout_ref[...] += _dot_tn(merged, dh_bf)
        dmerged = _dot_nt(dh_bf, wout_ref[...])
        da = (dmerged * gate_a).astype(BF16)
        dp = (dmerged * gate_p).astype(BF16)
        dgm_ref[:, :D_MODEL] = (dmerged * a * gate_a * (1.0 - gate_a)).astype(BF16)
        dgm_ref[:, D_MODEL:] = (dmerged * p * gate_p * (1.0 - gate_p)).astype(BF16)
        dy_attn = dy_pool = None
        for j in range(N_CHIPS):
            cols = slice(j * BRANCH_COLS, (j + 1) * BRANCH_COLS)
            dwba_ref[j] += _dot_tn(y_attn, da[:, cols])
            dwbp_ref[j] += _dot_tn(y_pool, dp[:, cols])
            pa = _dot_nt(da[:, cols], wba_ref[j])
            pp = _dot_nt(dp[:, cols], wbp_ref[j])
            dy_attn = pa if dy_attn is None else dy_attn + pa
            dy_pool = pp if dy_pool is None else dy_pool + pp

        do = dy_attn * silu_a
        do_ref[...] = do
        dga_ref[...] = (dy_attn * o * (sga * (1.0 + ga * (1.0 - sga)))).astype(BF16)
        doo = do * o
        for hd in range(MLA_HEADS):
            dl = jnp.sum(doo[:, hd * V_HEAD_DIM:(hd + 1) * V_HEAD_DIM], axis=1, keepdims=True)
            dl_ref[:, hd * HEAD_PAD:(hd + 1) * HEAD_PAD] = jnp.broadcast_to(dl, (tm, HEAD_PAD))

        dyp = dy_pool * silu_p
        dgp_ref[...] = (dy_pool * yp * (sgp * (1.0 + gp * (1.0 - sgp)))).astype(BF16)
        dps_ref[...] += _colsum(dyp * dm)
        dmm = (dyp * ps).astype(BF16)
        for g in range(len(POOL_WINDOWS)):
            cols = slice(g * POOL_GROUP_DIM, (g + 1) * POOL_GROUP_DIM)
            dpw_ref[g] += _dot_tn(d_bf[g], dmm[:, cols])
            ddc_ref[:, cols] = _dot_nt(dmm[:, cols], pw_ref[g]) * inv_cnt[g]

        @pl.when(i == n_tiles - 1)
        def _():
            dwout_out[...] = dwout_ref[...].astype(BF16)
            dwba_out[...] = dwba_ref[...].astype(BF16)
            dwbp_out[...] = dwbp_ref[...].astype(BF16)

    row_in = lambda n: _row_spec(tm, n)
    in_specs = [
        row_in(MLA_WIDTH),
        pl.BlockSpec((1, tm, SHARD_COLS), lambda i: (0, i, 0)), pl.BlockSpec((1, tm, SHARD_COLS), lambda i: (1, i, 0)),
        pl.BlockSpec((1, POOL_HALO, SHARD_COLS), lambda i: (1, jnp.maximum(i * halo_per_tile - 1, 0), 0)),
        pl.BlockSpec((1, tm, SHARD_COLS), lambda i: (2, i, 0)), pl.BlockSpec((1, tm, SHARD_COLS), lambda i: (3, i, 0)),
        row_in(D_MODEL), row_in(D_MODEL),
        _full_spec((4, POOL_GROUP_DIM, POOL_GROUP_DIM)), _full_spec((1, POOL_WIDTH)),
        _full_spec((N_CHIPS, MLA_WIDTH, BRANCH_COLS)), _full_spec((N_CHIPS, POOL_WIDTH, BRANCH_COLS)),
        _full_spec((D_MODEL, D_MODEL)), _full_spec((1, D_MODEL)),
    ]
    out_shape = [
        jax.ShapeDtypeStruct((s, MLA_WIDTH), F32),
        jax.ShapeDtypeStruct((s, MLA_HEADS * HEAD_PAD), F32),
        jax.ShapeDtypeStruct((s, MLA_WIDTH), BF16),
        jax.ShapeDtypeStruct((s, POOL_WIDTH), BF16),
        jax.ShapeDtypeStruct((s, 2 * D_MODEL), BF16),
        jax.ShapeDtypeStruct((s, POOL_WIDTH), F32),
        jax.ShapeDtypeStruct((s, D_MODEL), F32),
        jax.ShapeDtypeStruct((8, LANES), F32),
        jax.ShapeDtypeStruct((D_MODEL, D_MODEL), BF16),
        jax.ShapeDtypeStruct((N_CHIPS, MLA_WIDTH, BRANCH_COLS), BF16),
        jax.ShapeDtypeStruct((N_CHIPS, POOL_WIDTH, BRANCH_COLS), BF16),
        jax.ShapeDtypeStruct((4, POOL_GROUP_DIM, POOL_GROUP_DIM), F32),
        jax.ShapeDtypeStruct((1, POOL_WIDTH), F32),
        jax.ShapeDtypeStruct((1, D_MODEL), F32),
    ]
    out_specs = [
        row_in(MLA_WIDTH), row_in(MLA_HEADS * HEAD_PAD), row_in(MLA_WIDTH), row_in(POOL_WIDTH),
        row_in(2 * D_MODEL), row_in(POOL_WIDTH), row_in(D_MODEL),
        _full_spec((8, LANES)), _full_spec((D_MODEL, D_MODEL)), _full_spec((N_CHIPS, MLA_WIDTH, BRANCH_COLS)),
        _full_spec((N_CHIPS, POOL_WIDTH, BRANCH_COLS)), _full_spec((4, POOL_GROUP_DIM, POOL_GROUP_DIM)),
        _full_spec((1, POOL_WIDTH)), _full_spec((1, D_MODEL)),
    ]
    return pl.pallas_call(
        body,
        name="mid",
        grid=(n_tiles,),
        in_specs=in_specs,
        out_specs=out_specs,
        out_shape=out_shape,
        scratch_shapes=[
            pltpu.VMEM((tm + POOL_HALO, POOL_WIDTH), F32),
            pltpu.VMEM((D_MODEL, D_MODEL), F32),
            pltpu.VMEM((N_CHIPS, MLA_WIDTH, BRANCH_COLS), F32),
            pltpu.VMEM((N_CHIPS, POOL_WIDTH, BRANCH_COLS), F32),
        ],
        compiler_params=pltpu.CompilerParams(dimension_semantics=("arbitrary",), vmem_limit_bytes=VMEM_LIMIT),
    )(o, z_sh, z_sh, z_sh, z_sh, z_sh, x, target, pool_w, pool_scale, w_ba, w_bp, w_out, norm_final)


def _attn_bwd(q, q_t, k, v, do, lse, delta, late_grads, t):
    s = q.shape[0]
    pairs = MLA_HEADS // 2
    n_q = s // t
    red = _Reduce(COMM_PARAMS[3:])
    n_w = len(red.params)

    def body(q_ref, qt_ref, do_ref, lse_ref, dl_ref, k_ref, v_ref, *rest):
        g_in, (dq_ref, dk_ref, dv_ref), g_out = rest[:n_w], rest[n_w:n_w + 3], rest[n_w + 3:2 * n_w + 3]
        red.bind(g_in, g_out, rest[2 * n_w + 3:])
        i = pl.program_id(1)
        step_no = pl.program_id(0) * n_q + i
        pl.when(step_no == 0)(red.start)
        pl.when(step_no == n_q)(red.exchange)

        @pl.when(i == 0)
        def _():
            dk_ref[...] = jnp.zeros_like(dk_ref)
            dv_ref[...] = jnp.zeros_like(dv_ref)

        mask = _chunk_mask(t, False)
        qcs = [slice(hh * HEAD_PAD, (hh + 1) * HEAD_PAD) for hh in range(2)]
        vcs = [slice(hh * V_HEAD_DIM, (hh + 1) * V_HEAD_DIM) for hh in range(2)]
        qhs = [q_ref[:, qc] for qc in qcs]
        qts = [qt_ref[qc, :] for qc in qcs]
        dohs = [do_ref[:, vc].astype(BF16) for vc in vcs]
        do_t = do_ref[...].T.astype(BF16)
        dots = [do_t[vc, :] for vc in vcs]
        lses = [lse_ref[:, hh * HEAD_PAD:hh * HEAD_PAD + 1] for hh in range(2)]
        dls = [dl_ref[:, hh * HEAD_PAD:hh * HEAD_PAD + 1] for hh in range(2)]

        def step(j, dqs, masked):
            keys = pl.ds(pl.multiple_of(j * t, t), t)
            out = []
            for hh in range(2):
                kj = k_ref[keys, qcs[hh]]
                vj = v_ref[keys, vcs[hh]]
                p = jnp.exp2(_dot_nt(qhs[hh], kj) * ATT_SCALE_LOG2E - lses[hh])
                if masked:
                    p = jnp.where(mask, p, 0.0)
                ds = (p * (_dot_nt(dohs[hh], vj) - dls[hh])).astype(BF16)
                dv_ref[vcs[hh], keys] += _dot(dots[hh], p.astype(BF16))
                dk_ref[qcs[hh], keys] += _dot(qts[hh], ds) * ATT_SCALE
                out.append(dqs[hh] + _dot(ds, kj))
            return tuple(out)

        zero = jnp.zeros((t, HEAD_PAD), F32)
        dqs = lax.fori_loop(0, i, functools.partial(step, masked=False), (zero, zero))
        dqs = step(i, dqs, True)
        for hh in range(2):
            dq_ref[:, qcs[hh]] = dqs[hh] * ATT_SCALE
        pl.when(step_no == pairs * n_q - 1)(red.finish)

    hw = MLA_HEADS * HEAD_PAD
    any_spec = pl.BlockSpec(memory_space=pl.ANY)
    out = pl.pallas_call(
        body,
        name="attn_bwd",
        grid=(pairs, n_q),
        in_specs=[
            pl.BlockSpec((t, 2 * HEAD_PAD), lambda p, i: (i, p)),
            pl.BlockSpec((2 * HEAD_PAD, t), lambda p, i: (p, i)),
            pl.BlockSpec((t, 2 * V_HEAD_DIM), lambda p, i: (i, p)),
            pl.BlockSpec((t, 2 * HEAD_PAD), lambda p, i: (i, p)),
            pl.BlockSpec((t, 2 * HEAD_PAD), lambda p, i: (i, p)),
            pl.BlockSpec((s, 2 * HEAD_PAD), lambda p, i: (0, p)),
            pl.BlockSpec((s, 2 * V_HEAD_DIM), lambda p, i: (0, p)),
        ] + [any_spec] * n_w,
        out_specs=[
            pl.BlockSpec((t, 2 * HEAD_PAD), lambda p, i: (i, p)),
            pl.BlockSpec((2 * HEAD_PAD, s), lambda p, i: (p, 0)),
            pl.BlockSpec((2 * V_HEAD_DIM, s), lambda p, i: (p, 0)),
        ] + [any_spec] * n_w,
        out_shape=[jax.ShapeDtypeStruct((s, hw), F32), jax.ShapeDtypeStruct((hw, s), F32),
                   jax.ShapeDtypeStruct((MLA_WIDTH, s), F32)] + red.out_shape,
        scratch_shapes=red.scratch,
        compiler_params=pltpu.CompilerParams(dimension_semantics=("arbitrary", "arbitrary"),
                                             vmem_limit_bytes=VMEM_LIMIT),
    )(q, q_t, do, lse, delta, k, v, *late_grads)
    return out[0], out[1], out[2], out[3:]


def _qkv_bwd(dq, dk_t, dv_t, z_sh, q_norm, kv_norm, wuq_p, wk_p, wv, rc, rsa, rsb, tm):
    s = z_sh.shape[1]
    hw = MLA_HEADS * HEAD_PAD

    def body(dq_ref, dk_ref, dv_ref, z_ref, gq_ref, gkv_ref, wuq_ref, wk_ref, wv_ref,
             c_ref, sa_ref, sb_ref,
             dzq_ref, dzkv_ref, dzkr_ref, dwuq_ref, dwk_ref, dwv_ref, dgq_ref, dgkv_ref):
        i = pl.program_id(0)

        @pl.when(i == 0)
        def _():
            dwuq_ref[...] = jnp.zeros_like(dwuq_ref)
            dwk_ref[...] = jnp.zeros_like(dwk_ref)
            dwv_ref[...] = jnp.zeros_like(dwv_ref)
            dgq_ref[...] = jnp.zeros_like(dgq_ref)
            dgkv_ref[...] = jnp.zeros_like(dgkv_ref)

        c, sa, sb = c_ref[...], sa_ref[...], sb_ref[...]
        gq, gkv = gq_ref[...], gkv_ref[...]

        z0 = z_ref[0]
        cq, xq, rq = _rms_fwd(z0[:, ZQ_COLS], gq)
        dqp = jnp.concatenate(
            [_unrope(dq_ref[:, h * HEAD_PAD:(h + 1) * HEAD_PAD], c, sa, sb) for h in range(MLA_HEADS)],
            axis=1).astype(BF16)
        dwuq_ref[...] += _dot_tn(cq.astype(BF16), dqp)
        dcq = _dot_nt(dqp, wuq_ref[...])
        dgq_ref[...] += _colsum(dcq * xq)
        dzq_ref[...] = _rms_bwd(dcq, xq, rq, gq).astype(BF16)

        ckv, xkv, rkv = _rms_fwd(z0[:, ZKV_COLS], gkv)
        ckv = ckv.astype(BF16)
        dkf = dk_ref[...].T
        dk_bf = dkf.astype(BF16)
        dv_bf = dv_ref[...].T.astype(BF16)
        dwk_ref[...] += _dot_tn(ckv, dk_bf)
        dwv_ref[...] += _dot_tn(ckv, dv_bf)
        dckv = _dot_nt(dk_bf, wk_ref[...]) + _dot_nt(dv_bf, wv_ref[...])
        dgkv_ref[...] += _colsum(dckv * xkv)
        dzkv_ref[...] = _rms_bwd(dckv, xkv, rkv, gkv).astype(BF16)

        dkr = dkf[:, 0:HEAD_PAD]
        for h in range(1, MLA_HEADS):
            dkr = dkr + dkf[:, h * HEAD_PAD:(h + 1) * HEAD_PAD]
        dkr = pltpu.roll(_unrope(dkr, c, sa, sb), 64, 1)
        lane = lax.broadcasted_iota(jnp.int32, (tm, HEAD_PAD), 1)
        dzkr_ref[...] = jnp.where(lane < QK_ROPE_DIM, dkr, 0.0).astype(BF16)

    return pl.pallas_call(
        body,
        name="qkv_bwd",
        grid=(s // tm,),
        in_specs=[
            _row_spec(tm, hw), pl.BlockSpec((hw, tm), lambda i: (0, i)), pl.BlockSpec((MLA_WIDTH, tm), lambda i: (0, i)),
            pl.BlockSpec((1, tm, SHARD_COLS), lambda i: (0, i, 0)),
            _full_spec((1, Q_LORA_RANK)), _full_spec((1, KV_LORA_RANK)),
            _full_spec((Q_LORA_RANK, hw)), _full_spec((KV_LORA_RANK, hw)), _full_spec((KV_LORA_RANK, MLA_WIDTH)),
            _row_spec(tm, HEAD_PAD), _row_spec(tm, HEAD_PAD), _row_spec(tm, HEAD_PAD),
        ],
        out_specs=[
            _row_spec(tm, Q_LORA_RANK), _row_spec(tm, KV_LORA_RANK), _row_spec(tm, HEAD_PAD),
            _full_spec((Q_LORA_RANK, hw)), _full_spec((KV_LORA_RANK, hw)), _full_spec((KV_LORA_RANK, MLA_WIDTH)),
            _full_spec((1, Q_LORA_RANK)), _full_spec((1, KV_LORA_RANK)),
        ],
        out_shape=[
            jax.ShapeDtypeStruct((s, Q_LORA_RANK), BF16), jax.ShapeDtypeStruct((s, KV_LORA_RANK), BF16),
            jax.ShapeDtypeStruct((s, HEAD_PAD), BF16),
            jax.ShapeDtypeStruct((Q_LORA_RANK, hw), F32), jax.ShapeDtypeStruct((KV_LORA_RANK, hw), F32),
            jax.ShapeDtypeStruct((KV_LORA_RANK, MLA_WIDTH), F32),
            jax.ShapeDtypeStruct((1, Q_LORA_RANK), F32), jax.ShapeDtypeStruct((1, KV_LORA_RANK), F32),
        ],
        compiler_params=pltpu.CompilerParams(dimension_semantics=("arbitrary",), vmem_limit_bytes=VMEM_LIMIT),
    )(dq, dk_t, dv_t, z_sh, q_norm, kv_norm, wuq_p, wk_p, wv, rc, rsa, rsb)


def _inproj_bwd_x(dzq, dzkv, dzkr, dgattn, ddc, dgpool, dgmerge, x, dh, norm_in, w_in_t, tm):
    s = x.shape[0]
    n_tiles = s // tm
    halo_per_tile = tm // POOL_HALO
    n_halo = s // POOL_HALO
    u_seg = 4

    def body(dzq_ref, dzkv_ref, dzkr_ref, dga_ref, ddc_ref, ddn_ref, dgp_ref, dgm_ref, x_ref, dh_ref,
             g_ref, w_hbm, gx_ref, dgin_ref, dzs_ref, w_vmem, dbuf, sem):
        i = pl.program_id(0)

        @pl.when(i == 0)
        def _():
            cp = pltpu.make_async_copy(w_hbm, w_vmem, sem)
            cp.start()
            dgin_ref[...] = jnp.zeros_like(dgin_ref)
            cp.wait()

        dbuf[0:tm, :] = ddc_ref[...]
        dbuf[tm:, :] = jnp.where(i < n_tiles - 1, ddn_ref[...], 0.0)
        row = lax.broadcasted_iota(jnp.int32, (tm, POOL_GROUP_DIM), 0) + i * tm
        du = []
        for g, w in enumerate(POOL_WINDOWS):
            cols = slice(g * POOL_GROUP_DIM, (g + 1) * POOL_GROUP_DIM)
            fsum = dbuf[0:tm, cols]
            for kk in range(1, w):
                fsum = fsum + dbuf[kk:kk + tm, cols]
            du.append(fsum - dbuf[0:tm, cols] * jnp.minimum(row + 1, w).astype(F32))
        du = jnp.concatenate(du, axis=1).astype(BF16)

        dz = [dzq_ref[...], dzkv_ref[...], dzkr_ref[...], dga_ref[...], du, dgp_ref[...], dgm_ref[...]]
        dhn = None
        for j, pieces in enumerate(SHARD_PIECES):
            parts = [dz[seg][:, lo:hi] for seg, lo, hi, _ in pieces]
            dzj = parts[0] if len(parts) == 1 else jnp.concatenate(parts, axis=1)
            dzs_ref[j] = dzj.T
            part = _dot(dzj, w_vmem[j])
            dhn = part if dhn is None else dhn + part

        g = g_ref[...]
        _, xhat, r = _rms_fwd(x_ref[...], g)
        dgin_ref[...] += _colsum(dhn * xhat)
        gx_ref[...] = dh_ref[...] + _rms_bwd(dhn, xhat, r, g)

    any_spec = pl.BlockSpec(memory_space=pl.ANY)
    seg_w = [wide for _, wide in IN_SEGMENTS]
    return pl.pallas_call(
        body,
        name="inproj_bwd_x",
        grid=(n_tiles,),
        in_specs=[
            _row_spec(tm, seg_w[0]), _row_spec(tm, seg_w[1]), _row_spec(tm, seg_w[2]),
            _row_spec(tm, seg_w[3]), _row_spec(tm, seg_w[u_seg]),
            pl.BlockSpec((POOL_HALO, POOL_WIDTH), lambda i: (jnp.minimum((i + 1) * halo_per_tile, n_halo - 1), 0)),
            _row_spec(tm, seg_w[5]), _row_spec(tm, seg_w[6]),
            _row_spec(tm, D_MODEL), _row_spec(tm, D_MODEL),
            _full_spec((1, D_MODEL)), any_spec,
        ],
        out_specs=[_row_spec(tm, D_MODEL), _full_spec((1, D_MODEL)),
                   pl.BlockSpec((N_CHIPS, SHARD_COLS, tm), lambda i: (0, 0, i))],
        out_shape=[jax.ShapeDtypeStruct((s, D_MODEL), F32), jax.ShapeDtypeStruct((1, D_MODEL), F32),
                   jax.ShapeDtypeStruct((N_CHIPS, SHARD_COLS, s), BF16)],
        scratch_shapes=[
            pltpu.VMEM((N_CHIPS, SHARD_COLS, D_MODEL), BF16),
            pltpu.VMEM((tm + POOL_HALO, POOL_WIDTH), F32),
            pltpu.SemaphoreType.DMA,
        ],
        compiler_params=pltpu.CompilerParams(dimension_semantics=("arbitrary",), vmem_limit_bytes=VMEM_LIMIT),
    )(dzq, dzkv, dzkr, dgattn, ddc, ddc, dgpool, dgmerge, x, dh, norm_in, w_in_t)


def _inproj_bwd_w(order, dz_sh, hn, g_uq, g_ukv, gs, tm):
    s = hn.shape[0]
    n_tiles = s // tm
    mid = n_tiles // 2
    hc = D_MODEL // 2
    red = _Reduce(COMM_PARAMS[1:3])

    def body(order_ref, dz_ref, hn_ref, guq_hbm, gukv_hbm, gs_ref, gw_hbm, guq_out, gukv_out, gsum_ref,
             acc, pm_w, a_w, b_w, r_w, s_buf, w_send, w_recv, w_local, *red_scratch):
        ph, i = pl.program_id(0), pl.program_id(1)
        x, y, c = lax.axis_index("x"), lax.axis_index("y"), lax.axis_index("c")
        k = 2 * x + y
        me, sibling = (x, y, c), (x, y, 1 - c)
        chips = _other_chips(x, y)
        shard_of_phase = [2 * cx + cy for cx, cy in chips] + [k]
        copy = _remote_copier(w_send, w_recv)
        red.bind([guq_hbm, gukv_hbm], [guq_out, gukv_out], red_scratch)
        mine = pl.ds(pl.multiple_of(c * hc, hc), hc)
        theirs = pl.ds(pl.multiple_of((1 - c) * hc, hc), hc)
        flips = [(fx, fy, fc) for fx in (0, 1) for fy in (0, 1) for fc in (0, 1)][1:]

        def to_sibling(f):
            j = shard_of_phase[f]
            return copy(f, pm_w.at[j, 1 - c], a_w.at[j], sibling)

        def pair_sum(f):
            cx, cy = chips[f]
            return copy(4 + f, pm_w.at[shard_of_phase[f], c], b_w.at[f], (cx, cy, c))

        def small(f):
            fx, fy, fc = flips[f - 1]
            peer = (1 - x if fx else x, 1 - y if fy else y, 1 - c if fc else c)
            return copy(7 + f, gs_ref, s_buf.at[f], peer)

        def finished():
            return copy(7, r_w, gw_hbm.at[:, mine], sibling)

        @pl.when(jnp.logical_and(ph == 0, i == 0))
        def _():
            red.start()
            for f in range(1, N_DEV):
                small(f).start()
            s_buf[0] = gs_ref[...]

        part = _dot(dz_ref[0], hn_ref[...])

        @pl.when(i == 0)
        def _():
            acc[...] = part

        @pl.when(i > 0)
        def _():
            acc[...] += part

        for f in range(3):
            @pl.when(jnp.logical_and(ph == f + 1, i == mid))
            def _(f=f):
                j = shard_of_phase[f]
                copy(f, a_w.at[j], a_w.at[j], me).wait_recv()
                pm_w[j, c] = (pm_w[j, c].astype(F32) + a_w[j].astype(F32)).astype(BF16)
                pair_sum(f).start()
                if f == 0:
                    red.exchange()

        for f in range(4):
            @pl.when(jnp.logical_and(ph == f, i == n_tiles - 1))
            def _(f=f):
                j = shard_of_phase[f]
                pm_w[j, 0] = acc[:, :hc].astype(BF16)
                pm_w[j, 1] = acc[:, hc:].astype(BF16)
                to_sibling(f).start()
                if f < 3:
                    return
                copy(3, a_w.at[k], a_w.at[k], me).wait_recv()
                r_w[...] = pm_w[k, c].astype(F32) + a_w[k].astype(F32)
                for g in range(3):
                    copy(4 + g, b_w.at[g], b_w.at[g], me).wait_recv()
                    r_w[...] = r_w[...] + b_w[g].astype(F32)
                store = pltpu.make_async_copy(r_w, gw_hbm.at[:, mine], w_local)
                store.start()
                finished().start()
                red.finish()
                for g in range(1, N_DEV):
                    copy(7 + g, s_buf.at[g], s_buf.at[g], me).wait_recv()
                dev = 4 * x + 2 * y + c
                total = s_buf[dev]
                for d in range(1, N_DEV):
                    total = total + s_buf[jnp.bitwise_xor(dev, d)]
                gsum_ref[...] = total
                copy(7, gw_hbm.at[:, theirs], gw_hbm.at[:, theirs], me).wait_recv()
                store.wait()
                for g in range(4):
                    to_sibling(g).wait_send()
                for g in range(3):
                    pair_sum(g).wait_send()
                finished().wait_send()
                for g in range(1, N_DEV):
                    small(g).wait_send()

    any_spec = pl.BlockSpec(memory_space=pl.ANY)
    n_sem = 8 + N_DEV - 1
    grid_spec = pltpu.PrefetchScalarGridSpec(
        num_scalar_prefetch=1,
        grid=(N_CHIPS, n_tiles),
        in_specs=[
            pl.BlockSpec((1, SHARD_COLS, tm), lambda ph, i, order: (order[ph], 0, i)),
            pl.BlockSpec((tm, D_MODEL), lambda ph, i, order: (i, 0)),
            any_spec, any_spec,
            pl.BlockSpec((SMALL_ROWS, LANES), lambda ph, i, order: (0, 0)),
        ],
        out_specs=[any_spec, any_spec, any_spec, pl.BlockSpec((SMALL_ROWS, LANES), lambda ph, i, order: (0, 0))],
        scratch_shapes=[
            pltpu.VMEM((SHARD_COLS, D_MODEL), F32),
            pltpu.VMEM((N_CHIPS, 2, SHARD_COLS, hc), BF16),
            pltpu.VMEM((N_CHIPS, SHARD_COLS, hc), BF16),
            pltpu.VMEM((3, SHARD_COLS, hc), BF16),
            pltpu.VMEM((SHARD_COLS, hc), F32),
            pltpu.VMEM((N_DEV, SMALL_ROWS, LANES), F32),
            pltpu.SemaphoreType.DMA((n_sem,)), pltpu.SemaphoreType.DMA((n_sem,)), pltpu.SemaphoreType.DMA,
        ] + red.scratch,
    )
    out = pl.pallas_call(
        body,
        name="inproj_bwd_w",
        grid_spec=grid_spec,
        out_shape=[jax.ShapeDtypeStruct((SHARD_COLS, D_MODEL), F32)] + red.out_shape
        + [jax.ShapeDtypeStruct((SMALL_ROWS, LANES), F32)],
        compiler_params=pltpu.CompilerParams(dimension_semantics=("arbitrary", "arbitrary"),
                                             vmem_limit_bytes=VMEM_LIMIT),
    )(order, dz_sh, hn, g_uq, g_ukv, gs)
    return out[0], out[1], out[2], out[3]


def _other_chips(x, y):
    return ((1 - x, 1 - y), (1 - x, y), (x, 1 - y))


def _half(ref, axis, size, c, lead=()):
    window = pl.ds(pl.multiple_of(c * size, size), size)
    if axis == 0:
        return ref.at[(*lead, window, slice(None))]
    return ref.at[(*lead, slice(None), window)]


def _half_shape(rows, cols, axis, size):
    return (size, cols) if axis == 0 else (rows, size)


def _remote_copier(send_sems, recv_sems):
    def copy(sem, src, dst, to):
        return pltpu.make_async_remote_copy(src_ref=src, dst_ref=dst, send_sem=send_sems.at[sem],
                                            recv_sem=recv_sems.at[sem], device_id=to, device_id_type=MESH)
    return copy


class _Gather:
    def __init__(self, params):
        self.params = params
        n = len(params)
        self.scratch = [pltpu.SemaphoreType.DMA((6 * n,)), pltpu.SemaphoreType.DMA((6 * n,)),
                        pltpu.SemaphoreType.DMA((n,))]
        self.out_shape = [jax.ShapeDtypeStruct((N_CHIPS, r, cc), BF16) for _, r, cc, _, _ in params]

    def bind(self, ins, outs, scratch):
        self.ins, self.outs = ins, outs
        send_sems, recv_sems, self.local_sems = scratch
        self.copy = _remote_copier(send_sems, recv_sems)
        self.x, self.y, self.c = lax.axis_index("x"), lax.axis_index("y"), lax.axis_index("c")
        self.k = 2 * self.x + self.y
        self.chips = _other_chips(self.x, self.y)

    def _local(self, p):
        return pltpu.make_async_copy(self.ins[p], self.outs[p].at[self.k], self.local_sems.at[p])

    def _first(self, p, j):
        _, _, _, axis, size = self.params[p]
        cx, cy = self.chips[j]
        return self.copy(6 * p + j, _half(self.ins[p], axis, size, self.c),
                         _half(self.outs[p], axis, size, self.c, (self.k,)), (cx, cy, self.c))

    def _relay(self, p, j, half_of):
        _, _, _, axis, size = self.params[p]
        cx, cy = self.chips[j]
        block = _half(self.outs[p], axis, size, half_of, (2 * cx + cy,))
        return self.copy(6 * p + 3 + j, block, block, (self.x, self.y, 1 - self.c))

    def start(self):
        for p in range(len(self.params)):
            self._local(p).start()
            for j in range(3):
                self._first(p, j).start()

    def relay_one(self, p, j):
        _, _, _, axis, size = self.params[p]
        cx, cy = self.chips[j]
        landed = _half(self.outs[p], axis, size, self.c, (2 * cx + cy,))
        self.copy(6 * p + j, landed, landed, (self.x, self.y, self.c)).wait_recv()
        self._relay(p, j, self.c).start()

    def await_one(self, p, j):
        self._relay(p, j, 1 - self.c).wait_recv()

    def wait_sends(self):
        for p in range(len(self.params)):
            for j in range(3):
                self._first(p, j).wait_send()
                self._relay(p, j, self.c).wait_send()
            self._local(p).wait()

    def relay(self):
        for j in range(3):
            for p in range(len(self.params)):
                self.relay_one(p, j)

    def finish(self):
        for j in range(3):
            for p in range(len(self.params)):
                self.await_one(p, j)
        self.wait_sends()


class _Reduce:
    def __init__(self, params):
        self.params = params
        n = len(params)
        halves = [_half_shape(r, cc, axis, size) for _, r, cc, axis, size in params]
        self.scratch = ([pltpu.VMEM((N_CHIPS, *h), BF16) for h in halves]
                        + [pltpu.VMEM((N_CHIPS, *h), BF16) for h in halves]
                        + [pltpu.VMEM((3, *h), BF16) for h in halves]
                        + [pltpu.VMEM(h, F32) for h in halves]
                        + [pltpu.SemaphoreType.DMA((5 * n,)), pltpu.SemaphoreType.DMA((5 * n,)),
                           pltpu.SemaphoreType.DMA((2 * n,))])
        self.out_shape = [jax.ShapeDtypeStruct((r, cc), F32) for _, r, cc, _, _ in params]

    def bind(self, g_in, g_out, scratch):
        n = len(self.params)
        self.g_in, self.g_out = g_in, g_out
        self.pm, self.a_buf = scratch[0:n], scratch[n:2 * n]
        self.b_buf, self.r_buf = scratch[2 * n:3 * n], scratch[3 * n:4 * n]
        send_sems, recv_sems, self.local_sems = scratch[4 * n:]
        self.copy = _remote_copier(send_sems, recv_sems)
        self.x, self.y, self.c = lax.axis_index("x"), lax.axis_index("y"), lax.axis_index("c")
        self.k = 2 * self.x + self.y
        self.chips = _other_chips(self.x, self.y)
        self.me = (self.x, self.y, self.c)
        self.sibling = (self.x, self.y, 1 - self.c)

    def _load(self, p):
        _, _, _, axis, size = self.params[p]
        return pltpu.make_async_copy(_half(self.g_in[p], axis, size, self.c, (slice(None),)), self.pm[p],
                                     self.local_sems.at[p])

    def _to_sibling(self, p):
        _, _, _, axis, size = self.params[p]
        return self.copy(5 * p, _half(self.g_in[p], axis, size, 1 - self.c, (slice(None),)), self.a_buf[p],
                         self.sibling)

    def _pair_sum(self, p, j):
        cx, cy = self.chips[j]
        return self.copy(5 * p + 1 + j, self.pm[p].at[2 * cx + cy], self.b_buf[p].at[j], (cx, cy, self.c))

    def _store(self, p):
        _, _, _, axis, size = self.params[p]
        n = len(self.params)
        return pltpu.make_async_copy(self.r_buf[p], _half(self.g_out[p], axis, size, self.c),
                                     self.local_sems.at[n + p])

    def _finished(self, p):
        _, _, _, axis, size = self.params[p]
        return self.copy(5 * p + 4, self.r_buf[p], _half(self.g_out[p], axis, size, self.c), self.sibling)

    def start(self):
        for p in range(len(self.params)):
            self._load(p).start()
            self._to_sibling(p).start()

    def exchange(self):
        for p in range(len(self.params)):
            self._load(p).wait()
            self.copy(5 * p, self.a_buf[p], self.a_buf[p], self.me).wait_recv()
            for j, (cx, cy) in enumerate(self.chips):
                kj = 2 * cx + cy
                self.pm[p][kj] = (self.pm[p][kj].astype(F32) + self.a_buf[p][kj].astype(F32)).astype(BF16)
                self._pair_sum(p, j).start()
            self.r_buf[p][...] = self.pm[p][self.k].astype(F32) + self.a_buf[p][self.k].astype(F32)

    def finish(self):
        for p, (_, _, _, axis, size) in enumerate(self.params):
            for j in range(3):
                self.copy(5 * p + 1 + j, self.b_buf[p].at[j], self.b_buf[p].at[j], self.me).wait_recv()
                self.r_buf[p][...] = self.r_buf[p][...] + self.b_buf[p][j].astype(F32)
            self._store(p).start()
            self._finished(p).start()
        for p, (_, _, _, axis, size) in enumerate(self.params):
            theirs = _half(self.g_out[p], axis, size, 1 - self.c)
            self.copy(5 * p + 4, theirs, theirs, self.me).wait_recv()
            self._store(p).wait()
            self._to_sibling(p).wait_send()
            for j in range(3):
                self._pair_sum(p, j).wait_send()
            self._finished(p).wait_send()


def _adamw_math(w, g, m, v):
    m = ADAM_B1 * m + (1.0 - ADAM_B1) * g
    v = ADAM_B2 * v + (1.0 - ADAM_B2) * (g * g)
    m_hat = m / (1.0 - ADAM_B1 ** ADAM_STEP)
    v_hat = v / (1.0 - ADAM_B2 ** ADAM_STEP)
    delta = -ADAM_LR * (m_hat / (jnp.sqrt(v_hat) + ADAM_EPS) + ADAM_WD * w)
    return delta, m, v


def _adamw_tiled(w, g, m, v, tm):
    rows, cols = w.shape

    def body(w_ref, g_ref, m_ref, v_ref, d_ref, nm_ref, nv_ref):
        d_ref[...], nm_ref[...], nv_ref[...] = _adamw_math(w_ref[...], g_ref[...], m_ref[...], v_ref[...])

    spec = _row_spec(tm, cols)
    return pl.pallas_call(
        body,
        name="adamw_w_in",
        grid=(rows // tm,),
        in_specs=[spec] * 4,
        out_specs=[spec] * 3,
        out_shape=[jax.ShapeDtypeStruct(w.shape, F32)] * 3,
        compiler_params=pltpu.CompilerParams(dimension_semantics=("parallel",), vmem_limit_bytes=VMEM_LIMIT),
    )(w, g, m, v)


def _adamw_many(ws, gs, ms, vs):
    n = len(ws)

    def body(*refs):
        ins, outs = refs[:4 * n], refs[4 * n:]
        for i in range(n):
            d, nm, nv = _adamw_math(ins[i][...], ins[n + i][...], ins[2 * n + i][...], ins[3 * n + i][...])
            outs[i][...] = d
            outs[n + i][...] = nm
            outs[2 * n + i][...] = nv

    vmem_spec = pl.BlockSpec(memory_space=pltpu.VMEM)
    shapes = [jax.ShapeDtypeStruct(w.shape, F32) for w in ws]
    out = pl.pallas_call(
        body,
        name="adamw_small",
        in_specs=[vmem_spec] * (4 * n),
        out_specs=[vmem_spec] * (3 * n),
        out_shape=shapes * 3,
        compiler_params=pltpu.CompilerParams(vmem_limit_bytes=VMEM_LIMIT),
    )(*ws, *gs, *ms, *vs)
    return out[:n], out[n:2 * n], out[2 * n:]


def _pack_rows(parts, rows, dtype):
    flat = jnp.concatenate([p.reshape(-1).astype(dtype) for p in parts])
    flat = jnp.concatenate([flat, jnp.zeros((rows * LANES - flat.shape[0],), dtype)])
    return flat.reshape(rows, LANES)


def _unpack_rows(packed, shapes):
    flat = packed.reshape(-1)
    out, off = [], 0
    for _, shp in shapes:
        n = int(np.prod(shp))
        out.append(flat[off:off + n].reshape(shp))
        off += n
    return out


def _rope_tables(s):
    half = QK_ROPE_DIM // 2
    inv_freq = np.float32(ROPE_THETA) ** (-np.arange(half, dtype=np.float32) / np.float32(half))
    ang = (np.arange(s, dtype=np.float32)[:, None] * inv_freq[None, :]).astype(np.float32)
    cos, sin = np.cos(ang.astype(np.float64)).astype(np.float32), np.sin(ang.astype(np.float64)).astype(np.float32)
    z16 = np.zeros((s, half), np.float32)
    z32 = np.zeros((s, HEAD_PAD - QK_NOPE_DIM - QK_ROPE_DIM), np.float32)
    z64 = np.zeros((s, QK_NOPE_DIM), np.float32)
    rc = np.concatenate([np.ones((s, QK_NOPE_DIM), np.float32), cos, cos, z32], axis=1)
    rsa = np.concatenate([z64, -sin, z16, z32], axis=1)
    rsb = np.concatenate([z64, z16, sin, z32], axis=1)
    return jnp.asarray(rc), jnp.asarray(rsa), jnp.asarray(rsb)


def kernel(x, norm_in, w_in, q_norm, w_uq, kv_norm, w_ukv, pool_w, pool_scale, w_branch_attn, w_branch_pool, w_out, norm_final, loss_target, m_norm_in, m_w_in, m_q_norm, m_w_uq, m_kv_norm, m_w_ukv, m_pool_w, m_pool_scale, m_w_branch_attn, m_w_branch_pool, m_w_out, m_norm_final, v_norm_in, v_w_in, v_q_norm, v_w_uq, v_kv_norm, v_w_ukv, v_pool_w, v_pool_scale, v_w_branch_attn, v_w_branch_pool, v_w_out, v_norm_final):
    s = x.shape[1]
    t_att, t_row = _tiles(s)
    x2 = x.reshape(s, D_MODEL)
    tgt = loss_target.reshape(s, D_MODEL)

    local = [w_in.T, w_uq.reshape(96, 768), w_ukv.reshape(64, 1024), w_branch_attn, w_branch_pool, w_out]
    local = [a.astype(BF16) for a in local]
    cx, cy = lax.axis_index("x"), lax.axis_index("y")
    others = [2 * ox + oy for ox, oy in _other_chips(cx, cy)]
    hn, z_sh, (w_in_t, w_uq_all, w_ukv_all) = _inproj_fwd(
        jnp.stack([2 * cx + cy, others[1], others[2], others[0]]).astype(jnp.int32), x2, norm_in.reshape(1, -1),
        local[:3], 4 * t_row)
    w_uq_f = w_uq_all.reshape(Q_LORA_RANK, MLA_HEADS, QK_NOPE_DIM + QK_ROPE_DIM)
    w_ukv_f = w_ukv_all.reshape(KV_LORA_RANK, MLA_HEADS, QK_NOPE_DIM + V_HEAD_DIM)
    hw = MLA_HEADS * HEAD_PAD
    wuq_p = jnp.pad(w_uq_f, ((0, 0), (0, 0), (0, HEAD_PAD - QK_NOPE_DIM - QK_ROPE_DIM))).reshape(Q_LORA_RANK, hw)
    wk_p = jnp.pad(w_ukv_f[:, :, :QK_NOPE_DIM], ((0, 0), (0, 0), (0, HEAD_PAD - QK_NOPE_DIM))).reshape(KV_LORA_RANK, hw)
    wv = w_ukv_f[:, :, QK_NOPE_DIM:].reshape(KV_LORA_RANK, MLA_WIDTH)
    rc, rsa, rsb = _rope_tables(s)
    g_in = norm_in.reshape(1, -1)
    g_q = q_norm.reshape(1, -1)
    g_kv = kv_norm.reshape(1, -1)
    g_f = norm_final.reshape(1, -1)
    ps = pool_scale.reshape(1, -1)
    pw_bf = pool_w.astype(BF16)

    q, k, v, q_t, v_t = _qkv_fwd(z_sh, g_q, g_kv, wuq_p, wk_p, wv, rc, rsa, rsb, t_row)
    o, lse, (w_ba_all, w_bp_all, w_out_all) = _attn_fwd(q_t, k, v_t, local[3:], t_att)
    w_out_f = w_out_all.reshape(D_MODEL, D_MODEL)

    (do, delta, dgattn, dgpool, dgmerge, ddc, dh, sq_err, d_w_out, d_w_ba, d_w_bp, d_pool_w, d_pool_scale,
     d_norm_final) = _mid(o, z_sh, x2, tgt, pw_bf, ps, w_ba_all, w_bp_all, w_out_f, g_f, t_row)

    late_grads = [d_w_ba, d_w_bp, d_w_out.reshape(N_CHIPS, 256, D_MODEL)]
    dq, dk_t, dv_t, (g_w_ba, g_w_bp, g_w_out) = _attn_bwd(q, q_t, k, v, do, lse, delta, late_grads, t_att)
    dzq, dzkv, dzkr, d_wuq_p, d_wk_p, d_wv, d_q_norm, d_kv_norm = _qkv_bwd(
        dq, dk_t, dv_t, z_sh, g_q, g_kv, wuq_p, wk_p, wv, rc, rsa, rsb, t_row)
    grad_x, d_norm_in, dz_sh = _inproj_bwd_x(dzq, dzkv, dzkr, dgattn, ddc, dgpool, dgmerge, x2, dh, g_in, w_in_t,
                                             t_row)

    d_w_uq = d_wuq_p.reshape(Q_LORA_RANK, MLA_HEADS, HEAD_PAD)[:, :, :QK_NOPE_DIM + QK_ROPE_DIM]
    d_w_ukv = jnp.concatenate([d_wk_p.reshape(KV_LORA_RANK, MLA_HEADS, HEAD_PAD)[:, :, :QK_NOPE_DIM],
                               d_wv.reshape(KV_LORA_RANK, MLA_HEADS, V_HEAD_DIM)], axis=2)
    small = dict(norm_in=d_norm_in, q_norm=d_q_norm, kv_norm=d_kv_norm, pool_scale=d_pool_scale,
                 norm_final=d_norm_final, pool_w=d_pool_w, sq_err=sq_err)
    gs = _pack_rows([small[n] for n, _ in SMALL_SHAPES], SMALL_ROWS, F32)
    order = jnp.stack(others + [2 * cx + cy]).astype(jnp.int32)
    g_w_in_t, g_w_uq, g_w_ukv, g_small = _inproj_bwd_w(
        order, dz_sh, hn, d_w_uq.reshape(N_CHIPS, 96, 768).astype(BF16),
        d_w_ukv.reshape(N_CHIPS, 64, 1024).astype(BF16), gs, 4 * t_row)
    (g_norm_in, g_q_norm, g_kv_norm, g_pool_scale, g_norm_final, g_pool_w,
     sq_err_all) = _unpack_rows(g_small, SMALL_SHAPES)
    g_w_uq = g_w_uq.reshape(w_uq.shape)
    g_w_ukv = g_w_ukv.reshape(w_ukv.shape)

    dl_w_in, nm_w_in, nv_w_in = (a.T for a in _adamw_tiled(w_in.T, g_w_in_t, m_w_in.T, v_w_in.T, 152))

    def two_d(a):
        return a.reshape(1, -1) if a.ndim == 1 else a

    names = ["norm_in", "q_norm", "w_uq", "kv_norm", "w_ukv", "pool_w", "pool_scale", "w_branch_attn",
             "w_branch_pool", "w_out", "norm_final"]
    ws = dict(norm_in=norm_in, q_norm=q_norm, w_uq=w_uq, kv_norm=kv_norm, w_ukv=w_ukv, pool_w=pool_w,
              pool_scale=pool_scale, w_branch_attn=w_branch_attn, w_branch_pool=w_branch_pool, w_out=w_out,
              norm_final=norm_final)
    gsd = dict(norm_in=g_norm_in, q_norm=g_q_norm, w_uq=g_w_uq, kv_norm=g_kv_norm, w_ukv=g_w_ukv, pool_w=g_pool_w,
               pool_scale=g_pool_scale, w_branch_attn=g_w_ba, w_branch_pool=g_w_bp, w_out=g_w_out,
               norm_final=g_norm_final)
    msd = dict(norm_in=m_norm_in, q_norm=m_q_norm, w_uq=m_w_uq, kv_norm=m_kv_norm, w_ukv=m_w_ukv, pool_w=m_pool_w,
               pool_scale=m_pool_scale, w_branch_attn=m_w_branch_attn, w_branch_pool=m_w_branch_pool, w_out=m_w_out,
               norm_final=m_norm_final)
    vsd = dict(norm_in=v_norm_in, q_norm=v_q_norm, w_uq=v_w_uq, kv_norm=v_kv_norm, w_ukv=v_w_ukv, pool_w=v_pool_w,
               pool_scale=v_pool_scale, w_branch_attn=v_w_branch_attn, w_branch_pool=v_w_branch_pool, w_out=v_w_out,
               norm_final=v_norm_final)
    dls, nms, nvs = _adamw_many([two_d(ws[n]) for n in names], [two_d(gsd[n]) for n in names],
                                [two_d(msd[n]) for n in names], [two_d(vsd[n]) for n in names])

    grads = dict(gsd)
    grads["w_in"] = g_w_in_t.T
    delta_w = {n: d.reshape(ws[n].shape) for n, d in zip(names, dls)}
    new_m = {n: d.reshape(ws[n].shape) for n, d in zip(names, nms)}
    new_v = {n: d.reshape(ws[n].shape) for n, d in zip(names, nvs)}
    delta_w["w_in"], new_m["w_in"], new_v["w_in"] = dl_w_in, nm_w_in, nv_w_in
    ws["w_in"] = w_in

    order = ["norm_in", "w_in", "q_norm", "w_uq", "kv_norm", "w_ukv", "pool_w", "pool_scale", "w_branch_attn",
             "w_branch_pool", "w_out", "norm_final"]
    loss = 0.5 * jnp.sum(sq_err_all) / D_MODEL
    return (loss, grad_x.reshape(x.shape),
            *[grads[n].reshape(ws[n].shape) for n in order],
            *[delta_w[n] for n in order], *[new_m[n] for n in order], *[new_v[n] for n in order])
```

```python
import functools

import jax
import jax.numpy as jnp
import numpy as np
from jax import lax
from jax.experimental import pallas as pl
from jax.experimental.pallas import tpu as pltpu

F32 = jnp.float32
BF16 = jnp.bfloat16
MESH = pl.DeviceIdType.MESH

D_MODEL = 1024
CHUNK = 64
MLA_HEADS = 8
QK_NOPE_DIM = 64
QK_ROPE_DIM = 32
V_HEAD_DIM = 64
Q_LORA_RANK = 384
KV_LORA_RANK = 256
MLA_WIDTH = MLA_HEADS * V_HEAD_DIM
ROPE_THETA = 10000.0
POOL_WINDOWS = (2, 4, 8, 16)
POOL_WIDTH = 512
POOL_GROUP_DIM = 128
BRANCH_COLS = D_MODEL // 4
POOL_HALO = 16
EPS = 1e-6
IN_TOTAL = 4256
HEAD_PAD = 128
ATT_SCALE = (QK_NOPE_DIM + QK_ROPE_DIM) ** -0.5
ATT_SCALE_LOG2E = ATT_SCALE * 1.4426950408889634

ADAM_LR = 0.001
ADAM_B1 = 0.9
ADAM_B2 = 0.999
ADAM_EPS = 1e-08
ADAM_WD = 0.01
ADAM_STEP = 10

N_CHIPS = 4
N_DEV = 8
LANES = 128
VMEM_LIMIT = 60 * 1024 * 1024

IN_SEGMENTS = ((384, 384), (256, 256), (32, HEAD_PAD), (512, 512), (512, 512), (512, 512), (2048, 2048))
SHARD_COLS = IN_TOTAL // N_CHIPS
ZQ_COLS = slice(0, 384)
ZKV_COLS = slice(384, 640)
ZKR_TILE = slice(640, 768)


def _shard_pieces():
    bounds, off = [], 0
    for w, _ in IN_SEGMENTS:
        bounds.append((off, off + w))
        off += w
    out = []
    for j in range(N_CHIPS):
        lo, hi = SHARD_COLS * j, SHARD_COLS * (j + 1)
        out.append([(i, max(lo, a) - a, min(hi, b) - a, max(lo, a) - lo)
                    for i, (a, b) in enumerate(bounds) if max(lo, a) < min(hi, b)])
    return out


SHARD_PIECES = _shard_pieces()


def _segment(z_blocks, seg):
    parts = [z_blocks[j][:, col:col + hi - lo]
             for j, pieces in enumerate(SHARD_PIECES) for sg, lo, hi, col in pieces if sg == seg]
    return parts[0] if len(parts) == 1 else jnp.concatenate(parts, axis=1)

COMM_PARAMS = (
    ("w_in", SHARD_COLS, D_MODEL, 1, 512),
    ("w_uq", 96, 768, 0, 48),
    ("w_ukv", 64, 1024, 0, 32),
    ("w_branch_attn", 512, 256, 0, 256),
    ("w_branch_pool", 512, 256, 0, 256),
    ("w_out", 256, 1024, 0, 128),
)

SMALL_SHAPES = (
    ("norm_in", (1024,)),
    ("q_norm", (384,)),
    ("kv_norm", (256,)),
    ("pool_scale", (512,)),
    ("norm_final", (1024,)),
    ("pool_w", (4, 128, 128)),
    ("sq_err", (8, 128)),
)
SMALL_ELEMS = sum(int(np.prod(s)) for _, s in SMALL_SHAPES)
SMALL_ROWS = -(-SMALL_ELEMS // (LANES * 8)) * 8


def _dot(a, b):
    return jnp.dot(a, b, preferred_element_type=F32)


def _dot_nt(a, b):
    return lax.dot_general(a, b, (((1,), (1,)), ((), ())), preferred_element_type=F32)


def _dot_tn(a, b):
    return lax.dot_general(a, b, (((0,), (0,)), ((), ())), preferred_element_type=F32)


def _sigmoid(x):
    return 1.0 / (1.0 + jnp.exp(-x))


def _colsum(x):
    return jnp.sum(x, axis=0, keepdims=True)


def _rms_fwd(x, g):
    r = lax.rsqrt(jnp.mean(x * x, axis=-1, keepdims=True) + EPS)
    xhat = x * r
    return xhat * g, xhat, r


def _rms_bwd(dy, xhat, r, g):
    dxhat = dy * g
    return r * (dxhat - xhat * jnp.mean(dxhat * xhat, axis=-1, keepdims=True))


def _rope(v, c, sa, sb):
    return v * c + pltpu.roll(v, 112, 1) * sa + pltpu.roll(v, 16, 1) * sb


def _unrope(d, c, sa, sb):
    return d * c + pltpu.roll(d * sa, 16, 1) + pltpu.roll(d * sb, 112, 1)


def _row_spec(tm, n):
    return pl.BlockSpec((tm, n), lambda i: (i, 0))


def _full_spec(shape):
    nd = len(shape)
    return pl.BlockSpec(shape, lambda i: (0,) * nd)


def _tiles(s):
    t_att = 512 if s >= 2048 else 128
    t_row = 256 if s >= 1024 else 128
    return t_att, t_row


def _inproj_fwd(order, x, norm_in, early_shards, tm):
    s = x.shape[0]
    n_tiles = s // tm
    gat = _Gather(COMM_PARAMS[:3])
    n_w = len(gat.params)
    arrival = (1, 2, 0)

    def body(order_ref, x_ref, g_ref, *rest):
        w_loc, (hn_ref, z_ref), w_all = rest[:n_w], rest[n_w:n_w + 2], rest[n_w + 2:2 * n_w + 2]
        w_vmem, hn_all, w_sem = rest[2 * n_w + 2:2 * n_w + 5]
        gat.bind(w_loc, w_all, rest[2 * n_w + 5:])
        ph, i = pl.program_id(0), pl.program_id(1)
        pl.when(jnp.logical_and(ph == 0, i == 0))(gat.start)

        @pl.when(jnp.logical_and(ph == 0, i == 0))
        def _():
            cp = pltpu.make_async_copy(w_loc[0], w_vmem, w_sem)
            cp.start()
            cp.wait()

        for f in range(3):
            @pl.when(jnp.logical_and(ph == f + 1, i == 0))
            def _(f=f):
                gat.relay_one(0, arrival[f])
                gat.await_one(0, arrival[f])
                cp = pltpu.make_async_copy(w_all[0].at[order_ref[ph]], w_vmem, w_sem)
                cp.start()
                cp.wait()

        rows = pl.ds(pl.multiple_of(i * tm, tm), tm)

        @pl.when(ph == 0)
        def _():
            hn, _, _ = _rms_fwd(x_ref[...], g_ref[...])
            hn = hn.astype(BF16)
            hn_ref[...] = hn
            hn_all[rows, :] = hn

        z_ref[0] = _dot_nt(hn_all[rows, :], w_vmem[...])

        @pl.when(jnp.logical_and(ph == N_CHIPS - 1, i == n_tiles - 1))
        def _():
            for p in range(1, n_w):
                for j in range(3):
                    gat.relay_one(p, j)
            for p in range(1, n_w):
                for j in range(3):
                    gat.await_one(p, j)
            gat.wait_sends()

    def tile_in_phase0(ph, i, order):
        return (jnp.where(ph == 0, i, n_tiles - 1), 0)

    any_spec = pl.BlockSpec(memory_space=pl.ANY)
    grid_spec = pltpu.PrefetchScalarGridSpec(
        num_scalar_prefetch=1,
        grid=(N_CHIPS, n_tiles),
        in_specs=[pl.BlockSpec((tm, D_MODEL), tile_in_phase0),
                  pl.BlockSpec((1, D_MODEL), lambda ph, i, order: (0, 0))] + [any_spec] * n_w,
        out_specs=[pl.BlockSpec((tm, D_MODEL), tile_in_phase0),
                   pl.BlockSpec((1, tm, SHARD_COLS), lambda ph, i, order: (order[ph], i, 0))] + [any_spec] * n_w,
        scratch_shapes=[pltpu.VMEM((SHARD_COLS, D_MODEL), BF16), pltpu.VMEM((s, D_MODEL), BF16),
                        pltpu.SemaphoreType.DMA] + gat.scratch,
    )
    out = pl.pallas_call(
        body,
        name="inproj_fwd",
        grid_spec=grid_spec,
        out_shape=[jax.ShapeDtypeStruct((s, D_MODEL), BF16), jax.ShapeDtypeStruct((N_CHIPS, s, SHARD_COLS), F32)]
        + gat.out_shape,
        compiler_params=pltpu.CompilerParams(dimension_semantics=("arbitrary", "arbitrary"),
                                             vmem_limit_bytes=VMEM_LIMIT),
    )(order, x, norm_in, *early_shards)
    return out[0], out[1], out[2:]


def _qkv_fwd(z_sh, q_norm, kv_norm, wuq_p, wk_p, wv, rc, rsa, rsb, tm):
    s = z_sh.shape[1]
    hw = MLA_HEADS * HEAD_PAD

    def body(z_ref, gq_ref, gkv_ref, wuq_ref, wk_ref, wv_ref, c_ref, sa_ref, sb_ref,
             q_ref, k_ref, v_ref, qt_ref, vt_ref):
        c, sa, sb = c_ref[...], sa_ref[...], sb_ref[...]
        z0 = z_ref[0]
        cq, _, _ = _rms_fwd(z0[:, ZQ_COLS], gq_ref[...])
        qf = _dot(cq.astype(BF16), wuq_ref[...])
        ckv, _, _ = _rms_fwd(z0[:, ZKV_COLS], gkv_ref[...])
        ckv = ckv.astype(BF16)
        kn = _dot(ckv, wk_ref[...])
        lane = lax.broadcasted_iota(jnp.int32, (tm, HEAD_PAD), 1)
        zkr = jnp.where(lane < QK_ROPE_DIM, z0[:, ZKR_TILE], 0.0)
        kr = _rope(pltpu.roll(zkr, 64, 1), c, sa, sb)
        for h in range(MLA_HEADS):
            cols = slice(h * HEAD_PAD, (h + 1) * HEAD_PAD)
            qh = _rope(qf[:, cols], c, sa, sb)
            q_ref[:, cols] = qh.astype(BF16)
            qt_ref[cols, :] = qh.T.astype(BF16)
            k_ref[:, cols] = (kn[:, cols] + kr).astype(BF16)
        vf = _dot(ckv, wv_ref[...])
        v_ref[...] = vf.astype(BF16)
        vt_ref[...] = vf.T.astype(BF16)

    return pl.pallas_call(
        body,
        name="qkv_fwd",
        grid=(s // tm,),
        in_specs=[
            pl.BlockSpec((1, tm, SHARD_COLS), lambda i: (0, i, 0)),
            _full_spec((1, Q_LORA_RANK)), _full_spec((1, KV_LORA_RANK)),
            _full_spec((Q_LORA_RANK, hw)), _full_spec((KV_LORA_RANK, hw)), _full_spec((KV_LORA_RANK, MLA_WIDTH)),
            _row_spec(tm, HEAD_PAD), _row_spec(tm, HEAD_PAD), _row_spec(tm, HEAD_PAD),
        ],
        out_specs=[_row_spec(tm, hw), _row_spec(tm, hw), _row_spec(tm, MLA_WIDTH),
                   pl.BlockSpec((hw, tm), lambda i: (0, i)), pl.BlockSpec((MLA_WIDTH, tm), lambda i: (0, i))],
        out_shape=[jax.ShapeDtypeStruct((s, hw), BF16), jax.ShapeDtypeStruct((s, hw), BF16),
                   jax.ShapeDtypeStruct((s, MLA_WIDTH), BF16),
                   jax.ShapeDtypeStruct((hw, s), BF16), jax.ShapeDtypeStruct((MLA_WIDTH, s), BF16)],
        compiler_params=pltpu.CompilerParams(dimension_semantics=("parallel",), vmem_limit_bytes=VMEM_LIMIT),
    )(z_sh, q_norm, kv_norm, wuq_p, wk_p, wv, rc, rsa, rsb)


def _chunk_mask(t, keys_on_rows):
    rows = lax.broadcasted_iota(jnp.int32, (t, t), 0) // CHUNK
    cols = lax.broadcasted_iota(jnp.int32, (t, t), 1) // CHUNK
    return rows <= cols if keys_on_rows else cols <= rows


def _attn_fwd(q_t, k, v_t, late_shards, t):
    s = k.shape[0]
    pairs = MLA_HEADS // 2
    n_q = s // t
    gat = _Gather(COMM_PARAMS[3:])
    n_w = len(gat.params)

    def body(qt_ref, k_ref, k2_ref, vt_ref, *rest):
        w_in, (o_ref, lse_ref), w_out = rest[:n_w], rest[n_w:n_w + 2], rest[n_w + 2:2 * n_w + 2]
        gat.bind(w_in, w_out, rest[2 * n_w + 2:])
        i = pl.program_id(1)
        step_no = pl.program_id(0) * n_q + i
        pl.when(step_no == 0)(gat.start)
        pl.when(step_no == n_q)(gat.relay)
        mask = _chunk_mask(t, True)
        qcs = [slice(hh * HEAD_PAD, (hh + 1) * HEAD_PAD) for hh in range(2)]
        vcs = [slice(hh * V_HEAD_DIM, (hh + 1) * V_HEAD_DIM) for hh in range(2)]
        qts = [qt_ref[qc, :] for qc in qcs]

        def step(j, carry, masked):
            keys = pl.ds(pl.multiple_of(j * t, t), t)
            out = []
            for hh in range(2):
                m, l, acc = carry[hh]
                sc = _dot(k_ref[keys, qcs[hh]], qts[hh])
                if masked:
                    sc = jnp.where(mask, sc, -jnp.inf)
                m_new = jnp.maximum(m, jnp.max(sc, axis=0, keepdims=True))
                alpha = jnp.exp2((m - m_new) * ATT_SCALE_LOG2E)
                p = jnp.exp2((_dot(k2_ref[keys, qcs[hh]], qts[hh]) - m_new) * ATT_SCALE_LOG2E)
                if masked:
                    p = jnp.where(mask, p, 0.0)
                l = alpha * l + jnp.sum(p, axis=0, keepdims=True)
                acc = alpha * acc + _dot(vt_ref[vcs[hh], keys], p.astype(BF16))
                out.append((m_new, l, acc))
            return tuple(out)

        one = (jnp.full((1, t), -jnp.inf, F32), jnp.zeros((1, t), F32), jnp.zeros((V_HEAD_DIM, t), F32))
        carry = lax.fori_loop(0, i, functools.partial(step, masked=False), (one, one))
        carry = step(i, carry, True)
        o_ref[...] = jnp.concatenate([carry[hh][2] / carry[hh][1] for hh in range(2)], axis=0).T
        for hh in range(2):
            m, l, _ = carry[hh]
            lse_ref[:, qcs[hh]] = jnp.broadcast_to(m * ATT_SCALE_LOG2E + jnp.log2(l), (HEAD_PAD, t)).T
        pl.when(step_no == pairs * n_q - 1)(gat.finish)

    any_spec = pl.BlockSpec(memory_space=pl.ANY)
    out = pl.pallas_call(
        body,
        name="attn_fwd",
        grid=(pairs, n_q),
        in_specs=[
            pl.BlockSpec((2 * HEAD_PAD, t), lambda p, i: (p, i)),
            pl.BlockSpec((s, 2 * HEAD_PAD), lambda p, i: (0, p)),
            pl.BlockSpec((s, 2 * HEAD_PAD), lambda p, i: (0, p)),
            pl.BlockSpec((2 * V_HEAD_DIM, s), lambda p, i: (p, 0)),
        ] + [any_spec] * n_w,
        out_specs=[
            pl.BlockSpec((t, 2 * V_HEAD_DIM), lambda p, i: (i, p)),
            pl.BlockSpec((t, 2 * HEAD_PAD), lambda p, i: (i, p)),
        ] + [any_spec] * n_w,
        out_shape=[jax.ShapeDtypeStruct((s, MLA_WIDTH), F32), jax.ShapeDtypeStruct((s, MLA_HEADS * HEAD_PAD), F32)]
        + gat.out_shape,
        scratch_shapes=gat.scratch,
        compiler_params=pltpu.CompilerParams(dimension_semantics=("arbitrary", "arbitrary"),
                                             vmem_limit_bytes=VMEM_LIMIT),
    )(q_t, k, k, v_t, *late_shards)
    return out[0], out[1], out[2:]


def _mid(o, z_sh, x, target, pool_w, pool_scale, w_ba, w_bp, w_out, norm_final, tm):
    s = x.shape[0]
    n_tiles = s // tm
    halo_per_tile = tm // POOL_HALO

    def body(o_ref, z0_ref, z1_ref, z1h_ref, z2_ref, z3_ref, x_ref, t_ref, pw_ref, ps_ref, wba_ref, wbp_ref,
             wout_ref, gf_ref,
             do_ref, dl_ref, dga_ref, dgp_ref, dgm_ref, ddc_ref, dh_ref,
             loss_ref, dwout_out, dwba_out, dwbp_out, dpw_ref, dps_ref, dgf_ref,
             ubuf, dwout_ref, dwba_ref, dwbp_ref):
        i = pl.program_id(0)

        @pl.when(i == 0)
        def _():
            loss_ref[...] = jnp.zeros_like(loss_ref)
            dwout_ref[...] = jnp.zeros_like(dwout_ref)
            dwba_ref[...] = jnp.zeros_like(dwba_ref)
            dwbp_ref[...] = jnp.zeros_like(dwbp_ref)
            dpw_ref[...] = jnp.zeros_like(dpw_ref)
            dps_ref[...] = jnp.zeros_like(dps_ref)
            dgf_ref[...] = jnp.zeros_like(dgf_ref)

        zs = [z0_ref[0], z1_ref[0], z2_ref[0], z3_ref[0]]
        o = o_ref[...]
        ga = _segment(zs, 3)
        sga = _sigmoid(ga)
        silu_a = ga * sga
        y_attn = (o * silu_a).astype(BF16)

        ubuf[0:POOL_HALO, :] = jnp.where(i > 0, _segment([None, z1h_ref[0]], 4), 0.0)
        ubuf[POOL_HALO:, :] = _segment(zs, 4)
        row = lax.broadcasted_iota(jnp.int32, (tm, POOL_GROUP_DIM), 0) + i * tm
        ps = ps_ref[...]
        gp = _segment(zs, 5)
        sgp = _sigmoid(gp)
        silu_p = gp * sgp
        d_bf, dm, inv_cnt = [], [], []
        for g, w in enumerate(POOL_WINDOWS):
            cols = slice(g * POOL_GROUP_DIM, (g + 1) * POOL_GROUP_DIM)
            wsum = ubuf[POOL_HALO:, cols]
            for kk in range(1, w):
                wsum = wsum + ubuf[POOL_HALO - kk:POOL_HALO - kk + tm, cols]
            inv = 1.0 / jnp.minimum(row + 1, w).astype(F32)
            dg = (wsum * inv - ubuf[POOL_HALO:, cols]).astype(BF16)
            d_bf.append(dg)
            inv_cnt.append(inv)
            dm.append(_dot(dg, pw_ref[g]))
        dm = jnp.concatenate(dm, axis=1)
        yp = dm * ps
        y_pool = (yp * silu_p).astype(BF16)

        a = jnp.concatenate([_dot(y_attn, wba_ref[j]) for j in range(N_CHIPS)], axis=1)
        p = jnp.concatenate([_dot(y_pool, wbp_ref[j]) for j in range(N_CHIPS)], axis=1)
        gm = _segment(zs, 6)
        gate_a = _sigmoid(gm[:, :D_MODEL])
        gate_p = _sigmoid(gm[:, D_MODEL:])
        merged = (gate_a * a + gate_p * p).astype(BF16)
        h = x_ref[...] + _dot(merged, wout_ref[...])
        gf = gf_ref[...]
        y, xhat, r = _rms_fwd(h, gf)
        err = y - t_ref[...]
        e2 = err * err
        e2 = jnp.sum(e2.reshape(tm // 8, 8, D_MODEL), axis=0)
        acc = e2[:, 0:LANES]
        for cidx in range(1, D_MODEL // LANES):
            acc = acc + e2[:, cidx * LANES:(cidx + 1) * LANES]
        loss_ref[...] += acc

        dy = err * (1.0 / D_MODEL)
        dgf_ref[...] += _colsum(dy * xhat)
        dh = _rms_bwd(dy, xhat, r, gf)
        dh_ref[...] = dh
        dh_bf = dh.astype(BF16)
        dwout_ref[...] += _dot_tn(merged, dh_bf)
        dmerged = _dot_nt(dh_bf, wout_ref[...])
        da = (dmerged * gate_a).astype(BF16)
        dp = (dmerged * gate_p).astype(BF16)
        dgm_ref[:, :D_MODEL] = (dmerged * a * gate_a * (1.0 - gate_a)).astype(BF16)
        dgm_ref[:, D_MODEL:] = (dmerged * p * gate_p * (1.0 - gate_p)).astype(BF16)
        dy_attn = dy_pool = None
        for j in range(N_CHIPS):
            cols = slice(j * BRANCH_COLS, (j + 1) * BRANCH_COLS)
            dwba_ref[j] += _dot_tn(y_attn, da[:, cols])
            dwbp_ref[j] += _dot_tn(y_pool, dp[:, cols])
            pa = _dot_nt(da[:, cols], wba_ref[j])
            pp = _dot_nt(dp[:, cols], wbp_ref[j])
            dy_attn = pa if dy_attn is None else dy_attn + pa
            dy_pool = pp if dy_pool is None else dy_pool + pp

        do = dy_attn * silu_a
        do_ref[...] = do
        dga_ref[...] = (dy_attn * o * (sga * (1.0 + ga * (1.0 - sga)))).astype(BF16)
        doo = do * o
        for hd in range(MLA_HEADS):
            dl = jnp.sum(doo[:, hd * V_HEAD_DIM:(hd + 1) * V_HEAD_DIM], axis=1, keepdims=True)
            dl_ref[:, hd * HEAD_PAD:(hd + 1) * HEAD_PAD] = jnp.broadcast_to(dl, (tm, HEAD_PAD))

        dyp = dy_pool * silu_p
        dgp_ref[...] = (dy_pool * yp * (sgp * (1.0 + gp * (1.0 - sgp)))).astype(BF16)
        dps_ref[...] += _colsum(dyp * dm)
        dmm = (dyp * ps).astype(BF16)
        for g in range(len(POOL_WINDOWS)):
            cols = slice(g * POOL_GROUP_DIM, (g + 1) * POOL_GROUP_DIM)
            dpw_ref[g] += _dot_tn(d_bf[g], dmm[:, cols])
            ddc_ref[:, cols] = _dot_nt(dmm[:, cols], pw_ref[g]) * inv_cnt[g]

        @pl.when(i == n_tiles - 1)
        def _():
            dwout_out[...] = dwout_ref[...].astype(BF16)
            dwba_out[...] = dwba_ref[...].astype(BF16)
            dwbp_out[...] = dwbp_ref[...].astype(BF16)

    row_in = lambda n: _row_spec(tm, n)
    in_specs = [
        row_in(MLA_WIDTH),
        pl.BlockSpec((1, tm, SHARD_COLS), lambda i: (0, i, 0)), pl.BlockSpec((1, tm, SHARD_COLS), lambda i: (1, i, 0)),
        pl.BlockSpec((1, POOL_HALO, SHARD_COLS), lambda i: (1, jnp.maximum(i * halo_per_tile - 1, 0), 0)),
        pl.BlockSpec((1, tm, SHARD_COLS), lambda i: (2, i, 0)), pl.BlockSpec((1, tm, SHARD_COLS), lambda i: (3, i, 0)),
        row_in(D_MODEL), row_in(D_MODEL),
        _full_spec((4, POOL_GROUP_DIM, POOL_GROUP_DIM)), _full_spec((1, POOL_WIDTH)),
        _full_spec((N_CHIPS, MLA_WIDTH, BRANCH_COLS)), _full_spec((N_CHIPS, POOL_WIDTH, BRANCH_COLS)),
        _full_spec((D_MODEL, D_MODEL)), _full_spec((1, D_MODEL)),
    ]
    out_shape = [
        jax.ShapeDtypeStruct((s, MLA_WIDTH), F32),
        jax.ShapeDtypeStruct((s, MLA_HEADS * HEAD_PAD), F32),
        jax.ShapeDtypeStruct((s, MLA_WIDTH), BF16),
        jax.ShapeDtypeStruct((s, POOL_WIDTH), BF16),
        jax.ShapeDtypeStruct((s, 2 * D_MODEL), BF16),
        jax.ShapeDtypeStruct((s, POOL_WIDTH), F32),
        jax.ShapeDtypeStruct((s, D_MODEL), F32),
        jax.ShapeDtypeStruct((8, LANES), F32),
        jax.ShapeDtypeStruct((D_MODEL, D_MODEL), BF16),
        jax.ShapeDtypeStruct((N_CHIPS, MLA_WIDTH, BRANCH_COLS), BF16),
        jax.ShapeDtypeStruct((N_CHIPS, POOL_WIDTH, BRANCH_COLS), BF16),
        jax.ShapeDtypeStruct((4, POOL_GROUP_DIM, POOL_GROUP_DIM), F32),
        jax.ShapeDtypeStruct((1, POOL_WIDTH), F32),
        jax.ShapeDtypeStruct((1, D_MODEL), F32),
    ]
    out_specs = [
        row_in(MLA_WIDTH), row_in(MLA_HEADS * HEAD_PAD), row_in(MLA_WIDTH), row_in(POOL_WIDTH),
        row_in(2 * D_MODEL), row_in(POOL_WIDTH), row_in(D_MODEL),
        _full_spec((8, LANES)), _full_spec((D_MODEL, D_MODEL)), _full_spec((N_CHIPS, MLA_WIDTH, BRANCH_COLS)),
        _full_spec((N_CHIPS, POOL_WIDTH, BRANCH_COLS)), _full_spec((4, POOL_GROUP_DIM, POOL_GROUP_DIM)),
        _full_spec((1, POOL_WIDTH)), _full_spec((1, D_MODEL)),
    ]
    return pl.pallas_call(
        body,
        name="mid",
        grid=(n_tiles,),
        in_specs=in_specs,
        out_specs=out_specs,
        out_shape=out_shape,
        scratch_shapes=[
            pltpu.VMEM((tm + POOL_HALO, POOL_WIDTH), F32),
            pltpu.VMEM((D_MODEL, D_MODEL), F32),
            pltpu.VMEM((N_CHIPS, MLA_WIDTH, BRANCH_COLS), F32),
            pltpu.VMEM((N_CHIPS, POOL_WIDTH, BRANCH_COLS), F32),
        ],
        compiler_params=pltpu.CompilerParams(dimension_semantics=("arbitrary",), vmem_limit_bytes=VMEM_LIMIT),
    )(o, z_sh, z_sh, z_sh, z_sh, z_sh, x, target, pool_w, pool_scale, w_ba, w_bp, w_out, norm_final)


def _attn_bwd(q, q_t, k, v, do, lse, delta, late_grads, t):
    s = q.shape[0]
    pairs = MLA_HEADS // 2
    n_q = s // t
    red = _Reduce(COMM_PARAMS[3:])
    n_w = len(red.params)

    def body(q_ref, qt_ref, do_ref, lse_ref, dl_ref, k_ref, v_ref, *rest):
        g_in, (dq_ref, dk_ref, dv_ref), g_out = rest[:n_w], rest[n_w:n_w + 3], rest[n_w + 3:2 * n_w + 3]
        red.bind(g_in, g_out, rest[2 * n_w + 3:])
        i = pl.program_id(1)
        step_no = pl.program_id(0) * n_q + i
        pl.when(step_no == 0)(red.start)
        pl.when(step_no == n_q)(red.exchange)

        @pl.when(i == 0)
        def _():
            dk_ref[...] = jnp.zeros_like(dk_ref)
            dv_ref[...] = jnp.zeros_like(dv_ref)

        mask = _chunk_mask(t, False)
        qcs = [slice(hh * HEAD_PAD, (hh + 1) * HEAD_PAD) for hh in range(2)]
        vcs = [slice(hh * V_HEAD_DIM, (hh + 1) * V_HEAD_DIM) for hh in range(2)]
        qhs = [q_ref[:, qc] for qc in qcs]
        qts = [qt_ref[qc, :] for qc in qcs]
        dohs = [do_ref[:, vc].astype(BF16) for vc in vcs]
        do_t = do_ref[...].T.astype(BF16)
        dots = [do_t[vc, :] for vc in vcs]
        lses = [lse_ref[:, hh * HEAD_PAD:hh * HEAD_PAD + 1] for hh in range(2)]
        dls = [dl_ref[:, hh * HEAD_PAD:hh * HEAD_PAD + 1] for hh in range(2)]

        def step(j, dqs, masked):
            keys = pl.ds(pl.multiple_of(j * t, t), t)
            out = []
            for hh in range(2):
                kj = k_ref[keys, qcs[hh]]
                vj = v_ref[keys, vcs[hh]]
                p = jnp.exp2(_dot_nt(qhs[hh], kj) * ATT_SCALE_LOG2E - lses[hh])
                if masked:
                    p = jnp.where(mask, p, 0.0)
                ds = (p * (_dot_nt(dohs[hh], vj) - dls[hh])).astype(BF16)
                dv_ref[vcs[hh], keys] += _dot(dots[hh], p.astype(BF16))
                dk_ref[qcs[hh], keys] += _dot(qts[hh], ds) * ATT_SCALE
                out.append(dqs[hh] + _dot(ds, kj))
            return tuple(out)

        zero = jnp.zeros((t, HEAD_PAD), F32)
        dqs = lax.fori_loop(0, i, functools.partial(step, masked=False), (zero, zero))
        dqs = step(i, dqs, True)
        for hh in range(2):
            dq_ref[:, qcs[hh]] = dqs[hh] * ATT_SCALE
        pl.when(step_no == pairs * n_q - 1)(red.finish)

    hw = MLA_HEADS * HEAD_PAD
    any_spec = pl.BlockSpec(memory_space=pl.ANY)
    out = pl.pallas_call(
        body,
        name="attn_bwd",
        grid=(pairs, n_q),
        in_specs=[
            pl.BlockSpec((t, 2 * HEAD_PAD), lambda p, i: (i, p)),
            pl.BlockSpec((2 * HEAD_PAD, t), lambda p, i: (p, i)),
            pl.BlockSpec((t, 2 * V_HEAD_DIM), lambda p, i: (i, p)),
            pl.BlockSpec((t, 2 * HEAD_PAD), lambda p, i: (i, p)),
            pl.BlockSpec((t, 2 * HEAD_PAD), lambda p, i: (i, p)),
            pl.BlockSpec((s, 2 * HEAD_PAD), lambda p, i: (0, p)),
            pl.BlockSpec((s, 2 * V_HEAD_DIM), lambda p, i: (0, p)),
        ] + [any_spec] * n_w,
        out_specs=[
            pl.BlockSpec((t, 2 * HEAD_PAD), lambda p, i: (i, p)),
            pl.BlockSpec((2 * HEAD_PAD, s), lambda p, i: (p, 0)),
            pl.BlockSpec((2 * V_HEAD_DIM, s), lambda p, i: (p, 0)),
        ] + [any_spec] * n_w,
        out_shape=[jax.ShapeDtypeStruct((s, hw), F32), jax.ShapeDtypeStruct((hw, s), F32),
                   jax.ShapeDtypeStruct((MLA_WIDTH, s), F32)] + red.out_shape,
        scratch_shapes=red.scratch,
        compiler_params=pltpu.CompilerParams(dimension_semantics=("arbitrary", "arbitrary"),
                                             vmem_limit_bytes=VMEM_LIMIT),
    )(q, q_t, do, lse, delta, k, v, *late_grads)
    return out[0], out[1], out[2], out[3:]


def _qkv_bwd(dq, dk_t, dv_t, z_sh, q_norm, kv_norm, wuq_p, wk_p, wv, rc, rsa, rsb, tm):
    s = z_sh.shape[1]
    hw = MLA_HEADS * HEAD_PAD

    def body(dq_ref, dk_ref, dv_ref, z_ref, gq_ref, gkv_ref, wuq_ref, wk_ref, wv_ref,
             c_ref, sa_ref, sb_ref,
             dzq_ref, dzkv_ref, dzkr_ref, dwuq_ref, dwk_ref, dwv_ref, dgq_ref, dgkv_ref):
        i = pl.program_id(0)

        @pl.when(i == 0)
        def _():
            dwuq_ref[...] = jnp.zeros_like(dwuq_ref)
            dwk_ref[...] = jnp.zeros_like(dwk_ref)
            dwv_ref[...] = jnp.zeros_like(dwv_ref)
            dgq_ref[...] = jnp.zeros_like(dgq_ref)
            dgkv_ref[...] = jnp.zeros_like(dgkv_ref)

        c, sa, sb = c_ref[...], sa_ref[...], sb_ref[...]
        gq, gkv = gq_ref[...], gkv_ref[...]

        z0 = z_ref[0]
        cq, xq, rq = _rms_fwd(z0[:, ZQ_COLS], gq)
        dqp = jnp.concatenate(
            [_unrope(dq_ref[:, h * HEAD_PAD:(h + 1) * HEAD_PAD], c, sa, sb) for h in range(MLA_HEADS)],
            axis=1).astype(BF16)
        dwuq_ref[...] += _dot_tn(cq.astype(BF16), dqp)
        dcq = _dot_nt(dqp, wuq_ref[...])
        dgq_ref[...] += _colsum(dcq * xq)
        dzq_ref[...] = _rms_bwd(dcq, xq, rq, gq).astype(BF16)

        ckv, xkv, rkv = _rms_fwd(z0[:, ZKV_COLS], gkv)
        ckv = ckv.astype(BF16)
        dkf = dk_ref[...].T
        dk_bf = dkf.astype(BF16)
        dv_bf = dv_ref[...].T.astype(BF16)
        dwk_ref[...] += _dot_tn(ckv, dk_bf)
        dwv_ref[...] += _dot_tn(ckv, dv_bf)
        dckv = _dot_nt(dk_bf, wk_ref[...]) + _dot_nt(dv_bf, wv_ref[...])
        dgkv_ref[...] += _colsum(dckv * xkv)
        dzkv_ref[...] = _rms_bwd(dckv, xkv, rkv, gkv).astype(BF16)

        dkr = dkf[:, 0:HEAD_PAD]
        for h in range(1, MLA_HEADS):
            dkr = dkr + dkf[:, h * HEAD_PAD:(h + 1) * HEAD_PAD]
        dkr = pltpu.roll(_unrope(dkr, c, sa, sb), 64, 1)
        lane = lax.broadcasted_iota(jnp.int32, (tm, HEAD_PAD), 1)
        dzkr_ref[...] = jnp.where(lane < QK_ROPE_DIM, dkr, 0.0).astype(BF16)

    return pl.pallas_call(
        body,
        name="qkv_bwd",
        grid=(s // tm,),
        in_specs=[
            _row_spec(tm, hw), pl.BlockSpec((hw, tm), lambda i: (0, i)), pl.BlockSpec((MLA_WIDTH, tm), lambda i: (0, i)),
            pl.BlockSpec((1, tm, SHARD_COLS), lambda i: (0, i, 0)),
            _full_spec((1, Q_LORA_RANK)), _full_spec((1, KV_LORA_RANK)),
            _full_spec((Q_LORA_RANK, hw)), _full_spec((KV_LORA_RANK, hw)), _full_spec((KV_LORA_RANK, MLA_WIDTH)),
            _row_spec(tm, HEAD_PAD), _row_spec(tm, HEAD_PAD), _row_spec(tm, HEAD_PAD),
        ],
        out_specs=[
            _row_spec(tm, Q_LORA_RANK), _row_spec(tm, KV_LORA_RANK), _row_spec(tm, HEAD_PAD),
            _full_spec((Q_LORA_RANK, hw)), _full_spec((KV_LORA_RANK, hw)), _full_spec((KV_LORA_RANK, MLA_WIDTH)),
            _full_spec((1, Q_LORA_RANK)), _full_spec((1, KV_LORA_RANK)),
        ],
        out_shape=[
            jax.ShapeDtypeStruct((s, Q_LORA_RANK), BF16), jax.ShapeDtypeStruct((s, KV_LORA_RANK), BF16),
            jax.ShapeDtypeStruct((s, HEAD_PAD), BF16),
            jax.ShapeDtypeStruct((Q_LORA_RANK, hw), F32), jax.ShapeDtypeStruct((KV_LORA_RANK, hw), F32),
            jax.ShapeDtypeStruct((KV_LORA_RANK, MLA_WIDTH), F32),
            jax.ShapeDtypeStruct((1, Q_LORA_RANK), F32), jax.ShapeDtypeStruct((1, KV_LORA_RANK), F32),
        ],
        compiler_params=pltpu.CompilerParams(dimension_semantics=("arbitrary",), vmem_limit_bytes=VMEM_LIMIT),
    )(dq, dk_t, dv_t, z_sh, q_norm, kv_norm, wuq_p, wk_p, wv, rc, rsa, rsb)


def _inproj_bwd_x(dzq, dzkv, dzkr, dgattn, ddc, dgpool, dgmerge, x, dh, norm_in, w_in_t, tm):
    s = x.shape[0]
    n_tiles = s // tm
    halo_per_tile = tm // POOL_HALO
    n_halo = s // POOL_HALO
    u_seg = 4

    def body(dzq_ref, dzkv_ref, dzkr_ref, dga_ref, ddc_ref, ddn_ref, dgp_ref, dgm_ref, x_ref, dh_ref,
             g_ref, w_hbm, gx_ref, dgin_ref, dzs_ref, w_vmem, dbuf, sem):
        i = pl.program_id(0)

        @pl.when(i == 0)
        def _():
            cp = pltpu.make_async_copy(w_hbm, w_vmem, sem)
            cp.start()
            dgin_ref[...] = jnp.zeros_like(dgin_ref)
            cp.wait()

        dbuf[0:tm, :] = ddc_ref[...]
        dbuf[tm:, :] = jnp.where(i < n_tiles - 1, ddn_ref[...], 0.0)
        row = lax.broadcasted_iota(jnp.int32, (tm, POOL_GROUP_DIM), 0) + i * tm
        du = []
        for g, w in enumerate(POOL_WINDOWS):
            cols = slice(g * POOL_GROUP_DIM, (g + 1) * POOL_GROUP_DIM)
            fsum = dbuf[0:tm, cols]
            for kk in range(1, w):
                fsum = fsum + dbuf[kk:kk + tm, cols]
            du.append(fsum - dbuf[0:tm, cols] * jnp.minimum(row + 1, w).astype(F32))
        du = jnp.concatenate(du, axis=1).astype(BF16)

        dz = [dzq_ref[...], dzkv_ref[...], dzkr_ref[...], dga_ref[...], du, dgp_ref[...], dgm_ref[...]]
        dhn = None
        for j, pieces in enumerate(SHARD_PIECES):
            parts = [dz[seg][:, lo:hi] for seg, lo, hi, _ in pieces]
            dzj = parts[0] if len(parts) == 1 else jnp.concatenate(parts, axis=1)
            dzs_ref[j] = dzj.T
            part = _dot(dzj, w_vmem[j])
            dhn = part if dhn is None else dhn + part

        g = g_ref[...]
        _, xhat, r = _rms_fwd(x_ref[...], g)
        dgin_ref[...] += _colsum(dhn * xhat)
        gx_ref[...] = dh_ref[...] + _rms_bwd(dhn, xhat, r, g)

    any_spec = pl.BlockSpec(memory_space=pl.ANY)
    seg_w = [wide for _, wide in IN_SEGMENTS]
    return pl.pallas_call(
        body,
        name="inproj_bwd_x",
        grid=(n_tiles,),
        in_specs=[
            _row_spec(tm, seg_w[0]), _row_spec(tm, seg_w[1]), _row_spec(tm, seg_w[2]),
            _row_spec(tm, seg_w[3]), _row_spec(tm, seg_w[u_seg]),
            pl.BlockSpec((POOL_HALO, POOL_WIDTH), lambda i: (jnp.minimum((i + 1) * halo_per_tile, n_halo - 1), 0)),
            _row_spec(tm, seg_w[5]), _row_spec(tm, seg_w[6]),
            _row_spec(tm, D_MODEL), _row_spec(tm, D_MODEL),
            _full_spec((1, D_MODEL)), any_spec,
        ],
        out_specs=[_row_spec(tm, D_MODEL), _full_spec((1, D_MODEL)),
                   pl.BlockSpec((N_CHIPS, SHARD_COLS, tm), lambda i: (0, 0, i))],
        out_shape=[jax.ShapeDtypeStruct((s, D_MODEL), F32), jax.ShapeDtypeStruct((1, D_MODEL), F32),
                   jax.ShapeDtypeStruct((N_CHIPS, SHARD_COLS, s), BF16)],
        scratch_shapes=[
            pltpu.VMEM((N_CHIPS, SHARD_COLS, D_MODEL), BF16),
            pltpu.VMEM((tm + POOL_HALO, POOL_WIDTH), F32),
            pltpu.SemaphoreType.DMA,
        ],
        compiler_params=pltpu.CompilerParams(dimension_semantics=("arbitrary",), vmem_limit_bytes=VMEM_LIMIT),
    )(dzq, dzkv, dzkr, dgattn, ddc, ddc, dgpool, dgmerge, x, dh, norm_in, w_in_t)


def _inproj_bwd_w(order, dz_sh, hn, g_uq, g_ukv, gs, tm):
    s = hn.shape[0]
    n_tiles = s // tm
    hc = D_MODEL // 2
    red = _Reduce(COMM_PARAMS[1:3])

    def body(order_ref, dz_ref, hn_ref, guq_hbm, gukv_hbm, gs_ref, gw_hbm, guq_out, gukv_out, gsum_ref,
             acc, pm_w, a_w, b_w, r_w, s_buf, w_send, w_recv, w_local, *red_scratch):
        ph, i = pl.program_id(0), pl.program_id(1)
        x, y, c = lax.axis_index("x"), lax.axis_index("y"), lax.axis_index("c")
        k = 2 * x + y
        me, sibling = (x, y, c), (x, y, 1 - c)
        chips = _other_chips(x, y)
        shard_of_phase = [2 * cx + cy for cx, cy in chips] + [k]
        copy = _remote_copier(w_send, w_recv)
        red.bind([guq_hbm, gukv_hbm], [guq_out, gukv_out], red_scratch)
        mine = pl.ds(pl.multiple_of(c * hc, hc), hc)
        theirs = pl.ds(pl.multiple_of((1 - c) * hc, hc), hc)
        flips = [(fx, fy, fc) for fx in (0, 1) for fy in (0, 1) for fc in (0, 1)][1:]

        def to_sibling(f):
            j = shard_of_phase[f]
            return copy(f, pm_w.at[j, 1 - c], a_w.at[j], sibling)

        def pair_sum(f):
            cx, cy = chips[f]
            return copy(4 + f, pm_w.at[shard_of_phase[f], c], b_w.at[f], (cx, cy, c))

        def small(f):
            fx, fy, fc = flips[f - 1]
            peer = (1 - x if fx else x, 1 - y if fy else y, 1 - c if fc else c)
            return copy(7 + f, gs_ref, s_buf.at[f], peer)

        def finished():
            return copy(7, r_w, gw_hbm.at[:, mine], sibling)

        @pl.when(jnp.logical_and(ph == 0, i == 0))
        def _():
            red.start()
            for f in range(1, N_DEV):
                small(f).start()
            s_buf[0] = gs_ref[...]

        part = _dot(dz_ref[0], hn_ref[...])

        @pl.when(i == 0)
        def _():
            acc[...] = part

        @pl.when(i > 0)
        def _():
            acc[...] += part

        for f in range(3):
            @pl.when(jnp.logical_and(ph == f + 1, i == 0))
            def _(f=f):
                j = shard_of_phase[f]
                copy(f, a_w.at[j], a_w.at[j], me).wait_recv()
                pm_w[j, c] = (pm_w[j, c].astype(F32) + a_w[j].astype(F32)).astype(BF16)
                pair_sum(f).start()
                if f == 0:
                    red.exchange()

        for f in range(4):
            @pl.when(jnp.logical_and(ph == f, i == n_tiles - 1))
            def _(f=f):
                j = shard_of_phase[f]
                pm_w[j, 0] = acc[:, :hc].astype(BF16)
                pm_w[j, 1] = acc[:, hc:].astype(BF16)
                to_sibling(f).start()
                if f < 3:
                    return
                copy(3, a_w.at[k], a_w.at[k], me).wait_recv()
                r_w[...] = pm_w[k, c].astype(F32) + a_w[k].astype(F32)
                for g in range(3):
                    copy(4 + g, b_w.at[g], b_w.at[g], me).wait_recv()
                    r_w[...] = r_w[...] + b_w[g].astype(F32)
                store = pltpu.make_async_copy(r_w, gw_hbm.at[:, mine], w_local)
                store.start()
                finished().start()
                red.finish()
                for g in range(1, N_DEV):
                    copy(7 + g, s_buf.at[g], s_buf.at[g], me).wait_recv()
                dev = 4 * x + 2 * y + c
                total = s_buf[dev]
                for d in range(1, N_DEV):
                    total = total + s_buf[jnp.bitwise_xor(dev, d)]
                gsum_ref[...] = total
                copy(7, gw_hbm.at[:, theirs], gw_hbm.at[:, theirs], me).wait_recv()
                store.wait()
                for g in range(4):
                    to_sibling(g).wait_send()
                for g in range(3):
                    pair_sum(g).wait_send()
                finished().wait_send()
                for g in range(1, N_DEV):
                    small(g).wait_send()

    any_spec = pl.BlockSpec(memory_space=pl.ANY)
    n_sem = 8 + N_DEV - 1
    grid_spec = pltpu.PrefetchScalarGridSpec(
        num_scalar_prefetch=1,
        grid=(N_CHIPS, n_tiles),
        in_specs=[
            pl.BlockSpec((1, SHARD_COLS, tm), lambda ph, i, order: (order[ph], 0, i)),
            pl.BlockSpec((tm, D_MODEL), lambda ph, i, order: (i, 0)),
            any_spec, any_spec,
            pl.BlockSpec((SMALL_ROWS, LANES), lambda ph, i, order: (0, 0)),
        ],
        out_specs=[any_spec, any_spec, any_spec, pl.BlockSpec((SMALL_ROWS, LANES), lambda ph, i, order: (0, 0))],
        scratch_shapes=[
            pltpu.VMEM((SHARD_COLS, D_MODEL), F32),
            pltpu.VMEM((N_CHIPS, 2, SHARD_COLS, hc), BF16),
            pltpu.VMEM((N_CHIPS, SHARD_COLS, hc), BF16),
            pltpu.VMEM((3, SHARD_COLS, hc), BF16),
            pltpu.VMEM((SHARD_COLS, hc), F32),
            pltpu.VMEM((N_DEV, SMALL_ROWS, LANES), F32),
            pltpu.SemaphoreType.DMA((n_sem,)), pltpu.SemaphoreType.DMA((n_sem,)), pltpu.SemaphoreType.DMA,
        ] + red.scratch,
    )
    out = pl.pallas_call(
        body,
        name="inproj_bwd_w",
        grid_spec=grid_spec,
        out_shape=[jax.ShapeDtypeStruct((SHARD_COLS, D_MODEL), F32)] + red.out_shape
        + [jax.ShapeDtypeStruct((SMALL_ROWS, LANES), F32)],
        compiler_params=pltpu.CompilerParams(dimension_semantics=("arbitrary", "arbitrary"),
                                             vmem_limit_bytes=VMEM_LIMIT),
    )(order, dz_sh, hn, g_uq, g_ukv, gs)
    return out[0], out[1], out[2], out[3]


def _other_chips(x, y):
    return ((1 - x, 1 - y), (1 - x, y), (x, 1 - y))


def _half(ref, axis, size, c, lead=()):
    window = pl.ds(pl.multiple_of(c * size, size), size)
    if axis == 0:
        return ref.at[(*lead, window, slice(None))]
    return ref.at[(*lead, slice(None), window)]


def _half_shape(rows, cols, axis, size):
    return (size, cols) if axis == 0 else (rows, size)


def _remote_copier(send_sems, recv_sems):
    def copy(sem, src, dst, to):
        return pltpu.make_async_remote_copy(src_ref=src, dst_ref=dst, send_sem=send_sems.at[sem],
                                            recv_sem=recv_sems.at[sem], device_id=to, device_id_type=MESH)
    return copy


class _Gather:
    def __init__(self, params):
        self.params = params
        n = len(params)
        self.scratch = [pltpu.SemaphoreType.DMA((6 * n,)), pltpu.SemaphoreType.DMA((6 * n,)),
                        pltpu.SemaphoreType.DMA((n,))]
        self.out_shape = [jax.ShapeDtypeStruct((N_CHIPS, r, cc), BF16) for _, r, cc, _, _ in params]

    def bind(self, ins, outs, scratch):
        self.ins, self.outs = ins, outs
        send_sems, recv_sems, self.local_sems = scratch
        self.copy = _remote_copier(send_sems, recv_sems)
        self.x, self.y, self.c = lax.axis_index("x"), lax.axis_index("y"), lax.axis_index("c")
        self.k = 2 * self.x + self.y
        self.chips = _other_chips(self.x, self.y)

    def _local(self, p):
        return pltpu.make_async_copy(self.ins[p], self.outs[p].at[self.k], self.local_sems.at[p])

    def _first(self, p, j):
        _, _, _, axis, size = self.params[p]
        cx, cy = self.chips[j]
        return self.copy(6 * p + j, _half(self.ins[p], axis, size, self.c),
                         _half(self.outs[p], axis, size, self.c, (self.k,)), (cx, cy, self.c))

    def _relay(self, p, j, half_of):
        _, _, _, axis, size = self.params[p]
        cx, cy = self.chips[j]
        block = _half(self.outs[p], axis, size, half_of, (2 * cx + cy,))
        return self.copy(6 * p + 3 + j, block, block, (self.x, self.y, 1 - self.c))

    def start(self):
        for p in range(len(self.params)):
            self._local(p).start()
            for j in (1, 2, 0):
                self._first(p, j).start()

    def relay_one(self, p, j):
        _, _, _, axis, size = self.params[p]
        cx, cy = self.chips[j]
        landed = _half(self.outs[p], axis, size, self.c, (2 * cx + cy,))
        self.copy(6 * p + j, landed, landed, (self.x, self.y, self.c)).wait_recv()
        self._relay(p, j, self.c).start()

    def await_one(self, p, j):
        self._relay(p, j, 1 - self.c).wait_recv()

    def wait_sends(self):
        for p in range(len(self.params)):
            for j in range(3):
                self._first(p, j).wait_send()
                self._relay(p, j, self.c).wait_send()
            self._local(p).wait()

    def relay(self):
        for j in range(3):
            for p in range(len(self.params)):
                self.relay_one(p, j)

    def finish(self):
        for j in range(3):
            for p in range(len(self.params)):
                self.await_one(p, j)
        self.wait_sends()


class _Reduce:
    def __init__(self, params):
        self.params = params
        n = len(params)
        halves = [_half_shape(r, cc, axis, size) for _, r, cc, axis, size in params]
        self.scratch = ([pltpu.VMEM((N_CHIPS, *h), BF16) for h in halves]
                        + [pltpu.VMEM((N_CHIPS, *h), BF16) for h in halves]
                        + [pltpu.VMEM((3, *h), BF16) for h in halves]
                        + [pltpu.VMEM(h, F32) for h in halves]
                        + [pltpu.SemaphoreType.DMA((5 * n,)), pltpu.SemaphoreType.DMA((5 * n,)),
                           pltpu.SemaphoreType.DMA((2 * n,))])
        self.out_shape = [jax.ShapeDtypeStruct((r, cc), F32) for _, r, cc, _, _ in params]

    def bind(self, g_in, g_out, scratch):
        n = len(self.params)
        self.g_in, self.g_out = g_in, g_out
        self.pm, self.a_buf = scratch[0:n], scratch[n:2 * n]
        self.b_buf, self.r_buf = scratch[2 * n:3 * n], scratch[3 * n:4 * n]
        send_sems, recv_sems, self.local_sems = scratch[4 * n:]
        self.copy = _remote_copier(send_sems, recv_sems)
        self.x, self.y, self.c = lax.axis_index("x"), lax.axis_index("y"), lax.axis_index("c")
        self.k = 2 * self.x + self.y
        self.chips = _other_chips(self.x, self.y)
        self.me = (self.x, self.y, self.c)
        self.sibling = (self.x, self.y, 1 - self.c)

    def _load(self, p):
        _, _, _, axis, size = self.params[p]
        return pltpu.make_async_copy(_half(self.g_in[p], axis, size, self.c, (slice(None),)), self.pm[p],
                                     self.local_sems.at[p])

    def _to_sibling(self, p):
        _, _, _, axis, size = self.params[p]
        return self.copy(5 * p, _half(self.g_in[p], axis, size, 1 - self.c, (slice(None),)), self.a_buf[p],
                         self.sibling)

    def _pair_sum(self, p, j):
        cx, cy = self.chips[j]
        return self.copy(5 * p + 1 + j, self.pm[p].at[2 * cx + cy], self.b_buf[p].at[j], (cx, cy, self.c))

    def _store(self, p):
        _, _, _, axis, size = self.params[p]
        n = len(self.params)
        return pltpu.make_async_copy(self.r_buf[p], _half(self.g_out[p], axis, size, self.c),
                                     self.local_sems.at[n + p])

    def _finished(self, p):
        _, _, _, axis, size = self.params[p]
        return self.copy(5 * p + 4, self.r_buf[p], _half(self.g_out[p], axis, size, self.c), self.sibling)

    def start(self):
        for p in range(len(self.params)):
            self._load(p).start()
            self._to_sibling(p).start()

    def exchange(self):
        for p in range(len(self.params)):
            self._load(p).wait()
            self.copy(5 * p, self.a_buf[p], self.a_buf[p], self.me).wait_recv()
            for j, (cx, cy) in enumerate(self.chips):
                kj = 2 * cx + cy
                self.pm[p][kj] = (self.pm[p][kj].astype(F32) + self.a_buf[p][kj].astype(F32)).astype(BF16)
                self._pair_sum(p, j).start()
            self.r_buf[p][...] = self.pm[p][self.k].astype(F32) + self.a_buf[p][self.k].astype(F32)

    def finish(self):
        for p, (_, _, _, axis, size) in enumerate(self.params):
            for j in range(3):
                self.copy(5 * p + 1 + j, self.b_buf[p].at[j], self.b_buf[p].at[j], self.me).wait_recv()
                self.r_buf[p][...] = self.r_buf[p][...] + self.b_buf[p][j].astype(F32)
            self._store(p).start()
            self._finished(p).start()
        for p, (_, _, _, axis, size) in enumerate(self.params):
            theirs = _half(self.g_out[p], axis, size, 1 - self.c)
            self.copy(5 * p + 4, theirs, theirs, self.me).wait_recv()
            self._store(p).wait()
            self._to_sibling(p).wait_send()
            for j in range(3):
                self._pair_sum(p, j).wait_send()
            self._finished(p).wait_send()


def _adamw_math(w, g, m, v):
    m = ADAM_B1 * m + (1.0 - ADAM_B1) * g
    v = ADAM_B2 * v + (1.0 - ADAM_B2) * (g * g)
    m_hat = m / (1.0 - ADAM_B1 ** ADAM_STEP)
    v_hat = v / (1.0 - ADAM_B2 ** ADAM_STEP)
    delta = -ADAM_LR * (m_hat / (jnp.sqrt(v_hat) + ADAM_EPS) + ADAM_WD * w)
    return delta, m, v


def _adamw_tiled(w, g, m, v, tm):
    rows, cols = w.shape

    def body(w_ref, g_ref, m_ref, v_ref, d_ref, nm_ref, nv_ref):
        d_ref[...], nm_ref[...], nv_ref[...] = _adamw_math(w_ref[...], g_ref[...], m_ref[...], v_ref[...])

    spec = _row_spec(tm, cols)
    return pl.pallas_call(
        body,
        name="adamw_w_in",
        grid=(rows // tm,),
        in_specs=[spec] * 4,
        out_specs=[spec] * 3,
        out_shape=[jax.ShapeDtypeStruct(w.shape, F32)] * 3,
        compiler_params=pltpu.CompilerParams(dimension_semantics=("parallel",), vmem_limit_bytes=VMEM_LIMIT),
    )(w, g, m, v)


def _adamw_many(ws, gs, ms, vs):
    n = len(ws)

    def body(*refs):
        ins, outs = refs[:4 * n], refs[4 * n:]
        for i in range(n):
            d, nm, nv = _adamw_math(ins[i][...], ins[n + i][...], ins[2 * n + i][...], ins[3 * n + i][...])
            outs[i][...] = d
            outs[n + i][...] = nm
            outs[2 * n + i][...] = nv

    vmem_spec = pl.BlockSpec(memory_space=pltpu.VMEM)
    shapes = [jax.ShapeDtypeStruct(w.shape, F32) for w in ws]
    out = pl.pallas_call(
        body,
        name="adamw_small",
        in_specs=[vmem_spec] * (4 * n),
        out_specs=[vmem_spec] * (3 * n),
        out_shape=shapes * 3,
        compiler_params=pltpu.CompilerParams(vmem_limit_bytes=VMEM_LIMIT),
    )(*ws, *gs, *ms, *vs)
    return out[:n], out[n:2 * n], out[2 * n:]


def _pack_rows(parts, rows, dtype):
    flat = jnp.concatenate([p.reshape(-1).astype(dtype) for p in parts])
    flat = jnp.concatenate([flat, jnp.zeros((rows * LANES - flat.shape[0],), dtype)])
    return flat.reshape(rows, LANES)


def _unpack_rows(packed, shapes):
    flat = packed.reshape(-1)
    out, off = [], 0
    for _, shp in shapes:
        n = int(np.prod(shp))
        out.append(flat[off:off + n].reshape(shp))
        off += n
    return out


def _rope_tables(s):
    half = QK_ROPE_DIM // 2
    inv_freq = np.float32(ROPE_THETA) ** (-np.arange(half, dtype=np.float32) / np.float32(half))
    ang = (np.arange(s, dtype=np.float32)[:, None] * inv_freq[None, :]).astype(np.float32)
    cos, sin = np.cos(ang.astype(np.float64)).astype(np.float32), np.sin(ang.astype(np.float64)).astype(np.float32)
    z16 = np.zeros((s, half), np.float32)
    z32 = np.zeros((s, HEAD_PAD - QK_NOPE_DIM - QK_ROPE_DIM), np.float32)
    z64 = np.zeros((s, QK_NOPE_DIM), np.float32)
    rc = np.concatenate([np.ones((s, QK_NOPE_DIM), np.float32), cos, cos, z32], axis=1)
    rsa = np.concatenate([z64, -sin, z16, z32], axis=1)
    rsb = np.concatenate([z64, z16, sin, z32], axis=1)
    return jnp.asarray(rc), jnp.asarray(rsa), jnp.asarray(rsb)


def kernel(x, norm_in, w_in, q_norm, w_uq, kv_norm, w_ukv, pool_w, pool_scale, w_branch_attn, w_branch_pool, w_out, norm_final, loss_target, m_norm_in, m_w_in, m_q_norm, m_w_uq, m_kv_norm, m_w_ukv, m_pool_w, m_pool_scale, m_w_branch_attn, m_w_branch_pool, m_w_out, m_norm_final, v_norm_in, v_w_in, v_q_norm, v_w_uq, v_kv_norm, v_w_ukv, v_pool_w, v_pool_scale, v_w_branch_attn, v_w_branch_pool, v_w_out, v_norm_final):
    s = x.shape[1]
    t_att, t_row = _tiles(s)
    x2 = x.reshape(s, D_MODEL)
    tgt = loss_target.reshape(s, D_MODEL)

    local = [w_in.T, w_uq.reshape(96, 768), w_ukv.reshape(64, 1024), w_branch_attn, w_branch_pool, w_out]
    local = [a.astype(BF16) for a in local]
    cx, cy = lax.axis_index("x"), lax.axis_index("y")
    others = [2 * ox + oy for ox, oy in _other_chips(cx, cy)]
    hn, z_sh, (w_in_t, w_uq_all, w_ukv_all) = _inproj_fwd(
        jnp.stack([2 * cx + cy, others[1], others[2], others[0]]).astype(jnp.int32), x2, norm_in.reshape(1, -1),
        local[:3], 4 * t_row)
    w_uq_f = w_uq_all.reshape(Q_LORA_RANK, MLA_HEADS, QK_NOPE_DIM + QK_ROPE_DIM)
    w_ukv_f = w_ukv_all.reshape(KV_LORA_RANK, MLA_HEADS, QK_NOPE_DIM + V_HEAD_DIM)
    hw = MLA_HEADS * HEAD_PAD
    wuq_p = jnp.pad(w_uq_f, ((0, 0), (0, 0), (0, HEAD_PAD - QK_NOPE_DIM - QK_ROPE_DIM))).reshape(Q_LORA_RANK, hw)
    wk_p = jnp.pad(w_ukv_f[:, :, :QK_NOPE_DIM], ((0, 0), (0, 0), (0, HEAD_PAD - QK_NOPE_DIM))).reshape(KV_LORA_RANK, hw)
    wv = w_ukv_f[:, :, QK_NOPE_DIM:].reshape(KV_LORA_RANK, MLA_WIDTH)
    rc, rsa, rsb = _rope_tables(s)
    g_in = norm_in.reshape(1, -1)
    g_q = q_norm.reshape(1, -1)
    g_kv = kv_norm.reshape(1, -1)
    g_f = norm_final.reshape(1, -1)
    ps = pool_scale.reshape(1, -1)
    pw_bf = pool_w.astype(BF16)

    q, k, v, q_t, v_t = _qkv_fwd(z_sh, g_q, g_kv, wuq_p, wk_p, wv, rc, rsa, rsb, t_row)
    o, lse, (w_ba_all, w_bp_all, w_out_all) = _attn_fwd(q_t, k, v_t, local[3:], t_att)
    w_out_f = w_out_all.reshape(D_MODEL, D_MODEL)

    (do, delta, dgattn, dgpool, dgmerge, ddc, dh, sq_err, d_w_out, d_w_ba, d_w_bp, d_pool_w, d_pool_scale,
     d_norm_final) = _mid(o, z_sh, x2, tgt, pw_bf, ps, w_ba_all, w_bp_all, w_out_f, g_f, t_row)

    late_grads = [d_w_ba, d_w_bp, d_w_out.reshape(N_CHIPS, 256, D_MODEL)]
    dq, dk_t, dv_t, (g_w_ba, g_w_bp, g_w_out) = _attn_bwd(q, q_t, k, v, do, lse, delta, late_grads, t_att)
    dzq, dzkv, dzkr, d_wuq_p, d_wk_p, d_wv, d_q_norm, d_kv_norm = _qkv_bwd(
        dq, dk_t, dv_t, z_sh, g_q, g_kv, wuq_p, wk_p, wv, rc, rsa, rsb, t_row)
    grad_x, d_norm_in, dz_sh = _inproj_bwd_x(dzq, dzkv, dzkr, dgattn, ddc, dgpool, dgmerge, x2, dh, g_in, w_in_t,
                                             t_row)

    d_w_uq = d_wuq_p.reshape(Q_LORA_RANK, MLA_HEADS, HEAD_PAD)[:, :, :QK_NOPE_DIM + QK_ROPE_DIM]
    d_w_ukv = jnp.concatenate([d_wk_p.reshape(KV_LORA_RANK, MLA_HEADS, HEAD_PAD)[:, :, :QK_NOPE_DIM],
                               d_wv.reshape(KV_LORA_RANK, MLA_HEADS, V_HEAD_DIM)], axis=2)
    small = dict(norm_in=d_norm_in, q_norm=d_q_norm, kv_norm=d_kv_norm, pool_scale=d_pool_scale,
                 norm_final=d_norm_final, pool_w=d_pool_w, sq_err=sq_err)
    gs = _pack_rows([small[n] for n, _ in SMALL_SHAPES], SMALL_ROWS, F32)
    order = jnp.stack(others + [2 * cx + cy]).astype(jnp.int32)
    g_w_in_t, g_w_uq, g_w_ukv, g_small = _inproj_bwd_w(
        order, dz_sh, hn, d_w_uq.reshape(N_CHIPS, 96, 768).astype(BF16),
        d_w_ukv.reshape(N_CHIPS, 64, 1024).astype(BF16), gs, 4 * t_row)
    (g_norm_in, g_q_norm, g_kv_norm, g_pool_scale, g_norm_final, g_pool_w,
     sq_err_all) = _unpack_rows(g_small, SMALL_SHAPES)
    g_w_uq = g_w_uq.reshape(w_uq.shape)
    g_w_ukv = g_w_ukv.reshape(w_ukv.shape)

    dl_w_in, nm_w_in, nv_w_in = (a.T for a in _adamw_tiled(w_in.T, g_w_in_t, m_w_in.T, v_w_in.T, 152))

    def two_d(a):
        return a.reshape(1, -1) if a.ndim == 1 else a

    names = ["norm_in", "q_norm", "w_uq", "kv_norm", "w_ukv", "pool_w", "pool_scale", "w_branch_attn",
             "w_branch_pool", "w_out", "norm_final"]
    ws = dict(norm_in=norm_in, q_norm=q_norm, w_uq=w_uq, kv_norm=kv_norm, w_ukv=w_ukv, pool_w=pool_w,
              pool_scale=pool_scale, w_branch_attn=w_branch_attn, w_branch_pool=w_branch_pool, w_out=w_out,
              norm_final=norm_final)
    gsd = dict(norm_in=g_norm_in, q_norm=g_q_norm, w_uq=g_w_uq, kv_norm=g_kv_norm, w_ukv=g_w_ukv, pool_w=g_pool_w,
               pool_scale=g_pool_scale, w_branch_attn=g_w_ba, w_branch_pool=g_w_bp, w_out=g_w_out,
               norm_final=g_norm_final)
    msd = dict(norm_in=m_norm_in, q_norm=m_q_norm, w_uq=m_w_uq, kv_norm=m_kv_norm, w_ukv=m_w_ukv, pool_w=m_pool_w,
               pool_scale=m_pool_scale, w_branch_attn=m_w_branch_attn, w_branch_pool=m_w_branch_pool, w_out=m_w_out,
               norm_final=m_norm_final)
    vsd = dict(norm_in=v_norm_in, q_norm=v_q_norm, w_uq=v_w_uq, kv_norm=v_kv_norm, w_ukv=v_w_ukv, pool_w=v_pool_w,
               pool_scale=v_pool_scale, w_branch_attn=v_w_branch_attn, w_branch_pool=v_w_branch_pool, w_out=v_w_out,
               norm_final=v_norm_final)
    dls, nms, nvs = _adamw_many([two_d(ws[n]) for n in names], [two_d(gsd[n]) for n in names],
                                [two_d(msd[n]) for n in names], [two_d(vsd[n]) for n in names])

    grads = dict(gsd)
    grads["w_in"] = g_w_in_t.T
    delta_w = {n: d.reshape(ws[n].shape) for n, d in zip(names, dls)}
    new_m = {n: d.reshape(ws[n].shape) for n, d in zip(names, nms)}
    new_v = {n: d.reshape(ws[n].shape) for n, d in zip(names, nvs)}
    delta_w["w_in"], new_m["w_in"], new_v["w_in"] = dl_w_in, nm_w_in, nv_w_in
    ws["w_in"] = w_in

    order = ["norm_in", "w_in", "q_norm", "w_uq", "kv_norm", "w_ukv", "pool_w", "pool_scale", "w_branch_attn",
             "w_branch_pool", "w_out", "norm_final"]
    loss = 0.5 * jnp.sum(sq_err_all) / D_MODEL
    return (loss, grad_x.reshape(x.shape),
            *[grads[n].reshape(ws[n].shape) for n in order],
            *[delta_w[n] for n in order], *[new_m[n] for n in order], *[new_v[n] for n in order])
```

```python
import functools

import jax
import jax.numpy as jnp
import numpy as np
from jax import lax
from jax.experimental import pallas as pl
from jax.experimental.pallas import tpu as pltpu

F32 = jnp.float32
BF16 = jnp.bfloat16
MESH = pl.DeviceIdType.MESH

D_MODEL = 1024
CHUNK = 64
MLA_HEADS = 8
QK_NOPE_DIM = 64
QK_ROPE_DIM = 32
V_HEAD_DIM = 64
Q_LORA_RANK = 384
KV_LORA_RANK = 256
MLA_WIDTH = MLA_HEADS * V_HEAD_DIM
ROPE_THETA = 10000.0
POOL_WINDOWS = (2, 4, 8, 16)
POOL_WIDTH = 512
POOL_GROUP_DIM = 128
BRANCH_COLS = D_MODEL // 4
POOL_HALO = 16
EPS = 1e-6
IN_TOTAL = 4256
HEAD_PAD = 128
ATT_SCALE = (QK_NOPE_DIM + QK_ROPE_DIM) ** -0.5
ATT_SCALE_LOG2E = ATT_SCALE * 1.4426950408889634

ADAM_LR = 0.001
ADAM_B1 = 0.9
ADAM_B2 = 0.999
ADAM_EPS = 1e-08
ADAM_WD = 0.01
ADAM_STEP = 10

N_CHIPS = 4
N_DEV = 8
LANES = 128
VMEM_LIMIT = 60 * 1024 * 1024

IN_SEGMENTS = ((384, 384), (256, 256), (32, HEAD_PAD), (512, 512), (512, 512), (512, 512), (2048, 2048))
SHARD_COLS = IN_TOTAL // N_CHIPS
ZQ_COLS = slice(0, 384)
ZKV_COLS = slice(384, 640)
ZKR_TILE = slice(640, 768)


def _shard_pieces():
    bounds, off = [], 0
    for w, _ in IN_SEGMENTS:
        bounds.append((off, off + w))
        off += w
    out = []
    for j in range(N_CHIPS):
        lo, hi = SHARD_COLS * j, SHARD_COLS * (j + 1)
        out.append([(i, max(lo, a) - a, min(hi, b) - a, max(lo, a) - lo)
                    for i, (a, b) in enumerate(bounds) if max(lo, a) < min(hi, b)])
    return out


SHARD_PIECES = _shard_pieces()


def _segment(z_blocks, seg):
    parts = [z_blocks[j][:, col:col + hi - lo]
             for j, pieces in enumerate(SHARD_PIECES) for sg, lo, hi, col in pieces if sg == seg]
    return parts[0] if len(parts) == 1 else jnp.concatenate(parts, axis=1)

COMM_PARAMS = (
    ("w_in", SHARD_COLS, D_MODEL, 1, 512),
    ("w_uq", 96, 768, 0, 48),
    ("w_ukv", 64, 1024, 0, 32),
    ("w_branch_attn", 512, 256, 0, 256),
    ("w_branch_pool", 512, 256, 0, 256),
    ("w_out", 256, 1024, 0, 128),
)

SMALL_MID = (
    ("pool_scale", (512,)),
    ("norm_final", (1024,)),
    ("pool_w", (4, 128, 128)),
    ("sq_err", (8, 128)),
)
SMALL_LATE = (
    ("norm_in", (1024,)),
    ("q_norm", (384,)),
    ("kv_norm", (256,)),
)


def _small_rows(shapes):
    return -(-sum(int(np.prod(s)) for _, s in shapes) // (LANES * 8)) * 8


def _dot(a, b):
    return jnp.dot(a, b, preferred_element_type=F32)


def _dot_nt(a, b):
    return lax.dot_general(a, b, (((1,), (1,)), ((), ())), preferred_element_type=F32)


def _dot_tn(a, b):
    return lax.dot_general(a, b, (((0,), (0,)), ((), ())), preferred_element_type=F32)


def _sigmoid(x):
    return 1.0 / (1.0 + jnp.exp(-x))


def _colsum(x):
    return jnp.sum(x, axis=0, keepdims=True)


def _rms_fwd(x, g):
    r = lax.rsqrt(jnp.mean(x * x, axis=-1, keepdims=True) + EPS)
    xhat = x * r
    return xhat * g, xhat, r


def _rms_bwd(dy, xhat, r, g):
    dxhat = dy * g
    return r * (dxhat - xhat * jnp.mean(dxhat * xhat, axis=-1, keepdims=True))


def _rope(v, c, sa, sb):
    return v * c + pltpu.roll(v, 112, 1) * sa + pltpu.roll(v, 16, 1) * sb


def _unrope(d, c, sa, sb):
    return d * c + pltpu.roll(d * sa, 16, 1) + pltpu.roll(d * sb, 112, 1)


def _row_spec(tm, n):
    return pl.BlockSpec((tm, n), lambda i: (i, 0))


def _full_spec(shape):
    nd = len(shape)
    return pl.BlockSpec(shape, lambda i: (0,) * nd)


def _tiles(s):
    t_att = 512 if s >= 2048 else 128
    t_row = 256 if s >= 1024 else 128
    return t_att, t_row


def _inproj_fwd(order, x, norm_in, early_shards, tm):
    s = x.shape[0]
    n_tiles = s // tm
    gat = _Gather(COMM_PARAMS[:3])
    n_w = len(gat.params)
    arrival = (1, 2, 0)

    def body(order_ref, x_ref, g_ref, *rest):
        w_loc, (hn_ref, z_ref), w_all = rest[:n_w], rest[n_w:n_w + 2], rest[n_w + 2:2 * n_w + 2]
        w_vmem, hn_all, w_sem = rest[2 * n_w + 2:2 * n_w + 5]
        gat.bind(w_loc, w_all, rest[2 * n_w + 5:])
        ph, i = pl.program_id(0), pl.program_id(1)
        pl.when(jnp.logical_and(ph == 0, i == 0))(gat.start)

        @pl.when(jnp.logical_and(ph == 0, i == 0))
        def _():
            cp = pltpu.make_async_copy(w_loc[0], w_vmem, w_sem)
            cp.start()
            cp.wait()

        for f in range(3):
            @pl.when(jnp.logical_and(ph == f + 1, i == 0))
            def _(f=f):
                gat.relay_one(0, arrival[f])
                gat.await_one(0, arrival[f])
                cp = pltpu.make_async_copy(w_all[0].at[order_ref[ph]], w_vmem, w_sem)
                cp.start()
                cp.wait()

        rows = pl.ds(pl.multiple_of(i * tm, tm), tm)

        @pl.when(ph == 0)
        def _():
            hn, _, _ = _rms_fwd(x_ref[...], g_ref[...])
            hn = hn.astype(BF16)
            hn_ref[...] = hn
            hn_all[rows, :] = hn

        z_ref[0] = _dot_nt(hn_all[rows, :], w_vmem[...])

        @pl.when(jnp.logical_and(ph == N_CHIPS - 1, i == n_tiles - 1))
        def _():
            for p in range(1, n_w):
                for j in range(3):
                    gat.relay_one(p, j)
            for p in range(1, n_w):
                for j in range(3):
                    gat.await_one(p, j)
            gat.wait_sends()

    def tile_in_phase0(ph, i, order):
        return (jnp.where(ph == 0, i, n_tiles - 1), 0)

    any_spec = pl.BlockSpec(memory_space=pl.ANY)
    grid_spec = pltpu.PrefetchScalarGridSpec(
        num_scalar_prefetch=1,
        grid=(N_CHIPS, n_tiles),
        in_specs=[pl.BlockSpec((tm, D_MODEL), tile_in_phase0),
                  pl.BlockSpec((1, D_MODEL), lambda ph, i, order: (0, 0))] + [any_spec] * n_w,
        out_specs=[pl.BlockSpec((tm, D_MODEL), tile_in_phase0),
                   pl.BlockSpec((1, tm, SHARD_COLS), lambda ph, i, order: (order[ph], i, 0))] + [any_spec] * n_w,
        scratch_shapes=[pltpu.VMEM((SHARD_COLS, D_MODEL), BF16), pltpu.VMEM((s, D_MODEL), BF16),
                        pltpu.SemaphoreType.DMA] + gat.scratch,
    )
    out = pl.pallas_call(
        body,
        name="inproj_fwd",
        grid_spec=grid_spec,
        out_shape=[jax.ShapeDtypeStruct((s, D_MODEL), BF16), jax.ShapeDtypeStruct((N_CHIPS, s, SHARD_COLS), F32)]
        + gat.out_shape,
        compiler_params=pltpu.CompilerParams(dimension_semantics=("arbitrary", "arbitrary"),
                                             vmem_limit_bytes=VMEM_LIMIT),
    )(order, x, norm_in, *early_shards)
    return out[0], out[1], out[2:]


def _qkv_fwd(z_sh, q_norm, kv_norm, wuq_p, wk_p, wv, rc, rsa, rsb, tm):
    s = z_sh.shape[1]
    hw = MLA_HEADS * HEAD_PAD

    def body(z_ref, gq_ref, gkv_ref, wuq_ref, wk_ref, wv_ref, c_ref, sa_ref, sb_ref,
             q_ref, k_ref, v_ref, qt_ref, vt_ref):
        c, sa, sb = c_ref[...], sa_ref[...], sb_ref[...]
        z0 = z_ref[0]
        cq, _, _ = _rms_fwd(z0[:, ZQ_COLS], gq_ref[...])
        qf = _dot(cq.astype(BF16), wuq_ref[...])
        ckv, _, _ = _rms_fwd(z0[:, ZKV_COLS], gkv_ref[...])
        ckv = ckv.astype(BF16)
        kn = _dot(ckv, wk_ref[...])
        lane = lax.broadcasted_iota(jnp.int32, (tm, HEAD_PAD), 1)
        zkr = jnp.where(lane < QK_ROPE_DIM, z0[:, ZKR_TILE], 0.0)
        kr = _rope(pltpu.roll(zkr, 64, 1), c, sa, sb)
        for h in range(MLA_HEADS):
            cols = slice(h * HEAD_PAD, (h + 1) * HEAD_PAD)
            qh = _rope(qf[:, cols], c, sa, sb)
            q_ref[:, cols] = qh.astype(BF16)
            qt_ref[cols, :] = qh.T.astype(BF16)
            k_ref[:, cols] = (kn[:, cols] + kr).astype(BF16)
        vf = _dot(ckv, wv_ref[...])
        v_ref[...] = vf.astype(BF16)
        vt_ref[...] = vf.T.astype(BF16)

    return pl.pallas_call(
        body,
        name="qkv_fwd",
        grid=(s // tm,),
        in_specs=[
            pl.BlockSpec((1, tm, SHARD_COLS), lambda i: (0, i, 0)),
            _full_spec((1, Q_LORA_RANK)), _full_spec((1, KV_LORA_RANK)),
            _full_spec((Q_LORA_RANK, hw)), _full_spec((KV_LORA_RANK, hw)), _full_spec((KV_LORA_RANK, MLA_WIDTH)),
            _row_spec(tm, HEAD_PAD), _row_spec(tm, HEAD_PAD), _row_spec(tm, HEAD_PAD),
        ],
        out_specs=[_row_spec(tm, hw), _row_spec(tm, hw), _row_spec(tm, MLA_WIDTH),
                   pl.BlockSpec((hw, tm), lambda i: (0, i)), pl.BlockSpec((MLA_WIDTH, tm), lambda i: (0, i))],
        out_shape=[jax.ShapeDtypeStruct((s, hw), BF16), jax.ShapeDtypeStruct((s, hw), BF16),
                   jax.ShapeDtypeStruct((s, MLA_WIDTH), BF16),
                   jax.ShapeDtypeStruct((hw, s), BF16), jax.ShapeDtypeStruct((MLA_WIDTH, s), BF16)],
        compiler_params=pltpu.CompilerParams(dimension_semantics=("parallel",), vmem_limit_bytes=VMEM_LIMIT),
    )(z_sh, q_norm, kv_norm, wuq_p, wk_p, wv, rc, rsa, rsb)


def _chunk_mask(t, keys_on_rows):
    rows = lax.broadcasted_iota(jnp.int32, (t, t), 0) // CHUNK
    cols = lax.broadcasted_iota(jnp.int32, (t, t), 1) // CHUNK
    return rows <= cols if keys_on_rows else cols <= rows


def _attn_fwd(q_t, k, v_t, late_shards, t):
    s = k.shape[0]
    pairs = MLA_HEADS // 2
    n_q = s // t
    gat = _Gather(COMM_PARAMS[3:])
    n_w = len(gat.params)

    def body(qt_ref, k_ref, k2_ref, vt_ref, *rest):
        w_in, (o_ref, lse_ref), w_out = rest[:n_w], rest[n_w:n_w + 2], rest[n_w + 2:2 * n_w + 2]
        gat.bind(w_in, w_out, rest[2 * n_w + 2:])
        i = pl.program_id(1)
        step_no = pl.program_id(0) * n_q + i
        pl.when(step_no == 0)(gat.start)
        pl.when(step_no == n_q)(gat.relay)
        mask = _chunk_mask(t, True)
        qcs = [slice(hh * HEAD_PAD, (hh + 1) * HEAD_PAD) for hh in range(2)]
        vcs = [slice(hh * V_HEAD_DIM, (hh + 1) * V_HEAD_DIM) for hh in range(2)]
        qts = [qt_ref[qc, :] for qc in qcs]

        def step(j, carry, masked):
            keys = pl.ds(pl.multiple_of(j * t, t), t)
            out = []
            for hh in range(2):
                m, l, acc = carry[hh]
                sc = _dot(k_ref[keys, qcs[hh]], qts[hh])
                if masked:
                    sc = jnp.where(mask, sc, -jnp.inf)
                m_new = jnp.maximum(m, jnp.max(sc, axis=0, keepdims=True))
                alpha = jnp.exp2((m - m_new) * ATT_SCALE_LOG2E)
                p = jnp.exp2((_dot(k2_ref[keys, qcs[hh]], qts[hh]) - m_new) * ATT_SCALE_LOG2E)
                if masked:
                    p = jnp.where(mask, p, 0.0)
                l = alpha * l + jnp.sum(p, axis=0, keepdims=True)
                acc = alpha * acc + _dot(vt_ref[vcs[hh], keys], p.astype(BF16))
                out.append((m_new, l, acc))
            return tuple(out)

        one = (jnp.full((1, t), -jnp.inf, F32), jnp.zeros((1, t), F32), jnp.zeros((V_HEAD_DIM, t), F32))
        carry = lax.fori_loop(0, i, functools.partial(step, masked=False), (one, one))
        carry = step(i, carry, True)
        o_ref[...] = jnp.concatenate([carry[hh][2] / carry[hh][1] for hh in range(2)], axis=0).T
        for hh in range(2):
            m, l, _ = carry[hh]
            lse_ref[:, qcs[hh]] = jnp.broadcast_to(m * ATT_SCALE_LOG2E + jnp.log2(l), (HEAD_PAD, t)).T
        pl.when(step_no == pairs * n_q - 1)(gat.finish)

    any_spec = pl.BlockSpec(memory_space=pl.ANY)
    out = pl.pallas_call(
        body,
        name="attn_fwd",
        grid=(pairs, n_q),
        in_specs=[
            pl.BlockSpec((2 * HEAD_PAD, t), lambda p, i: (p, i)),
            pl.BlockSpec((s, 2 * HEAD_PAD), lambda p, i: (0, p)),
            pl.BlockSpec((s, 2 * HEAD_PAD), lambda p, i: (0, p)),
            pl.BlockSpec((2 * V_HEAD_DIM, s), lambda p, i: (p, 0)),
        ] + [any_spec] * n_w,
        out_specs=[
            pl.BlockSpec((t, 2 * V_HEAD_DIM), lambda p, i: (i, p)),
            pl.BlockSpec((t, 2 * HEAD_PAD), lambda p, i: (i, p)),
        ] + [any_spec] * n_w,
        out_shape=[jax.ShapeDtypeStruct((s, MLA_WIDTH), F32), jax.ShapeDtypeStruct((s, MLA_HEADS * HEAD_PAD), F32)]
        + gat.out_shape,
        scratch_shapes=gat.scratch,
        compiler_params=pltpu.CompilerParams(dimension_semantics=("arbitrary", "arbitrary"),
                                             vmem_limit_bytes=VMEM_LIMIT),
    )(q_t, k, k, v_t, *late_shards)
    return out[0], out[1], out[2:]


def _mid(o, z_sh, x, target, pool_w, pool_scale, w_ba, w_bp, w_out, norm_final, tm):
    s = x.shape[0]
    n_tiles = s // tm
    halo_per_tile = tm // POOL_HALO

    def body(o_ref, z0_ref, z1_ref, z1h_ref, z2_ref, z3_ref, x_ref, t_ref, pw_ref, ps_ref, wba_ref, wbp_ref,
             wout_ref, gf_ref,
             do_ref, dl_ref, dga_ref, dgp_ref, dgm_ref, ddc_ref, dh_ref,
             loss_ref, dwout_out, dwba_out, dwbp_out, dpw_ref, dps_ref, dgf_ref,
             ubuf, dwout_ref, dwba_ref, dwbp_ref):
        i = pl.program_id(0)

        @pl.when(i == 0)
        def _():
            loss_ref[...] = jnp.zeros_like(loss_ref)
            dwout_ref[...] = jnp.zeros_like(dwout_ref)
            dwba_ref[...] = jnp.zeros_like(dwba_ref)
            dwbp_ref[...] = jnp.zeros_like(dwbp_ref)
            dpw_ref[...] = jnp.zeros_like(dpw_ref)
            dps_ref[...] = jnp.zeros_like(dps_ref)
            dgf_ref[...] = jnp.zeros_like(dgf_ref)

        zs = [z0_ref[0], z1_ref[0], z2_ref[0], z3_ref[0]]
        o = o_ref[...]
        ga = _segment(zs, 3)
        sga = _sigmoid(ga)
        silu_a = ga * sga
        y_attn = (o * silu_a).astype(BF16)

        ubuf[0:POOL_HALO, :] = jnp.where(i > 0, _segment([None, z1h_ref[0]], 4), 0.0)
        ubuf[POOL_HALO:, :] = _segment(zs, 4)
        row = lax.broadcasted_iota(jnp.int32, (tm, POOL_GROUP_DIM), 0) + i * tm
        ps = ps_ref[...]
        gp = _segment(zs, 5)
        sgp = _sigmoid(gp)
        silu_p = gp * sgp
        d_bf, dm, inv_cnt = [], [], []
        for g, w in enumerate(POOL_WINDOWS):
            cols = slice(g * POOL_GROUP_DIM, (g + 1) * POOL_GROUP_DIM)
            wsum = ubuf[POOL_HALO:, cols]
            for kk in range(1, w):
                wsum = wsum + ubuf[POOL_HALO - kk:POOL_HALO - kk + tm, cols]
            inv = 1.0 / jnp.minimum(row + 1, w).astype(F32)
            dg = (wsum * inv - ubuf[POOL_HALO:, cols]).astype(BF16)
            d_bf.append(dg)
            inv_cnt.append(inv)
            dm.append(_dot(dg, pw_ref[g]))
        dm = jnp.concatenate(dm, axis=1)
        yp = dm * ps
        y_pool = (yp * silu_p).astype(BF16)

        a = jnp.concatenate([_dot(y_attn, wba_ref[j]) for j in range(N_CHIPS)], axis=1)
        p = jnp.concatenate([_dot(y_pool, wbp_ref[j]) for j in range(N_CHIPS)], axis=1)
        gm = _segment(zs, 6)
        gate_a = _sigmoid(gm[:, :D_MODEL])
        gate_p = _sigmoid(gm[:, D_MODEL:])
        merged = (gate_a * a + gate_p * p).astype(BF16)
        h = x_ref[...] + _dot(merged, wout_ref[...])
        gf = gf_ref[...]
        y, xhat, r = _rms_fwd(h, gf)
        err = y - t_ref[...]
        e2 = err * err
        e2 = jnp.sum(e2.reshape(tm // 8, 8, D_MODEL), axis=0)
        acc = e2[:, 0:LANES]
        for cidx in range(1, D_MODEL // LANES):
            acc = acc + e2[:, cidx * LANES:(cidx + 1) * LANES]
        loss_ref[...] += acc

        dy = err * (1.0 / D_MODEL)
        dgf_ref[...] += _colsum(dy * xhat)
        dh = _rms_bwd(dy, xhat, r, gf)
        dh_ref[...] = dh
        dh_bf = dh.astype(BF16)
        dwout_ref[...] += _dot_tn(merged, dh_bf)
        dmerged = _dot_nt(dh_bf, wout_ref[...])
        da = (dmerged * gate_a).astype(BF16)
        dp = (dmerged * gate_p).astype(BF16)
        dgm_ref[:, :D_MODEL] = (dmerged * a * gate_a * (1.0 - gate_a)).astype(BF16)
        dgm_ref[:, D_MODEL:] = (dmerged * p * gate_p * (1.0 - gate_p)).astype(BF16)
        dy_attn = dy_pool = None
        for j in range(N_CHIPS):
            cols = slice(j * BRANCH_COLS, (j + 1) * BRANCH_COLS)
            dwba_ref[j] += _dot_tn(y_attn, da[:, cols])
            dwbp_ref[j] += _dot_tn(y_pool, dp[:, cols])
            pa = _dot_nt(da[:, cols], wba_ref[j])
            pp = _dot_nt(dp[:, cols], wbp_ref[j])
            dy_attn = pa if dy_attn is None else dy_attn + pa
            dy_pool = pp if dy_pool is None else dy_pool + pp

        do = dy_attn * silu_a
        do_ref[...] = do
        dga_ref[...] = (dy_attn * o * (sga * (1.0 + ga * (1.0 - sga)))).astype(BF16)
        doo = do * o
        for hd in range(MLA_HEADS):
            dl = jnp.sum(doo[:, hd * V_HEAD_DIM:(hd + 1) * V_HEAD_DIM], axis=1, keepdims=True)
            dl_ref[:, hd * HEAD_PAD:(hd + 1) * HEAD_PAD] = jnp.broadcast_to(dl, (tm, HEAD_PAD))

        dyp = dy_pool * silu_p
        dgp_ref[...] = (dy_pool * yp * (sgp * (1.0 + gp * (1.0 - sgp)))).astype(BF16)
        dps_ref[...] += _colsum(dyp * dm)
        dmm = (dyp * ps).astype(BF16)
        for g in range(len(POOL_WINDOWS)):
            cols = slice(g * POOL_GROUP_DIM, (g + 1) * POOL_GROUP_DIM)
            dpw_ref[g] += _dot_tn(d_bf[g], dmm[:, cols])
            ddc_ref[:, cols] = _dot_nt(dmm[:, cols], pw_ref[g]) * inv_cnt[g]

        @pl.when(i == n_tiles - 1)
        def _():
            dwout_out[...] = dwout_ref[...].astype(BF16)
            dwba_out[...] = dwba_ref[...].astype(BF16)
            dwbp_out[...] = dwbp_ref[...].astype(BF16)

    row_in = lambda n: _row_spec(tm, n)
    in_specs = [
        row_in(MLA_WIDTH),
        pl.BlockSpec((1, tm, SHARD_COLS), lambda i: (0, i, 0)), pl.BlockSpec((1, tm, SHARD_COLS), lambda i: (1, i, 0)),
        pl.BlockSpec((1, POOL_HALO, SHARD_COLS), lambda i: (1, jnp.maximum(i * halo_per_tile - 1, 0), 0)),
        pl.BlockSpec((1, tm, SHARD_COLS), lambda i: (2, i, 0)), pl.BlockSpec((1, tm, SHARD_COLS), lambda i: (3, i, 0)),
        row_in(D_MODEL), row_in(D_MODEL),
        _full_spec((4, POOL_GROUP_DIM, POOL_GROUP_DIM)), _full_spec((1, POOL_WIDTH)),
        _full_spec((N_CHIPS, MLA_WIDTH, BRANCH_COLS)), _full_spec((N_CHIPS, POOL_WIDTH, BRANCH_COLS)),
        _full_spec((D_MODEL, D_MODEL)), _full_spec((1, D_MODEL)),
    ]
    out_shape = [
        jax.ShapeDtypeStruct((s, MLA_WIDTH), F32),
        jax.ShapeDtypeStruct((s, MLA_HEADS * HEAD_PAD), F32),
        jax.ShapeDtypeStruct((s, MLA_WIDTH), BF16),
        jax.ShapeDtypeStruct((s, POOL_WIDTH), BF16),
        jax.ShapeDtypeStruct((s, 2 * D_MODEL), BF16),
        jax.ShapeDtypeStruct((s, POOL_WIDTH), F32),
        jax.ShapeDtypeStruct((s, D_MODEL), F32),
        jax.ShapeDtypeStruct((8, LANES), F32),
        jax.ShapeDtypeStruct((D_MODEL, D_MODEL), BF16),
        jax.ShapeDtypeStruct((N_CHIPS, MLA_WIDTH, BRANCH_COLS), BF16),
        jax.ShapeDtypeStruct((N_CHIPS, POOL_WIDTH, BRANCH_COLS), BF16),
        jax.ShapeDtypeStruct((4, POOL_GROUP_DIM, POOL_GROUP_DIM), F32),
        jax.ShapeDtypeStruct((1, POOL_WIDTH), F32),
        jax.ShapeDtypeStruct((1, D_MODEL), F32),
    ]
    out_specs = [
        row_in(MLA_WIDTH), row_in(MLA_HEADS * HEAD_PAD), row_in(MLA_WIDTH), row_in(POOL_WIDTH),
        row_in(2 * D_MODEL), row_in(POOL_WIDTH), row_in(D_MODEL),
        _full_spec((8, LANES)), _full_spec((D_MODEL, D_MODEL)), _full_spec((N_CHIPS, MLA_WIDTH, BRANCH_COLS)),
        _full_spec((N_CHIPS, POOL_WIDTH, BRANCH_COLS)), _full_spec((4, POOL_GROUP_DIM, POOL_GROUP_DIM)),
        _full_spec((1, POOL_WIDTH)), _full_spec((1, D_MODEL)),
    ]
    return pl.pallas_call(
        body,
        name="mid",
        grid=(n_tiles,),
        in_specs=in_specs,
        out_specs=out_specs,
        out_shape=out_shape,
        scratch_shapes=[
            pltpu.VMEM((tm + POOL_HALO, POOL_WIDTH), F32),
            pltpu.VMEM((D_MODEL, D_MODEL), F32),
            pltpu.VMEM((N_CHIPS, MLA_WIDTH, BRANCH_COLS), F32),
            pltpu.VMEM((N_CHIPS, POOL_WIDTH, BRANCH_COLS), F32),
        ],
        compiler_params=pltpu.CompilerParams(dimension_semantics=("arbitrary",), vmem_limit_bytes=VMEM_LIMIT),
    )(o, z_sh, z_sh, z_sh, z_sh, z_sh, x, target, pool_w, pool_scale, w_ba, w_bp, w_out, norm_final)


def _attn_bwd(q, q_t, k, v, do, lse, delta, late_grads, gs_mid, t):
    s = q.shape[0]
    pairs = MLA_HEADS // 2
    n_q = s // t
    red = _Reduce(COMM_PARAMS[3:])
    n_w = len(red.params)
    small = _SmallSum(gs_mid.shape[0])
    n_red = len(red.scratch)

    def body(q_ref, qt_ref, do_ref, lse_ref, dl_ref, k_ref, v_ref, *rest):
        g_in, gs_ref = rest[:n_w], rest[n_w]
        (dq_ref, dk_ref, dv_ref), g_out, gsum_ref = rest[n_w + 1:n_w + 4], rest[n_w + 4:2 * n_w + 4], rest[2 * n_w + 4]
        scratch = rest[2 * n_w + 5:]
        red.bind(g_in, g_out, scratch[:n_red])
        small.bind(gs_ref, gsum_ref, scratch[n_red:])
        i = pl.program_id(1)
        step_no = pl.program_id(0) * n_q + i

        @pl.when(step_no == 0)
        def _():
            red.start()
            small.start()

        pl.when(step_no == n_q)(red.exchange)

        @pl.when(i == 0)
        def _():
            dk_ref[...] = jnp.zeros_like(dk_ref)
            dv_ref[...] = jnp.zeros_like(dv_ref)

        mask = _chunk_mask(t, False)
        qcs = [slice(hh * HEAD_PAD, (hh + 1) * HEAD_PAD) for hh in range(2)]
        vcs = [slice(hh * V_HEAD_DIM, (hh + 1) * V_HEAD_DIM) for hh in range(2)]
        qhs = [q_ref[:, qc] for qc in qcs]
        qts = [qt_ref[qc, :] for qc in qcs]
        dohs = [do_ref[:, vc].astype(BF16) for vc in vcs]
        do_t = do_ref[...].T.astype(BF16)
        dots = [do_t[vc, :] for vc in vcs]
        lses = [lse_ref[:, hh * HEAD_PAD:hh * HEAD_PAD + 1] for hh in range(2)]
        dls = [dl_ref[:, hh * HEAD_PAD:hh * HEAD_PAD + 1] for hh in range(2)]

        def step(j, dqs, masked):
            keys = pl.ds(pl.multiple_of(j * t, t), t)
            out = []
            for hh in range(2):
                kj = k_ref[keys, qcs[hh]]
                vj = v_ref[keys, vcs[hh]]
                p = jnp.exp2(_dot_nt(qhs[hh], kj) * ATT_SCALE_LOG2E - lses[hh])
                if masked:
                    p = jnp.where(mask, p, 0.0)
                ds = (p * (_dot_nt(dohs[hh], vj) - dls[hh])).astype(BF16)
                dv_ref[vcs[hh], keys] += _dot(dots[hh], p.astype(BF16))
                dk_ref[qcs[hh], keys] += _dot(qts[hh], ds) * ATT_SCALE
                out.append(dqs[hh] + _dot(ds, kj))
            return tuple(out)

        zero = jnp.zeros((t, HEAD_PAD), F32)
        dqs = lax.fori_loop(0, i, functools.partial(step, masked=False), (zero, zero))
        dqs = step(i, dqs, True)
        for hh in range(2):
            dq_ref[:, qcs[hh]] = dqs[hh] * ATT_SCALE

        @pl.when(step_no == pairs * n_q - 1)
        def _():
            red.finish()
            small.finish()

    hw = MLA_HEADS * HEAD_PAD
    any_spec = pl.BlockSpec(memory_space=pl.ANY)
    out = pl.pallas_call(
        body,
        name="attn_bwd",
        grid=(pairs, n_q),
        in_specs=[
            pl.BlockSpec((t, 2 * HEAD_PAD), lambda p, i: (i, p)),
            pl.BlockSpec((2 * HEAD_PAD, t), lambda p, i: (p, i)),
            pl.BlockSpec((t, 2 * V_HEAD_DIM), lambda p, i: (i, p)),
            pl.BlockSpec((t, 2 * HEAD_PAD), lambda p, i: (i, p)),
            pl.BlockSpec((t, 2 * HEAD_PAD), lambda p, i: (i, p)),
            pl.BlockSpec((s, 2 * HEAD_PAD), lambda p, i: (0, p)),
            pl.BlockSpec((s, 2 * V_HEAD_DIM), lambda p, i: (0, p)),
        ] + [any_spec] * n_w + [pl.BlockSpec(small.spec_shape, lambda p, i: (0, 0))],
        out_specs=[
            pl.BlockSpec((t, 2 * HEAD_PAD), lambda p, i: (i, p)),
            pl.BlockSpec((2 * HEAD_PAD, s), lambda p, i: (p, 0)),
            pl.BlockSpec((2 * V_HEAD_DIM, s), lambda p, i: (p, 0)),
        ] + [any_spec] * n_w + [pl.BlockSpec(small.spec_shape, lambda p, i: (0, 0))],
        out_shape=[jax.ShapeDtypeStruct((s, hw), F32), jax.ShapeDtypeStruct((hw, s), F32),
                   jax.ShapeDtypeStruct((MLA_WIDTH, s), F32)] + red.out_shape + [small.out_shape],
        scratch_shapes=red.scratch + small.scratch,
        compiler_params=pltpu.CompilerParams(dimension_semantics=("arbitrary", "arbitrary"),
                                             vmem_limit_bytes=VMEM_LIMIT),
    )(q, q_t, do, lse, delta, k, v, *late_grads, gs_mid)
    return out[0], out[1], out[2], out[3:3 + n_w], out[3 + n_w]


def _qkv_bwd(dq, dk_t, dv_t, z_sh, q_norm, kv_norm, wuq_p, wk_p, wv, rc, rsa, rsb, tm):
    s = z_sh.shape[1]
    hw = MLA_HEADS * HEAD_PAD

    def body(dq_ref, dk_ref, dv_ref, z_ref, gq_ref, gkv_ref, wuq_ref, wk_ref, wv_ref,
             c_ref, sa_ref, sb_ref,
             dzq_ref, dzkv_ref, dzkr_ref, dwuq_ref, dwk_ref, dwv_ref, dgq_ref, dgkv_ref):
        i = pl.program_id(0)

        @pl.when(i == 0)
        def _():
            dwuq_ref[...] = jnp.zeros_like(dwuq_ref)
            dwk_ref[...] = jnp.zeros_like(dwk_ref)
            dwv_ref[...] = jnp.zeros_like(dwv_ref)
            dgq_ref[...] = jnp.zeros_like(dgq_ref)
            dgkv_ref[...] = jnp.zeros_like(dgkv_ref)

        c, sa, sb = c_ref[...], sa_ref[...], sb_ref[...]
        gq, gkv = gq_ref[...], gkv_ref[...]

        z0 = z_ref[0]
        cq, xq, rq = _rms_fwd(z0[:, ZQ_COLS], gq)
        dqp = jnp.concatenate(
            [_unrope(dq_ref[:, h * HEAD_PAD:(h + 1) * HEAD_PAD], c, sa, sb) for h in range(MLA_HEADS)],
            axis=1).astype(BF16)
        dwuq_ref[...] += _dot_tn(cq.astype(BF16), dqp)
        dcq = _dot_nt(dqp, wuq_ref[...])
        dgq_ref[...] += _colsum(dcq * xq)
        dzq_ref[...] = _rms_bwd(dcq, xq, rq, gq).astype(BF16)

        ckv, xkv, rkv = _rms_fwd(z0[:, ZKV_COLS], gkv)
        ckv = ckv.astype(BF16)
        dkf = dk_ref[...].T
        dk_bf = dkf.astype(BF16)
        dv_bf = dv_ref[...].T.astype(BF16)
        dwk_ref[...] += _dot_tn(ckv, dk_bf)
        dwv_ref[...] += _dot_tn(ckv, dv_bf)
        dckv = _dot_nt(dk_bf, wk_ref[...]) + _dot_nt(dv_bf, wv_ref[...])
        dgkv_ref[...] += _colsum(dckv * xkv)
        dzkv_ref[...] = _rms_bwd(dckv, xkv, rkv, gkv).astype(BF16)

        dkr = dkf[:, 0:HEAD_PAD]
        for h in range(1, MLA_HEADS):
            dkr = dkr + dkf[:, h * HEAD_PAD:(h + 1) * HEAD_PAD]
        dkr = pltpu.roll(_unrope(dkr, c, sa, sb), 64, 1)
        lane = lax.broadcasted_iota(jnp.int32, (tm, HEAD_PAD), 1)
        dzkr_ref[...] = jnp.where(lane < QK_ROPE_DIM, dkr, 0.0).astype(BF16)

    return pl.pallas_call(
        body,
        name="qkv_bwd",
        grid=(s // tm,),
        in_specs=[
            _row_spec(tm, hw), pl.BlockSpec((hw, tm), lambda i: (0, i)), pl.BlockSpec((MLA_WIDTH, tm), lambda i: (0, i)),
            pl.BlockSpec((1, tm, SHARD_COLS), lambda i: (0, i, 0)),
            _full_spec((1, Q_LORA_RANK)), _full_spec((1, KV_LORA_RANK)),
            _full_spec((Q_LORA_RANK, hw)), _full_spec((KV_LORA_RANK, hw)), _full_spec((KV_LORA_RANK, MLA_WIDTH)),
            _row_spec(tm, HEAD_PAD), _row_spec(tm, HEAD_PAD), _row_spec(tm, HEAD_PAD),
        ],
        out_specs=[
            _row_spec(tm, Q_LORA_RANK), _row_spec(tm, KV_LORA_RANK), _row_spec(tm, HEAD_PAD),
            _full_spec((Q_LORA_RANK, hw)), _full_spec((KV_LORA_RANK, hw)), _full_spec((KV_LORA_RANK, MLA_WIDTH)),
            _full_spec((1, Q_LORA_RANK)), _full_spec((1, KV_LORA_RANK)),
        ],
        out_shape=[
            jax.ShapeDtypeStruct((s, Q_LORA_RANK), BF16), jax.ShapeDtypeStruct((s, KV_LORA_RANK), BF16),
            jax.ShapeDtypeStruct((s, HEAD_PAD), BF16),
            jax.ShapeDtypeStruct((Q_LORA_RANK, hw), F32), jax.ShapeDtypeStruct((KV_LORA_RANK, hw), F32),
            jax.ShapeDtypeStruct((KV_LORA_RANK, MLA_WIDTH), F32),
            jax.ShapeDtypeStruct((1, Q_LORA_RANK), F32), jax.ShapeDtypeStruct((1, KV_LORA_RANK), F32),
        ],
        compiler_params=pltpu.CompilerParams(dimension_semantics=("arbitrary",), vmem_limit_bytes=VMEM_LIMIT),
    )(dq, dk_t, dv_t, z_sh, q_norm, kv_norm, wuq_p, wk_p, wv, rc, rsa, rsb)


def _inproj_bwd_x(dzq, dzkv, dzkr, dgattn, ddc, dgpool, dgmerge, x, dh, norm_in, w_in_t, tm):
    s = x.shape[0]
    n_tiles = s // tm
    halo_per_tile = tm // POOL_HALO
    n_halo = s // POOL_HALO
    u_seg = 4

    def body(dzq_ref, dzkv_ref, dzkr_ref, dga_ref, ddc_ref, ddn_ref, dgp_ref, dgm_ref, x_ref, dh_ref,
             g_ref, w_hbm, gx_ref, dgin_ref, dzs_ref, w_vmem, dbuf, sem):
        i = pl.program_id(0)

        @pl.when(i == 0)
        def _():
            cp = pltpu.make_async_copy(w_hbm, w_vmem, sem)
            cp.start()
            dgin_ref[...] = jnp.zeros_like(dgin_ref)
            cp.wait()

        dbuf[0:tm, :] = ddc_ref[...]
        dbuf[tm:, :] = jnp.where(i < n_tiles - 1, ddn_ref[...], 0.0)
        row = lax.broadcasted_iota(jnp.int32, (tm, POOL_GROUP_DIM), 0) + i * tm
        du = []
        for g, w in enumerate(POOL_WINDOWS):
            cols = slice(g * POOL_GROUP_DIM, (g + 1) * POOL_GROUP_DIM)
            fsum = dbuf[0:tm, cols]
            for kk in range(1, w):
                fsum = fsum + dbuf[kk:kk + tm, cols]
            du.append(fsum - dbuf[0:tm, cols] * jnp.minimum(row + 1, w).astype(F32))
        du = jnp.concatenate(du, axis=1).astype(BF16)

        dz = [dzq_ref[...], dzkv_ref[...], dzkr_ref[...], dga_ref[...], du, dgp_ref[...], dgm_ref[...]]
        dhn = None
        for j, pieces in enumerate(SHARD_PIECES):
            parts = [dz[seg][:, lo:hi] for seg, lo, hi, _ in pieces]
            dzj = parts[0] if len(parts) == 1 else jnp.concatenate(parts, axis=1)
            dzs_ref[j] = dzj.T
            part = _dot(dzj, w_vmem[j])
            dhn = part if dhn is None else dhn + part

        g = g_ref[...]
        _, xhat, r = _rms_fwd(x_ref[...], g)
        dgin_ref[...] += _colsum(dhn * xhat)
        gx_ref[...] = dh_ref[...] + _rms_bwd(dhn, xhat, r, g)

    any_spec = pl.BlockSpec(memory_space=pl.ANY)
    seg_w = [wide for _, wide in IN_SEGMENTS]
    return pl.pallas_call(
        body,
        name="inproj_bwd_x",
        grid=(n_tiles,),
        in_specs=[
            _row_spec(tm, seg_w[0]), _row_spec(tm, seg_w[1]), _row_spec(tm, seg_w[2]),
            _row_spec(tm, seg_w[3]), _row_spec(tm, seg_w[u_seg]),
            pl.BlockSpec((POOL_HALO, POOL_WIDTH), lambda i: (jnp.minimum((i + 1) * halo_per_tile, n_halo - 1), 0)),
            _row_spec(tm, seg_w[5]), _row_spec(tm, seg_w[6]),
            _row_spec(tm, D_MODEL), _row_spec(tm, D_MODEL),
            _full_spec((1, D_MODEL)), any_spec,
        ],
        out_specs=[_row_spec(tm, D_MODEL), _full_spec((1, D_MODEL)),
                   pl.BlockSpec((N_CHIPS, SHARD_COLS, tm), lambda i: (0, 0, i))],
        out_shape=[jax.ShapeDtypeStruct((s, D_MODEL), F32), jax.ShapeDtypeStruct((1, D_MODEL), F32),
                   jax.ShapeDtypeStruct((N_CHIPS, SHARD_COLS, s), BF16)],
        scratch_shapes=[
            pltpu.VMEM((N_CHIPS, SHARD_COLS, D_MODEL), BF16),
            pltpu.VMEM((tm + POOL_HALO, POOL_WIDTH), F32),
            pltpu.SemaphoreType.DMA,
        ],
        compiler_params=pltpu.CompilerParams(dimension_semantics=("arbitrary",), vmem_limit_bytes=VMEM_LIMIT),
    )(dzq, dzkv, dzkr, dgattn, ddc, ddc, dgpool, dgmerge, x, dh, norm_in, w_in_t)


def _inproj_bwd_w(order, dz_sh, hn, g_uq, g_ukv, gs, tm):
    s = hn.shape[0]
    n_tiles = s // tm
    hc = D_MODEL // 2
    red = _Reduce(COMM_PARAMS[1:3])
    small = _SmallSum(gs.shape[0])
    n_red = len(red.scratch)

    def body(order_ref, dz_ref, hn_ref, guq_hbm, gukv_hbm, gs_ref, gw_hbm, guq_out, gukv_out, gsum_ref,
             acc, pm_w, a_w, b_w, r_w, w_send, w_recv, w_local, *more_scratch):
        ph, i = pl.program_id(0), pl.program_id(1)
        x, y, c = lax.axis_index("x"), lax.axis_index("y"), lax.axis_index("c")
        k = 2 * x + y
        me, sibling = (x, y, c), (x, y, 1 - c)
        chips = _other_chips(x, y)
        shard_of_phase = [2 * cx + cy for cx, cy in chips] + [k]
        copy = _remote_copier(w_send, w_recv)
        red.bind([guq_hbm, gukv_hbm], [guq_out, gukv_out], more_scratch[:n_red])
        small.bind(gs_ref, gsum_ref, more_scratch[n_red:])
        mine = pl.ds(pl.multiple_of(c * hc, hc), hc)
        theirs = pl.ds(pl.multiple_of((1 - c) * hc, hc), hc)

        def to_sibling(f):
            j = shard_of_phase[f]
            return copy(f, pm_w.at[j, 1 - c], a_w.at[j], sibling)

        def pair_sum(f):
            cx, cy = chips[f]
            return copy(4 + f, pm_w.at[shard_of_phase[f], c], b_w.at[f], (cx, cy, c))

        def finished():
            return copy(7, r_w, gw_hbm.at[:, mine], sibling)

        @pl.when(jnp.logical_and(ph == 0, i == 0))
        def _():
            red.start()
            small.start()

        part = _dot(dz_ref[0], hn_ref[...])

        @pl.when(i == 0)
        def _():
            acc[...] = part

        @pl.when(i > 0)
        def _():
            acc[...] += part

        for f in range(3):
            @pl.when(jnp.logical_and(ph == f + 1, i == 0))
            def _(f=f):
                j = shard_of_phase[f]
                copy(f, a_w.at[j], a_w.at[j], me).wait_recv()
                pm_w[j, c] = (pm_w[j, c].astype(F32) + a_w[j].astype(F32)).astype(BF16)
                pair_sum(f).start()
                if f == 0:
                    red.exchange()

        for f in range(4):
            @pl.when(jnp.logical_and(ph == f, i == n_tiles - 1))
            def _(f=f):
                j = shard_of_phase[f]
                pm_w[j, 0] = acc[:, :hc].astype(BF16)
                pm_w[j, 1] = acc[:, hc:].astype(BF16)
                to_sibling(f).start()
                if f < 3:
                    return
                copy(3, a_w.at[k], a_w.at[k], me).wait_recv()
                r_w[...] = pm_w[k, c].astype(F32) + a_w[k].astype(F32)
                for g in range(3):
                    copy(4 + g, b_w.at[g], b_w.at[g], me).wait_recv()
                    r_w[...] = r_w[...] + b_w[g].astype(F32)
                store = pltpu.make_async_copy(r_w, gw_hbm.at[:, mine], w_local)
                store.start()
                finished().start()
                red.finish()
                small.finish()
                copy(7, gw_hbm.at[:, theirs], gw_hbm.at[:, theirs], me).wait_recv()
                store.wait()
                for g in range(4):
                    to_sibling(g).wait_send()
                for g in range(3):
                    pair_sum(g).wait_send()
                finished().wait_send()

    any_spec = pl.BlockSpec(memory_space=pl.ANY)
    n_sem = 8
    grid_spec = pltpu.PrefetchScalarGridSpec(
        num_scalar_prefetch=1,
        grid=(N_CHIPS, n_tiles),
        in_specs=[
            pl.BlockSpec((1, SHARD_COLS, tm), lambda ph, i, order: (order[ph], 0, i)),
            pl.BlockSpec((tm, D_MODEL), lambda ph, i, order: (i, 0)),
            any_spec, any_spec,
            pl.BlockSpec(small.spec_shape, lambda ph, i, order: (0, 0)),
        ],
        out_specs=[any_spec, any_spec, any_spec, pl.BlockSpec(small.spec_shape, lambda ph, i, order: (0, 0))],
        scratch_shapes=[
            pltpu.VMEM((SHARD_COLS, D_MODEL), F32),
            pltpu.VMEM((N_CHIPS, 2, SHARD_COLS, hc), BF16),
            pltpu.VMEM((N_CHIPS, SHARD_COLS, hc), BF16),
            pltpu.VMEM((3, SHARD_COLS, hc), BF16),
            pltpu.VMEM((SHARD_COLS, hc), F32),
            pltpu.SemaphoreType.DMA((n_sem,)), pltpu.SemaphoreType.DMA((n_sem,)), pltpu.SemaphoreType.DMA,
        ] + red.scratch + small.scratch,
    )
    out = pl.pallas_call(
        body,
        name="inproj_bwd_w",
        grid_spec=grid_spec,
        out_shape=[jax.ShapeDtypeStruct((SHARD_COLS, D_MODEL), F32)] + red.out_shape
        + [small.out_shape],
        compiler_params=pltpu.CompilerParams(dimension_semantics=("arbitrary", "arbitrary"),
                                             vmem_limit_bytes=VMEM_LIMIT),
    )(order, dz_sh, hn, g_uq, g_ukv, gs)
    return out[0], out[1], out[2], out[3]


def _other_chips(x, y):
    return ((1 - x, 1 - y), (1 - x, y), (x, 1 - y))


def _half(ref, axis, size, c, lead=()):
    window = pl.ds(pl.multiple_of(c * size, size), size)
    if axis == 0:
        return ref.at[(*lead, window, slice(None))]
    return ref.at[(*lead, slice(None), window)]


def _half_shape(rows, cols, axis, size):
    return (size, cols) if axis == 0 else (rows, size)


def _remote_copier(send_sems, recv_sems):
    def copy(sem, src, dst, to):
        return pltpu.make_async_remote_copy(src_ref=src, dst_ref=dst, send_sem=send_sems.at[sem],
                                            recv_sem=recv_sems.at[sem], device_id=to, device_id_type=MESH)
    return copy


class _Gather:
    def __init__(self, params):
        self.params = params
        n = len(params)
        self.scratch = [pltpu.SemaphoreType.DMA((6 * n,)), pltpu.SemaphoreType.DMA((6 * n,)),
                        pltpu.SemaphoreType.DMA((n,))]
        self.out_shape = [jax.ShapeDtypeStruct((N_CHIPS, r, cc), BF16) for _, r, cc, _, _ in params]

    def bind(self, ins, outs, scratch):
        self.ins, self.outs = ins, outs
        send_sems, recv_sems, self.local_sems = scratch
        self.copy = _remote_copier(send_sems, recv_sems)
        self.x, self.y, self.c = lax.axis_index("x"), lax.axis_index("y"), lax.axis_index("c")
        self.k = 2 * self.x + self.y
        self.chips = _other_chips(self.x, self.y)

    def _local(self, p):
        return pltpu.make_async_copy(self.ins[p], self.outs[p].at[self.k], self.local_sems.at[p])

    def _first(self, p, j):
        _, _, _, axis, size = self.params[p]
        cx, cy = self.chips[j]
        return self.copy(6 * p + j, _half(self.ins[p], axis, size, self.c),
                         _half(self.outs[p], axis, size, self.c, (self.k,)), (cx, cy, self.c))

    def _relay(self, p, j, half_of):
        _, _, _, axis, size = self.params[p]
        cx, cy = self.chips[j]
        block = _half(self.outs[p], axis, size, half_of, (2 * cx + cy,))
        return self.copy(6 * p + 3 + j, block, block, (self.x, self.y, 1 - self.c))

    def start(self):
        for p in range(len(self.params)):
            self._local(p).start()
            for j in (1, 2, 0):
                self._first(p, j).start()

    def relay_one(self, p, j):
        _, _, _, axis, size = self.params[p]
        cx, cy = self.chips[j]
        landed = _half(self.outs[p], axis, size, self.c, (2 * cx + cy,))
        self.copy(6 * p + j, landed, landed, (self.x, self.y, self.c)).wait_recv()
        self._relay(p, j, self.c).start()

    def await_one(self, p, j):
        self._relay(p, j, 1 - self.c).wait_recv()

    def wait_sends(self):
        for p in range(len(self.params)):
            for j in range(3):
                self._first(p, j).wait_send()
                self._relay(p, j, self.c).wait_send()
            self._local(p).wait()

    def relay(self):
        for j in range(3):
            for p in range(len(self.params)):
                self.relay_one(p, j)

    def finish(self):
        for j in range(3):
            for p in range(len(self.params)):
                self.await_one(p, j)
        self.wait_sends()


class _Reduce:
    def __init__(self, params):
        self.params = params
        n = len(params)
        halves = [_half_shape(r, cc, axis, size) for _, r, cc, axis, size in params]
        self.scratch = ([pltpu.VMEM((N_CHIPS, *h), BF16) for h in halves]
                        + [pltpu.VMEM((N_CHIPS, *h), BF16) for h in halves]
                        + [pltpu.VMEM((3, *h), BF16) for h in halves]
                        + [pltpu.VMEM(h, F32) for h in halves]
                        + [pltpu.SemaphoreType.DMA((5 * n,)), pltpu.SemaphoreType.DMA((5 * n,)),
                           pltpu.SemaphoreType.DMA((2 * n,))])
        self.out_shape = [jax.ShapeDtypeStruct((r, cc), F32) for _, r, cc, _, _ in params]

    def bind(self, g_in, g_out, scratch):
        n = len(self.params)
        self.g_in, self.g_out = g_in, g_out
        self.pm, self.a_buf = scratch[0:n], scratch[n:2 * n]
        self.b_buf, self.r_buf = scratch[2 * n:3 * n], scratch[3 * n:4 * n]
        send_sems, recv_sems, self.local_sems = scratch[4 * n:]
        self.copy = _remote_copier(send_sems, recv_sems)
        self.x, self.y, self.c = lax.axis_index("x"), lax.axis_index("y"), lax.axis_index("c")
        self.k = 2 * self.x + self.y
        self.chips = _other_chips(self.x, self.y)
        self.me = (self.x, self.y, self.c)
        self.sibling = (self.x, self.y, 1 - self.c)

    def _load(self, p):
        _, _, _, axis, size = self.params[p]
        return pltpu.make_async_copy(_half(self.g_in[p], axis, size, self.c, (slice(None),)), self.pm[p],
                                     self.local_sems.at[p])

    def _to_sibling(self, p):
        _, _, _, axis, size = self.params[p]
        return self.copy(5 * p, _half(self.g_in[p], axis, size, 1 - self.c, (slice(None),)), self.a_buf[p],
                         self.sibling)

    def _pair_sum(self, p, j):
        cx, cy = self.chips[j]
        return self.copy(5 * p + 1 + j, self.pm[p].at[2 * cx + cy], self.b_buf[p].at[j], (cx, cy, self.c))

    def _store(self, p):
        _, _, _, axis, size = self.params[p]
        n = len(self.params)
        return pltpu.make_async_copy(self.r_buf[p], _half(self.g_out[p], axis, size, self.c),
                                     self.local_sems.at[n + p])

    def _finished(self, p):
        _, _, _, axis, size = self.params[p]
        return self.copy(5 * p + 4, self.r_buf[p], _half(self.g_out[p], axis, size, self.c), self.sibling)

    def start(self):
        for p in range(len(self.params)):
            self._load(p).start()
            self._to_sibling(p).start()

    def exchange(self):
        for p in range(len(self.params)):
            self._load(p).wait()
            self.copy(5 * p, self.a_buf[p], self.a_buf[p], self.me).wait_recv()
            for j, (cx, cy) in enumerate(self.chips):
                kj = 2 * cx + cy
                self.pm[p][kj] = (self.pm[p][kj].astype(F32) + self.a_buf[p][kj].astype(F32)).astype(BF16)
                self._pair_sum(p, j).start()
            self.r_buf[p][...] = self.pm[p][self.k].astype(F32) + self.a_buf[p][self.k].astype(F32)

    def finish(self):
        for p, (_, _, _, axis, size) in enumerate(self.params):
            for j in range(3):
                self.copy(5 * p + 1 + j, self.b_buf[p].at[j], self.b_buf[p].at[j], self.me).wait_recv()
                self.r_buf[p][...] = self.r_buf[p][...] + self.b_buf[p][j].astype(F32)
            self._store(p).start()
            self._finished(p).start()
        for p, (_, _, _, axis, size) in enumerate(self.params):
            theirs = _half(self.g_out[p], axis, size, 1 - self.c)
            self.copy(5 * p + 4, theirs, theirs, self.me).wait_recv()
            self._store(p).wait()
            self._to_sibling(p).wait_send()
            for j in range(3):
                self._pair_sum(p, j).wait_send()
            self._finished(p).wait_send()


class _SmallSum:
    def __init__(self, rows):
        self.rows = rows
        self.scratch = [pltpu.VMEM((N_DEV, rows, LANES), F32),
                        pltpu.SemaphoreType.DMA((N_DEV - 1,)), pltpu.SemaphoreType.DMA((N_DEV - 1,))]
        self.out_shape = jax.ShapeDtypeStruct((rows, LANES), F32)
        self.spec_shape = (rows, LANES)

    def bind(self, src, dst, scratch):
        self.src, self.dst = src, dst
        self.buf, send_sems, recv_sems = scratch
        self.copy = _remote_copier(send_sems, recv_sems)
        self.x, self.y, self.c = lax.axis_index("x"), lax.axis_index("y"), lax.axis_index("c")

    def _send(self, f):
        fx, fy, fc = [(a, b, d) for a in (0, 1) for b in (0, 1) for d in (0, 1)][f]
        x, y, c = self.x, self.y, self.c
        peer = (1 - x if fx else x, 1 - y if fy else y, 1 - c if fc else c)
        return self.copy(f - 1, self.src, self.buf.at[f], peer)

    def start(self):
        for f in range(1, N_DEV):
            self._send(f).start()
        self.buf[0] = self.src[...]

    def finish(self):
        me = (self.x, self.y, self.c)
        for f in range(1, N_DEV):
            self.copy(f - 1, self.buf.at[f], self.buf.at[f], me).wait_recv()
        dev = 4 * self.x + 2 * self.y + self.c
        total = self.buf[dev]
        for d in range(1, N_DEV):
            total = total + self.buf[jnp.bitwise_xor(dev, d)]
        self.dst[...] = total
        for f in range(1, N_DEV):
            self._send(f).wait_send()


def _adamw_math(w, g, m, v):
    m = ADAM_B1 * m + (1.0 - ADAM_B1) * g
    v = ADAM_B2 * v + (1.0 - ADAM_B2) * (g * g)
    m_hat = m / (1.0 - ADAM_B1 ** ADAM_STEP)
    v_hat = v / (1.0 - ADAM_B2 ** ADAM_STEP)
    delta = -ADAM_LR * (m_hat / (jnp.sqrt(v_hat) + ADAM_EPS) + ADAM_WD * w)
    return delta, m, v


def _adamw_tiled(w, g, m, v, tm):
    rows, cols = w.shape

    def body(w_ref, g_ref, m_ref, v_ref, d_ref, nm_ref, nv_ref):
        d_ref[...], nm_ref[...], nv_ref[...] = _adamw_math(w_ref[...], g_ref[...], m_ref[...], v_ref[...])

    spec = _row_spec(tm, cols)
    return pl.pallas_call(
        body,
        name="adamw_w_in",
        grid=(rows // tm,),
        in_specs=[spec] * 4,
        out_specs=[spec] * 3,
        out_shape=[jax.ShapeDtypeStruct(w.shape, F32)] * 3,
        compiler_params=pltpu.CompilerParams(dimension_semantics=("parallel",), vmem_limit_bytes=VMEM_LIMIT),
    )(w, g, m, v)


def _adamw_many(ws, gs, ms, vs):
    n = len(ws)

    def body(*refs):
        ins, outs = refs[:4 * n], refs[4 * n:]
        for i in range(n):
            d, nm, nv = _adamw_math(ins[i][...], ins[n + i][...], ins[2 * n + i][...], ins[3 * n + i][...])
            outs[i][...] = d
            outs[n + i][...] = nm
            outs[2 * n + i][...] = nv

    vmem_spec = pl.BlockSpec(memory_space=pltpu.VMEM)
    shapes = [jax.ShapeDtypeStruct(w.shape, F32) for w in ws]
    out = pl.pallas_call(
        body,
        name="adamw_small",
        in_specs=[vmem_spec] * (4 * n),
        out_specs=[vmem_spec] * (3 * n),
        out_shape=shapes * 3,
        compiler_params=pltpu.CompilerParams(vmem_limit_bytes=VMEM_LIMIT),
    )(*ws, *gs, *ms, *vs)
    return out[:n], out[n:2 * n], out[2 * n:]


def _pack_rows(parts, rows, dtype):
    flat = jnp.concatenate([p.reshape(-1).astype(dtype) for p in parts])
    flat = jnp.concatenate([flat, jnp.zeros((rows * LANES - flat.shape[0],), dtype)])
    return flat.reshape(rows, LANES)


def _unpack_rows(packed, shapes):
    flat = packed.reshape(-1)
    out, off = [], 0
    for _, shp in shapes:
        n = int(np.prod(shp))
        out.append(flat[off:off + n].reshape(shp))
        off += n
    return out


def _rope_tables(s):
    half = QK_ROPE_DIM // 2
    inv_freq = np.float32(ROPE_THETA) ** (-np.arange(half, dtype=np.float32) / np.float32(half))
    ang = (np.arange(s, dtype=np.float32)[:, None] * inv_freq[None, :]).astype(np.float32)
    cos, sin = np.cos(ang.astype(np.float64)).astype(np.float32), np.sin(ang.astype(np.float64)).astype(np.float32)
    z16 = np.zeros((s, half), np.float32)
    z32 = np.zeros((s, HEAD_PAD - QK_NOPE_DIM - QK_ROPE_DIM), np.float32)
    z64 = np.zeros((s, QK_NOPE_DIM), np.float32)
    rc = np.concatenate([np.ones((s, QK_NOPE_DIM), np.float32), cos, cos, z32], axis=1)
    rsa = np.concatenate([z64, -sin, z16, z32], axis=1)
    rsb = np.concatenate([z64, z16, sin, z32], axis=1)
    return jnp.asarray(rc), jnp.asarray(rsa), jnp.asarray(rsb)


def kernel(x, norm_in, w_in, q_norm, w_uq, kv_norm, w_ukv, pool_w, pool_scale, w_branch_attn, w_branch_pool, w_out, norm_final, loss_target, m_norm_in, m_w_in, m_q_norm, m_w_uq, m_kv_norm, m_w_ukv, m_pool_w, m_pool_scale, m_w_branch_attn, m_w_branch_pool, m_w_out, m_norm_final, v_norm_in, v_w_in, v_q_norm, v_w_uq, v_kv_norm, v_w_ukv, v_pool_w, v_pool_scale, v_w_branch_attn, v_w_branch_pool, v_w_out, v_norm_final):
    s = x.shape[1]
    t_att, t_row = _tiles(s)
    x2 = x.reshape(s, D_MODEL)
    tgt = loss_target.reshape(s, D_MODEL)

    local = [w_in.T, w_uq.reshape(96, 768), w_ukv.reshape(64, 1024), w_branch_attn, w_branch_pool, w_out]
    local = [a.astype(BF16) for a in local]
    cx, cy = lax.axis_index("x"), lax.axis_index("y")
    others = [2 * ox + oy for ox, oy in _other_chips(cx, cy)]
    hn, z_sh, (w_in_t, w_uq_all, w_ukv_all) = _inproj_fwd(
        jnp.stack([2 * cx + cy, others[1], others[2], others[0]]).astype(jnp.int32), x2, norm_in.reshape(1, -1),
        local[:3], 4 * t_row)
    w_uq_f = w_uq_all.reshape(Q_LORA_RANK, MLA_HEADS, QK_NOPE_DIM + QK_ROPE_DIM)
    w_ukv_f = w_ukv_all.reshape(KV_LORA_RANK, MLA_HEADS, QK_NOPE_DIM + V_HEAD_DIM)
    hw = MLA_HEADS * HEAD_PAD
    wuq_p = jnp.pad(w_uq_f, ((0, 0), (0, 0), (0, HEAD_PAD - QK_NOPE_DIM - QK_ROPE_DIM))).reshape(Q_LORA_RANK, hw)
    wk_p = jnp.pad(w_ukv_f[:, :, :QK_NOPE_DIM], ((0, 0), (0, 0), (0, HEAD_PAD - QK_NOPE_DIM))).reshape(KV_LORA_RANK, hw)
    wv = w_ukv_f[:, :, QK_NOPE_DIM:].reshape(KV_LORA_RANK, MLA_WIDTH)
    rc, rsa, rsb = _rope_tables(s)
    g_in = norm_in.reshape(1, -1)
    g_q = q_norm.reshape(1, -1)
    g_kv = kv_norm.reshape(1, -1)
    g_f = norm_final.reshape(1, -1)
    ps = pool_scale.reshape(1, -1)
    pw_bf = pool_w.astype(BF16)

    q, k, v, q_t, v_t = _qkv_fwd(z_sh, g_q, g_kv, wuq_p, wk_p, wv, rc, rsa, rsb, t_row)
    o, lse, (w_ba_all, w_bp_all, w_out_all) = _attn_fwd(q_t, k, v_t, local[3:], t_att)
    w_out_f = w_out_all.reshape(D_MODEL, D_MODEL)

    (do, delta, dgattn, dgpool, dgmerge, ddc, dh, sq_err, d_w_out, d_w_ba, d_w_bp, d_pool_w, d_pool_scale,
     d_norm_final) = _mid(o, z_sh, x2, tgt, pw_bf, ps, w_ba_all, w_bp_all, w_out_f, g_f, t_row)

    late_grads = [d_w_ba, d_w_bp, d_w_out.reshape(N_CHIPS, 256, D_MODEL)]
    small_mid = dict(pool_scale=d_pool_scale, norm_final=d_norm_final, pool_w=d_pool_w, sq_err=sq_err)
    gs_mid = _pack_rows([small_mid[n] for n, _ in SMALL_MID], _small_rows(SMALL_MID), F32)
    dq, dk_t, dv_t, (g_w_ba, g_w_bp, g_w_out), g_small_mid = _attn_bwd(q, q_t, k, v, do, lse, delta, late_grads,
                                                                      gs_mid, t_att)
    g_pool_scale, g_norm_final, g_pool_w, sq_err_all = _unpack_rows(g_small_mid, SMALL_MID)
    dzq, dzkv, dzkr, d_wuq_p, d_wk_p, d_wv, d_q_norm, d_kv_norm = _qkv_bwd(
        dq, dk_t, dv_t, z_sh, g_q, g_kv, wuq_p, wk_p, wv, rc, rsa, rsb, t_row)
    grad_x, d_norm_in, dz_sh = _inproj_bwd_x(dzq, dzkv, dzkr, dgattn, ddc, dgpool, dgmerge, x2, dh, g_in, w_in_t,
                                             t_row)

    d_w_uq = d_wuq_p.reshape(Q_LORA_RANK, MLA_HEADS, HEAD_PAD)[:, :, :QK_NOPE_DIM + QK_ROPE_DIM]
    d_w_ukv = jnp.concatenate([d_wk_p.reshape(KV_LORA_RANK, MLA_HEADS, HEAD_PAD)[:, :, :QK_NOPE_DIM],
                               d_wv.reshape(KV_LORA_RANK, MLA_HEADS, V_HEAD_DIM)], axis=2)
    small_late = dict(norm_in=d_norm_in, q_norm=d_q_norm, kv_norm=d_kv_norm)
    gs = _pack_rows([small_late[n] for n, _ in SMALL_LATE], _small_rows(SMALL_LATE), F32)
    order = jnp.stack(others + [2 * cx + cy]).astype(jnp.int32)
    g_w_in_t, g_w_uq, g_w_ukv, g_small = _inproj_bwd_w(
        order, dz_sh, hn, d_w_uq.reshape(N_CHIPS, 96, 768).astype(BF16),
        d_w_ukv.reshape(N_CHIPS, 64, 1024).astype(BF16), gs, 4 * t_row)
    g_norm_in, g_q_norm, g_kv_norm = _unpack_rows(g_small, SMALL_LATE)
    g_w_uq = g_w_uq.reshape(w_uq.shape)
    g_w_ukv = g_w_ukv.reshape(w_ukv.shape)

    dl_w_in, nm_w_in, nv_w_in = (a.T for a in _adamw_tiled(w_in.T, g_w_in_t, m_w_in.T, v_w_in.T, 152))

    def two_d(a):
        return a.reshape(1, -1) if a.ndim == 1 else a

    names = ["norm_in", "q_norm", "w_uq", "kv_norm", "w_ukv", "pool_w", "pool_scale", "w_branch_attn",
             "w_branch_pool", "w_out", "norm_final"]
    ws = dict(norm_in=norm_in, q_norm=q_norm, w_uq=w_uq, kv_norm=kv_norm, w_ukv=w_ukv, pool_w=pool_w,
              pool_scale=pool_scale, w_branch_attn=w_branch_attn, w_branch_pool=w_branch_pool, w_out=w_out,
              norm_final=norm_final)
    gsd = dict(norm_in=g_norm_in, q_norm=g_q_norm, w_uq=g_w_uq, kv_norm=g_kv_norm, w_ukv=g_w_ukv, pool_w=g_pool_w,
               pool_scale=g_pool_scale, w_branch_attn=g_w_ba, w_branch_pool=g_w_bp, w_out=g_w_out,
               norm_final=g_norm_final)
    msd = dict(norm_in=m_norm_in, q_norm=m_q_norm, w_uq=m_w_uq, kv_norm=m_kv_norm, w_ukv=m_w_ukv, pool_w=m_pool_w,
               pool_scale=m_pool_scale, w_branch_attn=m_w_branch_attn, w_branch_pool=m_w_branch_pool, w_out=m_w_out,
               norm_final=m_norm_final)
    vsd = dict(norm_in=v_norm_in, q_norm=v_q_norm, w_uq=v_w_uq, kv_norm=v_kv_norm, w_ukv=v_w_ukv, pool_w=v_pool_w,
               pool_scale=v_pool_scale, w_branch_attn=v_w_branch_attn, w_branch_pool=v_w_branch_pool, w_out=v_w_out,
               norm_final=v_norm_final)
    dls, nms, nvs = _adamw_many([two_d(ws[n]) for n in names], [two_d(gsd[n]) for n in names],
                                [two_d(msd[n]) for n in names], [two_d(vsd[n]) for n in names])

    grads = dict(gsd)
    grads["w_in"] = g_w_in_t.T
    delta_w = {n: d.reshape(ws[n].shape) for n, d in zip(names, dls)}
    new_m = {n: d.reshape(ws[n].shape) for n, d in zip(names, nms)}
    new_v = {n: d.reshape(ws[n].shape) for n, d in zip(names, nvs)}
    delta_w["w_in"], new_m["w_in"], new_v["w_in"] = dl_w_in, nm_w_in, nv_w_in
    ws["w_in"] = w_in

    order = ["norm_in", "w_in", "q_norm", "w_uq", "kv_norm", "w_ukv", "pool_w", "pool_scale", "w_branch_attn",
             "w_branch_pool", "w_out", "norm_final"]
    loss = 0.5 * jnp.sum(sq_err_all) / D_MODEL
    return (loss, grad_x.reshape(x.shape),
            *[grads[n].reshape(ws[n].shape) for n in order],
            *[delta_w[n] for n in order], *[new_m[n] for n in order], *[new_v[n] for n in order])
```

```python
import functools

import jax
import jax.numpy as jnp
import numpy as np
from jax import lax
from jax.experimental import pallas as pl
from jax.experimental.pallas import tpu as pltpu

F32 = jnp.float32
BF16 = jnp.bfloat16
MESH = pl.DeviceIdType.MESH

D_MODEL = 1024
CHUNK = 64
MLA_HEADS = 8
QK_NOPE_DIM = 64
QK_ROPE_DIM = 32
V_HEAD_DIM = 64
Q_LORA_RANK = 384
KV_LORA_RANK = 256
MLA_WIDTH = MLA_HEADS * V_HEAD_DIM
ROPE_THETA = 10000.0
POOL_WINDOWS = (2, 4, 8, 16)
POOL_WIDTH = 512
POOL_GROUP_DIM = 128
BRANCH_COLS = D_MODEL // 4
ATT_HEADS = 4
POOL_HALO = 16
EPS = 1e-6
IN_TOTAL = 4256
HEAD_PAD = 128
ATT_SCALE = (QK_NOPE_DIM + QK_ROPE_DIM) ** -0.5
ATT_SCALE_LOG2E = ATT_SCALE * 1.4426950408889634

ADAM_LR = 0.001
ADAM_B1 = 0.9
ADAM_B2 = 0.999
ADAM_EPS = 1e-08
ADAM_WD = 0.01
ADAM_STEP = 10

N_CHIPS = 4
N_DEV = 8
LANES = 128
VMEM_LIMIT = 60 * 1024 * 1024

IN_SEGMENTS = ((384, 384), (256, 256), (32, HEAD_PAD), (512, 512), (512, 512), (512, 512), (2048, 2048))
SHARD_COLS = IN_TOTAL // N_CHIPS
ZQ_COLS = slice(0, 384)
ZKV_COLS = slice(384, 640)
ZKR_TILE = slice(640, 768)


def _shard_pieces():
    bounds, off = [], 0
    for w, _ in IN_SEGMENTS:
        bounds.append((off, off + w))
        off += w
    out = []
    for j in range(N_CHIPS):
        lo, hi = SHARD_COLS * j, SHARD_COLS * (j + 1)
        out.append([(i, max(lo, a) - a, min(hi, b) - a, max(lo, a) - lo)
                    for i, (a, b) in enumerate(bounds) if max(lo, a) < min(hi, b)])
    return out


SHARD_PIECES = _shard_pieces()


def _segment(z_blocks, seg):
    parts = [z_blocks[j][:, col:col + hi - lo]
             for j, pieces in enumerate(SHARD_PIECES) for sg, lo, hi, col in pieces if sg == seg]
    return parts[0] if len(parts) == 1 else jnp.concatenate(parts, axis=1)

COMM_PARAMS = (
    ("w_in", SHARD_COLS, D_MODEL, 1, 512),
    ("w_uq", 96, 768, 0, 48),
    ("w_ukv", 64, 1024, 0, 32),
    ("w_branch_attn", 512, 256, 0, 256),
    ("w_branch_pool", 512, 256, 0, 256),
    ("w_out", 256, 1024, 0, 128),
)

SMALL_MID = (
    ("pool_scale", (512,)),
    ("norm_final", (1024,)),
    ("pool_w", (4, 128, 128)),
    ("sq_err", (8, 128)),
)
SMALL_LATE = (
    ("norm_in", (1024,)),
    ("q_norm", (384,)),
    ("kv_norm", (256,)),
)


def _small_rows(shapes):
    return -(-sum(int(np.prod(s)) for _, s in shapes) // (LANES * 8)) * 8


def _dot(a, b):
    return jnp.dot(a, b, preferred_element_type=F32)


def _dot_nt(a, b):
    return lax.dot_general(a, b, (((1,), (1,)), ((), ())), preferred_element_type=F32)


def _dot_tn(a, b):
    return lax.dot_general(a, b, (((0,), (0,)), ((), ())), preferred_element_type=F32)


def _sigmoid(x):
    return 1.0 / (1.0 + jnp.exp(-x))


def _colsum(x):
    return jnp.sum(x, axis=0, keepdims=True)


def _rms_fwd(x, g):
    r = lax.rsqrt(jnp.mean(x * x, axis=-1, keepdims=True) + EPS)
    xhat = x * r
    return xhat * g, xhat, r


def _rms_bwd(dy, xhat, r, g):
    dxhat = dy * g
    return r * (dxhat - xhat * jnp.mean(dxhat * xhat, axis=-1, keepdims=True))


def _rope(v, c, sa, sb):
    return v * c + pltpu.roll(v, 112, 1) * sa + pltpu.roll(v, 16, 1) * sb


def _unrope(d, c, sa, sb):
    return d * c + pltpu.roll(d * sa, 16, 1) + pltpu.roll(d * sb, 112, 1)


def _row_spec(tm, n):
    return pl.BlockSpec((tm, n), lambda i: (i, 0))


def _full_spec(shape):
    nd = len(shape)
    return pl.BlockSpec(shape, lambda i: (0,) * nd)


def _tiles(s):
    t_att = 512 if s >= 2048 else 128
    t_row = 256 if s >= 1024 else 128
    return t_att, t_row


def _inproj_fwd(order, x, norm_in, early_shards, tm):
    s = x.shape[0]
    n_tiles = s // tm
    gat = _Gather(COMM_PARAMS[:3])
    n_w = len(gat.params)
    arrival = (1, 2, 0)

    def body(order_ref, x_ref, g_ref, *rest):
        w_loc, (hn_ref, z_ref), w_all = rest[:n_w], rest[n_w:n_w + 2], rest[n_w + 2:2 * n_w + 2]
        w_vmem, hn_all, w_sem = rest[2 * n_w + 2:2 * n_w + 5]
        gat.bind(w_loc, w_all, rest[2 * n_w + 5:])
        ph, i = pl.program_id(0), pl.program_id(1)
        pl.when(jnp.logical_and(ph == 0, i == 0))(gat.start)

        @pl.when(jnp.logical_and(ph == 0, i == 0))
        def _():
            cp = pltpu.make_async_copy(w_loc[0], w_vmem, w_sem)
            cp.start()
            cp.wait()

        for f in range(3):
            @pl.when(jnp.logical_and(ph == f + 1, i == 0))
            def _(f=f):
                gat.relay_one(0, arrival[f])
                gat.await_one(0, arrival[f])
                cp = pltpu.make_async_copy(w_all[0].at[order_ref[ph]], w_vmem, w_sem)
                cp.start()
                cp.wait()

        rows = pl.ds(pl.multiple_of(i * tm, tm), tm)

        @pl.when(ph == 0)
        def _():
            hn, _, _ = _rms_fwd(x_ref[...], g_ref[...])
            hn = hn.astype(BF16)
            hn_ref[...] = hn
            hn_all[rows, :] = hn

        z_ref[0] = _dot_nt(hn_all[rows, :], w_vmem[...])

        @pl.when(jnp.logical_and(ph == N_CHIPS - 1, i == n_tiles - 1))
        def _():
            for p in range(1, n_w):
                for j in range(3):
                    gat.relay_one(p, j)
            for p in range(1, n_w):
                for j in range(3):
                    gat.await_one(p, j)
            gat.wait_sends()

    def tile_in_phase0(ph, i, order):
        return (jnp.where(ph == 0, i, n_tiles - 1), 0)

    any_spec = pl.BlockSpec(memory_space=pl.ANY)
    grid_spec = pltpu.PrefetchScalarGridSpec(
        num_scalar_prefetch=1,
        grid=(N_CHIPS, n_tiles),
        in_specs=[pl.BlockSpec((tm, D_MODEL), tile_in_phase0),
                  pl.BlockSpec((1, D_MODEL), lambda ph, i, order: (0, 0))] + [any_spec] * n_w,
        out_specs=[pl.BlockSpec((tm, D_MODEL), tile_in_phase0),
                   pl.BlockSpec((1, tm, SHARD_COLS), lambda ph, i, order: (order[ph], i, 0))] + [any_spec] * n_w,
        scratch_shapes=[pltpu.VMEM((SHARD_COLS, D_MODEL), BF16), pltpu.VMEM((s, D_MODEL), BF16),
                        pltpu.SemaphoreType.DMA] + gat.scratch,
    )
    out = pl.pallas_call(
        body,
        name="inproj_fwd",
        grid_spec=grid_spec,
        out_shape=[jax.ShapeDtypeStruct((s, D_MODEL), BF16), jax.ShapeDtypeStruct((N_CHIPS, s, SHARD_COLS), F32)]
        + gat.out_shape,
        compiler_params=pltpu.CompilerParams(dimension_semantics=("arbitrary", "arbitrary"),
                                             vmem_limit_bytes=VMEM_LIMIT),
    )(order, x, norm_in, *early_shards)
    return out[0], out[1], out[2:]


def _qkv_fwd(z_sh, q_norm, kv_norm, wuq_p, wk_p, wv, rc, rsa, rsb, tm):
    s = z_sh.shape[1]
    hw = MLA_HEADS * HEAD_PAD

    def body(z_ref, gq_ref, gkv_ref, wuq_ref, wk_ref, wv_ref, c_ref, sa_ref, sb_ref,
             q_ref, k_ref, v_ref, qt_ref, vt_ref):
        c, sa, sb = c_ref[...], sa_ref[...], sb_ref[...]
        z0 = z_ref[0]
        cq, _, _ = _rms_fwd(z0[:, ZQ_COLS], gq_ref[...])
        qf = _dot(cq.astype(BF16), wuq_ref[...])
        ckv, _, _ = _rms_fwd(z0[:, ZKV_COLS], gkv_ref[...])
        ckv = ckv.astype(BF16)
        kn = _dot(ckv, wk_ref[...])
        lane = lax.broadcasted_iota(jnp.int32, (tm, HEAD_PAD), 1)
        zkr = jnp.where(lane < QK_ROPE_DIM, z0[:, ZKR_TILE], 0.0)
        kr = _rope(pltpu.roll(zkr, 64, 1), c, sa, sb)
        for h in range(MLA_HEADS):
            cols = slice(h * HEAD_PAD, (h + 1) * HEAD_PAD)
            qh = _rope(qf[:, cols], c, sa, sb)
            q_ref[:, cols] = qh.astype(BF16)
            qt_ref[cols, :] = qh.T.astype(BF16)
            k_ref[:, cols] = (kn[:, cols] + kr).astype(BF16)
        vf = _dot(ckv, wv_ref[...])
        v_ref[...] = vf.astype(BF16)
        vt_ref[...] = vf.T.astype(BF16)

    return pl.pallas_call(
        body,
        name="qkv_fwd",
        grid=(s // tm,),
        in_specs=[
            pl.BlockSpec((1, tm, SHARD_COLS), lambda i: (0, i, 0)),
            _full_spec((1, Q_LORA_RANK)), _full_spec((1, KV_LORA_RANK)),
            _full_spec((Q_LORA_RANK, hw)), _full_spec((KV_LORA_RANK, hw)), _full_spec((KV_LORA_RANK, MLA_WIDTH)),
            _row_spec(tm, HEAD_PAD), _row_spec(tm, HEAD_PAD), _row_spec(tm, HEAD_PAD),
        ],
        out_specs=[_row_spec(tm, hw), _row_spec(tm, hw), _row_spec(tm, MLA_WIDTH),
                   pl.BlockSpec((hw, tm), lambda i: (0, i)), pl.BlockSpec((MLA_WIDTH, tm), lambda i: (0, i))],
        out_shape=[jax.ShapeDtypeStruct((s, hw), BF16), jax.ShapeDtypeStruct((s, hw), BF16),
                   jax.ShapeDtypeStruct((s, MLA_WIDTH), BF16),
                   jax.ShapeDtypeStruct((hw, s), BF16), jax.ShapeDtypeStruct((MLA_WIDTH, s), BF16)],
        compiler_params=pltpu.CompilerParams(dimension_semantics=("parallel",), vmem_limit_bytes=VMEM_LIMIT),
    )(z_sh, q_norm, kv_norm, wuq_p, wk_p, wv, rc, rsa, rsb)


def _chunk_mask(t, keys_on_rows):
    rows = lax.broadcasted_iota(jnp.int32, (t, t), 0) // CHUNK
    cols = lax.broadcasted_iota(jnp.int32, (t, t), 1) // CHUNK
    return rows <= cols if keys_on_rows else cols <= rows


def _attn_fwd(q_t, k, v_t, late_shards, t):
    s = k.shape[0]
    groups = MLA_HEADS // ATT_HEADS
    n_q = s // t
    gat = _Gather(COMM_PARAMS[3:])
    n_w = len(gat.params)

    def body(qt_ref, k_ref, k2_ref, vt_ref, *rest):
        w_in, (o_ref, lse_ref), w_out = rest[:n_w], rest[n_w:n_w + 2], rest[n_w + 2:2 * n_w + 2]
        gat.bind(w_in, w_out, rest[2 * n_w + 2:])
        i = pl.program_id(1)
        step_no = pl.program_id(0) * n_q + i
        pl.when(step_no == 0)(gat.start)
        pl.when(step_no == n_q)(gat.relay)
        mask = _chunk_mask(t, True)
        qcs = [slice(hh * HEAD_PAD, (hh + 1) * HEAD_PAD) for hh in range(ATT_HEADS)]
        vcs = [slice(hh * V_HEAD_DIM, (hh + 1) * V_HEAD_DIM) for hh in range(ATT_HEADS)]
        qts = [qt_ref[qc, :] for qc in qcs]

        def step(j, carry, masked):
            keys = pl.ds(pl.multiple_of(j * t, t), t)
            out = []
            for hh in range(ATT_HEADS):
                m, l, acc = carry[hh]
                sc = _dot(k_ref[keys, qcs[hh]], qts[hh])
                if masked:
                    sc = jnp.where(mask, sc, -jnp.inf)
                m_new = jnp.maximum(m, jnp.max(sc, axis=0, keepdims=True))
                alpha = jnp.exp2((m - m_new) * ATT_SCALE_LOG2E)
                p = jnp.exp2((_dot(k2_ref[keys, qcs[hh]], qts[hh]) - m_new) * ATT_SCALE_LOG2E)
                if masked:
                    p = jnp.where(mask, p, 0.0)
                l = alpha * l + jnp.sum(p, axis=0, keepdims=True)
                acc = alpha * acc + _dot(vt_ref[vcs[hh], keys], p.astype(BF16))
                out.append((m_new, l, acc))
            return tuple(out)

        one = (jnp.full((1, t), -jnp.inf, F32), jnp.zeros((1, t), F32), jnp.zeros((V_HEAD_DIM, t), F32))
        carry = lax.fori_loop(0, i, functools.partial(step, masked=False), (one,) * ATT_HEADS)
        carry = step(i, carry, True)
        o_ref[...] = jnp.concatenate([carry[hh][2] / carry[hh][1] for hh in range(ATT_HEADS)], axis=0).T
        for hh in range(ATT_HEADS):
            m, l, _ = carry[hh]
            lse_ref[:, qcs[hh]] = jnp.broadcast_to(m * ATT_SCALE_LOG2E + jnp.log2(l), (HEAD_PAD, t)).T
        pl.when(step_no == groups * n_q - 1)(gat.finish)

    any_spec = pl.BlockSpec(memory_space=pl.ANY)
    out = pl.pallas_call(
        body,
        name="attn_fwd",
        grid=(groups, n_q),
        in_specs=[
            pl.BlockSpec((ATT_HEADS * HEAD_PAD, t), lambda p, i: (p, i)),
            pl.BlockSpec((s, ATT_HEADS * HEAD_PAD), lambda p, i: (0, p)),
            pl.BlockSpec((s, ATT_HEADS * HEAD_PAD), lambda p, i: (0, p)),
            pl.BlockSpec((ATT_HEADS * V_HEAD_DIM, s), lambda p, i: (p, 0)),
        ] + [any_spec] * n_w,
        out_specs=[
            pl.BlockSpec((t, ATT_HEADS * V_HEAD_DIM), lambda p, i: (i, p)),
            pl.BlockSpec((t, ATT_HEADS * HEAD_PAD), lambda p, i: (i, p)),
        ] + [any_spec] * n_w,
        out_shape=[jax.ShapeDtypeStruct((s, MLA_WIDTH), F32), jax.ShapeDtypeStruct((s, MLA_HEADS * HEAD_PAD), F32)]
        + gat.out_shape,
        scratch_shapes=gat.scratch,
        compiler_params=pltpu.CompilerParams(dimension_semantics=("arbitrary", "arbitrary"),
                                             vmem_limit_bytes=VMEM_LIMIT),
    )(q_t, k, k, v_t, *late_shards)
    return out[0], out[1], out[2:]


def _mid(o, z_sh, x, target, pool_w, pool_scale, w_ba, w_bp, w_out, norm_final, tm):
    s = x.shape[0]
    n_tiles = s // tm
    halo_per_tile = tm // POOL_HALO

    def body(o_ref, z0_ref, z1_ref, z1h_ref, z2_ref, z3_ref, x_ref, t_ref, pw_ref, ps_ref, wba_ref, wbp_ref,
             wout_ref, gf_ref,
             do_ref, dl_ref, dga_ref, dgp_ref, dgm_ref, ddc_ref, dh_ref,
             loss_ref, dwout_out, dwba_out, dwbp_out, dpw_ref, dps_ref, dgf_ref,
             ubuf, dwout_ref, dwba_ref, dwbp_ref):
        i = pl.program_id(0)

        @pl.when(i == 0)
        def _():
            loss_ref[...] = jnp.zeros_like(loss_ref)
            dwout_ref[...] = jnp.zeros_like(dwout_ref)
            dwba_ref[...] = jnp.zeros_like(dwba_ref)
            dwbp_ref[...] = jnp.zeros_like(dwbp_ref)
            dpw_ref[...] = jnp.zeros_like(dpw_ref)
            dps_ref[...] = jnp.zeros_like(dps_ref)
            dgf_ref[...] = jnp.zeros_like(dgf_ref)

        zs = [z0_ref[0], z1_ref[0], z2_ref[0], z3_ref[0]]
        o = o_ref[...]
        ga = _segment(zs, 3)
        sga = _sigmoid(ga)
        silu_a = ga * sga
        y_attn = (o * silu_a).astype(BF16)

        ubuf[0:POOL_HALO, :] = jnp.where(i > 0, _segment([None, z1h_ref[0]], 4), 0.0)
        ubuf[POOL_HALO:, :] = _segment(zs, 4)
        row = lax.broadcasted_iota(jnp.int32, (tm, POOL_GROUP_DIM), 0) + i * tm
        ps = ps_ref[...]
        gp = _segment(zs, 5)
        sgp = _sigmoid(gp)
        silu_p = gp * sgp
        d_bf, dm, inv_cnt = [], [], []
        for g, w in enumerate(POOL_WINDOWS):
            cols = slice(g * POOL_GROUP_DIM, (g + 1) * POOL_GROUP_DIM)
            wsum = ubuf[POOL_HALO:, cols]
            for kk in range(1, w):
                wsum = wsum + ubuf[POOL_HALO - kk:POOL_HALO - kk + tm, cols]
            inv = 1.0 / jnp.minimum(row + 1, w).astype(F32)
            dg = (wsum * inv - ubuf[POOL_HALO:, cols]).astype(BF16)
            d_bf.append(dg)
            inv_cnt.append(inv)
            dm.append(_dot(dg, pw_ref[g]))
        dm = jnp.concatenate(dm, axis=1)
        yp = dm * ps
        y_pool = (yp * silu_p).astype(BF16)

        a = jnp.concatenate([_dot(y_attn, wba_ref[j]) for j in range(N_CHIPS)], axis=1)
        p = jnp.concatenate([_dot(y_pool, wbp_ref[j]) for j in range(N_CHIPS)], axis=1)
        gm = _segment(zs, 6)
        gate_a = _sigmoid(gm[:, :D_MODEL])
        gate_p = _sigmoid(gm[:, D_MODEL:])
        merged = (gate_a * a + gate_p * p).astype(BF16)
        h = x_ref[...] + _dot(merged, wout_ref[...])
        gf = gf_ref[...]
        y, xhat, r = _rms_fwd(h, gf)
        err = y - t_ref[...]
        e2 = err * err
        e2 = jnp.sum(e2.reshape(tm // 8, 8, D_MODEL), axis=0)
        acc = e2[:, 0:LANES]
        for cidx in range(1, D_MODEL // LANES):
            acc = acc + e2[:, cidx * LANES:(cidx + 1) * LANES]
        loss_ref[...] += acc

        dy = err * (1.0 / D_MODEL)
        dgf_ref[...] += _colsum(dy * xhat)
        dh = _rms_bwd(dy, xhat, r, gf)
        dh_ref[...] = dh
        dh_bf = dh.astype(BF16)
        dwout_ref[...] += _dot_tn(merged, dh_bf)
        dmerged = _dot_nt(dh_bf, wout_ref[...])
        da = (dmerged * gate_a).astype(BF16)
        dp = (dmerged * gate_p).astype(BF16)
        dgm_ref[:, :D_MODEL] = (dmerged * a * gate_a * (1.0 - gate_a)).astype(BF16)
        dgm_ref[:, D_MODEL:] = (dmerged * p * gate_p * (1.0 - gate_p)).astype(BF16)
        dy_attn = dy_pool = None
        for j in range(N_CHIPS):
            cols = slice(j * BRANCH_COLS, (j + 1) * BRANCH_COLS)
            dwba_ref[j] += _dot_tn(y_attn, da[:, cols])
            dwbp_ref[j] += _dot_tn(y_pool, dp[:, cols])
            pa = _dot_nt(da[:, cols], wba_ref[j])
            pp = _dot_nt(dp[:, cols], wbp_ref[j])
            dy_attn = pa if dy_attn is None else dy_attn + pa
            dy_pool = pp if dy_pool is None else dy_pool + pp

        do = dy_attn * silu_a
        do_ref[...] = do
        dga_ref[...] = (dy_attn * o * (sga * (1.0 + ga * (1.0 - sga)))).astype(BF16)
        doo = do * o
        for hd in range(MLA_HEADS):
            dl = jnp.sum(doo[:, hd * V_HEAD_DIM:(hd + 1) * V_HEAD_DIM], axis=1, keepdims=True)
            dl_ref[:, hd * HEAD_PAD:(hd + 1) * HEAD_PAD] = jnp.broadcast_to(dl, (tm, HEAD_PAD))

        dyp = dy_pool * silu_p
        dgp_ref[...] = (dy_pool * yp * (sgp * (1.0 + gp * (1.0 - sgp)))).astype(BF16)
        dps_ref[...] += _colsum(dyp * dm)
        dmm = (dyp * ps).astype(BF16)
        for g in range(len(POOL_WINDOWS)):
            cols = slice(g * POOL_GROUP_DIM, (g + 1) * POOL_GROUP_DIM)
            dpw_ref[g] += _dot_tn(d_bf[g], dmm[:, cols])
            ddc_ref[:, cols] = _dot_nt(dmm[:, cols], pw_ref[g]) * inv_cnt[g]

        @pl.when(i == n_tiles - 1)
        def _():
            dwout_out[...] = dwout_ref[...].astype(BF16)
            dwba_out[...] = dwba_ref[...].astype(BF16)
            dwbp_out[...] = dwbp_ref[...].astype(BF16)

    row_in = lambda n: _row_spec(tm, n)
    in_specs = [
        row_in(MLA_WIDTH),
        pl.BlockSpec((1, tm, SHARD_COLS), lambda i: (0, i, 0)), pl.BlockSpec((1, tm, SHARD_COLS), lambda i: (1, i, 0)),
        pl.BlockSpec((1, POOL_HALO, SHARD_COLS), lambda i: (1, jnp.maximum(i * halo_per_tile - 1, 0), 0)),
        pl.BlockSpec((1, tm, SHARD_COLS), lambda i: (2, i, 0)), pl.BlockSpec((1, tm, SHARD_COLS), lambda i: (3, i, 0)),
        row_in(D_MODEL), row_in(D_MODEL),
        _full_spec((4, POOL_GROUP_DIM, POOL_GROUP_DIM)), _full_spec((1, POOL_WIDTH)),
        _full_spec((N_CHIPS, MLA_WIDTH, BRANCH_COLS)), _full_spec((N_CHIPS, POOL_WIDTH, BRANCH_COLS)),
        _full_spec((D_MODEL, D_MODEL)), _full_spec((1, D_MODEL)),
    ]
    out_shape = [
        jax.ShapeDtypeStruct((s, MLA_WIDTH), F32),
        jax.ShapeDtypeStruct((s, MLA_HEADS * HEAD_PAD), F32),
        jax.ShapeDtypeStruct((s, MLA_WIDTH), BF16),
        jax.ShapeDtypeStruct((s, POOL_WIDTH), BF16),
        jax.ShapeDtypeStruct((s, 2 * D_MODEL), BF16),
        jax.ShapeDtypeStruct((s, POOL_WIDTH), F32),
        jax.ShapeDtypeStruct((s, D_MODEL), F32),
        jax.ShapeDtypeStruct((8, LANES), F32),
        jax.ShapeDtypeStruct((D_MODEL, D_MODEL), BF16),
        jax.ShapeDtypeStruct((N_CHIPS, MLA_WIDTH, BRANCH_COLS), BF16),
        jax.ShapeDtypeStruct((N_CHIPS, POOL_WIDTH, BRANCH_COLS), BF16),
        jax.ShapeDtypeStruct((4, POOL_GROUP_DIM, POOL_GROUP_DIM), F32),
        jax.ShapeDtypeStruct((1, POOL_WIDTH), F32),
        jax.ShapeDtypeStruct((1, D_MODEL), F32),
    ]
    out_specs = [
        row_in(MLA_WIDTH), row_in(MLA_HEADS * HEAD_PAD), row_in(MLA_WIDTH), row_in(POOL_WIDTH),
        row_in(2 * D_MODEL), row_in(POOL_WIDTH), row_in(D_MODEL),
        _full_spec((8, LANES)), _full_spec((D_MODEL, D_MODEL)), _full_spec((N_CHIPS, MLA_WIDTH, BRANCH_COLS)),
        _full_spec((N_CHIPS, POOL_WIDTH, BRANCH_COLS)), _full_spec((4, POOL_GROUP_DIM, POOL_GROUP_DIM)),
        _full_spec((1, POOL_WIDTH)), _full_spec((1, D_MODEL)),
    ]
    return pl.pallas_call(
        body,
        name="mid",
        grid=(n_tiles,),
        in_specs=in_specs,
        out_specs=out_specs,
        out_shape=out_shape,
        scratch_shapes=[
            pltpu.VMEM((tm + POOL_HALO, POOL_WIDTH), F32),
            pltpu.VMEM((D_MODEL, D_MODEL), F32),
            pltpu.VMEM((N_CHIPS, MLA_WIDTH, BRANCH_COLS), F32),
            pltpu.VMEM((N_CHIPS, POOL_WIDTH, BRANCH_COLS), F32),
        ],
        compiler_params=pltpu.CompilerParams(dimension_semantics=("arbitrary",), vmem_limit_bytes=VMEM_LIMIT),
    )(o, z_sh, z_sh, z_sh, z_sh, z_sh, x, target, pool_w, pool_scale, w_ba, w_bp, w_out, norm_final)


def _attn_bwd(q, q_t, k, v, do, lse, delta, late_grads, gs_mid, t):
    s = q.shape[0]
    groups = MLA_HEADS // ATT_HEADS
    n_q = s // t
    red = _Reduce(COMM_PARAMS[3:])
    n_w = len(red.params)
    small = _SmallSum(gs_mid.shape[0])
    n_red = len(red.scratch)

    def body(q_ref, qt_ref, do_ref, lse_ref, dl_ref, k_ref, v_ref, *rest):
        g_in, gs_ref = rest[:n_w], rest[n_w]
        (dq_ref, dk_ref, dv_ref), g_out, gsum_ref = rest[n_w + 1:n_w + 4], rest[n_w + 4:2 * n_w + 4], rest[2 * n_w + 4]
        scratch = rest[2 * n_w + 5:]
        red.bind(g_in, g_out, scratch[:n_red])
        small.bind(gs_ref, gsum_ref, scratch[n_red:])
        i = pl.program_id(1)
        step_no = pl.program_id(0) * n_q + i

        @pl.when(step_no == 0)
        def _():
            red.start()
            small.start()

        pl.when(step_no == n_q)(red.exchange)

        @pl.when(i == 0)
        def _():
            dk_ref[...] = jnp.zeros_like(dk_ref)
            dv_ref[...] = jnp.zeros_like(dv_ref)

        mask = _chunk_mask(t, False)
        qcs = [slice(hh * HEAD_PAD, (hh + 1) * HEAD_PAD) for hh in range(ATT_HEADS)]
        vcs = [slice(hh * V_HEAD_DIM, (hh + 1) * V_HEAD_DIM) for hh in range(ATT_HEADS)]
        qhs = [q_ref[:, qc] for qc in qcs]
        qts = [qt_ref[qc, :] for qc in qcs]
        dohs = [do_ref[:, vc].astype(BF16) for vc in vcs]
        do_t = do_ref[...].T.astype(BF16)
        dots = [do_t[vc, :] for vc in vcs]
        lses = [lse_ref[:, hh * HEAD_PAD:hh * HEAD_PAD + 1] for hh in range(ATT_HEADS)]
        dls = [dl_ref[:, hh * HEAD_PAD:hh * HEAD_PAD + 1] for hh in range(ATT_HEADS)]

        def step(j, dqs, masked):
            keys = pl.ds(pl.multiple_of(j * t, t), t)
            out = []
            for hh in range(ATT_HEADS):
                kj = k_ref[keys, qcs[hh]]
                vj = v_ref[keys, vcs[hh]]
                p = jnp.exp2(_dot_nt(qhs[hh], kj) * ATT_SCALE_LOG2E - lses[hh])
                if masked:
                    p = jnp.where(mask, p, 0.0)
                ds = (p * (_dot_nt(dohs[hh], vj) - dls[hh])).astype(BF16)
                dv_ref[vcs[hh], keys] += _dot(dots[hh], p.astype(BF16))
                dk_ref[qcs[hh], keys] += _dot(qts[hh], ds) * ATT_SCALE
                out.append(dqs[hh] + _dot(ds, kj))
            return tuple(out)

        zero = jnp.zeros((t, HEAD_PAD), F32)
        dqs = lax.fori_loop(0, i, functools.partial(step, masked=False), (zero,) * ATT_HEADS)
        dqs = step(i, dqs, True)
        for hh in range(ATT_HEADS):
            dq_ref[:, qcs[hh]] = dqs[hh] * ATT_SCALE

        @pl.when(step_no == groups * n_q - 1)
        def _():
            red.finish()
            small.finish()

    hw = MLA_HEADS * HEAD_PAD
    any_spec = pl.BlockSpec(memory_space=pl.ANY)
    out = pl.pallas_call(
        body,
        name="attn_bwd",
        grid=(groups, n_q),
        in_specs=[
            pl.BlockSpec((t, ATT_HEADS * HEAD_PAD), lambda p, i: (i, p)),
            pl.BlockSpec((ATT_HEADS * HEAD_PAD, t), lambda p, i: (p, i)),
            pl.BlockSpec((t, ATT_HEADS * V_HEAD_DIM), lambda p, i: (i, p)),
            pl.BlockSpec((t, ATT_HEADS * HEAD_PAD), lambda p, i: (i, p)),
            pl.BlockSpec((t, ATT_HEADS * HEAD_PAD), lambda p, i: (i, p)),
            pl.BlockSpec((s, ATT_HEADS * HEAD_PAD), lambda p, i: (0, p), pipeline_mode=pl.Buffered(1)),
            pl.BlockSpec((s, ATT_HEADS * V_HEAD_DIM), lambda p, i: (0, p), pipeline_mode=pl.Buffered(1)),
        ] + [any_spec] * n_w + [pl.BlockSpec(small.spec_shape, lambda p, i: (0, 0))],
        out_specs=[
            pl.BlockSpec((t, ATT_HEADS * HEAD_PAD), lambda p, i: (i, p)),
            pl.BlockSpec((ATT_HEADS * HEAD_PAD, s), lambda p, i: (p, 0)),
            pl.BlockSpec((ATT_HEADS * V_HEAD_DIM, s), lambda p, i: (p, 0)),
        ] + [any_spec] * n_w + [pl.BlockSpec(small.spec_shape, lambda p, i: (0, 0))],
        out_shape=[jax.ShapeDtypeStruct((s, hw), F32), jax.ShapeDtypeStruct((hw, s), F32),
                   jax.ShapeDtypeStruct((MLA_WIDTH, s), F32)] + red.out_shape + [small.out_shape],
        scratch_shapes=red.scratch + small.scratch,
        compiler_params=pltpu.CompilerParams(dimension_semantics=("arbitrary", "arbitrary"),
                                             vmem_limit_bytes=VMEM_LIMIT),
    )(q, q_t, do, lse, delta, k, v, *late_grads, gs_mid)
    return out[0], out[1], out[2], out[3:3 + n_w], out[3 + n_w]


def _qkv_bwd(dq, dk_t, dv_t, z_sh, q_norm, kv_norm, wuq_p, wk_p, wv, rc, rsa, rsb, tm):
    s = z_sh.shape[1]
    hw = MLA_HEADS * HEAD_PAD

    def body(dq_ref, dk_ref, dv_ref, z_ref, gq_ref, gkv_ref, wuq_ref, wk_ref, wv_ref,
             c_ref, sa_ref, sb_ref,
             dzq_ref, dzkv_ref, dzkr_ref, dwuq_ref, dwk_ref, dwv_ref, dgq_ref, dgkv_ref):
        i = pl.program_id(0)

        @pl.when(i == 0)
        def _():
            dwuq_ref[...] = jnp.zeros_like(dwuq_ref)
            dwk_ref[...] = jnp.zeros_like(dwk_ref)
            dwv_ref[...] = jnp.zeros_like(dwv_ref)
            dgq_ref[...] = jnp.zeros_like(dgq_ref)
            dgkv_ref[...] = jnp.zeros_like(dgkv_ref)

        c, sa, sb = c_ref[...], sa_ref[...], sb_ref[...]
        gq, gkv = gq_ref[...], gkv_ref[...]

        z0 = z_ref[0]
        cq, xq, rq = _rms_fwd(z0[:, ZQ_COLS], gq)
        dqp = jnp.concatenate(
            [_unrope(dq_ref[:, h * HEAD_PAD:(h + 1) * HEAD_PAD], c, sa, sb) for h in range(MLA_HEADS)],
            axis=1).astype(BF16)
        dwuq_ref[...] += _dot_tn(cq.astype(BF16), dqp)
        dcq = _dot_nt(dqp, wuq_ref[...])
        dgq_ref[...] += _colsum(dcq * xq)
        dzq_ref[...] = _rms_bwd(dcq, xq, rq, gq).astype(BF16)

        ckv, xkv, rkv = _rms_fwd(z0[:, ZKV_COLS], gkv)
        ckv = ckv.astype(BF16)
        dkf = dk_ref[...].T
        dk_bf = dkf.astype(BF16)
        dv_bf = dv_ref[...].T.astype(BF16)
        dwk_ref[...] += _dot_tn(ckv, dk_bf)
        dwv_ref[...] += _dot_tn(ckv, dv_bf)
        dckv = _dot_nt(dk_bf, wk_ref[...]) + _dot_nt(dv_bf, wv_ref[...])
        dgkv_ref[...] += _colsum(dckv * xkv)
        dzkv_ref[...] = _rms_bwd(dckv, xkv, rkv, gkv).astype(BF16)

        dkr = dkf[:, 0:HEAD_PAD]
        for h in range(1, MLA_HEADS):
            dkr = dkr + dkf[:, h * HEAD_PAD:(h + 1) * HEAD_PAD]
        dkr = pltpu.roll(_unrope(dkr, c, sa, sb), 64, 1)
        lane = lax.broadcasted_iota(jnp.int32, (tm, HEAD_PAD), 1)
        dzkr_ref[...] = jnp.where(lane < QK_ROPE_DIM, dkr, 0.0).astype(BF16)

    return pl.pallas_call(
        body,
        name="qkv_bwd",
        grid=(s // tm,),
        in_specs=[
            _row_spec(tm, hw), pl.BlockSpec((hw, tm), lambda i: (0, i)), pl.BlockSpec((MLA_WIDTH, tm), lambda i: (0, i)),
            pl.BlockSpec((1, tm, SHARD_COLS), lambda i: (0, i, 0)),
            _full_spec((1, Q_LORA_RANK)), _full_spec((1, KV_LORA_RANK)),
            _full_spec((Q_LORA_RANK, hw)), _full_spec((KV_LORA_RANK, hw)), _full_spec((KV_LORA_RANK, MLA_WIDTH)),
            _row_spec(tm, HEAD_PAD), _row_spec(tm, HEAD_PAD), _row_spec(tm, HEAD_PAD),
        ],
        out_specs=[
            _row_spec(tm, Q_LORA_RANK), _row_spec(tm, KV_LORA_RANK), _row_spec(tm, HEAD_PAD),
            _full_spec((Q_LORA_RANK, hw)), _full_spec((KV_LORA_RANK, hw)), _full_spec((KV_LORA_RANK, MLA_WIDTH)),
            _full_spec((1, Q_LORA_RANK)), _full_spec((1, KV_LORA_RANK)),
        ],
        out_shape=[
            jax.ShapeDtypeStruct((s, Q_LORA_RANK), BF16), jax.ShapeDtypeStruct((s, KV_LORA_RANK), BF16),
            jax.ShapeDtypeStruct((s, HEAD_PAD), BF16),
            jax.ShapeDtypeStruct((Q_LORA_RANK, hw), F32), jax.ShapeDtypeStruct((KV_LORA_RANK, hw), F32),
            jax.ShapeDtypeStruct((KV_LORA_RANK, MLA_WIDTH), F32),
            jax.ShapeDtypeStruct((1, Q_LORA_RANK), F32), jax.ShapeDtypeStruct((1, KV_LORA_RANK), F32),
        ],
        compiler_params=pltpu.CompilerParams(dimension_semantics=("arbitrary",), vmem_limit_bytes=VMEM_LIMIT),
    )(dq, dk_t, dv_t, z_sh, q_norm, kv_norm, wuq_p, wk_p, wv, rc, rsa, rsb)


def _inproj_bwd_x(dzq, dzkv, dzkr, dgattn, ddc, dgpool, dgmerge, x, dh, norm_in, w_in_t, tm):
    s = x.shape[0]
    n_tiles = s // tm
    halo_per_tile = tm // POOL_HALO
    n_halo = s // POOL_HALO
    u_seg = 4

    def body(dzq_ref, dzkv_ref, dzkr_ref, dga_ref, ddc_ref, ddn_ref, dgp_ref, dgm_ref, x_ref, dh_ref,
             g_ref, w_hbm, gx_ref, dgin_ref, dzs_ref, w_vmem, dbuf, sem):
        i = pl.program_id(0)

        @pl.when(i == 0)
        def _():
            cp = pltpu.make_async_copy(w_hbm, w_vmem, sem)
            cp.start()
            dgin_ref[...] = jnp.zeros_like(dgin_ref)
            cp.wait()

        dbuf[0:tm, :] = ddc_ref[...]
        dbuf[tm:, :] = jnp.where(i < n_tiles - 1, ddn_ref[...], 0.0)
        row = lax.broadcasted_iota(jnp.int32, (tm, POOL_GROUP_DIM), 0) + i * tm
        du = []
        for g, w in enumerate(POOL_WINDOWS):
            cols = slice(g * POOL_GROUP_DIM, (g + 1) * POOL_GROUP_DIM)
            fsum = dbuf[0:tm, cols]
            for kk in range(1, w):
                fsum = fsum + dbuf[kk:kk + tm, cols]
            du.append(fsum - dbuf[0:tm, cols] * jnp.minimum(row + 1, w).astype(F32))
        du = jnp.concatenate(du, axis=1).astype(BF16)

        dz = [dzq_ref[...], dzkv_ref[...], dzkr_ref[...], dga_ref[...], du, dgp_ref[...], dgm_ref[...]]
        dhn = None
        for j, pieces in enumerate(SHARD_PIECES):
            parts = [dz[seg][:, lo:hi] for seg, lo, hi, _ in pieces]
            dzj = parts[0] if len(parts) == 1 else jnp.concatenate(parts, axis=1)
            dzs_ref[j] = dzj.T
            part = _dot(dzj, w_vmem[j])
            dhn = part if dhn is None else dhn + part

        g = g_ref[...]
        _, xhat, r = _rms_fwd(x_ref[...], g)
        dgin_ref[...] += _colsum(dhn * xhat)
        gx_ref[...] = dh_ref[...] + _rms_bwd(dhn, xhat, r, g)

    any_spec = pl.BlockSpec(memory_space=pl.ANY)
    seg_w = [wide for _, wide in IN_SEGMENTS]
    return pl.pallas_call(
        body,
        name="inproj_bwd_x",
        grid=(n_tiles,),
        in_specs=[
            _row_spec(tm, seg_w[0]), _row_spec(tm, seg_w[1]), _row_spec(tm, seg_w[2]),
            _row_spec(tm, seg_w[3]), _row_spec(tm, seg_w[u_seg]),
            pl.BlockSpec((POOL_HALO, POOL_WIDTH), lambda i: (jnp.minimum((i + 1) * halo_per_tile, n_halo - 1), 0)),
            _row_spec(tm, seg_w[5]), _row_spec(tm, seg_w[6]),
            _row_spec(tm, D_MODEL), _row_spec(tm, D_MODEL),
            _full_spec((1, D_MODEL)), any_spec,
        ],
        out_specs=[_row_spec(tm, D_MODEL), _full_spec((1, D_MODEL)),
                   pl.BlockSpec((N_CHIPS, SHARD_COLS, tm), lambda i: (0, 0, i))],
        out_shape=[jax.ShapeDtypeStruct((s, D_MODEL), F32), jax.ShapeDtypeStruct((1, D_MODEL), F32),
                   jax.ShapeDtypeStruct((N_CHIPS, SHARD_COLS, s), BF16)],
        scratch_shapes=[
            pltpu.VMEM((N_CHIPS, SHARD_COLS, D_MODEL), BF16),
            pltpu.VMEM((tm + POOL_HALO, POOL_WIDTH), F32),
            pltpu.SemaphoreType.DMA,
        ],
        compiler_params=pltpu.CompilerParams(dimension_semantics=("arbitrary",), vmem_limit_bytes=VMEM_LIMIT),
    )(dzq, dzkv, dzkr, dgattn, ddc, ddc, dgpool, dgmerge, x, dh, norm_in, w_in_t)


def _inproj_bwd_w(order, dz_sh, hn, g_uq, g_ukv, gs, tm):
    s = hn.shape[0]
    n_tiles = s // tm
    hc = D_MODEL // 2
    red = _Reduce(COMM_PARAMS[1:3])
    small = _SmallSum(gs.shape[0])
    n_red = len(red.scratch)

    def body(order_ref, dz_ref, hn_ref, guq_hbm, gukv_hbm, gs_ref, gw_hbm, guq_out, gukv_out, gsum_ref,
             acc, pm_w, a_w, b_w, r_w, w_send, w_recv, w_local, *more_scratch):
        ph, i = pl.program_id(0), pl.program_id(1)
        x, y, c = lax.axis_index("x"), lax.axis_index("y"), lax.axis_index("c")
        k = 2 * x + y
        me, sibling = (x, y, c), (x, y, 1 - c)
        chips = _other_chips(x, y)
        shard_of_phase = [2 * cx + cy for cx, cy in chips] + [k]
        copy = _remote_copier(w_send, w_recv)
        red.bind([guq_hbm, gukv_hbm], [guq_out, gukv_out], more_scratch[:n_red])
        small.bind(gs_ref, gsum_ref, more_scratch[n_red:])
        mine = pl.ds(pl.multiple_of(c * hc, hc), hc)
        theirs = pl.ds(pl.multiple_of((1 - c) * hc, hc), hc)

        def to_sibling(f):
            j = shard_of_phase[f]
            return copy(f, pm_w.at[j, 1 - c], a_w.at[j], sibling)

        def pair_sum(f):
            cx, cy = chips[f]
            return copy(4 + f, pm_w.at[shard_of_phase[f], c], b_w.at[f], (cx, cy, c))

        def finished():
            return copy(7, r_w, gw_hbm.at[:, mine], sibling)

        @pl.when(jnp.logical_and(ph == 0, i == 0))
        def _():
            red.start()
            small.start()

        part = _dot(dz_ref[0], hn_ref[...])

        @pl.when(i == 0)
        def _():
            acc[...] = part

        @pl.when(i > 0)
        def _():
            acc[...] += part

        for f in range(3):
            @pl.when(jnp.logical_and(ph == f + 1, i == 0))
            def _(f=f):
                j = shard_of_phase[f]
                copy(f, a_w.at[j], a_w.at[j], me).wait_recv()
                pm_w[j, c] = (pm_w[j, c].astype(F32) + a_w[j].astype(F32)).astype(BF16)
                pair_sum(f).start()
                if f == 0:
                    red.exchange()

        for f in range(4):
            @pl.when(jnp.logical_and(ph == f, i == n_tiles - 1))
            def _(f=f):
                j = shard_of_phase[f]
                pm_w[j, 0] = acc[:, :hc].astype(BF16)
                pm_w[j, 1] = acc[:, hc:].astype(BF16)
                to_sibling(f).start()
                if f < 3:
                    return
                copy(3, a_w.at[k], a_w.at[k], me).wait_recv()
                r_w[...] = pm_w[k, c].astype(F32) + a_w[k].astype(F32)
                for g in range(3):
                    copy(4 + g, b_w.at[g], b_w.at[g], me).wait_recv()
                    r_w[...] = r_w[...] + b_w[g].astype(F32)
                store = pltpu.make_async_copy(r_w, gw_hbm.at[:, mine], w_local)
                store.start()
                finished().start()
                red.finish()
                small.finish()
                copy(7, gw_hbm.at[:, theirs], gw_hbm.at[:, theirs], me).wait_recv()
                store.wait()
                for g in range(4):
                    to_sibling(g).wait_send()
                for g in range(3):
                    pair_sum(g).wait_send()
                finished().wait_send()

    any_spec = pl.BlockSpec(memory_space=pl.ANY)
    n_sem = 8
    grid_spec = pltpu.PrefetchScalarGridSpec(
        num_scalar_prefetch=1,
        grid=(N_CHIPS, n_tiles),
        in_specs=[
            pl.BlockSpec((1, SHARD_COLS, tm), lambda ph, i, order: (order[ph], 0, i)),
            pl.BlockSpec((tm, D_MODEL), lambda ph, i, order: (i, 0)),
            any_spec, any_spec,
            pl.BlockSpec(small.spec_shape, lambda ph, i, order: (0, 0)),
        ],
        out_specs=[any_spec, any_spec, any_spec, pl.BlockSpec(small.spec_shape, lambda ph, i, order: (0, 0))],
        scratch_shapes=[
            pltpu.VMEM((SHARD_COLS, D_MODEL), F32),
            pltpu.VMEM((N_CHIPS, 2, SHARD_COLS, hc), BF16),
            pltpu.VMEM((N_CHIPS, SHARD_COLS, hc), BF16),
            pltpu.VMEM((3, SHARD_COLS, hc), BF16),
            pltpu.VMEM((SHARD_COLS, hc), F32),
            pltpu.SemaphoreType.DMA((n_sem,)), pltpu.SemaphoreType.DMA((n_sem,)), pltpu.SemaphoreType.DMA,
        ] + red.scratch + small.scratch,
    )
    out = pl.pallas_call(
        body,
        name="inproj_bwd_w",
        grid_spec=grid_spec,
        out_shape=[jax.ShapeDtypeStruct((SHARD_COLS, D_MODEL), F32)] + red.out_shape
        + [small.out_shape],
        compiler_params=pltpu.CompilerParams(dimension_semantics=("arbitrary", "arbitrary"),
                                             vmem_limit_bytes=VMEM_LIMIT),
    )(order, dz_sh, hn, g_uq, g_ukv, gs)
    return out[0], out[1], out[2], out[3]


def _other_chips(x, y):
    return ((1 - x, 1 - y), (1 - x, y), (x, 1 - y))


def _half(ref, axis, size, c, lead=()):
    window = pl.ds(pl.multiple_of(c * size, size), size)
    if axis == 0:
        return ref.at[(*lead, window, slice(None))]
    return ref.at[(*lead, slice(None), window)]


def _half_shape(rows, cols, axis, size):
    return (size, cols) if axis == 0 else (rows, size)


def _remote_copier(send_sems, recv_sems):
    def copy(sem, src, dst, to):
        return pltpu.make_async_remote_copy(src_ref=src, dst_ref=dst, send_sem=send_sems.at[sem],
                                            recv_sem=recv_sems.at[sem], device_id=to, device_id_type=MESH)
    return copy


class _Gather:
    def __init__(self, params):
        self.params = params
        n = len(params)
        self.scratch = [pltpu.SemaphoreType.DMA((6 * n,)), pltpu.SemaphoreType.DMA((6 * n,)),
                        pltpu.SemaphoreType.DMA((n,))]
        self.out_shape = [jax.ShapeDtypeStruct((N_CHIPS, r, cc), BF16) for _, r, cc, _, _ in params]

    def bind(self, ins, outs, scratch):
        self.ins, self.outs = ins, outs
        send_sems, recv_sems, self.local_sems = scratch
        self.copy = _remote_copier(send_sems, recv_sems)
        self.x, self.y, self.c = lax.axis_index("x"), lax.axis_index("y"), lax.axis_index("c")
        self.k = 2 * self.x + self.y
        self.chips = _other_chips(self.x, self.y)

    def _local(self, p):
        return pltpu.make_async_copy(self.ins[p], self.outs[p].at[self.k], self.local_sems.at[p])

    def _first(self, p, j):
        _, _, _, axis, size = self.params[p]
        cx, cy = self.chips[j]
        return self.copy(6 * p + j, _half(self.ins[p], axis, size, self.c),
                         _half(self.outs[p], axis, size, self.c, (self.k,)), (cx, cy, self.c))

    def _relay(self, p, j, half_of):
        _, _, _, axis, size = self.params[p]
        cx, cy = self.chips[j]
        block = _half(self.outs[p], axis, size, half_of, (2 * cx + cy,))
        return self.copy(6 * p + 3 + j, block, block, (self.x, self.y, 1 - self.c))

    def start(self):
        for p in range(len(self.params)):
            self._local(p).start()
            for j in (1, 2, 0):
                self._first(p, j).start()

    def relay_one(self, p, j):
        _, _, _, axis, size = self.params[p]
        cx, cy = self.chips[j]
        landed = _half(self.outs[p], axis, size, self.c, (2 * cx + cy,))
        self.copy(6 * p + j, landed, landed, (self.x, self.y, self.c)).wait_recv()
        self._relay(p, j, self.c).start()

    def await_one(self, p, j):
        self._relay(p, j, 1 - self.c).wait_recv()

    def wait_sends(self):
        for p in range(len(self.params)):
            for j in range(3):
                self._first(p, j).wait_send()
                self._relay(p, j, self.c).wait_send()
            self._local(p).wait()

    def relay(self):
        for j in range(3):
            for p in range(len(self.params)):
                self.relay_one(p, j)

    def finish(self):
        for j in range(3):
            for p in range(len(self.params)):
                self.await_one(p, j)
        self.wait_sends()


class _Reduce:
    def __init__(self, params):
        self.params = params
        n = len(params)
        halves = [_half_shape(r, cc, axis, size) for _, r, cc, axis, size in params]
        self.scratch = ([pltpu.VMEM((N_CHIPS, *h), BF16) for h in halves]
                        + [pltpu.VMEM((N_CHIPS, *h), BF16) for h in halves]
                        + [pltpu.VMEM((3, *h), BF16) for h in halves]
                        + [pltpu.VMEM(h, F32) for h in halves]
                        + [pltpu.SemaphoreType.DMA((5 * n,)), pltpu.SemaphoreType.DMA((5 * n,)),
                           pltpu.SemaphoreType.DMA((2 * n,))])
        self.out_shape = [jax.ShapeDtypeStruct((r, cc), F32) for _, r, cc, _, _ in params]

    def bind(self, g_in, g_out, scratch):
        n = len(self.params)
        self.g_in, self.g_out = g_in, g_out
        self.pm, self.a_buf = scratch[0:n], scratch[n:2 * n]
        self.b_buf, self.r_buf = scratch[2 * n:3 * n], scratch[3 * n:4 * n]
        send_sems, recv_sems, self.local_sems = scratch[4 * n:]
        self.copy = _remote_copier(send_sems, recv_sems)
        self.x, self.y, self.c = lax.axis_index("x"), lax.axis_index("y"), lax.axis_index("c")
        self.k = 2 * self.x + self.y
        self.chips = _other_chips(self.x, self.y)
        self.me = (self.x, self.y, self.c)
        self.sibling = (self.x, self.y, 1 - self.c)

    def _load(self, p):
        _, _, _, axis, size = self.params[p]
        return pltpu.make_async_copy(_half(self.g_in[p], axis, size, self.c, (slice(None),)), self.pm[p],
                                     self.local_sems.at[p])

    def _to_sibling(self, p):
        _, _, _, axis, size = self.params[p]
        return self.copy(5 * p, _half(self.g_in[p], axis, size, 1 - self.c, (slice(None),)), self.a_buf[p],
                         self.sibling)

    def _pair_sum(self, p, j):
        cx, cy = self.chips[j]
        return self.copy(5 * p + 1 + j, self.pm[p].at[2 * cx + cy], self.b_buf[p].at[j], (cx, cy, self.c))

    def _store(self, p):
        _, _, _, axis, size = self.params[p]
        n = len(self.params)
        return pltpu.make_async_copy(self.r_buf[p], _half(self.g_out[p], axis, size, self.c),
                                     self.local_sems.at[n + p])

    def _finished(self, p):
        _, _, _, axis, size = self.params[p]
        return self.copy(5 * p + 4, self.r_buf[p], _half(self.g_out[p], axis, size, self.c), self.sibling)

    def start(self):
        for p in range(len(self.params)):
            self._load(p).start()
            self._to_sibling(p).start()

    def exchange(self):
        for p in range(len(self.params)):
            self._load(p).wait()
            self.copy(5 * p, self.a_buf[p], self.a_buf[p], self.me).wait_recv()
            for j, (cx, cy) in enumerate(self.chips):
                kj = 2 * cx + cy
                self.pm[p][kj] = (self.pm[p][kj].astype(F32) + self.a_buf[p][kj].astype(F32)).astype(BF16)
                self._pair_sum(p, j).start()
            self.r_buf[p][...] = self.pm[p][self.k].astype(F32) + self.a_buf[p][self.k].astype(F32)

    def finish(self):
        for p, (_, _, _, axis, size) in enumerate(self.params):
            for j in range(3):
                self.copy(5 * p + 1 + j, self.b_buf[p].at[j], self.b_buf[p].at[j], self.me).wait_recv()
                self.r_buf[p][...] = self.r_buf[p][...] + self.b_buf[p][j].astype(F32)
            self._store(p).start()
            self._finished(p).start()
        for p, (_, _, _, axis, size) in enumerate(self.params):
            theirs = _half(self.g_out[p], axis, size, 1 - self.c)
            self.copy(5 * p + 4, theirs, theirs, self.me).wait_recv()
            self._store(p).wait()
            self._to_sibling(p).wait_send()
            for j in range(3):
                self._pair_sum(p, j).wait_send()
            self._finished(p).wait_send()


class _SmallSum:
    def __init__(self, rows):
        self.rows = rows
        self.scratch = [pltpu.VMEM((N_DEV, rows, LANES), F32),
                        pltpu.SemaphoreType.DMA((N_DEV - 1,)), pltpu.SemaphoreType.DMA((N_DEV - 1,))]
        self.out_shape = jax.ShapeDtypeStruct((rows, LANES), F32)
        self.spec_shape = (rows, LANES)

    def bind(self, src, dst, scratch):
        self.src, self.dst = src, dst
        self.buf, send_sems, recv_sems = scratch
        self.copy = _remote_copier(send_sems, recv_sems)
        self.x, self.y, self.c = lax.axis_index("x"), lax.axis_index("y"), lax.axis_index("c")

    def _send(self, f):
        fx, fy, fc = [(a, b, d) for a in (0, 1) for b in (0, 1) for d in (0, 1)][f]
        x, y, c = self.x, self.y, self.c
        peer = (1 - x if fx else x, 1 - y if fy else y, 1 - c if fc else c)
        return self.copy(f - 1, self.src, self.buf.at[f], peer)

    def start(self):
        for f in range(1, N_DEV):
            self._send(f).start()
        self.buf[0] = self.src[...]

    def finish(self):
        me = (self.x, self.y, self.c)
        for f in range(1, N_DEV):
            self.copy(f - 1, self.buf.at[f], self.buf.at[f], me).wait_recv()
        dev = 4 * self.x + 2 * self.y + self.c
        total = self.buf[dev]
        for d in range(1, N_DEV):
            total = total + self.buf[jnp.bitwise_xor(dev, d)]
        self.dst[...] = total
        for f in range(1, N_DEV):
            self._send(f).wait_send()


def _adamw_math(w, g, m, v):
    m = ADAM_B1 * m + (1.0 - ADAM_B1) * g
    v = ADAM_B2 * v + (1.0 - ADAM_B2) * (g * g)
    m_hat = m / (1.0 - ADAM_B1 ** ADAM_STEP)
    v_hat = v / (1.0 - ADAM_B2 ** ADAM_STEP)
    delta = -ADAM_LR * (m_hat / (jnp.sqrt(v_hat) + ADAM_EPS) + ADAM_WD * w)
    return delta, m, v


def _adamw_tiled(w, g, m, v, tm):
    rows, cols = w.shape

    def body(w_ref, g_ref, m_ref, v_ref, d_ref, nm_ref, nv_ref):
        d_ref[...], nm_ref[...], nv_ref[...] = _adamw_math(w_ref[...], g_ref[...], m_ref[...], v_ref[...])

    spec = _row_spec(tm, cols)
    return pl.pallas_call(
        body,
        name="adamw_w_in",
        grid=(rows // tm,),
        in_specs=[spec] * 4,
        out_specs=[spec] * 3,
        out_shape=[jax.ShapeDtypeStruct(w.shape, F32)] * 3,
        compiler_params=pltpu.CompilerParams(dimension_semantics=("parallel",), vmem_limit_bytes=VMEM_LIMIT),
    )(w, g, m, v)


def _adamw_many(ws, gs, ms, vs):
    n = len(ws)

    def body(*refs):
        ins, outs = refs[:4 * n], refs[4 * n:]
        for i in range(n):
            d, nm, nv = _adamw_math(ins[i][...], ins[n + i][...], ins[2 * n + i][...], ins[3 * n + i][...])
            outs[i][...] = d
            outs[n + i][...] = nm
            outs[2 * n + i][...] = nv

    vmem_spec = pl.BlockSpec(memory_space=pltpu.VMEM)
    shapes = [jax.ShapeDtypeStruct(w.shape, F32) for w in ws]
    out = pl.pallas_call(
        body,
        name="adamw_small",
        in_specs=[vmem_spec] * (4 * n),
        out_specs=[vmem_spec] * (3 * n),
        out_shape=shapes * 3,
        compiler_params=pltpu.CompilerParams(vmem_limit_bytes=VMEM_LIMIT),
    )(*ws, *gs, *ms, *vs)
    return out[:n], out[n:2 * n], out[2 * n:]


def _pack_rows(parts, rows, dtype):
    flat = jnp.concatenate([p.reshape(-1).astype(dtype) for p in parts])
    flat = jnp.concatenate([flat, jnp.zeros((rows * LANES - flat.shape[0],), dtype)])
    return flat.reshape(rows, LANES)


def _unpack_rows(packed, shapes):
    flat = packed.reshape(-1)
    out, off = [], 0
    for _, shp in shapes:
        n = int(np.prod(shp))
        out.append(flat[off:off + n].reshape(shp))
        off += n
    return out


def _rope_tables(s):
    half = QK_ROPE_DIM // 2
    inv_freq = np.float32(ROPE_THETA) ** (-np.arange(half, dtype=np.float32) / np.float32(half))
    ang = (np.arange(s, dtype=np.float32)[:, None] * inv_freq[None, :]).astype(np.float32)
    cos, sin = np.cos(ang.astype(np.float64)).astype(np.float32), np.sin(ang.astype(np.float64)).astype(np.float32)
    z16 = np.zeros((s, half), np.float32)
    z32 = np.zeros((s, HEAD_PAD - QK_NOPE_DIM - QK_ROPE_DIM), np.float32)
    z64 = np.zeros((s, QK_NOPE_DIM), np.float32)
    rc = np.concatenate([np.ones((s, QK_NOPE_DIM), np.float32), cos, cos, z32], axis=1)
    rsa = np.concatenate([z64, -sin, z16, z32], axis=1)
    rsb = np.concatenate([z64, z16, sin, z32], axis=1)
    return jnp.asarray(rc), jnp.asarray(rsa), jnp.asarray(rsb)


def kernel(x, norm_in, w_in, q_norm, w_uq, kv_norm, w_ukv, pool_w, pool_scale, w_branch_attn, w_branch_pool, w_out, norm_final, loss_target, m_norm_in, m_w_in, m_q_norm, m_w_uq, m_kv_norm, m_w_ukv, m_pool_w, m_pool_scale, m_w_branch_attn, m_w_branch_pool, m_w_out, m_norm_final, v_norm_in, v_w_in, v_q_norm, v_w_uq, v_kv_norm, v_w_ukv, v_pool_w, v_pool_scale, v_w_branch_attn, v_w_branch_pool, v_w_out, v_norm_final):
    s = x.shape[1]
    t_att, t_row = _tiles(s)
    x2 = x.reshape(s, D_MODEL)
    tgt = loss_target.reshape(s, D_MODEL)

    local = [w_in.T, w_uq.reshape(96, 768), w_ukv.reshape(64, 1024), w_branch_attn, w_branch_pool, w_out]
    local = [a.astype(BF16) for a in local]
    cx, cy = lax.axis_index("x"), lax.axis_index("y")
    others = [2 * ox + oy for ox, oy in _other_chips(cx, cy)]
    hn, z_sh, (w_in_t, w_uq_all, w_ukv_all) = _inproj_fwd(
        jnp.stack([2 * cx + cy, others[1], others[2], others[0]]).astype(jnp.int32), x2, norm_in.reshape(1, -1),
        local[:3], 4 * t_row)
    w_uq_f = w_uq_all.reshape(Q_LORA_RANK, MLA_HEADS, QK_NOPE_DIM + QK_ROPE_DIM)
    w_ukv_f = w_ukv_all.reshape(KV_LORA_RANK, MLA_HEADS, QK_NOPE_DIM + V_HEAD_DIM)
    hw = MLA_HEADS * HEAD_PAD
    wuq_p = jnp.pad(w_uq_f, ((0, 0), (0, 0), (0, HEAD_PAD - QK_NOPE_DIM - QK_ROPE_DIM))).reshape(Q_LORA_RANK, hw)
    wk_p = jnp.pad(w_ukv_f[:, :, :QK_NOPE_DIM], ((0, 0), (0, 0), (0, HEAD_PAD - QK_NOPE_DIM))).reshape(KV_LORA_RANK, hw)
    wv = w_ukv_f[:, :, QK_NOPE_DIM:].reshape(KV_LORA_RANK, MLA_WIDTH)
    rc, rsa, rsb = _rope_tables(s)
    g_in = norm_in.reshape(1, -1)
    g_q = q_norm.reshape(1, -1)
    g_kv = kv_norm.reshape(1, -1)
    g_f = norm_final.reshape(1, -1)
    ps = pool_scale.reshape(1, -1)
    pw_bf = pool_w.astype(BF16)

    q, k, v, q_t, v_t = _qkv_fwd(z_sh, g_q, g_kv, wuq_p, wk_p, wv, rc, rsa, rsb, t_row)
    o, lse, (w_ba_all, w_bp_all, w_out_all) = _attn_fwd(q_t, k, v_t, local[3:], t_att)
    w_out_f = w_out_all.reshape(D_MODEL, D_MODEL)

    (do, delta, dgattn, dgpool, dgmerge, ddc, dh, sq_err, d_w_out, d_w_ba, d_w_bp, d_pool_w, d_pool_scale,
     d_norm_final) = _mid(o, z_sh, x2, tgt, pw_bf, ps, w_ba_all, w_bp_all, w_out_f, g_f, t_row)

    late_grads = [d_w_ba, d_w_bp, d_w_out.reshape(N_CHIPS, 256, D_MODEL)]
    small_mid = dict(pool_scale=d_pool_scale, norm_final=d_norm_final, pool_w=d_pool_w, sq_err=sq_err)
    gs_mid = _pack_rows([small_mid[n] for n, _ in SMALL_MID], _small_rows(SMALL_MID), F32)
    dq, dk_t, dv_t, (g_w_ba, g_w_bp, g_w_out), g_small_mid = _attn_bwd(q, q_t, k, v, do, lse, delta, late_grads,
                                                                      gs_mid, t_att)
    g_pool_scale, g_norm_final, g_pool_w, sq_err_all = _unpack_rows(g_small_mid, SMALL_MID)
    dzq, dzkv, dzkr, d_wuq_p, d_wk_p, d_wv, d_q_norm, d_kv_norm = _qkv_bwd(
        dq, dk_t, dv_t, z_sh, g_q, g_kv, wuq_p, wk_p, wv, rc, rsa, rsb, t_row)
    grad_x, d_norm_in, dz_sh = _inproj_bwd_x(dzq, dzkv, dzkr, dgattn, ddc, dgpool, dgmerge, x2, dh, g_in, w_in_t,
                                             t_row)

    d_w_uq = d_wuq_p.reshape(Q_LORA_RANK, MLA_HEADS, HEAD_PAD)[:, :, :QK_NOPE_DIM + QK_ROPE_DIM]
    d_w_ukv = jnp.concatenate([d_wk_p.reshape(KV_LORA_RANK, MLA_HEADS, HEAD_PAD)[:, :, :QK_NOPE_DIM],
                               d_wv.reshape(KV_LORA_RANK, MLA_HEADS, V_HEAD_DIM)], axis=2)
    small_late = dict(norm_in=d_norm_in, q_norm=d_q_norm, kv_norm=d_kv_norm)
    gs = _pack_rows([small_late[n] for n, _ in SMALL_LATE], _small_rows(SMALL_LATE), F32)
    order = jnp.stack(others + [2 * cx + cy]).astype(jnp.int32)
    g_w_in_t, g_w_uq, g_w_ukv, g_small = _inproj_bwd_w(
        order, dz_sh, hn, d_w_uq.reshape(N_CHIPS, 96, 768).astype(BF16),
        d_w_ukv.reshape(N_CHIPS, 64, 1024).astype(BF16), gs, 4 * t_row)
    g_norm_in, g_q_norm, g_kv_norm = _unpack_rows(g_small, SMALL_LATE)
    g_w_uq = g_w_uq.reshape(w_uq.shape)
    g_w_ukv = g_w_ukv.reshape(w_ukv.shape)

    dl_w_in, nm_w_in, nv_w_in = (a.T for a in _adamw_tiled(w_in.T, g_w_in_t, m_w_in.T, v_w_in.T, 152))

    def two_d(a):
        return a.reshape(1, -1) if a.ndim == 1 else a

    names = ["norm_in", "q_norm", "w_uq", "kv_norm", "w_ukv", "pool_w", "pool_scale", "w_branch_attn",
             "w_branch_pool", "w_out", "norm_final"]
    ws = dict(norm_in=norm_in, q_norm=q_norm, w_uq=w_uq, kv_norm=kv_norm, w_ukv=w_ukv, pool_w=pool_w,
              pool_scale=pool_scale, w_branch_attn=w_branch_attn, w_branch_pool=w_branch_pool, w_out=w_out,
              norm_final=norm_final)
    gsd = dict(norm_in=g_norm_in, q_norm=g_q_norm, w_uq=g_w_uq, kv_norm=g_kv_norm, w_ukv=g_w_ukv, pool_w=g_pool_w,
               pool_scale=g_pool_scale, w_branch_attn=g_w_ba, w_branch_pool=g_w_bp, w_out=g_w_out,
               norm_final=g_norm_final)
    msd = dict(norm_in=m_norm_in, q_norm=m_q_norm, w_uq=m_w_uq, kv_norm=m_kv_norm, w_ukv=m_w_ukv, pool_w=m_pool_w,
               pool_scale=m_pool_scale, w_branch_attn=m_w_branch_attn, w_branch_pool=m_w_branch_pool, w_out=m_w_out,
               norm_final=m_norm_final)
    vsd = dict(norm_in=v_norm_in, q_norm=v_q_norm, w_uq=v_w_uq, kv_norm=v_kv_norm, w_ukv=v_w_ukv, pool_w=v_pool_w,
               pool_scale=v_pool_scale, w_branch_attn=v_w_branch_attn, w_branch_pool=v_w_branch_pool, w_out=v_w_out,
               norm_final=v_norm_final)
    dls, nms, nvs = _adamw_many([two_d(ws[n]) for n in names], [two_d(gsd[n]) for n in names],
                                [two_d(msd[n]) for n in names], [two_d(vsd[n]) for n in names])

    grads = dict(gsd)
    grads["w_in"] = g_w_in_t.T
    delta_w = {n: d.reshape(ws[n].shape) for n, d in zip(names, dls)}
    new_m = {n: d.reshape(ws[n].shape) for n, d in zip(names, nms)}
    new_v = {n: d.reshape(ws[n].shape) for n, d in zip(names, nvs)}
    delta_w["w_in"], new_m["w_in"], new_v["w_in"] = dl_w_in, nm_w_in, nv_w_in
    ws["w_in"] = w_in

    order = ["norm_in", "w_in", "q_norm", "w_uq", "kv_norm", "w_ukv", "pool_w", "pool_scale", "w_branch_attn",
             "w_branch_pool", "w_out", "norm_final"]
    loss = 0.5 * jnp.sum(sq_err_all) / D_MODEL
    return (loss, grad_x.reshape(x.shape),
            *[grads[n].reshape(ws[n].shape) for n in order],
            *[delta_w[n] for n in order], *[new_m[n] for n in order], *[new_v[n] for n in order])
```

```python
import functools

import jax
import jax.numpy as jnp
import numpy as np
from jax import lax
from jax.experimental import pallas as pl
from jax.experimental.pallas import tpu as pltpu

F32 = jnp.float32
BF16 = jnp.bfloat16
MESH = pl.DeviceIdType.MESH

D_MODEL = 1024
CHUNK = 64
MLA_HEADS = 8
QK_NOPE_DIM = 64
QK_ROPE_DIM = 32
V_HEAD_DIM = 64
Q_LORA_RANK = 384
KV_LORA_RANK = 256
MLA_WIDTH = MLA_HEADS * V_HEAD_DIM
ROPE_THETA = 10000.0
POOL_WINDOWS = (2, 4, 8, 16)
POOL_WIDTH = 512
POOL_GROUP_DIM = 128
BRANCH_COLS = D_MODEL // 4
ATT_HEADS = 4
MID_CHAINS = 2
POOL_HALO = 16
EPS = 1e-6
IN_TOTAL = 4256
HEAD_PAD = 128
ATT_SCALE = (QK_NOPE_DIM + QK_ROPE_DIM) ** -0.5
ATT_SCALE_LOG2E = ATT_SCALE * 1.4426950408889634

ADAM_LR = 0.001
ADAM_B1 = 0.9
ADAM_B2 = 0.999
ADAM_EPS = 1e-08
ADAM_WD = 0.01
ADAM_STEP = 10

N_CHIPS = 4
N_DEV = 8
LANES = 128
VMEM_LIMIT = 60 * 1024 * 1024

IN_SEGMENTS = ((384, 384), (256, 256), (32, HEAD_PAD), (512, 512), (512, 512), (512, 512), (2048, 2048))
SHARD_COLS = IN_TOTAL // N_CHIPS
ZQ_COLS = slice(0, 384)
ZKV_COLS = slice(384, 640)
ZKR_TILE = slice(640, 768)


def _shard_pieces():
    bounds, off = [], 0
    for w, _ in IN_SEGMENTS:
        bounds.append((off, off + w))
        off += w
    out = []
    for j in range(N_CHIPS):
        lo, hi = SHARD_COLS * j, SHARD_COLS * (j + 1)
        out.append([(i, max(lo, a) - a, min(hi, b) - a, max(lo, a) - lo)
                    for i, (a, b) in enumerate(bounds) if max(lo, a) < min(hi, b)])
    return out


SHARD_PIECES = _shard_pieces()


def _segment(z_blocks, seg):
    parts = [z_blocks[j][:, col:col + hi - lo]
             for j, pieces in enumerate(SHARD_PIECES) for sg, lo, hi, col in pieces if sg == seg]
    return parts[0] if len(parts) == 1 else jnp.concatenate(parts, axis=1)

COMM_PARAMS = (
    ("w_in", SHARD_COLS, D_MODEL, 1, 512),
    ("w_uq", 96, 768, 0, 48),
    ("w_ukv", 64, 1024, 0, 32),
    ("w_branch_attn", 512, 256, 0, 256),
    ("w_branch_pool", 512, 256, 0, 256),
    ("w_out", 256, 1024, 0, 128),
)

SMALL_MID = (
    ("pool_scale", (512,)),
    ("norm_final", (1024,)),
    ("pool_w", (4, 128, 128)),
    ("sq_err", (8, 128)),
)
SMALL_LATE = (
    ("norm_in", (1024,)),
    ("q_norm", (384,)),
    ("kv_norm", (256,)),
)


def _small_rows(shapes):
    return -(-sum(int(np.prod(s)) for _, s in shapes) // (LANES * 8)) * 8


def _dot(a, b):
    return jnp.dot(a, b, preferred_element_type=F32)


def _dot_nt(a, b):
    return lax.dot_general(a, b, (((1,), (1,)), ((), ())), preferred_element_type=F32)


def _dot_tn(a, b):
    return lax.dot_general(a, b, (((0,), (0,)), ((), ())), preferred_element_type=F32)


def _sigmoid(x):
    return 1.0 / (1.0 + jnp.exp(-x))


def _colsum(x):
    return jnp.sum(x, axis=0, keepdims=True)


def _rms_fwd(x, g):
    r = lax.rsqrt(jnp.mean(x * x, axis=-1, keepdims=True) + EPS)
    xhat = x * r
    return xhat * g, xhat, r


def _rms_bwd(dy, xhat, r, g):
    dxhat = dy * g
    return r * (dxhat - xhat * jnp.mean(dxhat * xhat, axis=-1, keepdims=True))


def _rope(v, c, sa, sb):
    return v * c + pltpu.roll(v, 112, 1) * sa + pltpu.roll(v, 16, 1) * sb


def _unrope(d, c, sa, sb):
    return d * c + pltpu.roll(d * sa, 16, 1) + pltpu.roll(d * sb, 112, 1)


def _row_spec(tm, n):
    return pl.BlockSpec((tm, n), lambda i: (i, 0))


def _full_spec(shape):
    nd = len(shape)
    return pl.BlockSpec(shape, lambda i: (0,) * nd)


def _tiles(s):
    t_att = 512 if s >= 2048 else 128
    t_row = 256 if s >= 1024 else 128
    return t_att, t_row


def _inproj_fwd(order, x, norm_in, early_shards, tm):
    s = x.shape[0]
    n_tiles = s // tm
    gat = _Gather(COMM_PARAMS[:3])
    n_w = len(gat.params)
    arrival = (1, 2, 0)

    def body(order_ref, x_ref, g_ref, *rest):
        w_loc, (hn_ref, z_ref), w_all = rest[:n_w], rest[n_w:n_w + 2], rest[n_w + 2:2 * n_w + 2]
        w_vmem, hn_all, w_sem = rest[2 * n_w + 2:2 * n_w + 5]
        gat.bind(w_loc, w_all, rest[2 * n_w + 5:])
        ph, i = pl.program_id(0), pl.program_id(1)
        pl.when(jnp.logical_and(ph == 0, i == 0))(gat.start)

        @pl.when(jnp.logical_and(ph == 0, i == 0))
        def _():
            cp = pltpu.make_async_copy(w_loc[0], w_vmem, w_sem)
            cp.start()
            cp.wait()

        for f in range(3):
            @pl.when(jnp.logical_and(ph == f + 1, i == 0))
            def _(f=f):
                gat.relay_one(0, arrival[f])
                gat.await_one(0, arrival[f])
                cp = pltpu.make_async_copy(w_all[0].at[order_ref[ph]], w_vmem, w_sem)
                cp.start()
                cp.wait()

        rows = pl.ds(pl.multiple_of(i * tm, tm), tm)

        @pl.when(ph == 0)
        def _():
            hn, _, _ = _rms_fwd(x_ref[...], g_ref[...])
            hn = hn.astype(BF16)
            hn_ref[...] = hn
            hn_all[rows, :] = hn

        z_ref[0] = _dot_nt(hn_all[rows, :], w_vmem[...])

        @pl.when(jnp.logical_and(ph == N_CHIPS - 1, i == n_tiles - 1))
        def _():
            for p in range(1, n_w):
                for j in range(3):
                    gat.relay_one(p, j)
            for p in range(1, n_w):
                for j in range(3):
                    gat.await_one(p, j)
            gat.wait_sends()

    def tile_in_phase0(ph, i, order):
        return (jnp.where(ph == 0, i, n_tiles - 1), 0)

    any_spec = pl.BlockSpec(memory_space=pl.ANY)
    grid_spec = pltpu.PrefetchScalarGridSpec(
        num_scalar_prefetch=1,
        grid=(N_CHIPS, n_tiles),
        in_specs=[pl.BlockSpec((tm, D_MODEL), tile_in_phase0),
                  pl.BlockSpec((1, D_MODEL), lambda ph, i, order: (0, 0))] + [any_spec] * n_w,
        out_specs=[pl.BlockSpec((tm, D_MODEL), tile_in_phase0),
                   pl.BlockSpec((1, tm, SHARD_COLS), lambda ph, i, order: (order[ph], i, 0))] + [any_spec] * n_w,
        scratch_shapes=[pltpu.VMEM((SHARD_COLS, D_MODEL), BF16), pltpu.VMEM((s, D_MODEL), BF16),
                        pltpu.SemaphoreType.DMA] + gat.scratch,
    )
    out = pl.pallas_call(
        body,
        name="inproj_fwd",
        grid_spec=grid_spec,
        out_shape=[jax.ShapeDtypeStruct((s, D_MODEL), BF16), jax.ShapeDtypeStruct((N_CHIPS, s, SHARD_COLS), F32)]
        + gat.out_shape,
        compiler_params=pltpu.CompilerParams(dimension_semantics=("arbitrary", "arbitrary"),
                                             vmem_limit_bytes=VMEM_LIMIT),
    )(order, x, norm_in, *early_shards)
    return out[0], out[1], out[2:]


def _qkv_fwd(z_sh, q_norm, kv_norm, wuq_p, wk_p, wv, rc, rsa, rsb, tm):
    s = z_sh.shape[1]
    hw = MLA_HEADS * HEAD_PAD

    def body(z_ref, gq_ref, gkv_ref, wuq_ref, wk_ref, wv_ref, c_ref, sa_ref, sb_ref,
             q_ref, k_ref, v_ref, qt_ref, vt_ref):
        c, sa, sb = c_ref[...], sa_ref[...], sb_ref[...]
        z0 = z_ref[0]
        cq, _, _ = _rms_fwd(z0[:, ZQ_COLS], gq_ref[...])
        qf = _dot(cq.astype(BF16), wuq_ref[...])
        ckv, _, _ = _rms_fwd(z0[:, ZKV_COLS], gkv_ref[...])
        ckv = ckv.astype(BF16)
        kn = _dot(ckv, wk_ref[...])
        lane = lax.broadcasted_iota(jnp.int32, (tm, HEAD_PAD), 1)
        zkr = jnp.where(lane < QK_ROPE_DIM, z0[:, ZKR_TILE], 0.0)
        kr = _rope(pltpu.roll(zkr, 64, 1), c, sa, sb)
        for h in range(MLA_HEADS):
            cols = slice(h * HEAD_PAD, (h + 1) * HEAD_PAD)
            qh = _rope(qf[:, cols], c, sa, sb)
            q_ref[:, cols] = qh.astype(BF16)
            qt_ref[cols, :] = qh.T.astype(BF16)
            k_ref[:, cols] = (kn[:, cols] + kr).astype(BF16)
        vf = _dot(ckv, wv_ref[...])
        v_ref[...] = vf.astype(BF16)
        vt_ref[...] = vf.T.astype(BF16)

    return pl.pallas_call(
        body,
        name="qkv_fwd",
        grid=(s // tm,),
        in_specs=[
            pl.BlockSpec((1, tm, SHARD_COLS), lambda i: (0, i, 0)),
            _full_spec((1, Q_LORA_RANK)), _full_spec((1, KV_LORA_RANK)),
            _full_spec((Q_LORA_RANK, hw)), _full_spec((KV_LORA_RANK, hw)), _full_spec((KV_LORA_RANK, MLA_WIDTH)),
            _row_spec(tm, HEAD_PAD), _row_spec(tm, HEAD_PAD), _row_spec(tm, HEAD_PAD),
        ],
        out_specs=[_row_spec(tm, hw), _row_spec(tm, hw), _row_spec(tm, MLA_WIDTH),
                   pl.BlockSpec((hw, tm), lambda i: (0, i)), pl.BlockSpec((MLA_WIDTH, tm), lambda i: (0, i))],
        out_shape=[jax.ShapeDtypeStruct((s, hw), BF16), jax.ShapeDtypeStruct((s, hw), BF16),
                   jax.ShapeDtypeStruct((s, MLA_WIDTH), BF16),
                   jax.ShapeDtypeStruct((hw, s), BF16), jax.ShapeDtypeStruct((MLA_WIDTH, s), BF16)],
        compiler_params=pltpu.CompilerParams(dimension_semantics=("parallel",), vmem_limit_bytes=VMEM_LIMIT),
    )(z_sh, q_norm, kv_norm, wuq_p, wk_p, wv, rc, rsa, rsb)


def _chunk_mask(t, keys_on_rows):
    rows = lax.broadcasted_iota(jnp.int32, (t, t), 0) // CHUNK
    cols = lax.broadcasted_iota(jnp.int32, (t, t), 1) // CHUNK
    return rows <= cols if keys_on_rows else cols <= rows


def _attn_fwd(q_t, k, v_t, late_shards, t):
    s = k.shape[0]
    groups = MLA_HEADS // ATT_HEADS
    n_q = s // t
    gat = _Gather(COMM_PARAMS[3:])
    n_w = len(gat.params)

    def body(qt_ref, k_ref, k2_ref, vt_ref, *rest):
        w_in, (o_ref, lse_ref), w_out = rest[:n_w], rest[n_w:n_w + 2], rest[n_w + 2:2 * n_w + 2]
        gat.bind(w_in, w_out, rest[2 * n_w + 2:])
        i = pl.program_id(1)
        step_no = pl.program_id(0) * n_q + i
        pl.when(step_no == 0)(gat.start)
        pl.when(step_no == n_q)(gat.relay)
        mask = _chunk_mask(t, True)
        qcs = [slice(hh * HEAD_PAD, (hh + 1) * HEAD_PAD) for hh in range(ATT_HEADS)]
        vcs = [slice(hh * V_HEAD_DIM, (hh + 1) * V_HEAD_DIM) for hh in range(ATT_HEADS)]
        qts = [qt_ref[qc, :] for qc in qcs]

        def step(j, carry, masked):
            keys = pl.ds(pl.multiple_of(j * t, t), t)
            out = []
            for hh in range(ATT_HEADS):
                m, l, acc = carry[hh]
                sc = _dot(k_ref[keys, qcs[hh]], qts[hh])
                if masked:
                    sc = jnp.where(mask, sc, -jnp.inf)
                m_new = jnp.maximum(m, jnp.max(sc, axis=0, keepdims=True))
                alpha = jnp.exp2((m - m_new) * ATT_SCALE_LOG2E)
                p = jnp.exp2((_dot(k2_ref[keys, qcs[hh]], qts[hh]) - m_new) * ATT_SCALE_LOG2E)
                if masked:
                    p = jnp.where(mask, p, 0.0)
                l = alpha * l + jnp.sum(p, axis=0, keepdims=True)
                acc = alpha * acc + _dot(vt_ref[vcs[hh], keys], p.astype(BF16))
                out.append((m_new, l, acc))
            return tuple(out)

        one = (jnp.full((1, t), -jnp.inf, F32), jnp.zeros((1, t), F32), jnp.zeros((V_HEAD_DIM, t), F32))
        carry = lax.fori_loop(0, i, functools.partial(step, masked=False), (one,) * ATT_HEADS)
        carry = step(i, carry, True)
        o_ref[...] = jnp.concatenate([carry[hh][2] / carry[hh][1] for hh in range(ATT_HEADS)], axis=0).T
        for hh in range(ATT_HEADS):
            m, l, _ = carry[hh]
            lse_ref[:, qcs[hh]] = jnp.broadcast_to(m * ATT_SCALE_LOG2E + jnp.log2(l), (HEAD_PAD, t)).T
        pl.when(step_no == groups * n_q - 1)(gat.finish)

    any_spec = pl.BlockSpec(memory_space=pl.ANY)
    out = pl.pallas_call(
        body,
        name="attn_fwd",
        grid=(groups, n_q),
        in_specs=[
            pl.BlockSpec((ATT_HEADS * HEAD_PAD, t), lambda p, i: (p, i)),
            pl.BlockSpec((s, ATT_HEADS * HEAD_PAD), lambda p, i: (0, p)),
            pl.BlockSpec((s, ATT_HEADS * HEAD_PAD), lambda p, i: (0, p)),
            pl.BlockSpec((ATT_HEADS * V_HEAD_DIM, s), lambda p, i: (p, 0)),
        ] + [any_spec] * n_w,
        out_specs=[
            pl.BlockSpec((t, ATT_HEADS * V_HEAD_DIM), lambda p, i: (i, p)),
            pl.BlockSpec((t, ATT_HEADS * HEAD_PAD), lambda p, i: (i, p)),
        ] + [any_spec] * n_w,
        out_shape=[jax.ShapeDtypeStruct((s, MLA_WIDTH), F32), jax.ShapeDtypeStruct((s, MLA_HEADS * HEAD_PAD), F32)]
        + gat.out_shape,
        scratch_shapes=gat.scratch,
        compiler_params=pltpu.CompilerParams(dimension_semantics=("arbitrary", "arbitrary"),
                                             vmem_limit_bytes=VMEM_LIMIT),
    )(q_t, k, k, v_t, *late_shards)
    return out[0], out[1], out[2:]


def _mid(o, z_sh, x, target, pool_w, pool_scale, w_ba, w_bp, w_out, transposed, norm_final, tm):
    s = x.shape[0]
    n_tiles = s // tm
    halo_per_tile = tm // POOL_HALO

    def body(o_ref, z0_ref, z1_ref, z1h_ref, z2_ref, z3_ref, x_ref, t_ref, pw_ref, ps_ref, wba_ref, wbp_ref,
             wout_ref, pwt_ref, wbat_ref, wbpt_ref, woutt_ref, gf_ref,
             do_ref, dl_ref, dga_ref, dgp_ref, dgm_ref, ddc_ref, dh_ref,
             loss_ref, dwout_out, dwba_out, dwbp_out, dpw_ref, dps_ref, dgf_ref,
             ubuf, dwout_ref, dwba_ref, dwbp_ref):
        i = pl.program_id(0)

        @pl.when(i == 0)
        def _():
            loss_ref[...] = jnp.zeros_like(loss_ref)
            dwout_ref[...] = jnp.zeros_like(dwout_ref)
            dwba_ref[...] = jnp.zeros_like(dwba_ref)
            dwbp_ref[...] = jnp.zeros_like(dwbp_ref)
            dpw_ref[...] = jnp.zeros_like(dpw_ref)
            dps_ref[...] = jnp.zeros_like(dps_ref)
            dgf_ref[...] = jnp.zeros_like(dgf_ref)

        ubuf[0:POOL_HALO, :] = jnp.where(i > 0, _segment([None, z1h_ref[0]], 4), 0.0)
        ubuf[POOL_HALO:, :] = _segment([None, z1_ref[0]], 4)
        ps = ps_ref[...]
        gf = gf_ref[...]
        th = tm // MID_CHAINS

        kept = []
        for hf in range(MID_CHAINS):
            rs = slice(hf * th, (hf + 1) * th)
            zs = [z0_ref[0, rs, :], z1_ref[0, rs, :], z2_ref[0, rs, :], z3_ref[0, rs, :]]
            o = o_ref[rs, :]
            ga = _segment(zs, 3)
            sga = _sigmoid(ga)
            silu_a = ga * sga
            y_attn = (o * silu_a).astype(BF16)

            base = POOL_HALO + hf * th
            row = lax.broadcasted_iota(jnp.int32, (th, POOL_GROUP_DIM), 0) + i * tm + hf * th
            gp = _segment(zs, 5)
            sgp = _sigmoid(gp)
            silu_p = gp * sgp
            d_bf, dm, inv_cnt = [], [], []
            for g, w in enumerate(POOL_WINDOWS):
                cols = slice(g * POOL_GROUP_DIM, (g + 1) * POOL_GROUP_DIM)
                wsum = ubuf[base:base + th, cols]
                for kk in range(1, w):
                    wsum = wsum + ubuf[base - kk:base - kk + th, cols]
                inv = 1.0 / jnp.minimum(row + 1, w).astype(F32)
                dg = (wsum * inv - ubuf[base:base + th, cols]).astype(BF16)
                d_bf.append(dg)
                inv_cnt.append(inv)
                dm.append(_dot(dg, pw_ref[g]))
            dm = jnp.concatenate(dm, axis=1)
            yp = dm * ps
            y_pool = (yp * silu_p).astype(BF16)

            a = jnp.concatenate([_dot(y_attn, wba_ref[j]) for j in range(N_CHIPS)], axis=1)
            p = jnp.concatenate([_dot(y_pool, wbp_ref[j]) for j in range(N_CHIPS)], axis=1)
            gm = _segment(zs, 6)
            gate_a = _sigmoid(gm[:, :D_MODEL])
            gate_p = _sigmoid(gm[:, D_MODEL:])
            merged = (gate_a * a + gate_p * p).astype(BF16)
            h = x_ref[rs, :] + _dot(merged, wout_ref[...])
            y, xhat, r = _rms_fwd(h, gf)
            err = y - t_ref[rs, :]
            e2 = err * err
            e2 = jnp.sum(e2.reshape(th // 8, 8, D_MODEL), axis=0)
            acc = e2[:, 0:LANES]
            for cidx in range(1, D_MODEL // LANES):
                acc = acc + e2[:, cidx * LANES:(cidx + 1) * LANES]
            loss_ref[...] += acc

            dy = err * (1.0 / D_MODEL)
            dgf_ref[...] += _colsum(dy * xhat)
            dh = _rms_bwd(dy, xhat, r, gf)
            dh_ref[rs, :] = dh
            dh_bf = dh.astype(BF16)
            dmerged = _dot(dh_bf, woutt_ref[...])
            da = (dmerged * gate_a).astype(BF16)
            dp = (dmerged * gate_p).astype(BF16)
            dgm_ref[rs, :D_MODEL] = (dmerged * a * gate_a * (1.0 - gate_a)).astype(BF16)
            dgm_ref[rs, D_MODEL:] = (dmerged * p * gate_p * (1.0 - gate_p)).astype(BF16)
            dy_attn = dy_pool = None
            for j in range(N_CHIPS):
                cols = slice(j * BRANCH_COLS, (j + 1) * BRANCH_COLS)
                pa = _dot(da[:, cols], wbat_ref[j])
                pp = _dot(dp[:, cols], wbpt_ref[j])
                dy_attn = pa if dy_attn is None else dy_attn + pa
                dy_pool = pp if dy_pool is None else dy_pool + pp

            do = dy_attn * silu_a
            do_ref[rs, :] = do
            dga_ref[rs, :] = (dy_attn * o * (sga * (1.0 + ga * (1.0 - sga)))).astype(BF16)
            doo = do * o
            for hd in range(MLA_HEADS):
                dl = jnp.sum(doo[:, hd * V_HEAD_DIM:(hd + 1) * V_HEAD_DIM], axis=1, keepdims=True)
                dl_ref[rs, hd * HEAD_PAD:(hd + 1) * HEAD_PAD] = jnp.broadcast_to(dl, (th, HEAD_PAD))

            dyp = dy_pool * silu_p
            dgp_ref[rs, :] = (dy_pool * yp * (sgp * (1.0 + gp * (1.0 - sgp)))).astype(BF16)
            dps_ref[...] += _colsum(dyp * dm)
            dmm = (dyp * ps).astype(BF16)
            for g in range(len(POOL_WINDOWS)):
                cols = slice(g * POOL_GROUP_DIM, (g + 1) * POOL_GROUP_DIM)
                ddc_ref[rs, cols] = _dot(dmm[:, cols], pwt_ref[g]) * inv_cnt[g]
            kept.append((merged, dh_bf, y_attn, da, y_pool, dp, d_bf, dmm))

        def whole(idx):
            return jnp.concatenate([kept_h[idx] for kept_h in kept], axis=0)

        merged, dh_bf, y_attn, da, y_pool, dp, dmm = (whole(n) for n in (0, 1, 2, 3, 4, 5, 7))
        dwout_ref[...] += _dot_tn(merged, dh_bf)
        for j in range(N_CHIPS):
            cols = slice(j * BRANCH_COLS, (j + 1) * BRANCH_COLS)
            dwba_ref[j] += _dot_tn(y_attn, da[:, cols])
            dwbp_ref[j] += _dot_tn(y_pool, dp[:, cols])
        for g in range(len(POOL_WINDOWS)):
            cols = slice(g * POOL_GROUP_DIM, (g + 1) * POOL_GROUP_DIM)
            d_g = jnp.concatenate([kept_h[6][g] for kept_h in kept], axis=0)
            dpw_ref[g] += _dot_tn(d_g, dmm[:, cols])

        @pl.when(i == n_tiles - 1)
        def _():
            dwout_out[...] = dwout_ref[...].astype(BF16)
            dwba_out[...] = dwba_ref[...].astype(BF16)
            dwbp_out[...] = dwbp_ref[...].astype(BF16)

    row_in = lambda n: _row_spec(tm, n)
    in_specs = [
        row_in(MLA_WIDTH),
        pl.BlockSpec((1, tm, SHARD_COLS), lambda i: (0, i, 0)), pl.BlockSpec((1, tm, SHARD_COLS), lambda i: (1, i, 0)),
        pl.BlockSpec((1, POOL_HALO, SHARD_COLS), lambda i: (1, jnp.maximum(i * halo_per_tile - 1, 0), 0)),
        pl.BlockSpec((1, tm, SHARD_COLS), lambda i: (2, i, 0)), pl.BlockSpec((1, tm, SHARD_COLS), lambda i: (3, i, 0)),
        row_in(D_MODEL), row_in(D_MODEL),
        _full_spec((4, POOL_GROUP_DIM, POOL_GROUP_DIM)), _full_spec((1, POOL_WIDTH)),
        _full_spec((N_CHIPS, MLA_WIDTH, BRANCH_COLS)), _full_spec((N_CHIPS, POOL_WIDTH, BRANCH_COLS)),
        _full_spec((D_MODEL, D_MODEL)),
        _full_spec((4, POOL_GROUP_DIM, POOL_GROUP_DIM)), _full_spec((N_CHIPS, BRANCH_COLS, MLA_WIDTH)),
        _full_spec((N_CHIPS, BRANCH_COLS, POOL_WIDTH)), _full_spec((D_MODEL, D_MODEL)), _full_spec((1, D_MODEL)),
    ]
    out_shape = [
        jax.ShapeDtypeStruct((s, MLA_WIDTH), F32),
        jax.ShapeDtypeStruct((s, MLA_HEADS * HEAD_PAD), F32),
        jax.ShapeDtypeStruct((s, MLA_WIDTH), BF16),
        jax.ShapeDtypeStruct((s, POOL_WIDTH), BF16),
        jax.ShapeDtypeStruct((s, 2 * D_MODEL), BF16),
        jax.ShapeDtypeStruct((s, POOL_WIDTH), F32),
        jax.ShapeDtypeStruct((s, D_MODEL), F32),
        jax.ShapeDtypeStruct((8, LANES), F32),
        jax.ShapeDtypeStruct((D_MODEL, D_MODEL), BF16),
        jax.ShapeDtypeStruct((N_CHIPS, MLA_WIDTH, BRANCH_COLS), BF16),
        jax.ShapeDtypeStruct((N_CHIPS, POOL_WIDTH, BRANCH_COLS), BF16),
        jax.ShapeDtypeStruct((4, POOL_GROUP_DIM, POOL_GROUP_DIM), F32),
        jax.ShapeDtypeStruct((1, POOL_WIDTH), F32),
        jax.ShapeDtypeStruct((1, D_MODEL), F32),
    ]
    out_specs = [
        row_in(MLA_WIDTH), row_in(MLA_HEADS * HEAD_PAD), row_in(MLA_WIDTH), row_in(POOL_WIDTH),
        row_in(2 * D_MODEL), row_in(POOL_WIDTH), row_in(D_MODEL),
        _full_spec((8, LANES)), _full_spec((D_MODEL, D_MODEL)), _full_spec((N_CHIPS, MLA_WIDTH, BRANCH_COLS)),
        _full_spec((N_CHIPS, POOL_WIDTH, BRANCH_COLS)), _full_spec((4, POOL_GROUP_DIM, POOL_GROUP_DIM)),
        _full_spec((1, POOL_WIDTH)), _full_spec((1, D_MODEL)),
    ]
    return pl.pallas_call(
        body,
        name="mid",
        grid=(n_tiles,),
        in_specs=in_specs,
        out_specs=out_specs,
        out_shape=out_shape,
        scratch_shapes=[
            pltpu.VMEM((tm + POOL_HALO, POOL_WIDTH), F32),
            pltpu.VMEM((D_MODEL, D_MODEL), F32),
            pltpu.VMEM((N_CHIPS, MLA_WIDTH, BRANCH_COLS), F32),
            pltpu.VMEM((N_CHIPS, POOL_WIDTH, BRANCH_COLS), F32),
        ],
        compiler_params=pltpu.CompilerParams(dimension_semantics=("arbitrary",), vmem_limit_bytes=VMEM_LIMIT),
    )(o, z_sh, z_sh, z_sh, z_sh, z_sh, x, target, pool_w, pool_scale, w_ba, w_bp, w_out, *transposed, norm_final)


def _attn_bwd(q, q_t, k, v, do, lse, delta, late_grads, gs_mid, t):
    s = q.shape[0]
    groups = MLA_HEADS // ATT_HEADS
    n_q = s // t
    red = _Reduce(COMM_PARAMS[3:])
    n_w = len(red.params)
    small = _SmallSum(gs_mid.shape[0])
    n_red = len(red.scratch)

    def body(q_ref, qt_ref, do_ref, lse_ref, dl_ref, k_ref, v_ref, *rest):
        g_in, gs_ref = rest[:n_w], rest[n_w]
        (dq_ref, dk_ref, dv_ref), g_out, gsum_ref = rest[n_w + 1:n_w + 4], rest[n_w + 4:2 * n_w + 4], rest[2 * n_w + 4]
        scratch = rest[2 * n_w + 5:]
        red.bind(g_in, g_out, scratch[:n_red])
        small.bind(gs_ref, gsum_ref, scratch[n_red:])
        i = pl.program_id(1)
        step_no = pl.program_id(0) * n_q + i

        @pl.when(step_no == 0)
        def _():
            red.start()
            small.start()

        pl.when(step_no == n_q)(red.exchange)

        @pl.when(i == 0)
        def _():
            dk_ref[...] = jnp.zeros_like(dk_ref)
            dv_ref[...] = jnp.zeros_like(dv_ref)

        mask = _chunk_mask(t, False)
        qcs = [slice(hh * HEAD_PAD, (hh + 1) * HEAD_PAD) for hh in range(ATT_HEADS)]
        vcs = [slice(hh * V_HEAD_DIM, (hh + 1) * V_HEAD_DIM) for hh in range(ATT_HEADS)]
        qhs = [q_ref[:, qc] for qc in qcs]
        qts = [qt_ref[qc, :] for qc in qcs]
        dohs = [do_ref[:, vc].astype(BF16) for vc in vcs]
        do_t = do_ref[...].T.astype(BF16)
        dots = [do_t[vc, :] for vc in vcs]
        lses = [lse_ref[:, hh * HEAD_PAD:hh * HEAD_PAD + 1] for hh in range(ATT_HEADS)]
        dls = [dl_ref[:, hh * HEAD_PAD:hh * HEAD_PAD + 1] for hh in range(ATT_HEADS)]

        def step(j, dqs, masked):
            keys = pl.ds(pl.multiple_of(j * t, t), t)
            out = []
            for hh in range(ATT_HEADS):
                kj = k_ref[keys, qcs[hh]]
                vj = v_ref[keys, vcs[hh]]
                p = jnp.exp2(_dot_nt(qhs[hh], kj) * ATT_SCALE_LOG2E - lses[hh])
                if masked:
                    p = jnp.where(mask, p, 0.0)
                ds = (p * (_dot_nt(dohs[hh], vj) - dls[hh])).astype(BF16)
                dv_ref[vcs[hh], keys] += _dot(dots[hh], p.astype(BF16))
                dk_ref[qcs[hh], keys] += _dot(qts[hh], ds) * ATT_SCALE
                out.append(dqs[hh] + _dot(ds, kj))
            return tuple(out)

        zero = jnp.zeros((t, HEAD_PAD), F32)
        dqs = lax.fori_loop(0, i, functools.partial(step, masked=False), (zero,) * ATT_HEADS)
        dqs = step(i, dqs, True)
        for hh in range(ATT_HEADS):
            dq_ref[:, qcs[hh]] = dqs[hh] * ATT_SCALE

        @pl.when(step_no == groups * n_q - 1)
        def _():
            red.finish()
            small.finish()

    hw = MLA_HEADS * HEAD_PAD
    any_spec = pl.BlockSpec(memory_space=pl.ANY)
    out = pl.pallas_call(
        body,
        name="attn_bwd",
        grid=(groups, n_q),
        in_specs=[
            pl.BlockSpec((t, ATT_HEADS * HEAD_PAD), lambda p, i: (i, p)),
            pl.BlockSpec((ATT_HEADS * HEAD_PAD, t), lambda p, i: (p, i)),
            pl.BlockSpec((t, ATT_HEADS * V_HEAD_DIM), lambda p, i: (i, p)),
            pl.BlockSpec((t, ATT_HEADS * HEAD_PAD), lambda p, i: (i, p)),
            pl.BlockSpec((t, ATT_HEADS * HEAD_PAD), lambda p, i: (i, p)),
            pl.BlockSpec((s, ATT_HEADS * HEAD_PAD), lambda p, i: (0, p), pipeline_mode=pl.Buffered(1)),
            pl.BlockSpec((s, ATT_HEADS * V_HEAD_DIM), lambda p, i: (0, p), pipeline_mode=pl.Buffered(1)),
        ] + [any_spec] * n_w + [pl.BlockSpec(small.spec_shape, lambda p, i: (0, 0))],
        out_specs=[
            pl.BlockSpec((t, ATT_HEADS * HEAD_PAD), lambda p, i: (i, p)),
            pl.BlockSpec((ATT_HEADS * HEAD_PAD, s), lambda p, i: (p, 0)),
            pl.BlockSpec((ATT_HEADS * V_HEAD_DIM, s), lambda p, i: (p, 0)),
        ] + [any_spec] * n_w + [pl.BlockSpec(small.spec_shape, lambda p, i: (0, 0))],
        out_shape=[jax.ShapeDtypeStruct((s, hw), F32), jax.ShapeDtypeStruct((hw, s), F32),
                   jax.ShapeDtypeStruct((MLA_WIDTH, s), F32)] + red.out_shape + [small.out_shape],
        scratch_shapes=red.scratch + small.scratch,
        compiler_params=pltpu.CompilerParams(dimension_semantics=("arbitrary", "arbitrary"),
                                             vmem_limit_bytes=VMEM_LIMIT),
    )(q, q_t, do, lse, delta, k, v, *late_grads, gs_mid)
    return out[0], out[1], out[2], out[3:3 + n_w], out[3 + n_w]


def _qkv_bwd(dq, dk_t, dv_t, z_sh, q_norm, kv_norm, wuq_p, wk_p, wv, rc, rsa, rsb, tm):
    s = z_sh.shape[1]
    hw = MLA_HEADS * HEAD_PAD

    def body(dq_ref, dk_ref, dv_ref, z_ref, gq_ref, gkv_ref, wuq_ref, wk_ref, wv_ref,
             c_ref, sa_ref, sb_ref,
             dzq_ref, dzkv_ref, dzkr_ref, dwuq_ref, dwk_ref, dwv_ref, dgq_ref, dgkv_ref):
        i = pl.program_id(0)

        @pl.when(i == 0)
        def _():
            dwuq_ref[...] = jnp.zeros_like(dwuq_ref)
            dwk_ref[...] = jnp.zeros_like(dwk_ref)
            dwv_ref[...] = jnp.zeros_like(dwv_ref)
            dgq_ref[...] = jnp.zeros_like(dgq_ref)
            dgkv_ref[...] = jnp.zeros_like(dgkv_ref)

        c, sa, sb = c_ref[...], sa_ref[...], sb_ref[...]
        gq, gkv = gq_ref[...], gkv_ref[...]

        z0 = z_ref[0]
        cq, xq, rq = _rms_fwd(z0[:, ZQ_COLS], gq)
        dqp = jnp.concatenate(
            [_unrope(dq_ref[:, h * HEAD_PAD:(h + 1) * HEAD_PAD], c, sa, sb) for h in range(MLA_HEADS)],
            axis=1).astype(BF16)
        dwuq_ref[...] += _dot_tn(cq.astype(BF16), dqp)
        dcq = _dot_nt(dqp, wuq_ref[...])
        dgq_ref[...] += _colsum(dcq * xq)
        dzq_ref[...] = _rms_bwd(dcq, xq, rq, gq).astype(BF16)

        ckv, xkv, rkv = _rms_fwd(z0[:, ZKV_COLS], gkv)
        ckv = ckv.astype(BF16)
        dkf = dk_ref[...].T
        dk_bf = dkf.astype(BF16)
        dv_bf = dv_ref[...].T.astype(BF16)
        dwk_ref[...] += _dot_tn(ckv, dk_bf)
        dwv_ref[...] += _dot_tn(ckv, dv_bf)
        dckv = _dot_nt(dk_bf, wk_ref[...]) + _dot_nt(dv_bf, wv_ref[...])
        dgkv_ref[...] += _colsum(dckv * xkv)
        dzkv_ref[...] = _rms_bwd(dckv, xkv, rkv, gkv).astype(BF16)

        dkr = dkf[:, 0:HEAD_PAD]
        for h in range(1, MLA_HEADS):
            dkr = dkr + dkf[:, h * HEAD_PAD:(h + 1) * HEAD_PAD]
        dkr = pltpu.roll(_unrope(dkr, c, sa, sb), 64, 1)
        lane = lax.broadcasted_iota(jnp.int32, (tm, HEAD_PAD), 1)
        dzkr_ref[...] = jnp.where(lane < QK_ROPE_DIM, dkr, 0.0).astype(BF16)

    return pl.pallas_call(
        body,
        name="qkv_bwd",
        grid=(s // tm,),
        in_specs=[
            _row_spec(tm, hw), pl.BlockSpec((hw, tm), lambda i: (0, i)), pl.BlockSpec((MLA_WIDTH, tm), lambda i: (0, i)),
            pl.BlockSpec((1, tm, SHARD_COLS), lambda i: (0, i, 0)),
            _full_spec((1, Q_LORA_RANK)), _full_spec((1, KV_LORA_RANK)),
            _full_spec((Q_LORA_RANK, hw)), _full_spec((KV_LORA_RANK, hw)), _full_spec((KV_LORA_RANK, MLA_WIDTH)),
            _row_spec(tm, HEAD_PAD), _row_spec(tm, HEAD_PAD), _row_spec(tm, HEAD_PAD),
        ],
        out_specs=[
            _row_spec(tm, Q_LORA_RANK), _row_spec(tm, KV_LORA_RANK), _row_spec(tm, HEAD_PAD),
            _full_spec((Q_LORA_RANK, hw)), _full_spec((KV_LORA_RANK, hw)), _full_spec((KV_LORA_RANK, MLA_WIDTH)),
            _full_spec((1, Q_LORA_RANK)), _full_spec((1, KV_LORA_RANK)),
        ],
        out_shape=[
            jax.ShapeDtypeStruct((s, Q_LORA_RANK), BF16), jax.ShapeDtypeStruct((s, KV_LORA_RANK), BF16),
            jax.ShapeDtypeStruct((s, HEAD_PAD), BF16),
            jax.ShapeDtypeStruct((Q_LORA_RANK, hw), F32), jax.ShapeDtypeStruct((KV_LORA_RANK, hw), F32),
            jax.ShapeDtypeStruct((KV_LORA_RANK, MLA_WIDTH), F32),
            jax.ShapeDtypeStruct((1, Q_LORA_RANK), F32), jax.ShapeDtypeStruct((1, KV_LORA_RANK), F32),
        ],
        compiler_params=pltpu.CompilerParams(dimension_semantics=("arbitrary",), vmem_limit_bytes=VMEM_LIMIT),
    )(dq, dk_t, dv_t, z_sh, q_norm, kv_norm, wuq_p, wk_p, wv, rc, rsa, rsb)


def _inproj_bwd_x(dzq, dzkv, dzkr, dgattn, ddc, dgpool, dgmerge, x, dh, norm_in, w_in_t, tm):
    s = x.shape[0]
    n_tiles = s // tm
    halo_per_tile = tm // POOL_HALO
    n_halo = s // POOL_HALO
    u_seg = 4

    def body(dzq_ref, dzkv_ref, dzkr_ref, dga_ref, ddc_ref, ddn_ref, dgp_ref, dgm_ref, x_ref, dh_ref,
             g_ref, w_hbm, gx_ref, dgin_ref, dzs_ref, w_vmem, dbuf, sem):
        i = pl.program_id(0)

        @pl.when(i == 0)
        def _():
            cp = pltpu.make_async_copy(w_hbm, w_vmem, sem)
            cp.start()
            dgin_ref[...] = jnp.zeros_like(dgin_ref)
            cp.wait()

        dbuf[0:tm, :] = ddc_ref[...]
        dbuf[tm:, :] = jnp.where(i < n_tiles - 1, ddn_ref[...], 0.0)
        row = lax.broadcasted_iota(jnp.int32, (tm, POOL_GROUP_DIM), 0) + i * tm
        du = []
        for g, w in enumerate(POOL_WINDOWS):
            cols = slice(g * POOL_GROUP_DIM, (g + 1) * POOL_GROUP_DIM)
            fsum = dbuf[0:tm, cols]
            for kk in range(1, w):
                fsum = fsum + dbuf[kk:kk + tm, cols]
            du.append(fsum - dbuf[0:tm, cols] * jnp.minimum(row + 1, w).astype(F32))
        du = jnp.concatenate(du, axis=1).astype(BF16)

        dz = [dzq_ref[...], dzkv_ref[...], dzkr_ref[...], dga_ref[...], du, dgp_ref[...], dgm_ref[...]]
        dz = jnp.concatenate([d[:, :w] for d, (w, _) in zip(dz, IN_SEGMENTS)], axis=1)
        for j in range(N_CHIPS):
            dzs_ref[j] = dz[:, j * SHARD_COLS:(j + 1) * SHARD_COLS].T
        dhn = _dot(dz, w_vmem[...])

        g = g_ref[...]
        _, xhat, r = _rms_fwd(x_ref[...], g)
        dgin_ref[...] += _colsum(dhn * xhat)
        gx_ref[...] = dh_ref[...] + _rms_bwd(dhn, xhat, r, g)

    any_spec = pl.BlockSpec(memory_space=pl.ANY)
    seg_w = [wide for _, wide in IN_SEGMENTS]
    return pl.pallas_call(
        body,
        name="inproj_bwd_x",
        grid=(n_tiles,),
        in_specs=[
            _row_spec(tm, seg_w[0]), _row_spec(tm, seg_w[1]), _row_spec(tm, seg_w[2]),
            _row_spec(tm, seg_w[3]), _row_spec(tm, seg_w[u_seg]),
            pl.BlockSpec((POOL_HALO, POOL_WIDTH), lambda i: (jnp.minimum((i + 1) * halo_per_tile, n_halo - 1), 0)),
            _row_spec(tm, seg_w[5]), _row_spec(tm, seg_w[6]),
            _row_spec(tm, D_MODEL), _row_spec(tm, D_MODEL),
            _full_spec((1, D_MODEL)), any_spec,
        ],
        out_specs=[_row_spec(tm, D_MODEL), _full_spec((1, D_MODEL)),
                   pl.BlockSpec((N_CHIPS, SHARD_COLS, tm), lambda i: (0, 0, i))],
        out_shape=[jax.ShapeDtypeStruct((s, D_MODEL), F32), jax.ShapeDtypeStruct((1, D_MODEL), F32),
                   jax.ShapeDtypeStruct((N_CHIPS, SHARD_COLS, s), BF16)],
        scratch_shapes=[
            pltpu.VMEM((IN_TOTAL, D_MODEL), BF16),
            pltpu.VMEM((tm + POOL_HALO, POOL_WIDTH), F32),
            pltpu.SemaphoreType.DMA,
        ],
        compiler_params=pltpu.CompilerParams(dimension_semantics=("arbitrary",), vmem_limit_bytes=VMEM_LIMIT),
    )(dzq, dzkv, dzkr, dgattn, ddc, ddc, dgpool, dgmerge, x, dh, norm_in, w_in_t.reshape(IN_TOTAL, D_MODEL))


def _inproj_bwd_w(order, dz_sh, hn, g_uq, g_ukv, gs, tm):
    s = hn.shape[0]
    n_tiles = s // tm
    hc = D_MODEL // 2
    red = _Reduce(COMM_PARAMS[1:3])
    small = _SmallSum(gs.shape[0])
    n_red = len(red.scratch)

    def body(order_ref, dz_ref, hn_ref, guq_hbm, gukv_hbm, gs_ref, gw_hbm, guq_out, gukv_out, gsum_ref,
             acc, pm_w, a_w, b_w, r_w, w_send, w_recv, w_local, *more_scratch):
        ph, i = pl.program_id(0), pl.program_id(1)
        x, y, c = lax.axis_index("x"), lax.axis_index("y"), lax.axis_index("c")
        k = 2 * x + y
        me, sibling = (x, y, c), (x, y, 1 - c)
        chips = _other_chips(x, y)
        shard_of_phase = [2 * cx + cy for cx, cy in chips] + [k]
        copy = _remote_copier(w_send, w_recv)
        red.bind([guq_hbm, gukv_hbm], [guq_out, gukv_out], more_scratch[:n_red])
        small.bind(gs_ref, gsum_ref, more_scratch[n_red:])
        mine = pl.ds(pl.multiple_of(c * hc, hc), hc)
        theirs = pl.ds(pl.multiple_of((1 - c) * hc, hc), hc)

        def to_sibling(f):
            j = shard_of_phase[f]
            return copy(f, pm_w.at[j, 1 - c], a_w.at[j], sibling)

        def pair_sum(f):
            cx, cy = chips[f]
            return copy(4 + f, pm_w.at[shard_of_phase[f], c], b_w.at[f], (cx, cy, c))

        def finished():
            return copy(7, r_w, gw_hbm.at[:, mine], sibling)

        @pl.when(jnp.logical_and(ph == 0, i == 0))
        def _():
            red.start()
            small.start()

        part = _dot(dz_ref[0], hn_ref[...])

        @pl.when(i == 0)
        def _():
            acc[...] = part

        @pl.when(i > 0)
        def _():
            acc[...] += part

        for f in range(3):
            @pl.when(jnp.logical_and(ph == f + 1, i == 0))
            def _(f=f):
                j = shard_of_phase[f]
                copy(f, a_w.at[j], a_w.at[j], me).wait_recv()
                pm_w[j, c] = (pm_w[j, c].astype(F32) + a_w[j].astype(F32)).astype(BF16)
                pair_sum(f).start()
                if f == 0:
                    red.exchange()

        for f in range(4):
            @pl.when(jnp.logical_and(ph == f, i == n_tiles - 1))
            def _(f=f):
                j = shard_of_phase[f]
                pm_w[j, 0] = acc[:, :hc].astype(BF16)
                pm_w[j, 1] = acc[:, hc:].astype(BF16)
                to_sibling(f).start()
                if f < 3:
                    return
                copy(3, a_w.at[k], a_w.at[k], me).wait_recv()
                r_w[...] = pm_w[k, c].astype(F32) + a_w[k].astype(F32)
                for g in range(3):
                    copy(4 + g, b_w.at[g], b_w.at[g], me).wait_recv()
                    r_w[...] = r_w[...] + b_w[g].astype(F32)
                store = pltpu.make_async_copy(r_w, gw_hbm.at[:, mine], w_local)
                store.start()
                finished().start()
                red.finish()
                small.finish()
                copy(7, gw_hbm.at[:, theirs], gw_hbm.at[:, theirs], me).wait_recv()
                store.wait()
                for g in range(4):
                    to_sibling(g).wait_send()
                for g in range(3):
                    pair_sum(g).wait_send()
                finished().wait_send()

    any_spec = pl.BlockSpec(memory_space=pl.ANY)
    n_sem = 8
    grid_spec = pltpu.PrefetchScalarGridSpec(
        num_scalar_prefetch=1,
        grid=(N_CHIPS, n_tiles),
        in_specs=[
            pl.BlockSpec((1, SHARD_COLS, tm), lambda ph, i, order: (order[ph], 0, i)),
            pl.BlockSpec((tm, D_MODEL), lambda ph, i, order: (i, 0)),
            any_spec, any_spec,
            pl.BlockSpec(small.spec_shape, lambda ph, i, order: (0, 0)),
        ],
        out_specs=[any_spec, any_spec, any_spec, pl.BlockSpec(small.spec_shape, lambda ph, i, order: (0, 0))],
        scratch_shapes=[
            pltpu.VMEM((SHARD_COLS, D_MODEL), F32),
            pltpu.VMEM((N_CHIPS, 2, SHARD_COLS, hc), BF16),
            pltpu.VMEM((N_CHIPS, SHARD_COLS, hc), BF16),
            pltpu.VMEM((3, SHARD_COLS, hc), BF16),
            pltpu.VMEM((SHARD_COLS, hc), F32),
            pltpu.SemaphoreType.DMA((n_sem,)), pltpu.SemaphoreType.DMA((n_sem,)), pltpu.SemaphoreType.DMA,
        ] + red.scratch + small.scratch,
    )
    out = pl.pallas_call(
        body,
        name="inproj_bwd_w",
        grid_spec=grid_spec,
        out_shape=[jax.ShapeDtypeStruct((SHARD_COLS, D_MODEL), F32)] + red.out_shape
        + [small.out_shape],
        compiler_params=pltpu.CompilerParams(dimension_semantics=("arbitrary", "arbitrary"),
                                             vmem_limit_bytes=VMEM_LIMIT),
    )(order, dz_sh, hn, g_uq, g_ukv, gs)
    return out[0], out[1], out[2], out[3]


def _other_chips(x, y):
    return ((1 - x, 1 - y), (1 - x, y), (x, 1 - y))


def _half(ref, axis, size, c, lead=()):
    window = pl.ds(pl.multiple_of(c * size, size), size)
    if axis == 0:
        return ref.at[(*lead, window, slice(None))]
    return ref.at[(*lead, slice(None), window)]


def _half_shape(rows, cols, axis, size):
    return (size, cols) if axis == 0 else (rows, size)


def _remote_copier(send_sems, recv_sems):
    def copy(sem, src, dst, to):
        return pltpu.make_async_remote_copy(src_ref=src, dst_ref=dst, send_sem=send_sems.at[sem],
                                            recv_sem=recv_sems.at[sem], device_id=to, device_id_type=MESH)
    return copy


class _Gather:
    def __init__(self, params):
        self.params = params
        n = len(params)
        self.scratch = [pltpu.SemaphoreType.DMA((6 * n,)), pltpu.SemaphoreType.DMA((6 * n,)),
                        pltpu.SemaphoreType.DMA((n,))]
        self.out_shape = [jax.ShapeDtypeStruct((N_CHIPS, r, cc), BF16) for _, r, cc, _, _ in params]

    def bind(self, ins, outs, scratch):
        self.ins, self.outs = ins, outs
        send_sems, recv_sems, self.local_sems = scratch
        self.copy = _remote_copier(send_sems, recv_sems)
        self.x, self.y, self.c = lax.axis_index("x"), lax.axis_index("y"), lax.axis_index("c")
        self.k = 2 * self.x + self.y
        self.chips = _other_chips(self.x, self.y)

    def _local(self, p):
        return pltpu.make_async_copy(self.ins[p], self.outs[p].at[self.k], self.local_sems.at[p])

    def _first(self, p, j):
        _, _, _, axis, size = self.params[p]
        cx, cy = self.chips[j]
        return self.copy(6 * p + j, _half(self.ins[p], axis, size, self.c),
                         _half(self.outs[p], axis, size, self.c, (self.k,)), (cx, cy, self.c))

    def _relay(self, p, j, half_of):
        _, _, _, axis, size = self.params[p]
        cx, cy = self.chips[j]
        block = _half(self.outs[p], axis, size, half_of, (2 * cx + cy,))
        return self.copy(6 * p + 3 + j, block, block, (self.x, self.y, 1 - self.c))

    def start(self):
        for p in range(len(self.params)):
            self._local(p).start()
            for j in (1, 2, 0):
                self._first(p, j).start()

    def relay_one(self, p, j):
        _, _, _, axis, size = self.params[p]
        cx, cy = self.chips[j]
        landed = _half(self.outs[p], axis, size, self.c, (2 * cx + cy,))
        self.copy(6 * p + j, landed, landed, (self.x, self.y, self.c)).wait_recv()
        self._relay(p, j, self.c).start()

    def await_one(self, p, j):
        self._relay(p, j, 1 - self.c).wait_recv()

    def wait_sends(self):
        for p in range(len(self.params)):
            for j in range(3):
                self._first(p, j).wait_send()
                self._relay(p, j, self.c).wait_send()
            self._local(p).wait()

    def relay(self):
        for j in range(3):
            for p in range(len(self.params)):
                self.relay_one(p, j)

    def finish(self):
        for j in range(3):
            for p in range(len(self.params)):
                self.await_one(p, j)
        self.wait_sends()


class _Reduce:
    def __init__(self, params):
        self.params = params
        n = len(params)
        halves = [_half_shape(r, cc, axis, size) for _, r, cc, axis, size in params]
        self.scratch = ([pltpu.VMEM((N_CHIPS, *h), BF16) for h in halves]
                        + [pltpu.VMEM((N_CHIPS, *h), BF16) for h in halves]
                        + [pltpu.VMEM((3, *h), BF16) for h in halves]
                        + [pltpu.VMEM(h, F32) for h in halves]
                        + [pltpu.SemaphoreType.DMA((5 * n,)), pltpu.SemaphoreType.DMA((5 * n,)),
                           pltpu.SemaphoreType.DMA((2 * n,))])
        self.out_shape = [jax.ShapeDtypeStruct((r, cc), F32) for _, r, cc, _, _ in params]

    def bind(self, g_in, g_out, scratch):
        n = len(self.params)
        self.g_in, self.g_out = g_in, g_out
        self.pm, self.a_buf = scratch[0:n], scratch[n:2 * n]
        self.b_buf, self.r_buf = scratch[2 * n:3 * n], scratch[3 * n:4 * n]
        send_sems, recv_sems, self.local_sems = scratch[4 * n:]
        self.copy = _remote_copier(send_sems, recv_sems)
        self.x, self.y, self.c = lax.axis_index("x"), lax.axis_index("y"), lax.axis_index("c")
        self.k = 2 * self.x + self.y
        self.chips = _other_chips(self.x, self.y)
        self.me = (self.x, self.y, self.c)
        self.sibling = (self.x, self.y, 1 - self.c)

    def _load(self, p):
        _, _, _, axis, size = self.params[p]
        return pltpu.make_async_copy(_half(self.g_in[p], axis, size, self.c, (slice(None),)), self.pm[p],
                                     self.local_sems.at[p])

    def _to_sibling(self, p):
        _, _, _, axis, size = self.params[p]
        return self.copy(5 * p, _half(self.g_in[p], axis, size, 1 - self.c, (slice(None),)), self.a_buf[p],
                         self.sibling)

    def _pair_sum(self, p, j):
        cx, cy = self.chips[j]
        return self.copy(5 * p + 1 + j, self.pm[p].at[2 * cx + cy], self.b_buf[p].at[j], (cx, cy, self.c))

    def _store(self, p):
        _, _, _, axis, size = self.params[p]
        n = len(self.params)
        return pltpu.make_async_copy(self.r_buf[p], _half(self.g_out[p], axis, size, self.c),
                                     self.local_sems.at[n + p])

    def _finished(self, p):
        _, _, _, axis, size = self.params[p]
        return self.copy(5 * p + 4, self.r_buf[p], _half(self.g_out[p], axis, size, self.c), self.sibling)

    def start(self):
        for p in range(len(self.params)):
            self._load(p).start()
            self._to_sibling(p).start()

    def exchange(self):
        for p in range(len(self.params)):
            self._load(p).wait()
            self.copy(5 * p, self.a_buf[p], self.a_buf[p], self.me).wait_recv()
            for j, (cx, cy) in enumerate(self.chips):
                kj = 2 * cx + cy
                self.pm[p][kj] = (self.pm[p][kj].astype(F32) + self.a_buf[p][kj].astype(F32)).astype(BF16)
                self._pair_sum(p, j).start()
            self.r_buf[p][...] = self.pm[p][self.k].astype(F32) + self.a_buf[p][self.k].astype(F32)

    def finish(self):
        for p, (_, _, _, axis, size) in enumerate(self.params):
            for j in range(3):
                self.copy(5 * p + 1 + j, self.b_buf[p].at[j], self.b_buf[p].at[j], self.me).wait_recv()
                self.r_buf[p][...] = self.r_buf[p][...] + self.b_buf[p][j].astype(F32)
            self._store(p).start()
            self._finished(p).start()
        for p, (_, _, _, axis, size) in enumerate(self.params):
            theirs = _half(self.g_out[p], axis, size, 1 - self.c)
            self.copy(5 * p + 4, theirs, theirs, self.me).wait_recv()
            self._store(p).wait()
            self._to_sibling(p).wait_send()
            for j in range(3):
                self._pair_sum(p, j).wait_send()
            self._finished(p).wait_send()


class _SmallSum:
    def __init__(self, rows):
        self.rows = rows
        self.scratch = [pltpu.VMEM((N_DEV, rows, LANES), F32),
                        pltpu.SemaphoreType.DMA((N_DEV - 1,)), pltpu.SemaphoreType.DMA((N_DEV - 1,))]
        self.out_shape = jax.ShapeDtypeStruct((rows, LANES), F32)
        self.spec_shape = (rows, LANES)

    def bind(self, src, dst, scratch):
        self.src, self.dst = src, dst
        self.buf, send_sems, recv_sems = scratch
        self.copy = _remote_copier(send_sems, recv_sems)
        self.x, self.y, self.c = lax.axis_index("x"), lax.axis_index("y"), lax.axis_index("c")

    def _send(self, f):
        fx, fy, fc = [(a, b, d) for a in (0, 1) for b in (0, 1) for d in (0, 1)][f]
        x, y, c = self.x, self.y, self.c
        peer = (1 - x if fx else x, 1 - y if fy else y, 1 - c if fc else c)
        return self.copy(f - 1, self.src, self.buf.at[f], peer)

    def start(self):
        for f in range(1, N_DEV):
            self._send(f).start()
        self.buf[0] = self.src[...]

    def finish(self):
        me = (self.x, self.y, self.c)
        for f in range(1, N_DEV):
            self.copy(f - 1, self.buf.at[f], self.buf.at[f], me).wait_recv()
        dev = 4 * self.x + 2 * self.y + self.c
        total = self.buf[dev]
        for d in range(1, N_DEV):
            total = total + self.buf[jnp.bitwise_xor(dev, d)]
        self.dst[...] = total
        for f in range(1, N_DEV):
            self._send(f).wait_send()


def _adamw_math(w, g, m, v):
    m = ADAM_B1 * m + (1.0 - ADAM_B1) * g
    v = ADAM_B2 * v + (1.0 - ADAM_B2) * (g * g)
    m_hat = m / (1.0 - ADAM_B1 ** ADAM_STEP)
    v_hat = v / (1.0 - ADAM_B2 ** ADAM_STEP)
    delta = -ADAM_LR * (m_hat / (jnp.sqrt(v_hat) + ADAM_EPS) + ADAM_WD * w)
    return delta, m, v


def _adamw_tiled(w, g, m, v, tm):
    rows, cols = w.shape

    def body(w_ref, g_ref, m_ref, v_ref, d_ref, nm_ref, nv_ref):
        d_ref[...], nm_ref[...], nv_ref[...] = _adamw_math(w_ref[...], g_ref[...], m_ref[...], v_ref[...])

    spec = _row_spec(tm, cols)
    return pl.pallas_call(
        body,
        name="adamw_w_in",
        grid=(rows // tm,),
        in_specs=[spec] * 4,
        out_specs=[spec] * 3,
        out_shape=[jax.ShapeDtypeStruct(w.shape, F32)] * 3,
        compiler_params=pltpu.CompilerParams(dimension_semantics=("parallel",), vmem_limit_bytes=VMEM_LIMIT),
    )(w, g, m, v)


def _adamw_many(ws, gs, ms, vs):
    n = len(ws)

    def body(*refs):
        ins, outs = refs[:4 * n], refs[4 * n:]
        for i in range(n):
            d, nm, nv = _adamw_math(ins[i][...], ins[n + i][...], ins[2 * n + i][...], ins[3 * n + i][...])
            outs[i][...] = d
            outs[n + i][...] = nm
            outs[2 * n + i][...] = nv

    vmem_spec = pl.BlockSpec(memory_space=pltpu.VMEM)
    shapes = [jax.ShapeDtypeStruct(w.shape, F32) for w in ws]
    out = pl.pallas_call(
        body,
        name="adamw_small",
        in_specs=[vmem_spec] * (4 * n),
        out_specs=[vmem_spec] * (3 * n),
        out_shape=shapes * 3,
        compiler_params=pltpu.CompilerParams(vmem_limit_bytes=VMEM_LIMIT),
    )(*ws, *gs, *ms, *vs)
    return out[:n], out[n:2 * n], out[2 * n:]


def _pack_rows(parts, rows, dtype):
    flat = jnp.concatenate([p.reshape(-1).astype(dtype) for p in parts])
    flat = jnp.concatenate([flat, jnp.zeros((rows * LANES - flat.shape[0],), dtype)])
    return flat.reshape(rows, LANES)


def _unpack_rows(packed, shapes):
    flat = packed.reshape(-1)
    out, off = [], 0
    for _, shp in shapes:
        n = int(np.prod(shp))
        out.append(flat[off:off + n].reshape(shp))
        off += n
    return out


def _rope_tables(s):
    half = QK_ROPE_DIM // 2
    inv_freq = np.float32(ROPE_THETA) ** (-np.arange(half, dtype=np.float32) / np.float32(half))
    ang = (np.arange(s, dtype=np.float32)[:, None] * inv_freq[None, :]).astype(np.float32)
    cos, sin = np.cos(ang.astype(np.float64)).astype(np.float32), np.sin(ang.astype(np.float64)).astype(np.float32)
    z16 = np.zeros((s, half), np.float32)
    z32 = np.zeros((s, HEAD_PAD - QK_NOPE_DIM - QK_ROPE_DIM), np.float32)
    z64 = np.zeros((s, QK_NOPE_DIM), np.float32)
    rc = np.concatenate([np.ones((s, QK_NOPE_DIM), np.float32), cos, cos, z32], axis=1)
    rsa = np.concatenate([z64, -sin, z16, z32], axis=1)
    rsb = np.concatenate([z64, z16, sin, z32], axis=1)
    return jnp.asarray(rc), jnp.asarray(rsa), jnp.asarray(rsb)


def kernel(x, norm_in, w_in, q_norm, w_uq, kv_norm, w_ukv, pool_w, pool_scale, w_branch_attn, w_branch_pool, w_out, norm_final, loss_target, m_norm_in, m_w_in, m_q_norm, m_w_uq, m_kv_norm, m_w_ukv, m_pool_w, m_pool_scale, m_w_branch_attn, m_w_branch_pool, m_w_out, m_norm_final, v_norm_in, v_w_in, v_q_norm, v_w_uq, v_kv_norm, v_w_ukv, v_pool_w, v_pool_scale, v_w_branch_attn, v_w_branch_pool, v_w_out, v_norm_final):
    s = x.shape[1]
    t_att, t_row = _tiles(s)
    x2 = x.reshape(s, D_MODEL)
    tgt = loss_target.reshape(s, D_MODEL)

    local = [w_in.T, w_uq.reshape(96, 768), w_ukv.reshape(64, 1024), w_branch_attn, w_branch_pool, w_out]
    local = [a.astype(BF16) for a in local]
    cx, cy = lax.axis_index("x"), lax.axis_index("y")
    others = [2 * ox + oy for ox, oy in _other_chips(cx, cy)]
    hn, z_sh, (w_in_t, w_uq_all, w_ukv_all) = _inproj_fwd(
        jnp.stack([2 * cx + cy, others[1], others[2], others[0]]).astype(jnp.int32), x2, norm_in.reshape(1, -1),
        local[:3], 4 * t_row)
    w_uq_f = w_uq_all.reshape(Q_LORA_RANK, MLA_HEADS, QK_NOPE_DIM + QK_ROPE_DIM)
    w_ukv_f = w_ukv_all.reshape(KV_LORA_RANK, MLA_HEADS, QK_NOPE_DIM + V_HEAD_DIM)
    hw = MLA_HEADS * HEAD_PAD
    wuq_p = jnp.pad(w_uq_f, ((0, 0), (0, 0), (0, HEAD_PAD - QK_NOPE_DIM - QK_ROPE_DIM))).reshape(Q_LORA_RANK, hw)
    wk_p = jnp.pad(w_ukv_f[:, :, :QK_NOPE_DIM], ((0, 0), (0, 0), (0, HEAD_PAD - QK_NOPE_DIM))).reshape(KV_LORA_RANK, hw)
    wv = w_ukv_f[:, :, QK_NOPE_DIM:].reshape(KV_LORA_RANK, MLA_WIDTH)
    rc, rsa, rsb = _rope_tables(s)
    g_in = norm_in.reshape(1, -1)
    g_q = q_norm.reshape(1, -1)
    g_kv = kv_norm.reshape(1, -1)
    g_f = norm_final.reshape(1, -1)
    ps = pool_scale.reshape(1, -1)
    pw_bf = pool_w.astype(BF16)

    q, k, v, q_t, v_t = _qkv_fwd(z_sh, g_q, g_kv, wuq_p, wk_p, wv, rc, rsa, rsb, t_row)
    o, lse, (w_ba_all, w_bp_all, w_out_all) = _attn_fwd(q_t, k, v_t, local[3:], t_att)
    w_out_f = w_out_all.reshape(D_MODEL, D_MODEL)

    (do, delta, dgattn, dgpool, dgmerge, ddc, dh, sq_err, d_w_out, d_w_ba, d_w_bp, d_pool_w, d_pool_scale,
     d_norm_final) = _mid(o, z_sh, x2, tgt, pw_bf, ps, w_ba_all, w_bp_all, w_out_f,
                          [jnp.swapaxes(w, -1, -2) for w in (pw_bf, w_ba_all, w_bp_all, w_out_f)], g_f, t_row)

    late_grads = [d_w_ba, d_w_bp, d_w_out.reshape(N_CHIPS, 256, D_MODEL)]
    small_mid = dict(pool_scale=d_pool_scale, norm_final=d_norm_final, pool_w=d_pool_w, sq_err=sq_err)
    gs_mid = _pack_rows([small_mid[n] for n, _ in SMALL_MID], _small_rows(SMALL_MID), F32)
    dq, dk_t, dv_t, (g_w_ba, g_w_bp, g_w_out), g_small_mid = _attn_bwd(q, q_t, k, v, do, lse, delta, late_grads,
                                                                      gs_mid, t_att)
    g_pool_scale, g_norm_final, g_pool_w, sq_err_all = _unpack_rows(g_small_mid, SMALL_MID)
    dzq, dzkv, dzkr, d_wuq_p, d_wk_p, d_wv, d_q_norm, d_kv_norm = _qkv_bwd(
        dq, dk_t, dv_t, z_sh, g_q, g_kv, wuq_p, wk_p, wv, rc, rsa, rsb, t_row)
    grad_x, d_norm_in, dz_sh = _inproj_bwd_x(dzq, dzkv, dzkr, dgattn, ddc, dgpool, dgmerge, x2, dh, g_in, w_in_t,
                                             t_row)

    d_w_uq = d_wuq_p.reshape(Q_LORA_RANK, MLA_HEADS, HEAD_PAD)[:, :, :QK_NOPE_DIM + QK_ROPE_DIM]
    d_w_ukv = jnp.concatenate([d_wk_p.reshape(KV_LORA_RANK, MLA_HEADS, HEAD_PAD)[:, :, :QK_NOPE_DIM],
                               d_wv.reshape(KV_LORA_RANK, MLA_HEADS, V_HEAD_DIM)], axis=2)
    small_late = dict(norm_in=d_norm_in, q_norm=d_q_norm, kv_norm=d_kv_norm)
    gs = _pack_rows([small_late[n] for n, _ in SMALL_LATE], _small_rows(SMALL_LATE), F32)
    order = jnp.stack(others + [2 * cx + cy]).astype(jnp.int32)
    g_w_in_t, g_w_uq, g_w_ukv, g_small = _inproj_bwd_w(
        order, dz_sh, hn, d_w_uq.reshape(N_CHIPS, 96, 768).astype(BF16),
        d_w_ukv.reshape(N_CHIPS, 64, 1024).astype(BF16), gs, 4 * t_row)
    g_norm_in, g_q_norm, g_kv_norm = _unpack_rows(g_small, SMALL_LATE)
    g_w_uq = g_w_uq.reshape(w_uq.shape)
    g_w_ukv = g_w_ukv.reshape(w_ukv.shape)

    dl_w_in, nm_w_in, nv_w_in = (a.T for a in _adamw_tiled(w_in.T, g_w_in_t, m_w_in.T, v_w_in.T, 152))

    def two_d(a):
        return a.reshape(1, -1) if a.ndim == 1 else a

    names = ["norm_in", "q_norm", "w_uq", "kv_norm", "w_ukv", "pool_w", "pool_scale", "w_branch_attn",
             "w_branch_pool", "w_out", "norm_final"]
    ws = dict(norm_in=norm_in, q_norm=q_norm, w_uq=w_uq, kv_norm=kv_norm, w_ukv=w_ukv, pool_w=pool_w,
              pool_scale=pool_scale, w_branch_attn=w_branch_attn, w_branch_pool=w_branch_pool, w_out=w_out,
              norm_final=norm_final)
    gsd = dict(norm_in=g_norm_in, q_norm=g_q_norm, w_uq=g_w_uq, kv_norm=g_kv_norm, w_ukv=g_w_ukv, pool_w=g_pool_w,
               pool_scale=g_pool_scale, w_branch_attn=g_w_ba, w_branch_pool=g_w_bp, w_out=g_w_out,
               norm_final=g_norm_final)
    msd = dict(norm_in=m_norm_in, q_norm=m_q_norm, w_uq=m_w_uq, kv_norm=m_kv_norm, w_ukv=m_w_ukv, pool_w=m_pool_w,
               pool_scale=m_pool_scale, w_branch_attn=m_w_branch_attn, w_branch_pool=m_w_branch_pool, w_out=m_w_out,
               norm_final=m_norm_final)
    vsd = dict(norm_in=v_norm_in, q_norm=v_q_norm, w_uq=v_w_uq, kv_norm=v_kv_norm, w_ukv=v_w_ukv, pool_w=v_pool_w,
               pool_scale=v_pool_scale, w_branch_attn=v_w_branch_attn, w_branch_pool=v_w_branch_pool, w_out=v_w_out,
               norm_final=v_norm_final)
    dls, nms, nvs = _adamw_many([two_d(ws[n]) for n in names], [two_d(gsd[n]) for n in names],
                                [two_d(msd[n]) for n in names], [two_d(vsd[n]) for n in names])

    grads = dict(gsd)
    grads["w_in"] = g_w_in_t.T
    delta_w = {n: d.reshape(ws[n].shape) for n, d in zip(names, dls)}
    new_m = {n: d.reshape(ws[n].shape) for n, d in zip(names, nms)}
    new_v = {n: d.reshape(ws[n].shape) for n, d in zip(names, nvs)}
    delta_w["w_in"], new_m["w_in"], new_v["w_in"] = dl_w_in, nm_w_in, nv_w_in
    ws["w_in"] = w_in

    order = ["norm_in", "w_in", "q_norm", "w_uq", "kv_norm", "w_ukv", "pool_w", "pool_scale", "w_branch_attn",
             "w_branch_pool", "w_out", "norm_final"]
    loss = 0.5 * jnp.sum(sq_err_all) / D_MODEL
    return (loss, grad_x.reshape(x.shape),
            *[grads[n].reshape(ws[n].shape) for n in order],
            *[delta_w[n] for n in order], *[new_m[n] for n in order], *[new_v[n] for n in order])
```

```python
import functools

import jax
import jax.numpy as jnp
import numpy as np
from jax import lax
from jax.experimental import pallas as pl
from jax.experimental.pallas import tpu as pltpu

F32 = jnp.float32
BF16 = jnp.bfloat16
MESH = pl.DeviceIdType.MESH

D_MODEL = 1024
CHUNK = 64
MLA_HEADS = 8
QK_NOPE_DIM = 64
QK_ROPE_DIM = 32
V_HEAD_DIM = 64
Q_LORA_RANK = 384
KV_LORA_RANK = 256
MLA_WIDTH = MLA_HEADS * V_HEAD_DIM
ROPE_THETA = 10000.0
POOL_WINDOWS = (2, 4, 8, 16)
POOL_WIDTH = 512
POOL_GROUP_DIM = 128
BRANCH_COLS = D_MODEL // 4
ATT_HEADS = 4
POOL_HALO = 16
EPS = 1e-6
IN_TOTAL = 4256
HEAD_PAD = 128
ATT_SCALE = (QK_NOPE_DIM + QK_ROPE_DIM) ** -0.5
ATT_SCALE_LOG2E = ATT_SCALE * 1.4426950408889634

ADAM_LR = 0.001
ADAM_B1 = 0.9
ADAM_B2 = 0.999
ADAM_EPS = 1e-08
ADAM_WD = 0.01
ADAM_STEP = 10

N_CHIPS = 4
N_DEV = 8
LANES = 128
VMEM_LIMIT = 60 * 1024 * 1024

IN_SEGMENTS = ((384, 384), (256, 256), (32, HEAD_PAD), (512, 512), (512, 512), (512, 512), (2048, 2048))
SHARD_COLS = IN_TOTAL // N_CHIPS
ZQ_COLS = slice(0, 384)
ZKV_COLS = slice(384, 640)
ZKR_TILE = slice(640, 768)


def _shard_pieces():
    bounds, off = [], 0
    for w, _ in IN_SEGMENTS:
        bounds.append((off, off + w))
        off += w
    out = []
    for j in range(N_CHIPS):
        lo, hi = SHARD_COLS * j, SHARD_COLS * (j + 1)
        out.append([(i, max(lo, a) - a, min(hi, b) - a, max(lo, a) - lo)
                    for i, (a, b) in enumerate(bounds) if max(lo, a) < min(hi, b)])
    return out


SHARD_PIECES = _shard_pieces()


def _segment(z_blocks, seg):
    parts = [z_blocks[j][:, col:col + hi - lo]
             for j, pieces in enumerate(SHARD_PIECES) for sg, lo, hi, col in pieces if sg == seg]
    return parts[0] if len(parts) == 1 else jnp.concatenate(parts, axis=1)

COMM_PARAMS = (
    ("w_in", SHARD_COLS, D_MODEL, 1, 512),
    ("w_uq", 96, 768, 0, 48),
    ("w_ukv", 64, 1024, 0, 32),
    ("w_branch_attn", 512, 256, 0, 256),
    ("w_branch_pool", 512, 256, 0, 256),
    ("w_out", 256, 1024, 0, 128),
)

SMALL_MID = (
    ("pool_scale", (512,)),
    ("norm_final", (1024,)),
    ("pool_w", (4, 128, 128)),
    ("sq_err", (8, 128)),
)
SMALL_LATE = (
    ("norm_in", (1024,)),
    ("q_norm", (384,)),
    ("kv_norm", (256,)),
)


def _small_rows(shapes):
    return -(-sum(int(np.prod(s)) for _, s in shapes) // (LANES * 8)) * 8


def _dot(a, b):
    return jnp.dot(a, b, preferred_element_type=F32)


def _dot_nt(a, b):
    return lax.dot_general(a, b, (((1,), (1,)), ((), ())), preferred_element_type=F32)


def _dot_tn(a, b):
    return lax.dot_general(a, b, (((0,), (0,)), ((), ())), preferred_element_type=F32)


def _sigmoid(x):
    return 1.0 / (1.0 + jnp.exp(-x))


def _colsum(x):
    return jnp.sum(x, axis=0, keepdims=True)


def _rms_fwd(x, g):
    r = lax.rsqrt(jnp.mean(x * x, axis=-1, keepdims=True) + EPS)
    xhat = x * r
    return xhat * g, xhat, r


def _rms_bwd(dy, xhat, r, g):
    dxhat = dy * g
    return r * (dxhat - xhat * jnp.mean(dxhat * xhat, axis=-1, keepdims=True))


def _rope(v, c, sa, sb):
    return v * c + pltpu.roll(v, 112, 1) * sa + pltpu.roll(v, 16, 1) * sb


def _unrope(d, c, sa, sb):
    return d * c + pltpu.roll(d * sa, 16, 1) + pltpu.roll(d * sb, 112, 1)


def _row_spec(tm, n):
    return pl.BlockSpec((tm, n), lambda i: (i, 0))


def _full_spec(shape):
    nd = len(shape)
    return pl.BlockSpec(shape, lambda i: (0,) * nd)


def _tiles(s):
    t_att = 512 if s >= 2048 else 128
    t_row = 256 if s >= 1024 else 128
    return t_att, t_row


def _inproj_fwd(order, x, norm_in, early_shards, tm):
    s = x.shape[0]
    n_tiles = s // tm
    gat = _Gather(COMM_PARAMS[:3])
    n_w = len(gat.params)
    arrival = (1, 2, 0)

    def body(order_ref, x_ref, g_ref, *rest):
        w_loc, (hn_ref, z_ref), w_all = rest[:n_w], rest[n_w:n_w + 2], rest[n_w + 2:2 * n_w + 2]
        w_vmem, hn_all, w_sem = rest[2 * n_w + 2:2 * n_w + 5]
        gat.bind(w_loc, w_all, rest[2 * n_w + 5:])
        ph, i = pl.program_id(0), pl.program_id(1)
        pl.when(jnp.logical_and(ph == 0, i == 0))(gat.start)

        @pl.when(jnp.logical_and(ph == 0, i == 0))
        def _():
            cp = pltpu.make_async_copy(w_loc[0], w_vmem, w_sem)
            cp.start()
            cp.wait()

        for f in range(3):
            @pl.when(jnp.logical_and(ph == f + 1, i == 0))
            def _(f=f):
                gat.relay_one(0, arrival[f])
                gat.await_one(0, arrival[f])
                cp = pltpu.make_async_copy(w_all[0].at[order_ref[ph]], w_vmem, w_sem)
                cp.start()
                cp.wait()

        rows = pl.ds(pl.multiple_of(i * tm, tm), tm)

        @pl.when(ph == 0)
        def _():
            hn, _, _ = _rms_fwd(x_ref[...], g_ref[...])
            hn = hn.astype(BF16)
            hn_ref[...] = hn
            hn_all[rows, :] = hn

        z_ref[0] = _dot_nt(hn_all[rows, :], w_vmem[...])

        @pl.when(jnp.logical_and(ph == N_CHIPS - 1, i == n_tiles - 1))
        def _():
            for p in range(1, n_w):
                for j in range(3):
                    gat.relay_one(p, j)
            for p in range(1, n_w):
                for j in range(3):
                    gat.await_one(p, j)
            gat.wait_sends()

    def tile_in_phase0(ph, i, order):
        return (jnp.where(ph == 0, i, n_tiles - 1), 0)

    any_spec = pl.BlockSpec(memory_space=pl.ANY)
    grid_spec = pltpu.PrefetchScalarGridSpec(
        num_scalar_prefetch=1,
        grid=(N_CHIPS, n_tiles),
        in_specs=[pl.BlockSpec((tm, D_MODEL), tile_in_phase0),
                  pl.BlockSpec((1, D_MODEL), lambda ph, i, order: (0, 0))] + [any_spec] * n_w,
        out_specs=[pl.BlockSpec((tm, D_MODEL), tile_in_phase0),
                   pl.BlockSpec((1, tm, SHARD_COLS), lambda ph, i, order: (order[ph], i, 0))] + [any_spec] * n_w,
        scratch_shapes=[pltpu.VMEM((SHARD_COLS, D_MODEL), BF16), pltpu.VMEM((s, D_MODEL), BF16),
                        pltpu.SemaphoreType.DMA] + gat.scratch,
    )
    out = pl.pallas_call(
        body,
        name="inproj_fwd",
        grid_spec=grid_spec,
        out_shape=[jax.ShapeDtypeStruct((s, D_MODEL), BF16), jax.ShapeDtypeStruct((N_CHIPS, s, SHARD_COLS), F32)]
        + gat.out_shape,
        compiler_params=pltpu.CompilerParams(dimension_semantics=("arbitrary", "arbitrary"),
                                             vmem_limit_bytes=VMEM_LIMIT),
    )(order, x, norm_in, *early_shards)
    return out[0], out[1], out[2:]


def _qkv_fwd(z_sh, q_norm, kv_norm, wuq_p, wk_p, wv, rc, rsa, rsb, tm):
    s = z_sh.shape[1]
    hw = MLA_HEADS * HEAD_PAD

    def body(z_ref, gq_ref, gkv_ref, wuq_ref, wk_ref, wv_ref, c_ref, sa_ref, sb_ref,
             q_ref, k_ref, v_ref, qt_ref, vt_ref):
        c, sa, sb = c_ref[...], sa_ref[...], sb_ref[...]
        z0 = z_ref[0]
        cq, _, _ = _rms_fwd(z0[:, ZQ_COLS], gq_ref[...])
        qf = _dot(cq.astype(BF16), wuq_ref[...])
        ckv, _, _ = _rms_fwd(z0[:, ZKV_COLS], gkv_ref[...])
        ckv = ckv.astype(BF16)
        kn = _dot(ckv, wk_ref[...])
        lane = lax.broadcasted_iota(jnp.int32, (tm, HEAD_PAD), 1)
        zkr = jnp.where(lane < QK_ROPE_DIM, z0[:, ZKR_TILE], 0.0)
        kr = _rope(pltpu.roll(zkr, 64, 1), c, sa, sb)
        for h in range(MLA_HEADS):
            cols = slice(h * HEAD_PAD, (h + 1) * HEAD_PAD)
            qh = _rope(qf[:, cols], c, sa, sb)
            q_ref[:, cols] = qh.astype(BF16)
            qt_ref[cols, :] = qh.T.astype(BF16)
            k_ref[:, cols] = (kn[:, cols] + kr).astype(BF16)
        vf = _dot(ckv, wv_ref[...])
        v_ref[...] = vf.astype(BF16)
        vt_ref[...] = vf.T.astype(BF16)

    return pl.pallas_call(
        body,
        name="qkv_fwd",
        grid=(s // tm,),
        in_specs=[
            pl.BlockSpec((1, tm, SHARD_COLS), lambda i: (0, i, 0)),
            _full_spec((1, Q_LORA_RANK)), _full_spec((1, KV_LORA_RANK)),
            _full_spec((Q_LORA_RANK, hw)), _full_spec((KV_LORA_RANK, hw)), _full_spec((KV_LORA_RANK, MLA_WIDTH)),
            _row_spec(tm, HEAD_PAD), _row_spec(tm, HEAD_PAD), _row_spec(tm, HEAD_PAD),
        ],
        out_specs=[_row_spec(tm, hw), _row_spec(tm, hw), _row_spec(tm, MLA_WIDTH),
                   pl.BlockSpec((hw, tm), lambda i: (0, i)), pl.BlockSpec((MLA_WIDTH, tm), lambda i: (0, i))],
        out_shape=[jax.ShapeDtypeStruct((s, hw), BF16), jax.ShapeDtypeStruct((s, hw), BF16),
                   jax.ShapeDtypeStruct((s, MLA_WIDTH), BF16),
                   jax.ShapeDtypeStruct((hw, s), BF16), jax.ShapeDtypeStruct((MLA_WIDTH, s), BF16)],
        compiler_params=pltpu.CompilerParams(dimension_semantics=("parallel",), vmem_limit_bytes=VMEM_LIMIT),
    )(z_sh, q_norm, kv_norm, wuq_p, wk_p, wv, rc, rsa, rsb)


def _chunk_mask(t, keys_on_rows):
    rows = lax.broadcasted_iota(jnp.int32, (t, t), 0) // CHUNK
    cols = lax.broadcasted_iota(jnp.int32, (t, t), 1) // CHUNK
    return rows <= cols if keys_on_rows else cols <= rows


def _attn_fwd(q_t, k, v_t, late_shards, t):
    s = k.shape[0]
    groups = MLA_HEADS // ATT_HEADS
    n_q = s // t
    gat = _Gather(COMM_PARAMS[3:])
    n_w = len(gat.params)

    def body(qt_ref, k_ref, k2_ref, vt_ref, *rest):
        w_in, (o_ref, lse_ref), w_out = rest[:n_w], rest[n_w:n_w + 2], rest[n_w + 2:2 * n_w + 2]
        gat.bind(w_in, w_out, rest[2 * n_w + 2:])
        i = pl.program_id(1)
        step_no = pl.program_id(0) * n_q + i
        pl.when(step_no == 0)(gat.start)
        pl.when(step_no == n_q)(gat.relay)
        mask = _chunk_mask(t, True)
        qcs = [slice(hh * HEAD_PAD, (hh + 1) * HEAD_PAD) for hh in range(ATT_HEADS)]
        vcs = [slice(hh * V_HEAD_DIM, (hh + 1) * V_HEAD_DIM) for hh in range(ATT_HEADS)]
        qts = [qt_ref[qc, :] for qc in qcs]

        def step(j, carry, masked):
            keys = pl.ds(pl.multiple_of(j * t, t), t)
            out = []
            for hh in range(ATT_HEADS):
                m, l, acc = carry[hh]
                sc = _dot(k_ref[keys, qcs[hh]], qts[hh])
                if masked:
                    sc = jnp.where(mask, sc, -jnp.inf)
                m_new = jnp.maximum(m, jnp.max(sc, axis=0, keepdims=True))
                alpha = jnp.exp2((m - m_new) * ATT_SCALE_LOG2E)
                p = jnp.exp2((_dot(k2_ref[keys, qcs[hh]], qts[hh]) - m_new) * ATT_SCALE_LOG2E)
                if masked:
                    p = jnp.where(mask, p, 0.0)
                l = alpha * l + jnp.sum(p, axis=0, keepdims=True)
                acc = alpha * acc + _dot(vt_ref[vcs[hh], keys], p.astype(BF16))
                out.append((m_new, l, acc))
            return tuple(out)

        one = (jnp.full((1, t), -jnp.inf, F32), jnp.zeros((1, t), F32), jnp.zeros((V_HEAD_DIM, t), F32))
        carry = lax.fori_loop(0, i, functools.partial(step, masked=False), (one,) * ATT_HEADS)
        carry = step(i, carry, True)
        o_ref[...] = jnp.concatenate([carry[hh][2] / carry[hh][1] for hh in range(ATT_HEADS)], axis=0).T
        for hh in range(ATT_HEADS):
            m, l, _ = carry[hh]
            lse_ref[:, qcs[hh]] = jnp.broadcast_to(m * ATT_SCALE_LOG2E + jnp.log2(l), (HEAD_PAD, t)).T
        pl.when(step_no == groups * n_q - 1)(gat.finish)

    any_spec = pl.BlockSpec(memory_space=pl.ANY)
    out = pl.pallas_call(
        body,
        name="attn_fwd",
        grid=(groups, n_q),
        in_specs=[
            pl.BlockSpec((ATT_HEADS * HEAD_PAD, t), lambda p, i: (p, i)),
            pl.BlockSpec((s, ATT_HEADS * HEAD_PAD), lambda p, i: (0, p)),
            pl.BlockSpec((s, ATT_HEADS * HEAD_PAD), lambda p, i: (0, p)),
            pl.BlockSpec((ATT_HEADS * V_HEAD_DIM, s), lambda p, i: (p, 0)),
        ] + [any_spec] * n_w,
        out_specs=[
            pl.BlockSpec((t, ATT_HEADS * V_HEAD_DIM), lambda p, i: (i, p)),
            pl.BlockSpec((t, ATT_HEADS * HEAD_PAD), lambda p, i: (i, p)),
        ] + [any_spec] * n_w,
        out_shape=[jax.ShapeDtypeStruct((s, MLA_WIDTH), F32), jax.ShapeDtypeStruct((s, MLA_HEADS * HEAD_PAD), F32)]
        + gat.out_shape,
        scratch_shapes=gat.scratch,
        compiler_params=pltpu.CompilerParams(dimension_semantics=("arbitrary", "arbitrary"),
                                             vmem_limit_bytes=VMEM_LIMIT),
    )(q_t, k, k, v_t, *late_shards)
    return out[0], out[1], out[2:]


def _mid(o, z_sh, x, target, pool_w, pool_scale, w_ba, w_bp, w_out, norm_final, tm):
    s = x.shape[0]
    n_tiles = s // tm
    halo_per_tile = tm // POOL_HALO

    def body(o_ref, z0_ref, z1_ref, z1h_ref, z2_ref, z3_ref, x_ref, t_ref, pw_ref, ps_ref, wba_ref, wbp_ref,
             wout_ref, gf_ref,
             do_ref, dl_ref, dga_ref, dgp_ref, dgm_ref, ddc_ref, dh_ref,
             loss_ref, dwout_out, dwba_out, dwbp_out, dpw_ref, dps_ref, dgf_ref,
             ubuf, dwout_ref, dwba_ref, dwbp_ref):
        i = pl.program_id(0)

        @pl.when(i == 0)
        def _():
            loss_ref[...] = jnp.zeros_like(loss_ref)
            dwout_ref[...] = jnp.zeros_like(dwout_ref)
            dwba_ref[...] = jnp.zeros_like(dwba_ref)
            dwbp_ref[...] = jnp.zeros_like(dwbp_ref)
            dpw_ref[...] = jnp.zeros_like(dpw_ref)
            dps_ref[...] = jnp.zeros_like(dps_ref)
            dgf_ref[...] = jnp.zeros_like(dgf_ref)

        zs = [z0_ref[0], z1_ref[0], z2_ref[0], z3_ref[0]]
        o = o_ref[...]
        ga = _segment(zs, 3)
        sga = _sigmoid(ga)
        silu_a = ga * sga
        y_attn = (o * silu_a).astype(BF16)

        ubuf[0:POOL_HALO, :] = jnp.where(i > 0, _segment([None, z1h_ref[0]], 4), 0.0)
        ubuf[POOL_HALO:, :] = _segment(zs, 4)
        row = lax.broadcasted_iota(jnp.int32, (tm, POOL_GROUP_DIM), 0) + i * tm
        ps = ps_ref[...]
        gp = _segment(zs, 5)
        sgp = _sigmoid(gp)
        silu_p = gp * sgp
        d_bf, dm, inv_cnt = [], [], []
        for g, w in enumerate(POOL_WINDOWS):
            cols = slice(g * POOL_GROUP_DIM, (g + 1) * POOL_GROUP_DIM)
            wsum = ubuf[POOL_HALO:, cols]
            for kk in range(1, w):
                wsum = wsum + ubuf[POOL_HALO - kk:POOL_HALO - kk + tm, cols]
            inv = 1.0 / jnp.minimum(row + 1, w).astype(F32)
            dg = (wsum * inv - ubuf[POOL_HALO:, cols]).astype(BF16)
            d_bf.append(dg)
            inv_cnt.append(inv)
            dm.append(_dot(dg, pw_ref[g]))
        dm = jnp.concatenate(dm, axis=1)
        yp = dm * ps
        y_pool = (yp * silu_p).astype(BF16)

        a = jnp.concatenate([_dot(y_attn, wba_ref[j]) for j in range(N_CHIPS)], axis=1)
        p = jnp.concatenate([_dot(y_pool, wbp_ref[j]) for j in range(N_CHIPS)], axis=1)
        gm = _segment(zs, 6)
        gate_a = _sigmoid(gm[:, :D_MODEL])
        gate_p = _sigmoid(gm[:, D_MODEL:])
        merged = (gate_a * a + gate_p * p).astype(BF16)
        h = x_ref[...] + _dot(merged, wout_ref[...])
        gf = gf_ref[...]
        y, xhat, r = _rms_fwd(h, gf)
        err = y - t_ref[...]
        e2 = err * err
        e2 = jnp.sum(e2.reshape(tm // 8, 8, D_MODEL), axis=0)
        acc = e2[:, 0:LANES]
        for cidx in range(1, D_MODEL // LANES):
            acc = acc + e2[:, cidx * LANES:(cidx + 1) * LANES]
        loss_ref[...] += acc

        dy = err * (1.0 / D_MODEL)
        dgf_ref[...] += _colsum(dy * xhat)
        dh = _rms_bwd(dy, xhat, r, gf)
        dh_ref[...] = dh
        dh_bf = dh.astype(BF16)
        dwout_ref[...] += _dot_tn(merged, dh_bf)
        dmerged = _dot_nt(dh_bf, wout_ref[...])
        da = (dmerged * gate_a).astype(BF16)
        dp = (dmerged * gate_p).astype(BF16)
        dgm_ref[:, :D_MODEL] = (dmerged * a * gate_a * (1.0 - gate_a)).astype(BF16)
        dgm_ref[:, D_MODEL:] = (dmerged * p * gate_p * (1.0 - gate_p)).astype(BF16)
        dy_attn = dy_pool = None
        for j in range(N_CHIPS):
            cols = slice(j * BRANCH_COLS, (j + 1) * BRANCH_COLS)
            dwba_ref[j] += _dot_tn(y_attn, da[:, cols])
            dwbp_ref[j] += _dot_tn(y_pool, dp[:, cols])
            pa = _dot_nt(da[:, cols], wba_ref[j])
            pp = _dot_nt(dp[:, cols], wbp_ref[j])
            dy_attn = pa if dy_attn is None else dy_attn + pa
            dy_pool = pp if dy_pool is None else dy_pool + pp

        do = dy_attn * silu_a
        do_ref[...] = do
        dga_ref[...] = (dy_attn * o * (sga * (1.0 + ga * (1.0 - sga)))).astype(BF16)
        doo = do * o
        for hd in range(MLA_HEADS):
            dl = jnp.sum(doo[:, hd * V_HEAD_DIM:(hd + 1) * V_HEAD_DIM], axis=1, keepdims=True)
            dl_ref[:, hd * HEAD_PAD:(hd + 1) * HEAD_PAD] = jnp.broadcast_to(dl, (tm, HEAD_PAD))

        dyp = dy_pool * silu_p
        dgp_ref[...] = (dy_pool * yp * (sgp * (1.0 + gp * (1.0 - sgp)))).astype(BF16)
        dps_ref[...] += _colsum(dyp * dm)
        dmm = (dyp * ps).astype(BF16)
        for g in range(len(POOL_WINDOWS)):
            cols = slice(g * POOL_GROUP_DIM, (g + 1) * POOL_GROUP_DIM)
            dpw_ref[g] += _dot_tn(d_bf[g], dmm[:, cols])
            ddc_ref[:, cols] = _dot_nt(dmm[:, cols], pw_ref[g]) * inv_cnt[g]

        @pl.when(i == n_tiles - 1)
        def _():
            dwout_out[...] = dwout_ref[...].astype(BF16)
            dwba_out[...] = dwba_ref[...].astype(BF16)
            dwbp_out[...] = dwbp_ref[...].astype(BF16)

    row_in = lambda n: _row_spec(tm, n)
    in_specs = [
        row_in(MLA_WIDTH),
        pl.BlockSpec((1, tm, SHARD_COLS), lambda i: (0, i, 0)), pl.BlockSpec((1, tm, SHARD_COLS), lambda i: (1, i, 0)),
        pl.BlockSpec((1, POOL_HALO, SHARD_COLS), lambda i: (1, jnp.maximum(i * halo_per_tile - 1, 0), 0)),
        pl.BlockSpec((1, tm, SHARD_COLS), lambda i: (2, i, 0)), pl.BlockSpec((1, tm, SHARD_COLS), lambda i: (3, i, 0)),
        row_in(D_MODEL), row_in(D_MODEL),
        _full_spec((4, POOL_GROUP_DIM, POOL_GROUP_DIM)), _full_spec((1, POOL_WIDTH)),
        _full_spec((N_CHIPS, MLA_WIDTH, BRANCH_COLS)), _full_spec((N_CHIPS, POOL_WIDTH, BRANCH_COLS)),
        _full_spec((D_MODEL, D_MODEL)), _full_spec((1, D_MODEL)),
    ]
    out_shape = [
        jax.ShapeDtypeStruct((s, MLA_WIDTH), F32),
        jax.ShapeDtypeStruct((s, MLA_HEADS * HEAD_PAD), F32),
        jax.ShapeDtypeStruct((s, MLA_WIDTH), BF16),
        jax.ShapeDtypeStruct((s, POOL_WIDTH), BF16),
        jax.ShapeDtypeStruct((s, 2 * D_MODEL), BF16),
        jax.ShapeDtypeStruct((s, POOL_WIDTH), F32),
        jax.ShapeDtypeStruct((s, D_MODEL), F32),
        jax.ShapeDtypeStruct((8, LANES), F32),
        jax.ShapeDtypeStruct((D_MODEL, D_MODEL), BF16),
        jax.ShapeDtypeStruct((N_CHIPS, MLA_WIDTH, BRANCH_COLS), BF16),
        jax.ShapeDtypeStruct((N_CHIPS, POOL_WIDTH, BRANCH_COLS), BF16),
        jax.ShapeDtypeStruct((4, POOL_GROUP_DIM, POOL_GROUP_DIM), F32),
        jax.ShapeDtypeStruct((1, POOL_WIDTH), F32),
        jax.ShapeDtypeStruct((1, D_MODEL), F32),
    ]
    out_specs = [
        row_in(MLA_WIDTH), row_in(MLA_HEADS * HEAD_PAD), row_in(MLA_WIDTH), row_in(POOL_WIDTH),
        row_in(2 * D_MODEL), row_in(POOL_WIDTH), row_in(D_MODEL),
        _full_spec((8, LANES)), _full_spec((D_MODEL, D_MODEL)), _full_spec((N_CHIPS, MLA_WIDTH, BRANCH_COLS)),
        _full_spec((N_CHIPS, POOL_WIDTH, BRANCH_COLS)), _full_spec((4, POOL_GROUP_DIM, POOL_GROUP_DIM)),
        _full_spec((1, POOL_WIDTH)), _full_spec((1, D_MODEL)),
    ]
    return pl.pallas_call(
        body,
        name="mid",
        grid=(n_tiles,),
        in_specs=in_specs,
        out_specs=out_specs,
        out_shape=out_shape,
        scratch_shapes=[
            pltpu.VMEM((tm + POOL_HALO, POOL_WIDTH), F32),
            pltpu.VMEM((D_MODEL, D_MODEL), F32),
            pltpu.VMEM((N_CHIPS, MLA_WIDTH, BRANCH_COLS), F32),
            pltpu.VMEM((N_CHIPS, POOL_WIDTH, BRANCH_COLS), F32),
        ],
        compiler_params=pltpu.CompilerParams(dimension_semantics=("arbitrary",), vmem_limit_bytes=VMEM_LIMIT),
    )(o, z_sh, z_sh, z_sh, z_sh, z_sh, x, target, pool_w, pool_scale, w_ba, w_bp, w_out, norm_final)


def _attn_bwd(q, q_t, k, v, do, lse, delta, late_grads, gs_mid, t):
    s = q.shape[0]
    groups = MLA_HEADS // ATT_HEADS
    n_q = s // t
    red = _Reduce(COMM_PARAMS[3:])
    n_w = len(red.params)
    small = _SmallSum(gs_mid.shape[0])
    n_red = len(red.scratch)

    def body(q_ref, qt_ref, do_ref, lse_ref, dl_ref, k_ref, v_ref, *rest):
        g_in, gs_ref = rest[:n_w], rest[n_w]
        (dq_ref, dk_ref, dv_ref), g_out, gsum_ref = rest[n_w + 1:n_w + 4], rest[n_w + 4:2 * n_w + 4], rest[2 * n_w + 4]
        scratch = rest[2 * n_w + 5:]
        red.bind(g_in, g_out, scratch[:n_red])
        small.bind(gs_ref, gsum_ref, scratch[n_red:])
        i = pl.program_id(1)
        step_no = pl.program_id(0) * n_q + i

        @pl.when(step_no == 0)
        def _():
            red.start()
            small.start()

        pl.when(step_no == n_q)(red.exchange)

        @pl.when(i == 0)
        def _():
            dk_ref[...] = jnp.zeros_like(dk_ref)
            dv_ref[...] = jnp.zeros_like(dv_ref)

        mask = _chunk_mask(t, False)
        qcs = [slice(hh * HEAD_PAD, (hh + 1) * HEAD_PAD) for hh in range(ATT_HEADS)]
        vcs = [slice(hh * V_HEAD_DIM, (hh + 1) * V_HEAD_DIM) for hh in range(ATT_HEADS)]
        qhs = [q_ref[:, qc] for qc in qcs]
        qts = [qt_ref[qc, :] for qc in qcs]
        dohs = [do_ref[:, vc].astype(BF16) for vc in vcs]
        do_t = do_ref[...].T.astype(BF16)
        dots = [do_t[vc, :] for vc in vcs]
        lses = [lse_ref[:, hh * HEAD_PAD:hh * HEAD_PAD + 1] for hh in range(ATT_HEADS)]
        dls = [dl_ref[:, hh * HEAD_PAD:hh * HEAD_PAD + 1] for hh in range(ATT_HEADS)]

        def step(j, dqs, masked):
            keys = pl.ds(pl.multiple_of(j * t, t), t)
            out = []
            for hh in range(ATT_HEADS):
                kj = k_ref[keys, qcs[hh]]
                vj = v_ref[keys, vcs[hh]]
                p = jnp.exp2(_dot_nt(qhs[hh], kj) * ATT_SCALE_LOG2E - lses[hh])
                if masked:
                    p = jnp.where(mask, p, 0.0)
                ds = (p * (_dot_nt(dohs[hh], vj) - dls[hh])).astype(BF16)
                dv_ref[vcs[hh], keys] += _dot(dots[hh], p.astype(BF16))
                dk_ref[qcs[hh], keys] += _dot(qts[hh], ds) * ATT_SCALE
                out.append(dqs[hh] + _dot(ds, kj))
            return tuple(out)

        zero = jnp.zeros((t, HEAD_PAD), F32)
        dqs = lax.fori_loop(0, i, functools.partial(step, masked=False), (zero,) * ATT_HEADS)
        dqs = step(i, dqs, True)
        for hh in range(ATT_HEADS):
            dq_ref[:, qcs[hh]] = dqs[hh] * ATT_SCALE

        @pl.when(step_no == groups * n_q - 1)
        def _():
            red.finish()
            small.finish()

    hw = MLA_HEADS * HEAD_PAD
    any_spec = pl.BlockSpec(memory_space=pl.ANY)
    out = pl.pallas_call(
        body,
        name="attn_bwd",
        grid=(groups, n_q),
        in_specs=[
            pl.BlockSpec((t, ATT_HEADS * HEAD_PAD), lambda p, i: (i, p)),
            pl.BlockSpec((ATT_HEADS * HEAD_PAD, t), lambda p, i: (p, i)),
            pl.BlockSpec((t, ATT_HEADS * V_HEAD_DIM), lambda p, i: (i, p)),
            pl.BlockSpec((t, ATT_HEADS * HEAD_PAD), lambda p, i: (i, p)),
            pl.BlockSpec((t, ATT_HEADS * HEAD_PAD), lambda p, i: (i, p)),
            pl.BlockSpec((s, ATT_HEADS * HEAD_PAD), lambda p, i: (0, p), pipeline_mode=pl.Buffered(1)),
            pl.BlockSpec((s, ATT_HEADS * V_HEAD_DIM), lambda p, i: (0, p), pipeline_mode=pl.Buffered(1)),
        ] + [any_spec] * n_w + [pl.BlockSpec(small.spec_shape, lambda p, i: (0, 0))],
        out_specs=[
            pl.BlockSpec((t, ATT_HEADS * HEAD_PAD), lambda p, i: (i, p)),
            pl.BlockSpec((ATT_HEADS * HEAD_PAD, s), lambda p, i: (p, 0)),
            pl.BlockSpec((ATT_HEADS * V_HEAD_DIM, s), lambda p, i: (p, 0)),
        ] + [any_spec] * n_w + [pl.BlockSpec(small.spec_shape, lambda p, i: (0, 0))],
        out_shape=[jax.ShapeDtypeStruct((s, hw), F32), jax.ShapeDtypeStruct((hw, s), F32),
                   jax.ShapeDtypeStruct((MLA_WIDTH, s), F32)] + red.out_shape + [small.out_shape],
        scratch_shapes=red.scratch + small.scratch,
        compiler_params=pltpu.CompilerParams(dimension_semantics=("arbitrary", "arbitrary"),
                                             vmem_limit_bytes=VMEM_LIMIT),
    )(q, q_t, do, lse, delta, k, v, *late_grads, gs_mid)
    return out[0], out[1], out[2], out[3:3 + n_w], out[3 + n_w]


def _qkv_bwd(dq, dk_t, dv_t, z_sh, q_norm, kv_norm, wuq_p, wk_p, wv, rc, rsa, rsb, tm):
    s = z_sh.shape[1]
    hw = MLA_HEADS * HEAD_PAD

    def body(dq_ref, dk_ref, dv_ref, z_ref, gq_ref, gkv_ref, wuq_ref, wk_ref, wv_ref,
             c_ref, sa_ref, sb_ref,
             dzq_ref, dzkv_ref, dzkr_ref, dwuq_ref, dwk_ref, dwv_ref, dgq_ref, dgkv_ref):
        i = pl.program_id(0)

        @pl.when(i == 0)
        def _():
            dwuq_ref[...] = jnp.zeros_like(dwuq_ref)
            dwk_ref[...] = jnp.zeros_like(dwk_ref)
            dwv_ref[...] = jnp.zeros_like(dwv_ref)
            dgq_ref[...] = jnp.zeros_like(dgq_ref)
            dgkv_ref[...] = jnp.zeros_like(dgkv_ref)

        c, sa, sb = c_ref[...], sa_ref[...], sb_ref[...]
        gq, gkv = gq_ref[...], gkv_ref[...]

        z0 = z_ref[0]
        cq, xq, rq = _rms_fwd(z0[:, ZQ_COLS], gq)
        dqp = jnp.concatenate(
            [_unrope(dq_ref[:, h * HEAD_PAD:(h + 1) * HEAD_PAD], c, sa, sb) for h in range(MLA_HEADS)],
            axis=1).astype(BF16)
        dwuq_ref[...] += _dot_tn(cq.astype(BF16), dqp)
        dcq = _dot_nt(dqp, wuq_ref[...])
        dgq_ref[...] += _colsum(dcq * xq)
        dzq_ref[...] = _rms_bwd(dcq, xq, rq, gq).astype(BF16)

        ckv, xkv, rkv = _rms_fwd(z0[:, ZKV_COLS], gkv)
        ckv = ckv.astype(BF16)
        dkf = dk_ref[...].T
        dk_bf = dkf.astype(BF16)
        dv_bf = dv_ref[...].T.astype(BF16)
        dwk_ref[...] += _dot_tn(ckv, dk_bf)
        dwv_ref[...] += _dot_tn(ckv, dv_bf)
        dckv = _dot_nt(dk_bf, wk_ref[...]) + _dot_nt(dv_bf, wv_ref[...])
        dgkv_ref[...] += _colsum(dckv * xkv)
        dzkv_ref[...] = _rms_bwd(dckv, xkv, rkv, gkv).astype(BF16)

        dkr = dkf[:, 0:HEAD_PAD]
        for h in range(1, MLA_HEADS):
            dkr = dkr + dkf[:, h * HEAD_PAD:(h + 1) * HEAD_PAD]
        dkr = pltpu.roll(_unrope(dkr, c, sa, sb), 64, 1)
        lane = lax.broadcasted_iota(jnp.int32, (tm, HEAD_PAD), 1)
        dzkr_ref[...] = jnp.where(lane < QK_ROPE_DIM, dkr, 0.0).astype(BF16)

    return pl.pallas_call(
        body,
        name="qkv_bwd",
        grid=(s // tm,),
        in_specs=[
            _row_spec(tm, hw), pl.BlockSpec((hw, tm), lambda i: (0, i)), pl.BlockSpec((MLA_WIDTH, tm), lambda i: (0, i)),
            pl.BlockSpec((1, tm, SHARD_COLS), lambda i: (0, i, 0)),
            _full_spec((1, Q_LORA_RANK)), _full_spec((1, KV_LORA_RANK)),
            _full_spec((Q_LORA_RANK, hw)), _full_spec((KV_LORA_RANK, hw)), _full_spec((KV_LORA_RANK, MLA_WIDTH)),
            _row_spec(tm, HEAD_PAD), _row_spec(tm, HEAD_PAD), _row_spec(tm, HEAD_PAD),
        ],
        out_specs=[
            _row_spec(tm, Q_LORA_RANK), _row_spec(tm, KV_LORA_RANK), _row_spec(tm, HEAD_PAD),
            _full_spec((Q_LORA_RANK, hw)), _full_spec((KV_LORA_RANK, hw)), _full_spec((KV_LORA_RANK, MLA_WIDTH)),
            _full_spec((1, Q_LORA_RANK)), _full_spec((1, KV_LORA_RANK)),
        ],
        out_shape=[
            jax.ShapeDtypeStruct((s, Q_LORA_RANK), BF16), jax.ShapeDtypeStruct((s, KV_LORA_RANK), BF16),
            jax.ShapeDtypeStruct((s, HEAD_PAD), BF16),
            jax.ShapeDtypeStruct((Q_LORA_RANK, hw), F32), jax.ShapeDtypeStruct((KV_LORA_RANK, hw), F32),
            jax.ShapeDtypeStruct((KV_LORA_RANK, MLA_WIDTH), F32),
            jax.ShapeDtypeStruct((1, Q_LORA_RANK), F32), jax.ShapeDtypeStruct((1, KV_LORA_RANK), F32),
        ],
        compiler_params=pltpu.CompilerParams(dimension_semantics=("arbitrary",), vmem_limit_bytes=VMEM_LIMIT),
    )(dq, dk_t, dv_t, z_sh, q_norm, kv_norm, wuq_p, wk_p, wv, rc, rsa, rsb)


def _inproj_bwd_x(dzq, dzkv, dzkr, dgattn, ddc, dgpool, dgmerge, x, dh, norm_in, w_in_t, tm):
    s = x.shape[0]
    n_tiles = s // tm
    halo_per_tile = tm // POOL_HALO
    n_halo = s // POOL_HALO
    u_seg = 4

    def body(dzq_ref, dzkv_ref, dzkr_ref, dga_ref, ddc_ref, ddn_ref, dgp_ref, dgm_ref, x_ref, dh_ref,
             g_ref, w_hbm, gx_ref, dgin_ref, dzs_ref, w_vmem, dbuf, sem):
        i = pl.program_id(0)

        @pl.when(i == 0)
        def _():
            cp = pltpu.make_async_copy(w_hbm, w_vmem, sem)
            cp.start()
            dgin_ref[...] = jnp.zeros_like(dgin_ref)
            cp.wait()

        dbuf[0:tm, :] = ddc_ref[...]
        dbuf[tm:, :] = jnp.where(i < n_tiles - 1, ddn_ref[...], 0.0)
        row = lax.broadcasted_iota(jnp.int32, (tm, POOL_GROUP_DIM), 0) + i * tm
        du = []
        for g, w in enumerate(POOL_WINDOWS):
            cols = slice(g * POOL_GROUP_DIM, (g + 1) * POOL_GROUP_DIM)
            fsum = dbuf[0:tm, cols]
            for kk in range(1, w):
                fsum = fsum + dbuf[kk:kk + tm, cols]
            du.append(fsum - dbuf[0:tm, cols] * jnp.minimum(row + 1, w).astype(F32))
        du = jnp.concatenate(du, axis=1).astype(BF16)

        dz = [dzq_ref[...], dzkv_ref[...], dzkr_ref[...], dga_ref[...], du, dgp_ref[...], dgm_ref[...]]
        dz = jnp.concatenate([d[:, :w] for d, (w, _) in zip(dz, IN_SEGMENTS)], axis=1)
        for j in range(N_CHIPS):
            dzs_ref[j] = dz[:, j * SHARD_COLS:(j + 1) * SHARD_COLS].T
        dhn = _dot(dz, w_vmem[...])

        g = g_ref[...]
        _, xhat, r = _rms_fwd(x_ref[...], g)
        dgin_ref[...] += _colsum(dhn * xhat)
        gx_ref[...] = dh_ref[...] + _rms_bwd(dhn, xhat, r, g)

    any_spec = pl.BlockSpec(memory_space=pl.ANY)
    seg_w = [wide for _, wide in IN_SEGMENTS]
    return pl.pallas_call(
        body,
        name="inproj_bwd_x",
        grid=(n_tiles,),
        in_specs=[
            _row_spec(tm, seg_w[0]), _row_spec(tm, seg_w[1]), _row_spec(tm, seg_w[2]),
            _row_spec(tm, seg_w[3]), _row_spec(tm, seg_w[u_seg]),
            pl.BlockSpec((POOL_HALO, POOL_WIDTH), lambda i: (jnp.minimum((i + 1) * halo_per_tile, n_halo - 1), 0)),
            _row_spec(tm, seg_w[5]), _row_spec(tm, seg_w[6]),
            _row_spec(tm, D_MODEL), _row_spec(tm, D_MODEL),
            _full_spec((1, D_MODEL)), any_spec,
        ],
        out_specs=[_row_spec(tm, D_MODEL), _full_spec((1, D_MODEL)),
                   pl.BlockSpec((N_CHIPS, SHARD_COLS, tm), lambda i: (0, 0, i))],
        out_shape=[jax.ShapeDtypeStruct((s, D_MODEL), F32), jax.ShapeDtypeStruct((1, D_MODEL), F32),
                   jax.ShapeDtypeStruct((N_CHIPS, SHARD_COLS, s), BF16)],
        scratch_shapes=[
            pltpu.VMEM((IN_TOTAL, D_MODEL), BF16),
            pltpu.VMEM((tm + POOL_HALO, POOL_WIDTH), F32),
            pltpu.SemaphoreType.DMA,
        ],
        compiler_params=pltpu.CompilerParams(dimension_semantics=("arbitrary",), vmem_limit_bytes=VMEM_LIMIT),
    )(dzq, dzkv, dzkr, dgattn, ddc, ddc, dgpool, dgmerge, x, dh, norm_in, w_in_t.reshape(IN_TOTAL, D_MODEL))


def _inproj_bwd_w(order, dz_sh, hn, g_uq, g_ukv, gs, tm):
    s = hn.shape[0]
    n_tiles = s // tm
    hc = D_MODEL // 2
    red = _Reduce(COMM_PARAMS[1:3])
    small = _SmallSum(gs.shape[0])
    n_red = len(red.scratch)

    def body(order_ref, dz_ref, hn_ref, guq_hbm, gukv_hbm, gs_ref, gw_hbm, guq_out, gukv_out, gsum_ref,
             acc, pm_w, a_w, b_w, r_w, w_send, w_recv, w_local, *more_scratch):
        ph, i = pl.program_id(0), pl.program_id(1)
        x, y, c = lax.axis_index("x"), lax.axis_index("y"), lax.axis_index("c")
        k = 2 * x + y
        me, sibling = (x, y, c), (x, y, 1 - c)
        chips = _other_chips(x, y)
        shard_of_phase = [2 * cx + cy for cx, cy in chips] + [k]
        copy = _remote_copier(w_send, w_recv)
        red.bind([guq_hbm, gukv_hbm], [guq_out, gukv_out], more_scratch[:n_red])
        small.bind(gs_ref, gsum_ref, more_scratch[n_red:])
        mine = pl.ds(pl.multiple_of(c * hc, hc), hc)
        theirs = pl.ds(pl.multiple_of((1 - c) * hc, hc), hc)

        def to_sibling(f):
            j = shard_of_phase[f]
            return copy(f, pm_w.at[j, 1 - c], a_w.at[j], sibling)

        def pair_sum(f):
            cx, cy = chips[f]
            return copy(4 + f, pm_w.at[shard_of_phase[f], c], b_w.at[f], (cx, cy, c))

        def finished():
            return copy(7, r_w, gw_hbm.at[:, mine], sibling)

        @pl.when(jnp.logical_and(ph == 0, i == 0))
        def _():
            red.start()
            small.start()

        part = _dot(dz_ref[0], hn_ref[...])

        @pl.when(i == 0)
        def _():
            acc[...] = part

        @pl.when(i > 0)
        def _():
            acc[...] += part

        for f in range(3):
            @pl.when(jnp.logical_and(ph == f + 1, i == 0))
            def _(f=f):
                j = shard_of_phase[f]
                copy(f, a_w.at[j], a_w.at[j], me).wait_recv()
                pm_w[j, c] = (pm_w[j, c].astype(F32) + a_w[j].astype(F32)).astype(BF16)
                pair_sum(f).start()
                if f == 0:
                    red.exchange()

        for f in range(4):
            @pl.when(jnp.logical_and(ph == f, i == n_tiles - 1))
            def _(f=f):
                j = shard_of_phase[f]
                pm_w[j, 0] = acc[:, :hc].astype(BF16)
                pm_w[j, 1] = acc[:, hc:].astype(BF16)
                to_sibling(f).start()
                if f < 3:
                    return
                copy(3, a_w.at[k], a_w.at[k], me).wait_recv()
                r_w[...] = pm_w[k, c].astype(F32) + a_w[k].astype(F32)
                for g in range(3):
                    copy(4 + g, b_w.at[g], b_w.at[g], me).wait_recv()
                    r_w[...] = r_w[...] + b_w[g].astype(F32)
                store = pltpu.make_async_copy(r_w, gw_hbm.at[:, mine], w_local)
                store.start()
                finished().start()
                red.finish()
                small.finish()
                copy(7, gw_hbm.at[:, theirs], gw_hbm.at[:, theirs], me).wait_recv()
                store.wait()
                for g in range(4):
                    to_sibling(g).wait_send()
                for g in range(3):
                    pair_sum(g).wait_send()
                finished().wait_send()

    any_spec = pl.BlockSpec(memory_space=pl.ANY)
    n_sem = 8
    grid_spec = pltpu.PrefetchScalarGridSpec(
        num_scalar_prefetch=1,
        grid=(N_CHIPS, n_tiles),
        in_specs=[
            pl.BlockSpec((1, SHARD_COLS, tm), lambda ph, i, order: (order[ph], 0, i)),
            pl.BlockSpec((tm, D_MODEL), lambda ph, i, order: (i, 0)),
            any_spec, any_spec,
            pl.BlockSpec(small.spec_shape, lambda ph, i, order: (0, 0)),
        ],
        out_specs=[any_spec, any_spec, any_spec, pl.BlockSpec(small.spec_shape, lambda ph, i, order: (0, 0))],
        scratch_shapes=[
            pltpu.VMEM((SHARD_COLS, D_MODEL), F32),
            pltpu.VMEM((N_CHIPS, 2, SHARD_COLS, hc), BF16),
            pltpu.VMEM((N_CHIPS, SHARD_COLS, hc), BF16),
            pltpu.VMEM((3, SHARD_COLS, hc), BF16),
            pltpu.VMEM((SHARD_COLS, hc), F32),
            pltpu.SemaphoreType.DMA((n_sem,)), pltpu.SemaphoreType.DMA((n_sem,)), pltpu.SemaphoreType.DMA,
        ] + red.scratch + small.scratch,
    )
    out = pl.pallas_call(
        body,
        name="inproj_bwd_w",
        grid_spec=grid_spec,
        out_shape=[jax.ShapeDtypeStruct((SHARD_COLS, D_MODEL), F32)] + red.out_shape
        + [small.out_shape],
        compiler_params=pltpu.CompilerParams(dimension_semantics=("arbitrary", "arbitrary"),
                                             vmem_limit_bytes=VMEM_LIMIT),
    )(order, dz_sh, hn, g_uq, g_ukv, gs)
    return out[0], out[1], out[2], out[3]


def _other_chips(x, y):
    return ((1 - x, 1 - y), (1 - x, y), (x, 1 - y))


def _half(ref, axis, size, c, lead=()):
    window = pl.ds(pl.multiple_of(c * size, size), size)
    if axis == 0:
        return ref.at[(*lead, window, slice(None))]
    return ref.at[(*lead, slice(None), window)]


def _half_shape(rows, cols, axis, size):
    return (size, cols) if axis == 0 else (rows, size)


def _remote_copier(send_sems, recv_sems):
    def copy(sem, src, dst, to):
        return pltpu.make_async_remote_copy(src_ref=src, dst_ref=dst, send_sem=send_sems.at[sem],
                                            recv_sem=recv_sems.at[sem], device_id=to, device_id_type=MESH)
    return copy


class _Gather:
    def __init__(self, params):
        self.params = params
        n = len(params)
        self.scratch = [pltpu.SemaphoreType.DMA((6 * n,)), pltpu.SemaphoreType.DMA((6 * n,)),
                        pltpu.SemaphoreType.DMA((n,))]
        self.out_shape = [jax.ShapeDtypeStruct((N_CHIPS, r, cc), BF16) for _, r, cc, _, _ in params]

    def bind(self, ins, outs, scratch):
        self.ins, self.outs = ins, outs
        send_sems, recv_sems, self.local_sems = scratch
        self.copy = _remote_copier(send_sems, recv_sems)
        self.x, self.y, self.c = lax.axis_index("x"), lax.axis_index("y"), lax.axis_index("c")
        self.k = 2 * self.x + self.y
        self.chips = _other_chips(self.x, self.y)

    def _local(self, p):
        return pltpu.make_async_copy(self.ins[p], self.outs[p].at[self.k], self.local_sems.at[p])

    def _first(self, p, j):
        _, _, _, axis, size = self.params[p]
        cx, cy = self.chips[j]
        return self.copy(6 * p + j, _half(self.ins[p], axis, size, self.c),
                         _half(self.outs[p], axis, size, self.c, (self.k,)), (cx, cy, self.c))

    def _relay(self, p, j, half_of):
        _, _, _, axis, size = self.params[p]
        cx, cy = self.chips[j]
        block = _half(self.outs[p], axis, size, half_of, (2 * cx + cy,))
        return self.copy(6 * p + 3 + j, block, block, (self.x, self.y, 1 - self.c))

    def start(self):
        for p in range(len(self.params)):
            self._local(p).start()
            for j in (1, 2, 0):
                self._first(p, j).start()

    def relay_one(self, p, j):
        _, _, _, axis, size = self.params[p]
        cx, cy = self.chips[j]
        landed = _half(self.outs[p], axis, size, self.c, (2 * cx + cy,))
        self.copy(6 * p + j, landed, landed, (self.x, self.y, self.c)).wait_recv()
        self._relay(p, j, self.c).start()

    def await_one(self, p, j):
        self._relay(p, j, 1 - self.c).wait_recv()

    def wait_sends(self):
        for p in range(len(self.params)):
            for j in range(3):
                self._first(p, j).wait_send()
                self._relay(p, j, self.c).wait_send()
            self._local(p).wait()

    def relay(self):
        for j in range(3):
            for p in range(len(self.params)):
                self.relay_one(p, j)

    def finish(self):
        for j in range(3):
            for p in range(len(self.params)):
                self.await_one(p, j)
        self.wait_sends()


class _Reduce:
    def __init__(self, params):
        self.params = params
        n = len(params)
        halves = [_half_shape(r, cc, axis, size) for _, r, cc, axis, size in params]
        self.scratch = ([pltpu.VMEM((N_CHIPS, *h), BF16) for h in halves]
                        + [pltpu.VMEM((N_CHIPS, *h), BF16) for h in halves]
                        + [pltpu.VMEM((3, *h), BF16) for h in halves]
                        + [pltpu.VMEM(h, F32) for h in halves]
                        + [pltpu.SemaphoreType.DMA((5 * n,)), pltpu.SemaphoreType.DMA((5 * n,)),
                           pltpu.SemaphoreType.DMA((2 * n,))])
        self.out_shape = [jax.ShapeDtypeStruct((r, cc), F32) for _, r, cc, _, _ in params]

    def bind(self, g_in, g_out, scratch):
        n = len(self.params)
        self.g_in, self.g_out = g_in, g_out
        self.pm, self.a_buf = scratch[0:n], scratch[n:2 * n]
        self.b_buf, self.r_buf = scratch[2 * n:3 * n], scratch[3 * n:4 * n]
        send_sems, recv_sems, self.local_sems = scratch[4 * n:]
        self.copy = _remote_copier(send_sems, recv_sems)
        self.x, self.y, self.c = lax.axis_index("x"), lax.axis_index("y"), lax.axis_index("c")
        self.k = 2 * self.x + self.y
        self.chips = _other_chips(self.x, self.y)
        self.me = (self.x, self.y, self.c)
        self.sibling = (self.x, self.y, 1 - self.c)

    def _load(self, p):
        _, _, _, axis, size = self.params[p]
        return pltpu.make_async_copy(_half(self.g_in[p], axis, size, self.c, (slice(None),)), self.pm[p],
                                     self.local_sems.at[p])

    def _to_sibling(self, p):
        _, _, _, axis, size = self.params[p]
        return self.copy(5 * p, _half(self.g_in[p], axis, size, 1 - self.c, (slice(None),)), self.a_buf[p],
                         self.sibling)

    def _pair_sum(self, p, j):
        cx, cy = self.chips[j]
        return self.copy(5 * p + 1 + j, self.pm[p].at[2 * cx + cy], self.b_buf[p].at[j], (cx, cy, self.c))

    def _store(self, p):
        _, _, _, axis, size = self.params[p]
        n = len(self.params)
        return pltpu.make_async_copy(self.r_buf[p], _half(self.g_out[p], axis, size, self.c),
                                     self.local_sems.at[n + p])

    def _finished(self, p):
        _, _, _, axis, size = self.params[p]
        return self.copy(5 * p + 4, self.r_buf[p], _half(self.g_out[p], axis, size, self.c), self.sibling)

    def start(self):
        for p in range(len(self.params)):
            self._load(p).start()
            self._to_sibling(p).start()

    def exchange(self):
        for p in range(len(self.params)):
            self._load(p).wait()
            self.copy(5 * p, self.a_buf[p], self.a_buf[p], self.me).wait_recv()
            for j, (cx, cy) in enumerate(self.chips):
                kj = 2 * cx + cy
                self.pm[p][kj] = (self.pm[p][kj].astype(F32) + self.a_buf[p][kj].astype(F32)).astype(BF16)
                self._pair_sum(p, j).start()
            self.r_buf[p][...] = self.pm[p][self.k].astype(F32) + self.a_buf[p][self.k].astype(F32)

    def finish(self):
        for p, (_, _, _, axis, size) in enumerate(self.params):
            for j in range(3):
                self.copy(5 * p + 1 + j, self.b_buf[p].at[j], self.b_buf[p].at[j], self.me).wait_recv()
                self.r_buf[p][...] = self.r_buf[p][...] + self.b_buf[p][j].astype(F32)
            self._store(p).start()
            self._finished(p).start()
        for p, (_, _, _, axis, size) in enumerate(self.params):
            theirs = _half(self.g_out[p], axis, size, 1 - self.c)
            self.copy(5 * p + 4, theirs, theirs, self.me).wait_recv()
            self._store(p).wait()
            self._to_sibling(p).wait_send()
            for j in range(3):
                self._pair_sum(p, j).wait_send()
            self._finished(p).wait_send()


class _SmallSum:
    def __init__(self, rows):
        self.rows = rows
        self.scratch = [pltpu.VMEM((N_DEV, rows, LANES), F32),
                        pltpu.SemaphoreType.DMA((N_DEV - 1,)), pltpu.SemaphoreType.DMA((N_DEV - 1,))]
        self.out_shape = jax.ShapeDtypeStruct((rows, LANES), F32)
        self.spec_shape = (rows, LANES)

    def bind(self, src, dst, scratch):
        self.src, self.dst = src, dst
        self.buf, send_sems, recv_sems = scratch
        self.copy = _remote_copier(send_sems, recv_sems)
        self.x, self.y, self.c = lax.axis_index("x"), lax.axis_index("y"), lax.axis_index("c")

    def _send(self, f):
        fx, fy, fc = [(a, b, d) for a in (0, 1) for b in (0, 1) for d in (0, 1)][f]
        x, y, c = self.x, self.y, self.c
        peer = (1 - x if fx else x, 1 - y if fy else y, 1 - c if fc else c)
        return self.copy(f - 1, self.src, self.buf.at[f], peer)

    def start(self):
        for f in range(1, N_DEV):
            self._send(f).start()
        self.buf[0] = self.src[...]

    def finish(self):
        me = (self.x, self.y, self.c)
        for f in range(1, N_DEV):
            self.copy(f - 1, self.buf.at[f], self.buf.at[f], me).wait_recv()
        dev = 4 * self.x + 2 * self.y + self.c
        total = self.buf[dev]
        for d in range(1, N_DEV):
            total = total + self.buf[jnp.bitwise_xor(dev, d)]
        self.dst[...] = total
        for f in range(1, N_DEV):
            self._send(f).wait_send()


def _adamw_math(w, g, m, v):
    m = ADAM_B1 * m + (1.0 - ADAM_B1) * g
    v = ADAM_B2 * v + (1.0 - ADAM_B2) * (g * g)
    m_hat = m / (1.0 - ADAM_B1 ** ADAM_STEP)
    v_hat = v / (1.0 - ADAM_B2 ** ADAM_STEP)
    delta = -ADAM_LR * (m_hat / (jnp.sqrt(v_hat) + ADAM_EPS) + ADAM_WD * w)
    return delta, m, v


def _adamw_tiled(w, g, m, v, tm):
    rows, cols = w.shape

    def body(w_ref, g_ref, m_ref, v_ref, d_ref, nm_ref, nv_ref):
        d_ref[...], nm_ref[...], nv_ref[...] = _adamw_math(w_ref[...], g_ref[...], m_ref[...], v_ref[...])

    spec = _row_spec(tm, cols)
    return pl.pallas_call(
        body,
        name="adamw_w_in",
        grid=(rows // tm,),
        in_specs=[spec] * 4,
        out_specs=[spec] * 3,
        out_shape=[jax.ShapeDtypeStruct(w.shape, F32)] * 3,
        compiler_params=pltpu.CompilerParams(dimension_semantics=("parallel",), vmem_limit_bytes=VMEM_LIMIT),
    )(w, g, m, v)


def _adamw_many(ws, gs, ms, vs):
    n = len(ws)

    def body(*refs):
        ins, outs = refs[:4 * n], refs[4 * n:]
        for i in range(n):
            d, nm, nv = _adamw_math(ins[i][...], ins[n + i][...], ins[2 * n + i][...], ins[3 * n + i][...])
            outs[i][...] = d
            outs[n + i][...] = nm
            outs[2 * n + i][...] = nv

    vmem_spec = pl.BlockSpec(memory_space=pltpu.VMEM)
    shapes = [jax.ShapeDtypeStruct(w.shape, F32) for w in ws]
    out = pl.pallas_call(
        body,
        name="adamw_small",
        in_specs=[vmem_spec] * (4 * n),
        out_specs=[vmem_spec] * (3 * n),
        out_shape=shapes * 3,
        compiler_params=pltpu.CompilerParams(vmem_limit_bytes=VMEM_LIMIT),
    )(*ws, *gs, *ms, *vs)
    return out[:n], out[n:2 * n], out[2 * n:]


def _pack_rows(parts, rows, dtype):
    flat = jnp.concatenate([p.reshape(-1).astype(dtype) for p in parts])
    flat = jnp.concatenate([flat, jnp.zeros((rows * LANES - flat.shape[0],), dtype)])
    return flat.reshape(rows, LANES)


def _unpack_rows(packed, shapes):
    flat = packed.reshape(-1)
    out, off = [], 0
    for _, shp in shapes:
        n = int(np.prod(shp))
        out.append(flat[off:off + n].reshape(shp))
        off += n
    return out


def _rope_tables(s):
    half = QK_ROPE_DIM // 2
    inv_freq = np.float32(ROPE_THETA) ** (-np.arange(half, dtype=np.float32) / np.float32(half))
    ang = (np.arange(s, dtype=np.float32)[:, None] * inv_freq[None, :]).astype(np.float32)
    cos, sin = np.cos(ang.astype(np.float64)).astype(np.float32), np.sin(ang.astype(np.float64)).astype(np.float32)
    z16 = np.zeros((s, half), np.float32)
    z32 = np.zeros((s, HEAD_PAD - QK_NOPE_DIM - QK_ROPE_DIM), np.float32)
    z64 = np.zeros((s, QK_NOPE_DIM), np.float32)
    rc = np.concatenate([np.ones((s, QK_NOPE_DIM), np.float32), cos, cos, z32], axis=1)
    rsa = np.concatenate([z64, -sin, z16, z32], axis=1)
    rsb = np.concatenate([z64, z16, sin, z32], axis=1)
    return jnp.asarray(rc), jnp.asarray(rsa), jnp.asarray(rsb)


def kernel(x, norm_in, w_in, q_norm, w_uq, kv_norm, w_ukv, pool_w, pool_scale, w_branch_attn, w_branch_pool, w_out, norm_final, loss_target, m_norm_in, m_w_in, m_q_norm, m_w_uq, m_kv_norm, m_w_ukv, m_pool_w, m_pool_scale, m_w_branch_attn, m_w_branch_pool, m_w_out, m_norm_final, v_norm_in, v_w_in, v_q_norm, v_w_uq, v_kv_norm, v_w_ukv, v_pool_w, v_pool_scale, v_w_branch_attn, v_w_branch_pool, v_w_out, v_norm_final):
    s = x.shape[1]
    t_att, t_row = _tiles(s)
    x2 = x.reshape(s, D_MODEL)
    tgt = loss_target.reshape(s, D_MODEL)

    local = [w_in.T, w_uq.reshape(96, 768), w_ukv.reshape(64, 1024), w_branch_attn, w_branch_pool, w_out]
    local = [a.astype(BF16) for a in local]
    cx, cy = lax.axis_index("x"), lax.axis_index("y")
    others = [2 * ox + oy for ox, oy in _other_chips(cx, cy)]
    hn, z_sh, (w_in_t, w_uq_all, w_ukv_all) = _inproj_fwd(
        jnp.stack([2 * cx + cy, others[1], others[2], others[0]]).astype(jnp.int32), x2, norm_in.reshape(1, -1),
        local[:3], 4 * t_row)
    w_uq_f = w_uq_all.reshape(Q_LORA_RANK, MLA_HEADS, QK_NOPE_DIM + QK_ROPE_DIM)
    w_ukv_f = w_ukv_all.reshape(KV_LORA_RANK, MLA_HEADS, QK_NOPE_DIM + V_HEAD_DIM)
    hw = MLA_HEADS * HEAD_PAD
    wuq_p = jnp.pad(w_uq_f, ((0, 0), (0, 0), (0, HEAD_PAD - QK_NOPE_DIM - QK_ROPE_DIM))).reshape(Q_LORA_RANK, hw)
    wk_p = jnp.pad(w_ukv_f[:, :, :QK_NOPE_DIM], ((0, 0), (0, 0), (0, HEAD_PAD - QK_NOPE_DIM))).reshape(KV_LORA_RANK, hw)
    wv = w_ukv_f[:, :, QK_NOPE_DIM:].reshape(KV_LORA_RANK, MLA_WIDTH)
    rc, rsa, rsb = _rope_tables(s)
    g_in = norm_in.reshape(1, -1)
    g_q = q_norm.reshape(1, -1)
    g_kv = kv_norm.reshape(1, -1)
    g_f = norm_final.reshape(1, -1)
    ps = pool_scale.reshape(1, -1)
    pw_bf = pool_w.astype(BF16)

    q, k, v, q_t, v_t = _qkv_fwd(z_sh, g_q, g_kv, wuq_p, wk_p, wv, rc, rsa, rsb, t_row)
    o, lse, (w_ba_all, w_bp_all, w_out_all) = _attn_fwd(q_t, k, v_t, local[3:], t_att)
    w_out_f = w_out_all.reshape(D_MODEL, D_MODEL)

    (do, delta, dgattn, dgpool, dgmerge, ddc, dh, sq_err, d_w_out, d_w_ba, d_w_bp, d_pool_w, d_pool_scale,
     d_norm_final) = _mid(o, z_sh, x2, tgt, pw_bf, ps, w_ba_all, w_bp_all, w_out_f, g_f, t_row)

    late_grads = [d_w_ba, d_w_bp, d_w_out.reshape(N_CHIPS, 256, D_MODEL)]
    small_mid = dict(pool_scale=d_pool_scale, norm_final=d_norm_final, pool_w=d_pool_w, sq_err=sq_err)
    gs_mid = _pack_rows([small_mid[n] for n, _ in SMALL_MID], _small_rows(SMALL_MID), F32)
    dq, dk_t, dv_t, (g_w_ba, g_w_bp, g_w_out), g_small_mid = _attn_bwd(q, q_t, k, v, do, lse, delta, late_grads,
                                                                      gs_mid, t_att)
    g_pool_scale, g_norm_final, g_pool_w, sq_err_all = _unpack_rows(g_small_mid, SMALL_MID)
    dzq, dzkv, dzkr, d_wuq_p, d_wk_p, d_wv, d_q_norm, d_kv_norm = _qkv_bwd(
        dq, dk_t, dv_t, z_sh, g_q, g_kv, wuq_p, wk_p, wv, rc, rsa, rsb, t_row)
    grad_x, d_norm_in, dz_sh = _inproj_bwd_x(dzq, dzkv, dzkr, dgattn, ddc, dgpool, dgmerge, x2, dh, g_in, w_in_t,
                                             2 * t_row)

    d_w_uq = d_wuq_p.reshape(Q_LORA_RANK, MLA_HEADS, HEAD_PAD)[:, :, :QK_NOPE_DIM + QK_ROPE_DIM]
    d_w_ukv = jnp.concatenate([d_wk_p.reshape(KV_LORA_RANK, MLA_HEADS, HEAD_PAD)[:, :, :QK_NOPE_DIM],
                               d_wv.reshape(KV_LORA_RANK, MLA_HEADS, V_HEAD_DIM)], axis=2)
    small_late = dict(norm_in=d_norm_in, q_norm=d_q_norm, kv_norm=d_kv_norm)
    gs = _pack_rows([small_late[n] for n, _ in SMALL_LATE], _small_rows(SMALL_LATE), F32)
    order = jnp.stack(others + [2 * cx + cy]).astype(jnp.int32)
    g_w_in_t, g_w_uq, g_w_ukv, g_small = _inproj_bwd_w(
        order, dz_sh, hn, d_w_uq.reshape(N_CHIPS, 96, 768).astype(BF16),
        d_w_ukv.reshape(N_CHIPS, 64, 1024).astype(BF16), gs, 4 * t_row)
    g_norm_in, g_q_norm, g_kv_norm = _unpack_rows(g_small, SMALL_LATE)
    g_w_uq = g_w_uq.reshape(w_uq.shape)
    g_w_ukv = g_w_ukv.reshape(w_ukv.shape)

    dl_w_in, nm_w_in, nv_w_in = (a.T for a in _adamw_tiled(w_in.T, g_w_in_t, m_w_in.T, v_w_in.T, 152))

    def two_d(a):
        return a.reshape(1, -1) if a.ndim == 1 else a

    names = ["norm_in", "q_norm", "w_uq", "kv_norm", "w_ukv", "pool_w", "pool_scale", "w_branch_attn",
             "w_branch_pool", "w_out", "norm_final"]
    ws = dict(norm_in=norm_in, q_norm=q_norm, w_uq=w_uq, kv_norm=kv_norm, w_ukv=w_ukv, pool_w=pool_w,
              pool_scale=pool_scale, w_branch_attn=w_branch_attn, w_branch_pool=w_branch_pool, w_out=w_out,
              norm_final=norm_final)
    gsd = dict(norm_in=g_norm_in, q_norm=g_q_norm, w_uq=g_w_uq, kv_norm=g_kv_norm, w_ukv=g_w_ukv, pool_w=g_pool_w,
               pool_scale=g_pool_scale, w_branch_attn=g_w_ba, w_branch_pool=g_w_bp, w_out=g_w_out,
               norm_final=g_norm_final)
    msd = dict(norm_in=m_norm_in, q_norm=m_q_norm, w_uq=m_w_uq, kv_norm=m_kv_norm, w_ukv=m_w_ukv, pool_w=m_pool_w,
               pool_scale=m_pool_scale, w_branch_attn=m_w_branch_attn, w_branch_pool=m_w_branch_pool, w_out=m_w_out,
               norm_final=m_norm_final)
    vsd = dict(norm_in=v_norm_in, q_norm=v_q_norm, w_uq=v_w_uq, kv_norm=v_kv_norm, w_ukv=v_w_ukv, pool_w=v_pool_w,
               pool_scale=v_pool_scale, w_branch_attn=v_w_branch_attn, w_branch_pool=v_w_branch_pool, w_out=v_w_out,
               norm_final=v_norm_final)
    dls, nms, nvs = _adamw_many([two_d(ws[n]) for n in names], [two_d(gsd[n]) for n in names],
                                [two_d(msd[n]) for n in names], [two_d(vsd[n]) for n in names])

    grads = dict(gsd)
    grads["w_in"] = g_w_in_t.T
    delta_w = {n: d.reshape(ws[n].shape) for n, d in zip(names, dls)}
    new_m = {n: d.reshape(ws[n].shape) for n, d in zip(names, nms)}
    new_v = {n: d.reshape(ws[n].shape) for n, d in zip(names, nvs)}
    delta_w["w_in"], new_m["w_in"], new_v["w_in"] = dl_w_in, nm_w_in, nv_w_in
    ws["w_in"] = w_in

    order = ["norm_in", "w_in", "q_norm", "w_uq", "kv_norm", "w_ukv", "pool_w", "pool_scale", "w_branch_attn",
             "w_branch_pool", "w_out", "norm_final"]
    loss = 0.5 * jnp.sum(sq_err_all) / D_MODEL
    return (loss, grad_x.reshape(x.shape),
            *[grads[n].reshape(ws[n].shape) for n in order],
            *[delta_w[n] for n in order], *[new_m[n] for n in order], *[new_v[n] for n in order])
```

```python
import functools

import jax
import jax.numpy as jnp
import numpy as np
from jax import lax
from jax.experimental import pallas as pl
from jax.experimental.pallas import tpu as pltpu

F32 = jnp.float32
BF16 = jnp.bfloat16
MESH = pl.DeviceIdType.MESH

D_MODEL = 1024
CHUNK = 64
MLA_HEADS = 8
QK_NOPE_DIM = 64
QK_ROPE_DIM = 32
V_HEAD_DIM = 64
Q_LORA_RANK = 384
KV_LORA_RANK = 256
MLA_WIDTH = MLA_HEADS * V_HEAD_DIM
ROPE_THETA = 10000.0
POOL_WINDOWS = (2, 4, 8, 16)
POOL_WIDTH = 512
POOL_GROUP_DIM = 128
BRANCH_COLS = D_MODEL // 4
ATT_HEADS = 4
POOL_HALO = 16
EPS = 1e-6
IN_TOTAL = 4256
HEAD_PAD = 128
ATT_SCALE = (QK_NOPE_DIM + QK_ROPE_DIM) ** -0.5
ATT_SCALE_LOG2E = ATT_SCALE * 1.4426950408889634

ADAM_LR = 0.001
ADAM_B1 = 0.9
ADAM_B2 = 0.999
ADAM_EPS = 1e-08
ADAM_WD = 0.01
ADAM_STEP = 10

N_CHIPS = 4
N_DEV = 8
LANES = 128
VMEM_LIMIT = 60 * 1024 * 1024

IN_SEGMENTS = ((384, 384), (256, 256), (32, HEAD_PAD), (512, 512), (512, 512), (512, 512), (2048, 2048))
SHARD_COLS = IN_TOTAL // N_CHIPS
ZQ_COLS = slice(0, 384)
ZKV_COLS = slice(384, 640)
ZKR_TILE = slice(640, 768)


def _shard_pieces():
    bounds, off = [], 0
    for w, _ in IN_SEGMENTS:
        bounds.append((off, off + w))
        off += w
    out = []
    for j in range(N_CHIPS):
        lo, hi = SHARD_COLS * j, SHARD_COLS * (j + 1)
        out.append([(i, max(lo, a) - a, min(hi, b) - a, max(lo, a) - lo)
                    for i, (a, b) in enumerate(bounds) if max(lo, a) < min(hi, b)])
    return out


SHARD_PIECES = _shard_pieces()


def _segment(z_blocks, seg):
    parts = [z_blocks[j][:, col:col + hi - lo]
             for j, pieces in enumerate(SHARD_PIECES) for sg, lo, hi, col in pieces if sg == seg]
    return parts[0] if len(parts) == 1 else jnp.concatenate(parts, axis=1)

COMM_PARAMS = (
    ("w_in", SHARD_COLS, D_MODEL, 1, 512),
    ("w_uq", 96, 768, 0, 48),
    ("w_ukv", 64, 1024, 0, 32),
    ("w_branch_attn", 512, 256, 0, 256),
    ("w_branch_pool", 512, 256, 0, 256),
    ("w_out", 256, 1024, 0, 128),
)

SMALL_MID = (
    ("pool_scale", (512,)),
    ("norm_final", (1024,)),
    ("pool_w", (4, 128, 128)),
    ("sq_err", (8, 128)),
)
SMALL_LATE = (
    ("norm_in", (1024,)),
    ("q_norm", (384,)),
    ("kv_norm", (256,)),
)


def _small_rows(shapes):
    return -(-sum(int(np.prod(s)) for _, s in shapes) // (LANES * 8)) * 8


def _dot(a, b):
    return jnp.dot(a, b, preferred_element_type=F32)


def _dot_nt(a, b):
    return lax.dot_general(a, b, (((1,), (1,)), ((), ())), preferred_element_type=F32)


def _dot_tn(a, b):
    return lax.dot_general(a, b, (((0,), (0,)), ((), ())), preferred_element_type=F32)


def _sigmoid(x):
    return 1.0 / (1.0 + jnp.exp(-x))


def _colsum(x):
    return jnp.sum(x, axis=0, keepdims=True)


def _rms_fwd(x, g):
    r = lax.rsqrt(jnp.mean(x * x, axis=-1, keepdims=True) + EPS)
    xhat = x * r
    return xhat * g, xhat, r


def _rms_bwd(dy, xhat, r, g):
    dxhat = dy * g
    return r * (dxhat - xhat * jnp.mean(dxhat * xhat, axis=-1, keepdims=True))


def _rope(v, c, sa, sb):
    return v * c + pltpu.roll(v, 112, 1) * sa + pltpu.roll(v, 16, 1) * sb


def _unrope(d, c, sa, sb):
    return d * c + pltpu.roll(d * sa, 16, 1) + pltpu.roll(d * sb, 112, 1)


def _row_spec(tm, n):
    return pl.BlockSpec((tm, n), lambda i: (i, 0))


def _full_spec(shape):
    nd = len(shape)
    return pl.BlockSpec(shape, lambda i: (0,) * nd)


def _tiles(s):
    t_att = 512 if s >= 2048 else 128
    t_row = 256 if s >= 1024 else 128
    return t_att, t_row


def _inproj_fwd(order, x, norm_in, early_shards, tm):
    s = x.shape[0]
    n_tiles = s // tm
    gat = _Gather(COMM_PARAMS[:3])
    n_w = len(gat.params)
    arrival = (1, 2, 0)

    def body(order_ref, x_ref, g_ref, *rest):
        w_loc, (hn_ref, z_ref), w_all = rest[:n_w], rest[n_w:n_w + 2], rest[n_w + 2:2 * n_w + 2]
        w_vmem, hn_all, w_sem = rest[2 * n_w + 2:2 * n_w + 5]
        gat.bind(w_loc, w_all, rest[2 * n_w + 5:])
        ph, i = pl.program_id(0), pl.program_id(1)
        pl.when(jnp.logical_and(ph == 0, i == 0))(gat.start)

        @pl.when(jnp.logical_and(ph == 0, i == 0))
        def _():
            cp = pltpu.make_async_copy(w_loc[0], w_vmem, w_sem)
            cp.start()
            cp.wait()

        for f in range(3):
            @pl.when(jnp.logical_and(ph == f + 1, i == 0))
            def _(f=f):
                gat.relay_one(0, arrival[f])
                gat.await_one(0, arrival[f])
                cp = pltpu.make_async_copy(w_all[0].at[order_ref[ph]], w_vmem, w_sem)
                cp.start()
                cp.wait()

        rows = pl.ds(pl.multiple_of(i * tm, tm), tm)

        @pl.when(ph == 0)
        def _():
            hn, _, _ = _rms_fwd(x_ref[...], g_ref[...])
            hn = hn.astype(BF16)
            hn_ref[...] = hn
            hn_all[rows, :] = hn

        z_ref[0] = _dot_nt(hn_all[rows, :], w_vmem[...])

        @pl.when(jnp.logical_and(ph == N_CHIPS - 1, i == n_tiles - 1))
        def _():
            for p in range(1, n_w):
                for j in range(3):
                    gat.relay_one(p, j)
            for p in range(1, n_w):
                for j in range(3):
                    gat.await_one(p, j)
            gat.wait_sends()

    def tile_in_phase0(ph, i, order):
        return (jnp.where(ph == 0, i, n_tiles - 1), 0)

    any_spec = pl.BlockSpec(memory_space=pl.ANY)
    grid_spec = pltpu.PrefetchScalarGridSpec(
        num_scalar_prefetch=1,
        grid=(N_CHIPS, n_tiles),
        in_specs=[pl.BlockSpec((tm, D_MODEL), tile_in_phase0),
                  pl.BlockSpec((1, D_MODEL), lambda ph, i, order: (0, 0))] + [any_spec] * n_w,
        out_specs=[pl.BlockSpec((tm, D_MODEL), tile_in_phase0),
                   pl.BlockSpec((1, tm, SHARD_COLS), lambda ph, i, order: (order[ph], i, 0))] + [any_spec] * n_w,
        scratch_shapes=[pltpu.VMEM((SHARD_COLS, D_MODEL), BF16), pltpu.VMEM((s, D_MODEL), BF16),
                        pltpu.SemaphoreType.DMA] + gat.scratch,
    )
    out = pl.pallas_call(
        body,
        name="inproj_fwd",
        grid_spec=grid_spec,
        out_shape=[jax.ShapeDtypeStruct((s, D_MODEL), BF16), jax.ShapeDtypeStruct((N_CHIPS, s, SHARD_COLS), F32)]
        + gat.out_shape,
        compiler_params=pltpu.CompilerParams(dimension_semantics=("arbitrary", "arbitrary"),
                                             vmem_limit_bytes=VMEM_LIMIT),
    )(order, x, norm_in, *early_shards)
    return out[0], out[1], out[2:]


def _qkv_fwd(z_sh, q_norm, kv_norm, wuq_p, wk_p, wv, rc, rsa, rsb, tm):
    s = z_sh.shape[1]
    hw = MLA_HEADS * HEAD_PAD

    def body(z_ref, gq_ref, gkv_ref, wuq_ref, wk_ref, wv_ref, c_ref, sa_ref, sb_ref,
             q_ref, k_ref, v_ref, qt_ref, vt_ref):
        c, sa, sb = c_ref[...], sa_ref[...], sb_ref[...]
        z0 = z_ref[0]
        cq, _, _ = _rms_fwd(z0[:, ZQ_COLS], gq_ref[...])
        qf = _dot(cq.astype(BF16), wuq_ref[...])
        ckv, _, _ = _rms_fwd(z0[:, ZKV_COLS], gkv_ref[...])
        ckv = ckv.astype(BF16)
        kn = _dot(ckv, wk_ref[...])
        lane = lax.broadcasted_iota(jnp.int32, (tm, HEAD_PAD), 1)
        zkr = jnp.where(lane < QK_ROPE_DIM, z0[:, ZKR_TILE], 0.0)
        kr = _rope(pltpu.roll(zkr, 64, 1), c, sa, sb)
        for h in range(MLA_HEADS):
            cols = slice(h * HEAD_PAD, (h + 1) * HEAD_PAD)
            qh = _rope(qf[:, cols], c, sa, sb)
            q_ref[:, cols] = qh.astype(BF16)
            qt_ref[cols, :] = qh.T.astype(BF16)
            k_ref[:, cols] = (kn[:, cols] + kr).astype(BF16)
        vf = _dot(ckv, wv_ref[...])
        v_ref[...] = vf.astype(BF16)
        vt_ref[...] = vf.T.astype(BF16)

    return pl.pallas_call(
        body,
        name="qkv_fwd",
        grid=(s // tm,),
        in_specs=[
            pl.BlockSpec((1, tm, SHARD_COLS), lambda i: (0, i, 0)),
            _full_spec((1, Q_LORA_RANK)), _full_spec((1, KV_LORA_RANK)),
            _full_spec((Q_LORA_RANK, hw)), _full_spec((KV_LORA_RANK, hw)), _full_spec((KV_LORA_RANK, MLA_WIDTH)),
            _row_spec(tm, HEAD_PAD), _row_spec(tm, HEAD_PAD), _row_spec(tm, HEAD_PAD),
        ],
        out_specs=[_row_spec(tm, hw), _row_spec(tm, hw), _row_spec(tm, MLA_WIDTH),
                   pl.BlockSpec((hw, tm), lambda i: (0, i)), pl.BlockSpec((MLA_WIDTH, tm), lambda i: (0, i))],
        out_shape=[jax.ShapeDtypeStruct((s, hw), BF16), jax.ShapeDtypeStruct((s, hw), BF16),
                   jax.ShapeDtypeStruct((s, MLA_WIDTH), BF16),
                   jax.ShapeDtypeStruct((hw, s), BF16), jax.ShapeDtypeStruct((MLA_WIDTH, s), BF16)],
        compiler_params=pltpu.CompilerParams(dimension_semantics=("parallel",), vmem_limit_bytes=VMEM_LIMIT),
    )(z_sh, q_norm, kv_norm, wuq_p, wk_p, wv, rc, rsa, rsb)


def _chunk_mask(t, keys_on_rows):
    rows = lax.broadcasted_iota(jnp.int32, (t, t), 0) // CHUNK
    cols = lax.broadcasted_iota(jnp.int32, (t, t), 1) // CHUNK
    return rows <= cols if keys_on_rows else cols <= rows


def _attn_fwd(q_t, k, v_t, late_shards, t):
    s = k.shape[0]
    groups = MLA_HEADS // ATT_HEADS
    n_q = s // t
    gat = _Gather(COMM_PARAMS[3:])
    n_w = len(gat.params)

    def body(qt_ref, k_ref, k2_ref, vt_ref, *rest):
        w_in, (o_ref, lse_ref), w_out = rest[:n_w], rest[n_w:n_w + 2], rest[n_w + 2:2 * n_w + 2]
        gat.bind(w_in, w_out, rest[2 * n_w + 2:])
        i = pl.program_id(1)
        step_no = pl.program_id(0) * n_q + i
        pl.when(step_no == 0)(gat.start)
        pl.when(step_no == n_q)(gat.relay)
        mask = _chunk_mask(t, True)
        qcs = [slice(hh * HEAD_PAD, (hh + 1) * HEAD_PAD) for hh in range(ATT_HEADS)]
        vcs = [slice(hh * V_HEAD_DIM, (hh + 1) * V_HEAD_DIM) for hh in range(ATT_HEADS)]
        qts = [qt_ref[qc, :] for qc in qcs]

        def step(j, carry, masked):
            keys = pl.ds(pl.multiple_of(j * t, t), t)
            out = []
            for hh in range(ATT_HEADS):
                m, l, acc = carry[hh]
                sc = _dot(k_ref[keys, qcs[hh]], qts[hh])
                if masked:
                    sc = jnp.where(mask, sc, -jnp.inf)
                m_new = jnp.maximum(m, jnp.max(sc, axis=0, keepdims=True))
                alpha = jnp.exp2((m - m_new) * ATT_SCALE_LOG2E)
                p = jnp.exp2((_dot(k2_ref[keys, qcs[hh]], qts[hh]) - m_new) * ATT_SCALE_LOG2E)
                if masked:
                    p = jnp.where(mask, p, 0.0)
                l = alpha * l + jnp.sum(p, axis=0, keepdims=True)
                acc = alpha * acc + _dot(vt_ref[vcs[hh], keys], p.astype(BF16))
                out.append((m_new, l, acc))
            return tuple(out)

        one = (jnp.full((1, t), -jnp.inf, F32), jnp.zeros((1, t), F32), jnp.zeros((V_HEAD_DIM, t), F32))
        carry = lax.fori_loop(0, i, functools.partial(step, masked=False), (one,) * ATT_HEADS)
        carry = step(i, carry, True)
        o_ref[...] = jnp.concatenate([carry[hh][2] / carry[hh][1] for hh in range(ATT_HEADS)], axis=0).T
        for hh in range(ATT_HEADS):
            m, l, _ = carry[hh]
            lse_ref[:, qcs[hh]] = jnp.broadcast_to(m * ATT_SCALE_LOG2E + jnp.log2(l), (HEAD_PAD, t)).T
        pl.when(step_no == groups * n_q - 1)(gat.finish)

    any_spec = pl.BlockSpec(memory_space=pl.ANY)
    out = pl.pallas_call(
        body,
        name="attn_fwd",
        grid=(groups, n_q),
        in_specs=[
            pl.BlockSpec((ATT_HEADS * HEAD_PAD, t), lambda p, i: (p, i)),
            pl.BlockSpec((s, ATT_HEADS * HEAD_PAD), lambda p, i: (0, p)),
            pl.BlockSpec((s, ATT_HEADS * HEAD_PAD), lambda p, i: (0, p)),
            pl.BlockSpec((ATT_HEADS * V_HEAD_DIM, s), lambda p, i: (p, 0)),
        ] + [any_spec] * n_w,
        out_specs=[
            pl.BlockSpec((t, ATT_HEADS * V_HEAD_DIM), lambda p, i: (i, p)),
            pl.BlockSpec((t, ATT_HEADS * HEAD_PAD), lambda p, i: (i, p)),
        ] + [any_spec] * n_w,
        out_shape=[jax.ShapeDtypeStruct((s, MLA_WIDTH), F32), jax.ShapeDtypeStruct((s, MLA_HEADS * HEAD_PAD), F32)]
        + gat.out_shape,
        scratch_shapes=gat.scratch,
        compiler_params=pltpu.CompilerParams(dimension_semantics=("arbitrary", "arbitrary"),
                                             vmem_limit_bytes=VMEM_LIMIT),
    )(q_t, k, k, v_t, *late_shards)
    return out[0], out[1], out[2:]


def _mid(o, z_sh, x, target, pool_w, pool_scale, w_ba, w_bp, w_out, norm_final, tm):
    s = x.shape[0]
    n_tiles = s // tm
    halo_per_tile = tm // POOL_HALO

    def body(o_ref, z0_ref, z1_ref, z1h_ref, z2_ref, z3_ref, x_ref, t_ref, pw_ref, ps_ref, wba_ref, wbp_ref,
             wout_ref, gf_ref,
             do_ref, dl_ref, dga_ref, dgp_ref, dgm_ref, ddc_ref, dh_ref,
             loss_ref, dwout_out, dwba_out, dwbp_out, dpw_ref, dps_ref, dgf_ref,
             ubuf, dwout_ref, dwba_ref, dwbp_ref):
        i = pl.program_id(0)

        @pl.when(i == 0)
        def _():
            loss_ref[...] = jnp.zeros_like(loss_ref)
            dwout_ref[...] = jnp.zeros_like(dwout_ref)
            dwba_ref[...] = jnp.zeros_like(dwba_ref)
            dwbp_ref[...] = jnp.zeros_like(dwbp_ref)
            dpw_ref[...] = jnp.zeros_like(dpw_ref)
            dps_ref[...] = jnp.zeros_like(dps_ref)
            dgf_ref[...] = jnp.zeros_like(dgf_ref)

        zs = [z0_ref[0], z1_ref[0], z2_ref[0], z3_ref[0]]
        o = o_ref[...]
        ga = _segment(zs, 3)
        sga = _sigmoid(ga)
        silu_a = ga * sga
        y_attn = (o * silu_a).astype(BF16)

        ubuf[0:POOL_HALO, :] = jnp.where(i > 0, _segment([None, z1h_ref[0]], 4), 0.0)
        ubuf[POOL_HALO:, :] = _segment(zs, 4)
        row = lax.broadcasted_iota(jnp.int32, (tm, POOL_GROUP_DIM), 0) + i * tm
        ps = ps_ref[...]
        gp = _segment(zs, 5)
        sgp = _sigmoid(gp)
        silu_p = gp * sgp
        d_bf, dm, inv_cnt = [], [], []
        for g, w in enumerate(POOL_WINDOWS):
            cols = slice(g * POOL_GROUP_DIM, (g + 1) * POOL_GROUP_DIM)
            wsum = ubuf[POOL_HALO:, cols]
            for kk in range(1, w):
                wsum = wsum + ubuf[POOL_HALO - kk:POOL_HALO - kk + tm, cols]
            inv = 1.0 / jnp.minimum(row + 1, w).astype(F32)
            dg = (wsum * inv - ubuf[POOL_HALO:, cols]).astype(BF16)
            d_bf.append(dg)
            inv_cnt.append(inv)
            dm.append(_dot(dg, pw_ref[g]))
        dm = jnp.concatenate(dm, axis=1)
        yp = dm * ps
        y_pool = (yp * silu_p).astype(BF16)

        a = jnp.concatenate([_dot(y_attn, wba_ref[j]) for j in range(N_CHIPS)], axis=1)
        p = jnp.concatenate([_dot(y_pool, wbp_ref[j]) for j in range(N_CHIPS)], axis=1)
        gm = _segment(zs, 6)
        gate_a = _sigmoid(gm[:, :D_MODEL])
        gate_p = _sigmoid(gm[:, D_MODEL:])
        merged = (gate_a * a + gate_p * p).astype(BF16)
        h = x_ref[...] + _dot(merged, wout_ref[...])
        gf = gf_ref[...]
        y, xhat, r = _rms_fwd(h, gf)
        err = y - t_ref[...]
        e2 = err * err
        e2 = jnp.sum(e2.reshape(tm // 8, 8, D_MODEL), axis=0)
        acc = e2[:, 0:LANES]
        for cidx in range(1, D_MODEL // LANES):
            acc = acc + e2[:, cidx * LANES:(cidx + 1) * LANES]
        loss_ref[...] += acc

        dy = err * (1.0 / D_MODEL)
        dgf_ref[...] += _colsum(dy * xhat)
        dh = _rms_bwd(dy, xhat, r, gf)
        dh_ref[...] = dh
        dh_bf = dh.astype(BF16)
        dwout_ref[...] += _dot_tn(merged, dh_bf)
        dmerged = _dot_nt(dh_bf, wout_ref[...])
        da = (dmerged * gate_a).astype(BF16)
        dp = (dmerged * gate_p).astype(BF16)
        dgm_ref[:, :D_MODEL] = (dmerged * a * gate_a * (1.0 - gate_a)).astype(BF16)
        dgm_ref[:, D_MODEL:] = (dmerged * p * gate_p * (1.0 - gate_p)).astype(BF16)
        dy_attn = dy_pool = None
        for j in range(N_CHIPS):
            cols = slice(j * BRANCH_COLS, (j + 1) * BRANCH_COLS)
            dwba_ref[j] += _dot_tn(y_attn, da[:, cols])
            dwbp_ref[j] += _dot_tn(y_pool, dp[:, cols])
            pa = _dot_nt(da[:, cols], wba_ref[j])
            pp = _dot_nt(dp[:, cols], wbp_ref[j])
            dy_attn = pa if dy_attn is None else dy_attn + pa
            dy_pool = pp if dy_pool is None else dy_pool + pp

        do = dy_attn * silu_a
        do_ref[...] = do
        dga_ref[...] = (dy_attn * o * (sga * (1.0 + ga * (1.0 - sga)))).astype(BF16)
        doo = do * o
        for hd in range(MLA_HEADS):
            dl = jnp.sum(doo[:, hd * V_HEAD_DIM:(hd + 1) * V_HEAD_DIM], axis=1, keepdims=True)
            dl_ref[:, hd * HEAD_PAD:(hd + 1) * HEAD_PAD] = jnp.broadcast_to(dl, (tm, HEAD_PAD))

        dyp = dy_pool * silu_p
        dgp_ref[...] = (dy_pool * yp * (sgp * (1.0 + gp * (1.0 - sgp)))).astype(BF16)
        dps_ref[...] += _colsum(dyp * dm)
        dmm = (dyp * ps).astype(BF16)
        for g in range(len(POOL_WINDOWS)):
            cols = slice(g * POOL_GROUP_DIM, (g + 1) * POOL_GROUP_DIM)
            dpw_ref[g] += _dot_tn(d_bf[g], dmm[:, cols])
            ddc_ref[:, cols] = _dot_nt(dmm[:, cols], pw_ref[g]) * inv_cnt[g]

        @pl.when(i == n_tiles - 1)
        def _():
            dwout_out[...] = dwout_ref[...].astype(BF16)
            dwba_out[...] = dwba_ref[...].astype(BF16)
            dwbp_out[...] = dwbp_ref[...].astype(BF16)

    row_in = lambda n: _row_spec(tm, n)
    in_specs = [
        row_in(MLA_WIDTH),
        pl.BlockSpec((1, tm, SHARD_COLS), lambda i: (0, i, 0)), pl.BlockSpec((1, tm, SHARD_COLS), lambda i: (1, i, 0)),
        pl.BlockSpec((1, POOL_HALO, SHARD_COLS), lambda i: (1, jnp.maximum(i * halo_per_tile - 1, 0), 0)),
        pl.BlockSpec((1, tm, SHARD_COLS), lambda i: (2, i, 0)), pl.BlockSpec((1, tm, SHARD_COLS), lambda i: (3, i, 0)),
        row_in(D_MODEL), row_in(D_MODEL),
        _full_spec((4, POOL_GROUP_DIM, POOL_GROUP_DIM)), _full_spec((1, POOL_WIDTH)),
        _full_spec((N_CHIPS, MLA_WIDTH, BRANCH_COLS)), _full_spec((N_CHIPS, POOL_WIDTH, BRANCH_COLS)),
        _full_spec((D_MODEL, D_MODEL)), _full_spec((1, D_MODEL)),
    ]
    out_shape = [
        jax.ShapeDtypeStruct((s, MLA_WIDTH), F32),
        jax.ShapeDtypeStruct((s, MLA_HEADS * HEAD_PAD), F32),
        jax.ShapeDtypeStruct((s, MLA_WIDTH), BF16),
        jax.ShapeDtypeStruct((s, POOL_WIDTH), BF16),
        jax.ShapeDtypeStruct((s, 2 * D_MODEL), BF16),
        jax.ShapeDtypeStruct((s, POOL_WIDTH), F32),
        jax.ShapeDtypeStruct((s, D_MODEL), F32),
        jax.ShapeDtypeStruct((8, LANES), F32),
        jax.ShapeDtypeStruct((D_MODEL, D_MODEL), BF16),
        jax.ShapeDtypeStruct((N_CHIPS, MLA_WIDTH, BRANCH_COLS), BF16),
        jax.ShapeDtypeStruct((N_CHIPS, POOL_WIDTH, BRANCH_COLS), BF16),
        jax.ShapeDtypeStruct((4, POOL_GROUP_DIM, POOL_GROUP_DIM), F32),
        jax.ShapeDtypeStruct((1, POOL_WIDTH), F32),
        jax.ShapeDtypeStruct((1, D_MODEL), F32),
    ]
    out_specs = [
        row_in(MLA_WIDTH), row_in(MLA_HEADS * HEAD_PAD), row_in(MLA_WIDTH), row_in(POOL_WIDTH),
        row_in(2 * D_MODEL), row_in(POOL_WIDTH), row_in(D_MODEL),
        _full_spec((8, LANES)), _full_spec((D_MODEL, D_MODEL)), _full_spec((N_CHIPS, MLA_WIDTH, BRANCH_COLS)),
        _full_spec((N_CHIPS, POOL_WIDTH, BRANCH_COLS)), _full_spec((4, POOL_GROUP_DIM, POOL_GROUP_DIM)),
        _full_spec((1, POOL_WIDTH)), _full_spec((1, D_MODEL)),
    ]
    return pl.pallas_call(
        body,
        name="mid",
        grid=(n_tiles,),
        in_specs=in_specs,
        out_specs=out_specs,
        out_shape=out_shape,
        scratch_shapes=[
            pltpu.VMEM((tm + POOL_HALO, POOL_WIDTH), F32),
            pltpu.VMEM((D_MODEL, D_MODEL), F32),
            pltpu.VMEM((N_CHIPS, MLA_WIDTH, BRANCH_COLS), F32),
            pltpu.VMEM((N_CHIPS, POOL_WIDTH, BRANCH_COLS), F32),
        ],
        compiler_params=pltpu.CompilerParams(dimension_semantics=("arbitrary",), vmem_limit_bytes=VMEM_LIMIT),
    )(o, z_sh, z_sh, z_sh, z_sh, z_sh, x, target, pool_w, pool_scale, w_ba, w_bp, w_out, norm_final)


def _attn_bwd(q, q_t, k, v, do, lse, delta, late_grads, gs_mid, t):
    s = q.shape[0]
    groups = MLA_HEADS // ATT_HEADS
    n_q = s // t
    red = _Reduce(COMM_PARAMS[3:])
    n_w = len(red.params)
    small = _SmallSum(gs_mid.shape[0])
    n_red = len(red.scratch)

    def body(q_ref, qt_ref, do_ref, lse_ref, dl_ref, k_ref, v_ref, *rest):
        g_in, gs_ref = rest[:n_w], rest[n_w]
        (dq_ref, dk_ref, dv_ref), g_out, gsum_ref = rest[n_w + 1:n_w + 4], rest[n_w + 4:2 * n_w + 4], rest[2 * n_w + 4]
        scratch = rest[2 * n_w + 5:]
        red.bind(g_in, g_out, scratch[:n_red])
        small.bind(gs_ref, gsum_ref, scratch[n_red:])
        i = pl.program_id(1)
        step_no = pl.program_id(0) * n_q + i

        @pl.when(step_no == 0)
        def _():
            red.start()
            small.start()

        pl.when(step_no == n_q)(red.exchange)

        @pl.when(i == 0)
        def _():
            dk_ref[...] = jnp.zeros_like(dk_ref)
            dv_ref[...] = jnp.zeros_like(dv_ref)

        mask = _chunk_mask(t, False)
        qcs = [slice(hh * HEAD_PAD, (hh + 1) * HEAD_PAD) for hh in range(ATT_HEADS)]
        vcs = [slice(hh * V_HEAD_DIM, (hh + 1) * V_HEAD_DIM) for hh in range(ATT_HEADS)]
        qhs = [q_ref[:, qc] for qc in qcs]
        qts = [qt_ref[qc, :] for qc in qcs]
        dohs = [do_ref[:, vc].astype(BF16) for vc in vcs]
        do_t = do_ref[...].T.astype(BF16)
        dots = [do_t[vc, :] for vc in vcs]
        lses = [lse_ref[:, hh * HEAD_PAD:hh * HEAD_PAD + 1] for hh in range(ATT_HEADS)]
        dls = [dl_ref[:, hh * HEAD_PAD:hh * HEAD_PAD + 1] for hh in range(ATT_HEADS)]

        def step(j, dqs, masked):
            keys = pl.ds(pl.multiple_of(j * t, t), t)
            out = []
            for hh in range(ATT_HEADS):
                kj = k_ref[keys, qcs[hh]]
                vj = v_ref[keys, vcs[hh]]
                p = jnp.exp2(_dot_nt(qhs[hh], kj) * ATT_SCALE_LOG2E - lses[hh])
                if masked:
                    p = jnp.where(mask, p, 0.0)
                ds = (p * (_dot_nt(dohs[hh], vj) - dls[hh])).astype(BF16)
                dv_ref[vcs[hh], keys] += _dot(dots[hh], p.astype(BF16))
                dk_ref[qcs[hh], keys] += _dot(qts[hh], ds) * ATT_SCALE
                out.append(dqs[hh] + _dot(ds, kj))
            return tuple(out)

        zero = jnp.zeros((t, HEAD_PAD), F32)
        dqs = lax.fori_loop(0, i, functools.partial(step, masked=False), (zero,) * ATT_HEADS)
        dqs = step(i, dqs, True)
        for hh in range(ATT_HEADS):
            dq_ref[:, qcs[hh]] = dqs[hh] * ATT_SCALE

        @pl.when(step_no == groups * n_q - 1)
        def _():
            red.finish()
            small.finish()

    hw = MLA_HEADS * HEAD_PAD
    any_spec = pl.BlockSpec(memory_space=pl.ANY)
    out = pl.pallas_call(
        body,
        name="attn_bwd",
        grid=(groups, n_q),
        in_specs=[
            pl.BlockSpec((t, ATT_HEADS * HEAD_PAD), lambda p, i: (i, p)),
            pl.BlockSpec((ATT_HEADS * HEAD_PAD, t), lambda p, i: (p, i)),
            pl.BlockSpec((t, ATT_HEADS * V_HEAD_DIM), lambda p, i: (i, p)),
            pl.BlockSpec((t, ATT_HEADS * HEAD_PAD), lambda p, i: (i, p)),
            pl.BlockSpec((t, ATT_HEADS * HEAD_PAD), lambda p, i: (i, p)),
            pl.BlockSpec((s, ATT_HEADS * HEAD_PAD), lambda p, i: (0, p), pipeline_mode=pl.Buffered(1)),
            pl.BlockSpec((s, ATT_HEADS * V_HEAD_DIM), lambda p, i: (0, p), pipeline_mode=pl.Buffered(1)),
        ] + [any_spec] * n_w + [pl.BlockSpec(small.spec_shape, lambda p, i: (0, 0))],
        out_specs=[
            pl.BlockSpec((t, ATT_HEADS * HEAD_PAD), lambda p, i: (i, p)),
            pl.BlockSpec((ATT_HEADS * HEAD_PAD, s), lambda p, i: (p, 0)),
            pl.BlockSpec((ATT_HEADS * V_HEAD_DIM, s), lambda p, i: (p, 0)),
        ] + [any_spec] * n_w + [pl.BlockSpec(small.spec_shape, lambda p, i: (0, 0))],
        out_shape=[jax.ShapeDtypeStruct((s, hw), F32), jax.ShapeDtypeStruct((hw, s), F32),
                   jax.ShapeDtypeStruct((MLA_WIDTH, s), F32)] + red.out_shape + [small.out_shape],
        scratch_shapes=red.scratch + small.scratch,
        compiler_params=pltpu.CompilerParams(dimension_semantics=("arbitrary", "arbitrary"),
                                             vmem_limit_bytes=VMEM_LIMIT),
    )(q, q_t, do, lse, delta, k, v, *late_grads, gs_mid)
    return out[0], out[1], out[2], out[3:3 + n_w], out[3 + n_w]


def _qkv_bwd(dq, dk_t, dv_t, z_sh, q_norm, kv_norm, wuq_p, wk_p, wv, rc, rsa, rsb, tm):
    s = z_sh.shape[1]
    hw = MLA_HEADS * HEAD_PAD

    def body(dq_ref, dk_ref, dv_ref, z_ref, gq_ref, gkv_ref, wuq_ref, wk_ref, wv_ref,
             c_ref, sa_ref, sb_ref,
             dzq_ref, dzkv_ref, dzkr_ref, dwuq_ref, dwk_ref, dwv_ref, dgq_ref, dgkv_ref):
        i = pl.program_id(0)

        @pl.when(i == 0)
        def _():
            dwuq_ref[...] = jnp.zeros_like(dwuq_ref)
            dwk_ref[...] = jnp.zeros_like(dwk_ref)
            dwv_ref[...] = jnp.zeros_like(dwv_ref)
            dgq_ref[...] = jnp.zeros_like(dgq_ref)
            dgkv_ref[...] = jnp.zeros_like(dgkv_ref)

        c, sa, sb = c_ref[...], sa_ref[...], sb_ref[...]
        gq, gkv = gq_ref[...], gkv_ref[...]

        z0 = z_ref[0]
        cq, xq, rq = _rms_fwd(z0[:, ZQ_COLS], gq)
        dqp = jnp.concatenate(
            [_unrope(dq_ref[:, h * HEAD_PAD:(h + 1) * HEAD_PAD], c, sa, sb) for h in range(MLA_HEADS)],
            axis=1).astype(BF16)
        dwuq_ref[...] += _dot_tn(cq.astype(BF16), dqp)
        dcq = _dot_nt(dqp, wuq_ref[...])
        dgq_ref[...] += _colsum(dcq * xq)
        dzq_ref[...] = _rms_bwd(dcq, xq, rq, gq).astype(BF16)

        ckv, xkv, rkv = _rms_fwd(z0[:, ZKV_COLS], gkv)
        ckv = ckv.astype(BF16)
        dkf = dk_ref[...].T
        dk_bf = dkf.astype(BF16)
        dv_bf = dv_ref[...].T.astype(BF16)
        dwk_ref[...] += _dot_tn(ckv, dk_bf)
        dwv_ref[...] += _dot_tn(ckv, dv_bf)
        dckv = _dot_nt(dk_bf, wk_ref[...]) + _dot_nt(dv_bf, wv_ref[...])
        dgkv_ref[...] += _colsum(dckv * xkv)
        dzkv_ref[...] = _rms_bwd(dckv, xkv, rkv, gkv).astype(BF16)

        dkr = dkf[:, 0:HEAD_PAD]
        for h in range(1, MLA_HEADS):
            dkr = dkr + dkf[:, h * HEAD_PAD:(h + 1) * HEAD_PAD]
        dkr = pltpu.roll(_unrope(dkr, c, sa, sb), 64, 1)
        lane = lax.broadcasted_iota(jnp.int32, (tm, HEAD_PAD), 1)
        dzkr_ref[...] = jnp.where(lane < QK_ROPE_DIM, dkr, 0.0).astype(BF16)

    return pl.pallas_call(
        body,
        name="qkv_bwd",
        grid=(s // tm,),
        in_specs=[
            _row_spec(tm, hw), pl.BlockSpec((hw, tm), lambda i: (0, i)), pl.BlockSpec((MLA_WIDTH, tm), lambda i: (0, i)),
            pl.BlockSpec((1, tm, SHARD_COLS), lambda i: (0, i, 0)),
            _full_spec((1, Q_LORA_RANK)), _full_spec((1, KV_LORA_RANK)),
            _full_spec((Q_LORA_RANK, hw)), _full_spec((KV_LORA_RANK, hw)), _full_spec((KV_LORA_RANK, MLA_WIDTH)),
            _row_spec(tm, HEAD_PAD), _row_spec(tm, HEAD_PAD), _row_spec(tm, HEAD_PAD),
        ],
        out_specs=[
            _row_spec(tm, Q_LORA_RANK), _row_spec(tm, KV_LORA_RANK), _row_spec(tm, HEAD_PAD),
            _full_spec((Q_LORA_RANK, hw)), _full_spec((KV_LORA_RANK, hw)), _full_spec((KV_LORA_RANK, MLA_WIDTH)),
            _full_spec((1, Q_LORA_RANK)), _full_spec((1, KV_LORA_RANK)),
        ],
        out_shape=[
            jax.ShapeDtypeStruct((s, Q_LORA_RANK), BF16), jax.ShapeDtypeStruct((s, KV_LORA_RANK), BF16),
            jax.ShapeDtypeStruct((s, HEAD_PAD), BF16),
            jax.ShapeDtypeStruct((Q_LORA_RANK, hw), F32), jax.ShapeDtypeStruct((KV_LORA_RANK, hw), F32),
            jax.ShapeDtypeStruct((KV_LORA_RANK, MLA_WIDTH), F32),
            jax.ShapeDtypeStruct((1, Q_LORA_RANK), F32), jax.ShapeDtypeStruct((1, KV_LORA_RANK), F32),
        ],
        compiler_params=pltpu.CompilerParams(dimension_semantics=("arbitrary",), vmem_limit_bytes=VMEM_LIMIT),
    )(dq, dk_t, dv_t, z_sh, q_norm, kv_norm, wuq_p, wk_p, wv, rc, rsa, rsb)


def _inproj_bwd_x(dzq, dzkv, dzkr, dgattn, ddc, dgpool, dgmerge, x, dh, norm_in, w_in_t, tm):
    s = x.shape[0]
    n_tiles = s // tm
    halo_per_tile = tm // POOL_HALO
    n_halo = s // POOL_HALO
    u_seg = 4

    def body(dzq_ref, dzkv_ref, dzkr_ref, dga_ref, ddc_ref, ddn_ref, dgp_ref, dgm_ref, x_ref, dh_ref,
             g_ref, w_hbm, gx_ref, dgin_ref, dzs_ref, w_vmem, dbuf, sem):
        i = pl.program_id(0)

        @pl.when(i == 0)
        def _():
            cp = pltpu.make_async_copy(w_hbm, w_vmem, sem)
            cp.start()
            dgin_ref[...] = jnp.zeros_like(dgin_ref)
            cp.wait()

        dbuf[0:tm, :] = ddc_ref[...]
        dbuf[tm:, :] = jnp.where(i < n_tiles - 1, ddn_ref[...], 0.0)
        row = lax.broadcasted_iota(jnp.int32, (tm, POOL_GROUP_DIM), 0) + i * tm
        du = []
        for g, w in enumerate(POOL_WINDOWS):
            cols = slice(g * POOL_GROUP_DIM, (g + 1) * POOL_GROUP_DIM)
            fsum = dbuf[0:tm, cols]
            for kk in range(1, w):
                fsum = fsum + dbuf[kk:kk + tm, cols]
            du.append(fsum - dbuf[0:tm, cols] * jnp.minimum(row + 1, w).astype(F32))
        du = jnp.concatenate(du, axis=1).astype(BF16)

        dz = [dzq_ref[...], dzkv_ref[...], dzkr_ref[...], dga_ref[...], du, dgp_ref[...], dgm_ref[...]]
        dz = jnp.concatenate([d[:, :w] for d, (w, _) in zip(dz, IN_SEGMENTS)], axis=1)
        for j in range(N_CHIPS):
            dzs_ref[j] = dz[:, j * SHARD_COLS:(j + 1) * SHARD_COLS].T
        dhn = _dot(dz, w_vmem[...])

        g = g_ref[...]
        _, xhat, r = _rms_fwd(x_ref[...], g)
        dgin_ref[...] += _colsum(dhn * xhat)
        gx_ref[...] = dh_ref[...] + _rms_bwd(dhn, xhat, r, g)

    any_spec = pl.BlockSpec(memory_space=pl.ANY)
    seg_w = [wide for _, wide in IN_SEGMENTS]
    return pl.pallas_call(
        body,
        name="inproj_bwd_x",
        grid=(n_tiles,),
        in_specs=[
            _row_spec(tm, seg_w[0]), _row_spec(tm, seg_w[1]), _row_spec(tm, seg_w[2]),
            _row_spec(tm, seg_w[3]), _row_spec(tm, seg_w[u_seg]),
            pl.BlockSpec((POOL_HALO, POOL_WIDTH), lambda i: (jnp.minimum((i + 1) * halo_per_tile, n_halo - 1), 0)),
            _row_spec(tm, seg_w[5]), _row_spec(tm, seg_w[6]),
            _row_spec(tm, D_MODEL), _row_spec(tm, D_MODEL),
            _full_spec((1, D_MODEL)), any_spec,
        ],
        out_specs=[_row_spec(tm, D_MODEL), _full_spec((1, D_MODEL)),
                   pl.BlockSpec((N_CHIPS, SHARD_COLS, tm), lambda i: (0, 0, i))],
        out_shape=[jax.ShapeDtypeStruct((s, D_MODEL), F32), jax.ShapeDtypeStruct((1, D_MODEL), F32),
                   jax.ShapeDtypeStruct((N_CHIPS, SHARD_COLS, s), BF16)],
        scratch_shapes=[
            pltpu.VMEM((IN_TOTAL, D_MODEL), BF16),
            pltpu.VMEM((tm + POOL_HALO, POOL_WIDTH), F32),
            pltpu.SemaphoreType.DMA,
        ],
        compiler_params=pltpu.CompilerParams(dimension_semantics=("arbitrary",), vmem_limit_bytes=VMEM_LIMIT),
    )(dzq, dzkv, dzkr, dgattn, ddc, ddc, dgpool, dgmerge, x, dh, norm_in, w_in_t.reshape(IN_TOTAL, D_MODEL))


def _inproj_bwd_w(order, dz_sh, hn, g_uq, g_ukv, gs, tm):
    s = hn.shape[0]
    n_tiles = s // tm
    hc = D_MODEL // 2
    red = _Reduce(COMM_PARAMS[1:3])
    small = _SmallSum(gs.shape[0])
    n_red = len(red.scratch)

    def body(order_ref, dz_ref, hn_ref, guq_hbm, gukv_hbm, gs_ref, gw_hbm, guq_out, gukv_out, gsum_ref,
             acc, pm_w, a_w, b_w, r_w, w_send, w_recv, w_local, *more_scratch):
        ph, i = pl.program_id(0), pl.program_id(1)
        x, y, c = lax.axis_index("x"), lax.axis_index("y"), lax.axis_index("c")
        k = 2 * x + y
        me, sibling = (x, y, c), (x, y, 1 - c)
        chips = _other_chips(x, y)
        shard_of_phase = [2 * cx + cy for cx, cy in chips] + [k]
        copy = _remote_copier(w_send, w_recv)
        red.bind([guq_hbm, gukv_hbm], [guq_out, gukv_out], more_scratch[:n_red])
        small.bind(gs_ref, gsum_ref, more_scratch[n_red:])
        mine = pl.ds(pl.multiple_of(c * hc, hc), hc)
        theirs = pl.ds(pl.multiple_of((1 - c) * hc, hc), hc)

        def to_sibling(f):
            j = shard_of_phase[f]
            return copy(f, pm_w.at[j, 1 - c], a_w.at[j], sibling)

        def pair_sum(f):
            cx, cy = chips[f]
            return copy(4 + f, pm_w.at[shard_of_phase[f], c], b_w.at[f], (cx, cy, c))

        def finished():
            return copy(7, r_w, gw_hbm.at[:, mine], sibling)

        @pl.when(jnp.logical_and(ph == 0, i == 0))
        def _():
            red.start()
            small.start()

        part = _dot(dz_ref[0], hn_ref[...])

        @pl.when(i == 0)
        def _():
            acc[...] = part

        @pl.when(i > 0)
        def _():
            acc[...] += part

        for f in range(3):
            @pl.when(jnp.logical_and(ph == f + 1, i == 0))
            def _(f=f):
                j = shard_of_phase[f]
                copy(f, a_w.at[j], a_w.at[j], me).wait_recv()
                pm_w[j, c] = (pm_w[j, c].astype(F32) + a_w[j].astype(F32)).astype(BF16)
                pair_sum(f).start()
                if f == 0:
                    red.exchange()

        for f in range(4):
            @pl.when(jnp.logical_and(ph == f, i == n_tiles - 1))
            def _(f=f):
                j = shard_of_phase[f]
                pm_w[j, 0] = acc[:, :hc].astype(BF16)
                pm_w[j, 1] = acc[:, hc:].astype(BF16)
                to_sibling(f).start()
                if f < 3:
                    return
                copy(3, a_w.at[k], a_w.at[k], me).wait_recv()
                r_w[...] = pm_w[k, c].astype(F32) + a_w[k].astype(F32)
                for g in range(3):
                    copy(4 + g, b_w.at[g], b_w.at[g], me).wait_recv()
                    r_w[...] = r_w[...] + b_w[g].astype(F32)
                store = pltpu.make_async_copy(r_w, gw_hbm.at[:, mine], w_local)
                store.start()
                finished().start()
                red.finish()
                small.finish()
                copy(7, gw_hbm.at[:, theirs], gw_hbm.at[:, theirs], me).wait_recv()
                store.wait()
                for g in range(4):
                    to_sibling(g).wait_send()
                for g in range(3):
                    pair_sum(g).wait_send()
                finished().wait_send()

    any_spec = pl.BlockSpec(memory_space=pl.ANY)
    n_sem = 8
    grid_spec = pltpu.PrefetchScalarGridSpec(
        num_scalar_prefetch=1,
        grid=(N_CHIPS, n_tiles),
        in_specs=[
            pl.BlockSpec((1, SHARD_COLS, tm), lambda ph, i, order: (order[ph], 0, i)),
            pl.BlockSpec((tm, D_MODEL), lambda ph, i, order: (i, 0)),
            any_spec, any_spec,
            pl.BlockSpec(small.spec_shape, lambda ph, i, order: (0, 0)),
        ],
        out_specs=[any_spec, any_spec, any_spec, pl.BlockSpec(small.spec_shape, lambda ph, i, order: (0, 0))],
        scratch_shapes=[
            pltpu.VMEM((SHARD_COLS, D_MODEL), F32),
            pltpu.VMEM((N_CHIPS, 2, SHARD_COLS, hc), BF16),
            pltpu.VMEM((N_CHIPS, SHARD_COLS, hc), BF16),
            pltpu.VMEM((3, SHARD_COLS, hc), BF16),
            pltpu.VMEM((SHARD_COLS, hc), F32),
            pltpu.SemaphoreType.DMA((n_sem,)), pltpu.SemaphoreType.DMA((n_sem,)), pltpu.SemaphoreType.DMA,
        ] + red.scratch + small.scratch,
    )
    out = pl.pallas_call(
        body,
        name="inproj_bwd_w",
        grid_spec=grid_spec,
        out_shape=[jax.ShapeDtypeStruct((SHARD_COLS, D_MODEL), F32)] + red.out_shape
        + [small.out_shape],
        compiler_params=pltpu.CompilerParams(dimension_semantics=("arbitrary", "arbitrary"),
                                             vmem_limit_bytes=VMEM_LIMIT),
    )(order, dz_sh, hn, g_uq, g_ukv, gs)
    return out[0], out[1], out[2], out[3]


def _other_chips(x, y):
    return ((1 - x, 1 - y), (1 - x, y), (x, 1 - y))


def _half(ref, axis, size, c, lead=()):
    window = pl.ds(pl.multiple_of(c * size, size), size)
    if axis == 0:
        return ref.at[(*lead, window, slice(None))]
    return ref.at[(*lead, slice(None), window)]


def _half_shape(rows, cols, axis, size):
    return (size, cols) if axis == 0 else (rows, size)


def _remote_copier(send_sems, recv_sems):
    def copy(sem, src, dst, to):
        return pltpu.make_async_remote_copy(src_ref=src, dst_ref=dst, send_sem=send_sems.at[sem],
                                            recv_sem=recv_sems.at[sem], device_id=to, device_id_type=MESH)
    return copy


class _Gather:
    def __init__(self, params):
        self.params = params
        n = len(params)
        self.scratch = [pltpu.SemaphoreType.DMA((6 * n,)), pltpu.SemaphoreType.DMA((6 * n,)),
                        pltpu.SemaphoreType.DMA((n,))]
        self.out_shape = [jax.ShapeDtypeStruct((N_CHIPS, r, cc), BF16) for _, r, cc, _, _ in params]

    def bind(self, ins, outs, scratch):
        self.ins, self.outs = ins, outs
        send_sems, recv_sems, self.local_sems = scratch
        self.copy = _remote_copier(send_sems, recv_sems)
        self.x, self.y, self.c = lax.axis_index("x"), lax.axis_index("y"), lax.axis_index("c")
        self.k = 2 * self.x + self.y
        self.chips = _other_chips(self.x, self.y)

    def _local(self, p):
        return pltpu.make_async_copy(self.ins[p], self.outs[p].at[self.k], self.local_sems.at[p])

    def _first(self, p, j):
        _, _, _, axis, size = self.params[p]
        cx, cy = self.chips[j]
        return self.copy(6 * p + j, _half(self.ins[p], axis, size, self.c),
                         _half(self.outs[p], axis, size, self.c, (self.k,)), (cx, cy, self.c))

    def _relay(self, p, j, half_of):
        _, _, _, axis, size = self.params[p]
        cx, cy = self.chips[j]
        block = _half(self.outs[p], axis, size, half_of, (2 * cx + cy,))
        return self.copy(6 * p + 3 + j, block, block, (self.x, self.y, 1 - self.c))

    def start(self):
        for p in range(len(self.params)):
            self._local(p).start()
            for j in (1, 2, 0):
                self._first(p, j).start()

    def relay_one(self, p, j):
        _, _, _, axis, size = self.params[p]
        cx, cy = self.chips[j]
        landed = _half(self.outs[p], axis, size, self.c, (2 * cx + cy,))
        self.copy(6 * p + j, landed, landed, (self.x, self.y, self.c)).wait_recv()
        self._relay(p, j, self.c).start()

    def await_one(self, p, j):
        self._relay(p, j, 1 - self.c).wait_recv()

    def wait_sends(self):
        for p in range(len(self.params)):
            for j in range(3):
                self._first(p, j).wait_send()
                self._relay(p, j, self.c).wait_send()
            self._local(p).wait()

    def relay(self):
        for j in range(3):
            for p in range(len(self.params)):
                self.relay_one(p, j)

    def finish(self):
        for j in range(3):
            for p in range(len(self.params)):
                self.await_one(p, j)
        self.wait_sends()


class _Reduce:
    def __init__(self, params):
        self.params = params
        n = len(params)
        halves = [_half_shape(r, cc, axis, size) for _, r, cc, axis, size in params]
        self.scratch = ([pltpu.VMEM((N_CHIPS, *h), BF16) for h in halves]
                        + [pltpu.VMEM((N_CHIPS, *h), BF16) for h in halves]
                        + [pltpu.VMEM((3, *h), BF16) for h in halves]
                        + [pltpu.VMEM(h, F32) for h in halves]
                        + [pltpu.SemaphoreType.DMA((5 * n,)), pltpu.SemaphoreType.DMA((5 * n,)),
                           pltpu.SemaphoreType.DMA((2 * n,))])
        self.out_shape = [jax.ShapeDtypeStruct((r, cc), F32) for _, r, cc, _, _ in params]

    def bind(self, g_in, g_out, scratch):
        n = len(self.params)
        self.g_in, self.g_out = g_in, g_out
        self.pm, self.a_buf = scratch[0:n], scratch[n:2 * n]
        self.b_buf, self.r_buf = scratch[2 * n:3 * n], scratch[3 * n:4 * n]
        send_sems, recv_sems, self.local_sems = scratch[4 * n:]
        self.copy = _remote_copier(send_sems, recv_sems)
        self.x, self.y, self.c = lax.axis_index("x"), lax.axis_index("y"), lax.axis_index("c")
        self.k = 2 * self.x + self.y
        self.chips = _other_chips(self.x, self.y)
        self.me = (self.x, self.y, self.c)
        self.sibling = (self.x, self.y, 1 - self.c)

    def _load(self, p):
        _, _, _, axis, size = self.params[p]
        return pltpu.make_async_copy(_half(self.g_in[p], axis, size, self.c, (slice(None),)), self.pm[p],
                                     self.local_sems.at[p])

    def _to_sibling(self, p):
        _, _, _, axis, size = self.params[p]
        return self.copy(5 * p, _half(self.g_in[p], axis, size, 1 - self.c, (slice(None),)), self.a_buf[p],
                         self.sibling)

    def _pair_sum(self, p, j):
        cx, cy = self.chips[j]
        return self.copy(5 * p + 1 + j, self.pm[p].at[2 * cx + cy], self.b_buf[p].at[j], (cx, cy, self.c))

    def _store(self, p):
        _, _, _, axis, size = self.params[p]
        n = len(self.params)
        return pltpu.make_async_copy(self.r_buf[p], _half(self.g_out[p], axis, size, self.c),
                                     self.local_sems.at[n + p])

    def _finished(self, p):
        _, _, _, axis, size = self.params[p]
        return self.copy(5 * p + 4, self.r_buf[p], _half(self.g_out[p], axis, size, self.c), self.sibling)

    def start(self):
        for p in range(len(self.params)):
            self._load(p).start()
            self._to_sibling(p).start()

    def exchange(self):
        for p in range(len(self.params)):
            self._load(p).wait()
            self.copy(5 * p, self.a_buf[p], self.a_buf[p], self.me).wait_recv()
            for j, (cx, cy) in enumerate(self.chips):
                kj = 2 * cx + cy
                self.pm[p][kj] = (self.pm[p][kj].astype(F32) + self.a_buf[p][kj].astype(F32)).astype(BF16)
                self._pair_sum(p, j).start()
            self.r_buf[p][...] = self.pm[p][self.k].astype(F32) + self.a_buf[p][self.k].astype(F32)

    def finish(self):
        for p, (_, _, _, axis, size) in enumerate(self.params):
            for j in range(3):
                self.copy(5 * p + 1 + j, self.b_buf[p].at[j], self.b_buf[p].at[j], self.me).wait_recv()
                self.r_buf[p][...] = self.r_buf[p][...] + self.b_buf[p][j].astype(F32)
            self._store(p).start()
            self._finished(p).start()
        for p, (_, _, _, axis, size) in enumerate(self.params):
            theirs = _half(self.g_out[p], axis, size, 1 - self.c)
            self.copy(5 * p + 4, theirs, theirs, self.me).wait_recv()
            self._store(p).wait()
            self._to_sibling(p).wait_send()
            for j in range(3):
                self._pair_sum(p, j).wait_send()
            self._finished(p).wait_send()


class _SmallSum:
    def __init__(self, rows):
        self.rows = rows
        self.scratch = [pltpu.VMEM((N_DEV, rows, LANES), F32),
                        pltpu.SemaphoreType.DMA((N_DEV - 1,)), pltpu.SemaphoreType.DMA((N_DEV - 1,))]
        self.out_shape = jax.ShapeDtypeStruct((rows, LANES), F32)
        self.spec_shape = (rows, LANES)

    def bind(self, src, dst, scratch):
        self.src, self.dst = src, dst
        self.buf, send_sems, recv_sems = scratch
        self.copy = _remote_copier(send_sems, recv_sems)
        self.x, self.y, self.c = lax.axis_index("x"), lax.axis_index("y"), lax.axis_index("c")

    def _send(self, f):
        fx, fy, fc = [(a, b, d) for a in (0, 1) for b in (0, 1) for d in (0, 1)][f]
        x, y, c = self.x, self.y, self.c
        peer = (1 - x if fx else x, 1 - y if fy else y, 1 - c if fc else c)
        return self.copy(f - 1, self.src, self.buf.at[f], peer)

    def start(self):
        for f in range(1, N_DEV):
            self._send(f).start()
        self.buf[0] = self.src[...]

    def finish(self):
        me = (self.x, self.y, self.c)
        for f in range(1, N_DEV):
            self.copy(f - 1, self.buf.at[f], self.buf.at[f], me).wait_recv()
        dev = 4 * self.x + 2 * self.y + self.c
        total = self.buf[dev]
        for d in range(1, N_DEV):
            total = total + self.buf[jnp.bitwise_xor(dev, d)]
        self.dst[...] = total
        for f in range(1, N_DEV):
            self._send(f).wait_send()


def _adamw_math(w, g, m, v):
    m = ADAM_B1 * m + (1.0 - ADAM_B1) * g
    v = ADAM_B2 * v + (1.0 - ADAM_B2) * (g * g)
    m_hat = m / (1.0 - ADAM_B1 ** ADAM_STEP)
    v_hat = v / (1.0 - ADAM_B2 ** ADAM_STEP)
    delta = -ADAM_LR * (m_hat / (jnp.sqrt(v_hat) + ADAM_EPS) + ADAM_WD * w)
    return delta, m, v


def _adamw_tiled(w, g, m, v, tm):
    rows, cols = w.shape

    def body(w_ref, g_ref, m_ref, v_ref, d_ref, nm_ref, nv_ref):
        d_ref[...], nm_ref[...], nv_ref[...] = _adamw_math(w_ref[...], g_ref[...], m_ref[...], v_ref[...])

    spec = _row_spec(tm, cols)
    return pl.pallas_call(
        body,
        name="adamw_w_in",
        grid=(rows // tm,),
        in_specs=[spec] * 4,
        out_specs=[spec] * 3,
        out_shape=[jax.ShapeDtypeStruct(w.shape, F32)] * 3,
        compiler_params=pltpu.CompilerParams(dimension_semantics=("parallel",), vmem_limit_bytes=VMEM_LIMIT),
    )(w, g, m, v)


def _adamw_many(ws, gs, ms, vs):
    n = len(ws)

    def body(*refs):
        ins, outs = refs[:4 * n], refs[4 * n:]
        for i in range(n):
            d, nm, nv = _adamw_math(ins[i][...], ins[n + i][...], ins[2 * n + i][...], ins[3 * n + i][...])
            outs[i][...] = d
            outs[n + i][...] = nm
            outs[2 * n + i][...] = nv

    vmem_spec = pl.BlockSpec(memory_space=pltpu.VMEM)
    shapes = [jax.ShapeDtypeStruct(w.shape, F32) for w in ws]
    out = pl.pallas_call(
        body,
        name="adamw_small",
        in_specs=[vmem_spec] * (4 * n),
        out_specs=[vmem_spec] * (3 * n),
        out_shape=shapes * 3,
        compiler_params=pltpu.CompilerParams(vmem_limit_bytes=VMEM_LIMIT),
    )(*ws, *gs, *ms, *vs)
    return out[:n], out[n:2 * n], out[2 * n:]


def _pack_rows(parts, rows, dtype):
    flat = jnp.concatenate([p.reshape(-1).astype(dtype) for p in parts])
    flat = jnp.concatenate([flat, jnp.zeros((rows * LANES - flat.shape[0],), dtype)])
    return flat.reshape(rows, LANES)


def _unpack_rows(packed, shapes):
    flat = packed.reshape(-1)
    out, off = [], 0
    for _, shp in shapes:
        n = int(np.prod(shp))
        out.append(flat[off:off + n].reshape(shp))
        off += n
    return out


def _rope_tables(s):
    half = QK_ROPE_DIM // 2
    inv_freq = np.float32(ROPE_THETA) ** (-np.arange(half, dtype=np.float32) / np.float32(half))
    ang = (np.arange(s, dtype=np.float32)[:, None] * inv_freq[None, :]).astype(np.float32)
    cos, sin = np.cos(ang.astype(np.float64)).astype(np.float32), np.sin(ang.astype(np.float64)).astype(np.float32)
    z16 = np.zeros((s, half), np.float32)
    z32 = np.zeros((s, HEAD_PAD - QK_NOPE_DIM - QK_ROPE_DIM), np.float32)
    z64 = np.zeros((s, QK_NOPE_DIM), np.float32)
    rc = np.concatenate([np.ones((s, QK_NOPE_DIM), np.float32), cos, cos, z32], axis=1)
    rsa = np.concatenate([z64, -sin, z16, z32], axis=1)
    rsb = np.concatenate([z64, z16, sin, z32], axis=1)
    return jnp.asarray(rc), jnp.asarray(rsa), jnp.asarray(rsb)


def kernel(x, norm_in, w_in, q_norm, w_uq, kv_norm, w_ukv, pool_w, pool_scale, w_branch_attn, w_branch_pool, w_out, norm_final, loss_target, m_norm_in, m_w_in, m_q_norm, m_w_uq, m_kv_norm, m_w_ukv, m_pool_w, m_pool_scale, m_w_branch_attn, m_w_branch_pool, m_w_out, m_norm_final, v_norm_in, v_w_in, v_q_norm, v_w_uq, v_kv_norm, v_w_ukv, v_pool_w, v_pool_scale, v_w_branch_attn, v_w_branch_pool, v_w_out, v_norm_final):
    s = x.shape[1]
    t_att, t_row = _tiles(s)
    x2 = x.reshape(s, D_MODEL)
    tgt = loss_target.reshape(s, D_MODEL)

    local = [w_in.T, w_uq.reshape(96, 768), w_ukv.reshape(64, 1024), w_branch_attn, w_branch_pool, w_out]
    local = [a.astype(BF16) for a in local]
    cx, cy = lax.axis_index("x"), lax.axis_index("y")
    others = [2 * ox + oy for ox, oy in _other_chips(cx, cy)]
    hn, z_sh, (w_in_t, w_uq_all, w_ukv_all) = _inproj_fwd(
        jnp.stack([2 * cx + cy, others[1], others[2], others[0]]).astype(jnp.int32), x2, norm_in.reshape(1, -1),
        local[:3], 4 * t_row)
    w_uq_f = w_uq_all.reshape(Q_LORA_RANK, MLA_HEADS, QK_NOPE_DIM + QK_ROPE_DIM)
    w_ukv_f = w_ukv_all.reshape(KV_LORA_RANK, MLA_HEADS, QK_NOPE_DIM + V_HEAD_DIM)
    hw = MLA_HEADS * HEAD_PAD
    wuq_p = jnp.pad(w_uq_f, ((0, 0), (0, 0), (0, HEAD_PAD - QK_NOPE_DIM - QK_ROPE_DIM))).reshape(Q_LORA_RANK, hw)
    wk_p = jnp.pad(w_ukv_f[:, :, :QK_NOPE_DIM], ((0, 0), (0, 0), (0, HEAD_PAD - QK_NOPE_DIM))).reshape(KV_LORA_RANK, hw)
    wv = w_ukv_f[:, :, QK_NOPE_DIM:].reshape(KV_LORA_RANK, MLA_WIDTH)
    rc, rsa, rsb = _rope_tables(s)
    g_in = norm_in.reshape(1, -1)
    g_q = q_norm.reshape(1, -1)
    g_kv = kv_norm.reshape(1, -1)
    g_f = norm_final.reshape(1, -1)
    ps = pool_scale.reshape(1, -1)
    pw_bf = pool_w.astype(BF16)

    q, k, v, q_t, v_t = _qkv_fwd(z_sh, g_q, g_kv, wuq_p, wk_p, wv, rc, rsa, rsb, 2 * t_row)
    o, lse, (w_ba_all, w_bp_all, w_out_all) = _attn_fwd(q_t, k, v_t, local[3:], t_att)
    w_out_f = w_out_all.reshape(D_MODEL, D_MODEL)

    (do, delta, dgattn, dgpool, dgmerge, ddc, dh, sq_err, d_w_out, d_w_ba, d_w_bp, d_pool_w, d_pool_scale,
     d_norm_final) = _mid(o, z_sh, x2, tgt, pw_bf, ps, w_ba_all, w_bp_all, w_out_f, g_f, t_row)

    late_grads = [d_w_ba, d_w_bp, d_w_out.reshape(N_CHIPS, 256, D_MODEL)]
    small_mid = dict(pool_scale=d_pool_scale, norm_final=d_norm_final, pool_w=d_pool_w, sq_err=sq_err)
    gs_mid = _pack_rows([small_mid[n] for n, _ in SMALL_MID], _small_rows(SMALL_MID), F32)
    dq, dk_t, dv_t, (g_w_ba, g_w_bp, g_w_out), g_small_mid = _attn_bwd(q, q_t, k, v, do, lse, delta, late_grads,
                                                                      gs_mid, t_att)
    g_pool_scale, g_norm_final, g_pool_w, sq_err_all = _unpack_rows(g_small_mid, SMALL_MID)
    dzq, dzkv, dzkr, d_wuq_p, d_wk_p, d_wv, d_q_norm, d_kv_norm = _qkv_bwd(
        dq, dk_t, dv_t, z_sh, g_q, g_kv, wuq_p, wk_p, wv, rc, rsa, rsb, 2 * t_row)
    grad_x, d_norm_in, dz_sh = _inproj_bwd_x(dzq, dzkv, dzkr, dgattn, ddc, dgpool, dgmerge, x2, dh, g_in, w_in_t,
                                             2 * t_row)

    d_w_uq = d_wuq_p.reshape(Q_LORA_RANK, MLA_HEADS, HEAD_PAD)[:, :, :QK_NOPE_DIM + QK_ROPE_DIM]
    d_w_ukv = jnp.concatenate([d_wk_p.reshape(KV_LORA_RANK, MLA_HEADS, HEAD_PAD)[:, :, :QK_NOPE_DIM],
                               d_wv.reshape(KV_LORA_RANK, MLA_HEADS, V_HEAD_DIM)], axis=2)
    small_late = dict(norm_in=d_norm_in, q_norm=d_q_norm, kv_norm=d_kv_norm)
    gs = _pack_rows([small_late[n] for n, _ in SMALL_LATE], _small_rows(SMALL_LATE), F32)
    order = jnp.stack(others + [2 * cx + cy]).astype(jnp.int32)
    g_w_in_t, g_w_uq, g_w_ukv, g_small = _inproj_bwd_w(
        order, dz_sh, hn, d_w_uq.reshape(N_CHIPS, 96, 768).astype(BF16),
        d_w_ukv.reshape(N_CHIPS, 64, 1024).astype(BF16), gs, 4 * t_row)
    g_norm_in, g_q_norm, g_kv_norm = _unpack_rows(g_small, SMALL_LATE)
    g_w_uq = g_w_uq.reshape(w_uq.shape)
    g_w_ukv = g_w_ukv.reshape(w_ukv.shape)

    dl_w_in, nm_w_in, nv_w_in = (a.T for a in _adamw_tiled(w_in.T, g_w_in_t, m_w_in.T, v_w_in.T, 152))

    def two_d(a):
        return a.reshape(1, -1) if a.ndim == 1 else a

    names = ["norm_in", "q_norm", "w_uq", "kv_norm", "w_ukv", "pool_w", "pool_scale", "w_branch_attn",
             "w_branch_pool", "w_out", "norm_final"]
    ws = dict(norm_in=norm_in, q_norm=q_norm, w_uq=w_uq, kv_norm=kv_norm, w_ukv=w_ukv, pool_w=pool_w,
              pool_scale=pool_scale, w_branch_attn=w_branch_attn, w_branch_pool=w_branch_pool, w_out=w_out,
              norm_final=norm_final)
    gsd = dict(norm_in=g_norm_in, q_norm=g_q_norm, w_uq=g_w_uq, kv_norm=g_kv_norm, w_ukv=g_w_ukv, pool_w=g_pool_w,
               pool_scale=g_pool_scale, w_branch_attn=g_w_ba, w_branch_pool=g_w_bp, w_out=g_w_out,
               norm_final=g_norm_final)
    msd = dict(norm_in=m_norm_in, q_norm=m_q_norm, w_uq=m_w_uq, kv_norm=m_kv_norm, w_ukv=m_w_ukv, pool_w=m_pool_w,
               pool_scale=m_pool_scale, w_branch_attn=m_w_branch_attn, w_branch_pool=m_w_branch_pool, w_out=m_w_out,
               norm_final=m_norm_final)
    vsd = dict(norm_in=v_norm_in, q_norm=v_q_norm, w_uq=v_w_uq, kv_norm=v_kv_norm, w_ukv=v_w_ukv, pool_w=v_pool_w,
               pool_scale=v_pool_scale, w_branch_attn=v_w_branch_attn, w_branch_pool=v_w_branch_pool, w_out=v_w_out,
               norm_final=v_norm_final)
    dls, nms, nvs = _adamw_many([two_d(ws[n]) for n in names], [two_d(gsd[n]) for n in names],
                                [two_d(msd[n]) for n in names], [two_d(vsd[n]) for n in names])

    grads = dict(gsd)
    grads["w_in"] = g_w_in_t.T
    delta_w = {n: d.reshape(ws[n].shape) for n, d in zip(names, dls)}
    new_m = {n: d.reshape(ws[n].shape) for n, d in zip(names, nms)}
    new_v = {n: d.reshape(ws[n].shape) for n, d in zip(names, nvs)}
    delta_w["w_in"], new_m["w_in"], new_v["w_in"] = dl_w_in, nm_w_in, nv_w_in
    ws["w_in"] = w_in

    order = ["norm_in", "w_in", "q_norm", "w_uq", "kv_norm", "w_ukv", "pool_w", "pool_scale", "w_branch_attn",
             "w_branch_pool", "w_out", "norm_final"]
    loss = 0.5 * jnp.sum(sq_err_all) / D_MODEL
    return (loss, grad_x.reshape(x.shape),
            *[grads[n].reshape(ws[n].shape) for n in order],
            *[delta_w[n] for n in order], *[new_m[n] for n in order], *[new_v[n] for n in order])
```

```python
import functools

import jax
import jax.numpy as jnp
import numpy as np
from jax import lax
from jax.experimental import pallas as pl
from jax.experimental.pallas import tpu as pltpu

F32 = jnp.float32
BF16 = jnp.bfloat16
MESH = pl.DeviceIdType.MESH

D_MODEL = 1024
CHUNK = 64
MLA_HEADS = 8
QK_NOPE_DIM = 64
QK_ROPE_DIM = 32
V_HEAD_DIM = 64
Q_LORA_RANK = 384
KV_LORA_RANK = 256
MLA_WIDTH = MLA_HEADS * V_HEAD_DIM
ROPE_THETA = 10000.0
POOL_WINDOWS = (2, 4, 8, 16)
POOL_WIDTH = 512
POOL_GROUP_DIM = 128
BRANCH_COLS = D_MODEL // 4
FWD_HEADS = 8
BWD_HEADS = 4
POOL_HALO = 16
EPS = 1e-6
IN_TOTAL = 4256
HEAD_PAD = 128
ATT_SCALE = (QK_NOPE_DIM + QK_ROPE_DIM) ** -0.5
ATT_SCALE_LOG2E = ATT_SCALE * 1.4426950408889634

ADAM_LR = 0.001
ADAM_B1 = 0.9
ADAM_B2 = 0.999
ADAM_EPS = 1e-08
ADAM_WD = 0.01
ADAM_STEP = 10

N_CHIPS = 4
N_DEV = 8
LANES = 128
VMEM_LIMIT = 60 * 1024 * 1024

IN_SEGMENTS = ((384, 384), (256, 256), (32, HEAD_PAD), (512, 512), (512, 512), (512, 512), (2048, 2048))
SHARD_COLS = IN_TOTAL // N_CHIPS
ZQ_COLS = slice(0, 384)
ZKV_COLS = slice(384, 640)
ZKR_TILE = slice(640, 768)


def _shard_pieces():
    bounds, off = [], 0
    for w, _ in IN_SEGMENTS:
        bounds.append((off, off + w))
        off += w
    out = []
    for j in range(N_CHIPS):
        lo, hi = SHARD_COLS * j, SHARD_COLS * (j + 1)
        out.append([(i, max(lo, a) - a, min(hi, b) - a, max(lo, a) - lo)
                    for i, (a, b) in enumerate(bounds) if max(lo, a) < min(hi, b)])
    return out


SHARD_PIECES = _shard_pieces()


def _segment(z_blocks, seg):
    parts = [z_blocks[j][:, col:col + hi - lo]
             for j, pieces in enumerate(SHARD_PIECES) for sg, lo, hi, col in pieces if sg == seg]
    return parts[0] if len(parts) == 1 else jnp.concatenate(parts, axis=1)

COMM_PARAMS = (
    ("w_in", SHARD_COLS, D_MODEL, 1, 512),
    ("w_uq", 96, 768, 0, 48),
    ("w_ukv", 64, 1024, 0, 32),
    ("w_branch_attn", 512, 256, 0, 256),
    ("w_branch_pool", 512, 256, 0, 256),
    ("w_out", 256, 1024, 0, 128),
)

SMALL_MID = (
    ("pool_scale", (512,)),
    ("norm_final", (1024,)),
    ("pool_w", (4, 128, 128)),
    ("sq_err", (8, 128)),
)
SMALL_LATE = (
    ("norm_in", (1024,)),
    ("q_norm", (384,)),
    ("kv_norm", (256,)),
)


def _small_rows(shapes):
    return -(-sum(int(np.prod(s)) for _, s in shapes) // (LANES * 8)) * 8


def _dot(a, b):
    return jnp.dot(a, b, preferred_element_type=F32)


def _dot_nt(a, b):
    return lax.dot_general(a, b, (((1,), (1,)), ((), ())), preferred_element_type=F32)


def _dot_tn(a, b):
    return lax.dot_general(a, b, (((0,), (0,)), ((), ())), preferred_element_type=F32)


def _sigmoid(x):
    return 1.0 / (1.0 + jnp.exp(-x))


def _colsum(x):
    return jnp.sum(x, axis=0, keepdims=True)


def _rms_fwd(x, g):
    r = lax.rsqrt(jnp.mean(x * x, axis=-1, keepdims=True) + EPS)
    xhat = x * r
    return xhat * g, xhat, r


def _rms_bwd(dy, xhat, r, g):
    dxhat = dy * g
    return r * (dxhat - xhat * jnp.mean(dxhat * xhat, axis=-1, keepdims=True))


def _rope(v, c, sa, sb):
    return v * c + pltpu.roll(v, 112, 1) * sa + pltpu.roll(v, 16, 1) * sb


def _unrope(d, c, sa, sb):
    return d * c + pltpu.roll(d * sa, 16, 1) + pltpu.roll(d * sb, 112, 1)


def _row_spec(tm, n):
    return pl.BlockSpec((tm, n), lambda i: (i, 0))


def _full_spec(shape):
    nd = len(shape)
    return pl.BlockSpec(shape, lambda i: (0,) * nd)


def _tiles(s):
    t_att = 512 if s >= 2048 else 128
    t_row = 256 if s >= 1024 else 128
    return t_att, t_row


def _inproj_fwd(order, x, norm_in, early_shards, tm):
    s = x.shape[0]
    n_tiles = s // tm
    gat = _Gather(COMM_PARAMS[:3])
    n_w = len(gat.params)
    arrival = (1, 2, 0)

    def body(order_ref, x_ref, g_ref, *rest):
        w_loc, (hn_ref, z_ref), w_all = rest[:n_w], rest[n_w:n_w + 2], rest[n_w + 2:2 * n_w + 2]
        w_vmem, hn_all, w_sem = rest[2 * n_w + 2:2 * n_w + 5]
        gat.bind(w_loc, w_all, rest[2 * n_w + 5:])
        ph, i = pl.program_id(0), pl.program_id(1)
        pl.when(jnp.logical_and(ph == 0, i == 0))(gat.start)

        @pl.when(jnp.logical_and(ph == 0, i == 0))
        def _():
            cp = pltpu.make_async_copy(w_loc[0], w_vmem, w_sem)
            cp.start()
            cp.wait()

        for f in range(3):
            @pl.when(jnp.logical_and(ph == f + 1, i == 0))
            def _(f=f):
                gat.relay_one(0, arrival[f])
                gat.await_one(0, arrival[f])
                cp = pltpu.make_async_copy(w_all[0].at[order_ref[ph]], w_vmem, w_sem)
                cp.start()
                cp.wait()

        rows = pl.ds(pl.multiple_of(i * tm, tm), tm)

        @pl.when(ph == 0)
        def _():
            hn, _, _ = _rms_fwd(x_ref[...], g_ref[...])
            hn = hn.astype(BF16)
            hn_ref[...] = hn
            hn_all[rows, :] = hn

        z_ref[0] = _dot_nt(hn_all[rows, :], w_vmem[...])

        @pl.when(jnp.logical_and(ph == N_CHIPS - 1, i == n_tiles - 1))
        def _():
            for p in range(1, n_w):
                for j in range(3):
                    gat.relay_one(p, j)
            for p in range(1, n_w):
                for j in range(3):
                    gat.await_one(p, j)
            gat.wait_sends()

    def tile_in_phase0(ph, i, order):
        return (jnp.where(ph == 0, i, n_tiles - 1), 0)

    any_spec = pl.BlockSpec(memory_space=pl.ANY)
    grid_spec = pltpu.PrefetchScalarGridSpec(
        num_scalar_prefetch=1,
        grid=(N_CHIPS, n_tiles),
        in_specs=[pl.BlockSpec((tm, D_MODEL), tile_in_phase0),
                  pl.BlockSpec((1, D_MODEL), lambda ph, i, order: (0, 0))] + [any_spec] * n_w,
        out_specs=[pl.BlockSpec((tm, D_MODEL), tile_in_phase0),
                   pl.BlockSpec((1, tm, SHARD_COLS), lambda ph, i, order: (order[ph], i, 0))] + [any_spec] * n_w,
        scratch_shapes=[pltpu.VMEM((SHARD_COLS, D_MODEL), BF16), pltpu.VMEM((s, D_MODEL), BF16),
                        pltpu.SemaphoreType.DMA] + gat.scratch,
    )
    out = pl.pallas_call(
        body,
        name="inproj_fwd",
        grid_spec=grid_spec,
        out_shape=[jax.ShapeDtypeStruct((s, D_MODEL), BF16), jax.ShapeDtypeStruct((N_CHIPS, s, SHARD_COLS), F32)]
        + gat.out_shape,
        compiler_params=pltpu.CompilerParams(dimension_semantics=("arbitrary", "arbitrary"),
                                             vmem_limit_bytes=VMEM_LIMIT),
    )(order, x, norm_in, *early_shards)
    return out[0], out[1], out[2:]


def _qkv_fwd(z_sh, q_norm, kv_norm, wuq_p, wk_p, wv, rc, rsa, rsb, tm):
    s = z_sh.shape[1]
    hw = MLA_HEADS * HEAD_PAD

    def body(z_ref, gq_ref, gkv_ref, wuq_ref, wk_ref, wv_ref, c_ref, sa_ref, sb_ref,
             q_ref, k_ref, v_ref, qt_ref, vt_ref):
        c, sa, sb = c_ref[...], sa_ref[...], sb_ref[...]
        z0 = z_ref[0]
        cq, _, _ = _rms_fwd(z0[:, ZQ_COLS], gq_ref[...])
        qf = _dot(cq.astype(BF16), wuq_ref[...])
        ckv, _, _ = _rms_fwd(z0[:, ZKV_COLS], gkv_ref[...])
        ckv = ckv.astype(BF16)
        kn = _dot(ckv, wk_ref[...])
        lane = lax.broadcasted_iota(jnp.int32, (tm, HEAD_PAD), 1)
        zkr = jnp.where(lane < QK_ROPE_DIM, z0[:, ZKR_TILE], 0.0)
        kr = _rope(pltpu.roll(zkr, 64, 1), c, sa, sb)
        for h in range(MLA_HEADS):
            cols = slice(h * HEAD_PAD, (h + 1) * HEAD_PAD)
            qh = _rope(qf[:, cols], c, sa, sb)
            q_ref[:, cols] = qh.astype(BF16)
            qt_ref[cols, :] = qh.T.astype(BF16)
            k_ref[:, cols] = (kn[:, cols] + kr).astype(BF16)
        vf = _dot(ckv, wv_ref[...])
        v_ref[...] = vf.astype(BF16)
        vt_ref[...] = vf.T.astype(BF16)

    return pl.pallas_call(
        body,
        name="qkv_fwd",
        grid=(s // tm,),
        in_specs=[
            pl.BlockSpec((1, tm, SHARD_COLS), lambda i: (0, i, 0)),
            _full_spec((1, Q_LORA_RANK)), _full_spec((1, KV_LORA_RANK)),
            _full_spec((Q_LORA_RANK, hw)), _full_spec((KV_LORA_RANK, hw)), _full_spec((KV_LORA_RANK, MLA_WIDTH)),
            _row_spec(tm, HEAD_PAD), _row_spec(tm, HEAD_PAD), _row_spec(tm, HEAD_PAD),
        ],
        out_specs=[_row_spec(tm, hw), _row_spec(tm, hw), _row_spec(tm, MLA_WIDTH),
                   pl.BlockSpec((hw, tm), lambda i: (0, i)), pl.BlockSpec((MLA_WIDTH, tm), lambda i: (0, i))],
        out_shape=[jax.ShapeDtypeStruct((s, hw), BF16), jax.ShapeDtypeStruct((s, hw), BF16),
                   jax.ShapeDtypeStruct((s, MLA_WIDTH), BF16),
                   jax.ShapeDtypeStruct((hw, s), BF16), jax.ShapeDtypeStruct((MLA_WIDTH, s), BF16)],
        compiler_params=pltpu.CompilerParams(dimension_semantics=("parallel",), vmem_limit_bytes=VMEM_LIMIT),
    )(z_sh, q_norm, kv_norm, wuq_p, wk_p, wv, rc, rsa, rsb)


def _chunk_mask(t, keys_on_rows):
    rows = lax.broadcasted_iota(jnp.int32, (t, t), 0) // CHUNK
    cols = lax.broadcasted_iota(jnp.int32, (t, t), 1) // CHUNK
    return rows <= cols if keys_on_rows else cols <= rows


def _attn_fwd(q_t, k, v_t, late_shards, t):
    s = k.shape[0]
    groups = MLA_HEADS // FWD_HEADS
    n_q = s // t
    gat = _Gather(COMM_PARAMS[3:])
    n_w = len(gat.params)

    def body(qt_ref, k_ref, k2_ref, vt_ref, *rest):
        w_in, (o_ref, lse_ref), w_out = rest[:n_w], rest[n_w:n_w + 2], rest[n_w + 2:2 * n_w + 2]
        gat.bind(w_in, w_out, rest[2 * n_w + 2:])
        i = pl.program_id(1)
        step_no = pl.program_id(0) * n_q + i
        pl.when(step_no == 0)(gat.start)
        pl.when(step_no == groups * n_q // 2)(gat.relay)
        mask = _chunk_mask(t, True)
        qcs = [slice(hh * HEAD_PAD, (hh + 1) * HEAD_PAD) for hh in range(FWD_HEADS)]
        vcs = [slice(hh * V_HEAD_DIM, (hh + 1) * V_HEAD_DIM) for hh in range(FWD_HEADS)]
        qts = [qt_ref[qc, :] for qc in qcs]

        def step(j, carry, masked):
            keys = pl.ds(pl.multiple_of(j * t, t), t)
            out = []
            for hh in range(FWD_HEADS):
                m, l, acc = carry[hh]
                sc = _dot(k_ref[keys, qcs[hh]], qts[hh])
                if masked:
                    sc = jnp.where(mask, sc, -jnp.inf)
                m_new = jnp.maximum(m, jnp.max(sc, axis=0, keepdims=True))
                alpha = jnp.exp2((m - m_new) * ATT_SCALE_LOG2E)
                p = jnp.exp2((_dot(k2_ref[keys, qcs[hh]], qts[hh]) - m_new) * ATT_SCALE_LOG2E)
                if masked:
                    p = jnp.where(mask, p, 0.0)
                l = alpha * l + jnp.sum(p, axis=0, keepdims=True)
                acc = alpha * acc + _dot(vt_ref[vcs[hh], keys], p.astype(BF16))
                out.append((m_new, l, acc))
            return tuple(out)

        one = (jnp.full((1, t), -jnp.inf, F32), jnp.zeros((1, t), F32), jnp.zeros((V_HEAD_DIM, t), F32))
        carry = lax.fori_loop(0, i, functools.partial(step, masked=False), (one,) * FWD_HEADS)
        carry = step(i, carry, True)
        o_ref[...] = jnp.concatenate([carry[hh][2] / carry[hh][1] for hh in range(FWD_HEADS)], axis=0).T
        for hh in range(FWD_HEADS):
            m, l, _ = carry[hh]
            lse_ref[:, qcs[hh]] = jnp.broadcast_to(m * ATT_SCALE_LOG2E + jnp.log2(l), (HEAD_PAD, t)).T
        pl.when(step_no == groups * n_q - 1)(gat.finish)

    any_spec = pl.BlockSpec(memory_space=pl.ANY)
    out = pl.pallas_call(
        body,
        name="attn_fwd",
        grid=(groups, n_q),
        in_specs=[
            pl.BlockSpec((FWD_HEADS * HEAD_PAD, t), lambda p, i: (p, i)),
            pl.BlockSpec((s, FWD_HEADS * HEAD_PAD), lambda p, i: (0, p), pipeline_mode=pl.Buffered(1)),
            pl.BlockSpec((s, FWD_HEADS * HEAD_PAD), lambda p, i: (0, p), pipeline_mode=pl.Buffered(1)),
            pl.BlockSpec((FWD_HEADS * V_HEAD_DIM, s), lambda p, i: (p, 0), pipeline_mode=pl.Buffered(1)),
        ] + [any_spec] * n_w,
        out_specs=[
            pl.BlockSpec((t, FWD_HEADS * V_HEAD_DIM), lambda p, i: (i, p)),
            pl.BlockSpec((t, FWD_HEADS * HEAD_PAD), lambda p, i: (i, p)),
        ] + [any_spec] * n_w,
        out_shape=[jax.ShapeDtypeStruct((s, MLA_WIDTH), F32), jax.ShapeDtypeStruct((s, MLA_HEADS * HEAD_PAD), F32)]
        + gat.out_shape,
        scratch_shapes=gat.scratch,
        compiler_params=pltpu.CompilerParams(dimension_semantics=("arbitrary", "arbitrary"),
                                             vmem_limit_bytes=VMEM_LIMIT),
    )(q_t, k, k, v_t, *late_shards)
    return out[0], out[1], out[2:]


def _mid(o, z_sh, x, target, pool_w, pool_scale, w_ba, w_bp, w_out, norm_final, tm):
    s = x.shape[0]
    n_tiles = s // tm
    halo_per_tile = tm // POOL_HALO

    def body(o_ref, z0_ref, z1_ref, z1h_ref, z2_ref, z3_ref, x_ref, t_ref, pw_ref, ps_ref, wba_ref, wbp_ref,
             wout_ref, gf_ref,
             do_ref, dl_ref, dga_ref, dgp_ref, dgm_ref, ddc_ref, dh_ref,
             loss_ref, dwout_out, dwba_out, dwbp_out, dpw_ref, dps_ref, dgf_ref,
             ubuf, dwout_ref, dwba_ref, dwbp_ref):
        i = pl.program_id(0)

        @pl.when(i == 0)
        def _():
            loss_ref[...] = jnp.zeros_like(loss_ref)
            dwout_ref[...] = jnp.zeros_like(dwout_ref)
            dwba_ref[...] = jnp.zeros_like(dwba_ref)
            dwbp_ref[...] = jnp.zeros_like(dwbp_ref)
            dpw_ref[...] = jnp.zeros_like(dpw_ref)
            dps_ref[...] = jnp.zeros_like(dps_ref)
            dgf_ref[...] = jnp.zeros_like(dgf_ref)

        zs = [z0_ref[0], z1_ref[0], z2_ref[0], z3_ref[0]]
        o = o_ref[...]
        ga = _segment(zs, 3)
        sga = _sigmoid(ga)
        silu_a = ga * sga
        y_attn = (o * silu_a).astype(BF16)

        ubuf[0:POOL_HALO, :] = jnp.where(i > 0, _segment([None, z1h_ref[0]], 4), 0.0)
        ubuf[POOL_HALO:, :] = _segment(zs, 4)
        row = lax.broadcasted_iota(jnp.int32, (tm, POOL_GROUP_DIM), 0) + i * tm
        ps = ps_ref[...]
        gp = _segment(zs, 5)
        sgp = _sigmoid(gp)
        silu_p = gp * sgp
        d_bf, dm, inv_cnt = [], [], []
        for g, w in enumerate(POOL_WINDOWS):
            cols = slice(g * POOL_GROUP_DIM, (g + 1) * POOL_GROUP_DIM)
            wsum = ubuf[POOL_HALO:, cols]
            for kk in range(1, w):
                wsum = wsum + ubuf[POOL_HALO - kk:POOL_HALO - kk + tm, cols]
            inv = 1.0 / jnp.minimum(row + 1, w).astype(F32)
            dg = (wsum * inv - ubuf[POOL_HALO:, cols]).astype(BF16)
            d_bf.append(dg)
            inv_cnt.append(inv)
            dm.append(_dot(dg, pw_ref[g]))
        dm = jnp.concatenate(dm, axis=1)
        yp = dm * ps
        y_pool = (yp * silu_p).astype(BF16)

        a = jnp.concatenate([_dot(y_attn, wba_ref[j]) for j in range(N_CHIPS)], axis=1)
        p = jnp.concatenate([_dot(y_pool, wbp_ref[j]) for j in range(N_CHIPS)], axis=1)
        gm = _segment(zs, 6)
        gate_a = _sigmoid(gm[:, :D_MODEL])
        gate_p = _sigmoid(gm[:, D_MODEL:])
        merged = (gate_a * a + gate_p * p).astype(BF16)
        h = x_ref[...] + _dot(merged, wout_ref[...])
        gf = gf_ref[...]
        y, xhat, r = _rms_fwd(h, gf)
        err = y - t_ref[...]
        e2 = err * err
        e2 = jnp.sum(e2.reshape(tm // 8, 8, D_MODEL), axis=0)
        acc = e2[:, 0:LANES]
        for cidx in range(1, D_MODEL // LANES):
            acc = acc + e2[:, cidx * LANES:(cidx + 1) * LANES]
        loss_ref[...] += acc

        dy = err * (1.0 / D_MODEL)
        dgf_ref[...] += _colsum(dy * xhat)
        dh = _rms_bwd(dy, xhat, r, gf)
        dh_ref[...] = dh
        dh_bf = dh.astype(BF16)
        dwout_ref[...] += _dot_tn(merged, dh_bf)
        dmerged = _dot_nt(dh_bf, wout_ref[...])
        da = (dmerged * gate_a).astype(BF16)
        dp = (dmerged * gate_p).astype(BF16)
        dgm_ref[:, :D_MODEL] = (dmerged * a * gate_a * (1.0 - gate_a)).astype(BF16)
        dgm_ref[:, D_MODEL:] = (dmerged * p * gate_p * (1.0 - gate_p)).astype(BF16)
        dy_attn = dy_pool = None
        for j in range(N_CHIPS):
            cols = slice(j * BRANCH_COLS, (j + 1) * BRANCH_COLS)
            dwba_ref[j] += _dot_tn(y_attn, da[:, cols])
            dwbp_ref[j] += _dot_tn(y_pool, dp[:, cols])
            pa = _dot_nt(da[:, cols], wba_ref[j])
            pp = _dot_nt(dp[:, cols], wbp_ref[j])
            dy_attn = pa if dy_attn is None else dy_attn + pa
            dy_pool = pp if dy_pool is None else dy_pool + pp

        do = dy_attn * silu_a
        do_ref[...] = do
        dga_ref[...] = (dy_attn * o * (sga * (1.0 + ga * (1.0 - sga)))).astype(BF16)
        doo = do * o
        for hd in range(MLA_HEADS):
            dl = jnp.sum(doo[:, hd * V_HEAD_DIM:(hd + 1) * V_HEAD_DIM], axis=1, keepdims=True)
            dl_ref[:, hd * HEAD_PAD:(hd + 1) * HEAD_PAD] = jnp.broadcast_to(dl, (tm, HEAD_PAD))

        dyp = dy_pool * silu_p
        dgp_ref[...] = (dy_pool * yp * (sgp * (1.0 + gp * (1.0 - sgp)))).astype(BF16)
        dps_ref[...] += _colsum(dyp * dm)
        dmm = (dyp * ps).astype(BF16)
        for g in range(len(POOL_WINDOWS)):
            cols = slice(g * POOL_GROUP_DIM, (g + 1) * POOL_GROUP_DIM)
            dpw_ref[g] += _dot_tn(d_bf[g], dmm[:, cols])
            ddc_ref[:, cols] = _dot_nt(dmm[:, cols], pw_ref[g]) * inv_cnt[g]

        @pl.when(i == n_tiles - 1)
        def _():
            dwout_out[...] = dwout_ref[...].astype(BF16)
            dwba_out[...] = dwba_ref[...].astype(BF16)
            dwbp_out[...] = dwbp_ref[...].astype(BF16)

    row_in = lambda n: _row_spec(tm, n)
    in_specs = [
        row_in(MLA_WIDTH),
        pl.BlockSpec((1, tm, SHARD_COLS), lambda i: (0, i, 0)), pl.BlockSpec((1, tm, SHARD_COLS), lambda i: (1, i, 0)),
        pl.BlockSpec((1, POOL_HALO, SHARD_COLS), lambda i: (1, jnp.maximum(i * halo_per_tile - 1, 0), 0)),
        pl.BlockSpec((1, tm, SHARD_COLS), lambda i: (2, i, 0)), pl.BlockSpec((1, tm, SHARD_COLS), lambda i: (3, i, 0)),
        row_in(D_MODEL), row_in(D_MODEL),
        _full_spec((4, POOL_GROUP_DIM, POOL_GROUP_DIM)), _full_spec((1, POOL_WIDTH)),
        _full_spec((N_CHIPS, MLA_WIDTH, BRANCH_COLS)), _full_spec((N_CHIPS, POOL_WIDTH, BRANCH_COLS)),
        _full_spec((D_MODEL, D_MODEL)), _full_spec((1, D_MODEL)),
    ]
    out_shape = [
        jax.ShapeDtypeStruct((s, MLA_WIDTH), F32),
        jax.ShapeDtypeStruct((s, MLA_HEADS * HEAD_PAD), F32),
        jax.ShapeDtypeStruct((s, MLA_WIDTH), BF16),
        jax.ShapeDtypeStruct((s, POOL_WIDTH), BF16),
        jax.ShapeDtypeStruct((s, 2 * D_MODEL), BF16),
        jax.ShapeDtypeStruct((s, POOL_WIDTH), F32),
        jax.ShapeDtypeStruct((s, D_MODEL), F32),
        jax.ShapeDtypeStruct((8, LANES), F32),
        jax.ShapeDtypeStruct((D_MODEL, D_MODEL), BF16),
        jax.ShapeDtypeStruct((N_CHIPS, MLA_WIDTH, BRANCH_COLS), BF16),
        jax.ShapeDtypeStruct((N_CHIPS, POOL_WIDTH, BRANCH_COLS), BF16),
        jax.ShapeDtypeStruct((4, POOL_GROUP_DIM, POOL_GROUP_DIM), F32),
        jax.ShapeDtypeStruct((1, POOL_WIDTH), F32),
        jax.ShapeDtypeStruct((1, D_MODEL), F32),
    ]
    out_specs = [
        row_in(MLA_WIDTH), row_in(MLA_HEADS * HEAD_PAD), row_in(MLA_WIDTH), row_in(POOL_WIDTH),
        row_in(2 * D_MODEL), row_in(POOL_WIDTH), row_in(D_MODEL),
        _full_spec((8, LANES)), _full_spec((D_MODEL, D_MODEL)), _full_spec((N_CHIPS, MLA_WIDTH, BRANCH_COLS)),
        _full_spec((N_CHIPS, POOL_WIDTH, BRANCH_COLS)), _full_spec((4, POOL_GROUP_DIM, POOL_GROUP_DIM)),
        _full_spec((1, POOL_WIDTH)), _full_spec((1, D_MODEL)),
    ]
    return pl.pallas_call(
        body,
        name="mid",
        grid=(n_tiles,),
        in_specs=in_specs,
        out_specs=out_specs,
        out_shape=out_shape,
        scratch_shapes=[
            pltpu.VMEM((tm + POOL_HALO, POOL_WIDTH), F32),
            pltpu.VMEM((D_MODEL, D_MODEL), F32),
            pltpu.VMEM((N_CHIPS, MLA_WIDTH, BRANCH_COLS), F32),
            pltpu.VMEM((N_CHIPS, POOL_WIDTH, BRANCH_COLS), F32),
        ],
        compiler_params=pltpu.CompilerParams(dimension_semantics=("arbitrary",), vmem_limit_bytes=VMEM_LIMIT),
    )(o, z_sh, z_sh, z_sh, z_sh, z_sh, x, target, pool_w, pool_scale, w_ba, w_bp, w_out, norm_final)


def _attn_bwd(q, q_t, k, v, do, lse, delta, late_grads, gs_mid, t):
    s = q.shape[0]
    groups = MLA_HEADS // BWD_HEADS
    n_q = s // t
    red = _Reduce(COMM_PARAMS[3:])
    n_w = len(red.params)
    small = _SmallSum(gs_mid.shape[0])
    n_red = len(red.scratch)

    def body(q_ref, qt_ref, do_ref, lse_ref, dl_ref, k_ref, v_ref, *rest):
        g_in, gs_ref = rest[:n_w], rest[n_w]
        (dq_ref, dk_ref, dv_ref), g_out, gsum_ref = rest[n_w + 1:n_w + 4], rest[n_w + 4:2 * n_w + 4], rest[2 * n_w + 4]
        scratch = rest[2 * n_w + 5:]
        red.bind(g_in, g_out, scratch[:n_red])
        small.bind(gs_ref, gsum_ref, scratch[n_red:])
        i = pl.program_id(1)
        step_no = pl.program_id(0) * n_q + i

        @pl.when(step_no == 0)
        def _():
            red.start()
            small.start()

        pl.when(step_no == groups * n_q // 2)(red.exchange)

        @pl.when(i == 0)
        def _():
            dk_ref[...] = jnp.zeros_like(dk_ref)
            dv_ref[...] = jnp.zeros_like(dv_ref)

        mask = _chunk_mask(t, False)
        qcs = [slice(hh * HEAD_PAD, (hh + 1) * HEAD_PAD) for hh in range(BWD_HEADS)]
        vcs = [slice(hh * V_HEAD_DIM, (hh + 1) * V_HEAD_DIM) for hh in range(BWD_HEADS)]
        qhs = [q_ref[:, qc] for qc in qcs]
        qts = [qt_ref[qc, :] for qc in qcs]
        dohs = [do_ref[:, vc].astype(BF16) for vc in vcs]
        do_t = do_ref[...].T.astype(BF16)
        dots = [do_t[vc, :] for vc in vcs]
        lses = [jnp.tile(lse_ref[:, qc], (1, t // HEAD_PAD)) for qc in qcs]
        dls = [jnp.tile(dl_ref[:, qc], (1, t // HEAD_PAD)) for qc in qcs]

        def step(j, dqs, masked):
            keys = pl.ds(pl.multiple_of(j * t, t), t)
            out = []
            for hh in range(BWD_HEADS):
                kj = k_ref[keys, qcs[hh]]
                vj = v_ref[keys, vcs[hh]]
                p = jnp.exp2(_dot_nt(qhs[hh], kj) * ATT_SCALE_LOG2E - lses[hh])
                if masked:
                    p = jnp.where(mask, p, 0.0)
                ds = (p * (_dot_nt(dohs[hh], vj) - dls[hh])).astype(BF16)
                dv_ref[vcs[hh], keys] += _dot(dots[hh], p.astype(BF16))
                dk_ref[qcs[hh], keys] += _dot(qts[hh], ds) * ATT_SCALE
                out.append(dqs[hh] + _dot(ds, kj))
            return tuple(out)

        zero = jnp.zeros((t, HEAD_PAD), F32)
        dqs = lax.fori_loop(0, i, functools.partial(step, masked=False), (zero,) * BWD_HEADS)
        dqs = step(i, dqs, True)
        for hh in range(BWD_HEADS):
            dq_ref[:, qcs[hh]] = dqs[hh] * ATT_SCALE

        @pl.when(step_no == groups * n_q - 1)
        def _():
            red.finish()
            small.finish()

    hw = MLA_HEADS * HEAD_PAD
    any_spec = pl.BlockSpec(memory_space=pl.ANY)
    out = pl.pallas_call(
        body,
        name="attn_bwd",
        grid=(groups, n_q),
        in_specs=[
            pl.BlockSpec((t, BWD_HEADS * HEAD_PAD), lambda p, i: (i, p)),
            pl.BlockSpec((BWD_HEADS * HEAD_PAD, t), lambda p, i: (p, i)),
            pl.BlockSpec((t, BWD_HEADS * V_HEAD_DIM), lambda p, i: (i, p)),
            pl.BlockSpec((t, BWD_HEADS * HEAD_PAD), lambda p, i: (i, p)),
            pl.BlockSpec((t, BWD_HEADS * HEAD_PAD), lambda p, i: (i, p)),
            pl.BlockSpec((s, BWD_HEADS * HEAD_PAD), lambda p, i: (0, p), pipeline_mode=pl.Buffered(1)),
            pl.BlockSpec((s, BWD_HEADS * V_HEAD_DIM), lambda p, i: (0, p), pipeline_mode=pl.Buffered(1)),
        ] + [any_spec] * n_w + [pl.BlockSpec(small.spec_shape, lambda p, i: (0, 0))],
        out_specs=[
            pl.BlockSpec((t, BWD_HEADS * HEAD_PAD), lambda p, i: (i, p)),
            pl.BlockSpec((BWD_HEADS * HEAD_PAD, s), lambda p, i: (p, 0)),
            pl.BlockSpec((BWD_HEADS * V_HEAD_DIM, s), lambda p, i: (p, 0)),
        ] + [any_spec] * n_w + [pl.BlockSpec(small.spec_shape, lambda p, i: (0, 0))],
        out_shape=[jax.ShapeDtypeStruct((s, hw), F32), jax.ShapeDtypeStruct((hw, s), F32),
                   jax.ShapeDtypeStruct((MLA_WIDTH, s), F32)] + red.out_shape + [small.out_shape],
        scratch_shapes=red.scratch + small.scratch,
        compiler_params=pltpu.CompilerParams(dimension_semantics=("arbitrary", "arbitrary"),
                                             vmem_limit_bytes=VMEM_LIMIT),
    )(q, q_t, do, lse, delta, k, v, *late_grads, gs_mid)
    return out[0], out[1], out[2], out[3:3 + n_w], out[3 + n_w]


def _qkv_bwd(dq, dk_t, dv_t, z_sh, q_norm, kv_norm, wuq_p, wk_p, wv, rc, rsa, rsb, tm):
    s = z_sh.shape[1]
    hw = MLA_HEADS * HEAD_PAD

    def body(dq_ref, dk_ref, dv_ref, z_ref, gq_ref, gkv_ref, wuq_ref, wk_ref, wv_ref,
             c_ref, sa_ref, sb_ref,
             dzq_ref, dzkv_ref, dzkr_ref, dwuq_ref, dwk_ref, dwv_ref, dgq_ref, dgkv_ref):
        i = pl.program_id(0)

        @pl.when(i == 0)
        def _():
            dwuq_ref[...] = jnp.zeros_like(dwuq_ref)
            dwk_ref[...] = jnp.zeros_like(dwk_ref)
            dwv_ref[...] = jnp.zeros_like(dwv_ref)
            dgq_ref[...] = jnp.zeros_like(dgq_ref)
            dgkv_ref[...] = jnp.zeros_like(dgkv_ref)

        c, sa, sb = c_ref[...], sa_ref[...], sb_ref[...]
        gq, gkv = gq_ref[...], gkv_ref[...]

        z0 = z_ref[0]
        cq, xq, rq = _rms_fwd(z0[:, ZQ_COLS], gq)
        dqp = jnp.concatenate(
            [_unrope(dq_ref[:, h * HEAD_PAD:(h + 1) * HEAD_PAD], c, sa, sb) for h in range(MLA_HEADS)],
            axis=1).astype(BF16)
        dwuq_ref[...] += _dot_tn(cq.astype(BF16), dqp)
        dcq = _dot_nt(dqp, wuq_ref[...])
        dgq_ref[...] += _colsum(dcq * xq)
        dzq_ref[...] = _rms_bwd(dcq, xq, rq, gq).astype(BF16)

        ckv, xkv, rkv = _rms_fwd(z0[:, ZKV_COLS], gkv)
        ckv = ckv.astype(BF16)
        dkf = dk_ref[...].T
        dk_bf = dkf.astype(BF16)
        dv_bf = dv_ref[...].T.astype(BF16)
        dwk_ref[...] += _dot_tn(ckv, dk_bf)
        dwv_ref[...] += _dot_tn(ckv, dv_bf)
        dckv = _dot_nt(dk_bf, wk_ref[...]) + _dot_nt(dv_bf, wv_ref[...])
        dgkv_ref[...] += _colsum(dckv * xkv)
        dzkv_ref[...] = _rms_bwd(dckv, xkv, rkv, gkv).astype(BF16)

        dkr = dkf[:, 0:HEAD_PAD]
        for h in range(1, MLA_HEADS):
            dkr = dkr + dkf[:, h * HEAD_PAD:(h + 1) * HEAD_PAD]
        dkr = pltpu.roll(_unrope(dkr, c, sa, sb), 64, 1)
        lane = lax.broadcasted_iota(jnp.int32, (tm, HEAD_PAD), 1)
        dzkr_ref[...] = jnp.where(lane < QK_ROPE_DIM, dkr, 0.0).astype(BF16)

    return pl.pallas_call(
        body,
        name="qkv_bwd",
        grid=(s // tm,),
        in_specs=[
            _row_spec(tm, hw), pl.BlockSpec((hw, tm), lambda i: (0, i)), pl.BlockSpec((MLA_WIDTH, tm), lambda i: (0, i)),
            pl.BlockSpec((1, tm, SHARD_COLS), lambda i: (0, i, 0)),
            _full_spec((1, Q_LORA_RANK)), _full_spec((1, KV_LORA_RANK)),
            _full_spec((Q_LORA_RANK, hw)), _full_spec((KV_LORA_RANK, hw)), _full_spec((KV_LORA_RANK, MLA_WIDTH)),
            _row_spec(tm, HEAD_PAD), _row_spec(tm, HEAD_PAD), _row_spec(tm, HEAD_PAD),
        ],
        out_specs=[
            _row_spec(tm, Q_LORA_RANK), _row_spec(tm, KV_LORA_RANK), _row_spec(tm, HEAD_PAD),
            _full_spec((Q_LORA_RANK, hw)), _full_spec((KV_LORA_RANK, hw)), _full_spec((KV_LORA_RANK, MLA_WIDTH)),
            _full_spec((1, Q_LORA_RANK)), _full_spec((1, KV_LORA_RANK)),
        ],
        out_shape=[
            jax.ShapeDtypeStruct((s, Q_LORA_RANK), BF16), jax.ShapeDtypeStruct((s, KV_LORA_RANK), BF16),
            jax.ShapeDtypeStruct((s, HEAD_PAD), BF16),
            jax.ShapeDtypeStruct((Q_LORA_RANK, hw), F32), jax.ShapeDtypeStruct((KV_LORA_RANK, hw), F32),
            jax.ShapeDtypeStruct((KV_LORA_RANK, MLA_WIDTH), F32),
            jax.ShapeDtypeStruct((1, Q_LORA_RANK), F32), jax.ShapeDtypeStruct((1, KV_LORA_RANK), F32),
        ],
        compiler_params=pltpu.CompilerParams(dimension_semantics=("arbitrary",), vmem_limit_bytes=VMEM_LIMIT),
    )(dq, dk_t, dv_t, z_sh, q_norm, kv_norm, wuq_p, wk_p, wv, rc, rsa, rsb)


def _inproj_bwd_x(dzq, dzkv, dzkr, dgattn, ddc, dgpool, dgmerge, x, dh, norm_in, w_in_t, tm):
    s = x.shape[0]
    n_tiles = s // tm
    halo_per_tile = tm // POOL_HALO
    n_halo = s // POOL_HALO
    u_seg = 4

    def body(dzq_ref, dzkv_ref, dzkr_ref, dga_ref, ddc_ref, ddn_ref, dgp_ref, dgm_ref, x_ref, dh_ref,
             g_ref, w_hbm, gx_ref, dgin_ref, dzs_ref, w_vmem, dbuf, sem):
        i = pl.program_id(0)

        @pl.when(i == 0)
        def _():
            cp = pltpu.make_async_copy(w_hbm, w_vmem, sem)
            cp.start()
            dgin_ref[...] = jnp.zeros_like(dgin_ref)
            cp.wait()

        dbuf[0:tm, :] = ddc_ref[...]
        dbuf[tm:, :] = jnp.where(i < n_tiles - 1, ddn_ref[...], 0.0)
        row = lax.broadcasted_iota(jnp.int32, (tm, POOL_GROUP_DIM), 0) + i * tm
        du = []
        for g, w in enumerate(POOL_WINDOWS):
            cols = slice(g * POOL_GROUP_DIM, (g + 1) * POOL_GROUP_DIM)
            fsum = dbuf[0:tm, cols]
            for kk in range(1, w):
                fsum = fsum + dbuf[kk:kk + tm, cols]
            du.append(fsum - dbuf[0:tm, cols] * jnp.minimum(row + 1, w).astype(F32))
        du = jnp.concatenate(du, axis=1).astype(BF16)

        dz = [dzq_ref[...], dzkv_ref[...], dzkr_ref[...], dga_ref[...], du, dgp_ref[...], dgm_ref[...]]
        dz = jnp.concatenate([d[:, :w] for d, (w, _) in zip(dz, IN_SEGMENTS)], axis=1)
        for j in range(N_CHIPS):
            dzs_ref[j] = dz[:, j * SHARD_COLS:(j + 1) * SHARD_COLS].T
        dhn = _dot(dz, w_vmem[...])

        g = g_ref[...]
        _, xhat, r = _rms_fwd(x_ref[...], g)
        dgin_ref[...] += _colsum(dhn * xhat)
        gx_ref[...] = dh_ref[...] + _rms_bwd(dhn, xhat, r, g)

    any_spec = pl.BlockSpec(memory_space=pl.ANY)
    seg_w = [wide for _, wide in IN_SEGMENTS]
    return pl.pallas_call(
        body,
        name="inproj_bwd_x",
        grid=(n_tiles,),
        in_specs=[
            _row_spec(tm, seg_w[0]), _row_spec(tm, seg_w[1]), _row_spec(tm, seg_w[2]),
            _row_spec(tm, seg_w[3]), _row_spec(tm, seg_w[u_seg]),
            pl.BlockSpec((POOL_HALO, POOL_WIDTH), lambda i: (jnp.minimum((i + 1) * halo_per_tile, n_halo - 1), 0)),
            _row_spec(tm, seg_w[5]), _row_spec(tm, seg_w[6]),
            _row_spec(tm, D_MODEL), _row_spec(tm, D_MODEL),
            _full_spec((1, D_MODEL)), any_spec,
        ],
        out_specs=[_row_spec(tm, D_MODEL), _full_spec((1, D_MODEL)),
                   pl.BlockSpec((N_CHIPS, SHARD_COLS, tm), lambda i: (0, 0, i))],
        out_shape=[jax.ShapeDtypeStruct((s, D_MODEL), F32), jax.ShapeDtypeStruct((1, D_MODEL), F32),
                   jax.ShapeDtypeStruct((N_CHIPS, SHARD_COLS, s), BF16)],
        scratch_shapes=[
            pltpu.VMEM((IN_TOTAL, D_MODEL), BF16),
            pltpu.VMEM((tm + POOL_HALO, POOL_WIDTH), F32),
            pltpu.SemaphoreType.DMA,
        ],
        compiler_params=pltpu.CompilerParams(dimension_semantics=("arbitrary",), vmem_limit_bytes=VMEM_LIMIT),
    )(dzq, dzkv, dzkr, dgattn, ddc, ddc, dgpool, dgmerge, x, dh, norm_in, w_in_t.reshape(IN_TOTAL, D_MODEL))


def _inproj_bwd_w(order, dz_sh, hn, g_uq, g_ukv, gs, tm):
    s = hn.shape[0]
    n_tiles = s // tm
    hc = D_MODEL // 2
    red = _Reduce(COMM_PARAMS[1:3])
    small = _SmallSum(gs.shape[0])
    n_red = len(red.scratch)

    def body(order_ref, dz_ref, hn_ref, guq_hbm, gukv_hbm, gs_ref, gw_hbm, guq_out, gukv_out, gsum_ref,
             acc, pm_w, a_w, b_w, r_w, w_send, w_recv, w_local, *more_scratch):
        ph, i = pl.program_id(0), pl.program_id(1)
        x, y, c = lax.axis_index("x"), lax.axis_index("y"), lax.axis_index("c")
        k = 2 * x + y
        me, sibling = (x, y, c), (x, y, 1 - c)
        chips = _other_chips(x, y)
        shard_of_phase = [2 * cx + cy for cx, cy in chips] + [k]
        copy = _remote_copier(w_send, w_recv)
        red.bind([guq_hbm, gukv_hbm], [guq_out, gukv_out], more_scratch[:n_red])
        small.bind(gs_ref, gsum_ref, more_scratch[n_red:])
        mine = pl.ds(pl.multiple_of(c * hc, hc), hc)
        theirs = pl.ds(pl.multiple_of((1 - c) * hc, hc), hc)

        def to_sibling(f):
            j = shard_of_phase[f]
            return copy(f, pm_w.at[j, 1 - c], a_w.at[j], sibling)

        def pair_sum(f):
            cx, cy = chips[f]
            return copy(4 + f, pm_w.at[shard_of_phase[f], c], b_w.at[f], (cx, cy, c))

        def finished():
            return copy(7, r_w, gw_hbm.at[:, mine], sibling)

        @pl.when(jnp.logical_and(ph == 0, i == 0))
        def _():
            red.start()
            small.start()

        part = _dot(dz_ref[0], hn_ref[...])

        @pl.when(i == 0)
        def _():
            acc[...] = part

        @pl.when(i > 0)
        def _():
            acc[...] += part

        for f in range(3):
            @pl.when(jnp.logical_and(ph == f + 1, i == 0))
            def _(f=f):
                j = shard_of_phase[f]
                copy(f, a_w.at[j], a_w.at[j], me).wait_recv()
                pm_w[j, c] = (pm_w[j, c].astype(F32) + a_w[j].astype(F32)).astype(BF16)
                pair_sum(f).start()
                if f == 0:
                    red.exchange()

        for f in range(4):
            @pl.when(jnp.logical_and(ph == f, i == n_tiles - 1))
            def _(f=f):
                j = shard_of_phase[f]
                pm_w[j, 0] = acc[:, :hc].astype(BF16)
                pm_w[j, 1] = acc[:, hc:].astype(BF16)
                to_sibling(f).start()
                if f < 3:
                    return
                copy(3, a_w.at[k], a_w.at[k], me).wait_recv()
                r_w[...] = pm_w[k, c].astype(F32) + a_w[k].astype(F32)
                for g in range(3):
                    copy(4 + g, b_w.at[g], b_w.at[g], me).wait_recv()
                    r_w[...] = r_w[...] + b_w[g].astype(F32)
                store = pltpu.make_async_copy(r_w, gw_hbm.at[:, mine], w_local)
                store.start()
                finished().start()
                red.finish()
                small.finish()
                copy(7, gw_hbm.at[:, theirs], gw_hbm.at[:, theirs], me).wait_recv()
                store.wait()
                for g in range(4):
                    to_sibling(g).wait_send()
                for g in range(3):
                    pair_sum(g).wait_send()
                finished().wait_send()

    any_spec = pl.BlockSpec(memory_space=pl.ANY)
    n_sem = 8
    grid_spec = pltpu.PrefetchScalarGridSpec(
        num_scalar_prefetch=1,
        grid=(N_CHIPS, n_tiles),
        in_specs=[
            pl.BlockSpec((1, SHARD_COLS, tm), lambda ph, i, order: (order[ph], 0, i)),
            pl.BlockSpec((tm, D_MODEL), lambda ph, i, order: (i, 0)),
            any_spec, any_spec,
            pl.BlockSpec(small.spec_shape, lambda ph, i, order: (0, 0)),
        ],
        out_specs=[any_spec, any_spec, any_spec, pl.BlockSpec(small.spec_shape, lambda ph, i, order: (0, 0))],
        scratch_shapes=[
            pltpu.VMEM((SHARD_COLS, D_MODEL), F32),
            pltpu.VMEM((N_CHIPS, 2, SHARD_COLS, hc), BF16),
            pltpu.VMEM((N_CHIPS, SHARD_COLS, hc), BF16),
            pltpu.VMEM((3, SHARD_COLS, hc), BF16),
            pltpu.VMEM((SHARD_COLS, hc), F32),
            pltpu.SemaphoreType.DMA((n_sem,)), pltpu.SemaphoreType.DMA((n_sem,)), pltpu.SemaphoreType.DMA,
        ] + red.scratch + small.scratch,
    )
    out = pl.pallas_call(
        body,
        name="inproj_bwd_w",
        grid_spec=grid_spec,
        out_shape=[jax.ShapeDtypeStruct((SHARD_COLS, D_MODEL), F32)] + red.out_shape
        + [small.out_shape],
        compiler_params=pltpu.CompilerParams(dimension_semantics=("arbitrary", "arbitrary"),
                                             vmem_limit_bytes=VMEM_LIMIT),
    )(order, dz_sh, hn, g_uq, g_ukv, gs)
    return out[0], out[1], out[2], out[3]


def _other_chips(x, y):
    return ((1 - x, 1 - y), (1 - x, y), (x, 1 - y))


def _half(ref, axis, size, c, lead=()):
    window = pl.ds(pl.multiple_of(c * size, size), size)
    if axis == 0:
        return ref.at[(*lead, window, slice(None))]
    return ref.at[(*lead, slice(None), window)]


def _half_shape(rows, cols, axis, size):
    return (size, cols) if axis == 0 else (rows, size)


def _remote_copier(send_sems, recv_sems):
    def copy(sem, src, dst, to):
        return pltpu.make_async_remote_copy(src_ref=src, dst_ref=dst, send_sem=send_sems.at[sem],
                                            recv_sem=recv_sems.at[sem], device_id=to, device_id_type=MESH)
    return copy


class _Gather:
    def __init__(self, params):
        self.params = params
        n = len(params)
        self.scratch = [pltpu.SemaphoreType.DMA((6 * n,)), pltpu.SemaphoreType.DMA((6 * n,)),
                        pltpu.SemaphoreType.DMA((n,))]
        self.out_shape = [jax.ShapeDtypeStruct((N_CHIPS, r, cc), BF16) for _, r, cc, _, _ in params]

    def bind(self, ins, outs, scratch):
        self.ins, self.outs = ins, outs
        send_sems, recv_sems, self.local_sems = scratch
        self.copy = _remote_copier(send_sems, recv_sems)
        self.x, self.y, self.c = lax.axis_index("x"), lax.axis_index("y"), lax.axis_index("c")
        self.k = 2 * self.x + self.y
        self.chips = _other_chips(self.x, self.y)

    def _local(self, p):
        return pltpu.make_async_copy(self.ins[p], self.outs[p].at[self.k], self.local_sems.at[p])

    def _first(self, p, j):
        _, _, _, axis, size = self.params[p]
        cx, cy = self.chips[j]
        return self.copy(6 * p + j, _half(self.ins[p], axis, size, self.c),
                         _half(self.outs[p], axis, size, self.c, (self.k,)), (cx, cy, self.c))

    def _relay(self, p, j, half_of):
        _, _, _, axis, size = self.params[p]
        cx, cy = self.chips[j]
        block = _half(self.outs[p], axis, size, half_of, (2 * cx + cy,))
        return self.copy(6 * p + 3 + j, block, block, (self.x, self.y, 1 - self.c))

    def start(self):
        for p in range(len(self.params)):
            self._local(p).start()
            for j in (1, 2, 0):
                self._first(p, j).start()

    def relay_one(self, p, j):
        _, _, _, axis, size = self.params[p]
        cx, cy = self.chips[j]
        landed = _half(self.outs[p], axis, size, self.c, (2 * cx + cy,))
        self.copy(6 * p + j, landed, landed, (self.x, self.y, self.c)).wait_recv()
        self._relay(p, j, self.c).start()

    def await_one(self, p, j):
        self._relay(p, j, 1 - self.c).wait_recv()

    def wait_sends(self):
        for p in range(len(self.params)):
            for j in range(3):
                self._first(p, j).wait_send()
                self._relay(p, j, self.c).wait_send()
            self._local(p).wait()

    def relay(self):
        for j in range(3):
            for p in range(len(self.params)):
                self.relay_one(p, j)

    def finish(self):
        for j in range(3):
            for p in range(len(self.params)):
                self.await_one(p, j)
        self.wait_sends()


class _Reduce:
    def __init__(self, params):
        self.params = params
        n = len(params)
        halves = [_half_shape(r, cc, axis, size) for _, r, cc, axis, size in params]
        self.scratch = ([pltpu.VMEM((N_CHIPS, *h), BF16) for h in halves]
                        + [pltpu.VMEM((N_CHIPS, *h), BF16) for h in halves]
                        + [pltpu.VMEM((3, *h), BF16) for h in halves]
                        + [pltpu.VMEM(h, F32) for h in halves]
                        + [pltpu.SemaphoreType.DMA((5 * n,)), pltpu.SemaphoreType.DMA((5 * n,)),
                           pltpu.SemaphoreType.DMA((2 * n,))])
        self.out_shape = [jax.ShapeDtypeStruct((r, cc), F32) for _, r, cc, _, _ in params]

    def bind(self, g_in, g_out, scratch):
        n = len(self.params)
        self.g_in, self.g_out = g_in, g_out
        self.pm, self.a_buf = scratch[0:n], scratch[n:2 * n]
        self.b_buf, self.r_buf = scratch[2 * n:3 * n], scratch[3 * n:4 * n]
        send_sems, recv_sems, self.local_sems = scratch[4 * n:]
        self.copy = _remote_copier(send_sems, recv_sems)
        self.x, self.y, self.c = lax.axis_index("x"), lax.axis_index("y"), lax.axis_index("c")
        self.k = 2 * self.x + self.y
        self.chips = _other_chips(self.x, self.y)
        self.me = (self.x, self.y, self.c)
        self.sibling = (self.x, self.y, 1 - self.c)

    def _load(self, p):
        _, _, _, axis, size = self.params[p]
        return pltpu.make_async_copy(_half(self.g_in[p], axis, size, self.c, (slice(None),)), self.pm[p],
                                     self.local_sems.at[p])

    def _to_sibling(self, p):
        _, _, _, axis, size = self.params[p]
        return self.copy(5 * p, _half(self.g_in[p], axis, size, 1 - self.c, (slice(None),)), self.a_buf[p],
                         self.sibling)

    def _pair_sum(self, p, j):
        cx, cy = self.chips[j]
        return self.copy(5 * p + 1 + j, self.pm[p].at[2 * cx + cy], self.b_buf[p].at[j], (cx, cy, self.c))

    def _store(self, p):
        _, _, _, axis, size = self.params[p]
        n = len(self.params)
        return pltpu.make_async_copy(self.r_buf[p], _half(self.g_out[p], axis, size, self.c),
                                     self.local_sems.at[n + p])

    def _finished(self, p):
        _, _, _, axis, size = self.params[p]
        return self.copy(5 * p + 4, self.r_buf[p], _half(self.g_out[p], axis, size, self.c), self.sibling)

    def start(self):
        for p in range(len(self.params)):
            self._load(p).start()
            self._to_sibling(p).start()

    def exchange(self):
        for p in range(len(self.params)):
            self._load(p).wait()
            self.copy(5 * p, self.a_buf[p], self.a_buf[p], self.me).wait_recv()
            for j, (cx, cy) in enumerate(self.chips):
                kj = 2 * cx + cy
                self.pm[p][kj] = (self.pm[p][kj].astype(F32) + self.a_buf[p][kj].astype(F32)).astype(BF16)
                self._pair_sum(p, j).start()
            self.r_buf[p][...] = self.pm[p][self.k].astype(F32) + self.a_buf[p][self.k].astype(F32)

    def finish(self):
        for p, (_, _, _, axis, size) in enumerate(self.params):
            for j in range(3):
                self.copy(5 * p + 1 + j, self.b_buf[p].at[j], self.b_buf[p].at[j], self.me).wait_recv()
                self.r_buf[p][...] = self.r_buf[p][...] + self.b_buf[p][j].astype(F32)
            self._store(p).start()
            self._finished(p).start()
        for p, (_, _, _, axis, size) in enumerate(self.params):
            theirs = _half(self.g_out[p], axis, size, 1 - self.c)
            self.copy(5 * p + 4, theirs, theirs, self.me).wait_recv()
            self._store(p).wait()
            self._to_sibling(p).wait_send()
            for j in range(3):
                self._pair_sum(p, j).wait_send()
            self._finished(p).wait_send()


class _SmallSum:
    def __init__(self, rows):
        self.rows = rows
        self.scratch = [pltpu.VMEM((N_DEV, rows, LANES), F32),
                        pltpu.SemaphoreType.DMA((N_DEV - 1,)), pltpu.SemaphoreType.DMA((N_DEV - 1,))]
        self.out_shape = jax.ShapeDtypeStruct((rows, LANES), F32)
        self.spec_shape = (rows, LANES)

    def bind(self, src, dst, scratch):
        self.src, self.dst = src, dst
        self.buf, send_sems, recv_sems = scratch
        self.copy = _remote_copier(send_sems, recv_sems)
        self.x, self.y, self.c = lax.axis_index("x"), lax.axis_index("y"), lax.axis_index("c")

    def _send(self, f):
        fx, fy, fc = [(a, b, d) for a in (0, 1) for b in (0, 1) for d in (0, 1)][f]
        x, y, c = self.x, self.y, self.c
        peer = (1 - x if fx else x, 1 - y if fy else y, 1 - c if fc else c)
        return self.copy(f - 1, self.src, self.buf.at[f], peer)

    def start(self):
        for f in range(1, N_DEV):
            self._send(f).start()
        self.buf[0] = self.src[...]

    def finish(self):
        me = (self.x, self.y, self.c)
        for f in range(1, N_DEV):
            self.copy(f - 1, self.buf.at[f], self.buf.at[f], me).wait_recv()
        dev = 4 * self.x + 2 * self.y + self.c
        total = self.buf[dev]
        for d in range(1, N_DEV):
            total = total + self.buf[jnp.bitwise_xor(dev, d)]
        self.dst[...] = total
        for f in range(1, N_DEV):
            self._send(f).wait_send()


def _adamw_math(w, g, m, v):
    m = ADAM_B1 * m + (1.0 - ADAM_B1) * g
    v = ADAM_B2 * v + (1.0 - ADAM_B2) * (g * g)
    m_hat = m / (1.0 - ADAM_B1 ** ADAM_STEP)
    v_hat = v / (1.0 - ADAM_B2 ** ADAM_STEP)
    delta = -ADAM_LR * (m_hat / (jnp.sqrt(v_hat) + ADAM_EPS) + ADAM_WD * w)
    return delta, m, v


def _adamw_tiled(w, g, m, v, tm):
    rows, cols = w.shape

    def body(w_ref, g_ref, m_ref, v_ref, d_ref, nm_ref, nv_ref):
        d_ref[...], nm_ref[...], nv_ref[...] = _adamw_math(w_ref[...], g_ref[...], m_ref[...], v_ref[...])

    spec = _row_spec(tm, cols)
    return pl.pallas_call(
        body,
        name="adamw_w_in",
        grid=(rows // tm,),
        in_specs=[spec] * 4,
        out_specs=[spec] * 3,
        out_shape=[jax.ShapeDtypeStruct(w.shape, F32)] * 3,
        compiler_params=pltpu.CompilerParams(dimension_semantics=("parallel",), vmem_limit_bytes=VMEM_LIMIT),
    )(w, g, m, v)


def _adamw_many(ws, gs, ms, vs):
    n = len(ws)

    def body(*refs):
        ins, outs = refs[:4 * n], refs[4 * n:]
        for i in range(n):
            d, nm, nv = _adamw_math(ins[i][...], ins[n + i][...], ins[2 * n + i][...], ins[3 * n + i][...])
            outs[i][...] = d
            outs[n + i][...] = nm
            outs[2 * n + i][...] = nv

    vmem_spec = pl.BlockSpec(memory_space=pltpu.VMEM)
    shapes = [jax.ShapeDtypeStruct(w.shape, F32) for w in ws]
    out = pl.pallas_call(
        body,
        name="adamw_small",
        in_specs=[vmem_spec] * (4 * n),
        out_specs=[vmem_spec] * (3 * n),
        out_shape=shapes * 3,
        compiler_params=pltpu.CompilerParams(vmem_limit_bytes=VMEM_LIMIT),
    )(*ws, *gs, *ms, *vs)
    return out[:n], out[n:2 * n], out[2 * n:]


def _pack_rows(parts, rows, dtype):
    flat = jnp.concatenate([p.reshape(-1).astype(dtype) for p in parts])
    flat = jnp.concatenate([flat, jnp.zeros((rows * LANES - flat.shape[0],), dtype)])
    return flat.reshape(rows, LANES)


def _unpack_rows(packed, shapes):
    flat = packed.reshape(-1)
    out, off = [], 0
    for _, shp in shapes:
        n = int(np.prod(shp))
        out.append(flat[off:off + n].reshape(shp))
        off += n
    return out


def _rope_tables(s):
    half = QK_ROPE_DIM // 2
    inv_freq = np.float32(ROPE_THETA) ** (-np.arange(half, dtype=np.float32) / np.float32(half))
    ang = (np.arange(s, dtype=np.float32)[:, None] * inv_freq[None, :]).astype(np.float32)
    cos, sin = np.cos(ang.astype(np.float64)).astype(np.float32), np.sin(ang.astype(np.float64)).astype(np.float32)
    z16 = np.zeros((s, half), np.float32)
    z32 = np.zeros((s, HEAD_PAD - QK_NOPE_DIM - QK_ROPE_DIM), np.float32)
    z64 = np.zeros((s, QK_NOPE_DIM), np.float32)
    rc = np.concatenate([np.ones((s, QK_NOPE_DIM), np.float32), cos, cos, z32], axis=1)
    rsa = np.concatenate([z64, -sin, z16, z32], axis=1)
    rsb = np.concatenate([z64, z16, sin, z32], axis=1)
    return jnp.asarray(rc), jnp.asarray(rsa), jnp.asarray(rsb)


def kernel(x, norm_in, w_in, q_norm, w_uq, kv_norm, w_ukv, pool_w, pool_scale, w_branch_attn, w_branch_pool, w_out, norm_final, loss_target, m_norm_in, m_w_in, m_q_norm, m_w_uq, m_kv_norm, m_w_ukv, m_pool_w, m_pool_scale, m_w_branch_attn, m_w_branch_pool, m_w_out, m_norm_final, v_norm_in, v_w_in, v_q_norm, v_w_uq, v_kv_norm, v_w_ukv, v_pool_w, v_pool_scale, v_w_branch_attn, v_w_branch_pool, v_w_out, v_norm_final):
    s = x.shape[1]
    t_att, t_row = _tiles(s)
    x2 = x.reshape(s, D_MODEL)
    tgt = loss_target.reshape(s, D_MODEL)

    local = [w_in.T, w_uq.reshape(96, 768), w_ukv.reshape(64, 1024), w_branch_attn, w_branch_pool, w_out]
    local = [a.astype(BF16) for a in local]
    cx, cy = lax.axis_index("x"), lax.axis_index("y")
    others = [2 * ox + oy for ox, oy in _other_chips(cx, cy)]
    hn, z_sh, (w_in_t, w_uq_all, w_ukv_all) = _inproj_fwd(
        jnp.stack([2 * cx + cy, others[1], others[2], others[0]]).astype(jnp.int32), x2, norm_in.reshape(1, -1),
        local[:3], 4 * t_row)
    w_uq_f = w_uq_all.reshape(Q_LORA_RANK, MLA_HEADS, QK_NOPE_DIM + QK_ROPE_DIM)
    w_ukv_f = w_ukv_all.reshape(KV_LORA_RANK, MLA_HEADS, QK_NOPE_DIM + V_HEAD_DIM)
    hw = MLA_HEADS * HEAD_PAD
    wuq_p = jnp.pad(w_uq_f, ((0, 0), (0, 0), (0, HEAD_PAD - QK_NOPE_DIM - QK_ROPE_DIM))).reshape(Q_LORA_RANK, hw)
    wk_p = jnp.pad(w_ukv_f[:, :, :QK_NOPE_DIM], ((0, 0), (0, 0), (0, HEAD_PAD - QK_NOPE_DIM))).reshape(KV_LORA_RANK, hw)
    wv = w_ukv_f[:, :, QK_NOPE_DIM:].reshape(KV_LORA_RANK, MLA_WIDTH)
    rc, rsa, rsb = _rope_tables(s)
    g_in = norm_in.reshape(1, -1)
    g_q = q_norm.reshape(1, -1)
    g_kv = kv_norm.reshape(1, -1)
    g_f = norm_final.reshape(1, -1)
    ps = pool_scale.reshape(1, -1)
    pw_bf = pool_w.astype(BF16)

    q, k, v, q_t, v_t = _qkv_fwd(z_sh, g_q, g_kv, wuq_p, wk_p, wv, rc, rsa, rsb, 2 * t_row)
    o, lse, (w_ba_all, w_bp_all, w_out_all) = _attn_fwd(q_t, k, v_t, local[3:], t_att)
    w_out_f = w_out_all.reshape(D_MODEL, D_MODEL)

    (do, delta, dgattn, dgpool, dgmerge, ddc, dh, sq_err, d_w_out, d_w_ba, d_w_bp, d_pool_w, d_pool_scale,
     d_norm_final) = _mid(o, z_sh, x2, tgt, pw_bf, ps, w_ba_all, w_bp_all, w_out_f, g_f, t_row)

    late_grads = [d_w_ba, d_w_bp, d_w_out.reshape(N_CHIPS, 256, D_MODEL)]
    small_mid = dict(pool_scale=d_pool_scale, norm_final=d_norm_final, pool_w=d_pool_w, sq_err=sq_err)
    gs_mid = _pack_rows([small_mid[n] for n, _ in SMALL_MID], _small_rows(SMALL_MID), F32)
    dq, dk_t, dv_t, (g_w_ba, g_w_bp, g_w_out), g_small_mid = _attn_bwd(q, q_t, k, v, do, lse, delta, late_grads,
                                                                      gs_mid, t_att)
    g_pool_scale, g_norm_final, g_pool_w, sq_err_all = _unpack_rows(g_small_mid, SMALL_MID)
    dzq, dzkv, dzkr, d_wuq_p, d_wk_p, d_wv, d_q_norm, d_kv_norm = _qkv_bwd(
        dq, dk_t, dv_t, z_sh, g_q, g_kv, wuq_p, wk_p, wv, rc, rsa, rsb, 2 * t_row)
    grad_x, d_norm_in, dz_sh = _inproj_bwd_x(dzq, dzkv, dzkr, dgattn, ddc, dgpool, dgmerge, x2, dh, g_in, w_in_t,
                                             2 * t_row)

    d_w_uq = d_wuq_p.reshape(Q_LORA_RANK, MLA_HEADS, HEAD_PAD)[:, :, :QK_NOPE_DIM + QK_ROPE_DIM]
    d_w_ukv = jnp.concatenate([d_wk_p.reshape(KV_LORA_RANK, MLA_HEADS, HEAD_PAD)[:, :, :QK_NOPE_DIM],
                               d_wv.reshape(KV_LORA_RANK, MLA_HEADS, V_HEAD_DIM)], axis=2)
    small_late = dict(norm_in=d_norm_in, q_norm=d_q_norm, kv_norm=d_kv_norm)
    gs = _pack_rows([small_late[n] for n, _ in SMALL_LATE], _small_rows(SMALL_LATE), F32)
    order = jnp.stack(others + [2 * cx + cy]).astype(jnp.int32)
    g_w_in_t, g_w_uq, g_w_ukv, g_small = _inproj_bwd_w(
        order, dz_sh, hn, d_w_uq.reshape(N_CHIPS, 96, 768).astype(BF16),
        d_w_ukv.reshape(N_CHIPS, 64, 1024).astype(BF16), gs, 4 * t_row)
    g_norm_in, g_q_norm, g_kv_norm = _unpack_rows(g_small, SMALL_LATE)
    g_w_uq = g_w_uq.reshape(w_uq.shape)
    g_w_ukv = g_w_ukv.reshape(w_ukv.shape)

    dl_w_in, nm_w_in, nv_w_in = (a.T for a in _adamw_tiled(w_in.T, g_w_in_t, m_w_in.T, v_w_in.T, 152))

    def two_d(a):
        return a.reshape(1, -1) if a.ndim == 1 else a

    names = ["norm_in", "q_norm", "w_uq", "kv_norm", "w_ukv", "pool_w", "pool_scale", "w_branch_attn",
             "w_branch_pool", "w_out", "norm_final"]
    ws = dict(norm_in=norm_in, q_norm=q_norm, w_uq=w_uq, kv_norm=kv_norm, w_ukv=w_ukv, pool_w=pool_w,
              pool_scale=pool_scale, w_branch_attn=w_branch_attn, w_branch_pool=w_branch_pool, w_out=w_out,
              norm_final=norm_final)
    gsd = dict(norm_in=g_norm_in, q_norm=g_q_norm, w_uq=g_w_uq, kv_norm=g_kv_norm, w_ukv=g_w_ukv, pool_w=g_pool_w,
               pool_scale=g_pool_scale, w_branch_attn=g_w_ba, w_branch_pool=g_w_bp, w_out=g_w_out,
               norm_final=g_norm_final)
    msd = dict(norm_in=m_norm_in, q_norm=m_q_norm, w_uq=m_w_uq, kv_norm=m_kv_norm, w_ukv=m_w_ukv, pool_w=m_pool_w,
               pool_scale=m_pool_scale, w_branch_attn=m_w_branch_attn, w_branch_pool=m_w_branch_pool, w_out=m_w_out,
               norm_final=m_norm_final)
    vsd = dict(norm_in=v_norm_in, q_norm=v_q_norm, w_uq=v_w_uq, kv_norm=v_kv_norm, w_ukv=v_w_ukv, pool_w=v_pool_w,
               pool_scale=v_pool_scale, w_branch_attn=v_w_branch_attn, w_branch_pool=v_w_branch_pool, w_out=v_w_out,
               norm_final=v_norm_final)
    dls, nms, nvs = _adamw_many([two_d(ws[n]) for n in names], [two_d(gsd[n]) for n in names],
                                [two_d(msd[n]) for n in names], [two_d(vsd[n]) for n in names])

    grads = dict(gsd)
    grads["w_in"] = g_w_in_t.T
    delta_w = {n: d.reshape(ws[n].shape) for n, d in zip(names, dls)}
    new_m = {n: d.reshape(ws[n].shape) for n, d in zip(names, nms)}
    new_v = {n: d.reshape(ws[n].shape) for n, d in zip(names, nvs)}
    delta_w["w_in"], new_m["w_in"], new_v["w_in"] = dl_w_in, nm_w_in, nv_w_in
    ws["w_in"] = w_in

    order = ["norm_in", "w_in", "q_norm", "w_uq", "kv_norm", "w_ukv", "pool_w", "pool_scale", "w_branch_attn",
             "w_branch_pool", "w_out", "norm_final"]
    loss = 0.5 * jnp.sum(sq_err_all) / D_MODEL
    return (loss, grad_x.reshape(x.shape),
            *[grads[n].reshape(ws[n].shape) for n in order],
            *[delta_w[n] for n in order], *[new_m[n] for n in order], *[new_v[n] for n in order])
```

```python
import jax
import jax.numpy as jnp
import numpy as np
from jax import lax
from jax.experimental import pallas as pl
from jax.experimental.pallas import tpu as pltpu

F32 = jnp.float32
BF16 = jnp.bfloat16
MESH = pl.DeviceIdType.MESH

D_MODEL = 1024
CHUNK = 64
MLA_HEADS = 8
QK_NOPE_DIM = 64
QK_ROPE_DIM = 32
V_HEAD_DIM = 64
Q_LORA_RANK = 384
KV_LORA_RANK = 256
MLA_WIDTH = MLA_HEADS * V_HEAD_DIM
ROPE_THETA = 10000.0
POOL_WINDOWS = (2, 4, 8, 16)
POOL_WIDTH = 512
POOL_GROUP_DIM = 128
BRANCH_COLS = D_MODEL // 4
FWD_HEADS = 8
BWD_HEADS = 4
POOL_HALO = 16
EPS = 1e-6
IN_TOTAL = 4256
HEAD_PAD = 128
ATT_SCALE = (QK_NOPE_DIM + QK_ROPE_DIM) ** -0.5
ATT_SCALE_LOG2E = ATT_SCALE * 1.4426950408889634

ADAM_LR = 0.001
ADAM_B1 = 0.9
ADAM_B2 = 0.999
ADAM_EPS = 1e-08
ADAM_WD = 0.01
ADAM_STEP = 10

N_CHIPS = 4
N_DEV = 8
LANES = 128
VMEM_LIMIT = 60 * 1024 * 1024

IN_SEGMENTS = ((384, 384), (256, 256), (32, HEAD_PAD), (512, 512), (512, 512), (512, 512), (2048, 2048))
SHARD_COLS = IN_TOTAL // N_CHIPS
ZQ_COLS = slice(0, 384)
ZKV_COLS = slice(384, 640)
ZKR_TILE = slice(640, 768)


def _shard_pieces():
    bounds, off = [], 0
    for w, _ in IN_SEGMENTS:
        bounds.append((off, off + w))
        off += w
    out = []
    for j in range(N_CHIPS):
        lo, hi = SHARD_COLS * j, SHARD_COLS * (j + 1)
        out.append([(i, max(lo, a) - a, min(hi, b) - a, max(lo, a) - lo)
                    for i, (a, b) in enumerate(bounds) if max(lo, a) < min(hi, b)])
    return out


SHARD_PIECES = _shard_pieces()


def _segment(z_blocks, seg):
    parts = [z_blocks[j][:, col:col + hi - lo]
             for j, pieces in enumerate(SHARD_PIECES) for sg, lo, hi, col in pieces if sg == seg]
    return parts[0] if len(parts) == 1 else jnp.concatenate(parts, axis=1)

COMM_PARAMS = (
    ("w_in", SHARD_COLS, D_MODEL, 1, 512),
    ("w_uq", 96, 768, 0, 48),
    ("w_ukv", 64, 1024, 0, 32),
    ("w_branch_attn", 512, 256, 0, 256),
    ("w_branch_pool", 512, 256, 0, 256),
    ("w_out", 256, 1024, 0, 128),
)

SMALL_MID = (
    ("pool_scale", (512,)),
    ("norm_final", (1024,)),
    ("pool_w", (4, 128, 128)),
    ("sq_err", (8, 128)),
)
SMALL_LATE = (
    ("norm_in", (1024,)),
    ("q_norm", (384,)),
    ("kv_norm", (256,)),
)


def _small_rows(shapes):
    return -(-sum(int(np.prod(s)) for _, s in shapes) // (LANES * 8)) * 8


def _dot(a, b):
    return jnp.dot(a, b, preferred_element_type=F32)


def _dot_nt(a, b):
    return lax.dot_general(a, b, (((1,), (1,)), ((), ())), preferred_element_type=F32)


def _dot_tn(a, b):
    return lax.dot_general(a, b, (((0,), (0,)), ((), ())), preferred_element_type=F32)


def _sigmoid(x):
    return 1.0 / (1.0 + jnp.exp(-x))


def _colsum(x):
    return jnp.sum(x, axis=0, keepdims=True)


def _rms_fwd(x, g):
    r = lax.rsqrt(jnp.mean(x * x, axis=-1, keepdims=True) + EPS)
    xhat = x * r
    return xhat * g, xhat, r


def _rms_bwd(dy, xhat, r, g):
    dxhat = dy * g
    return r * (dxhat - xhat * jnp.mean(dxhat * xhat, axis=-1, keepdims=True))


def _rope(v, c, sa, sb):
    return v * c + pltpu.roll(v, 112, 1) * sa + pltpu.roll(v, 16, 1) * sb


def _unrope(d, c, sa, sb):
    return d * c + pltpu.roll(d * sa, 16, 1) + pltpu.roll(d * sb, 112, 1)


def _row_spec(tm, n):
    return pl.BlockSpec((tm, n), lambda i: (i, 0))


def _full_spec(shape):
    nd = len(shape)
    return pl.BlockSpec(shape, lambda i: (0,) * nd)


def _tiles(s):
    t_att = 512 if s >= 2048 else 128
    t_row = 256 if s >= 1024 else 128
    return t_att, t_row


def _inproj_fwd(order, x, norm_in, early_shards, tm):
    s = x.shape[0]
    n_tiles = s // tm
    gat = _Gather(COMM_PARAMS[:3])
    n_w = len(gat.params)
    arrival = (1, 2, 0)

    def body(order_ref, x_ref, g_ref, *rest):
        w_loc, (hn_ref, z_ref), w_all = rest[:n_w], rest[n_w:n_w + 2], rest[n_w + 2:2 * n_w + 2]
        w_vmem, hn_all, w_sem = rest[2 * n_w + 2:2 * n_w + 5]
        gat.bind(w_loc, w_all, rest[2 * n_w + 5:])
        ph, i = pl.program_id(0), pl.program_id(1)
        pl.when(jnp.logical_and(ph == 0, i == 0))(gat.start)

        @pl.when(jnp.logical_and(ph == 0, i == 0))
        def _():
            cp = pltpu.make_async_copy(w_loc[0], w_vmem, w_sem)
            cp.start()
            cp.wait()

        for f in range(3):
            @pl.when(jnp.logical_and(ph == f + 1, i == 0))
            def _(f=f):
                gat.relay_one(0, arrival[f])
                gat.await_one(0, arrival[f])
                cp = pltpu.make_async_copy(w_all[0].at[order_ref[ph]], w_vmem, w_sem)
                cp.start()
                cp.wait()

        rows = pl.ds(pl.multiple_of(i * tm, tm), tm)

        @pl.when(ph == 0)
        def _():
            hn, _, _ = _rms_fwd(x_ref[...], g_ref[...])
            hn = hn.astype(BF16)
            hn_ref[...] = hn
            hn_all[rows, :] = hn

        z_ref[0] = _dot_nt(hn_all[rows, :], w_vmem[...])

        @pl.when(jnp.logical_and(ph == N_CHIPS - 1, i == n_tiles - 1))
        def _():
            for p in range(1, n_w):
                for j in range(3):
                    gat.relay_one(p, j)
            for p in range(1, n_w):
                for j in range(3):
                    gat.await_one(p, j)
            gat.wait_sends()

    def tile_in_phase0(ph, i, order):
        return (jnp.where(ph == 0, i, n_tiles - 1), 0)

    any_spec = pl.BlockSpec(memory_space=pl.ANY)
    grid_spec = pltpu.PrefetchScalarGridSpec(
        num_scalar_prefetch=1,
        grid=(N_CHIPS, n_tiles),
        in_specs=[pl.BlockSpec((tm, D_MODEL), tile_in_phase0),
                  pl.BlockSpec((1, D_MODEL), lambda ph, i, order: (0, 0))] + [any_spec] * n_w,
        out_specs=[pl.BlockSpec((tm, D_MODEL), tile_in_phase0),
                   pl.BlockSpec((1, tm, SHARD_COLS), lambda ph, i, order: (order[ph], i, 0))] + [any_spec] * n_w,
        scratch_shapes=[pltpu.VMEM((SHARD_COLS, D_MODEL), BF16), pltpu.VMEM((s, D_MODEL), BF16),
                        pltpu.SemaphoreType.DMA] + gat.scratch,
    )
    out = pl.pallas_call(
        body,
        name="inproj_fwd",
        grid_spec=grid_spec,
        out_shape=[jax.ShapeDtypeStruct((s, D_MODEL), BF16), jax.ShapeDtypeStruct((N_CHIPS, s, SHARD_COLS), F32)]
        + gat.out_shape,
        compiler_params=pltpu.CompilerParams(dimension_semantics=("arbitrary", "arbitrary"),
                                             vmem_limit_bytes=VMEM_LIMIT),
    )(order, x, norm_in, *early_shards)
    return out[0], out[1], out[2:]


def _qkv_fwd(z_sh, q_norm, kv_norm, wuq_p, wk_p, wv, rc, rsa, rsb, tm):
    s = z_sh.shape[1]
    hw = MLA_HEADS * HEAD_PAD

    def body(z_ref, gq_ref, gkv_ref, wuq_ref, wk_ref, wv_ref, c_ref, sa_ref, sb_ref,
             q_ref, k_ref, v_ref, qt_ref, vt_ref):
        c, sa, sb = c_ref[...], sa_ref[...], sb_ref[...]
        z0 = z_ref[0]
        cq, _, _ = _rms_fwd(z0[:, ZQ_COLS], gq_ref[...])
        qf = _dot(cq.astype(BF16), wuq_ref[...])
        ckv, _, _ = _rms_fwd(z0[:, ZKV_COLS], gkv_ref[...])
        ckv = ckv.astype(BF16)
        kn = _dot(ckv, wk_ref[...])
        lane = lax.broadcasted_iota(jnp.int32, (tm, HEAD_PAD), 1)
        zkr = jnp.where(lane < QK_ROPE_DIM, z0[:, ZKR_TILE], 0.0)
        kr = _rope(pltpu.roll(zkr, 64, 1), c, sa, sb)
        for h in range(MLA_HEADS):
            cols = slice(h * HEAD_PAD, (h + 1) * HEAD_PAD)
            qh = _rope(qf[:, cols], c, sa, sb)
            q_ref[:, cols] = qh.astype(BF16)
            qt_ref[cols, :] = qh.T.astype(BF16)
            k_ref[:, cols] = (kn[:, cols] + kr).astype(BF16)
        vf = _dot(ckv, wv_ref[...])
        v_ref[...] = vf.astype(BF16)
        vt_ref[...] = vf.T.astype(BF16)

    return pl.pallas_call(
        body,
        name="qkv_fwd",
        grid=(s // tm,),
        in_specs=[
            pl.BlockSpec((1, tm, SHARD_COLS), lambda i: (0, i, 0)),
            _full_spec((1, Q_LORA_RANK)), _full_spec((1, KV_LORA_RANK)),
            _full_spec((Q_LORA_RANK, hw)), _full_spec((KV_LORA_RANK, hw)), _full_spec((KV_LORA_RANK, MLA_WIDTH)),
            _row_spec(tm, HEAD_PAD), _row_spec(tm, HEAD_PAD), _row_spec(tm, HEAD_PAD),
        ],
        out_specs=[_row_spec(tm, hw), _row_spec(tm, hw), _row_spec(tm, MLA_WIDTH),
                   pl.BlockSpec((hw, tm), lambda i: (0, i)), pl.BlockSpec((MLA_WIDTH, tm), lambda i: (0, i))],
        out_shape=[jax.ShapeDtypeStruct((s, hw), BF16), jax.ShapeDtypeStruct((s, hw), BF16),
                   jax.ShapeDtypeStruct((s, MLA_WIDTH), BF16),
                   jax.ShapeDtypeStruct((hw, s), BF16), jax.ShapeDtypeStruct((MLA_WIDTH, s), BF16)],
        compiler_params=pltpu.CompilerParams(dimension_semantics=("parallel",), vmem_limit_bytes=VMEM_LIMIT),
    )(z_sh, q_norm, kv_norm, wuq_p, wk_p, wv, rc, rsa, rsb)


def _chunk_mask(t, keys_on_rows):
    rows = lax.broadcasted_iota(jnp.int32, (t, t), 0) // CHUNK
    cols = lax.broadcasted_iota(jnp.int32, (t, t), 1) // CHUNK
    return rows <= cols if keys_on_rows else cols <= rows


def _attn_fwd(q_t, k, v_t, late_shards, t):
    s = k.shape[0]
    groups = MLA_HEADS // FWD_HEADS
    n_q = s // t
    gat = _Gather(COMM_PARAMS[3:])
    n_w = len(gat.params)

    def body(qt_ref, k_ref, k2_ref, vt_ref, *rest):
        w_in, (o_ref, lse_ref), w_out = rest[:n_w], rest[n_w:n_w + 2], rest[n_w + 2:2 * n_w + 2]
        gat.bind(w_in, w_out, rest[2 * n_w + 2:])
        i = pl.program_id(1)
        step_no = pl.program_id(0) * n_q + i
        pl.when(step_no == 0)(gat.start)
        pl.when(step_no == groups * n_q // 2)(gat.relay)
        mask = _chunk_mask(t, True)
        qcs = [slice(hh * HEAD_PAD, (hh + 1) * HEAD_PAD) for hh in range(FWD_HEADS)]
        vcs = [slice(hh * V_HEAD_DIM, (hh + 1) * V_HEAD_DIM) for hh in range(FWD_HEADS)]
        qts = [qt_ref[qc, :] for qc in qcs]

        def attend(hh, stats, keys, lanes, blk_mask):
            m, l, acc = stats
            qt = qts[hh][:, lanes]
            sc = _dot(k_ref[keys, qcs[hh]], qt)
            if blk_mask is not None:
                sc = jnp.where(blk_mask, sc, -jnp.inf)
            m_new = jnp.maximum(m, jnp.max(sc, axis=0, keepdims=True))
            alpha = jnp.exp2((m - m_new) * ATT_SCALE_LOG2E)
            p = jnp.exp2((_dot(k2_ref[keys, qcs[hh]], qt) - m_new) * ATT_SCALE_LOG2E)
            if blk_mask is not None:
                p = jnp.where(blk_mask, p, 0.0)
            l = alpha * l + jnp.sum(p, axis=0, keepdims=True)
            acc = alpha * acc + _dot(vt_ref[vcs[hh], keys], p.astype(BF16))
            return m_new, l, acc

        def step(j, carry):
            keys = pl.ds(pl.multiple_of(j * t, t), t)
            return tuple(attend(hh, carry[hh], keys, slice(None), None) for hh in range(FWD_HEADS))

        def diagonal(carry):
            th = t // 2
            first = pl.ds(pl.multiple_of(i * t, t), th)
            second = pl.ds(pl.multiple_of(i * t + th, th), th)
            out = []
            for hh in range(FWD_HEADS):
                stats = attend(hh, carry[hh], first, slice(None), mask[:th, :])
                right = attend(hh, tuple(a[:, th:] for a in stats), second, slice(th, t), mask[th:, th:])
                out.append(tuple(jnp.concatenate([a[:, :th], b], axis=1) for a, b in zip(stats, right)))
            return tuple(out)

        one = (jnp.full((1, t), -jnp.inf, F32), jnp.zeros((1, t), F32), jnp.zeros((V_HEAD_DIM, t), F32))
        carry = lax.fori_loop(0, i, step, (one,) * FWD_HEADS)
        carry = diagonal(carry)
        o_ref[...] = jnp.concatenate([carry[hh][2] / carry[hh][1] for hh in range(FWD_HEADS)], axis=0).T
        for hh in range(FWD_HEADS):
            m, l, _ = carry[hh]
            lse_ref[:, qcs[hh]] = jnp.broadcast_to(m * ATT_SCALE_LOG2E + jnp.log2(l), (HEAD_PAD, t)).T
        pl.when(step_no == groups * n_q - 1)(gat.finish)

    any_spec = pl.BlockSpec(memory_space=pl.ANY)
    out = pl.pallas_call(
        body,
        name="attn_fwd",
        grid=(groups, n_q),
        in_specs=[
            pl.BlockSpec((FWD_HEADS * HEAD_PAD, t), lambda p, i: (p, i)),
            pl.BlockSpec((s, FWD_HEADS * HEAD_PAD), lambda p, i: (0, p), pipeline_mode=pl.Buffered(1)),
            pl.BlockSpec((s, FWD_HEADS * HEAD_PAD), lambda p, i: (0, p), pipeline_mode=pl.Buffered(1)),
            pl.BlockSpec((FWD_HEADS * V_HEAD_DIM, s), lambda p, i: (p, 0), pipeline_mode=pl.Buffered(1)),
        ] + [any_spec] * n_w,
        out_specs=[
            pl.BlockSpec((t, FWD_HEADS * V_HEAD_DIM), lambda p, i: (i, p)),
            pl.BlockSpec((t, FWD_HEADS * HEAD_PAD), lambda p, i: (i, p)),
        ] + [any_spec] * n_w,
        out_shape=[jax.ShapeDtypeStruct((s, MLA_WIDTH), F32), jax.ShapeDtypeStruct((s, MLA_HEADS * HEAD_PAD), F32)]
        + gat.out_shape,
        scratch_shapes=gat.scratch,
        compiler_params=pltpu.CompilerParams(dimension_semantics=("arbitrary", "arbitrary"),
                                             vmem_limit_bytes=VMEM_LIMIT),
    )(q_t, k, k, v_t, *late_shards)
    return out[0], out[1], out[2:]


def _mid(o, z_sh, x, target, pool_w, pool_scale, w_ba, w_bp, w_out, norm_final, tm):
    s = x.shape[0]
    n_tiles = s // tm
    halo_per_tile = tm // POOL_HALO

    def body(o_ref, z0_ref, z1_ref, z1h_ref, z2_ref, z3_ref, x_ref, t_ref, pw_ref, ps_ref, wba_ref, wbp_ref,
             wout_ref, gf_ref,
             do_ref, dl_ref, dga_ref, dgp_ref, dgm_ref, ddc_ref, dh_ref,
             loss_ref, dwout_out, dwba_out, dwbp_out, dpw_ref, dps_ref, dgf_ref,
             ubuf, dwout_ref, dwba_ref, dwbp_ref):
        i = pl.program_id(0)

        @pl.when(i == 0)
        def _():
            loss_ref[...] = jnp.zeros_like(loss_ref)
            dwout_ref[...] = jnp.zeros_like(dwout_ref)
            dwba_ref[...] = jnp.zeros_like(dwba_ref)
            dwbp_ref[...] = jnp.zeros_like(dwbp_ref)
            dpw_ref[...] = jnp.zeros_like(dpw_ref)
            dps_ref[...] = jnp.zeros_like(dps_ref)
            dgf_ref[...] = jnp.zeros_like(dgf_ref)

        zs = [z0_ref[0], z1_ref[0], z2_ref[0], z3_ref[0]]
        o = o_ref[...]
        ga = _segment(zs, 3)
        sga = _sigmoid(ga)
        silu_a = ga * sga
        y_attn = (o * silu_a).astype(BF16)

        ubuf[0:POOL_HALO, :] = jnp.where(i > 0, _segment([None, z1h_ref[0]], 4), 0.0)
        ubuf[POOL_HALO:, :] = _segment(zs, 4)
        row = lax.broadcasted_iota(jnp.int32, (tm, POOL_GROUP_DIM), 0) + i * tm
        ps = ps_ref[...]
        gp = _segment(zs, 5)
        sgp = _sigmoid(gp)
        silu_p = gp * sgp
        d_bf, dm, inv_cnt = [], [], []
        for g, w in enumerate(POOL_WINDOWS):
            cols = slice(g * POOL_GROUP_DIM, (g + 1) * POOL_GROUP_DIM)
            wsum = ubuf[POOL_HALO:, cols]
            for kk in range(1, w):
                wsum = wsum + ubuf[POOL_HALO - kk:POOL_HALO - kk + tm, cols]
            inv = 1.0 / jnp.minimum(row + 1, w).astype(F32)
            dg = (wsum * inv - ubuf[POOL_HALO:, cols]).astype(BF16)
            d_bf.append(dg)
            inv_cnt.append(inv)
            dm.append(_dot(dg, pw_ref[g]))
        dm = jnp.concatenate(dm, axis=1)
        yp = dm * ps
        y_pool = (yp * silu_p).astype(BF16)

        a = jnp.concatenate([_dot(y_attn, wba_ref[j]) for j in range(N_CHIPS)], axis=1)
        p = jnp.concatenate([_dot(y_pool, wbp_ref[j]) for j in range(N_CHIPS)], axis=1)
        gm = _segment(zs, 6)
        gate_a = _sigmoid(gm[:, :D_MODEL])
        gate_p = _sigmoid(gm[:, D_MODEL:])
        merged = (gate_a * a + gate_p * p).astype(BF16)
        h = x_ref[...] + _dot(merged, wout_ref[...])
        gf = gf_ref[...]
        y, xhat, r = _rms_fwd(h, gf)
        err = y - t_ref[...]
        e2 = err * err
        e2 = jnp.sum(e2.reshape(tm // 8, 8, D_MODEL), axis=0)
        acc = e2[:, 0:LANES]
        for cidx in range(1, D_MODEL // LANES):
            acc = acc + e2[:, cidx * LANES:(cidx + 1) * LANES]
        loss_ref[...] += acc

        dy = err * (1.0 / D_MODEL)
        dgf_ref[...] += _colsum(dy * xhat)
        dh = _rms_bwd(dy, xhat, r, gf)
        dh_ref[...] = dh
        dh_bf = dh.astype(BF16)
        dwout_ref[...] += _dot_tn(merged, dh_bf)
        dmerged = _dot_nt(dh_bf, wout_ref[...])
        da = (dmerged * gate_a).astype(BF16)
        dp = (dmerged * gate_p).astype(BF16)
        dgm_ref[:, :D_MODEL] = (dmerged * a * gate_a * (1.0 - gate_a)).astype(BF16)
        dgm_ref[:, D_MODEL:] = (dmerged * p * gate_p * (1.0 - gate_p)).astype(BF16)
        dy_attn = dy_pool = None
        for j in range(N_CHIPS):
            cols = slice(j * BRANCH_COLS, (j + 1) * BRANCH_COLS)
            dwba_ref[j] += _dot_tn(y_attn, da[:, cols])
            dwbp_ref[j] += _dot_tn(y_pool, dp[:, cols])
            pa = _dot_nt(da[:, cols], wba_ref[j])
            pp = _dot_nt(dp[:, cols], wbp_ref[j])
            dy_attn = pa if dy_attn is None else dy_attn + pa
            dy_pool = pp if dy_pool is None else dy_pool + pp

        do = dy_attn * silu_a
        do_ref[...] = do
        dga_ref[...] = (dy_attn * o * (sga * (1.0 + ga * (1.0 - sga)))).astype(BF16)
        doo = do * o
        for hd in range(MLA_HEADS):
            dl = jnp.sum(doo[:, hd * V_HEAD_DIM:(hd + 1) * V_HEAD_DIM], axis=1, keepdims=True)
            dl_ref[:, hd * HEAD_PAD:(hd + 1) * HEAD_PAD] = jnp.broadcast_to(dl, (tm, HEAD_PAD))

        dyp = dy_pool * silu_p
        dgp_ref[...] = (dy_pool * yp * (sgp * (1.0 + gp * (1.0 - sgp)))).astype(BF16)
        dps_ref[...] += _colsum(dyp * dm)
        dmm = (dyp * ps).astype(BF16)
        for g in range(len(POOL_WINDOWS)):
            cols = slice(g * POOL_GROUP_DIM, (g + 1) * POOL_GROUP_DIM)
            dpw_ref[g] += _dot_tn(d_bf[g], dmm[:, cols])
            ddc_ref[:, cols] = _dot_nt(dmm[:, cols], pw_ref[g]) * inv_cnt[g]

        @pl.when(i == n_tiles - 1)
        def _():
            dwout_out[...] = dwout_ref[...].astype(BF16)
            dwba_out[...] = dwba_ref[...].astype(BF16)
            dwbp_out[...] = dwbp_ref[...].astype(BF16)

    row_in = lambda n: _row_spec(tm, n)
    in_specs = [
        row_in(MLA_WIDTH),
        pl.BlockSpec((1, tm, SHARD_COLS), lambda i: (0, i, 0)), pl.BlockSpec((1, tm, SHARD_COLS), lambda i: (1, i, 0)),
        pl.BlockSpec((1, POOL_HALO, SHARD_COLS), lambda i: (1, jnp.maximum(i * halo_per_tile - 1, 0), 0)),
        pl.BlockSpec((1, tm, SHARD_COLS), lambda i: (2, i, 0)), pl.BlockSpec((1, tm, SHARD_COLS), lambda i: (3, i, 0)),
        row_in(D_MODEL), row_in(D_MODEL),
        _full_spec((4, POOL_GROUP_DIM, POOL_GROUP_DIM)), _full_spec((1, POOL_WIDTH)),
        _full_spec((N_CHIPS, MLA_WIDTH, BRANCH_COLS)), _full_spec((N_CHIPS, POOL_WIDTH, BRANCH_COLS)),
        _full_spec((D_MODEL, D_MODEL)), _full_spec((1, D_MODEL)),
    ]
    out_shape = [
        jax.ShapeDtypeStruct((s, MLA_WIDTH), F32),
        jax.ShapeDtypeStruct((s, MLA_HEADS * HEAD_PAD), F32),
        jax.ShapeDtypeStruct((s, MLA_WIDTH), BF16),
        jax.ShapeDtypeStruct((s, POOL_WIDTH), BF16),
        jax.ShapeDtypeStruct((s, 2 * D_MODEL), BF16),
        jax.ShapeDtypeStruct((s, POOL_WIDTH), F32),
        jax.ShapeDtypeStruct((s, D_MODEL), F32),
        jax.ShapeDtypeStruct((8, LANES), F32),
        jax.ShapeDtypeStruct((D_MODEL, D_MODEL), BF16),
        jax.ShapeDtypeStruct((N_CHIPS, MLA_WIDTH, BRANCH_COLS), BF16),
        jax.ShapeDtypeStruct((N_CHIPS, POOL_WIDTH, BRANCH_COLS), BF16),
        jax.ShapeDtypeStruct((4, POOL_GROUP_DIM, POOL_GROUP_DIM), F32),
        jax.ShapeDtypeStruct((1, POOL_WIDTH), F32),
        jax.ShapeDtypeStruct((1, D_MODEL), F32),
    ]
    out_specs = [
        row_in(MLA_WIDTH), row_in(MLA_HEADS * HEAD_PAD), row_in(MLA_WIDTH), row_in(POOL_WIDTH),
        row_in(2 * D_MODEL), row_in(POOL_WIDTH), row_in(D_MODEL),
        _full_spec((8, LANES)), _full_spec((D_MODEL, D_MODEL)), _full_spec((N_CHIPS, MLA_WIDTH, BRANCH_COLS)),
        _full_spec((N_CHIPS, POOL_WIDTH, BRANCH_COLS)), _full_spec((4, POOL_GROUP_DIM, POOL_GROUP_DIM)),
        _full_spec((1, POOL_WIDTH)), _full_spec((1, D_MODEL)),
    ]
    return pl.pallas_call(
        body,
        name="mid",
        grid=(n_tiles,),
        in_specs=in_specs,
        out_specs=out_specs,
        out_shape=out_shape,
        scratch_shapes=[
            pltpu.VMEM((tm + POOL_HALO, POOL_WIDTH), F32),
            pltpu.VMEM((D_MODEL, D_MODEL), F32),
            pltpu.VMEM((N_CHIPS, MLA_WIDTH, BRANCH_COLS), F32),
            pltpu.VMEM((N_CHIPS, POOL_WIDTH, BRANCH_COLS), F32),
        ],
        compiler_params=pltpu.CompilerParams(dimension_semantics=("arbitrary",), vmem_limit_bytes=VMEM_LIMIT),
    )(o, z_sh, z_sh, z_sh, z_sh, z_sh, x, target, pool_w, pool_scale, w_ba, w_bp, w_out, norm_final)


def _attn_bwd(q, q_t, k, v, do, lse, delta, late_grads, gs_mid, t):
    s = q.shape[0]
    groups = MLA_HEADS // BWD_HEADS
    n_q = s // t
    red = _Reduce(COMM_PARAMS[3:])
    n_w = len(red.params)
    small = _SmallSum(gs_mid.shape[0])
    n_red = len(red.scratch)

    def body(q_ref, qt_ref, do_ref, lse_ref, dl_ref, k_ref, v_ref, *rest):
        g_in, gs_ref = rest[:n_w], rest[n_w]
        (dq_ref, dk_ref, dv_ref), g_out, gsum_ref = rest[n_w + 1:n_w + 4], rest[n_w + 4:2 * n_w + 4], rest[2 * n_w + 4]
        scratch = rest[2 * n_w + 5:]
        red.bind(g_in, g_out, scratch[:n_red])
        small.bind(gs_ref, gsum_ref, scratch[n_red:])
        i = pl.program_id(1)
        step_no = pl.program_id(0) * n_q + i

        @pl.when(step_no == 0)
        def _():
            red.start()
            small.start()

        pl.when(step_no == groups * n_q // 2)(red.exchange)

        @pl.when(i == 0)
        def _():
            dk_ref[...] = jnp.zeros_like(dk_ref)
            dv_ref[...] = jnp.zeros_like(dv_ref)

        mask = _chunk_mask(t, False)
        qcs = [slice(hh * HEAD_PAD, (hh + 1) * HEAD_PAD) for hh in range(BWD_HEADS)]
        vcs = [slice(hh * V_HEAD_DIM, (hh + 1) * V_HEAD_DIM) for hh in range(BWD_HEADS)]
        qhs = [q_ref[:, qc] for qc in qcs]
        qts = [qt_ref[qc, :] for qc in qcs]
        dohs = [do_ref[:, vc].astype(BF16) for vc in vcs]
        do_t = do_ref[...].T.astype(BF16)
        dots = [do_t[vc, :] for vc in vcs]
        lses = [jnp.tile(lse_ref[:, qc], (1, t // HEAD_PAD)) for qc in qcs]
        dls = [jnp.tile(dl_ref[:, qc], (1, t // HEAD_PAD)) for qc in qcs]

        def grads(hh, dq, keys, rows, blk_mask):
            nk = keys.size
            kj = k_ref[keys, qcs[hh]]
            vj = v_ref[keys, vcs[hh]]
            p = jnp.exp2(_dot_nt(qhs[hh][rows], kj) * ATT_SCALE_LOG2E - lses[hh][rows, :nk])
            if blk_mask is not None:
                p = jnp.where(blk_mask, p, 0.0)
            ds = (p * (_dot_nt(dohs[hh][rows], vj) - dls[hh][rows, :nk])).astype(BF16)
            dv_ref[vcs[hh], keys] += _dot(dots[hh][:, rows], p.astype(BF16))
            dk_ref[qcs[hh], keys] += _dot(qts[hh][:, rows], ds) * ATT_SCALE
            return dq + _dot(ds, kj)

        def step(j, dqs):
            keys = pl.ds(pl.multiple_of(j * t, t), t)
            return tuple(grads(hh, dqs[hh], keys, slice(None), None) for hh in range(BWD_HEADS))

        def diagonal(dqs):
            th = t // 2
            first = pl.ds(pl.multiple_of(i * t, t), th)
            second = pl.ds(pl.multiple_of(i * t + th, th), th)
            out = []
            for hh in range(BWD_HEADS):
                dq = grads(hh, dqs[hh], first, slice(None), mask[:, :th])
                low = grads(hh, dq[th:], second, slice(th, t), mask[th:, th:])
                out.append(jnp.concatenate([dq[:th], low], axis=0))
            return tuple(out)

        zero = jnp.zeros((t, HEAD_PAD), F32)
        dqs = lax.fori_loop(0, i, step, (zero,) * BWD_HEADS)
        dqs = diagonal(dqs)
        for hh in range(BWD_HEADS):
            dq_ref[:, qcs[hh]] = dqs[hh] * ATT_SCALE

        @pl.when(step_no == groups * n_q - 1)
        def _():
            red.finish()
            small.finish()

    hw = MLA_HEADS * HEAD_PAD
    any_spec = pl.BlockSpec(memory_space=pl.ANY)
    out = pl.pallas_call(
        body,
        name="attn_bwd",
        grid=(groups, n_q),
        in_specs=[
            pl.BlockSpec((t, BWD_HEADS * HEAD_PAD), lambda p, i: (i, p)),
            pl.BlockSpec((BWD_HEADS * HEAD_PAD, t), lambda p, i: (p, i)),
            pl.BlockSpec((t, BWD_HEADS * V_HEAD_DIM), lambda p, i: (i, p)),
            pl.BlockSpec((t, BWD_HEADS * HEAD_PAD), lambda p, i: (i, p)),
            pl.BlockSpec((t, BWD_HEADS * HEAD_PAD), lambda p, i: (i, p)),
            pl.BlockSpec((s, BWD_HEADS * HEAD_PAD), lambda p, i: (0, p), pipeline_mode=pl.Buffered(1)),
            pl.BlockSpec((s, BWD_HEADS * V_HEAD_DIM), lambda p, i: (0, p), pipeline_mode=pl.Buffered(1)),
        ] + [any_spec] * n_w + [pl.BlockSpec(small.spec_shape, lambda p, i: (0, 0))],
        out_specs=[
            pl.BlockSpec((t, BWD_HEADS * HEAD_PAD), lambda p, i: (i, p)),
            pl.BlockSpec((BWD_HEADS * HEAD_PAD, s), lambda p, i: (p, 0)),
            pl.BlockSpec((BWD_HEADS * V_HEAD_DIM, s), lambda p, i: (p, 0)),
        ] + [any_spec] * n_w + [pl.BlockSpec(small.spec_shape, lambda p, i: (0, 0))],
        out_shape=[jax.ShapeDtypeStruct((s, hw), F32), jax.ShapeDtypeStruct((hw, s), F32),
                   jax.ShapeDtypeStruct((MLA_WIDTH, s), F32)] + red.out_shape + [small.out_shape],
        scratch_shapes=red.scratch + small.scratch,
        compiler_params=pltpu.CompilerParams(dimension_semantics=("arbitrary", "arbitrary"),
                                             vmem_limit_bytes=VMEM_LIMIT),
    )(q, q_t, do, lse, delta, k, v, *late_grads, gs_mid)
    return out[0], out[1], out[2], out[3:3 + n_w], out[3 + n_w]


def _qkv_bwd(dq, dk_t, dv_t, z_sh, q_norm, kv_norm, wuq_p, wk_p, wv, rc, rsa, rsb, tm):
    s = z_sh.shape[1]
    hw = MLA_HEADS * HEAD_PAD

    def body(dq_ref, dk_ref, dv_ref, z_ref, gq_ref, gkv_ref, wuq_ref, wk_ref, wv_ref,
             c_ref, sa_ref, sb_ref,
             dzq_ref, dzkv_ref, dzkr_ref, dwuq_ref, dwk_ref, dwv_ref, dgq_ref, dgkv_ref):
        i = pl.program_id(0)

        @pl.when(i == 0)
        def _():
            dwuq_ref[...] = jnp.zeros_like(dwuq_ref)
            dwk_ref[...] = jnp.zeros_like(dwk_ref)
            dwv_ref[...] = jnp.zeros_like(dwv_ref)
            dgq_ref[...] = jnp.zeros_like(dgq_ref)
            dgkv_ref[...] = jnp.zeros_like(dgkv_ref)

        c, sa, sb = c_ref[...], sa_ref[...], sb_ref[...]
        gq, gkv = gq_ref[...], gkv_ref[...]

        z0 = z_ref[0]
        cq, xq, rq = _rms_fwd(z0[:, ZQ_COLS], gq)
        dqp = jnp.concatenate(
            [_unrope(dq_ref[:, h * HEAD_PAD:(h + 1) * HEAD_PAD], c, sa, sb) for h in range(MLA_HEADS)],
            axis=1).astype(BF16)
        dwuq_ref[...] += _dot_tn(cq.astype(BF16), dqp)
        dcq = _dot_nt(dqp, wuq_ref[...])
        dgq_ref[...] += _colsum(dcq * xq)
        dzq_ref[...] = _rms_bwd(dcq, xq, rq, gq).astype(BF16)

        ckv, xkv, rkv = _rms_fwd(z0[:, ZKV_COLS], gkv)
        ckv = ckv.astype(BF16)
        dkf = dk_ref[...].T
        dk_bf = dkf.astype(BF16)
        dv_bf = dv_ref[...].T.astype(BF16)
        dwk_ref[...] += _dot_tn(ckv, dk_bf)
        dwv_ref[...] += _dot_tn(ckv, dv_bf)
        dckv = _dot_nt(dk_bf, wk_ref[...]) + _dot_nt(dv_bf, wv_ref[...])
        dgkv_ref[...] += _colsum(dckv * xkv)
        dzkv_ref[...] = _rms_bwd(dckv, xkv, rkv, gkv).astype(BF16)

        dkr = dkf[:, 0:HEAD_PAD]
        for h in range(1, MLA_HEADS):
            dkr = dkr + dkf[:, h * HEAD_PAD:(h + 1) * HEAD_PAD]
        dkr = pltpu.roll(_unrope(dkr, c, sa, sb), 64, 1)
        lane = lax.broadcasted_iota(jnp.int32, (tm, HEAD_PAD), 1)
        dzkr_ref[...] = jnp.where(lane < QK_ROPE_DIM, dkr, 0.0).astype(BF16)

    return pl.pallas_call(
        body,
        name="qkv_bwd",
        grid=(s // tm,),
        in_specs=[
            _row_spec(tm, hw), pl.BlockSpec((hw, tm), lambda i: (0, i)), pl.BlockSpec((MLA_WIDTH, tm), lambda i: (0, i)),
            pl.BlockSpec((1, tm, SHARD_COLS), lambda i: (0, i, 0)),
            _full_spec((1, Q_LORA_RANK)), _full_spec((1, KV_LORA_RANK)),
            _full_spec((Q_LORA_RANK, hw)), _full_spec((KV_LORA_RANK, hw)), _full_spec((KV_LORA_RANK, MLA_WIDTH)),
            _row_spec(tm, HEAD_PAD), _row_spec(tm, HEAD_PAD), _row_spec(tm, HEAD_PAD),
        ],
        out_specs=[
            _row_spec(tm, Q_LORA_RANK), _row_spec(tm, KV_LORA_RANK), _row_spec(tm, HEAD_PAD),
            _full_spec((Q_LORA_RANK, hw)), _full_spec((KV_LORA_RANK, hw)), _full_spec((KV_LORA_RANK, MLA_WIDTH)),
            _full_spec((1, Q_LORA_RANK)), _full_spec((1, KV_LORA_RANK)),
        ],
        out_shape=[
            jax.ShapeDtypeStruct((s, Q_LORA_RANK), BF16), jax.ShapeDtypeStruct((s, KV_LORA_RANK), BF16),
            jax.ShapeDtypeStruct((s, HEAD_PAD), BF16),
            jax.ShapeDtypeStruct((Q_LORA_RANK, hw), F32), jax.ShapeDtypeStruct((KV_LORA_RANK, hw), F32),
            jax.ShapeDtypeStruct((KV_LORA_RANK, MLA_WIDTH), F32),
            jax.ShapeDtypeStruct((1, Q_LORA_RANK), F32), jax.ShapeDtypeStruct((1, KV_LORA_RANK), F32),
        ],
        compiler_params=pltpu.CompilerParams(dimension_semantics=("arbitrary",), vmem_limit_bytes=VMEM_LIMIT),
    )(dq, dk_t, dv_t, z_sh, q_norm, kv_norm, wuq_p, wk_p, wv, rc, rsa, rsb)


def _inproj_bwd_x(dzq, dzkv, dzkr, dgattn, ddc, dgpool, dgmerge, x, dh, norm_in, w_in_t, tm):
    s = x.shape[0]
    n_tiles = s // tm
    halo_per_tile = tm // POOL_HALO
    n_halo = s // POOL_HALO
    u_seg = 4

    def body(dzq_ref, dzkv_ref, dzkr_ref, dga_ref, ddc_ref, ddn_ref, dgp_ref, dgm_ref, x_ref, dh_ref,
             g_ref, w_hbm, gx_ref, dgin_ref, dzs_ref, w_vmem, dbuf, sem):
        i = pl.program_id(0)

        @pl.when(i == 0)
        def _():
            cp = pltpu.make_async_copy(w_hbm, w_vmem, sem)
            cp.start()
            dgin_ref[...] = jnp.zeros_like(dgin_ref)
            cp.wait()

        dbuf[0:tm, :] = ddc_ref[...]
        dbuf[tm:, :] = jnp.where(i < n_tiles - 1, ddn_ref[...], 0.0)
        row = lax.broadcasted_iota(jnp.int32, (tm, POOL_GROUP_DIM), 0) + i * tm
        du = []
        for g, w in enumerate(POOL_WINDOWS):
            cols = slice(g * POOL_GROUP_DIM, (g + 1) * POOL_GROUP_DIM)
            fsum = dbuf[0:tm, cols]
            for kk in range(1, w):
                fsum = fsum + dbuf[kk:kk + tm, cols]
            du.append(fsum - dbuf[0:tm, cols] * jnp.minimum(row + 1, w).astype(F32))
        du = jnp.concatenate(du, axis=1).astype(BF16)

        dz = [dzq_ref[...], dzkv_ref[...], dzkr_ref[...], dga_ref[...], du, dgp_ref[...], dgm_ref[...]]
        dz = jnp.concatenate([d[:, :w] for d, (w, _) in zip(dz, IN_SEGMENTS)], axis=1)
        for j in range(N_CHIPS):
            dzs_ref[j] = dz[:, j * SHARD_COLS:(j + 1) * SHARD_COLS].T
        dhn = _dot(dz, w_vmem[...])

        g = g_ref[...]
        _, xhat, r = _rms_fwd(x_ref[...], g)
        dgin_ref[...] += _colsum(dhn * xhat)
        gx_ref[...] = dh_ref[...] + _rms_bwd(dhn, xhat, r, g)

    any_spec = pl.BlockSpec(memory_space=pl.ANY)
    seg_w = [wide for _, wide in IN_SEGMENTS]
    return pl.pallas_call(
        body,
        name="inproj_bwd_x",
        grid=(n_tiles,),
        in_specs=[
            _row_spec(tm, seg_w[0]), _row_spec(tm, seg_w[1]), _row_spec(tm, seg_w[2]),
            _row_spec(tm, seg_w[3]), _row_spec(tm, seg_w[u_seg]),
            pl.BlockSpec((POOL_HALO, POOL_WIDTH), lambda i: (jnp.minimum((i + 1) * halo_per_tile, n_halo - 1), 0)),
            _row_spec(tm, seg_w[5]), _row_spec(tm, seg_w[6]),
            _row_spec(tm, D_MODEL), _row_spec(tm, D_MODEL),
            _full_spec((1, D_MODEL)), any_spec,
        ],
        out_specs=[_row_spec(tm, D_MODEL), _full_spec((1, D_MODEL)),
                   pl.BlockSpec((N_CHIPS, SHARD_COLS, tm), lambda i: (0, 0, i))],
        out_shape=[jax.ShapeDtypeStruct((s, D_MODEL), F32), jax.ShapeDtypeStruct((1, D_MODEL), F32),
                   jax.ShapeDtypeStruct((N_CHIPS, SHARD_COLS, s), BF16)],
        scratch_shapes=[
            pltpu.VMEM((IN_TOTAL, D_MODEL), BF16),
            pltpu.VMEM((tm + POOL_HALO, POOL_WIDTH), F32),
            pltpu.SemaphoreType.DMA,
        ],
        compiler_params=pltpu.CompilerParams(dimension_semantics=("arbitrary",), vmem_limit_bytes=VMEM_LIMIT),
    )(dzq, dzkv, dzkr, dgattn, ddc, ddc, dgpool, dgmerge, x, dh, norm_in, w_in_t.reshape(IN_TOTAL, D_MODEL))


def _inproj_bwd_w(order, dz_sh, hn, g_uq, g_ukv, gs, tm):
    s = hn.shape[0]
    n_tiles = s // tm
    hc = D_MODEL // 2
    red = _Reduce(COMM_PARAMS[1:3])
    small = _SmallSum(gs.shape[0])
    n_red = len(red.scratch)

    def body(order_ref, dz_ref, hn_ref, guq_hbm, gukv_hbm, gs_ref, gw_hbm, guq_out, gukv_out, gsum_ref,
             acc, pm_w, a_w, b_w, r_w, w_send, w_recv, w_local, *more_scratch):
        ph, i = pl.program_id(0), pl.program_id(1)
        x, y, c = lax.axis_index("x"), lax.axis_index("y"), lax.axis_index("c")
        k = 2 * x + y
        me, sibling = (x, y, c), (x, y, 1 - c)
        chips = _other_chips(x, y)
        shard_of_phase = [2 * cx + cy for cx, cy in chips] + [k]
        copy = _remote_copier(w_send, w_recv)
        red.bind([guq_hbm, gukv_hbm], [guq_out, gukv_out], more_scratch[:n_red])
        small.bind(gs_ref, gsum_ref, more_scratch[n_red:])
        mine = pl.ds(pl.multiple_of(c * hc, hc), hc)
        theirs = pl.ds(pl.multiple_of((1 - c) * hc, hc), hc)

        def to_sibling(f):
            j = shard_of_phase[f]
            return copy(f, pm_w.at[j, 1 - c], a_w.at[j], sibling)

        def pair_sum(f):
            cx, cy = chips[f]
            return copy(4 + f, pm_w.at[shard_of_phase[f], c], b_w.at[f], (cx, cy, c))

        def finished():
            return copy(7, r_w, gw_hbm.at[:, mine], sibling)

        @pl.when(jnp.logical_and(ph == 0, i == 0))
        def _():
            red.start()
            small.start()

        part = _dot(dz_ref[0], hn_ref[...])

        @pl.when(i == 0)
        def _():
            acc[...] = part

        @pl.when(i > 0)
        def _():
            acc[...] += part

        for f in range(3):
            @pl.when(jnp.logical_and(ph == f + 1, i == 0))
            def _(f=f):
                j = shard_of_phase[f]
                copy(f, a_w.at[j], a_w.at[j], me).wait_recv()
                pm_w[j, c] = (pm_w[j, c].astype(F32) + a_w[j].astype(F32)).astype(BF16)
                pair_sum(f).start()
                if f == 0:
                    red.exchange()

        for f in range(4):
            @pl.when(jnp.logical_and(ph == f, i == n_tiles - 1))
            def _(f=f):
                j = shard_of_phase[f]
                pm_w[j, 0] = acc[:, :hc].astype(BF16)
                pm_w[j, 1] = acc[:, hc:].astype(BF16)
                to_sibling(f).start()
                if f < 3:
                    return
                copy(3, a_w.at[k], a_w.at[k], me).wait_recv()
                r_w[...] = pm_w[k, c].astype(F32) + a_w[k].astype(F32)
                for g in range(3):
                    copy(4 + g, b_w.at[g], b_w.at[g], me).wait_recv()
                    r_w[...] = r_w[...] + b_w[g].astype(F32)
                store = pltpu.make_async_copy(r_w, gw_hbm.at[:, mine], w_local)
                store.start()
                finished().start()
                red.finish()
                small.finish()
                copy(7, gw_hbm.at[:, theirs], gw_hbm.at[:, theirs], me).wait_recv()
                store.wait()
                for g in range(4):
                    to_sibling(g).wait_send()
                for g in range(3):
                    pair_sum(g).wait_send()
                finished().wait_send()

    any_spec = pl.BlockSpec(memory_space=pl.ANY)
    n_sem = 8
    grid_spec = pltpu.PrefetchScalarGridSpec(
        num_scalar_prefetch=1,
        grid=(N_CHIPS, n_tiles),
        in_specs=[
            pl.BlockSpec((1, SHARD_COLS, tm), lambda ph, i, order: (order[ph], 0, i)),
            pl.BlockSpec((tm, D_MODEL), lambda ph, i, order: (i, 0)),
            any_spec, any_spec,
            pl.BlockSpec(small.spec_shape, lambda ph, i, order: (0, 0)),
        ],
        out_specs=[any_spec, any_spec, any_spec, pl.BlockSpec(small.spec_shape, lambda ph, i, order: (0, 0))],
        scratch_shapes=[
            pltpu.VMEM((SHARD_COLS, D_MODEL), F32),
            pltpu.VMEM((N_CHIPS, 2, SHARD_COLS, hc), BF16),
            pltpu.VMEM((N_CHIPS, SHARD_COLS, hc), BF16),
            pltpu.VMEM((3, SHARD_COLS, hc), BF16),
            pltpu.VMEM((SHARD_COLS, hc), F32),
            pltpu.SemaphoreType.DMA((n_sem,)), pltpu.SemaphoreType.DMA((n_sem,)), pltpu.SemaphoreType.DMA,
        ] + red.scratch + small.scratch,
    )
    out = pl.pallas_call(
        body,
        name="inproj_bwd_w",
        grid_spec=grid_spec,
        out_shape=[jax.ShapeDtypeStruct((SHARD_COLS, D_MODEL), F32)] + red.out_shape
        + [small.out_shape],
        compiler_params=pltpu.CompilerParams(dimension_semantics=("arbitrary", "arbitrary"),
                                             vmem_limit_bytes=VMEM_LIMIT),
    )(order, dz_sh, hn, g_uq, g_ukv, gs)
    return out[0], out[1], out[2], out[3]


def _other_chips(x, y):
    return ((1 - x, 1 - y), (1 - x, y), (x, 1 - y))


def _half(ref, axis, size, c, lead=()):
    window = pl.ds(pl.multiple_of(c * size, size), size)
    if axis == 0:
        return ref.at[(*lead, window, slice(None))]
    return ref.at[(*lead, slice(None), window)]


def _half_shape(rows, cols, axis, size):
    return (size, cols) if axis == 0 else (rows, size)


def _remote_copier(send_sems, recv_sems):
    def copy(sem, src, dst, to):
        return pltpu.make_async_remote_copy(src_ref=src, dst_ref=dst, send_sem=send_sems.at[sem],
                                            recv_sem=recv_sems.at[sem], device_id=to, device_id_type=MESH)
    return copy


class _Gather:
    def __init__(self, params):
        self.params = params
        n = len(params)
        self.scratch = [pltpu.SemaphoreType.DMA((6 * n,)), pltpu.SemaphoreType.DMA((6 * n,)),
                        pltpu.SemaphoreType.DMA((n,))]
        self.out_shape = [jax.ShapeDtypeStruct((N_CHIPS, r, cc), BF16) for _, r, cc, _, _ in params]

    def bind(self, ins, outs, scratch):
        self.ins, self.outs = ins, outs
        send_sems, recv_sems, self.local_sems = scratch
        self.copy = _remote_copier(send_sems, recv_sems)
        self.x, self.y, self.c = lax.axis_index("x"), lax.axis_index("y"), lax.axis_index("c")
        self.k = 2 * self.x + self.y
        self.chips = _other_chips(self.x, self.y)

    def _local(self, p):
        return pltpu.make_async_copy(self.ins[p], self.outs[p].at[self.k], self.local_sems.at[p])

    def _first(self, p, j):
        _, _, _, axis, size = self.params[p]
        cx, cy = self.chips[j]
        return self.copy(6 * p + j, _half(self.ins[p], axis, size, self.c),
                         _half(self.outs[p], axis, size, self.c, (self.k,)), (cx, cy, self.c))

    def _relay(self, p, j, half_of):
        _, _, _, axis, size = self.params[p]
        cx, cy = self.chips[j]
        block = _half(self.outs[p], axis, size, half_of, (2 * cx + cy,))
        return self.copy(6 * p + 3 + j, block, block, (self.x, self.y, 1 - self.c))

    def start(self):
        for p in range(len(self.params)):
            self._local(p).start()
            for j in (1, 2, 0):
                self._first(p, j).start()

    def relay_one(self, p, j):
        _, _, _, axis, size = self.params[p]
        cx, cy = self.chips[j]
        landed = _half(self.outs[p], axis, size, self.c, (2 * cx + cy,))
        self.copy(6 * p + j, landed, landed, (self.x, self.y, self.c)).wait_recv()
        self._relay(p, j, self.c).start()

    def await_one(self, p, j):
        self._relay(p, j, 1 - self.c).wait_recv()

    def wait_sends(self):
        for p in range(len(self.params)):
            for j in range(3):
                self._first(p, j).wait_send()
                self._relay(p, j, self.c).wait_send()
            self._local(p).wait()

    def relay(self):
        for j in range(3):
            for p in range(len(self.params)):
                self.relay_one(p, j)

    def finish(self):
        for j in range(3):
            for p in range(len(self.params)):
                self.await_one(p, j)
        self.wait_sends()


class _Reduce:
    def __init__(self, params):
        self.params = params
        n = len(params)
        halves = [_half_shape(r, cc, axis, size) for _, r, cc, axis, size in params]
        self.scratch = ([pltpu.VMEM((N_CHIPS, *h), BF16) for h in halves]
                        + [pltpu.VMEM((N_CHIPS, *h), BF16) for h in halves]
                        + [pltpu.VMEM((3, *h), BF16) for h in halves]
                        + [pltpu.VMEM(h, F32) for h in halves]
                        + [pltpu.SemaphoreType.DMA((5 * n,)), pltpu.SemaphoreType.DMA((5 * n,)),
                           pltpu.SemaphoreType.DMA((2 * n,))])
        self.out_shape = [jax.ShapeDtypeStruct((r, cc), F32) for _, r, cc, _, _ in params]

    def bind(self, g_in, g_out, scratch):
        n = len(self.params)
        self.g_in, self.g_out = g_in, g_out
        self.pm, self.a_buf = scratch[0:n], scratch[n:2 * n]
        self.b_buf, self.r_buf = scratch[2 * n:3 * n], scratch[3 * n:4 * n]
        send_sems, recv_sems, self.local_sems = scratch[4 * n:]
        self.copy = _remote_copier(send_sems, recv_sems)
        self.x, self.y, self.c = lax.axis_index("x"), lax.axis_index("y"), lax.axis_index("c")
        self.k = 2 * self.x + self.y
        self.chips = _other_chips(self.x, self.y)
        self.me = (self.x, self.y, self.c)
        self.sibling = (self.x, self.y, 1 - self.c)

    def _load(self, p):
        _, _, _, axis, size = self.params[p]
        return pltpu.make_async_copy(_half(self.g_in[p], axis, size, self.c, (slice(None),)), self.pm[p],
                                     self.local_sems.at[p])

    def _to_sibling(self, p):
        _, _, _, axis, size = self.params[p]
        return self.copy(5 * p, _half(self.g_in[p], axis, size, 1 - self.c, (slice(None),)), self.a_buf[p],
                         self.sibling)

    def _pair_sum(self, p, j):
        cx, cy = self.chips[j]
        return self.copy(5 * p + 1 + j, self.pm[p].at[2 * cx + cy], self.b_buf[p].at[j], (cx, cy, self.c))

    def _store(self, p):
        _, _, _, axis, size = self.params[p]
        n = len(self.params)
        return pltpu.make_async_copy(self.r_buf[p], _half(self.g_out[p], axis, size, self.c),
                                     self.local_sems.at[n + p])

    def _finished(self, p):
        _, _, _, axis, size = self.params[p]
        return self.copy(5 * p + 4, self.r_buf[p], _half(self.g_out[p], axis, size, self.c), self.sibling)

    def start(self):
        for p in range(len(self.params)):
            self._load(p).start()
            self._to_sibling(p).start()

    def exchange(self):
        for p in range(len(self.params)):
            self._load(p).wait()
            self.copy(5 * p, self.a_buf[p], self.a_buf[p], self.me).wait_recv()
            for j, (cx, cy) in enumerate(self.chips):
                kj = 2 * cx + cy
                self.pm[p][kj] = (self.pm[p][kj].astype(F32) + self.a_buf[p][kj].astype(F32)).astype(BF16)
                self._pair_sum(p, j).start()
            self.r_buf[p][...] = self.pm[p][self.k].astype(F32) + self.a_buf[p][self.k].astype(F32)

    def finish(self):
        for p, (_, _, _, axis, size) in enumerate(self.params):
            for j in range(3):
                self.copy(5 * p + 1 + j, self.b_buf[p].at[j], self.b_buf[p].at[j], self.me).wait_recv()
                self.r_buf[p][...] = self.r_buf[p][...] + self.b_buf[p][j].astype(F32)
            self._store(p).start()
            self._finished(p).start()
        for p, (_, _, _, axis, size) in enumerate(self.params):
            theirs = _half(self.g_out[p], axis, size, 1 - self.c)
            self.copy(5 * p + 4, theirs, theirs, self.me).wait_recv()
            self._store(p).wait()
            self._to_sibling(p).wait_send()
            for j in range(3):
                self._pair_sum(p, j).wait_send()
            self._finished(p).wait_send()


class _SmallSum:
    def __init__(self, rows):
        self.rows = rows
        self.scratch = [pltpu.VMEM((N_DEV, rows, LANES), F32),
                        pltpu.SemaphoreType.DMA((N_DEV - 1,)), pltpu.SemaphoreType.DMA((N_DEV - 1,))]
        self.out_shape = jax.ShapeDtypeStruct((rows, LANES), F32)
        self.spec_shape = (rows, LANES)

    def bind(self, src, dst, scratch):
        self.src, self.dst = src, dst
        self.buf, send_sems, recv_sems = scratch
        self.copy = _remote_copier(send_sems, recv_sems)
        self.x, self.y, self.c = lax.axis_index("x"), lax.axis_index("y"), lax.axis_index("c")

    def _send(self, f):
        fx, fy, fc = [(a, b, d) for a in (0, 1) for b in (0, 1) for d in (0, 1)][f]
        x, y, c = self.x, self.y, self.c
        peer = (1 - x if fx else x, 1 - y if fy else y, 1 - c if fc else c)
        return self.copy(f - 1, self.src, self.buf.at[f], peer)

    def start(self):
        for f in range(1, N_DEV):
            self._send(f).start()
        self.buf[0] = self.src[...]

    def finish(self):
        me = (self.x, self.y, self.c)
        for f in range(1, N_DEV):
            self.copy(f - 1, self.buf.at[f], self.buf.at[f], me).wait_recv()
        dev = 4 * self.x + 2 * self.y + self.c
        total = self.buf[dev]
        for d in range(1, N_DEV):
            total = total + self.buf[jnp.bitwise_xor(dev, d)]
        self.dst[...] = total
        for f in range(1, N_DEV):
            self._send(f).wait_send()


def _adamw_math(w, g, m, v):
    m = ADAM_B1 * m + (1.0 - ADAM_B1) * g
    v = ADAM_B2 * v + (1.0 - ADAM_B2) * (g * g)
    m_hat = m / (1.0 - ADAM_B1 ** ADAM_STEP)
    v_hat = v / (1.0 - ADAM_B2 ** ADAM_STEP)
    delta = -ADAM_LR * (m_hat / (jnp.sqrt(v_hat) + ADAM_EPS) + ADAM_WD * w)
    return delta, m, v


def _adamw_tiled(w, g, m, v, tm):
    rows, cols = w.shape

    def body(w_ref, g_ref, m_ref, v_ref, d_ref, nm_ref, nv_ref):
        d_ref[...], nm_ref[...], nv_ref[...] = _adamw_math(w_ref[...], g_ref[...], m_ref[...], v_ref[...])

    spec = _row_spec(tm, cols)
    return pl.pallas_call(
        body,
        name="adamw_w_in",
        grid=(rows // tm,),
        in_specs=[spec] * 4,
        out_specs=[spec] * 3,
        out_shape=[jax.ShapeDtypeStruct(w.shape, F32)] * 3,
        compiler_params=pltpu.CompilerParams(dimension_semantics=("parallel",), vmem_limit_bytes=VMEM_LIMIT),
    )(w, g, m, v)


def _adamw_many(ws, gs, ms, vs):
    n = len(ws)

    def body(*refs):
        ins, outs = refs[:4 * n], refs[4 * n:]
        for i in range(n):
            d, nm, nv = _adamw_math(ins[i][...], ins[n + i][...], ins[2 * n + i][...], ins[3 * n + i][...])
            outs[i][...] = d
            outs[n + i][...] = nm
            outs[2 * n + i][...] = nv

    vmem_spec = pl.BlockSpec(memory_space=pltpu.VMEM)
    shapes = [jax.ShapeDtypeStruct(w.shape, F32) for w in ws]
    out = pl.pallas_call(
        body,
        name="adamw_small",
        in_specs=[vmem_spec] * (4 * n),
        out_specs=[vmem_spec] * (3 * n),
        out_shape=shapes * 3,
        compiler_params=pltpu.CompilerParams(vmem_limit_bytes=VMEM_LIMIT),
    )(*ws, *gs, *ms, *vs)
    return out[:n], out[n:2 * n], out[2 * n:]


def _pack_rows(parts, rows, dtype):
    flat = jnp.concatenate([p.reshape(-1).astype(dtype) for p in parts])
    flat = jnp.concatenate([flat, jnp.zeros((rows * LANES - flat.shape[0],), dtype)])
    return flat.reshape(rows, LANES)


def _unpack_rows(packed, shapes):
    flat = packed.reshape(-1)
    out, off = [], 0
    for _, shp in shapes:
        n = int(np.prod(shp))
        out.append(flat[off:off + n].reshape(shp))
        off += n
    return out


def _rope_tables(s):
    half = QK_ROPE_DIM // 2
    inv_freq = np.float32(ROPE_THETA) ** (-np.arange(half, dtype=np.float32) / np.float32(half))
    ang = (np.arange(s, dtype=np.float32)[:, None] * inv_freq[None, :]).astype(np.float32)
    cos, sin = np.cos(ang.astype(np.float64)).astype(np.float32), np.sin(ang.astype(np.float64)).astype(np.float32)
    z16 = np.zeros((s, half), np.float32)
    z32 = np.zeros((s, HEAD_PAD - QK_NOPE_DIM - QK_ROPE_DIM), np.float32)
    z64 = np.zeros((s, QK_NOPE_DIM), np.float32)
    rc = np.concatenate([np.ones((s, QK_NOPE_DIM), np.float32), cos, cos, z32], axis=1)
    rsa = np.concatenate([z64, -sin, z16, z32], axis=1)
    rsb = np.concatenate([z64, z16, sin, z32], axis=1)
    return jnp.asarray(rc), jnp.asarray(rsa), jnp.asarray(rsb)


def kernel(x, norm_in, w_in, q_norm, w_uq, kv_norm, w_ukv, pool_w, pool_scale, w_branch_attn, w_branch_pool, w_out, norm_final, loss_target, m_norm_in, m_w_in, m_q_norm, m_w_uq, m_kv_norm, m_w_ukv, m_pool_w, m_pool_scale, m_w_branch_attn, m_w_branch_pool, m_w_out, m_norm_final, v_norm_in, v_w_in, v_q_norm, v_w_uq, v_kv_norm, v_w_ukv, v_pool_w, v_pool_scale, v_w_branch_attn, v_w_branch_pool, v_w_out, v_norm_final):
    s = x.shape[1]
    t_att, t_row = _tiles(s)
    x2 = x.reshape(s, D_MODEL)
    tgt = loss_target.reshape(s, D_MODEL)

    local = [w_in.T, w_uq.reshape(96, 768), w_ukv.reshape(64, 1024), w_branch_attn, w_branch_pool, w_out]
    local = [a.astype(BF16) for a in local]
    cx, cy = lax.axis_index("x"), lax.axis_index("y")
    others = [2 * ox + oy for ox, oy in _other_chips(cx, cy)]
    hn, z_sh, (w_in_t, w_uq_all, w_ukv_all) = _inproj_fwd(
        jnp.stack([2 * cx + cy, others[1], others[2], others[0]]).astype(jnp.int32), x2, norm_in.reshape(1, -1),
        local[:3], 4 * t_row)
    w_uq_f = w_uq_all.reshape(Q_LORA_RANK, MLA_HEADS, QK_NOPE_DIM + QK_ROPE_DIM)
    w_ukv_f = w_ukv_all.reshape(KV_LORA_RANK, MLA_HEADS, QK_NOPE_DIM + V_HEAD_DIM)
    hw = MLA_HEADS * HEAD_PAD
    wuq_p = jnp.pad(w_uq_f, ((0, 0), (0, 0), (0, HEAD_PAD - QK_NOPE_DIM - QK_ROPE_DIM))).reshape(Q_LORA_RANK, hw)
    wk_p = jnp.pad(w_ukv_f[:, :, :QK_NOPE_DIM], ((0, 0), (0, 0), (0, HEAD_PAD - QK_NOPE_DIM))).reshape(KV_LORA_RANK, hw)
    wv = w_ukv_f[:, :, QK_NOPE_DIM:].reshape(KV_LORA_RANK, MLA_WIDTH)
    rc, rsa, rsb = _rope_tables(s)
    g_in = norm_in.reshape(1, -1)
    g_q = q_norm.reshape(1, -1)
    g_kv = kv_norm.reshape(1, -1)
    g_f = norm_final.reshape(1, -1)
    ps = pool_scale.reshape(1, -1)
    pw_bf = pool_w.astype(BF16)

    q, k, v, q_t, v_t = _qkv_fwd(z_sh, g_q, g_kv, wuq_p, wk_p, wv, rc, rsa, rsb, 2 * t_row)
    o, lse, (w_ba_all, w_bp_all, w_out_all) = _attn_fwd(q_t, k, v_t, local[3:], t_att)
    w_out_f = w_out_all.reshape(D_MODEL, D_MODEL)

    (do, delta, dgattn, dgpool, dgmerge, ddc, dh, sq_err, d_w_out, d_w_ba, d_w_bp, d_pool_w, d_pool_scale,
     d_norm_final) = _mid(o, z_sh, x2, tgt, pw_bf, ps, w_ba_all, w_bp_all, w_out_f, g_f, t_row)

    late_grads = [d_w_ba, d_w_bp, d_w_out.reshape(N_CHIPS, 256, D_MODEL)]
    small_mid = dict(pool_scale=d_pool_scale, norm_final=d_norm_final, pool_w=d_pool_w, sq_err=sq_err)
    gs_mid = _pack_rows([small_mid[n] for n, _ in SMALL_MID], _small_rows(SMALL_MID), F32)
    dq, dk_t, dv_t, (g_w_ba, g_w_bp, g_w_out), g_small_mid = _attn_bwd(q, q_t, k, v, do, lse, delta, late_grads,
                                                                      gs_mid, t_att)
    g_pool_scale, g_norm_final, g_pool_w, sq_err_all = _unpack_rows(g_small_mid, SMALL_MID)
    dzq, dzkv, dzkr, d_wuq_p, d_wk_p, d_wv, d_q_norm, d_kv_norm = _qkv_bwd(
        dq, dk_t, dv_t, z_sh, g_q, g_kv, wuq_p, wk_p, wv, rc, rsa, rsb, 2 * t_row)
    grad_x, d_norm_in, dz_sh = _inproj_bwd_x(dzq, dzkv, dzkr, dgattn, ddc, dgpool, dgmerge, x2, dh, g_in, w_in_t,
                                             2 * t_row)

    d_w_uq = d_wuq_p.reshape(Q_LORA_RANK, MLA_HEADS, HEAD_PAD)[:, :, :QK_NOPE_DIM + QK_ROPE_DIM]
    d_w_ukv = jnp.concatenate([d_wk_p.reshape(KV_LORA_RANK, MLA_HEADS, HEAD_PAD)[:, :, :QK_NOPE_DIM],
                               d_wv.reshape(KV_LORA_RANK, MLA_HEADS, V_HEAD_DIM)], axis=2)
    small_late = dict(norm_in=d_norm_in, q_norm=d_q_norm, kv_norm=d_kv_norm)
    gs = _pack_rows([small_late[n] for n, _ in SMALL_LATE], _small_rows(SMALL_LATE), F32)
    order = jnp.stack(others + [2 * cx + cy]).astype(jnp.int32)
    g_w_in_t, g_w_uq, g_w_ukv, g_small = _inproj_bwd_w(
        order, dz_sh, hn, d_w_uq.reshape(N_CHIPS, 96, 768).astype(BF16),
        d_w_ukv.reshape(N_CHIPS, 64, 1024).astype(BF16), gs, 4 * t_row)
    g_norm_in, g_q_norm, g_kv_norm = _unpack_rows(g_small, SMALL_LATE)
    g_w_uq = g_w_uq.reshape(w_uq.shape)
    g_w_ukv = g_w_ukv.reshape(w_ukv.shape)

    dl_w_in, nm_w_in, nv_w_in = (a.T for a in _adamw_tiled(w_in.T, g_w_in_t, m_w_in.T, v_w_in.T, 152))

    def two_d(a):
        return a.reshape(1, -1) if a.ndim == 1 else a

    names = ["norm_in", "q_norm", "w_uq", "kv_norm", "w_ukv", "pool_w", "pool_scale", "w_branch_attn",
             "w_branch_pool", "w_out", "norm_final"]
    ws = dict(norm_in=norm_in, q_norm=q_norm, w_uq=w_uq, kv_norm=kv_norm, w_ukv=w_ukv, pool_w=pool_w,
              pool_scale=pool_scale, w_branch_attn=w_branch_attn, w_branch_pool=w_branch_pool, w_out=w_out,
              norm_final=norm_final)
    gsd = dict(norm_in=g_norm_in, q_norm=g_q_norm, w_uq=g_w_uq, kv_norm=g_kv_norm, w_ukv=g_w_ukv, pool_w=g_pool_w,
               pool_scale=g_pool_scale, w_branch_attn=g_w_ba, w_branch_pool=g_w_bp, w_out=g_w_out,
               norm_final=g_norm_final)
    msd = dict(norm_in=m_norm_in, q_norm=m_q_norm, w_uq=m_w_uq, kv_norm=m_kv_norm, w_ukv=m_w_ukv, pool_w=m_pool_w,
               pool_scale=m_pool_scale, w_branch_attn=m_w_branch_attn, w_branch_pool=m_w_branch_pool, w_out=m_w_out,
               norm_final=m_norm_final)
    vsd = dict(norm_in=v_norm_in, q_norm=v_q_norm, w_uq=v_w_uq, kv_norm=v_kv_norm, w_ukv=v_w_ukv, pool_w=v_pool_w,
               pool_scale=v_pool_scale, w_branch_attn=v_w_branch_attn, w_branch_pool=v_w_branch_pool, w_out=v_w_out,
               norm_final=v_norm_final)
    dls, nms, nvs = _adamw_many([two_d(ws[n]) for n in names], [two_d(gsd[n]) for n in names],
                                [two_d(msd[n]) for n in names], [two_d(vsd[n]) for n in names])

    grads = dict(gsd)
    grads["w_in"] = g_w_in_t.T
    delta_w = {n: d.reshape(ws[n].shape) for n, d in zip(names, dls)}
    new_m = {n: d.reshape(ws[n].shape) for n, d in zip(names, nms)}
    new_v = {n: d.reshape(ws[n].shape) for n, d in zip(names, nvs)}
    delta_w["w_in"], new_m["w_in"], new_v["w_in"] = dl_w_in, nm_w_in, nv_w_in
    ws["w_in"] = w_in

    order = ["norm_in", "w_in", "q_norm", "w_uq", "kv_norm", "w_ukv", "pool_w", "pool_scale", "w_branch_attn",
             "w_branch_pool", "w_out", "norm_final"]
    loss = 0.5 * jnp.sum(sq_err_all) / D_MODEL
    return (loss, grad_x.reshape(x.shape),
            *[grads[n].reshape(ws[n].shape) for n in order],
            *[delta_w[n] for n in order], *[new_m[n] for n in order], *[new_v[n] for n in order])
```

```python
import functools

import jax
import jax.numpy as jnp
import numpy as np
from jax import lax
from jax.experimental import pallas as pl
from jax.experimental.pallas import tpu as pltpu

F32 = jnp.float32
BF16 = jnp.bfloat16
MESH = pl.DeviceIdType.MESH

D_MODEL = 1024
CHUNK = 64
MLA_HEADS = 8
QK_NOPE_DIM = 64
QK_ROPE_DIM = 32
V_HEAD_DIM = 64
Q_LORA_RANK = 384
KV_LORA_RANK = 256
MLA_WIDTH = MLA_HEADS * V_HEAD_DIM
ROPE_THETA = 10000.0
POOL_WINDOWS = (2, 4, 8, 16)
POOL_WIDTH = 512
POOL_GROUP_DIM = 128
BRANCH_COLS = D_MODEL // 4
FWD_HEADS = 8
BWD_HEADS = 4
POOL_HALO = 16
EPS = 1e-6
IN_TOTAL = 4256
HEAD_PAD = 128
ATT_SCALE = (QK_NOPE_DIM + QK_ROPE_DIM) ** -0.5
ATT_SCALE_LOG2E = ATT_SCALE * 1.4426950408889634

ADAM_LR = 0.001
ADAM_B1 = 0.9
ADAM_B2 = 0.999
ADAM_EPS = 1e-08
ADAM_WD = 0.01
ADAM_STEP = 10

N_CHIPS = 4
N_DEV = 8
LANES = 128
VMEM_LIMIT = 60 * 1024 * 1024

IN_SEGMENTS = ((384, 384), (256, 256), (32, HEAD_PAD), (512, 512), (512, 512), (512, 512), (2048, 2048))
SHARD_COLS = IN_TOTAL // N_CHIPS
ZQ_COLS = slice(0, 384)
ZKV_COLS = slice(384, 640)
ZKR_TILE = slice(640, 768)


def _shard_pieces():
    bounds, off = [], 0
    for w, _ in IN_SEGMENTS:
        bounds.append((off, off + w))
        off += w
    out = []
    for j in range(N_CHIPS):
        lo, hi = SHARD_COLS * j, SHARD_COLS * (j + 1)
        out.append([(i, max(lo, a) - a, min(hi, b) - a, max(lo, a) - lo)
                    for i, (a, b) in enumerate(bounds) if max(lo, a) < min(hi, b)])
    return out


SHARD_PIECES = _shard_pieces()


def _segment(z_blocks, seg):
    parts = [z_blocks[j][:, col:col + hi - lo]
             for j, pieces in enumerate(SHARD_PIECES) for sg, lo, hi, col in pieces if sg == seg]
    return parts[0] if len(parts) == 1 else jnp.concatenate(parts, axis=1)

COMM_PARAMS = (
    ("w_in", SHARD_COLS, D_MODEL, 1, 512),
    ("w_uq", 96, 768, 0, 48),
    ("w_ukv", 64, 1024, 0, 32),
    ("w_branch_attn", 512, 256, 0, 256),
    ("w_branch_pool", 512, 256, 0, 256),
    ("w_out", 256, 1024, 0, 128),
)

SMALL_MID = (
    ("pool_scale", (512,)),
    ("norm_final", (1024,)),
    ("pool_w", (4, 128, 128)),
    ("sq_err", (8, 128)),
)
SMALL_LATE = (
    ("norm_in", (1024,)),
    ("q_norm", (384,)),
    ("kv_norm", (256,)),
)


def _small_rows(shapes):
    return -(-sum(int(np.prod(s)) for _, s in shapes) // (LANES * 8)) * 8


def _dot(a, b):
    return jnp.dot(a, b, preferred_element_type=F32)


def _dot_nt(a, b):
    return lax.dot_general(a, b, (((1,), (1,)), ((), ())), preferred_element_type=F32)


def _dot_tn(a, b):
    return lax.dot_general(a, b, (((0,), (0,)), ((), ())), preferred_element_type=F32)


def _sigmoid(x):
    return 1.0 / (1.0 + jnp.exp(-x))


def _colsum(x):
    return jnp.sum(x, axis=0, keepdims=True)


def _rms_fwd(x, g):
    r = lax.rsqrt(jnp.mean(x * x, axis=-1, keepdims=True) + EPS)
    xhat = x * r
    return xhat * g, xhat, r


def _rms_bwd(dy, xhat, r, g):
    dxhat = dy * g
    return r * (dxhat - xhat * jnp.mean(dxhat * xhat, axis=-1, keepdims=True))


def _rope(v, c, sa, sb):
    return v * c + pltpu.roll(v, 112, 1) * sa + pltpu.roll(v, 16, 1) * sb


def _unrope(d, c, sa, sb):
    return d * c + pltpu.roll(d * sa, 16, 1) + pltpu.roll(d * sb, 112, 1)


def _row_spec(tm, n):
    return pl.BlockSpec((tm, n), lambda i: (i, 0))


def _full_spec(shape):
    nd = len(shape)
    return pl.BlockSpec(shape, lambda i: (0,) * nd)


def _tiles(s):
    t_att = 512 if s >= 2048 else 128
    t_row = 256 if s >= 1024 else 128
    return t_att, t_row


def _inproj_fwd(order, x, norm_in, early_shards, tm):
    s = x.shape[0]
    n_tiles = s // tm
    gat = _Gather(COMM_PARAMS[:3])
    n_w = len(gat.params)
    arrival = (1, 2, 0)

    def body(order_ref, x_ref, g_ref, *rest):
        w_loc, (hn_ref, z_ref), w_all = rest[:n_w], rest[n_w:n_w + 2], rest[n_w + 2:2 * n_w + 2]
        w_vmem, hn_all, w_sem = rest[2 * n_w + 2:2 * n_w + 5]
        gat.bind(w_loc, w_all, rest[2 * n_w + 5:])
        ph, i = pl.program_id(0), pl.program_id(1)
        pl.when(jnp.logical_and(ph == 0, i == 0))(gat.start)

        @pl.when(jnp.logical_and(ph == 0, i == 0))
        def _():
            cp = pltpu.make_async_copy(w_loc[0], w_vmem, w_sem)
            cp.start()
            cp.wait()

        for f in range(3):
            @pl.when(jnp.logical_and(ph == f + 1, i == 0))
            def _(f=f):
                gat.relay_one(0, arrival[f])
                gat.await_one(0, arrival[f])
                cp = pltpu.make_async_copy(w_all[0].at[order_ref[ph]], w_vmem, w_sem)
                cp.start()
                cp.wait()

        rows = pl.ds(pl.multiple_of(i * tm, tm), tm)

        @pl.when(ph == 0)
        def _():
            hn, _, _ = _rms_fwd(x_ref[...], g_ref[...])
            hn = hn.astype(BF16)
            hn_ref[...] = hn
            hn_all[rows, :] = hn

        z_ref[0] = _dot_nt(hn_all[rows, :], w_vmem[...])

        @pl.when(jnp.logical_and(ph == N_CHIPS - 1, i == n_tiles - 1))
        def _():
            for p in range(1, n_w):
                for j in range(3):
                    gat.relay_one(p, j)
            for p in range(1, n_w):
                for j in range(3):
                    gat.await_one(p, j)
            gat.wait_sends()

    def tile_in_phase0(ph, i, order):
        return (jnp.where(ph == 0, i, n_tiles - 1), 0)

    any_spec = pl.BlockSpec(memory_space=pl.ANY)
    grid_spec = pltpu.PrefetchScalarGridSpec(
        num_scalar_prefetch=1,
        grid=(N_CHIPS, n_tiles),
        in_specs=[pl.BlockSpec((tm, D_MODEL), tile_in_phase0),
                  pl.BlockSpec((1, D_MODEL), lambda ph, i, order: (0, 0))] + [any_spec] * n_w,
        out_specs=[pl.BlockSpec((tm, D_MODEL), tile_in_phase0),
                   pl.BlockSpec((1, tm, SHARD_COLS), lambda ph, i, order: (order[ph], i, 0))] + [any_spec] * n_w,
        scratch_shapes=[pltpu.VMEM((SHARD_COLS, D_MODEL), BF16), pltpu.VMEM((s, D_MODEL), BF16),
                        pltpu.SemaphoreType.DMA] + gat.scratch,
    )
    out = pl.pallas_call(
        body,
        name="inproj_fwd",
        grid_spec=grid_spec,
        out_shape=[jax.ShapeDtypeStruct((s, D_MODEL), BF16), jax.ShapeDtypeStruct((N_CHIPS, s, SHARD_COLS), F32)]
        + gat.out_shape,
        compiler_params=pltpu.CompilerParams(dimension_semantics=("arbitrary", "arbitrary"),
                                             vmem_limit_bytes=VMEM_LIMIT),
    )(order, x, norm_in, *early_shards)
    return out[0], out[1], out[2:]


def _qkv_fwd(z_sh, q_norm, kv_norm, wuq_p, wk_p, wv, rc, rsa, rsb, tm):
    s = z_sh.shape[1]
    hw = MLA_HEADS * HEAD_PAD

    def body(z_ref, gq_ref, gkv_ref, wuq_ref, wk_ref, wv_ref, c_ref, sa_ref, sb_ref,
             q_ref, k_ref, v_ref, qt_ref, vt_ref):
        c, sa, sb = c_ref[...], sa_ref[...], sb_ref[...]
        z0 = z_ref[0]
        cq, _, _ = _rms_fwd(z0[:, ZQ_COLS], gq_ref[...])
        qf = _dot(cq.astype(BF16), wuq_ref[...])
        ckv, _, _ = _rms_fwd(z0[:, ZKV_COLS], gkv_ref[...])
        ckv = ckv.astype(BF16)
        kn = _dot(ckv, wk_ref[...])
        lane = lax.broadcasted_iota(jnp.int32, (tm, HEAD_PAD), 1)
        zkr = jnp.where(lane < QK_ROPE_DIM, z0[:, ZKR_TILE], 0.0)
        kr = _rope(pltpu.roll(zkr, 64, 1), c, sa, sb)
        for h in range(MLA_HEADS):
            cols = slice(h * HEAD_PAD, (h + 1) * HEAD_PAD)
            qh = _rope(qf[:, cols], c, sa, sb)
            q_ref[:, cols] = qh.astype(BF16)
            qt_ref[cols, :] = qh.T.astype(BF16)
            k_ref[:, cols] = (kn[:, cols] + kr).astype(BF16)
        vf = _dot(ckv, wv_ref[...])
        v_ref[...] = vf.astype(BF16)
        vt_ref[...] = vf.T.astype(BF16)

    return pl.pallas_call(
        body,
        name="qkv_fwd",
        grid=(s // tm,),
        in_specs=[
            pl.BlockSpec((1, tm, SHARD_COLS), lambda i: (0, i, 0)),
            _full_spec((1, Q_LORA_RANK)), _full_spec((1, KV_LORA_RANK)),
            _full_spec((Q_LORA_RANK, hw)), _full_spec((KV_LORA_RANK, hw)), _full_spec((KV_LORA_RANK, MLA_WIDTH)),
            _row_spec(tm, HEAD_PAD), _row_spec(tm, HEAD_PAD), _row_spec(tm, HEAD_PAD),
        ],
        out_specs=[_row_spec(tm, hw), _row_spec(tm, hw), _row_spec(tm, MLA_WIDTH),
                   pl.BlockSpec((hw, tm), lambda i: (0, i)), pl.BlockSpec((MLA_WIDTH, tm), lambda i: (0, i))],
        out_shape=[jax.ShapeDtypeStruct((s, hw), BF16), jax.ShapeDtypeStruct((s, hw), BF16),
                   jax.ShapeDtypeStruct((s, MLA_WIDTH), BF16),
                   jax.ShapeDtypeStruct((hw, s), BF16), jax.ShapeDtypeStruct((MLA_WIDTH, s), BF16)],
        compiler_params=pltpu.CompilerParams(dimension_semantics=("parallel",), vmem_limit_bytes=VMEM_LIMIT),
    )(z_sh, q_norm, kv_norm, wuq_p, wk_p, wv, rc, rsa, rsb)


def _chunk_mask(t, keys_on_rows):
    rows = lax.broadcasted_iota(jnp.int32, (t, t), 0) // CHUNK
    cols = lax.broadcasted_iota(jnp.int32, (t, t), 1) // CHUNK
    return rows <= cols if keys_on_rows else cols <= rows


def _attn_fwd(q_t, k, v_t, late_shards, t):
    s = k.shape[0]
    groups = MLA_HEADS // FWD_HEADS
    n_q = s // t
    gat = _Gather(COMM_PARAMS[3:])
    n_w = len(gat.params)

    def body(qt_ref, k_ref, k2_ref, vt_ref, *rest):
        w_in, (o_ref, lse_ref), w_out = rest[:n_w], rest[n_w:n_w + 2], rest[n_w + 2:2 * n_w + 2]
        gat.bind(w_in, w_out, rest[2 * n_w + 2:])
        i = pl.program_id(1)
        step_no = pl.program_id(0) * n_q + i
        pl.when(step_no == 0)(gat.start)
        pl.when(step_no == groups * n_q // 2)(gat.relay)
        mask = _chunk_mask(t, True)
        qcs = [slice(hh * HEAD_PAD, (hh + 1) * HEAD_PAD) for hh in range(FWD_HEADS)]
        vcs = [slice(hh * V_HEAD_DIM, (hh + 1) * V_HEAD_DIM) for hh in range(FWD_HEADS)]
        qts = [qt_ref[qc, :] for qc in qcs]

        def step(j, carry, masked):
            keys = pl.ds(pl.multiple_of(j * t, t), t)
            out = []
            for hh in range(FWD_HEADS):
                m, l, acc = carry[hh]
                sc = _dot(k_ref[keys, qcs[hh]], qts[hh])
                if masked:
                    sc = jnp.where(mask, sc, -jnp.inf)
                m_new = jnp.maximum(m, jnp.max(sc, axis=0, keepdims=True))
                alpha = jnp.exp2((m - m_new) * ATT_SCALE_LOG2E)
                p = jnp.exp2((_dot(k2_ref[keys, qcs[hh]], qts[hh]) - m_new) * ATT_SCALE_LOG2E)
                if masked:
                    p = jnp.where(mask, p, 0.0)
                l = alpha * l + jnp.sum(p, axis=0, keepdims=True)
                acc = alpha * acc + _dot(vt_ref[vcs[hh], keys], p.astype(BF16))
                out.append((m_new, l, acc))
            return tuple(out)

        one = (jnp.full((1, t), -jnp.inf, F32), jnp.zeros((1, t), F32), jnp.zeros((V_HEAD_DIM, t), F32))
        carry = lax.fori_loop(0, i, functools.partial(step, masked=False), (one,) * FWD_HEADS)
        carry = step(i, carry, True)
        o_ref[...] = jnp.concatenate([carry[hh][2] / carry[hh][1] for hh in range(FWD_HEADS)], axis=0).T
        for hh in range(FWD_HEADS):
            m, l, _ = carry[hh]
            lse_ref[:, qcs[hh]] = jnp.broadcast_to(m * ATT_SCALE_LOG2E + jnp.log2(l), (HEAD_PAD, t)).T
        pl.when(step_no == groups * n_q - 1)(gat.finish)

    any_spec = pl.BlockSpec(memory_space=pl.ANY)
    out = pl.pallas_call(
        body,
        name="attn_fwd",
        grid=(groups, n_q),
        in_specs=[
            pl.BlockSpec((FWD_HEADS * HEAD_PAD, t), lambda p, i: (p, i)),
            pl.BlockSpec((s, FWD_HEADS * HEAD_PAD), lambda p, i: (0, p), pipeline_mode=pl.Buffered(1)),
            pl.BlockSpec((s, FWD_HEADS * HEAD_PAD), lambda p, i: (0, p), pipeline_mode=pl.Buffered(1)),
            pl.BlockSpec((FWD_HEADS * V_HEAD_DIM, s), lambda p, i: (p, 0), pipeline_mode=pl.Buffered(1)),
        ] + [any_spec] * n_w,
        out_specs=[
            pl.BlockSpec((t, FWD_HEADS * V_HEAD_DIM), lambda p, i: (i, p)),
            pl.BlockSpec((t, FWD_HEADS * HEAD_PAD), lambda p, i: (i, p)),
        ] + [any_spec] * n_w,
        out_shape=[jax.ShapeDtypeStruct((s, MLA_WIDTH), F32), jax.ShapeDtypeStruct((s, MLA_HEADS * HEAD_PAD), F32)]
        + gat.out_shape,
        scratch_shapes=gat.scratch,
        compiler_params=pltpu.CompilerParams(dimension_semantics=("arbitrary", "arbitrary"),
                                             vmem_limit_bytes=VMEM_LIMIT),
    )(q_t, k, k, v_t, *late_shards)
    return out[0], out[1], out[2:]


def _mid(o, z_sh, x, target, pool_w, pool_scale, w_ba, w_bp, w_out, norm_final, tm):
    s = x.shape[0]
    n_tiles = s // tm
    halo_per_tile = tm // POOL_HALO

    def body(o_ref, z0_ref, z1_ref, z1h_ref, z2_ref, z3_ref, x_ref, t_ref, pw_ref, ps_ref, wba_ref, wbp_ref,
             wout_ref, gf_ref,
             do_ref, dl_ref, dga_ref, dgp_ref, dgm_ref, ddc_ref, dh_ref,
             loss_ref, dwout_out, dwba_out, dwbp_out, dpw_ref, dps_ref, dgf_ref,
             ubuf, dwout_ref, dwba_ref, dwbp_ref):
        i = pl.program_id(0)

        @pl.when(i == 0)
        def _():
            loss_ref[...] = jnp.zeros_like(loss_ref)
            dwout_ref[...] = jnp.zeros_like(dwout_ref)
            dwba_ref[...] = jnp.zeros_like(dwba_ref)
            dwbp_ref[...] = jnp.zeros_like(dwbp_ref)
            dpw_ref[...] = jnp.zeros_like(dpw_ref)
            dps_ref[...] = jnp.zeros_like(dps_ref)
            dgf_ref[...] = jnp.zeros_like(dgf_ref)

        zs = [z0_ref[0], z1_ref[0], z2_ref[0], z3_ref[0]]
        o = o_ref[...]
        ga = _segment(zs, 3)
        sga = _sigmoid(ga)
        silu_a = ga * sga
        y_attn = (o * silu_a).astype(BF16)

        ubuf[0:POOL_HALO, :] = jnp.where(i > 0, _segment([None, z1h_ref[0]], 4), 0.0)
        ubuf[POOL_HALO:, :] = _segment(zs, 4)
        row = lax.broadcasted_iota(jnp.int32, (tm, POOL_GROUP_DIM), 0) + i * tm
        ps = ps_ref[...]
        gp = _segment(zs, 5)
        sgp = _sigmoid(gp)
        silu_p = gp * sgp
        d_bf, dm, inv_cnt = [], [], []
        for g, w in enumerate(POOL_WINDOWS):
            cols = slice(g * POOL_GROUP_DIM, (g + 1) * POOL_GROUP_DIM)
            wsum = ubuf[POOL_HALO:, cols]
            for kk in range(1, w):
                wsum = wsum + ubuf[POOL_HALO - kk:POOL_HALO - kk + tm, cols]
            inv = 1.0 / jnp.minimum(row + 1, w).astype(F32)
            dg = (wsum * inv - ubuf[POOL_HALO:, cols]).astype(BF16)
            d_bf.append(dg)
            inv_cnt.append(inv)
            dm.append(_dot(dg, pw_ref[g]))
        dm = jnp.concatenate(dm, axis=1)
        yp = dm * ps
        y_pool = (yp * silu_p).astype(BF16)

        a = jnp.concatenate([_dot(y_attn, wba_ref[j]) for j in range(N_CHIPS)], axis=1)
        p = jnp.concatenate([_dot(y_pool, wbp_ref[j]) for j in range(N_CHIPS)], axis=1)
        gm = _segment(zs, 6)
        gate_a = _sigmoid(gm[:, :D_MODEL])
        gate_p = _sigmoid(gm[:, D_MODEL:])
        merged = (gate_a * a + gate_p * p).astype(BF16)
        h = x_ref[...] + _dot(merged, wout_ref[...])
        gf = gf_ref[...]
        y, xhat, r = _rms_fwd(h, gf)
        err = y - t_ref[...]
        e2 = err * err
        e2 = jnp.sum(e2.reshape(tm // 8, 8, D_MODEL), axis=0)
        acc = e2[:, 0:LANES]
        for cidx in range(1, D_MODEL // LANES):
            acc = acc + e2[:, cidx * LANES:(cidx + 1) * LANES]
        loss_ref[...] += acc

        dy = err * (1.0 / D_MODEL)
        dgf_ref[...] += _colsum(dy * xhat)
        dh = _rms_bwd(dy, xhat, r, gf)
        dh_ref[...] = dh
        dh_bf = dh.astype(BF16)
        dwout_ref[...] += _dot_tn(merged, dh_bf)
        dmerged = _dot_nt(dh_bf, wout_ref[...])
        da = (dmerged * gate_a).astype(BF16)
        dp = (dmerged * gate_p).astype(BF16)
        dgm_ref[:, :D_MODEL] = (dmerged * a * gate_a * (1.0 - gate_a)).astype(BF16)
        dgm_ref[:, D_MODEL:] = (dmerged * p * gate_p * (1.0 - gate_p)).astype(BF16)
        dy_attn = dy_pool = None
        for j in range(N_CHIPS):
            cols = slice(j * BRANCH_COLS, (j + 1) * BRANCH_COLS)
            dwba_ref[j] += _dot_tn(y_attn, da[:, cols])
            dwbp_ref[j] += _dot_tn(y_pool, dp[:, cols])
            pa = _dot_nt(da[:, cols], wba_ref[j])
            pp = _dot_nt(dp[:, cols], wbp_ref[j])
            dy_attn = pa if dy_attn is None else dy_attn + pa
            dy_pool = pp if dy_pool is None else dy_pool + pp

        do = dy_attn * silu_a
        do_ref[...] = do
        dga_ref[...] = (dy_attn * o * (sga * (1.0 + ga * (1.0 - sga)))).astype(BF16)
        doo = do * o
        for hd in range(MLA_HEADS):
            dl = jnp.sum(doo[:, hd * V_HEAD_DIM:(hd + 1) * V_HEAD_DIM], axis=1, keepdims=True)
            dl_ref[:, hd * HEAD_PAD:(hd + 1) * HEAD_PAD] = jnp.broadcast_to(dl, (tm, HEAD_PAD))

        dyp = dy_pool * silu_p
        dgp_ref[...] = (dy_pool * yp * (sgp * (1.0 + gp * (1.0 - sgp)))).astype(BF16)
        dps_ref[...] += _colsum(dyp * dm)
        dmm = (dyp * ps).astype(BF16)
        for g in range(len(POOL_WINDOWS)):
            cols = slice(g * POOL_GROUP_DIM, (g + 1) * POOL_GROUP_DIM)
            dpw_ref[g] += _dot_tn(d_bf[g], dmm[:, cols])
            ddc_ref[:, cols] = _dot_nt(dmm[:, cols], pw_ref[g]) * inv_cnt[g]

        @pl.when(i == n_tiles - 1)
        def _():
            dwout_out[...] = dwout_ref[...].astype(BF16)
            dwba_out[...] = dwba_ref[...].astype(BF16)
            dwbp_out[...] = dwbp_ref[...].astype(BF16)

    row_in = lambda n: _row_spec(tm, n)
    in_specs = [
        row_in(MLA_WIDTH),
        pl.BlockSpec((1, tm, SHARD_COLS), lambda i: (0, i, 0)), pl.BlockSpec((1, tm, SHARD_COLS), lambda i: (1, i, 0)),
        pl.BlockSpec((1, POOL_HALO, SHARD_COLS), lambda i: (1, jnp.maximum(i * halo_per_tile - 1, 0), 0)),
        pl.BlockSpec((1, tm, SHARD_COLS), lambda i: (2, i, 0)), pl.BlockSpec((1, tm, SHARD_COLS), lambda i: (3, i, 0)),
        row_in(D_MODEL), row_in(D_MODEL),
        _full_spec((4, POOL_GROUP_DIM, POOL_GROUP_DIM)), _full_spec((1, POOL_WIDTH)),
        _full_spec((N_CHIPS, MLA_WIDTH, BRANCH_COLS)), _full_spec((N_CHIPS, POOL_WIDTH, BRANCH_COLS)),
        _full_spec((D_MODEL, D_MODEL)), _full_spec((1, D_MODEL)),
    ]
    out_shape = [
        jax.ShapeDtypeStruct((s, MLA_WIDTH), F32),
        jax.ShapeDtypeStruct((s, MLA_HEADS * HEAD_PAD), F32),
        jax.ShapeDtypeStruct((s, MLA_WIDTH), BF16),
        jax.ShapeDtypeStruct((s, POOL_WIDTH), BF16),
        jax.ShapeDtypeStruct((s, 2 * D_MODEL), BF16),
        jax.ShapeDtypeStruct((s, POOL_WIDTH), F32),
        jax.ShapeDtypeStruct((s, D_MODEL), F32),
        jax.ShapeDtypeStruct((8, LANES), F32),
        jax.ShapeDtypeStruct((D_MODEL, D_MODEL), BF16),
        jax.ShapeDtypeStruct((N_CHIPS, MLA_WIDTH, BRANCH_COLS), BF16),
        jax.ShapeDtypeStruct((N_CHIPS, POOL_WIDTH, BRANCH_COLS), BF16),
        jax.ShapeDtypeStruct((4, POOL_GROUP_DIM, POOL_GROUP_DIM), F32),
        jax.ShapeDtypeStruct((1, POOL_WIDTH), F32),
        jax.ShapeDtypeStruct((1, D_MODEL), F32),
    ]
    out_specs = [
        row_in(MLA_WIDTH), row_in(MLA_HEADS * HEAD_PAD), row_in(MLA_WIDTH), row_in(POOL_WIDTH),
        row_in(2 * D_MODEL), row_in(POOL_WIDTH), row_in(D_MODEL),
        _full_spec((8, LANES)), _full_spec((D_MODEL, D_MODEL)), _full_spec((N_CHIPS, MLA_WIDTH, BRANCH_COLS)),
        _full_spec((N_CHIPS, POOL_WIDTH, BRANCH_COLS)), _full_spec((4, POOL_GROUP_DIM, POOL_GROUP_DIM)),
        _full_spec((1, POOL_WIDTH)), _full_spec((1, D_MODEL)),
    ]
    return pl.pallas_call(
        body,
        name="mid",
        grid=(n_tiles,),
        in_specs=in_specs,
        out_specs=out_specs,
        out_shape=out_shape,
        scratch_shapes=[
            pltpu.VMEM((tm + POOL_HALO, POOL_WIDTH), F32),
            pltpu.VMEM((D_MODEL, D_MODEL), F32),
            pltpu.VMEM((N_CHIPS, MLA_WIDTH, BRANCH_COLS), F32),
            pltpu.VMEM((N_CHIPS, POOL_WIDTH, BRANCH_COLS), F32),
        ],
        compiler_params=pltpu.CompilerParams(dimension_semantics=("arbitrary",), vmem_limit_bytes=VMEM_LIMIT),
    )(o, z_sh, z_sh, z_sh, z_sh, z_sh, x, target, pool_w, pool_scale, w_ba, w_bp, w_out, norm_final)


def _attn_bwd(q, q_t, k, v, do, lse, delta, late_grads, gs_mid, t):
    s = q.shape[0]
    groups = MLA_HEADS // BWD_HEADS
    n_q = s // t
    red = _Reduce(COMM_PARAMS[3:])
    n_w = len(red.params)
    small = _SmallSum(gs_mid.shape[0])
    n_red = len(red.scratch)

    def body(q_ref, qt_ref, do_ref, lse_ref, dl_ref, k_ref, v_ref, *rest):
        g_in, gs_ref = rest[:n_w], rest[n_w]
        (dq_ref, dk_ref, dv_ref), g_out, gsum_ref = rest[n_w + 1:n_w + 4], rest[n_w + 4:2 * n_w + 4], rest[2 * n_w + 4]
        scratch = rest[2 * n_w + 5:]
        red.bind(g_in, g_out, scratch[:n_red])
        small.bind(gs_ref, gsum_ref, scratch[n_red:])
        i = pl.program_id(1)
        step_no = pl.program_id(0) * n_q + i

        @pl.when(step_no == 0)
        def _():
            red.start()
            small.start()

        pl.when(step_no == groups * n_q // 2)(red.exchange)

        @pl.when(i == 0)
        def _():
            dk_ref[...] = jnp.zeros_like(dk_ref)
            dv_ref[...] = jnp.zeros_like(dv_ref)

        mask = _chunk_mask(t, False)
        qcs = [slice(hh * HEAD_PAD, (hh + 1) * HEAD_PAD) for hh in range(BWD_HEADS)]
        vcs = [slice(hh * V_HEAD_DIM, (hh + 1) * V_HEAD_DIM) for hh in range(BWD_HEADS)]
        qhs = [q_ref[:, qc] for qc in qcs]
        qts = [qt_ref[qc, :] for qc in qcs]
        dohs = [do_ref[:, vc].astype(BF16) for vc in vcs]
        do_t = do_ref[...].T.astype(BF16)
        dots = [do_t[vc, :] for vc in vcs]
        lses = [jnp.tile(lse_ref[:, qc], (1, t // HEAD_PAD)) for qc in qcs]
        dls = [jnp.tile(dl_ref[:, qc], (1, t // HEAD_PAD)) for qc in qcs]

        def step(j, dqs, masked):
            keys = pl.ds(pl.multiple_of(j * t, t), t)
            out = []
            for hh in range(BWD_HEADS):
                kj = k_ref[keys, qcs[hh]]
                vj = v_ref[keys, vcs[hh]]
                p = jnp.exp2(_dot_nt(qhs[hh], kj) * ATT_SCALE_LOG2E - lses[hh])
                if masked:
                    p = jnp.where(mask, p, 0.0)
                ds = (p * (_dot_nt(dohs[hh], vj) - dls[hh])).astype(BF16)
                dv_ref[vcs[hh], keys] += _dot(dots[hh], p.astype(BF16))
                dk_ref[qcs[hh], keys] += _dot(qts[hh], ds) * ATT_SCALE
                out.append(dqs[hh] + _dot(ds, kj))
            return tuple(out)

        zero = jnp.zeros((t, HEAD_PAD), F32)
        dqs = lax.fori_loop(0, i, functools.partial(step, masked=False), (zero,) * BWD_HEADS)
        dqs = step(i, dqs, True)
        for hh in range(BWD_HEADS):
            dq_ref[:, qcs[hh]] = dqs[hh] * ATT_SCALE

        @pl.when(step_no == groups * n_q - 1)
        def _():
            red.finish()
            small.finish()

    hw = MLA_HEADS * HEAD_PAD
    any_spec = pl.BlockSpec(memory_space=pl.ANY)
    out = pl.pallas_call(
        body,
        name="attn_bwd",
        grid=(groups, n_q),
        in_specs=[
            pl.BlockSpec((t, BWD_HEADS * HEAD_PAD), lambda p, i: (i, p)),
            pl.BlockSpec((BWD_HEADS * HEAD_PAD, t), lambda p, i: (p, i)),
            pl.BlockSpec((t, BWD_HEADS * V_HEAD_DIM), lambda p, i: (i, p)),
            pl.BlockSpec((t, BWD_HEADS * HEAD_PAD), lambda p, i: (i, p)),
            pl.BlockSpec((t, BWD_HEADS * HEAD_PAD), lambda p, i: (i, p)),
            pl.BlockSpec((s, BWD_HEADS * HEAD_PAD), lambda p, i: (0, p), pipeline_mode=pl.Buffered(1)),
            pl.BlockSpec((s, BWD_HEADS * V_HEAD_DIM), lambda p, i: (0, p), pipeline_mode=pl.Buffered(1)),
        ] + [any_spec] * n_w + [pl.BlockSpec(small.spec_shape, lambda p, i: (0, 0))],
        out_specs=[
            pl.BlockSpec((t, BWD_HEADS * HEAD_PAD), lambda p, i: (i, p)),
            pl.BlockSpec((BWD_HEADS * HEAD_PAD, s), lambda p, i: (p, 0)),
            pl.BlockSpec((BWD_HEADS * V_HEAD_DIM, s), lambda p, i: (p, 0)),
        ] + [any_spec] * n_w + [pl.BlockSpec(small.spec_shape, lambda p, i: (0, 0))],
        out_shape=[jax.ShapeDtypeStruct((s, hw), F32), jax.ShapeDtypeStruct((hw, s), F32),
                   jax.ShapeDtypeStruct((MLA_WIDTH, s), F32)] + red.out_shape + [small.out_shape],
        scratch_shapes=red.scratch + small.scratch,
        compiler_params=pltpu.CompilerParams(dimension_semantics=("arbitrary", "arbitrary"),
                                             vmem_limit_bytes=VMEM_LIMIT),
    )(q, q_t, do, lse, delta, k, v, *late_grads, gs_mid)
    return out[0], out[1], out[2], out[3:3 + n_w], out[3 + n_w]


def _qkv_bwd(dq, dk_t, dv_t, z_sh, q_norm, kv_norm, wuq_p, wk_p, wv, rc, rsa, rsb, tm):
    s = z_sh.shape[1]
    hw = MLA_HEADS * HEAD_PAD

    def body(dq_ref, dk_ref, dv_ref, z_ref, gq_ref, gkv_ref, wuq_ref, wk_ref, wv_ref,
             c_ref, sa_ref, sb_ref,
             dzq_ref, dzkv_ref, dzkr_ref, dwuq_ref, dwk_ref, dwv_ref, dgq_ref, dgkv_ref):
        i = pl.program_id(0)

        @pl.when(i == 0)
        def _():
            dwuq_ref[...] = jnp.zeros_like(dwuq_ref)
            dwk_ref[...] = jnp.zeros_like(dwk_ref)
            dwv_ref[...] = jnp.zeros_like(dwv_ref)
            dgq_ref[...] = jnp.zeros_like(dgq_ref)
            dgkv_ref[...] = jnp.zeros_like(dgkv_ref)

        c, sa, sb = c_ref[...], sa_ref[...], sb_ref[...]
        gq, gkv = gq_ref[...], gkv_ref[...]

        z0 = z_ref[0]
        cq, xq, rq = _rms_fwd(z0[:, ZQ_COLS], gq)
        dqp = jnp.concatenate(
            [_unrope(dq_ref[:, h * HEAD_PAD:(h + 1) * HEAD_PAD], c, sa, sb) for h in range(MLA_HEADS)],
            axis=1).astype(BF16)
        dwuq_ref[...] += _dot_tn(cq.astype(BF16), dqp)
        dcq = _dot_nt(dqp, wuq_ref[...])
        dgq_ref[...] += _colsum(dcq * xq)
        dzq_ref[...] = _rms_bwd(dcq, xq, rq, gq).astype(BF16)

        ckv, xkv, rkv = _rms_fwd(z0[:, ZKV_COLS], gkv)
        ckv = ckv.astype(BF16)
        dkf = dk_ref[...].T
        dk_bf = dkf.astype(BF16)
        dv_bf = dv_ref[...].T.astype(BF16)
        dwk_ref[...] += _dot_tn(ckv, dk_bf)
        dwv_ref[...] += _dot_tn(ckv, dv_bf)
        dckv = _dot_nt(dk_bf, wk_ref[...]) + _dot_nt(dv_bf, wv_ref[...])
        dgkv_ref[...] += _colsum(dckv * xkv)
        dzkv_ref[...] = _rms_bwd(dckv, xkv, rkv, gkv).astype(BF16)

        dkr = dkf[:, 0:HEAD_PAD]
        for h in range(1, MLA_HEADS):
            dkr = dkr + dkf[:, h * HEAD_PAD:(h + 1) * HEAD_PAD]
        dkr = pltpu.roll(_unrope(dkr, c, sa, sb), 64, 1)
        lane = lax.broadcasted_iota(jnp.int32, (tm, HEAD_PAD), 1)
        dzkr_ref[...] = jnp.where(lane < QK_ROPE_DIM, dkr, 0.0).astype(BF16)

    return pl.pallas_call(
        body,
        name="qkv_bwd",
        grid=(s // tm,),
        in_specs=[
            _row_spec(tm, hw), pl.BlockSpec((hw, tm), lambda i: (0, i)), pl.BlockSpec((MLA_WIDTH, tm), lambda i: (0, i)),
            pl.BlockSpec((1, tm, SHARD_COLS), lambda i: (0, i, 0)),
            _full_spec((1, Q_LORA_RANK)), _full_spec((1, KV_LORA_RANK)),
            _full_spec((Q_LORA_RANK, hw)), _full_spec((KV_LORA_RANK, hw)), _full_spec((KV_LORA_RANK, MLA_WIDTH)),
            _row_spec(tm, HEAD_PAD), _row_spec(tm, HEAD_PAD), _row_spec(tm, HEAD_PAD),
        ],
        out_specs=[
            _row_spec(tm, Q_LORA_RANK), _row_spec(tm, KV_LORA_RANK), _row_spec(tm, HEAD_PAD),
            _full_spec((Q_LORA_RANK, hw)), _full_spec((KV_LORA_RANK, hw)), _full_spec((KV_LORA_RANK, MLA_WIDTH)),
            _full_spec((1, Q_LORA_RANK)), _full_spec((1, KV_LORA_RANK)),
        ],
        out_shape=[
            jax.ShapeDtypeStruct((s, Q_LORA_RANK), BF16), jax.ShapeDtypeStruct((s, KV_LORA_RANK), BF16),
            jax.ShapeDtypeStruct((s, HEAD_PAD), BF16),
            jax.ShapeDtypeStruct((Q_LORA_RANK, hw), F32), jax.ShapeDtypeStruct((KV_LORA_RANK, hw), F32),
            jax.ShapeDtypeStruct((KV_LORA_RANK, MLA_WIDTH), F32),
            jax.ShapeDtypeStruct((1, Q_LORA_RANK), F32), jax.ShapeDtypeStruct((1, KV_LORA_RANK), F32),
        ],
        compiler_params=pltpu.CompilerParams(dimension_semantics=("arbitrary",), vmem_limit_bytes=VMEM_LIMIT),
    )(dq, dk_t, dv_t, z_sh, q_norm, kv_norm, wuq_p, wk_p, wv, rc, rsa, rsb)


def _inproj_bwd_x(dzq, dzkv, dzkr, dgattn, ddc, dgpool, dgmerge, x, dh, norm_in, w_in_t, tm):
    s = x.shape[0]
    n_tiles = s // tm
    halo_per_tile = tm // POOL_HALO
    n_halo = s // POOL_HALO
    u_seg = 4

    def body(dzq_ref, dzkv_ref, dzkr_ref, dga_ref, ddc_ref, ddn_ref, dgp_ref, dgm_ref, x_ref, dh_ref,
             g_ref, w_hbm, gx_ref, dgin_ref, dzs_ref, w_vmem, dbuf, sem):
        i = pl.program_id(0)

        @pl.when(i == 0)
        def _():
            cp = pltpu.make_async_copy(w_hbm, w_vmem, sem)
            cp.start()
            dgin_ref[...] = jnp.zeros_like(dgin_ref)
            cp.wait()

        dbuf[0:tm, :] = ddc_ref[...]
        dbuf[tm:, :] = jnp.where(i < n_tiles - 1, ddn_ref[...], 0.0)
        row = lax.broadcasted_iota(jnp.int32, (tm, POOL_GROUP_DIM), 0) + i * tm
        du = []
        for g, w in enumerate(POOL_WINDOWS):
            cols = slice(g * POOL_GROUP_DIM, (g + 1) * POOL_GROUP_DIM)
            fsum = dbuf[0:tm, cols]
            for kk in range(1, w):
                fsum = fsum + dbuf[kk:kk + tm, cols]
            du.append(fsum - dbuf[0:tm, cols] * jnp.minimum(row + 1, w).astype(F32))
        du = jnp.concatenate(du, axis=1).astype(BF16)

        dz = [dzq_ref[...], dzkv_ref[...], dzkr_ref[...], dga_ref[...], du, dgp_ref[...], dgm_ref[...]]
        dz = jnp.concatenate([d[:, :w] for d, (w, _) in zip(dz, IN_SEGMENTS)], axis=1)
        for j in range(N_CHIPS):
            dzs_ref[j] = dz[:, j * SHARD_COLS:(j + 1) * SHARD_COLS].T
        dhn = _dot(dz, w_vmem[...])

        g = g_ref[...]
        _, xhat, r = _rms_fwd(x_ref[...], g)
        dgin_ref[...] += _colsum(dhn * xhat)
        gx_ref[...] = dh_ref[...] + _rms_bwd(dhn, xhat, r, g)

    any_spec = pl.BlockSpec(memory_space=pl.ANY)
    seg_w = [wide for _, wide in IN_SEGMENTS]
    return pl.pallas_call(
        body,
        name="inproj_bwd_x",
        grid=(n_tiles,),
        in_specs=[
            _row_spec(tm, seg_w[0]), _row_spec(tm, seg_w[1]), _row_spec(tm, seg_w[2]),
            _row_spec(tm, seg_w[3]), _row_spec(tm, seg_w[u_seg]),
            pl.BlockSpec((POOL_HALO, POOL_WIDTH), lambda i: (jnp.minimum((i + 1) * halo_per_tile, n_halo - 1), 0)),
            _row_spec(tm, seg_w[5]), _row_spec(tm, seg_w[6]),
            _row_spec(tm, D_MODEL), _row_spec(tm, D_MODEL),
            _full_spec((1, D_MODEL)), any_spec,
        ],
        out_specs=[_row_spec(tm, D_MODEL), _full_spec((1, D_MODEL)),
                   pl.BlockSpec((N_CHIPS, SHARD_COLS, tm), lambda i: (0, 0, i))],
        out_shape=[jax.ShapeDtypeStruct((s, D_MODEL), F32), jax.ShapeDtypeStruct((1, D_MODEL), F32),
                   jax.ShapeDtypeStruct((N_CHIPS, SHARD_COLS, s), BF16)],
        scratch_shapes=[
            pltpu.VMEM((IN_TOTAL, D_MODEL), BF16),
            pltpu.VMEM((tm + POOL_HALO, POOL_WIDTH), F32),
            pltpu.SemaphoreType.DMA,
        ],
        compiler_params=pltpu.CompilerParams(dimension_semantics=("arbitrary",), vmem_limit_bytes=VMEM_LIMIT),
    )(dzq, dzkv, dzkr, dgattn, ddc, ddc, dgpool, dgmerge, x, dh, norm_in, w_in_t.reshape(IN_TOTAL, D_MODEL))


def _inproj_bwd_w(order, dz_sh, hn, g_uq, g_ukv, gs, tm):
    s = hn.shape[0]
    n_tiles = s // tm
    hc = D_MODEL // 2
    red = _Reduce(COMM_PARAMS[1:3])
    small = _SmallSum(gs.shape[0])
    n_red = len(red.scratch)

    def body(order_ref, dz_ref, hn_ref, guq_hbm, gukv_hbm, gs_ref, gw_hbm, guq_out, gukv_out, gsum_ref,
             acc, pm_w, a_w, b_w, r_w, w_send, w_recv, w_local, *more_scratch):
        ph, i = pl.program_id(0), pl.program_id(1)
        x, y, c = lax.axis_index("x"), lax.axis_index("y"), lax.axis_index("c")
        k = 2 * x + y
        me, sibling = (x, y, c), (x, y, 1 - c)
        chips = _other_chips(x, y)
        shard_of_phase = [2 * cx + cy for cx, cy in chips] + [k]
        copy = _remote_copier(w_send, w_recv)
        red.bind([guq_hbm, gukv_hbm], [guq_out, gukv_out], more_scratch[:n_red])
        small.bind(gs_ref, gsum_ref, more_scratch[n_red:])
        mine = pl.ds(pl.multiple_of(c * hc, hc), hc)
        theirs = pl.ds(pl.multiple_of((1 - c) * hc, hc), hc)

        def to_sibling(f):
            j = shard_of_phase[f]
            return copy(f, pm_w.at[j, 1 - c], a_w.at[j], sibling)

        def pair_sum(f):
            cx, cy = chips[f]
            return copy(4 + f, pm_w.at[shard_of_phase[f], c], b_w.at[f], (cx, cy, c))

        def finished():
            return copy(7, r_w, gw_hbm.at[:, mine], sibling)

        @pl.when(jnp.logical_and(ph == 0, i == 0))
        def _():
            red.start()
            small.start()

        part = _dot(dz_ref[0], hn_ref[...])

        @pl.when(i == 0)
        def _():
            acc[...] = part

        @pl.when(i > 0)
        def _():
            acc[...] += part

        for f in range(3):
            @pl.when(jnp.logical_and(ph == f + 1, i == 0))
            def _(f=f):
                j = shard_of_phase[f]
                copy(f, a_w.at[j], a_w.at[j], me).wait_recv()
                pm_w[j, c] = (pm_w[j, c].astype(F32) + a_w[j].astype(F32)).astype(BF16)
                pair_sum(f).start()
                if f == 0:
                    red.exchange()

        for f in range(4):
            @pl.when(jnp.logical_and(ph == f, i == n_tiles - 1))
            def _(f=f):
                j = shard_of_phase[f]
                pm_w[j, 0] = acc[:, :hc].astype(BF16)
                pm_w[j, 1] = acc[:, hc:].astype(BF16)
                to_sibling(f).start()
                if f < 3:
                    return
                copy(3, a_w.at[k], a_w.at[k], me).wait_recv()
                r_w[...] = pm_w[k, c].astype(F32) + a_w[k].astype(F32)
                for g in range(3):
                    copy(4 + g, b_w.at[g], b_w.at[g], me).wait_recv()
                    r_w[...] = r_w[...] + b_w[g].astype(F32)
                store = pltpu.make_async_copy(r_w, gw_hbm.at[:, mine], w_local)
                store.start()
                finished().start()
                red.finish()
                small.finish()
                copy(7, gw_hbm.at[:, theirs], gw_hbm.at[:, theirs], me).wait_recv()
                store.wait()
                for g in range(4):
                    to_sibling(g).wait_send()
                for g in range(3):
                    pair_sum(g).wait_send()
                finished().wait_send()

    any_spec = pl.BlockSpec(memory_space=pl.ANY)
    n_sem = 8
    grid_spec = pltpu.PrefetchScalarGridSpec(
        num_scalar_prefetch=1,
        grid=(N_CHIPS, n_tiles),
        in_specs=[
            pl.BlockSpec((1, SHARD_COLS, tm), lambda ph, i, order: (order[ph], 0, i)),
            pl.BlockSpec((tm, D_MODEL), lambda ph, i, order: (i, 0)),
            any_spec, any_spec,
            pl.BlockSpec(small.spec_shape, lambda ph, i, order: (0, 0)),
        ],
        out_specs=[any_spec, any_spec, any_spec, pl.BlockSpec(small.spec_shape, lambda ph, i, order: (0, 0))],
        scratch_shapes=[
            pltpu.VMEM((SHARD_COLS, D_MODEL), F32),
            pltpu.VMEM((N_CHIPS, 2, SHARD_COLS, hc), BF16),
            pltpu.VMEM((N_CHIPS, SHARD_COLS, hc), BF16),
            pltpu.VMEM((3, SHARD_COLS, hc), BF16),
            pltpu.VMEM((SHARD_COLS, hc), F32),
            pltpu.SemaphoreType.DMA((n_sem,)), pltpu.SemaphoreType.DMA((n_sem,)), pltpu.SemaphoreType.DMA,
        ] + red.scratch + small.scratch,
    )
    out = pl.pallas_call(
        body,
        name="inproj_bwd_w",
        grid_spec=grid_spec,
        out_shape=[jax.ShapeDtypeStruct((SHARD_COLS, D_MODEL), F32)] + red.out_shape
        + [small.out_shape],
        compiler_params=pltpu.CompilerParams(dimension_semantics=("arbitrary", "arbitrary"),
                                             vmem_limit_bytes=VMEM_LIMIT),
    )(order, dz_sh, hn, g_uq, g_ukv, gs)
    return out[0], out[1], out[2], out[3]


def _other_chips(x, y):
    return ((1 - x, 1 - y), (1 - x, y), (x, 1 - y))


def _half(ref, axis, size, c, lead=()):
    window = pl.ds(pl.multiple_of(c * size, size), size)
    if axis == 0:
        return ref.at[(*lead, window, slice(None))]
    return ref.at[(*lead, slice(None), window)]


def _half_shape(rows, cols, axis, size):
    return (size, cols) if axis == 0 else (rows, size)


def _remote_copier(send_sems, recv_sems):
    def copy(sem, src, dst, to):
        return pltpu.make_async_remote_copy(src_ref=src, dst_ref=dst, send_sem=send_sems.at[sem],
                                            recv_sem=recv_sems.at[sem], device_id=to, device_id_type=MESH)
    return copy


class _Gather:
    def __init__(self, params):
        self.params = params
        n = len(params)
        self.scratch = [pltpu.SemaphoreType.DMA((6 * n,)), pltpu.SemaphoreType.DMA((6 * n,)),
                        pltpu.SemaphoreType.DMA((n,))]
        self.out_shape = [jax.ShapeDtypeStruct((N_CHIPS, r, cc), BF16) for _, r, cc, _, _ in params]

    def bind(self, ins, outs, scratch):
        self.ins, self.outs = ins, outs
        send_sems, recv_sems, self.local_sems = scratch
        self.copy = _remote_copier(send_sems, recv_sems)
        self.x, self.y, self.c = lax.axis_index("x"), lax.axis_index("y"), lax.axis_index("c")
        self.k = 2 * self.x + self.y
        self.chips = _other_chips(self.x, self.y)

    def _local(self, p):
        return pltpu.make_async_copy(self.ins[p], self.outs[p].at[self.k], self.local_sems.at[p])

    def _first(self, p, j):
        _, _, _, axis, size = self.params[p]
        cx, cy = self.chips[j]
        return self.copy(6 * p + j, _half(self.ins[p], axis, size, self.c),
                         _half(self.outs[p], axis, size, self.c, (self.k,)), (cx, cy, self.c))

    def _relay(self, p, j, half_of):
        _, _, _, axis, size = self.params[p]
        cx, cy = self.chips[j]
        block = _half(self.outs[p], axis, size, half_of, (2 * cx + cy,))
        return self.copy(6 * p + 3 + j, block, block, (self.x, self.y, 1 - self.c))

    def start(self):
        for p in range(len(self.params)):
            self._local(p).start()
            for j in (1, 2, 0):
                self._first(p, j).start()

    def relay_one(self, p, j):
        _, _, _, axis, size = self.params[p]
        cx, cy = self.chips[j]
        landed = _half(self.outs[p], axis, size, self.c, (2 * cx + cy,))
        self.copy(6 * p + j, landed, landed, (self.x, self.y, self.c)).wait_recv()
        self._relay(p, j, self.c).start()

    def await_one(self, p, j):
        self._relay(p, j, 1 - self.c).wait_recv()

    def wait_sends(self):
        for p in range(len(self.params)):
            for j in range(3):
                self._first(p, j).wait_send()
                self._relay(p, j, self.c).wait_send()
            self._local(p).wait()

    def relay(self):
        for j in range(3):
            for p in range(len(self.params)):
                self.relay_one(p, j)

    def finish(self):
        for j in range(3):
            for p in range(len(self.params)):
                self.await_one(p, j)
        self.wait_sends()


class _Reduce:
    def __init__(self, params):
        self.params = params
        n = len(params)
        halves = [_half_shape(r, cc, axis, size) for _, r, cc, axis, size in params]
        self.scratch = ([pltpu.VMEM((N_CHIPS, *h), BF16) for h in halves]
                        + [pltpu.VMEM((N_CHIPS, *h), BF16) for h in halves]
                        + [pltpu.VMEM((3, *h), BF16) for h in halves]
                        + [pltpu.VMEM(h, F32) for h in halves]
                        + [pltpu.SemaphoreType.DMA((5 * n,)), pltpu.SemaphoreType.DMA((5 * n,)),
                           pltpu.SemaphoreType.DMA((2 * n,))])
        self.out_shape = [jax.ShapeDtypeStruct((r, cc), F32) for _, r, cc, _, _ in params]

    def bind(self, g_in, g_out, scratch):
        n = len(self.params)
        self.g_in, self.g_out = g_in, g_out
        self.pm, self.a_buf = scratch[0:n], scratch[n:2 * n]
        self.b_buf, self.r_buf = scratch[2 * n:3 * n], scratch[3 * n:4 * n]
        send_sems, recv_sems, self.local_sems = scratch[4 * n:]
        self.copy = _remote_copier(send_sems, recv_sems)
        self.x, self.y, self.c = lax.axis_index("x"), lax.axis_index("y"), lax.axis_index("c")
        self.k = 2 * self.x + self.y
        self.chips = _other_chips(self.x, self.y)
        self.me = (self.x, self.y, self.c)
        self.sibling = (self.x, self.y, 1 - self.c)

    def _load(self, p):
        _, _, _, axis, size = self.params[p]
        return pltpu.make_async_copy(_half(self.g_in[p], axis, size, self.c, (slice(None),)), self.pm[p],
                                     self.local_sems.at[p])

    def _to_sibling(self, p):
        _, _, _, axis, size = self.params[p]
        return self.copy(5 * p, _half(self.g_in[p], axis, size, 1 - self.c, (slice(None),)), self.a_buf[p],
                         self.sibling)

    def _pair_sum(self, p, j):
        cx, cy = self.chips[j]
        return self.copy(5 * p + 1 + j, self.pm[p].at[2 * cx + cy], self.b_buf[p].at[j], (cx, cy, self.c))

    def _store(self, p):
        _, _, _, axis, size = self.params[p]
        n = len(self.params)
        return pltpu.make_async_copy(self.r_buf[p], _half(self.g_out[p], axis, size, self.c),
                                     self.local_sems.at[n + p])

    def _finished(self, p):
        _, _, _, axis, size = self.params[p]
        return self.copy(5 * p + 4, self.r_buf[p], _half(self.g_out[p], axis, size, self.c), self.sibling)

    def start(self):
        for p in range(len(self.params)):
            self._load(p).start()
            self._to_sibling(p).start()

    def exchange(self):
        for p in range(len(self.params)):
            self._load(p).wait()
            self.copy(5 * p, self.a_buf[p], self.a_buf[p], self.me).wait_recv()
            for j, (cx, cy) in enumerate(self.chips):
                kj = 2 * cx + cy
                self.pm[p][kj] = (self.pm[p][kj].astype(F32) + self.a_buf[p][kj].astype(F32)).astype(BF16)
                self._pair_sum(p, j).start()
            self.r_buf[p][...] = self.pm[p][self.k].astype(F32) + self.a_buf[p][self.k].astype(F32)

    def finish(self):
        for p, (_, _, _, axis, size) in enumerate(self.params):
            for j in range(3):
                self.copy(5 * p + 1 + j, self.b_buf[p].at[j], self.b_buf[p].at[j], self.me).wait_recv()
                self.r_buf[p][...] = self.r_buf[p][...] + self.b_buf[p][j].astype(F32)
            self._store(p).start()
            self._finished(p).start()
        for p, (_, _, _, axis, size) in enumerate(self.params):
            theirs = _half(self.g_out[p], axis, size, 1 - self.c)
            self.copy(5 * p + 4, theirs, theirs, self.me).wait_recv()
            self._store(p).wait()
            self._to_sibling(p).wait_send()
            for j in range(3):
                self._pair_sum(p, j).wait_send()
            self._finished(p).wait_send()


class _SmallSum:
    def __init__(self, rows):
        self.rows = rows
        self.scratch = [pltpu.VMEM((N_DEV, rows, LANES), F32),
                        pltpu.SemaphoreType.DMA((N_DEV - 1,)), pltpu.SemaphoreType.DMA((N_DEV - 1,))]
        self.out_shape = jax.ShapeDtypeStruct((rows, LANES), F32)
        self.spec_shape = (rows, LANES)

    def bind(self, src, dst, scratch):
        self.src, self.dst = src, dst
        self.buf, send_sems, recv_sems = scratch
        self.copy = _remote_copier(send_sems, recv_sems)
        self.x, self.y, self.c = lax.axis_index("x"), lax.axis_index("y"), lax.axis_index("c")

    def _send(self, f):
        fx, fy, fc = [(a, b, d) for a in (0, 1) for b in (0, 1) for d in (0, 1)][f]
        x, y, c = self.x, self.y, self.c
        peer = (1 - x if fx else x, 1 - y if fy else y, 1 - c if fc else c)
        return self.copy(f - 1, self.src, self.buf.at[f], peer)

    def start(self):
        for f in range(1, N_DEV):
            self._send(f).start()
        self.buf[0] = self.src[...]

    def finish(self):
        me = (self.x, self.y, self.c)
        for f in range(1, N_DEV):
            self.copy(f - 1, self.buf.at[f], self.buf.at[f], me).wait_recv()
        dev = 4 * self.x + 2 * self.y + self.c
        total = self.buf[dev]
        for d in range(1, N_DEV):
            total = total + self.buf[jnp.bitwise_xor(dev, d)]
        self.dst[...] = total
        for f in range(1, N_DEV):
            self._send(f).wait_send()


def _adamw_math(w, g, m, v):
    m = ADAM_B1 * m + (1.0 - ADAM_B1) * g
    v = ADAM_B2 * v + (1.0 - ADAM_B2) * (g * g)
    m_hat = m / (1.0 - ADAM_B1 ** ADAM_STEP)
    v_hat = v / (1.0 - ADAM_B2 ** ADAM_STEP)
    delta = -ADAM_LR * (m_hat / (jnp.sqrt(v_hat) + ADAM_EPS) + ADAM_WD * w)
    return delta, m, v


def _adamw_tiled(w, g, m, v, tm):
    rows, cols = w.shape

    def body(w_ref, g_ref, m_ref, v_ref, d_ref, nm_ref, nv_ref, g_out):
        g = g_ref[...]
        d_ref[...], nm_ref[...], nv_ref[...] = _adamw_math(w_ref[...], g, m_ref[...], v_ref[...])
        g_out[...] = g

    spec = _row_spec(tm, cols)
    return pl.pallas_call(
        body,
        name="adamw_w_in",
        grid=(rows // tm,),
        in_specs=[spec] * 4,
        out_specs=[spec] * 4,
        out_shape=[jax.ShapeDtypeStruct(w.shape, F32)] * 4,
        compiler_params=pltpu.CompilerParams(dimension_semantics=("parallel",), vmem_limit_bytes=VMEM_LIMIT),
    )(w, g, m, v)


def _adamw_many(ws, gs, ms, vs):
    n = len(ws)

    def body(*refs):
        ins, outs = refs[:4 * n], refs[4 * n:]
        for i in range(n):
            g = ins[n + i][...]
            d, nm, nv = _adamw_math(ins[i][...], g, ins[2 * n + i][...], ins[3 * n + i][...])
            outs[i][...] = d
            outs[n + i][...] = nm
            outs[2 * n + i][...] = nv
            outs[3 * n + i][...] = g

    vmem_spec = pl.BlockSpec(memory_space=pltpu.VMEM)
    shapes = [jax.ShapeDtypeStruct(w.shape, F32) for w in ws]
    out = pl.pallas_call(
        body,
        name="adamw_small",
        in_specs=[vmem_spec] * (4 * n),
        out_specs=[vmem_spec] * (4 * n),
        out_shape=shapes * 4,
        compiler_params=pltpu.CompilerParams(vmem_limit_bytes=VMEM_LIMIT),
    )(*ws, *gs, *ms, *vs)
    return out[:n], out[n:2 * n], out[2 * n:3 * n], out[3 * n:]


def _pack_rows(parts, rows, dtype):
    flat = jnp.concatenate([p.reshape(-1).astype(dtype) for p in parts])
    flat = jnp.concatenate([flat, jnp.zeros((rows * LANES - flat.shape[0],), dtype)])
    return flat.reshape(rows, LANES)


def _unpack_rows(packed, shapes):
    flat = packed.reshape(-1)
    out, off = [], 0
    for _, shp in shapes:
        n = int(np.prod(shp))
        out.append(flat[off:off + n].reshape(shp))
        off += n
    return out


def _rope_tables(s):
    half = QK_ROPE_DIM // 2
    inv_freq = np.float32(ROPE_THETA) ** (-np.arange(half, dtype=np.float32) / np.float32(half))
    ang = (np.arange(s, dtype=np.float32)[:, None] * inv_freq[None, :]).astype(np.float32)
    cos, sin = np.cos(ang.astype(np.float64)).astype(np.float32), np.sin(ang.astype(np.float64)).astype(np.float32)
    z16 = np.zeros((s, half), np.float32)
    z32 = np.zeros((s, HEAD_PAD - QK_NOPE_DIM - QK_ROPE_DIM), np.float32)
    z64 = np.zeros((s, QK_NOPE_DIM), np.float32)
    rc = np.concatenate([np.ones((s, QK_NOPE_DIM), np.float32), cos, cos, z32], axis=1)
    rsa = np.concatenate([z64, -sin, z16, z32], axis=1)
    rsb = np.concatenate([z64, z16, sin, z32], axis=1)
    return jnp.asarray(rc), jnp.asarray(rsa), jnp.asarray(rsb)


def kernel(x, norm_in, w_in, q_norm, w_uq, kv_norm, w_ukv, pool_w, pool_scale, w_branch_attn, w_branch_pool, w_out, norm_final, loss_target, m_norm_in, m_w_in, m_q_norm, m_w_uq, m_kv_norm, m_w_ukv, m_pool_w, m_pool_scale, m_w_branch_attn, m_w_branch_pool, m_w_out, m_norm_final, v_norm_in, v_w_in, v_q_norm, v_w_uq, v_kv_norm, v_w_ukv, v_pool_w, v_pool_scale, v_w_branch_attn, v_w_branch_pool, v_w_out, v_norm_final):
    s = x.shape[1]
    t_att, t_row = _tiles(s)
    x2 = x.reshape(s, D_MODEL)
    tgt = loss_target.reshape(s, D_MODEL)

    local = [w_in.T, w_uq.reshape(96, 768), w_ukv.reshape(64, 1024), w_branch_attn, w_branch_pool, w_out]
    local = [a.astype(BF16) for a in local]
    cx, cy = lax.axis_index("x"), lax.axis_index("y")
    others = [2 * ox + oy for ox, oy in _other_chips(cx, cy)]
    hn, z_sh, (w_in_t, w_uq_all, w_ukv_all) = _inproj_fwd(
        jnp.stack([2 * cx + cy, others[1], others[2], others[0]]).astype(jnp.int32), x2, norm_in.reshape(1, -1),
        local[:3], 4 * t_row)
    w_uq_f = w_uq_all.reshape(Q_LORA_RANK, MLA_HEADS, QK_NOPE_DIM + QK_ROPE_DIM)
    w_ukv_f = w_ukv_all.reshape(KV_LORA_RANK, MLA_HEADS, QK_NOPE_DIM + V_HEAD_DIM)
    hw = MLA_HEADS * HEAD_PAD
    wuq_p = jnp.pad(w_uq_f, ((0, 0), (0, 0), (0, HEAD_PAD - QK_NOPE_DIM - QK_ROPE_DIM))).reshape(Q_LORA_RANK, hw)
    wk_p = jnp.pad(w_ukv_f[:, :, :QK_NOPE_DIM], ((0, 0), (0, 0), (0, HEAD_PAD - QK_NOPE_DIM))).reshape(KV_LORA_RANK, hw)
    wv = w_ukv_f[:, :, QK_NOPE_DIM:].reshape(KV_LORA_RANK, MLA_WIDTH)
    rc, rsa, rsb = _rope_tables(s)
    g_in = norm_in.reshape(1, -1)
    g_q = q_norm.reshape(1, -1)
    g_kv = kv_norm.reshape(1, -1)
    g_f = norm_final.reshape(1, -1)
    ps = pool_scale.reshape(1, -1)
    pw_bf = pool_w.astype(BF16)

    q, k, v, q_t, v_t = _qkv_fwd(z_sh, g_q, g_kv, wuq_p, wk_p, wv, rc, rsa, rsb, 2 * t_row)
    o, lse, (w_ba_all, w_bp_all, w_out_all) = _attn_fwd(q_t, k, v_t, local[3:], t_att)
    w_out_f = w_out_all.reshape(D_MODEL, D_MODEL)

    (do, delta, dgattn, dgpool, dgmerge, ddc, dh, sq_err, d_w_out, d_w_ba, d_w_bp, d_pool_w, d_pool_scale,
     d_norm_final) = _mid(o, z_sh, x2, tgt, pw_bf, ps, w_ba_all, w_bp_all, w_out_f, g_f, t_row)

    late_grads = [d_w_ba, d_w_bp, d_w_out.reshape(N_CHIPS, 256, D_MODEL)]
    small_mid = dict(pool_scale=d_pool_scale, norm_final=d_norm_final, pool_w=d_pool_w, sq_err=sq_err)
    gs_mid = _pack_rows([small_mid[n] for n, _ in SMALL_MID], _small_rows(SMALL_MID), F32)
    dq, dk_t, dv_t, (g_w_ba, g_w_bp, g_w_out), g_small_mid = _attn_bwd(q, q_t, k, v, do, lse, delta, late_grads,
                                                                      gs_mid, t_att)
    g_pool_scale, g_norm_final, g_pool_w, sq_err_all = _unpack_rows(g_small_mid, SMALL_MID)
    dzq, dzkv, dzkr, d_wuq_p, d_wk_p, d_wv, d_q_norm, d_kv_norm = _qkv_bwd(
        dq, dk_t, dv_t, z_sh, g_q, g_kv, wuq_p, wk_p, wv, rc, rsa, rsb, 2 * t_row)
    grad_x, d_norm_in, dz_sh = _inproj_bwd_x(dzq, dzkv, dzkr, dgattn, ddc, dgpool, dgmerge, x2, dh, g_in, w_in_t,
                                             2 * t_row)

    d_w_uq = d_wuq_p.reshape(Q_LORA_RANK, MLA_HEADS, HEAD_PAD)[:, :, :QK_NOPE_DIM + QK_ROPE_DIM]
    d_w_ukv = jnp.concatenate([d_wk_p.reshape(KV_LORA_RANK, MLA_HEADS, HEAD_PAD)[:, :, :QK_NOPE_DIM],
                               d_wv.reshape(KV_LORA_RANK, MLA_HEADS, V_HEAD_DIM)], axis=2)
    small_late = dict(norm_in=d_norm_in, q_norm=d_q_norm, kv_norm=d_kv_norm)
    gs = _pack_rows([small_late[n] for n, _ in SMALL_LATE], _small_rows(SMALL_LATE), F32)
    order = jnp.stack(others + [2 * cx + cy]).astype(jnp.int32)
    g_w_in_t, g_w_uq, g_w_ukv, g_small = _inproj_bwd_w(
        order, dz_sh, hn, d_w_uq.reshape(N_CHIPS, 96, 768).astype(BF16),
        d_w_ukv.reshape(N_CHIPS, 64, 1024).astype(BF16), gs, 4 * t_row)
    g_norm_in, g_q_norm, g_kv_norm = _unpack_rows(g_small, SMALL_LATE)
    g_w_uq = g_w_uq.reshape(w_uq.shape)
    g_w_ukv = g_w_ukv.reshape(w_ukv.shape)

    dl_w_in, nm_w_in, nv_w_in, g_w_in = (a.T for a in _adamw_tiled(w_in.T, g_w_in_t, m_w_in.T, v_w_in.T, 152))

    def two_d(a):
        return a.reshape(1, -1) if a.ndim == 1 else a

    names = ["norm_in", "q_norm", "w_uq", "kv_norm", "w_ukv", "pool_w", "pool_scale", "w_branch_attn",
             "w_branch_pool", "w_out", "norm_final"]
    ws = dict(norm_in=norm_in, q_norm=q_norm, w_uq=w_uq, kv_norm=kv_norm, w_ukv=w_ukv, pool_w=pool_w,
              pool_scale=pool_scale, w_branch_attn=w_branch_attn, w_branch_pool=w_branch_pool, w_out=w_out,
              norm_final=norm_final)
    gsd = dict(norm_in=g_norm_in, q_norm=g_q_norm, w_uq=g_w_uq, kv_norm=g_kv_norm, w_ukv=g_w_ukv, pool_w=g_pool_w,
               pool_scale=g_pool_scale, w_branch_attn=g_w_ba, w_branch_pool=g_w_bp, w_out=g_w_out,
               norm_final=g_norm_final)
    msd = dict(norm_in=m_norm_in, q_norm=m_q_norm, w_uq=m_w_uq, kv_norm=m_kv_norm, w_ukv=m_w_ukv, pool_w=m_pool_w,
               pool_scale=m_pool_scale, w_branch_attn=m_w_branch_attn, w_branch_pool=m_w_branch_pool, w_out=m_w_out,
               norm_final=m_norm_final)
    vsd = dict(norm_in=v_norm_in, q_norm=v_q_norm, w_uq=v_w_uq, kv_norm=v_kv_norm, w_ukv=v_w_ukv, pool_w=v_pool_w,
               pool_scale=v_pool_scale, w_branch_attn=v_w_branch_attn, w_branch_pool=v_w_branch_pool, w_out=v_w_out,
               norm_final=v_norm_final)
    dls, nms, nvs, g_outs = _adamw_many([two_d(ws[n]) for n in names], [two_d(gsd[n]) for n in names],
                                [two_d(msd[n]) for n in names], [two_d(vsd[n]) for n in names])

    grads = dict(zip(names, g_outs))
    grads["w_in"] = g_w_in
    delta_w = {n: d.reshape(ws[n].shape) for n, d in zip(names, dls)}
    new_m = {n: d.reshape(ws[n].shape) for n, d in zip(names, nms)}
    new_v = {n: d.reshape(ws[n].shape) for n, d in zip(names, nvs)}
    delta_w["w_in"], new_m["w_in"], new_v["w_in"] = dl_w_in, nm_w_in, nv_w_in
    ws["w_in"] = w_in

    order = ["norm_in", "w_in", "q_norm", "w_uq", "kv_norm", "w_ukv", "pool_w", "pool_scale", "w_branch_attn",
             "w_branch_pool", "w_out", "norm_final"]
    loss = 0.5 * jnp.sum(sq_err_all) / D_MODEL
    return (loss, grad_x.reshape(x.shape),
            *[grads[n].reshape(ws[n].shape) for n in order],
            *[delta_w[n] for n in order], *[new_m[n] for n in order], *[new_v[n] for n in order])
```

```python
import functools

import jax
import jax.numpy as jnp
import numpy as np
from jax import lax
from jax.experimental import pallas as pl
from jax.experimental.pallas import tpu as pltpu

F32 = jnp.float32
BF16 = jnp.bfloat16
MESH = pl.DeviceIdType.MESH

D_MODEL = 1024
CHUNK = 64
MLA_HEADS = 8
QK_NOPE_DIM = 64
QK_ROPE_DIM = 32
V_HEAD_DIM = 64
Q_LORA_RANK = 384
KV_LORA_RANK = 256
MLA_WIDTH = MLA_HEADS * V_HEAD_DIM
ROPE_THETA = 10000.0
POOL_WINDOWS = (2, 4, 8, 16)
POOL_WIDTH = 512
POOL_GROUP_DIM = 128
BRANCH_COLS = D_MODEL // 4
FWD_HEADS = 8
BWD_HEADS = 4
POOL_HALO = 16
EPS = 1e-6
IN_TOTAL = 4256
HEAD_PAD = 128
ATT_SCALE = (QK_NOPE_DIM + QK_ROPE_DIM) ** -0.5
ATT_SCALE_LOG2E = ATT_SCALE * 1.4426950408889634

ADAM_LR = 0.001
ADAM_B1 = 0.9
ADAM_B2 = 0.999
ADAM_EPS = 1e-08
ADAM_WD = 0.01
ADAM_STEP = 10

N_CHIPS = 4
N_DEV = 8
LANES = 128
VMEM_LIMIT = 60 * 1024 * 1024

IN_SEGMENTS = ((384, 384), (256, 256), (32, HEAD_PAD), (512, 512), (512, 512), (512, 512), (2048, 2048))
SHARD_COLS = IN_TOTAL // N_CHIPS
ZQ_COLS = slice(0, 384)
ZKV_COLS = slice(384, 640)
ZKR_TILE = slice(640, 768)


def _shard_pieces():
    bounds, off = [], 0
    for w, _ in IN_SEGMENTS:
        bounds.append((off, off + w))
        off += w
    out = []
    for j in range(N_CHIPS):
        lo, hi = SHARD_COLS * j, SHARD_COLS * (j + 1)
        out.append([(i, max(lo, a) - a, min(hi, b) - a, max(lo, a) - lo)
                    for i, (a, b) in enumerate(bounds) if max(lo, a) < min(hi, b)])
    return out


SHARD_PIECES = _shard_pieces()


def _segment(z_blocks, seg):
    parts = [z_blocks[j][:, col:col + hi - lo]
             for j, pieces in enumerate(SHARD_PIECES) for sg, lo, hi, col in pieces if sg == seg]
    return parts[0] if len(parts) == 1 else jnp.concatenate(parts, axis=1)

COMM_PARAMS = (
    ("w_in", SHARD_COLS, D_MODEL, 1, 512),
    ("w_uq", 96, 768, 0, 48),
    ("w_ukv", 64, 1024, 0, 32),
    ("w_branch_attn", 512, 256, 0, 256),
    ("w_branch_pool", 512, 256, 0, 256),
    ("w_out", 256, 1024, 0, 128),
)

SMALL_MID = (
    ("pool_scale", (512,)),
    ("norm_final", (1024,)),
    ("pool_w", (4, 128, 128)),
    ("sq_err", (8, 128)),
)
SMALL_LATE = (
    ("norm_in", (1024,)),
    ("q_norm", (384,)),
    ("kv_norm", (256,)),
)


def _small_rows(shapes):
    return -(-sum(int(np.prod(s)) for _, s in shapes) // (LANES * 8)) * 8


def _dot(a, b):
    return jnp.dot(a, b, preferred_element_type=F32)


def _dot_nt(a, b):
    return lax.dot_general(a, b, (((1,), (1,)), ((), ())), preferred_element_type=F32)


def _dot_tn(a, b):
    return lax.dot_general(a, b, (((0,), (0,)), ((), ())), preferred_element_type=F32)


def _sigmoid(x):
    return 1.0 / (1.0 + jnp.exp(-x))


def _colsum(x):
    return jnp.sum(x, axis=0, keepdims=True)


def _rms_fwd(x, g):
    r = lax.rsqrt(jnp.mean(x * x, axis=-1, keepdims=True) + EPS)
    xhat = x * r
    return xhat * g, xhat, r


def _rms_bwd(dy, xhat, r, g):
    dxhat = dy * g
    return r * (dxhat - xhat * jnp.mean(dxhat * xhat, axis=-1, keepdims=True))


def _rope(v, c, sa, sb):
    return v * c + pltpu.roll(v, 112, 1) * sa + pltpu.roll(v, 16, 1) * sb


def _unrope(d, c, sa, sb):
    return d * c + pltpu.roll(d * sa, 16, 1) + pltpu.roll(d * sb, 112, 1)


def _row_spec(tm, n):
    return pl.BlockSpec((tm, n), lambda i: (i, 0))


def _full_spec(shape):
    nd = len(shape)
    return pl.BlockSpec(shape, lambda i: (0,) * nd)


def _tiles(s):
    t_att = 512 if s >= 2048 else 128
    t_row = 256 if s >= 1024 else 128
    return t_att, t_row


def _inproj_fwd(order, x, norm_in, early_shards, tm):
    s = x.shape[0]
    n_tiles = s // tm
    gat = _Gather(COMM_PARAMS[:3])
    n_w = len(gat.params)
    arrival = (1, 2, 0)

    def body(order_ref, x_ref, g_ref, *rest):
        w_loc, (hn_ref, z_ref), w_all = rest[:n_w], rest[n_w:n_w + 2], rest[n_w + 2:2 * n_w + 2]
        w_vmem, hn_all, w_sem = rest[2 * n_w + 2:2 * n_w + 5]
        gat.bind(w_loc, w_all, rest[2 * n_w + 5:])
        ph, i = pl.program_id(0), pl.program_id(1)
        pl.when(jnp.logical_and(ph == 0, i == 0))(gat.start)

        def fetch(phase):
            src = w_loc[0] if phase == 0 else w_all[0].at[order_ref[phase]]
            return pltpu.make_async_copy(src, w_vmem.at[phase % 2], w_sem.at[phase % 2])

        def landed(f):
            gat.relay_one(0, arrival[f])
            gat.await_one(0, arrival[f])

        @pl.when(jnp.logical_and(ph == 0, i == 0))
        def _():
            fetch(0).start()
            fetch(0).wait()

        @pl.when(jnp.logical_and(ph == 1, i == 0))
        def _():
            landed(0)
            fetch(1).start()
            fetch(1).wait()

        for f in (1, 2):
            @pl.when(jnp.logical_and(ph == f, i == n_tiles - 1))
            def _(f=f):
                landed(f)
                fetch(f + 1).start()

            @pl.when(jnp.logical_and(ph == f + 1, i == 0))
            def _(f=f):
                fetch(f + 1).wait()

        rows = pl.ds(pl.multiple_of(i * tm, tm), tm)

        @pl.when(ph == 0)
        def _():
            hn, _, _ = _rms_fwd(x_ref[...], g_ref[...])
            hn = hn.astype(BF16)
            hn_ref[...] = hn
            hn_all[rows, :] = hn

        z_ref[0] = _dot_nt(hn_all[rows, :], w_vmem[ph % 2])

        @pl.when(jnp.logical_and(ph == N_CHIPS - 1, i == n_tiles - 1))
        def _():
            for p in range(1, n_w):
                for j in range(3):
                    gat.relay_one(p, j)
            for p in range(1, n_w):
                for j in range(3):
                    gat.await_one(p, j)
            gat.wait_sends()

    def tile_in_phase0(ph, i, order):
        return (jnp.where(ph == 0, i, n_tiles - 1), 0)

    any_spec = pl.BlockSpec(memory_space=pl.ANY)
    grid_spec = pltpu.PrefetchScalarGridSpec(
        num_scalar_prefetch=1,
        grid=(N_CHIPS, n_tiles),
        in_specs=[pl.BlockSpec((tm, D_MODEL), tile_in_phase0),
                  pl.BlockSpec((1, D_MODEL), lambda ph, i, order: (0, 0))] + [any_spec] * n_w,
        out_specs=[pl.BlockSpec((tm, D_MODEL), tile_in_phase0),
                   pl.BlockSpec((1, tm, SHARD_COLS), lambda ph, i, order: (order[ph], i, 0))] + [any_spec] * n_w,
        scratch_shapes=[pltpu.VMEM((2, SHARD_COLS, D_MODEL), BF16), pltpu.VMEM((s, D_MODEL), BF16),
                        pltpu.SemaphoreType.DMA((2,))] + gat.scratch,
    )
    out = pl.pallas_call(
        body,
        name="inproj_fwd",
        grid_spec=grid_spec,
        out_shape=[jax.ShapeDtypeStruct((s, D_MODEL), BF16), jax.ShapeDtypeStruct((N_CHIPS, s, SHARD_COLS), F32)]
        + gat.out_shape,
        compiler_params=pltpu.CompilerParams(dimension_semantics=("arbitrary", "arbitrary"),
                                             vmem_limit_bytes=VMEM_LIMIT),
    )(order, x, norm_in, *early_shards)
    return out[0], out[1], out[2:]


def _qkv_fwd(z_sh, q_norm, kv_norm, wuq_p, wk_p, wv, rc, rsa, rsb, tm):
    s = z_sh.shape[1]
    hw = MLA_HEADS * HEAD_PAD

    def body(z_ref, gq_ref, gkv_ref, wuq_ref, wk_ref, wv_ref, c_ref, sa_ref, sb_ref,
             q_ref, k_ref, v_ref, qt_ref, vt_ref):
        c, sa, sb = c_ref[...], sa_ref[...], sb_ref[...]
        z0 = z_ref[0]
        cq, _, _ = _rms_fwd(z0[:, ZQ_COLS], gq_ref[...])
        qf = _dot(cq.astype(BF16), wuq_ref[...])
        ckv, _, _ = _rms_fwd(z0[:, ZKV_COLS], gkv_ref[...])
        ckv = ckv.astype(BF16)
        kn = _dot(ckv, wk_ref[...])
        lane = lax.broadcasted_iota(jnp.int32, (tm, HEAD_PAD), 1)
        zkr = jnp.where(lane < QK_ROPE_DIM, z0[:, ZKR_TILE], 0.0)
        kr = _rope(pltpu.roll(zkr, 64, 1), c, sa, sb)
        for h in range(MLA_HEADS):
            cols = slice(h * HEAD_PAD, (h + 1) * HEAD_PAD)
            qh = _rope(qf[:, cols], c, sa, sb)
            q_ref[:, cols] = qh.astype(BF16)
            qt_ref[cols, :] = qh.T.astype(BF16)
            k_ref[:, cols] = (kn[:, cols] + kr).astype(BF16)
        vf = _dot(ckv, wv_ref[...])
        v_ref[...] = vf.astype(BF16)
        vt_ref[...] = vf.T.astype(BF16)

    return pl.pallas_call(
        body,
        name="qkv_fwd",
        grid=(s // tm,),
        in_specs=[
            pl.BlockSpec((1, tm, SHARD_COLS), lambda i: (0, i, 0)),
            _full_spec((1, Q_LORA_RANK)), _full_spec((1, KV_LORA_RANK)),
            _full_spec((Q_LORA_RANK, hw)), _full_spec((KV_LORA_RANK, hw)), _full_spec((KV_LORA_RANK, MLA_WIDTH)),
            _row_spec(tm, HEAD_PAD), _row_spec(tm, HEAD_PAD), _row_spec(tm, HEAD_PAD),
        ],
        out_specs=[_row_spec(tm, hw), _row_spec(tm, hw), _row_spec(tm, MLA_WIDTH),
                   pl.BlockSpec((hw, tm), lambda i: (0, i)), pl.BlockSpec((MLA_WIDTH, tm), lambda i: (0, i))],
        out_shape=[jax.ShapeDtypeStruct((s, hw), BF16), jax.ShapeDtypeStruct((s, hw), BF16),
                   jax.ShapeDtypeStruct((s, MLA_WIDTH), BF16),
                   jax.ShapeDtypeStruct((hw, s), BF16), jax.ShapeDtypeStruct((MLA_WIDTH, s), BF16)],
        compiler_params=pltpu.CompilerParams(dimension_semantics=("parallel",), vmem_limit_bytes=VMEM_LIMIT),
    )(z_sh, q_norm, kv_norm, wuq_p, wk_p, wv, rc, rsa, rsb)


def _chunk_mask(t, keys_on_rows):
    rows = lax.broadcasted_iota(jnp.int32, (t, t), 0) // CHUNK
    cols = lax.broadcasted_iota(jnp.int32, (t, t), 1) // CHUNK
    return rows <= cols if keys_on_rows else cols <= rows


def _attn_fwd(q_t, k, v_t, late_shards, t):
    s = k.shape[0]
    groups = MLA_HEADS // FWD_HEADS
    n_q = s // t
    gat = _Gather(COMM_PARAMS[3:])
    n_w = len(gat.params)

    def body(qt_ref, k_ref, k2_ref, vt_ref, *rest):
        w_in, (o_ref, lse_ref), w_out = rest[:n_w], rest[n_w:n_w + 2], rest[n_w + 2:2 * n_w + 2]
        gat.bind(w_in, w_out, rest[2 * n_w + 2:])
        i = pl.program_id(1)
        step_no = pl.program_id(0) * n_q + i
        pl.when(step_no == 0)(gat.start)
        pl.when(step_no == groups * n_q // 2)(gat.relay)
        mask = _chunk_mask(t, True)
        qcs = [slice(hh * HEAD_PAD, (hh + 1) * HEAD_PAD) for hh in range(FWD_HEADS)]
        vcs = [slice(hh * V_HEAD_DIM, (hh + 1) * V_HEAD_DIM) for hh in range(FWD_HEADS)]
        qts = [qt_ref[qc, :] for qc in qcs]

        def step(j, carry, masked):
            keys = pl.ds(pl.multiple_of(j * t, t), t)
            out = []
            for hh in range(FWD_HEADS):
                m, l, acc = carry[hh]
                sc = _dot(k_ref[keys, qcs[hh]], qts[hh])
                if masked:
                    sc = jnp.where(mask, sc, -jnp.inf)
                m_new = jnp.maximum(m, jnp.max(sc, axis=0, keepdims=True))
                alpha = jnp.exp2((m - m_new) * ATT_SCALE_LOG2E)
                p = jnp.exp2((_dot(k2_ref[keys, qcs[hh]], qts[hh]) - m_new) * ATT_SCALE_LOG2E)
                if masked:
                    p = jnp.where(mask, p, 0.0)
                l = alpha * l + jnp.sum(p, axis=0, keepdims=True)
                acc = alpha * acc + _dot(vt_ref[vcs[hh], keys], p.astype(BF16))
                out.append((m_new, l, acc))
            return tuple(out)

        one = (jnp.full((1, t), -jnp.inf, F32), jnp.zeros((1, t), F32), jnp.zeros((V_HEAD_DIM, t), F32))
        carry = lax.fori_loop(0, i, functools.partial(step, masked=False), (one,) * FWD_HEADS)
        carry = step(i, carry, True)
        o_ref[...] = jnp.concatenate([carry[hh][2] / carry[hh][1] for hh in range(FWD_HEADS)], axis=0).T
        for hh in range(FWD_HEADS):
            m, l, _ = carry[hh]
            lse_ref[:, qcs[hh]] = jnp.broadcast_to(m * ATT_SCALE_LOG2E + jnp.log2(l), (HEAD_PAD, t)).T
        pl.when(step_no == groups * n_q - 1)(gat.finish)

    any_spec = pl.BlockSpec(memory_space=pl.ANY)
    out = pl.pallas_call(
        body,
        name="attn_fwd",
        grid=(groups, n_q),
        in_specs=[
            pl.BlockSpec((FWD_HEADS * HEAD_PAD, t), lambda p, i: (p, i)),
            pl.BlockSpec((s, FWD_HEADS * HEAD_PAD), lambda p, i: (0, p), pipeline_mode=pl.Buffered(1)),
            pl.BlockSpec((s, FWD_HEADS * HEAD_PAD), lambda p, i: (0, p), pipeline_mode=pl.Buffered(1)),
            pl.BlockSpec((FWD_HEADS * V_HEAD_DIM, s), lambda p, i: (p, 0), pipeline_mode=pl.Buffered(1)),
        ] + [any_spec] * n_w,
        out_specs=[
            pl.BlockSpec((t, FWD_HEADS * V_HEAD_DIM), lambda p, i: (i, p)),
            pl.BlockSpec((t, FWD_HEADS * HEAD_PAD), lambda p, i: (i, p)),
        ] + [any_spec] * n_w,
        out_shape=[jax.ShapeDtypeStruct((s, MLA_WIDTH), F32), jax.ShapeDtypeStruct((s, MLA_HEADS * HEAD_PAD), F32)]
        + gat.out_shape,
        scratch_shapes=gat.scratch,
        compiler_params=pltpu.CompilerParams(dimension_semantics=("arbitrary", "arbitrary"),
                                             vmem_limit_bytes=VMEM_LIMIT),
    )(q_t, k, k, v_t, *late_shards)
    return out[0], out[1], out[2:]


def _mid(o, z_sh, x, target, pool_w, pool_scale, w_ba, w_bp, w_out, norm_final, tm):
    s = x.shape[0]
    n_tiles = s // tm
    halo_per_tile = tm // POOL_HALO

    def body(o_ref, z0_ref, z1_ref, z1h_ref, z2_ref, z3_ref, x_ref, t_ref, pw_ref, ps_ref, wba_ref, wbp_ref,
             wout_ref, gf_ref,
             do_ref, dl_ref, dga_ref, dgp_ref, dgm_ref, ddc_ref, dh_ref,
             loss_ref, dwout_out, dwba_out, dwbp_out, dpw_ref, dps_ref, dgf_ref,
             ubuf, dwout_ref, dwba_ref, dwbp_ref):
        i = pl.program_id(0)

        @pl.when(i == 0)
        def _():
            loss_ref[...] = jnp.zeros_like(loss_ref)
            dwout_ref[...] = jnp.zeros_like(dwout_ref)
            dwba_ref[...] = jnp.zeros_like(dwba_ref)
            dwbp_ref[...] = jnp.zeros_like(dwbp_ref)
            dpw_ref[...] = jnp.zeros_like(dpw_ref)
            dps_ref[...] = jnp.zeros_like(dps_ref)
            dgf_ref[...] = jnp.zeros_like(dgf_ref)

        zs = [z0_ref[0], z1_ref[0], z2_ref[0], z3_ref[0]]
        o = o_ref[...]
        ga = _segment(zs, 3)
        sga = _sigmoid(ga)
        silu_a = ga * sga
        y_attn = (o * silu_a).astype(BF16)

        ubuf[0:POOL_HALO, :] = jnp.where(i > 0, _segment([None, z1h_ref[0]], 4), 0.0)
        ubuf[POOL_HALO:, :] = _segment(zs, 4)
        row = lax.broadcasted_iota(jnp.int32, (tm, POOL_GROUP_DIM), 0) + i * tm
        ps = ps_ref[...]
        gp = _segment(zs, 5)
        sgp = _sigmoid(gp)
        silu_p = gp * sgp
        d_bf, dm, inv_cnt = [], [], []
        for g, w in enumerate(POOL_WINDOWS):
            cols = slice(g * POOL_GROUP_DIM, (g + 1) * POOL_GROUP_DIM)
            wsum = ubuf[POOL_HALO:, cols]
            for kk in range(1, w):
                wsum = wsum + ubuf[POOL_HALO - kk:POOL_HALO - kk + tm, cols]
            inv = 1.0 / jnp.minimum(row + 1, w).astype(F32)
            dg = (wsum * inv - ubuf[POOL_HALO:, cols]).astype(BF16)
            d_bf.append(dg)
            inv_cnt.append(inv)
            dm.append(_dot(dg, pw_ref[g]))
        dm = jnp.concatenate(dm, axis=1)
        yp = dm * ps
        y_pool = (yp * silu_p).astype(BF16)

        a = jnp.concatenate([_dot(y_attn, wba_ref[j]) for j in range(N_CHIPS)], axis=1)
        p = jnp.concatenate([_dot(y_pool, wbp_ref[j]) for j in range(N_CHIPS)], axis=1)
        gm = _segment(zs, 6)
        gate_a = _sigmoid(gm[:, :D_MODEL])
        gate_p = _sigmoid(gm[:, D_MODEL:])
        merged = (gate_a * a + gate_p * p).astype(BF16)
        h = x_ref[...] + _dot(merged, wout_ref[...])
        gf = gf_ref[...]
        y, xhat, r = _rms_fwd(h, gf)
        err = y - t_ref[...]
        e2 = err * err
        e2 = jnp.sum(e2.reshape(tm // 8, 8, D_MODEL), axis=0)
        acc = e2[:, 0:LANES]
        for cidx in range(1, D_MODEL // LANES):
            acc = acc + e2[:, cidx * LANES:(cidx + 1) * LANES]
        loss_ref[...] += acc

        dy = err * (1.0 / D_MODEL)
        dgf_ref[...] += _colsum(dy * xhat)
        dh = _rms_bwd(dy, xhat, r, gf)
        dh_ref[...] = dh
        dh_bf = dh.astype(BF16)
        dwout_ref[...] += _dot_tn(merged, dh_bf)
        dmerged = _dot_nt(dh_bf, wout_ref[...])
        da = (dmerged * gate_a).astype(BF16)
        dp = (dmerged * gate_p).astype(BF16)
        dgm_ref[:, :D_MODEL] = (dmerged * a * gate_a * (1.0 - gate_a)).astype(BF16)
        dgm_ref[:, D_MODEL:] = (dmerged * p * gate_p * (1.0 - gate_p)).astype(BF16)
        dy_attn = dy_pool = None
        for j in range(N_CHIPS):
            cols = slice(j * BRANCH_COLS, (j + 1) * BRANCH_COLS)
            dwba_ref[j] += _dot_tn(y_attn, da[:, cols])
            dwbp_ref[j] += _dot_tn(y_pool, dp[:, cols])
            pa = _dot_nt(da[:, cols], wba_ref[j])
            pp = _dot_nt(dp[:, cols], wbp_ref[j])
            dy_attn = pa if dy_attn is None else dy_attn + pa
            dy_pool = pp if dy_pool is None else dy_pool + pp

        do = dy_attn * silu_a
        do_ref[...] = do
        dga_ref[...] = (dy_attn * o * (sga * (1.0 + ga * (1.0 - sga)))).astype(BF16)
        doo = do * o
        for hd in range(MLA_HEADS):
            dl = jnp.sum(doo[:, hd * V_HEAD_DIM:(hd + 1) * V_HEAD_DIM], axis=1, keepdims=True)
            dl_ref[:, hd * HEAD_PAD:(hd + 1) * HEAD_PAD] = jnp.broadcast_to(dl, (tm, HEAD_PAD))

        dyp = dy_pool * silu_p
        dgp_ref[...] = (dy_pool * yp * (sgp * (1.0 + gp * (1.0 - sgp)))).astype(BF16)
        dps_ref[...] += _colsum(dyp * dm)
        dmm = (dyp * ps).astype(BF16)
        for g in range(len(POOL_WINDOWS)):
            cols = slice(g * POOL_GROUP_DIM, (g + 1) * POOL_GROUP_DIM)
            dpw_ref[g] += _dot_tn(d_bf[g], dmm[:, cols])
            ddc_ref[:, cols] = _dot_nt(dmm[:, cols], pw_ref[g]) * inv_cnt[g]

        @pl.when(i == n_tiles - 1)
        def _():
            dwout_out[...] = dwout_ref[...].astype(BF16)
            dwba_out[...] = dwba_ref[...].astype(BF16)
            dwbp_out[...] = dwbp_ref[...].astype(BF16)

    row_in = lambda n: _row_spec(tm, n)
    in_specs = [
        row_in(MLA_WIDTH),
        pl.BlockSpec((1, tm, SHARD_COLS), lambda i: (0, i, 0)), pl.BlockSpec((1, tm, SHARD_COLS), lambda i: (1, i, 0)),
        pl.BlockSpec((1, POOL_HALO, SHARD_COLS), lambda i: (1, jnp.maximum(i * halo_per_tile - 1, 0), 0)),
        pl.BlockSpec((1, tm, SHARD_COLS), lambda i: (2, i, 0)), pl.BlockSpec((1, tm, SHARD_COLS), lambda i: (3, i, 0)),
        row_in(D_MODEL), row_in(D_MODEL),
        _full_spec((4, POOL_GROUP_DIM, POOL_GROUP_DIM)), _full_spec((1, POOL_WIDTH)),
        _full_spec((N_CHIPS, MLA_WIDTH, BRANCH_COLS)), _full_spec((N_CHIPS, POOL_WIDTH, BRANCH_COLS)),
        _full_spec((D_MODEL, D_MODEL)), _full_spec((1, D_MODEL)),
    ]
    out_shape = [
        jax.ShapeDtypeStruct((s, MLA_WIDTH), F32),
        jax.ShapeDtypeStruct((s, MLA_HEADS * HEAD_PAD), F32),
        jax.ShapeDtypeStruct((s, MLA_WIDTH), BF16),
        jax.ShapeDtypeStruct((s, POOL_WIDTH), BF16),
        jax.ShapeDtypeStruct((s, 2 * D_MODEL), BF16),
        jax.ShapeDtypeStruct((s, POOL_WIDTH), F32),
        jax.ShapeDtypeStruct((s, D_MODEL), F32),
        jax.ShapeDtypeStruct((8, LANES), F32),
        jax.ShapeDtypeStruct((D_MODEL, D_MODEL), BF16),
        jax.ShapeDtypeStruct((N_CHIPS, MLA_WIDTH, BRANCH_COLS), BF16),
        jax.ShapeDtypeStruct((N_CHIPS, POOL_WIDTH, BRANCH_COLS), BF16),
        jax.ShapeDtypeStruct((4, POOL_GROUP_DIM, POOL_GROUP_DIM), F32),
        jax.ShapeDtypeStruct((1, POOL_WIDTH), F32),
        jax.ShapeDtypeStruct((1, D_MODEL), F32),
    ]
    out_specs = [
        row_in(MLA_WIDTH), row_in(MLA_HEADS * HEAD_PAD), row_in(MLA_WIDTH), row_in(POOL_WIDTH),
        row_in(2 * D_MODEL), row_in(POOL_WIDTH), row_in(D_MODEL),
        _full_spec((8, LANES)), _full_spec((D_MODEL, D_MODEL)), _full_spec((N_CHIPS, MLA_WIDTH, BRANCH_COLS)),
        _full_spec((N_CHIPS, POOL_WIDTH, BRANCH_COLS)), _full_spec((4, POOL_GROUP_DIM, POOL_GROUP_DIM)),
        _full_spec((1, POOL_WIDTH)), _full_spec((1, D_MODEL)),
    ]
    return pl.pallas_call(
        body,
        name="mid",
        grid=(n_tiles,),
        in_specs=in_specs,
        out_specs=out_specs,
        out_shape=out_shape,
        scratch_shapes=[
            pltpu.VMEM((tm + POOL_HALO, POOL_WIDTH), F32),
            pltpu.VMEM((D_MODEL, D_MODEL), F32),
            pltpu.VMEM((N_CHIPS, MLA_WIDTH, BRANCH_COLS), F32),
            pltpu.VMEM((N_CHIPS, POOL_WIDTH, BRANCH_COLS), F32),
        ],
        compiler_params=pltpu.CompilerParams(dimension_semantics=("arbitrary",), vmem_limit_bytes=VMEM_LIMIT),
    )(o, z_sh, z_sh, z_sh, z_sh, z_sh, x, target, pool_w, pool_scale, w_ba, w_bp, w_out, norm_final)


def _attn_bwd(q, q_t, k, v, do, lse, delta, late_grads, gs_mid, t):
    s = q.shape[0]
    groups = MLA_HEADS // BWD_HEADS
    n_q = s // t
    red = _Reduce(COMM_PARAMS[3:])
    n_w = len(red.params)
    small = _SmallSum(gs_mid.shape[0])
    n_red = len(red.scratch)

    def body(q_ref, qt_ref, do_ref, lse_ref, dl_ref, k_ref, v_ref, *rest):
        g_in, gs_ref = rest[:n_w], rest[n_w]
        (dq_ref, dk_ref, dv_ref), g_out, gsum_ref = rest[n_w + 1:n_w + 4], rest[n_w + 4:2 * n_w + 4], rest[2 * n_w + 4]
        scratch = rest[2 * n_w + 5:]
        red.bind(g_in, g_out, scratch[:n_red])
        small.bind(gs_ref, gsum_ref, scratch[n_red:])
        i = pl.program_id(1)
        step_no = pl.program_id(0) * n_q + i

        @pl.when(step_no == 0)
        def _():
            red.start()
            small.start()

        pl.when(step_no == groups * n_q // 2)(red.exchange)

        @pl.when(i == 0)
        def _():
            dk_ref[...] = jnp.zeros_like(dk_ref)
            dv_ref[...] = jnp.zeros_like(dv_ref)

        mask = _chunk_mask(t, False)
        qcs = [slice(hh * HEAD_PAD, (hh + 1) * HEAD_PAD) for hh in range(BWD_HEADS)]
        vcs = [slice(hh * V_HEAD_DIM, (hh + 1) * V_HEAD_DIM) for hh in range(BWD_HEADS)]
        qhs = [q_ref[:, qc] for qc in qcs]
        qts = [qt_ref[qc, :] for qc in qcs]
        dohs = [do_ref[:, vc].astype(BF16) for vc in vcs]
        do_t = do_ref[...].T.astype(BF16)
        dots = [do_t[vc, :] for vc in vcs]
        lses = [jnp.tile(lse_ref[:, qc], (1, t // HEAD_PAD)) for qc in qcs]
        dls = [jnp.tile(dl_ref[:, qc], (1, t // HEAD_PAD)) for qc in qcs]

        def step(j, dqs, masked):
            keys = pl.ds(pl.multiple_of(j * t, t), t)
            out = []
            for hh in range(BWD_HEADS):
                kj = k_ref[keys, qcs[hh]]
                vj = v_ref[keys, vcs[hh]]
                p = jnp.exp2(_dot_nt(qhs[hh], kj) * ATT_SCALE_LOG2E - lses[hh])
                if masked:
                    p = jnp.where(mask, p, 0.0)
                ds = (p * (_dot_nt(dohs[hh], vj) - dls[hh])).astype(BF16)
                dv_ref[vcs[hh], keys] += _dot(dots[hh], p.astype(BF16))
                dk_ref[qcs[hh], keys] += _dot(qts[hh], ds) * ATT_SCALE
                out.append(dqs[hh] + _dot(ds, kj))
            return tuple(out)

        zero = jnp.zeros((t, HEAD_PAD), F32)
        dqs = lax.fori_loop(0, i, functools.partial(step, masked=False), (zero,) * BWD_HEADS)
        dqs = step(i, dqs, True)
        for hh in range(BWD_HEADS):
            dq_ref[:, qcs[hh]] = dqs[hh] * ATT_SCALE

        @pl.when(step_no == groups * n_q - 1)
        def _():
            red.finish()
            small.finish()

    hw = MLA_HEADS * HEAD_PAD
    any_spec = pl.BlockSpec(memory_space=pl.ANY)
    out = pl.pallas_call(
        body,
        name="attn_bwd",
        grid=(groups, n_q),
        in_specs=[
            pl.BlockSpec((t, BWD_HEADS * HEAD_PAD), lambda p, i: (i, p)),
            pl.BlockSpec((BWD_HEADS * HEAD_PAD, t), lambda p, i: (p, i)),
            pl.BlockSpec((t, BWD_HEADS * V_HEAD_DIM), lambda p, i: (i, p)),
            pl.BlockSpec((t, BWD_HEADS * HEAD_PAD), lambda p, i: (i, p)),
            pl.BlockSpec((t, BWD_HEADS * HEAD_PAD), lambda p, i: (i, p)),
            pl.BlockSpec((s, BWD_HEADS * HEAD_PAD), lambda p, i: (0, p), pipeline_mode=pl.Buffered(1)),
            pl.BlockSpec((s, BWD_HEADS * V_HEAD_DIM), lambda p, i: (0, p), pipeline_mode=pl.Buffered(1)),
        ] + [any_spec] * n_w + [pl.BlockSpec(small.spec_shape, lambda p, i: (0, 0))],
        out_specs=[
            pl.BlockSpec((t, BWD_HEADS * HEAD_PAD), lambda p, i: (i, p)),
            pl.BlockSpec((BWD_HEADS * HEAD_PAD, s), lambda p, i: (p, 0)),
            pl.BlockSpec((BWD_HEADS * V_HEAD_DIM, s), lambda p, i: (p, 0)),
        ] + [any_spec] * n_w + [pl.BlockSpec(small.spec_shape, lambda p, i: (0, 0))],
        out_shape=[jax.ShapeDtypeStruct((s, hw), F32), jax.ShapeDtypeStruct((hw, s), F32),
                   jax.ShapeDtypeStruct((MLA_WIDTH, s), F32)] + red.out_shape + [small.out_shape],
        scratch_shapes=red.scratch + small.scratch,
        compiler_params=pltpu.CompilerParams(dimension_semantics=("arbitrary", "arbitrary"),
                                             vmem_limit_bytes=VMEM_LIMIT),
    )(q, q_t, do, lse, delta, k, v, *late_grads, gs_mid)
    return out[0], out[1], out[2], out[3:3 + n_w], out[3 + n_w]


def _qkv_bwd(dq, dk_t, dv_t, z_sh, q_norm, kv_norm, wuq_p, wk_p, wv, rc, rsa, rsb, tm):
    s = z_sh.shape[1]
    hw = MLA_HEADS * HEAD_PAD

    def body(dq_ref, dk_ref, dv_ref, z_ref, gq_ref, gkv_ref, wuq_ref, wk_ref, wv_ref,
             c_ref, sa_ref, sb_ref,
             dzq_ref, dzkv_ref, dzkr_ref, dwuq_ref, dwk_ref, dwv_ref, dgq_ref, dgkv_ref):
        i = pl.program_id(0)

        @pl.when(i == 0)
        def _():
            dwuq_ref[...] = jnp.zeros_like(dwuq_ref)
            dwk_ref[...] = jnp.zeros_like(dwk_ref)
            dwv_ref[...] = jnp.zeros_like(dwv_ref)
            dgq_ref[...] = jnp.zeros_like(dgq_ref)
            dgkv_ref[...] = jnp.zeros_like(dgkv_ref)

        c, sa, sb = c_ref[...], sa_ref[...], sb_ref[...]
        gq, gkv = gq_ref[...], gkv_ref[...]

        z0 = z_ref[0]
        cq, xq, rq = _rms_fwd(z0[:, ZQ_COLS], gq)
        dqp = jnp.concatenate(
            [_unrope(dq_ref[:, h * HEAD_PAD:(h + 1) * HEAD_PAD], c, sa, sb) for h in range(MLA_HEADS)],
            axis=1).astype(BF16)
        dwuq_ref[...] += _dot_tn(cq.astype(BF16), dqp)
        dcq = _dot_nt(dqp, wuq_ref[...])
        dgq_ref[...] += _colsum(dcq * xq)
        dzq_ref[...] = _rms_bwd(dcq, xq, rq, gq).astype(BF16)

        ckv, xkv, rkv = _rms_fwd(z0[:, ZKV_COLS], gkv)
        ckv = ckv.astype(BF16)
        dkf = dk_ref[...].T
        dk_bf = dkf.astype(BF16)
        dv_bf = dv_ref[...].T.astype(BF16)
        dwk_ref[...] += _dot_tn(ckv, dk_bf)
        dwv_ref[...] += _dot_tn(ckv, dv_bf)
        dckv = _dot_nt(dk_bf, wk_ref[...]) + _dot_nt(dv_bf, wv_ref[...])
        dgkv_ref[...] += _colsum(dckv * xkv)
        dzkv_ref[...] = _rms_bwd(dckv, xkv, rkv, gkv).astype(BF16)

        dkr = dkf[:, 0:HEAD_PAD]
        for h in range(1, MLA_HEADS):
            dkr = dkr + dkf[:, h * HEAD_PAD:(h + 1) * HEAD_PAD]
        dkr = pltpu.roll(_unrope(dkr, c, sa, sb), 64, 1)
        lane = lax.broadcasted_iota(jnp.int32, (tm, HEAD_PAD), 1)
        dzkr_ref[...] = jnp.where(lane < QK_ROPE_DIM, dkr, 0.0).astype(BF16)

    return pl.pallas_call(
        body,
        name="qkv_bwd",
        grid=(s // tm,),
        in_specs=[
            _row_spec(tm, hw), pl.BlockSpec((hw, tm), lambda i: (0, i)), pl.BlockSpec((MLA_WIDTH, tm), lambda i: (0, i)),
            pl.BlockSpec((1, tm, SHARD_COLS), lambda i: (0, i, 0)),
            _full_spec((1, Q_LORA_RANK)), _full_spec((1, KV_LORA_RANK)),
            _full_spec((Q_LORA_RANK, hw)), _full_spec((KV_LORA_RANK, hw)), _full_spec((KV_LORA_RANK, MLA_WIDTH)),
            _row_spec(tm, HEAD_PAD), _row_spec(tm, HEAD_PAD), _row_spec(tm, HEAD_PAD),
        ],
        out_specs=[
            _row_spec(tm, Q_LORA_RANK), _row_spec(tm, KV_LORA_RANK), _row_spec(tm, HEAD_PAD),
            _full_spec((Q_LORA_RANK, hw)), _full_spec((KV_LORA_RANK, hw)), _full_spec((KV_LORA_RANK, MLA_WIDTH)),
            _full_spec((1, Q_LORA_RANK)), _full_spec((1, KV_LORA_RANK)),
        ],
        out_shape=[
            jax.ShapeDtypeStruct((s, Q_LORA_RANK), BF16), jax.ShapeDtypeStruct((s, KV_LORA_RANK), BF16),
            jax.ShapeDtypeStruct((s, HEAD_PAD), BF16),
            jax.ShapeDtypeStruct((Q_LORA_RANK, hw), F32), jax.ShapeDtypeStruct((KV_LORA_RANK, hw), F32),
            jax.ShapeDtypeStruct((KV_LORA_RANK, MLA_WIDTH), F32),
            jax.ShapeDtypeStruct((1, Q_LORA_RANK), F32), jax.ShapeDtypeStruct((1, KV_LORA_RANK), F32),
        ],
        compiler_params=pltpu.CompilerParams(dimension_semantics=("arbitrary",), vmem_limit_bytes=VMEM_LIMIT),
    )(dq, dk_t, dv_t, z_sh, q_norm, kv_norm, wuq_p, wk_p, wv, rc, rsa, rsb)


def _inproj_bwd_x(dzq, dzkv, dzkr, dgattn, ddc, dgpool, dgmerge, x, dh, norm_in, w_in_t, tm):
    s = x.shape[0]
    n_tiles = s // tm
    halo_per_tile = tm // POOL_HALO
    n_halo = s // POOL_HALO
    u_seg = 4

    def body(dzq_ref, dzkv_ref, dzkr_ref, dga_ref, ddc_ref, ddn_ref, dgp_ref, dgm_ref, x_ref, dh_ref,
             g_ref, w_hbm, gx_ref, dgin_ref, dzs_ref, w_vmem, dbuf, sem):
        i = pl.program_id(0)

        @pl.when(i == 0)
        def _():
            cp = pltpu.make_async_copy(w_hbm, w_vmem, sem)
            cp.start()
            dgin_ref[...] = jnp.zeros_like(dgin_ref)
            cp.wait()

        dbuf[0:tm, :] = ddc_ref[...]
        dbuf[tm:, :] = jnp.where(i < n_tiles - 1, ddn_ref[...], 0.0)
        row = lax.broadcasted_iota(jnp.int32, (tm, POOL_GROUP_DIM), 0) + i * tm
        du = []
        for g, w in enumerate(POOL_WINDOWS):
            cols = slice(g * POOL_GROUP_DIM, (g + 1) * POOL_GROUP_DIM)
            fsum = dbuf[0:tm, cols]
            for kk in range(1, w):
                fsum = fsum + dbuf[kk:kk + tm, cols]
            du.append(fsum - dbuf[0:tm, cols] * jnp.minimum(row + 1, w).astype(F32))
        du = jnp.concatenate(du, axis=1).astype(BF16)

        dz = [dzq_ref[...], dzkv_ref[...], dzkr_ref[...], dga_ref[...], du, dgp_ref[...], dgm_ref[...]]
        dz = jnp.concatenate([d[:, :w] for d, (w, _) in zip(dz, IN_SEGMENTS)], axis=1)
        for j in range(N_CHIPS):
            dzs_ref[j] = dz[:, j * SHARD_COLS:(j + 1) * SHARD_COLS].T
        dhn = _dot(dz, w_vmem[...])

        g = g_ref[...]
        _, xhat, r = _rms_fwd(x_ref[...], g)
        dgin_ref[...] += _colsum(dhn * xhat)
        gx_ref[...] = dh_ref[...] + _rms_bwd(dhn, xhat, r, g)

    any_spec = pl.BlockSpec(memory_space=pl.ANY)
    seg_w = [wide for _, wide in IN_SEGMENTS]
    return pl.pallas_call(
        body,
        name="inproj_bwd_x",
        grid=(n_tiles,),
        in_specs=[
            _row_spec(tm, seg_w[0]), _row_spec(tm, seg_w[1]), _row_spec(tm, seg_w[2]),
            _row_spec(tm, seg_w[3]), _row_spec(tm, seg_w[u_seg]),
            pl.BlockSpec((POOL_HALO, POOL_WIDTH), lambda i: (jnp.minimum((i + 1) * halo_per_tile, n_halo - 1), 0)),
            _row_spec(tm, seg_w[5]), _row_spec(tm, seg_w[6]),
            _row_spec(tm, D_MODEL), _row_spec(tm, D_MODEL),
            _full_spec((1, D_MODEL)), any_spec,
        ],
        out_specs=[_row_spec(tm, D_MODEL), _full_spec((1, D_MODEL)),
                   pl.BlockSpec((N_CHIPS, SHARD_COLS, tm), lambda i: (0, 0, i))],
        out_shape=[jax.ShapeDtypeStruct((s, D_MODEL), F32), jax.ShapeDtypeStruct((1, D_MODEL), F32),
                   jax.ShapeDtypeStruct((N_CHIPS, SHARD_COLS, s), BF16)],
        scratch_shapes=[
            pltpu.VMEM((IN_TOTAL, D_MODEL), BF16),
            pltpu.VMEM((tm + POOL_HALO, POOL_WIDTH), F32),
            pltpu.SemaphoreType.DMA,
        ],
        compiler_params=pltpu.CompilerParams(dimension_semantics=("arbitrary",), vmem_limit_bytes=VMEM_LIMIT),
    )(dzq, dzkv, dzkr, dgattn, ddc, ddc, dgpool, dgmerge, x, dh, norm_in, w_in_t.reshape(IN_TOTAL, D_MODEL))


def _inproj_bwd_w(order, dz_sh, hn, g_uq, g_ukv, gs, tm):
    s = hn.shape[0]
    n_tiles = s // tm
    hc = D_MODEL // 2
    red = _Reduce(COMM_PARAMS[1:3])
    small = _SmallSum(gs.shape[0])
    n_red = len(red.scratch)

    def body(order_ref, dz_ref, hn_ref, guq_hbm, gukv_hbm, gs_ref, gw_hbm, guq_out, gukv_out, gsum_ref,
             acc, pm_w, a_w, b_w, r_w, w_send, w_recv, w_local, *more_scratch):
        ph, i = pl.program_id(0), pl.program_id(1)
        x, y, c = lax.axis_index("x"), lax.axis_index("y"), lax.axis_index("c")
        k = 2 * x + y
        me, sibling = (x, y, c), (x, y, 1 - c)
        chips = _other_chips(x, y)
        shard_of_phase = [2 * cx + cy for cx, cy in chips] + [k]
        copy = _remote_copier(w_send, w_recv)
        red.bind([guq_hbm, gukv_hbm], [guq_out, gukv_out], more_scratch[:n_red])
        small.bind(gs_ref, gsum_ref, more_scratch[n_red:])
        mine = pl.ds(pl.multiple_of(c * hc, hc), hc)
        theirs = pl.ds(pl.multiple_of((1 - c) * hc, hc), hc)

        def to_sibling(f):
            j = shard_of_phase[f]
            return copy(f, pm_w.at[j, 1 - c], a_w.at[j], sibling)

        def pair_sum(f):
            cx, cy = chips[f]
            return copy(4 + f, pm_w.at[shard_of_phase[f], c], b_w.at[f], (cx, cy, c))

        def finished():
            return copy(7, r_w, gw_hbm.at[:, mine], sibling)

        @pl.when(jnp.logical_and(ph == 0, i == 0))
        def _():
            red.start()
            small.start()

        part = _dot(dz_ref[0], hn_ref[...])

        @pl.when(i == 0)
        def _():
            acc[...] = part

        @pl.when(i > 0)
        def _():
            acc[...] += part

        for f in range(3):
            @pl.when(jnp.logical_and(ph == f + 1, i == 0))
            def _(f=f):
                j = shard_of_phase[f]
                copy(f, a_w.at[j], a_w.at[j], me).wait_recv()
                pm_w[j, c] = (pm_w[j, c].astype(F32) + a_w[j].astype(F32)).astype(BF16)
                pair_sum(f).start()
                if f == 0:
                    red.exchange()

        for f in range(4):
            @pl.when(jnp.logical_and(ph == f, i == n_tiles - 1))
            def _(f=f):
                j = shard_of_phase[f]
                pm_w[j, 0] = acc[:, :hc].astype(BF16)
                pm_w[j, 1] = acc[:, hc:].astype(BF16)
                to_sibling(f).start()
                if f < 3:
                    return
                copy(3, a_w.at[k], a_w.at[k], me).wait_recv()
                r_w[...] = pm_w[k, c].astype(F32) + a_w[k].astype(F32)
                for g in range(3):
                    copy(4 + g, b_w.at[g], b_w.at[g], me).wait_recv()
                    r_w[...] = r_w[...] + b_w[g].astype(F32)
                store = pltpu.make_async_copy(r_w, gw_hbm.at[:, mine], w_local)
                store.start()
                finished().start()
                red.finish()
                small.finish()
                copy(7, gw_hbm.at[:, theirs], gw_hbm.at[:, theirs], me).wait_recv()
                store.wait()
                for g in range(4):
                    to_sibling(g).wait_send()
                for g in range(3):
                    pair_sum(g).wait_send()
                finished().wait_send()

    any_spec = pl.BlockSpec(memory_space=pl.ANY)
    n_sem = 8
    grid_spec = pltpu.PrefetchScalarGridSpec(
        num_scalar_prefetch=1,
        grid=(N_CHIPS, n_tiles),
        in_specs=[
            pl.BlockSpec((1, SHARD_COLS, tm), lambda ph, i, order: (order[ph], 0, i)),
            pl.BlockSpec((tm, D_MODEL), lambda ph, i, order: (i, 0)),
            any_spec, any_spec,
            pl.BlockSpec(small.spec_shape, lambda ph, i, order: (0, 0)),
        ],
        out_specs=[any_spec, any_spec, any_spec, pl.BlockSpec(small.spec_shape, lambda ph, i, order: (0, 0))],
        scratch_shapes=[
            pltpu.VMEM((SHARD_COLS, D_MODEL), F32),
            pltpu.VMEM((N_CHIPS, 2, SHARD_COLS, hc), BF16),
            pltpu.VMEM((N_CHIPS, SHARD_COLS, hc), BF16),
            pltpu.VMEM((3, SHARD_COLS, hc), BF16),
            pltpu.VMEM((SHARD_COLS, hc), F32),
            pltpu.SemaphoreType.DMA((n_sem,)), pltpu.SemaphoreType.DMA((n_sem,)), pltpu.SemaphoreType.DMA,
        ] + red.scratch + small.scratch,
    )
    out = pl.pallas_call(
        body,
        name="inproj_bwd_w",
        grid_spec=grid_spec,
        out_shape=[jax.ShapeDtypeStruct((SHARD_COLS, D_MODEL), F32)] + red.out_shape
        + [small.out_shape],
        compiler_params=pltpu.CompilerParams(dimension_semantics=("arbitrary", "arbitrary"),
                                             vmem_limit_bytes=VMEM_LIMIT),
    )(order, dz_sh, hn, g_uq, g_ukv, gs)
    return out[0], out[1], out[2], out[3]


def _other_chips(x, y):
    return ((1 - x, 1 - y), (1 - x, y), (x, 1 - y))


def _half(ref, axis, size, c, lead=()):
    window = pl.ds(pl.multiple_of(c * size, size), size)
    if axis == 0:
        return ref.at[(*lead, window, slice(None))]
    return ref.at[(*lead, slice(None), window)]


def _half_shape(rows, cols, axis, size):
    return (size, cols) if axis == 0 else (rows, size)


def _remote_copier(send_sems, recv_sems):
    def copy(sem, src, dst, to):
        return pltpu.make_async_remote_copy(src_ref=src, dst_ref=dst, send_sem=send_sems.at[sem],
                                            recv_sem=recv_sems.at[sem], device_id=to, device_id_type=MESH)
    return copy


class _Gather:
    def __init__(self, params):
        self.params = params
        n = len(params)
        self.scratch = [pltpu.SemaphoreType.DMA((6 * n,)), pltpu.SemaphoreType.DMA((6 * n,)),
                        pltpu.SemaphoreType.DMA((n,))]
        self.out_shape = [jax.ShapeDtypeStruct((N_CHIPS, r, cc), BF16) for _, r, cc, _, _ in params]

    def bind(self, ins, outs, scratch):
        self.ins, self.outs = ins, outs
        send_sems, recv_sems, self.local_sems = scratch
        self.copy = _remote_copier(send_sems, recv_sems)
        self.x, self.y, self.c = lax.axis_index("x"), lax.axis_index("y"), lax.axis_index("c")
        self.k = 2 * self.x + self.y
        self.chips = _other_chips(self.x, self.y)

    def _local(self, p):
        return pltpu.make_async_copy(self.ins[p], self.outs[p].at[self.k], self.local_sems.at[p])

    def _first(self, p, j):
        _, _, _, axis, size = self.params[p]
        cx, cy = self.chips[j]
        return self.copy(6 * p + j, _half(self.ins[p], axis, size, self.c),
                         _half(self.outs[p], axis, size, self.c, (self.k,)), (cx, cy, self.c))

    def _relay(self, p, j, half_of):
        _, _, _, axis, size = self.params[p]
        cx, cy = self.chips[j]
        block = _half(self.outs[p], axis, size, half_of, (2 * cx + cy,))
        return self.copy(6 * p + 3 + j, block, block, (self.x, self.y, 1 - self.c))

    def start(self):
        for p in range(len(self.params)):
            self._local(p).start()
            for j in (1, 2, 0):
                self._first(p, j).start()

    def relay_one(self, p, j):
        _, _, _, axis, size = self.params[p]
        cx, cy = self.chips[j]
        landed = _half(self.outs[p], axis, size, self.c, (2 * cx + cy,))
        self.copy(6 * p + j, landed, landed, (self.x, self.y, self.c)).wait_recv()
        self._relay(p, j, self.c).start()

    def await_one(self, p, j):
        self._relay(p, j, 1 - self.c).wait_recv()

    def wait_sends(self):
        for p in range(len(self.params)):
            for j in range(3):
                self._first(p, j).wait_send()
                self._relay(p, j, self.c).wait_send()
            self._local(p).wait()

    def relay(self):
        for j in range(3):
            for p in range(len(self.params)):
                self.relay_one(p, j)

    def finish(self):
        for j in range(3):
            for p in range(len(self.params)):
                self.await_one(p, j)
        self.wait_sends()


class _Reduce:
    def __init__(self, params):
        self.params = params
        n = len(params)
        halves = [_half_shape(r, cc, axis, size) for _, r, cc, axis, size in params]
        self.scratch = ([pltpu.VMEM((N_CHIPS, *h), BF16) for h in halves]
                        + [pltpu.VMEM((N_CHIPS, *h), BF16) for h in halves]
                        + [pltpu.VMEM((3, *h), BF16) for h in halves]
                        + [pltpu.VMEM(h, F32) for h in halves]
                        + [pltpu.SemaphoreType.DMA((5 * n,)), pltpu.SemaphoreType.DMA((5 * n,)),
                           pltpu.SemaphoreType.DMA((2 * n,))])
        self.out_shape = [jax.ShapeDtypeStruct((r, cc), F32) for _, r, cc, _, _ in params]

    def bind(self, g_in, g_out, scratch):
        n = len(self.params)
        self.g_in, self.g_out = g_in, g_out
        self.pm, self.a_buf = scratch[0:n], scratch[n:2 * n]
        self.b_buf, self.r_buf = scratch[2 * n:3 * n], scratch[3 * n:4 * n]
        send_sems, recv_sems, self.local_sems = scratch[4 * n:]
        self.copy = _remote_copier(send_sems, recv_sems)
        self.x, self.y, self.c = lax.axis_index("x"), lax.axis_index("y"), lax.axis_index("c")
        self.k = 2 * self.x + self.y
        self.chips = _other_chips(self.x, self.y)
        self.me = (self.x, self.y, self.c)
        self.sibling = (self.x, self.y, 1 - self.c)

    def _load(self, p):
        _, _, _, axis, size = self.params[p]
        return pltpu.make_async_copy(_half(self.g_in[p], axis, size, self.c, (slice(None),)), self.pm[p],
                                     self.local_sems.at[p])

    def _to_sibling(self, p):
        _, _, _, axis, size = self.params[p]
        return self.copy(5 * p, _half(self.g_in[p], axis, size, 1 - self.c, (slice(None),)), self.a_buf[p],
                         self.sibling)

    def _pair_sum(self, p, j):
        cx, cy = self.chips[j]
        return self.copy(5 * p + 1 + j, self.pm[p].at[2 * cx + cy], self.b_buf[p].at[j], (cx, cy, self.c))

    def _store(self, p):
        _, _, _, axis, size = self.params[p]
        n = len(self.params)
        return pltpu.make_async_copy(self.r_buf[p], _half(self.g_out[p], axis, size, self.c),
                                     self.local_sems.at[n + p])

    def _finished(self, p):
        _, _, _, axis, size = self.params[p]
        return self.copy(5 * p + 4, self.r_buf[p], _half(self.g_out[p], axis, size, self.c), self.sibling)

    def start(self):
        for p in range(len(self.params)):
            self._load(p).start()
            self._to_sibling(p).start()

    def exchange(self):
        for p in range(len(self.params)):
            self._load(p).wait()
            self.copy(5 * p, self.a_buf[p], self.a_buf[p], self.me).wait_recv()
            for j, (cx, cy) in enumerate(self.chips):
                kj = 2 * cx + cy
                self.pm[p][kj] = (self.pm[p][kj].astype(F32) + self.a_buf[p][kj].astype(F32)).astype(BF16)
                self._pair_sum(p, j).start()
            self.r_buf[p][...] = self.pm[p][self.k].astype(F32) + self.a_buf[p][self.k].astype(F32)

    def finish(self):
        for p, (_, _, _, axis, size) in enumerate(self.params):
            for j in range(3):
                self.copy(5 * p + 1 + j, self.b_buf[p].at[j], self.b_buf[p].at[j], self.me).wait_recv()
                self.r_buf[p][...] = self.r_buf[p][...] + self.b_buf[p][j].astype(F32)
            self._store(p).start()
            self._finished(p).start()
        for p, (_, _, _, axis, size) in enumerate(self.params):
            theirs = _half(self.g_out[p], axis, size, 1 - self.c)
            self.copy(5 * p + 4, theirs, theirs, self.me).wait_recv()
            self._store(p).wait()
            self._to_sibling(p).wait_send()
            for j in range(3):
                self._pair_sum(p, j).wait_send()
            self._finished(p).wait_send()


class _SmallSum:
    def __init__(self, rows):
        self.rows = rows
        self.scratch = [pltpu.VMEM((N_DEV, rows, LANES), F32),
                        pltpu.SemaphoreType.DMA((N_DEV - 1,)), pltpu.SemaphoreType.DMA((N_DEV - 1,))]
        self.out_shape = jax.ShapeDtypeStruct((rows, LANES), F32)
        self.spec_shape = (rows, LANES)

    def bind(self, src, dst, scratch):
        self.src, self.dst = src, dst
        self.buf, send_sems, recv_sems = scratch
        self.copy = _remote_copier(send_sems, recv_sems)
        self.x, self.y, self.c = lax.axis_index("x"), lax.axis_index("y"), lax.axis_index("c")

    def _send(self, f):
        fx, fy, fc = [(a, b, d) for a in (0, 1) for b in (0, 1) for d in (0, 1)][f]
        x, y, c = self.x, self.y, self.c
        peer = (1 - x if fx else x, 1 - y if fy else y, 1 - c if fc else c)
        return self.copy(f - 1, self.src, self.buf.at[f], peer)

    def start(self):
        for f in range(1, N_DEV):
            self._send(f).start()
        self.buf[0] = self.src[...]

    def finish(self):
        me = (self.x, self.y, self.c)
        for f in range(1, N_DEV):
            self.copy(f - 1, self.buf.at[f], self.buf.at[f], me).wait_recv()
        dev = 4 * self.x + 2 * self.y + self.c
        total = self.buf[dev]
        for d in range(1, N_DEV):
            total = total + self.buf[jnp.bitwise_xor(dev, d)]
        self.dst[...] = total
        for f in range(1, N_DEV):
            self._send(f).wait_send()


def _adamw_math(w, g, m, v):
    m = ADAM_B1 * m + (1.0 - ADAM_B1) * g
    v = ADAM_B2 * v + (1.0 - ADAM_B2) * (g * g)
    m_hat = m / (1.0 - ADAM_B1 ** ADAM_STEP)
    v_hat = v / (1.0 - ADAM_B2 ** ADAM_STEP)
    delta = -ADAM_LR * (m_hat / (jnp.sqrt(v_hat) + ADAM_EPS) + ADAM_WD * w)
    return delta, m, v


def _adamw_tiled(w, g, m, v, tm):
    rows, cols = w.shape

    def body(w_ref, g_ref, m_ref, v_ref, d_ref, nm_ref, nv_ref, g_out):
        g = g_ref[...]
        d_ref[...], nm_ref[...], nv_ref[...] = _adamw_math(w_ref[...], g, m_ref[...], v_ref[...])
        g_out[...] = g

    spec = _row_spec(tm, cols)
    return pl.pallas_call(
        body,
        name="adamw_w_in",
        grid=(rows // tm,),
        in_specs=[spec] * 4,
        out_specs=[spec] * 4,
        out_shape=[jax.ShapeDtypeStruct(w.shape, F32)] * 4,
        compiler_params=pltpu.CompilerParams(dimension_semantics=("parallel",), vmem_limit_bytes=VMEM_LIMIT),
    )(w, g, m, v)


def _adamw_many(ws, gs, ms, vs):
    n = len(ws)

    def body(*refs):
        ins, outs = refs[:4 * n], refs[4 * n:]
        for i in range(n):
            g = ins[n + i][...]
            d, nm, nv = _adamw_math(ins[i][...], g, ins[2 * n + i][...], ins[3 * n + i][...])
            outs[i][...] = d
            outs[n + i][...] = nm
            outs[2 * n + i][...] = nv
            outs[3 * n + i][...] = g

    vmem_spec = pl.BlockSpec(memory_space=pltpu.VMEM)
    shapes = [jax.ShapeDtypeStruct(w.shape, F32) for w in ws]
    out = pl.pallas_call(
        body,
        name="adamw_small",
        in_specs=[vmem_spec] * (4 * n),
        out_specs=[vmem_spec] * (4 * n),
        out_shape=shapes * 4,
        compiler_params=pltpu.CompilerParams(vmem_limit_bytes=VMEM_LIMIT),
    )(*ws, *gs, *ms, *vs)
    return out[:n], out[n:2 * n], out[2 * n:3 * n], out[3 * n:]


def _pack_rows(parts, rows, dtype):
    flat = jnp.concatenate([p.reshape(-1).astype(dtype) for p in parts])
    flat = jnp.concatenate([flat, jnp.zeros((rows * LANES - flat.shape[0],), dtype)])
    return flat.reshape(rows, LANES)


def _unpack_rows(packed, shapes):
    flat = packed.reshape(-1)
    out, off = [], 0
    for _, shp in shapes:
        n = int(np.prod(shp))
        out.append(flat[off:off + n].reshape(shp))
        off += n
    return out


def _rope_tables(s):
    half = QK_ROPE_DIM // 2
    inv_freq = np.float32(ROPE_THETA) ** (-np.arange(half, dtype=np.float32) / np.float32(half))
    ang = (np.arange(s, dtype=np.float32)[:, None] * inv_freq[None, :]).astype(np.float32)
    cos, sin = np.cos(ang.astype(np.float64)).astype(np.float32), np.sin(ang.astype(np.float64)).astype(np.float32)
    z16 = np.zeros((s, half), np.float32)
    z32 = np.zeros((s, HEAD_PAD - QK_NOPE_DIM - QK_ROPE_DIM), np.float32)
    z64 = np.zeros((s, QK_NOPE_DIM), np.float32)
    rc = np.concatenate([np.ones((s, QK_NOPE_DIM), np.float32), cos, cos, z32], axis=1)
    rsa = np.concatenate([z64, -sin, z16, z32], axis=1)
    rsb = np.concatenate([z64, z16, sin, z32], axis=1)
    return jnp.asarray(rc), jnp.asarray(rsa), jnp.asarray(rsb)


def kernel(x, norm_in, w_in, q_norm, w_uq, kv_norm, w_ukv, pool_w, pool_scale, w_branch_attn, w_branch_pool, w_out, norm_final, loss_target, m_norm_in, m_w_in, m_q_norm, m_w_uq, m_kv_norm, m_w_ukv, m_pool_w, m_pool_scale, m_w_branch_attn, m_w_branch_pool, m_w_out, m_norm_final, v_norm_in, v_w_in, v_q_norm, v_w_uq, v_kv_norm, v_w_ukv, v_pool_w, v_pool_scale, v_w_branch_attn, v_w_branch_pool, v_w_out, v_norm_final):
    s = x.shape[1]
    t_att, t_row = _tiles(s)
    x2 = x.reshape(s, D_MODEL)
    tgt = loss_target.reshape(s, D_MODEL)

    local = [w_in.T, w_uq.reshape(96, 768), w_ukv.reshape(64, 1024), w_branch_attn, w_branch_pool, w_out]
    local = [a.astype(BF16) for a in local]
    cx, cy = lax.axis_index("x"), lax.axis_index("y")
    others = [2 * ox + oy for ox, oy in _other_chips(cx, cy)]
    hn, z_sh, (w_in_t, w_uq_all, w_ukv_all) = _inproj_fwd(
        jnp.stack([2 * cx + cy, others[1], others[2], others[0]]).astype(jnp.int32), x2, norm_in.reshape(1, -1),
        local[:3], 4 * t_row)
    w_uq_f = w_uq_all.reshape(Q_LORA_RANK, MLA_HEADS, QK_NOPE_DIM + QK_ROPE_DIM)
    w_ukv_f = w_ukv_all.reshape(KV_LORA_RANK, MLA_HEADS, QK_NOPE_DIM + V_HEAD_DIM)
    hw = MLA_HEADS * HEAD_PAD
    wuq_p = jnp.pad(w_uq_f, ((0, 0), (0, 0), (0, HEAD_PAD - QK_NOPE_DIM - QK_ROPE_DIM))).reshape(Q_LORA_RANK, hw)
    wk_p = jnp.pad(w_ukv_f[:, :, :QK_NOPE_DIM], ((0, 0), (0, 0), (0, HEAD_PAD - QK_NOPE_DIM))).reshape(KV_LORA_RANK, hw)
    wv = w_ukv_f[:, :, QK_NOPE_DIM:].reshape(KV_LORA_RANK, MLA_WIDTH)
    rc, rsa, rsb = _rope_tables(s)
    g_in = norm_in.reshape(1, -1)
    g_q = q_norm.reshape(1, -1)
    g_kv = kv_norm.reshape(1, -1)
    g_f = norm_final.reshape(1, -1)
    ps = pool_scale.reshape(1, -1)
    pw_bf = pool_w.astype(BF16)

    q, k, v, q_t, v_t = _qkv_fwd(z_sh, g_q, g_kv, wuq_p, wk_p, wv, rc, rsa, rsb, 2 * t_row)
    o, lse, (w_ba_all, w_bp_all, w_out_all) = _attn_fwd(q_t, k, v_t, local[3:], t_att)
    w_out_f = w_out_all.reshape(D_MODEL, D_MODEL)

    (do, delta, dgattn, dgpool, dgmerge, ddc, dh, sq_err, d_w_out, d_w_ba, d_w_bp, d_pool_w, d_pool_scale,
     d_norm_final) = _mid(o, z_sh, x2, tgt, pw_bf, ps, w_ba_all, w_bp_all, w_out_f, g_f, t_row)

    late_grads = [d_w_ba, d_w_bp, d_w_out.reshape(N_CHIPS, 256, D_MODEL)]
    small_mid = dict(pool_scale=d_pool_scale, norm_final=d_norm_final, pool_w=d_pool_w, sq_err=sq_err)
    gs_mid = _pack_rows([small_mid[n] for n, _ in SMALL_MID], _small_rows(SMALL_MID), F32)
    dq, dk_t, dv_t, (g_w_ba, g_w_bp, g_w_out), g_small_mid = _attn_bwd(q, q_t, k, v, do, lse, delta, late_grads,
                                                                      gs_mid, t_att)
    g_pool_scale, g_norm_final, g_pool_w, sq_err_all = _unpack_rows(g_small_mid, SMALL_MID)
    dzq, dzkv, dzkr, d_wuq_p, d_wk_p, d_wv, d_q_norm, d_kv_norm = _qkv_bwd(
        dq, dk_t, dv_t, z_sh, g_q, g_kv, wuq_p, wk_p, wv, rc, rsa, rsb, 2 * t_row)
    grad_x, d_norm_in, dz_sh = _inproj_bwd_x(dzq, dzkv, dzkr, dgattn, ddc, dgpool, dgmerge, x2, dh, g_in, w_in_t,
                                             2 * t_row)

    d_w_uq = d_wuq_p.reshape(Q_LORA_RANK, MLA_HEADS, HEAD_PAD)[:, :, :QK_NOPE_DIM + QK_ROPE_DIM]
    d_w_ukv = jnp.concatenate([d_wk_p.reshape(KV_LORA_RANK, MLA_HEADS, HEAD_PAD)[:, :, :QK_NOPE_DIM],
                               d_wv.reshape(KV_LORA_RANK, MLA_HEADS, V_HEAD_DIM)], axis=2)
    small_late = dict(norm_in=d_norm_in, q_norm=d_q_norm, kv_norm=d_kv_norm)
    gs = _pack_rows([small_late[n] for n, _ in SMALL_LATE], _small_rows(SMALL_LATE), F32)
    order = jnp.stack(others + [2 * cx + cy]).astype(jnp.int32)
    g_w_in_t, g_w_uq, g_w_ukv, g_small = _inproj_bwd_w(
        order, dz_sh, hn, d_w_uq.reshape(N_CHIPS, 96, 768).astype(BF16),
        d_w_ukv.reshape(N_CHIPS, 64, 1024).astype(BF16), gs, 4 * t_row)
    g_norm_in, g_q_norm, g_kv_norm = _unpack_rows(g_small, SMALL_LATE)
    g_w_uq = g_w_uq.reshape(w_uq.shape)
    g_w_ukv = g_w_ukv.reshape(w_ukv.shape)

    dl_w_in, nm_w_in, nv_w_in, g_w_in = (a.T for a in _adamw_tiled(w_in.T, g_w_in_t, m_w_in.T, v_w_in.T, 152))

    def two_d(a):
        return a.reshape(1, -1) if a.ndim == 1 else a

    names = ["norm_in", "q_norm", "w_uq", "kv_norm", "w_ukv", "pool_w", "pool_scale", "w_branch_attn",
             "w_branch_pool", "w_out", "norm_final"]
    ws = dict(norm_in=norm_in, q_norm=q_norm, w_uq=w_uq, kv_norm=kv_norm, w_ukv=w_ukv, pool_w=pool_w,
              pool_scale=pool_scale, w_branch_attn=w_branch_attn, w_branch_pool=w_branch_pool, w_out=w_out,
              norm_final=norm_final)
    gsd = dict(norm_in=g_norm_in, q_norm=g_q_norm, w_uq=g_w_uq, kv_norm=g_kv_norm, w_ukv=g_w_ukv, pool_w=g_pool_w,
               pool_scale=g_pool_scale, w_branch_attn=g_w_ba, w_branch_pool=g_w_bp, w_out=g_w_out,
               norm_final=g_norm_final)
    msd = dict(norm_in=m_norm_in, q_norm=m_q_norm, w_uq=m_w_uq, kv_norm=m_kv_norm, w_ukv=m_w_ukv, pool_w=m_pool_w,
               pool_scale=m_pool_scale, w_branch_attn=m_w_branch_attn, w_branch_pool=m_w_branch_pool, w_out=m_w_out,
               norm_final=m_norm_final)
    vsd = dict(norm_in=v_norm_in, q_norm=v_q_norm, w_uq=v_w_uq, kv_norm=v_kv_norm, w_ukv=v_w_ukv, pool_w=v_pool_w,
               pool_scale=v_pool_scale, w_branch_attn=v_w_branch_attn, w_branch_pool=v_w_branch_pool, w_out=v_w_out,
               norm_final=v_norm_final)
    dls, nms, nvs, g_outs = _adamw_many([two_d(ws[n]) for n in names], [two_d(gsd[n]) for n in names],
                                [two_d(msd[n]) for n in names], [two_d(vsd[n]) for n in names])

    grads = dict(zip(names, g_outs))
    grads["w_in"] = g_w_in
    delta_w = {n: d.reshape(ws[n].shape) for n, d in zip(names, dls)}
    new_m = {n: d.reshape(ws[n].shape) for n, d in zip(names, nms)}
    new_v = {n: d.reshape(ws[n].shape) for n, d in zip(names, nvs)}
    delta_w["w_in"], new_m["w_in"], new_v["w_in"] = dl_w_in, nm_w_in, nv_w_in
    ws["w_in"] = w_in

    order = ["norm_in", "w_in", "q_norm", "w_uq", "kv_norm", "w_ukv", "pool_w", "pool_scale", "w_branch_attn",
             "w_branch_pool", "w_out", "norm_final"]
    loss = 0.5 * jnp.sum(sq_err_all) / D_MODEL
    return (loss, grad_x.reshape(x.shape),
            *[grads[n].reshape(ws[n].shape) for n in order],
            *[delta_w[n] for n in order], *[new_m[n] for n in order], *[new_v[n] for n in order])
```

```python
import functools

import jax
import jax.numpy as jnp
import numpy as np
from jax import lax
from jax.experimental import pallas as pl
from jax.experimental.pallas import tpu as pltpu

F32 = jnp.float32
BF16 = jnp.bfloat16
MESH = pl.DeviceIdType.MESH

D_MODEL = 1024
CHUNK = 64
MLA_HEADS = 8
QK_NOPE_DIM = 64
QK_ROPE_DIM = 32
V_HEAD_DIM = 64
Q_LORA_RANK = 384
KV_LORA_RANK = 256
MLA_WIDTH = MLA_HEADS * V_HEAD_DIM
ROPE_THETA = 10000.0
POOL_WINDOWS = (2, 4, 8, 16)
POOL_WIDTH = 512
POOL_GROUP_DIM = 128
BRANCH_COLS = D_MODEL // 4
FWD_HEADS = 8
BWD_HEADS = 4
POOL_HALO = 16
EPS = 1e-6
IN_TOTAL = 4256
HEAD_PAD = 128
ATT_SCALE = (QK_NOPE_DIM + QK_ROPE_DIM) ** -0.5
ATT_SCALE_LOG2E = ATT_SCALE * 1.4426950408889634

ADAM_LR = 0.001
ADAM_B1 = 0.9
ADAM_B2 = 0.999
ADAM_EPS = 1e-08
ADAM_WD = 0.01
ADAM_STEP = 10

N_CHIPS = 4
N_DEV = 8
LANES = 128
VMEM_LIMIT = 60 * 1024 * 1024

IN_SEGMENTS = ((384, 384), (256, 256), (32, HEAD_PAD), (512, 512), (512, 512), (512, 512), (2048, 2048))
SHARD_COLS = IN_TOTAL // N_CHIPS
ZQ_COLS = slice(0, 384)
ZKV_COLS = slice(384, 640)
ZKR_TILE = slice(640, 768)


def _shard_pieces():
    bounds, off = [], 0
    for w, _ in IN_SEGMENTS:
        bounds.append((off, off + w))
        off += w
    out = []
    for j in range(N_CHIPS):
        lo, hi = SHARD_COLS * j, SHARD_COLS * (j + 1)
        out.append([(i, max(lo, a) - a, min(hi, b) - a, max(lo, a) - lo)
                    for i, (a, b) in enumerate(bounds) if max(lo, a) < min(hi, b)])
    return out


SHARD_PIECES = _shard_pieces()


def _segment(z_blocks, seg):
    parts = [z_blocks[j][:, col:col + hi - lo]
             for j, pieces in enumerate(SHARD_PIECES) for sg, lo, hi, col in pieces if sg == seg]
    return parts[0] if len(parts) == 1 else jnp.concatenate(parts, axis=1)

COMM_PARAMS = (
    ("w_in", SHARD_COLS, D_MODEL, 1, 512),
    ("w_uq", 96, 768, 0, 48),
    ("w_ukv", 64, 1024, 0, 32),
    ("w_branch_attn", 512, 256, 0, 256),
    ("w_branch_pool", 512, 256, 0, 256),
    ("w_out", 256, 1024, 0, 128),
)

SMALL_MID = (
    ("pool_scale", (512,)),
    ("norm_final", (1024,)),
    ("pool_w", (4, 128, 128)),
    ("sq_err", (8, 128)),
)
SMALL_LATE = (
    ("norm_in", (1024,)),
    ("q_norm", (384,)),
    ("kv_norm", (256,)),
)


def _small_rows(shapes):
    return -(-sum(int(np.prod(s)) for _, s in shapes) // (LANES * 8)) * 8


def _first_rows(shapes):
    out, off = {}, 0
    for name, shp in shapes:
        out[name], rem = divmod(off, LANES)
        assert rem == 0, name
        off += int(np.prod(shp))
    return out


def _dot(a, b):
    return jnp.dot(a, b, preferred_element_type=F32)


def _dot_nt(a, b):
    return lax.dot_general(a, b, (((1,), (1,)), ((), ())), preferred_element_type=F32)


def _dot_tn(a, b):
    return lax.dot_general(a, b, (((0,), (0,)), ((), ())), preferred_element_type=F32)


def _sigmoid(x):
    return 1.0 / (1.0 + jnp.exp(-x))


def _colsum(x):
    return jnp.sum(x, axis=0, keepdims=True)


def _rms_fwd(x, g):
    r = lax.rsqrt(jnp.mean(x * x, axis=-1, keepdims=True) + EPS)
    xhat = x * r
    return xhat * g, xhat, r


def _rms_bwd(dy, xhat, r, g):
    dxhat = dy * g
    return r * (dxhat - xhat * jnp.mean(dxhat * xhat, axis=-1, keepdims=True))


def _rope(v, c, sa, sb):
    return v * c + pltpu.roll(v, 112, 1) * sa + pltpu.roll(v, 16, 1) * sb


def _unrope(d, c, sa, sb):
    return d * c + pltpu.roll(d * sa, 16, 1) + pltpu.roll(d * sb, 112, 1)


def _row_spec(tm, n):
    return pl.BlockSpec((tm, n), lambda i: (i, 0))


def _full_spec(shape):
    nd = len(shape)
    return pl.BlockSpec(shape, lambda i: (0,) * nd)


def _tiles(s):
    t_att = 512 if s >= 2048 else 128
    t_row = 256 if s >= 1024 else 128
    return t_att, t_row


def _inproj_fwd(order, x, norm_in, early_shards, tm):
    s = x.shape[0]
    n_tiles = s // tm
    gat = _Gather(COMM_PARAMS[:3])
    n_w = len(gat.params)
    arrival = (1, 2, 0)

    def body(order_ref, x_ref, g_ref, *rest):
        w_loc, (hn_ref, z_ref), w_all = rest[:n_w], rest[n_w:n_w + 2], rest[n_w + 2:2 * n_w + 2]
        w_vmem, hn_all, w_sem = rest[2 * n_w + 2:2 * n_w + 5]
        gat.bind(w_loc, w_all, rest[2 * n_w + 5:])
        ph, i = pl.program_id(0), pl.program_id(1)
        pl.when(jnp.logical_and(ph == 0, i == 0))(gat.start)

        def fetch(phase):
            src = w_loc[0] if phase == 0 else w_all[0].at[order_ref[phase]]
            return pltpu.make_async_copy(src, w_vmem.at[phase % 2], w_sem.at[phase % 2])

        def landed(f):
            gat.relay_one(0, arrival[f])
            gat.await_one(0, arrival[f])

        @pl.when(jnp.logical_and(ph == 0, i == 0))
        def _():
            fetch(0).start()
            fetch(0).wait()

        @pl.when(jnp.logical_and(ph == 1, i == 0))
        def _():
            landed(0)
            fetch(1).start()
            fetch(1).wait()

        for f in (1, 2):
            @pl.when(jnp.logical_and(ph == f, i == n_tiles - 1))
            def _(f=f):
                landed(f)
                fetch(f + 1).start()

            @pl.when(jnp.logical_and(ph == f + 1, i == 0))
            def _(f=f):
                fetch(f + 1).wait()

        rows = pl.ds(pl.multiple_of(i * tm, tm), tm)

        @pl.when(ph == 0)
        def _():
            hn, _, _ = _rms_fwd(x_ref[...], g_ref[...])
            hn = hn.astype(BF16)
            hn_ref[...] = hn
            hn_all[rows, :] = hn

        z_ref[0] = _dot_nt(hn_all[rows, :], w_vmem[ph % 2])

        @pl.when(jnp.logical_and(ph == N_CHIPS - 1, i == n_tiles - 1))
        def _():
            for p in range(1, n_w):
                for j in range(3):
                    gat.relay_one(p, j)
            for p in range(1, n_w):
                for j in range(3):
                    gat.await_one(p, j)
            gat.wait_sends()

    def tile_in_phase0(ph, i, order):
        return (jnp.where(ph == 0, i, n_tiles - 1), 0)

    any_spec = pl.BlockSpec(memory_space=pl.ANY)
    grid_spec = pltpu.PrefetchScalarGridSpec(
        num_scalar_prefetch=1,
        grid=(N_CHIPS, n_tiles),
        in_specs=[pl.BlockSpec((tm, D_MODEL), tile_in_phase0),
                  pl.BlockSpec((1, D_MODEL), lambda ph, i, order: (0, 0))] + [any_spec] * n_w,
        out_specs=[pl.BlockSpec((tm, D_MODEL), tile_in_phase0),
                   pl.BlockSpec((1, tm, SHARD_COLS), lambda ph, i, order: (order[ph], i, 0))] + [any_spec] * n_w,
        scratch_shapes=[pltpu.VMEM((2, SHARD_COLS, D_MODEL), BF16), pltpu.VMEM((s, D_MODEL), BF16),
                        pltpu.SemaphoreType.DMA((2,))] + gat.scratch,
    )
    out = pl.pallas_call(
        body,
        name="inproj_fwd",
        grid_spec=grid_spec,
        out_shape=[jax.ShapeDtypeStruct((s, D_MODEL), BF16), jax.ShapeDtypeStruct((N_CHIPS, s, SHARD_COLS), F32)]
        + gat.out_shape,
        compiler_params=pltpu.CompilerParams(dimension_semantics=("arbitrary", "arbitrary"),
                                             vmem_limit_bytes=VMEM_LIMIT),
    )(order, x, norm_in, *early_shards)
    return out[0], out[1], out[2:]


def _qkv_fwd(z_sh, q_norm, kv_norm, wuq_p, wk_p, wv, rc, rsa, rsb, tm):
    s = z_sh.shape[1]
    hw = MLA_HEADS * HEAD_PAD

    def body(z_ref, gq_ref, gkv_ref, wuq_ref, wk_ref, wv_ref, c_ref, sa_ref, sb_ref,
             q_ref, k_ref, v_ref, qt_ref, vt_ref):
        c, sa, sb = c_ref[...], sa_ref[...], sb_ref[...]
        z0 = z_ref[0]
        cq, _, _ = _rms_fwd(z0[:, ZQ_COLS], gq_ref[...])
        qf = _dot(cq.astype(BF16), wuq_ref[...])
        ckv, _, _ = _rms_fwd(z0[:, ZKV_COLS], gkv_ref[...])
        ckv = ckv.astype(BF16)
        kn = _dot(ckv, wk_ref[...])
        lane = lax.broadcasted_iota(jnp.int32, (tm, HEAD_PAD), 1)
        zkr = jnp.where(lane < QK_ROPE_DIM, z0[:, ZKR_TILE], 0.0)
        kr = _rope(pltpu.roll(zkr, 64, 1), c, sa, sb)
        for h in range(MLA_HEADS):
            cols = slice(h * HEAD_PAD, (h + 1) * HEAD_PAD)
            qh = _rope(qf[:, cols], c, sa, sb)
            q_ref[:, cols] = qh.astype(BF16)
            qt_ref[cols, :] = qh.T.astype(BF16)
            k_ref[:, cols] = (kn[:, cols] + kr).astype(BF16)
        vf = _dot(ckv, wv_ref[...])
        v_ref[...] = vf.astype(BF16)
        vt_ref[...] = vf.T.astype(BF16)

    return pl.pallas_call(
        body,
        name="qkv_fwd",
        grid=(s // tm,),
        in_specs=[
            pl.BlockSpec((1, tm, SHARD_COLS), lambda i: (0, i, 0)),
            _full_spec((1, Q_LORA_RANK)), _full_spec((1, KV_LORA_RANK)),
            _full_spec((Q_LORA_RANK, hw)), _full_spec((KV_LORA_RANK, hw)), _full_spec((KV_LORA_RANK, MLA_WIDTH)),
            _row_spec(tm, HEAD_PAD), _row_spec(tm, HEAD_PAD), _row_spec(tm, HEAD_PAD),
        ],
        out_specs=[_row_spec(tm, hw), _row_spec(tm, hw), _row_spec(tm, MLA_WIDTH),
                   pl.BlockSpec((hw, tm), lambda i: (0, i)), pl.BlockSpec((MLA_WIDTH, tm), lambda i: (0, i))],
        out_shape=[jax.ShapeDtypeStruct((s, hw), BF16), jax.ShapeDtypeStruct((s, hw), BF16),
                   jax.ShapeDtypeStruct((s, MLA_WIDTH), BF16),
                   jax.ShapeDtypeStruct((hw, s), BF16), jax.ShapeDtypeStruct((MLA_WIDTH, s), BF16)],
        compiler_params=pltpu.CompilerParams(dimension_semantics=("parallel",), vmem_limit_bytes=VMEM_LIMIT),
    )(z_sh, q_norm, kv_norm, wuq_p, wk_p, wv, rc, rsa, rsb)


def _chunk_mask(t, keys_on_rows):
    rows = lax.broadcasted_iota(jnp.int32, (t, t), 0) // CHUNK
    cols = lax.broadcasted_iota(jnp.int32, (t, t), 1) // CHUNK
    return rows <= cols if keys_on_rows else cols <= rows


def _attn_fwd(q_t, k, v_t, late_shards, t):
    s = k.shape[0]
    groups = MLA_HEADS // FWD_HEADS
    n_q = s // t
    gat = _Gather(COMM_PARAMS[3:])
    n_w = len(gat.params)

    def body(qt_ref, k_ref, k2_ref, vt_ref, *rest):
        w_in, (o_ref, lse_ref), w_out = rest[:n_w], rest[n_w:n_w + 2], rest[n_w + 2:2 * n_w + 2]
        gat.bind(w_in, w_out, rest[2 * n_w + 2:])
        i = pl.program_id(1)
        step_no = pl.program_id(0) * n_q + i
        pl.when(step_no == 0)(gat.start)
        pl.when(step_no == groups * n_q // 2)(gat.relay)
        mask = _chunk_mask(t, True)
        qcs = [slice(hh * HEAD_PAD, (hh + 1) * HEAD_PAD) for hh in range(FWD_HEADS)]
        vcs = [slice(hh * V_HEAD_DIM, (hh + 1) * V_HEAD_DIM) for hh in range(FWD_HEADS)]
        qts = [qt_ref[qc, :] for qc in qcs]

        def step(j, carry, masked):
            keys = pl.ds(pl.multiple_of(j * t, t), t)
            out = []
            for hh in range(FWD_HEADS):
                m, l, acc = carry[hh]
                sc = _dot(k_ref[keys, qcs[hh]], qts[hh])
                if masked:
                    sc = jnp.where(mask, sc, -jnp.inf)
                m_new = jnp.maximum(m, jnp.max(sc, axis=0, keepdims=True))
                alpha = jnp.exp2((m - m_new) * ATT_SCALE_LOG2E)
                p = jnp.exp2((_dot(k2_ref[keys, qcs[hh]], qts[hh]) - m_new) * ATT_SCALE_LOG2E)
                if masked:
                    p = jnp.where(mask, p, 0.0)
                l = alpha * l + jnp.sum(p, axis=0, keepdims=True)
                acc = alpha * acc + _dot(vt_ref[vcs[hh], keys], p.astype(BF16))
                out.append((m_new, l, acc))
            return tuple(out)

        one = (jnp.full((1, t), -jnp.inf, F32), jnp.zeros((1, t), F32), jnp.zeros((V_HEAD_DIM, t), F32))
        carry = lax.fori_loop(0, i, functools.partial(step, masked=False), (one,) * FWD_HEADS)
        carry = step(i, carry, True)
        o_ref[...] = jnp.concatenate([carry[hh][2] / carry[hh][1] for hh in range(FWD_HEADS)], axis=0).T
        for hh in range(FWD_HEADS):
            m, l, _ = carry[hh]
            lse_ref[:, qcs[hh]] = jnp.broadcast_to(m * ATT_SCALE_LOG2E + jnp.log2(l), (HEAD_PAD, t)).T
        pl.when(step_no == groups * n_q - 1)(gat.finish)

    any_spec = pl.BlockSpec(memory_space=pl.ANY)
    out = pl.pallas_call(
        body,
        name="attn_fwd",
        grid=(groups, n_q),
        in_specs=[
            pl.BlockSpec((FWD_HEADS * HEAD_PAD, t), lambda p, i: (p, i)),
            pl.BlockSpec((s, FWD_HEADS * HEAD_PAD), lambda p, i: (0, p), pipeline_mode=pl.Buffered(1)),
            pl.BlockSpec((s, FWD_HEADS * HEAD_PAD), lambda p, i: (0, p), pipeline_mode=pl.Buffered(1)),
            pl.BlockSpec((FWD_HEADS * V_HEAD_DIM, s), lambda p, i: (p, 0), pipeline_mode=pl.Buffered(1)),
        ] + [any_spec] * n_w,
        out_specs=[
            pl.BlockSpec((t, FWD_HEADS * V_HEAD_DIM), lambda p, i: (i, p)),
            pl.BlockSpec((t, FWD_HEADS * HEAD_PAD), lambda p, i: (i, p)),
        ] + [any_spec] * n_w,
        out_shape=[jax.ShapeDtypeStruct((s, MLA_WIDTH), F32), jax.ShapeDtypeStruct((s, MLA_HEADS * HEAD_PAD), F32)]
        + gat.out_shape,
        scratch_shapes=gat.scratch,
        compiler_params=pltpu.CompilerParams(dimension_semantics=("arbitrary", "arbitrary"),
                                             vmem_limit_bytes=VMEM_LIMIT),
    )(q_t, k, k, v_t, *late_shards)
    return out[0], out[1], out[2:]


def _mid(o, z_sh, x, target, pool_w, pool_scale, w_ba, w_bp, w_out, norm_final, tm):
    s = x.shape[0]
    n_tiles = s // tm
    halo_per_tile = tm // POOL_HALO

    def body(o_ref, z0_ref, z1_ref, z1h_ref, z2_ref, z3_ref, x_ref, t_ref, pw_ref, ps_ref, wba_ref, wbp_ref,
             wout_ref, gf_ref,
             do_ref, dl_ref, dga_ref, dgp_ref, dgm_ref, ddc_ref, dh_ref,
             loss_ref, dwout_out, dwba_out, dwbp_out, dpw_ref, dps_ref, dgf_ref,
             ubuf, dwout_ref, dwba_ref, dwbp_ref):
        i = pl.program_id(0)

        @pl.when(i == 0)
        def _():
            loss_ref[...] = jnp.zeros_like(loss_ref)
            dwout_ref[...] = jnp.zeros_like(dwout_ref)
            dwba_ref[...] = jnp.zeros_like(dwba_ref)
            dwbp_ref[...] = jnp.zeros_like(dwbp_ref)
            dpw_ref[...] = jnp.zeros_like(dpw_ref)
            dps_ref[...] = jnp.zeros_like(dps_ref)
            dgf_ref[...] = jnp.zeros_like(dgf_ref)

        zs = [z0_ref[0], z1_ref[0], z2_ref[0], z3_ref[0]]
        o = o_ref[...]
        ga = _segment(zs, 3)
        sga = _sigmoid(ga)
        silu_a = ga * sga
        y_attn = (o * silu_a).astype(BF16)

        ubuf[0:POOL_HALO, :] = jnp.where(i > 0, _segment([None, z1h_ref[0]], 4), 0.0)
        ubuf[POOL_HALO:, :] = _segment(zs, 4)
        row = lax.broadcasted_iota(jnp.int32, (tm, POOL_GROUP_DIM), 0) + i * tm
        ps = ps_ref[...]
        gp = _segment(zs, 5)
        sgp = _sigmoid(gp)
        silu_p = gp * sgp
        d_bf, dm, inv_cnt = [], [], []
        for g, w in enumerate(POOL_WINDOWS):
            cols = slice(g * POOL_GROUP_DIM, (g + 1) * POOL_GROUP_DIM)
            wsum = ubuf[POOL_HALO:, cols]
            for kk in range(1, w):
                wsum = wsum + ubuf[POOL_HALO - kk:POOL_HALO - kk + tm, cols]
            inv = 1.0 / jnp.minimum(row + 1, w).astype(F32)
            dg = (wsum * inv - ubuf[POOL_HALO:, cols]).astype(BF16)
            d_bf.append(dg)
            inv_cnt.append(inv)
            dm.append(_dot(dg, pw_ref[g]))
        dm = jnp.concatenate(dm, axis=1)
        yp = dm * ps
        y_pool = (yp * silu_p).astype(BF16)

        a = jnp.concatenate([_dot(y_attn, wba_ref[j]) for j in range(N_CHIPS)], axis=1)
        p = jnp.concatenate([_dot(y_pool, wbp_ref[j]) for j in range(N_CHIPS)], axis=1)
        gm = _segment(zs, 6)
        gate_a = _sigmoid(gm[:, :D_MODEL])
        gate_p = _sigmoid(gm[:, D_MODEL:])
        merged = (gate_a * a + gate_p * p).astype(BF16)
        h = x_ref[...] + _dot(merged, wout_ref[...])
        gf = gf_ref[...]
        y, xhat, r = _rms_fwd(h, gf)
        err = y - t_ref[...]
        e2 = err * err
        e2 = jnp.sum(e2.reshape(tm // 8, 8, D_MODEL), axis=0)
        acc = e2[:, 0:LANES]
        for cidx in range(1, D_MODEL // LANES):
            acc = acc + e2[:, cidx * LANES:(cidx + 1) * LANES]
        loss_ref[...] += acc

        dy = err * (1.0 / D_MODEL)
        dgf_ref[...] += _colsum(dy * xhat)
        dh = _rms_bwd(dy, xhat, r, gf)
        dh_ref[...] = dh
        dh_bf = dh.astype(BF16)
        dwout_ref[...] += _dot_tn(merged, dh_bf)
        dmerged = _dot_nt(dh_bf, wout_ref[...])
        da = (dmerged * gate_a).astype(BF16)
        dp = (dmerged * gate_p).astype(BF16)
        dgm_ref[:, :D_MODEL] = (dmerged * a * gate_a * (1.0 - gate_a)).astype(BF16)
        dgm_ref[:, D_MODEL:] = (dmerged * p * gate_p * (1.0 - gate_p)).astype(BF16)
        dy_attn = dy_pool = None
        for j in range(N_CHIPS):
            cols = slice(j * BRANCH_COLS, (j + 1) * BRANCH_COLS)
            dwba_ref[j] += _dot_tn(y_attn, da[:, cols])
            dwbp_ref[j] += _dot_tn(y_pool, dp[:, cols])
            pa = _dot_nt(da[:, cols], wba_ref[j])
            pp = _dot_nt(dp[:, cols], wbp_ref[j])
            dy_attn = pa if dy_attn is None else dy_attn + pa
            dy_pool = pp if dy_pool is None else dy_pool + pp

        do = dy_attn * silu_a
        do_ref[...] = do
        dga_ref[...] = (dy_attn * o * (sga * (1.0 + ga * (1.0 - sga)))).astype(BF16)
        doo = do * o
        for hd in range(MLA_HEADS):
            dl = jnp.sum(doo[:, hd * V_HEAD_DIM:(hd + 1) * V_HEAD_DIM], axis=1, keepdims=True)
            dl_ref[:, hd * HEAD_PAD:(hd + 1) * HEAD_PAD] = jnp.broadcast_to(dl, (tm, HEAD_PAD))

        dyp = dy_pool * silu_p
        dgp_ref[...] = (dy_pool * yp * (sgp * (1.0 + gp * (1.0 - sgp)))).astype(BF16)
        dps_ref[...] += _colsum(dyp * dm)
        dmm = (dyp * ps).astype(BF16)
        for g in range(len(POOL_WINDOWS)):
            cols = slice(g * POOL_GROUP_DIM, (g + 1) * POOL_GROUP_DIM)
            dpw_ref[g] += _dot_tn(d_bf[g], dmm[:, cols])
            ddc_ref[:, cols] = _dot_nt(dmm[:, cols], pw_ref[g]) * inv_cnt[g]

        @pl.when(i == n_tiles - 1)
        def _():
            dwout_out[...] = dwout_ref[...].astype(BF16)
            dwba_out[...] = dwba_ref[...].astype(BF16)
            dwbp_out[...] = dwbp_ref[...].astype(BF16)

    row_in = lambda n: _row_spec(tm, n)
    in_specs = [
        row_in(MLA_WIDTH),
        pl.BlockSpec((1, tm, SHARD_COLS), lambda i: (0, i, 0)), pl.BlockSpec((1, tm, SHARD_COLS), lambda i: (1, i, 0)),
        pl.BlockSpec((1, POOL_HALO, SHARD_COLS), lambda i: (1, jnp.maximum(i * halo_per_tile - 1, 0), 0)),
        pl.BlockSpec((1, tm, SHARD_COLS), lambda i: (2, i, 0)), pl.BlockSpec((1, tm, SHARD_COLS), lambda i: (3, i, 0)),
        row_in(D_MODEL), row_in(D_MODEL),
        _full_spec((4, POOL_GROUP_DIM, POOL_GROUP_DIM)), _full_spec((1, POOL_WIDTH)),
        _full_spec((N_CHIPS, MLA_WIDTH, BRANCH_COLS)), _full_spec((N_CHIPS, POOL_WIDTH, BRANCH_COLS)),
        _full_spec((D_MODEL, D_MODEL)), _full_spec((1, D_MODEL)),
    ]
    out_shape = [
        jax.ShapeDtypeStruct((s, MLA_WIDTH), F32),
        jax.ShapeDtypeStruct((s, MLA_HEADS * HEAD_PAD), F32),
        jax.ShapeDtypeStruct((s, MLA_WIDTH), BF16),
        jax.ShapeDtypeStruct((s, POOL_WIDTH), BF16),
        jax.ShapeDtypeStruct((s, 2 * D_MODEL), BF16),
        jax.ShapeDtypeStruct((s, POOL_WIDTH), F32),
        jax.ShapeDtypeStruct((s, D_MODEL), F32),
        jax.ShapeDtypeStruct((8, LANES), F32),
        jax.ShapeDtypeStruct((D_MODEL, D_MODEL), BF16),
        jax.ShapeDtypeStruct((N_CHIPS, MLA_WIDTH, BRANCH_COLS), BF16),
        jax.ShapeDtypeStruct((N_CHIPS, POOL_WIDTH, BRANCH_COLS), BF16),
        jax.ShapeDtypeStruct((4, POOL_GROUP_DIM, POOL_GROUP_DIM), F32),
        jax.ShapeDtypeStruct((1, POOL_WIDTH), F32),
        jax.ShapeDtypeStruct((1, D_MODEL), F32),
    ]
    out_specs = [
        row_in(MLA_WIDTH), row_in(MLA_HEADS * HEAD_PAD), row_in(MLA_WIDTH), row_in(POOL_WIDTH),
        row_in(2 * D_MODEL), row_in(POOL_WIDTH), row_in(D_MODEL),
        _full_spec((8, LANES)), _full_spec((D_MODEL, D_MODEL)), _full_spec((N_CHIPS, MLA_WIDTH, BRANCH_COLS)),
        _full_spec((N_CHIPS, POOL_WIDTH, BRANCH_COLS)), _full_spec((4, POOL_GROUP_DIM, POOL_GROUP_DIM)),
        _full_spec((1, POOL_WIDTH)), _full_spec((1, D_MODEL)),
    ]
    return pl.pallas_call(
        body,
        name="mid",
        grid=(n_tiles,),
        in_specs=in_specs,
        out_specs=out_specs,
        out_shape=out_shape,
        scratch_shapes=[
            pltpu.VMEM((tm + POOL_HALO, POOL_WIDTH), F32),
            pltpu.VMEM((D_MODEL, D_MODEL), F32),
            pltpu.VMEM((N_CHIPS, MLA_WIDTH, BRANCH_COLS), F32),
            pltpu.VMEM((N_CHIPS, POOL_WIDTH, BRANCH_COLS), F32),
        ],
        compiler_params=pltpu.CompilerParams(dimension_semantics=("arbitrary",), vmem_limit_bytes=VMEM_LIMIT),
    )(o, z_sh, z_sh, z_sh, z_sh, z_sh, x, target, pool_w, pool_scale, w_ba, w_bp, w_out, norm_final)


def _attn_bwd(q, q_t, k, v, do, lse, delta, late_grads, gs_mid, t):
    s = q.shape[0]
    groups = MLA_HEADS // BWD_HEADS
    n_q = s // t
    red = _Reduce(COMM_PARAMS[3:])
    n_w = len(red.params)
    small = _SmallSum(gs_mid.shape[0])
    n_red = len(red.scratch)

    def body(q_ref, qt_ref, do_ref, lse_ref, dl_ref, k_ref, v_ref, *rest):
        g_in, gs_ref = rest[:n_w], rest[n_w]
        (dq_ref, dk_ref, dv_ref), g_out, gsum_ref = rest[n_w + 1:n_w + 4], rest[n_w + 4:2 * n_w + 4], rest[2 * n_w + 4]
        scratch = rest[2 * n_w + 5:]
        red.bind(g_in, g_out, scratch[:n_red])
        small.bind(gs_ref, gsum_ref, scratch[n_red:])
        i = pl.program_id(1)
        step_no = pl.program_id(0) * n_q + i

        @pl.when(step_no == 0)
        def _():
            red.start()
            small.start()

        pl.when(step_no == groups * n_q // 2)(red.exchange)

        @pl.when(i == 0)
        def _():
            dk_ref[...] = jnp.zeros_like(dk_ref)
            dv_ref[...] = jnp.zeros_like(dv_ref)

        mask = _chunk_mask(t, False)
        qcs = [slice(hh * HEAD_PAD, (hh + 1) * HEAD_PAD) for hh in range(BWD_HEADS)]
        vcs = [slice(hh * V_HEAD_DIM, (hh + 1) * V_HEAD_DIM) for hh in range(BWD_HEADS)]
        qhs = [q_ref[:, qc] for qc in qcs]
        qts = [qt_ref[qc, :] for qc in qcs]
        dohs = [do_ref[:, vc].astype(BF16) for vc in vcs]
        do_t = do_ref[...].T.astype(BF16)
        dots = [do_t[vc, :] for vc in vcs]
        lses = [jnp.tile(lse_ref[:, qc], (1, t // HEAD_PAD)) for qc in qcs]
        dls = [jnp.tile(dl_ref[:, qc], (1, t // HEAD_PAD)) for qc in qcs]

        def step(j, dqs, masked):
            keys = pl.ds(pl.multiple_of(j * t, t), t)
            out = []
            for hh in range(BWD_HEADS):
                kj = k_ref[keys, qcs[hh]]
                vj = v_ref[keys, vcs[hh]]
                p = jnp.exp2(_dot_nt(qhs[hh], kj) * ATT_SCALE_LOG2E - lses[hh])
                if masked:
                    p = jnp.where(mask, p, 0.0)
                ds = (p * (_dot_nt(dohs[hh], vj) - dls[hh])).astype(BF16)
                dv_ref[vcs[hh], keys] += _dot(dots[hh], p.astype(BF16))
                dk_ref[qcs[hh], keys] += _dot(qts[hh], ds) * ATT_SCALE
                out.append(dqs[hh] + _dot(ds, kj))
            return tuple(out)

        zero = jnp.zeros((t, HEAD_PAD), F32)
        dqs = lax.fori_loop(0, i, functools.partial(step, masked=False), (zero,) * BWD_HEADS)
        dqs = step(i, dqs, True)
        for hh in range(BWD_HEADS):
            dq_ref[:, qcs[hh]] = dqs[hh] * ATT_SCALE

        @pl.when(step_no == groups * n_q - 1)
        def _():
            red.finish()
            small.finish()

    hw = MLA_HEADS * HEAD_PAD
    any_spec = pl.BlockSpec(memory_space=pl.ANY)
    out = pl.pallas_call(
        body,
        name="attn_bwd",
        grid=(groups, n_q),
        in_specs=[
            pl.BlockSpec((t, BWD_HEADS * HEAD_PAD), lambda p, i: (i, p)),
            pl.BlockSpec((BWD_HEADS * HEAD_PAD, t), lambda p, i: (p, i)),
            pl.BlockSpec((t, BWD_HEADS * V_HEAD_DIM), lambda p, i: (i, p)),
            pl.BlockSpec((t, BWD_HEADS * HEAD_PAD), lambda p, i: (i, p)),
            pl.BlockSpec((t, BWD_HEADS * HEAD_PAD), lambda p, i: (i, p)),
            pl.BlockSpec((s, BWD_HEADS * HEAD_PAD), lambda p, i: (0, p), pipeline_mode=pl.Buffered(1)),
            pl.BlockSpec((s, BWD_HEADS * V_HEAD_DIM), lambda p, i: (0, p), pipeline_mode=pl.Buffered(1)),
        ] + [any_spec] * n_w + [pl.BlockSpec(small.spec_shape, lambda p, i: (0, 0))],
        out_specs=[
            pl.BlockSpec((t, BWD_HEADS * HEAD_PAD), lambda p, i: (i, p)),
            pl.BlockSpec((BWD_HEADS * HEAD_PAD, s), lambda p, i: (p, 0)),
            pl.BlockSpec((BWD_HEADS * V_HEAD_DIM, s), lambda p, i: (p, 0)),
        ] + [any_spec] * n_w + [pl.BlockSpec(small.spec_shape, lambda p, i: (0, 0))],
        out_shape=[jax.ShapeDtypeStruct((s, hw), F32), jax.ShapeDtypeStruct((hw, s), F32),
                   jax.ShapeDtypeStruct((MLA_WIDTH, s), F32)] + red.out_shape + [small.out_shape],
        scratch_shapes=red.scratch + small.scratch,
        compiler_params=pltpu.CompilerParams(dimension_semantics=("arbitrary", "arbitrary"),
                                             vmem_limit_bytes=VMEM_LIMIT),
    )(q, q_t, do, lse, delta, k, v, *late_grads, gs_mid)
    return out[0], out[1], out[2], out[3:3 + n_w], out[3 + n_w]


def _qkv_bwd(dq, dk_t, dv_t, z_sh, q_norm, kv_norm, wuq_p, wk_p, wv, rc, rsa, rsb, tm):
    s = z_sh.shape[1]
    hw = MLA_HEADS * HEAD_PAD

    def body(dq_ref, dk_ref, dv_ref, z_ref, gq_ref, gkv_ref, wuq_ref, wk_ref, wv_ref,
             c_ref, sa_ref, sb_ref,
             dzq_ref, dzkv_ref, dzkr_ref, dwuq_ref, dwk_ref, dwv_ref, dgq_ref, dgkv_ref):
        i = pl.program_id(0)

        @pl.when(i == 0)
        def _():
            dwuq_ref[...] = jnp.zeros_like(dwuq_ref)
            dwk_ref[...] = jnp.zeros_like(dwk_ref)
            dwv_ref[...] = jnp.zeros_like(dwv_ref)
            dgq_ref[...] = jnp.zeros_like(dgq_ref)
            dgkv_ref[...] = jnp.zeros_like(dgkv_ref)

        c, sa, sb = c_ref[...], sa_ref[...], sb_ref[...]
        gq, gkv = gq_ref[...], gkv_ref[...]

        z0 = z_ref[0]
        cq, xq, rq = _rms_fwd(z0[:, ZQ_COLS], gq)
        dqp = jnp.concatenate(
            [_unrope(dq_ref[:, h * HEAD_PAD:(h + 1) * HEAD_PAD], c, sa, sb) for h in range(MLA_HEADS)],
            axis=1).astype(BF16)
        dwuq_ref[...] += _dot_tn(cq.astype(BF16), dqp)
        dcq = _dot_nt(dqp, wuq_ref[...])
        dgq_ref[...] += _colsum(dcq * xq)
        dzq_ref[...] = _rms_bwd(dcq, xq, rq, gq).astype(BF16)

        ckv, xkv, rkv = _rms_fwd(z0[:, ZKV_COLS], gkv)
        ckv = ckv.astype(BF16)
        dkf = dk_ref[...].T
        dk_bf = dkf.astype(BF16)
        dv_bf = dv_ref[...].T.astype(BF16)
        dwk_ref[...] += _dot_tn(ckv, dk_bf)
        dwv_ref[...] += _dot_tn(ckv, dv_bf)
        dckv = _dot_nt(dk_bf, wk_ref[...]) + _dot_nt(dv_bf, wv_ref[...])
        dgkv_ref[...] += _colsum(dckv * xkv)
        dzkv_ref[...] = _rms_bwd(dckv, xkv, rkv, gkv).astype(BF16)

        dkr = dkf[:, 0:HEAD_PAD]
        for h in range(1, MLA_HEADS):
            dkr = dkr + dkf[:, h * HEAD_PAD:(h + 1) * HEAD_PAD]
        dkr = pltpu.roll(_unrope(dkr, c, sa, sb), 64, 1)
        lane = lax.broadcasted_iota(jnp.int32, (tm, HEAD_PAD), 1)
        dzkr_ref[...] = jnp.where(lane < QK_ROPE_DIM, dkr, 0.0).astype(BF16)

    return pl.pallas_call(
        body,
        name="qkv_bwd",
        grid=(s // tm,),
        in_specs=[
            _row_spec(tm, hw), pl.BlockSpec((hw, tm), lambda i: (0, i)), pl.BlockSpec((MLA_WIDTH, tm), lambda i: (0, i)),
            pl.BlockSpec((1, tm, SHARD_COLS), lambda i: (0, i, 0)),
            _full_spec((1, Q_LORA_RANK)), _full_spec((1, KV_LORA_RANK)),
            _full_spec((Q_LORA_RANK, hw)), _full_spec((KV_LORA_RANK, hw)), _full_spec((KV_LORA_RANK, MLA_WIDTH)),
            _row_spec(tm, HEAD_PAD), _row_spec(tm, HEAD_PAD), _row_spec(tm, HEAD_PAD),
        ],
        out_specs=[
            _row_spec(tm, Q_LORA_RANK), _row_spec(tm, KV_LORA_RANK), _row_spec(tm, HEAD_PAD),
            _full_spec((Q_LORA_RANK, hw)), _full_spec((KV_LORA_RANK, hw)), _full_spec((KV_LORA_RANK, MLA_WIDTH)),
            _full_spec((1, Q_LORA_RANK)), _full_spec((1, KV_LORA_RANK)),
        ],
        out_shape=[
            jax.ShapeDtypeStruct((s, Q_LORA_RANK), BF16), jax.ShapeDtypeStruct((s, KV_LORA_RANK), BF16),
            jax.ShapeDtypeStruct((s, HEAD_PAD), BF16),
            jax.ShapeDtypeStruct((Q_LORA_RANK, hw), F32), jax.ShapeDtypeStruct((KV_LORA_RANK, hw), F32),
            jax.ShapeDtypeStruct((KV_LORA_RANK, MLA_WIDTH), F32),
            jax.ShapeDtypeStruct((1, Q_LORA_RANK), F32), jax.ShapeDtypeStruct((1, KV_LORA_RANK), F32),
        ],
        compiler_params=pltpu.CompilerParams(dimension_semantics=("arbitrary",), vmem_limit_bytes=VMEM_LIMIT),
    )(dq, dk_t, dv_t, z_sh, q_norm, kv_norm, wuq_p, wk_p, wv, rc, rsa, rsb)


def _inproj_bwd_x(dzq, dzkv, dzkr, dgattn, ddc, dgpool, dgmerge, x, dh, norm_in, w_in_t, tm):
    s = x.shape[0]
    n_tiles = s // tm
    halo_per_tile = tm // POOL_HALO
    n_halo = s // POOL_HALO
    u_seg = 4

    def body(dzq_ref, dzkv_ref, dzkr_ref, dga_ref, ddc_ref, ddn_ref, dgp_ref, dgm_ref, x_ref, dh_ref,
             g_ref, w_hbm, gx_ref, dgin_ref, dzs_ref, w_vmem, dbuf, sem):
        i = pl.program_id(0)

        @pl.when(i == 0)
        def _():
            cp = pltpu.make_async_copy(w_hbm, w_vmem, sem)
            cp.start()
            dgin_ref[...] = jnp.zeros_like(dgin_ref)
            cp.wait()

        dbuf[0:tm, :] = ddc_ref[...]
        dbuf[tm:, :] = jnp.where(i < n_tiles - 1, ddn_ref[...], 0.0)
        row = lax.broadcasted_iota(jnp.int32, (tm, POOL_GROUP_DIM), 0) + i * tm
        du = []
        for g, w in enumerate(POOL_WINDOWS):
            cols = slice(g * POOL_GROUP_DIM, (g + 1) * POOL_GROUP_DIM)
            fsum = dbuf[0:tm, cols]
            for kk in range(1, w):
                fsum = fsum + dbuf[kk:kk + tm, cols]
            du.append(fsum - dbuf[0:tm, cols] * jnp.minimum(row + 1, w).astype(F32))
        du = jnp.concatenate(du, axis=1).astype(BF16)

        dz = [dzq_ref[...], dzkv_ref[...], dzkr_ref[...], dga_ref[...], du, dgp_ref[...], dgm_ref[...]]
        dz = jnp.concatenate([d[:, :w] for d, (w, _) in zip(dz, IN_SEGMENTS)], axis=1)
        for j in range(N_CHIPS):
            dzs_ref[j] = dz[:, j * SHARD_COLS:(j + 1) * SHARD_COLS].T
        dhn = _dot(dz, w_vmem[...])

        g = g_ref[...]
        _, xhat, r = _rms_fwd(x_ref[...], g)
        dgin_ref[...] += _colsum(dhn * xhat)
        gx_ref[...] = dh_ref[...] + _rms_bwd(dhn, xhat, r, g)

    any_spec = pl.BlockSpec(memory_space=pl.ANY)
    seg_w = [wide for _, wide in IN_SEGMENTS]
    return pl.pallas_call(
        body,
        name="inproj_bwd_x",
        grid=(n_tiles,),
        in_specs=[
            _row_spec(tm, seg_w[0]), _row_spec(tm, seg_w[1]), _row_spec(tm, seg_w[2]),
            _row_spec(tm, seg_w[3]), _row_spec(tm, seg_w[u_seg]),
            pl.BlockSpec((POOL_HALO, POOL_WIDTH), lambda i: (jnp.minimum((i + 1) * halo_per_tile, n_halo - 1), 0)),
            _row_spec(tm, seg_w[5]), _row_spec(tm, seg_w[6]),
            _row_spec(tm, D_MODEL), _row_spec(tm, D_MODEL),
            _full_spec((1, D_MODEL)), any_spec,
        ],
        out_specs=[_row_spec(tm, D_MODEL), _full_spec((1, D_MODEL)),
                   pl.BlockSpec((N_CHIPS, SHARD_COLS, tm), lambda i: (0, 0, i))],
        out_shape=[jax.ShapeDtypeStruct((s, D_MODEL), F32), jax.ShapeDtypeStruct((1, D_MODEL), F32),
                   jax.ShapeDtypeStruct((N_CHIPS, SHARD_COLS, s), BF16)],
        scratch_shapes=[
            pltpu.VMEM((IN_TOTAL, D_MODEL), BF16),
            pltpu.VMEM((tm + POOL_HALO, POOL_WIDTH), F32),
            pltpu.SemaphoreType.DMA,
        ],
        compiler_params=pltpu.CompilerParams(dimension_semantics=("arbitrary",), vmem_limit_bytes=VMEM_LIMIT),
    )(dzq, dzkv, dzkr, dgattn, ddc, ddc, dgpool, dgmerge, x, dh, norm_in, w_in_t.reshape(IN_TOTAL, D_MODEL))


def _inproj_bwd_w(order, dz_sh, hn, g_uq, g_ukv, gs, tm):
    s = hn.shape[0]
    n_tiles = s // tm
    hc = D_MODEL // 2
    red = _Reduce(COMM_PARAMS[1:3])
    small = _SmallSum(gs.shape[0])
    n_red = len(red.scratch)

    def body(order_ref, dz_ref, hn_ref, guq_hbm, gukv_hbm, gs_ref, gw_hbm, guq_out, gukv_out, gsum_ref,
             acc, pm_w, a_w, b_w, r_w, w_send, w_recv, w_local, *more_scratch):
        ph, i = pl.program_id(0), pl.program_id(1)
        x, y, c = lax.axis_index("x"), lax.axis_index("y"), lax.axis_index("c")
        k = 2 * x + y
        me, sibling = (x, y, c), (x, y, 1 - c)
        chips = _other_chips(x, y)
        shard_of_phase = [2 * cx + cy for cx, cy in chips] + [k]
        copy = _remote_copier(w_send, w_recv)
        red.bind([guq_hbm, gukv_hbm], [guq_out, gukv_out], more_scratch[:n_red])
        small.bind(gs_ref, gsum_ref, more_scratch[n_red:])
        mine = pl.ds(pl.multiple_of(c * hc, hc), hc)
        theirs = pl.ds(pl.multiple_of((1 - c) * hc, hc), hc)

        def to_sibling(f):
            j = shard_of_phase[f]
            return copy(f, pm_w.at[j, 1 - c], a_w.at[j], sibling)

        def pair_sum(f):
            cx, cy = chips[f]
            return copy(4 + f, pm_w.at[shard_of_phase[f], c], b_w.at[f], (cx, cy, c))

        def finished():
            return copy(7, r_w, gw_hbm.at[:, mine], sibling)

        @pl.when(jnp.logical_and(ph == 0, i == 0))
        def _():
            red.start()
            small.start()

        part = _dot(dz_ref[0], hn_ref[...])

        @pl.when(i == 0)
        def _():
            acc[...] = part

        @pl.when(i > 0)
        def _():
            acc[...] += part

        for f in range(3):
            @pl.when(jnp.logical_and(ph == f + 1, i == 0))
            def _(f=f):
                j = shard_of_phase[f]
                copy(f, a_w.at[j], a_w.at[j], me).wait_recv()
                pm_w[j, c] = (pm_w[j, c].astype(F32) + a_w[j].astype(F32)).astype(BF16)
                pair_sum(f).start()
                if f == 0:
                    red.exchange()

        for f in range(4):
            @pl.when(jnp.logical_and(ph == f, i == n_tiles - 1))
            def _(f=f):
                j = shard_of_phase[f]
                pm_w[j, 0] = acc[:, :hc].astype(BF16)
                pm_w[j, 1] = acc[:, hc:].astype(BF16)
                to_sibling(f).start()
                if f < 3:
                    return
                copy(3, a_w.at[k], a_w.at[k], me).wait_recv()
                r_w[...] = pm_w[k, c].astype(F32) + a_w[k].astype(F32)
                for g in range(3):
                    copy(4 + g, b_w.at[g], b_w.at[g], me).wait_recv()
                    r_w[...] = r_w[...] + b_w[g].astype(F32)
                store = pltpu.make_async_copy(r_w, gw_hbm.at[:, mine], w_local)
                store.start()
                finished().start()
                red.finish()
                small.finish()
                copy(7, gw_hbm.at[:, theirs], gw_hbm.at[:, theirs], me).wait_recv()
                store.wait()
                for g in range(4):
                    to_sibling(g).wait_send()
                for g in range(3):
                    pair_sum(g).wait_send()
                finished().wait_send()

    any_spec = pl.BlockSpec(memory_space=pl.ANY)
    n_sem = 8
    grid_spec = pltpu.PrefetchScalarGridSpec(
        num_scalar_prefetch=1,
        grid=(N_CHIPS, n_tiles),
        in_specs=[
            pl.BlockSpec((1, SHARD_COLS, tm), lambda ph, i, order: (order[ph], 0, i)),
            pl.BlockSpec((tm, D_MODEL), lambda ph, i, order: (i, 0)),
            any_spec, any_spec,
            pl.BlockSpec(small.spec_shape, lambda ph, i, order: (0, 0)),
        ],
        out_specs=[any_spec, any_spec, any_spec, pl.BlockSpec(small.spec_shape, lambda ph, i, order: (0, 0))],
        scratch_shapes=[
            pltpu.VMEM((SHARD_COLS, D_MODEL), F32),
            pltpu.VMEM((N_CHIPS, 2, SHARD_COLS, hc), BF16),
            pltpu.VMEM((N_CHIPS, SHARD_COLS, hc), BF16),
            pltpu.VMEM((3, SHARD_COLS, hc), BF16),
            pltpu.VMEM((SHARD_COLS, hc), F32),
            pltpu.SemaphoreType.DMA((n_sem,)), pltpu.SemaphoreType.DMA((n_sem,)), pltpu.SemaphoreType.DMA,
        ] + red.scratch + small.scratch,
    )
    out = pl.pallas_call(
        body,
        name="inproj_bwd_w",
        grid_spec=grid_spec,
        out_shape=[jax.ShapeDtypeStruct((SHARD_COLS, D_MODEL), F32)] + red.out_shape
        + [small.out_shape],
        compiler_params=pltpu.CompilerParams(dimension_semantics=("arbitrary", "arbitrary"),
                                             vmem_limit_bytes=VMEM_LIMIT),
    )(order, dz_sh, hn, g_uq, g_ukv, gs)
    return out[0], out[1], out[2], out[3]


def _other_chips(x, y):
    return ((1 - x, 1 - y), (1 - x, y), (x, 1 - y))


def _half(ref, axis, size, c, lead=()):
    window = pl.ds(pl.multiple_of(c * size, size), size)
    if axis == 0:
        return ref.at[(*lead, window, slice(None))]
    return ref.at[(*lead, slice(None), window)]


def _half_shape(rows, cols, axis, size):
    return (size, cols) if axis == 0 else (rows, size)


def _remote_copier(send_sems, recv_sems):
    def copy(sem, src, dst, to):
        return pltpu.make_async_remote_copy(src_ref=src, dst_ref=dst, send_sem=send_sems.at[sem],
                                            recv_sem=recv_sems.at[sem], device_id=to, device_id_type=MESH)
    return copy


class _Gather:
    def __init__(self, params):
        self.params = params
        n = len(params)
        self.scratch = [pltpu.SemaphoreType.DMA((6 * n,)), pltpu.SemaphoreType.DMA((6 * n,)),
                        pltpu.SemaphoreType.DMA((n,))]
        self.out_shape = [jax.ShapeDtypeStruct((N_CHIPS, r, cc), BF16) for _, r, cc, _, _ in params]

    def bind(self, ins, outs, scratch):
        self.ins, self.outs = ins, outs
        send_sems, recv_sems, self.local_sems = scratch
        self.copy = _remote_copier(send_sems, recv_sems)
        self.x, self.y, self.c = lax.axis_index("x"), lax.axis_index("y"), lax.axis_index("c")
        self.k = 2 * self.x + self.y
        self.chips = _other_chips(self.x, self.y)

    def _local(self, p):
        return pltpu.make_async_copy(self.ins[p], self.outs[p].at[self.k], self.local_sems.at[p])

    def _first(self, p, j):
        _, _, _, axis, size = self.params[p]
        cx, cy = self.chips[j]
        return self.copy(6 * p + j, _half(self.ins[p], axis, size, self.c),
                         _half(self.outs[p], axis, size, self.c, (self.k,)), (cx, cy, self.c))

    def _relay(self, p, j, half_of):
        _, _, _, axis, size = self.params[p]
        cx, cy = self.chips[j]
        block = _half(self.outs[p], axis, size, half_of, (2 * cx + cy,))
        return self.copy(6 * p + 3 + j, block, block, (self.x, self.y, 1 - self.c))

    def start(self):
        for p in range(len(self.params)):
            self._local(p).start()
            for j in (1, 2, 0):
                self._first(p, j).start()

    def relay_one(self, p, j):
        _, _, _, axis, size = self.params[p]
        cx, cy = self.chips[j]
        landed = _half(self.outs[p], axis, size, self.c, (2 * cx + cy,))
        self.copy(6 * p + j, landed, landed, (self.x, self.y, self.c)).wait_recv()
        self._relay(p, j, self.c).start()

    def await_one(self, p, j):
        self._relay(p, j, 1 - self.c).wait_recv()

    def wait_sends(self):
        for p in range(len(self.params)):
            for j in range(3):
                self._first(p, j).wait_send()
                self._relay(p, j, self.c).wait_send()
            self._local(p).wait()

    def relay(self):
        for j in range(3):
            for p in range(len(self.params)):
                self.relay_one(p, j)

    def finish(self):
        for j in range(3):
            for p in range(len(self.params)):
                self.await_one(p, j)
        self.wait_sends()


class _Reduce:
    def __init__(self, params):
        self.params = params
        n = len(params)
        halves = [_half_shape(r, cc, axis, size) for _, r, cc, axis, size in params]
        self.scratch = ([pltpu.VMEM((N_CHIPS, *h), BF16) for h in halves]
                        + [pltpu.VMEM((N_CHIPS, *h), BF16) for h in halves]
                        + [pltpu.VMEM((3, *h), BF16) for h in halves]
                        + [pltpu.VMEM(h, F32) for h in halves]
                        + [pltpu.SemaphoreType.DMA((5 * n,)), pltpu.SemaphoreType.DMA((5 * n,)),
                           pltpu.SemaphoreType.DMA((2 * n,))])
        self.out_shape = [jax.ShapeDtypeStruct((r, cc), F32) for _, r, cc, _, _ in params]

    def bind(self, g_in, g_out, scratch):
        n = len(self.params)
        self.g_in, self.g_out = g_in, g_out
        self.pm, self.a_buf = scratch[0:n], scratch[n:2 * n]
        self.b_buf, self.r_buf = scratch[2 * n:3 * n], scratch[3 * n:4 * n]
        send_sems, recv_sems, self.local_sems = scratch[4 * n:]
        self.copy = _remote_copier(send_sems, recv_sems)
        self.x, self.y, self.c = lax.axis_index("x"), lax.axis_index("y"), lax.axis_index("c")
        self.k = 2 * self.x + self.y
        self.chips = _other_chips(self.x, self.y)
        self.me = (self.x, self.y, self.c)
        self.sibling = (self.x, self.y, 1 - self.c)

    def _load(self, p):
        _, _, _, axis, size = self.params[p]
        return pltpu.make_async_copy(_half(self.g_in[p], axis, size, self.c, (slice(None),)), self.pm[p],
                                     self.local_sems.at[p])

    def _to_sibling(self, p):
        _, _, _, axis, size = self.params[p]
        return self.copy(5 * p, _half(self.g_in[p], axis, size, 1 - self.c, (slice(None),)), self.a_buf[p],
                         self.sibling)

    def _pair_sum(self, p, j):
        cx, cy = self.chips[j]
        return self.copy(5 * p + 1 + j, self.pm[p].at[2 * cx + cy], self.b_buf[p].at[j], (cx, cy, self.c))

    def _store(self, p):
        _, _, _, axis, size = self.params[p]
        n = len(self.params)
        return pltpu.make_async_copy(self.r_buf[p], _half(self.g_out[p], axis, size, self.c),
                                     self.local_sems.at[n + p])

    def _finished(self, p):
        _, _, _, axis, size = self.params[p]
        return self.copy(5 * p + 4, self.r_buf[p], _half(self.g_out[p], axis, size, self.c), self.sibling)

    def start(self):
        for p in range(len(self.params)):
            self._load(p).start()
            self._to_sibling(p).start()

    def exchange(self):
        for p in range(len(self.params)):
            self._load(p).wait()
            self.copy(5 * p, self.a_buf[p], self.a_buf[p], self.me).wait_recv()
            for j, (cx, cy) in enumerate(self.chips):
                kj = 2 * cx + cy
                self.pm[p][kj] = (self.pm[p][kj].astype(F32) + self.a_buf[p][kj].astype(F32)).astype(BF16)
                self._pair_sum(p, j).start()
            self.r_buf[p][...] = self.pm[p][self.k].astype(F32) + self.a_buf[p][self.k].astype(F32)

    def finish(self):
        for p, (_, _, _, axis, size) in enumerate(self.params):
            for j in range(3):
                self.copy(5 * p + 1 + j, self.b_buf[p].at[j], self.b_buf[p].at[j], self.me).wait_recv()
                self.r_buf[p][...] = self.r_buf[p][...] + self.b_buf[p][j].astype(F32)
            self._store(p).start()
            self._finished(p).start()
        for p, (_, _, _, axis, size) in enumerate(self.params):
            theirs = _half(self.g_out[p], axis, size, 1 - self.c)
            self.copy(5 * p + 4, theirs, theirs, self.me).wait_recv()
            self._store(p).wait()
            self._to_sibling(p).wait_send()
            for j in range(3):
                self._pair_sum(p, j).wait_send()
            self._finished(p).wait_send()


class _SmallSum:
    def __init__(self, rows):
        self.rows = rows
        self.scratch = [pltpu.VMEM((N_DEV, rows, LANES), F32),
                        pltpu.SemaphoreType.DMA((N_DEV - 1,)), pltpu.SemaphoreType.DMA((N_DEV - 1,))]
        self.out_shape = jax.ShapeDtypeStruct((rows, LANES), F32)
        self.spec_shape = (rows, LANES)

    def bind(self, src, dst, scratch):
        self.src, self.dst = src, dst
        self.buf, send_sems, recv_sems = scratch
        self.copy = _remote_copier(send_sems, recv_sems)
        self.x, self.y, self.c = lax.axis_index("x"), lax.axis_index("y"), lax.axis_index("c")

    def _send(self, f):
        fx, fy, fc = [(a, b, d) for a in (0, 1) for b in (0, 1) for d in (0, 1)][f]
        x, y, c = self.x, self.y, self.c
        peer = (1 - x if fx else x, 1 - y if fy else y, 1 - c if fc else c)
        return self.copy(f - 1, self.src, self.buf.at[f], peer)

    def start(self):
        for f in range(1, N_DEV):
            self._send(f).start()
        self.buf[0] = self.src[...]

    def finish(self):
        me = (self.x, self.y, self.c)
        for f in range(1, N_DEV):
            self.copy(f - 1, self.buf.at[f], self.buf.at[f], me).wait_recv()
        dev = 4 * self.x + 2 * self.y + self.c
        total = self.buf[dev]
        for d in range(1, N_DEV):
            total = total + self.buf[jnp.bitwise_xor(dev, d)]
        self.dst[...] = total
        for f in range(1, N_DEV):
            self._send(f).wait_send()


def _adamw_math(w, g, m, v):
    m = ADAM_B1 * m + (1.0 - ADAM_B1) * g
    v = ADAM_B2 * v + (1.0 - ADAM_B2) * (g * g)
    m_hat = m / (1.0 - ADAM_B1 ** ADAM_STEP)
    v_hat = v / (1.0 - ADAM_B2 ** ADAM_STEP)
    delta = -ADAM_LR * (m_hat / (jnp.sqrt(v_hat) + ADAM_EPS) + ADAM_WD * w)
    return delta, m, v


def _adamw_tiled(w, g, m, v, tm):
    rows, cols = w.shape

    def body(w_ref, g_ref, m_ref, v_ref, d_ref, nm_ref, nv_ref, g_out):
        g = g_ref[...]
        d_ref[...], nm_ref[...], nv_ref[...] = _adamw_math(w_ref[...], g, m_ref[...], v_ref[...])
        g_out[...] = g

    spec = _row_spec(tm, cols)
    return pl.pallas_call(
        body,
        name="adamw_w_in",
        grid=(rows // tm,),
        in_specs=[spec] * 4,
        out_specs=[spec] * 4,
        out_shape=[jax.ShapeDtypeStruct(w.shape, F32)] * 4,
        compiler_params=pltpu.CompilerParams(dimension_semantics=("parallel",), vmem_limit_bytes=VMEM_LIMIT),
    )(w, g, m, v)


def _adamw_many(ws, gs, ms, vs):
    n = len(ws)
    g_arrays, g_at = [], []
    for g in gs:
        arr, row = g if isinstance(g, tuple) else (g, None)
        k = next((j for j, a in enumerate(g_arrays) if a is arr), len(g_arrays))
        if k == len(g_arrays):
            g_arrays.append(arr)
        g_at.append((k, row))
    n_g = len(g_arrays)

    def body(*refs):
        w_refs, m_refs, v_refs, g_refs, outs = (refs[:n], refs[n:2 * n], refs[2 * n:3 * n], refs[3 * n:3 * n + n_g],
                                                refs[3 * n + n_g:])
        for i in range(n):
            k, row = g_at[i]
            g = g_refs[k][...] if row is None else g_refs[k][row:row + ws[i].shape[0], :]
            d, nm, nv = _adamw_math(w_refs[i][...], g, m_refs[i][...], v_refs[i][...])
            outs[i][...] = d
            outs[n + i][...] = nm
            outs[2 * n + i][...] = nv
            outs[3 * n + i][...] = g

    vmem_spec = pl.BlockSpec(memory_space=pltpu.VMEM)
    shapes = [jax.ShapeDtypeStruct(w.shape, F32) for w in ws]
    out = pl.pallas_call(
        body,
        name="adamw_small",
        in_specs=[vmem_spec] * (3 * n + n_g),
        out_specs=[vmem_spec] * (4 * n),
        out_shape=shapes * 4,
        compiler_params=pltpu.CompilerParams(vmem_limit_bytes=VMEM_LIMIT),
    )(*ws, *ms, *vs, *g_arrays)
    return out[:n], out[n:2 * n], out[2 * n:3 * n], out[3 * n:]


def _pack_rows(parts, rows, dtype):
    flat = jnp.concatenate([p.reshape(-1).astype(dtype) for p in parts])
    flat = jnp.concatenate([flat, jnp.zeros((rows * LANES - flat.shape[0],), dtype)])
    return flat.reshape(rows, LANES)


def _unpack_rows(packed, shapes):
    flat = packed.reshape(-1)
    out, off = [], 0
    for _, shp in shapes:
        n = int(np.prod(shp))
        out.append(flat[off:off + n].reshape(shp))
        off += n
    return out


def _rope_tables(s):
    half = QK_ROPE_DIM // 2
    inv_freq = np.float32(ROPE_THETA) ** (-np.arange(half, dtype=np.float32) / np.float32(half))
    ang = (np.arange(s, dtype=np.float32)[:, None] * inv_freq[None, :]).astype(np.float32)
    cos, sin = np.cos(ang.astype(np.float64)).astype(np.float32), np.sin(ang.astype(np.float64)).astype(np.float32)
    z16 = np.zeros((s, half), np.float32)
    z32 = np.zeros((s, HEAD_PAD - QK_NOPE_DIM - QK_ROPE_DIM), np.float32)
    z64 = np.zeros((s, QK_NOPE_DIM), np.float32)
    rc = np.concatenate([np.ones((s, QK_NOPE_DIM), np.float32), cos, cos, z32], axis=1)
    rsa = np.concatenate([z64, -sin, z16, z32], axis=1)
    rsb = np.concatenate([z64, z16, sin, z32], axis=1)
    return jnp.asarray(rc), jnp.asarray(rsa), jnp.asarray(rsb)


def kernel(x, norm_in, w_in, q_norm, w_uq, kv_norm, w_ukv, pool_w, pool_scale, w_branch_attn, w_branch_pool, w_out, norm_final, loss_target, m_norm_in, m_w_in, m_q_norm, m_w_uq, m_kv_norm, m_w_ukv, m_pool_w, m_pool_scale, m_w_branch_attn, m_w_branch_pool, m_w_out, m_norm_final, v_norm_in, v_w_in, v_q_norm, v_w_uq, v_kv_norm, v_w_ukv, v_pool_w, v_pool_scale, v_w_branch_attn, v_w_branch_pool, v_w_out, v_norm_final):
    s = x.shape[1]
    t_att, t_row = _tiles(s)
    x2 = x.reshape(s, D_MODEL)
    tgt = loss_target.reshape(s, D_MODEL)

    local = [w_in.T, w_uq.reshape(96, 768), w_ukv.reshape(64, 1024), w_branch_attn, w_branch_pool, w_out]
    local = [a.astype(BF16) for a in local]
    cx, cy = lax.axis_index("x"), lax.axis_index("y")
    others = [2 * ox + oy for ox, oy in _other_chips(cx, cy)]
    hn, z_sh, (w_in_t, w_uq_all, w_ukv_all) = _inproj_fwd(
        jnp.stack([2 * cx + cy, others[1], others[2], others[0]]).astype(jnp.int32), x2, norm_in.reshape(1, -1),
        local[:3], 4 * t_row)
    w_uq_f = w_uq_all.reshape(Q_LORA_RANK, MLA_HEADS, QK_NOPE_DIM + QK_ROPE_DIM)
    w_ukv_f = w_ukv_all.reshape(KV_LORA_RANK, MLA_HEADS, QK_NOPE_DIM + V_HEAD_DIM)
    hw = MLA_HEADS * HEAD_PAD
    wuq_p = jnp.pad(w_uq_f, ((0, 0), (0, 0), (0, HEAD_PAD - QK_NOPE_DIM - QK_ROPE_DIM))).reshape(Q_LORA_RANK, hw)
    wk_p = jnp.pad(w_ukv_f[:, :, :QK_NOPE_DIM], ((0, 0), (0, 0), (0, HEAD_PAD - QK_NOPE_DIM))).reshape(KV_LORA_RANK, hw)
    wv = w_ukv_f[:, :, QK_NOPE_DIM:].reshape(KV_LORA_RANK, MLA_WIDTH)
    rc, rsa, rsb = _rope_tables(s)
    g_in = norm_in.reshape(1, -1)
    g_q = q_norm.reshape(1, -1)
    g_kv = kv_norm.reshape(1, -1)
    g_f = norm_final.reshape(1, -1)
    ps = pool_scale.reshape(1, -1)
    pw_bf = pool_w.astype(BF16)

    q, k, v, q_t, v_t = _qkv_fwd(z_sh, g_q, g_kv, wuq_p, wk_p, wv, rc, rsa, rsb, 2 * t_row)
    o, lse, (w_ba_all, w_bp_all, w_out_all) = _attn_fwd(q_t, k, v_t, local[3:], t_att)
    w_out_f = w_out_all.reshape(D_MODEL, D_MODEL)

    (do, delta, dgattn, dgpool, dgmerge, ddc, dh, sq_err, d_w_out, d_w_ba, d_w_bp, d_pool_w, d_pool_scale,
     d_norm_final) = _mid(o, z_sh, x2, tgt, pw_bf, ps, w_ba_all, w_bp_all, w_out_f, g_f, t_row)

    late_grads = [d_w_ba, d_w_bp, d_w_out.reshape(N_CHIPS, 256, D_MODEL)]
    small_mid = dict(pool_scale=d_pool_scale, norm_final=d_norm_final, pool_w=d_pool_w, sq_err=sq_err)
    gs_mid = _pack_rows([small_mid[n] for n, _ in SMALL_MID], _small_rows(SMALL_MID), F32)
    dq, dk_t, dv_t, (g_w_ba, g_w_bp, g_w_out), g_small_mid = _attn_bwd(q, q_t, k, v, do, lse, delta, late_grads,
                                                                      gs_mid, t_att)
    sq_err_all = _unpack_rows(g_small_mid, SMALL_MID)[-1]
    dzq, dzkv, dzkr, d_wuq_p, d_wk_p, d_wv, d_q_norm, d_kv_norm = _qkv_bwd(
        dq, dk_t, dv_t, z_sh, g_q, g_kv, wuq_p, wk_p, wv, rc, rsa, rsb, 2 * t_row)
    grad_x, d_norm_in, dz_sh = _inproj_bwd_x(dzq, dzkv, dzkr, dgattn, ddc, dgpool, dgmerge, x2, dh, g_in, w_in_t,
                                             2 * t_row)

    d_w_uq = d_wuq_p.reshape(Q_LORA_RANK, MLA_HEADS, HEAD_PAD)[:, :, :QK_NOPE_DIM + QK_ROPE_DIM]
    d_w_ukv = jnp.concatenate([d_wk_p.reshape(KV_LORA_RANK, MLA_HEADS, HEAD_PAD)[:, :, :QK_NOPE_DIM],
                               d_wv.reshape(KV_LORA_RANK, MLA_HEADS, V_HEAD_DIM)], axis=2)
    small_late = dict(norm_in=d_norm_in, q_norm=d_q_norm, kv_norm=d_kv_norm)
    gs = _pack_rows([small_late[n] for n, _ in SMALL_LATE], _small_rows(SMALL_LATE), F32)
    order = jnp.stack(others + [2 * cx + cy]).astype(jnp.int32)
    g_w_in_t, g_w_uq, g_w_ukv, g_small = _inproj_bwd_w(
        order, dz_sh, hn, d_w_uq.reshape(N_CHIPS, 96, 768).astype(BF16),
        d_w_ukv.reshape(N_CHIPS, 64, 1024).astype(BF16), gs, 4 * t_row)
    g_w_uq = g_w_uq.reshape(w_uq.shape)
    g_w_ukv = g_w_ukv.reshape(w_ukv.shape)

    dl_w_in, nm_w_in, nv_w_in, g_w_in = (a.T for a in _adamw_tiled(w_in.T, g_w_in_t, m_w_in.T, v_w_in.T, 152))

    packed = {n: (g_small_mid, r) for n, r in _first_rows(SMALL_MID).items() if n != "sq_err"}
    packed.update({n: (g_small, r) for n, r in _first_rows(SMALL_LATE).items()})

    def as_rows(n, a):
        return a.reshape(-1, LANES) if n in packed else a

    names = ["norm_in", "q_norm", "w_uq", "kv_norm", "w_ukv", "pool_w", "pool_scale", "w_branch_attn",
             "w_branch_pool", "w_out", "norm_final"]
    ws = dict(norm_in=norm_in, q_norm=q_norm, w_uq=w_uq, kv_norm=kv_norm, w_ukv=w_ukv, pool_w=pool_w,
              pool_scale=pool_scale, w_branch_attn=w_branch_attn, w_branch_pool=w_branch_pool, w_out=w_out,
              norm_final=norm_final)
    gsd = dict(packed, w_uq=g_w_uq, w_ukv=g_w_ukv, w_branch_attn=g_w_ba, w_branch_pool=g_w_bp, w_out=g_w_out)
    msd = dict(norm_in=m_norm_in, q_norm=m_q_norm, w_uq=m_w_uq, kv_norm=m_kv_norm, w_ukv=m_w_ukv, pool_w=m_pool_w,
               pool_scale=m_pool_scale, w_branch_attn=m_w_branch_attn, w_branch_pool=m_w_branch_pool, w_out=m_w_out,
               norm_final=m_norm_final)
    vsd = dict(norm_in=v_norm_in, q_norm=v_q_norm, w_uq=v_w_uq, kv_norm=v_kv_norm, w_ukv=v_w_ukv, pool_w=v_pool_w,
               pool_scale=v_pool_scale, w_branch_attn=v_w_branch_attn, w_branch_pool=v_w_branch_pool, w_out=v_w_out,
               norm_final=v_norm_final)
    dls, nms, nvs, g_outs = _adamw_many([as_rows(n, ws[n]) for n in names], [gsd[n] for n in names],
                                        [as_rows(n, msd[n]) for n in names], [as_rows(n, vsd[n]) for n in names])

    grads = dict(zip(names, g_outs))
    grads["w_in"] = g_w_in
    delta_w = {n: d.reshape(ws[n].shape) for n, d in zip(names, dls)}
    new_m = {n: d.reshape(ws[n].shape) for n, d in zip(names, nms)}
    new_v = {n: d.reshape(ws[n].shape) for n, d in zip(names, nvs)}
    delta_w["w_in"], new_m["w_in"], new_v["w_in"] = dl_w_in, nm_w_in, nv_w_in
    ws["w_in"] = w_in

    order = ["norm_in", "w_in", "q_norm", "w_uq", "kv_norm", "w_ukv", "pool_w", "pool_scale", "w_branch_attn",
             "w_branch_pool", "w_out", "norm_final"]
    loss = 0.5 * jnp.sum(sq_err_all) / D_MODEL
    return (loss, grad_x.reshape(x.shape),
            *[grads[n].reshape(ws[n].shape) for n in order],
            *[delta_w[n] for n in order], *[new_m[n] for n in order], *[new_v[n] for n in order])
```

```python
import functools

import jax
import jax.numpy as jnp
import numpy as np
from jax import lax
from jax.experimental import pallas as pl
from jax.experimental.pallas import tpu as pltpu

F32 = jnp.float32
BF16 = jnp.bfloat16
MESH = pl.DeviceIdType.MESH

D_MODEL = 1024
CHUNK = 64
MLA_HEADS = 8
QK_NOPE_DIM = 64
QK_ROPE_DIM = 32
V_HEAD_DIM = 64
Q_LORA_RANK = 384
KV_LORA_RANK = 256
MLA_WIDTH = MLA_HEADS * V_HEAD_DIM
ROPE_THETA = 10000.0
POOL_WINDOWS = (2, 4, 8, 16)
POOL_WIDTH = 512
POOL_GROUP_DIM = 128
BRANCH_COLS = D_MODEL // 4
FWD_HEADS = 8
BWD_HEADS = 4
POOL_HALO = 16
EPS = 1e-6
IN_TOTAL = 4256
HEAD_PAD = 128
ATT_SCALE = (QK_NOPE_DIM + QK_ROPE_DIM) ** -0.5
ATT_SCALE_LOG2E = ATT_SCALE * 1.4426950408889634

ADAM_LR = 0.001
ADAM_B1 = 0.9
ADAM_B2 = 0.999
ADAM_EPS = 1e-08
ADAM_WD = 0.01
ADAM_STEP = 10

N_CHIPS = 4
N_DEV = 8
LANES = 128
VMEM_LIMIT = 60 * 1024 * 1024

IN_SEGMENTS = ((384, 384), (256, 256), (32, HEAD_PAD), (512, 512), (512, 512), (512, 512), (2048, 2048))
SHARD_COLS = IN_TOTAL // N_CHIPS
ZQ_COLS = slice(0, 384)
ZKV_COLS = slice(384, 640)
ZKR_TILE = slice(640, 768)


def _shard_pieces():
    bounds, off = [], 0
    for w, _ in IN_SEGMENTS:
        bounds.append((off, off + w))
        off += w
    out = []
    for j in range(N_CHIPS):
        lo, hi = SHARD_COLS * j, SHARD_COLS * (j + 1)
        out.append([(i, max(lo, a) - a, min(hi, b) - a, max(lo, a) - lo)
                    for i, (a, b) in enumerate(bounds) if max(lo, a) < min(hi, b)])
    return out


SHARD_PIECES = _shard_pieces()


def _segment(z_blocks, seg):
    parts = [z_blocks[j][:, col:col + hi - lo]
             for j, pieces in enumerate(SHARD_PIECES) for sg, lo, hi, col in pieces if sg == seg]
    return parts[0] if len(parts) == 1 else jnp.concatenate(parts, axis=1)

COMM_PARAMS = (
    ("w_in", SHARD_COLS, D_MODEL, 1, 512),
    ("w_uq", 96, 768, 0, 48),
    ("w_ukv", 64, 1024, 0, 32),
    ("w_branch_attn", 512, 256, 0, 256),
    ("w_branch_pool", 512, 256, 0, 256),
    ("w_out", 256, 1024, 0, 128),
)

SMALL_MID = (
    ("pool_scale", (512,)),
    ("norm_final", (1024,)),
    ("pool_w", (4, 128, 128)),
    ("sq_err", (8, 128)),
)
SMALL_LATE = (
    ("norm_in", (1024,)),
    ("q_norm", (384,)),
    ("kv_norm", (256,)),
)


def _small_rows(shapes):
    return -(-sum(int(np.prod(s)) for _, s in shapes) // (LANES * 8)) * 8


def _first_rows(shapes):
    out, off = {}, 0
    for name, shp in shapes:
        out[name], rem = divmod(off, LANES)
        assert rem == 0, name
        off += int(np.prod(shp))
    return out


def _dot(a, b):
    return jnp.dot(a, b, preferred_element_type=F32)


def _dot_nt(a, b):
    return lax.dot_general(a, b, (((1,), (1,)), ((), ())), preferred_element_type=F32)


def _dot_tn(a, b):
    return lax.dot_general(a, b, (((0,), (0,)), ((), ())), preferred_element_type=F32)


def _sigmoid(x):
    return 1.0 / (1.0 + jnp.exp(-x))


def _colsum(x):
    return jnp.sum(x, axis=0, keepdims=True)


def _rms_fwd(x, g):
    r = lax.rsqrt(jnp.mean(x * x, axis=-1, keepdims=True) + EPS)
    xhat = x * r
    return xhat * g, xhat, r


def _rms_bwd(dy, xhat, r, g):
    dxhat = dy * g
    return r * (dxhat - xhat * jnp.mean(dxhat * xhat, axis=-1, keepdims=True))


def _rope(v, c, sa, sb):
    return v * c + pltpu.roll(v, 112, 1) * sa + pltpu.roll(v, 16, 1) * sb


def _unrope(d, c, sa, sb):
    return d * c + pltpu.roll(d * sa, 16, 1) + pltpu.roll(d * sb, 112, 1)


def _row_spec(tm, n):
    return pl.BlockSpec((tm, n), lambda i: (i, 0))


def _full_spec(shape):
    nd = len(shape)
    return pl.BlockSpec(shape, lambda i: (0,) * nd)


def _tiles(s):
    t_att = 512 if s >= 2048 else 128
    t_row = 256 if s >= 1024 else 128
    return t_att, t_row


def _inproj_fwd(order, x, norm_in, early_shards, tm):
    s = x.shape[0]
    n_tiles = s // tm
    gat = _Gather(COMM_PARAMS[:3])
    n_w = len(gat.params)
    arrival = (1, 2, 0)

    def body(order_ref, x_ref, g_ref, *rest):
        w_loc, (hn_ref, z_ref), w_all = rest[:n_w], rest[n_w:n_w + 2], rest[n_w + 2:2 * n_w + 2]
        w_vmem, hn_all, w_sem = rest[2 * n_w + 2:2 * n_w + 5]
        gat.bind(w_loc, w_all, rest[2 * n_w + 5:])
        ph, i = pl.program_id(0), pl.program_id(1)
        pl.when(jnp.logical_and(ph == 0, i == 0))(gat.start)

        def fetch(phase):
            src = w_loc[0] if phase == 0 else w_all[0].at[order_ref[phase]]
            return pltpu.make_async_copy(src, w_vmem.at[phase % 2], w_sem.at[phase % 2])

        def landed(f):
            gat.relay_one(0, arrival[f])
            gat.await_one(0, arrival[f])

        @pl.when(jnp.logical_and(ph == 0, i == 0))
        def _():
            fetch(0).start()
            fetch(0).wait()

        @pl.when(jnp.logical_and(ph == 1, i == 0))
        def _():
            landed(0)
            fetch(1).start()
            fetch(1).wait()

        for f in (1, 2):
            @pl.when(jnp.logical_and(ph == f, i == n_tiles - 1))
            def _(f=f):
                landed(f)
                fetch(f + 1).start()

            @pl.when(jnp.logical_and(ph == f + 1, i == 0))
            def _(f=f):
                fetch(f + 1).wait()

        rows = pl.ds(pl.multiple_of(i * tm, tm), tm)

        @pl.when(ph == 0)
        def _():
            hn, _, _ = _rms_fwd(x_ref[...], g_ref[...])
            hn = hn.astype(BF16)
            hn_ref[...] = hn
            hn_all[rows, :] = hn

        z_ref[0] = _dot_nt(hn_all[rows, :], w_vmem[ph % 2])

        @pl.when(jnp.logical_and(ph == N_CHIPS - 1, i == n_tiles - 1))
        def _():
            for p in range(1, n_w):
                for j in range(3):
                    gat.relay_one(p, j)
            for p in range(1, n_w):
                for j in range(3):
                    gat.await_one(p, j)
            gat.wait_sends()

    def tile_in_phase0(ph, i, order):
        return (jnp.where(ph == 0, i, n_tiles - 1), 0)

    any_spec = pl.BlockSpec(memory_space=pl.ANY)
    grid_spec = pltpu.PrefetchScalarGridSpec(
        num_scalar_prefetch=1,
        grid=(N_CHIPS, n_tiles),
        in_specs=[pl.BlockSpec((tm, D_MODEL), tile_in_phase0),
                  pl.BlockSpec((1, D_MODEL), lambda ph, i, order: (0, 0))] + [any_spec] * n_w,
        out_specs=[pl.BlockSpec((tm, D_MODEL), tile_in_phase0),
                   pl.BlockSpec((1, tm, SHARD_COLS), lambda ph, i, order: (order[ph], i, 0))] + [any_spec] * n_w,
        scratch_shapes=[pltpu.VMEM((2, SHARD_COLS, D_MODEL), BF16), pltpu.VMEM((s, D_MODEL), BF16),
                        pltpu.SemaphoreType.DMA((2,))] + gat.scratch,
    )
    out = pl.pallas_call(
        body,
        name="inproj_fwd",
        grid_spec=grid_spec,
        out_shape=[jax.ShapeDtypeStruct((s, D_MODEL), BF16), jax.ShapeDtypeStruct((N_CHIPS, s, SHARD_COLS), F32)]
        + gat.out_shape,
        compiler_params=pltpu.CompilerParams(dimension_semantics=("arbitrary", "arbitrary"),
                                             vmem_limit_bytes=VMEM_LIMIT),
    )(order, x, norm_in, *early_shards)
    return out[0], out[1], out[2:]


def _qkv_fwd(z_sh, q_norm, kv_norm, wuq_p, wk_p, wv, rc, rsa, rsb, tm):
    s = z_sh.shape[1]
    hw = MLA_HEADS * HEAD_PAD

    def body(z_ref, gq_ref, gkv_ref, wuq_ref, wk_ref, wv_ref, c_ref, sa_ref, sb_ref,
             q_ref, k_ref, v_ref, qt_ref, vt_ref):
        c, sa, sb = c_ref[...], sa_ref[...], sb_ref[...]
        z0 = z_ref[0]
        cq, _, _ = _rms_fwd(z0[:, ZQ_COLS], gq_ref[...])
        qf = _dot(cq.astype(BF16), wuq_ref[...])
        ckv, _, _ = _rms_fwd(z0[:, ZKV_COLS], gkv_ref[...])
        ckv = ckv.astype(BF16)
        kn = _dot(ckv, wk_ref[...])
        lane = lax.broadcasted_iota(jnp.int32, (tm, HEAD_PAD), 1)
        zkr = jnp.where(lane < QK_ROPE_DIM, z0[:, ZKR_TILE], 0.0)
        kr = _rope(pltpu.roll(zkr, 64, 1), c, sa, sb)
        for h in range(MLA_HEADS):
            cols = slice(h * HEAD_PAD, (h + 1) * HEAD_PAD)
            qh = _rope(qf[:, cols], c, sa, sb)
            q_ref[:, cols] = qh.astype(BF16)
            qt_ref[cols, :] = qh.T.astype(BF16)
            k_ref[:, cols] = (kn[:, cols] + kr).astype(BF16)
        vf = _dot(ckv, wv_ref[...])
        v_ref[...] = vf.astype(BF16)
        vt_ref[...] = vf.T.astype(BF16)

    return pl.pallas_call(
        body,
        name="qkv_fwd",
        grid=(s // tm,),
        in_specs=[
            pl.BlockSpec((1, tm, SHARD_COLS), lambda i: (0, i, 0)),
            _full_spec((1, Q_LORA_RANK)), _full_spec((1, KV_LORA_RANK)),
            _full_spec((Q_LORA_RANK, hw)), _full_spec((KV_LORA_RANK, hw)), _full_spec((KV_LORA_RANK, MLA_WIDTH)),
            _row_spec(tm, HEAD_PAD), _row_spec(tm, HEAD_PAD), _row_spec(tm, HEAD_PAD),
        ],
        out_specs=[_row_spec(tm, hw), _row_spec(tm, hw), _row_spec(tm, MLA_WIDTH),
                   pl.BlockSpec((hw, tm), lambda i: (0, i)), pl.BlockSpec((MLA_WIDTH, tm), lambda i: (0, i))],
        out_shape=[jax.ShapeDtypeStruct((s, hw), BF16), jax.ShapeDtypeStruct((s, hw), BF16),
                   jax.ShapeDtypeStruct((s, MLA_WIDTH), BF16),
                   jax.ShapeDtypeStruct((hw, s), BF16), jax.ShapeDtypeStruct((MLA_WIDTH, s), BF16)],
        compiler_params=pltpu.CompilerParams(dimension_semantics=("parallel",), vmem_limit_bytes=VMEM_LIMIT),
    )(z_sh, q_norm, kv_norm, wuq_p, wk_p, wv, rc, rsa, rsb)


def _chunk_mask(t, keys_on_rows):
    rows = lax.broadcasted_iota(jnp.int32, (t, t), 0) // CHUNK
    cols = lax.broadcasted_iota(jnp.int32, (t, t), 1) // CHUNK
    return rows <= cols if keys_on_rows else cols <= rows


def _attn_fwd(q_t, k, v_t, late_shards, t):
    s = k.shape[0]
    groups = MLA_HEADS // FWD_HEADS
    n_q = s // t
    gat = _Gather(COMM_PARAMS[3:])
    n_w = len(gat.params)

    def body(qt_ref, k_ref, k2_ref, vt_ref, *rest):
        w_in, (o_ref, lse_ref), w_out = rest[:n_w], rest[n_w:n_w + 2], rest[n_w + 2:2 * n_w + 2]
        gat.bind(w_in, w_out, rest[2 * n_w + 2:])
        i = pl.program_id(1)
        step_no = pl.program_id(0) * n_q + i
        pl.when(step_no == 0)(gat.start)
        pl.when(step_no == groups * n_q // 2)(gat.relay)
        mask = _chunk_mask(t, True)
        qcs = [slice(hh * HEAD_PAD, (hh + 1) * HEAD_PAD) for hh in range(FWD_HEADS)]
        vcs = [slice(hh * V_HEAD_DIM, (hh + 1) * V_HEAD_DIM) for hh in range(FWD_HEADS)]
        qts = [qt_ref[qc, :] for qc in qcs]

        def step(j, carry, masked):
            keys = pl.ds(pl.multiple_of(j * t, t), t)
            out = []
            for hh in range(FWD_HEADS):
                m, l, acc = carry[hh]
                sc = _dot(k_ref[keys, qcs[hh]], qts[hh])
                if masked:
                    sc = jnp.where(mask, sc, -jnp.inf)
                m_new = jnp.maximum(m, jnp.max(sc, axis=0, keepdims=True))
                alpha = jnp.exp2((m - m_new) * ATT_SCALE_LOG2E)
                p = jnp.exp2((_dot(k2_ref[keys, qcs[hh]], qts[hh]) - m_new) * ATT_SCALE_LOG2E)
                if masked:
                    p = jnp.where(mask, p, 0.0)
                l = alpha * l + jnp.sum(p, axis=0, keepdims=True)
                acc = alpha * acc + _dot(vt_ref[vcs[hh], keys], p.astype(BF16))
                out.append((m_new, l, acc))
            return tuple(out)

        one = (jnp.full((1, t), -jnp.inf, F32), jnp.zeros((1, t), F32), jnp.zeros((V_HEAD_DIM, t), F32))
        carry = lax.fori_loop(0, i, functools.partial(step, masked=False), (one,) * FWD_HEADS)
        carry = step(i, carry, True)
        o_ref[...] = jnp.concatenate([carry[hh][2] / carry[hh][1] for hh in range(FWD_HEADS)], axis=0).T
        for hh in range(FWD_HEADS):
            m, l, _ = carry[hh]
            lse_ref[:, qcs[hh]] = jnp.broadcast_to(m * ATT_SCALE_LOG2E + jnp.log2(l), (HEAD_PAD, t)).T
        pl.when(step_no == groups * n_q - 1)(gat.finish)

    any_spec = pl.BlockSpec(memory_space=pl.ANY)
    out = pl.pallas_call(
        body,
        name="attn_fwd",
        grid=(groups, n_q),
        in_specs=[
            pl.BlockSpec((FWD_HEADS * HEAD_PAD, t), lambda p, i: (p, i)),
            pl.BlockSpec((s, FWD_HEADS * HEAD_PAD), lambda p, i: (0, p), pipeline_mode=pl.Buffered(1)),
            pl.BlockSpec((s, FWD_HEADS * HEAD_PAD), lambda p, i: (0, p), pipeline_mode=pl.Buffered(1)),
            pl.BlockSpec((FWD_HEADS * V_HEAD_DIM, s), lambda p, i: (p, 0), pipeline_mode=pl.Buffered(1)),
        ] + [any_spec] * n_w,
        out_specs=[
            pl.BlockSpec((t, FWD_HEADS * V_HEAD_DIM), lambda p, i: (i, p)),
            pl.BlockSpec((t, FWD_HEADS * HEAD_PAD), lambda p, i: (i, p)),
        ] + [any_spec] * n_w,
        out_shape=[jax.ShapeDtypeStruct((s, MLA_WIDTH), F32), jax.ShapeDtypeStruct((s, MLA_HEADS * HEAD_PAD), F32)]
        + gat.out_shape,
        scratch_shapes=gat.scratch,
        compiler_params=pltpu.CompilerParams(dimension_semantics=("arbitrary", "arbitrary"),
                                             vmem_limit_bytes=VMEM_LIMIT),
    )(q_t, k, k, v_t, *late_shards)
    return out[0], out[1], out[2:]


def _mid(o, z_sh, x, target, pool_w, pool_scale, w_ba, w_bp, w_out, norm_final, tm):
    s = x.shape[0]
    n_tiles = s // tm
    halo_per_tile = tm // POOL_HALO

    def body(o_ref, z0_ref, z1_ref, z1h_ref, z2_ref, z3_ref, x_ref, t_ref, pw_ref, ps_ref, wba_ref, wbp_ref,
             wout_ref, gf_ref,
             do_ref, dl_ref, dga_ref, dgp_ref, dgm_ref, ddc_ref, dh_ref,
             loss_ref, dwout_out, dwba_out, dwbp_out, dpw_ref, dps_ref, dgf_ref,
             ubuf, dwout_ref, dwba_ref, dwbp_ref):
        i = pl.program_id(0)

        @pl.when(i == 0)
        def _():
            loss_ref[...] = jnp.zeros_like(loss_ref)
            dwout_ref[...] = jnp.zeros_like(dwout_ref)
            dwba_ref[...] = jnp.zeros_like(dwba_ref)
            dwbp_ref[...] = jnp.zeros_like(dwbp_ref)
            dpw_ref[...] = jnp.zeros_like(dpw_ref)
            dps_ref[...] = jnp.zeros_like(dps_ref)
            dgf_ref[...] = jnp.zeros_like(dgf_ref)

        zs = [z0_ref[0], z1_ref[0], z2_ref[0], z3_ref[0]]
        o = o_ref[...]
        ga = _segment(zs, 3)
        sga = _sigmoid(ga)
        silu_a = ga * sga
        y_attn = (o * silu_a).astype(BF16)

        ubuf[0:POOL_HALO, :] = jnp.where(i > 0, _segment([None, z1h_ref[0]], 4), 0.0)
        ubuf[POOL_HALO:, :] = _segment(zs, 4)
        row = lax.broadcasted_iota(jnp.int32, (tm, POOL_GROUP_DIM), 0) + i * tm
        ps = ps_ref[...]
        gp = _segment(zs, 5)
        sgp = _sigmoid(gp)
        silu_p = gp * sgp
        d_bf, dm, inv_cnt = [], [], []
        for g, w in enumerate(POOL_WINDOWS):
            cols = slice(g * POOL_GROUP_DIM, (g + 1) * POOL_GROUP_DIM)
            wsum = ubuf[POOL_HALO:, cols]
            for kk in range(1, w):
                wsum = wsum + ubuf[POOL_HALO - kk:POOL_HALO - kk + tm, cols]
            inv = 1.0 / jnp.minimum(row + 1, w).astype(F32)
            dg = (wsum * inv - ubuf[POOL_HALO:, cols]).astype(BF16)
            d_bf.append(dg)
            inv_cnt.append(inv)
            dm.append(_dot(dg, pw_ref[g]))
        dm = jnp.concatenate(dm, axis=1)
        yp = dm * ps
        y_pool = (yp * silu_p).astype(BF16)

        a = jnp.concatenate([_dot(y_attn, wba_ref[j]) for j in range(N_CHIPS)], axis=1)
        p = jnp.concatenate([_dot(y_pool, wbp_ref[j]) for j in range(N_CHIPS)], axis=1)
        gm = _segment(zs, 6)
        gate_a = _sigmoid(gm[:, :D_MODEL])
        gate_p = _sigmoid(gm[:, D_MODEL:])
        merged = (gate_a * a + gate_p * p).astype(BF16)
        h = x_ref[...] + _dot(merged, wout_ref[...])
        gf = gf_ref[...]
        y, xhat, r = _rms_fwd(h, gf)
        err = y - t_ref[...]
        e2 = err * err
        e2 = jnp.sum(e2.reshape(tm // 8, 8, D_MODEL), axis=0)
        acc = e2[:, 0:LANES]
        for cidx in range(1, D_MODEL // LANES):
            acc = acc + e2[:, cidx * LANES:(cidx + 1) * LANES]
        loss_ref[...] += acc

        dy = err * (1.0 / D_MODEL)
        dgf_ref[...] += _colsum(dy * xhat)
        dh = _rms_bwd(dy, xhat, r, gf)
        dh_ref[...] = dh
        dh_bf = dh.astype(BF16)
        dwout_ref[...] += _dot_tn(merged, dh_bf)
        dmerged = _dot_nt(dh_bf, wout_ref[...])
        da = (dmerged * gate_a).astype(BF16)
        dp = (dmerged * gate_p).astype(BF16)
        dgm_ref[:, :D_MODEL] = (dmerged * a * gate_a * (1.0 - gate_a)).astype(BF16)
        dgm_ref[:, D_MODEL:] = (dmerged * p * gate_p * (1.0 - gate_p)).astype(BF16)
        dy_attn = dy_pool = None
        for j in range(N_CHIPS):
            cols = slice(j * BRANCH_COLS, (j + 1) * BRANCH_COLS)
            dwba_ref[j] += _dot_tn(y_attn, da[:, cols])
            dwbp_ref[j] += _dot_tn(y_pool, dp[:, cols])
            pa = _dot_nt(da[:, cols], wba_ref[j])
            pp = _dot_nt(dp[:, cols], wbp_ref[j])
            dy_attn = pa if dy_attn is None else dy_attn + pa
            dy_pool = pp if dy_pool is None else dy_pool + pp

        do = dy_attn * silu_a
        do_ref[...] = do
        dga_ref[...] = (dy_attn * o * (sga * (1.0 + ga * (1.0 - sga)))).astype(BF16)
        doo = do * o
        for hd in range(MLA_HEADS):
            dl = jnp.sum(doo[:, hd * V_HEAD_DIM:(hd + 1) * V_HEAD_DIM], axis=1, keepdims=True)
            dl_ref[:, hd * HEAD_PAD:(hd + 1) * HEAD_PAD] = jnp.broadcast_to(dl, (tm, HEAD_PAD))

        dyp = dy_pool * silu_p
        dgp_ref[...] = (dy_pool * yp * (sgp * (1.0 + gp * (1.0 - sgp)))).astype(BF16)
        dps_ref[...] += _colsum(dyp * dm)
        dmm = (dyp * ps).astype(BF16)
        for g in range(len(POOL_WINDOWS)):
            cols = slice(g * POOL_GROUP_DIM, (g + 1) * POOL_GROUP_DIM)
            dpw_ref[g] += _dot_tn(d_bf[g], dmm[:, cols])
            ddc_ref[:, cols] = _dot_nt(dmm[:, cols], pw_ref[g]) * inv_cnt[g]

        @pl.when(i == n_tiles - 1)
        def _():
            dwout_out[...] = dwout_ref[...].astype(BF16)
            dwba_out[...] = dwba_ref[...].astype(BF16)
            dwbp_out[...] = dwbp_ref[...].astype(BF16)

    row_in = lambda n: _row_spec(tm, n)
    in_specs = [
        row_in(MLA_WIDTH),
        pl.BlockSpec((1, tm, SHARD_COLS), lambda i: (0, i, 0)), pl.BlockSpec((1, tm, SHARD_COLS), lambda i: (1, i, 0)),
        pl.BlockSpec((1, POOL_HALO, SHARD_COLS), lambda i: (1, jnp.maximum(i * halo_per_tile - 1, 0), 0)),
        pl.BlockSpec((1, tm, SHARD_COLS), lambda i: (2, i, 0)), pl.BlockSpec((1, tm, SHARD_COLS), lambda i: (3, i, 0)),
        row_in(D_MODEL), row_in(D_MODEL),
        _full_spec((4, POOL_GROUP_DIM, POOL_GROUP_DIM)), _full_spec((1, POOL_WIDTH)),
        _full_spec((N_CHIPS, MLA_WIDTH, BRANCH_COLS)), _full_spec((N_CHIPS, POOL_WIDTH, BRANCH_COLS)),
        _full_spec((D_MODEL, D_MODEL)), _full_spec((1, D_MODEL)),
    ]
    out_shape = [
        jax.ShapeDtypeStruct((s, MLA_WIDTH), F32),
        jax.ShapeDtypeStruct((s, MLA_HEADS * HEAD_PAD), F32),
        jax.ShapeDtypeStruct((s, MLA_WIDTH), BF16),
        jax.ShapeDtypeStruct((s, POOL_WIDTH), BF16),
        jax.ShapeDtypeStruct((s, 2 * D_MODEL), BF16),
        jax.ShapeDtypeStruct((s, POOL_WIDTH), F32),
        jax.ShapeDtypeStruct((s, D_MODEL), F32),
        jax.ShapeDtypeStruct((8, LANES), F32),
        jax.ShapeDtypeStruct((D_MODEL, D_MODEL), BF16),
        jax.ShapeDtypeStruct((N_CHIPS, MLA_WIDTH, BRANCH_COLS), BF16),
        jax.ShapeDtypeStruct((N_CHIPS, POOL_WIDTH, BRANCH_COLS), BF16),
        jax.ShapeDtypeStruct((4, POOL_GROUP_DIM, POOL_GROUP_DIM), F32),
        jax.ShapeDtypeStruct((1, POOL_WIDTH), F32),
        jax.ShapeDtypeStruct((1, D_MODEL), F32),
    ]
    out_specs = [
        row_in(MLA_WIDTH), row_in(MLA_HEADS * HEAD_PAD), row_in(MLA_WIDTH), row_in(POOL_WIDTH),
        row_in(2 * D_MODEL), row_in(POOL_WIDTH), row_in(D_MODEL),
        _full_spec((8, LANES)), _full_spec((D_MODEL, D_MODEL)), _full_spec((N_CHIPS, MLA_WIDTH, BRANCH_COLS)),
        _full_spec((N_CHIPS, POOL_WIDTH, BRANCH_COLS)), _full_spec((4, POOL_GROUP_DIM, POOL_GROUP_DIM)),
        _full_spec((1, POOL_WIDTH)), _full_spec((1, D_MODEL)),
    ]
    return pl.pallas_call(
        body,
        name="mid",
        grid=(n_tiles,),
        in_specs=in_specs,
        out_specs=out_specs,
        out_shape=out_shape,
        scratch_shapes=[
            pltpu.VMEM((tm + POOL_HALO, POOL_WIDTH), F32),
            pltpu.VMEM((D_MODEL, D_MODEL), F32),
            pltpu.VMEM((N_CHIPS, MLA_WIDTH, BRANCH_COLS), F32),
            pltpu.VMEM((N_CHIPS, POOL_WIDTH, BRANCH_COLS), F32),
        ],
        compiler_params=pltpu.CompilerParams(dimension_semantics=("arbitrary",), vmem_limit_bytes=VMEM_LIMIT),
    )(o, z_sh, z_sh, z_sh, z_sh, z_sh, x, target, pool_w, pool_scale, w_ba, w_bp, w_out, norm_final)


def _attn_bwd(q, q_t, k, v, do, lse, delta, late_grads, gs_mid, t):
    s = q.shape[0]
    groups = MLA_HEADS // BWD_HEADS
    n_q = s // t
    red = _Reduce(COMM_PARAMS[3:])
    n_w = len(red.params)
    small = _SmallSum(gs_mid.shape[0])
    n_red = len(red.scratch)

    def body(q_ref, qt_ref, do_ref, lse_ref, dl_ref, k_ref, v_ref, *rest):
        g_in, gs_ref = rest[:n_w], rest[n_w]
        (dq_ref, dk_ref, dv_ref), g_out, gsum_ref = rest[n_w + 1:n_w + 4], rest[n_w + 4:2 * n_w + 4], rest[2 * n_w + 4]
        scratch = rest[2 * n_w + 5:]
        red.bind(g_in, g_out, scratch[:n_red])
        small.bind(gs_ref, gsum_ref, scratch[n_red:])
        i = pl.program_id(1)
        step_no = pl.program_id(0) * n_q + i

        @pl.when(step_no == 0)
        def _():
            red.start()
            small.start()

        pl.when(step_no == groups * n_q // 2)(red.exchange)

        @pl.when(i == 0)
        def _():
            dk_ref[...] = jnp.zeros_like(dk_ref)
            dv_ref[...] = jnp.zeros_like(dv_ref)

        mask = _chunk_mask(t, False)
        qcs = [slice(hh * HEAD_PAD, (hh + 1) * HEAD_PAD) for hh in range(BWD_HEADS)]
        vcs = [slice(hh * V_HEAD_DIM, (hh + 1) * V_HEAD_DIM) for hh in range(BWD_HEADS)]
        qhs = [q_ref[:, qc] for qc in qcs]
        qts = [qt_ref[qc, :] for qc in qcs]
        dohs = [do_ref[:, vc].astype(BF16) for vc in vcs]
        do_t = do_ref[...].T.astype(BF16)
        dots = [do_t[vc, :] for vc in vcs]
        lses = [jnp.tile(lse_ref[:, qc], (1, t // HEAD_PAD)) for qc in qcs]
        dls = [jnp.tile(dl_ref[:, qc], (1, t // HEAD_PAD)) for qc in qcs]

        def step(j, dqs, masked):
            keys = pl.ds(pl.multiple_of(j * t, t), t)
            out = []
            for hh in range(BWD_HEADS):
                kj = k_ref[keys, qcs[hh]]
                vj = v_ref[keys, vcs[hh]]
                p = jnp.exp2(_dot_nt(qhs[hh], kj) * ATT_SCALE_LOG2E - lses[hh])
                if masked:
                    p = jnp.where(mask, p, 0.0)
                ds = (p * (_dot_nt(dohs[hh], vj) - dls[hh])).astype(BF16)
                dv_ref[vcs[hh], keys] += _dot(dots[hh], p.astype(BF16))
                dk_ref[qcs[hh], keys] += _dot(qts[hh], ds) * ATT_SCALE
                out.append(dqs[hh] + _dot(ds, kj))
            return tuple(out)

        zero = jnp.zeros((t, HEAD_PAD), F32)
        dqs = lax.fori_loop(0, i, functools.partial(step, masked=False), (zero,) * BWD_HEADS)
        dqs = step(i, dqs, True)
        for hh in range(BWD_HEADS):
            dq_ref[:, qcs[hh]] = dqs[hh] * ATT_SCALE

        @pl.when(step_no == groups * n_q - 1)
        def _():
            red.finish()
            small.finish()

    hw = MLA_HEADS * HEAD_PAD
    any_spec = pl.BlockSpec(memory_space=pl.ANY)
    out = pl.pallas_call(
        body,
        name="attn_bwd",
        grid=(groups, n_q),
        in_specs=[
            pl.BlockSpec((t, BWD_HEADS * HEAD_PAD), lambda p, i: (i, p)),
            pl.BlockSpec((BWD_HEADS * HEAD_PAD, t), lambda p, i: (p, i)),
            pl.BlockSpec((t, BWD_HEADS * V_HEAD_DIM), lambda p, i: (i, p)),
            pl.BlockSpec((t, BWD_HEADS * HEAD_PAD), lambda p, i: (i, p)),
            pl.BlockSpec((t, BWD_HEADS * HEAD_PAD), lambda p, i: (i, p)),
            pl.BlockSpec((s, BWD_HEADS * HEAD_PAD), lambda p, i: (0, p), pipeline_mode=pl.Buffered(1)),
            pl.BlockSpec((s, BWD_HEADS * V_HEAD_DIM), lambda p, i: (0, p), pipeline_mode=pl.Buffered(1)),
        ] + [any_spec] * n_w + [pl.BlockSpec(small.spec_shape, lambda p, i: (0, 0))],
        out_specs=[
            pl.BlockSpec((t, BWD_HEADS * HEAD_PAD), lambda p, i: (i, p)),
            pl.BlockSpec((BWD_HEADS * HEAD_PAD, s), lambda p, i: (p, 0)),
            pl.BlockSpec((BWD_HEADS * V_HEAD_DIM, s), lambda p, i: (p, 0)),
        ] + [any_spec] * n_w + [pl.BlockSpec(small.spec_shape, lambda p, i: (0, 0))],
        out_shape=[jax.ShapeDtypeStruct((s, hw), F32), jax.ShapeDtypeStruct((hw, s), F32),
                   jax.ShapeDtypeStruct((MLA_WIDTH, s), F32)] + red.out_shape + [small.out_shape],
        scratch_shapes=red.scratch + small.scratch,
        compiler_params=pltpu.CompilerParams(dimension_semantics=("arbitrary", "arbitrary"),
                                             vmem_limit_bytes=VMEM_LIMIT),
    )(q, q_t, do, lse, delta, k, v, *late_grads, gs_mid)
    return out[0], out[1], out[2], out[3:3 + n_w], out[3 + n_w]


def _qkv_bwd(dq, dk_t, dv_t, z_sh, q_norm, kv_norm, wuq_p, wk_p, wv, rc, rsa, rsb, tm):
    s = z_sh.shape[1]
    hw = MLA_HEADS * HEAD_PAD
    n_tiles = s // tm
    uq_shape, ukv_shape = (N_CHIPS,) + COMM_PARAMS[1][1:3], (N_CHIPS,) + COMM_PARAMS[2][1:3]

    def body(dq_ref, dk_ref, dv_ref, z_ref, gq_ref, gkv_ref, wuq_ref, wk_ref, wv_ref,
             c_ref, sa_ref, sb_ref,
             dzq_ref, dzkv_ref, dzkr_ref, duq_ref, dukv_ref, dgq_ref, dgkv_ref, dwuq_ref, dwk_ref, dwv_ref):
        i = pl.program_id(0)

        @pl.when(i == 0)
        def _():
            dwuq_ref[...] = jnp.zeros_like(dwuq_ref)
            dwk_ref[...] = jnp.zeros_like(dwk_ref)
            dwv_ref[...] = jnp.zeros_like(dwv_ref)
            dgq_ref[...] = jnp.zeros_like(dgq_ref)
            dgkv_ref[...] = jnp.zeros_like(dgkv_ref)

        c, sa, sb = c_ref[...], sa_ref[...], sb_ref[...]
        gq, gkv = gq_ref[...], gkv_ref[...]

        z0 = z_ref[0]
        cq, xq, rq = _rms_fwd(z0[:, ZQ_COLS], gq)
        dqp = jnp.concatenate(
            [_unrope(dq_ref[:, h * HEAD_PAD:(h + 1) * HEAD_PAD], c, sa, sb) for h in range(MLA_HEADS)],
            axis=1).astype(BF16)
        dwuq_ref[...] += _dot_tn(cq.astype(BF16), dqp)
        dcq = _dot_nt(dqp, wuq_ref[...])
        dgq_ref[...] += _colsum(dcq * xq)
        dzq_ref[...] = _rms_bwd(dcq, xq, rq, gq).astype(BF16)

        ckv, xkv, rkv = _rms_fwd(z0[:, ZKV_COLS], gkv)
        ckv = ckv.astype(BF16)
        dkf = dk_ref[...].T
        dk_bf = dkf.astype(BF16)
        dv_bf = dv_ref[...].T.astype(BF16)
        dwk_ref[...] += _dot_tn(ckv, dk_bf)
        dwv_ref[...] += _dot_tn(ckv, dv_bf)
        dckv = _dot_nt(dk_bf, wk_ref[...]) + _dot_nt(dv_bf, wv_ref[...])
        dgkv_ref[...] += _colsum(dckv * xkv)
        dzkv_ref[...] = _rms_bwd(dckv, xkv, rkv, gkv).astype(BF16)

        dkr = dkf[:, 0:HEAD_PAD]
        for h in range(1, MLA_HEADS):
            dkr = dkr + dkf[:, h * HEAD_PAD:(h + 1) * HEAD_PAD]
        dkr = pltpu.roll(_unrope(dkr, c, sa, sb), 64, 1)
        lane = lax.broadcasted_iota(jnp.int32, (tm, HEAD_PAD), 1)
        dzkr_ref[...] = jnp.where(lane < QK_ROPE_DIM, dkr, 0.0).astype(BF16)

        @pl.when(i == n_tiles - 1)
        def _():
            qk = QK_NOPE_DIM + QK_ROPE_DIM
            d_uq = jnp.concatenate([dwuq_ref[:, h * HEAD_PAD:h * HEAD_PAD + qk] for h in range(MLA_HEADS)],
                                   axis=1).astype(BF16)
            d_ukv = jnp.concatenate(
                [part for h in range(MLA_HEADS)
                 for part in (dwk_ref[:, h * HEAD_PAD:h * HEAD_PAD + QK_NOPE_DIM],
                              dwv_ref[:, h * V_HEAD_DIM:(h + 1) * V_HEAD_DIM])], axis=1).astype(BF16)
            for j in range(N_CHIPS):
                duq_ref[j] = d_uq[j * uq_shape[1]:(j + 1) * uq_shape[1]]
                dukv_ref[j] = d_ukv[j * ukv_shape[1]:(j + 1) * ukv_shape[1]]

    return pl.pallas_call(
        body,
        name="qkv_bwd",
        grid=(s // tm,),
        in_specs=[
            _row_spec(tm, hw), pl.BlockSpec((hw, tm), lambda i: (0, i)), pl.BlockSpec((MLA_WIDTH, tm), lambda i: (0, i)),
            pl.BlockSpec((1, tm, SHARD_COLS), lambda i: (0, i, 0)),
            _full_spec((1, Q_LORA_RANK)), _full_spec((1, KV_LORA_RANK)),
            _full_spec((Q_LORA_RANK, hw)), _full_spec((KV_LORA_RANK, hw)), _full_spec((KV_LORA_RANK, MLA_WIDTH)),
            _row_spec(tm, HEAD_PAD), _row_spec(tm, HEAD_PAD), _row_spec(tm, HEAD_PAD),
        ],
        out_specs=[
            _row_spec(tm, Q_LORA_RANK), _row_spec(tm, KV_LORA_RANK), _row_spec(tm, HEAD_PAD),
            _full_spec(uq_shape), _full_spec(ukv_shape),
            _full_spec((1, Q_LORA_RANK)), _full_spec((1, KV_LORA_RANK)),
        ],
        out_shape=[
            jax.ShapeDtypeStruct((s, Q_LORA_RANK), BF16), jax.ShapeDtypeStruct((s, KV_LORA_RANK), BF16),
            jax.ShapeDtypeStruct((s, HEAD_PAD), BF16),
            jax.ShapeDtypeStruct(uq_shape, BF16), jax.ShapeDtypeStruct(ukv_shape, BF16),
            jax.ShapeDtypeStruct((1, Q_LORA_RANK), F32), jax.ShapeDtypeStruct((1, KV_LORA_RANK), F32),
        ],
        scratch_shapes=[pltpu.VMEM((Q_LORA_RANK, hw), F32), pltpu.VMEM((KV_LORA_RANK, hw), F32),
                        pltpu.VMEM((KV_LORA_RANK, MLA_WIDTH), F32)],
        compiler_params=pltpu.CompilerParams(dimension_semantics=("arbitrary",), vmem_limit_bytes=VMEM_LIMIT),
    )(dq, dk_t, dv_t, z_sh, q_norm, kv_norm, wuq_p, wk_p, wv, rc, rsa, rsb)


def _inproj_bwd_x(dzq, dzkv, dzkr, dgattn, ddc, dgpool, dgmerge, x, dh, norm_in, w_in_t, tm):
    s = x.shape[0]
    n_tiles = s // tm
    halo_per_tile = tm // POOL_HALO
    n_halo = s // POOL_HALO
    u_seg = 4

    def body(dzq_ref, dzkv_ref, dzkr_ref, dga_ref, ddc_ref, ddn_ref, dgp_ref, dgm_ref, x_ref, dh_ref,
             g_ref, w_hbm, gx_ref, dgin_ref, dzs_ref, w_vmem, dbuf, sem):
        i = pl.program_id(0)

        @pl.when(i == 0)
        def _():
            cp = pltpu.make_async_copy(w_hbm, w_vmem, sem)
            cp.start()
            dgin_ref[...] = jnp.zeros_like(dgin_ref)
            cp.wait()

        dbuf[0:tm, :] = ddc_ref[...]
        dbuf[tm:, :] = jnp.where(i < n_tiles - 1, ddn_ref[...], 0.0)
        row = lax.broadcasted_iota(jnp.int32, (tm, POOL_GROUP_DIM), 0) + i * tm
        du = []
        for g, w in enumerate(POOL_WINDOWS):
            cols = slice(g * POOL_GROUP_DIM, (g + 1) * POOL_GROUP_DIM)
            fsum = dbuf[0:tm, cols]
            for kk in range(1, w):
                fsum = fsum + dbuf[kk:kk + tm, cols]
            du.append(fsum - dbuf[0:tm, cols] * jnp.minimum(row + 1, w).astype(F32))
        du = jnp.concatenate(du, axis=1).astype(BF16)

        dz = [dzq_ref[...], dzkv_ref[...], dzkr_ref[...], dga_ref[...], du, dgp_ref[...], dgm_ref[...]]
        dz = jnp.concatenate([d[:, :w] for d, (w, _) in zip(dz, IN_SEGMENTS)], axis=1)
        for j in range(N_CHIPS):
            dzs_ref[j] = dz[:, j * SHARD_COLS:(j + 1) * SHARD_COLS].T
        dhn = _dot(dz, w_vmem[...])

        g = g_ref[...]
        _, xhat, r = _rms_fwd(x_ref[...], g)
        dgin_ref[...] += _colsum(dhn * xhat)
        gx_ref[...] = dh_ref[...] + _rms_bwd(dhn, xhat, r, g)

    any_spec = pl.BlockSpec(memory_space=pl.ANY)
    seg_w = [wide for _, wide in IN_SEGMENTS]
    return pl.pallas_call(
        body,
        name="inproj_bwd_x",
        grid=(n_tiles,),
        in_specs=[
            _row_spec(tm, seg_w[0]), _row_spec(tm, seg_w[1]), _row_spec(tm, seg_w[2]),
            _row_spec(tm, seg_w[3]), _row_spec(tm, seg_w[u_seg]),
            pl.BlockSpec((POOL_HALO, POOL_WIDTH), lambda i: (jnp.minimum((i + 1) * halo_per_tile, n_halo - 1), 0)),
            _row_spec(tm, seg_w[5]), _row_spec(tm, seg_w[6]),
            _row_spec(tm, D_MODEL), _row_spec(tm, D_MODEL),
            _full_spec((1, D_MODEL)), any_spec,
        ],
        out_specs=[_row_spec(tm, D_MODEL), _full_spec((1, D_MODEL)),
                   pl.BlockSpec((N_CHIPS, SHARD_COLS, tm), lambda i: (0, 0, i))],
        out_shape=[jax.ShapeDtypeStruct((s, D_MODEL), F32), jax.ShapeDtypeStruct((1, D_MODEL), F32),
                   jax.ShapeDtypeStruct((N_CHIPS, SHARD_COLS, s), BF16)],
        scratch_shapes=[
            pltpu.VMEM((IN_TOTAL, D_MODEL), BF16),
            pltpu.VMEM((tm + POOL_HALO, POOL_WIDTH), F32),
            pltpu.SemaphoreType.DMA,
        ],
        compiler_params=pltpu.CompilerParams(dimension_semantics=("arbitrary",), vmem_limit_bytes=VMEM_LIMIT),
    )(dzq, dzkv, dzkr, dgattn, ddc, ddc, dgpool, dgmerge, x, dh, norm_in, w_in_t.reshape(IN_TOTAL, D_MODEL))


def _inproj_bwd_w(order, dz_sh, hn, g_uq, g_ukv, gs, tm):
    s = hn.shape[0]
    n_tiles = s // tm
    hc = D_MODEL // 2
    red = _Reduce(COMM_PARAMS[1:3])
    small = _SmallSum(gs.shape[0])
    n_red = len(red.scratch)

    def body(order_ref, dz_ref, hn_ref, guq_hbm, gukv_hbm, gs_ref, gw_hbm, guq_out, gukv_out, gsum_ref,
             acc, pm_w, a_w, b_w, r_w, w_send, w_recv, w_local, *more_scratch):
        ph, i = pl.program_id(0), pl.program_id(1)
        x, y, c = lax.axis_index("x"), lax.axis_index("y"), lax.axis_index("c")
        k = 2 * x + y
        me, sibling = (x, y, c), (x, y, 1 - c)
        chips = _other_chips(x, y)
        shard_of_phase = [2 * cx + cy for cx, cy in chips] + [k]
        copy = _remote_copier(w_send, w_recv)
        red.bind([guq_hbm, gukv_hbm], [guq_out, gukv_out], more_scratch[:n_red])
        small.bind(gs_ref, gsum_ref, more_scratch[n_red:])
        mine = pl.ds(pl.multiple_of(c * hc, hc), hc)
        theirs = pl.ds(pl.multiple_of((1 - c) * hc, hc), hc)

        def to_sibling(f):
            j = shard_of_phase[f]
            return copy(f, pm_w.at[j, 1 - c], a_w.at[j], sibling)

        def pair_sum(f):
            cx, cy = chips[f]
            return copy(4 + f, pm_w.at[shard_of_phase[f], c], b_w.at[f], (cx, cy, c))

        def finished():
            return copy(7, r_w, gw_hbm.at[:, mine], sibling)

        @pl.when(jnp.logical_and(ph == 0, i == 0))
        def _():
            red.start()
            small.start()

        part = _dot(dz_ref[0], hn_ref[...])

        @pl.when(i == 0)
        def _():
            acc[...] = part

        @pl.when(i > 0)
        def _():
            acc[...] += part

        for f in range(3):
            @pl.when(jnp.logical_and(ph == f + 1, i == 0))
            def _(f=f):
                j = shard_of_phase[f]
                copy(f, a_w.at[j], a_w.at[j], me).wait_recv()
                pm_w[j, c] = (pm_w[j, c].astype(F32) + a_w[j].astype(F32)).astype(BF16)
                pair_sum(f).start()
                if f == 0:
                    red.exchange()

        for f in range(4):
            @pl.when(jnp.logical_and(ph == f, i == n_tiles - 1))
            def _(f=f):
                j = shard_of_phase[f]
                pm_w[j, 0] = acc[:, :hc].astype(BF16)
                pm_w[j, 1] = acc[:, hc:].astype(BF16)
                to_sibling(f).start()
                if f < 3:
                    return
                copy(3, a_w.at[k], a_w.at[k], me).wait_recv()
                r_w[...] = pm_w[k, c].astype(F32) + a_w[k].astype(F32)
                for g in range(3):
                    copy(4 + g, b_w.at[g], b_w.at[g], me).wait_recv()
                    r_w[...] = r_w[...] + b_w[g].astype(F32)
                store = pltpu.make_async_copy(r_w, gw_hbm.at[:, mine], w_local)
                store.start()
                finished().start()
                red.finish()
                small.finish()
                copy(7, gw_hbm.at[:, theirs], gw_hbm.at[:, theirs], me).wait_recv()
                store.wait()
                for g in range(4):
                    to_sibling(g).wait_send()
                for g in range(3):
                    pair_sum(g).wait_send()
                finished().wait_send()

    any_spec = pl.BlockSpec(memory_space=pl.ANY)
    n_sem = 8
    grid_spec = pltpu.PrefetchScalarGridSpec(
        num_scalar_prefetch=1,
        grid=(N_CHIPS, n_tiles),
        in_specs=[
            pl.BlockSpec((1, SHARD_COLS, tm), lambda ph, i, order: (order[ph], 0, i)),
            pl.BlockSpec((tm, D_MODEL), lambda ph, i, order: (i, 0)),
            any_spec, any_spec,
            pl.BlockSpec(small.spec_shape, lambda ph, i, order: (0, 0)),
        ],
        out_specs=[any_spec, any_spec, any_spec, pl.BlockSpec(small.spec_shape, lambda ph, i, order: (0, 0))],
        scratch_shapes=[
            pltpu.VMEM((SHARD_COLS, D_MODEL), F32),
            pltpu.VMEM((N_CHIPS, 2, SHARD_COLS, hc), BF16),
            pltpu.VMEM((N_CHIPS, SHARD_COLS, hc), BF16),
            pltpu.VMEM((3, SHARD_COLS, hc), BF16),
            pltpu.VMEM((SHARD_COLS, hc), F32),
            pltpu.SemaphoreType.DMA((n_sem,)), pltpu.SemaphoreType.DMA((n_sem,)), pltpu.SemaphoreType.DMA,
        ] + red.scratch + small.scratch,
    )
    out = pl.pallas_call(
        body,
        name="inproj_bwd_w",
        grid_spec=grid_spec,
        out_shape=[jax.ShapeDtypeStruct((SHARD_COLS, D_MODEL), F32)] + red.out_shape
        + [small.out_shape],
        compiler_params=pltpu.CompilerParams(dimension_semantics=("arbitrary", "arbitrary"),
                                             vmem_limit_bytes=VMEM_LIMIT),
    )(order, dz_sh, hn, g_uq, g_ukv, gs)
    return out[0], out[1], out[2], out[3]


def _other_chips(x, y):
    return ((1 - x, 1 - y), (1 - x, y), (x, 1 - y))


def _half(ref, axis, size, c, lead=()):
    window = pl.ds(pl.multiple_of(c * size, size), size)
    if axis == 0:
        return ref.at[(*lead, window, slice(None))]
    return ref.at[(*lead, slice(None), window)]


def _half_shape(rows, cols, axis, size):
    return (size, cols) if axis == 0 else (rows, size)


def _remote_copier(send_sems, recv_sems):
    def copy(sem, src, dst, to):
        return pltpu.make_async_remote_copy(src_ref=src, dst_ref=dst, send_sem=send_sems.at[sem],
                                            recv_sem=recv_sems.at[sem], device_id=to, device_id_type=MESH)
    return copy


class _Gather:
    def __init__(self, params):
        self.params = params
        n = len(params)
        self.scratch = [pltpu.SemaphoreType.DMA((6 * n,)), pltpu.SemaphoreType.DMA((6 * n,)),
                        pltpu.SemaphoreType.DMA((n,))]
        self.out_shape = [jax.ShapeDtypeStruct((N_CHIPS, r, cc), BF16) for _, r, cc, _, _ in params]

    def bind(self, ins, outs, scratch):
        self.ins, self.outs = ins, outs
        send_sems, recv_sems, self.local_sems = scratch
        self.copy = _remote_copier(send_sems, recv_sems)
        self.x, self.y, self.c = lax.axis_index("x"), lax.axis_index("y"), lax.axis_index("c")
        self.k = 2 * self.x + self.y
        self.chips = _other_chips(self.x, self.y)

    def _local(self, p):
        return pltpu.make_async_copy(self.ins[p], self.outs[p].at[self.k], self.local_sems.at[p])

    def _first(self, p, j):
        _, _, _, axis, size = self.params[p]
        cx, cy = self.chips[j]
        return self.copy(6 * p + j, _half(self.ins[p], axis, size, self.c),
                         _half(self.outs[p], axis, size, self.c, (self.k,)), (cx, cy, self.c))

    def _relay(self, p, j, half_of):
        _, _, _, axis, size = self.params[p]
        cx, cy = self.chips[j]
        block = _half(self.outs[p], axis, size, half_of, (2 * cx + cy,))
        return self.copy(6 * p + 3 + j, block, block, (self.x, self.y, 1 - self.c))

    def start(self):
        for p in range(len(self.params)):
            self._local(p).start()
            for j in (1, 2, 0):
                self._first(p, j).start()

    def relay_one(self, p, j):
        _, _, _, axis, size = self.params[p]
        cx, cy = self.chips[j]
        landed = _half(self.outs[p], axis, size, self.c, (2 * cx + cy,))
        self.copy(6 * p + j, landed, landed, (self.x, self.y, self.c)).wait_recv()
        self._relay(p, j, self.c).start()

    def await_one(self, p, j):
        self._relay(p, j, 1 - self.c).wait_recv()

    def wait_sends(self):
        for p in range(len(self.params)):
            for j in range(3):
                self._first(p, j).wait_send()
                self._relay(p, j, self.c).wait_send()
            self._local(p).wait()

    def relay(self):
        for j in range(3):
            for p in range(len(self.params)):
                self.relay_one(p, j)

    def finish(self):
        for j in range(3):
            for p in range(len(self.params)):
                self.await_one(p, j)
        self.wait_sends()


class _Reduce:
    def __init__(self, params):
        self.params = params
        n = len(params)
        halves = [_half_shape(r, cc, axis, size) for _, r, cc, axis, size in params]
        self.scratch = ([pltpu.VMEM((N_CHIPS, *h), BF16) for h in halves]
                        + [pltpu.VMEM((N_CHIPS, *h), BF16) for h in halves]
                        + [pltpu.VMEM((3, *h), BF16) for h in halves]
                        + [pltpu.VMEM(h, F32) for h in halves]
                        + [pltpu.SemaphoreType.DMA((5 * n,)), pltpu.SemaphoreType.DMA((5 * n,)),
                           pltpu.SemaphoreType.DMA((2 * n,))])
        self.out_shape = [jax.ShapeDtypeStruct((r, cc), F32) for _, r, cc, _, _ in params]

    def bind(self, g_in, g_out, scratch):
        n = len(self.params)
        self.g_in, self.g_out = g_in, g_out
        self.pm, self.a_buf = scratch[0:n], scratch[n:2 * n]
        self.b_buf, self.r_buf = scratch[2 * n:3 * n], scratch[3 * n:4 * n]
        send_sems, recv_sems, self.local_sems = scratch[4 * n:]
        self.copy = _remote_copier(send_sems, recv_sems)
        self.x, self.y, self.c = lax.axis_index("x"), lax.axis_index("y"), lax.axis_index("c")
        self.k = 2 * self.x + self.y
        self.chips = _other_chips(self.x, self.y)
        self.me = (self.x, self.y, self.c)
        self.sibling = (self.x, self.y, 1 - self.c)

    def _load(self, p):
        _, _, _, axis, size = self.params[p]
        return pltpu.make_async_copy(_half(self.g_in[p], axis, size, self.c, (slice(None),)), self.pm[p],
                                     self.local_sems.at[p])

    def _to_sibling(self, p):
        _, _, _, axis, size = self.params[p]
        return self.copy(5 * p, _half(self.g_in[p], axis, size, 1 - self.c, (slice(None),)), self.a_buf[p],
                         self.sibling)

    def _pair_sum(self, p, j):
        cx, cy = self.chips[j]
        return self.copy(5 * p + 1 + j, self.pm[p].at[2 * cx + cy], self.b_buf[p].at[j], (cx, cy, self.c))

    def _store(self, p):
        _, _, _, axis, size = self.params[p]
        n = len(self.params)
        return pltpu.make_async_copy(self.r_buf[p], _half(self.g_out[p], axis, size, self.c),
                                     self.local_sems.at[n + p])

    def _finished(self, p):
        _, _, _, axis, size = self.params[p]
        return self.copy(5 * p + 4, self.r_buf[p], _half(self.g_out[p], axis, size, self.c), self.sibling)

    def start(self):
        for p in range(len(self.params)):
            self._load(p).start()
            self._to_sibling(p).start()

    def exchange(self):
        for p in range(len(self.params)):
            self._load(p).wait()
            self.copy(5 * p, self.a_buf[p], self.a_buf[p], self.me).wait_recv()
            for j, (cx, cy) in enumerate(self.chips):
                kj = 2 * cx + cy
                self.pm[p][kj] = (self.pm[p][kj].astype(F32) + self.a_buf[p][kj].astype(F32)).astype(BF16)
                self._pair_sum(p, j).start()
            self.r_buf[p][...] = self.pm[p][self.k].astype(F32) + self.a_buf[p][self.k].astype(F32)

    def finish(self):
        for p, (_, _, _, axis, size) in enumerate(self.params):
            for j in range(3):
                self.copy(5 * p + 1 + j, self.b_buf[p].at[j], self.b_buf[p].at[j], self.me).wait_recv()
                self.r_buf[p][...] = self.r_buf[p][...] + self.b_buf[p][j].astype(F32)
            self._store(p).start()
            self._finished(p).start()
        for p, (_, _, _, axis, size) in enumerate(self.params):
            theirs = _half(self.g_out[p], axis, size, 1 - self.c)
            self.copy(5 * p + 4, theirs, theirs, self.me).wait_recv()
            self._store(p).wait()
            self._to_sibling(p).wait_send()
            for j in range(3):
                self._pair_sum(p, j).wait_send()
            self._finished(p).wait_send()


class _SmallSum:
    def __init__(self, rows):
        self.rows = rows
        self.scratch = [pltpu.VMEM((N_DEV, rows, LANES), F32),
                        pltpu.SemaphoreType.DMA((N_DEV - 1,)), pltpu.SemaphoreType.DMA((N_DEV - 1,))]
        self.out_shape = jax.ShapeDtypeStruct((rows, LANES), F32)
        self.spec_shape = (rows, LANES)

    def bind(self, src, dst, scratch):
        self.src, self.dst = src, dst
        self.buf, send_sems, recv_sems = scratch
        self.copy = _remote_copier(send_sems, recv_sems)
        self.x, self.y, self.c = lax.axis_index("x"), lax.axis_index("y"), lax.axis_index("c")

    def _send(self, f):
        fx, fy, fc = [(a, b, d) for a in (0, 1) for b in (0, 1) for d in (0, 1)][f]
        x, y, c = self.x, self.y, self.c
        peer = (1 - x if fx else x, 1 - y if fy else y, 1 - c if fc else c)
        return self.copy(f - 1, self.src, self.buf.at[f], peer)

    def start(self):
        for f in range(1, N_DEV):
            self._send(f).start()
        self.buf[0] = self.src[...]

    def finish(self):
        me = (self.x, self.y, self.c)
        for f in range(1, N_DEV):
            self.copy(f - 1, self.buf.at[f], self.buf.at[f], me).wait_recv()
        dev = 4 * self.x + 2 * self.y + self.c
        total = self.buf[dev]
        for d in range(1, N_DEV):
            total = total + self.buf[jnp.bitwise_xor(dev, d)]
        self.dst[...] = total
        for f in range(1, N_DEV):
            self._send(f).wait_send()


def _adamw_math(w, g, m, v):
    m = ADAM_B1 * m + (1.0 - ADAM_B1) * g
    v = ADAM_B2 * v + (1.0 - ADAM_B2) * (g * g)
    m_hat = m / (1.0 - ADAM_B1 ** ADAM_STEP)
    v_hat = v / (1.0 - ADAM_B2 ** ADAM_STEP)
    delta = -ADAM_LR * (m_hat / (jnp.sqrt(v_hat) + ADAM_EPS) + ADAM_WD * w)
    return delta, m, v


def _adamw_tiled(w, g, m, v, tm):
    rows, cols = w.shape

    def body(w_ref, g_ref, m_ref, v_ref, d_ref, nm_ref, nv_ref, g_out):
        g = g_ref[...]
        d_ref[...], nm_ref[...], nv_ref[...] = _adamw_math(w_ref[...], g, m_ref[...], v_ref[...])
        g_out[...] = g

    spec = _row_spec(tm, cols)
    return pl.pallas_call(
        body,
        name="adamw_w_in",
        grid=(rows // tm,),
        in_specs=[spec] * 4,
        out_specs=[spec] * 4,
        out_shape=[jax.ShapeDtypeStruct(w.shape, F32)] * 4,
        compiler_params=pltpu.CompilerParams(dimension_semantics=("parallel",), vmem_limit_bytes=VMEM_LIMIT),
    )(w, g, m, v)


def _adamw_many(ws, gs, ms, vs):
    n = len(ws)
    g_arrays, g_at = [], []
    for g in gs:
        arr, row = g if isinstance(g, tuple) else (g, None)
        k = next((j for j, a in enumerate(g_arrays) if a is arr), len(g_arrays))
        if k == len(g_arrays):
            g_arrays.append(arr)
        g_at.append((k, row))
    n_g = len(g_arrays)

    def body(*refs):
        w_refs, m_refs, v_refs, g_refs, outs = (refs[:n], refs[n:2 * n], refs[2 * n:3 * n], refs[3 * n:3 * n + n_g],
                                                refs[3 * n + n_g:])
        for i in range(n):
            k, row = g_at[i]
            g = g_refs[k][...] if row is None else g_refs[k][row:row + ws[i].shape[0], :]
            d, nm, nv = _adamw_math(w_refs[i][...], g, m_refs[i][...], v_refs[i][...])
            outs[i][...] = d
            outs[n + i][...] = nm
            outs[2 * n + i][...] = nv
            outs[3 * n + i][...] = g

    vmem_spec = pl.BlockSpec(memory_space=pltpu.VMEM)
    shapes = [jax.ShapeDtypeStruct(w.shape, F32) for w in ws]
    out = pl.pallas_call(
        body,
        name="adamw_small",
        in_specs=[vmem_spec] * (3 * n + n_g),
        out_specs=[vmem_spec] * (4 * n),
        out_shape=shapes * 4,
        compiler_params=pltpu.CompilerParams(vmem_limit_bytes=VMEM_LIMIT),
    )(*ws, *ms, *vs, *g_arrays)
    return out[:n], out[n:2 * n], out[2 * n:3 * n], out[3 * n:]


def _pack_rows(parts, rows, dtype):
    flat = jnp.concatenate([p.reshape(-1).astype(dtype) for p in parts])
    flat = jnp.concatenate([flat, jnp.zeros((rows * LANES - flat.shape[0],), dtype)])
    return flat.reshape(rows, LANES)


def _unpack_rows(packed, shapes):
    flat = packed.reshape(-1)
    out, off = [], 0
    for _, shp in shapes:
        n = int(np.prod(shp))
        out.append(flat[off:off + n].reshape(shp))
        off += n
    return out


def _rope_tables(s):
    half = QK_ROPE_DIM // 2
    inv_freq = np.float32(ROPE_THETA) ** (-np.arange(half, dtype=np.float32) / np.float32(half))
    ang = (np.arange(s, dtype=np.float32)[:, None] * inv_freq[None, :]).astype(np.float32)
    cos, sin = np.cos(ang.astype(np.float64)).astype(np.float32), np.sin(ang.astype(np.float64)).astype(np.float32)
    z16 = np.zeros((s, half), np.float32)
    z32 = np.zeros((s, HEAD_PAD - QK_NOPE_DIM - QK_ROPE_DIM), np.float32)
    z64 = np.zeros((s, QK_NOPE_DIM), np.float32)
    rc = np.concatenate([np.ones((s, QK_NOPE_DIM), np.float32), cos, cos, z32], axis=1)
    rsa = np.concatenate([z64, -sin, z16, z32], axis=1)
    rsb = np.concatenate([z64, z16, sin, z32], axis=1)
    return jnp.asarray(rc), jnp.asarray(rsa), jnp.asarray(rsb)


def kernel(x, norm_in, w_in, q_norm, w_uq, kv_norm, w_ukv, pool_w, pool_scale, w_branch_attn, w_branch_pool, w_out, norm_final, loss_target, m_norm_in, m_w_in, m_q_norm, m_w_uq, m_kv_norm, m_w_ukv, m_pool_w, m_pool_scale, m_w_branch_attn, m_w_branch_pool, m_w_out, m_norm_final, v_norm_in, v_w_in, v_q_norm, v_w_uq, v_kv_norm, v_w_ukv, v_pool_w, v_pool_scale, v_w_branch_attn, v_w_branch_pool, v_w_out, v_norm_final):
    s = x.shape[1]
    t_att, t_row = _tiles(s)
    x2 = x.reshape(s, D_MODEL)
    tgt = loss_target.reshape(s, D_MODEL)

    local = [w_in.T, w_uq.reshape(96, 768), w_ukv.reshape(64, 1024), w_branch_attn, w_branch_pool, w_out]
    local = [a.astype(BF16) for a in local]
    cx, cy = lax.axis_index("x"), lax.axis_index("y")
    others = [2 * ox + oy for ox, oy in _other_chips(cx, cy)]
    hn, z_sh, (w_in_t, w_uq_all, w_ukv_all) = _inproj_fwd(
        jnp.stack([2 * cx + cy, others[1], others[2], others[0]]).astype(jnp.int32), x2, norm_in.reshape(1, -1),
        local[:3], 4 * t_row)
    w_uq_f = w_uq_all.reshape(Q_LORA_RANK, MLA_HEADS, QK_NOPE_DIM + QK_ROPE_DIM)
    w_ukv_f = w_ukv_all.reshape(KV_LORA_RANK, MLA_HEADS, QK_NOPE_DIM + V_HEAD_DIM)
    hw = MLA_HEADS * HEAD_PAD
    wuq_p = jnp.pad(w_uq_f, ((0, 0), (0, 0), (0, HEAD_PAD - QK_NOPE_DIM - QK_ROPE_DIM))).reshape(Q_LORA_RANK, hw)
    wk_p = jnp.pad(w_ukv_f[:, :, :QK_NOPE_DIM], ((0, 0), (0, 0), (0, HEAD_PAD - QK_NOPE_DIM))).reshape(KV_LORA_RANK, hw)
    wv = w_ukv_f[:, :, QK_NOPE_DIM:].reshape(KV_LORA_RANK, MLA_WIDTH)
    rc, rsa, rsb = _rope_tables(s)
    g_in = norm_in.reshape(1, -1)
    g_q = q_norm.reshape(1, -1)
    g_kv = kv_norm.reshape(1, -1)
    g_f = norm_final.reshape(1, -1)
    ps = pool_scale.reshape(1, -1)
    pw_bf = pool_w.astype(BF16)

    q, k, v, q_t, v_t = _qkv_fwd(z_sh, g_q, g_kv, wuq_p, wk_p, wv, rc, rsa, rsb, 2 * t_row)
    o, lse, (w_ba_all, w_bp_all, w_out_all) = _attn_fwd(q_t, k, v_t, local[3:], t_att)
    w_out_f = w_out_all.reshape(D_MODEL, D_MODEL)

    (do, delta, dgattn, dgpool, dgmerge, ddc, dh, sq_err, d_w_out, d_w_ba, d_w_bp, d_pool_w, d_pool_scale,
     d_norm_final) = _mid(o, z_sh, x2, tgt, pw_bf, ps, w_ba_all, w_bp_all, w_out_f, g_f, t_row)

    late_grads = [d_w_ba, d_w_bp, d_w_out.reshape(N_CHIPS, 256, D_MODEL)]
    small_mid = dict(pool_scale=d_pool_scale, norm_final=d_norm_final, pool_w=d_pool_w, sq_err=sq_err)
    gs_mid = _pack_rows([small_mid[n] for n, _ in SMALL_MID], _small_rows(SMALL_MID), F32)
    dq, dk_t, dv_t, (g_w_ba, g_w_bp, g_w_out), g_small_mid = _attn_bwd(q, q_t, k, v, do, lse, delta, late_grads,
                                                                      gs_mid, t_att)
    sq_err_all = _unpack_rows(g_small_mid, SMALL_MID)[-1]
    dzq, dzkv, dzkr, d_w_uq, d_w_ukv, d_q_norm, d_kv_norm = _qkv_bwd(
        dq, dk_t, dv_t, z_sh, g_q, g_kv, wuq_p, wk_p, wv, rc, rsa, rsb, 2 * t_row)
    grad_x, d_norm_in, dz_sh = _inproj_bwd_x(dzq, dzkv, dzkr, dgattn, ddc, dgpool, dgmerge, x2, dh, g_in, w_in_t,
                                             2 * t_row)

    small_late = dict(norm_in=d_norm_in, q_norm=d_q_norm, kv_norm=d_kv_norm)
    gs = _pack_rows([small_late[n] for n, _ in SMALL_LATE], _small_rows(SMALL_LATE), F32)
    order = jnp.stack(others + [2 * cx + cy]).astype(jnp.int32)
    g_w_in_t, g_w_uq, g_w_ukv, g_small = _inproj_bwd_w(
        order, dz_sh, hn, d_w_uq, d_w_ukv, gs, 4 * t_row)
    g_w_uq = g_w_uq.reshape(w_uq.shape)
    g_w_ukv = g_w_ukv.reshape(w_ukv.shape)

    dl_w_in, nm_w_in, nv_w_in, g_w_in = (a.T for a in _adamw_tiled(w_in.T, g_w_in_t, m_w_in.T, v_w_in.T, 152))

    packed = {n: (g_small_mid, r) for n, r in _first_rows(SMALL_MID).items() if n != "sq_err"}
    packed.update({n: (g_small, r) for n, r in _first_rows(SMALL_LATE).items()})

    def as_rows(n, a):
        return a.reshape(-1, LANES) if n in packed else a

    names = ["norm_in", "q_norm", "w_uq", "kv_norm", "w_ukv", "pool_w", "pool_scale", "w_branch_attn",
             "w_branch_pool", "w_out", "norm_final"]
    ws = dict(norm_in=norm_in, q_norm=q_norm, w_uq=w_uq, kv_norm=kv_norm, w_ukv=w_ukv, pool_w=pool_w,
              pool_scale=pool_scale, w_branch_attn=w_branch_attn, w_branch_pool=w_branch_pool, w_out=w_out,
              norm_final=norm_final)
    gsd = dict(packed, w_uq=g_w_uq, w_ukv=g_w_ukv, w_branch_attn=g_w_ba, w_branch_pool=g_w_bp, w_out=g_w_out)
    msd = dict(norm_in=m_norm_in, q_norm=m_q_norm, w_uq=m_w_uq, kv_norm=m_kv_norm, w_ukv=m_w_ukv, pool_w=m_pool_w,
               pool_scale=m_pool_scale, w_branch_attn=m_w_branch_attn, w_branch_pool=m_w_branch_pool, w_out=m_w_out,
               norm_final=m_norm_final)
    vsd = dict(norm_in=v_norm_in, q_norm=v_q_norm, w_uq=v_w_uq, kv_norm=v_kv_norm, w_ukv=v_w_ukv, pool_w=v_pool_w,
               pool_scale=v_pool_scale, w_branch_attn=v_w_branch_attn, w_branch_pool=v_w_branch_pool, w_out=v_w_out,
               norm_final=v_norm_final)
    dls, nms, nvs, g_outs = _adamw_many([as_rows(n, ws[n]) for n in names], [gsd[n] for n in names],
                                        [as_rows(n, msd[n]) for n in names], [as_rows(n, vsd[n]) for n in names])

    grads = dict(zip(names, g_outs))
    grads["w_in"] = g_w_in
    delta_w = {n: d.reshape(ws[n].shape) for n, d in zip(names, dls)}
    new_m = {n: d.reshape(ws[n].shape) for n, d in zip(names, nms)}
    new_v = {n: d.reshape(ws[n].shape) for n, d in zip(names, nvs)}
    delta_w["w_in"], new_m["w_in"], new_v["w_in"] = dl_w_in, nm_w_in, nv_w_in
    ws["w_in"] = w_in

    order = ["norm_in", "w_in", "q_norm", "w_uq", "kv_norm", "w_ukv", "pool_w", "pool_scale", "w_branch_attn",
             "w_branch_pool", "w_out", "norm_final"]
    loss = 0.5 * jnp.sum(sq_err_all) / D_MODEL
    return (loss, grad_x.reshape(x.shape),
            *[grads[n].reshape(ws[n].shape) for n in order],
            *[delta_w[n] for n in order], *[new_m[n] for n in order], *[new_v[n] for n in order])
```

```python
import functools

import jax
import jax.numpy as jnp
import numpy as np
from jax import lax
from jax.experimental import pallas as pl
from jax.experimental.pallas import tpu as pltpu

F32 = jnp.float32
BF16 = jnp.bfloat16
MESH = pl.DeviceIdType.MESH

D_MODEL = 1024
CHUNK = 64
MLA_HEADS = 8
QK_NOPE_DIM = 64
QK_ROPE_DIM = 32
V_HEAD_DIM = 64
Q_LORA_RANK = 384
KV_LORA_RANK = 256
MLA_WIDTH = MLA_HEADS * V_HEAD_DIM
ROPE_THETA = 10000.0
POOL_WINDOWS = (2, 4, 8, 16)
POOL_WIDTH = 512
POOL_GROUP_DIM = 128
BRANCH_COLS = D_MODEL // 4
FWD_HEADS = 8
BWD_HEADS = 4
POOL_HALO = 16
EPS = 1e-6
IN_TOTAL = 4256
HEAD_PAD = 128
ATT_SCALE = (QK_NOPE_DIM + QK_ROPE_DIM) ** -0.5
ATT_SCALE_LOG2E = ATT_SCALE * 1.4426950408889634

ADAM_LR = 0.001
ADAM_B1 = 0.9
ADAM_B2 = 0.999
ADAM_EPS = 1e-08
ADAM_WD = 0.01
ADAM_STEP = 10

N_CHIPS = 4
N_DEV = 8
LANES = 128
VMEM_LIMIT = 60 * 1024 * 1024

IN_SEGMENTS = ((384, 384), (256, 256), (32, HEAD_PAD), (512, 512), (512, 512), (512, 512), (2048, 2048))
SHARD_COLS = IN_TOTAL // N_CHIPS
ZQ_COLS = slice(0, 384)
ZKV_COLS = slice(384, 640)
ZKR_TILE = slice(640, 768)


def _shard_pieces():
    bounds, off = [], 0
    for w, _ in IN_SEGMENTS:
        bounds.append((off, off + w))
        off += w
    out = []
    for j in range(N_CHIPS):
        lo, hi = SHARD_COLS * j, SHARD_COLS * (j + 1)
        out.append([(i, max(lo, a) - a, min(hi, b) - a, max(lo, a) - lo)
                    for i, (a, b) in enumerate(bounds) if max(lo, a) < min(hi, b)])
    return out


SHARD_PIECES = _shard_pieces()


def _segment(z_blocks, seg):
    parts = [z_blocks[j][:, col:col + hi - lo]
             for j, pieces in enumerate(SHARD_PIECES) for sg, lo, hi, col in pieces if sg == seg]
    return parts[0] if len(parts) == 1 else jnp.concatenate(parts, axis=1)

COMM_PARAMS = (
    ("w_in", SHARD_COLS, D_MODEL, 1, 512),
    ("w_uq", 96, 768, 0, 48),
    ("w_ukv", 64, 1024, 0, 32),
    ("w_branch_attn", 512, 256, 0, 256),
    ("w_branch_pool", 512, 256, 0, 256),
    ("w_out", 256, 1024, 0, 128),
)

SMALL_MID = (
    ("pool_scale", (512,)),
    ("norm_final", (1024,)),
    ("pool_w", (4, 128, 128)),
    ("sq_err", (8, 128)),
)
SMALL_LATE = (
    ("norm_in", (1024,)),
    ("q_norm", (384,)),
    ("kv_norm", (256,)),
)


def _small_rows(shapes):
    return -(-sum(int(np.prod(s)) for _, s in shapes) // (LANES * 8)) * 8


def _first_rows(shapes):
    out, off = {}, 0
    for name, shp in shapes:
        out[name], rem = divmod(off, LANES)
        assert rem == 0, name
        off += int(np.prod(shp))
    return out


def _dot(a, b):
    return jnp.dot(a, b, preferred_element_type=F32)


def _dot_nt(a, b):
    return lax.dot_general(a, b, (((1,), (1,)), ((), ())), preferred_element_type=F32)


def _dot_tn(a, b):
    return lax.dot_general(a, b, (((0,), (0,)), ((), ())), preferred_element_type=F32)


def _sigmoid(x):
    return 1.0 / (1.0 + jnp.exp(-x))


def _colsum(x):
    return jnp.sum(x, axis=0, keepdims=True)


def _rms_fwd(x, g):
    r = lax.rsqrt(jnp.mean(x * x, axis=-1, keepdims=True) + EPS)
    xhat = x * r
    return xhat * g, xhat, r


def _rms_bwd(dy, xhat, r, g):
    dxhat = dy * g
    return r * (dxhat - xhat * jnp.mean(dxhat * xhat, axis=-1, keepdims=True))


def _rope(v, c, sa, sb):
    return v * c + pltpu.roll(v, 112, 1) * sa + pltpu.roll(v, 16, 1) * sb


def _unrope(d, c, sa, sb):
    return d * c + pltpu.roll(d * sa, 16, 1) + pltpu.roll(d * sb, 112, 1)


def _row_spec(tm, n):
    return pl.BlockSpec((tm, n), lambda i: (i, 0))


def _full_spec(shape):
    nd = len(shape)
    return pl.BlockSpec(shape, lambda i: (0,) * nd)


def _tiles(s):
    t_att = 512 if s >= 2048 else 128
    t_row = 256 if s >= 1024 else 128
    return t_att, t_row


def _inproj_fwd(order, x, norm_in, early_shards, tm):
    s = x.shape[0]
    n_tiles = s // tm
    gat = _Gather(COMM_PARAMS[:3])
    n_w = len(gat.params)
    arrival = (1, 2, 0)

    def body(order_ref, x_ref, g_ref, *rest):
        w_loc, (hn_ref, z_ref), w_all = rest[:n_w], rest[n_w:n_w + 2], rest[n_w + 2:2 * n_w + 2]
        w_vmem, hn_all, w_sem = rest[2 * n_w + 2:2 * n_w + 5]
        gat.bind(w_loc, w_all, rest[2 * n_w + 5:])
        ph, i = pl.program_id(0), pl.program_id(1)
        pl.when(jnp.logical_and(ph == 0, i == 0))(gat.start)

        def fetch(phase):
            src = w_loc[0] if phase == 0 else w_all[0].at[order_ref[phase]]
            return pltpu.make_async_copy(src, w_vmem.at[phase % 2], w_sem.at[phase % 2])

        def landed(f):
            gat.relay_one(0, arrival[f])
            gat.await_one(0, arrival[f])

        @pl.when(jnp.logical_and(ph == 0, i == 0))
        def _():
            fetch(0).start()
            fetch(0).wait()

        @pl.when(jnp.logical_and(ph == 1, i == 0))
        def _():
            landed(0)
            fetch(1).start()
            fetch(1).wait()

        for f in (1, 2):
            @pl.when(jnp.logical_and(ph == f, i == n_tiles - 1))
            def _(f=f):
                landed(f)
                fetch(f + 1).start()

            @pl.when(jnp.logical_and(ph == f + 1, i == 0))
            def _(f=f):
                fetch(f + 1).wait()

        rows = pl.ds(pl.multiple_of(i * tm, tm), tm)

        @pl.when(ph == 0)
        def _():
            hn, _, _ = _rms_fwd(x_ref[...], g_ref[...])
            hn = hn.astype(BF16)
            hn_ref[...] = hn
            hn_all[rows, :] = hn

        z_ref[0] = _dot_nt(hn_all[rows, :], w_vmem[ph % 2])

        @pl.when(jnp.logical_and(ph == N_CHIPS - 1, i == n_tiles - 1))
        def _():
            for p in range(1, n_w):
                for j in range(3):
                    gat.relay_one(p, j)
            for p in range(1, n_w):
                for j in range(3):
                    gat.await_one(p, j)
            gat.wait_sends()

    def tile_in_phase0(ph, i, order):
        return (jnp.where(ph == 0, i, n_tiles - 1), 0)

    any_spec = pl.BlockSpec(memory_space=pl.ANY)
    grid_spec = pltpu.PrefetchScalarGridSpec(
        num_scalar_prefetch=1,
        grid=(N_CHIPS, n_tiles),
        in_specs=[pl.BlockSpec((tm, D_MODEL), tile_in_phase0),
                  pl.BlockSpec((1, D_MODEL), lambda ph, i, order: (0, 0))] + [any_spec] * n_w,
        out_specs=[pl.BlockSpec((tm, D_MODEL), tile_in_phase0),
                   pl.BlockSpec((1, tm, SHARD_COLS), lambda ph, i, order: (order[ph], i, 0))] + [any_spec] * n_w,
        scratch_shapes=[pltpu.VMEM((2, SHARD_COLS, D_MODEL), BF16), pltpu.VMEM((s, D_MODEL), BF16),
                        pltpu.SemaphoreType.DMA((2,))] + gat.scratch,
    )
    out = pl.pallas_call(
        body,
        name="inproj_fwd",
        grid_spec=grid_spec,
        out_shape=[jax.ShapeDtypeStruct((s, D_MODEL), BF16), jax.ShapeDtypeStruct((N_CHIPS, s, SHARD_COLS), F32)]
        + gat.out_shape,
        compiler_params=pltpu.CompilerParams(dimension_semantics=("arbitrary", "arbitrary"),
                                             vmem_limit_bytes=VMEM_LIMIT),
    )(order, x, norm_in, *early_shards)
    return out[0], out[1], out[2:]


def _qkv_fwd(z_sh, q_norm, kv_norm, w_uq_all, w_ukv_all, rc, rsa, rsb, tm):
    s = z_sh.shape[1]
    hw = MLA_HEADS * HEAD_PAD
    qk = QK_NOPE_DIM + QK_ROPE_DIM

    def body(z_ref, gq_ref, gkv_ref, uq_ref, ukv_ref, c_ref, sa_ref, sb_ref,
             q_ref, k_ref, v_ref, qt_ref, vt_ref, wuq_ref, wk_ref, wv_ref):
        @pl.when(pl.program_id(0) == 0)
        def _():
            w_q = jnp.concatenate([uq_ref[j] for j in range(N_CHIPS)], axis=0).astype(F32)
            gap = jnp.zeros((Q_LORA_RANK, HEAD_PAD - qk), F32)
            wuq_ref[...] = jnp.concatenate(
                [part for h in range(MLA_HEADS) for part in (w_q[:, h * qk:(h + 1) * qk], gap)], axis=1).astype(BF16)
            w_kv = jnp.concatenate([ukv_ref[j] for j in range(N_CHIPS)], axis=0).astype(F32)
            lane = lax.broadcasted_iota(jnp.int32, w_kv.shape, 1)
            wk_ref[...] = jnp.where(lane % HEAD_PAD < QK_NOPE_DIM, w_kv, 0.0).astype(BF16)
            wv_ref[...] = jnp.concatenate(
                [w_kv[:, h * HEAD_PAD + QK_NOPE_DIM:(h + 1) * HEAD_PAD] for h in range(MLA_HEADS)],
                axis=1).astype(BF16)

        c, sa, sb = c_ref[...], sa_ref[...], sb_ref[...]
        z0 = z_ref[0]
        cq, _, _ = _rms_fwd(z0[:, ZQ_COLS], gq_ref[...])
        qf = _dot(cq.astype(BF16), wuq_ref[...])
        ckv, _, _ = _rms_fwd(z0[:, ZKV_COLS], gkv_ref[...])
        ckv = ckv.astype(BF16)
        kn = _dot(ckv, wk_ref[...])
        lane = lax.broadcasted_iota(jnp.int32, (tm, HEAD_PAD), 1)
        zkr = jnp.where(lane < QK_ROPE_DIM, z0[:, ZKR_TILE], 0.0)
        kr = _rope(pltpu.roll(zkr, 64, 1), c, sa, sb)
        for h in range(MLA_HEADS):
            cols = slice(h * HEAD_PAD, (h + 1) * HEAD_PAD)
            qh = _rope(qf[:, cols], c, sa, sb)
            q_ref[:, cols] = qh.astype(BF16)
            qt_ref[cols, :] = qh.T.astype(BF16)
            k_ref[:, cols] = (kn[:, cols] + kr).astype(BF16)
        vf = _dot(ckv, wv_ref[...])
        v_ref[...] = vf.astype(BF16)
        vt_ref[...] = vf.T.astype(BF16)

    return pl.pallas_call(
        body,
        name="qkv_fwd",
        grid=(s // tm,),
        in_specs=[
            pl.BlockSpec((1, tm, SHARD_COLS), lambda i: (0, i, 0)),
            _full_spec((1, Q_LORA_RANK)), _full_spec((1, KV_LORA_RANK)),
            _full_spec(w_uq_all.shape), _full_spec(w_ukv_all.shape),
            _row_spec(tm, HEAD_PAD), _row_spec(tm, HEAD_PAD), _row_spec(tm, HEAD_PAD),
        ],
        out_specs=[_row_spec(tm, hw), _row_spec(tm, hw), _row_spec(tm, MLA_WIDTH),
                   pl.BlockSpec((hw, tm), lambda i: (0, i)), pl.BlockSpec((MLA_WIDTH, tm), lambda i: (0, i)),
                   _full_spec((Q_LORA_RANK, hw)), _full_spec((KV_LORA_RANK, hw)), _full_spec((KV_LORA_RANK, MLA_WIDTH))],
        out_shape=[jax.ShapeDtypeStruct((s, hw), BF16), jax.ShapeDtypeStruct((s, hw), BF16),
                   jax.ShapeDtypeStruct((s, MLA_WIDTH), BF16),
                   jax.ShapeDtypeStruct((hw, s), BF16), jax.ShapeDtypeStruct((MLA_WIDTH, s), BF16),
                   jax.ShapeDtypeStruct((Q_LORA_RANK, hw), BF16), jax.ShapeDtypeStruct((KV_LORA_RANK, hw), BF16),
                   jax.ShapeDtypeStruct((KV_LORA_RANK, MLA_WIDTH), BF16)],
        compiler_params=pltpu.CompilerParams(dimension_semantics=("arbitrary",), vmem_limit_bytes=VMEM_LIMIT),
    )(z_sh, q_norm, kv_norm, w_uq_all, w_ukv_all, rc, rsa, rsb)


def _chunk_mask(t, keys_on_rows):
    rows = lax.broadcasted_iota(jnp.int32, (t, t), 0) // CHUNK
    cols = lax.broadcasted_iota(jnp.int32, (t, t), 1) // CHUNK
    return rows <= cols if keys_on_rows else cols <= rows


def _attn_fwd(q_t, k, v_t, late_shards, t):
    s = k.shape[0]
    groups = MLA_HEADS // FWD_HEADS
    n_q = s // t
    gat = _Gather(COMM_PARAMS[3:])
    n_w = len(gat.params)

    def body(qt_ref, k_ref, k2_ref, vt_ref, *rest):
        w_in, (o_ref, lse_ref), w_out = rest[:n_w], rest[n_w:n_w + 2], rest[n_w + 2:2 * n_w + 2]
        gat.bind(w_in, w_out, rest[2 * n_w + 2:])
        i = pl.program_id(1)
        step_no = pl.program_id(0) * n_q + i
        pl.when(step_no == 0)(gat.start)
        pl.when(step_no == groups * n_q // 2)(gat.relay)
        mask = _chunk_mask(t, True)
        qcs = [slice(hh * HEAD_PAD, (hh + 1) * HEAD_PAD) for hh in range(FWD_HEADS)]
        vcs = [slice(hh * V_HEAD_DIM, (hh + 1) * V_HEAD_DIM) for hh in range(FWD_HEADS)]
        qts = [qt_ref[qc, :] for qc in qcs]

        def step(j, carry, masked):
            keys = pl.ds(pl.multiple_of(j * t, t), t)
            out = []
            for hh in range(FWD_HEADS):
                m, l, acc = carry[hh]
                sc = _dot(k_ref[keys, qcs[hh]], qts[hh])
                if masked:
                    sc = jnp.where(mask, sc, -jnp.inf)
                m_new = jnp.maximum(m, jnp.max(sc, axis=0, keepdims=True))
                alpha = jnp.exp2((m - m_new) * ATT_SCALE_LOG2E)
                p = jnp.exp2((_dot(k2_ref[keys, qcs[hh]], qts[hh]) - m_new) * ATT_SCALE_LOG2E)
                if masked:
                    p = jnp.where(mask, p, 0.0)
                l = alpha * l + jnp.sum(p, axis=0, keepdims=True)
                acc = alpha * acc + _dot(vt_ref[vcs[hh], keys], p.astype(BF16))
                out.append((m_new, l, acc))
            return tuple(out)

        one = (jnp.full((1, t), -jnp.inf, F32), jnp.zeros((1, t), F32), jnp.zeros((V_HEAD_DIM, t), F32))
        carry = lax.fori_loop(0, i, functools.partial(step, masked=False), (one,) * FWD_HEADS)
        carry = step(i, carry, True)
        o_ref[...] = jnp.concatenate([carry[hh][2] / carry[hh][1] for hh in range(FWD_HEADS)], axis=0).T
        for hh in range(FWD_HEADS):
            m, l, _ = carry[hh]
            lse_ref[:, qcs[hh]] = jnp.broadcast_to(m * ATT_SCALE_LOG2E + jnp.log2(l), (HEAD_PAD, t)).T
        pl.when(step_no == groups * n_q - 1)(gat.finish)

    any_spec = pl.BlockSpec(memory_space=pl.ANY)
    out = pl.pallas_call(
        body,
        name="attn_fwd",
        grid=(groups, n_q),
        in_specs=[
            pl.BlockSpec((FWD_HEADS * HEAD_PAD, t), lambda p, i: (p, i)),
            pl.BlockSpec((s, FWD_HEADS * HEAD_PAD), lambda p, i: (0, p), pipeline_mode=pl.Buffered(1)),
            pl.BlockSpec((s, FWD_HEADS * HEAD_PAD), lambda p, i: (0, p), pipeline_mode=pl.Buffered(1)),
            pl.BlockSpec((FWD_HEADS * V_HEAD_DIM, s), lambda p, i: (p, 0), pipeline_mode=pl.Buffered(1)),
        ] + [any_spec] * n_w,
        out_specs=[
            pl.BlockSpec((t, FWD_HEADS * V_HEAD_DIM), lambda p, i: (i, p)),
            pl.BlockSpec((t, FWD_HEADS * HEAD_PAD), lambda p, i: (i, p)),
        ] + [any_spec] * n_w,
        out_shape=[jax.ShapeDtypeStruct((s, MLA_WIDTH), F32), jax.ShapeDtypeStruct((s, MLA_HEADS * HEAD_PAD), F32)]
        + gat.out_shape,
        scratch_shapes=gat.scratch,
        compiler_params=pltpu.CompilerParams(dimension_semantics=("arbitrary", "arbitrary"),
                                             vmem_limit_bytes=VMEM_LIMIT),
    )(q_t, k, k, v_t, *late_shards)
    return out[0], out[1], out[2:]


def _mid(o, z_sh, x, target, pool_w, pool_scale, w_ba, w_bp, w_out, norm_final, tm):
    s = x.shape[0]
    n_tiles = s // tm
    halo_per_tile = tm // POOL_HALO

    def body(o_ref, z0_ref, z1_ref, z1h_ref, z2_ref, z3_ref, x_ref, t_ref, pw_ref, ps_ref, wba_ref, wbp_ref,
             wout_ref, gf_ref,
             do_ref, dl_ref, dga_ref, dgp_ref, dgm_ref, ddc_ref, dh_ref,
             loss_ref, dwout_out, dwba_out, dwbp_out, dpw_ref, dps_ref, dgf_ref,
             ubuf, dwout_ref, dwba_ref, dwbp_ref):
        i = pl.program_id(0)

        @pl.when(i == 0)
        def _():
            loss_ref[...] = jnp.zeros_like(loss_ref)
            dwout_ref[...] = jnp.zeros_like(dwout_ref)
            dwba_ref[...] = jnp.zeros_like(dwba_ref)
            dwbp_ref[...] = jnp.zeros_like(dwbp_ref)
            dpw_ref[...] = jnp.zeros_like(dpw_ref)
            dps_ref[...] = jnp.zeros_like(dps_ref)
            dgf_ref[...] = jnp.zeros_like(dgf_ref)

        zs = [z0_ref[0], z1_ref[0], z2_ref[0], z3_ref[0]]
        o = o_ref[...]
        ga = _segment(zs, 3)
        sga = _sigmoid(ga)
        silu_a = ga * sga
        y_attn = (o * silu_a).astype(BF16)

        ubuf[0:POOL_HALO, :] = jnp.where(i > 0, _segment([None, z1h_ref[0]], 4), 0.0)
        ubuf[POOL_HALO:, :] = _segment(zs, 4)
        row = lax.broadcasted_iota(jnp.int32, (tm, POOL_GROUP_DIM), 0) + i * tm
        ps = ps_ref[...]
        gp = _segment(zs, 5)
        sgp = _sigmoid(gp)
        silu_p = gp * sgp
        d_bf, dm, inv_cnt = [], [], []
        for g, w in enumerate(POOL_WINDOWS):
            cols = slice(g * POOL_GROUP_DIM, (g + 1) * POOL_GROUP_DIM)
            wsum = ubuf[POOL_HALO:, cols]
            for kk in range(1, w):
                wsum = wsum + ubuf[POOL_HALO - kk:POOL_HALO - kk + tm, cols]
            inv = 1.0 / jnp.minimum(row + 1, w).astype(F32)
            dg = (wsum * inv - ubuf[POOL_HALO:, cols]).astype(BF16)
            d_bf.append(dg)
            inv_cnt.append(inv)
            dm.append(_dot(dg, pw_ref[g]))
        dm = jnp.concatenate(dm, axis=1)
        yp = dm * ps
        y_pool = (yp * silu_p).astype(BF16)

        a = jnp.concatenate([_dot(y_attn, wba_ref[j]) for j in range(N_CHIPS)], axis=1)
        p = jnp.concatenate([_dot(y_pool, wbp_ref[j]) for j in range(N_CHIPS)], axis=1)
        gm = _segment(zs, 6)
        gate_a = _sigmoid(gm[:, :D_MODEL])
        gate_p = _sigmoid(gm[:, D_MODEL:])
        merged = (gate_a * a + gate_p * p).astype(BF16)
        h = x_ref[...] + _dot(merged, wout_ref[...])
        gf = gf_ref[...]
        y, xhat, r = _rms_fwd(h, gf)
        err = y - t_ref[...]
        e2 = err * err
        e2 = jnp.sum(e2.reshape(tm // 8, 8, D_MODEL), axis=0)
        acc = e2[:, 0:LANES]
        for cidx in range(1, D_MODEL // LANES):
            acc = acc + e2[:, cidx * LANES:(cidx + 1) * LANES]
        loss_ref[...] += acc

        dy = err * (1.0 / D_MODEL)
        dgf_ref[...] += _colsum(dy * xhat)
        dh = _rms_bwd(dy, xhat, r, gf)
        dh_ref[...] = dh
        dh_bf = dh.astype(BF16)
        dwout_ref[...] += _dot_tn(merged, dh_bf)
        dmerged = _dot_nt(dh_bf, wout_ref[...])
        da = (dmerged * gate_a).astype(BF16)
        dp = (dmerged * gate_p).astype(BF16)
        dgm_ref[:, :D_MODEL] = (dmerged * a * gate_a * (1.0 - gate_a)).astype(BF16)
        dgm_ref[:, D_MODEL:] = (dmerged * p * gate_p * (1.0 - gate_p)).astype(BF16)
        dy_attn = dy_pool = None
        for j in range(N_CHIPS):
            cols = slice(j * BRANCH_COLS, (j + 1) * BRANCH_COLS)
            dwba_ref[j] += _dot_tn(y_attn, da[:, cols])
            dwbp_ref[j] += _dot_tn(y_pool, dp[:, cols])
            pa = _dot_nt(da[:, cols], wba_ref[j])
            pp = _dot_nt(dp[:, cols], wbp_ref[j])
            dy_attn = pa if dy_attn is None else dy_attn + pa
            dy_pool = pp if dy_pool is None else dy_pool + pp

        do = dy_attn * silu_a
        do_ref[...] = do
        dga_ref[...] = (dy_attn * o * (sga * (1.0 + ga * (1.0 - sga)))).astype(BF16)
        doo = do * o
        for hd in range(MLA_HEADS):
            dl = jnp.sum(doo[:, hd * V_HEAD_DIM:(hd + 1) * V_HEAD_DIM], axis=1, keepdims=True)
            dl_ref[:, hd * HEAD_PAD:(hd + 1) * HEAD_PAD] = jnp.broadcast_to(dl, (tm, HEAD_PAD))

        dyp = dy_pool * silu_p
        dgp_ref[...] = (dy_pool * yp * (sgp * (1.0 + gp * (1.0 - sgp)))).astype(BF16)
        dps_ref[...] += _colsum(dyp * dm)
        dmm = (dyp * ps).astype(BF16)
        for g in range(len(POOL_WINDOWS)):
            cols = slice(g * POOL_GROUP_DIM, (g + 1) * POOL_GROUP_DIM)
            dpw_ref[g] += _dot_tn(d_bf[g], dmm[:, cols])
            ddc_ref[:, cols] = _dot_nt(dmm[:, cols], pw_ref[g]) * inv_cnt[g]

        @pl.when(i == n_tiles - 1)
        def _():
            dwout_out[...] = dwout_ref[...].astype(BF16)
            dwba_out[...] = dwba_ref[...].astype(BF16)
            dwbp_out[...] = dwbp_ref[...].astype(BF16)

    row_in = lambda n: _row_spec(tm, n)
    in_specs = [
        row_in(MLA_WIDTH),
        pl.BlockSpec((1, tm, SHARD_COLS), lambda i: (0, i, 0)), pl.BlockSpec((1, tm, SHARD_COLS), lambda i: (1, i, 0)),
        pl.BlockSpec((1, POOL_HALO, SHARD_COLS), lambda i: (1, jnp.maximum(i * halo_per_tile - 1, 0), 0)),
        pl.BlockSpec((1, tm, SHARD_COLS), lambda i: (2, i, 0)), pl.BlockSpec((1, tm, SHARD_COLS), lambda i: (3, i, 0)),
        row_in(D_MODEL), row_in(D_MODEL),
        _full_spec((4, POOL_GROUP_DIM, POOL_GROUP_DIM)), _full_spec((1, POOL_WIDTH)),
        _full_spec((N_CHIPS, MLA_WIDTH, BRANCH_COLS)), _full_spec((N_CHIPS, POOL_WIDTH, BRANCH_COLS)),
        _full_spec((D_MODEL, D_MODEL)), _full_spec((1, D_MODEL)),
    ]
    out_shape = [
        jax.ShapeDtypeStruct((s, MLA_WIDTH), F32),
        jax.ShapeDtypeStruct((s, MLA_HEADS * HEAD_PAD), F32),
        jax.ShapeDtypeStruct((s, MLA_WIDTH), BF16),
        jax.ShapeDtypeStruct((s, POOL_WIDTH), BF16),
        jax.ShapeDtypeStruct((s, 2 * D_MODEL), BF16),
        jax.ShapeDtypeStruct((s, POOL_WIDTH), F32),
        jax.ShapeDtypeStruct((s, D_MODEL), F32),
        jax.ShapeDtypeStruct((8, LANES), F32),
        jax.ShapeDtypeStruct((D_MODEL, D_MODEL), BF16),
        jax.ShapeDtypeStruct((N_CHIPS, MLA_WIDTH, BRANCH_COLS), BF16),
        jax.ShapeDtypeStruct((N_CHIPS, POOL_WIDTH, BRANCH_COLS), BF16),
        jax.ShapeDtypeStruct((4, POOL_GROUP_DIM, POOL_GROUP_DIM), F32),
        jax.ShapeDtypeStruct((1, POOL_WIDTH), F32),
        jax.ShapeDtypeStruct((1, D_MODEL), F32),
    ]
    out_specs = [
        row_in(MLA_WIDTH), row_in(MLA_HEADS * HEAD_PAD), row_in(MLA_WIDTH), row_in(POOL_WIDTH),
        row_in(2 * D_MODEL), row_in(POOL_WIDTH), row_in(D_MODEL),
        _full_spec((8, LANES)), _full_spec((D_MODEL, D_MODEL)), _full_spec((N_CHIPS, MLA_WIDTH, BRANCH_COLS)),
        _full_spec((N_CHIPS, POOL_WIDTH, BRANCH_COLS)), _full_spec((4, POOL_GROUP_DIM, POOL_GROUP_DIM)),
        _full_spec((1, POOL_WIDTH)), _full_spec((1, D_MODEL)),
    ]
    return pl.pallas_call(
        body,
        name="mid",
        grid=(n_tiles,),
        in_specs=in_specs,
        out_specs=out_specs,
        out_shape=out_shape,
        scratch_shapes=[
            pltpu.VMEM((tm + POOL_HALO, POOL_WIDTH), F32),
            pltpu.VMEM((D_MODEL, D_MODEL), F32),
            pltpu.VMEM((N_CHIPS, MLA_WIDTH, BRANCH_COLS), F32),
            pltpu.VMEM((N_CHIPS, POOL_WIDTH, BRANCH_COLS), F32),
        ],
        compiler_params=pltpu.CompilerParams(dimension_semantics=("arbitrary",), vmem_limit_bytes=VMEM_LIMIT),
    )(o, z_sh, z_sh, z_sh, z_sh, z_sh, x, target, pool_w, pool_scale, w_ba, w_bp, w_out, norm_final)


def _attn_bwd(q, q_t, k, v, do, lse, delta, late_grads, gs_mid, t):
    s = q.shape[0]
    groups = MLA_HEADS // BWD_HEADS
    n_q = s // t
    red = _Reduce(COMM_PARAMS[3:])
    n_w = len(red.params)
    small = _SmallSum(gs_mid.shape[0])
    n_red = len(red.scratch)

    def body(q_ref, qt_ref, do_ref, lse_ref, dl_ref, k_ref, v_ref, *rest):
        g_in, gs_ref = rest[:n_w], rest[n_w]
        (dq_ref, dk_ref, dv_ref), g_out, gsum_ref = rest[n_w + 1:n_w + 4], rest[n_w + 4:2 * n_w + 4], rest[2 * n_w + 4]
        scratch = rest[2 * n_w + 5:]
        red.bind(g_in, g_out, scratch[:n_red])
        small.bind(gs_ref, gsum_ref, scratch[n_red:])
        i = pl.program_id(1)
        step_no = pl.program_id(0) * n_q + i

        @pl.when(step_no == 0)
        def _():
            red.start()
            small.start()

        pl.when(step_no == groups * n_q // 2)(red.exchange)

        @pl.when(i == 0)
        def _():
            dk_ref[...] = jnp.zeros_like(dk_ref)
            dv_ref[...] = jnp.zeros_like(dv_ref)

        mask = _chunk_mask(t, False)
        qcs = [slice(hh * HEAD_PAD, (hh + 1) * HEAD_PAD) for hh in range(BWD_HEADS)]
        vcs = [slice(hh * V_HEAD_DIM, (hh + 1) * V_HEAD_DIM) for hh in range(BWD_HEADS)]
        qhs = [q_ref[:, qc] for qc in qcs]
        qts = [qt_ref[qc, :] for qc in qcs]
        dohs = [do_ref[:, vc].astype(BF16) for vc in vcs]
        do_t = do_ref[...].T.astype(BF16)
        dots = [do_t[vc, :] for vc in vcs]
        lses = [jnp.tile(lse_ref[:, qc], (1, t // HEAD_PAD)) for qc in qcs]
        dls = [jnp.tile(dl_ref[:, qc], (1, t // HEAD_PAD)) for qc in qcs]

        def step(j, dqs, masked):
            keys = pl.ds(pl.multiple_of(j * t, t), t)
            out = []
            for hh in range(BWD_HEADS):
                kj = k_ref[keys, qcs[hh]]
                vj = v_ref[keys, vcs[hh]]
                p = jnp.exp2(_dot_nt(qhs[hh], kj) * ATT_SCALE_LOG2E - lses[hh])
                if masked:
                    p = jnp.where(mask, p, 0.0)
                ds = (p * (_dot_nt(dohs[hh], vj) - dls[hh])).astype(BF16)
                dv_ref[vcs[hh], keys] += _dot(dots[hh], p.astype(BF16))
                dk_ref[qcs[hh], keys] += _dot(qts[hh], ds) * ATT_SCALE
                out.append(dqs[hh] + _dot(ds, kj))
            return tuple(out)

        zero = jnp.zeros((t, HEAD_PAD), F32)
        dqs = lax.fori_loop(0, i, functools.partial(step, masked=False), (zero,) * BWD_HEADS)
        dqs = step(i, dqs, True)
        for hh in range(BWD_HEADS):
            dq_ref[:, qcs[hh]] = dqs[hh] * ATT_SCALE

        @pl.when(step_no == groups * n_q - 1)
        def _():
            red.finish()
            small.finish()

    hw = MLA_HEADS * HEAD_PAD
    any_spec = pl.BlockSpec(memory_space=pl.ANY)
    out = pl.pallas_call(
        body,
        name="attn_bwd",
        grid=(groups, n_q),
        in_specs=[
            pl.BlockSpec((t, BWD_HEADS * HEAD_PAD), lambda p, i: (i, p)),
            pl.BlockSpec((BWD_HEADS * HEAD_PAD, t), lambda p, i: (p, i)),
            pl.BlockSpec((t, BWD_HEADS * V_HEAD_DIM), lambda p, i: (i, p)),
            pl.BlockSpec((t, BWD_HEADS * HEAD_PAD), lambda p, i: (i, p)),
            pl.BlockSpec((t, BWD_HEADS * HEAD_PAD), lambda p, i: (i, p)),
            pl.BlockSpec((s, BWD_HEADS * HEAD_PAD), lambda p, i: (0, p), pipeline_mode=pl.Buffered(1)),
            pl.BlockSpec((s, BWD_HEADS * V_HEAD_DIM), lambda p, i: (0, p), pipeline_mode=pl.Buffered(1)),
        ] + [any_spec] * n_w + [pl.BlockSpec(small.spec_shape, lambda p, i: (0, 0))],
        out_specs=[
            pl.BlockSpec((t, BWD_HEADS * HEAD_PAD), lambda p, i: (i, p)),
            pl.BlockSpec((BWD_HEADS * HEAD_PAD, s), lambda p, i: (p, 0)),
            pl.BlockSpec((BWD_HEADS * V_HEAD_DIM, s), lambda p, i: (p, 0)),
        ] + [any_spec] * n_w + [pl.BlockSpec(small.spec_shape, lambda p, i: (0, 0))],
        out_shape=[jax.ShapeDtypeStruct((s, hw), F32), jax.ShapeDtypeStruct((hw, s), F32),
                   jax.ShapeDtypeStruct((MLA_WIDTH, s), F32)] + red.out_shape + [small.out_shape],
        scratch_shapes=red.scratch + small.scratch,
        compiler_params=pltpu.CompilerParams(dimension_semantics=("arbitrary", "arbitrary"),
                                             vmem_limit_bytes=VMEM_LIMIT),
    )(q, q_t, do, lse, delta, k, v, *late_grads, gs_mid)
    return out[0], out[1], out[2], out[3:3 + n_w], out[3 + n_w]


def _qkv_bwd(dq, dk_t, dv_t, z_sh, q_norm, kv_norm, wuq_p, wk_p, wv, rc, rsa, rsb, tm):
    s = z_sh.shape[1]
    hw = MLA_HEADS * HEAD_PAD
    n_tiles = s // tm
    uq_shape, ukv_shape = (N_CHIPS,) + COMM_PARAMS[1][1:3], (N_CHIPS,) + COMM_PARAMS[2][1:3]

    def body(dq_ref, dk_ref, dv_ref, z_ref, gq_ref, gkv_ref, wuq_ref, wk_ref, wv_ref,
             c_ref, sa_ref, sb_ref,
             dzq_ref, dzkv_ref, dzkr_ref, duq_ref, dukv_ref, dgq_ref, dgkv_ref, dwuq_ref, dwk_ref, dwv_ref):
        i = pl.program_id(0)

        @pl.when(i == 0)
        def _():
            dwuq_ref[...] = jnp.zeros_like(dwuq_ref)
            dwk_ref[...] = jnp.zeros_like(dwk_ref)
            dwv_ref[...] = jnp.zeros_like(dwv_ref)
            dgq_ref[...] = jnp.zeros_like(dgq_ref)
            dgkv_ref[...] = jnp.zeros_like(dgkv_ref)

        c, sa, sb = c_ref[...], sa_ref[...], sb_ref[...]
        gq, gkv = gq_ref[...], gkv_ref[...]

        z0 = z_ref[0]
        cq, xq, rq = _rms_fwd(z0[:, ZQ_COLS], gq)
        dqp = jnp.concatenate(
            [_unrope(dq_ref[:, h * HEAD_PAD:(h + 1) * HEAD_PAD], c, sa, sb) for h in range(MLA_HEADS)],
            axis=1).astype(BF16)
        dwuq_ref[...] += _dot_tn(cq.astype(BF16), dqp)
        dcq = _dot_nt(dqp, wuq_ref[...])
        dgq_ref[...] += _colsum(dcq * xq)
        dzq_ref[...] = _rms_bwd(dcq, xq, rq, gq).astype(BF16)

        ckv, xkv, rkv = _rms_fwd(z0[:, ZKV_COLS], gkv)
        ckv = ckv.astype(BF16)
        dkf = dk_ref[...].T
        dk_bf = dkf.astype(BF16)
        dv_bf = dv_ref[...].T.astype(BF16)
        dwk_ref[...] += _dot_tn(ckv, dk_bf)
        dwv_ref[...] += _dot_tn(ckv, dv_bf)
        dckv = _dot_nt(dk_bf, wk_ref[...]) + _dot_nt(dv_bf, wv_ref[...])
        dgkv_ref[...] += _colsum(dckv * xkv)
        dzkv_ref[...] = _rms_bwd(dckv, xkv, rkv, gkv).astype(BF16)

        dkr = dkf[:, 0:HEAD_PAD]
        for h in range(1, MLA_HEADS):
            dkr = dkr + dkf[:, h * HEAD_PAD:(h + 1) * HEAD_PAD]
        dkr = pltpu.roll(_unrope(dkr, c, sa, sb), 64, 1)
        lane = lax.broadcasted_iota(jnp.int32, (tm, HEAD_PAD), 1)
        dzkr_ref[...] = jnp.where(lane < QK_ROPE_DIM, dkr, 0.0).astype(BF16)

        @pl.when(i == n_tiles - 1)
        def _():
            qk = QK_NOPE_DIM + QK_ROPE_DIM
            d_uq = jnp.concatenate([dwuq_ref[:, h * HEAD_PAD:h * HEAD_PAD + qk] for h in range(MLA_HEADS)],
                                   axis=1).astype(BF16)
            d_ukv = jnp.concatenate(
                [part for h in range(MLA_HEADS)
                 for part in (dwk_ref[:, h * HEAD_PAD:h * HEAD_PAD + QK_NOPE_DIM],
                              dwv_ref[:, h * V_HEAD_DIM:(h + 1) * V_HEAD_DIM])], axis=1).astype(BF16)
            for j in range(N_CHIPS):
                duq_ref[j] = d_uq[j * uq_shape[1]:(j + 1) * uq_shape[1]]
                dukv_ref[j] = d_ukv[j * ukv_shape[1]:(j + 1) * ukv_shape[1]]

    return pl.pallas_call(
        body,
        name="qkv_bwd",
        grid=(s // tm,),
        in_specs=[
            _row_spec(tm, hw), pl.BlockSpec((hw, tm), lambda i: (0, i)), pl.BlockSpec((MLA_WIDTH, tm), lambda i: (0, i)),
            pl.BlockSpec((1, tm, SHARD_COLS), lambda i: (0, i, 0)),
            _full_spec((1, Q_LORA_RANK)), _full_spec((1, KV_LORA_RANK)),
            _full_spec((Q_LORA_RANK, hw)), _full_spec((KV_LORA_RANK, hw)), _full_spec((KV_LORA_RANK, MLA_WIDTH)),
            _row_spec(tm, HEAD_PAD), _row_spec(tm, HEAD_PAD), _row_spec(tm, HEAD_PAD),
        ],
        out_specs=[
            _row_spec(tm, Q_LORA_RANK), _row_spec(tm, KV_LORA_RANK), _row_spec(tm, HEAD_PAD),
            _full_spec(uq_shape), _full_spec(ukv_shape),
            _full_spec((1, Q_LORA_RANK)), _full_spec((1, KV_LORA_RANK)),
        ],
        out_shape=[
            jax.ShapeDtypeStruct((s, Q_LORA_RANK), BF16), jax.ShapeDtypeStruct((s, KV_LORA_RANK), BF16),
            jax.ShapeDtypeStruct((s, HEAD_PAD), BF16),
            jax.ShapeDtypeStruct(uq_shape, BF16), jax.ShapeDtypeStruct(ukv_shape, BF16),
            jax.ShapeDtypeStruct((1, Q_LORA_RANK), F32), jax.ShapeDtypeStruct((1, KV_LORA_RANK), F32),
        ],
        scratch_shapes=[pltpu.VMEM((Q_LORA_RANK, hw), F32), pltpu.VMEM((KV_LORA_RANK, hw), F32),
                        pltpu.VMEM((KV_LORA_RANK, MLA_WIDTH), F32)],
        compiler_params=pltpu.CompilerParams(dimension_semantics=("arbitrary",), vmem_limit_bytes=VMEM_LIMIT),
    )(dq, dk_t, dv_t, z_sh, q_norm, kv_norm, wuq_p, wk_p, wv, rc, rsa, rsb)


def _inproj_bwd_x(dzq, dzkv, dzkr, dgattn, ddc, dgpool, dgmerge, x, dh, norm_in, w_in_t, tm):
    s = x.shape[0]
    n_tiles = s // tm
    halo_per_tile = tm // POOL_HALO
    n_halo = s // POOL_HALO
    u_seg = 4

    def body(dzq_ref, dzkv_ref, dzkr_ref, dga_ref, ddc_ref, ddn_ref, dgp_ref, dgm_ref, x_ref, dh_ref,
             g_ref, w_hbm, gx_ref, dgin_ref, dzs_ref, w_vmem, dbuf, sem):
        i = pl.program_id(0)

        @pl.when(i == 0)
        def _():
            cp = pltpu.make_async_copy(w_hbm, w_vmem, sem)
            cp.start()
            dgin_ref[...] = jnp.zeros_like(dgin_ref)
            cp.wait()

        dbuf[0:tm, :] = ddc_ref[...]
        dbuf[tm:, :] = jnp.where(i < n_tiles - 1, ddn_ref[...], 0.0)
        row = lax.broadcasted_iota(jnp.int32, (tm, POOL_GROUP_DIM), 0) + i * tm
        du = []
        for g, w in enumerate(POOL_WINDOWS):
            cols = slice(g * POOL_GROUP_DIM, (g + 1) * POOL_GROUP_DIM)
            fsum = dbuf[0:tm, cols]
            for kk in range(1, w):
                fsum = fsum + dbuf[kk:kk + tm, cols]
            du.append(fsum - dbuf[0:tm, cols] * jnp.minimum(row + 1, w).astype(F32))
        du = jnp.concatenate(du, axis=1).astype(BF16)

        dz = [dzq_ref[...], dzkv_ref[...], dzkr_ref[...], dga_ref[...], du, dgp_ref[...], dgm_ref[...]]
        dz = jnp.concatenate([d[:, :w] for d, (w, _) in zip(dz, IN_SEGMENTS)], axis=1)
        for j in range(N_CHIPS):
            dzs_ref[j] = dz[:, j * SHARD_COLS:(j + 1) * SHARD_COLS].T
        dhn = _dot(dz, w_vmem[...])

        g = g_ref[...]
        _, xhat, r = _rms_fwd(x_ref[...], g)
        dgin_ref[...] += _colsum(dhn * xhat)
        gx_ref[...] = dh_ref[...] + _rms_bwd(dhn, xhat, r, g)

    any_spec = pl.BlockSpec(memory_space=pl.ANY)
    seg_w = [wide for _, wide in IN_SEGMENTS]
    return pl.pallas_call(
        body,
        name="inproj_bwd_x",
        grid=(n_tiles,),
        in_specs=[
            _row_spec(tm, seg_w[0]), _row_spec(tm, seg_w[1]), _row_spec(tm, seg_w[2]),
            _row_spec(tm, seg_w[3]), _row_spec(tm, seg_w[u_seg]),
            pl.BlockSpec((POOL_HALO, POOL_WIDTH), lambda i: (jnp.minimum((i + 1) * halo_per_tile, n_halo - 1), 0)),
            _row_spec(tm, seg_w[5]), _row_spec(tm, seg_w[6]),
            _row_spec(tm, D_MODEL), _row_spec(tm, D_MODEL),
            _full_spec((1, D_MODEL)), any_spec,
        ],
        out_specs=[_row_spec(tm, D_MODEL), _full_spec((1, D_MODEL)),
                   pl.BlockSpec((N_CHIPS, SHARD_COLS, tm), lambda i: (0, 0, i))],
        out_shape=[jax.ShapeDtypeStruct((s, D_MODEL), F32), jax.ShapeDtypeStruct((1, D_MODEL), F32),
                   jax.ShapeDtypeStruct((N_CHIPS, SHARD_COLS, s), BF16)],
        scratch_shapes=[
            pltpu.VMEM((IN_TOTAL, D_MODEL), BF16),
            pltpu.VMEM((tm + POOL_HALO, POOL_WIDTH), F32),
            pltpu.SemaphoreType.DMA,
        ],
        compiler_params=pltpu.CompilerParams(dimension_semantics=("arbitrary",), vmem_limit_bytes=VMEM_LIMIT),
    )(dzq, dzkv, dzkr, dgattn, ddc, ddc, dgpool, dgmerge, x, dh, norm_in, w_in_t.reshape(IN_TOTAL, D_MODEL))


def _inproj_bwd_w(order, dz_sh, hn, g_uq, g_ukv, gs, tm):
    s = hn.shape[0]
    n_tiles = s // tm
    hc = D_MODEL // 2
    red = _Reduce(COMM_PARAMS[1:3])
    small = _SmallSum(gs.shape[0])
    n_red = len(red.scratch)

    def body(order_ref, dz_ref, hn_ref, guq_hbm, gukv_hbm, gs_ref, gw_hbm, guq_out, gukv_out, gsum_ref,
             acc, pm_w, a_w, b_w, r_w, w_send, w_recv, w_local, *more_scratch):
        ph, i = pl.program_id(0), pl.program_id(1)
        x, y, c = lax.axis_index("x"), lax.axis_index("y"), lax.axis_index("c")
        k = 2 * x + y
        me, sibling = (x, y, c), (x, y, 1 - c)
        chips = _other_chips(x, y)
        shard_of_phase = [2 * cx + cy for cx, cy in chips] + [k]
        copy = _remote_copier(w_send, w_recv)
        red.bind([guq_hbm, gukv_hbm], [guq_out, gukv_out], more_scratch[:n_red])
        small.bind(gs_ref, gsum_ref, more_scratch[n_red:])
        mine = pl.ds(pl.multiple_of(c * hc, hc), hc)
        theirs = pl.ds(pl.multiple_of((1 - c) * hc, hc), hc)

        def to_sibling(f):
            j = shard_of_phase[f]
            return copy(f, pm_w.at[j, 1 - c], a_w.at[j], sibling)

        def pair_sum(f):
            cx, cy = chips[f]
            return copy(4 + f, pm_w.at[shard_of_phase[f], c], b_w.at[f], (cx, cy, c))

        def finished():
            return copy(7, r_w, gw_hbm.at[:, mine], sibling)

        @pl.when(jnp.logical_and(ph == 0, i == 0))
        def _():
            red.start()
            small.start()

        part = _dot(dz_ref[0], hn_ref[...])

        @pl.when(i == 0)
        def _():
            acc[...] = part

        @pl.when(i > 0)
        def _():
            acc[...] += part

        for f in range(3):
            @pl.when(jnp.logical_and(ph == f + 1, i == 0))
            def _(f=f):
                j = shard_of_phase[f]
                copy(f, a_w.at[j], a_w.at[j], me).wait_recv()
                pm_w[j, c] = (pm_w[j, c].astype(F32) + a_w[j].astype(F32)).astype(BF16)
                pair_sum(f).start()
                if f == 0:
                    red.exchange()

        for f in range(4):
            @pl.when(jnp.logical_and(ph == f, i == n_tiles - 1))
            def _(f=f):
                j = shard_of_phase[f]
                pm_w[j, 0] = acc[:, :hc].astype(BF16)
                pm_w[j, 1] = acc[:, hc:].astype(BF16)
                to_sibling(f).start()
                if f < 3:
                    return
                copy(3, a_w.at[k], a_w.at[k], me).wait_recv()
                r_w[...] = pm_w[k, c].astype(F32) + a_w[k].astype(F32)
                for g in range(3):
                    copy(4 + g, b_w.at[g], b_w.at[g], me).wait_recv()
                    r_w[...] = r_w[...] + b_w[g].astype(F32)
                store = pltpu.make_async_copy(r_w, gw_hbm.at[:, mine], w_local)
                store.start()
                finished().start()
                red.finish()
                small.finish()
                copy(7, gw_hbm.at[:, theirs], gw_hbm.at[:, theirs], me).wait_recv()
                store.wait()
                for g in range(4):
                    to_sibling(g).wait_send()
                for g in range(3):
                    pair_sum(g).wait_send()
                finished().wait_send()

    any_spec = pl.BlockSpec(memory_space=pl.ANY)
    n_sem = 8
    grid_spec = pltpu.PrefetchScalarGridSpec(
        num_scalar_prefetch=1,
        grid=(N_CHIPS, n_tiles),
        in_specs=[
            pl.BlockSpec((1, SHARD_COLS, tm), lambda ph, i, order: (order[ph], 0, i)),
            pl.BlockSpec((tm, D_MODEL), lambda ph, i, order: (i, 0)),
            any_spec, any_spec,
            pl.BlockSpec(small.spec_shape, lambda ph, i, order: (0, 0)),
        ],
        out_specs=[any_spec, any_spec, any_spec, pl.BlockSpec(small.spec_shape, lambda ph, i, order: (0, 0))],
        scratch_shapes=[
            pltpu.VMEM((SHARD_COLS, D_MODEL), F32),
            pltpu.VMEM((N_CHIPS, 2, SHARD_COLS, hc), BF16),
            pltpu.VMEM((N_CHIPS, SHARD_COLS, hc), BF16),
            pltpu.VMEM((3, SHARD_COLS, hc), BF16),
            pltpu.VMEM((SHARD_COLS, hc), F32),
            pltpu.SemaphoreType.DMA((n_sem,)), pltpu.SemaphoreType.DMA((n_sem,)), pltpu.SemaphoreType.DMA,
        ] + red.scratch + small.scratch,
    )
    out = pl.pallas_call(
        body,
        name="inproj_bwd_w",
        grid_spec=grid_spec,
        out_shape=[jax.ShapeDtypeStruct((SHARD_COLS, D_MODEL), F32)] + red.out_shape
        + [small.out_shape],
        compiler_params=pltpu.CompilerParams(dimension_semantics=("arbitrary", "arbitrary"),
                                             vmem_limit_bytes=VMEM_LIMIT),
    )(order, dz_sh, hn, g_uq, g_ukv, gs)
    return out[0], out[1], out[2], out[3]


def _other_chips(x, y):
    return ((1 - x, 1 - y), (1 - x, y), (x, 1 - y))


def _half(ref, axis, size, c, lead=()):
    window = pl.ds(pl.multiple_of(c * size, size), size)
    if axis == 0:
        return ref.at[(*lead, window, slice(None))]
    return ref.at[(*lead, slice(None), window)]


def _half_shape(rows, cols, axis, size):
    return (size, cols) if axis == 0 else (rows, size)


def _remote_copier(send_sems, recv_sems):
    def copy(sem, src, dst, to):
        return pltpu.make_async_remote_copy(src_ref=src, dst_ref=dst, send_sem=send_sems.at[sem],
                                            recv_sem=recv_sems.at[sem], device_id=to, device_id_type=MESH)
    return copy


class _Gather:
    def __init__(self, params):
        self.params = params
        n = len(params)
        self.scratch = [pltpu.SemaphoreType.DMA((6 * n,)), pltpu.SemaphoreType.DMA((6 * n,)),
                        pltpu.SemaphoreType.DMA((n,))]
        self.out_shape = [jax.ShapeDtypeStruct((N_CHIPS, r, cc), BF16) for _, r, cc, _, _ in params]

    def bind(self, ins, outs, scratch):
        self.ins, self.outs = ins, outs
        send_sems, recv_sems, self.local_sems = scratch
        self.copy = _remote_copier(send_sems, recv_sems)
        self.x, self.y, self.c = lax.axis_index("x"), lax.axis_index("y"), lax.axis_index("c")
        self.k = 2 * self.x + self.y
        self.chips = _other_chips(self.x, self.y)

    def _local(self, p):
        return pltpu.make_async_copy(self.ins[p], self.outs[p].at[self.k], self.local_sems.at[p])

    def _first(self, p, j):
        _, _, _, axis, size = self.params[p]
        cx, cy = self.chips[j]
        return self.copy(6 * p + j, _half(self.ins[p], axis, size, self.c),
                         _half(self.outs[p], axis, size, self.c, (self.k,)), (cx, cy, self.c))

    def _relay(self, p, j, half_of):
        _, _, _, axis, size = self.params[p]
        cx, cy = self.chips[j]
        block = _half(self.outs[p], axis, size, half_of, (2 * cx + cy,))
        return self.copy(6 * p + 3 + j, block, block, (self.x, self.y, 1 - self.c))

    def start(self):
        for p in range(len(self.params)):
            self._local(p).start()
            for j in (1, 2, 0):
                self._first(p, j).start()

    def relay_one(self, p, j):
        _, _, _, axis, size = self.params[p]
        cx, cy = self.chips[j]
        landed = _half(self.outs[p], axis, size, self.c, (2 * cx + cy,))
        self.copy(6 * p + j, landed, landed, (self.x, self.y, self.c)).wait_recv()
        self._relay(p, j, self.c).start()

    def await_one(self, p, j):
        self._relay(p, j, 1 - self.c).wait_recv()

    def wait_sends(self):
        for p in range(len(self.params)):
            for j in range(3):
                self._first(p, j).wait_send()
                self._relay(p, j, self.c).wait_send()
            self._local(p).wait()

    def relay(self):
        for j in range(3):
            for p in range(len(self.params)):
                self.relay_one(p, j)

    def finish(self):
        for j in range(3):
            for p in range(len(self.params)):
                self.await_one(p, j)
        self.wait_sends()


class _Reduce:
    def __init__(self, params):
        self.params = params
        n = len(params)
        halves = [_half_shape(r, cc, axis, size) for _, r, cc, axis, size in params]
        self.scratch = ([pltpu.VMEM((N_CHIPS, *h), BF16) for h in halves]
                        + [pltpu.VMEM((N_CHIPS, *h), BF16) for h in halves]
                        + [pltpu.VMEM((3, *h), BF16) for h in halves]
                        + [pltpu.VMEM(h, F32) for h in halves]
                        + [pltpu.SemaphoreType.DMA((5 * n,)), pltpu.SemaphoreType.DMA((5 * n,)),
                           pltpu.SemaphoreType.DMA((2 * n,))])
        self.out_shape = [jax.ShapeDtypeStruct((r, cc), F32) for _, r, cc, _, _ in params]

    def bind(self, g_in, g_out, scratch):
        n = len(self.params)
        self.g_in, self.g_out = g_in, g_out
        self.pm, self.a_buf = scratch[0:n], scratch[n:2 * n]
        self.b_buf, self.r_buf = scratch[2 * n:3 * n], scratch[3 * n:4 * n]
        send_sems, recv_sems, self.local_sems = scratch[4 * n:]
        self.copy = _remote_copier(send_sems, recv_sems)
        self.x, self.y, self.c = lax.axis_index("x"), lax.axis_index("y"), lax.axis_index("c")
        self.k = 2 * self.x + self.y
        self.chips = _other_chips(self.x, self.y)
        self.me = (self.x, self.y, self.c)
        self.sibling = (self.x, self.y, 1 - self.c)

    def _load(self, p):
        _, _, _, axis, size = self.params[p]
        return pltpu.make_async_copy(_half(self.g_in[p], axis, size, self.c, (slice(None),)), self.pm[p],
                                     self.local_sems.at[p])

    def _to_sibling(self, p):
        _, _, _, axis, size = self.params[p]
        return self.copy(5 * p, _half(self.g_in[p], axis, size, 1 - self.c, (slice(None),)), self.a_buf[p],
                         self.sibling)

    def _pair_sum(self, p, j):
        cx, cy = self.chips[j]
        return self.copy(5 * p + 1 + j, self.pm[p].at[2 * cx + cy], self.b_buf[p].at[j], (cx, cy, self.c))

    def _store(self, p):
        _, _, _, axis, size = self.params[p]
        n = len(self.params)
        return pltpu.make_async_copy(self.r_buf[p], _half(self.g_out[p], axis, size, self.c),
                                     self.local_sems.at[n + p])

    def _finished(self, p):
        _, _, _, axis, size = self.params[p]
        return self.copy(5 * p + 4, self.r_buf[p], _half(self.g_out[p], axis, size, self.c), self.sibling)

    def start(self):
        for p in range(len(self.params)):
            self._load(p).start()
            self._to_sibling(p).start()

    def exchange(self):
        for p in range(len(self.params)):
            self._load(p).wait()
            self.copy(5 * p, self.a_buf[p], self.a_buf[p], self.me).wait_recv()
            for j, (cx, cy) in enumerate(self.chips):
                kj = 2 * cx + cy
                self.pm[p][kj] = (self.pm[p][kj].astype(F32) + self.a_buf[p][kj].astype(F32)).astype(BF16)
                self._pair_sum(p, j).start()
            self.r_buf[p][...] = self.pm[p][self.k].astype(F32) + self.a_buf[p][self.k].astype(F32)

    def finish(self):
        for p, (_, _, _, axis, size) in enumerate(self.params):
            for j in range(3):
                self.copy(5 * p + 1 + j, self.b_buf[p].at[j], self.b_buf[p].at[j], self.me).wait_recv()
                self.r_buf[p][...] = self.r_buf[p][...] + self.b_buf[p][j].astype(F32)
            self._store(p).start()
            self._finished(p).start()
        for p, (_, _, _, axis, size) in enumerate(self.params):
            theirs = _half(self.g_out[p], axis, size, 1 - self.c)
            self.copy(5 * p + 4, theirs, theirs, self.me).wait_recv()
            self._store(p).wait()
            self._to_sibling(p).wait_send()
            for j in range(3):
                self._pair_sum(p, j).wait_send()
            self._finished(p).wait_send()


class _SmallSum:
    def __init__(self, rows):
        self.rows = rows
        self.scratch = [pltpu.VMEM((N_DEV, rows, LANES), F32),
                        pltpu.SemaphoreType.DMA((N_DEV - 1,)), pltpu.SemaphoreType.DMA((N_DEV - 1,))]
        self.out_shape = jax.ShapeDtypeStruct((rows, LANES), F32)
        self.spec_shape = (rows, LANES)

    def bind(self, src, dst, scratch):
        self.src, self.dst = src, dst
        self.buf, send_sems, recv_sems = scratch
        self.copy = _remote_copier(send_sems, recv_sems)
        self.x, self.y, self.c = lax.axis_index("x"), lax.axis_index("y"), lax.axis_index("c")

    def _send(self, f):
        fx, fy, fc = [(a, b, d) for a in (0, 1) for b in (0, 1) for d in (0, 1)][f]
        x, y, c = self.x, self.y, self.c
        peer = (1 - x if fx else x, 1 - y if fy else y, 1 - c if fc else c)
        return self.copy(f - 1, self.src, self.buf.at[f], peer)

    def start(self):
        for f in range(1, N_DEV):
            self._send(f).start()
        self.buf[0] = self.src[...]

    def finish(self):
        me = (self.x, self.y, self.c)
        for f in range(1, N_DEV):
            self.copy(f - 1, self.buf.at[f], self.buf.at[f], me).wait_recv()
        dev = 4 * self.x + 2 * self.y + self.c
        total = self.buf[dev]
        for d in range(1, N_DEV):
            total = total + self.buf[jnp.bitwise_xor(dev, d)]
        self.dst[...] = total
        for f in range(1, N_DEV):
            self._send(f).wait_send()


def _adamw_math(w, g, m, v):
    m = ADAM_B1 * m + (1.0 - ADAM_B1) * g
    v = ADAM_B2 * v + (1.0 - ADAM_B2) * (g * g)
    m_hat = m / (1.0 - ADAM_B1 ** ADAM_STEP)
    v_hat = v / (1.0 - ADAM_B2 ** ADAM_STEP)
    delta = -ADAM_LR * (m_hat / (jnp.sqrt(v_hat) + ADAM_EPS) + ADAM_WD * w)
    return delta, m, v


def _adamw_tiled(w, g, m, v, tm):
    rows, cols = w.shape

    def body(w_ref, g_ref, m_ref, v_ref, d_ref, nm_ref, nv_ref, g_out):
        g = g_ref[...]
        d_ref[...], nm_ref[...], nv_ref[...] = _adamw_math(w_ref[...], g, m_ref[...], v_ref[...])
        g_out[...] = g

    spec = _row_spec(tm, cols)
    return pl.pallas_call(
        body,
        name="adamw_w_in",
        grid=(rows // tm,),
        in_specs=[spec] * 4,
        out_specs=[spec] * 4,
        out_shape=[jax.ShapeDtypeStruct(w.shape, F32)] * 4,
        compiler_params=pltpu.CompilerParams(dimension_semantics=("parallel",), vmem_limit_bytes=VMEM_LIMIT),
    )(w, g, m, v)


def _adamw_many(ws, gs, ms, vs):
    n = len(ws)
    g_arrays, g_at = [], []
    for g in gs:
        arr, row = g if isinstance(g, tuple) else (g, None)
        k = next((j for j, a in enumerate(g_arrays) if a is arr), len(g_arrays))
        if k == len(g_arrays):
            g_arrays.append(arr)
        g_at.append((k, row))
    n_g = len(g_arrays)

    def body(*refs):
        w_refs, m_refs, v_refs, g_refs, outs = (refs[:n], refs[n:2 * n], refs[2 * n:3 * n], refs[3 * n:3 * n + n_g],
                                                refs[3 * n + n_g:])
        for i in range(n):
            k, row = g_at[i]
            g = g_refs[k][...] if row is None else g_refs[k][row:row + ws[i].shape[0], :]
            d, nm, nv = _adamw_math(w_refs[i][...], g, m_refs[i][...], v_refs[i][...])
            outs[i][...] = d
            outs[n + i][...] = nm
            outs[2 * n + i][...] = nv
            outs[3 * n + i][...] = g

    vmem_spec = pl.BlockSpec(memory_space=pltpu.VMEM)
    shapes = [jax.ShapeDtypeStruct(w.shape, F32) for w in ws]
    out = pl.pallas_call(
        body,
        name="adamw_small",
        in_specs=[vmem_spec] * (3 * n + n_g),
        out_specs=[vmem_spec] * (4 * n),
        out_shape=shapes * 4,
        compiler_params=pltpu.CompilerParams(vmem_limit_bytes=VMEM_LIMIT),
    )(*ws, *ms, *vs, *g_arrays)
    return out[:n], out[n:2 * n], out[2 * n:3 * n], out[3 * n:]


def _pack_rows(parts, rows, dtype):
    flat = jnp.concatenate([p.reshape(-1).astype(dtype) for p in parts])
    flat = jnp.concatenate([flat, jnp.zeros((rows * LANES - flat.shape[0],), dtype)])
    return flat.reshape(rows, LANES)


def _unpack_rows(packed, shapes):
    flat = packed.reshape(-1)
    out, off = [], 0
    for _, shp in shapes:
        n = int(np.prod(shp))
        out.append(flat[off:off + n].reshape(shp))
        off += n
    return out


def _rope_tables(s):
    half = QK_ROPE_DIM // 2
    inv_freq = np.float32(ROPE_THETA) ** (-np.arange(half, dtype=np.float32) / np.float32(half))
    ang = (np.arange(s, dtype=np.float32)[:, None] * inv_freq[None, :]).astype(np.float32)
    cos, sin = np.cos(ang.astype(np.float64)).astype(np.float32), np.sin(ang.astype(np.float64)).astype(np.float32)
    z16 = np.zeros((s, half), np.float32)
    z32 = np.zeros((s, HEAD_PAD - QK_NOPE_DIM - QK_ROPE_DIM), np.float32)
    z64 = np.zeros((s, QK_NOPE_DIM), np.float32)
    rc = np.concatenate([np.ones((s, QK_NOPE_DIM), np.float32), cos, cos, z32], axis=1)
    rsa = np.concatenate([z64, -sin, z16, z32], axis=1)
    rsb = np.concatenate([z64, z16, sin, z32], axis=1)
    return jnp.asarray(rc), jnp.asarray(rsa), jnp.asarray(rsb)


def kernel(x, norm_in, w_in, q_norm, w_uq, kv_norm, w_ukv, pool_w, pool_scale, w_branch_attn, w_branch_pool, w_out, norm_final, loss_target, m_norm_in, m_w_in, m_q_norm, m_w_uq, m_kv_norm, m_w_ukv, m_pool_w, m_pool_scale, m_w_branch_attn, m_w_branch_pool, m_w_out, m_norm_final, v_norm_in, v_w_in, v_q_norm, v_w_uq, v_kv_norm, v_w_ukv, v_pool_w, v_pool_scale, v_w_branch_attn, v_w_branch_pool, v_w_out, v_norm_final):
    s = x.shape[1]
    t_att, t_row = _tiles(s)
    x2 = x.reshape(s, D_MODEL)
    tgt = loss_target.reshape(s, D_MODEL)

    local = [w_in.T, w_uq.reshape(96, 768), w_ukv.reshape(64, 1024), w_branch_attn, w_branch_pool, w_out]
    local = [a.astype(BF16) for a in local]
    cx, cy = lax.axis_index("x"), lax.axis_index("y")
    others = [2 * ox + oy for ox, oy in _other_chips(cx, cy)]
    hn, z_sh, (w_in_t, w_uq_all, w_ukv_all) = _inproj_fwd(
        jnp.stack([2 * cx + cy, others[1], others[2], others[0]]).astype(jnp.int32), x2, norm_in.reshape(1, -1),
        local[:3], 4 * t_row)
    rc, rsa, rsb = _rope_tables(s)
    g_in = norm_in.reshape(1, -1)
    g_q = q_norm.reshape(1, -1)
    g_kv = kv_norm.reshape(1, -1)
    g_f = norm_final.reshape(1, -1)
    ps = pool_scale.reshape(1, -1)
    pw_bf = pool_w.astype(BF16)

    q, k, v, q_t, v_t, wuq_p, wk_p, wv = _qkv_fwd(z_sh, g_q, g_kv, w_uq_all, w_ukv_all, rc, rsa, rsb, 2 * t_row)
    o, lse, (w_ba_all, w_bp_all, w_out_all) = _attn_fwd(q_t, k, v_t, local[3:], t_att)
    w_out_f = w_out_all.reshape(D_MODEL, D_MODEL)

    (do, delta, dgattn, dgpool, dgmerge, ddc, dh, sq_err, d_w_out, d_w_ba, d_w_bp, d_pool_w, d_pool_scale,
     d_norm_final) = _mid(o, z_sh, x2, tgt, pw_bf, ps, w_ba_all, w_bp_all, w_out_f, g_f, t_row)

    late_grads = [d_w_ba, d_w_bp, d_w_out.reshape(N_CHIPS, 256, D_MODEL)]
    small_mid = dict(pool_scale=d_pool_scale, norm_final=d_norm_final, pool_w=d_pool_w, sq_err=sq_err)
    gs_mid = _pack_rows([small_mid[n] for n, _ in SMALL_MID], _small_rows(SMALL_MID), F32)
    dq, dk_t, dv_t, (g_w_ba, g_w_bp, g_w_out), g_small_mid = _attn_bwd(q, q_t, k, v, do, lse, delta, late_grads,
                                                                      gs_mid, t_att)
    sq_err_all = _unpack_rows(g_small_mid, SMALL_MID)[-1]
    dzq, dzkv, dzkr, d_w_uq, d_w_ukv, d_q_norm, d_kv_norm = _qkv_bwd(
        dq, dk_t, dv_t, z_sh, g_q, g_kv, wuq_p, wk_p, wv, rc, rsa, rsb, 2 * t_row)
    grad_x, d_norm_in, dz_sh = _inproj_bwd_x(dzq, dzkv, dzkr, dgattn, ddc, dgpool, dgmerge, x2, dh, g_in, w_in_t,
                                             2 * t_row)

    small_late = dict(norm_in=d_norm_in, q_norm=d_q_norm, kv_norm=d_kv_norm)
    gs = _pack_rows([small_late[n] for n, _ in SMALL_LATE], _small_rows(SMALL_LATE), F32)
    order = jnp.stack(others + [2 * cx + cy]).astype(jnp.int32)
    g_w_in_t, g_w_uq, g_w_ukv, g_small = _inproj_bwd_w(
        order, dz_sh, hn, d_w_uq, d_w_ukv, gs, 4 * t_row)
    g_w_uq = g_w_uq.reshape(w_uq.shape)
    g_w_ukv = g_w_ukv.reshape(w_ukv.shape)

    dl_w_in, nm_w_in, nv_w_in, g_w_in = (a.T for a in _adamw_tiled(w_in.T, g_w_in_t, m_w_in.T, v_w_in.T, 152))

    packed = {n: (g_small_mid, r) for n, r in _first_rows(SMALL_MID).items() if n != "sq_err"}
    packed.update({n: (g_small, r) for n, r in _first_rows(SMALL_LATE).items()})

    def as_rows(n, a):
        return a.reshape(-1, LANES) if n in packed else a

    names = ["norm_in", "q_norm", "w_uq", "kv_norm", "w_ukv", "pool_w", "pool_scale", "w_branch_attn",
             "w_branch_pool", "w_out", "norm_final"]
    ws = dict(norm_in=norm_in, q_norm=q_norm, w_uq=w_uq, kv_norm=kv_norm, w_ukv=w_ukv, pool_w=pool_w,
              pool_scale=pool_scale, w_branch_attn=w_branch_attn, w_branch_pool=w_branch_pool, w_out=w_out,
              norm_final=norm_final)
    gsd = dict(packed, w_uq=g_w_uq, w_ukv=g_w_ukv, w_branch_attn=g_w_ba, w_branch_pool=g_w_bp, w_out=g_w_out)
    msd = dict(norm_in=m_norm_in, q_norm=m_q_norm, w_uq=m_w_uq, kv_norm=m_kv_norm, w_ukv=m_w_ukv, pool_w=m_pool_w,
               pool_scale=m_pool_scale, w_branch_attn=m_w_branch_attn, w_branch_pool=m_w_branch_pool, w_out=m_w_out,
               norm_final=m_norm_final)
    vsd = dict(norm_in=v_norm_in, q_norm=v_q_norm, w_uq=v_w_uq, kv_norm=v_kv_norm, w_ukv=v_w_ukv, pool_w=v_pool_w,
               pool_scale=v_pool_scale, w_branch_attn=v_w_branch_attn, w_branch_pool=v_w_branch_pool, w_out=v_w_out,
               norm_final=v_norm_final)
    dls, nms, nvs, g_outs = _adamw_many([as_rows(n, ws[n]) for n in names], [gsd[n] for n in names],
                                        [as_rows(n, msd[n]) for n in names], [as_rows(n, vsd[n]) for n in names])

    grads = dict(zip(names, g_outs))
    grads["w_in"] = g_w_in
    delta_w = {n: d.reshape(ws[n].shape) for n, d in zip(names, dls)}
    new_m = {n: d.reshape(ws[n].shape) for n, d in zip(names, nms)}
    new_v = {n: d.reshape(ws[n].shape) for n, d in zip(names, nvs)}
    delta_w["w_in"], new_m["w_in"], new_v["w_in"] = dl_w_in, nm_w_in, nv_w_in
    ws["w_in"] = w_in

    order = ["norm_in", "w_in", "q_norm", "w_uq", "kv_norm", "w_ukv", "pool_w", "pool_scale", "w_branch_attn",
             "w_branch_pool", "w_out", "norm_final"]
    loss = 0.5 * jnp.sum(sq_err_all) / D_MODEL
    return (loss, grad_x.reshape(x.shape),
            *[grads[n].reshape(ws[n].shape) for n in order],
            *[delta_w[n] for n in order], *[new_m[n] for n in order], *[new_v[n] for n in order])
```

```python
import functools

import jax
import jax.numpy as jnp
import numpy as np
from jax import lax
from jax.experimental import pallas as pl
from jax.experimental.pallas import tpu as pltpu

F32 = jnp.float32
BF16 = jnp.bfloat16
MESH = pl.DeviceIdType.MESH

D_MODEL = 1024
CHUNK = 64
MLA_HEADS = 8
QK_NOPE_DIM = 64
QK_ROPE_DIM = 32
V_HEAD_DIM = 64
Q_LORA_RANK = 384
KV_LORA_RANK = 256
MLA_WIDTH = MLA_HEADS * V_HEAD_DIM
ROPE_THETA = 10000.0
POOL_WINDOWS = (2, 4, 8, 16)
POOL_WIDTH = 512
POOL_GROUP_DIM = 128
BRANCH_COLS = D_MODEL // 4
FWD_HEADS = 8
BWD_HEADS = 4
POOL_HALO = 16
EPS = 1e-6
IN_TOTAL = 4256
HEAD_PAD = 128
ATT_SCALE = (QK_NOPE_DIM + QK_ROPE_DIM) ** -0.5
ATT_SCALE_LOG2E = ATT_SCALE * 1.4426950408889634

ADAM_LR = 0.001
ADAM_B1 = 0.9
ADAM_B2 = 0.999
ADAM_EPS = 1e-08
ADAM_WD = 0.01
ADAM_STEP = 10

N_CHIPS = 4
N_DEV = 8
LANES = 128
VMEM_LIMIT = 60 * 1024 * 1024

IN_SEGMENTS = ((384, 384), (256, 256), (32, HEAD_PAD), (512, 512), (512, 512), (512, 512), (2048, 2048))
SHARD_COLS = IN_TOTAL // N_CHIPS
ZQ_COLS = slice(0, 384)
ZKV_COLS = slice(384, 640)
ZKR_TILE = slice(640, 768)


def _shard_pieces():
    bounds, off = [], 0
    for w, _ in IN_SEGMENTS:
        bounds.append((off, off + w))
        off += w
    out = []
    for j in range(N_CHIPS):
        lo, hi = SHARD_COLS * j, SHARD_COLS * (j + 1)
        out.append([(i, max(lo, a) - a, min(hi, b) - a, max(lo, a) - lo)
                    for i, (a, b) in enumerate(bounds) if max(lo, a) < min(hi, b)])
    return out


SHARD_PIECES = _shard_pieces()


def _segment(z_blocks, seg):
    parts = [z_blocks[j][:, col:col + hi - lo]
             for j, pieces in enumerate(SHARD_PIECES) for sg, lo, hi, col in pieces if sg == seg]
    return parts[0] if len(parts) == 1 else jnp.concatenate(parts, axis=1)

COMM_PARAMS = (
    ("w_in", SHARD_COLS, D_MODEL, 1, 512),
    ("w_uq", 96, 768, 0, 48),
    ("w_ukv", 64, 1024, 0, 32),
    ("w_branch_attn", 512, 256, 0, 256),
    ("w_branch_pool", 512, 256, 0, 256),
    ("w_out", 256, 1024, 0, 128),
)

SMALL_MID = (
    ("pool_w", (4, 128, 128)),
    ("norm_final", (1024,)),
    ("sq_err", (8, 128)),
    ("pool_scale", (512,)),
)
SMALL_LATE = (
    ("norm_in", (1024,)),
    ("q_norm", (384,)),
    ("kv_norm", (256,)),
)


def _small_rows(shapes):
    return -(-sum(int(np.prod(s)) for _, s in shapes) // (LANES * 8)) * 8


def _first_rows(shapes):
    out, off = {}, 0
    for name, shp in shapes:
        out[name], rem = divmod(off, LANES)
        assert rem == 0, name
        off += int(np.prod(shp))
    return out


def _pack_into(dst_ref, shapes, values):
    first = _first_rows(shapes)
    for name, shp in shapes:
        v, row = values[name], first[name]
        if v.ndim == 3:
            for g in range(v.shape[0]):
                dst_ref[row + g * v.shape[1]:row + (g + 1) * v.shape[1], :] = v[g]
        elif v.shape[0] == 1 and v.shape[1] > LANES:
            for j in range(v.shape[1] // LANES):
                dst_ref[row + j:row + j + 1, :] = v[:, j * LANES:(j + 1) * LANES]
        else:
            dst_ref[row:row + v.shape[0], :] = v
    used = sum(int(np.prod(shp)) for _, shp in shapes) // LANES
    if used < dst_ref.shape[0]:
        dst_ref[used:, :] = jnp.zeros((dst_ref.shape[0] - used, LANES), dst_ref.dtype)


def _dot(a, b):
    return jnp.dot(a, b, preferred_element_type=F32)


def _dot_nt(a, b):
    return lax.dot_general(a, b, (((1,), (1,)), ((), ())), preferred_element_type=F32)


def _dot_tn(a, b):
    return lax.dot_general(a, b, (((0,), (0,)), ((), ())), preferred_element_type=F32)


def _sigmoid(x):
    return 1.0 / (1.0 + jnp.exp(-x))


def _colsum(x):
    return jnp.sum(x, axis=0, keepdims=True)


def _rms_fwd(x, g):
    r = lax.rsqrt(jnp.mean(x * x, axis=-1, keepdims=True) + EPS)
    xhat = x * r
    return xhat * g, xhat, r


def _rms_bwd(dy, xhat, r, g):
    dxhat = dy * g
    return r * (dxhat - xhat * jnp.mean(dxhat * xhat, axis=-1, keepdims=True))


def _rope(v, c, sa, sb):
    return v * c + pltpu.roll(v, 112, 1) * sa + pltpu.roll(v, 16, 1) * sb


def _unrope(d, c, sa, sb):
    return d * c + pltpu.roll(d * sa, 16, 1) + pltpu.roll(d * sb, 112, 1)


def _row_spec(tm, n):
    return pl.BlockSpec((tm, n), lambda i: (i, 0))


def _full_spec(shape):
    nd = len(shape)
    return pl.BlockSpec(shape, lambda i: (0,) * nd)


def _tiles(s):
    t_att = 512 if s >= 2048 else 128
    t_row = 256 if s >= 1024 else 128
    return t_att, t_row


def _inproj_fwd(order, x, norm_in, early_shards, tm):
    s = x.shape[0]
    n_tiles = s // tm
    gat = _Gather(COMM_PARAMS[:3])
    n_w = len(gat.params)
    arrival = (1, 2, 0)

    def body(order_ref, x_ref, g_ref, *rest):
        w_loc, (hn_ref, z_ref), w_all = rest[:n_w], rest[n_w:n_w + 2], rest[n_w + 2:2 * n_w + 2]
        w_vmem, hn_all, w_sem = rest[2 * n_w + 2:2 * n_w + 5]
        gat.bind(w_loc, w_all, rest[2 * n_w + 5:])
        ph, i = pl.program_id(0), pl.program_id(1)
        pl.when(jnp.logical_and(ph == 0, i == 0))(gat.start)

        def fetch(phase):
            src = w_loc[0] if phase == 0 else w_all[0].at[order_ref[phase]]
            return pltpu.make_async_copy(src, w_vmem.at[phase % 2], w_sem.at[phase % 2])

        def landed(f):
            gat.relay_one(0, arrival[f])
            gat.await_one(0, arrival[f])

        @pl.when(jnp.logical_and(ph == 0, i == 0))
        def _():
            fetch(0).start()
            fetch(0).wait()

        @pl.when(jnp.logical_and(ph == 1, i == 0))
        def _():
            landed(0)
            fetch(1).start()
            fetch(1).wait()

        for f in (1, 2):
            @pl.when(jnp.logical_and(ph == f, i == n_tiles - 1))
            def _(f=f):
                landed(f)
                fetch(f + 1).start()

            @pl.when(jnp.logical_and(ph == f + 1, i == 0))
            def _(f=f):
                fetch(f + 1).wait()

        rows = pl.ds(pl.multiple_of(i * tm, tm), tm)

        @pl.when(ph == 0)
        def _():
            hn, _, _ = _rms_fwd(x_ref[...], g_ref[...])
            hn = hn.astype(BF16)
            hn_ref[...] = hn
            hn_all[rows, :] = hn

        z_ref[0] = _dot_nt(hn_all[rows, :], w_vmem[ph % 2])

        @pl.when(jnp.logical_and(ph == N_CHIPS - 1, i == n_tiles - 1))
        def _():
            for p in range(1, n_w):
                for j in range(3):
                    gat.relay_one(p, j)
            for p in range(1, n_w):
                for j in range(3):
                    gat.await_one(p, j)
            gat.wait_sends()

    def tile_in_phase0(ph, i, order):
        return (jnp.where(ph == 0, i, n_tiles - 1), 0)

    any_spec = pl.BlockSpec(memory_space=pl.ANY)
    grid_spec = pltpu.PrefetchScalarGridSpec(
        num_scalar_prefetch=1,
        grid=(N_CHIPS, n_tiles),
        in_specs=[pl.BlockSpec((tm, D_MODEL), tile_in_phase0),
                  pl.BlockSpec((1, D_MODEL), lambda ph, i, order: (0, 0))] + [any_spec] * n_w,
        out_specs=[pl.BlockSpec((tm, D_MODEL), tile_in_phase0),
                   pl.BlockSpec((1, tm, SHARD_COLS), lambda ph, i, order: (order[ph], i, 0))] + [any_spec] * n_w,
        scratch_shapes=[pltpu.VMEM((2, SHARD_COLS, D_MODEL), BF16), pltpu.VMEM((s, D_MODEL), BF16),
                        pltpu.SemaphoreType.DMA((2,))] + gat.scratch,
    )
    out = pl.pallas_call(
        body,
        name="inproj_fwd",
        grid_spec=grid_spec,
        out_shape=[jax.ShapeDtypeStruct((s, D_MODEL), BF16), jax.ShapeDtypeStruct((N_CHIPS, s, SHARD_COLS), F32)]
        + gat.out_shape,
        compiler_params=pltpu.CompilerParams(dimension_semantics=("arbitrary", "arbitrary"),
                                             vmem_limit_bytes=VMEM_LIMIT),
    )(order, x, norm_in, *early_shards)
    return out[0], out[1], out[2:]


def _qkv_fwd(z_sh, q_norm, kv_norm, w_uq_all, w_ukv_all, rc, rsa, rsb, tm):
    s = z_sh.shape[1]
    hw = MLA_HEADS * HEAD_PAD
    qk = QK_NOPE_DIM + QK_ROPE_DIM

    def body(z_ref, gq_ref, gkv_ref, uq_ref, ukv_ref, c_ref, sa_ref, sb_ref,
             q_ref, k_ref, v_ref, qt_ref, vt_ref, wuq_ref, wk_ref, wv_ref):
        @pl.when(pl.program_id(0) == 0)
        def _():
            w_q = jnp.concatenate([uq_ref[j] for j in range(N_CHIPS)], axis=0).astype(F32)
            gap = jnp.zeros((Q_LORA_RANK, HEAD_PAD - qk), F32)
            wuq_ref[...] = jnp.concatenate(
                [part for h in range(MLA_HEADS) for part in (w_q[:, h * qk:(h + 1) * qk], gap)], axis=1).astype(BF16)
            w_kv = jnp.concatenate([ukv_ref[j] for j in range(N_CHIPS)], axis=0).astype(F32)
            lane = lax.broadcasted_iota(jnp.int32, w_kv.shape, 1)
            wk_ref[...] = jnp.where(lane % HEAD_PAD < QK_NOPE_DIM, w_kv, 0.0).astype(BF16)
            wv_ref[...] = jnp.concatenate(
                [w_kv[:, h * HEAD_PAD + QK_NOPE_DIM:(h + 1) * HEAD_PAD] for h in range(MLA_HEADS)],
                axis=1).astype(BF16)

        c, sa, sb = c_ref[...], sa_ref[...], sb_ref[...]
        z0 = z_ref[0]
        cq, _, _ = _rms_fwd(z0[:, ZQ_COLS], gq_ref[...])
        qf = _dot(cq.astype(BF16), wuq_ref[...])
        ckv, _, _ = _rms_fwd(z0[:, ZKV_COLS], gkv_ref[...])
        ckv = ckv.astype(BF16)
        kn = _dot(ckv, wk_ref[...])
        lane = lax.broadcasted_iota(jnp.int32, (tm, HEAD_PAD), 1)
        zkr = jnp.where(lane < QK_ROPE_DIM, z0[:, ZKR_TILE], 0.0)
        kr = _rope(pltpu.roll(zkr, 64, 1), c, sa, sb)
        for h in range(MLA_HEADS):
            cols = slice(h * HEAD_PAD, (h + 1) * HEAD_PAD)
            qh = _rope(qf[:, cols], c, sa, sb)
            q_ref[:, cols] = qh.astype(BF16)
            qt_ref[cols, :] = qh.T.astype(BF16)
            k_ref[:, cols] = (kn[:, cols] + kr).astype(BF16)
        vf = _dot(ckv, wv_ref[...])
        v_ref[...] = vf.astype(BF16)
        vt_ref[...] = vf.T.astype(BF16)

    return pl.pallas_call(
        body,
        name="qkv_fwd",
        grid=(s // tm,),
        in_specs=[
            pl.BlockSpec((1, tm, SHARD_COLS), lambda i: (0, i, 0)),
            _full_spec((1, Q_LORA_RANK)), _full_spec((1, KV_LORA_RANK)),
            _full_spec(w_uq_all.shape), _full_spec(w_ukv_all.shape),
            _row_spec(tm, HEAD_PAD), _row_spec(tm, HEAD_PAD), _row_spec(tm, HEAD_PAD),
        ],
        out_specs=[_row_spec(tm, hw), _row_spec(tm, hw), _row_spec(tm, MLA_WIDTH),
                   pl.BlockSpec((hw, tm), lambda i: (0, i)), pl.BlockSpec((MLA_WIDTH, tm), lambda i: (0, i)),
                   _full_spec((Q_LORA_RANK, hw)), _full_spec((KV_LORA_RANK, hw)), _full_spec((KV_LORA_RANK, MLA_WIDTH))],
        out_shape=[jax.ShapeDtypeStruct((s, hw), BF16), jax.ShapeDtypeStruct((s, hw), BF16),
                   jax.ShapeDtypeStruct((s, MLA_WIDTH), BF16),
                   jax.ShapeDtypeStruct((hw, s), BF16), jax.ShapeDtypeStruct((MLA_WIDTH, s), BF16),
                   jax.ShapeDtypeStruct((Q_LORA_RANK, hw), BF16), jax.ShapeDtypeStruct((KV_LORA_RANK, hw), BF16),
                   jax.ShapeDtypeStruct((KV_LORA_RANK, MLA_WIDTH), BF16)],
        compiler_params=pltpu.CompilerParams(dimension_semantics=("arbitrary",), vmem_limit_bytes=VMEM_LIMIT),
    )(z_sh, q_norm, kv_norm, w_uq_all, w_ukv_all, rc, rsa, rsb)


def _chunk_mask(t, keys_on_rows):
    rows = lax.broadcasted_iota(jnp.int32, (t, t), 0) // CHUNK
    cols = lax.broadcasted_iota(jnp.int32, (t, t), 1) // CHUNK
    return rows <= cols if keys_on_rows else cols <= rows


def _attn_fwd(q_t, k, v_t, late_shards, t):
    s = k.shape[0]
    groups = MLA_HEADS // FWD_HEADS
    n_q = s // t
    gat = _Gather(COMM_PARAMS[3:])
    n_w = len(gat.params)

    def body(qt_ref, k_ref, k2_ref, vt_ref, *rest):
        w_in, (o_ref, lse_ref), w_out = rest[:n_w], rest[n_w:n_w + 2], rest[n_w + 2:2 * n_w + 2]
        gat.bind(w_in, w_out, rest[2 * n_w + 2:])
        i = pl.program_id(1)
        step_no = pl.program_id(0) * n_q + i
        pl.when(step_no == 0)(gat.start)
        pl.when(step_no == groups * n_q // 2)(gat.relay)
        mask = _chunk_mask(t, True)
        qcs = [slice(hh * HEAD_PAD, (hh + 1) * HEAD_PAD) for hh in range(FWD_HEADS)]
        vcs = [slice(hh * V_HEAD_DIM, (hh + 1) * V_HEAD_DIM) for hh in range(FWD_HEADS)]
        qts = [qt_ref[qc, :] for qc in qcs]

        def step(j, carry, masked):
            keys = pl.ds(pl.multiple_of(j * t, t), t)
            out = []
            for hh in range(FWD_HEADS):
                m, l, acc = carry[hh]
                sc = _dot(k_ref[keys, qcs[hh]], qts[hh])
                if masked:
                    sc = jnp.where(mask, sc, -jnp.inf)
                m_new = jnp.maximum(m, jnp.max(sc, axis=0, keepdims=True))
                alpha = jnp.exp2((m - m_new) * ATT_SCALE_LOG2E)
                p = jnp.exp2((_dot(k2_ref[keys, qcs[hh]], qts[hh]) - m_new) * ATT_SCALE_LOG2E)
                if masked:
                    p = jnp.where(mask, p, 0.0)
                l = alpha * l + jnp.sum(p, axis=0, keepdims=True)
                acc = alpha * acc + _dot(vt_ref[vcs[hh], keys], p.astype(BF16))
                out.append((m_new, l, acc))
            return tuple(out)

        one = (jnp.full((1, t), -jnp.inf, F32), jnp.zeros((1, t), F32), jnp.zeros((V_HEAD_DIM, t), F32))
        carry = lax.fori_loop(0, i, functools.partial(step, masked=False), (one,) * FWD_HEADS)
        carry = step(i, carry, True)
        o_ref[...] = jnp.concatenate([carry[hh][2] / carry[hh][1] for hh in range(FWD_HEADS)], axis=0).T
        for hh in range(FWD_HEADS):
            m, l, _ = carry[hh]
            lse_ref[:, qcs[hh]] = jnp.broadcast_to(m * ATT_SCALE_LOG2E + jnp.log2(l), (HEAD_PAD, t)).T
        pl.when(step_no == groups * n_q - 1)(gat.finish)

    any_spec = pl.BlockSpec(memory_space=pl.ANY)
    out = pl.pallas_call(
        body,
        name="attn_fwd",
        grid=(groups, n_q),
        in_specs=[
            pl.BlockSpec((FWD_HEADS * HEAD_PAD, t), lambda p, i: (p, i)),
            pl.BlockSpec((s, FWD_HEADS * HEAD_PAD), lambda p, i: (0, p), pipeline_mode=pl.Buffered(1)),
            pl.BlockSpec((s, FWD_HEADS * HEAD_PAD), lambda p, i: (0, p), pipeline_mode=pl.Buffered(1)),
            pl.BlockSpec((FWD_HEADS * V_HEAD_DIM, s), lambda p, i: (p, 0), pipeline_mode=pl.Buffered(1)),
        ] + [any_spec] * n_w,
        out_specs=[
            pl.BlockSpec((t, FWD_HEADS * V_HEAD_DIM), lambda p, i: (i, p)),
            pl.BlockSpec((t, FWD_HEADS * HEAD_PAD), lambda p, i: (i, p)),
        ] + [any_spec] * n_w,
        out_shape=[jax.ShapeDtypeStruct((s, MLA_WIDTH), F32), jax.ShapeDtypeStruct((s, MLA_HEADS * HEAD_PAD), F32)]
        + gat.out_shape,
        scratch_shapes=gat.scratch,
        compiler_params=pltpu.CompilerParams(dimension_semantics=("arbitrary", "arbitrary"),
                                             vmem_limit_bytes=VMEM_LIMIT),
    )(q_t, k, k, v_t, *late_shards)
    return out[0], out[1], out[2:]


def _mid(o, z_sh, x, target, pool_w, pool_scale, w_ba, w_bp, w_out, norm_final, tm):
    s = x.shape[0]
    n_tiles = s // tm
    halo_per_tile = tm // POOL_HALO
    small_rows = _small_rows(SMALL_MID)

    def body(o_ref, z0_ref, z1_ref, z1h_ref, z2_ref, z3_ref, x_ref, t_ref, pw_ref, ps_ref, wba_ref, wbp_ref,
             wout_ref, gf_ref,
             do_ref, dl_ref, dga_ref, dgp_ref, dgm_ref, ddc_ref, dh_ref,
             small_ref, dwout_out, dwba_out, dwbp_out,
             ubuf, dwout_ref, dwba_ref, dwbp_ref, loss_ref, dpw_ref, dps_ref, dgf_ref):
        i = pl.program_id(0)

        @pl.when(i == 0)
        def _():
            loss_ref[...] = jnp.zeros_like(loss_ref)
            dwout_ref[...] = jnp.zeros_like(dwout_ref)
            dwba_ref[...] = jnp.zeros_like(dwba_ref)
            dwbp_ref[...] = jnp.zeros_like(dwbp_ref)
            dpw_ref[...] = jnp.zeros_like(dpw_ref)
            dps_ref[...] = jnp.zeros_like(dps_ref)
            dgf_ref[...] = jnp.zeros_like(dgf_ref)

        zs = [z0_ref[0], z1_ref[0], z2_ref[0], z3_ref[0]]
        o = o_ref[...]
        ga = _segment(zs, 3)
        sga = _sigmoid(ga)
        silu_a = ga * sga
        y_attn = (o * silu_a).astype(BF16)

        ubuf[0:POOL_HALO, :] = jnp.where(i > 0, _segment([None, z1h_ref[0]], 4), 0.0)
        ubuf[POOL_HALO:, :] = _segment(zs, 4)
        row = lax.broadcasted_iota(jnp.int32, (tm, POOL_GROUP_DIM), 0) + i * tm
        ps = ps_ref[...]
        gp = _segment(zs, 5)
        sgp = _sigmoid(gp)
        silu_p = gp * sgp
        d_bf, dm, inv_cnt = [], [], []
        for g, w in enumerate(POOL_WINDOWS):
            cols = slice(g * POOL_GROUP_DIM, (g + 1) * POOL_GROUP_DIM)
            wsum = ubuf[POOL_HALO:, cols]
            for kk in range(1, w):
                wsum = wsum + ubuf[POOL_HALO - kk:POOL_HALO - kk + tm, cols]
            inv = 1.0 / jnp.minimum(row + 1, w).astype(F32)
            dg = (wsum * inv - ubuf[POOL_HALO:, cols]).astype(BF16)
            d_bf.append(dg)
            inv_cnt.append(inv)
            dm.append(_dot(dg, pw_ref[g]))
        dm = jnp.concatenate(dm, axis=1)
        yp = dm * ps
        y_pool = (yp * silu_p).astype(BF16)

        a = jnp.concatenate([_dot(y_attn, wba_ref[j]) for j in range(N_CHIPS)], axis=1)
        p = jnp.concatenate([_dot(y_pool, wbp_ref[j]) for j in range(N_CHIPS)], axis=1)
        gm = _segment(zs, 6)
        gate_a = _sigmoid(gm[:, :D_MODEL])
        gate_p = _sigmoid(gm[:, D_MODEL:])
        merged = (gate_a * a + gate_p * p).astype(BF16)
        h = x_ref[...] + _dot(merged, wout_ref[...])
        gf = gf_ref[...]
        y, xhat, r = _rms_fwd(h, gf)
        err = y - t_ref[...]
        e2 = err * err
        e2 = jnp.sum(e2.reshape(tm // 8, 8, D_MODEL), axis=0)
        acc = e2[:, 0:LANES]
        for cidx in range(1, D_MODEL // LANES):
            acc = acc + e2[:, cidx * LANES:(cidx + 1) * LANES]
        loss_ref[...] += acc

        dy = err * (1.0 / D_MODEL)
        dgf_ref[...] += _colsum(dy * xhat)
        dh = _rms_bwd(dy, xhat, r, gf)
        dh_ref[...] = dh
        dh_bf = dh.astype(BF16)
        dwout_ref[...] += _dot_tn(merged, dh_bf)
        dmerged = _dot_nt(dh_bf, wout_ref[...])
        da = (dmerged * gate_a).astype(BF16)
        dp = (dmerged * gate_p).astype(BF16)
        dgm_ref[:, :D_MODEL] = (dmerged * a * gate_a * (1.0 - gate_a)).astype(BF16)
        dgm_ref[:, D_MODEL:] = (dmerged * p * gate_p * (1.0 - gate_p)).astype(BF16)
        dy_attn = dy_pool = None
        for j in range(N_CHIPS):
            cols = slice(j * BRANCH_COLS, (j + 1) * BRANCH_COLS)
            dwba_ref[j] += _dot_tn(y_attn, da[:, cols])
            dwbp_ref[j] += _dot_tn(y_pool, dp[:, cols])
            pa = _dot_nt(da[:, cols], wba_ref[j])
            pp = _dot_nt(dp[:, cols], wbp_ref[j])
            dy_attn = pa if dy_attn is None else dy_attn + pa
            dy_pool = pp if dy_pool is None else dy_pool + pp

        do = dy_attn * silu_a
        do_ref[...] = do
        dga_ref[...] = (dy_attn * o * (sga * (1.0 + ga * (1.0 - sga)))).astype(BF16)
        doo = do * o
        for hd in range(MLA_HEADS):
            dl = jnp.sum(doo[:, hd * V_HEAD_DIM:(hd + 1) * V_HEAD_DIM], axis=1, keepdims=True)
            dl_ref[:, hd * HEAD_PAD:(hd + 1) * HEAD_PAD] = jnp.broadcast_to(dl, (tm, HEAD_PAD))

        dyp = dy_pool * silu_p
        dgp_ref[...] = (dy_pool * yp * (sgp * (1.0 + gp * (1.0 - sgp)))).astype(BF16)
        dps_ref[...] += _colsum(dyp * dm)
        dmm = (dyp * ps).astype(BF16)
        for g in range(len(POOL_WINDOWS)):
            cols = slice(g * POOL_GROUP_DIM, (g + 1) * POOL_GROUP_DIM)
            dpw_ref[g] += _dot_tn(d_bf[g], dmm[:, cols])
            ddc_ref[:, cols] = _dot_nt(dmm[:, cols], pw_ref[g]) * inv_cnt[g]

        @pl.when(i == n_tiles - 1)
        def _():
            dwout_out[...] = dwout_ref[...].astype(BF16)
            dwba_out[...] = dwba_ref[...].astype(BF16)
            dwbp_out[...] = dwbp_ref[...].astype(BF16)
            _pack_into(small_ref, SMALL_MID, dict(pool_w=dpw_ref[...], norm_final=dgf_ref[...], sq_err=loss_ref[...],
                                                  pool_scale=dps_ref[...]))

    row_in = lambda n: _row_spec(tm, n)
    in_specs = [
        row_in(MLA_WIDTH),
        pl.BlockSpec((1, tm, SHARD_COLS), lambda i: (0, i, 0)), pl.BlockSpec((1, tm, SHARD_COLS), lambda i: (1, i, 0)),
        pl.BlockSpec((1, POOL_HALO, SHARD_COLS), lambda i: (1, jnp.maximum(i * halo_per_tile - 1, 0), 0)),
        pl.BlockSpec((1, tm, SHARD_COLS), lambda i: (2, i, 0)), pl.BlockSpec((1, tm, SHARD_COLS), lambda i: (3, i, 0)),
        row_in(D_MODEL), row_in(D_MODEL),
        _full_spec((4, POOL_GROUP_DIM, POOL_GROUP_DIM)), _full_spec((1, POOL_WIDTH)),
        _full_spec((N_CHIPS, MLA_WIDTH, BRANCH_COLS)), _full_spec((N_CHIPS, POOL_WIDTH, BRANCH_COLS)),
        _full_spec((D_MODEL, D_MODEL)), _full_spec((1, D_MODEL)),
    ]
    out_shape = [
        jax.ShapeDtypeStruct((s, MLA_WIDTH), F32),
        jax.ShapeDtypeStruct((s, MLA_HEADS * HEAD_PAD), F32),
        jax.ShapeDtypeStruct((s, MLA_WIDTH), BF16),
        jax.ShapeDtypeStruct((s, POOL_WIDTH), BF16),
        jax.ShapeDtypeStruct((s, 2 * D_MODEL), BF16),
        jax.ShapeDtypeStruct((s, POOL_WIDTH), F32),
        jax.ShapeDtypeStruct((s, D_MODEL), F32),
        jax.ShapeDtypeStruct((small_rows, LANES), F32),
        jax.ShapeDtypeStruct((D_MODEL, D_MODEL), BF16),
        jax.ShapeDtypeStruct((N_CHIPS, MLA_WIDTH, BRANCH_COLS), BF16),
        jax.ShapeDtypeStruct((N_CHIPS, POOL_WIDTH, BRANCH_COLS), BF16),
    ]
    out_specs = [
        row_in(MLA_WIDTH), row_in(MLA_HEADS * HEAD_PAD), row_in(MLA_WIDTH), row_in(POOL_WIDTH),
        row_in(2 * D_MODEL), row_in(POOL_WIDTH), row_in(D_MODEL),
        _full_spec((small_rows, LANES)), _full_spec((D_MODEL, D_MODEL)),
        _full_spec((N_CHIPS, MLA_WIDTH, BRANCH_COLS)), _full_spec((N_CHIPS, POOL_WIDTH, BRANCH_COLS)),
    ]
    return pl.pallas_call(
        body,
        name="mid",
        grid=(n_tiles,),
        in_specs=in_specs,
        out_specs=out_specs,
        out_shape=out_shape,
        scratch_shapes=[
            pltpu.VMEM((tm + POOL_HALO, POOL_WIDTH), F32),
            pltpu.VMEM((D_MODEL, D_MODEL), F32),
            pltpu.VMEM((N_CHIPS, MLA_WIDTH, BRANCH_COLS), F32),
            pltpu.VMEM((N_CHIPS, POOL_WIDTH, BRANCH_COLS), F32),
            pltpu.VMEM((8, LANES), F32),
            pltpu.VMEM((4, POOL_GROUP_DIM, POOL_GROUP_DIM), F32),
            pltpu.VMEM((1, POOL_WIDTH), F32),
            pltpu.VMEM((1, D_MODEL), F32),
        ],
        compiler_params=pltpu.CompilerParams(dimension_semantics=("arbitrary",), vmem_limit_bytes=VMEM_LIMIT),
    )(o, z_sh, z_sh, z_sh, z_sh, z_sh, x, target, pool_w, pool_scale, w_ba, w_bp, w_out, norm_final)


def _attn_bwd(q, q_t, k, v, do, lse, delta, late_grads, gs_mid, t):
    s = q.shape[0]
    groups = MLA_HEADS // BWD_HEADS
    n_q = s // t
    red = _Reduce(COMM_PARAMS[3:])
    n_w = len(red.params)
    small = _SmallSum(gs_mid.shape[0])
    n_red = len(red.scratch)

    def body(q_ref, qt_ref, do_ref, lse_ref, dl_ref, k_ref, v_ref, *rest):
        g_in, gs_ref = rest[:n_w], rest[n_w]
        (dq_ref, dk_ref, dv_ref), g_out, gsum_ref = rest[n_w + 1:n_w + 4], rest[n_w + 4:2 * n_w + 4], rest[2 * n_w + 4]
        scratch = rest[2 * n_w + 5:]
        red.bind(g_in, g_out, scratch[:n_red])
        small.bind(gs_ref, gsum_ref, scratch[n_red:])
        i = pl.program_id(1)
        step_no = pl.program_id(0) * n_q + i

        @pl.when(step_no == 0)
        def _():
            red.start()
            small.start()

        pl.when(step_no == groups * n_q // 2)(red.exchange)

        @pl.when(i == 0)
        def _():
            dk_ref[...] = jnp.zeros_like(dk_ref)
            dv_ref[...] = jnp.zeros_like(dv_ref)

        mask = _chunk_mask(t, False)
        qcs = [slice(hh * HEAD_PAD, (hh + 1) * HEAD_PAD) for hh in range(BWD_HEADS)]
        vcs = [slice(hh * V_HEAD_DIM, (hh + 1) * V_HEAD_DIM) for hh in range(BWD_HEADS)]
        qhs = [q_ref[:, qc] for qc in qcs]
        qts = [qt_ref[qc, :] for qc in qcs]
        dohs = [do_ref[:, vc].astype(BF16) for vc in vcs]
        do_t = do_ref[...].T.astype(BF16)
        dots = [do_t[vc, :] for vc in vcs]
        lses = [jnp.tile(lse_ref[:, qc], (1, t // HEAD_PAD)) for qc in qcs]
        dls = [jnp.tile(dl_ref[:, qc], (1, t // HEAD_PAD)) for qc in qcs]

        def step(j, dqs, masked):
            keys = pl.ds(pl.multiple_of(j * t, t), t)
            out = []
            for hh in range(BWD_HEADS):
                kj = k_ref[keys, qcs[hh]]
                vj = v_ref[keys, vcs[hh]]
                p = jnp.exp2(_dot_nt(qhs[hh], kj) * ATT_SCALE_LOG2E - lses[hh])
                if masked:
                    p = jnp.where(mask, p, 0.0)
                ds = (p * (_dot_nt(dohs[hh], vj) - dls[hh])).astype(BF16)
                dv_ref[vcs[hh], keys] += _dot(dots[hh], p.astype(BF16))
                dk_ref[qcs[hh], keys] += _dot(qts[hh], ds) * ATT_SCALE
                out.append(dqs[hh] + _dot(ds, kj))
            return tuple(out)

        zero = jnp.zeros((t, HEAD_PAD), F32)
        dqs = lax.fori_loop(0, i, functools.partial(step, masked=False), (zero,) * BWD_HEADS)
        dqs = step(i, dqs, True)
        for hh in range(BWD_HEADS):
            dq_ref[:, qcs[hh]] = dqs[hh] * ATT_SCALE

        @pl.when(step_no == groups * n_q - 1)
        def _():
            red.finish()
            small.finish()

    hw = MLA_HEADS * HEAD_PAD
    any_spec = pl.BlockSpec(memory_space=pl.ANY)
    out = pl.pallas_call(
        body,
        name="attn_bwd",
        grid=(groups, n_q),
        in_specs=[
            pl.BlockSpec((t, BWD_HEADS * HEAD_PAD), lambda p, i: (i, p)),
            pl.BlockSpec((BWD_HEADS * HEAD_PAD, t), lambda p, i: (p, i)),
            pl.BlockSpec((t, BWD_HEADS * V_HEAD_DIM), lambda p, i: (i, p)),
            pl.BlockSpec((t, BWD_HEADS * HEAD_PAD), lambda p, i: (i, p)),
            pl.BlockSpec((t, BWD_HEADS * HEAD_PAD), lambda p, i: (i, p)),
            pl.BlockSpec((s, BWD_HEADS * HEAD_PAD), lambda p, i: (0, p), pipeline_mode=pl.Buffered(1)),
            pl.BlockSpec((s, BWD_HEADS * V_HEAD_DIM), lambda p, i: (0, p), pipeline_mode=pl.Buffered(1)),
        ] + [any_spec] * n_w + [pl.BlockSpec(small.spec_shape, lambda p, i: (0, 0))],
        out_specs=[
            pl.BlockSpec((t, BWD_HEADS * HEAD_PAD), lambda p, i: (i, p)),
            pl.BlockSpec((BWD_HEADS * HEAD_PAD, s), lambda p, i: (p, 0)),
            pl.BlockSpec((BWD_HEADS * V_HEAD_DIM, s), lambda p, i: (p, 0)),
        ] + [any_spec] * n_w + [pl.BlockSpec(small.spec_shape, lambda p, i: (0, 0))],
        out_shape=[jax.ShapeDtypeStruct((s, hw), F32), jax.ShapeDtypeStruct((hw, s), F32),
                   jax.ShapeDtypeStruct((MLA_WIDTH, s), F32)] + red.out_shape + [small.out_shape],
        scratch_shapes=red.scratch + small.scratch,
        compiler_params=pltpu.CompilerParams(dimension_semantics=("arbitrary", "arbitrary"),
                                             vmem_limit_bytes=VMEM_LIMIT),
    )(q, q_t, do, lse, delta, k, v, *late_grads, gs_mid)
    return out[0], out[1], out[2], out[3:3 + n_w], out[3 + n_w]


def _qkv_bwd(dq, dk_t, dv_t, z_sh, q_norm, kv_norm, wuq_p, wk_p, wv, rc, rsa, rsb, tm):
    s = z_sh.shape[1]
    hw = MLA_HEADS * HEAD_PAD
    n_tiles = s // tm
    uq_shape, ukv_shape = (N_CHIPS,) + COMM_PARAMS[1][1:3], (N_CHIPS,) + COMM_PARAMS[2][1:3]

    def body(dq_ref, dk_ref, dv_ref, z_ref, gq_ref, gkv_ref, wuq_ref, wk_ref, wv_ref,
             c_ref, sa_ref, sb_ref,
             dzq_ref, dzkv_ref, dzkr_ref, duq_ref, dukv_ref, dgq_ref, dgkv_ref, dwuq_ref, dwk_ref, dwv_ref):
        i = pl.program_id(0)

        @pl.when(i == 0)
        def _():
            dwuq_ref[...] = jnp.zeros_like(dwuq_ref)
            dwk_ref[...] = jnp.zeros_like(dwk_ref)
            dwv_ref[...] = jnp.zeros_like(dwv_ref)
            dgq_ref[...] = jnp.zeros_like(dgq_ref)
            dgkv_ref[...] = jnp.zeros_like(dgkv_ref)

        c, sa, sb = c_ref[...], sa_ref[...], sb_ref[...]
        gq, gkv = gq_ref[...], gkv_ref[...]

        z0 = z_ref[0]
        cq, xq, rq = _rms_fwd(z0[:, ZQ_COLS], gq)
        dqp = jnp.concatenate(
            [_unrope(dq_ref[:, h * HEAD_PAD:(h + 1) * HEAD_PAD], c, sa, sb) for h in range(MLA_HEADS)],
            axis=1).astype(BF16)
        dwuq_ref[...] += _dot_tn(cq.astype(BF16), dqp)
        dcq = _dot_nt(dqp, wuq_ref[...])
        dgq_ref[...] += _colsum(dcq * xq)
        dzq_ref[...] = _rms_bwd(dcq, xq, rq, gq).astype(BF16)

        ckv, xkv, rkv = _rms_fwd(z0[:, ZKV_COLS], gkv)
        ckv = ckv.astype(BF16)
        dkf = dk_ref[...].T
        dk_bf = dkf.astype(BF16)
        dv_bf = dv_ref[...].T.astype(BF16)
        dwk_ref[...] += _dot_tn(ckv, dk_bf)
        dwv_ref[...] += _dot_tn(ckv, dv_bf)
        dckv = _dot_nt(dk_bf, wk_ref[...]) + _dot_nt(dv_bf, wv_ref[...])
        dgkv_ref[...] += _colsum(dckv * xkv)
        dzkv_ref[...] = _rms_bwd(dckv, xkv, rkv, gkv).astype(BF16)

        dkr = dkf[:, 0:HEAD_PAD]
        for h in range(1, MLA_HEADS):
            dkr = dkr + dkf[:, h * HEAD_PAD:(h + 1) * HEAD_PAD]
        dkr = pltpu.roll(_unrope(dkr, c, sa, sb), 64, 1)
        lane = lax.broadcasted_iota(jnp.int32, (tm, HEAD_PAD), 1)
        dzkr_ref[...] = jnp.where(lane < QK_ROPE_DIM, dkr, 0.0).astype(BF16)

        @pl.when(i == n_tiles - 1)
        def _():
            qk = QK_NOPE_DIM + QK_ROPE_DIM
            d_uq = jnp.concatenate([dwuq_ref[:, h * HEAD_PAD:h * HEAD_PAD + qk] for h in range(MLA_HEADS)],
                                   axis=1).astype(BF16)
            d_ukv = jnp.concatenate(
                [part for h in range(MLA_HEADS)
                 for part in (dwk_ref[:, h * HEAD_PAD:h * HEAD_PAD + QK_NOPE_DIM],
                              dwv_ref[:, h * V_HEAD_DIM:(h + 1) * V_HEAD_DIM])], axis=1).astype(BF16)
            for j in range(N_CHIPS):
                duq_ref[j] = d_uq[j * uq_shape[1]:(j + 1) * uq_shape[1]]
                dukv_ref[j] = d_ukv[j * ukv_shape[1]:(j + 1) * ukv_shape[1]]

    return pl.pallas_call(
        body,
        name="qkv_bwd",
        grid=(s // tm,),
        in_specs=[
            _row_spec(tm, hw), pl.BlockSpec((hw, tm), lambda i: (0, i)), pl.BlockSpec((MLA_WIDTH, tm), lambda i: (0, i)),
            pl.BlockSpec((1, tm, SHARD_COLS), lambda i: (0, i, 0)),
            _full_spec((1, Q_LORA_RANK)), _full_spec((1, KV_LORA_RANK)),
            _full_spec((Q_LORA_RANK, hw)), _full_spec((KV_LORA_RANK, hw)), _full_spec((KV_LORA_RANK, MLA_WIDTH)),
            _row_spec(tm, HEAD_PAD), _row_spec(tm, HEAD_PAD), _row_spec(tm, HEAD_PAD),
        ],
        out_specs=[
            _row_spec(tm, Q_LORA_RANK), _row_spec(tm, KV_LORA_RANK), _row_spec(tm, HEAD_PAD),
            _full_spec(uq_shape), _full_spec(ukv_shape),
            _full_spec((1, Q_LORA_RANK)), _full_spec((1, KV_LORA_RANK)),
        ],
        out_shape=[
            jax.ShapeDtypeStruct((s, Q_LORA_RANK), BF16), jax.ShapeDtypeStruct((s, KV_LORA_RANK), BF16),
            jax.ShapeDtypeStruct((s, HEAD_PAD), BF16),
            jax.ShapeDtypeStruct(uq_shape, BF16), jax.ShapeDtypeStruct(ukv_shape, BF16),
            jax.ShapeDtypeStruct((1, Q_LORA_RANK), F32), jax.ShapeDtypeStruct((1, KV_LORA_RANK), F32),
        ],
        scratch_shapes=[pltpu.VMEM((Q_LORA_RANK, hw), F32), pltpu.VMEM((KV_LORA_RANK, hw), F32),
                        pltpu.VMEM((KV_LORA_RANK, MLA_WIDTH), F32)],
        compiler_params=pltpu.CompilerParams(dimension_semantics=("arbitrary",), vmem_limit_bytes=VMEM_LIMIT),
    )(dq, dk_t, dv_t, z_sh, q_norm, kv_norm, wuq_p, wk_p, wv, rc, rsa, rsb)


def _inproj_bwd_x(dzq, dzkv, dzkr, dgattn, ddc, dgpool, dgmerge, x, dh, norm_in, d_q_norm, d_kv_norm, w_in_t, tm):
    s = x.shape[0]
    n_tiles = s // tm
    halo_per_tile = tm // POOL_HALO
    n_halo = s // POOL_HALO
    u_seg = 4
    small_rows = _small_rows(SMALL_LATE)

    def body(dzq_ref, dzkv_ref, dzkr_ref, dga_ref, ddc_ref, ddn_ref, dgp_ref, dgm_ref, x_ref, dh_ref,
             g_ref, dgq_ref, dgkv_ref, w_hbm, gx_ref, small_ref, dzs_ref, w_vmem, dbuf, sem, dgin_ref):
        i = pl.program_id(0)

        @pl.when(i == 0)
        def _():
            cp = pltpu.make_async_copy(w_hbm, w_vmem, sem)
            cp.start()
            dgin_ref[...] = jnp.zeros_like(dgin_ref)
            cp.wait()

        dbuf[0:tm, :] = ddc_ref[...]
        dbuf[tm:, :] = jnp.where(i < n_tiles - 1, ddn_ref[...], 0.0)
        row = lax.broadcasted_iota(jnp.int32, (tm, POOL_GROUP_DIM), 0) + i * tm
        du = []
        for g, w in enumerate(POOL_WINDOWS):
            cols = slice(g * POOL_GROUP_DIM, (g + 1) * POOL_GROUP_DIM)
            fsum = dbuf[0:tm, cols]
            for kk in range(1, w):
                fsum = fsum + dbuf[kk:kk + tm, cols]
            du.append(fsum - dbuf[0:tm, cols] * jnp.minimum(row + 1, w).astype(F32))
        du = jnp.concatenate(du, axis=1).astype(BF16)

        dz = [dzq_ref[...], dzkv_ref[...], dzkr_ref[...], dga_ref[...], du, dgp_ref[...], dgm_ref[...]]
        dz = jnp.concatenate([d[:, :w] for d, (w, _) in zip(dz, IN_SEGMENTS)], axis=1)
        for j in range(N_CHIPS):
            dzs_ref[j] = dz[:, j * SHARD_COLS:(j + 1) * SHARD_COLS].T
        dhn = _dot(dz, w_vmem[...])

        g = g_ref[...]
        _, xhat, r = _rms_fwd(x_ref[...], g)
        dgin_ref[...] += _colsum(dhn * xhat)
        gx_ref[...] = dh_ref[...] + _rms_bwd(dhn, xhat, r, g)

        @pl.when(i == n_tiles - 1)
        def _():
            _pack_into(small_ref, SMALL_LATE, dict(norm_in=dgin_ref[...], q_norm=dgq_ref[...], kv_norm=dgkv_ref[...]))

    any_spec = pl.BlockSpec(memory_space=pl.ANY)
    seg_w = [wide for _, wide in IN_SEGMENTS]
    return pl.pallas_call(
        body,
        name="inproj_bwd_x",
        grid=(n_tiles,),
        in_specs=[
            _row_spec(tm, seg_w[0]), _row_spec(tm, seg_w[1]), _row_spec(tm, seg_w[2]),
            _row_spec(tm, seg_w[3]), _row_spec(tm, seg_w[u_seg]),
            pl.BlockSpec((POOL_HALO, POOL_WIDTH), lambda i: (jnp.minimum((i + 1) * halo_per_tile, n_halo - 1), 0)),
            _row_spec(tm, seg_w[5]), _row_spec(tm, seg_w[6]),
            _row_spec(tm, D_MODEL), _row_spec(tm, D_MODEL),
            _full_spec((1, D_MODEL)), _full_spec((1, Q_LORA_RANK)), _full_spec((1, KV_LORA_RANK)), any_spec,
        ],
        out_specs=[_row_spec(tm, D_MODEL), _full_spec((small_rows, LANES)),
                   pl.BlockSpec((N_CHIPS, SHARD_COLS, tm), lambda i: (0, 0, i))],
        out_shape=[jax.ShapeDtypeStruct((s, D_MODEL), F32), jax.ShapeDtypeStruct((small_rows, LANES), F32),
                   jax.ShapeDtypeStruct((N_CHIPS, SHARD_COLS, s), BF16)],
        scratch_shapes=[
            pltpu.VMEM((IN_TOTAL, D_MODEL), BF16),
            pltpu.VMEM((tm + POOL_HALO, POOL_WIDTH), F32),
            pltpu.SemaphoreType.DMA,
            pltpu.VMEM((1, D_MODEL), F32),
        ],
        compiler_params=pltpu.CompilerParams(dimension_semantics=("arbitrary",), vmem_limit_bytes=VMEM_LIMIT),
    )(dzq, dzkv, dzkr, dgattn, ddc, ddc, dgpool, dgmerge, x, dh, norm_in, d_q_norm, d_kv_norm,
      w_in_t.reshape(IN_TOTAL, D_MODEL))


def _inproj_bwd_w(order, dz_sh, hn, g_uq, g_ukv, gs, tm):
    s = hn.shape[0]
    n_tiles = s // tm
    hc = D_MODEL // 2
    red = _Reduce(COMM_PARAMS[1:3])
    small = _SmallSum(gs.shape[0])
    n_red = len(red.scratch)

    def body(order_ref, dz_ref, hn_ref, guq_hbm, gukv_hbm, gs_ref, gw_hbm, guq_out, gukv_out, gsum_ref,
             acc, pm_w, a_w, b_w, r_w, w_send, w_recv, w_local, *more_scratch):
        ph, i = pl.program_id(0), pl.program_id(1)
        x, y, c = lax.axis_index("x"), lax.axis_index("y"), lax.axis_index("c")
        k = 2 * x + y
        me, sibling = (x, y, c), (x, y, 1 - c)
        chips = _other_chips(x, y)
        shard_of_phase = [2 * cx + cy for cx, cy in chips] + [k]
        copy = _remote_copier(w_send, w_recv)
        red.bind([guq_hbm, gukv_hbm], [guq_out, gukv_out], more_scratch[:n_red])
        small.bind(gs_ref, gsum_ref, more_scratch[n_red:])
        mine = pl.ds(pl.multiple_of(c * hc, hc), hc)
        theirs = pl.ds(pl.multiple_of((1 - c) * hc, hc), hc)

        def to_sibling(f):
            j = shard_of_phase[f]
            return copy(f, pm_w.at[j, 1 - c], a_w.at[j], sibling)

        def pair_sum(f):
            cx, cy = chips[f]
            return copy(4 + f, pm_w.at[shard_of_phase[f], c], b_w.at[f], (cx, cy, c))

        def finished():
            return copy(7, r_w, gw_hbm.at[:, mine], sibling)

        @pl.when(jnp.logical_and(ph == 0, i == 0))
        def _():
            red.start()
            small.start()

        part = _dot(dz_ref[0], hn_ref[...])

        @pl.when(i == 0)
        def _():
            acc[...] = part

        @pl.when(i > 0)
        def _():
            acc[...] += part

        for f in range(3):
            @pl.when(jnp.logical_and(ph == f + 1, i == 0))
            def _(f=f):
                j = shard_of_phase[f]
                copy(f, a_w.at[j], a_w.at[j], me).wait_recv()
                pm_w[j, c] = (pm_w[j, c].astype(F32) + a_w[j].astype(F32)).astype(BF16)
                pair_sum(f).start()
                if f == 0:
                    red.exchange()

        for f in range(4):
            @pl.when(jnp.logical_and(ph == f, i == n_tiles - 1))
            def _(f=f):
                j = shard_of_phase[f]
                pm_w[j, 0] = acc[:, :hc].astype(BF16)
                pm_w[j, 1] = acc[:, hc:].astype(BF16)
                to_sibling(f).start()
                if f < 3:
                    return
                copy(3, a_w.at[k], a_w.at[k], me).wait_recv()
                r_w[...] = pm_w[k, c].astype(F32) + a_w[k].astype(F32)
                for g in range(3):
                    copy(4 + g, b_w.at[g], b_w.at[g], me).wait_recv()
                    r_w[...] = r_w[...] + b_w[g].astype(F32)
                store = pltpu.make_async_copy(r_w, gw_hbm.at[:, mine], w_local)
                store.start()
                finished().start()
                red.finish()
                small.finish()
                copy(7, gw_hbm.at[:, theirs], gw_hbm.at[:, theirs], me).wait_recv()
                store.wait()
                for g in range(4):
                    to_sibling(g).wait_send()
                for g in range(3):
                    pair_sum(g).wait_send()
                finished().wait_send()

    any_spec = pl.BlockSpec(memory_space=pl.ANY)
    n_sem = 8
    grid_spec = pltpu.PrefetchScalarGridSpec(
        num_scalar_prefetch=1,
        grid=(N_CHIPS, n_tiles),
        in_specs=[
            pl.BlockSpec((1, SHARD_COLS, tm), lambda ph, i, order: (order[ph], 0, i)),
            pl.BlockSpec((tm, D_MODEL), lambda ph, i, order: (i, 0)),
            any_spec, any_spec,
            pl.BlockSpec(small.spec_shape, lambda ph, i, order: (0, 0)),
        ],
        out_specs=[any_spec, any_spec, any_spec, pl.BlockSpec(small.spec_shape, lambda ph, i, order: (0, 0))],
        scratch_shapes=[
            pltpu.VMEM((SHARD_COLS, D_MODEL), F32),
            pltpu.VMEM((N_CHIPS, 2, SHARD_COLS, hc), BF16),
            pltpu.VMEM((N_CHIPS, SHARD_COLS, hc), BF16),
            pltpu.VMEM((3, SHARD_COLS, hc), BF16),
            pltpu.VMEM((SHARD_COLS, hc), F32),
            pltpu.SemaphoreType.DMA((n_sem,)), pltpu.SemaphoreType.DMA((n_sem,)), pltpu.SemaphoreType.DMA,
        ] + red.scratch + small.scratch,
    )
    out = pl.pallas_call(
        body,
        name="inproj_bwd_w",
        grid_spec=grid_spec,
        out_shape=[jax.ShapeDtypeStruct((SHARD_COLS, D_MODEL), F32)] + red.out_shape
        + [small.out_shape],
        compiler_params=pltpu.CompilerParams(dimension_semantics=("arbitrary", "arbitrary"),
                                             vmem_limit_bytes=VMEM_LIMIT),
    )(order, dz_sh, hn, g_uq, g_ukv, gs)
    return out[0], out[1], out[2], out[3]


def _other_chips(x, y):
    return ((1 - x, 1 - y), (1 - x, y), (x, 1 - y))


def _half(ref, axis, size, c, lead=()):
    window = pl.ds(pl.multiple_of(c * size, size), size)
    if axis == 0:
        return ref.at[(*lead, window, slice(None))]
    return ref.at[(*lead, slice(None), window)]


def _half_shape(rows, cols, axis, size):
    return (size, cols) if axis == 0 else (rows, size)


def _remote_copier(send_sems, recv_sems):
    def copy(sem, src, dst, to):
        return pltpu.make_async_remote_copy(src_ref=src, dst_ref=dst, send_sem=send_sems.at[sem],
                                            recv_sem=recv_sems.at[sem], device_id=to, device_id_type=MESH)
    return copy


class _Gather:
    def __init__(self, params):
        self.params = params
        n = len(params)
        self.scratch = [pltpu.SemaphoreType.DMA((6 * n,)), pltpu.SemaphoreType.DMA((6 * n,)),
                        pltpu.SemaphoreType.DMA((n,))]
        self.out_shape = [jax.ShapeDtypeStruct((N_CHIPS, r, cc), BF16) for _, r, cc, _, _ in params]

    def bind(self, ins, outs, scratch):
        self.ins, self.outs = ins, outs
        send_sems, recv_sems, self.local_sems = scratch
        self.copy = _remote_copier(send_sems, recv_sems)
        self.x, self.y, self.c = lax.axis_index("x"), lax.axis_index("y"), lax.axis_index("c")
        self.k = 2 * self.x + self.y
        self.chips = _other_chips(self.x, self.y)

    def _local(self, p):
        return pltpu.make_async_copy(self.ins[p], self.outs[p].at[self.k], self.local_sems.at[p])

    def _first(self, p, j):
        _, _, _, axis, size = self.params[p]
        cx, cy = self.chips[j]
        return self.copy(6 * p + j, _half(self.ins[p], axis, size, self.c),
                         _half(self.outs[p], axis, size, self.c, (self.k,)), (cx, cy, self.c))

    def _relay(self, p, j, half_of):
        _, _, _, axis, size = self.params[p]
        cx, cy = self.chips[j]
        block = _half(self.outs[p], axis, size, half_of, (2 * cx + cy,))
        return self.copy(6 * p + 3 + j, block, block, (self.x, self.y, 1 - self.c))

    def start(self):
        for p in range(len(self.params)):
            self._local(p).start()
            for j in (1, 2, 0):
                self._first(p, j).start()

    def relay_one(self, p, j):
        _, _, _, axis, size = self.params[p]
        cx, cy = self.chips[j]
        landed = _half(self.outs[p], axis, size, self.c, (2 * cx + cy,))
        self.copy(6 * p + j, landed, landed, (self.x, self.y, self.c)).wait_recv()
        self._relay(p, j, self.c).start()

    def await_one(self, p, j):
        self._relay(p, j, 1 - self.c).wait_recv()

    def wait_sends(self):
        for p in range(len(self.params)):
            for j in range(3):
                self._first(p, j).wait_send()
                self._relay(p, j, self.c).wait_send()
            self._local(p).wait()

    def relay(self):
        for j in range(3):
            for p in range(len(self.params)):
                self.relay_one(p, j)

    def finish(self):
        for j in range(3):
            for p in range(len(self.params)):
                self.await_one(p, j)
        self.wait_sends()


class _Reduce:
    def __init__(self, params):
        self.params = params
        n = len(params)
        halves = [_half_shape(r, cc, axis, size) for _, r, cc, axis, size in params]
        self.scratch = ([pltpu.VMEM((N_CHIPS, *h), BF16) for h in halves]
                        + [pltpu.VMEM((N_CHIPS, *h), BF16) for h in halves]
                        + [pltpu.VMEM((3, *h), BF16) for h in halves]
                        + [pltpu.VMEM(h, F32) for h in halves]
                        + [pltpu.SemaphoreType.DMA((5 * n,)), pltpu.SemaphoreType.DMA((5 * n,)),
                           pltpu.SemaphoreType.DMA((2 * n,))])
        self.out_shape = [jax.ShapeDtypeStruct((r, cc), F32) for _, r, cc, _, _ in params]

    def bind(self, g_in, g_out, scratch):
        n = len(self.params)
        self.g_in, self.g_out = g_in, g_out
        self.pm, self.a_buf = scratch[0:n], scratch[n:2 * n]
        self.b_buf, self.r_buf = scratch[2 * n:3 * n], scratch[3 * n:4 * n]
        send_sems, recv_sems, self.local_sems = scratch[4 * n:]
        self.copy = _remote_copier(send_sems, recv_sems)
        self.x, self.y, self.c = lax.axis_index("x"), lax.axis_index("y"), lax.axis_index("c")
        self.k = 2 * self.x + self.y
        self.chips = _other_chips(self.x, self.y)
        self.me = (self.x, self.y, self.c)
        self.sibling = (self.x, self.y, 1 - self.c)

    def _load(self, p):
        _, _, _, axis, size = self.params[p]
        return pltpu.make_async_copy(_half(self.g_in[p], axis, size, self.c, (slice(None),)), self.pm[p],
                                     self.local_sems.at[p])

    def _to_sibling(self, p):
        _, _, _, axis, size = self.params[p]
        return self.copy(5 * p, _half(self.g_in[p], axis, size, 1 - self.c, (slice(None),)), self.a_buf[p],
                         self.sibling)

    def _pair_sum(self, p, j):
        cx, cy = self.chips[j]
        return self.copy(5 * p + 1 + j, self.pm[p].at[2 * cx + cy], self.b_buf[p].at[j], (cx, cy, self.c))

    def _store(self, p):
        _, _, _, axis, size = self.params[p]
        n = len(self.params)
        return pltpu.make_async_copy(self.r_buf[p], _half(self.g_out[p], axis, size, self.c),
                                     self.local_sems.at[n + p])

    def _finished(self, p):
        _, _, _, axis, size = self.params[p]
        return self.copy(5 * p + 4, self.r_buf[p], _half(self.g_out[p], axis, size, self.c), self.sibling)

    def start(self):
        for p in range(len(self.params)):
            self._load(p).start()
            self._to_sibling(p).start()

    def exchange(self):
        for p in range(len(self.params)):
            self._load(p).wait()
            self.copy(5 * p, self.a_buf[p], self.a_buf[p], self.me).wait_recv()
            for j, (cx, cy) in enumerate(self.chips):
                kj = 2 * cx + cy
                self.pm[p][kj] = (self.pm[p][kj].astype(F32) + self.a_buf[p][kj].astype(F32)).astype(BF16)
                self._pair_sum(p, j).start()
            self.r_buf[p][...] = self.pm[p][self.k].astype(F32) + self.a_buf[p][self.k].astype(F32)

    def finish(self):
        for p, (_, _, _, axis, size) in enumerate(self.params):
            for j in range(3):
                self.copy(5 * p + 1 + j, self.b_buf[p].at[j], self.b_buf[p].at[j], self.me).wait_recv()
                self.r_buf[p][...] = self.r_buf[p][...] + self.b_buf[p][j].astype(F32)
            self._store(p).start()
            self._finished(p).start()
        for p, (_, _, _, axis, size) in enumerate(self.params):
            theirs = _half(self.g_out[p], axis, size, 1 - self.c)
            self.copy(5 * p + 4, theirs, theirs, self.me).wait_recv()
            self._store(p).wait()
            self._to_sibling(p).wait_send()
            for j in range(3):
                self._pair_sum(p, j).wait_send()
            self._finished(p).wait_send()


class _SmallSum:
    def __init__(self, rows):
        self.rows = rows
        self.scratch = [pltpu.VMEM((N_DEV, rows, LANES), F32),
                        pltpu.SemaphoreType.DMA((N_DEV - 1,)), pltpu.SemaphoreType.DMA((N_DEV - 1,))]
        self.out_shape = jax.ShapeDtypeStruct((rows, LANES), F32)
        self.spec_shape = (rows, LANES)

    def bind(self, src, dst, scratch):
        self.src, self.dst = src, dst
        self.buf, send_sems, recv_sems = scratch
        self.copy = _remote_copier(send_sems, recv_sems)
        self.x, self.y, self.c = lax.axis_index("x"), lax.axis_index("y"), lax.axis_index("c")

    def _send(self, f):
        fx, fy, fc = [(a, b, d) for a in (0, 1) for b in (0, 1) for d in (0, 1)][f]
        x, y, c = self.x, self.y, self.c
        peer = (1 - x if fx else x, 1 - y if fy else y, 1 - c if fc else c)
        return self.copy(f - 1, self.src, self.buf.at[f], peer)

    def start(self):
        for f in range(1, N_DEV):
            self._send(f).start()
        self.buf[0] = self.src[...]

    def finish(self):
        me = (self.x, self.y, self.c)
        for f in range(1, N_DEV):
            self.copy(f - 1, self.buf.at[f], self.buf.at[f], me).wait_recv()
        dev = 4 * self.x + 2 * self.y + self.c
        total = self.buf[dev]
        for d in range(1, N_DEV):
            total = total + self.buf[jnp.bitwise_xor(dev, d)]
        self.dst[...] = total
        for f in range(1, N_DEV):
            self._send(f).wait_send()


def _adamw_math(w, g, m, v):
    m = ADAM_B1 * m + (1.0 - ADAM_B1) * g
    v = ADAM_B2 * v + (1.0 - ADAM_B2) * (g * g)
    m_hat = m / (1.0 - ADAM_B1 ** ADAM_STEP)
    v_hat = v / (1.0 - ADAM_B2 ** ADAM_STEP)
    delta = -ADAM_LR * (m_hat / (jnp.sqrt(v_hat) + ADAM_EPS) + ADAM_WD * w)
    return delta, m, v


def _adamw_tiled(w, g, m, v, tm):
    rows, cols = w.shape

    def body(w_ref, g_ref, m_ref, v_ref, d_ref, nm_ref, nv_ref, g_out):
        g = g_ref[...]
        d_ref[...], nm_ref[...], nv_ref[...] = _adamw_math(w_ref[...], g, m_ref[...], v_ref[...])
        g_out[...] = g

    spec = _row_spec(tm, cols)
    return pl.pallas_call(
        body,
        name="adamw_w_in",
        grid=(rows // tm,),
        in_specs=[spec] * 4,
        out_specs=[spec] * 4,
        out_shape=[jax.ShapeDtypeStruct(w.shape, F32)] * 4,
        compiler_params=pltpu.CompilerParams(dimension_semantics=("parallel",), vmem_limit_bytes=VMEM_LIMIT),
    )(w, g, m, v)


def _adamw_many(ws, gs, ms, vs):
    n = len(ws)
    g_arrays, g_at = [], []
    for g in gs:
        arr, row = g if isinstance(g, tuple) else (g, None)
        k = next((j for j, a in enumerate(g_arrays) if a is arr), len(g_arrays))
        if k == len(g_arrays):
            g_arrays.append(arr)
        g_at.append((k, row))
    n_g = len(g_arrays)

    def body(*refs):
        w_refs, m_refs, v_refs, g_refs, outs = (refs[:n], refs[n:2 * n], refs[2 * n:3 * n], refs[3 * n:3 * n + n_g],
                                                refs[3 * n + n_g:])
        for i in range(n):
            k, row = g_at[i]
            g = g_refs[k][...] if row is None else g_refs[k][row:row + ws[i].shape[0], :]
            d, nm, nv = _adamw_math(w_refs[i][...], g, m_refs[i][...], v_refs[i][...])
            outs[i][...] = d
            outs[n + i][...] = nm
            outs[2 * n + i][...] = nv
            outs[3 * n + i][...] = g

    vmem_spec = pl.BlockSpec(memory_space=pltpu.VMEM)
    shapes = [jax.ShapeDtypeStruct(w.shape, F32) for w in ws]
    out = pl.pallas_call(
        body,
        name="adamw_small",
        in_specs=[vmem_spec] * (3 * n + n_g),
        out_specs=[vmem_spec] * (4 * n),
        out_shape=shapes * 4,
        compiler_params=pltpu.CompilerParams(vmem_limit_bytes=VMEM_LIMIT),
    )(*ws, *ms, *vs, *g_arrays)
    return out[:n], out[n:2 * n], out[2 * n:3 * n], out[3 * n:]


def _rope_tables(s):
    half = QK_ROPE_DIM // 2
    inv_freq = np.float32(ROPE_THETA) ** (-np.arange(half, dtype=np.float32) / np.float32(half))
    ang = (np.arange(s, dtype=np.float32)[:, None] * inv_freq[None, :]).astype(np.float32)
    cos, sin = np.cos(ang.astype(np.float64)).astype(np.float32), np.sin(ang.astype(np.float64)).astype(np.float32)
    z16 = np.zeros((s, half), np.float32)
    z32 = np.zeros((s, HEAD_PAD - QK_NOPE_DIM - QK_ROPE_DIM), np.float32)
    z64 = np.zeros((s, QK_NOPE_DIM), np.float32)
    rc = np.concatenate([np.ones((s, QK_NOPE_DIM), np.float32), cos, cos, z32], axis=1)
    rsa = np.concatenate([z64, -sin, z16, z32], axis=1)
    rsb = np.concatenate([z64, z16, sin, z32], axis=1)
    return jnp.asarray(rc), jnp.asarray(rsa), jnp.asarray(rsb)


def kernel(x, norm_in, w_in, q_norm, w_uq, kv_norm, w_ukv, pool_w, pool_scale, w_branch_attn, w_branch_pool, w_out, norm_final, loss_target, m_norm_in, m_w_in, m_q_norm, m_w_uq, m_kv_norm, m_w_ukv, m_pool_w, m_pool_scale, m_w_branch_attn, m_w_branch_pool, m_w_out, m_norm_final, v_norm_in, v_w_in, v_q_norm, v_w_uq, v_kv_norm, v_w_ukv, v_pool_w, v_pool_scale, v_w_branch_attn, v_w_branch_pool, v_w_out, v_norm_final):
    s = x.shape[1]
    t_att, t_row = _tiles(s)
    x2 = x.reshape(s, D_MODEL)
    tgt = loss_target.reshape(s, D_MODEL)

    local = [w_in.T, w_uq.reshape(96, 768), w_ukv.reshape(64, 1024), w_branch_attn, w_branch_pool, w_out]
    local = [a.astype(BF16) for a in local]
    cx, cy = lax.axis_index("x"), lax.axis_index("y")
    others = [2 * ox + oy for ox, oy in _other_chips(cx, cy)]
    hn, z_sh, (w_in_t, w_uq_all, w_ukv_all) = _inproj_fwd(
        jnp.stack([2 * cx + cy, others[1], others[2], others[0]]).astype(jnp.int32), x2, norm_in.reshape(1, -1),
        local[:3], 4 * t_row)
    rc, rsa, rsb = _rope_tables(s)
    g_in = norm_in.reshape(1, -1)
    g_q = q_norm.reshape(1, -1)
    g_kv = kv_norm.reshape(1, -1)
    g_f = norm_final.reshape(1, -1)
    ps = pool_scale.reshape(1, -1)
    pw_bf = pool_w.astype(BF16)

    q, k, v, q_t, v_t, wuq_p, wk_p, wv = _qkv_fwd(z_sh, g_q, g_kv, w_uq_all, w_ukv_all, rc, rsa, rsb, 2 * t_row)
    o, lse, (w_ba_all, w_bp_all, w_out_all) = _attn_fwd(q_t, k, v_t, local[3:], t_att)
    w_out_f = w_out_all.reshape(D_MODEL, D_MODEL)

    do, delta, dgattn, dgpool, dgmerge, ddc, dh, gs_mid, d_w_out, d_w_ba, d_w_bp = _mid(o, z_sh, x2, tgt, pw_bf, ps, w_ba_all, w_bp_all, w_out_f, g_f, t_row)

    late_grads = [d_w_ba, d_w_bp, d_w_out.reshape(N_CHIPS, 256, D_MODEL)]
    dq, dk_t, dv_t, (g_w_ba, g_w_bp, g_w_out), g_small_mid = _attn_bwd(q, q_t, k, v, do, lse, delta, late_grads,
                                                                      gs_mid, t_att)
    sq_err_row = _first_rows(SMALL_MID)["sq_err"]
    sq_err_all = g_small_mid[sq_err_row:sq_err_row + 8]
    dzq, dzkv, dzkr, d_w_uq, d_w_ukv, d_q_norm, d_kv_norm = _qkv_bwd(
        dq, dk_t, dv_t, z_sh, g_q, g_kv, wuq_p, wk_p, wv, rc, rsa, rsb, 2 * t_row)
    grad_x, gs, dz_sh = _inproj_bwd_x(dzq, dzkv, dzkr, dgattn, ddc, dgpool, dgmerge, x2, dh, g_in, d_q_norm, d_kv_norm,
                                      w_in_t, 2 * t_row)

    order = jnp.stack(others + [2 * cx + cy]).astype(jnp.int32)
    g_w_in_t, g_w_uq, g_w_ukv, g_small = _inproj_bwd_w(
        order, dz_sh, hn, d_w_uq, d_w_ukv, gs, 4 * t_row)
    g_w_uq = g_w_uq.reshape(w_uq.shape)
    g_w_ukv = g_w_ukv.reshape(w_ukv.shape)

    dl_w_in, nm_w_in, nv_w_in, g_w_in = (a.T for a in _adamw_tiled(w_in.T, g_w_in_t, m_w_in.T, v_w_in.T, 152))

    packed = {n: (g_small_mid, r) for n, r in _first_rows(SMALL_MID).items() if n != "sq_err"}
    packed.update({n: (g_small, r) for n, r in _first_rows(SMALL_LATE).items()})

    def as_rows(n, a):
        return a.reshape(-1, LANES) if n in packed else a

    names = ["norm_in", "q_norm", "w_uq", "kv_norm", "w_ukv", "pool_w", "pool_scale", "w_branch_attn",
             "w_branch_pool", "w_out", "norm_final"]
    ws = dict(norm_in=norm_in, q_norm=q_norm, w_uq=w_uq, kv_norm=kv_norm, w_ukv=w_ukv, pool_w=pool_w,
              pool_scale=pool_scale, w_branch_attn=w_branch_attn, w_branch_pool=w_branch_pool, w_out=w_out,
              norm_final=norm_final)
    gsd = dict(packed, w_uq=g_w_uq, w_ukv=g_w_ukv, w_branch_attn=g_w_ba, w_branch_pool=g_w_bp, w_out=g_w_out)
    msd = dict(norm_in=m_norm_in, q_norm=m_q_norm, w_uq=m_w_uq, kv_norm=m_kv_norm, w_ukv=m_w_ukv, pool_w=m_pool_w,
               pool_scale=m_pool_scale, w_branch_attn=m_w_branch_attn, w_branch_pool=m_w_branch_pool, w_out=m_w_out,
               norm_final=m_norm_final)
    vsd = dict(norm_in=v_norm_in, q_norm=v_q_norm, w_uq=v_w_uq, kv_norm=v_kv_norm, w_ukv=v_w_ukv, pool_w=v_pool_w,
               pool_scale=v_pool_scale, w_branch_attn=v_w_branch_attn, w_branch_pool=v_w_branch_pool, w_out=v_w_out,
               norm_final=v_norm_final)
    dls, nms, nvs, g_outs = _adamw_many([as_rows(n, ws[n]) for n in names], [gsd[n] for n in names],
                                        [as_rows(n, msd[n]) for n in names], [as_rows(n, vsd[n]) for n in names])

    grads = dict(zip(names, g_outs))
    grads["w_in"] = g_w_in
    delta_w = {n: d.reshape(ws[n].shape) for n, d in zip(names, dls)}
    new_m = {n: d.reshape(ws[n].shape) for n, d in zip(names, nms)}
    new_v = {n: d.reshape(ws[n].shape) for n, d in zip(names, nvs)}
    delta_w["w_in"], new_m["w_in"], new_v["w_in"] = dl_w_in, nm_w_in, nv_w_in
    ws["w_in"] = w_in

    order = ["norm_in", "w_in", "q_norm", "w_uq", "kv_norm", "w_ukv", "pool_w", "pool_scale", "w_branch_attn",
             "w_branch_pool", "w_out", "norm_final"]
    loss = 0.5 * jnp.sum(sq_err_all) / D_MODEL
    return (loss, grad_x.reshape(x.shape),
            *[grads[n].reshape(ws[n].shape) for n in order],
            *[delta_w[n] for n in order], *[new_m[n] for n in order], *[new_v[n] for n in order])
```

```python
import functools

import jax
import jax.numpy as jnp
import numpy as np
from jax import lax
from jax.experimental import pallas as pl
from jax.experimental.pallas import tpu as pltpu

F32 = jnp.float32
BF16 = jnp.bfloat16
MESH = pl.DeviceIdType.MESH

D_MODEL = 1024
CHUNK = 64
MLA_HEADS = 8
QK_NOPE_DIM = 64
QK_ROPE_DIM = 32
V_HEAD_DIM = 64
Q_LORA_RANK = 384
KV_LORA_RANK = 256
MLA_WIDTH = MLA_HEADS * V_HEAD_DIM
ROPE_THETA = 10000.0
POOL_WINDOWS = (2, 4, 8, 16)
POOL_WIDTH = 512
POOL_GROUP_DIM = 128
BRANCH_COLS = D_MODEL // 4
FWD_HEADS = 8
BWD_HEADS = 4
POOL_HALO = 16
EPS = 1e-6
IN_TOTAL = 4256
HEAD_PAD = 128
ATT_SCALE = (QK_NOPE_DIM + QK_ROPE_DIM) ** -0.5
ATT_SCALE_LOG2E = ATT_SCALE * 1.4426950408889634

ADAM_LR = 0.001
ADAM_B1 = 0.9
ADAM_B2 = 0.999
ADAM_EPS = 1e-08
ADAM_WD = 0.01
ADAM_STEP = 10

N_CHIPS = 4
N_DEV = 8
LANES = 128
VMEM_LIMIT = 60 * 1024 * 1024

IN_SEGMENTS = ((384, 384), (256, 256), (32, HEAD_PAD), (512, 512), (512, 512), (512, 512), (2048, 2048))
SHARD_COLS = IN_TOTAL // N_CHIPS
ZQ_COLS = slice(0, 384)
ZKV_COLS = slice(384, 640)
ZKR_TILE = slice(640, 768)


def _shard_pieces():
    bounds, off = [], 0
    for w, _ in IN_SEGMENTS:
        bounds.append((off, off + w))
        off += w
    out = []
    for j in range(N_CHIPS):
        lo, hi = SHARD_COLS * j, SHARD_COLS * (j + 1)
        out.append([(i, max(lo, a) - a, min(hi, b) - a, max(lo, a) - lo)
                    for i, (a, b) in enumerate(bounds) if max(lo, a) < min(hi, b)])
    return out


SHARD_PIECES = _shard_pieces()


def _segment(z_blocks, seg):
    parts = [z_blocks[j][:, col:col + hi - lo]
             for j, pieces in enumerate(SHARD_PIECES) for sg, lo, hi, col in pieces if sg == seg]
    return parts[0] if len(parts) == 1 else jnp.concatenate(parts, axis=1)

COMM_PARAMS = (
    ("w_in", SHARD_COLS, D_MODEL, 1, 512),
    ("w_uq", 96, 768, 0, 48),
    ("w_ukv", 64, 1024, 0, 32),
    ("w_branch_attn", 512, 256, 0, 256),
    ("w_branch_pool", 512, 256, 0, 256),
    ("w_out", 256, 1024, 0, 128),
)

SMALL_MID = (
    ("pool_w", (4, 128, 128)),
    ("norm_final", (1024,)),
    ("sq_err", (8, 128)),
    ("pool_scale", (512,)),
)
SMALL_LATE = (
    ("norm_in", (1024,)),
    ("q_norm", (384,)),
    ("kv_norm", (256,)),
)


def _small_rows(shapes):
    return -(-sum(int(np.prod(s)) for _, s in shapes) // (LANES * 8)) * 8


def _first_rows(shapes):
    out, off = {}, 0
    for name, shp in shapes:
        out[name], rem = divmod(off, LANES)
        assert rem == 0, name
        off += int(np.prod(shp))
    return out


def _pack_into(dst_ref, shapes, values):
    first = _first_rows(shapes)
    for name, shp in shapes:
        v, row = values[name], first[name]
        if v.ndim == 3:
            for g in range(v.shape[0]):
                dst_ref[row + g * v.shape[1]:row + (g + 1) * v.shape[1], :] = v[g]
        elif v.shape[0] == 1 and v.shape[1] > LANES:
            for j in range(v.shape[1] // LANES):
                dst_ref[row + j:row + j + 1, :] = v[:, j * LANES:(j + 1) * LANES]
        else:
            dst_ref[row:row + v.shape[0], :] = v
    used = sum(int(np.prod(shp)) for _, shp in shapes) // LANES
    if used < dst_ref.shape[0]:
        dst_ref[used:, :] = jnp.zeros((dst_ref.shape[0] - used, LANES), dst_ref.dtype)


def _as_one_row(g):
    return jnp.concatenate([g[j:j + 1] for j in range(g.shape[0])], axis=1)


def _dot(a, b):
    return jnp.dot(a, b, preferred_element_type=F32)


def _dot_nt(a, b):
    return lax.dot_general(a, b, (((1,), (1,)), ((), ())), preferred_element_type=F32)


def _dot_tn(a, b):
    return lax.dot_general(a, b, (((0,), (0,)), ((), ())), preferred_element_type=F32)


def _sigmoid(x):
    return 1.0 / (1.0 + jnp.exp(-x))


def _colsum(x):
    return jnp.sum(x, axis=0, keepdims=True)


def _rms_fwd(x, g):
    r = lax.rsqrt(jnp.mean(x * x, axis=-1, keepdims=True) + EPS)
    xhat = x * r
    return xhat * g, xhat, r


def _rms_bwd(dy, xhat, r, g):
    dxhat = dy * g
    return r * (dxhat - xhat * jnp.mean(dxhat * xhat, axis=-1, keepdims=True))


def _rope(v, c, sa, sb):
    return v * c + pltpu.roll(v, 112, 1) * sa + pltpu.roll(v, 16, 1) * sb


def _unrope(d, c, sa, sb):
    return d * c + pltpu.roll(d * sa, 16, 1) + pltpu.roll(d * sb, 112, 1)


def _row_spec(tm, n):
    return pl.BlockSpec((tm, n), lambda i: (i, 0))


def _full_spec(shape):
    nd = len(shape)
    return pl.BlockSpec(shape, lambda i: (0,) * nd)


def _tiles(s):
    t_att = 512 if s >= 2048 else 128
    t_row = 256 if s >= 1024 else 128
    return t_att, t_row


def _inproj_fwd(order, x, norm_in, early_shards, tm):
    s = x.shape[0]
    n_tiles = s // tm
    gat = _Gather(COMM_PARAMS[:3])
    n_w = len(gat.params)
    arrival = (1, 2, 0)

    def body(order_ref, x_ref, g_ref, *rest):
        w_loc, (hn_ref, z_ref), w_all = rest[:n_w], rest[n_w:n_w + 2], rest[n_w + 2:2 * n_w + 2]
        w_vmem, hn_all, w_sem = rest[2 * n_w + 2:2 * n_w + 5]
        gat.bind(w_loc, w_all, rest[2 * n_w + 5:])
        ph, i = pl.program_id(0), pl.program_id(1)
        pl.when(jnp.logical_and(ph == 0, i == 0))(gat.start)

        def fetch(phase):
            src = w_loc[0] if phase == 0 else w_all[0].at[order_ref[phase]]
            return pltpu.make_async_copy(src, w_vmem.at[phase % 2], w_sem.at[phase % 2])

        def landed(f):
            gat.relay_one(0, arrival[f])
            gat.await_one(0, arrival[f])

        @pl.when(jnp.logical_and(ph == 0, i == 0))
        def _():
            fetch(0).start()
            fetch(0).wait()

        @pl.when(jnp.logical_and(ph == 1, i == 0))
        def _():
            landed(0)
            fetch(1).start()
            fetch(1).wait()

        for f in (1, 2):
            @pl.when(jnp.logical_and(ph == f, i == n_tiles - 1))
            def _(f=f):
                landed(f)
                fetch(f + 1).start()

            @pl.when(jnp.logical_and(ph == f + 1, i == 0))
            def _(f=f):
                fetch(f + 1).wait()

        rows = pl.ds(pl.multiple_of(i * tm, tm), tm)

        @pl.when(ph == 0)
        def _():
            hn, _, _ = _rms_fwd(x_ref[...], g_ref[...])
            hn = hn.astype(BF16)
            hn_ref[...] = hn
            hn_all[rows, :] = hn

        z_ref[0] = _dot_nt(hn_all[rows, :], w_vmem[ph % 2])

        @pl.when(jnp.logical_and(ph == N_CHIPS - 1, i == n_tiles - 1))
        def _():
            for p in range(1, n_w):
                for j in range(3):
                    gat.relay_one(p, j)
            for p in range(1, n_w):
                for j in range(3):
                    gat.await_one(p, j)
            gat.wait_sends()

    def tile_in_phase0(ph, i, order):
        return (jnp.where(ph == 0, i, n_tiles - 1), 0)

    any_spec = pl.BlockSpec(memory_space=pl.ANY)
    grid_spec = pltpu.PrefetchScalarGridSpec(
        num_scalar_prefetch=1,
        grid=(N_CHIPS, n_tiles),
        in_specs=[pl.BlockSpec((tm, D_MODEL), tile_in_phase0),
                  pl.BlockSpec((1, D_MODEL), lambda ph, i, order: (0, 0))] + [any_spec] * n_w,
        out_specs=[pl.BlockSpec((tm, D_MODEL), tile_in_phase0),
                   pl.BlockSpec((1, tm, SHARD_COLS), lambda ph, i, order: (order[ph], i, 0))] + [any_spec] * n_w,
        scratch_shapes=[pltpu.VMEM((2, SHARD_COLS, D_MODEL), BF16), pltpu.VMEM((s, D_MODEL), BF16),
                        pltpu.SemaphoreType.DMA((2,))] + gat.scratch,
    )
    out = pl.pallas_call(
        body,
        name="inproj_fwd",
        grid_spec=grid_spec,
        out_shape=[jax.ShapeDtypeStruct((s, D_MODEL), BF16), jax.ShapeDtypeStruct((N_CHIPS, s, SHARD_COLS), F32)]
        + gat.out_shape,
        compiler_params=pltpu.CompilerParams(dimension_semantics=("arbitrary", "arbitrary"),
                                             vmem_limit_bytes=VMEM_LIMIT),
    )(order, x, norm_in, *early_shards)
    return out[0], out[1], out[2:]


def _qkv_fwd(z_sh, q_norm, kv_norm, w_uq_all, w_ukv_all, rc, rsa, rsb, to_bf16, tm):
    s = z_sh.shape[1]
    n_cast = len(to_bf16)
    hw = MLA_HEADS * HEAD_PAD
    qk = QK_NOPE_DIM + QK_ROPE_DIM

    def body(z_ref, gq_ref, gkv_ref, uq_ref, ukv_ref, c_ref, sa_ref, sb_ref, *rest):
        f32_refs, rest = rest[:n_cast], rest[n_cast:]
        (q_ref, k_ref, v_ref, qt_ref, vt_ref, wuq_ref, wk_ref, wv_ref), bf_refs = rest[:8], rest[8:]

        @pl.when(pl.program_id(0) == 0)
        def _():
            for src, dst in zip(f32_refs, bf_refs):
                dst[...] = src[...].astype(BF16)
            w_q = jnp.concatenate([uq_ref[j] for j in range(N_CHIPS)], axis=0).astype(F32)
            gap = jnp.zeros((Q_LORA_RANK, HEAD_PAD - qk), F32)
            wuq_ref[...] = jnp.concatenate(
                [part for h in range(MLA_HEADS) for part in (w_q[:, h * qk:(h + 1) * qk], gap)], axis=1).astype(BF16)
            w_kv = jnp.concatenate([ukv_ref[j] for j in range(N_CHIPS)], axis=0).astype(F32)
            lane = lax.broadcasted_iota(jnp.int32, w_kv.shape, 1)
            wk_ref[...] = jnp.where(lane % HEAD_PAD < QK_NOPE_DIM, w_kv, 0.0).astype(BF16)
            wv_ref[...] = jnp.concatenate(
                [w_kv[:, h * HEAD_PAD + QK_NOPE_DIM:(h + 1) * HEAD_PAD] for h in range(MLA_HEADS)],
                axis=1).astype(BF16)

        c, sa, sb = c_ref[...], sa_ref[...], sb_ref[...]
        z0 = z_ref[0]
        cq, _, _ = _rms_fwd(z0[:, ZQ_COLS], _as_one_row(gq_ref[...]))
        qf = _dot(cq.astype(BF16), wuq_ref[...])
        ckv, _, _ = _rms_fwd(z0[:, ZKV_COLS], gkv_ref[...])
        ckv = ckv.astype(BF16)
        kn = _dot(ckv, wk_ref[...])
        lane = lax.broadcasted_iota(jnp.int32, (tm, HEAD_PAD), 1)
        zkr = jnp.where(lane < QK_ROPE_DIM, z0[:, ZKR_TILE], 0.0)
        kr = _rope(pltpu.roll(zkr, 64, 1), c, sa, sb)
        for h in range(MLA_HEADS):
            cols = slice(h * HEAD_PAD, (h + 1) * HEAD_PAD)
            qh = _rope(qf[:, cols], c, sa, sb)
            q_ref[:, cols] = qh.astype(BF16)
            qt_ref[cols, :] = qh.T.astype(BF16)
            k_ref[:, cols] = (kn[:, cols] + kr).astype(BF16)
        vf = _dot(ckv, wv_ref[...])
        v_ref[...] = vf.astype(BF16)
        vt_ref[...] = vf.T.astype(BF16)

    return pl.pallas_call(
        body,
        name="qkv_fwd",
        grid=(s // tm,),
        in_specs=[
            pl.BlockSpec((1, tm, SHARD_COLS), lambda i: (0, i, 0)),
            _full_spec(q_norm.shape), _full_spec((1, KV_LORA_RANK)),
            _full_spec(w_uq_all.shape), _full_spec(w_ukv_all.shape),
            _row_spec(tm, HEAD_PAD), _row_spec(tm, HEAD_PAD), _row_spec(tm, HEAD_PAD),
        ] + [_full_spec(a.shape) for a in to_bf16],
        out_specs=[_row_spec(tm, hw), _row_spec(tm, hw), _row_spec(tm, MLA_WIDTH),
                   pl.BlockSpec((hw, tm), lambda i: (0, i)), pl.BlockSpec((MLA_WIDTH, tm), lambda i: (0, i)),
                   _full_spec((Q_LORA_RANK, hw)), _full_spec((KV_LORA_RANK, hw)), _full_spec((KV_LORA_RANK, MLA_WIDTH))
                   ] + [_full_spec(a.shape) for a in to_bf16],
        out_shape=[jax.ShapeDtypeStruct((s, hw), BF16), jax.ShapeDtypeStruct((s, hw), BF16),
                   jax.ShapeDtypeStruct((s, MLA_WIDTH), BF16),
                   jax.ShapeDtypeStruct((hw, s), BF16), jax.ShapeDtypeStruct((MLA_WIDTH, s), BF16),
                   jax.ShapeDtypeStruct((Q_LORA_RANK, hw), BF16), jax.ShapeDtypeStruct((KV_LORA_RANK, hw), BF16),
                   jax.ShapeDtypeStruct((KV_LORA_RANK, MLA_WIDTH), BF16)
                   ] + [jax.ShapeDtypeStruct(a.shape, BF16) for a in to_bf16],
        compiler_params=pltpu.CompilerParams(dimension_semantics=("arbitrary",), vmem_limit_bytes=VMEM_LIMIT),
    )(z_sh, q_norm, kv_norm, w_uq_all, w_ukv_all, rc, rsa, rsb, *to_bf16)


def _chunk_mask(t, keys_on_rows):
    rows = lax.broadcasted_iota(jnp.int32, (t, t), 0) // CHUNK
    cols = lax.broadcasted_iota(jnp.int32, (t, t), 1) // CHUNK
    return rows <= cols if keys_on_rows else cols <= rows


def _attn_fwd(q_t, k, v_t, late_shards, t):
    s = k.shape[0]
    groups = MLA_HEADS // FWD_HEADS
    n_q = s // t
    gat = _Gather(COMM_PARAMS[3:])
    n_w = len(gat.params)

    def body(qt_ref, k_ref, k2_ref, vt_ref, *rest):
        w_in, (o_ref, lse_ref), w_out = rest[:n_w], rest[n_w:n_w + 2], rest[n_w + 2:2 * n_w + 2]
        gat.bind(w_in, w_out, rest[2 * n_w + 2:])
        i = pl.program_id(1)
        step_no = pl.program_id(0) * n_q + i
        pl.when(step_no == 0)(gat.start)
        pl.when(step_no == groups * n_q // 2)(gat.relay)
        mask = _chunk_mask(t, True)
        qcs = [slice(hh * HEAD_PAD, (hh + 1) * HEAD_PAD) for hh in range(FWD_HEADS)]
        vcs = [slice(hh * V_HEAD_DIM, (hh + 1) * V_HEAD_DIM) for hh in range(FWD_HEADS)]
        qts = [qt_ref[qc, :] for qc in qcs]

        def step(j, carry, masked):
            keys = pl.ds(pl.multiple_of(j * t, t), t)
            out = []
            for hh in range(FWD_HEADS):
                m, l, acc = carry[hh]
                sc = _dot(k_ref[keys, qcs[hh]], qts[hh])
                if masked:
                    sc = jnp.where(mask, sc, -jnp.inf)
                m_new = jnp.maximum(m, jnp.max(sc, axis=0, keepdims=True))
                alpha = jnp.exp2((m - m_new) * ATT_SCALE_LOG2E)
                p = jnp.exp2((_dot(k2_ref[keys, qcs[hh]], qts[hh]) - m_new) * ATT_SCALE_LOG2E)
                if masked:
                    p = jnp.where(mask, p, 0.0)
                l = alpha * l + jnp.sum(p, axis=0, keepdims=True)
                acc = alpha * acc + _dot(vt_ref[vcs[hh], keys], p.astype(BF16))
                out.append((m_new, l, acc))
            return tuple(out)

        one = (jnp.full((1, t), -jnp.inf, F32), jnp.zeros((1, t), F32), jnp.zeros((V_HEAD_DIM, t), F32))
        carry = lax.fori_loop(0, i, functools.partial(step, masked=False), (one,) * FWD_HEADS)
        carry = step(i, carry, True)
        o_ref[...] = jnp.concatenate([carry[hh][2] / carry[hh][1] for hh in range(FWD_HEADS)], axis=0).T
        for hh in range(FWD_HEADS):
            m, l, _ = carry[hh]
            lse_ref[:, qcs[hh]] = jnp.broadcast_to(m * ATT_SCALE_LOG2E + jnp.log2(l), (HEAD_PAD, t)).T
        pl.when(step_no == groups * n_q - 1)(gat.finish)

    any_spec = pl.BlockSpec(memory_space=pl.ANY)
    out = pl.pallas_call(
        body,
        name="attn_fwd",
        grid=(groups, n_q),
        in_specs=[
            pl.BlockSpec((FWD_HEADS * HEAD_PAD, t), lambda p, i: (p, i)),
            pl.BlockSpec((s, FWD_HEADS * HEAD_PAD), lambda p, i: (0, p), pipeline_mode=pl.Buffered(1)),
            pl.BlockSpec((s, FWD_HEADS * HEAD_PAD), lambda p, i: (0, p), pipeline_mode=pl.Buffered(1)),
            pl.BlockSpec((FWD_HEADS * V_HEAD_DIM, s), lambda p, i: (p, 0), pipeline_mode=pl.Buffered(1)),
        ] + [any_spec] * n_w,
        out_specs=[
            pl.BlockSpec((t, FWD_HEADS * V_HEAD_DIM), lambda p, i: (i, p)),
            pl.BlockSpec((t, FWD_HEADS * HEAD_PAD), lambda p, i: (i, p)),
        ] + [any_spec] * n_w,
        out_shape=[jax.ShapeDtypeStruct((s, MLA_WIDTH), F32), jax.ShapeDtypeStruct((s, MLA_HEADS * HEAD_PAD), F32)]
        + gat.out_shape,
        scratch_shapes=gat.scratch,
        compiler_params=pltpu.CompilerParams(dimension_semantics=("arbitrary", "arbitrary"),
                                             vmem_limit_bytes=VMEM_LIMIT),
    )(q_t, k, k, v_t, *late_shards)
    return out[0], out[1], out[2:]


def _mid(o, z_sh, x, target, pool_w, pool_scale, w_ba, w_bp, w_out, norm_final, tm):
    s = x.shape[0]
    n_tiles = s // tm
    halo_per_tile = tm // POOL_HALO
    small_rows = _small_rows(SMALL_MID)

    def body(o_ref, z0_ref, z1_ref, z1h_ref, z2_ref, z3_ref, x_ref, t_ref, pw_ref, ps_ref, wba_ref, wbp_ref,
             wout_ref, gf_ref,
             do_ref, dl_ref, dga_ref, dgp_ref, dgm_ref, ddc_ref, dh_ref,
             small_ref, dwout_out, dwba_out, dwbp_out,
             ubuf, dwout_ref, dwba_ref, dwbp_ref, loss_ref, dpw_ref, dps_ref, dgf_ref):
        i = pl.program_id(0)

        @pl.when(i == 0)
        def _():
            loss_ref[...] = jnp.zeros_like(loss_ref)
            dwout_ref[...] = jnp.zeros_like(dwout_ref)
            dwba_ref[...] = jnp.zeros_like(dwba_ref)
            dwbp_ref[...] = jnp.zeros_like(dwbp_ref)
            dpw_ref[...] = jnp.zeros_like(dpw_ref)
            dps_ref[...] = jnp.zeros_like(dps_ref)
            dgf_ref[...] = jnp.zeros_like(dgf_ref)

        zs = [z0_ref[0], z1_ref[0], z2_ref[0], z3_ref[0]]
        o = o_ref[...]
        ga = _segment(zs, 3)
        sga = _sigmoid(ga)
        silu_a = ga * sga
        y_attn = (o * silu_a).astype(BF16)

        ubuf[0:POOL_HALO, :] = jnp.where(i > 0, _segment([None, z1h_ref[0]], 4), 0.0)
        ubuf[POOL_HALO:, :] = _segment(zs, 4)
        row = lax.broadcasted_iota(jnp.int32, (tm, POOL_GROUP_DIM), 0) + i * tm
        ps = ps_ref[...]
        gp = _segment(zs, 5)
        sgp = _sigmoid(gp)
        silu_p = gp * sgp
        d_bf, dm, inv_cnt = [], [], []
        for g, w in enumerate(POOL_WINDOWS):
            cols = slice(g * POOL_GROUP_DIM, (g + 1) * POOL_GROUP_DIM)
            wsum = ubuf[POOL_HALO:, cols]
            for kk in range(1, w):
                wsum = wsum + ubuf[POOL_HALO - kk:POOL_HALO - kk + tm, cols]
            inv = 1.0 / jnp.minimum(row + 1, w).astype(F32)
            dg = (wsum * inv - ubuf[POOL_HALO:, cols]).astype(BF16)
            d_bf.append(dg)
            inv_cnt.append(inv)
            dm.append(_dot(dg, pw_ref[g]))
        dm = jnp.concatenate(dm, axis=1)
        yp = dm * ps
        y_pool = (yp * silu_p).astype(BF16)

        a = jnp.concatenate([_dot(y_attn, wba_ref[j]) for j in range(N_CHIPS)], axis=1)
        p = jnp.concatenate([_dot(y_pool, wbp_ref[j]) for j in range(N_CHIPS)], axis=1)
        gm = _segment(zs, 6)
        gate_a = _sigmoid(gm[:, :D_MODEL])
        gate_p = _sigmoid(gm[:, D_MODEL:])
        merged = (gate_a * a + gate_p * p).astype(BF16)
        h = x_ref[...] + _dot(merged, wout_ref[...])
        gf = gf_ref[...]
        y, xhat, r = _rms_fwd(h, gf)
        err = y - t_ref[...]
        e2 = err * err
        e2 = jnp.sum(e2.reshape(tm // 8, 8, D_MODEL), axis=0)
        acc = e2[:, 0:LANES]
        for cidx in range(1, D_MODEL // LANES):
            acc = acc + e2[:, cidx * LANES:(cidx + 1) * LANES]
        loss_ref[...] += acc

        dy = err * (1.0 / D_MODEL)
        dgf_ref[...] += _colsum(dy * xhat)
        dh = _rms_bwd(dy, xhat, r, gf)
        dh_ref[...] = dh
        dh_bf = dh.astype(BF16)
        dwout_ref[...] += _dot_tn(merged, dh_bf)
        dmerged = _dot_nt(dh_bf, wout_ref[...])
        da = (dmerged * gate_a).astype(BF16)
        dp = (dmerged * gate_p).astype(BF16)
        dgm_ref[:, :D_MODEL] = (dmerged * a * gate_a * (1.0 - gate_a)).astype(BF16)
        dgm_ref[:, D_MODEL:] = (dmerged * p * gate_p * (1.0 - gate_p)).astype(BF16)
        dy_attn = dy_pool = None
        for j in range(N_CHIPS):
            cols = slice(j * BRANCH_COLS, (j + 1) * BRANCH_COLS)
            dwba_ref[j] += _dot_tn(y_attn, da[:, cols])
            dwbp_ref[j] += _dot_tn(y_pool, dp[:, cols])
            pa = _dot_nt(da[:, cols], wba_ref[j])
            pp = _dot_nt(dp[:, cols], wbp_ref[j])
            dy_attn = pa if dy_attn is None else dy_attn + pa
            dy_pool = pp if dy_pool is None else dy_pool + pp

        do = dy_attn * silu_a
        do_ref[...] = do
        dga_ref[...] = (dy_attn * o * (sga * (1.0 + ga * (1.0 - sga)))).astype(BF16)
        doo = do * o
        for hd in range(MLA_HEADS):
            dl = jnp.sum(doo[:, hd * V_HEAD_DIM:(hd + 1) * V_HEAD_DIM], axis=1, keepdims=True)
            dl_ref[:, hd * HEAD_PAD:(hd + 1) * HEAD_PAD] = jnp.broadcast_to(dl, (tm, HEAD_PAD))

        dyp = dy_pool * silu_p
        dgp_ref[...] = (dy_pool * yp * (sgp * (1.0 + gp * (1.0 - sgp)))).astype(BF16)
        dps_ref[...] += _colsum(dyp * dm)
        dmm = (dyp * ps).astype(BF16)
        for g in range(len(POOL_WINDOWS)):
            cols = slice(g * POOL_GROUP_DIM, (g + 1) * POOL_GROUP_DIM)
            dpw_ref[g] += _dot_tn(d_bf[g], dmm[:, cols])
            ddc_ref[:, cols] = _dot_nt(dmm[:, cols], pw_ref[g]) * inv_cnt[g]

        @pl.when(i == n_tiles - 1)
        def _():
            dwout_out[...] = dwout_ref[...].astype(BF16)
            dwba_out[...] = dwba_ref[...].astype(BF16)
            dwbp_out[...] = dwbp_ref[...].astype(BF16)
            _pack_into(small_ref, SMALL_MID, dict(pool_w=dpw_ref[...], norm_final=dgf_ref[...], sq_err=loss_ref[...],
                                                  pool_scale=dps_ref[...]))

    row_in = lambda n: _row_spec(tm, n)
    in_specs = [
        row_in(MLA_WIDTH),
        pl.BlockSpec((1, tm, SHARD_COLS), lambda i: (0, i, 0)), pl.BlockSpec((1, tm, SHARD_COLS), lambda i: (1, i, 0)),
        pl.BlockSpec((1, POOL_HALO, SHARD_COLS), lambda i: (1, jnp.maximum(i * halo_per_tile - 1, 0), 0)),
        pl.BlockSpec((1, tm, SHARD_COLS), lambda i: (2, i, 0)), pl.BlockSpec((1, tm, SHARD_COLS), lambda i: (3, i, 0)),
        row_in(D_MODEL), row_in(D_MODEL),
        _full_spec((4, POOL_GROUP_DIM, POOL_GROUP_DIM)), _full_spec((1, POOL_WIDTH)),
        _full_spec((N_CHIPS, MLA_WIDTH, BRANCH_COLS)), _full_spec((N_CHIPS, POOL_WIDTH, BRANCH_COLS)),
        _full_spec((D_MODEL, D_MODEL)), _full_spec((1, D_MODEL)),
    ]
    out_shape = [
        jax.ShapeDtypeStruct((s, MLA_WIDTH), F32),
        jax.ShapeDtypeStruct((s, MLA_HEADS * HEAD_PAD), F32),
        jax.ShapeDtypeStruct((s, MLA_WIDTH), BF16),
        jax.ShapeDtypeStruct((s, POOL_WIDTH), BF16),
        jax.ShapeDtypeStruct((s, 2 * D_MODEL), BF16),
        jax.ShapeDtypeStruct((s, POOL_WIDTH), F32),
        jax.ShapeDtypeStruct((s, D_MODEL), F32),
        jax.ShapeDtypeStruct((small_rows, LANES), F32),
        jax.ShapeDtypeStruct((D_MODEL, D_MODEL), BF16),
        jax.ShapeDtypeStruct((N_CHIPS, MLA_WIDTH, BRANCH_COLS), BF16),
        jax.ShapeDtypeStruct((N_CHIPS, POOL_WIDTH, BRANCH_COLS), BF16),
    ]
    out_specs = [
        row_in(MLA_WIDTH), row_in(MLA_HEADS * HEAD_PAD), row_in(MLA_WIDTH), row_in(POOL_WIDTH),
        row_in(2 * D_MODEL), row_in(POOL_WIDTH), row_in(D_MODEL),
        _full_spec((small_rows, LANES)), _full_spec((D_MODEL, D_MODEL)),
        _full_spec((N_CHIPS, MLA_WIDTH, BRANCH_COLS)), _full_spec((N_CHIPS, POOL_WIDTH, BRANCH_COLS)),
    ]
    return pl.pallas_call(
        body,
        name="mid",
        grid=(n_tiles,),
        in_specs=in_specs,
        out_specs=out_specs,
        out_shape=out_shape,
        scratch_shapes=[
            pltpu.VMEM((tm + POOL_HALO, POOL_WIDTH), F32),
            pltpu.VMEM((D_MODEL, D_MODEL), F32),
            pltpu.VMEM((N_CHIPS, MLA_WIDTH, BRANCH_COLS), F32),
            pltpu.VMEM((N_CHIPS, POOL_WIDTH, BRANCH_COLS), F32),
            pltpu.VMEM((8, LANES), F32),
            pltpu.VMEM((4, POOL_GROUP_DIM, POOL_GROUP_DIM), F32),
            pltpu.VMEM((1, POOL_WIDTH), F32),
            pltpu.VMEM((1, D_MODEL), F32),
        ],
        compiler_params=pltpu.CompilerParams(dimension_semantics=("arbitrary",), vmem_limit_bytes=VMEM_LIMIT),
    )(o, z_sh, z_sh, z_sh, z_sh, z_sh, x, target, pool_w, pool_scale, w_ba, w_bp, w_out, norm_final)


def _attn_bwd(q, q_t, k, v, do, lse, delta, late_grads, gs_mid, t):
    s = q.shape[0]
    groups = MLA_HEADS // BWD_HEADS
    n_q = s // t
    red = _Reduce(COMM_PARAMS[3:])
    n_w = len(red.params)
    small = _SmallSum(gs_mid.shape[0])
    n_red = len(red.scratch)

    def body(q_ref, qt_ref, do_ref, lse_ref, dl_ref, k_ref, v_ref, *rest):
        g_in, gs_ref = rest[:n_w], rest[n_w]
        (dq_ref, dk_ref, dv_ref), g_out, gsum_ref = rest[n_w + 1:n_w + 4], rest[n_w + 4:2 * n_w + 4], rest[2 * n_w + 4]
        scratch = rest[2 * n_w + 5:]
        red.bind(g_in, g_out, scratch[:n_red])
        small.bind(gs_ref, gsum_ref, scratch[n_red:])
        i = pl.program_id(1)
        step_no = pl.program_id(0) * n_q + i

        @pl.when(step_no == 0)
        def _():
            red.start()
            small.start()

        pl.when(step_no == groups * n_q // 2)(red.exchange)

        @pl.when(i == 0)
        def _():
            dk_ref[...] = jnp.zeros_like(dk_ref)
            dv_ref[...] = jnp.zeros_like(dv_ref)

        mask = _chunk_mask(t, False)
        qcs = [slice(hh * HEAD_PAD, (hh + 1) * HEAD_PAD) for hh in range(BWD_HEADS)]
        vcs = [slice(hh * V_HEAD_DIM, (hh + 1) * V_HEAD_DIM) for hh in range(BWD_HEADS)]
        qhs = [q_ref[:, qc] for qc in qcs]
        qts = [qt_ref[qc, :] for qc in qcs]
        dohs = [do_ref[:, vc].astype(BF16) for vc in vcs]
        do_t = do_ref[...].T.astype(BF16)
        dots = [do_t[vc, :] for vc in vcs]
        lses = [jnp.tile(lse_ref[:, qc], (1, t // HEAD_PAD)) for qc in qcs]
        dls = [jnp.tile(dl_ref[:, qc], (1, t // HEAD_PAD)) for qc in qcs]

        def step(j, dqs, masked):
            keys = pl.ds(pl.multiple_of(j * t, t), t)
            out = []
            for hh in range(BWD_HEADS):
                kj = k_ref[keys, qcs[hh]]
                vj = v_ref[keys, vcs[hh]]
                p = jnp.exp2(_dot_nt(qhs[hh], kj) * ATT_SCALE_LOG2E - lses[hh])
                if masked:
                    p = jnp.where(mask, p, 0.0)
                ds = (p * (_dot_nt(dohs[hh], vj) - dls[hh])).astype(BF16)
                dv_ref[vcs[hh], keys] += _dot(dots[hh], p.astype(BF16))
                dk_ref[qcs[hh], keys] += _dot(qts[hh], ds) * ATT_SCALE
                out.append(dqs[hh] + _dot(ds, kj))
            return tuple(out)

        zero = jnp.zeros((t, HEAD_PAD), F32)
        dqs = lax.fori_loop(0, i, functools.partial(step, masked=False), (zero,) * BWD_HEADS)
        dqs = step(i, dqs, True)
        for hh in range(BWD_HEADS):
            dq_ref[:, qcs[hh]] = dqs[hh] * ATT_SCALE

        @pl.when(step_no == groups * n_q - 1)
        def _():
            red.finish()
            small.finish()

    hw = MLA_HEADS * HEAD_PAD
    any_spec = pl.BlockSpec(memory_space=pl.ANY)
    out = pl.pallas_call(
        body,
        name="attn_bwd",
        grid=(groups, n_q),
        in_specs=[
            pl.BlockSpec((t, BWD_HEADS * HEAD_PAD), lambda p, i: (i, p)),
            pl.BlockSpec((BWD_HEADS * HEAD_PAD, t), lambda p, i: (p, i)),
            pl.BlockSpec((t, BWD_HEADS * V_HEAD_DIM), lambda p, i: (i, p)),
            pl.BlockSpec((t, BWD_HEADS * HEAD_PAD), lambda p, i: (i, p)),
            pl.BlockSpec((t, BWD_HEADS * HEAD_PAD), lambda p, i: (i, p)),
            pl.BlockSpec((s, BWD_HEADS * HEAD_PAD), lambda p, i: (0, p), pipeline_mode=pl.Buffered(1)),
            pl.BlockSpec((s, BWD_HEADS * V_HEAD_DIM), lambda p, i: (0, p), pipeline_mode=pl.Buffered(1)),
        ] + [any_spec] * n_w + [pl.BlockSpec(small.spec_shape, lambda p, i: (0, 0))],
        out_specs=[
            pl.BlockSpec((t, BWD_HEADS * HEAD_PAD), lambda p, i: (i, p)),
            pl.BlockSpec((BWD_HEADS * HEAD_PAD, s), lambda p, i: (p, 0)),
            pl.BlockSpec((BWD_HEADS * V_HEAD_DIM, s), lambda p, i: (p, 0)),
        ] + [any_spec] * n_w + [pl.BlockSpec(small.spec_shape, lambda p, i: (0, 0))],
        out_shape=[jax.ShapeDtypeStruct((s, hw), F32), jax.ShapeDtypeStruct((hw, s), F32),
                   jax.ShapeDtypeStruct((MLA_WIDTH, s), F32)] + red.out_shape + [small.out_shape],
        scratch_shapes=red.scratch + small.scratch,
        compiler_params=pltpu.CompilerParams(dimension_semantics=("arbitrary", "arbitrary"),
                                             vmem_limit_bytes=VMEM_LIMIT),
    )(q, q_t, do, lse, delta, k, v, *late_grads, gs_mid)
    return out[0], out[1], out[2], out[3:3 + n_w], out[3 + n_w]


def _qkv_bwd(dq, dk_t, dv_t, z_sh, q_norm, kv_norm, wuq_p, wk_p, wv, rc, rsa, rsb, tm):
    s = z_sh.shape[1]
    hw = MLA_HEADS * HEAD_PAD
    n_tiles = s // tm
    uq_shape, ukv_shape = (N_CHIPS,) + COMM_PARAMS[1][1:3], (N_CHIPS,) + COMM_PARAMS[2][1:3]

    def body(dq_ref, dk_ref, dv_ref, z_ref, gq_ref, gkv_ref, wuq_ref, wk_ref, wv_ref,
             c_ref, sa_ref, sb_ref,
             dzq_ref, dzkv_ref, dzkr_ref, duq_ref, dukv_ref, dgq_ref, dgkv_ref, dwuq_ref, dwk_ref, dwv_ref):
        i = pl.program_id(0)

        @pl.when(i == 0)
        def _():
            dwuq_ref[...] = jnp.zeros_like(dwuq_ref)
            dwk_ref[...] = jnp.zeros_like(dwk_ref)
            dwv_ref[...] = jnp.zeros_like(dwv_ref)
            dgq_ref[...] = jnp.zeros_like(dgq_ref)
            dgkv_ref[...] = jnp.zeros_like(dgkv_ref)

        c, sa, sb = c_ref[...], sa_ref[...], sb_ref[...]
        gq, gkv = _as_one_row(gq_ref[...]), gkv_ref[...]

        z0 = z_ref[0]
        cq, xq, rq = _rms_fwd(z0[:, ZQ_COLS], gq)
        dqp = jnp.concatenate(
            [_unrope(dq_ref[:, h * HEAD_PAD:(h + 1) * HEAD_PAD], c, sa, sb) for h in range(MLA_HEADS)],
            axis=1).astype(BF16)
        dwuq_ref[...] += _dot_tn(cq.astype(BF16), dqp)
        dcq = _dot_nt(dqp, wuq_ref[...])
        dgq_ref[...] += _colsum(dcq * xq)
        dzq_ref[...] = _rms_bwd(dcq, xq, rq, gq).astype(BF16)

        ckv, xkv, rkv = _rms_fwd(z0[:, ZKV_COLS], gkv)
        ckv = ckv.astype(BF16)
        dkf = dk_ref[...].T
        dk_bf = dkf.astype(BF16)
        dv_bf = dv_ref[...].T.astype(BF16)
        dwk_ref[...] += _dot_tn(ckv, dk_bf)
        dwv_ref[...] += _dot_tn(ckv, dv_bf)
        dckv = _dot_nt(dk_bf, wk_ref[...]) + _dot_nt(dv_bf, wv_ref[...])
        dgkv_ref[...] += _colsum(dckv * xkv)
        dzkv_ref[...] = _rms_bwd(dckv, xkv, rkv, gkv).astype(BF16)

        dkr = dkf[:, 0:HEAD_PAD]
        for h in range(1, MLA_HEADS):
            dkr = dkr + dkf[:, h * HEAD_PAD:(h + 1) * HEAD_PAD]
        dkr = pltpu.roll(_unrope(dkr, c, sa, sb), 64, 1)
        lane = lax.broadcasted_iota(jnp.int32, (tm, HEAD_PAD), 1)
        dzkr_ref[...] = jnp.where(lane < QK_ROPE_DIM, dkr, 0.0).astype(BF16)

        @pl.when(i == n_tiles - 1)
        def _():
            qk = QK_NOPE_DIM + QK_ROPE_DIM
            d_uq = jnp.concatenate([dwuq_ref[:, h * HEAD_PAD:h * HEAD_PAD + qk] for h in range(MLA_HEADS)],
                                   axis=1).astype(BF16)
            d_ukv = jnp.concatenate(
                [part for h in range(MLA_HEADS)
                 for part in (dwk_ref[:, h * HEAD_PAD:h * HEAD_PAD + QK_NOPE_DIM],
                              dwv_ref[:, h * V_HEAD_DIM:(h + 1) * V_HEAD_DIM])], axis=1).astype(BF16)
            for j in range(N_CHIPS):
                duq_ref[j] = d_uq[j * uq_shape[1]:(j + 1) * uq_shape[1]]
                dukv_ref[j] = d_ukv[j * ukv_shape[1]:(j + 1) * ukv_shape[1]]

    return pl.pallas_call(
        body,
        name="qkv_bwd",
        grid=(s // tm,),
        in_specs=[
            _row_spec(tm, hw), pl.BlockSpec((hw, tm), lambda i: (0, i)), pl.BlockSpec((MLA_WIDTH, tm), lambda i: (0, i)),
            pl.BlockSpec((1, tm, SHARD_COLS), lambda i: (0, i, 0)),
            _full_spec(q_norm.shape), _full_spec((1, KV_LORA_RANK)),
            _full_spec((Q_LORA_RANK, hw)), _full_spec((KV_LORA_RANK, hw)), _full_spec((KV_LORA_RANK, MLA_WIDTH)),
            _row_spec(tm, HEAD_PAD), _row_spec(tm, HEAD_PAD), _row_spec(tm, HEAD_PAD),
        ],
        out_specs=[
            _row_spec(tm, Q_LORA_RANK), _row_spec(tm, KV_LORA_RANK), _row_spec(tm, HEAD_PAD),
            _full_spec(uq_shape), _full_spec(ukv_shape),
            _full_spec((1, Q_LORA_RANK)), _full_spec((1, KV_LORA_RANK)),
        ],
        out_shape=[
            jax.ShapeDtypeStruct((s, Q_LORA_RANK), BF16), jax.ShapeDtypeStruct((s, KV_LORA_RANK), BF16),
            jax.ShapeDtypeStruct((s, HEAD_PAD), BF16),
            jax.ShapeDtypeStruct(uq_shape, BF16), jax.ShapeDtypeStruct(ukv_shape, BF16),
            jax.ShapeDtypeStruct((1, Q_LORA_RANK), F32), jax.ShapeDtypeStruct((1, KV_LORA_RANK), F32),
        ],
        scratch_shapes=[pltpu.VMEM((Q_LORA_RANK, hw), F32), pltpu.VMEM((KV_LORA_RANK, hw), F32),
                        pltpu.VMEM((KV_LORA_RANK, MLA_WIDTH), F32)],
        compiler_params=pltpu.CompilerParams(dimension_semantics=("arbitrary",), vmem_limit_bytes=VMEM_LIMIT),
    )(dq, dk_t, dv_t, z_sh, q_norm, kv_norm, wuq_p, wk_p, wv, rc, rsa, rsb)


def _inproj_bwd_x(dzq, dzkv, dzkr, dgattn, ddc, dgpool, dgmerge, x, dh, norm_in, d_q_norm, d_kv_norm, w_in_t, tm):
    s = x.shape[0]
    n_tiles = s // tm
    halo_per_tile = tm // POOL_HALO
    n_halo = s // POOL_HALO
    u_seg = 4
    small_rows = _small_rows(SMALL_LATE)

    def body(dzq_ref, dzkv_ref, dzkr_ref, dga_ref, ddc_ref, ddn_ref, dgp_ref, dgm_ref, x_ref, dh_ref,
             g_ref, dgq_ref, dgkv_ref, w_hbm, gx_ref, small_ref, dzs_ref, w_vmem, dbuf, sem, dgin_ref):
        i = pl.program_id(0)

        @pl.when(i == 0)
        def _():
            cp = pltpu.make_async_copy(w_hbm, w_vmem, sem)
            cp.start()
            dgin_ref[...] = jnp.zeros_like(dgin_ref)
            cp.wait()

        dbuf[0:tm, :] = ddc_ref[...]
        dbuf[tm:, :] = jnp.where(i < n_tiles - 1, ddn_ref[...], 0.0)
        row = lax.broadcasted_iota(jnp.int32, (tm, POOL_GROUP_DIM), 0) + i * tm
        du = []
        for g, w in enumerate(POOL_WINDOWS):
            cols = slice(g * POOL_GROUP_DIM, (g + 1) * POOL_GROUP_DIM)
            fsum = dbuf[0:tm, cols]
            for kk in range(1, w):
                fsum = fsum + dbuf[kk:kk + tm, cols]
            du.append(fsum - dbuf[0:tm, cols] * jnp.minimum(row + 1, w).astype(F32))
        du = jnp.concatenate(du, axis=1).astype(BF16)

        dz = [dzq_ref[...], dzkv_ref[...], dzkr_ref[...], dga_ref[...], du, dgp_ref[...], dgm_ref[...]]
        dz = jnp.concatenate([d[:, :w] for d, (w, _) in zip(dz, IN_SEGMENTS)], axis=1)
        for j in range(N_CHIPS):
            dzs_ref[j] = dz[:, j * SHARD_COLS:(j + 1) * SHARD_COLS].T
        dhn = _dot(dz, w_vmem[...])

        g = g_ref[...]
        _, xhat, r = _rms_fwd(x_ref[...], g)
        dgin_ref[...] += _colsum(dhn * xhat)
        gx_ref[...] = dh_ref[...] + _rms_bwd(dhn, xhat, r, g)

        @pl.when(i == n_tiles - 1)
        def _():
            _pack_into(small_ref, SMALL_LATE, dict(norm_in=dgin_ref[...], q_norm=dgq_ref[...], kv_norm=dgkv_ref[...]))

    any_spec = pl.BlockSpec(memory_space=pl.ANY)
    seg_w = [wide for _, wide in IN_SEGMENTS]
    return pl.pallas_call(
        body,
        name="inproj_bwd_x",
        grid=(n_tiles,),
        in_specs=[
            _row_spec(tm, seg_w[0]), _row_spec(tm, seg_w[1]), _row_spec(tm, seg_w[2]),
            _row_spec(tm, seg_w[3]), _row_spec(tm, seg_w[u_seg]),
            pl.BlockSpec((POOL_HALO, POOL_WIDTH), lambda i: (jnp.minimum((i + 1) * halo_per_tile, n_halo - 1), 0)),
            _row_spec(tm, seg_w[5]), _row_spec(tm, seg_w[6]),
            _row_spec(tm, D_MODEL), _row_spec(tm, D_MODEL),
            _full_spec((1, D_MODEL)), _full_spec((1, Q_LORA_RANK)), _full_spec((1, KV_LORA_RANK)), any_spec,
        ],
        out_specs=[_row_spec(tm, D_MODEL), _full_spec((small_rows, LANES)),
                   pl.BlockSpec((N_CHIPS, SHARD_COLS, tm), lambda i: (0, 0, i))],
        out_shape=[jax.ShapeDtypeStruct((s, D_MODEL), F32), jax.ShapeDtypeStruct((small_rows, LANES), F32),
                   jax.ShapeDtypeStruct((N_CHIPS, SHARD_COLS, s), BF16)],
        scratch_shapes=[
            pltpu.VMEM((IN_TOTAL, D_MODEL), BF16),
            pltpu.VMEM((tm + POOL_HALO, POOL_WIDTH), F32),
            pltpu.SemaphoreType.DMA,
            pltpu.VMEM((1, D_MODEL), F32),
        ],
        compiler_params=pltpu.CompilerParams(dimension_semantics=("arbitrary",), vmem_limit_bytes=VMEM_LIMIT),
    )(dzq, dzkv, dzkr, dgattn, ddc, ddc, dgpool, dgmerge, x, dh, norm_in, d_q_norm, d_kv_norm,
      w_in_t.reshape(IN_TOTAL, D_MODEL))


def _inproj_bwd_w(order, dz_sh, hn, g_uq, g_ukv, gs, tm):
    s = hn.shape[0]
    n_tiles = s // tm
    hc = D_MODEL // 2
    red = _Reduce(COMM_PARAMS[1:3])
    small = _SmallSum(gs.shape[0])
    n_red = len(red.scratch)

    def body(order_ref, dz_ref, hn_ref, guq_hbm, gukv_hbm, gs_ref, gw_hbm, guq_out, gukv_out, gsum_ref,
             acc, pm_w, a_w, b_w, r_w, w_send, w_recv, w_local, *more_scratch):
        ph, i = pl.program_id(0), pl.program_id(1)
        x, y, c = lax.axis_index("x"), lax.axis_index("y"), lax.axis_index("c")
        k = 2 * x + y
        me, sibling = (x, y, c), (x, y, 1 - c)
        chips = _other_chips(x, y)
        shard_of_phase = [2 * cx + cy for cx, cy in chips] + [k]
        copy = _remote_copier(w_send, w_recv)
        red.bind([guq_hbm, gukv_hbm], [guq_out, gukv_out], more_scratch[:n_red])
        small.bind(gs_ref, gsum_ref, more_scratch[n_red:])
        mine = pl.ds(pl.multiple_of(c * hc, hc), hc)
        theirs = pl.ds(pl.multiple_of((1 - c) * hc, hc), hc)

        def to_sibling(f):
            j = shard_of_phase[f]
            return copy(f, pm_w.at[j, 1 - c], a_w.at[j], sibling)

        def pair_sum(f):
            cx, cy = chips[f]
            return copy(4 + f, pm_w.at[shard_of_phase[f], c], b_w.at[f], (cx, cy, c))

        def finished():
            return copy(7, r_w, gw_hbm.at[:, mine], sibling)

        @pl.when(jnp.logical_and(ph == 0, i == 0))
        def _():
            red.start()
            small.start()

        part = _dot(dz_ref[0], hn_ref[...])

        @pl.when(i == 0)
        def _():
            acc[...] = part

        @pl.when(i > 0)
        def _():
            acc[...] += part

        for f in range(3):
            @pl.when(jnp.logical_and(ph == f + 1, i == 0))
            def _(f=f):
                j = shard_of_phase[f]
                copy(f, a_w.at[j], a_w.at[j], me).wait_recv()
                pm_w[j, c] = (pm_w[j, c].astype(F32) + a_w[j].astype(F32)).astype(BF16)
                pair_sum(f).start()
                if f == 0:
                    red.exchange()

        for f in range(4):
            @pl.when(jnp.logical_and(ph == f, i == n_tiles - 1))
            def _(f=f):
                j = shard_of_phase[f]
                pm_w[j, 0] = acc[:, :hc].astype(BF16)
                pm_w[j, 1] = acc[:, hc:].astype(BF16)
                to_sibling(f).start()
                if f < 3:
                    return
                copy(3, a_w.at[k], a_w.at[k], me).wait_recv()
                r_w[...] = pm_w[k, c].astype(F32) + a_w[k].astype(F32)
                for g in range(3):
                    copy(4 + g, b_w.at[g], b_w.at[g], me).wait_recv()
                    r_w[...] = r_w[...] + b_w[g].astype(F32)
                store = pltpu.make_async_copy(r_w, gw_hbm.at[:, mine], w_local)
                store.start()
                finished().start()
                red.finish()
                small.finish()
                copy(7, gw_hbm.at[:, theirs], gw_hbm.at[:, theirs], me).wait_recv()
                store.wait()
                for g in range(4):
                    to_sibling(g).wait_send()
                for g in range(3):
                    pair_sum(g).wait_send()
                finished().wait_send()

    any_spec = pl.BlockSpec(memory_space=pl.ANY)
    n_sem = 8
    grid_spec = pltpu.PrefetchScalarGridSpec(
        num_scalar_prefetch=1,
        grid=(N_CHIPS, n_tiles),
        in_specs=[
            pl.BlockSpec((1, SHARD_COLS, tm), lambda ph, i, order: (order[ph], 0, i)),
            pl.BlockSpec((tm, D_MODEL), lambda ph, i, order: (i, 0)),
            any_spec, any_spec,
            pl.BlockSpec(small.spec_shape, lambda ph, i, order: (0, 0)),
        ],
        out_specs=[any_spec, any_spec, any_spec, pl.BlockSpec(small.spec_shape, lambda ph, i, order: (0, 0))],
        scratch_shapes=[
            pltpu.VMEM((SHARD_COLS, D_MODEL), F32),
            pltpu.VMEM((N_CHIPS, 2, SHARD_COLS, hc), BF16),
            pltpu.VMEM((N_CHIPS, SHARD_COLS, hc), BF16),
            pltpu.VMEM((3, SHARD_COLS, hc), BF16),
            pltpu.VMEM((SHARD_COLS, hc), F32),
            pltpu.SemaphoreType.DMA((n_sem,)), pltpu.SemaphoreType.DMA((n_sem,)), pltpu.SemaphoreType.DMA,
        ] + red.scratch + small.scratch,
    )
    out = pl.pallas_call(
        body,
        name="inproj_bwd_w",
        grid_spec=grid_spec,
        out_shape=[jax.ShapeDtypeStruct((SHARD_COLS, D_MODEL), F32)] + red.out_shape
        + [small.out_shape],
        compiler_params=pltpu.CompilerParams(dimension_semantics=("arbitrary", "arbitrary"),
                                             vmem_limit_bytes=VMEM_LIMIT),
    )(order, dz_sh, hn, g_uq, g_ukv, gs)
    return out[0], out[1], out[2], out[3]


def _other_chips(x, y):
    return ((1 - x, 1 - y), (1 - x, y), (x, 1 - y))


def _half(ref, axis, size, c, lead=()):
    window = pl.ds(pl.multiple_of(c * size, size), size)
    if axis == 0:
        return ref.at[(*lead, window, slice(None))]
    return ref.at[(*lead, slice(None), window)]


def _half_shape(rows, cols, axis, size):
    return (size, cols) if axis == 0 else (rows, size)


def _remote_copier(send_sems, recv_sems):
    def copy(sem, src, dst, to):
        return pltpu.make_async_remote_copy(src_ref=src, dst_ref=dst, send_sem=send_sems.at[sem],
                                            recv_sem=recv_sems.at[sem], device_id=to, device_id_type=MESH)
    return copy


class _Gather:
    def __init__(self, params):
        self.params = params
        n = len(params)
        self.scratch = [pltpu.SemaphoreType.DMA((6 * n,)), pltpu.SemaphoreType.DMA((6 * n,)),
                        pltpu.SemaphoreType.DMA((n,))]
        self.out_shape = [jax.ShapeDtypeStruct((N_CHIPS, r, cc), BF16) for _, r, cc, _, _ in params]

    def bind(self, ins, outs, scratch):
        self.ins, self.outs = ins, outs
        send_sems, recv_sems, self.local_sems = scratch
        self.copy = _remote_copier(send_sems, recv_sems)
        self.x, self.y, self.c = lax.axis_index("x"), lax.axis_index("y"), lax.axis_index("c")
        self.k = 2 * self.x + self.y
        self.chips = _other_chips(self.x, self.y)

    def _local(self, p):
        return pltpu.make_async_copy(self.ins[p], self.outs[p].at[self.k], self.local_sems.at[p])

    def _first(self, p, j):
        _, _, _, axis, size = self.params[p]
        cx, cy = self.chips[j]
        return self.copy(6 * p + j, _half(self.ins[p], axis, size, self.c),
                         _half(self.outs[p], axis, size, self.c, (self.k,)), (cx, cy, self.c))

    def _relay(self, p, j, half_of):
        _, _, _, axis, size = self.params[p]
        cx, cy = self.chips[j]
        block = _half(self.outs[p], axis, size, half_of, (2 * cx + cy,))
        return self.copy(6 * p + 3 + j, block, block, (self.x, self.y, 1 - self.c))

    def start(self):
        for p in range(len(self.params)):
            self._local(p).start()
            for j in (1, 2, 0):
                self._first(p, j).start()

    def relay_one(self, p, j):
        _, _, _, axis, size = self.params[p]
        cx, cy = self.chips[j]
        landed = _half(self.outs[p], axis, size, self.c, (2 * cx + cy,))
        self.copy(6 * p + j, landed, landed, (self.x, self.y, self.c)).wait_recv()
        self._relay(p, j, self.c).start()

    def await_one(self, p, j):
        self._relay(p, j, 1 - self.c).wait_recv()

    def wait_sends(self):
        for p in range(len(self.params)):
            for j in range(3):
                self._first(p, j).wait_send()
                self._relay(p, j, self.c).wait_send()
            self._local(p).wait()

    def relay(self):
        for j in range(3):
            for p in range(len(self.params)):
                self.relay_one(p, j)

    def finish(self):
        for j in range(3):
            for p in range(len(self.params)):
                self.await_one(p, j)
        self.wait_sends()


class _Reduce:
    def __init__(self, params):
        self.params = params
        n = len(params)
        halves = [_half_shape(r, cc, axis, size) for _, r, cc, axis, size in params]
        self.scratch = ([pltpu.VMEM((N_CHIPS, *h), BF16) for h in halves]
                        + [pltpu.VMEM((N_CHIPS, *h), BF16) for h in halves]
                        + [pltpu.VMEM((3, *h), BF16) for h in halves]
                        + [pltpu.VMEM(h, F32) for h in halves]
                        + [pltpu.SemaphoreType.DMA((5 * n,)), pltpu.SemaphoreType.DMA((5 * n,)),
                           pltpu.SemaphoreType.DMA((2 * n,))])
        self.out_shape = [jax.ShapeDtypeStruct((r, cc), F32) for _, r, cc, _, _ in params]

    def bind(self, g_in, g_out, scratch):
        n = len(self.params)
        self.g_in, self.g_out = g_in, g_out
        self.pm, self.a_buf = scratch[0:n], scratch[n:2 * n]
        self.b_buf, self.r_buf = scratch[2 * n:3 * n], scratch[3 * n:4 * n]
        send_sems, recv_sems, self.local_sems = scratch[4 * n:]
        self.copy = _remote_copier(send_sems, recv_sems)
        self.x, self.y, self.c = lax.axis_index("x"), lax.axis_index("y"), lax.axis_index("c")
        self.k = 2 * self.x + self.y
        self.chips = _other_chips(self.x, self.y)
        self.me = (self.x, self.y, self.c)
        self.sibling = (self.x, self.y, 1 - self.c)

    def _load(self, p):
        _, _, _, axis, size = self.params[p]
        return pltpu.make_async_copy(_half(self.g_in[p], axis, size, self.c, (slice(None),)), self.pm[p],
                                     self.local_sems.at[p])

    def _to_sibling(self, p):
        _, _, _, axis, size = self.params[p]
        return self.copy(5 * p, _half(self.g_in[p], axis, size, 1 - self.c, (slice(None),)), self.a_buf[p],
                         self.sibling)

    def _pair_sum(self, p, j):
        cx, cy = self.chips[j]
        return self.copy(5 * p + 1 + j, self.pm[p].at[2 * cx + cy], self.b_buf[p].at[j], (cx, cy, self.c))

    def _store(self, p):
        _, _, _, axis, size = self.params[p]
        n = len(self.params)
        return pltpu.make_async_copy(self.r_buf[p], _half(self.g_out[p], axis, size, self.c),
                                     self.local_sems.at[n + p])

    def _finished(self, p):
        _, _, _, axis, size = self.params[p]
        return self.copy(5 * p + 4, self.r_buf[p], _half(self.g_out[p], axis, size, self.c), self.sibling)

    def start(self):
        for p in range(len(self.params)):
            self._load(p).start()
            self._to_sibling(p).start()

    def exchange(self):
        for p in range(len(self.params)):
            self._load(p).wait()
            self.copy(5 * p, self.a_buf[p], self.a_buf[p], self.me).wait_recv()
            for j, (cx, cy) in enumerate(self.chips):
                kj = 2 * cx + cy
                self.pm[p][kj] = (self.pm[p][kj].astype(F32) + self.a_buf[p][kj].astype(F32)).astype(BF16)
                self._pair_sum(p, j).start()
            self.r_buf[p][...] = self.pm[p][self.k].astype(F32) + self.a_buf[p][self.k].astype(F32)

    def finish(self):
        for p, (_, _, _, axis, size) in enumerate(self.params):
            for j in range(3):
                self.copy(5 * p + 1 + j, self.b_buf[p].at[j], self.b_buf[p].at[j], self.me).wait_recv()
                self.r_buf[p][...] = self.r_buf[p][...] + self.b_buf[p][j].astype(F32)
            self._store(p).start()
            self._finished(p).start()
        for p, (_, _, _, axis, size) in enumerate(self.params):
            theirs = _half(self.g_out[p], axis, size, 1 - self.c)
            self.copy(5 * p + 4, theirs, theirs, self.me).wait_recv()
            self._store(p).wait()
            self._to_sibling(p).wait_send()
            for j in range(3):
                self._pair_sum(p, j).wait_send()
            self._finished(p).wait_send()


class _SmallSum:
    def __init__(self, rows):
        self.rows = rows
        self.scratch = [pltpu.VMEM((N_DEV, rows, LANES), F32),
                        pltpu.SemaphoreType.DMA((N_DEV - 1,)), pltpu.SemaphoreType.DMA((N_DEV - 1,))]
        self.out_shape = jax.ShapeDtypeStruct((rows, LANES), F32)
        self.spec_shape = (rows, LANES)

    def bind(self, src, dst, scratch):
        self.src, self.dst = src, dst
        self.buf, send_sems, recv_sems = scratch
        self.copy = _remote_copier(send_sems, recv_sems)
        self.x, self.y, self.c = lax.axis_index("x"), lax.axis_index("y"), lax.axis_index("c")

    def _send(self, f):
        fx, fy, fc = [(a, b, d) for a in (0, 1) for b in (0, 1) for d in (0, 1)][f]
        x, y, c = self.x, self.y, self.c
        peer = (1 - x if fx else x, 1 - y if fy else y, 1 - c if fc else c)
        return self.copy(f - 1, self.src, self.buf.at[f], peer)

    def start(self):
        for f in range(1, N_DEV):
            self._send(f).start()
        self.buf[0] = self.src[...]

    def finish(self):
        me = (self.x, self.y, self.c)
        for f in range(1, N_DEV):
            self.copy(f - 1, self.buf.at[f], self.buf.at[f], me).wait_recv()
        dev = 4 * self.x + 2 * self.y + self.c
        total = self.buf[dev]
        for d in range(1, N_DEV):
            total = total + self.buf[jnp.bitwise_xor(dev, d)]
        self.dst[...] = total
        for f in range(1, N_DEV):
            self._send(f).wait_send()


def _adamw_math(w, g, m, v):
    m = ADAM_B1 * m + (1.0 - ADAM_B1) * g
    v = ADAM_B2 * v + (1.0 - ADAM_B2) * (g * g)
    m_hat = m / (1.0 - ADAM_B1 ** ADAM_STEP)
    v_hat = v / (1.0 - ADAM_B2 ** ADAM_STEP)
    delta = -ADAM_LR * (m_hat / (jnp.sqrt(v_hat) + ADAM_EPS) + ADAM_WD * w)
    return delta, m, v


def _adamw_tiled(w, g, m, v, tm):
    rows, cols = w.shape

    def body(w_ref, g_ref, m_ref, v_ref, d_ref, nm_ref, nv_ref, g_out):
        g = g_ref[...]
        d_ref[...], nm_ref[...], nv_ref[...] = _adamw_math(w_ref[...], g, m_ref[...], v_ref[...])
        g_out[...] = g

    spec = _row_spec(tm, cols)
    return pl.pallas_call(
        body,
        name="adamw_w_in",
        grid=(rows // tm,),
        in_specs=[spec] * 4,
        out_specs=[spec] * 4,
        out_shape=[jax.ShapeDtypeStruct(w.shape, F32)] * 4,
        compiler_params=pltpu.CompilerParams(dimension_semantics=("parallel",), vmem_limit_bytes=VMEM_LIMIT),
    )(w, g, m, v)


def _adamw_many(ws, gs, ms, vs, sq_err):
    n = len(ws)
    gs = list(gs) + [sq_err]
    g_arrays, g_at = [], []
    for g in gs:
        arr, row = g if isinstance(g, tuple) else (g, None)
        k = next((j for j, a in enumerate(g_arrays) if a is arr), len(g_arrays))
        if k == len(g_arrays):
            g_arrays.append(arr)
        g_at.append((k, row))
    n_g = len(g_arrays)

    def body(*refs):
        w_refs, m_refs, v_refs, g_refs, outs = (refs[:n], refs[n:2 * n], refs[2 * n:3 * n], refs[3 * n:3 * n + n_g],
                                                refs[3 * n + n_g:])
        k, row = g_at[n]
        sq = g_refs[k][...] if row is None else g_refs[k][row:row + 8, :]
        outs[4 * n][...] = jnp.full((1, 1), 0.5 * jnp.sum(sq) / D_MODEL, F32)
        for i in range(n):
            k, row = g_at[i]
            g = g_refs[k][...] if row is None else g_refs[k][row:row + ws[i].shape[0], :]
            d, nm, nv = _adamw_math(w_refs[i][...], g, m_refs[i][...], v_refs[i][...])
            outs[i][...] = d
            outs[n + i][...] = nm
            outs[2 * n + i][...] = nv
            outs[3 * n + i][...] = g

    vmem_spec = pl.BlockSpec(memory_space=pltpu.VMEM)
    shapes = [jax.ShapeDtypeStruct(w.shape, F32) for w in ws]
    out = pl.pallas_call(
        body,
        name="adamw_small",
        in_specs=[vmem_spec] * (3 * n + n_g),
        out_specs=[vmem_spec] * (4 * n + 1),
        out_shape=shapes * 4 + [jax.ShapeDtypeStruct((1, 1), F32)],
        compiler_params=pltpu.CompilerParams(vmem_limit_bytes=VMEM_LIMIT),
    )(*ws, *ms, *vs, *g_arrays)
    return out[:n], out[n:2 * n], out[2 * n:3 * n], out[3 * n:4 * n], out[4 * n]


def _rope_tables(s):
    half = QK_ROPE_DIM // 2
    inv_freq = np.float32(ROPE_THETA) ** (-np.arange(half, dtype=np.float32) / np.float32(half))
    ang = (np.arange(s, dtype=np.float32)[:, None] * inv_freq[None, :]).astype(np.float32)
    cos, sin = np.cos(ang.astype(np.float64)).astype(np.float32), np.sin(ang.astype(np.float64)).astype(np.float32)
    z16 = np.zeros((s, half), np.float32)
    z32 = np.zeros((s, HEAD_PAD - QK_NOPE_DIM - QK_ROPE_DIM), np.float32)
    z64 = np.zeros((s, QK_NOPE_DIM), np.float32)
    rc = np.concatenate([np.ones((s, QK_NOPE_DIM), np.float32), cos, cos, z32], axis=1)
    rsa = np.concatenate([z64, -sin, z16, z32], axis=1)
    rsb = np.concatenate([z64, z16, sin, z32], axis=1)
    return jnp.asarray(rc), jnp.asarray(rsa), jnp.asarray(rsb)


def kernel(x, norm_in, w_in, q_norm, w_uq, kv_norm, w_ukv, pool_w, pool_scale, w_branch_attn, w_branch_pool, w_out, norm_final, loss_target, m_norm_in, m_w_in, m_q_norm, m_w_uq, m_kv_norm, m_w_ukv, m_pool_w, m_pool_scale, m_w_branch_attn, m_w_branch_pool, m_w_out, m_norm_final, v_norm_in, v_w_in, v_q_norm, v_w_uq, v_kv_norm, v_w_ukv, v_pool_w, v_pool_scale, v_w_branch_attn, v_w_branch_pool, v_w_out, v_norm_final):
    s = x.shape[1]
    t_att, t_row = _tiles(s)
    x2 = x.reshape(s, D_MODEL)
    tgt = loss_target.reshape(s, D_MODEL)

    early = [a.astype(BF16) for a in (w_in.T, w_uq.reshape(96, 768), w_ukv.reshape(64, 1024))]
    cx, cy = lax.axis_index("x"), lax.axis_index("y")
    others = [2 * ox + oy for ox, oy in _other_chips(cx, cy)]
    hn, z_sh, (w_in_t, w_uq_all, w_ukv_all) = _inproj_fwd(
        jnp.stack([2 * cx + cy, others[1], others[2], others[0]]).astype(jnp.int32), x2, norm_in.reshape(1, -1),
        early, 4 * t_row)
    rc, rsa, rsb = _rope_tables(s)
    g_in = norm_in.reshape(1, -1)
    g_q = q_norm.reshape(-1, LANES)
    g_kv = kv_norm.reshape(1, -1)
    g_f = norm_final.reshape(1, -1)
    ps = pool_scale.reshape(1, -1)

    q, k, v, q_t, v_t, wuq_p, wk_p, wv, *late, pw_bf = _qkv_fwd(
        z_sh, g_q, g_kv, w_uq_all, w_ukv_all, rc, rsa, rsb, [w_branch_attn, w_branch_pool, w_out, pool_w], 2 * t_row)
    o, lse, (w_ba_all, w_bp_all, w_out_all) = _attn_fwd(q_t, k, v_t, late, t_att)
    w_out_f = w_out_all.reshape(D_MODEL, D_MODEL)

    do, delta, dgattn, dgpool, dgmerge, ddc, dh, gs_mid, d_w_out, d_w_ba, d_w_bp = _mid(o, z_sh, x2, tgt, pw_bf, ps, w_ba_all, w_bp_all, w_out_f, g_f, t_row)

    late_grads = [d_w_ba, d_w_bp, d_w_out.reshape(N_CHIPS, 256, D_MODEL)]
    dq, dk_t, dv_t, (g_w_ba, g_w_bp, g_w_out), g_small_mid = _attn_bwd(q, q_t, k, v, do, lse, delta, late_grads,
                                                                      gs_mid, t_att)
    dzq, dzkv, dzkr, d_w_uq, d_w_ukv, d_q_norm, d_kv_norm = _qkv_bwd(
        dq, dk_t, dv_t, z_sh, g_q, g_kv, wuq_p, wk_p, wv, rc, rsa, rsb, 2 * t_row)
    grad_x, gs, dz_sh = _inproj_bwd_x(dzq, dzkv, dzkr, dgattn, ddc, dgpool, dgmerge, x2, dh, g_in, d_q_norm, d_kv_norm,
                                      w_in_t, 2 * t_row)

    order = jnp.stack(others + [2 * cx + cy]).astype(jnp.int32)
    g_w_in_t, g_w_uq, g_w_ukv, g_small = _inproj_bwd_w(
        order, dz_sh, hn, d_w_uq, d_w_ukv, gs, 4 * t_row)
    g_w_uq = g_w_uq.reshape(w_uq.shape)
    g_w_ukv = g_w_ukv.reshape(w_ukv.shape)

    dl_w_in, nm_w_in, nv_w_in, g_w_in = (a.T for a in _adamw_tiled(w_in.T, g_w_in_t, m_w_in.T, v_w_in.T, 152))

    packed = {n: (g_small_mid, r) for n, r in _first_rows(SMALL_MID).items() if n != "sq_err"}
    packed.update({n: (g_small, r) for n, r in _first_rows(SMALL_LATE).items()})

    def as_rows(n, a):
        return a.reshape(-1, LANES) if n in packed else a

    names = ["norm_in", "q_norm", "w_uq", "kv_norm", "w_ukv", "pool_w", "pool_scale", "w_branch_attn",
             "w_branch_pool", "w_out", "norm_final"]
    ws = dict(norm_in=norm_in, q_norm=q_norm, w_uq=w_uq, kv_norm=kv_norm, w_ukv=w_ukv, pool_w=pool_w,
              pool_scale=pool_scale, w_branch_attn=w_branch_attn, w_branch_pool=w_branch_pool, w_out=w_out,
              norm_final=norm_final)
    gsd = dict(packed, w_uq=g_w_uq, w_ukv=g_w_ukv, w_branch_attn=g_w_ba, w_branch_pool=g_w_bp, w_out=g_w_out)
    msd = dict(norm_in=m_norm_in, q_norm=m_q_norm, w_uq=m_w_uq, kv_norm=m_kv_norm, w_ukv=m_w_ukv, pool_w=m_pool_w,
               pool_scale=m_pool_scale, w_branch_attn=m_w_branch_attn, w_branch_pool=m_w_branch_pool, w_out=m_w_out,
               norm_final=m_norm_final)
    vsd = dict(norm_in=v_norm_in, q_norm=v_q_norm, w_uq=v_w_uq, kv_norm=v_kv_norm, w_ukv=v_w_ukv, pool_w=v_pool_w,
               pool_scale=v_pool_scale, w_branch_attn=v_w_branch_attn, w_branch_pool=v_w_branch_pool, w_out=v_w_out,
               norm_final=v_norm_final)
    dls, nms, nvs, g_outs, loss = _adamw_many(
        [as_rows(n, ws[n]) for n in names], [gsd[n] for n in names], [as_rows(n, msd[n]) for n in names],
        [as_rows(n, vsd[n]) for n in names], (g_small_mid, _first_rows(SMALL_MID)["sq_err"]))

    grads = dict(zip(names, g_outs))
    grads["w_in"] = g_w_in
    delta_w = {n: d.reshape(ws[n].shape) for n, d in zip(names, dls)}
    new_m = {n: d.reshape(ws[n].shape) for n, d in zip(names, nms)}
    new_v = {n: d.reshape(ws[n].shape) for n, d in zip(names, nvs)}
    delta_w["w_in"], new_m["w_in"], new_v["w_in"] = dl_w_in, nm_w_in, nv_w_in
    ws["w_in"] = w_in

    order = ["norm_in", "w_in", "q_norm", "w_uq", "kv_norm", "w_ukv", "pool_w", "pool_scale", "w_branch_attn",
             "w_branch_pool", "w_out", "norm_final"]
    return (loss.reshape(()), grad_x.reshape(x.shape),
            *[grads[n].reshape(ws[n].shape) for n in order],
            *[delta_w[n] for n in order], *[new_m[n] for n in order], *[new_v[n] for n in order])
```

```python
import functools

import jax
import jax.numpy as jnp
import numpy as np
from jax import lax
from jax.experimental import pallas as pl
from jax.experimental.pallas import tpu as pltpu

F32 = jnp.float32
BF16 = jnp.bfloat16
MESH = pl.DeviceIdType.MESH

D_MODEL = 1024
CHUNK = 64
MLA_HEADS = 8
QK_NOPE_DIM = 64
QK_ROPE_DIM = 32
V_HEAD_DIM = 64
Q_LORA_RANK = 384
KV_LORA_RANK = 256
MLA_WIDTH = MLA_HEADS * V_HEAD_DIM
ROPE_THETA = 10000.0
POOL_WINDOWS = (2, 4, 8, 16)
POOL_WIDTH = 512
POOL_GROUP_DIM = 128
BRANCH_COLS = D_MODEL // 4
FWD_HEADS = 8
BWD_HEADS = 4
POOL_HALO = 16
EPS = 1e-6
IN_TOTAL = 4256
HEAD_PAD = 128
ATT_SCALE = (QK_NOPE_DIM + QK_ROPE_DIM) ** -0.5
ATT_SCALE_LOG2E = ATT_SCALE * 1.4426950408889634

ADAM_LR = 0.001
ADAM_B1 = 0.9
ADAM_B2 = 0.999
ADAM_EPS = 1e-08
ADAM_WD = 0.01
ADAM_STEP = 10

N_CHIPS = 4
N_DEV = 8
LANES = 128
VMEM_LIMIT = 60 * 1024 * 1024

IN_SEGMENTS = ((384, 384), (256, 256), (32, HEAD_PAD), (512, 512), (512, 512), (512, 512), (2048, 2048))
SHARD_COLS = IN_TOTAL // N_CHIPS
ZQ_COLS = slice(0, 384)
ZKV_COLS = slice(384, 640)
ZKR_TILE = slice(640, 768)


def _shard_pieces():
    bounds, off = [], 0
    for w, _ in IN_SEGMENTS:
        bounds.append((off, off + w))
        off += w
    out = []
    for j in range(N_CHIPS):
        lo, hi = SHARD_COLS * j, SHARD_COLS * (j + 1)
        out.append([(i, max(lo, a) - a, min(hi, b) - a, max(lo, a) - lo)
                    for i, (a, b) in enumerate(bounds) if max(lo, a) < min(hi, b)])
    return out


SHARD_PIECES = _shard_pieces()


def _segment(z_blocks, seg):
    parts = [z_blocks[j][:, col:col + hi - lo]
             for j, pieces in enumerate(SHARD_PIECES) for sg, lo, hi, col in pieces if sg == seg]
    return parts[0] if len(parts) == 1 else jnp.concatenate(parts, axis=1)

COMM_PARAMS = (
    ("w_in", SHARD_COLS, D_MODEL, 1, 512),
    ("w_uq", 96, 768, 0, 48),
    ("w_ukv", 64, 1024, 0, 32),
    ("w_branch_attn", 512, 256, 0, 256),
    ("w_branch_pool", 512, 256, 0, 256),
    ("w_out", 256, 1024, 0, 128),
)

SMALL_MID = (
    ("pool_w", (4, 128, 128)),
    ("norm_final", (1024,)),
    ("sq_err", (8, 128)),
    ("pool_scale", (512,)),
)
SMALL_LATE = (
    ("norm_in", (1024,)),
    ("q_norm", (384,)),
    ("kv_norm", (256,)),
)


def _small_rows(shapes):
    return -(-sum(int(np.prod(s)) for _, s in shapes) // (LANES * 8)) * 8


def _first_rows(shapes):
    out, off = {}, 0
    for name, shp in shapes:
        out[name], rem = divmod(off, LANES)
        assert rem == 0, name
        off += int(np.prod(shp))
    return out


def _pack_into(dst_ref, shapes, values):
    first = _first_rows(shapes)
    for name, shp in shapes:
        v, row = values[name], first[name]
        if v.ndim == 3:
            for g in range(v.shape[0]):
                dst_ref[row + g * v.shape[1]:row + (g + 1) * v.shape[1], :] = v[g]
        elif v.shape[0] == 1 and v.shape[1] > LANES:
            for j in range(v.shape[1] // LANES):
                dst_ref[row + j:row + j + 1, :] = v[:, j * LANES:(j + 1) * LANES]
        else:
            dst_ref[row:row + v.shape[0], :] = v
    used = sum(int(np.prod(shp)) for _, shp in shapes) // LANES
    if used < dst_ref.shape[0]:
        dst_ref[used:, :] = jnp.zeros((dst_ref.shape[0] - used, LANES), dst_ref.dtype)


def _as_one_row(g):
    return jnp.concatenate([g[j:j + 1] for j in range(g.shape[0])], axis=1)


def _dot(a, b):
    return jnp.dot(a, b, preferred_element_type=F32)


def _dot_nt(a, b):
    return lax.dot_general(a, b, (((1,), (1,)), ((), ())), preferred_element_type=F32)


def _dot_tn(a, b):
    return lax.dot_general(a, b, (((0,), (0,)), ((), ())), preferred_element_type=F32)


def _sigmoid(x):
    return 1.0 / (1.0 + jnp.exp(-x))


def _colsum(x):
    return jnp.sum(x, axis=0, keepdims=True)


def _rms_fwd(x, g):
    r = lax.rsqrt(jnp.mean(x * x, axis=-1, keepdims=True) + EPS)
    xhat = x * r
    return xhat * g, xhat, r


def _rms_bwd(dy, xhat, r, g):
    dxhat = dy * g
    return r * (dxhat - xhat * jnp.mean(dxhat * xhat, axis=-1, keepdims=True))


def _rope(v, c, sa, sb):
    return v * c + pltpu.roll(v, 112, 1) * sa + pltpu.roll(v, 16, 1) * sb


def _unrope(d, c, sa, sb):
    return d * c + pltpu.roll(d * sa, 16, 1) + pltpu.roll(d * sb, 112, 1)


def _row_spec(tm, n):
    return pl.BlockSpec((tm, n), lambda i: (i, 0))


def _full_spec(shape):
    nd = len(shape)
    return pl.BlockSpec(shape, lambda i: (0,) * nd)


def _tiles(s):
    t_att = 512 if s >= 2048 else 128
    t_row = 256 if s >= 1024 else 128
    return t_att, t_row


def _inproj_fwd(order, x, norm_in, w_in_shard, up_shards, tm):
    s = x.shape[0]
    n_tiles = s // tm
    gat = _Gather(COMM_PARAMS[:3])
    n_w = len(gat.params)
    arrival = (1, 2, 0)
    n_up = len(up_shards)

    def body(order_ref, x_ref, g_ref, w_in_loc, *rest):
        up_refs, (hn_ref, z_ref), w_all = rest[:n_up], rest[n_up:n_up + 2], rest[n_up + 2:n_up + 2 + n_w]
        rest = rest[n_up + 2 + n_w:]
        (w_vmem, hn_all, w_sem), up_flat = rest[:3], rest[3:3 + n_up]
        gat.bind((w_in_loc,) + tuple(up_flat), w_all, rest[3 + n_up:])
        ph, i = pl.program_id(0), pl.program_id(1)

        @pl.when(jnp.logical_and(ph == 0, i == 0))
        def _():
            for src, dst in zip(up_refs, up_flat):
                dst[...] = jnp.concatenate([src[:, h, :] for h in range(MLA_HEADS)], axis=1).astype(BF16)
            gat.start()

        def fetch(phase):
            src = w_in_loc if phase == 0 else w_all[0].at[order_ref[phase]]
            return pltpu.make_async_copy(src, w_vmem.at[phase % 2], w_sem.at[phase % 2])

        def landed(f):
            gat.relay_one(0, arrival[f])
            gat.await_one(0, arrival[f])

        @pl.when(jnp.logical_and(ph == 0, i == 0))
        def _():
            fetch(0).start()
            fetch(0).wait()

        @pl.when(jnp.logical_and(ph == 1, i == 0))
        def _():
            landed(0)
            fetch(1).start()
            fetch(1).wait()

        for f in (1, 2):
            @pl.when(jnp.logical_and(ph == f, i == n_tiles - 1))
            def _(f=f):
                landed(f)
                fetch(f + 1).start()

            @pl.when(jnp.logical_and(ph == f + 1, i == 0))
            def _(f=f):
                fetch(f + 1).wait()

        rows = pl.ds(pl.multiple_of(i * tm, tm), tm)

        @pl.when(ph == 0)
        def _():
            hn, _, _ = _rms_fwd(x_ref[...], g_ref[...])
            hn = hn.astype(BF16)
            hn_ref[...] = hn
            hn_all[rows, :] = hn

        z_ref[0] = _dot_nt(hn_all[rows, :], w_vmem[ph % 2])

        @pl.when(jnp.logical_and(ph == N_CHIPS - 1, i == n_tiles - 1))
        def _():
            for p in range(1, n_w):
                for j in range(3):
                    gat.relay_one(p, j)
            for p in range(1, n_w):
                for j in range(3):
                    gat.await_one(p, j)
            gat.wait_sends()

    def tile_in_phase0(ph, i, order):
        return (jnp.where(ph == 0, i, n_tiles - 1), 0)

    any_spec = pl.BlockSpec(memory_space=pl.ANY)
    grid_spec = pltpu.PrefetchScalarGridSpec(
        num_scalar_prefetch=1,
        grid=(N_CHIPS, n_tiles),
        in_specs=[pl.BlockSpec((tm, D_MODEL), tile_in_phase0),
                  pl.BlockSpec((1, D_MODEL), lambda ph, i, order: (0, 0)), any_spec]
        + [pl.BlockSpec(a.shape, lambda ph, i, order: (0, 0, 0)) for a in up_shards],
        out_specs=[pl.BlockSpec((tm, D_MODEL), tile_in_phase0),
                   pl.BlockSpec((1, tm, SHARD_COLS), lambda ph, i, order: (order[ph], i, 0))] + [any_spec] * n_w,
        scratch_shapes=[pltpu.VMEM((2, SHARD_COLS, D_MODEL), BF16), pltpu.VMEM((s, D_MODEL), BF16),
                        pltpu.SemaphoreType.DMA((2,))]
        + [pltpu.VMEM((r, cc), BF16) for _, r, cc, _, _ in gat.params[1:]] + gat.scratch,
    )
    out = pl.pallas_call(
        body,
        name="inproj_fwd",
        grid_spec=grid_spec,
        out_shape=[jax.ShapeDtypeStruct((s, D_MODEL), BF16), jax.ShapeDtypeStruct((N_CHIPS, s, SHARD_COLS), F32)]
        + gat.out_shape,
        compiler_params=pltpu.CompilerParams(dimension_semantics=("arbitrary", "arbitrary"),
                                             vmem_limit_bytes=VMEM_LIMIT),
    )(order, x, norm_in, w_in_shard, *up_shards)
    return out[0], out[1], out[2:]


def _qkv_fwd(z_sh, q_norm, kv_norm, w_uq_all, w_ukv_all, rc, rsa, rsb, to_bf16, tm):
    s = z_sh.shape[1]
    n_cast = len(to_bf16)
    hw = MLA_HEADS * HEAD_PAD
    qk = QK_NOPE_DIM + QK_ROPE_DIM

    def body(z_ref, gq_ref, gkv_ref, uq_ref, ukv_ref, c_ref, sa_ref, sb_ref, *rest):
        f32_refs, rest = rest[:n_cast], rest[n_cast:]
        (q_ref, k_ref, v_ref, qt_ref, vt_ref, wuq_ref, wk_ref, wv_ref), bf_refs = rest[:8], rest[8:]

        @pl.when(pl.program_id(0) == 0)
        def _():
            for src, dst in zip(f32_refs, bf_refs):
                dst[...] = src[...].astype(BF16)
            w_q = jnp.concatenate([uq_ref[j] for j in range(N_CHIPS)], axis=0).astype(F32)
            gap = jnp.zeros((Q_LORA_RANK, HEAD_PAD - qk), F32)
            wuq_ref[...] = jnp.concatenate(
                [part for h in range(MLA_HEADS) for part in (w_q[:, h * qk:(h + 1) * qk], gap)], axis=1).astype(BF16)
            w_kv = jnp.concatenate([ukv_ref[j] for j in range(N_CHIPS)], axis=0).astype(F32)
            lane = lax.broadcasted_iota(jnp.int32, w_kv.shape, 1)
            wk_ref[...] = jnp.where(lane % HEAD_PAD < QK_NOPE_DIM, w_kv, 0.0).astype(BF16)
            wv_ref[...] = jnp.concatenate(
                [w_kv[:, h * HEAD_PAD + QK_NOPE_DIM:(h + 1) * HEAD_PAD] for h in range(MLA_HEADS)],
                axis=1).astype(BF16)

        c, sa, sb = c_ref[...], sa_ref[...], sb_ref[...]
        z0 = z_ref[0]
        cq, _, _ = _rms_fwd(z0[:, ZQ_COLS], _as_one_row(gq_ref[...]))
        qf = _dot(cq.astype(BF16), wuq_ref[...])
        ckv, _, _ = _rms_fwd(z0[:, ZKV_COLS], gkv_ref[...])
        ckv = ckv.astype(BF16)
        kn = _dot(ckv, wk_ref[...])
        lane = lax.broadcasted_iota(jnp.int32, (tm, HEAD_PAD), 1)
        zkr = jnp.where(lane < QK_ROPE_DIM, z0[:, ZKR_TILE], 0.0)
        kr = _rope(pltpu.roll(zkr, 64, 1), c, sa, sb)
        for h in range(MLA_HEADS):
            cols = slice(h * HEAD_PAD, (h + 1) * HEAD_PAD)
            qh = _rope(qf[:, cols], c, sa, sb)
            q_ref[:, cols] = qh.astype(BF16)
            qt_ref[cols, :] = qh.T.astype(BF16)
            k_ref[:, cols] = (kn[:, cols] + kr).astype(BF16)
        vf = _dot(ckv, wv_ref[...])
        v_ref[...] = vf.astype(BF16)
        vt_ref[...] = vf.T.astype(BF16)

    return pl.pallas_call(
        body,
        name="qkv_fwd",
        grid=(s // tm,),
        in_specs=[
            pl.BlockSpec((1, tm, SHARD_COLS), lambda i: (0, i, 0)),
            _full_spec(q_norm.shape), _full_spec((1, KV_LORA_RANK)),
            _full_spec(w_uq_all.shape), _full_spec(w_ukv_all.shape),
            _row_spec(tm, HEAD_PAD), _row_spec(tm, HEAD_PAD), _row_spec(tm, HEAD_PAD),
        ] + [_full_spec(a.shape) for a in to_bf16],
        out_specs=[_row_spec(tm, hw), _row_spec(tm, hw), _row_spec(tm, MLA_WIDTH),
                   pl.BlockSpec((hw, tm), lambda i: (0, i)), pl.BlockSpec((MLA_WIDTH, tm), lambda i: (0, i)),
                   _full_spec((Q_LORA_RANK, hw)), _full_spec((KV_LORA_RANK, hw)), _full_spec((KV_LORA_RANK, MLA_WIDTH))
                   ] + [_full_spec(a.shape) for a in to_bf16],
        out_shape=[jax.ShapeDtypeStruct((s, hw), BF16), jax.ShapeDtypeStruct((s, hw), BF16),
                   jax.ShapeDtypeStruct((s, MLA_WIDTH), BF16),
                   jax.ShapeDtypeStruct((hw, s), BF16), jax.ShapeDtypeStruct((MLA_WIDTH, s), BF16),
                   jax.ShapeDtypeStruct((Q_LORA_RANK, hw), BF16), jax.ShapeDtypeStruct((KV_LORA_RANK, hw), BF16),
                   jax.ShapeDtypeStruct((KV_LORA_RANK, MLA_WIDTH), BF16)
                   ] + [jax.ShapeDtypeStruct(a.shape, BF16) for a in to_bf16],
        compiler_params=pltpu.CompilerParams(dimension_semantics=("arbitrary",), vmem_limit_bytes=VMEM_LIMIT),
    )(z_sh, q_norm, kv_norm, w_uq_all, w_ukv_all, rc, rsa, rsb, *to_bf16)


def _chunk_mask(t, keys_on_rows):
    rows = lax.broadcasted_iota(jnp.int32, (t, t), 0) // CHUNK
    cols = lax.broadcasted_iota(jnp.int32, (t, t), 1) // CHUNK
    return rows <= cols if keys_on_rows else cols <= rows


def _attn_fwd(q_t, k, v_t, late_shards, t):
    s = k.shape[0]
    groups = MLA_HEADS // FWD_HEADS
    n_q = s // t
    gat = _Gather(COMM_PARAMS[3:])
    n_w = len(gat.params)

    def body(qt_ref, k_ref, k2_ref, vt_ref, *rest):
        w_in, (o_ref, lse_ref), w_out = rest[:n_w], rest[n_w:n_w + 2], rest[n_w + 2:2 * n_w + 2]
        gat.bind(w_in, w_out, rest[2 * n_w + 2:])
        i = pl.program_id(1)
        step_no = pl.program_id(0) * n_q + i
        pl.when(step_no == 0)(gat.start)
        pl.when(step_no == groups * n_q // 2)(gat.relay)
        mask = _chunk_mask(t, True)
        qcs = [slice(hh * HEAD_PAD, (hh + 1) * HEAD_PAD) for hh in range(FWD_HEADS)]
        vcs = [slice(hh * V_HEAD_DIM, (hh + 1) * V_HEAD_DIM) for hh in range(FWD_HEADS)]
        qts = [qt_ref[qc, :] for qc in qcs]

        def step(j, carry, masked):
            keys = pl.ds(pl.multiple_of(j * t, t), t)
            out = []
            for hh in range(FWD_HEADS):
                m, l, acc = carry[hh]
                sc = _dot(k_ref[keys, qcs[hh]], qts[hh])
                if masked:
                    sc = jnp.where(mask, sc, -jnp.inf)
                m_new = jnp.maximum(m, jnp.max(sc, axis=0, keepdims=True))
                alpha = jnp.exp2((m - m_new) * ATT_SCALE_LOG2E)
                p = jnp.exp2((_dot(k2_ref[keys, qcs[hh]], qts[hh]) - m_new) * ATT_SCALE_LOG2E)
                if masked:
                    p = jnp.where(mask, p, 0.0)
                l = alpha * l + jnp.sum(p, axis=0, keepdims=True)
                acc = alpha * acc + _dot(vt_ref[vcs[hh], keys], p.astype(BF16))
                out.append((m_new, l, acc))
            return tuple(out)

        one = (jnp.full((1, t), -jnp.inf, F32), jnp.zeros((1, t), F32), jnp.zeros((V_HEAD_DIM, t), F32))
        carry = lax.fori_loop(0, i, functools.partial(step, masked=False), (one,) * FWD_HEADS)
        carry = step(i, carry, True)
        o_ref[...] = jnp.concatenate([carry[hh][2] / carry[hh][1] for hh in range(FWD_HEADS)], axis=0).T
        for hh in range(FWD_HEADS):
            m, l, _ = carry[hh]
            lse_ref[:, qcs[hh]] = jnp.broadcast_to(m * ATT_SCALE_LOG2E + jnp.log2(l), (HEAD_PAD, t)).T
        pl.when(step_no == groups * n_q - 1)(gat.finish)

    any_spec = pl.BlockSpec(memory_space=pl.ANY)
    out = pl.pallas_call(
        body,
        name="attn_fwd",
        grid=(groups, n_q),
        in_specs=[
            pl.BlockSpec((FWD_HEADS * HEAD_PAD, t), lambda p, i: (p, i)),
            pl.BlockSpec((s, FWD_HEADS * HEAD_PAD), lambda p, i: (0, p), pipeline_mode=pl.Buffered(1)),
            pl.BlockSpec((s, FWD_HEADS * HEAD_PAD), lambda p, i: (0, p), pipeline_mode=pl.Buffered(1)),
            pl.BlockSpec((FWD_HEADS * V_HEAD_DIM, s), lambda p, i: (p, 0), pipeline_mode=pl.Buffered(1)),
        ] + [any_spec] * n_w,
        out_specs=[
            pl.BlockSpec((t, FWD_HEADS * V_HEAD_DIM), lambda p, i: (i, p)),
            pl.BlockSpec((t, FWD_HEADS * HEAD_PAD), lambda p, i: (i, p)),
        ] + [any_spec] * n_w,
        out_shape=[jax.ShapeDtypeStruct((s, MLA_WIDTH), F32), jax.ShapeDtypeStruct((s, MLA_HEADS * HEAD_PAD), F32)]
        + gat.out_shape,
        scratch_shapes=gat.scratch,
        compiler_params=pltpu.CompilerParams(dimension_semantics=("arbitrary", "arbitrary"),
                                             vmem_limit_bytes=VMEM_LIMIT),
    )(q_t, k, k, v_t, *late_shards)
    return out[0], out[1], out[2:]


def _mid(o, z_sh, x, target, pool_w, pool_scale, w_ba, w_bp, w_out, norm_final, tm):
    s = x.shape[0]
    n_tiles = s // tm
    halo_per_tile = tm // POOL_HALO
    small_rows = _small_rows(SMALL_MID)

    def body(o_ref, z0_ref, z1_ref, z1h_ref, z2_ref, z3_ref, x_ref, t_ref, pw_ref, ps_ref, wba_ref, wbp_ref,
             wout_ref, gf_ref,
             do_ref, dl_ref, dga_ref, dgp_ref, dgm_ref, ddc_ref, dh_ref,
             small_ref, dwout_out, dwba_out, dwbp_out,
             ubuf, dwout_ref, dwba_ref, dwbp_ref, loss_ref, dpw_ref, dps_ref, dgf_ref):
        i = pl.program_id(0)

        @pl.when(i == 0)
        def _():
            loss_ref[...] = jnp.zeros_like(loss_ref)
            dwout_ref[...] = jnp.zeros_like(dwout_ref)
            dwba_ref[...] = jnp.zeros_like(dwba_ref)
            dwbp_ref[...] = jnp.zeros_like(dwbp_ref)
            dpw_ref[...] = jnp.zeros_like(dpw_ref)
            dps_ref[...] = jnp.zeros_like(dps_ref)
            dgf_ref[...] = jnp.zeros_like(dgf_ref)

        zs = [z0_ref[0], z1_ref[0], z2_ref[0], z3_ref[0]]
        o = o_ref[...]
        ga = _segment(zs, 3)
        sga = _sigmoid(ga)
        silu_a = ga * sga
        y_attn = (o * silu_a).astype(BF16)

        ubuf[0:POOL_HALO, :] = jnp.where(i > 0, _segment([None, z1h_ref[0]], 4), 0.0)
        ubuf[POOL_HALO:, :] = _segment(zs, 4)
        row = lax.broadcasted_iota(jnp.int32, (tm, POOL_GROUP_DIM), 0) + i * tm
        ps = ps_ref[...]
        gp = _segment(zs, 5)
        sgp = _sigmoid(gp)
        silu_p = gp * sgp
        d_bf, dm, inv_cnt = [], [], []
        for g, w in enumerate(POOL_WINDOWS):
            cols = slice(g * POOL_GROUP_DIM, (g + 1) * POOL_GROUP_DIM)
            wsum = ubuf[POOL_HALO:, cols]
            for kk in range(1, w):
                wsum = wsum + ubuf[POOL_HALO - kk:POOL_HALO - kk + tm, cols]
            inv = 1.0 / jnp.minimum(row + 1, w).astype(F32)
            dg = (wsum * inv - ubuf[POOL_HALO:, cols]).astype(BF16)
            d_bf.append(dg)
            inv_cnt.append(inv)
            dm.append(_dot(dg, pw_ref[g]))
        dm = jnp.concatenate(dm, axis=1)
        yp = dm * ps
        y_pool = (yp * silu_p).astype(BF16)

        a = jnp.concatenate([_dot(y_attn, wba_ref[j]) for j in range(N_CHIPS)], axis=1)
        p = jnp.concatenate([_dot(y_pool, wbp_ref[j]) for j in range(N_CHIPS)], axis=1)
        gm = _segment(zs, 6)
        gate_a = _sigmoid(gm[:, :D_MODEL])
        gate_p = _sigmoid(gm[:, D_MODEL:])
        merged = (gate_a * a + gate_p * p).astype(BF16)
        h = x_ref[...] + _dot(merged, wout_ref[...])
        gf = gf_ref[...]
        y, xhat, r = _rms_fwd(h, gf)
        err = y - t_ref[...]
        e2 = err * err
        e2 = jnp.sum(e2.reshape(tm // 8, 8, D_MODEL), axis=0)
        acc = e2[:, 0:LANES]
        for cidx in range(1, D_MODEL // LANES):
            acc = acc + e2[:, cidx * LANES:(cidx + 1) * LANES]
        loss_ref[...] += acc

        dy = err * (1.0 / D_MODEL)
        dgf_ref[...] += _colsum(dy * xhat)
        dh = _rms_bwd(dy, xhat, r, gf)
        dh_ref[...] = dh
        dh_bf = dh.astype(BF16)
        dwout_ref[...] += _dot_tn(merged, dh_bf)
        dmerged = _dot_nt(dh_bf, wout_ref[...])
        da = (dmerged * gate_a).astype(BF16)
        dp = (dmerged * gate_p).astype(BF16)
        dgm_ref[:, :D_MODEL] = (dmerged * a * gate_a * (1.0 - gate_a)).astype(BF16)
        dgm_ref[:, D_MODEL:] = (dmerged * p * gate_p * (1.0 - gate_p)).astype(BF16)
        dy_attn = dy_pool = None
        for j in range(N_CHIPS):
            cols = slice(j * BRANCH_COLS, (j + 1) * BRANCH_COLS)
            dwba_ref[j] += _dot_tn(y_attn, da[:, cols])
            dwbp_ref[j] += _dot_tn(y_pool, dp[:, cols])
            pa = _dot_nt(da[:, cols], wba_ref[j])
            pp = _dot_nt(dp[:, cols], wbp_ref[j])
            dy_attn = pa if dy_attn is None else dy_attn + pa
            dy_pool = pp if dy_pool is None else dy_pool + pp

        do = dy_attn * silu_a
        do_ref[...] = do
        dga_ref[...] = (dy_attn * o * (sga * (1.0 + ga * (1.0 - sga)))).astype(BF16)
        doo = do * o
        for hd in range(MLA_HEADS):
            dl = jnp.sum(doo[:, hd * V_HEAD_DIM:(hd + 1) * V_HEAD_DIM], axis=1, keepdims=True)
            dl_ref[:, hd * HEAD_PAD:(hd + 1) * HEAD_PAD] = jnp.broadcast_to(dl, (tm, HEAD_PAD))

        dyp = dy_pool * silu_p
        dgp_ref[...] = (dy_pool * yp * (sgp * (1.0 + gp * (1.0 - sgp)))).astype(BF16)
        dps_ref[...] += _colsum(dyp * dm)
        dmm = (dyp * ps).astype(BF16)
        for g in range(len(POOL_WINDOWS)):
            cols = slice(g * POOL_GROUP_DIM, (g + 1) * POOL_GROUP_DIM)
            dpw_ref[g] += _dot_tn(d_bf[g], dmm[:, cols])
            ddc_ref[:, cols] = _dot_nt(dmm[:, cols], pw_ref[g]) * inv_cnt[g]

        @pl.when(i == n_tiles - 1)
        def _():
            dwout_out[...] = dwout_ref[...].astype(BF16)
            dwba_out[...] = dwba_ref[...].astype(BF16)
            dwbp_out[...] = dwbp_ref[...].astype(BF16)
            _pack_into(small_ref, SMALL_MID, dict(pool_w=dpw_ref[...], norm_final=dgf_ref[...], sq_err=loss_ref[...],
                                                  pool_scale=dps_ref[...]))

    row_in = lambda n: _row_spec(tm, n)
    in_specs = [
        row_in(MLA_WIDTH),
        pl.BlockSpec((1, tm, SHARD_COLS), lambda i: (0, i, 0)), pl.BlockSpec((1, tm, SHARD_COLS), lambda i: (1, i, 0)),
        pl.BlockSpec((1, POOL_HALO, SHARD_COLS), lambda i: (1, jnp.maximum(i * halo_per_tile - 1, 0), 0)),
        pl.BlockSpec((1, tm, SHARD_COLS), lambda i: (2, i, 0)), pl.BlockSpec((1, tm, SHARD_COLS), lambda i: (3, i, 0)),
        row_in(D_MODEL), row_in(D_MODEL),
        _full_spec((4, POOL_GROUP_DIM, POOL_GROUP_DIM)), _full_spec((1, POOL_WIDTH)),
        _full_spec((N_CHIPS, MLA_WIDTH, BRANCH_COLS)), _full_spec((N_CHIPS, POOL_WIDTH, BRANCH_COLS)),
        _full_spec((D_MODEL, D_MODEL)), _full_spec((1, D_MODEL)),
    ]
    out_shape = [
        jax.ShapeDtypeStruct((s, MLA_WIDTH), F32),
        jax.ShapeDtypeStruct((s, MLA_HEADS * HEAD_PAD), F32),
        jax.ShapeDtypeStruct((s, MLA_WIDTH), BF16),
        jax.ShapeDtypeStruct((s, POOL_WIDTH), BF16),
        jax.ShapeDtypeStruct((s, 2 * D_MODEL), BF16),
        jax.ShapeDtypeStruct((s, POOL_WIDTH), F32),
        jax.ShapeDtypeStruct((s, D_MODEL), F32),
        jax.ShapeDtypeStruct((small_rows, LANES), F32),
        jax.ShapeDtypeStruct((D_MODEL, D_MODEL), BF16),
        jax.ShapeDtypeStruct((N_CHIPS, MLA_WIDTH, BRANCH_COLS), BF16),
        jax.ShapeDtypeStruct((N_CHIPS, POOL_WIDTH, BRANCH_COLS), BF16),
    ]
    out_specs = [
        row_in(MLA_WIDTH), row_in(MLA_HEADS * HEAD_PAD), row_in(MLA_WIDTH), row_in(POOL_WIDTH),
        row_in(2 * D_MODEL), row_in(POOL_WIDTH), row_in(D_MODEL),
        _full_spec((small_rows, LANES)), _full_spec((D_MODEL, D_MODEL)),
        _full_spec((N_CHIPS, MLA_WIDTH, BRANCH_COLS)), _full_spec((N_CHIPS, POOL_WIDTH, BRANCH_COLS)),
    ]
    return pl.pallas_call(
        body,
        name="mid",
        grid=(n_tiles,),
        in_specs=in_specs,
        out_specs=out_specs,
        out_shape=out_shape,
        scratch_shapes=[
            pltpu.VMEM((tm + POOL_HALO, POOL_WIDTH), F32),
            pltpu.VMEM((D_MODEL, D_MODEL), F32),
            pltpu.VMEM((N_CHIPS, MLA_WIDTH, BRANCH_COLS), F32),
            pltpu.VMEM((N_CHIPS, POOL_WIDTH, BRANCH_COLS), F32),
            pltpu.VMEM((8, LANES), F32),
            pltpu.VMEM((4, POOL_GROUP_DIM, POOL_GROUP_DIM), F32),
            pltpu.VMEM((1, POOL_WIDTH), F32),
            pltpu.VMEM((1, D_MODEL), F32),
        ],
        compiler_params=pltpu.CompilerParams(dimension_semantics=("arbitrary",), vmem_limit_bytes=VMEM_LIMIT),
    )(o, z_sh, z_sh, z_sh, z_sh, z_sh, x, target, pool_w, pool_scale, w_ba, w_bp, w_out, norm_final)


def _attn_bwd(q, q_t, k, v, do, lse, delta, late_grads, gs_mid, t):
    s = q.shape[0]
    groups = MLA_HEADS // BWD_HEADS
    n_q = s // t
    red = _Reduce(COMM_PARAMS[3:])
    n_w = len(red.params)
    small = _SmallSum(gs_mid.shape[0])
    n_red = len(red.scratch)

    def body(q_ref, qt_ref, do_ref, lse_ref, dl_ref, k_ref, v_ref, *rest):
        g_in, gs_ref = rest[:n_w], rest[n_w]
        (dq_ref, dk_ref, dv_ref), g_out, gsum_ref = rest[n_w + 1:n_w + 4], rest[n_w + 4:2 * n_w + 4], rest[2 * n_w + 4]
        scratch = rest[2 * n_w + 5:]
        red.bind(g_in, g_out, scratch[:n_red])
        small.bind(gs_ref, gsum_ref, scratch[n_red:])
        i = pl.program_id(1)
        step_no = pl.program_id(0) * n_q + i

        @pl.when(step_no == 0)
        def _():
            red.start()
            small.start()

        pl.when(step_no == groups * n_q // 2)(red.exchange)

        @pl.when(i == 0)
        def _():
            dk_ref[...] = jnp.zeros_like(dk_ref)
            dv_ref[...] = jnp.zeros_like(dv_ref)

        mask = _chunk_mask(t, False)
        qcs = [slice(hh * HEAD_PAD, (hh + 1) * HEAD_PAD) for hh in range(BWD_HEADS)]
        vcs = [slice(hh * V_HEAD_DIM, (hh + 1) * V_HEAD_DIM) for hh in range(BWD_HEADS)]
        qhs = [q_ref[:, qc] for qc in qcs]
        qts = [qt_ref[qc, :] for qc in qcs]
        dohs = [do_ref[:, vc].astype(BF16) for vc in vcs]
        do_t = do_ref[...].T.astype(BF16)
        dots = [do_t[vc, :] for vc in vcs]
        lses = [jnp.tile(lse_ref[:, qc], (1, t // HEAD_PAD)) for qc in qcs]
        dls = [jnp.tile(dl_ref[:, qc], (1, t // HEAD_PAD)) for qc in qcs]

        def step(j, dqs, masked):
            keys = pl.ds(pl.multiple_of(j * t, t), t)
            out = []
            for hh in range(BWD_HEADS):
                kj = k_ref[keys, qcs[hh]]
                vj = v_ref[keys, vcs[hh]]
                p = jnp.exp2(_dot_nt(qhs[hh], kj) * ATT_SCALE_LOG2E - lses[hh])
                if masked:
                    p = jnp.where(mask, p, 0.0)
                ds = (p * (_dot_nt(dohs[hh], vj) - dls[hh])).astype(BF16)
                dv_ref[vcs[hh], keys] += _dot(dots[hh], p.astype(BF16))
                dk_ref[qcs[hh], keys] += _dot(qts[hh], ds) * ATT_SCALE
                out.append(dqs[hh] + _dot(ds, kj))
            return tuple(out)

        zero = jnp.zeros((t, HEAD_PAD), F32)
        dqs = lax.fori_loop(0, i, functools.partial(step, masked=False), (zero,) * BWD_HEADS)
        dqs = step(i, dqs, True)
        for hh in range(BWD_HEADS):
            dq_ref[:, qcs[hh]] = dqs[hh] * ATT_SCALE

        @pl.when(step_no == groups * n_q - 1)
        def _():
            red.finish()
            small.finish()

    hw = MLA_HEADS * HEAD_PAD
    any_spec = pl.BlockSpec(memory_space=pl.ANY)
    out = pl.pallas_call(
        body,
        name="attn_bwd",
        grid=(groups, n_q),
        in_specs=[
            pl.BlockSpec((t, BWD_HEADS * HEAD_PAD), lambda p, i: (i, p)),
            pl.BlockSpec((BWD_HEADS * HEAD_PAD, t), lambda p, i: (p, i)),
            pl.BlockSpec((t, BWD_HEADS * V_HEAD_DIM), lambda p, i: (i, p)),
            pl.BlockSpec((t, BWD_HEADS * HEAD_PAD), lambda p, i: (i, p)),
            pl.BlockSpec((t, BWD_HEADS * HEAD_PAD), lambda p, i: (i, p)),
            pl.BlockSpec((s, BWD_HEADS * HEAD_PAD), lambda p, i: (0, p), pipeline_mode=pl.Buffered(1)),
            pl.BlockSpec((s, BWD_HEADS * V_HEAD_DIM), lambda p, i: (0, p), pipeline_mode=pl.Buffered(1)),
        ] + [any_spec] * n_w + [pl.BlockSpec(small.spec_shape, lambda p, i: (0, 0))],
        out_specs=[
            pl.BlockSpec((t, BWD_HEADS * HEAD_PAD), lambda p, i: (i, p)),
            pl.BlockSpec((BWD_HEADS * HEAD_PAD, s), lambda p, i: (p, 0)),
            pl.BlockSpec((BWD_HEADS * V_HEAD_DIM, s), lambda p, i: (p, 0)),
        ] + [any_spec] * n_w + [pl.BlockSpec(small.spec_shape, lambda p, i: (0, 0))],
        out_shape=[jax.ShapeDtypeStruct((s, hw), F32), jax.ShapeDtypeStruct((hw, s), F32),
                   jax.ShapeDtypeStruct((MLA_WIDTH, s), F32)] + red.out_shape + [small.out_shape],
        scratch_shapes=red.scratch + small.scratch,
        compiler_params=pltpu.CompilerParams(dimension_semantics=("arbitrary", "arbitrary"),
                                             vmem_limit_bytes=VMEM_LIMIT),
    )(q, q_t, do, lse, delta, k, v, *late_grads, gs_mid)
    return out[0], out[1], out[2], out[3:3 + n_w], out[3 + n_w]


def _qkv_bwd(dq, dk_t, dv_t, z_sh, q_norm, kv_norm, wuq_p, wk_p, wv, rc, rsa, rsb, tm):
    s = z_sh.shape[1]
    hw = MLA_HEADS * HEAD_PAD
    n_tiles = s // tm
    uq_shape, ukv_shape = (N_CHIPS,) + COMM_PARAMS[1][1:3], (N_CHIPS,) + COMM_PARAMS[2][1:3]

    def body(dq_ref, dk_ref, dv_ref, z_ref, gq_ref, gkv_ref, wuq_ref, wk_ref, wv_ref,
             c_ref, sa_ref, sb_ref,
             dzq_ref, dzkv_ref, dzkr_ref, duq_ref, dukv_ref, dgq_ref, dgkv_ref, dwuq_ref, dwk_ref, dwv_ref):
        i = pl.program_id(0)

        @pl.when(i == 0)
        def _():
            dwuq_ref[...] = jnp.zeros_like(dwuq_ref)
            dwk_ref[...] = jnp.zeros_like(dwk_ref)
            dwv_ref[...] = jnp.zeros_like(dwv_ref)
            dgq_ref[...] = jnp.zeros_like(dgq_ref)
            dgkv_ref[...] = jnp.zeros_like(dgkv_ref)

        c, sa, sb = c_ref[...], sa_ref[...], sb_ref[...]
        gq, gkv = _as_one_row(gq_ref[...]), gkv_ref[...]

        z0 = z_ref[0]
        cq, xq, rq = _rms_fwd(z0[:, ZQ_COLS], gq)
        dqp = jnp.concatenate(
            [_unrope(dq_ref[:, h * HEAD_PAD:(h + 1) * HEAD_PAD], c, sa, sb) for h in range(MLA_HEADS)],
            axis=1).astype(BF16)
        dwuq_ref[...] += _dot_tn(cq.astype(BF16), dqp)
        dcq = _dot_nt(dqp, wuq_ref[...])
        dgq_ref[...] += _colsum(dcq * xq)
        dzq_ref[...] = _rms_bwd(dcq, xq, rq, gq).astype(BF16)

        ckv, xkv, rkv = _rms_fwd(z0[:, ZKV_COLS], gkv)
        ckv = ckv.astype(BF16)
        dkf = dk_ref[...].T
        dk_bf = dkf.astype(BF16)
        dv_bf = dv_ref[...].T.astype(BF16)
        dwk_ref[...] += _dot_tn(ckv, dk_bf)
        dwv_ref[...] += _dot_tn(ckv, dv_bf)
        dckv = _dot_nt(dk_bf, wk_ref[...]) + _dot_nt(dv_bf, wv_ref[...])
        dgkv_ref[...] += _colsum(dckv * xkv)
        dzkv_ref[...] = _rms_bwd(dckv, xkv, rkv, gkv).astype(BF16)

        dkr = dkf[:, 0:HEAD_PAD]
        for h in range(1, MLA_HEADS):
            dkr = dkr + dkf[:, h * HEAD_PAD:(h + 1) * HEAD_PAD]
        dkr = pltpu.roll(_unrope(dkr, c, sa, sb), 64, 1)
        lane = lax.broadcasted_iota(jnp.int32, (tm, HEAD_PAD), 1)
        dzkr_ref[...] = jnp.where(lane < QK_ROPE_DIM, dkr, 0.0).astype(BF16)

        @pl.when(i == n_tiles - 1)
        def _():
            qk = QK_NOPE_DIM + QK_ROPE_DIM
            d_uq = jnp.concatenate([dwuq_ref[:, h * HEAD_PAD:h * HEAD_PAD + qk] for h in range(MLA_HEADS)],
                                   axis=1).astype(BF16)
            d_ukv = jnp.concatenate(
                [part for h in range(MLA_HEADS)
                 for part in (dwk_ref[:, h * HEAD_PAD:h * HEAD_PAD + QK_NOPE_DIM],
                              dwv_ref[:, h * V_HEAD_DIM:(h + 1) * V_HEAD_DIM])], axis=1).astype(BF16)
            for j in range(N_CHIPS):
                duq_ref[j] = d_uq[j * uq_shape[1]:(j + 1) * uq_shape[1]]
                dukv_ref[j] = d_ukv[j * ukv_shape[1]:(j + 1) * ukv_shape[1]]

    return pl.pallas_call(
        body,
        name="qkv_bwd",
        grid=(s // tm,),
        in_specs=[
            _row_spec(tm, hw), pl.BlockSpec((hw, tm), lambda i: (0, i)), pl.BlockSpec((MLA_WIDTH, tm), lambda i: (0, i)),
            pl.BlockSpec((1, tm, SHARD_COLS), lambda i: (0, i, 0)),
            _full_spec(q_norm.shape), _full_spec((1, KV_LORA_RANK)),
            _full_spec((Q_LORA_RANK, hw)), _full_spec((KV_LORA_RANK, hw)), _full_spec((KV_LORA_RANK, MLA_WIDTH)),
            _row_spec(tm, HEAD_PAD), _row_spec(tm, HEAD_PAD), _row_spec(tm, HEAD_PAD),
        ],
        out_specs=[
            _row_spec(tm, Q_LORA_RANK), _row_spec(tm, KV_LORA_RANK), _row_spec(tm, HEAD_PAD),
            _full_spec(uq_shape), _full_spec(ukv_shape),
            _full_spec((1, Q_LORA_RANK)), _full_spec((1, KV_LORA_RANK)),
        ],
        out_shape=[
            jax.ShapeDtypeStruct((s, Q_LORA_RANK), BF16), jax.ShapeDtypeStruct((s, KV_LORA_RANK), BF16),
            jax.ShapeDtypeStruct((s, HEAD_PAD), BF16),
            jax.ShapeDtypeStruct(uq_shape, BF16), jax.ShapeDtypeStruct(ukv_shape, BF16),
            jax.ShapeDtypeStruct((1, Q_LORA_RANK), F32), jax.ShapeDtypeStruct((1, KV_LORA_RANK), F32),
        ],
        scratch_shapes=[pltpu.VMEM((Q_LORA_RANK, hw), F32), pltpu.VMEM((KV_LORA_RANK, hw), F32),
                        pltpu.VMEM((KV_LORA_RANK, MLA_WIDTH), F32)],
        compiler_params=pltpu.CompilerParams(dimension_semantics=("arbitrary",), vmem_limit_bytes=VMEM_LIMIT),
    )(dq, dk_t, dv_t, z_sh, q_norm, kv_norm, wuq_p, wk_p, wv, rc, rsa, rsb)


def _inproj_bwd_x(dzq, dzkv, dzkr, dgattn, ddc, dgpool, dgmerge, x, dh, norm_in, d_q_norm, d_kv_norm, w_in_t, tm):
    s = x.shape[0]
    n_tiles = s // tm
    halo_per_tile = tm // POOL_HALO
    n_halo = s // POOL_HALO
    u_seg = 4
    small_rows = _small_rows(SMALL_LATE)

    def body(dzq_ref, dzkv_ref, dzkr_ref, dga_ref, ddc_ref, ddn_ref, dgp_ref, dgm_ref, x_ref, dh_ref,
             g_ref, dgq_ref, dgkv_ref, w_hbm, gx_ref, small_ref, dzs_ref, w_vmem, dbuf, sem, dgin_ref):
        i = pl.program_id(0)

        @pl.when(i == 0)
        def _():
            cp = pltpu.make_async_copy(w_hbm, w_vmem, sem)
            cp.start()
            dgin_ref[...] = jnp.zeros_like(dgin_ref)
            cp.wait()

        dbuf[0:tm, :] = ddc_ref[...]
        dbuf[tm:, :] = jnp.where(i < n_tiles - 1, ddn_ref[...], 0.0)
        row = lax.broadcasted_iota(jnp.int32, (tm, POOL_GROUP_DIM), 0) + i * tm
        du = []
        for g, w in enumerate(POOL_WINDOWS):
            cols = slice(g * POOL_GROUP_DIM, (g + 1) * POOL_GROUP_DIM)
            fsum = dbuf[0:tm, cols]
            for kk in range(1, w):
                fsum = fsum + dbuf[kk:kk + tm, cols]
            du.append(fsum - dbuf[0:tm, cols] * jnp.minimum(row + 1, w).astype(F32))
        du = jnp.concatenate(du, axis=1).astype(BF16)

        dz = [dzq_ref[...], dzkv_ref[...], dzkr_ref[...], dga_ref[...], du, dgp_ref[...], dgm_ref[...]]
        dz = jnp.concatenate([d[:, :w] for d, (w, _) in zip(dz, IN_SEGMENTS)], axis=1)
        for j in range(N_CHIPS):
            dzs_ref[j] = dz[:, j * SHARD_COLS:(j + 1) * SHARD_COLS].T
        dhn = _dot(dz, w_vmem[...])

        g = g_ref[...]
        _, xhat, r = _rms_fwd(x_ref[...], g)
        dgin_ref[...] += _colsum(dhn * xhat)
        gx_ref[...] = dh_ref[...] + _rms_bwd(dhn, xhat, r, g)

        @pl.when(i == n_tiles - 1)
        def _():
            _pack_into(small_ref, SMALL_LATE, dict(norm_in=dgin_ref[...], q_norm=dgq_ref[...], kv_norm=dgkv_ref[...]))

    any_spec = pl.BlockSpec(memory_space=pl.ANY)
    seg_w = [wide for _, wide in IN_SEGMENTS]
    return pl.pallas_call(
        body,
        name="inproj_bwd_x",
        grid=(n_tiles,),
        in_specs=[
            _row_spec(tm, seg_w[0]), _row_spec(tm, seg_w[1]), _row_spec(tm, seg_w[2]),
            _row_spec(tm, seg_w[3]), _row_spec(tm, seg_w[u_seg]),
            pl.BlockSpec((POOL_HALO, POOL_WIDTH), lambda i: (jnp.minimum((i + 1) * halo_per_tile, n_halo - 1), 0)),
            _row_spec(tm, seg_w[5]), _row_spec(tm, seg_w[6]),
            _row_spec(tm, D_MODEL), _row_spec(tm, D_MODEL),
            _full_spec((1, D_MODEL)), _full_spec((1, Q_LORA_RANK)), _full_spec((1, KV_LORA_RANK)), any_spec,
        ],
        out_specs=[_row_spec(tm, D_MODEL), _full_spec((small_rows, LANES)),
                   pl.BlockSpec((N_CHIPS, SHARD_COLS, tm), lambda i: (0, 0, i))],
        out_shape=[jax.ShapeDtypeStruct((s, D_MODEL), F32), jax.ShapeDtypeStruct((small_rows, LANES), F32),
                   jax.ShapeDtypeStruct((N_CHIPS, SHARD_COLS, s), BF16)],
        scratch_shapes=[
            pltpu.VMEM((IN_TOTAL, D_MODEL), BF16),
            pltpu.VMEM((tm + POOL_HALO, POOL_WIDTH), F32),
            pltpu.SemaphoreType.DMA,
            pltpu.VMEM((1, D_MODEL), F32),
        ],
        compiler_params=pltpu.CompilerParams(dimension_semantics=("arbitrary",), vmem_limit_bytes=VMEM_LIMIT),
    )(dzq, dzkv, dzkr, dgattn, ddc, ddc, dgpool, dgmerge, x, dh, norm_in, d_q_norm, d_kv_norm,
      w_in_t.reshape(IN_TOTAL, D_MODEL))


def _inproj_bwd_w(order, dz_sh, hn, g_uq, g_ukv, gs, tm):
    s = hn.shape[0]
    n_tiles = s // tm
    hc = D_MODEL // 2
    red = _Reduce(COMM_PARAMS[1:3])
    small = _SmallSum(gs.shape[0])
    n_red = len(red.scratch)

    def body(order_ref, dz_ref, hn_ref, guq_hbm, gukv_hbm, gs_ref, gw_hbm, guq_out, gukv_out, gsum_ref,
             acc, pm_w, a_w, b_w, r_w, w_send, w_recv, w_local, *more_scratch):
        ph, i = pl.program_id(0), pl.program_id(1)
        x, y, c = lax.axis_index("x"), lax.axis_index("y"), lax.axis_index("c")
        k = 2 * x + y
        me, sibling = (x, y, c), (x, y, 1 - c)
        chips = _other_chips(x, y)
        shard_of_phase = [2 * cx + cy for cx, cy in chips] + [k]
        copy = _remote_copier(w_send, w_recv)
        red.bind([guq_hbm, gukv_hbm], [guq_out, gukv_out], more_scratch[:n_red])
        small.bind(gs_ref, gsum_ref, more_scratch[n_red:])
        mine = pl.ds(pl.multiple_of(c * hc, hc), hc)
        theirs = pl.ds(pl.multiple_of((1 - c) * hc, hc), hc)

        def to_sibling(f):
            j = shard_of_phase[f]
            return copy(f, pm_w.at[j, 1 - c], a_w.at[j], sibling)

        def pair_sum(f):
            cx, cy = chips[f]
            return copy(4 + f, pm_w.at[shard_of_phase[f], c], b_w.at[f], (cx, cy, c))

        def finished():
            return copy(7, r_w, gw_hbm.at[:, mine], sibling)

        @pl.when(jnp.logical_and(ph == 0, i == 0))
        def _():
            red.start()
            small.start()

        part = _dot(dz_ref[0], hn_ref[...])

        @pl.when(i == 0)
        def _():
            acc[...] = part

        @pl.when(i > 0)
        def _():
            acc[...] += part

        for f in range(3):
            @pl.when(jnp.logical_and(ph == f + 1, i == 0))
            def _(f=f):
                j = shard_of_phase[f]
                copy(f, a_w.at[j], a_w.at[j], me).wait_recv()
                pm_w[j, c] = (pm_w[j, c].astype(F32) + a_w[j].astype(F32)).astype(BF16)
                pair_sum(f).start()
                if f == 0:
                    red.exchange()

        for f in range(4):
            @pl.when(jnp.logical_and(ph == f, i == n_tiles - 1))
            def _(f=f):
                j = shard_of_phase[f]
                pm_w[j, 0] = acc[:, :hc].astype(BF16)
                pm_w[j, 1] = acc[:, hc:].astype(BF16)
                to_sibling(f).start()
                if f < 3:
                    return
                copy(3, a_w.at[k], a_w.at[k], me).wait_recv()
                r_w[...] = pm_w[k, c].astype(F32) + a_w[k].astype(F32)
                for g in range(3):
                    copy(4 + g, b_w.at[g], b_w.at[g], me).wait_recv()
                    r_w[...] = r_w[...] + b_w[g].astype(F32)
                store = pltpu.make_async_copy(r_w, gw_hbm.at[:, mine], w_local)
                store.start()
                finished().start()
                red.finish()
                small.finish()
                copy(7, gw_hbm.at[:, theirs], gw_hbm.at[:, theirs], me).wait_recv()
                store.wait()
                for g in range(4):
                    to_sibling(g).wait_send()
                for g in range(3):
                    pair_sum(g).wait_send()
                finished().wait_send()

    any_spec = pl.BlockSpec(memory_space=pl.ANY)
    n_sem = 8
    grid_spec = pltpu.PrefetchScalarGridSpec(
        num_scalar_prefetch=1,
        grid=(N_CHIPS, n_tiles),
        in_specs=[
            pl.BlockSpec((1, SHARD_COLS, tm), lambda ph, i, order: (order[ph], 0, i)),
            pl.BlockSpec((tm, D_MODEL), lambda ph, i, order: (i, 0)),
            any_spec, any_spec,
            pl.BlockSpec(small.spec_shape, lambda ph, i, order: (0, 0)),
        ],
        out_specs=[any_spec, any_spec, any_spec, pl.BlockSpec(small.spec_shape, lambda ph, i, order: (0, 0))],
        scratch_shapes=[
            pltpu.VMEM((SHARD_COLS, D_MODEL), F32),
            pltpu.VMEM((N_CHIPS, 2, SHARD_COLS, hc), BF16),
            pltpu.VMEM((N_CHIPS, SHARD_COLS, hc), BF16),
            pltpu.VMEM((3, SHARD_COLS, hc), BF16),
            pltpu.VMEM((SHARD_COLS, hc), F32),
            pltpu.SemaphoreType.DMA((n_sem,)), pltpu.SemaphoreType.DMA((n_sem,)), pltpu.SemaphoreType.DMA,
        ] + red.scratch + small.scratch,
    )
    out = pl.pallas_call(
        body,
        name="inproj_bwd_w",
        grid_spec=grid_spec,
        out_shape=[jax.ShapeDtypeStruct((SHARD_COLS, D_MODEL), F32)] + red.out_shape
        + [small.out_shape],
        compiler_params=pltpu.CompilerParams(dimension_semantics=("arbitrary", "arbitrary"),
                                             vmem_limit_bytes=VMEM_LIMIT),
    )(order, dz_sh, hn, g_uq, g_ukv, gs)
    return out[0], out[1], out[2], out[3]


def _other_chips(x, y):
    return ((1 - x, 1 - y), (1 - x, y), (x, 1 - y))


def _half(ref, axis, size, c, lead=()):
    window = pl.ds(pl.multiple_of(c * size, size), size)
    if axis == 0:
        return ref.at[(*lead, window, slice(None))]
    return ref.at[(*lead, slice(None), window)]


def _half_shape(rows, cols, axis, size):
    return (size, cols) if axis == 0 else (rows, size)


def _remote_copier(send_sems, recv_sems):
    def copy(sem, src, dst, to):
        return pltpu.make_async_remote_copy(src_ref=src, dst_ref=dst, send_sem=send_sems.at[sem],
                                            recv_sem=recv_sems.at[sem], device_id=to, device_id_type=MESH)
    return copy


class _Gather:
    def __init__(self, params):
        self.params = params
        n = len(params)
        self.scratch = [pltpu.SemaphoreType.DMA((6 * n,)), pltpu.SemaphoreType.DMA((6 * n,)),
                        pltpu.SemaphoreType.DMA((n,))]
        self.out_shape = [jax.ShapeDtypeStruct((N_CHIPS, r, cc), BF16) for _, r, cc, _, _ in params]

    def bind(self, ins, outs, scratch):
        self.ins, self.outs = ins, outs
        send_sems, recv_sems, self.local_sems = scratch
        self.copy = _remote_copier(send_sems, recv_sems)
        self.x, self.y, self.c = lax.axis_index("x"), lax.axis_index("y"), lax.axis_index("c")
        self.k = 2 * self.x + self.y
        self.chips = _other_chips(self.x, self.y)

    def _local(self, p):
        return pltpu.make_async_copy(self.ins[p], self.outs[p].at[self.k], self.local_sems.at[p])

    def _first(self, p, j):
        _, _, _, axis, size = self.params[p]
        cx, cy = self.chips[j]
        return self.copy(6 * p + j, _half(self.ins[p], axis, size, self.c),
                         _half(self.outs[p], axis, size, self.c, (self.k,)), (cx, cy, self.c))

    def _relay(self, p, j, half_of):
        _, _, _, axis, size = self.params[p]
        cx, cy = self.chips[j]
        block = _half(self.outs[p], axis, size, half_of, (2 * cx + cy,))
        return self.copy(6 * p + 3 + j, block, block, (self.x, self.y, 1 - self.c))

    def start(self):
        for p in range(len(self.params)):
            self._local(p).start()
            for j in (1, 2, 0):
                self._first(p, j).start()

    def relay_one(self, p, j):
        _, _, _, axis, size = self.params[p]
        cx, cy = self.chips[j]
        landed = _half(self.outs[p], axis, size, self.c, (2 * cx + cy,))
        self.copy(6 * p + j, landed, landed, (self.x, self.y, self.c)).wait_recv()
        self._relay(p, j, self.c).start()

    def await_one(self, p, j):
        self._relay(p, j, 1 - self.c).wait_recv()

    def wait_sends(self):
        for p in range(len(self.params)):
            for j in range(3):
                self._first(p, j).wait_send()
                self._relay(p, j, self.c).wait_send()
            self._local(p).wait()

    def relay(self):
        for j in range(3):
            for p in range(len(self.params)):
                self.relay_one(p, j)

    def finish(self):
        for j in range(3):
            for p in range(len(self.params)):
                self.await_one(p, j)
        self.wait_sends()


class _Reduce:
    def __init__(self, params):
        self.params = params
        n = len(params)
        halves = [_half_shape(r, cc, axis, size) for _, r, cc, axis, size in params]
        self.scratch = ([pltpu.VMEM((N_CHIPS, *h), BF16) for h in halves]
                        + [pltpu.VMEM((N_CHIPS, *h), BF16) for h in halves]
                        + [pltpu.VMEM((3, *h), BF16) for h in halves]
                        + [pltpu.VMEM(h, F32) for h in halves]
                        + [pltpu.SemaphoreType.DMA((5 * n,)), pltpu.SemaphoreType.DMA((5 * n,)),
                           pltpu.SemaphoreType.DMA((2 * n,))])
        self.out_shape = [jax.ShapeDtypeStruct((r, cc), F32) for _, r, cc, _, _ in params]

    def bind(self, g_in, g_out, scratch):
        n = len(self.params)
        self.g_in, self.g_out = g_in, g_out
        self.pm, self.a_buf = scratch[0:n], scratch[n:2 * n]
        self.b_buf, self.r_buf = scratch[2 * n:3 * n], scratch[3 * n:4 * n]
        send_sems, recv_sems, self.local_sems = scratch[4 * n:]
        self.copy = _remote_copier(send_sems, recv_sems)
        self.x, self.y, self.c = lax.axis_index("x"), lax.axis_index("y"), lax.axis_index("c")
        self.k = 2 * self.x + self.y
        self.chips = _other_chips(self.x, self.y)
        self.me = (self.x, self.y, self.c)
        self.sibling = (self.x, self.y, 1 - self.c)

    def _load(self, p):
        _, _, _, axis, size = self.params[p]
        return pltpu.make_async_copy(_half(self.g_in[p], axis, size, self.c, (slice(None),)), self.pm[p],
                                     self.local_sems.at[p])

    def _to_sibling(self, p):
        _, _, _, axis, size = self.params[p]
        return self.copy(5 * p, _half(self.g_in[p], axis, size, 1 - self.c, (slice(None),)), self.a_buf[p],
                         self.sibling)

    def _pair_sum(self, p, j):
        cx, cy = self.chips[j]
        return self.copy(5 * p + 1 + j, self.pm[p].at[2 * cx + cy], self.b_buf[p].at[j], (cx, cy, self.c))

    def _store(self, p):
        _, _, _, axis, size = self.params[p]
        n = len(self.params)
        return pltpu.make_async_copy(self.r_buf[p], _half(self.g_out[p], axis, size, self.c),
                                     self.local_sems.at[n + p])

    def _finished(self, p):
        _, _, _, axis, size = self.params[p]
        return self.copy(5 * p + 4, self.r_buf[p], _half(self.g_out[p], axis, size, self.c), self.sibling)

    def start(self):
        for p in range(len(self.params)):
            self._load(p).start()
            self._to_sibling(p).start()

    def exchange(self):
        for p in range(len(self.params)):
            self._load(p).wait()
            self.copy(5 * p, self.a_buf[p], self.a_buf[p], self.me).wait_recv()
            for j, (cx, cy) in enumerate(self.chips):
                kj = 2 * cx + cy
                self.pm[p][kj] = (self.pm[p][kj].astype(F32) + self.a_buf[p][kj].astype(F32)).astype(BF16)
                self._pair_sum(p, j).start()
            self.r_buf[p][...] = self.pm[p][self.k].astype(F32) + self.a_buf[p][self.k].astype(F32)

    def finish(self):
        for p, (_, _, _, axis, size) in enumerate(self.params):
            for j in range(3):
                self.copy(5 * p + 1 + j, self.b_buf[p].at[j], self.b_buf[p].at[j], self.me).wait_recv()
                self.r_buf[p][...] = self.r_buf[p][...] + self.b_buf[p][j].astype(F32)
            self._store(p).start()
            self._finished(p).start()
        for p, (_, _, _, axis, size) in enumerate(self.params):
            theirs = _half(self.g_out[p], axis, size, 1 - self.c)
            self.copy(5 * p + 4, theirs, theirs, self.me).wait_recv()
            self._store(p).wait()
            self._to_sibling(p).wait_send()
            for j in range(3):
                self._pair_sum(p, j).wait_send()
            self._finished(p).wait_send()


class _SmallSum:
    def __init__(self, rows):
        self.rows = rows
        self.scratch = [pltpu.VMEM((N_DEV, rows, LANES), F32),
                        pltpu.SemaphoreType.DMA((N_DEV - 1,)), pltpu.SemaphoreType.DMA((N_DEV - 1,))]
        self.out_shape = jax.ShapeDtypeStruct((rows, LANES), F32)
        self.spec_shape = (rows, LANES)

    def bind(self, src, dst, scratch):
        self.src, self.dst = src, dst
        self.buf, send_sems, recv_sems = scratch
        self.copy = _remote_copier(send_sems, recv_sems)
        self.x, self.y, self.c = lax.axis_index("x"), lax.axis_index("y"), lax.axis_index("c")

    def _send(self, f):
        fx, fy, fc = [(a, b, d) for a in (0, 1) for b in (0, 1) for d in (0, 1)][f]
        x, y, c = self.x, self.y, self.c
        peer = (1 - x if fx else x, 1 - y if fy else y, 1 - c if fc else c)
        return self.copy(f - 1, self.src, self.buf.at[f], peer)

    def start(self):
        for f in range(1, N_DEV):
            self._send(f).start()
        self.buf[0] = self.src[...]

    def finish(self):
        me = (self.x, self.y, self.c)
        for f in range(1, N_DEV):
            self.copy(f - 1, self.buf.at[f], self.buf.at[f], me).wait_recv()
        dev = 4 * self.x + 2 * self.y + self.c
        total = self.buf[dev]
        for d in range(1, N_DEV):
            total = total + self.buf[jnp.bitwise_xor(dev, d)]
        self.dst[...] = total
        for f in range(1, N_DEV):
            self._send(f).wait_send()


def _adamw_math(w, g, m, v):
    m = ADAM_B1 * m + (1.0 - ADAM_B1) * g
    v = ADAM_B2 * v + (1.0 - ADAM_B2) * (g * g)
    m_hat = m / (1.0 - ADAM_B1 ** ADAM_STEP)
    v_hat = v / (1.0 - ADAM_B2 ** ADAM_STEP)
    delta = -ADAM_LR * (m_hat / (jnp.sqrt(v_hat) + ADAM_EPS) + ADAM_WD * w)
    return delta, m, v


def _adamw_tiled(w, g, m, v, tm):
    rows, cols = w.shape

    def body(w_ref, g_ref, m_ref, v_ref, d_ref, nm_ref, nv_ref, g_out):
        g = g_ref[...]
        d_ref[...], nm_ref[...], nv_ref[...] = _adamw_math(w_ref[...], g, m_ref[...], v_ref[...])
        g_out[...] = g

    spec = _row_spec(tm, cols)
    return pl.pallas_call(
        body,
        name="adamw_w_in",
        grid=(rows // tm,),
        in_specs=[spec] * 4,
        out_specs=[spec] * 4,
        out_shape=[jax.ShapeDtypeStruct(w.shape, F32)] * 4,
        compiler_params=pltpu.CompilerParams(dimension_semantics=("parallel",), vmem_limit_bytes=VMEM_LIMIT),
    )(w, g, m, v)


def _adamw_many(ws, gs, ms, vs, sq_err):
    n = len(ws)
    gs = list(gs) + [sq_err]
    g_arrays, g_at = [], []
    for g in gs:
        arr, row = g if isinstance(g, tuple) else (g, None)
        k = next((j for j, a in enumerate(g_arrays) if a is arr), len(g_arrays))
        if k == len(g_arrays):
            g_arrays.append(arr)
        g_at.append((k, row))
    n_g = len(g_arrays)

    def body(*refs):
        w_refs, m_refs, v_refs, g_refs, outs = (refs[:n], refs[n:2 * n], refs[2 * n:3 * n], refs[3 * n:3 * n + n_g],
                                                refs[3 * n + n_g:])
        k, row = g_at[n]
        sq = g_refs[k][...] if row is None else g_refs[k][row:row + 8, :]
        outs[4 * n][...] = jnp.full((1, 1), 0.5 * jnp.sum(sq) / D_MODEL, F32)
        for i in range(n):
            k, row = g_at[i]
            g = g_refs[k][...] if row is None else g_refs[k][row:row + ws[i].shape[0], :]
            d, nm, nv = _adamw_math(w_refs[i][...], g, m_refs[i][...], v_refs[i][...])
            outs[i][...] = d
            outs[n + i][...] = nm
            outs[2 * n + i][...] = nv
            outs[3 * n + i][...] = g

    vmem_spec = pl.BlockSpec(memory_space=pltpu.VMEM)
    shapes = [jax.ShapeDtypeStruct(w.shape, F32) for w in ws]
    out = pl.pallas_call(
        body,
        name="adamw_small",
        in_specs=[vmem_spec] * (3 * n + n_g),
        out_specs=[vmem_spec] * (4 * n + 1),
        out_shape=shapes * 4 + [jax.ShapeDtypeStruct((1, 1), F32)],
        compiler_params=pltpu.CompilerParams(vmem_limit_bytes=VMEM_LIMIT),
    )(*ws, *ms, *vs, *g_arrays)
    return out[:n], out[n:2 * n], out[2 * n:3 * n], out[3 * n:4 * n], out[4 * n]


def _rope_tables(s):
    half = QK_ROPE_DIM // 2
    inv_freq = np.float32(ROPE_THETA) ** (-np.arange(half, dtype=np.float32) / np.float32(half))
    ang = (np.arange(s, dtype=np.float32)[:, None] * inv_freq[None, :]).astype(np.float32)
    cos, sin = np.cos(ang.astype(np.float64)).astype(np.float32), np.sin(ang.astype(np.float64)).astype(np.float32)
    z16 = np.zeros((s, half), np.float32)
    z32 = np.zeros((s, HEAD_PAD - QK_NOPE_DIM - QK_ROPE_DIM), np.float32)
    z64 = np.zeros((s, QK_NOPE_DIM), np.float32)
    rc = np.concatenate([np.ones((s, QK_NOPE_DIM), np.float32), cos, cos, z32], axis=1)
    rsa = np.concatenate([z64, -sin, z16, z32], axis=1)
    rsb = np.concatenate([z64, z16, sin, z32], axis=1)
    return jnp.asarray(rc), jnp.asarray(rsa), jnp.asarray(rsb)


def kernel(x, norm_in, w_in, q_norm, w_uq, kv_norm, w_ukv, pool_w, pool_scale, w_branch_attn, w_branch_pool, w_out, norm_final, loss_target, m_norm_in, m_w_in, m_q_norm, m_w_uq, m_kv_norm, m_w_ukv, m_pool_w, m_pool_scale, m_w_branch_attn, m_w_branch_pool, m_w_out, m_norm_final, v_norm_in, v_w_in, v_q_norm, v_w_uq, v_kv_norm, v_w_ukv, v_pool_w, v_pool_scale, v_w_branch_attn, v_w_branch_pool, v_w_out, v_norm_final):
    s = x.shape[1]
    t_att, t_row = _tiles(s)
    x2 = x.reshape(s, D_MODEL)
    tgt = loss_target.reshape(s, D_MODEL)

    cx, cy = lax.axis_index("x"), lax.axis_index("y")
    others = [2 * ox + oy for ox, oy in _other_chips(cx, cy)]
    hn, z_sh, (w_in_t, w_uq_all, w_ukv_all) = _inproj_fwd(
        jnp.stack([2 * cx + cy, others[1], others[2], others[0]]).astype(jnp.int32), x2, norm_in.reshape(1, -1),
        w_in.T.astype(BF16), [w_uq, w_ukv], 4 * t_row)
    rc, rsa, rsb = _rope_tables(s)
    g_in = norm_in.reshape(1, -1)
    g_q = q_norm.reshape(-1, LANES)
    g_kv = kv_norm.reshape(1, -1)
    g_f = norm_final.reshape(1, -1)
    ps = pool_scale.reshape(1, -1)

    q, k, v, q_t, v_t, wuq_p, wk_p, wv, *late, pw_bf = _qkv_fwd(
        z_sh, g_q, g_kv, w_uq_all, w_ukv_all, rc, rsa, rsb, [w_branch_attn, w_branch_pool, w_out, pool_w], 2 * t_row)
    o, lse, (w_ba_all, w_bp_all, w_out_all) = _attn_fwd(q_t, k, v_t, late, t_att)
    w_out_f = w_out_all.reshape(D_MODEL, D_MODEL)

    do, delta, dgattn, dgpool, dgmerge, ddc, dh, gs_mid, d_w_out, d_w_ba, d_w_bp = _mid(o, z_sh, x2, tgt, pw_bf, ps, w_ba_all, w_bp_all, w_out_f, g_f, t_row)

    late_grads = [d_w_ba, d_w_bp, d_w_out.reshape(N_CHIPS, 256, D_MODEL)]
    dq, dk_t, dv_t, (g_w_ba, g_w_bp, g_w_out), g_small_mid = _attn_bwd(q, q_t, k, v, do, lse, delta, late_grads,
                                                                      gs_mid, t_att)
    dzq, dzkv, dzkr, d_w_uq, d_w_ukv, d_q_norm, d_kv_norm = _qkv_bwd(
        dq, dk_t, dv_t, z_sh, g_q, g_kv, wuq_p, wk_p, wv, rc, rsa, rsb, 2 * t_row)
    grad_x, gs, dz_sh = _inproj_bwd_x(dzq, dzkv, dzkr, dgattn, ddc, dgpool, dgmerge, x2, dh, g_in, d_q_norm, d_kv_norm,
                                      w_in_t, 2 * t_row)

    order = jnp.stack(others + [2 * cx + cy]).astype(jnp.int32)
    g_w_in_t, g_w_uq, g_w_ukv, g_small = _inproj_bwd_w(
        order, dz_sh, hn, d_w_uq, d_w_ukv, gs, 4 * t_row)
    g_w_uq = g_w_uq.reshape(w_uq.shape)
    g_w_ukv = g_w_ukv.reshape(w_ukv.shape)

    dl_w_in, nm_w_in, nv_w_in, g_w_in = (a.T for a in _adamw_tiled(w_in.T, g_w_in_t, m_w_in.T, v_w_in.T, 152))

    packed = {n: (g_small_mid, r) for n, r in _first_rows(SMALL_MID).items() if n != "sq_err"}
    packed.update({n: (g_small, r) for n, r in _first_rows(SMALL_LATE).items()})

    def as_rows(n, a):
        return a.reshape(-1, LANES) if n in packed else a

    names = ["norm_in", "q_norm", "w_uq", "kv_norm", "w_ukv", "pool_w", "pool_scale", "w_branch_attn",
             "w_branch_pool", "w_out", "norm_final"]
    ws = dict(norm_in=norm_in, q_norm=q_norm, w_uq=w_uq, kv_norm=kv_norm, w_ukv=w_ukv, pool_w=pool_w,
              pool_scale=pool_scale, w_branch_attn=w_branch_attn, w_branch_pool=w_branch_pool, w_out=w_out,
              norm_final=norm_final)
    gsd = dict(packed, w_uq=g_w_uq, w_ukv=g_w_ukv, w_branch_attn=g_w_ba, w_branch_pool=g_w_bp, w_out=g_w_out)
    msd = dict(norm_in=m_norm_in, q_norm=m_q_norm, w_uq=m_w_uq, kv_norm=m_kv_norm, w_ukv=m_w_ukv, pool_w=m_pool_w,
               pool_scale=m_pool_scale, w_branch_attn=m_w_branch_attn, w_branch_pool=m_w_branch_pool, w_out=m_w_out,
               norm_final=m_norm_final)
    vsd = dict(norm_in=v_norm_in, q_norm=v_q_norm, w_uq=v_w_uq, kv_norm=v_kv_norm, w_ukv=v_w_ukv, pool_w=v_pool_w,
               pool_scale=v_pool_scale, w_branch_attn=v_w_branch_attn, w_branch_pool=v_w_branch_pool, w_out=v_w_out,
               norm_final=v_norm_final)
    dls, nms, nvs, g_outs, loss = _adamw_many(
        [as_rows(n, ws[n]) for n in names], [gsd[n] for n in names], [as_rows(n, msd[n]) for n in names],
        [as_rows(n, vsd[n]) for n in names], (g_small_mid, _first_rows(SMALL_MID)["sq_err"]))

    grads = dict(zip(names, g_outs))
    grads["w_in"] = g_w_in
    delta_w = {n: d.reshape(ws[n].shape) for n, d in zip(names, dls)}
    new_m = {n: d.reshape(ws[n].shape) for n, d in zip(names, nms)}
    new_v = {n: d.reshape(ws[n].shape) for n, d in zip(names, nvs)}
    delta_w["w_in"], new_m["w_in"], new_v["w_in"] = dl_w_in, nm_w_in, nv_w_in
    ws["w_in"] = w_in

    order = ["norm_in", "w_in", "q_norm", "w_uq", "kv_norm", "w_ukv", "pool_w", "pool_scale", "w_branch_attn",
             "w_branch_pool", "w_out", "norm_final"]
    return (loss.reshape(()), grad_x.reshape(x.shape),
            *[grads[n].reshape(ws[n].shape) for n in order],
            *[delta_w[n] for n in order], *[new_m[n] for n in order], *[new_v[n] for n in order])
```

```python
import functools

import jax
import jax.numpy as jnp
import numpy as np
from jax import lax
from jax.experimental import pallas as pl
from jax.experimental.pallas import tpu as pltpu

F32 = jnp.float32
BF16 = jnp.bfloat16
MESH = pl.DeviceIdType.MESH

D_MODEL = 1024
CHUNK = 64
MLA_HEADS = 8
QK_NOPE_DIM = 64
QK_ROPE_DIM = 32
V_HEAD_DIM = 64
Q_LORA_RANK = 384
KV_LORA_RANK = 256
MLA_WIDTH = MLA_HEADS * V_HEAD_DIM
ROPE_THETA = 10000.0
POOL_WINDOWS = (2, 4, 8, 16)
POOL_WIDTH = 512
POOL_GROUP_DIM = 128
BRANCH_COLS = D_MODEL // 4
FWD_HEADS = 8
BWD_HEADS = 4
POOL_HALO = 16
EPS = 1e-6
IN_TOTAL = 4256
HEAD_PAD = 128
ATT_SCALE = (QK_NOPE_DIM + QK_ROPE_DIM) ** -0.5
ATT_SCALE_LOG2E = ATT_SCALE * 1.4426950408889634

ADAM_LR = 0.001
ADAM_B1 = 0.9
ADAM_B2 = 0.999
ADAM_EPS = 1e-08
ADAM_WD = 0.01
ADAM_STEP = 10

N_CHIPS = 4
N_DEV = 8
LANES = 128
VMEM_LIMIT = 60 * 1024 * 1024

IN_SEGMENTS = ((384, 384), (256, 256), (32, HEAD_PAD), (512, 512), (512, 512), (512, 512), (2048, 2048))
SHARD_COLS = IN_TOTAL // N_CHIPS
ZQ_COLS = slice(0, 384)
ZKV_COLS = slice(384, 640)
ZKR_TILE = slice(640, 768)


def _shard_pieces():
    bounds, off = [], 0
    for w, _ in IN_SEGMENTS:
        bounds.append((off, off + w))
        off += w
    out = []
    for j in range(N_CHIPS):
        lo, hi = SHARD_COLS * j, SHARD_COLS * (j + 1)
        out.append([(i, max(lo, a) - a, min(hi, b) - a, max(lo, a) - lo)
                    for i, (a, b) in enumerate(bounds) if max(lo, a) < min(hi, b)])
    return out


SHARD_PIECES = _shard_pieces()


def _segment(z_blocks, seg):
    parts = [z_blocks[j][:, col:col + hi - lo]
             for j, pieces in enumerate(SHARD_PIECES) for sg, lo, hi, col in pieces if sg == seg]
    return parts[0] if len(parts) == 1 else jnp.concatenate(parts, axis=1)

COMM_PARAMS = (
    ("w_in", SHARD_COLS, D_MODEL, 1, 512),
    ("w_uq", 96, 768, 0, 48),
    ("w_ukv", 64, 1024, 0, 32),
    ("w_branch_attn", 512, 256, 0, 256),
    ("w_branch_pool", 512, 256, 0, 256),
    ("w_out", 256, 1024, 0, 128),
)

SMALL_MID = (
    ("pool_w", (4, 128, 128)),
    ("norm_final", (1024,)),
    ("sq_err", (8, 128)),
    ("pool_scale", (512,)),
)
SMALL_LATE = (
    ("norm_in", (1024,)),
    ("q_norm", (384,)),
    ("kv_norm", (256,)),
)


def _small_rows(shapes):
    return -(-sum(int(np.prod(s)) for _, s in shapes) // (LANES * 8)) * 8


def _first_rows(shapes):
    out, off = {}, 0
    for name, shp in shapes:
        out[name], rem = divmod(off, LANES)
        assert rem == 0, name
        off += int(np.prod(shp))
    return out


def _pack_into(dst_ref, shapes, values):
    first = _first_rows(shapes)
    for name, shp in shapes:
        v, row = values[name], first[name]
        if v.ndim == 3:
            for g in range(v.shape[0]):
                dst_ref[row + g * v.shape[1]:row + (g + 1) * v.shape[1], :] = v[g]
        elif v.shape[0] == 1 and v.shape[1] > LANES:
            for j in range(v.shape[1] // LANES):
                dst_ref[row + j:row + j + 1, :] = v[:, j * LANES:(j + 1) * LANES]
        else:
            dst_ref[row:row + v.shape[0], :] = v
    used = sum(int(np.prod(shp)) for _, shp in shapes) // LANES
    if used < dst_ref.shape[0]:
        dst_ref[used:, :] = jnp.zeros((dst_ref.shape[0] - used, LANES), dst_ref.dtype)


def _as_one_row(g):
    return jnp.concatenate([g[j:j + 1] for j in range(g.shape[0])], axis=1)


def _dot(a, b):
    return jnp.dot(a, b, preferred_element_type=F32)


def _dot_nt(a, b):
    return lax.dot_general(a, b, (((1,), (1,)), ((), ())), preferred_element_type=F32)


def _dot_tn(a, b):
    return lax.dot_general(a, b, (((0,), (0,)), ((), ())), preferred_element_type=F32)


def _sigmoid(x):
    return 1.0 / (1.0 + jnp.exp(-x))


def _colsum(x):
    return jnp.sum(x, axis=0, keepdims=True)


def _rms_fwd(x, g):
    r = lax.rsqrt(jnp.mean(x * x, axis=-1, keepdims=True) + EPS)
    xhat = x * r
    return xhat * g, xhat, r


def _rms_bwd(dy, xhat, r, g):
    dxhat = dy * g
    return r * (dxhat - xhat * jnp.mean(dxhat * xhat, axis=-1, keepdims=True))


def _rope(v, c, sa, sb):
    return v * c + pltpu.roll(v, 112, 1) * sa + pltpu.roll(v, 16, 1) * sb


def _unrope(d, c, sa, sb):
    return d * c + pltpu.roll(d * sa, 16, 1) + pltpu.roll(d * sb, 112, 1)


def _row_spec(tm, n):
    return pl.BlockSpec((tm, n), lambda i: (i, 0))


def _full_spec(shape):
    nd = len(shape)
    return pl.BlockSpec(shape, lambda i: (0,) * nd)


def _tiles(s):
    t_att = 512 if s >= 2048 else 128
    t_row = 256 if s >= 1024 else 128
    return t_att, t_row


def _inproj_fwd(order, x, norm_in, w_in_shard, up_shards, tm):
    s = x.shape[0]
    n_tiles = s // tm
    gat = _Gather(COMM_PARAMS[:3])
    n_w = len(gat.params)
    arrival = (1, 2, 0)
    n_up = len(up_shards)

    def body(order_ref, x_ref, g_ref, w_in_loc, *rest):
        up_refs, (hn_ref, z_ref), w_all = rest[:n_up], rest[n_up:n_up + 2], rest[n_up + 2:n_up + 2 + n_w]
        rest = rest[n_up + 2 + n_w:]
        (w_vmem, hn_all, w_sem), up_flat = rest[:3], rest[3:3 + n_up]
        gat.bind((w_in_loc,) + tuple(up_flat), w_all, rest[3 + n_up:])
        ph, i = pl.program_id(0), pl.program_id(1)

        @pl.when(jnp.logical_and(ph == 0, i == 0))
        def _():
            for src, dst in zip(up_refs, up_flat):
                dst[...] = jnp.concatenate([src[:, h, :] for h in range(MLA_HEADS)], axis=1).astype(BF16)
            gat.start()

        def fetch(phase):
            src = w_in_loc if phase == 0 else w_all[0].at[order_ref[phase]]
            return pltpu.make_async_copy(src, w_vmem.at[phase % 2], w_sem.at[phase % 2])

        def landed(f):
            gat.relay_one(0, arrival[f])
            gat.await_one(0, arrival[f])

        @pl.when(jnp.logical_and(ph == 0, i == 0))
        def _():
            fetch(0).start()
            fetch(0).wait()

        @pl.when(jnp.logical_and(ph == 1, i == 0))
        def _():
            landed(0)
            fetch(1).start()
            fetch(1).wait()

        for f in (1, 2):
            @pl.when(jnp.logical_and(ph == f, i == n_tiles - 1))
            def _(f=f):
                landed(f)
                fetch(f + 1).start()

            @pl.when(jnp.logical_and(ph == f + 1, i == 0))
            def _(f=f):
                fetch(f + 1).wait()

        rows = pl.ds(pl.multiple_of(i * tm, tm), tm)

        @pl.when(ph == 0)
        def _():
            hn, _, _ = _rms_fwd(x_ref[...], g_ref[...])
            hn = hn.astype(BF16)
            hn_ref[...] = hn
            hn_all[rows, :] = hn

        z_ref[0] = _dot_nt(hn_all[rows, :], w_vmem[ph % 2])

        @pl.when(jnp.logical_and(ph == N_CHIPS - 1, i == n_tiles - 1))
        def _():
            for p in range(1, n_w):
                for j in range(3):
                    gat.relay_one(p, j)
            for p in range(1, n_w):
                for j in range(3):
                    gat.await_one(p, j)
            gat.wait_sends()

    def tile_in_phase0(ph, i, order):
        return (jnp.where(ph == 0, i, n_tiles - 1), 0)

    any_spec = pl.BlockSpec(memory_space=pl.ANY)
    grid_spec = pltpu.PrefetchScalarGridSpec(
        num_scalar_prefetch=1,
        grid=(N_CHIPS, n_tiles),
        in_specs=[pl.BlockSpec((tm, D_MODEL), tile_in_phase0),
                  pl.BlockSpec((1, D_MODEL), lambda ph, i, order: (0, 0)), any_spec]
        + [pl.BlockSpec(a.shape, lambda ph, i, order: (0, 0, 0)) for a in up_shards],
        out_specs=[pl.BlockSpec((tm, D_MODEL), tile_in_phase0),
                   pl.BlockSpec((1, tm, SHARD_COLS), lambda ph, i, order: (order[ph], i, 0))] + [any_spec] * n_w,
        scratch_shapes=[pltpu.VMEM((2, SHARD_COLS, D_MODEL), BF16), pltpu.VMEM((s, D_MODEL), BF16),
                        pltpu.SemaphoreType.DMA((2,))]
        + [pltpu.VMEM((r, cc), BF16) for _, r, cc, _, _ in gat.params[1:]] + gat.scratch,
    )
    out = pl.pallas_call(
        body,
        name="inproj_fwd",
        grid_spec=grid_spec,
        out_shape=[jax.ShapeDtypeStruct((s, D_MODEL), BF16), jax.ShapeDtypeStruct((N_CHIPS, s, SHARD_COLS), F32)]
        + gat.out_shape,
        compiler_params=pltpu.CompilerParams(dimension_semantics=("arbitrary", "arbitrary"),
                                             vmem_limit_bytes=VMEM_LIMIT),
    )(order, x, norm_in, w_in_shard, *up_shards)
    return out[0], out[1], out[2:]


def _qkv_fwd(z_sh, q_norm, kv_norm, w_uq_all, w_ukv_all, rc, rsa, rsb, to_bf16, tm):
    s = z_sh.shape[1]
    n_cast = len(to_bf16)
    hw = MLA_HEADS * HEAD_PAD
    qk = QK_NOPE_DIM + QK_ROPE_DIM

    def body(z_ref, gq_ref, gkv_ref, uq_ref, ukv_ref, c_ref, sa_ref, sb_ref, *rest):
        f32_refs, rest = rest[:n_cast], rest[n_cast:]
        (q_ref, k_ref, v_ref, qt_ref, vt_ref, wuq_ref, wk_ref, wv_ref), bf_refs = rest[:8], rest[8:]

        @pl.when(pl.program_id(0) == 0)
        def _():
            for src, dst in zip(f32_refs, bf_refs):
                dst[...] = src[...].astype(BF16)
            w_q = jnp.concatenate([uq_ref[j] for j in range(N_CHIPS)], axis=0).astype(F32)
            gap = jnp.zeros((Q_LORA_RANK, HEAD_PAD - qk), F32)
            wuq_ref[...] = jnp.concatenate(
                [part for h in range(MLA_HEADS) for part in (w_q[:, h * qk:(h + 1) * qk], gap)], axis=1).astype(BF16)
            w_kv = jnp.concatenate([ukv_ref[j] for j in range(N_CHIPS)], axis=0).astype(F32)
            lane = lax.broadcasted_iota(jnp.int32, w_kv.shape, 1)
            wk_ref[...] = jnp.where(lane % HEAD_PAD < QK_NOPE_DIM, w_kv, 0.0).astype(BF16)
            wv_ref[...] = jnp.concatenate(
                [w_kv[:, h * HEAD_PAD + QK_NOPE_DIM:(h + 1) * HEAD_PAD] for h in range(MLA_HEADS)],
                axis=1).astype(BF16)

        c, sa, sb = c_ref[...], sa_ref[...], sb_ref[...]
        z0 = z_ref[0]
        cq, _, _ = _rms_fwd(z0[:, ZQ_COLS], _as_one_row(gq_ref[...]))
        qf = _dot(cq.astype(BF16), wuq_ref[...])
        ckv, _, _ = _rms_fwd(z0[:, ZKV_COLS], gkv_ref[...])
        ckv = ckv.astype(BF16)
        kn = _dot(ckv, wk_ref[...])
        lane = lax.broadcasted_iota(jnp.int32, (tm, HEAD_PAD), 1)
        zkr = jnp.where(lane < QK_ROPE_DIM, z0[:, ZKR_TILE], 0.0)
        kr = _rope(pltpu.roll(zkr, 64, 1), c, sa, sb)
        for h in range(MLA_HEADS):
            cols = slice(h * HEAD_PAD, (h + 1) * HEAD_PAD)
            qh = _rope(qf[:, cols], c, sa, sb)
            q_ref[:, cols] = qh.astype(BF16)
            qt_ref[cols, :] = qh.T.astype(BF16)
            k_ref[:, cols] = (kn[:, cols] + kr).astype(BF16)
        vf = _dot(ckv, wv_ref[...])
        v_ref[...] = vf.astype(BF16)
        vt_ref[...] = vf.T.astype(BF16)

    return pl.pallas_call(
        body,
        name="qkv_fwd",
        grid=(s // tm,),
        in_specs=[
            pl.BlockSpec((1, tm, SHARD_COLS), lambda i: (0, i, 0)),
            _full_spec(q_norm.shape), _full_spec((1, KV_LORA_RANK)),
            _full_spec(w_uq_all.shape), _full_spec(w_ukv_all.shape),
            _row_spec(tm, HEAD_PAD), _row_spec(tm, HEAD_PAD), _row_spec(tm, HEAD_PAD),
        ] + [_full_spec(a.shape) for a in to_bf16],
        out_specs=[_row_spec(tm, hw), _row_spec(tm, hw), _row_spec(tm, MLA_WIDTH),
                   pl.BlockSpec((hw, tm), lambda i: (0, i)), pl.BlockSpec((MLA_WIDTH, tm), lambda i: (0, i)),
                   _full_spec((Q_LORA_RANK, hw)), _full_spec((KV_LORA_RANK, hw)), _full_spec((KV_LORA_RANK, MLA_WIDTH))
                   ] + [_full_spec(a.shape) for a in to_bf16],
        out_shape=[jax.ShapeDtypeStruct((s, hw), BF16), jax.ShapeDtypeStruct((s, hw), BF16),
                   jax.ShapeDtypeStruct((s, MLA_WIDTH), BF16),
                   jax.ShapeDtypeStruct((hw, s), BF16), jax.ShapeDtypeStruct((MLA_WIDTH, s), BF16),
                   jax.ShapeDtypeStruct((Q_LORA_RANK, hw), BF16), jax.ShapeDtypeStruct((KV_LORA_RANK, hw), BF16),
                   jax.ShapeDtypeStruct((KV_LORA_RANK, MLA_WIDTH), BF16)
                   ] + [jax.ShapeDtypeStruct(a.shape, BF16) for a in to_bf16],
        compiler_params=pltpu.CompilerParams(dimension_semantics=("arbitrary",), vmem_limit_bytes=VMEM_LIMIT),
    )(z_sh, q_norm, kv_norm, w_uq_all, w_ukv_all, rc, rsa, rsb, *to_bf16)


def _chunk_mask(t, keys_on_rows):
    rows = lax.broadcasted_iota(jnp.int32, (t, t), 0) // CHUNK
    cols = lax.broadcasted_iota(jnp.int32, (t, t), 1) // CHUNK
    return rows <= cols if keys_on_rows else cols <= rows


def _attn_fwd(q_t, k, v_t, late_shards, t):
    s = k.shape[0]
    groups = MLA_HEADS // FWD_HEADS
    n_q = s // t
    gat = _Gather(COMM_PARAMS[3:])
    n_w = len(gat.params)

    def body(qt_ref, k_ref, k2_ref, vt_ref, *rest):
        w_in, (o_ref, lse_ref), w_out = rest[:n_w], rest[n_w:n_w + 2], rest[n_w + 2:2 * n_w + 2]
        gat.bind(w_in, w_out, rest[2 * n_w + 2:])
        i = pl.program_id(1)
        step_no = pl.program_id(0) * n_q + i
        pl.when(step_no == 0)(gat.start)
        pl.when(step_no == groups * n_q // 2)(gat.relay)
        mask = _chunk_mask(t, True)
        qcs = [slice(hh * HEAD_PAD, (hh + 1) * HEAD_PAD) for hh in range(FWD_HEADS)]
        vcs = [slice(hh * V_HEAD_DIM, (hh + 1) * V_HEAD_DIM) for hh in range(FWD_HEADS)]
        qts = [qt_ref[qc, :] for qc in qcs]

        def step(j, carry, masked):
            keys = pl.ds(pl.multiple_of(j * t, t), t)
            out = []
            for hh in range(FWD_HEADS):
                m, l, acc = carry[hh]
                sc = _dot(k_ref[keys, qcs[hh]], qts[hh])
                if masked:
                    sc = jnp.where(mask, sc, -jnp.inf)
                m_new = jnp.maximum(m, jnp.max(sc, axis=0, keepdims=True))
                alpha = jnp.exp2((m - m_new) * ATT_SCALE_LOG2E)
                p = jnp.exp2((_dot(k2_ref[keys, qcs[hh]], qts[hh]) - m_new) * ATT_SCALE_LOG2E)
                if masked:
                    p = jnp.where(mask, p, 0.0)
                l = alpha * l + jnp.sum(p, axis=0, keepdims=True)
                acc = alpha * acc + _dot(vt_ref[vcs[hh], keys], p.astype(BF16))
                out.append((m_new, l, acc))
            return tuple(out)

        one = (jnp.full((1, t), -jnp.inf, F32), jnp.zeros((1, t), F32), jnp.zeros((V_HEAD_DIM, t), F32))
        carry = lax.fori_loop(0, i, functools.partial(step, masked=False), (one,) * FWD_HEADS)
        carry = step(i, carry, True)
        o_ref[...] = jnp.concatenate([carry[hh][2] / carry[hh][1] for hh in range(FWD_HEADS)], axis=0).T
        for hh in range(FWD_HEADS):
            m, l, _ = carry[hh]
            lse_ref[:, qcs[hh]] = jnp.broadcast_to(m * ATT_SCALE_LOG2E + jnp.log2(l), (HEAD_PAD, t)).T
        pl.when(step_no == groups * n_q - 1)(gat.finish)

    any_spec = pl.BlockSpec(memory_space=pl.ANY)
    out = pl.pallas_call(
        body,
        name="attn_fwd",
        grid=(groups, n_q),
        in_specs=[
            pl.BlockSpec((FWD_HEADS * HEAD_PAD, t), lambda p, i: (p, i)),
            pl.BlockSpec((s, FWD_HEADS * HEAD_PAD), lambda p, i: (0, p), pipeline_mode=pl.Buffered(1)),
            pl.BlockSpec((s, FWD_HEADS * HEAD_PAD), lambda p, i: (0, p), pipeline_mode=pl.Buffered(1)),
            pl.BlockSpec((FWD_HEADS * V_HEAD_DIM, s), lambda p, i: (p, 0), pipeline_mode=pl.Buffered(1)),
        ] + [any_spec] * n_w,
        out_specs=[
            pl.BlockSpec((t, FWD_HEADS * V_HEAD_DIM), lambda p, i: (i, p)),
            pl.BlockSpec((t, FWD_HEADS * HEAD_PAD), lambda p, i: (i, p)),
        ] + [any_spec] * n_w,
        out_shape=[jax.ShapeDtypeStruct((s, MLA_WIDTH), F32), jax.ShapeDtypeStruct((s, MLA_HEADS * HEAD_PAD), F32)]
        + gat.out_shape,
        scratch_shapes=gat.scratch,
        compiler_params=pltpu.CompilerParams(dimension_semantics=("arbitrary", "arbitrary"),
                                             vmem_limit_bytes=VMEM_LIMIT),
    )(q_t, k, k, v_t, *late_shards)
    return out[0], out[1], out[2:]


def _mid(o, z_sh, x, target, pool_w, pool_scale, w_ba, w_bp, w_out, norm_final, tm):
    s = x.shape[0]
    n_tiles = s // tm
    halo_per_tile = tm // POOL_HALO
    small_rows = _small_rows(SMALL_MID)

    def body(o_ref, z0_ref, z1_ref, z1h_ref, z2_ref, z3_ref, x_ref, t_ref, pw_ref, ps_ref, wba_ref, wbp_ref,
             wout_ref, gf_ref,
             do_ref, dl_ref, dga_ref, dgp_ref, dgm_ref, ddc_ref, dh_ref,
             small_ref, dwout_out, dwba_out, dwbp_out,
             ubuf, dwout_ref, dwba_ref, dwbp_ref, loss_ref, dpw_ref, dps_ref, dgf_ref):
        i = pl.program_id(0)

        @pl.when(i == 0)
        def _():
            loss_ref[...] = jnp.zeros_like(loss_ref)
            dwout_ref[...] = jnp.zeros_like(dwout_ref)
            dwba_ref[...] = jnp.zeros_like(dwba_ref)
            dwbp_ref[...] = jnp.zeros_like(dwbp_ref)
            dpw_ref[...] = jnp.zeros_like(dpw_ref)
            dps_ref[...] = jnp.zeros_like(dps_ref)
            dgf_ref[...] = jnp.zeros_like(dgf_ref)

        zs = [z0_ref[0], z1_ref[0], z2_ref[0], z3_ref[0]]
        o = o_ref[...]
        ga = _segment(zs, 3)
        sga = _sigmoid(ga)
        silu_a = ga * sga
        y_attn = (o * silu_a).astype(BF16)

        ubuf[0:POOL_HALO, :] = jnp.where(i > 0, _segment([None, z1h_ref[0]], 4), 0.0)
        ubuf[POOL_HALO:, :] = _segment(zs, 4)
        row = lax.broadcasted_iota(jnp.int32, (tm, POOL_GROUP_DIM), 0) + i * tm
        ps = ps_ref[...]
        gp = _segment(zs, 5)
        sgp = _sigmoid(gp)
        silu_p = gp * sgp
        d_bf, dm, inv_cnt = [], [], []
        for g, w in enumerate(POOL_WINDOWS):
            cols = slice(g * POOL_GROUP_DIM, (g + 1) * POOL_GROUP_DIM)
            wsum = ubuf[POOL_HALO:, cols]
            for kk in range(1, w):
                wsum = wsum + ubuf[POOL_HALO - kk:POOL_HALO - kk + tm, cols]
            inv = 1.0 / jnp.minimum(row + 1, w).astype(F32)
            dg = (wsum * inv - ubuf[POOL_HALO:, cols]).astype(BF16)
            d_bf.append(dg)
            inv_cnt.append(inv)
            dm.append(_dot(dg, pw_ref[g]))
        dm = jnp.concatenate(dm, axis=1)
        yp = dm * ps
        y_pool = (yp * silu_p).astype(BF16)

        a = jnp.concatenate([_dot(y_attn, wba_ref[j]) for j in range(N_CHIPS)], axis=1)
        p = jnp.concatenate([_dot(y_pool, wbp_ref[j]) for j in range(N_CHIPS)], axis=1)
        gm = _segment(zs, 6)
        gate_a = _sigmoid(gm[:, :D_MODEL])
        gate_p = _sigmoid(gm[:, D_MODEL:])
        merged = (gate_a * a + gate_p * p).astype(BF16)
        h = x_ref[...] + _dot(merged, wout_ref[...])
        gf = gf_ref[...]
        y, xhat, r = _rms_fwd(h, gf)
        err = y - t_ref[...]
        e2 = err * err
        e2 = jnp.sum(e2.reshape(tm // 8, 8, D_MODEL), axis=0)
        acc = e2[:, 0:LANES]
        for cidx in range(1, D_MODEL // LANES):
            acc = acc + e2[:, cidx * LANES:(cidx + 1) * LANES]
        loss_ref[...] += acc

        dy = err * (1.0 / D_MODEL)
        dgf_ref[...] += _colsum(dy * xhat)
        dh = _rms_bwd(dy, xhat, r, gf)
        dh_ref[...] = dh
        dh_bf = dh.astype(BF16)
        dwout_ref[...] += _dot_tn(merged, dh_bf)
        dmerged = _dot_nt(dh_bf, wout_ref[...])
        da = (dmerged * gate_a).astype(BF16)
        dp = (dmerged * gate_p).astype(BF16)
        dgm_ref[:, :D_MODEL] = (dmerged * a * gate_a * (1.0 - gate_a)).astype(BF16)
        dgm_ref[:, D_MODEL:] = (dmerged * p * gate_p * (1.0 - gate_p)).astype(BF16)
        dy_attn = dy_pool = None
        for j in range(N_CHIPS):
            cols = slice(j * BRANCH_COLS, (j + 1) * BRANCH_COLS)
            dwba_ref[j] += _dot_tn(y_attn, da[:, cols])
            dwbp_ref[j] += _dot_tn(y_pool, dp[:, cols])
            pa = _dot_nt(da[:, cols], wba_ref[j])
            pp = _dot_nt(dp[:, cols], wbp_ref[j])
            dy_attn = pa if dy_attn is None else dy_attn + pa
            dy_pool = pp if dy_pool is None else dy_pool + pp

        do = dy_attn * silu_a
        do_ref[...] = do
        dga_ref[...] = (dy_attn * o * (sga * (1.0 + ga * (1.0 - sga)))).astype(BF16)
        doo = do * o
        for hd in range(MLA_HEADS):
            dl = jnp.sum(doo[:, hd * V_HEAD_DIM:(hd + 1) * V_HEAD_DIM], axis=1, keepdims=True)
            dl_ref[:, hd * HEAD_PAD:(hd + 1) * HEAD_PAD] = jnp.broadcast_to(dl, (tm, HEAD_PAD))

        dyp = dy_pool * silu_p
        dgp_ref[...] = (dy_pool * yp * (sgp * (1.0 + gp * (1.0 - sgp)))).astype(BF16)
        dps_ref[...] += _colsum(dyp * dm)
        dmm = (dyp * ps).astype(BF16)
        for g in range(len(POOL_WINDOWS)):
            cols = slice(g * POOL_GROUP_DIM, (g + 1) * POOL_GROUP_DIM)
            dpw_ref[g] += _dot_tn(d_bf[g], dmm[:, cols])
            ddc_ref[:, cols] = _dot_nt(dmm[:, cols], pw_ref[g]) * inv_cnt[g]

        @pl.when(i == n_tiles - 1)
        def _():
            dwout_out[...] = dwout_ref[...].astype(BF16)
            dwba_out[...] = dwba_ref[...].astype(BF16)
            dwbp_out[...] = dwbp_ref[...].astype(BF16)
            _pack_into(small_ref, SMALL_MID, dict(pool_w=dpw_ref[...], norm_final=dgf_ref[...], sq_err=loss_ref[...],
                                                  pool_scale=dps_ref[...]))

    row_in = lambda n: _row_spec(tm, n)
    in_specs = [
        row_in(MLA_WIDTH),
        pl.BlockSpec((1, tm, SHARD_COLS), lambda i: (0, i, 0)), pl.BlockSpec((1, tm, SHARD_COLS), lambda i: (1, i, 0)),
        pl.BlockSpec((1, POOL_HALO, SHARD_COLS), lambda i: (1, jnp.maximum(i * halo_per_tile - 1, 0), 0)),
        pl.BlockSpec((1, tm, SHARD_COLS), lambda i: (2, i, 0)), pl.BlockSpec((1, tm, SHARD_COLS), lambda i: (3, i, 0)),
        row_in(D_MODEL), row_in(D_MODEL),
        _full_spec((4, POOL_GROUP_DIM, POOL_GROUP_DIM)), _full_spec((1, POOL_WIDTH)),
        _full_spec((N_CHIPS, MLA_WIDTH, BRANCH_COLS)), _full_spec((N_CHIPS, POOL_WIDTH, BRANCH_COLS)),
        _full_spec((D_MODEL, D_MODEL)), _full_spec((1, D_MODEL)),
    ]
    out_shape = [
        jax.ShapeDtypeStruct((s, MLA_WIDTH), F32),
        jax.ShapeDtypeStruct((s, MLA_HEADS * HEAD_PAD), F32),
        jax.ShapeDtypeStruct((s, MLA_WIDTH), BF16),
        jax.ShapeDtypeStruct((s, POOL_WIDTH), BF16),
        jax.ShapeDtypeStruct((s, 2 * D_MODEL), BF16),
        jax.ShapeDtypeStruct((s, POOL_WIDTH), F32),
        jax.ShapeDtypeStruct((s, D_MODEL), F32),
        jax.ShapeDtypeStruct((small_rows, LANES), F32),
        jax.ShapeDtypeStruct((D_MODEL, D_MODEL), BF16),
        jax.ShapeDtypeStruct((N_CHIPS, MLA_WIDTH, BRANCH_COLS), BF16),
        jax.ShapeDtypeStruct((N_CHIPS, POOL_WIDTH, BRANCH_COLS), BF16),
    ]
    out_specs = [
        row_in(MLA_WIDTH), row_in(MLA_HEADS * HEAD_PAD), row_in(MLA_WIDTH), row_in(POOL_WIDTH),
        row_in(2 * D_MODEL), row_in(POOL_WIDTH), row_in(D_MODEL),
        _full_spec((small_rows, LANES)), _full_spec((D_MODEL, D_MODEL)),
        _full_spec((N_CHIPS, MLA_WIDTH, BRANCH_COLS)), _full_spec((N_CHIPS, POOL_WIDTH, BRANCH_COLS)),
    ]
    return pl.pallas_call(
        body,
        name="mid",
        grid=(n_tiles,),
        in_specs=in_specs,
        out_specs=out_specs,
        out_shape=out_shape,
        scratch_shapes=[
            pltpu.VMEM((tm + POOL_HALO, POOL_WIDTH), F32),
            pltpu.VMEM((D_MODEL, D_MODEL), F32),
            pltpu.VMEM((N_CHIPS, MLA_WIDTH, BRANCH_COLS), F32),
            pltpu.VMEM((N_CHIPS, POOL_WIDTH, BRANCH_COLS), F32),
            pltpu.VMEM((8, LANES), F32),
            pltpu.VMEM((4, POOL_GROUP_DIM, POOL_GROUP_DIM), F32),
            pltpu.VMEM((1, POOL_WIDTH), F32),
            pltpu.VMEM((1, D_MODEL), F32),
        ],
        compiler_params=pltpu.CompilerParams(dimension_semantics=("arbitrary",), vmem_limit_bytes=VMEM_LIMIT),
    )(o, z_sh, z_sh, z_sh, z_sh, z_sh, x, target, pool_w, pool_scale, w_ba, w_bp, w_out, norm_final)


def _attn_bwd(q, q_t, k, v, do, lse, delta, late_grads, gs_mid, t):
    s = q.shape[0]
    groups = MLA_HEADS // BWD_HEADS
    n_q = s // t
    red = _Reduce(COMM_PARAMS[3:])
    n_w = len(red.params)
    small = _SmallSum(gs_mid.shape[0])
    n_red = len(red.scratch)

    def body(q_ref, qt_ref, do_ref, lse_ref, dl_ref, k_ref, v_ref, *rest):
        g_in, gs_ref = rest[:n_w], rest[n_w]
        (dq_ref, dk_ref, dv_ref), g_out, gsum_ref = rest[n_w + 1:n_w + 4], rest[n_w + 4:2 * n_w + 4], rest[2 * n_w + 4]
        scratch = rest[2 * n_w + 5:]
        red.bind(g_in, g_out, scratch[:n_red])
        small.bind(gs_ref, gsum_ref, scratch[n_red:])
        i = pl.program_id(1)
        step_no = pl.program_id(0) * n_q + i

        @pl.when(step_no == 0)
        def _():
            red.start()
            small.start()

        pl.when(step_no == groups * n_q // 2)(red.exchange)

        @pl.when(i == 0)
        def _():
            dk_ref[...] = jnp.zeros_like(dk_ref)
            dv_ref[...] = jnp.zeros_like(dv_ref)

        mask = _chunk_mask(t, False)
        qcs = [slice(hh * HEAD_PAD, (hh + 1) * HEAD_PAD) for hh in range(BWD_HEADS)]
        vcs = [slice(hh * V_HEAD_DIM, (hh + 1) * V_HEAD_DIM) for hh in range(BWD_HEADS)]
        qhs = [q_ref[:, qc] for qc in qcs]
        qts = [qt_ref[qc, :] for qc in qcs]
        dohs = [do_ref[:, vc].astype(BF16) for vc in vcs]
        do_t = do_ref[...].T.astype(BF16)
        dots = [do_t[vc, :] for vc in vcs]
        lses = [jnp.tile(lse_ref[:, qc], (1, t // HEAD_PAD)) for qc in qcs]
        dls = [jnp.tile(dl_ref[:, qc], (1, t // HEAD_PAD)) for qc in qcs]

        def step(j, dqs, masked):
            keys = pl.ds(pl.multiple_of(j * t, t), t)
            out = []
            for hh in range(BWD_HEADS):
                kj = k_ref[keys, qcs[hh]]
                vj = v_ref[keys, vcs[hh]]
                p = jnp.exp2(_dot_nt(qhs[hh], kj) * ATT_SCALE_LOG2E - lses[hh])
                if masked:
                    p = jnp.where(mask, p, 0.0)
                ds = (p * (_dot_nt(dohs[hh], vj) - dls[hh])).astype(BF16)
                dv_ref[vcs[hh], keys] += _dot(dots[hh], p.astype(BF16))
                dk_ref[qcs[hh], keys] += _dot(qts[hh], ds) * ATT_SCALE
                out.append(dqs[hh] + _dot(ds, kj))
            return tuple(out)

        zero = jnp.zeros((t, HEAD_PAD), F32)
        dqs = lax.fori_loop(0, i, functools.partial(step, masked=False), (zero,) * BWD_HEADS)
        dqs = step(i, dqs, True)
        for hh in range(BWD_HEADS):
            dq_ref[:, qcs[hh]] = dqs[hh] * ATT_SCALE

        @pl.when(step_no == groups * n_q - 1)
        def _():
            red.finish()
            small.finish()

    hw = MLA_HEADS * HEAD_PAD
    any_spec = pl.BlockSpec(memory_space=pl.ANY)
    out = pl.pallas_call(
        body,
        name="attn_bwd",
        grid=(groups, n_q),
        in_specs=[
            pl.BlockSpec((t, BWD_HEADS * HEAD_PAD), lambda p, i: (i, p)),
            pl.BlockSpec((BWD_HEADS * HEAD_PAD, t), lambda p, i: (p, i)),
            pl.BlockSpec((t, BWD_HEADS * V_HEAD_DIM), lambda p, i: (i, p)),
            pl.BlockSpec((t, BWD_HEADS * HEAD_PAD), lambda p, i: (i, p)),
            pl.BlockSpec((t, BWD_HEADS * HEAD_PAD), lambda p, i: (i, p)),
            pl.BlockSpec((s, BWD_HEADS * HEAD_PAD), lambda p, i: (0, p), pipeline_mode=pl.Buffered(1)),
            pl.BlockSpec((s, BWD_HEADS * V_HEAD_DIM), lambda p, i: (0, p), pipeline_mode=pl.Buffered(1)),
        ] + [any_spec] * n_w + [pl.BlockSpec(small.spec_shape, lambda p, i: (0, 0))],
        out_specs=[
            pl.BlockSpec((t, BWD_HEADS * HEAD_PAD), lambda p, i: (i, p)),
            pl.BlockSpec((BWD_HEADS * HEAD_PAD, s), lambda p, i: (p, 0)),
            pl.BlockSpec((BWD_HEADS * V_HEAD_DIM, s), lambda p, i: (p, 0)),
        ] + [any_spec] * n_w + [pl.BlockSpec(small.spec_shape, lambda p, i: (0, 0))],
        out_shape=[jax.ShapeDtypeStruct((s, hw), F32), jax.ShapeDtypeStruct((hw, s), F32),
                   jax.ShapeDtypeStruct((MLA_WIDTH, s), F32)] + red.out_shape + [small.out_shape],
        scratch_shapes=red.scratch + small.scratch,
        compiler_params=pltpu.CompilerParams(dimension_semantics=("arbitrary", "arbitrary"),
                                             vmem_limit_bytes=VMEM_LIMIT),
    )(q, q_t, do, lse, delta, k, v, *late_grads, gs_mid)
    return out[0], out[1], out[2], out[3:3 + n_w], out[3 + n_w]


def _qkv_bwd(dq, dk_t, dv_t, z_sh, q_norm, kv_norm, wuq_p, wk_p, wv, rc, rsa, rsb, tm):
    s = z_sh.shape[1]
    hw = MLA_HEADS * HEAD_PAD
    n_tiles = s // tm
    uq_shape, ukv_shape = (N_CHIPS,) + COMM_PARAMS[1][1:3], (N_CHIPS,) + COMM_PARAMS[2][1:3]

    def body(dq_ref, dk_ref, dv_ref, z_ref, gq_ref, gkv_ref, wuq_ref, wk_ref, wv_ref,
             c_ref, sa_ref, sb_ref,
             dzq_ref, dzkv_ref, dzkr_ref, duq_ref, dukv_ref, dgq_ref, dgkv_ref, dwuq_ref, dwk_ref, dwv_ref):
        i = pl.program_id(0)

        @pl.when(i == 0)
        def _():
            dwuq_ref[...] = jnp.zeros_like(dwuq_ref)
            dwk_ref[...] = jnp.zeros_like(dwk_ref)
            dwv_ref[...] = jnp.zeros_like(dwv_ref)
            dgq_ref[...] = jnp.zeros_like(dgq_ref)
            dgkv_ref[...] = jnp.zeros_like(dgkv_ref)

        c, sa, sb = c_ref[...], sa_ref[...], sb_ref[...]
        gq, gkv = _as_one_row(gq_ref[...]), gkv_ref[...]

        z0 = z_ref[0]
        cq, xq, rq = _rms_fwd(z0[:, ZQ_COLS], gq)
        dqp = jnp.concatenate(
            [_unrope(dq_ref[:, h * HEAD_PAD:(h + 1) * HEAD_PAD], c, sa, sb) for h in range(MLA_HEADS)],
            axis=1).astype(BF16)
        dwuq_ref[...] += _dot_tn(cq.astype(BF16), dqp)
        dcq = _dot_nt(dqp, wuq_ref[...])
        dgq_ref[...] += _colsum(dcq * xq)
        dzq_ref[...] = _rms_bwd(dcq, xq, rq, gq).astype(BF16)

        ckv, xkv, rkv = _rms_fwd(z0[:, ZKV_COLS], gkv)
        ckv = ckv.astype(BF16)
        dkf = dk_ref[...].T
        dk_bf = dkf.astype(BF16)
        dv_bf = dv_ref[...].T.astype(BF16)
        dwk_ref[...] += _dot_tn(ckv, dk_bf)
        dwv_ref[...] += _dot_tn(ckv, dv_bf)
        dckv = _dot_nt(dk_bf, wk_ref[...]) + _dot_nt(dv_bf, wv_ref[...])
        dgkv_ref[...] += _colsum(dckv * xkv)
        dzkv_ref[...] = _rms_bwd(dckv, xkv, rkv, gkv).astype(BF16)

        dkr = dkf[:, 0:HEAD_PAD]
        for h in range(1, MLA_HEADS):
            dkr = dkr + dkf[:, h * HEAD_PAD:(h + 1) * HEAD_PAD]
        dkr = pltpu.roll(_unrope(dkr, c, sa, sb), 64, 1)
        lane = lax.broadcasted_iota(jnp.int32, (tm, HEAD_PAD), 1)
        dzkr_ref[...] = jnp.where(lane < QK_ROPE_DIM, dkr, 0.0).astype(BF16)

        @pl.when(i == n_tiles - 1)
        def _():
            qk = QK_NOPE_DIM + QK_ROPE_DIM
            d_uq = jnp.concatenate([dwuq_ref[:, h * HEAD_PAD:h * HEAD_PAD + qk] for h in range(MLA_HEADS)],
                                   axis=1).astype(BF16)
            d_ukv = jnp.concatenate(
                [part for h in range(MLA_HEADS)
                 for part in (dwk_ref[:, h * HEAD_PAD:h * HEAD_PAD + QK_NOPE_DIM],
                              dwv_ref[:, h * V_HEAD_DIM:(h + 1) * V_HEAD_DIM])], axis=1).astype(BF16)
            for j in range(N_CHIPS):
                duq_ref[j] = d_uq[j * uq_shape[1]:(j + 1) * uq_shape[1]]
                dukv_ref[j] = d_ukv[j * ukv_shape[1]:(j + 1) * ukv_shape[1]]

    return pl.pallas_call(
        body,
        name="qkv_bwd",
        grid=(s // tm,),
        in_specs=[
            _row_spec(tm, hw), pl.BlockSpec((hw, tm), lambda i: (0, i)), pl.BlockSpec((MLA_WIDTH, tm), lambda i: (0, i)),
            pl.BlockSpec((1, tm, SHARD_COLS), lambda i: (0, i, 0)),
            _full_spec(q_norm.shape), _full_spec((1, KV_LORA_RANK)),
            _full_spec((Q_LORA_RANK, hw)), _full_spec((KV_LORA_RANK, hw)), _full_spec((KV_LORA_RANK, MLA_WIDTH)),
            _row_spec(tm, HEAD_PAD), _row_spec(tm, HEAD_PAD), _row_spec(tm, HEAD_PAD),
        ],
        out_specs=[
            _row_spec(tm, Q_LORA_RANK), _row_spec(tm, KV_LORA_RANK), _row_spec(tm, HEAD_PAD),
            _full_spec(uq_shape), _full_spec(ukv_shape),
            _full_spec((1, Q_LORA_RANK)), _full_spec((1, KV_LORA_RANK)),
        ],
        out_shape=[
            jax.ShapeDtypeStruct((s, Q_LORA_RANK), BF16), jax.ShapeDtypeStruct((s, KV_LORA_RANK), BF16),
            jax.ShapeDtypeStruct((s, HEAD_PAD), BF16),
            jax.ShapeDtypeStruct(uq_shape, BF16), jax.ShapeDtypeStruct(ukv_shape, BF16),
            jax.ShapeDtypeStruct((1, Q_LORA_RANK), F32), jax.ShapeDtypeStruct((1, KV_LORA_RANK), F32),
        ],
        scratch_shapes=[pltpu.VMEM((Q_LORA_RANK, hw), F32), pltpu.VMEM((KV_LORA_RANK, hw), F32),
                        pltpu.VMEM((KV_LORA_RANK, MLA_WIDTH), F32)],
        compiler_params=pltpu.CompilerParams(dimension_semantics=("arbitrary",), vmem_limit_bytes=VMEM_LIMIT),
    )(dq, dk_t, dv_t, z_sh, q_norm, kv_norm, wuq_p, wk_p, wv, rc, rsa, rsb)


def _inproj_bwd_x(dzq, dzkv, dzkr, dgattn, ddc, dgpool, dgmerge, x, dh, norm_in, d_q_norm, d_kv_norm, w_in_t, tm):
    s = x.shape[0]
    n_tiles = s // tm
    halo_per_tile = tm // POOL_HALO
    n_halo = s // POOL_HALO
    u_seg = 4
    small_rows = _small_rows(SMALL_LATE)

    def body(dzq_ref, dzkv_ref, dzkr_ref, dga_ref, ddc_ref, ddn_ref, dgp_ref, dgm_ref, x_ref, dh_ref,
             g_ref, dgq_ref, dgkv_ref, w_hbm, gx_ref, small_ref, dzs_ref, w_vmem, dbuf, sem, dgin_ref):
        i = pl.program_id(0)

        @pl.when(i == 0)
        def _():
            cp = pltpu.make_async_copy(w_hbm, w_vmem, sem)
            cp.start()
            dgin_ref[...] = jnp.zeros_like(dgin_ref)
            cp.wait()

        dbuf[0:tm, :] = ddc_ref[...]
        dbuf[tm:, :] = jnp.where(i < n_tiles - 1, ddn_ref[...], 0.0)
        row = lax.broadcasted_iota(jnp.int32, (tm, POOL_GROUP_DIM), 0) + i * tm
        du = []
        for g, w in enumerate(POOL_WINDOWS):
            cols = slice(g * POOL_GROUP_DIM, (g + 1) * POOL_GROUP_DIM)
            fsum = dbuf[0:tm, cols]
            for kk in range(1, w):
                fsum = fsum + dbuf[kk:kk + tm, cols]
            du.append(fsum - dbuf[0:tm, cols] * jnp.minimum(row + 1, w).astype(F32))
        du = jnp.concatenate(du, axis=1).astype(BF16)

        dz = [dzq_ref[...], dzkv_ref[...], dzkr_ref[...], dga_ref[...], du, dgp_ref[...], dgm_ref[...]]
        dz = jnp.concatenate([d[:, :w] for d, (w, _) in zip(dz, IN_SEGMENTS)], axis=1)
        for j in range(N_CHIPS):
            dzs_ref[j] = dz[:, j * SHARD_COLS:(j + 1) * SHARD_COLS].T
        dhn = _dot(dz, w_vmem[...])

        g = g_ref[...]
        _, xhat, r = _rms_fwd(x_ref[...], g)
        dgin_ref[...] += _colsum(dhn * xhat)
        gx_ref[...] = dh_ref[...] + _rms_bwd(dhn, xhat, r, g)

        @pl.when(i == n_tiles - 1)
        def _():
            _pack_into(small_ref, SMALL_LATE, dict(norm_in=dgin_ref[...], q_norm=dgq_ref[...], kv_norm=dgkv_ref[...]))

    any_spec = pl.BlockSpec(memory_space=pl.ANY)
    seg_w = [wide for _, wide in IN_SEGMENTS]
    return pl.pallas_call(
        body,
        name="inproj_bwd_x",
        grid=(n_tiles,),
        in_specs=[
            _row_spec(tm, seg_w[0]), _row_spec(tm, seg_w[1]), _row_spec(tm, seg_w[2]),
            _row_spec(tm, seg_w[3]), _row_spec(tm, seg_w[u_seg]),
            pl.BlockSpec((POOL_HALO, POOL_WIDTH), lambda i: (jnp.minimum((i + 1) * halo_per_tile, n_halo - 1), 0)),
            _row_spec(tm, seg_w[5]), _row_spec(tm, seg_w[6]),
            _row_spec(tm, D_MODEL), _row_spec(tm, D_MODEL),
            _full_spec((1, D_MODEL)), _full_spec((1, Q_LORA_RANK)), _full_spec((1, KV_LORA_RANK)), any_spec,
        ],
        out_specs=[_row_spec(tm, D_MODEL), _full_spec((small_rows, LANES)),
                   pl.BlockSpec((N_CHIPS, SHARD_COLS, tm), lambda i: (0, 0, i))],
        out_shape=[jax.ShapeDtypeStruct((s, D_MODEL), F32), jax.ShapeDtypeStruct((small_rows, LANES), F32),
                   jax.ShapeDtypeStruct((N_CHIPS, SHARD_COLS, s), BF16)],
        scratch_shapes=[
            pltpu.VMEM((IN_TOTAL, D_MODEL), BF16),
            pltpu.VMEM((tm + POOL_HALO, POOL_WIDTH), F32),
            pltpu.SemaphoreType.DMA,
            pltpu.VMEM((1, D_MODEL), F32),
        ],
        compiler_params=pltpu.CompilerParams(dimension_semantics=("arbitrary",), vmem_limit_bytes=VMEM_LIMIT),
    )(dzq, dzkv, dzkr, dgattn, ddc, ddc, dgpool, dgmerge, x, dh, norm_in, d_q_norm, d_kv_norm,
      w_in_t.reshape(IN_TOTAL, D_MODEL))


def _inproj_bwd_w(order, dz_sh, hn, g_uq, g_ukv, gs, tm):
    s = hn.shape[0]
    n_tiles = s // tm
    hc = D_MODEL // 2
    red = _Reduce(COMM_PARAMS[1:3])
    small = _SmallSum(gs.shape[0])
    n_red = len(red.scratch)

    def body(order_ref, dz_ref, hn_ref, guq_hbm, gukv_hbm, gs_ref, gw_hbm, guq_out, gukv_out, gsum_ref,
             acc, pm_w, a_w, b_w, r_w, w_send, w_recv, w_local, *more_scratch):
        ph, i = pl.program_id(0), pl.program_id(1)
        x, y, c = lax.axis_index("x"), lax.axis_index("y"), lax.axis_index("c")
        k = 2 * x + y
        me, sibling = (x, y, c), (x, y, 1 - c)
        chips = _other_chips(x, y)
        shard_of_phase = [2 * cx + cy for cx, cy in chips] + [k]
        copy = _remote_copier(w_send, w_recv)
        red.bind([guq_hbm, gukv_hbm], [guq_out, gukv_out], more_scratch[:n_red])
        small.bind(gs_ref, gsum_ref, more_scratch[n_red:])
        mine = pl.ds(pl.multiple_of(c * hc, hc), hc)
        theirs = pl.ds(pl.multiple_of((1 - c) * hc, hc), hc)

        def to_sibling(f):
            j = shard_of_phase[f]
            return copy(f, pm_w.at[j, 1 - c], a_w.at[j], sibling)

        def pair_sum(f):
            cx, cy = chips[f]
            return copy(4 + f, pm_w.at[shard_of_phase[f], c], b_w.at[f], (cx, cy, c))

        def finished():
            return copy(7, r_w, gw_hbm.at[:, mine], sibling)

        @pl.when(jnp.logical_and(ph == 0, i == 0))
        def _():
            red.start()
            small.start()

        part = _dot(dz_ref[0], hn_ref[...])

        @pl.when(i == 0)
        def _():
            acc[...] = part

        @pl.when(i > 0)
        def _():
            acc[...] += part

        for f in range(3):
            @pl.when(jnp.logical_and(ph == f + 1, i == 0))
            def _(f=f):
                j = shard_of_phase[f]
                copy(f, a_w.at[j], a_w.at[j], me).wait_recv()
                pm_w[j, c] = (pm_w[j, c].astype(F32) + a_w[j].astype(F32)).astype(BF16)
                pair_sum(f).start()
                if f == 0:
                    red.exchange()

        for f in range(4):
            @pl.when(jnp.logical_and(ph == f, i == n_tiles - 1))
            def _(f=f):
                j = shard_of_phase[f]
                pm_w[j, 0] = acc[:, :hc].astype(BF16)
                pm_w[j, 1] = acc[:, hc:].astype(BF16)
                to_sibling(f).start()
                if f < 3:
                    return
                copy(3, a_w.at[k], a_w.at[k], me).wait_recv()
                r_w[...] = pm_w[k, c].astype(F32) + a_w[k].astype(F32)
                for g in range(3):
                    copy(4 + g, b_w.at[g], b_w.at[g], me).wait_recv()
                    r_w[...] = r_w[...] + b_w[g].astype(F32)
                store = pltpu.make_async_copy(r_w, gw_hbm.at[:, mine], w_local)
                store.start()
                finished().start()
                red.finish()
                small.finish()
                copy(7, gw_hbm.at[:, theirs], gw_hbm.at[:, theirs], me).wait_recv()
                store.wait()
                for g in range(4):
                    to_sibling(g).wait_send()
                for g in range(3):
                    pair_sum(g).wait_send()
                finished().wait_send()

    any_spec = pl.BlockSpec(memory_space=pl.ANY)
    n_sem = 8
    grid_spec = pltpu.PrefetchScalarGridSpec(
        num_scalar_prefetch=1,
        grid=(N_CHIPS, n_tiles),
        in_specs=[
            pl.BlockSpec((1, SHARD_COLS, tm), lambda ph, i, order: (order[ph], 0, i)),
            pl.BlockSpec((tm, D_MODEL), lambda ph, i, order: (i, 0)),
            any_spec, any_spec,
            pl.BlockSpec(small.spec_shape, lambda ph, i, order: (0, 0)),
        ],
        out_specs=[any_spec, any_spec, any_spec, pl.BlockSpec(small.spec_shape, lambda ph, i, order: (0, 0))],
        scratch_shapes=[
            pltpu.VMEM((SHARD_COLS, D_MODEL), F32),
            pltpu.VMEM((N_CHIPS, 2, SHARD_COLS, hc), BF16),
            pltpu.VMEM((N_CHIPS, SHARD_COLS, hc), BF16),
            pltpu.VMEM((3, SHARD_COLS, hc), BF16),
            pltpu.VMEM((SHARD_COLS, hc), F32),
            pltpu.SemaphoreType.DMA((n_sem,)), pltpu.SemaphoreType.DMA((n_sem,)), pltpu.SemaphoreType.DMA,
        ] + red.scratch + small.scratch,
    )
    out = pl.pallas_call(
        body,
        name="inproj_bwd_w",
        grid_spec=grid_spec,
        out_shape=[jax.ShapeDtypeStruct((SHARD_COLS, D_MODEL), F32)] + red.out_shape
        + [small.out_shape],
        compiler_params=pltpu.CompilerParams(dimension_semantics=("arbitrary", "arbitrary"),
                                             vmem_limit_bytes=VMEM_LIMIT),
    )(order, dz_sh, hn, g_uq, g_ukv, gs)
    return out[0], out[1], out[2], out[3]


def _other_chips(x, y):
    return ((1 - x, 1 - y), (1 - x, y), (x, 1 - y))


def _half(ref, axis, size, c, lead=()):
    window = pl.ds(pl.multiple_of(c * size, size), size)
    if axis == 0:
        return ref.at[(*lead, window, slice(None))]
    return ref.at[(*lead, slice(None), window)]


def _half_shape(rows, cols, axis, size):
    return (size, cols) if axis == 0 else (rows, size)


def _remote_copier(send_sems, recv_sems):
    def copy(sem, src, dst, to):
        return pltpu.make_async_remote_copy(src_ref=src, dst_ref=dst, send_sem=send_sems.at[sem],
                                            recv_sem=recv_sems.at[sem], device_id=to, device_id_type=MESH)
    return copy


class _Gather:
    def __init__(self, params):
        self.params = params
        n = len(params)
        self.scratch = [pltpu.SemaphoreType.DMA((6 * n,)), pltpu.SemaphoreType.DMA((6 * n,)),
                        pltpu.SemaphoreType.DMA((n,))]
        self.out_shape = [jax.ShapeDtypeStruct((N_CHIPS, r, cc), BF16) for _, r, cc, _, _ in params]

    def bind(self, ins, outs, scratch):
        self.ins, self.outs = ins, outs
        send_sems, recv_sems, self.local_sems = scratch
        self.copy = _remote_copier(send_sems, recv_sems)
        self.x, self.y, self.c = lax.axis_index("x"), lax.axis_index("y"), lax.axis_index("c")
        self.k = 2 * self.x + self.y
        self.chips = _other_chips(self.x, self.y)

    def _local(self, p):
        return pltpu.make_async_copy(self.ins[p], self.outs[p].at[self.k], self.local_sems.at[p])

    def _first(self, p, j):
        _, _, _, axis, size = self.params[p]
        cx, cy = self.chips[j]
        return self.copy(6 * p + j, _half(self.ins[p], axis, size, self.c),
                         _half(self.outs[p], axis, size, self.c, (self.k,)), (cx, cy, self.c))

    def _relay(self, p, j, half_of):
        _, _, _, axis, size = self.params[p]
        cx, cy = self.chips[j]
        block = _half(self.outs[p], axis, size, half_of, (2 * cx + cy,))
        return self.copy(6 * p + 3 + j, block, block, (self.x, self.y, 1 - self.c))

    def start(self):
        for p in range(len(self.params)):
            self._local(p).start()
            for j in (1, 2, 0):
                self._first(p, j).start()

    def relay_one(self, p, j):
        _, _, _, axis, size = self.params[p]
        cx, cy = self.chips[j]
        landed = _half(self.outs[p], axis, size, self.c, (2 * cx + cy,))
        self.copy(6 * p + j, landed, landed, (self.x, self.y, self.c)).wait_recv()
        self._relay(p, j, self.c).start()

    def await_one(self, p, j):
        self._relay(p, j, 1 - self.c).wait_recv()

    def wait_sends(self):
        for p in range(len(self.params)):
            for j in range(3):
                self._first(p, j).wait_send()
                self._relay(p, j, self.c).wait_send()
            self._local(p).wait()

    def relay(self):
        for j in range(3):
            for p in range(len(self.params)):
                self.relay_one(p, j)

    def finish(self):
        for j in range(3):
            for p in range(len(self.params)):
                self.await_one(p, j)
        self.wait_sends()


class _Reduce:
    def __init__(self, params):
        self.params = params
        n = len(params)
        halves = [_half_shape(r, cc, axis, size) for _, r, cc, axis, size in params]
        self.scratch = ([pltpu.VMEM((N_CHIPS, *h), BF16) for h in halves]
                        + [pltpu.VMEM((N_CHIPS, *h), BF16) for h in halves]
                        + [pltpu.VMEM((3, *h), BF16) for h in halves]
                        + [pltpu.VMEM(h, F32) for h in halves]
                        + [pltpu.SemaphoreType.DMA((5 * n,)), pltpu.SemaphoreType.DMA((5 * n,)),
                           pltpu.SemaphoreType.DMA((2 * n,))])
        self.out_shape = [jax.ShapeDtypeStruct((r, cc), F32) for _, r, cc, _, _ in params]

    def bind(self, g_in, g_out, scratch):
        n = len(self.params)
        self.g_in, self.g_out = g_in, g_out
        self.pm, self.a_buf = scratch[0:n], scratch[n:2 * n]
        self.b_buf, self.r_buf = scratch[2 * n:3 * n], scratch[3 * n:4 * n]
        send_sems, recv_sems, self.local_sems = scratch[4 * n:]
        self.copy = _remote_copier(send_sems, recv_sems)
        self.x, self.y, self.c = lax.axis_index("x"), lax.axis_index("y"), lax.axis_index("c")
        self.k = 2 * self.x + self.y
        self.chips = _other_chips(self.x, self.y)
        self.me = (self.x, self.y, self.c)
        self.sibling = (self.x, self.y, 1 - self.c)

    def _load(self, p):
        _, _, _, axis, size = self.params[p]
        return pltpu.make_async_copy(_half(self.g_in[p], axis, size, self.c, (slice(None),)), self.pm[p],
                                     self.local_sems.at[p])

    def _to_sibling(self, p):
        _, _, _, axis, size = self.params[p]
        return self.copy(5 * p, _half(self.g_in[p], axis, size, 1 - self.c, (slice(None),)), self.a_buf[p],
                         self.sibling)

    def _pair_sum(self, p, j):
        cx, cy = self.chips[j]
        return self.copy(5 * p + 1 + j, self.pm[p].at[2 * cx + cy], self.b_buf[p].at[j], (cx, cy, self.c))

    def _store(self, p):
        _, _, _, axis, size = self.params[p]
        n = len(self.params)
        return pltpu.make_async_copy(self.r_buf[p], _half(self.g_out[p], axis, size, self.c),
                                     self.local_sems.at[n + p])

    def _finished(self, p):
        _, _, _, axis, size = self.params[p]
        return self.copy(5 * p + 4, self.r_buf[p], _half(self.g_out[p], axis, size, self.c), self.sibling)

    def start(self):
        for p in range(len(self.params)):
            self._load(p).start()
            self._to_sibling(p).start()

    def exchange(self):
        for p in range(len(self.params)):
            self._load(p).wait()
            self.copy(5 * p, self.a_buf[p], self.a_buf[p], self.me).wait_recv()
            for j, (cx, cy) in enumerate(self.chips):
                kj = 2 * cx + cy
                self.pm[p][kj] = (self.pm[p][kj].astype(F32) + self.a_buf[p][kj].astype(F32)).astype(BF16)
                self._pair_sum(p, j).start()
            self.r_buf[p][...] = self.pm[p][self.k].astype(F32) + self.a_buf[p][self.k].astype(F32)

    def finish(self):
        for p, (_, _, _, axis, size) in enumerate(self.params):
            for j in range(3):
                self.copy(5 * p + 1 + j, self.b_buf[p].at[j], self.b_buf[p].at[j], self.me).wait_recv()
                self.r_buf[p][...] = self.r_buf[p][...] + self.b_buf[p][j].astype(F32)
            self._store(p).start()
            self._finished(p).start()
        for p, (_, _, _, axis, size) in enumerate(self.params):
            theirs = _half(self.g_out[p], axis, size, 1 - self.c)
            self.copy(5 * p + 4, theirs, theirs, self.me).wait_recv()
            self._store(p).wait()
            self._to_sibling(p).wait_send()
            for j in range(3):
                self._pair_sum(p, j).wait_send()
            self._finished(p).wait_send()


class _SmallSum:
    def __init__(self, rows):
        self.rows = rows
        self.scratch = [pltpu.VMEM((N_DEV, rows, LANES), F32),
                        pltpu.SemaphoreType.DMA((N_DEV - 1,)), pltpu.SemaphoreType.DMA((N_DEV - 1,))]
        self.out_shape = jax.ShapeDtypeStruct((rows, LANES), F32)
        self.spec_shape = (rows, LANES)

    def bind(self, src, dst, scratch):
        self.src, self.dst = src, dst
        self.buf, send_sems, recv_sems = scratch
        self.copy = _remote_copier(send_sems, recv_sems)
        self.x, self.y, self.c = lax.axis_index("x"), lax.axis_index("y"), lax.axis_index("c")

    def _send(self, f):
        fx, fy, fc = [(a, b, d) for a in (0, 1) for b in (0, 1) for d in (0, 1)][f]
        x, y, c = self.x, self.y, self.c
        peer = (1 - x if fx else x, 1 - y if fy else y, 1 - c if fc else c)
        return self.copy(f - 1, self.src, self.buf.at[f], peer)

    def start(self):
        for f in range(1, N_DEV):
            self._send(f).start()
        self.buf[0] = self.src[...]

    def finish(self):
        me = (self.x, self.y, self.c)
        for f in range(1, N_DEV):
            self.copy(f - 1, self.buf.at[f], self.buf.at[f], me).wait_recv()
        dev = 4 * self.x + 2 * self.y + self.c
        total = self.buf[dev]
        for d in range(1, N_DEV):
            total = total + self.buf[jnp.bitwise_xor(dev, d)]
        self.dst[...] = total
        for f in range(1, N_DEV):
            self._send(f).wait_send()


def _adamw_math(w, g, m, v):
    m = ADAM_B1 * m + (1.0 - ADAM_B1) * g
    v = ADAM_B2 * v + (1.0 - ADAM_B2) * (g * g)
    m_hat = m / (1.0 - ADAM_B1 ** ADAM_STEP)
    v_hat = v / (1.0 - ADAM_B2 ** ADAM_STEP)
    delta = -ADAM_LR * (m_hat / (jnp.sqrt(v_hat) + ADAM_EPS) + ADAM_WD * w)
    return delta, m, v


def _adamw_tiled(w, g, m, v, tm):
    rows, cols = w.shape

    def body(w_ref, g_ref, m_ref, v_ref, d_ref, nm_ref, nv_ref, g_out):
        g = g_ref[...]
        d_ref[...], nm_ref[...], nv_ref[...] = _adamw_math(w_ref[...], g, m_ref[...], v_ref[...])
        g_out[...] = g

    spec = _row_spec(tm, cols)
    return pl.pallas_call(
        body,
        name="adamw_w_in",
        grid=(rows // tm,),
        in_specs=[spec] * 4,
        out_specs=[spec] * 4,
        out_shape=[jax.ShapeDtypeStruct(w.shape, F32)] * 4,
        compiler_params=pltpu.CompilerParams(dimension_semantics=("parallel",), vmem_limit_bytes=VMEM_LIMIT),
    )(w, g, m, v)


def _adamw_many(ws, gs, ms, vs, sq_err):
    n = len(ws)
    gs = list(gs) + [sq_err]
    g_arrays, g_at = [], []
    for g in gs:
        arr, row = g if isinstance(g, tuple) else (g, None)
        k = next((j for j, a in enumerate(g_arrays) if a is arr), len(g_arrays))
        if k == len(g_arrays):
            g_arrays.append(arr)
        g_at.append((k, row))
    n_g = len(g_arrays)

    def body(*refs):
        w_refs, m_refs, v_refs, g_refs, outs = (refs[:n], refs[n:2 * n], refs[2 * n:3 * n], refs[3 * n:3 * n + n_g],
                                                refs[3 * n + n_g:])
        k, row = g_at[n]
        sq = g_refs[k][...] if row is None else g_refs[k][row:row + 8, :]
        outs[4 * n][...] = jnp.full((1, 1), 0.5 * jnp.sum(sq) / D_MODEL, F32)
        for i in range(n):
            k, row = g_at[i]
            if row is None and g_refs[k].ndim == 2 and w_refs[i].ndim == 3:
                cols = ws[i].shape[2]
                for h in range(ws[i].shape[1]):
                    outs[3 * n + i][:, h, :] = g_refs[k][:, h * cols:(h + 1) * cols]
                g = outs[3 * n + i][...]
            else:
                g = g_refs[k][...] if row is None else g_refs[k][row:row + ws[i].shape[0], :]
                outs[3 * n + i][...] = g
            d, nm, nv = _adamw_math(w_refs[i][...], g, m_refs[i][...], v_refs[i][...])
            outs[i][...] = d
            outs[n + i][...] = nm
            outs[2 * n + i][...] = nv

    vmem_spec = pl.BlockSpec(memory_space=pltpu.VMEM)
    shapes = [jax.ShapeDtypeStruct(w.shape, F32) for w in ws]
    out = pl.pallas_call(
        body,
        name="adamw_small",
        in_specs=[vmem_spec] * (3 * n + n_g),
        out_specs=[vmem_spec] * (4 * n + 1),
        out_shape=shapes * 4 + [jax.ShapeDtypeStruct((1, 1), F32)],
        compiler_params=pltpu.CompilerParams(vmem_limit_bytes=VMEM_LIMIT),
    )(*ws, *ms, *vs, *g_arrays)
    return out[:n], out[n:2 * n], out[2 * n:3 * n], out[3 * n:4 * n], out[4 * n]


def _rope_tables(s):
    half = QK_ROPE_DIM // 2
    inv_freq = np.float32(ROPE_THETA) ** (-np.arange(half, dtype=np.float32) / np.float32(half))
    ang = (np.arange(s, dtype=np.float32)[:, None] * inv_freq[None, :]).astype(np.float32)
    cos, sin = np.cos(ang.astype(np.float64)).astype(np.float32), np.sin(ang.astype(np.float64)).astype(np.float32)
    z16 = np.zeros((s, half), np.float32)
    z32 = np.zeros((s, HEAD_PAD - QK_NOPE_DIM - QK_ROPE_DIM), np.float32)
    z64 = np.zeros((s, QK_NOPE_DIM), np.float32)
    rc = np.concatenate([np.ones((s, QK_NOPE_DIM), np.float32), cos, cos, z32], axis=1)
    rsa = np.concatenate([z64, -sin, z16, z32], axis=1)
    rsb = np.concatenate([z64, z16, sin, z32], axis=1)
    return jnp.asarray(rc), jnp.asarray(rsa), jnp.asarray(rsb)


def kernel(x, norm_in, w_in, q_norm, w_uq, kv_norm, w_ukv, pool_w, pool_scale, w_branch_attn, w_branch_pool, w_out, norm_final, loss_target, m_norm_in, m_w_in, m_q_norm, m_w_uq, m_kv_norm, m_w_ukv, m_pool_w, m_pool_scale, m_w_branch_attn, m_w_branch_pool, m_w_out, m_norm_final, v_norm_in, v_w_in, v_q_norm, v_w_uq, v_kv_norm, v_w_ukv, v_pool_w, v_pool_scale, v_w_branch_attn, v_w_branch_pool, v_w_out, v_norm_final):
    s = x.shape[1]
    t_att, t_row = _tiles(s)
    x2 = x.reshape(s, D_MODEL)
    tgt = loss_target.reshape(s, D_MODEL)

    cx, cy = lax.axis_index("x"), lax.axis_index("y")
    others = [2 * ox + oy for ox, oy in _other_chips(cx, cy)]
    hn, z_sh, (w_in_t, w_uq_all, w_ukv_all) = _inproj_fwd(
        jnp.stack([2 * cx + cy, others[1], others[2], others[0]]).astype(jnp.int32), x2, norm_in.reshape(1, -1),
        w_in.T.astype(BF16), [w_uq, w_ukv], 4 * t_row)
    rc, rsa, rsb = _rope_tables(s)
    g_in = norm_in.reshape(1, -1)
    g_q = q_norm.reshape(-1, LANES)
    g_kv = kv_norm.reshape(1, -1)
    g_f = norm_final.reshape(1, -1)
    ps = pool_scale.reshape(1, -1)

    q, k, v, q_t, v_t, wuq_p, wk_p, wv, *late, pw_bf = _qkv_fwd(
        z_sh, g_q, g_kv, w_uq_all, w_ukv_all, rc, rsa, rsb, [w_branch_attn, w_branch_pool, w_out, pool_w], 2 * t_row)
    o, lse, (w_ba_all, w_bp_all, w_out_all) = _attn_fwd(q_t, k, v_t, late, t_att)
    w_out_f = w_out_all.reshape(D_MODEL, D_MODEL)

    do, delta, dgattn, dgpool, dgmerge, ddc, dh, gs_mid, d_w_out, d_w_ba, d_w_bp = _mid(o, z_sh, x2, tgt, pw_bf, ps, w_ba_all, w_bp_all, w_out_f, g_f, t_row)

    late_grads = [d_w_ba, d_w_bp, d_w_out.reshape(N_CHIPS, 256, D_MODEL)]
    dq, dk_t, dv_t, (g_w_ba, g_w_bp, g_w_out), g_small_mid = _attn_bwd(q, q_t, k, v, do, lse, delta, late_grads,
                                                                      gs_mid, t_att)
    dzq, dzkv, dzkr, d_w_uq, d_w_ukv, d_q_norm, d_kv_norm = _qkv_bwd(
        dq, dk_t, dv_t, z_sh, g_q, g_kv, wuq_p, wk_p, wv, rc, rsa, rsb, 2 * t_row)
    grad_x, gs, dz_sh = _inproj_bwd_x(dzq, dzkv, dzkr, dgattn, ddc, dgpool, dgmerge, x2, dh, g_in, d_q_norm, d_kv_norm,
                                      w_in_t, 2 * t_row)

    order = jnp.stack(others + [2 * cx + cy]).astype(jnp.int32)
    g_w_in_t, g_w_uq, g_w_ukv, g_small = _inproj_bwd_w(
        order, dz_sh, hn, d_w_uq, d_w_ukv, gs, 4 * t_row)

    dl_w_in, nm_w_in, nv_w_in, g_w_in = (a.T for a in _adamw_tiled(w_in.T, g_w_in_t, m_w_in.T, v_w_in.T, 152))

    packed = {n: (g_small_mid, r) for n, r in _first_rows(SMALL_MID).items() if n != "sq_err"}
    packed.update({n: (g_small, r) for n, r in _first_rows(SMALL_LATE).items()})

    def as_rows(n, a):
        return a.reshape(-1, LANES) if n in packed else a

    names = ["norm_in", "q_norm", "w_uq", "kv_norm", "w_ukv", "pool_w", "pool_scale", "w_branch_attn",
             "w_branch_pool", "w_out", "norm_final"]
    ws = dict(norm_in=norm_in, q_norm=q_norm, w_uq=w_uq, kv_norm=kv_norm, w_ukv=w_ukv, pool_w=pool_w,
              pool_scale=pool_scale, w_branch_attn=w_branch_attn, w_branch_pool=w_branch_pool, w_out=w_out,
              norm_final=norm_final)
    gsd = dict(packed, w_uq=g_w_uq, w_ukv=g_w_ukv, w_branch_attn=g_w_ba, w_branch_pool=g_w_bp, w_out=g_w_out)
    msd = dict(norm_in=m_norm_in, q_norm=m_q_norm, w_uq=m_w_uq, kv_norm=m_kv_norm, w_ukv=m_w_ukv, pool_w=m_pool_w,
               pool_scale=m_pool_scale, w_branch_attn=m_w_branch_attn, w_branch_pool=m_w_branch_pool, w_out=m_w_out,
               norm_final=m_norm_final)
    vsd = dict(norm_in=v_norm_in, q_norm=v_q_norm, w_uq=v_w_uq, kv_norm=v_kv_norm, w_ukv=v_w_ukv, pool_w=v_pool_w,
               pool_scale=v_pool_scale, w_branch_attn=v_w_branch_attn, w_branch_pool=v_w_branch_pool, w_out=v_w_out,
               norm_final=v_norm_final)
    dls, nms, nvs, g_outs, loss = _adamw_many(
        [as_rows(n, ws[n]) for n in names], [gsd[n] for n in names], [as_rows(n, msd[n]) for n in names],
        [as_rows(n, vsd[n]) for n in names], (g_small_mid, _first_rows(SMALL_MID)["sq_err"]))

    grads = dict(zip(names, g_outs))
    grads["w_in"] = g_w_in
    delta_w = {n: d.reshape(ws[n].shape) for n, d in zip(names, dls)}
    new_m = {n: d.reshape(ws[n].shape) for n, d in zip(names, nms)}
    new_v = {n: d.reshape(ws[n].shape) for n, d in zip(names, nvs)}
    delta_w["w_in"], new_m["w_in"], new_v["w_in"] = dl_w_in, nm_w_in, nv_w_in
    ws["w_in"] = w_in

    order = ["norm_in", "w_in", "q_norm", "w_uq", "kv_norm", "w_ukv", "pool_w", "pool_scale", "w_branch_attn",
             "w_branch_pool", "w_out", "norm_final"]
    return (loss.reshape(()), grad_x.reshape(x.shape),
            *[grads[n].reshape(ws[n].shape) for n in order],
            *[delta_w[n] for n in order], *[new_m[n] for n in order], *[new_v[n] for n in order])
```

```python
import functools

import jax
import jax.numpy as jnp
import numpy as np
from jax import lax
from jax.experimental import pallas as pl
from jax.experimental.pallas import tpu as pltpu

F32 = jnp.float32
BF16 = jnp.bfloat16
MESH = pl.DeviceIdType.MESH

D_MODEL = 1024
CHUNK = 64
MLA_HEADS = 8
QK_NOPE_DIM = 64
QK_ROPE_DIM = 32
V_HEAD_DIM = 64
Q_LORA_RANK = 384
KV_LORA_RANK = 256
MLA_WIDTH = MLA_HEADS * V_HEAD_DIM
ROPE_THETA = 10000.0
POOL_WINDOWS = (2, 4, 8, 16)
POOL_WIDTH = 512
POOL_GROUP_DIM = 128
BRANCH_COLS = D_MODEL // 4
FWD_HEADS = 8
BWD_HEADS = 4
POOL_HALO = 16
EPS = 1e-6
IN_TOTAL = 4256
HEAD_PAD = 128
ATT_SCALE = (QK_NOPE_DIM + QK_ROPE_DIM) ** -0.5
ATT_SCALE_LOG2E = ATT_SCALE * 1.4426950408889634

ADAM_LR = 0.001
ADAM_B1 = 0.9
ADAM_B2 = 0.999
ADAM_EPS = 1e-08
ADAM_WD = 0.01
ADAM_STEP = 10

N_CHIPS = 4
N_DEV = 8
LANES = 128
VMEM_LIMIT = 60 * 1024 * 1024

IN_SEGMENTS = ((384, 384), (256, 256), (32, HEAD_PAD), (512, 512), (512, 512), (512, 512), (2048, 2048))
SHARD_COLS = IN_TOTAL // N_CHIPS
ZQ_COLS = slice(0, 384)
ZKV_COLS = slice(384, 640)
ZKR_TILE = slice(640, 768)


def _shard_pieces():
    bounds, off = [], 0
    for w, _ in IN_SEGMENTS:
        bounds.append((off, off + w))
        off += w
    out = []
    for j in range(N_CHIPS):
        lo, hi = SHARD_COLS * j, SHARD_COLS * (j + 1)
        out.append([(i, max(lo, a) - a, min(hi, b) - a, max(lo, a) - lo)
                    for i, (a, b) in enumerate(bounds) if max(lo, a) < min(hi, b)])
    return out


SHARD_PIECES = _shard_pieces()


def _segment(z_blocks, seg):
    parts = [z_blocks[j][:, col:col + hi - lo]
             for j, pieces in enumerate(SHARD_PIECES) for sg, lo, hi, col in pieces if sg == seg]
    return parts[0] if len(parts) == 1 else jnp.concatenate(parts, axis=1)

COMM_PARAMS = (
    ("w_in", SHARD_COLS, D_MODEL, 1, 512),
    ("w_uq", 96, 768, 0, 48),
    ("w_ukv", 64, 1024, 0, 32),
    ("w_branch_attn", 512, 256, 0, 256),
    ("w_branch_pool", 512, 256, 0, 256),
    ("w_out", 256, 1024, 0, 128),
)

SMALL_MID = (
    ("pool_w", (4, 128, 128)),
    ("norm_final", (1024,)),
    ("sq_err", (8, 128)),
    ("pool_scale", (512,)),
)
SMALL_LATE = (
    ("norm_in", (1024,)),
    ("q_norm", (384,)),
    ("kv_norm", (256,)),
)


def _small_rows(shapes):
    return -(-sum(int(np.prod(s)) for _, s in shapes) // (LANES * 8)) * 8


def _first_rows(shapes):
    out, off = {}, 0
    for name, shp in shapes:
        out[name], rem = divmod(off, LANES)
        assert rem == 0, name
        off += int(np.prod(shp))
    return out


def _pack_into(dst_ref, shapes, values):
    first = _first_rows(shapes)
    for name, shp in shapes:
        v, row = values[name], first[name]
        if v.ndim == 3:
            for g in range(v.shape[0]):
                dst_ref[row + g * v.shape[1]:row + (g + 1) * v.shape[1], :] = v[g]
        elif v.shape[0] == 1 and v.shape[1] > LANES:
            for j in range(v.shape[1] // LANES):
                dst_ref[row + j:row + j + 1, :] = v[:, j * LANES:(j + 1) * LANES]
        else:
            dst_ref[row:row + v.shape[0], :] = v
    used = sum(int(np.prod(shp)) for _, shp in shapes) // LANES
    if used < dst_ref.shape[0]:
        dst_ref[used:, :] = jnp.zeros((dst_ref.shape[0] - used, LANES), dst_ref.dtype)


def _as_one_row(g):
    return jnp.concatenate([g[j:j + 1] for j in range(g.shape[0])], axis=1)


def _dot(a, b):
    return jnp.dot(a, b, preferred_element_type=F32)


def _dot_nt(a, b):
    return lax.dot_general(a, b, (((1,), (1,)), ((), ())), preferred_element_type=F32)


def _dot_tn(a, b):
    return lax.dot_general(a, b, (((0,), (0,)), ((), ())), preferred_element_type=F32)


def _sigmoid(x):
    return 1.0 / (1.0 + jnp.exp(-x))


def _colsum(x):
    return jnp.sum(x, axis=0, keepdims=True)


def _rms_fwd(x, g):
    r = lax.rsqrt(jnp.mean(x * x, axis=-1, keepdims=True) + EPS)
    xhat = x * r
    return xhat * g, xhat, r


def _rms_bwd(dy, xhat, r, g):
    dxhat = dy * g
    return r * (dxhat - xhat * jnp.mean(dxhat * xhat, axis=-1, keepdims=True))


def _rope(v, c, sa, sb):
    return v * c + pltpu.roll(v, 112, 1) * sa + pltpu.roll(v, 16, 1) * sb


def _unrope(d, c, sa, sb):
    return d * c + pltpu.roll(d * sa, 16, 1) + pltpu.roll(d * sb, 112, 1)


def _row_spec(tm, n):
    return pl.BlockSpec((tm, n), lambda i: (i, 0))


def _full_spec(shape):
    nd = len(shape)
    return pl.BlockSpec(shape, lambda i: (0,) * nd)


def _tiles(s):
    t_att = 512 if s >= 2048 else 128
    t_row = 256 if s >= 1024 else 128
    return t_att, t_row


def _inproj_fwd(order, x, norm_in, w_in_shard, up_shards, tm):
    s = x.shape[0]
    n_tiles = s // tm
    gat = _Gather(COMM_PARAMS[:3])
    n_w = len(gat.params)
    arrival = (1, 2, 0)
    n_up = len(up_shards)

    def body(order_ref, x_ref, g_ref, w_in_loc, *rest):
        up_refs, (hn_ref, z_ref), w_all = rest[:n_up], rest[n_up:n_up + 2], rest[n_up + 2:n_up + 2 + n_w]
        rest = rest[n_up + 2 + n_w:]
        (w_vmem, hn_all, w_sem), up_flat = rest[:3], rest[3:3 + n_up]
        gat.bind((w_in_loc,) + tuple(up_flat), w_all, rest[3 + n_up:])
        ph, i = pl.program_id(0), pl.program_id(1)

        @pl.when(jnp.logical_and(ph == 0, i == 0))
        def _():
            for src, dst in zip(up_refs, up_flat):
                dst[...] = jnp.concatenate([src[:, h, :] for h in range(MLA_HEADS)], axis=1).astype(BF16)
            gat.start()

        def fetch(phase):
            src = w_in_loc if phase == 0 else w_all[0].at[order_ref[phase]]
            return pltpu.make_async_copy(src, w_vmem.at[phase % 2], w_sem.at[phase % 2])

        def landed(f):
            gat.relay_one(0, arrival[f])
            gat.await_one(0, arrival[f])

        @pl.when(jnp.logical_and(ph == 0, i == 0))
        def _():
            fetch(0).start()
            fetch(0).wait()

        @pl.when(jnp.logical_and(ph == 1, i == 0))
        def _():
            landed(0)
            fetch(1).start()
            fetch(1).wait()

        for f in (1, 2):
            @pl.when(jnp.logical_and(ph == f, i == n_tiles - 1))
            def _(f=f):
                landed(f)
                fetch(f + 1).start()

            @pl.when(jnp.logical_and(ph == f + 1, i == 0))
            def _(f=f):
                fetch(f + 1).wait()

        rows = pl.ds(pl.multiple_of(i * tm, tm), tm)

        @pl.when(ph == 0)
        def _():
            hn, _, _ = _rms_fwd(x_ref[...], g_ref[...])
            hn = hn.astype(BF16)
            hn_ref[...] = hn
            hn_all[rows, :] = hn

        z_ref[0] = _dot_nt(hn_all[rows, :], w_vmem[ph % 2])

        @pl.when(jnp.logical_and(ph == N_CHIPS - 1, i == n_tiles - 1))
        def _():
            for p in range(1, n_w):
                for j in range(3):
                    gat.relay_one(p, j)
            for p in range(1, n_w):
                for j in range(3):
                    gat.await_one(p, j)
            gat.wait_sends()

    def tile_in_phase0(ph, i, order):
        return (jnp.where(ph == 0, i, n_tiles - 1), 0)

    any_spec = pl.BlockSpec(memory_space=pl.ANY)
    grid_spec = pltpu.PrefetchScalarGridSpec(
        num_scalar_prefetch=1,
        grid=(N_CHIPS, n_tiles),
        in_specs=[pl.BlockSpec((tm, D_MODEL), tile_in_phase0),
                  pl.BlockSpec((1, D_MODEL), lambda ph, i, order: (0, 0)), any_spec]
        + [pl.BlockSpec(a.shape, lambda ph, i, order: (0, 0, 0)) for a in up_shards],
        out_specs=[pl.BlockSpec((tm, D_MODEL), tile_in_phase0),
                   pl.BlockSpec((1, tm, SHARD_COLS), lambda ph, i, order: (order[ph], i, 0))] + [any_spec] * n_w,
        scratch_shapes=[pltpu.VMEM((2, SHARD_COLS, D_MODEL), BF16), pltpu.VMEM((s, D_MODEL), BF16),
                        pltpu.SemaphoreType.DMA((2,))]
        + [pltpu.VMEM((r, cc), BF16) for _, r, cc, _, _ in gat.params[1:]] + gat.scratch,
    )
    out = pl.pallas_call(
        body,
        name="inproj_fwd",
        grid_spec=grid_spec,
        out_shape=[jax.ShapeDtypeStruct((s, D_MODEL), BF16), jax.ShapeDtypeStruct((N_CHIPS, s, SHARD_COLS), F32)]
        + gat.out_shape,
        compiler_params=pltpu.CompilerParams(dimension_semantics=("arbitrary", "arbitrary"),
                                             vmem_limit_bytes=VMEM_LIMIT),
    )(order, x, norm_in, w_in_shard, *up_shards)
    return out[0], out[1], out[2:]


def _qkv_fwd(z_sh, q_norm, kv_norm, w_uq_all, w_ukv_all, rc, rsa, rsb, to_bf16, tm):
    s = z_sh.shape[1]
    n_cast = len(to_bf16)
    hw = MLA_HEADS * HEAD_PAD
    qk = QK_NOPE_DIM + QK_ROPE_DIM

    def body(z_ref, gq_ref, gkv_ref, uq_ref, ukv_ref, c_ref, sa_ref, sb_ref, *rest):
        f32_refs, rest = rest[:n_cast], rest[n_cast:]
        (q_ref, k_ref, v_ref, qt_ref, vt_ref, wuq_ref, wk_ref, wv_ref), bf_refs = rest[:8], rest[8:]

        @pl.when(pl.program_id(0) == 0)
        def _():
            for src, dst in zip(f32_refs, bf_refs):
                dst[...] = src[...].astype(BF16)
            w_q = jnp.concatenate([uq_ref[j] for j in range(N_CHIPS)], axis=0).astype(F32)
            gap = jnp.zeros((Q_LORA_RANK, HEAD_PAD - qk), F32)
            wuq_ref[...] = jnp.concatenate(
                [part for h in range(MLA_HEADS) for part in (w_q[:, h * qk:(h + 1) * qk], gap)], axis=1).astype(BF16)
            w_kv = jnp.concatenate([ukv_ref[j] for j in range(N_CHIPS)], axis=0).astype(F32)
            lane = lax.broadcasted_iota(jnp.int32, w_kv.shape, 1)
            wk_ref[...] = jnp.where(lane % HEAD_PAD < QK_NOPE_DIM, w_kv, 0.0).astype(BF16)
            wv_ref[...] = jnp.concatenate(
                [w_kv[:, h * HEAD_PAD + QK_NOPE_DIM:(h + 1) * HEAD_PAD] for h in range(MLA_HEADS)],
                axis=1).astype(BF16)

        c, sa, sb = c_ref[...], sa_ref[...], sb_ref[...]
        z0 = z_ref[0]
        cq, _, _ = _rms_fwd(z0[:, ZQ_COLS], _as_one_row(gq_ref[...]))
        qf = _dot(cq.astype(BF16), wuq_ref[...])
        ckv, _, _ = _rms_fwd(z0[:, ZKV_COLS], gkv_ref[...])
        ckv = ckv.astype(BF16)
        kn = _dot(ckv, wk_ref[...])
        lane = lax.broadcasted_iota(jnp.int32, (tm, HEAD_PAD), 1)
        zkr = jnp.where(lane < QK_ROPE_DIM, z0[:, ZKR_TILE], 0.0)
        kr = _rope(pltpu.roll(zkr, 64, 1), c, sa, sb)
        for h in range(MLA_HEADS):
            cols = slice(h * HEAD_PAD, (h + 1) * HEAD_PAD)
            qh = _rope(qf[:, cols], c, sa, sb)
            q_ref[:, cols] = qh.astype(BF16)
            qt_ref[cols, :] = qh.T.astype(BF16)
            k_ref[:, cols] = (kn[:, cols] + kr).astype(BF16)
        vf = _dot(ckv, wv_ref[...])
        v_ref[...] = vf.astype(BF16)
        vt_ref[...] = vf.T.astype(BF16)

    return pl.pallas_call(
        body,
        name="qkv_fwd",
        grid=(s // tm,),
        in_specs=[
            pl.BlockSpec((1, tm, SHARD_COLS), lambda i: (0, i, 0)),
            _full_spec(q_norm.shape), _full_spec((1, KV_LORA_RANK)),
            _full_spec(w_uq_all.shape), _full_spec(w_ukv_all.shape),
            _row_spec(tm, HEAD_PAD), _row_spec(tm, HEAD_PAD), _row_spec(tm, HEAD_PAD),
        ] + [_full_spec(a.shape) for a in to_bf16],
        out_specs=[_row_spec(tm, hw), _row_spec(tm, hw), _row_spec(tm, MLA_WIDTH),
                   pl.BlockSpec((hw, tm), lambda i: (0, i)), pl.BlockSpec((MLA_WIDTH, tm), lambda i: (0, i)),
                   _full_spec((Q_LORA_RANK, hw)), _full_spec((KV_LORA_RANK, hw)), _full_spec((KV_LORA_RANK, MLA_WIDTH))
                   ] + [_full_spec(a.shape) for a in to_bf16],
        out_shape=[jax.ShapeDtypeStruct((s, hw), BF16), jax.ShapeDtypeStruct((s, hw), BF16),
                   jax.ShapeDtypeStruct((s, MLA_WIDTH), BF16),
                   jax.ShapeDtypeStruct((hw, s), BF16), jax.ShapeDtypeStruct((MLA_WIDTH, s), BF16),
                   jax.ShapeDtypeStruct((Q_LORA_RANK, hw), BF16), jax.ShapeDtypeStruct((KV_LORA_RANK, hw), BF16),
                   jax.ShapeDtypeStruct((KV_LORA_RANK, MLA_WIDTH), BF16)
                   ] + [jax.ShapeDtypeStruct(a.shape, BF16) for a in to_bf16],
        compiler_params=pltpu.CompilerParams(dimension_semantics=("arbitrary",), vmem_limit_bytes=VMEM_LIMIT),
    )(z_sh, q_norm, kv_norm, w_uq_all, w_ukv_all, rc, rsa, rsb, *to_bf16)


def _chunk_mask(t, keys_on_rows):
    rows = lax.broadcasted_iota(jnp.int32, (t, t), 0) // CHUNK
    cols = lax.broadcasted_iota(jnp.int32, (t, t), 1) // CHUNK
    return rows <= cols if keys_on_rows else cols <= rows


def _attn_fwd(q_t, k, v_t, late_shards, t):
    s = k.shape[0]
    groups = MLA_HEADS // FWD_HEADS
    n_q = s // t
    gat = _Gather(COMM_PARAMS[3:])
    n_w = len(gat.params)

    def body(qt_ref, k_ref, k2_ref, vt_ref, *rest):
        w_in, (o_ref, lse_ref), w_out = rest[:n_w], rest[n_w:n_w + 2], rest[n_w + 2:2 * n_w + 2]
        gat.bind(w_in, w_out, rest[2 * n_w + 2:])
        i = pl.program_id(1)
        step_no = pl.program_id(0) * n_q + i
        pl.when(step_no == 0)(gat.start)
        pl.when(step_no == groups * n_q // 2)(gat.relay)
        mask = _chunk_mask(t, True)
        qcs = [slice(hh * HEAD_PAD, (hh + 1) * HEAD_PAD) for hh in range(FWD_HEADS)]
        vcs = [slice(hh * V_HEAD_DIM, (hh + 1) * V_HEAD_DIM) for hh in range(FWD_HEADS)]
        qts = [qt_ref[qc, :] for qc in qcs]

        def step(j, carry, masked):
            keys = pl.ds(pl.multiple_of(j * t, t), t)
            out = []
            for hh in range(FWD_HEADS):
                m, l, acc = carry[hh]
                sc = _dot(k_ref[keys, qcs[hh]], qts[hh])
                if masked:
                    sc = jnp.where(mask, sc, -jnp.inf)
                m_new = jnp.maximum(m, jnp.max(sc, axis=0, keepdims=True))
                alpha = jnp.exp2((m - m_new) * ATT_SCALE_LOG2E)
                p = jnp.exp2((_dot(k2_ref[keys, qcs[hh]], qts[hh]) - m_new) * ATT_SCALE_LOG2E)
                if masked:
                    p = jnp.where(mask, p, 0.0)
                l = alpha * l + jnp.sum(p, axis=0, keepdims=True)
                acc = alpha * acc + _dot(vt_ref[vcs[hh], keys], p.astype(BF16))
                out.append((m_new, l, acc))
            return tuple(out)

        one = (jnp.full((1, t), -jnp.inf, F32), jnp.zeros((1, t), F32), jnp.zeros((V_HEAD_DIM, t), F32))
        carry = lax.fori_loop(0, i, functools.partial(step, masked=False), (one,) * FWD_HEADS)
        carry = step(i, carry, True)
        o_ref[...] = jnp.concatenate([carry[hh][2] / carry[hh][1] for hh in range(FWD_HEADS)], axis=0).T
        for hh in range(FWD_HEADS):
            m, l, _ = carry[hh]
            lse_ref[:, qcs[hh]] = jnp.broadcast_to(m * ATT_SCALE_LOG2E + jnp.log2(l), (HEAD_PAD, t)).T
        pl.when(step_no == groups * n_q - 1)(gat.finish)

    any_spec = pl.BlockSpec(memory_space=pl.ANY)
    out = pl.pallas_call(
        body,
        name="attn_fwd",
        grid=(groups, n_q),
        in_specs=[
            pl.BlockSpec((FWD_HEADS * HEAD_PAD, t), lambda p, i: (p, i)),
            pl.BlockSpec((s, FWD_HEADS * HEAD_PAD), lambda p, i: (0, p), pipeline_mode=pl.Buffered(1)),
            pl.BlockSpec((s, FWD_HEADS * HEAD_PAD), lambda p, i: (0, p), pipeline_mode=pl.Buffered(1)),
            pl.BlockSpec((FWD_HEADS * V_HEAD_DIM, s), lambda p, i: (p, 0), pipeline_mode=pl.Buffered(1)),
        ] + [any_spec] * n_w,
        out_specs=[
            pl.BlockSpec((t, FWD_HEADS * V_HEAD_DIM), lambda p, i: (i, p)),
            pl.BlockSpec((t, FWD_HEADS * HEAD_PAD), lambda p, i: (i, p)),
        ] + [any_spec] * n_w,
        out_shape=[jax.ShapeDtypeStruct((s, MLA_WIDTH), F32), jax.ShapeDtypeStruct((s, MLA_HEADS * HEAD_PAD), F32)]
        + gat.out_shape,
        scratch_shapes=gat.scratch,
        compiler_params=pltpu.CompilerParams(dimension_semantics=("arbitrary", "arbitrary"),
                                             vmem_limit_bytes=VMEM_LIMIT),
    )(q_t, k, k, v_t, *late_shards)
    return out[0], out[1], out[2:]


def _mid(o, z_sh, x, target, pool_w, pool_scale, w_ba, w_bp, w_out, norm_final, tm):
    s = x.shape[0]
    n_tiles = s // tm
    halo_per_tile = tm // POOL_HALO
    small_rows = _small_rows(SMALL_MID)

    def body(o_ref, z0_ref, z1_ref, z1h_ref, z2_ref, z3_ref, x_ref, t_ref, pw_ref, ps_ref, wba_ref, wbp_ref,
             wout_ref, gf_ref,
             do_ref, dl_ref, dga_ref, dgp_ref, dgm_ref, ddc_ref, dh_ref,
             small_ref, dwout_out, dwba_out, dwbp_out,
             ubuf, dwout_ref, dwba_ref, dwbp_ref, loss_ref, dpw_ref, dps_ref, dgf_ref):
        i = pl.program_id(0)

        @pl.when(i == 0)
        def _():
            loss_ref[...] = jnp.zeros_like(loss_ref)
            dwout_ref[...] = jnp.zeros_like(dwout_ref)
            dwba_ref[...] = jnp.zeros_like(dwba_ref)
            dwbp_ref[...] = jnp.zeros_like(dwbp_ref)
            dpw_ref[...] = jnp.zeros_like(dpw_ref)
            dps_ref[...] = jnp.zeros_like(dps_ref)
            dgf_ref[...] = jnp.zeros_like(dgf_ref)

        zs = [z0_ref[0], z1_ref[0], z2_ref[0], z3_ref[0]]
        o = o_ref[...]
        ga = _segment(zs, 3)
        sga = _sigmoid(ga)
        silu_a = ga * sga
        y_attn = (o * silu_a).astype(BF16)

        ubuf[0:POOL_HALO, :] = jnp.where(i > 0, _segment([None, z1h_ref[0]], 4), 0.0)
        ubuf[POOL_HALO:, :] = _segment(zs, 4)
        row = lax.broadcasted_iota(jnp.int32, (tm, POOL_GROUP_DIM), 0) + i * tm
        ps = ps_ref[...]
        gp = _segment(zs, 5)
        sgp = _sigmoid(gp)
        silu_p = gp * sgp
        d_bf, dm, inv_cnt = [], [], []
        for g, w in enumerate(POOL_WINDOWS):
            cols = slice(g * POOL_GROUP_DIM, (g + 1) * POOL_GROUP_DIM)
            wsum = ubuf[POOL_HALO:, cols]
            for kk in range(1, w):
                wsum = wsum + ubuf[POOL_HALO - kk:POOL_HALO - kk + tm, cols]
            inv = 1.0 / jnp.minimum(row + 1, w).astype(F32)
            dg = (wsum * inv - ubuf[POOL_HALO:, cols]).astype(BF16)
            d_bf.append(dg)
            inv_cnt.append(inv)
            dm.append(_dot(dg, pw_ref[g]))
        dm = jnp.concatenate(dm, axis=1)
        yp = dm * ps
        y_pool = (yp * silu_p).astype(BF16)

        a = jnp.concatenate([_dot(y_attn, wba_ref[j]) for j in range(N_CHIPS)], axis=1)
        p = jnp.concatenate([_dot(y_pool, wbp_ref[j]) for j in range(N_CHIPS)], axis=1)
        gm = _segment(zs, 6)
        gate_a = _sigmoid(gm[:, :D_MODEL])
        gate_p = _sigmoid(gm[:, D_MODEL:])
        merged = (gate_a * a + gate_p * p).astype(BF16)
        h = x_ref[...] + _dot(merged, wout_ref[...])
        gf = gf_ref[...]
        y, xhat, r = _rms_fwd(h, gf)
        err = y - t_ref[...]
        e2 = err * err
        e2 = jnp.sum(e2.reshape(tm // 8, 8, D_MODEL), axis=0)
        acc = e2[:, 0:LANES]
        for cidx in range(1, D_MODEL // LANES):
            acc = acc + e2[:, cidx * LANES:(cidx + 1) * LANES]
        loss_ref[...] += acc

        dy = err * (1.0 / D_MODEL)
        dgf_ref[...] += _colsum(dy * xhat)
        dh = _rms_bwd(dy, xhat, r, gf)
        dh_ref[...] = dh
        dh_bf = dh.astype(BF16)
        dwout_ref[...] += _dot_tn(merged, dh_bf)
        dmerged = _dot_nt(dh_bf, wout_ref[...])
        da = (dmerged * gate_a).astype(BF16)
        dp = (dmerged * gate_p).astype(BF16)
        dgm_ref[:, :D_MODEL] = (dmerged * a * gate_a * (1.0 - gate_a)).astype(BF16)
        dgm_ref[:, D_MODEL:] = (dmerged * p * gate_p * (1.0 - gate_p)).astype(BF16)
        dy_attn = dy_pool = None
        for j in range(N_CHIPS):
            cols = slice(j * BRANCH_COLS, (j + 1) * BRANCH_COLS)
            dwba_ref[j] += _dot_tn(y_attn, da[:, cols])
            dwbp_ref[j] += _dot_tn(y_pool, dp[:, cols])
            pa = _dot_nt(da[:, cols], wba_ref[j])
            pp = _dot_nt(dp[:, cols], wbp_ref[j])
            dy_attn = pa if dy_attn is None else dy_attn + pa
            dy_pool = pp if dy_pool is None else dy_pool + pp

        do = dy_attn * silu_a
        do_ref[...] = do
        dga_ref[...] = (dy_attn * o * (sga * (1.0 + ga * (1.0 - sga)))).astype(BF16)
        doo = do * o
        for hd in range(MLA_HEADS):
            dl = jnp.sum(doo[:, hd * V_HEAD_DIM:(hd + 1) * V_HEAD_DIM], axis=1, keepdims=True)
            dl_ref[:, hd * HEAD_PAD:(hd + 1) * HEAD_PAD] = jnp.broadcast_to(dl, (tm, HEAD_PAD))

        dyp = dy_pool * silu_p
        dgp_ref[...] = (dy_pool * yp * (sgp * (1.0 + gp * (1.0 - sgp)))).astype(BF16)
        dps_ref[...] += _colsum(dyp * dm)
        dmm = (dyp * ps).astype(BF16)
        for g in range(len(POOL_WINDOWS)):
            cols = slice(g * POOL_GROUP_DIM, (g + 1) * POOL_GROUP_DIM)
            dpw_ref[g] += _dot_tn(d_bf[g], dmm[:, cols])
            ddc_ref[:, cols] = _dot_nt(dmm[:, cols], pw_ref[g]) * inv_cnt[g]

        @pl.when(i == n_tiles - 1)
        def _():
            dwout_out[...] = dwout_ref[...].astype(BF16)
            dwba_out[...] = dwba_ref[...].astype(BF16)
            dwbp_out[...] = dwbp_ref[...].astype(BF16)
            _pack_into(small_ref, SMALL_MID, dict(pool_w=dpw_ref[...], norm_final=dgf_ref[...], sq_err=loss_ref[...],
                                                  pool_scale=dps_ref[...]))

    row_in = lambda n: _row_spec(tm, n)
    in_specs = [
        row_in(MLA_WIDTH),
        pl.BlockSpec((1, tm, SHARD_COLS), lambda i: (0, i, 0)), pl.BlockSpec((1, tm, SHARD_COLS), lambda i: (1, i, 0)),
        pl.BlockSpec((1, POOL_HALO, SHARD_COLS), lambda i: (1, jnp.maximum(i * halo_per_tile - 1, 0), 0)),
        pl.BlockSpec((1, tm, SHARD_COLS), lambda i: (2, i, 0)), pl.BlockSpec((1, tm, SHARD_COLS), lambda i: (3, i, 0)),
        row_in(D_MODEL), row_in(D_MODEL),
        _full_spec((4, POOL_GROUP_DIM, POOL_GROUP_DIM)), _full_spec((1, POOL_WIDTH)),
        _full_spec((N_CHIPS, MLA_WIDTH, BRANCH_COLS)), _full_spec((N_CHIPS, POOL_WIDTH, BRANCH_COLS)),
        _full_spec((D_MODEL, D_MODEL)), _full_spec((1, D_MODEL)),
    ]
    out_shape = [
        jax.ShapeDtypeStruct((s, MLA_WIDTH), F32),
        jax.ShapeDtypeStruct((s, MLA_HEADS * HEAD_PAD), F32),
        jax.ShapeDtypeStruct((s, MLA_WIDTH), BF16),
        jax.ShapeDtypeStruct((s, POOL_WIDTH), BF16),
        jax.ShapeDtypeStruct((s, 2 * D_MODEL), BF16),
        jax.ShapeDtypeStruct((s, POOL_WIDTH), F32),
        jax.ShapeDtypeStruct((s, D_MODEL), F32),
        jax.ShapeDtypeStruct((small_rows, LANES), F32),
        jax.ShapeDtypeStruct((D_MODEL, D_MODEL), BF16),
        jax.ShapeDtypeStruct((N_CHIPS, MLA_WIDTH, BRANCH_COLS), BF16),
        jax.ShapeDtypeStruct((N_CHIPS, POOL_WIDTH, BRANCH_COLS), BF16),
    ]
    out_specs = [
        row_in(MLA_WIDTH), row_in(MLA_HEADS * HEAD_PAD), row_in(MLA_WIDTH), row_in(POOL_WIDTH),
        row_in(2 * D_MODEL), row_in(POOL_WIDTH), row_in(D_MODEL),
        _full_spec((small_rows, LANES)), _full_spec((D_MODEL, D_MODEL)),
        _full_spec((N_CHIPS, MLA_WIDTH, BRANCH_COLS)), _full_spec((N_CHIPS, POOL_WIDTH, BRANCH_COLS)),
    ]
    return pl.pallas_call(
        body,
        name="mid",
        grid=(n_tiles,),
        in_specs=in_specs,
        out_specs=out_specs,
        out_shape=out_shape,
        scratch_shapes=[
            pltpu.VMEM((tm + POOL_HALO, POOL_WIDTH), F32),
            pltpu.VMEM((D_MODEL, D_MODEL), F32),
            pltpu.VMEM((N_CHIPS, MLA_WIDTH, BRANCH_COLS), F32),
            pltpu.VMEM((N_CHIPS, POOL_WIDTH, BRANCH_COLS), F32),
            pltpu.VMEM((8, LANES), F32),
            pltpu.VMEM((4, POOL_GROUP_DIM, POOL_GROUP_DIM), F32),
            pltpu.VMEM((1, POOL_WIDTH), F32),
            pltpu.VMEM((1, D_MODEL), F32),
        ],
        compiler_params=pltpu.CompilerParams(dimension_semantics=("arbitrary",), vmem_limit_bytes=VMEM_LIMIT),
    )(o, z_sh, z_sh, z_sh, z_sh, z_sh, x, target, pool_w, pool_scale, w_ba, w_bp, w_out, norm_final)


def _attn_bwd(q, q_t, k, v, do, lse, delta, late_grads, gs_mid, t):
    s = q.shape[0]
    groups = MLA_HEADS // BWD_HEADS
    n_q = s // t
    red = _Reduce(COMM_PARAMS[3:])
    n_w = len(red.params)
    small = _SmallSum(gs_mid.shape[0])
    n_red = len(red.scratch)

    def body(q_ref, qt_ref, do_ref, lse_ref, dl_ref, k_ref, v_ref, *rest):
        g_in, gs_ref = rest[:n_w], rest[n_w]
        (dq_ref, dk_ref, dv_ref), g_out, gsum_ref = rest[n_w + 1:n_w + 4], rest[n_w + 4:2 * n_w + 4], rest[2 * n_w + 4]
        scratch = rest[2 * n_w + 5:]
        red.bind(g_in, g_out, scratch[:n_red])
        small.bind(gs_ref, gsum_ref, scratch[n_red:])
        i = pl.program_id(1)
        step_no = pl.program_id(0) * n_q + i

        @pl.when(step_no == 0)
        def _():
            red.start()
            small.start()

        pl.when(step_no == groups * n_q // 2)(red.exchange)
        pl.when(step_no == groups * n_q - 1)(red.finish_sum)

        @pl.when(i == 0)
        def _():
            dk_ref[...] = jnp.zeros_like(dk_ref)
            dv_ref[...] = jnp.zeros_like(dv_ref)

        mask = _chunk_mask(t, False)
        qcs = [slice(hh * HEAD_PAD, (hh + 1) * HEAD_PAD) for hh in range(BWD_HEADS)]
        vcs = [slice(hh * V_HEAD_DIM, (hh + 1) * V_HEAD_DIM) for hh in range(BWD_HEADS)]
        qhs = [q_ref[:, qc] for qc in qcs]
        qts = [qt_ref[qc, :] for qc in qcs]
        dohs = [do_ref[:, vc].astype(BF16) for vc in vcs]
        do_t = do_ref[...].T.astype(BF16)
        dots = [do_t[vc, :] for vc in vcs]
        lses = [jnp.tile(lse_ref[:, qc], (1, t // HEAD_PAD)) for qc in qcs]
        dls = [jnp.tile(dl_ref[:, qc], (1, t // HEAD_PAD)) for qc in qcs]

        def step(j, dqs, masked):
            keys = pl.ds(pl.multiple_of(j * t, t), t)
            out = []
            for hh in range(BWD_HEADS):
                kj = k_ref[keys, qcs[hh]]
                vj = v_ref[keys, vcs[hh]]
                p = jnp.exp2(_dot_nt(qhs[hh], kj) * ATT_SCALE_LOG2E - lses[hh])
                if masked:
                    p = jnp.where(mask, p, 0.0)
                ds = (p * (_dot_nt(dohs[hh], vj) - dls[hh])).astype(BF16)
                dv_ref[vcs[hh], keys] += _dot(dots[hh], p.astype(BF16))
                dk_ref[qcs[hh], keys] += _dot(qts[hh], ds) * ATT_SCALE
                out.append(dqs[hh] + _dot(ds, kj))
            return tuple(out)

        zero = jnp.zeros((t, HEAD_PAD), F32)
        dqs = lax.fori_loop(0, i, functools.partial(step, masked=False), (zero,) * BWD_HEADS)
        dqs = step(i, dqs, True)
        for hh in range(BWD_HEADS):
            dq_ref[:, qcs[hh]] = dqs[hh] * ATT_SCALE

        @pl.when(step_no == groups * n_q - 1)
        def _():
            red.finish_wait()
            small.finish()

    hw = MLA_HEADS * HEAD_PAD
    any_spec = pl.BlockSpec(memory_space=pl.ANY)
    out = pl.pallas_call(
        body,
        name="attn_bwd",
        grid=(groups, n_q),
        in_specs=[
            pl.BlockSpec((t, BWD_HEADS * HEAD_PAD), lambda p, i: (i, p)),
            pl.BlockSpec((BWD_HEADS * HEAD_PAD, t), lambda p, i: (p, i)),
            pl.BlockSpec((t, BWD_HEADS * V_HEAD_DIM), lambda p, i: (i, p)),
            pl.BlockSpec((t, BWD_HEADS * HEAD_PAD), lambda p, i: (i, p)),
            pl.BlockSpec((t, BWD_HEADS * HEAD_PAD), lambda p, i: (i, p)),
            pl.BlockSpec((s, BWD_HEADS * HEAD_PAD), lambda p, i: (0, p), pipeline_mode=pl.Buffered(1)),
            pl.BlockSpec((s, BWD_HEADS * V_HEAD_DIM), lambda p, i: (0, p), pipeline_mode=pl.Buffered(1)),
        ] + [any_spec] * n_w + [pl.BlockSpec(small.spec_shape, lambda p, i: (0, 0))],
        out_specs=[
            pl.BlockSpec((t, BWD_HEADS * HEAD_PAD), lambda p, i: (i, p)),
            pl.BlockSpec((BWD_HEADS * HEAD_PAD, s), lambda p, i: (p, 0)),
            pl.BlockSpec((BWD_HEADS * V_HEAD_DIM, s), lambda p, i: (p, 0)),
        ] + [any_spec] * n_w + [pl.BlockSpec(small.spec_shape, lambda p, i: (0, 0))],
        out_shape=[jax.ShapeDtypeStruct((s, hw), F32), jax.ShapeDtypeStruct((hw, s), F32),
                   jax.ShapeDtypeStruct((MLA_WIDTH, s), F32)] + red.out_shape + [small.out_shape],
        scratch_shapes=red.scratch + small.scratch,
        compiler_params=pltpu.CompilerParams(dimension_semantics=("arbitrary", "arbitrary"),
                                             vmem_limit_bytes=VMEM_LIMIT),
    )(q, q_t, do, lse, delta, k, v, *late_grads, gs_mid)
    return out[0], out[1], out[2], out[3:3 + n_w], out[3 + n_w]


def _qkv_bwd(dq, dk_t, dv_t, z_sh, q_norm, kv_norm, wuq_p, wk_p, wv, rc, rsa, rsb, tm):
    s = z_sh.shape[1]
    hw = MLA_HEADS * HEAD_PAD
    n_tiles = s // tm
    uq_shape, ukv_shape = (N_CHIPS,) + COMM_PARAMS[1][1:3], (N_CHIPS,) + COMM_PARAMS[2][1:3]

    def body(dq_ref, dk_ref, dv_ref, z_ref, gq_ref, gkv_ref, wuq_ref, wk_ref, wv_ref,
             c_ref, sa_ref, sb_ref,
             dzq_ref, dzkv_ref, dzkr_ref, duq_ref, dukv_ref, dgq_ref, dgkv_ref, dwuq_ref, dwk_ref, dwv_ref):
        i = pl.program_id(0)

        @pl.when(i == 0)
        def _():
            dwuq_ref[...] = jnp.zeros_like(dwuq_ref)
            dwk_ref[...] = jnp.zeros_like(dwk_ref)
            dwv_ref[...] = jnp.zeros_like(dwv_ref)
            dgq_ref[...] = jnp.zeros_like(dgq_ref)
            dgkv_ref[...] = jnp.zeros_like(dgkv_ref)

        c, sa, sb = c_ref[...], sa_ref[...], sb_ref[...]
        gq, gkv = _as_one_row(gq_ref[...]), gkv_ref[...]

        z0 = z_ref[0]
        cq, xq, rq = _rms_fwd(z0[:, ZQ_COLS], gq)
        dqp = jnp.concatenate(
            [_unrope(dq_ref[:, h * HEAD_PAD:(h + 1) * HEAD_PAD], c, sa, sb) for h in range(MLA_HEADS)],
            axis=1).astype(BF16)
        dwuq_ref[...] += _dot_tn(cq.astype(BF16), dqp)
        dcq = _dot_nt(dqp, wuq_ref[...])
        dgq_ref[...] += _colsum(dcq * xq)
        dzq_ref[...] = _rms_bwd(dcq, xq, rq, gq).astype(BF16)

        ckv, xkv, rkv = _rms_fwd(z0[:, ZKV_COLS], gkv)
        ckv = ckv.astype(BF16)
        dkf = dk_ref[...].T
        dk_bf = dkf.astype(BF16)
        dv_bf = dv_ref[...].T.astype(BF16)
        dwk_ref[...] += _dot_tn(ckv, dk_bf)
        dwv_ref[...] += _dot_tn(ckv, dv_bf)
        dckv = _dot_nt(dk_bf, wk_ref[...]) + _dot_nt(dv_bf, wv_ref[...])
        dgkv_ref[...] += _colsum(dckv * xkv)
        dzkv_ref[...] = _rms_bwd(dckv, xkv, rkv, gkv).astype(BF16)

        dkr = dkf[:, 0:HEAD_PAD]
        for h in range(1, MLA_HEADS):
            dkr = dkr + dkf[:, h * HEAD_PAD:(h + 1) * HEAD_PAD]
        dkr = pltpu.roll(_unrope(dkr, c, sa, sb), 64, 1)
        lane = lax.broadcasted_iota(jnp.int32, (tm, HEAD_PAD), 1)
        dzkr_ref[...] = jnp.where(lane < QK_ROPE_DIM, dkr, 0.0).astype(BF16)

        @pl.when(i == n_tiles - 1)
        def _():
            qk = QK_NOPE_DIM + QK_ROPE_DIM
            d_uq = jnp.concatenate([dwuq_ref[:, h * HEAD_PAD:h * HEAD_PAD + qk] for h in range(MLA_HEADS)],
                                   axis=1).astype(BF16)
            d_ukv = jnp.concatenate(
                [part for h in range(MLA_HEADS)
                 for part in (dwk_ref[:, h * HEAD_PAD:h * HEAD_PAD + QK_NOPE_DIM],
                              dwv_ref[:, h * V_HEAD_DIM:(h + 1) * V_HEAD_DIM])], axis=1).astype(BF16)
            for j in range(N_CHIPS):
                duq_ref[j] = d_uq[j * uq_shape[1]:(j + 1) * uq_shape[1]]
                dukv_ref[j] = d_ukv[j * ukv_shape[1]:(j + 1) * ukv_shape[1]]

    return pl.pallas_call(
        body,
        name="qkv_bwd",
        grid=(s // tm,),
        in_specs=[
            _row_spec(tm, hw), pl.BlockSpec((hw, tm), lambda i: (0, i)), pl.BlockSpec((MLA_WIDTH, tm), lambda i: (0, i)),
            pl.BlockSpec((1, tm, SHARD_COLS), lambda i: (0, i, 0)),
            _full_spec(q_norm.shape), _full_spec((1, KV_LORA_RANK)),
            _full_spec((Q_LORA_RANK, hw)), _full_spec((KV_LORA_RANK, hw)), _full_spec((KV_LORA_RANK, MLA_WIDTH)),
            _row_spec(tm, HEAD_PAD), _row_spec(tm, HEAD_PAD), _row_spec(tm, HEAD_PAD),
        ],
        out_specs=[
            _row_spec(tm, Q_LORA_RANK), _row_spec(tm, KV_LORA_RANK), _row_spec(tm, HEAD_PAD),
            _full_spec(uq_shape), _full_spec(ukv_shape),
            _full_spec((1, Q_LORA_RANK)), _full_spec((1, KV_LORA_RANK)),
        ],
        out_shape=[
            jax.ShapeDtypeStruct((s, Q_LORA_RANK), BF16), jax.ShapeDtypeStruct((s, KV_LORA_RANK), BF16),
            jax.ShapeDtypeStruct((s, HEAD_PAD), BF16),
            jax.ShapeDtypeStruct(uq_shape, BF16), jax.ShapeDtypeStruct(ukv_shape, BF16),
            jax.ShapeDtypeStruct((1, Q_LORA_RANK), F32), jax.ShapeDtypeStruct((1, KV_LORA_RANK), F32),
        ],
        scratch_shapes=[pltpu.VMEM((Q_LORA_RANK, hw), F32), pltpu.VMEM((KV_LORA_RANK, hw), F32),
                        pltpu.VMEM((KV_LORA_RANK, MLA_WIDTH), F32)],
        compiler_params=pltpu.CompilerParams(dimension_semantics=("arbitrary",), vmem_limit_bytes=VMEM_LIMIT),
    )(dq, dk_t, dv_t, z_sh, q_norm, kv_norm, wuq_p, wk_p, wv, rc, rsa, rsb)


def _inproj_bwd_x(dzq, dzkv, dzkr, dgattn, ddc, dgpool, dgmerge, x, dh, norm_in, d_q_norm, d_kv_norm, w_in_t, tm):
    s = x.shape[0]
    n_tiles = s // tm
    halo_per_tile = tm // POOL_HALO
    n_halo = s // POOL_HALO
    u_seg = 4
    small_rows = _small_rows(SMALL_LATE)

    def body(dzq_ref, dzkv_ref, dzkr_ref, dga_ref, ddc_ref, ddn_ref, dgp_ref, dgm_ref, x_ref, dh_ref,
             g_ref, dgq_ref, dgkv_ref, w_hbm, gx_ref, small_ref, dzs_ref, w_vmem, dbuf, sem, dgin_ref):
        i = pl.program_id(0)

        @pl.when(i == 0)
        def _():
            cp = pltpu.make_async_copy(w_hbm, w_vmem, sem)
            cp.start()
            dgin_ref[...] = jnp.zeros_like(dgin_ref)
            cp.wait()

        dbuf[0:tm, :] = ddc_ref[...]
        dbuf[tm:, :] = jnp.where(i < n_tiles - 1, ddn_ref[...], 0.0)
        row = lax.broadcasted_iota(jnp.int32, (tm, POOL_GROUP_DIM), 0) + i * tm
        du = []
        for g, w in enumerate(POOL_WINDOWS):
            cols = slice(g * POOL_GROUP_DIM, (g + 1) * POOL_GROUP_DIM)
            fsum = dbuf[0:tm, cols]
            for kk in range(1, w):
                fsum = fsum + dbuf[kk:kk + tm, cols]
            du.append(fsum - dbuf[0:tm, cols] * jnp.minimum(row + 1, w).astype(F32))
        du = jnp.concatenate(du, axis=1).astype(BF16)

        dz = [dzq_ref[...], dzkv_ref[...], dzkr_ref[...], dga_ref[...], du, dgp_ref[...], dgm_ref[...]]
        dz = jnp.concatenate([d[:, :w] for d, (w, _) in zip(dz, IN_SEGMENTS)], axis=1)
        for j in range(N_CHIPS):
            dzs_ref[j] = dz[:, j * SHARD_COLS:(j + 1) * SHARD_COLS].T
        dhn = _dot(dz, w_vmem[...])

        g = g_ref[...]
        _, xhat, r = _rms_fwd(x_ref[...], g)
        dgin_ref[...] += _colsum(dhn * xhat)
        gx_ref[...] = dh_ref[...] + _rms_bwd(dhn, xhat, r, g)

        @pl.when(i == n_tiles - 1)
        def _():
            _pack_into(small_ref, SMALL_LATE, dict(norm_in=dgin_ref[...], q_norm=dgq_ref[...], kv_norm=dgkv_ref[...]))

    any_spec = pl.BlockSpec(memory_space=pl.ANY)
    seg_w = [wide for _, wide in IN_SEGMENTS]
    return pl.pallas_call(
        body,
        name="inproj_bwd_x",
        grid=(n_tiles,),
        in_specs=[
            _row_spec(tm, seg_w[0]), _row_spec(tm, seg_w[1]), _row_spec(tm, seg_w[2]),
            _row_spec(tm, seg_w[3]), _row_spec(tm, seg_w[u_seg]),
            pl.BlockSpec((POOL_HALO, POOL_WIDTH), lambda i: (jnp.minimum((i + 1) * halo_per_tile, n_halo - 1), 0)),
            _row_spec(tm, seg_w[5]), _row_spec(tm, seg_w[6]),
            _row_spec(tm, D_MODEL), _row_spec(tm, D_MODEL),
            _full_spec((1, D_MODEL)), _full_spec((1, Q_LORA_RANK)), _full_spec((1, KV_LORA_RANK)), any_spec,
        ],
        out_specs=[_row_spec(tm, D_MODEL), _full_spec((small_rows, LANES)),
                   pl.BlockSpec((N_CHIPS, SHARD_COLS, tm), lambda i: (0, 0, i))],
        out_shape=[jax.ShapeDtypeStruct((s, D_MODEL), F32), jax.ShapeDtypeStruct((small_rows, LANES), F32),
                   jax.ShapeDtypeStruct((N_CHIPS, SHARD_COLS, s), BF16)],
        scratch_shapes=[
            pltpu.VMEM((IN_TOTAL, D_MODEL), BF16),
            pltpu.VMEM((tm + POOL_HALO, POOL_WIDTH), F32),
            pltpu.SemaphoreType.DMA,
            pltpu.VMEM((1, D_MODEL), F32),
        ],
        compiler_params=pltpu.CompilerParams(dimension_semantics=("arbitrary",), vmem_limit_bytes=VMEM_LIMIT),
    )(dzq, dzkv, dzkr, dgattn, ddc, ddc, dgpool, dgmerge, x, dh, norm_in, d_q_norm, d_kv_norm,
      w_in_t.reshape(IN_TOTAL, D_MODEL))


def _inproj_bwd_w(order, dz_sh, hn, g_uq, g_ukv, gs, tm):
    s = hn.shape[0]
    n_tiles = s // tm
    hc = D_MODEL // 2
    red = _Reduce(COMM_PARAMS[1:3])
    small = _SmallSum(gs.shape[0])
    n_red = len(red.scratch)

    def body(order_ref, dz_ref, hn_ref, guq_hbm, gukv_hbm, gs_ref, gw_hbm, guq_out, gukv_out, gsum_ref,
             acc, pm_w, a_w, b_w, r_w, w_send, w_recv, w_local, *more_scratch):
        ph, i = pl.program_id(0), pl.program_id(1)
        x, y, c = lax.axis_index("x"), lax.axis_index("y"), lax.axis_index("c")
        k = 2 * x + y
        me, sibling = (x, y, c), (x, y, 1 - c)
        chips = _other_chips(x, y)
        shard_of_phase = [2 * cx + cy for cx, cy in chips] + [k]
        copy = _remote_copier(w_send, w_recv)
        red.bind([guq_hbm, gukv_hbm], [guq_out, gukv_out], more_scratch[:n_red])
        small.bind(gs_ref, gsum_ref, more_scratch[n_red:])
        mine = pl.ds(pl.multiple_of(c * hc, hc), hc)
        theirs = pl.ds(pl.multiple_of((1 - c) * hc, hc), hc)

        def to_sibling(f):
            j = shard_of_phase[f]
            return copy(f, pm_w.at[j, 1 - c], a_w.at[j], sibling)

        def pair_sum(f):
            cx, cy = chips[f]
            return copy(4 + f, pm_w.at[shard_of_phase[f], c], b_w.at[f], (cx, cy, c))

        def finished():
            return copy(7, r_w, gw_hbm.at[:, mine], sibling)

        @pl.when(jnp.logical_and(ph == 0, i == 0))
        def _():
            red.start()
            small.start()

        part = _dot(dz_ref[0], hn_ref[...])

        @pl.when(i == 0)
        def _():
            acc[...] = part

        @pl.when(i > 0)
        def _():
            acc[...] += part

        for f in range(3):
            @pl.when(jnp.logical_and(ph == f + 1, i == 0))
            def _(f=f):
                j = shard_of_phase[f]
                copy(f, a_w.at[j], a_w.at[j], me).wait_recv()
                pm_w[j, c] = (pm_w[j, c].astype(F32) + a_w[j].astype(F32)).astype(BF16)
                pair_sum(f).start()
                if f == 0:
                    red.exchange()

        for f in range(4):
            @pl.when(jnp.logical_and(ph == f, i == n_tiles - 1))
            def _(f=f):
                j = shard_of_phase[f]
                pm_w[j, 0] = acc[:, :hc].astype(BF16)
                pm_w[j, 1] = acc[:, hc:].astype(BF16)
                to_sibling(f).start()
                if f < 3:
                    return
                copy(3, a_w.at[k], a_w.at[k], me).wait_recv()
                r_w[...] = pm_w[k, c].astype(F32) + a_w[k].astype(F32)
                for g in range(3):
                    copy(4 + g, b_w.at[g], b_w.at[g], me).wait_recv()
                    r_w[...] = r_w[...] + b_w[g].astype(F32)
                store = pltpu.make_async_copy(r_w, gw_hbm.at[:, mine], w_local)
                store.start()
                finished().start()
                red.finish()
                small.finish()
                copy(7, gw_hbm.at[:, theirs], gw_hbm.at[:, theirs], me).wait_recv()
                store.wait()
                for g in range(4):
                    to_sibling(g).wait_send()
                for g in range(3):
                    pair_sum(g).wait_send()
                finished().wait_send()

    any_spec = pl.BlockSpec(memory_space=pl.ANY)
    n_sem = 8
    grid_spec = pltpu.PrefetchScalarGridSpec(
        num_scalar_prefetch=1,
        grid=(N_CHIPS, n_tiles),
        in_specs=[
            pl.BlockSpec((1, SHARD_COLS, tm), lambda ph, i, order: (order[ph], 0, i)),
            pl.BlockSpec((tm, D_MODEL), lambda ph, i, order: (i, 0)),
            any_spec, any_spec,
            pl.BlockSpec(small.spec_shape, lambda ph, i, order: (0, 0)),
        ],
        out_specs=[any_spec, any_spec, any_spec, pl.BlockSpec(small.spec_shape, lambda ph, i, order: (0, 0))],
        scratch_shapes=[
            pltpu.VMEM((SHARD_COLS, D_MODEL), F32),
            pltpu.VMEM((N_CHIPS, 2, SHARD_COLS, hc), BF16),
            pltpu.VMEM((N_CHIPS, SHARD_COLS, hc), BF16),
            pltpu.VMEM((3, SHARD_COLS, hc), BF16),
            pltpu.VMEM((SHARD_COLS, hc), F32),
            pltpu.SemaphoreType.DMA((n_sem,)), pltpu.SemaphoreType.DMA((n_sem,)), pltpu.SemaphoreType.DMA,
        ] + red.scratch + small.scratch,
    )
    out = pl.pallas_call(
        body,
        name="inproj_bwd_w",
        grid_spec=grid_spec,
        out_shape=[jax.ShapeDtypeStruct((SHARD_COLS, D_MODEL), F32)] + red.out_shape
        + [small.out_shape],
        compiler_params=pltpu.CompilerParams(dimension_semantics=("arbitrary", "arbitrary"),
                                             vmem_limit_bytes=VMEM_LIMIT),
    )(order, dz_sh, hn, g_uq, g_ukv, gs)
    return out[0], out[1], out[2], out[3]


def _other_chips(x, y):
    return ((1 - x, 1 - y), (1 - x, y), (x, 1 - y))


def _half(ref, axis, size, c, lead=()):
    window = pl.ds(pl.multiple_of(c * size, size), size)
    if axis == 0:
        return ref.at[(*lead, window, slice(None))]
    return ref.at[(*lead, slice(None), window)]


def _half_shape(rows, cols, axis, size):
    return (size, cols) if axis == 0 else (rows, size)


def _remote_copier(send_sems, recv_sems):
    def copy(sem, src, dst, to):
        return pltpu.make_async_remote_copy(src_ref=src, dst_ref=dst, send_sem=send_sems.at[sem],
                                            recv_sem=recv_sems.at[sem], device_id=to, device_id_type=MESH)
    return copy


class _Gather:
    def __init__(self, params):
        self.params = params
        n = len(params)
        self.scratch = [pltpu.SemaphoreType.DMA((6 * n,)), pltpu.SemaphoreType.DMA((6 * n,)),
                        pltpu.SemaphoreType.DMA((n,))]
        self.out_shape = [jax.ShapeDtypeStruct((N_CHIPS, r, cc), BF16) for _, r, cc, _, _ in params]

    def bind(self, ins, outs, scratch):
        self.ins, self.outs = ins, outs
        send_sems, recv_sems, self.local_sems = scratch
        self.copy = _remote_copier(send_sems, recv_sems)
        self.x, self.y, self.c = lax.axis_index("x"), lax.axis_index("y"), lax.axis_index("c")
        self.k = 2 * self.x + self.y
        self.chips = _other_chips(self.x, self.y)

    def _local(self, p):
        return pltpu.make_async_copy(self.ins[p], self.outs[p].at[self.k], self.local_sems.at[p])

    def _first(self, p, j):
        _, _, _, axis, size = self.params[p]
        cx, cy = self.chips[j]
        return self.copy(6 * p + j, _half(self.ins[p], axis, size, self.c),
                         _half(self.outs[p], axis, size, self.c, (self.k,)), (cx, cy, self.c))

    def _relay(self, p, j, half_of):
        _, _, _, axis, size = self.params[p]
        cx, cy = self.chips[j]
        block = _half(self.outs[p], axis, size, half_of, (2 * cx + cy,))
        return self.copy(6 * p + 3 + j, block, block, (self.x, self.y, 1 - self.c))

    def start(self):
        for p in range(len(self.params)):
            self._local(p).start()
            for j in (1, 2, 0):
                self._first(p, j).start()

    def relay_one(self, p, j):
        _, _, _, axis, size = self.params[p]
        cx, cy = self.chips[j]
        landed = _half(self.outs[p], axis, size, self.c, (2 * cx + cy,))
        self.copy(6 * p + j, landed, landed, (self.x, self.y, self.c)).wait_recv()
        self._relay(p, j, self.c).start()

    def await_one(self, p, j):
        self._relay(p, j, 1 - self.c).wait_recv()

    def wait_sends(self):
        for p in range(len(self.params)):
            for j in range(3):
                self._first(p, j).wait_send()
                self._relay(p, j, self.c).wait_send()
            self._local(p).wait()

    def relay(self):
        for j in range(3):
            for p in range(len(self.params)):
                self.relay_one(p, j)

    def finish(self):
        for j in range(3):
            for p in range(len(self.params)):
                self.await_one(p, j)
        self.wait_sends()


class _Reduce:
    def __init__(self, params):
        self.params = params
        n = len(params)
        halves = [_half_shape(r, cc, axis, size) for _, r, cc, axis, size in params]
        self.scratch = ([pltpu.VMEM((N_CHIPS, *h), BF16) for h in halves]
                        + [pltpu.VMEM((N_CHIPS, *h), BF16) for h in halves]
                        + [pltpu.VMEM((3, *h), BF16) for h in halves]
                        + [pltpu.VMEM(h, F32) for h in halves]
                        + [pltpu.SemaphoreType.DMA((5 * n,)), pltpu.SemaphoreType.DMA((5 * n,)),
                           pltpu.SemaphoreType.DMA((2 * n,))])
        self.out_shape = [jax.ShapeDtypeStruct((r, cc), F32) for _, r, cc, _, _ in params]

    def bind(self, g_in, g_out, scratch):
        n = len(self.params)
        self.g_in, self.g_out = g_in, g_out
        self.pm, self.a_buf = scratch[0:n], scratch[n:2 * n]
        self.b_buf, self.r_buf = scratch[2 * n:3 * n], scratch[3 * n:4 * n]
        send_sems, recv_sems, self.local_sems = scratch[4 * n:]
        self.copy = _remote_copier(send_sems, recv_sems)
        self.x, self.y, self.c = lax.axis_index("x"), lax.axis_index("y"), lax.axis_index("c")
        self.k = 2 * self.x + self.y
        self.chips = _other_chips(self.x, self.y)
        self.me = (self.x, self.y, self.c)
        self.sibling = (self.x, self.y, 1 - self.c)

    def _load(self, p):
        _, _, _, axis, size = self.params[p]
        return pltpu.make_async_copy(_half(self.g_in[p], axis, size, self.c, (slice(None),)), self.pm[p],
                                     self.local_sems.at[p])

    def _to_sibling(self, p):
        _, _, _, axis, size = self.params[p]
        return self.copy(5 * p, _half(self.g_in[p], axis, size, 1 - self.c, (slice(None),)), self.a_buf[p],
                         self.sibling)

    def _pair_sum(self, p, j):
        cx, cy = self.chips[j]
        return self.copy(5 * p + 1 + j, self.pm[p].at[2 * cx + cy], self.b_buf[p].at[j], (cx, cy, self.c))

    def _store(self, p):
        _, _, _, axis, size = self.params[p]
        n = len(self.params)
        return pltpu.make_async_copy(self.r_buf[p], _half(self.g_out[p], axis, size, self.c),
                                     self.local_sems.at[n + p])

    def _finished(self, p):
        _, _, _, axis, size = self.params[p]
        return self.copy(5 * p + 4, self.r_buf[p], _half(self.g_out[p], axis, size, self.c), self.sibling)

    def start(self):
        for p in range(len(self.params)):
            self._load(p).start()
            self._to_sibling(p).start()

    def exchange(self):
        for p in range(len(self.params)):
            self._load(p).wait()
            self.copy(5 * p, self.a_buf[p], self.a_buf[p], self.me).wait_recv()
            for j, (cx, cy) in enumerate(self.chips):
                kj = 2 * cx + cy
                self.pm[p][kj] = (self.pm[p][kj].astype(F32) + self.a_buf[p][kj].astype(F32)).astype(BF16)
                self._pair_sum(p, j).start()
            self.r_buf[p][...] = self.pm[p][self.k].astype(F32) + self.a_buf[p][self.k].astype(F32)

    def finish(self):
        self.finish_sum()
        self.finish_wait()

    def finish_sum(self):
        for p in range(len(self.params)):
            for j in range(3):
                self.copy(5 * p + 1 + j, self.b_buf[p].at[j], self.b_buf[p].at[j], self.me).wait_recv()
                self.r_buf[p][...] = self.r_buf[p][...] + self.b_buf[p][j].astype(F32)
            self._store(p).start()
            self._finished(p).start()

    def finish_wait(self):
        for p, (_, _, _, axis, size) in enumerate(self.params):
            theirs = _half(self.g_out[p], axis, size, 1 - self.c)
            self.copy(5 * p + 4, theirs, theirs, self.me).wait_recv()
            self._store(p).wait()
            self._to_sibling(p).wait_send()
            for j in range(3):
                self._pair_sum(p, j).wait_send()
            self._finished(p).wait_send()


class _SmallSum:
    def __init__(self, rows):
        self.rows = rows
        self.scratch = [pltpu.VMEM((N_DEV, rows, LANES), F32),
                        pltpu.SemaphoreType.DMA((N_DEV - 1,)), pltpu.SemaphoreType.DMA((N_DEV - 1,))]
        self.out_shape = jax.ShapeDtypeStruct((rows, LANES), F32)
        self.spec_shape = (rows, LANES)

    def bind(self, src, dst, scratch):
        self.src, self.dst = src, dst
        self.buf, send_sems, recv_sems = scratch
        self.copy = _remote_copier(send_sems, recv_sems)
        self.x, self.y, self.c = lax.axis_index("x"), lax.axis_index("y"), lax.axis_index("c")

    def _send(self, f):
        fx, fy, fc = [(a, b, d) for a in (0, 1) for b in (0, 1) for d in (0, 1)][f]
        x, y, c = self.x, self.y, self.c
        peer = (1 - x if fx else x, 1 - y if fy else y, 1 - c if fc else c)
        return self.copy(f - 1, self.src, self.buf.at[f], peer)

    def start(self):
        for f in range(1, N_DEV):
            self._send(f).start()
        self.buf[0] = self.src[...]

    def finish(self):
        me = (self.x, self.y, self.c)
        for f in range(1, N_DEV):
            self.copy(f - 1, self.buf.at[f], self.buf.at[f], me).wait_recv()
        dev = 4 * self.x + 2 * self.y + self.c
        total = self.buf[dev]
        for d in range(1, N_DEV):
            total = total + self.buf[jnp.bitwise_xor(dev, d)]
        self.dst[...] = total
        for f in range(1, N_DEV):
            self._send(f).wait_send()


def _adamw_math(w, g, m, v):
    m = ADAM_B1 * m + (1.0 - ADAM_B1) * g
    v = ADAM_B2 * v + (1.0 - ADAM_B2) * (g * g)
    m_hat = m / (1.0 - ADAM_B1 ** ADAM_STEP)
    v_hat = v / (1.0 - ADAM_B2 ** ADAM_STEP)
    delta = -ADAM_LR * (m_hat / (jnp.sqrt(v_hat) + ADAM_EPS) + ADAM_WD * w)
    return delta, m, v


def _adamw_tiled(w, g, m, v, tm):
    rows, cols = w.shape

    def body(w_ref, g_ref, m_ref, v_ref, d_ref, nm_ref, nv_ref, g_out):
        g = g_ref[...]
        d_ref[...], nm_ref[...], nv_ref[...] = _adamw_math(w_ref[...], g, m_ref[...], v_ref[...])
        g_out[...] = g

    spec = _row_spec(tm, cols)
    return pl.pallas_call(
        body,
        name="adamw_w_in",
        grid=(rows // tm,),
        in_specs=[spec] * 4,
        out_specs=[spec] * 4,
        out_shape=[jax.ShapeDtypeStruct(w.shape, F32)] * 4,
        compiler_params=pltpu.CompilerParams(dimension_semantics=("parallel",), vmem_limit_bytes=VMEM_LIMIT),
    )(w, g, m, v)


def _adamw_many(ws, gs, ms, vs, sq_err):
    n = len(ws)
    gs = list(gs) + [sq_err]
    g_arrays, g_at = [], []
    for g in gs:
        arr, row = g if isinstance(g, tuple) else (g, None)
        k = next((j for j, a in enumerate(g_arrays) if a is arr), len(g_arrays))
        if k == len(g_arrays):
            g_arrays.append(arr)
        g_at.append((k, row))
    n_g = len(g_arrays)

    def body(*refs):
        w_refs, m_refs, v_refs, g_refs, outs = (refs[:n], refs[n:2 * n], refs[2 * n:3 * n], refs[3 * n:3 * n + n_g],
                                                refs[3 * n + n_g:])
        k, row = g_at[n]
        sq = g_refs[k][...] if row is None else g_refs[k][row:row + 8, :]
        outs[4 * n][...] = jnp.full((1, 1), 0.5 * jnp.sum(sq) / D_MODEL, F32)
        for i in range(n):
            k, row = g_at[i]
            if row is None and g_refs[k].ndim == 2 and w_refs[i].ndim == 3:
                cols = ws[i].shape[2]
                for h in range(ws[i].shape[1]):
                    outs[3 * n + i][:, h, :] = g_refs[k][:, h * cols:(h + 1) * cols]
                g = outs[3 * n + i][...]
            else:
                g = g_refs[k][...] if row is None else g_refs[k][row:row + ws[i].shape[0], :]
                outs[3 * n + i][...] = g
            d, nm, nv = _adamw_math(w_refs[i][...], g, m_refs[i][...], v_refs[i][...])
            outs[i][...] = d
            outs[n + i][...] = nm
            outs[2 * n + i][...] = nv

    vmem_spec = pl.BlockSpec(memory_space=pltpu.VMEM)
    shapes = [jax.ShapeDtypeStruct(w.shape, F32) for w in ws]
    out = pl.pallas_call(
        body,
        name="adamw_small",
        in_specs=[vmem_spec] * (3 * n + n_g),
        out_specs=[vmem_spec] * (4 * n + 1),
        out_shape=shapes * 4 + [jax.ShapeDtypeStruct((1, 1), F32)],
        compiler_params=pltpu.CompilerParams(vmem_limit_bytes=VMEM_LIMIT),
    )(*ws, *ms, *vs, *g_arrays)
    return out[:n], out[n:2 * n], out[2 * n:3 * n], out[3 * n:4 * n], out[4 * n]


def _rope_tables(s):
    half = QK_ROPE_DIM // 2
    inv_freq = np.float32(ROPE_THETA) ** (-np.arange(half, dtype=np.float32) / np.float32(half))
    ang = (np.arange(s, dtype=np.float32)[:, None] * inv_freq[None, :]).astype(np.float32)
    cos, sin = np.cos(ang.astype(np.float64)).astype(np.float32), np.sin(ang.astype(np.float64)).astype(np.float32)
    z16 = np.zeros((s, half), np.float32)
    z32 = np.zeros((s, HEAD_PAD - QK_NOPE_DIM - QK_ROPE_DIM), np.float32)
    z64 = np.zeros((s, QK_NOPE_DIM), np.float32)
    rc = np.concatenate([np.ones((s, QK_NOPE_DIM), np.float32), cos, cos, z32], axis=1)
    rsa = np.concatenate([z64, -sin, z16, z32], axis=1)
    rsb = np.concatenate([z64, z16, sin, z32], axis=1)
    return jnp.asarray(rc), jnp.asarray(rsa), jnp.asarray(rsb)


def kernel(x, norm_in, w_in, q_norm, w_uq, kv_norm, w_ukv, pool_w, pool_scale, w_branch_attn, w_branch_pool, w_out, norm_final, loss_target, m_norm_in, m_w_in, m_q_norm, m_w_uq, m_kv_norm, m_w_ukv, m_pool_w, m_pool_scale, m_w_branch_attn, m_w_branch_pool, m_w_out, m_norm_final, v_norm_in, v_w_in, v_q_norm, v_w_uq, v_kv_norm, v_w_ukv, v_pool_w, v_pool_scale, v_w_branch_attn, v_w_branch_pool, v_w_out, v_norm_final):
    s = x.shape[1]
    t_att, t_row = _tiles(s)
    x2 = x.reshape(s, D_MODEL)
    tgt = loss_target.reshape(s, D_MODEL)

    cx, cy = lax.axis_index("x"), lax.axis_index("y")
    others = [2 * ox + oy for ox, oy in _other_chips(cx, cy)]
    hn, z_sh, (w_in_t, w_uq_all, w_ukv_all) = _inproj_fwd(
        jnp.stack([2 * cx + cy, others[1], others[2], others[0]]).astype(jnp.int32), x2, norm_in.reshape(1, -1),
        w_in.T.astype(BF16), [w_uq, w_ukv], 4 * t_row)
    rc, rsa, rsb = _rope_tables(s)
    g_in = norm_in.reshape(1, -1)
    g_q = q_norm.reshape(-1, LANES)
    g_kv = kv_norm.reshape(1, -1)
    g_f = norm_final.reshape(1, -1)
    ps = pool_scale.reshape(1, -1)

    q, k, v, q_t, v_t, wuq_p, wk_p, wv, *late, pw_bf = _qkv_fwd(
        z_sh, g_q, g_kv, w_uq_all, w_ukv_all, rc, rsa, rsb, [w_branch_attn, w_branch_pool, w_out, pool_w], 2 * t_row)
    o, lse, (w_ba_all, w_bp_all, w_out_all) = _attn_fwd(q_t, k, v_t, late, t_att)
    w_out_f = w_out_all.reshape(D_MODEL, D_MODEL)

    do, delta, dgattn, dgpool, dgmerge, ddc, dh, gs_mid, d_w_out, d_w_ba, d_w_bp = _mid(o, z_sh, x2, tgt, pw_bf, ps, w_ba_all, w_bp_all, w_out_f, g_f, t_row)

    late_grads = [d_w_ba, d_w_bp, d_w_out.reshape(N_CHIPS, 256, D_MODEL)]
    dq, dk_t, dv_t, (g_w_ba, g_w_bp, g_w_out), g_small_mid = _attn_bwd(q, q_t, k, v, do, lse, delta, late_grads,
                                                                      gs_mid, t_att)
    dzq, dzkv, dzkr, d_w_uq, d_w_ukv, d_q_norm, d_kv_norm = _qkv_bwd(
        dq, dk_t, dv_t, z_sh, g_q, g_kv, wuq_p, wk_p, wv, rc, rsa, rsb, 2 * t_row)
    grad_x, gs, dz_sh = _inproj_bwd_x(dzq, dzkv, dzkr, dgattn, ddc, dgpool, dgmerge, x2, dh, g_in, d_q_norm, d_kv_norm,
                                      w_in_t, 2 * t_row)

    order = jnp.stack(others + [2 * cx + cy]).astype(jnp.int32)
    g_w_in_t, g_w_uq, g_w_ukv, g_small = _inproj_bwd_w(
        order, dz_sh, hn, d_w_uq, d_w_ukv, gs, 4 * t_row)

    dl_w_in, nm_w_in, nv_w_in, g_w_in = (a.T for a in _adamw_tiled(w_in.T, g_w_in_t, m_w_in.T, v_w_in.T, 152))

    packed = {n: (g_small_mid, r) for n, r in _first_rows(SMALL_MID).items() if n != "sq_err"}
    packed.update({n: (g_small, r) for n, r in _first_rows(SMALL_LATE).items()})

    def as_rows(n, a):
        return a.reshape(-1, LANES) if n in packed else a

    names = ["norm_in", "q_norm", "w_uq", "kv_norm", "w_ukv", "pool_w", "pool_scale", "w_branch_attn",
             "w_branch_pool", "w_out", "norm_final"]
    ws = dict(norm_in=norm_in, q_norm=q_norm, w_uq=w_uq, kv_norm=kv_norm, w_ukv=w_ukv, pool_w=pool_w,
              pool_scale=pool_scale, w_branch_attn=w_branch_attn, w_branch_pool=w_branch_pool, w_out=w_out,
              norm_final=norm_final)
    gsd = dict(packed, w_uq=g_w_uq, w_ukv=g_w_ukv, w_branch_attn=g_w_ba, w_branch_pool=g_w_bp, w_out=g_w_out)
    msd = dict(norm_in=m_norm_in, q_norm=m_q_norm, w_uq=m_w_uq, kv_norm=m_kv_norm, w_ukv=m_w_ukv, pool_w=m_pool_w,
               pool_scale=m_pool_scale, w_branch_attn=m_w_branch_attn, w_branch_pool=m_w_branch_pool, w_out=m_w_out,
               norm_final=m_norm_final)
    vsd = dict(norm_in=v_norm_in, q_norm=v_q_norm, w_uq=v_w_uq, kv_norm=v_kv_norm, w_ukv=v_w_ukv, pool_w=v_pool_w,
               pool_scale=v_pool_scale, w_branch_attn=v_w_branch_attn, w_branch_pool=v_w_branch_pool, w_out=v_w_out,
               norm_final=v_norm_final)
    dls, nms, nvs, g_outs, loss = _adamw_many(
        [as_rows(n, ws[n]) for n in names], [gsd[n] for n in names], [as_rows(n, msd[n]) for n in names],
        [as_rows(n, vsd[n]) for n in names], (g_small_mid, _first_rows(SMALL_MID)["sq_err"]))

    grads = dict(zip(names, g_outs))
    grads["w_in"] = g_w_in
    delta_w = {n: d.reshape(ws[n].shape) for n, d in zip(names, dls)}
    new_m = {n: d.reshape(ws[n].shape) for n, d in zip(names, nms)}
    new_v = {n: d.reshape(ws[n].shape) for n, d in zip(names, nvs)}
    delta_w["w_in"], new_m["w_in"], new_v["w_in"] = dl_w_in, nm_w_in, nv_w_in
    ws["w_in"] = w_in

    order = ["norm_in", "w_in", "q_norm", "w_uq", "kv_norm", "w_ukv", "pool_w", "pool_scale", "w_branch_attn",
             "w_branch_pool", "w_out", "norm_final"]
    return (loss.reshape(()), grad_x.reshape(x.shape),
            *[grads[n].reshape(ws[n].shape) for n in order],
            *[delta_w[n] for n in order], *[new_m[n] for n in order], *[new_v[n] for n in order])
```

```python
import functools

import jax
import jax.numpy as jnp
import numpy as np
from jax import lax
from jax.experimental import pallas as pl
from jax.experimental.pallas import tpu as pltpu

F32 = jnp.float32
BF16 = jnp.bfloat16
MESH = pl.DeviceIdType.MESH

D_MODEL = 1024
CHUNK = 64
MLA_HEADS = 8
QK_NOPE_DIM = 64
QK_ROPE_DIM = 32
V_HEAD_DIM = 64
Q_LORA_RANK = 384
KV_LORA_RANK = 256
MLA_WIDTH = MLA_HEADS * V_HEAD_DIM
ROPE_THETA = 10000.0
POOL_WINDOWS = (2, 4, 8, 16)
POOL_WIDTH = 512
POOL_GROUP_DIM = 128
BRANCH_COLS = D_MODEL // 4
FWD_HEADS = 8
BWD_HEADS = 4
POOL_HALO = 16
EPS = 1e-6
IN_TOTAL = 4256
HEAD_PAD = 128
ATT_SCALE = (QK_NOPE_DIM + QK_ROPE_DIM) ** -0.5
ATT_SCALE_LOG2E = ATT_SCALE * 1.4426950408889634

ADAM_LR = 0.001
ADAM_B1 = 0.9
ADAM_B2 = 0.999
ADAM_EPS = 1e-08
ADAM_WD = 0.01
ADAM_STEP = 10

N_CHIPS = 4
N_DEV = 8
LANES = 128
VMEM_LIMIT = 60 * 1024 * 1024

IN_SEGMENTS = ((384, 384), (256, 256), (32, HEAD_PAD), (512, 512), (512, 512), (512, 512), (2048, 2048))
SHARD_COLS = IN_TOTAL // N_CHIPS
ZQ_COLS = slice(0, 384)
ZKV_COLS = slice(384, 640)
ZKR_TILE = slice(640, 768)


def _shard_pieces():
    bounds, off = [], 0
    for w, _ in IN_SEGMENTS:
        bounds.append((off, off + w))
        off += w
    out = []
    for j in range(N_CHIPS):
        lo, hi = SHARD_COLS * j, SHARD_COLS * (j + 1)
        out.append([(i, max(lo, a) - a, min(hi, b) - a, max(lo, a) - lo)
                    for i, (a, b) in enumerate(bounds) if max(lo, a) < min(hi, b)])
    return out


SHARD_PIECES = _shard_pieces()


def _segment(z_blocks, seg):
    parts = [z_blocks[j][:, col:col + hi - lo]
             for j, pieces in enumerate(SHARD_PIECES) for sg, lo, hi, col in pieces if sg == seg]
    return parts[0] if len(parts) == 1 else jnp.concatenate(parts, axis=1)

COMM_PARAMS = (
    ("w_in", SHARD_COLS, D_MODEL, 1, 512),
    ("w_uq", 96, 768, 0, 48),
    ("w_ukv", 64, 1024, 0, 32),
    ("w_branch_attn", 512, 256, 0, 256),
    ("w_branch_pool", 512, 256, 0, 256),
    ("w_out", 256, 1024, 0, 128),
)

SMALL_MID = (
    ("pool_w", (4, 128, 128)),
    ("norm_final", (1024,)),
    ("sq_err", (8, 128)),
    ("pool_scale", (512,)),
)
SMALL_LATE = (
    ("norm_in", (1024,)),
    ("q_norm", (384,)),
    ("kv_norm", (256,)),
)


def _small_rows(shapes):
    return -(-sum(int(np.prod(s)) for _, s in shapes) // (LANES * 8)) * 8


def _first_rows(shapes):
    out, off = {}, 0
    for name, shp in shapes:
        out[name], rem = divmod(off, LANES)
        assert rem == 0, name
        off += int(np.prod(shp))
    return out


def _pack_into(dst_ref, shapes, values):
    first = _first_rows(shapes)
    for name, shp in shapes:
        v, row = values[name], first[name]
        if v.ndim == 3:
            for g in range(v.shape[0]):
                dst_ref[row + g * v.shape[1]:row + (g + 1) * v.shape[1], :] = v[g]
        elif v.shape[0] == 1 and v.shape[1] > LANES:
            for j in range(v.shape[1] // LANES):
                dst_ref[row + j:row + j + 1, :] = v[:, j * LANES:(j + 1) * LANES]
        else:
            dst_ref[row:row + v.shape[0], :] = v
    used = sum(int(np.prod(shp)) for _, shp in shapes) // LANES
    if used < dst_ref.shape[0]:
        dst_ref[used:, :] = jnp.zeros((dst_ref.shape[0] - used, LANES), dst_ref.dtype)


def _as_one_row(g):
    return jnp.concatenate([g[j:j + 1] for j in range(g.shape[0])], axis=1)


def _dot(a, b):
    return jnp.dot(a, b, preferred_element_type=F32)


def _dot_nt(a, b):
    return lax.dot_general(a, b, (((1,), (1,)), ((), ())), preferred_element_type=F32)


def _dot_tn(a, b):
    return lax.dot_general(a, b, (((0,), (0,)), ((), ())), preferred_element_type=F32)


def _sigmoid(x):
    return 1.0 / (1.0 + jnp.exp(-x))


def _colsum(x):
    return jnp.sum(x, axis=0, keepdims=True)


def _rms_fwd(x, g):
    r = lax.rsqrt(jnp.mean(x * x, axis=-1, keepdims=True) + EPS)
    xhat = x * r
    return xhat * g, xhat, r


def _rms_bwd(dy, xhat, r, g):
    dxhat = dy * g
    return r * (dxhat - xhat * jnp.mean(dxhat * xhat, axis=-1, keepdims=True))


def _rope(v, c, sa, sb):
    return v * c + pltpu.roll(v, 112, 1) * sa + pltpu.roll(v, 16, 1) * sb


def _unrope(d, c, sa, sb):
    return d * c + pltpu.roll(d * sa, 16, 1) + pltpu.roll(d * sb, 112, 1)


def _row_spec(tm, n):
    return pl.BlockSpec((tm, n), lambda i: (i, 0))


def _full_spec(shape):
    nd = len(shape)
    return pl.BlockSpec(shape, lambda i: (0,) * nd)


def _tiles(s):
    t_att = 512 if s >= 2048 else 128
    t_row = 256 if s >= 1024 else 128
    return t_att, t_row


def _inproj_fwd(order, x, norm_in, w_in_shard, up_shards, tm):
    s = x.shape[0]
    n_tiles = s // tm
    gat = _Gather(COMM_PARAMS[:3])
    n_w = len(gat.params)
    arrival = (1, 2, 0)
    n_up = len(up_shards)

    def body(order_ref, x_ref, g_ref, w_in_loc, *rest):
        up_refs, (hn_ref, z_ref), w_all = rest[:n_up], rest[n_up:n_up + 2], rest[n_up + 2:n_up + 2 + n_w]
        rest = rest[n_up + 2 + n_w:]
        (w_vmem, hn_all, w_sem), up_flat = rest[:3], rest[3:3 + n_up]
        gat.bind((w_in_loc,) + tuple(up_flat), w_all, rest[3 + n_up:])
        ph, i = pl.program_id(0), pl.program_id(1)

        @pl.when(jnp.logical_and(ph == 0, i == 0))
        def _():
            for src, dst in zip(up_refs, up_flat):
                dst[...] = jnp.concatenate([src[:, h, :] for h in range(MLA_HEADS)], axis=1).astype(BF16)
            gat.start()

        def fetch(phase):
            src = w_in_loc if phase == 0 else w_all[0].at[order_ref[phase]]
            return pltpu.make_async_copy(src, w_vmem.at[phase % 2], w_sem.at[phase % 2])

        def landed(f):
            gat.relay_one(0, arrival[f])
            gat.await_one(0, arrival[f])

        @pl.when(jnp.logical_and(ph == 0, i == 0))
        def _():
            fetch(0).start()
            fetch(0).wait()

        @pl.when(jnp.logical_and(ph == 1, i == 0))
        def _():
            landed(0)
            fetch(1).start()
            fetch(1).wait()

        for f in (1, 2):
            @pl.when(jnp.logical_and(ph == f, i == n_tiles - 1))
            def _(f=f):
                landed(f)
                fetch(f + 1).start()

            @pl.when(jnp.logical_and(ph == f + 1, i == 0))
            def _(f=f):
                fetch(f + 1).wait()

        rows = pl.ds(pl.multiple_of(i * tm, tm), tm)

        @pl.when(ph == 0)
        def _():
            hn, _, _ = _rms_fwd(x_ref[...], g_ref[...])
            hn = hn.astype(BF16)
            hn_ref[...] = hn
            hn_all[rows, :] = hn

        z_ref[0] = _dot_nt(hn_all[rows, :], w_vmem[ph % 2])

        @pl.when(jnp.logical_and(ph == N_CHIPS - 1, i == n_tiles - 1))
        def _():
            for p in range(1, n_w):
                for j in range(3):
                    gat.relay_one(p, j)
            for p in range(1, n_w):
                for j in range(3):
                    gat.await_one(p, j)
            gat.wait_sends()

    def tile_in_phase0(ph, i, order):
        return (jnp.where(ph == 0, i, n_tiles - 1), 0)

    any_spec = pl.BlockSpec(memory_space=pl.ANY)
    grid_spec = pltpu.PrefetchScalarGridSpec(
        num_scalar_prefetch=1,
        grid=(N_CHIPS, n_tiles),
        in_specs=[pl.BlockSpec((tm, D_MODEL), tile_in_phase0),
                  pl.BlockSpec((1, D_MODEL), lambda ph, i, order: (0, 0)), any_spec]
        + [pl.BlockSpec(a.shape, lambda ph, i, order: (0, 0, 0)) for a in up_shards],
        out_specs=[pl.BlockSpec((tm, D_MODEL), tile_in_phase0),
                   pl.BlockSpec((1, tm, SHARD_COLS), lambda ph, i, order: (order[ph], i, 0))] + [any_spec] * n_w,
        scratch_shapes=[pltpu.VMEM((2, SHARD_COLS, D_MODEL), BF16), pltpu.VMEM((s, D_MODEL), BF16),
                        pltpu.SemaphoreType.DMA((2,))]
        + [pltpu.VMEM((r, cc), BF16) for _, r, cc, _, _ in gat.params[1:]] + gat.scratch,
    )
    out = pl.pallas_call(
        body,
        name="inproj_fwd",
        grid_spec=grid_spec,
        out_shape=[jax.ShapeDtypeStruct((s, D_MODEL), BF16), jax.ShapeDtypeStruct((N_CHIPS, s, SHARD_COLS), F32)]
        + gat.out_shape,
        compiler_params=pltpu.CompilerParams(dimension_semantics=("arbitrary", "arbitrary"),
                                             vmem_limit_bytes=VMEM_LIMIT),
    )(order, x, norm_in, w_in_shard, *up_shards)
    return out[0], out[1], out[2:]


def _qkv_fwd(z_sh, q_norm, kv_norm, w_uq_all, w_ukv_all, rc, rsa, rsb, to_bf16, tm):
    s = z_sh.shape[1]
    n_cast = len(to_bf16)
    hw = MLA_HEADS * HEAD_PAD
    qk = QK_NOPE_DIM + QK_ROPE_DIM

    def body(z_ref, gq_ref, gkv_ref, uq_ref, ukv_ref, c_ref, sa_ref, sb_ref, *rest):
        f32_refs, rest = rest[:n_cast], rest[n_cast:]
        (q_ref, k_ref, v_ref, qt_ref, vt_ref, wuq_ref, wk_ref, wv_ref), bf_refs = rest[:8], rest[8:]

        @pl.when(pl.program_id(0) == 0)
        def _():
            for src, dst in zip(f32_refs, bf_refs):
                dst[...] = src[...].astype(BF16)
            w_q = jnp.concatenate([uq_ref[j] for j in range(N_CHIPS)], axis=0).astype(F32)
            gap = jnp.zeros((Q_LORA_RANK, HEAD_PAD - qk), F32)
            wuq_ref[...] = jnp.concatenate(
                [part for h in range(MLA_HEADS) for part in (w_q[:, h * qk:(h + 1) * qk], gap)], axis=1).astype(BF16)
            w_kv = jnp.concatenate([ukv_ref[j] for j in range(N_CHIPS)], axis=0).astype(F32)
            lane = lax.broadcasted_iota(jnp.int32, w_kv.shape, 1)
            wk_ref[...] = jnp.where(lane % HEAD_PAD < QK_NOPE_DIM, w_kv, 0.0).astype(BF16)
            wv_ref[...] = jnp.concatenate(
                [w_kv[:, h * HEAD_PAD + QK_NOPE_DIM:(h + 1) * HEAD_PAD] for h in range(MLA_HEADS)],
                axis=1).astype(BF16)

        c, sa, sb = c_ref[...], sa_ref[...], sb_ref[...]
        z0 = z_ref[0]
        cq, _, _ = _rms_fwd(z0[:, ZQ_COLS], _as_one_row(gq_ref[...]))
        qf = _dot(cq.astype(BF16), wuq_ref[...])
        ckv, _, _ = _rms_fwd(z0[:, ZKV_COLS], gkv_ref[...])
        ckv = ckv.astype(BF16)
        kn = _dot(ckv, wk_ref[...])
        lane = lax.broadcasted_iota(jnp.int32, (tm, HEAD_PAD), 1)
        zkr = jnp.where(lane < QK_ROPE_DIM, z0[:, ZKR_TILE], 0.0)
        kr = _rope(pltpu.roll(zkr, 64, 1), c, sa, sb)
        for h in range(MLA_HEADS):
            cols = slice(h * HEAD_PAD, (h + 1) * HEAD_PAD)
            qh = _rope(qf[:, cols], c, sa, sb)
            q_ref[:, cols] = qh.astype(BF16)
            qt_ref[cols, :] = qh.T.astype(BF16)
            k_ref[:, cols] = (kn[:, cols] + kr).astype(BF16)
        vf = _dot(ckv, wv_ref[...])
        v_ref[...] = vf.astype(BF16)
        vt_ref[...] = vf.T.astype(BF16)

    return pl.pallas_call(
        body,
        name="qkv_fwd",
        grid=(s // tm,),
        in_specs=[
            pl.BlockSpec((1, tm, SHARD_COLS), lambda i: (0, i, 0)),
            _full_spec(q_norm.shape), _full_spec((1, KV_LORA_RANK)),
            _full_spec(w_uq_all.shape), _full_spec(w_ukv_all.shape),
            _row_spec(tm, HEAD_PAD), _row_spec(tm, HEAD_PAD), _row_spec(tm, HEAD_PAD),
        ] + [_full_spec(a.shape) for a in to_bf16],
        out_specs=[_row_spec(tm, hw), _row_spec(tm, hw), _row_spec(tm, MLA_WIDTH),
                   pl.BlockSpec((hw, tm), lambda i: (0, i)), pl.BlockSpec((MLA_WIDTH, tm), lambda i: (0, i)),
                   _full_spec((Q_LORA_RANK, hw)), _full_spec((KV_LORA_RANK, hw)), _full_spec((KV_LORA_RANK, MLA_WIDTH))
                   ] + [_full_spec(a.shape) for a in to_bf16],
        out_shape=[jax.ShapeDtypeStruct((s, hw), BF16), jax.ShapeDtypeStruct((s, hw), BF16),
                   jax.ShapeDtypeStruct((s, MLA_WIDTH), BF16),
                   jax.ShapeDtypeStruct((hw, s), BF16), jax.ShapeDtypeStruct((MLA_WIDTH, s), BF16),
                   jax.ShapeDtypeStruct((Q_LORA_RANK, hw), BF16), jax.ShapeDtypeStruct((KV_LORA_RANK, hw), BF16),
                   jax.ShapeDtypeStruct((KV_LORA_RANK, MLA_WIDTH), BF16)
                   ] + [jax.ShapeDtypeStruct(a.shape, BF16) for a in to_bf16],
        compiler_params=pltpu.CompilerParams(dimension_semantics=("arbitrary",), vmem_limit_bytes=VMEM_LIMIT),
    )(z_sh, q_norm, kv_norm, w_uq_all, w_ukv_all, rc, rsa, rsb, *to_bf16)


def _chunk_mask(t, keys_on_rows):
    rows = lax.broadcasted_iota(jnp.int32, (t, t), 0) // CHUNK
    cols = lax.broadcasted_iota(jnp.int32, (t, t), 1) // CHUNK
    return rows <= cols if keys_on_rows else cols <= rows


def _attn_fwd(q_t, k, v_t, late_shards, t):
    s = k.shape[0]
    groups = MLA_HEADS // FWD_HEADS
    n_q = s // t
    gat = _Gather(COMM_PARAMS[3:])
    n_w = len(gat.params)

    def body(qt_ref, k_ref, k2_ref, vt_ref, *rest):
        w_in, (o_ref, lse_ref), w_out = rest[:n_w], rest[n_w:n_w + 2], rest[n_w + 2:2 * n_w + 2]
        gat.bind(w_in, w_out, rest[2 * n_w + 2:])
        i = pl.program_id(1)
        step_no = pl.program_id(0) * n_q + i
        pl.when(step_no == 0)(gat.start)
        pl.when(step_no == groups * n_q // 2)(gat.relay)
        mask = _chunk_mask(t, True)
        qcs = [slice(hh * HEAD_PAD, (hh + 1) * HEAD_PAD) for hh in range(FWD_HEADS)]
        vcs = [slice(hh * V_HEAD_DIM, (hh + 1) * V_HEAD_DIM) for hh in range(FWD_HEADS)]
        qts = [qt_ref[qc, :] for qc in qcs]

        def step(j, carry, masked):
            keys = pl.ds(pl.multiple_of(j * t, t), t)
            out = []
            for hh in range(FWD_HEADS):
                m, l, acc = carry[hh]
                sc = _dot(k_ref[keys, qcs[hh]], qts[hh])
                if masked:
                    sc = jnp.where(mask, sc, -jnp.inf)
                m_new = jnp.maximum(m, jnp.max(sc, axis=0, keepdims=True))
                alpha = jnp.exp2((m - m_new) * ATT_SCALE_LOG2E)
                p = jnp.exp2((_dot(k2_ref[keys, qcs[hh]], qts[hh]) - m_new) * ATT_SCALE_LOG2E)
                if masked:
                    p = jnp.where(mask, p, 0.0)
                l = alpha * l + jnp.sum(p, axis=0, keepdims=True)
                acc = alpha * acc + _dot(vt_ref[vcs[hh], keys], p.astype(BF16))
                out.append((m_new, l, acc))
            return tuple(out)

        one = (jnp.full((1, t), -jnp.inf, F32), jnp.zeros((1, t), F32), jnp.zeros((V_HEAD_DIM, t), F32))
        carry = lax.fori_loop(0, i, functools.partial(step, masked=False), (one,) * FWD_HEADS)
        carry = step(i, carry, True)
        o_ref[...] = jnp.concatenate([carry[hh][2] / carry[hh][1] for hh in range(FWD_HEADS)], axis=0).T
        for hh in range(FWD_HEADS):
            m, l, _ = carry[hh]
            lse_ref[:, qcs[hh]] = jnp.broadcast_to(m * ATT_SCALE_LOG2E + jnp.log2(l), (HEAD_PAD, t)).T
        pl.when(step_no == groups * n_q - 1)(gat.finish)

    any_spec = pl.BlockSpec(memory_space=pl.ANY)
    out = pl.pallas_call(
        body,
        name="attn_fwd",
        grid=(groups, n_q),
        in_specs=[
            pl.BlockSpec((FWD_HEADS * HEAD_PAD, t), lambda p, i: (p, i)),
            pl.BlockSpec((s, FWD_HEADS * HEAD_PAD), lambda p, i: (0, p), pipeline_mode=pl.Buffered(1)),
            pl.BlockSpec((s, FWD_HEADS * HEAD_PAD), lambda p, i: (0, p), pipeline_mode=pl.Buffered(1)),
            pl.BlockSpec((FWD_HEADS * V_HEAD_DIM, s), lambda p, i: (p, 0), pipeline_mode=pl.Buffered(1)),
        ] + [any_spec] * n_w,
        out_specs=[
            pl.BlockSpec((t, FWD_HEADS * V_HEAD_DIM), lambda p, i: (i, p)),
            pl.BlockSpec((t, FWD_HEADS * HEAD_PAD), lambda p, i: (i, p)),
        ] + [any_spec] * n_w,
        out_shape=[jax.ShapeDtypeStruct((s, MLA_WIDTH), F32), jax.ShapeDtypeStruct((s, MLA_HEADS * HEAD_PAD), F32)]
        + gat.out_shape,
        scratch_shapes=gat.scratch,
        compiler_params=pltpu.CompilerParams(dimension_semantics=("arbitrary", "arbitrary"),
                                             vmem_limit_bytes=VMEM_LIMIT),
    )(q_t, k, k, v_t, *late_shards)
    return out[0], out[1], out[2:]


def _mid(o, z_sh, x, target, pool_w, pool_scale, w_ba, w_bp, w_out, norm_final, tm):
    s = x.shape[0]
    n_tiles = s // tm
    halo_per_tile = tm // POOL_HALO
    small_rows = _small_rows(SMALL_MID)

    def body(o_ref, z0_ref, z1_ref, z1h_ref, z2_ref, z3_ref, x_ref, t_ref, pw_ref, ps_ref, wba_ref, wbp_ref,
             wout_ref, gf_ref,
             do_ref, dl_ref, dga_ref, dgp_ref, dgm_ref, ddc_ref, dh_ref,
             small_ref, dwout_out, dwba_out, dwbp_out,
             ubuf, dwout_ref, dwba_ref, dwbp_ref, loss_ref, dpw_ref, dps_ref, dgf_ref):
        i = pl.program_id(0)

        @pl.when(i == 0)
        def _():
            loss_ref[...] = jnp.zeros_like(loss_ref)
            dwout_ref[...] = jnp.zeros_like(dwout_ref)
            dwba_ref[...] = jnp.zeros_like(dwba_ref)
            dwbp_ref[...] = jnp.zeros_like(dwbp_ref)
            dpw_ref[...] = jnp.zeros_like(dpw_ref)
            dps_ref[...] = jnp.zeros_like(dps_ref)
            dgf_ref[...] = jnp.zeros_like(dgf_ref)

        zs = [z0_ref[0], z1_ref[0], z2_ref[0], z3_ref[0]]
        o = o_ref[...]
        ga = _segment(zs, 3)
        sga = _sigmoid(ga)
        silu_a = ga * sga
        y_attn = (o * silu_a).astype(BF16)

        ubuf[0:POOL_HALO, :] = jnp.where(i > 0, _segment([None, z1h_ref[0]], 4), 0.0)
        ubuf[POOL_HALO:, :] = _segment(zs, 4)
        row = lax.broadcasted_iota(jnp.int32, (tm, POOL_GROUP_DIM), 0) + i * tm
        ps = ps_ref[...]
        gp = _segment(zs, 5)
        sgp = _sigmoid(gp)
        silu_p = gp * sgp
        d_bf, dm, inv_cnt = [], [], []
        for g, w in enumerate(POOL_WINDOWS):
            cols = slice(g * POOL_GROUP_DIM, (g + 1) * POOL_GROUP_DIM)
            wsum = ubuf[POOL_HALO:, cols]
            for kk in range(1, w):
                wsum = wsum + ubuf[POOL_HALO - kk:POOL_HALO - kk + tm, cols]
            inv = 1.0 / jnp.minimum(row + 1, w).astype(F32)
            dg = (wsum * inv - ubuf[POOL_HALO:, cols]).astype(BF16)
            d_bf.append(dg)
            inv_cnt.append(inv)
            dm.append(_dot(dg, pw_ref[g]))
        dm = jnp.concatenate(dm, axis=1)
        yp = dm * ps
        y_pool = (yp * silu_p).astype(BF16)

        a = jnp.concatenate([_dot(y_attn, wba_ref[j]) for j in range(N_CHIPS)], axis=1)
        p = jnp.concatenate([_dot(y_pool, wbp_ref[j]) for j in range(N_CHIPS)], axis=1)
        gm = _segment(zs, 6)
        gate_a = _sigmoid(gm[:, :D_MODEL])
        gate_p = _sigmoid(gm[:, D_MODEL:])
        merged = (gate_a * a + gate_p * p).astype(BF16)
        h = x_ref[...] + _dot(merged, wout_ref[...])
        gf = gf_ref[...]
        y, xhat, r = _rms_fwd(h, gf)
        err = y - t_ref[...]
        e2 = err * err
        e2 = jnp.sum(e2.reshape(tm // 8, 8, D_MODEL), axis=0)
        acc = e2[:, 0:LANES]
        for cidx in range(1, D_MODEL // LANES):
            acc = acc + e2[:, cidx * LANES:(cidx + 1) * LANES]
        loss_ref[...] += acc

        dy = err * (1.0 / D_MODEL)
        dgf_ref[...] += _colsum(dy * xhat)
        dh = _rms_bwd(dy, xhat, r, gf)
        dh_ref[...] = dh
        dh_bf = dh.astype(BF16)
        dwout_ref[...] += _dot_tn(merged, dh_bf)
        dmerged = _dot_nt(dh_bf, wout_ref[...])
        da = (dmerged * gate_a).astype(BF16)
        dp = (dmerged * gate_p).astype(BF16)
        dgm_ref[:, :D_MODEL] = (dmerged * a * gate_a * (1.0 - gate_a)).astype(BF16)
        dgm_ref[:, D_MODEL:] = (dmerged * p * gate_p * (1.0 - gate_p)).astype(BF16)
        dy_attn = dy_pool = None
        for j in range(N_CHIPS):
            cols = slice(j * BRANCH_COLS, (j + 1) * BRANCH_COLS)
            dwba_ref[j] += _dot_tn(y_attn, da[:, cols])
            dwbp_ref[j] += _dot_tn(y_pool, dp[:, cols])
            pa = _dot_nt(da[:, cols], wba_ref[j])
            pp = _dot_nt(dp[:, cols], wbp_ref[j])
            dy_attn = pa if dy_attn is None else dy_attn + pa
            dy_pool = pp if dy_pool is None else dy_pool + pp

        do = dy_attn * silu_a
        do_ref[...] = do
        dga_ref[...] = (dy_attn * o * (sga * (1.0 + ga * (1.0 - sga)))).astype(BF16)
        doo = do * o
        for hd in range(MLA_HEADS):
            dl = jnp.sum(doo[:, hd * V_HEAD_DIM:(hd + 1) * V_HEAD_DIM], axis=1, keepdims=True)
            dl_ref[:, hd * HEAD_PAD:(hd + 1) * HEAD_PAD] = jnp.broadcast_to(dl, (tm, HEAD_PAD))

        dyp = dy_pool * silu_p
        dgp_ref[...] = (dy_pool * yp * (sgp * (1.0 + gp * (1.0 - sgp)))).astype(BF16)
        dps_ref[...] += _colsum(dyp * dm)
        dmm = (dyp * ps).astype(BF16)
        for g in range(len(POOL_WINDOWS)):
            cols = slice(g * POOL_GROUP_DIM, (g + 1) * POOL_GROUP_DIM)
            dpw_ref[g] += _dot_tn(d_bf[g], dmm[:, cols])
            ddc_ref[:, cols] = _dot_nt(dmm[:, cols], pw_ref[g]) * inv_cnt[g]

        @pl.when(i == n_tiles - 1)
        def _():
            dwout_out[...] = dwout_ref[...].astype(BF16)
            dwba_out[...] = dwba_ref[...].astype(BF16)
            dwbp_out[...] = dwbp_ref[...].astype(BF16)
            _pack_into(small_ref, SMALL_MID, dict(pool_w=dpw_ref[...], norm_final=dgf_ref[...], sq_err=loss_ref[...],
                                                  pool_scale=dps_ref[...]))

    row_in = lambda n: _row_spec(tm, n)
    in_specs = [
        row_in(MLA_WIDTH),
        pl.BlockSpec((1, tm, SHARD_COLS), lambda i: (0, i, 0)), pl.BlockSpec((1, tm, SHARD_COLS), lambda i: (1, i, 0)),
        pl.BlockSpec((1, POOL_HALO, SHARD_COLS), lambda i: (1, jnp.maximum(i * halo_per_tile - 1, 0), 0)),
        pl.BlockSpec((1, tm, SHARD_COLS), lambda i: (2, i, 0)), pl.BlockSpec((1, tm, SHARD_COLS), lambda i: (3, i, 0)),
        row_in(D_MODEL), row_in(D_MODEL),
        _full_spec((4, POOL_GROUP_DIM, POOL_GROUP_DIM)), _full_spec((1, POOL_WIDTH)),
        _full_spec((N_CHIPS, MLA_WIDTH, BRANCH_COLS)), _full_spec((N_CHIPS, POOL_WIDTH, BRANCH_COLS)),
        _full_spec((D_MODEL, D_MODEL)), _full_spec((1, D_MODEL)),
    ]
    out_shape = [
        jax.ShapeDtypeStruct((s, MLA_WIDTH), F32),
        jax.ShapeDtypeStruct((s, MLA_HEADS * HEAD_PAD), F32),
        jax.ShapeDtypeStruct((s, MLA_WIDTH), BF16),
        jax.ShapeDtypeStruct((s, POOL_WIDTH), BF16),
        jax.ShapeDtypeStruct((s, 2 * D_MODEL), BF16),
        jax.ShapeDtypeStruct((s, POOL_WIDTH), F32),
        jax.ShapeDtypeStruct((s, D_MODEL), F32),
        jax.ShapeDtypeStruct((small_rows, LANES), F32),
        jax.ShapeDtypeStruct((D_MODEL, D_MODEL), BF16),
        jax.ShapeDtypeStruct((N_CHIPS, MLA_WIDTH, BRANCH_COLS), BF16),
        jax.ShapeDtypeStruct((N_CHIPS, POOL_WIDTH, BRANCH_COLS), BF16),
    ]
    out_specs = [
        row_in(MLA_WIDTH), row_in(MLA_HEADS * HEAD_PAD), row_in(MLA_WIDTH), row_in(POOL_WIDTH),
        row_in(2 * D_MODEL), row_in(POOL_WIDTH), row_in(D_MODEL),
        _full_spec((small_rows, LANES)), _full_spec((D_MODEL, D_MODEL)),
        _full_spec((N_CHIPS, MLA_WIDTH, BRANCH_COLS)), _full_spec((N_CHIPS, POOL_WIDTH, BRANCH_COLS)),
    ]
    return pl.pallas_call(
        body,
        name="mid",
        grid=(n_tiles,),
        in_specs=in_specs,
        out_specs=out_specs,
        out_shape=out_shape,
        scratch_shapes=[
            pltpu.VMEM((tm + POOL_HALO, POOL_WIDTH), F32),
            pltpu.VMEM((D_MODEL, D_MODEL), F32),
            pltpu.VMEM((N_CHIPS, MLA_WIDTH, BRANCH_COLS), F32),
            pltpu.VMEM((N_CHIPS, POOL_WIDTH, BRANCH_COLS), F32),
            pltpu.VMEM((8, LANES), F32),
            pltpu.VMEM((4, POOL_GROUP_DIM, POOL_GROUP_DIM), F32),
            pltpu.VMEM((1, POOL_WIDTH), F32),
            pltpu.VMEM((1, D_MODEL), F32),
        ],
        compiler_params=pltpu.CompilerParams(dimension_semantics=("arbitrary",), vmem_limit_bytes=VMEM_LIMIT),
    )(o, z_sh, z_sh, z_sh, z_sh, z_sh, x, target, pool_w, pool_scale, w_ba, w_bp, w_out, norm_final)


def _attn_bwd(q, q_t, k, v, do, lse, delta, late_grads, gs_mid, t):
    s = q.shape[0]
    groups = MLA_HEADS // BWD_HEADS
    n_q = s // t
    red = _Reduce(COMM_PARAMS[3:])
    n_w = len(red.params)
    small = _SmallSum(gs_mid.shape[0])
    n_red = len(red.scratch)

    def body(q_ref, qt_ref, do_ref, lse_ref, dl_ref, k_ref, v_ref, *rest):
        g_in, gs_ref = rest[:n_w], rest[n_w]
        (dq_ref, dk_ref, dv_ref), g_out, gsum_ref = rest[n_w + 1:n_w + 4], rest[n_w + 4:2 * n_w + 4], rest[2 * n_w + 4]
        scratch = rest[2 * n_w + 5:]
        red.bind(g_in, g_out, scratch[:n_red])
        small.bind(gs_ref, gsum_ref, scratch[n_red:])
        i = pl.program_id(1)
        step_no = pl.program_id(0) * n_q + i

        @pl.when(step_no == 0)
        def _():
            red.start()
            small.start()

        pl.when(step_no == groups * n_q // 2)(red.exchange)
        pl.when(step_no == groups * n_q - 1)(red.finish_sum)

        @pl.when(i == 0)
        def _():
            dk_ref[...] = jnp.zeros_like(dk_ref)
            dv_ref[...] = jnp.zeros_like(dv_ref)

        mask = _chunk_mask(t, False)
        qcs = [slice(hh * HEAD_PAD, (hh + 1) * HEAD_PAD) for hh in range(BWD_HEADS)]
        vcs = [slice(hh * V_HEAD_DIM, (hh + 1) * V_HEAD_DIM) for hh in range(BWD_HEADS)]
        qhs = [q_ref[:, qc] for qc in qcs]
        qts = [qt_ref[qc, :] for qc in qcs]
        dohs = [do_ref[:, vc].astype(BF16) for vc in vcs]
        do_t = do_ref[...].T.astype(BF16)
        dots = [do_t[vc, :] for vc in vcs]
        lses = [jnp.tile(lse_ref[:, qc], (1, t // HEAD_PAD)) for qc in qcs]
        dls = [jnp.tile(dl_ref[:, qc], (1, t // HEAD_PAD)) for qc in qcs]

        def step(j, dqs, masked):
            keys = pl.ds(pl.multiple_of(j * t, t), t)
            out = []
            for hh in range(BWD_HEADS):
                kj = k_ref[keys, qcs[hh]]
                vj = v_ref[keys, vcs[hh]]
                p = jnp.exp2(_dot_nt(qhs[hh], kj) * ATT_SCALE_LOG2E - lses[hh])
                if masked:
                    p = jnp.where(mask, p, 0.0)
                ds = (p * (_dot_nt(dohs[hh], vj) - dls[hh])).astype(BF16)
                dv_ref[vcs[hh], keys] += _dot(dots[hh], p.astype(BF16))
                dk_ref[qcs[hh], keys] += _dot(qts[hh], ds) * ATT_SCALE
                out.append(dqs[hh] + _dot(ds, kj))
            return tuple(out)

        zero = jnp.zeros((t, HEAD_PAD), F32)
        dqs = lax.fori_loop(0, i, functools.partial(step, masked=False), (zero,) * BWD_HEADS)
        dqs = step(i, dqs, True)
        for hh in range(BWD_HEADS):
            dq_ref[:, qcs[hh]] = dqs[hh] * ATT_SCALE

        @pl.when(step_no == groups * n_q - 1)
        def _():
            red.finish_wait()
            small.finish()

    hw = MLA_HEADS * HEAD_PAD
    any_spec = pl.BlockSpec(memory_space=pl.ANY)
    out = pl.pallas_call(
        body,
        name="attn_bwd",
        grid=(groups, n_q),
        in_specs=[
            pl.BlockSpec((t, BWD_HEADS * HEAD_PAD), lambda p, i: (i, p)),
            pl.BlockSpec((BWD_HEADS * HEAD_PAD, t), lambda p, i: (p, i)),
            pl.BlockSpec((t, BWD_HEADS * V_HEAD_DIM), lambda p, i: (i, p)),
            pl.BlockSpec((t, BWD_HEADS * HEAD_PAD), lambda p, i: (i, p)),
            pl.BlockSpec((t, BWD_HEADS * HEAD_PAD), lambda p, i: (i, p)),
            pl.BlockSpec((s, BWD_HEADS * HEAD_PAD), lambda p, i: (0, p), pipeline_mode=pl.Buffered(1)),
            pl.BlockSpec((s, BWD_HEADS * V_HEAD_DIM), lambda p, i: (0, p), pipeline_mode=pl.Buffered(1)),
        ] + [any_spec] * n_w + [pl.BlockSpec(small.spec_shape, lambda p, i: (0, 0))],
        out_specs=[
            pl.BlockSpec((t, BWD_HEADS * HEAD_PAD), lambda p, i: (i, p)),
            pl.BlockSpec((BWD_HEADS * HEAD_PAD, s), lambda p, i: (p, 0)),
            pl.BlockSpec((BWD_HEADS * V_HEAD_DIM, s), lambda p, i: (p, 0)),
        ] + [any_spec] * n_w + [pl.BlockSpec(small.spec_shape, lambda p, i: (0, 0))],
        out_shape=[jax.ShapeDtypeStruct((s, hw), F32), jax.ShapeDtypeStruct((hw, s), F32),
                   jax.ShapeDtypeStruct((MLA_WIDTH, s), F32)] + red.out_shape + [small.out_shape],
        scratch_shapes=red.scratch + small.scratch,
        compiler_params=pltpu.CompilerParams(dimension_semantics=("arbitrary", "arbitrary"),
                                             vmem_limit_bytes=VMEM_LIMIT),
    )(q, q_t, do, lse, delta, k, v, *late_grads, gs_mid)
    return out[0], out[1], out[2], out[3:3 + n_w], out[3 + n_w]


def _qkv_bwd(dq, dk_t, dv_t, z_sh, q_norm, kv_norm, wuq_p, wk_p, wv, rc, rsa, rsb, tm):
    s = z_sh.shape[1]
    hw = MLA_HEADS * HEAD_PAD
    n_tiles = s // tm
    uq_shape, ukv_shape = (N_CHIPS,) + COMM_PARAMS[1][1:3], (N_CHIPS,) + COMM_PARAMS[2][1:3]

    def body(dq_ref, dk_ref, dv_ref, z_ref, gq_ref, gkv_ref, wuq_ref, wk_ref, wv_ref,
             c_ref, sa_ref, sb_ref,
             dzq_ref, dzkv_ref, dzkr_ref, duq_ref, dukv_ref, dgq_ref, dgkv_ref, dwuq_ref, dwk_ref, dwv_ref):
        i = pl.program_id(0)

        @pl.when(i == 0)
        def _():
            dwuq_ref[...] = jnp.zeros_like(dwuq_ref)
            dwk_ref[...] = jnp.zeros_like(dwk_ref)
            dwv_ref[...] = jnp.zeros_like(dwv_ref)
            dgq_ref[...] = jnp.zeros_like(dgq_ref)
            dgkv_ref[...] = jnp.zeros_like(dgkv_ref)

        c, sa, sb = c_ref[...], sa_ref[...], sb_ref[...]
        gq, gkv = _as_one_row(gq_ref[...]), gkv_ref[...]

        z0 = z_ref[0]
        cq, xq, rq = _rms_fwd(z0[:, ZQ_COLS], gq)
        dqp = jnp.concatenate(
            [_unrope(dq_ref[:, h * HEAD_PAD:(h + 1) * HEAD_PAD], c, sa, sb) for h in range(MLA_HEADS)],
            axis=1).astype(BF16)
        dwuq_ref[...] += _dot_tn(cq.astype(BF16), dqp)
        dcq = _dot_nt(dqp, wuq_ref[...])
        dgq_ref[...] += _colsum(dcq * xq)
        dzq_ref[...] = _rms_bwd(dcq, xq, rq, gq).astype(BF16)

        ckv, xkv, rkv = _rms_fwd(z0[:, ZKV_COLS], gkv)
        ckv = ckv.astype(BF16)
        dkf = dk_ref[...].T
        dk_bf = dkf.astype(BF16)
        dv_bf = dv_ref[...].T.astype(BF16)
        dwk_ref[...] += _dot_tn(ckv, dk_bf)
        dwv_ref[...] += _dot_tn(ckv, dv_bf)
        dckv = _dot_nt(dk_bf, wk_ref[...]) + _dot_nt(dv_bf, wv_ref[...])
        dgkv_ref[...] += _colsum(dckv * xkv)
        dzkv_ref[...] = _rms_bwd(dckv, xkv, rkv, gkv).astype(BF16)

        dkr = dkf[:, 0:HEAD_PAD]
        for h in range(1, MLA_HEADS):
            dkr = dkr + dkf[:, h * HEAD_PAD:(h + 1) * HEAD_PAD]
        dkr = pltpu.roll(_unrope(dkr, c, sa, sb), 64, 1)
        lane = lax.broadcasted_iota(jnp.int32, (tm, HEAD_PAD), 1)
        dzkr_ref[...] = jnp.where(lane < QK_ROPE_DIM, dkr, 0.0).astype(BF16)

        @pl.when(i == n_tiles - 1)
        def _():
            qk = QK_NOPE_DIM + QK_ROPE_DIM
            d_uq = jnp.concatenate([dwuq_ref[:, h * HEAD_PAD:h * HEAD_PAD + qk] for h in range(MLA_HEADS)],
                                   axis=1).astype(BF16)
            d_ukv = jnp.concatenate(
                [part for h in range(MLA_HEADS)
                 for part in (dwk_ref[:, h * HEAD_PAD:h * HEAD_PAD + QK_NOPE_DIM],
                              dwv_ref[:, h * V_HEAD_DIM:(h + 1) * V_HEAD_DIM])], axis=1).astype(BF16)
            for j in range(N_CHIPS):
                duq_ref[j] = d_uq[j * uq_shape[1]:(j + 1) * uq_shape[1]]
                dukv_ref[j] = d_ukv[j * ukv_shape[1]:(j + 1) * ukv_shape[1]]

    return pl.pallas_call(
        body,
        name="qkv_bwd",
        grid=(s // tm,),
        in_specs=[
            _row_spec(tm, hw), pl.BlockSpec((hw, tm), lambda i: (0, i)), pl.BlockSpec((MLA_WIDTH, tm), lambda i: (0, i)),
            pl.BlockSpec((1, tm, SHARD_COLS), lambda i: (0, i, 0)),
            _full_spec(q_norm.shape), _full_spec((1, KV_LORA_RANK)),
            _full_spec((Q_LORA_RANK, hw)), _full_spec((KV_LORA_RANK, hw)), _full_spec((KV_LORA_RANK, MLA_WIDTH)),
            _row_spec(tm, HEAD_PAD), _row_spec(tm, HEAD_PAD), _row_spec(tm, HEAD_PAD),
        ],
        out_specs=[
            _row_spec(tm, Q_LORA_RANK), _row_spec(tm, KV_LORA_RANK), _row_spec(tm, HEAD_PAD),
            _full_spec(uq_shape), _full_spec(ukv_shape),
            _full_spec((1, Q_LORA_RANK)), _full_spec((1, KV_LORA_RANK)),
        ],
        out_shape=[
            jax.ShapeDtypeStruct((s, Q_LORA_RANK), BF16), jax.ShapeDtypeStruct((s, KV_LORA_RANK), BF16),
            jax.ShapeDtypeStruct((s, HEAD_PAD), BF16),
            jax.ShapeDtypeStruct(uq_shape, BF16), jax.ShapeDtypeStruct(ukv_shape, BF16),
            jax.ShapeDtypeStruct((1, Q_LORA_RANK), F32), jax.ShapeDtypeStruct((1, KV_LORA_RANK), F32),
        ],
        scratch_shapes=[pltpu.VMEM((Q_LORA_RANK, hw), F32), pltpu.VMEM((KV_LORA_RANK, hw), F32),
                        pltpu.VMEM((KV_LORA_RANK, MLA_WIDTH), F32)],
        compiler_params=pltpu.CompilerParams(dimension_semantics=("arbitrary",), vmem_limit_bytes=VMEM_LIMIT),
    )(dq, dk_t, dv_t, z_sh, q_norm, kv_norm, wuq_p, wk_p, wv, rc, rsa, rsb)


def _inproj_bwd_x(dzq, dzkv, dzkr, dgattn, ddc, dgpool, dgmerge, x, dh, norm_in, d_q_norm, d_kv_norm, w_in_t, tm):
    s = x.shape[0]
    n_tiles = s // tm
    halo_per_tile = tm // POOL_HALO
    n_halo = s // POOL_HALO
    u_seg = 4
    small_rows = _small_rows(SMALL_LATE)

    def body(dzq_ref, dzkv_ref, dzkr_ref, dga_ref, ddc_ref, ddn_ref, dgp_ref, dgm_ref, x_ref, dh_ref,
             g_ref, dgq_ref, dgkv_ref, w_hbm, gx_ref, small_ref, dzs_ref, w_vmem, dbuf, sem, dgin_ref):
        i = pl.program_id(0)

        @pl.when(i == 0)
        def _():
            cp = pltpu.make_async_copy(w_hbm, w_vmem, sem)
            cp.start()
            dgin_ref[...] = jnp.zeros_like(dgin_ref)
            cp.wait()

        dbuf[0:tm, :] = ddc_ref[...]
        dbuf[tm:, :] = jnp.where(i < n_tiles - 1, ddn_ref[...], 0.0)
        row = lax.broadcasted_iota(jnp.int32, (tm, POOL_GROUP_DIM), 0) + i * tm
        du = []
        for g, w in enumerate(POOL_WINDOWS):
            cols = slice(g * POOL_GROUP_DIM, (g + 1) * POOL_GROUP_DIM)
            fsum = dbuf[0:tm, cols]
            for kk in range(1, w):
                fsum = fsum + dbuf[kk:kk + tm, cols]
            du.append(fsum - dbuf[0:tm, cols] * jnp.minimum(row + 1, w).astype(F32))
        du = jnp.concatenate(du, axis=1).astype(BF16)

        dz = [dzq_ref[...], dzkv_ref[...], dzkr_ref[...], dga_ref[...], du, dgp_ref[...], dgm_ref[...]]
        dz = jnp.concatenate([d[:, :w] for d, (w, _) in zip(dz, IN_SEGMENTS)], axis=1)
        for j in range(N_CHIPS):
            dzs_ref[j] = dz[:, j * SHARD_COLS:(j + 1) * SHARD_COLS].T
        dhn = _dot(dz, w_vmem[...])

        g = g_ref[...]
        _, xhat, r = _rms_fwd(x_ref[...], g)
        dgin_ref[...] += _colsum(dhn * xhat)
        gx_ref[...] = dh_ref[...] + _rms_bwd(dhn, xhat, r, g)

        @pl.when(i == n_tiles - 1)
        def _():
            _pack_into(small_ref, SMALL_LATE, dict(norm_in=dgin_ref[...], q_norm=dgq_ref[...], kv_norm=dgkv_ref[...]))

    any_spec = pl.BlockSpec(memory_space=pl.ANY)
    seg_w = [wide for _, wide in IN_SEGMENTS]
    return pl.pallas_call(
        body,
        name="inproj_bwd_x",
        grid=(n_tiles,),
        in_specs=[
            _row_spec(tm, seg_w[0]), _row_spec(tm, seg_w[1]), _row_spec(tm, seg_w[2]),
            _row_spec(tm, seg_w[3]), _row_spec(tm, seg_w[u_seg]),
            pl.BlockSpec((POOL_HALO, POOL_WIDTH), lambda i: (jnp.minimum((i + 1) * halo_per_tile, n_halo - 1), 0)),
            _row_spec(tm, seg_w[5]), _row_spec(tm, seg_w[6]),
            _row_spec(tm, D_MODEL), _row_spec(tm, D_MODEL),
            _full_spec((1, D_MODEL)), _full_spec((1, Q_LORA_RANK)), _full_spec((1, KV_LORA_RANK)), any_spec,
        ],
        out_specs=[_row_spec(tm, D_MODEL), _full_spec((small_rows, LANES)),
                   pl.BlockSpec((N_CHIPS, SHARD_COLS, tm), lambda i: (0, 0, i))],
        out_shape=[jax.ShapeDtypeStruct((s, D_MODEL), F32), jax.ShapeDtypeStruct((small_rows, LANES), F32),
                   jax.ShapeDtypeStruct((N_CHIPS, SHARD_COLS, s), BF16)],
        scratch_shapes=[
            pltpu.VMEM((IN_TOTAL, D_MODEL), BF16),
            pltpu.VMEM((tm + POOL_HALO, POOL_WIDTH), F32),
            pltpu.SemaphoreType.DMA,
            pltpu.VMEM((1, D_MODEL), F32),
        ],
        compiler_params=pltpu.CompilerParams(dimension_semantics=("arbitrary",), vmem_limit_bytes=VMEM_LIMIT),
    )(dzq, dzkv, dzkr, dgattn, ddc, ddc, dgpool, dgmerge, x, dh, norm_in, d_q_norm, d_kv_norm,
      w_in_t.reshape(IN_TOTAL, D_MODEL))


def _inproj_bwd_w(order, dz_sh, hn, g_uq, g_ukv, gs, tm):
    s = hn.shape[0]
    n_tiles = s // tm
    hc = D_MODEL // 2
    red = _Reduce(COMM_PARAMS[1:3])
    small = _SmallSum(gs.shape[0])
    n_red = len(red.scratch)

    def body(order_ref, dz_ref, hn_ref, guq_hbm, gukv_hbm, gs_ref, gw_hbm, guq_out, gukv_out, gsum_ref,
             acc, pm_w, a_w, b_w, r_w, w_send, w_recv, w_local, *more_scratch):
        ph, i = pl.program_id(0), pl.program_id(1)
        x, y, c = lax.axis_index("x"), lax.axis_index("y"), lax.axis_index("c")
        k = 2 * x + y
        me, sibling = (x, y, c), (x, y, 1 - c)
        chips = _other_chips(x, y)
        shard_of_phase = [2 * cx + cy for cx, cy in chips] + [k]
        copy = _remote_copier(w_send, w_recv)
        red.bind([guq_hbm, gukv_hbm], [guq_out, gukv_out], more_scratch[:n_red])
        small.bind(gs_ref, gsum_ref, more_scratch[n_red:])
        mine = pl.ds(pl.multiple_of(c * hc, hc), hc)
        theirs = pl.ds(pl.multiple_of((1 - c) * hc, hc), hc)

        def to_sibling(f):
            j = shard_of_phase[f]
            return copy(f, pm_w.at[j, 1 - c], a_w.at[j], sibling)

        def pair_sum(f):
            cx, cy = chips[f]
            return copy(4 + f, pm_w.at[shard_of_phase[f], c], b_w.at[f], (cx, cy, c))

        def finished():
            return copy(7, r_w, gw_hbm.at[:, mine], sibling)

        @pl.when(jnp.logical_and(ph == 0, i == 0))
        def _():
            red.start()
            small.start()

        part = _dot(dz_ref[0], hn_ref[...])

        @pl.when(i == 0)
        def _():
            acc[...] = part

        @pl.when(i > 0)
        def _():
            acc[...] += part

        for f in range(3):
            @pl.when(jnp.logical_and(ph == f + 1, i == 0))
            def _(f=f):
                j = shard_of_phase[f]
                copy(f, a_w.at[j], a_w.at[j], me).wait_recv()
                pm_w[j, c] = (pm_w[j, c].astype(F32) + a_w[j].astype(F32)).astype(BF16)
                pair_sum(f).start()
                if f == 0:
                    red.exchange()
                if f == 2:
                    red.finish_sum()

        for f in range(4):
            @pl.when(jnp.logical_and(ph == f, i == n_tiles - 1))
            def _(f=f):
                j = shard_of_phase[f]
                pm_w[j, 0] = acc[:, :hc].astype(BF16)
                pm_w[j, 1] = acc[:, hc:].astype(BF16)
                to_sibling(f).start()
                if f < 3:
                    return
                copy(3, a_w.at[k], a_w.at[k], me).wait_recv()
                r_w[...] = pm_w[k, c].astype(F32) + a_w[k].astype(F32)
                for g in range(3):
                    copy(4 + g, b_w.at[g], b_w.at[g], me).wait_recv()
                    r_w[...] = r_w[...] + b_w[g].astype(F32)
                store = pltpu.make_async_copy(r_w, gw_hbm.at[:, mine], w_local)
                store.start()
                finished().start()
                red.finish_wait()
                small.finish()
                copy(7, gw_hbm.at[:, theirs], gw_hbm.at[:, theirs], me).wait_recv()
                store.wait()
                for g in range(4):
                    to_sibling(g).wait_send()
                for g in range(3):
                    pair_sum(g).wait_send()
                finished().wait_send()

    any_spec = pl.BlockSpec(memory_space=pl.ANY)
    n_sem = 8
    grid_spec = pltpu.PrefetchScalarGridSpec(
        num_scalar_prefetch=1,
        grid=(N_CHIPS, n_tiles),
        in_specs=[
            pl.BlockSpec((1, SHARD_COLS, tm), lambda ph, i, order: (order[ph], 0, i)),
            pl.BlockSpec((tm, D_MODEL), lambda ph, i, order: (i, 0)),
            any_spec, any_spec,
            pl.BlockSpec(small.spec_shape, lambda ph, i, order: (0, 0)),
        ],
        out_specs=[any_spec, any_spec, any_spec, pl.BlockSpec(small.spec_shape, lambda ph, i, order: (0, 0))],
        scratch_shapes=[
            pltpu.VMEM((SHARD_COLS, D_MODEL), F32),
            pltpu.VMEM((N_CHIPS, 2, SHARD_COLS, hc), BF16),
            pltpu.VMEM((N_CHIPS, SHARD_COLS, hc), BF16),
            pltpu.VMEM((3, SHARD_COLS, hc), BF16),
            pltpu.VMEM((SHARD_COLS, hc), F32),
            pltpu.SemaphoreType.DMA((n_sem,)), pltpu.SemaphoreType.DMA((n_sem,)), pltpu.SemaphoreType.DMA,
        ] + red.scratch + small.scratch,
    )
    out = pl.pallas_call(
        body,
        name="inproj_bwd_w",
        grid_spec=grid_spec,
        out_shape=[jax.ShapeDtypeStruct((SHARD_COLS, D_MODEL), F32)] + red.out_shape
        + [small.out_shape],
        compiler_params=pltpu.CompilerParams(dimension_semantics=("arbitrary", "arbitrary"),
                                             vmem_limit_bytes=VMEM_LIMIT),
    )(order, dz_sh, hn, g_uq, g_ukv, gs)
    return out[0], out[1], out[2], out[3]


def _other_chips(x, y):
    return ((1 - x, 1 - y), (1 - x, y), (x, 1 - y))


def _half(ref, axis, size, c, lead=()):
    window = pl.ds(pl.multiple_of(c * size, size), size)
    if axis == 0:
        return ref.at[(*lead, window, slice(None))]
    return ref.at[(*lead, slice(None), window)]


def _half_shape(rows, cols, axis, size):
    return (size, cols) if axis == 0 else (rows, size)


def _remote_copier(send_sems, recv_sems):
    def copy(sem, src, dst, to):
        return pltpu.make_async_remote_copy(src_ref=src, dst_ref=dst, send_sem=send_sems.at[sem],
                                            recv_sem=recv_sems.at[sem], device_id=to, device_id_type=MESH)
    return copy


class _Gather:
    def __init__(self, params):
        self.params = params
        n = len(params)
        self.scratch = [pltpu.SemaphoreType.DMA((6 * n,)), pltpu.SemaphoreType.DMA((6 * n,)),
                        pltpu.SemaphoreType.DMA((n,))]
        self.out_shape = [jax.ShapeDtypeStruct((N_CHIPS, r, cc), BF16) for _, r, cc, _, _ in params]

    def bind(self, ins, outs, scratch):
        self.ins, self.outs = ins, outs
        send_sems, recv_sems, self.local_sems = scratch
        self.copy = _remote_copier(send_sems, recv_sems)
        self.x, self.y, self.c = lax.axis_index("x"), lax.axis_index("y"), lax.axis_index("c")
        self.k = 2 * self.x + self.y
        self.chips = _other_chips(self.x, self.y)

    def _local(self, p):
        return pltpu.make_async_copy(self.ins[p], self.outs[p].at[self.k], self.local_sems.at[p])

    def _first(self, p, j):
        _, _, _, axis, size = self.params[p]
        cx, cy = self.chips[j]
        return self.copy(6 * p + j, _half(self.ins[p], axis, size, self.c),
                         _half(self.outs[p], axis, size, self.c, (self.k,)), (cx, cy, self.c))

    def _relay(self, p, j, half_of):
        _, _, _, axis, size = self.params[p]
        cx, cy = self.chips[j]
        block = _half(self.outs[p], axis, size, half_of, (2 * cx + cy,))
        return self.copy(6 * p + 3 + j, block, block, (self.x, self.y, 1 - self.c))

    def start(self):
        for p in range(len(self.params)):
            self._local(p).start()
            for j in (1, 2, 0):
                self._first(p, j).start()

    def relay_one(self, p, j):
        _, _, _, axis, size = self.params[p]
        cx, cy = self.chips[j]
        landed = _half(self.outs[p], axis, size, self.c, (2 * cx + cy,))
        self.copy(6 * p + j, landed, landed, (self.x, self.y, self.c)).wait_recv()
        self._relay(p, j, self.c).start()

    def await_one(self, p, j):
        self._relay(p, j, 1 - self.c).wait_recv()

    def wait_sends(self):
        for p in range(len(self.params)):
            for j in range(3):
                self._first(p, j).wait_send()
                self._relay(p, j, self.c).wait_send()
            self._local(p).wait()

    def relay(self):
        for j in range(3):
            for p in range(len(self.params)):
                self.relay_one(p, j)

    def finish(self):
        for j in range(3):
            for p in range(len(self.params)):
                self.await_one(p, j)
        self.wait_sends()


class _Reduce:
    def __init__(self, params):
        self.params = params
        n = len(params)
        halves = [_half_shape(r, cc, axis, size) for _, r, cc, axis, size in params]
        self.scratch = ([pltpu.VMEM((N_CHIPS, *h), BF16) for h in halves]
                        + [pltpu.VMEM((N_CHIPS, *h), BF16) for h in halves]
                        + [pltpu.VMEM((3, *h), BF16) for h in halves]
                        + [pltpu.VMEM(h, F32) for h in halves]
                        + [pltpu.SemaphoreType.DMA((5 * n,)), pltpu.SemaphoreType.DMA((5 * n,)),
                           pltpu.SemaphoreType.DMA((2 * n,))])
        self.out_shape = [jax.ShapeDtypeStruct((r, cc), F32) for _, r, cc, _, _ in params]

    def bind(self, g_in, g_out, scratch):
        n = len(self.params)
        self.g_in, self.g_out = g_in, g_out
        self.pm, self.a_buf = scratch[0:n], scratch[n:2 * n]
        self.b_buf, self.r_buf = scratch[2 * n:3 * n], scratch[3 * n:4 * n]
        send_sems, recv_sems, self.local_sems = scratch[4 * n:]
        self.copy = _remote_copier(send_sems, recv_sems)
        self.x, self.y, self.c = lax.axis_index("x"), lax.axis_index("y"), lax.axis_index("c")
        self.k = 2 * self.x + self.y
        self.chips = _other_chips(self.x, self.y)
        self.me = (self.x, self.y, self.c)
        self.sibling = (self.x, self.y, 1 - self.c)

    def _load(self, p):
        _, _, _, axis, size = self.params[p]
        return pltpu.make_async_copy(_half(self.g_in[p], axis, size, self.c, (slice(None),)), self.pm[p],
                                     self.local_sems.at[p])

    def _to_sibling(self, p):
        _, _, _, axis, size = self.params[p]
        return self.copy(5 * p, _half(self.g_in[p], axis, size, 1 - self.c, (slice(None),)), self.a_buf[p],
                         self.sibling)

    def _pair_sum(self, p, j):
        cx, cy = self.chips[j]
        return self.copy(5 * p + 1 + j, self.pm[p].at[2 * cx + cy], self.b_buf[p].at[j], (cx, cy, self.c))

    def _store(self, p):
        _, _, _, axis, size = self.params[p]
        n = len(self.params)
        return pltpu.make_async_copy(self.r_buf[p], _half(self.g_out[p], axis, size, self.c),
                                     self.local_sems.at[n + p])

    def _finished(self, p):
        _, _, _, axis, size = self.params[p]
        return self.copy(5 * p + 4, self.r_buf[p], _half(self.g_out[p], axis, size, self.c), self.sibling)

    def start(self):
        for p in range(len(self.params)):
            self._load(p).start()
            self._to_sibling(p).start()

    def exchange(self):
        for p in range(len(self.params)):
            self._load(p).wait()
            self.copy(5 * p, self.a_buf[p], self.a_buf[p], self.me).wait_recv()
            for j, (cx, cy) in enumerate(self.chips):
                kj = 2 * cx + cy
                self.pm[p][kj] = (self.pm[p][kj].astype(F32) + self.a_buf[p][kj].astype(F32)).astype(BF16)
                self._pair_sum(p, j).start()
            self.r_buf[p][...] = self.pm[p][self.k].astype(F32) + self.a_buf[p][self.k].astype(F32)

    def finish(self):
        self.finish_sum()
        self.finish_wait()

    def finish_sum(self):
        for p in range(len(self.params)):
            for j in range(3):
                self.copy(5 * p + 1 + j, self.b_buf[p].at[j], self.b_buf[p].at[j], self.me).wait_recv()
                self.r_buf[p][...] = self.r_buf[p][...] + self.b_buf[p][j].astype(F32)
            self._store(p).start()
            self._finished(p).start()

    def finish_wait(self):
        for p, (_, _, _, axis, size) in enumerate(self.params):
            theirs = _half(self.g_out[p], axis, size, 1 - self.c)
            self.copy(5 * p + 4, theirs, theirs, self.me).wait_recv()
            self._store(p).wait()
            self._to_sibling(p).wait_send()
            for j in range(3):
                self._pair_sum(p, j).wait_send()
            self._finished(p).wait_send()


class _SmallSum:
    def __init__(self, rows):
        self.rows = rows
        self.scratch = [pltpu.VMEM((N_DEV, rows, LANES), F32),
                        pltpu.SemaphoreType.DMA((N_DEV - 1,)), pltpu.SemaphoreType.DMA((N_DEV - 1,))]
        self.out_shape = jax.ShapeDtypeStruct((rows, LANES), F32)
        self.spec_shape = (rows, LANES)

    def bind(self, src, dst, scratch):
        self.src, self.dst = src, dst
        self.buf, send_sems, recv_sems = scratch
        self.copy = _remote_copier(send_sems, recv_sems)
        self.x, self.y, self.c = lax.axis_index("x"), lax.axis_index("y"), lax.axis_index("c")

    def _send(self, f):
        fx, fy, fc = [(a, b, d) for a in (0, 1) for b in (0, 1) for d in (0, 1)][f]
        x, y, c = self.x, self.y, self.c
        peer = (1 - x if fx else x, 1 - y if fy else y, 1 - c if fc else c)
        return self.copy(f - 1, self.src, self.buf.at[f], peer)

    def start(self):
        for f in range(1, N_DEV):
            self._send(f).start()
        self.buf[0] = self.src[...]

    def finish(self):
        me = (self.x, self.y, self.c)
        for f in range(1, N_DEV):
            self.copy(f - 1, self.buf.at[f], self.buf.at[f], me).wait_recv()
        dev = 4 * self.x + 2 * self.y + self.c
        total = self.buf[dev]
        for d in range(1, N_DEV):
            total = total + self.buf[jnp.bitwise_xor(dev, d)]
        self.dst[...] = total
        for f in range(1, N_DEV):
            self._send(f).wait_send()


def _adamw_math(w, g, m, v):
    m = ADAM_B1 * m + (1.0 - ADAM_B1) * g
    v = ADAM_B2 * v + (1.0 - ADAM_B2) * (g * g)
    m_hat = m / (1.0 - ADAM_B1 ** ADAM_STEP)
    v_hat = v / (1.0 - ADAM_B2 ** ADAM_STEP)
    delta = -ADAM_LR * (m_hat / (jnp.sqrt(v_hat) + ADAM_EPS) + ADAM_WD * w)
    return delta, m, v


def _adamw_tiled(w, g, m, v, tm):
    rows, cols = w.shape

    def body(w_ref, g_ref, m_ref, v_ref, d_ref, nm_ref, nv_ref, g_out):
        g = g_ref[...]
        d_ref[...], nm_ref[...], nv_ref[...] = _adamw_math(w_ref[...], g, m_ref[...], v_ref[...])
        g_out[...] = g

    spec = _row_spec(tm, cols)
    return pl.pallas_call(
        body,
        name="adamw_w_in",
        grid=(rows // tm,),
        in_specs=[spec] * 4,
        out_specs=[spec] * 4,
        out_shape=[jax.ShapeDtypeStruct(w.shape, F32)] * 4,
        compiler_params=pltpu.CompilerParams(dimension_semantics=("parallel",), vmem_limit_bytes=VMEM_LIMIT),
    )(w, g, m, v)


def _adamw_many(ws, gs, ms, vs, sq_err):
    n = len(ws)
    gs = list(gs) + [sq_err]
    g_arrays, g_at = [], []
    for g in gs:
        arr, row = g if isinstance(g, tuple) else (g, None)
        k = next((j for j, a in enumerate(g_arrays) if a is arr), len(g_arrays))
        if k == len(g_arrays):
            g_arrays.append(arr)
        g_at.append((k, row))
    n_g = len(g_arrays)

    def body(*refs):
        w_refs, m_refs, v_refs, g_refs, outs = (refs[:n], refs[n:2 * n], refs[2 * n:3 * n], refs[3 * n:3 * n + n_g],
                                                refs[3 * n + n_g:])
        k, row = g_at[n]
        sq = g_refs[k][...] if row is None else g_refs[k][row:row + 8, :]
        outs[4 * n][...] = jnp.full((1, 1), 0.5 * jnp.sum(sq) / D_MODEL, F32)
        for i in range(n):
            k, row = g_at[i]
            if row is None and g_refs[k].ndim == 2 and w_refs[i].ndim == 3:
                cols = ws[i].shape[2]
                for h in range(ws[i].shape[1]):
                    outs[3 * n + i][:, h, :] = g_refs[k][:, h * cols:(h + 1) * cols]
                g = outs[3 * n + i][...]
            else:
                g = g_refs[k][...] if row is None else g_refs[k][row:row + ws[i].shape[0], :]
                outs[3 * n + i][...] = g
            d, nm, nv = _adamw_math(w_refs[i][...], g, m_refs[i][...], v_refs[i][...])
            outs[i][...] = d
            outs[n + i][...] = nm
            outs[2 * n + i][...] = nv

    vmem_spec = pl.BlockSpec(memory_space=pltpu.VMEM)
    shapes = [jax.ShapeDtypeStruct(w.shape, F32) for w in ws]
    out = pl.pallas_call(
        body,
        name="adamw_small",
        in_specs=[vmem_spec] * (3 * n + n_g),
        out_specs=[vmem_spec] * (4 * n + 1),
        out_shape=shapes * 4 + [jax.ShapeDtypeStruct((1, 1), F32)],
        compiler_params=pltpu.CompilerParams(vmem_limit_bytes=VMEM_LIMIT),
    )(*ws, *ms, *vs, *g_arrays)
    return out[:n], out[n:2 * n], out[2 * n:3 * n], out[3 * n:4 * n], out[4 * n]


def _rope_tables(s):
    half = QK_ROPE_DIM // 2
    inv_freq = np.float32(ROPE_THETA) ** (-np.arange(half, dtype=np.float32) / np.float32(half))
    ang = (np.arange(s, dtype=np.float32)[:, None] * inv_freq[None, :]).astype(np.float32)
    cos, sin = np.cos(ang.astype(np.float64)).astype(np.float32), np.sin(ang.astype(np.float64)).astype(np.float32)
    z16 = np.zeros((s, half), np.float32)
    z32 = np.zeros((s, HEAD_PAD - QK_NOPE_DIM - QK_ROPE_DIM), np.float32)
    z64 = np.zeros((s, QK_NOPE_DIM), np.float32)
    rc = np.concatenate([np.ones((s, QK_NOPE_DIM), np.float32), cos, cos, z32], axis=1)
    rsa = np.concatenate([z64, -sin, z16, z32], axis=1)
    rsb = np.concatenate([z64, z16, sin, z32], axis=1)
    return jnp.asarray(rc), jnp.asarray(rsa), jnp.asarray(rsb)


def kernel(x, norm_in, w_in, q_norm, w_uq, kv_norm, w_ukv, pool_w, pool_scale, w_branch_attn, w_branch_pool, w_out, norm_final, loss_target, m_norm_in, m_w_in, m_q_norm, m_w_uq, m_kv_norm, m_w_ukv, m_pool_w, m_pool_scale, m_w_branch_attn, m_w_branch_pool, m_w_out, m_norm_final, v_norm_in, v_w_in, v_q_norm, v_w_uq, v_kv_norm, v_w_ukv, v_pool_w, v_pool_scale, v_w_branch_attn, v_w_branch_pool, v_w_out, v_norm_final):
    s = x.shape[1]
    t_att, t_row = _tiles(s)
    x2 = x.reshape(s, D_MODEL)
    tgt = loss_target.reshape(s, D_MODEL)

    cx, cy = lax.axis_index("x"), lax.axis_index("y")
    others = [2 * ox + oy for ox, oy in _other_chips(cx, cy)]
    hn, z_sh, (w_in_t, w_uq_all, w_ukv_all) = _inproj_fwd(
        jnp.stack([2 * cx + cy, others[1], others[2], others[0]]).astype(jnp.int32), x2, norm_in.reshape(1, -1),
        w_in.T.astype(BF16), [w_uq, w_ukv], 4 * t_row)
    rc, rsa, rsb = _rope_tables(s)
    g_in = norm_in.reshape(1, -1)
    g_q = q_norm.reshape(-1, LANES)
    g_kv = kv_norm.reshape(1, -1)
    g_f = norm_final.reshape(1, -1)
    ps = pool_scale.reshape(1, -1)

    q, k, v, q_t, v_t, wuq_p, wk_p, wv, *late, pw_bf = _qkv_fwd(
        z_sh, g_q, g_kv, w_uq_all, w_ukv_all, rc, rsa, rsb, [w_branch_attn, w_branch_pool, w_out, pool_w], 2 * t_row)
    o, lse, (w_ba_all, w_bp_all, w_out_all) = _attn_fwd(q_t, k, v_t, late, t_att)
    w_out_f = w_out_all.reshape(D_MODEL, D_MODEL)

    do, delta, dgattn, dgpool, dgmerge, ddc, dh, gs_mid, d_w_out, d_w_ba, d_w_bp = _mid(o, z_sh, x2, tgt, pw_bf, ps, w_ba_all, w_bp_all, w_out_f, g_f, t_row)

    late_grads = [d_w_ba, d_w_bp, d_w_out.reshape(N_CHIPS, 256, D_MODEL)]
    dq, dk_t, dv_t, (g_w_ba, g_w_bp, g_w_out), g_small_mid = _attn_bwd(q, q_t, k, v, do, lse, delta, late_grads,
                                                                      gs_mid, t_att)
    dzq, dzkv, dzkr, d_w_uq, d_w_ukv, d_q_norm, d_kv_norm = _qkv_bwd(
        dq, dk_t, dv_t, z_sh, g_q, g_kv, wuq_p, wk_p, wv, rc, rsa, rsb, 2 * t_row)
    grad_x, gs, dz_sh = _inproj_bwd_x(dzq, dzkv, dzkr, dgattn, ddc, dgpool, dgmerge, x2, dh, g_in, d_q_norm, d_kv_norm,
                                      w_in_t, 2 * t_row)

    order = jnp.stack(others + [2 * cx + cy]).astype(jnp.int32)
    g_w_in_t, g_w_uq, g_w_ukv, g_small = _inproj_bwd_w(
        order, dz_sh, hn, d_w_uq, d_w_ukv, gs, 4 * t_row)

    dl_w_in, nm_w_in, nv_w_in, g_w_in = (a.T for a in _adamw_tiled(w_in.T, g_w_in_t, m_w_in.T, v_w_in.T, 152))

    packed = {n: (g_small_mid, r) for n, r in _first_rows(SMALL_MID).items() if n != "sq_err"}
    packed.update({n: (g_small, r) for n, r in _first_rows(SMALL_LATE).items()})

    def as_rows(n, a):
        return a.reshape(-1, LANES) if n in packed else a

    names = ["norm_in", "q_norm", "w_uq", "kv_norm", "w_ukv", "pool_w", "pool_scale", "w_branch_attn",
             "w_branch_pool", "w_out", "norm_final"]
    ws = dict(norm_in=norm_in, q_norm=q_norm, w_uq=w_uq, kv_norm=kv_norm, w_ukv=w_ukv, pool_w=pool_w,
              pool_scale=pool_scale, w_branch_attn=w_branch_attn, w_branch_pool=w_branch_pool, w_out=w_out,
              norm_final=norm_final)
    gsd = dict(packed, w_uq=g_w_uq, w_ukv=g_w_ukv, w_branch_attn=g_w_ba, w_branch_pool=g_w_bp, w_out=g_w_out)
    msd = dict(norm_in=m_norm_in, q_norm=m_q_norm, w_uq=m_w_uq, kv_norm=m_kv_norm, w_ukv=m_w_ukv, pool_w=m_pool_w,
               pool_scale=m_pool_scale, w_branch_attn=m_w_branch_attn, w_branch_pool=m_w_branch_pool, w_out=m_w_out,
               norm_final=m_norm_final)
    vsd = dict(norm_in=v_norm_in, q_norm=v_q_norm, w_uq=v_w_uq, kv_norm=v_kv_norm, w_ukv=v_w_ukv, pool_w=v_pool_w,
               pool_scale=v_pool_scale, w_branch_attn=v_w_branch_attn, w_branch_pool=v_w_branch_pool, w_out=v_w_out,
               norm_final=v_norm_final)
    dls, nms, nvs, g_outs, loss = _adamw_many(
        [as_rows(n, ws[n]) for n in names], [gsd[n] for n in names], [as_rows(n, msd[n]) for n in names],
        [as_rows(n, vsd[n]) for n in names], (g_small_mid, _first_rows(SMALL_MID)["sq_err"]))

    grads = dict(zip(names, g_outs))
    grads["w_in"] = g_w_in
    delta_w = {n: d.reshape(ws[n].shape) for n, d in zip(names, dls)}
    new_m = {n: d.reshape(ws[n].shape) for n, d in zip(names, nms)}
    new_v = {n: d.reshape(ws[n].shape) for n, d in zip(names, nvs)}
    delta_w["w_in"], new_m["w_in"], new_v["w_in"] = dl_w_in, nm_w_in, nv_w_in
    ws["w_in"] = w_in

    order = ["norm_in", "w_in", "q_norm", "w_uq", "kv_norm", "w_ukv", "pool_w", "pool_scale", "w_branch_attn",
             "w_branch_pool", "w_out", "norm_final"]
    return (loss.reshape(()), grad_x.reshape(x.shape),
            *[grads[n].reshape(ws[n].shape) for n in order],
            *[delta_w[n] for n in order], *[new_m[n] for n in order], *[new_v[n] for n in order])
```

```python
import functools

import jax
import jax.numpy as jnp
import numpy as np
from jax import lax
from jax.experimental import pallas as pl
from jax.experimental.pallas import tpu as pltpu

F32 = jnp.float32
BF16 = jnp.bfloat16
MESH = pl.DeviceIdType.MESH

D_MODEL = 1024
CHUNK = 64
MLA_HEADS = 8
QK_NOPE_DIM = 64
QK_ROPE_DIM = 32
V_HEAD_DIM = 64
Q_LORA_RANK = 384
KV_LORA_RANK = 256
MLA_WIDTH = MLA_HEADS * V_HEAD_DIM
ROPE_THETA = 10000.0
POOL_WINDOWS = (2, 4, 8, 16)
POOL_WIDTH = 512
POOL_GROUP_DIM = 128
BRANCH_COLS = D_MODEL // 4
FWD_HEADS = 8
BWD_HEADS = 4
POOL_HALO = 16
EPS = 1e-6
IN_TOTAL = 4256
HEAD_PAD = 128
ATT_SCALE = (QK_NOPE_DIM + QK_ROPE_DIM) ** -0.5
ATT_SCALE_LOG2E = ATT_SCALE * 1.4426950408889634

ADAM_LR = 0.001
ADAM_B1 = 0.9
ADAM_B2 = 0.999
ADAM_EPS = 1e-08
ADAM_WD = 0.01
ADAM_STEP = 10

N_CHIPS = 4
N_DEV = 8
LANES = 128
VMEM_LIMIT = 60 * 1024 * 1024

IN_SEGMENTS = ((384, 384), (256, 256), (32, HEAD_PAD), (512, 512), (512, 512), (512, 512), (2048, 2048))
SHARD_COLS = IN_TOTAL // N_CHIPS
ZQ_COLS = slice(0, 384)
ZKV_COLS = slice(384, 640)
ZKR_TILE = slice(640, 768)


def _shard_pieces():
    bounds, off = [], 0
    for w, _ in IN_SEGMENTS:
        bounds.append((off, off + w))
        off += w
    out = []
    for j in range(N_CHIPS):
        lo, hi = SHARD_COLS * j, SHARD_COLS * (j + 1)
        out.append([(i, max(lo, a) - a, min(hi, b) - a, max(lo, a) - lo)
                    for i, (a, b) in enumerate(bounds) if max(lo, a) < min(hi, b)])
    return out


SHARD_PIECES = _shard_pieces()


def _segment(z_blocks, seg):
    parts = [z_blocks[j][:, col:col + hi - lo]
             for j, pieces in enumerate(SHARD_PIECES) for sg, lo, hi, col in pieces if sg == seg]
    return parts[0] if len(parts) == 1 else jnp.concatenate(parts, axis=1)

COMM_PARAMS = (
    ("w_in", SHARD_COLS, D_MODEL, 1, 512),
    ("w_uq", 96, 768, 0, 48),
    ("w_ukv", 64, 1024, 0, 32),
    ("w_branch_attn", 512, 256, 0, 256),
    ("w_branch_pool", 512, 256, 0, 256),
    ("w_out", 256, 1024, 0, 128),
)

SMALL_MID = (
    ("pool_w", (4, 128, 128)),
    ("norm_final", (1024,)),
    ("sq_err", (8, 128)),
    ("pool_scale", (512,)),
)
SMALL_LATE = (
    ("norm_in", (1024,)),
    ("q_norm", (384,)),
    ("kv_norm", (256,)),
)


def _small_rows(shapes):
    return -(-sum(int(np.prod(s)) for _, s in shapes) // (LANES * 8)) * 8


def _first_rows(shapes):
    out, off = {}, 0
    for name, shp in shapes:
        out[name], rem = divmod(off, LANES)
        assert rem == 0, name
        off += int(np.prod(shp))
    return out


def _pack_into(dst_ref, shapes, values):
    first = _first_rows(shapes)
    for name, shp in shapes:
        v, row = values[name], first[name]
        if v.ndim == 3:
            for g in range(v.shape[0]):
                dst_ref[row + g * v.shape[1]:row + (g + 1) * v.shape[1], :] = v[g]
        elif v.shape[0] == 1 and v.shape[1] > LANES:
            for j in range(v.shape[1] // LANES):
                dst_ref[row + j:row + j + 1, :] = v[:, j * LANES:(j + 1) * LANES]
        else:
            dst_ref[row:row + v.shape[0], :] = v
    used = sum(int(np.prod(shp)) for _, shp in shapes) // LANES
    if used < dst_ref.shape[0]:
        dst_ref[used:, :] = jnp.zeros((dst_ref.shape[0] - used, LANES), dst_ref.dtype)


def _as_one_row(g):
    return jnp.concatenate([g[j:j + 1] for j in range(g.shape[0])], axis=1)


def _dot(a, b):
    return jnp.dot(a, b, preferred_element_type=F32)


def _dot_nt(a, b):
    return lax.dot_general(a, b, (((1,), (1,)), ((), ())), preferred_element_type=F32)


def _dot_tn(a, b):
    return lax.dot_general(a, b, (((0,), (0,)), ((), ())), preferred_element_type=F32)


def _sigmoid(x):
    return 1.0 / (1.0 + jnp.exp(-x))


def _colsum(x):
    return jnp.sum(x, axis=0, keepdims=True)


def _rms_fwd(x, g):
    r = lax.rsqrt(jnp.mean(x * x, axis=-1, keepdims=True) + EPS)
    xhat = x * r
    return xhat * g, xhat, r


def _rms_bwd(dy, xhat, r, g):
    dxhat = dy * g
    return r * (dxhat - xhat * jnp.mean(dxhat * xhat, axis=-1, keepdims=True))


def _rope(v, c, sa, sb):
    return v * c + pltpu.roll(v, 112, 1) * sa + pltpu.roll(v, 16, 1) * sb


def _unrope(d, c, sa, sb):
    return d * c + pltpu.roll(d * sa, 16, 1) + pltpu.roll(d * sb, 112, 1)


def _row_spec(tm, n):
    return pl.BlockSpec((tm, n), lambda i: (i, 0))


def _full_spec(shape):
    nd = len(shape)
    return pl.BlockSpec(shape, lambda i: (0,) * nd)


def _tiles(s):
    t_att = 512 if s >= 2048 else 128
    t_row = 256 if s >= 1024 else 128
    return t_att, t_row


def _inproj_fwd(order, x, norm_in, w_in_shard, up_shards, tm):
    s = x.shape[0]
    n_tiles = s // tm
    gat = _Gather(COMM_PARAMS[:3])
    n_w = len(gat.params)
    arrival = (1, 2, 0)
    n_up = len(up_shards)

    def body(order_ref, x_ref, g_ref, w_in_loc, *rest):
        up_refs, (hn_ref, z_ref), w_all = rest[:n_up], rest[n_up:n_up + 2], rest[n_up + 2:n_up + 2 + n_w]
        rest = rest[n_up + 2 + n_w:]
        (w_vmem, hn_all, w_sem), up_flat = rest[:3], rest[3:3 + n_up]
        gat.bind((w_in_loc,) + tuple(up_flat), w_all, rest[3 + n_up:])
        ph, i = pl.program_id(0), pl.program_id(1)

        @pl.when(jnp.logical_and(ph == 0, i == 0))
        def _():
            for src, dst in zip(up_refs, up_flat):
                dst[...] = jnp.concatenate([src[:, h, :] for h in range(MLA_HEADS)], axis=1).astype(BF16)
            gat.start()

        def fetch(phase):
            src = w_in_loc if phase == 0 else w_all[0].at[order_ref[phase]]
            return pltpu.make_async_copy(src, w_vmem.at[phase % 2], w_sem.at[phase % 2])

        def landed(f):
            gat.relay_one(0, arrival[f])
            gat.await_one(0, arrival[f])

        @pl.when(jnp.logical_and(ph == 0, i == 0))
        def _():
            fetch(0).start()
            fetch(0).wait()

        @pl.when(jnp.logical_and(ph == 1, i == 0))
        def _():
            landed(0)
            fetch(1).start()
            fetch(1).wait()

        for f in (1, 2):
            @pl.when(jnp.logical_and(ph == f, i == n_tiles - 1))
            def _(f=f):
                landed(f)
                fetch(f + 1).start()

            @pl.when(jnp.logical_and(ph == f + 1, i == 0))
            def _(f=f):
                fetch(f + 1).wait()

        rows = pl.ds(pl.multiple_of(i * tm, tm), tm)

        @pl.when(ph == 0)
        def _():
            hn, _, _ = _rms_fwd(x_ref[...], g_ref[...])
            hn = hn.astype(BF16)
            hn_ref[...] = hn
            hn_all[rows, :] = hn

        z_ref[0] = _dot_nt(hn_all[rows, :], w_vmem[ph % 2])

        @pl.when(jnp.logical_and(ph == N_CHIPS - 1, i == n_tiles - 1))
        def _():
            for p in range(1, n_w):
                for j in range(3):
                    gat.relay_one(p, j)
            for p in range(1, n_w):
                for j in range(3):
                    gat.await_one(p, j)
            gat.wait_sends()

    def tile_in_phase0(ph, i, order):
        return (jnp.where(ph == 0, i, n_tiles - 1), 0)

    any_spec = pl.BlockSpec(memory_space=pl.ANY)
    grid_spec = pltpu.PrefetchScalarGridSpec(
        num_scalar_prefetch=1,
        grid=(N_CHIPS, n_tiles),
        in_specs=[pl.BlockSpec((tm, D_MODEL), tile_in_phase0),
                  pl.BlockSpec((1, D_MODEL), lambda ph, i, order: (0, 0)), any_spec]
        + [pl.BlockSpec(a.shape, lambda ph, i, order: (0, 0, 0)) for a in up_shards],
        out_specs=[pl.BlockSpec((tm, D_MODEL), tile_in_phase0),
                   pl.BlockSpec((1, tm, SHARD_COLS), lambda ph, i, order: (order[ph], i, 0))] + [any_spec] * n_w,
        scratch_shapes=[pltpu.VMEM((2, SHARD_COLS, D_MODEL), BF16), pltpu.VMEM((s, D_MODEL), BF16),
                        pltpu.SemaphoreType.DMA((2,))]
        + [pltpu.VMEM((r, cc), BF16) for _, r, cc, _, _ in gat.params[1:]] + gat.scratch,
    )
    out = pl.pallas_call(
        body,
        name="inproj_fwd",
        grid_spec=grid_spec,
        out_shape=[jax.ShapeDtypeStruct((s, D_MODEL), BF16), jax.ShapeDtypeStruct((N_CHIPS, s, SHARD_COLS), F32)]
        + gat.out_shape,
        compiler_params=pltpu.CompilerParams(dimension_semantics=("arbitrary", "arbitrary"),
                                             vmem_limit_bytes=VMEM_LIMIT),
    )(order, x, norm_in, w_in_shard, *up_shards)
    return out[0], out[1], out[2:]


def _qkv_fwd(z_sh, q_norm, kv_norm, w_uq_all, w_ukv_all, rc, rsa, rsb, to_bf16, tm):
    s = z_sh.shape[1]
    n_cast = len(to_bf16)
    hw = MLA_HEADS * HEAD_PAD
    qk = QK_NOPE_DIM + QK_ROPE_DIM

    def body(z_ref, gq_ref, gkv_ref, uq_ref, ukv_ref, c_ref, sa_ref, sb_ref, *rest):
        f32_refs, rest = rest[:n_cast], rest[n_cast:]
        (q_ref, k_ref, v_ref, qt_ref, vt_ref, wuq_ref, wk_ref, wv_ref), bf_refs = rest[:8], rest[8:]

        @pl.when(pl.program_id(0) == 0)
        def _():
            for src, dst in zip(f32_refs, bf_refs):
                dst[...] = src[...].astype(BF16)
            w_q = jnp.concatenate([uq_ref[j] for j in range(N_CHIPS)], axis=0).astype(F32)
            gap = jnp.zeros((Q_LORA_RANK, HEAD_PAD - qk), F32)
            wuq_ref[...] = jnp.concatenate(
                [part for h in range(MLA_HEADS) for part in (w_q[:, h * qk:(h + 1) * qk], gap)], axis=1).astype(BF16)
            w_kv = jnp.concatenate([ukv_ref[j] for j in range(N_CHIPS)], axis=0).astype(F32)
            lane = lax.broadcasted_iota(jnp.int32, w_kv.shape, 1)
            wk_ref[...] = jnp.where(lane % HEAD_PAD < QK_NOPE_DIM, w_kv, 0.0).astype(BF16)
            wv_ref[...] = jnp.concatenate(
                [w_kv[:, h * HEAD_PAD + QK_NOPE_DIM:(h + 1) * HEAD_PAD] for h in range(MLA_HEADS)],
                axis=1).astype(BF16)

        c, sa, sb = c_ref[...], sa_ref[...], sb_ref[...]
        z0 = z_ref[0]
        cq, _, _ = _rms_fwd(z0[:, ZQ_COLS], _as_one_row(gq_ref[...]))
        qf = _dot(cq.astype(BF16), wuq_ref[...])
        ckv, _, _ = _rms_fwd(z0[:, ZKV_COLS], gkv_ref[...])
        ckv = ckv.astype(BF16)
        kn = _dot(ckv, wk_ref[...])
        lane = lax.broadcasted_iota(jnp.int32, (tm, HEAD_PAD), 1)
        zkr = jnp.where(lane < QK_ROPE_DIM, z0[:, ZKR_TILE], 0.0)
        kr = _rope(pltpu.roll(zkr, 64, 1), c, sa, sb)
        for h in range(MLA_HEADS):
            cols = slice(h * HEAD_PAD, (h + 1) * HEAD_PAD)
            qh = _rope(qf[:, cols], c, sa, sb)
            q_ref[:, cols] = qh.astype(BF16)
            qt_ref[cols, :] = qh.T.astype(BF16)
            k_ref[:, cols] = (kn[:, cols] + kr).astype(BF16)
        vf = _dot(ckv, wv_ref[...])
        v_ref[...] = vf.astype(BF16)
        vt_ref[...] = vf.T.astype(BF16)

    return pl.pallas_call(
        body,
        name="qkv_fwd",
        grid=(s // tm,),
        in_specs=[
            pl.BlockSpec((1, tm, SHARD_COLS), lambda i: (0, i, 0)),
            _full_spec(q_norm.shape), _full_spec((1, KV_LORA_RANK)),
            _full_spec(w_uq_all.shape), _full_spec(w_ukv_all.shape),
            _row_spec(tm, HEAD_PAD), _row_spec(tm, HEAD_PAD), _row_spec(tm, HEAD_PAD),
        ] + [_full_spec(a.shape) for a in to_bf16],
        out_specs=[_row_spec(tm, hw), _row_spec(tm, hw), _row_spec(tm, MLA_WIDTH),
                   pl.BlockSpec((hw, tm), lambda i: (0, i)), pl.BlockSpec((MLA_WIDTH, tm), lambda i: (0, i)),
                   _full_spec((Q_LORA_RANK, hw)), _full_spec((KV_LORA_RANK, hw)), _full_spec((KV_LORA_RANK, MLA_WIDTH))
                   ] + [_full_spec(a.shape) for a in to_bf16],
        out_shape=[jax.ShapeDtypeStruct((s, hw), BF16), jax.ShapeDtypeStruct((s, hw), BF16),
                   jax.ShapeDtypeStruct((s, MLA_WIDTH), BF16),
                   jax.ShapeDtypeStruct((hw, s), BF16), jax.ShapeDtypeStruct((MLA_WIDTH, s), BF16),
                   jax.ShapeDtypeStruct((Q_LORA_RANK, hw), BF16), jax.ShapeDtypeStruct((KV_LORA_RANK, hw), BF16),
                   jax.ShapeDtypeStruct((KV_LORA_RANK, MLA_WIDTH), BF16)
                   ] + [jax.ShapeDtypeStruct(a.shape, BF16) for a in to_bf16],
        compiler_params=pltpu.CompilerParams(dimension_semantics=("arbitrary",), vmem_limit_bytes=VMEM_LIMIT),
    )(z_sh, q_norm, kv_norm, w_uq_all, w_ukv_all, rc, rsa, rsb, *to_bf16)


def _chunk_mask(t, keys_on_rows):
    rows = lax.broadcasted_iota(jnp.int32, (t, t), 0) // CHUNK
    cols = lax.broadcasted_iota(jnp.int32, (t, t), 1) // CHUNK
    return rows <= cols if keys_on_rows else cols <= rows


def _attn_fwd(q_t, k, v_t, late_shards, t):
    s = k.shape[0]
    groups = MLA_HEADS // FWD_HEADS
    n_q = s // t
    gat = _Gather(COMM_PARAMS[3:])
    n_w = len(gat.params)

    def body(qt_ref, k_ref, k2_ref, vt_ref, *rest):
        w_in, (o_ref, lse_ref), w_out = rest[:n_w], rest[n_w:n_w + 2], rest[n_w + 2:2 * n_w + 2]
        gat.bind(w_in, w_out, rest[2 * n_w + 2:])
        i = pl.program_id(1)
        step_no = pl.program_id(0) * n_q + i
        pl.when(step_no == 0)(gat.start)
        pl.when(step_no == groups * n_q // 2)(gat.relay)
        mask = _chunk_mask(t, True)
        qcs = [slice(hh * HEAD_PAD, (hh + 1) * HEAD_PAD) for hh in range(FWD_HEADS)]
        vcs = [slice(hh * V_HEAD_DIM, (hh + 1) * V_HEAD_DIM) for hh in range(FWD_HEADS)]
        qts = [qt_ref[qc, :] for qc in qcs]

        def step(j, carry, masked):
            keys = pl.ds(pl.multiple_of(j * t, t), t)
            out = []
            for hh in range(FWD_HEADS):
                m, l, acc = carry[hh]
                sc = _dot(k_ref[keys, qcs[hh]], qts[hh])
                if masked:
                    sc = jnp.where(mask, sc, -jnp.inf)
                m_new = jnp.maximum(m, jnp.max(sc, axis=0, keepdims=True))
                alpha = jnp.exp2((m - m_new) * ATT_SCALE_LOG2E)
                p = jnp.exp2((_dot(k2_ref[keys, qcs[hh]], qts[hh]) - m_new) * ATT_SCALE_LOG2E)
                if masked:
                    p = jnp.where(mask, p, 0.0)
                l = alpha * l + jnp.sum(p, axis=0, keepdims=True)
                acc = alpha * acc + _dot(vt_ref[vcs[hh], keys], p.astype(BF16))
                out.append((m_new, l, acc))
            return tuple(out)

        one = (jnp.full((1, t), -jnp.inf, F32), jnp.zeros((1, t), F32), jnp.zeros((V_HEAD_DIM, t), F32))
        carry = lax.fori_loop(0, i, functools.partial(step, masked=False), (one,) * FWD_HEADS)
        carry = step(i, carry, True)
        o_ref[...] = jnp.concatenate([carry[hh][2] / carry[hh][1] for hh in range(FWD_HEADS)], axis=0).T
        for hh in range(FWD_HEADS):
            m, l, _ = carry[hh]
            lse_ref[:, qcs[hh]] = jnp.broadcast_to(m * ATT_SCALE_LOG2E + jnp.log2(l), (HEAD_PAD, t)).T
        pl.when(step_no == groups * n_q - 1)(gat.finish)

    any_spec = pl.BlockSpec(memory_space=pl.ANY)
    out = pl.pallas_call(
        body,
        name="attn_fwd",
        grid=(groups, n_q),
        in_specs=[
            pl.BlockSpec((FWD_HEADS * HEAD_PAD, t), lambda p, i: (p, i)),
            pl.BlockSpec((s, FWD_HEADS * HEAD_PAD), lambda p, i: (0, p), pipeline_mode=pl.Buffered(1)),
            pl.BlockSpec((s, FWD_HEADS * HEAD_PAD), lambda p, i: (0, p), pipeline_mode=pl.Buffered(1)),
            pl.BlockSpec((FWD_HEADS * V_HEAD_DIM, s), lambda p, i: (p, 0), pipeline_mode=pl.Buffered(1)),
        ] + [any_spec] * n_w,
        out_specs=[
            pl.BlockSpec((t, FWD_HEADS * V_HEAD_DIM), lambda p, i: (i, p)),
            pl.BlockSpec((t, FWD_HEADS * HEAD_PAD), lambda p, i: (i, p)),
        ] + [any_spec] * n_w,
        out_shape=[jax.ShapeDtypeStruct((s, MLA_WIDTH), F32), jax.ShapeDtypeStruct((s, MLA_HEADS * HEAD_PAD), F32)]
        + gat.out_shape,
        scratch_shapes=gat.scratch,
        compiler_params=pltpu.CompilerParams(dimension_semantics=("arbitrary", "arbitrary"),
                                             vmem_limit_bytes=VMEM_LIMIT),
    )(q_t, k, k, v_t, *late_shards)
    return out[0], out[1], out[2:]


def _mid(o, z_sh, x, target, pool_w, pool_scale, w_ba, w_bp, w_out, norm_final, tm):
    s = x.shape[0]
    n_tiles = s // tm
    halo_per_tile = tm // POOL_HALO
    small_rows = _small_rows(SMALL_MID)

    def body(o_ref, z0_ref, z1_ref, z1h_ref, z2_ref, z3_ref, x_ref, t_ref, pw_ref, ps_ref, wba_ref, wbp_ref,
             wout_ref, gf_ref,
             do_ref, dl_ref, dga_ref, dgp_ref, dgm_ref, ddc_ref, dh_ref,
             small_ref, dwout_out, dwba_out, dwbp_out,
             ubuf, dwout_ref, dwba_ref, dwbp_ref, loss_ref, dpw_ref, dps_ref, dgf_ref):
        i = pl.program_id(0)

        @pl.when(i == 0)
        def _():
            loss_ref[...] = jnp.zeros_like(loss_ref)
            dwout_ref[...] = jnp.zeros_like(dwout_ref)
            dwba_ref[...] = jnp.zeros_like(dwba_ref)
            dwbp_ref[...] = jnp.zeros_like(dwbp_ref)
            dpw_ref[...] = jnp.zeros_like(dpw_ref)
            dps_ref[...] = jnp.zeros_like(dps_ref)
            dgf_ref[...] = jnp.zeros_like(dgf_ref)

        zs = [z0_ref[0], z1_ref[0], z2_ref[0], z3_ref[0]]
        o = o_ref[...]
        ga = _segment(zs, 3)
        sga = _sigmoid(ga)
        silu_a = ga * sga
        y_attn = (o * silu_a).astype(BF16)

        ubuf[0:POOL_HALO, :] = jnp.where(i > 0, _segment([None, z1h_ref[0]], 4), 0.0)
        ubuf[POOL_HALO:, :] = _segment(zs, 4)
        row = lax.broadcasted_iota(jnp.int32, (tm, POOL_GROUP_DIM), 0) + i * tm
        ps = ps_ref[...]
        gp = _segment(zs, 5)
        sgp = _sigmoid(gp)
        silu_p = gp * sgp
        d_bf, dm, inv_cnt = [], [], []
        for g, w in enumerate(POOL_WINDOWS):
            cols = slice(g * POOL_GROUP_DIM, (g + 1) * POOL_GROUP_DIM)
            wsum = ubuf[POOL_HALO:, cols]
            for kk in range(1, w):
                wsum = wsum + ubuf[POOL_HALO - kk:POOL_HALO - kk + tm, cols]
            inv = 1.0 / jnp.minimum(row + 1, w).astype(F32)
            dg = (wsum * inv - ubuf[POOL_HALO:, cols]).astype(BF16)
            d_bf.append(dg)
            inv_cnt.append(inv)
            dm.append(_dot(dg, pw_ref[g]))
        dm = jnp.concatenate(dm, axis=1)
        yp = dm * ps
        y_pool = (yp * silu_p).astype(BF16)

        a = jnp.concatenate([_dot(y_attn, wba_ref[j]) for j in range(N_CHIPS)], axis=1)
        p = jnp.concatenate([_dot(y_pool, wbp_ref[j]) for j in range(N_CHIPS)], axis=1)
        gm = _segment(zs, 6)
        gate_a = _sigmoid(gm[:, :D_MODEL])
        gate_p = _sigmoid(gm[:, D_MODEL:])
        merged = (gate_a * a + gate_p * p).astype(BF16)
        h = x_ref[...] + _dot(merged, wout_ref[...])
        gf = gf_ref[...]
        y, xhat, r = _rms_fwd(h, gf)
        err = y - t_ref[...]
        e2 = err * err
        e2 = jnp.sum(e2.reshape(tm // 8, 8, D_MODEL), axis=0)
        acc = e2[:, 0:LANES]
        for cidx in range(1, D_MODEL // LANES):
            acc = acc + e2[:, cidx * LANES:(cidx + 1) * LANES]
        loss_ref[...] += acc

        dy = err * (1.0 / D_MODEL)
        dgf_ref[...] += _colsum(dy * xhat)
        dh = _rms_bwd(dy, xhat, r, gf)
        dh_ref[...] = dh
        dh_bf = dh.astype(BF16)
        dwout_ref[...] += _dot_tn(merged, dh_bf)
        dmerged = _dot_nt(dh_bf, wout_ref[...])
        da = (dmerged * gate_a).astype(BF16)
        dp = (dmerged * gate_p).astype(BF16)
        dgm_ref[:, :D_MODEL] = (dmerged * a * gate_a * (1.0 - gate_a)).astype(BF16)
        dgm_ref[:, D_MODEL:] = (dmerged * p * gate_p * (1.0 - gate_p)).astype(BF16)
        dy_attn = dy_pool = None
        for j in range(N_CHIPS):
            cols = slice(j * BRANCH_COLS, (j + 1) * BRANCH_COLS)
            dwba_ref[j] += _dot_tn(y_attn, da[:, cols])
            dwbp_ref[j] += _dot_tn(y_pool, dp[:, cols])
            pa = _dot_nt(da[:, cols], wba_ref[j])
            pp = _dot_nt(dp[:, cols], wbp_ref[j])
            dy_attn = pa if dy_attn is None else dy_attn + pa
            dy_pool = pp if dy_pool is None else dy_pool + pp

        do = dy_attn * silu_a
        do_ref[...] = do
        dga_ref[...] = (dy_attn * o * (sga * (1.0 + ga * (1.0 - sga)))).astype(BF16)
        doo = do * o
        for hd in range(MLA_HEADS):
            dl = jnp.sum(doo[:, hd * V_HEAD_DIM:(hd + 1) * V_HEAD_DIM], axis=1, keepdims=True)
            dl_ref[:, hd * HEAD_PAD:(hd + 1) * HEAD_PAD] = jnp.broadcast_to(dl, (tm, HEAD_PAD))

        dyp = dy_pool * silu_p
        dgp_ref[...] = (dy_pool * yp * (sgp * (1.0 + gp * (1.0 - sgp)))).astype(BF16)
        dps_ref[...] += _colsum(dyp * dm)
        dmm = (dyp * ps).astype(BF16)
        for g in range(len(POOL_WINDOWS)):
            cols = slice(g * POOL_GROUP_DIM, (g + 1) * POOL_GROUP_DIM)
            dpw_ref[g] += _dot_tn(d_bf[g], dmm[:, cols])
            ddc_ref[:, cols] = _dot_nt(dmm[:, cols], pw_ref[g]) * inv_cnt[g]

        @pl.when(i == n_tiles - 1)
        def _():
            dwout_out[...] = dwout_ref[...].astype(BF16)
            dwba_out[...] = dwba_ref[...].astype(BF16)
            dwbp_out[...] = dwbp_ref[...].astype(BF16)
            _pack_into(small_ref, SMALL_MID, dict(pool_w=dpw_ref[...], norm_final=dgf_ref[...], sq_err=loss_ref[...],
                                                  pool_scale=dps_ref[...]))

    row_in = lambda n: _row_spec(tm, n)
    in_specs = [
        row_in(MLA_WIDTH),
        pl.BlockSpec((1, tm, SHARD_COLS), lambda i: (0, i, 0)), pl.BlockSpec((1, tm, SHARD_COLS), lambda i: (1, i, 0)),
        pl.BlockSpec((1, POOL_HALO, SHARD_COLS), lambda i: (1, jnp.maximum(i * halo_per_tile - 1, 0), 0)),
        pl.BlockSpec((1, tm, SHARD_COLS), lambda i: (2, i, 0)), pl.BlockSpec((1, tm, SHARD_COLS), lambda i: (3, i, 0)),
        row_in(D_MODEL), row_in(D_MODEL),
        _full_spec((4, POOL_GROUP_DIM, POOL_GROUP_DIM)), _full_spec((1, POOL_WIDTH)),
        _full_spec((N_CHIPS, MLA_WIDTH, BRANCH_COLS)), _full_spec((N_CHIPS, POOL_WIDTH, BRANCH_COLS)),
        _full_spec((D_MODEL, D_MODEL)), _full_spec((1, D_MODEL)),
    ]
    out_shape = [
        jax.ShapeDtypeStruct((s, MLA_WIDTH), F32),
        jax.ShapeDtypeStruct((s, MLA_HEADS * HEAD_PAD), F32),
        jax.ShapeDtypeStruct((s, MLA_WIDTH), BF16),
        jax.ShapeDtypeStruct((s, POOL_WIDTH), BF16),
        jax.ShapeDtypeStruct((s, 2 * D_MODEL), BF16),
        jax.ShapeDtypeStruct((s, POOL_WIDTH), F32),
        jax.ShapeDtypeStruct((s, D_MODEL), F32),
        jax.ShapeDtypeStruct((small_rows, LANES), F32),
        jax.ShapeDtypeStruct((D_MODEL, D_MODEL), BF16),
        jax.ShapeDtypeStruct((N_CHIPS, MLA_WIDTH, BRANCH_COLS), BF16),
        jax.ShapeDtypeStruct((N_CHIPS, POOL_WIDTH, BRANCH_COLS), BF16),
    ]
    out_specs = [
        row_in(MLA_WIDTH), row_in(MLA_HEADS * HEAD_PAD), row_in(MLA_WIDTH), row_in(POOL_WIDTH),
        row_in(2 * D_MODEL), row_in(POOL_WIDTH), row_in(D_MODEL),
        _full_spec((small_rows, LANES)), _full_spec((D_MODEL, D_MODEL)),
        _full_spec((N_CHIPS, MLA_WIDTH, BRANCH_COLS)), _full_spec((N_CHIPS, POOL_WIDTH, BRANCH_COLS)),
    ]
    return pl.pallas_call(
        body,
        name="mid",
        grid=(n_tiles,),
        in_specs=in_specs,
        out_specs=out_specs,
        out_shape=out_shape,
        scratch_shapes=[
            pltpu.VMEM((tm + POOL_HALO, POOL_WIDTH), F32),
            pltpu.VMEM((D_MODEL, D_MODEL), F32),
            pltpu.VMEM((N_CHIPS, MLA_WIDTH, BRANCH_COLS), F32),
            pltpu.VMEM((N_CHIPS, POOL_WIDTH, BRANCH_COLS), F32),
            pltpu.VMEM((8, LANES), F32),
            pltpu.VMEM((4, POOL_GROUP_DIM, POOL_GROUP_DIM), F32),
            pltpu.VMEM((1, POOL_WIDTH), F32),
            pltpu.VMEM((1, D_MODEL), F32),
        ],
        compiler_params=pltpu.CompilerParams(dimension_semantics=("arbitrary",), vmem_limit_bytes=VMEM_LIMIT),
    )(o, z_sh, z_sh, z_sh, z_sh, z_sh, x, target, pool_w, pool_scale, w_ba, w_bp, w_out, norm_final)


def _attn_bwd(q, q_t, k, v, do, lse, delta, late_grads, gs_mid, t):
    s = q.shape[0]
    groups = MLA_HEADS // BWD_HEADS
    n_q = s // t
    red = _Reduce(COMM_PARAMS[3:])
    n_w = len(red.params)
    small = _SmallSum(gs_mid.shape[0])
    n_red = len(red.scratch)

    def body(q_ref, qt_ref, do_ref, lse_ref, dl_ref, k_ref, v_ref, *rest):
        g_in, gs_ref = rest[:n_w], rest[n_w]
        (dq_ref, dk_ref, dv_ref), g_out, gsum_ref = rest[n_w + 1:n_w + 4], rest[n_w + 4:2 * n_w + 4], rest[2 * n_w + 4]
        scratch = rest[2 * n_w + 5:]
        red.bind(g_in, g_out, scratch[:n_red])
        small.bind(gs_ref, gsum_ref, scratch[n_red:])
        i = pl.program_id(1)
        step_no = pl.program_id(0) * n_q + i

        @pl.when(step_no == 0)
        def _():
            red.start()
            small.start()

        pl.when(step_no == groups * n_q // 2)(red.exchange)
        @pl.when(step_no == groups * n_q - 1)
        def _():
            red.finish_sum()
            small.finish()

        @pl.when(i == 0)
        def _():
            dk_ref[...] = jnp.zeros_like(dk_ref)
            dv_ref[...] = jnp.zeros_like(dv_ref)

        mask = _chunk_mask(t, False)
        qcs = [slice(hh * HEAD_PAD, (hh + 1) * HEAD_PAD) for hh in range(BWD_HEADS)]
        vcs = [slice(hh * V_HEAD_DIM, (hh + 1) * V_HEAD_DIM) for hh in range(BWD_HEADS)]
        qhs = [q_ref[:, qc] for qc in qcs]
        qts = [qt_ref[qc, :] for qc in qcs]
        dohs = [do_ref[:, vc].astype(BF16) for vc in vcs]
        do_t = do_ref[...].T.astype(BF16)
        dots = [do_t[vc, :] for vc in vcs]
        lses = [jnp.tile(lse_ref[:, qc], (1, t // HEAD_PAD)) for qc in qcs]
        dls = [jnp.tile(dl_ref[:, qc], (1, t // HEAD_PAD)) for qc in qcs]

        def step(j, dqs, masked):
            keys = pl.ds(pl.multiple_of(j * t, t), t)
            out = []
            for hh in range(BWD_HEADS):
                kj = k_ref[keys, qcs[hh]]
                vj = v_ref[keys, vcs[hh]]
                p = jnp.exp2(_dot_nt(qhs[hh], kj) * ATT_SCALE_LOG2E - lses[hh])
                if masked:
                    p = jnp.where(mask, p, 0.0)
                ds = (p * (_dot_nt(dohs[hh], vj) - dls[hh])).astype(BF16)
                dv_ref[vcs[hh], keys] += _dot(dots[hh], p.astype(BF16))
                dk_ref[qcs[hh], keys] += _dot(qts[hh], ds) * ATT_SCALE
                out.append(dqs[hh] + _dot(ds, kj))
            return tuple(out)

        zero = jnp.zeros((t, HEAD_PAD), F32)
        dqs = lax.fori_loop(0, i, functools.partial(step, masked=False), (zero,) * BWD_HEADS)
        dqs = step(i, dqs, True)
        for hh in range(BWD_HEADS):
            dq_ref[:, qcs[hh]] = dqs[hh] * ATT_SCALE

        @pl.when(step_no == groups * n_q - 1)
        def _():
            red.finish_wait()

    hw = MLA_HEADS * HEAD_PAD
    any_spec = pl.BlockSpec(memory_space=pl.ANY)
    out = pl.pallas_call(
        body,
        name="attn_bwd",
        grid=(groups, n_q),
        in_specs=[
            pl.BlockSpec((t, BWD_HEADS * HEAD_PAD), lambda p, i: (i, p)),
            pl.BlockSpec((BWD_HEADS * HEAD_PAD, t), lambda p, i: (p, i)),
            pl.BlockSpec((t, BWD_HEADS * V_HEAD_DIM), lambda p, i: (i, p)),
            pl.BlockSpec((t, BWD_HEADS * HEAD_PAD), lambda p, i: (i, p)),
            pl.BlockSpec((t, BWD_HEADS * HEAD_PAD), lambda p, i: (i, p)),
            pl.BlockSpec((s, BWD_HEADS * HEAD_PAD), lambda p, i: (0, p), pipeline_mode=pl.Buffered(1)),
            pl.BlockSpec((s, BWD_HEADS * V_HEAD_DIM), lambda p, i: (0, p), pipeline_mode=pl.Buffered(1)),
        ] + [any_spec] * n_w + [pl.BlockSpec(small.spec_shape, lambda p, i: (0, 0))],
        out_specs=[
            pl.BlockSpec((t, BWD_HEADS * HEAD_PAD), lambda p, i: (i, p)),
            pl.BlockSpec((BWD_HEADS * HEAD_PAD, s), lambda p, i: (p, 0)),
            pl.BlockSpec((BWD_HEADS * V_HEAD_DIM, s), lambda p, i: (p, 0)),
        ] + [any_spec] * n_w + [pl.BlockSpec(small.spec_shape, lambda p, i: (0, 0))],
        out_shape=[jax.ShapeDtypeStruct((s, hw), F32), jax.ShapeDtypeStruct((hw, s), F32),
                   jax.ShapeDtypeStruct((MLA_WIDTH, s), F32)] + red.out_shape + [small.out_shape],
        scratch_shapes=red.scratch + small.scratch,
        compiler_params=pltpu.CompilerParams(dimension_semantics=("arbitrary", "arbitrary"),
                                             vmem_limit_bytes=VMEM_LIMIT),
    )(q, q_t, do, lse, delta, k, v, *late_grads, gs_mid)
    return out[0], out[1], out[2], out[3:3 + n_w], out[3 + n_w]


def _qkv_bwd(dq, dk_t, dv_t, z_sh, q_norm, kv_norm, wuq_p, wk_p, wv, rc, rsa, rsb, tm):
    s = z_sh.shape[1]
    hw = MLA_HEADS * HEAD_PAD
    n_tiles = s // tm
    uq_shape, ukv_shape = (N_CHIPS,) + COMM_PARAMS[1][1:3], (N_CHIPS,) + COMM_PARAMS[2][1:3]

    def body(dq_ref, dk_ref, dv_ref, z_ref, gq_ref, gkv_ref, wuq_ref, wk_ref, wv_ref,
             c_ref, sa_ref, sb_ref,
             dzq_ref, dzkv_ref, dzkr_ref, duq_ref, dukv_ref, dgq_ref, dgkv_ref, dwuq_ref, dwk_ref, dwv_ref):
        i = pl.program_id(0)

        @pl.when(i == 0)
        def _():
            dwuq_ref[...] = jnp.zeros_like(dwuq_ref)
            dwk_ref[...] = jnp.zeros_like(dwk_ref)
            dwv_ref[...] = jnp.zeros_like(dwv_ref)
            dgq_ref[...] = jnp.zeros_like(dgq_ref)
            dgkv_ref[...] = jnp.zeros_like(dgkv_ref)

        c, sa, sb = c_ref[...], sa_ref[...], sb_ref[...]
        gq, gkv = _as_one_row(gq_ref[...]), gkv_ref[...]

        z0 = z_ref[0]
        cq, xq, rq = _rms_fwd(z0[:, ZQ_COLS], gq)
        dqp = jnp.concatenate(
            [_unrope(dq_ref[:, h * HEAD_PAD:(h + 1) * HEAD_PAD], c, sa, sb) for h in range(MLA_HEADS)],
            axis=1).astype(BF16)
        dwuq_ref[...] += _dot_tn(cq.astype(BF16), dqp)
        dcq = _dot_nt(dqp, wuq_ref[...])
        dgq_ref[...] += _colsum(dcq * xq)
        dzq_ref[...] = _rms_bwd(dcq, xq, rq, gq).astype(BF16)

        ckv, xkv, rkv = _rms_fwd(z0[:, ZKV_COLS], gkv)
        ckv = ckv.astype(BF16)
        dkf = dk_ref[...].T
        dk_bf = dkf.astype(BF16)
        dv_bf = dv_ref[...].T.astype(BF16)
        dwk_ref[...] += _dot_tn(ckv, dk_bf)
        dwv_ref[...] += _dot_tn(ckv, dv_bf)
        dckv = _dot_nt(dk_bf, wk_ref[...]) + _dot_nt(dv_bf, wv_ref[...])
        dgkv_ref[...] += _colsum(dckv * xkv)
        dzkv_ref[...] = _rms_bwd(dckv, xkv, rkv, gkv).astype(BF16)

        dkr = dkf[:, 0:HEAD_PAD]
        for h in range(1, MLA_HEADS):
            dkr = dkr + dkf[:, h * HEAD_PAD:(h + 1) * HEAD_PAD]
        dkr = pltpu.roll(_unrope(dkr, c, sa, sb), 64, 1)
        lane = lax.broadcasted_iota(jnp.int32, (tm, HEAD_PAD), 1)
        dzkr_ref[...] = jnp.where(lane < QK_ROPE_DIM, dkr, 0.0).astype(BF16)

        @pl.when(i == n_tiles - 1)
        def _():
            qk = QK_NOPE_DIM + QK_ROPE_DIM
            d_uq = jnp.concatenate([dwuq_ref[:, h * HEAD_PAD:h * HEAD_PAD + qk] for h in range(MLA_HEADS)],
                                   axis=1).astype(BF16)
            d_ukv = jnp.concatenate(
                [part for h in range(MLA_HEADS)
                 for part in (dwk_ref[:, h * HEAD_PAD:h * HEAD_PAD + QK_NOPE_DIM],
                              dwv_ref[:, h * V_HEAD_DIM:(h + 1) * V_HEAD_DIM])], axis=1).astype(BF16)
            for j in range(N_CHIPS):
                duq_ref[j] = d_uq[j * uq_shape[1]:(j + 1) * uq_shape[1]]
                dukv_ref[j] = d_ukv[j * ukv_shape[1]:(j + 1) * ukv_shape[1]]

    return pl.pallas_call(
        body,
        name="qkv_bwd",
        grid=(s // tm,),
        in_specs=[
            _row_spec(tm, hw), pl.BlockSpec((hw, tm), lambda i: (0, i)), pl.BlockSpec((MLA_WIDTH, tm), lambda i: (0, i)),
            pl.BlockSpec((1, tm, SHARD_COLS), lambda i: (0, i, 0)),
            _full_spec(q_norm.shape), _full_spec((1, KV_LORA_RANK)),
            _full_spec((Q_LORA_RANK, hw)), _full_spec((KV_LORA_RANK, hw)), _full_spec((KV_LORA_RANK, MLA_WIDTH)),
            _row_spec(tm, HEAD_PAD), _row_spec(tm, HEAD_PAD), _row_spec(tm, HEAD_PAD),
        ],
        out_specs=[
            _row_spec(tm, Q_LORA_RANK), _row_spec(tm, KV_LORA_RANK), _row_spec(tm, HEAD_PAD),
            _full_spec(uq_shape), _full_spec(ukv_shape),
            _full_spec((1, Q_LORA_RANK)), _full_spec((1, KV_LORA_RANK)),
        ],
        out_shape=[
            jax.ShapeDtypeStruct((s, Q_LORA_RANK), BF16), jax.ShapeDtypeStruct((s, KV_LORA_RANK), BF16),
            jax.ShapeDtypeStruct((s, HEAD_PAD), BF16),
            jax.ShapeDtypeStruct(uq_shape, BF16), jax.ShapeDtypeStruct(ukv_shape, BF16),
            jax.ShapeDtypeStruct((1, Q_LORA_RANK), F32), jax.ShapeDtypeStruct((1, KV_LORA_RANK), F32),
        ],
        scratch_shapes=[pltpu.VMEM((Q_LORA_RANK, hw), F32), pltpu.VMEM((KV_LORA_RANK, hw), F32),
                        pltpu.VMEM((KV_LORA_RANK, MLA_WIDTH), F32)],
        compiler_params=pltpu.CompilerParams(dimension_semantics=("arbitrary",), vmem_limit_bytes=VMEM_LIMIT),
    )(dq, dk_t, dv_t, z_sh, q_norm, kv_norm, wuq_p, wk_p, wv, rc, rsa, rsb)


def _inproj_bwd_x(dzq, dzkv, dzkr, dgattn, ddc, dgpool, dgmerge, x, dh, norm_in, d_q_norm, d_kv_norm, w_in_t, tm):
    s = x.shape[0]
    n_tiles = s // tm
    halo_per_tile = tm // POOL_HALO
    n_halo = s // POOL_HALO
    u_seg = 4
    small_rows = _small_rows(SMALL_LATE)

    def body(dzq_ref, dzkv_ref, dzkr_ref, dga_ref, ddc_ref, ddn_ref, dgp_ref, dgm_ref, x_ref, dh_ref,
             g_ref, dgq_ref, dgkv_ref, w_hbm, gx_ref, small_ref, dzs_ref, w_vmem, dbuf, sem, dgin_ref):
        i = pl.program_id(0)

        @pl.when(i == 0)
        def _():
            cp = pltpu.make_async_copy(w_hbm, w_vmem, sem)
            cp.start()
            dgin_ref[...] = jnp.zeros_like(dgin_ref)
            cp.wait()

        dbuf[0:tm, :] = ddc_ref[...]
        dbuf[tm:, :] = jnp.where(i < n_tiles - 1, ddn_ref[...], 0.0)
        row = lax.broadcasted_iota(jnp.int32, (tm, POOL_GROUP_DIM), 0) + i * tm
        du = []
        for g, w in enumerate(POOL_WINDOWS):
            cols = slice(g * POOL_GROUP_DIM, (g + 1) * POOL_GROUP_DIM)
            fsum = dbuf[0:tm, cols]
            for kk in range(1, w):
                fsum = fsum + dbuf[kk:kk + tm, cols]
            du.append(fsum - dbuf[0:tm, cols] * jnp.minimum(row + 1, w).astype(F32))
        du = jnp.concatenate(du, axis=1).astype(BF16)

        dz = [dzq_ref[...], dzkv_ref[...], dzkr_ref[...], dga_ref[...], du, dgp_ref[...], dgm_ref[...]]
        dz = jnp.concatenate([d[:, :w] for d, (w, _) in zip(dz, IN_SEGMENTS)], axis=1)
        for j in range(N_CHIPS):
            dzs_ref[j] = dz[:, j * SHARD_COLS:(j + 1) * SHARD_COLS].T
        dhn = _dot(dz, w_vmem[...])

        g = g_ref[...]
        _, xhat, r = _rms_fwd(x_ref[...], g)
        dgin_ref[...] += _colsum(dhn * xhat)
        gx_ref[...] = dh_ref[...] + _rms_bwd(dhn, xhat, r, g)

        @pl.when(i == n_tiles - 1)
        def _():
            _pack_into(small_ref, SMALL_LATE, dict(norm_in=dgin_ref[...], q_norm=dgq_ref[...], kv_norm=dgkv_ref[...]))

    any_spec = pl.BlockSpec(memory_space=pl.ANY)
    seg_w = [wide for _, wide in IN_SEGMENTS]
    return pl.pallas_call(
        body,
        name="inproj_bwd_x",
        grid=(n_tiles,),
        in_specs=[
            _row_spec(tm, seg_w[0]), _row_spec(tm, seg_w[1]), _row_spec(tm, seg_w[2]),
            _row_spec(tm, seg_w[3]), _row_spec(tm, seg_w[u_seg]),
            pl.BlockSpec((POOL_HALO, POOL_WIDTH), lambda i: (jnp.minimum((i + 1) * halo_per_tile, n_halo - 1), 0)),
            _row_spec(tm, seg_w[5]), _row_spec(tm, seg_w[6]),
            _row_spec(tm, D_MODEL), _row_spec(tm, D_MODEL),
            _full_spec((1, D_MODEL)), _full_spec((1, Q_LORA_RANK)), _full_spec((1, KV_LORA_RANK)), any_spec,
        ],
        out_specs=[_row_spec(tm, D_MODEL), _full_spec((small_rows, LANES)),
                   pl.BlockSpec((N_CHIPS, SHARD_COLS, tm), lambda i: (0, 0, i))],
        out_shape=[jax.ShapeDtypeStruct((s, D_MODEL), F32), jax.ShapeDtypeStruct((small_rows, LANES), F32),
                   jax.ShapeDtypeStruct((N_CHIPS, SHARD_COLS, s), BF16)],
        scratch_shapes=[
            pltpu.VMEM((IN_TOTAL, D_MODEL), BF16),
            pltpu.VMEM((tm + POOL_HALO, POOL_WIDTH), F32),
            pltpu.SemaphoreType.DMA,
            pltpu.VMEM((1, D_MODEL), F32),
        ],
        compiler_params=pltpu.CompilerParams(dimension_semantics=("arbitrary",), vmem_limit_bytes=VMEM_LIMIT),
    )(dzq, dzkv, dzkr, dgattn, ddc, ddc, dgpool, dgmerge, x, dh, norm_in, d_q_norm, d_kv_norm,
      w_in_t.reshape(IN_TOTAL, D_MODEL))


def _inproj_bwd_w(order, dz_sh, hn, g_uq, g_ukv, gs, tm):
    s = hn.shape[0]
    n_tiles = s // tm
    hc = D_MODEL // 2
    red = _Reduce(COMM_PARAMS[1:3])
    small = _SmallSum(gs.shape[0])
    n_red = len(red.scratch)

    def body(order_ref, dz_ref, hn_ref, guq_hbm, gukv_hbm, gs_ref, gw_hbm, guq_out, gukv_out, gsum_ref,
             acc, pm_w, a_w, b_w, r_w, w_send, w_recv, w_local, *more_scratch):
        ph, i = pl.program_id(0), pl.program_id(1)
        x, y, c = lax.axis_index("x"), lax.axis_index("y"), lax.axis_index("c")
        k = 2 * x + y
        me, sibling = (x, y, c), (x, y, 1 - c)
        chips = _other_chips(x, y)
        shard_of_phase = [2 * cx + cy for cx, cy in chips] + [k]
        copy = _remote_copier(w_send, w_recv)
        red.bind([guq_hbm, gukv_hbm], [guq_out, gukv_out], more_scratch[:n_red])
        small.bind(gs_ref, gsum_ref, more_scratch[n_red:])
        mine = pl.ds(pl.multiple_of(c * hc, hc), hc)
        theirs = pl.ds(pl.multiple_of((1 - c) * hc, hc), hc)

        def to_sibling(f):
            j = shard_of_phase[f]
            return copy(f, pm_w.at[j, 1 - c], a_w.at[j], sibling)

        def pair_sum(f):
            cx, cy = chips[f]
            return copy(4 + f, pm_w.at[shard_of_phase[f], c], b_w.at[f], (cx, cy, c))

        def finished():
            return copy(7, r_w, gw_hbm.at[:, mine], sibling)

        @pl.when(jnp.logical_and(ph == 0, i == 0))
        def _():
            red.start()
            small.start()

        part = _dot(dz_ref[0], hn_ref[...])

        @pl.when(i == 0)
        def _():
            acc[...] = part

        @pl.when(i > 0)
        def _():
            acc[...] += part

        for f in range(3):
            @pl.when(jnp.logical_and(ph == f + 1, i == 0))
            def _(f=f):
                j = shard_of_phase[f]
                copy(f, a_w.at[j], a_w.at[j], me).wait_recv()
                pm_w[j, c] = (pm_w[j, c].astype(F32) + a_w[j].astype(F32)).astype(BF16)
                pair_sum(f).start()
                if f == 0:
                    red.exchange()
                if f == 2:
                    red.finish_sum()

        for f in range(4):
            @pl.when(jnp.logical_and(ph == f, i == n_tiles - 1))
            def _(f=f):
                j = shard_of_phase[f]
                pm_w[j, 0] = acc[:, :hc].astype(BF16)
                pm_w[j, 1] = acc[:, hc:].astype(BF16)
                to_sibling(f).start()
                if f < 3:
                    return
                copy(3, a_w.at[k], a_w.at[k], me).wait_recv()
                r_w[...] = pm_w[k, c].astype(F32) + a_w[k].astype(F32)
                for g in range(3):
                    copy(4 + g, b_w.at[g], b_w.at[g], me).wait_recv()
                    r_w[...] = r_w[...] + b_w[g].astype(F32)
                store = pltpu.make_async_copy(r_w, gw_hbm.at[:, mine], w_local)
                store.start()
                finished().start()
                red.finish_wait()
                small.finish()
                copy(7, gw_hbm.at[:, theirs], gw_hbm.at[:, theirs], me).wait_recv()
                store.wait()
                for g in range(4):
                    to_sibling(g).wait_send()
                for g in range(3):
                    pair_sum(g).wait_send()
                finished().wait_send()

    any_spec = pl.BlockSpec(memory_space=pl.ANY)
    n_sem = 8
    grid_spec = pltpu.PrefetchScalarGridSpec(
        num_scalar_prefetch=1,
        grid=(N_CHIPS, n_tiles),
        in_specs=[
            pl.BlockSpec((1, SHARD_COLS, tm), lambda ph, i, order: (order[ph], 0, i)),
            pl.BlockSpec((tm, D_MODEL), lambda ph, i, order: (i, 0)),
            any_spec, any_spec,
            pl.BlockSpec(small.spec_shape, lambda ph, i, order: (0, 0)),
        ],
        out_specs=[any_spec, any_spec, any_spec, pl.BlockSpec(small.spec_shape, lambda ph, i, order: (0, 0))],
        scratch_shapes=[
            pltpu.VMEM((SHARD_COLS, D_MODEL), F32),
            pltpu.VMEM((N_CHIPS, 2, SHARD_COLS, hc), BF16),
            pltpu.VMEM((N_CHIPS, SHARD_COLS, hc), BF16),
            pltpu.VMEM((3, SHARD_COLS, hc), BF16),
            pltpu.VMEM((SHARD_COLS, hc), F32),
            pltpu.SemaphoreType.DMA((n_sem,)), pltpu.SemaphoreType.DMA((n_sem,)), pltpu.SemaphoreType.DMA,
        ] + red.scratch + small.scratch,
    )
    out = pl.pallas_call(
        body,
        name="inproj_bwd_w",
        grid_spec=grid_spec,
        out_shape=[jax.ShapeDtypeStruct((SHARD_COLS, D_MODEL), F32)] + red.out_shape
        + [small.out_shape],
        compiler_params=pltpu.CompilerParams(dimension_semantics=("arbitrary", "arbitrary"),
                                             vmem_limit_bytes=VMEM_LIMIT),
    )(order, dz_sh, hn, g_uq, g_ukv, gs)
    return out[0], out[1], out[2], out[3]


def _other_chips(x, y):
    return ((1 - x, 1 - y), (1 - x, y), (x, 1 - y))


def _half(ref, axis, size, c, lead=()):
    window = pl.ds(pl.multiple_of(c * size, size), size)
    if axis == 0:
        return ref.at[(*lead, window, slice(None))]
    return ref.at[(*lead, slice(None), window)]


def _half_shape(rows, cols, axis, size):
    return (size, cols) if axis == 0 else (rows, size)


def _remote_copier(send_sems, recv_sems):
    def copy(sem, src, dst, to):
        return pltpu.make_async_remote_copy(src_ref=src, dst_ref=dst, send_sem=send_sems.at[sem],
                                            recv_sem=recv_sems.at[sem], device_id=to, device_id_type=MESH)
    return copy


class _Gather:
    def __init__(self, params):
        self.params = params
        n = len(params)
        self.scratch = [pltpu.SemaphoreType.DMA((6 * n,)), pltpu.SemaphoreType.DMA((6 * n,)),
                        pltpu.SemaphoreType.DMA((n,))]
        self.out_shape = [jax.ShapeDtypeStruct((N_CHIPS, r, cc), BF16) for _, r, cc, _, _ in params]

    def bind(self, ins, outs, scratch):
        self.ins, self.outs = ins, outs
        send_sems, recv_sems, self.local_sems = scratch
        self.copy = _remote_copier(send_sems, recv_sems)
        self.x, self.y, self.c = lax.axis_index("x"), lax.axis_index("y"), lax.axis_index("c")
        self.k = 2 * self.x + self.y
        self.chips = _other_chips(self.x, self.y)

    def _local(self, p):
        return pltpu.make_async_copy(self.ins[p], self.outs[p].at[self.k], self.local_sems.at[p])

    def _first(self, p, j):
        _, _, _, axis, size = self.params[p]
        cx, cy = self.chips[j]
        return self.copy(6 * p + j, _half(self.ins[p], axis, size, self.c),
                         _half(self.outs[p], axis, size, self.c, (self.k,)), (cx, cy, self.c))

    def _relay(self, p, j, half_of):
        _, _, _, axis, size = self.params[p]
        cx, cy = self.chips[j]
        block = _half(self.outs[p], axis, size, half_of, (2 * cx + cy,))
        return self.copy(6 * p + 3 + j, block, block, (self.x, self.y, 1 - self.c))

    def start(self):
        for p in range(len(self.params)):
            self._local(p).start()
            for j in (1, 2, 0):
                self._first(p, j).start()

    def relay_one(self, p, j):
        _, _, _, axis, size = self.params[p]
        cx, cy = self.chips[j]
        landed = _half(self.outs[p], axis, size, self.c, (2 * cx + cy,))
        self.copy(6 * p + j, landed, landed, (self.x, self.y, self.c)).wait_recv()
        self._relay(p, j, self.c).start()

    def await_one(self, p, j):
        self._relay(p, j, 1 - self.c).wait_recv()

    def wait_sends(self):
        for p in range(len(self.params)):
            for j in range(3):
                self._first(p, j).wait_send()
                self._relay(p, j, self.c).wait_send()
            self._local(p).wait()

    def relay(self):
        for j in range(3):
            for p in range(len(self.params)):
                self.relay_one(p, j)

    def finish(self):
        for j in range(3):
            for p in range(len(self.params)):
                self.await_one(p, j)
        self.wait_sends()


class _Reduce:
    def __init__(self, params):
        self.params = params
        n = len(params)
        halves = [_half_shape(r, cc, axis, size) for _, r, cc, axis, size in params]
        self.scratch = ([pltpu.VMEM((N_CHIPS, *h), BF16) for h in halves]
                        + [pltpu.VMEM((N_CHIPS, *h), BF16) for h in halves]
                        + [pltpu.VMEM((3, *h), BF16) for h in halves]
                        + [pltpu.VMEM(h, F32) for h in halves]
                        + [pltpu.SemaphoreType.DMA((5 * n,)), pltpu.SemaphoreType.DMA((5 * n,)),
                           pltpu.SemaphoreType.DMA((2 * n,))])
        self.out_shape = [jax.ShapeDtypeStruct((r, cc), F32) for _, r, cc, _, _ in params]

    def bind(self, g_in, g_out, scratch):
        n = len(self.params)
        self.g_in, self.g_out = g_in, g_out
        self.pm, self.a_buf = scratch[0:n], scratch[n:2 * n]
        self.b_buf, self.r_buf = scratch[2 * n:3 * n], scratch[3 * n:4 * n]
        send_sems, recv_sems, self.local_sems = scratch[4 * n:]
        self.copy = _remote_copier(send_sems, recv_sems)
        self.x, self.y, self.c = lax.axis_index("x"), lax.axis_index("y"), lax.axis_index("c")
        self.k = 2 * self.x + self.y
        self.chips = _other_chips(self.x, self.y)
        self.me = (self.x, self.y, self.c)
        self.sibling = (self.x, self.y, 1 - self.c)

    def _load(self, p):
        _, _, _, axis, size = self.params[p]
        return pltpu.make_async_copy(_half(self.g_in[p], axis, size, self.c, (slice(None),)), self.pm[p],
                                     self.local_sems.at[p])

    def _to_sibling(self, p):
        _, _, _, axis, size = self.params[p]
        return self.copy(5 * p, _half(self.g_in[p], axis, size, 1 - self.c, (slice(None),)), self.a_buf[p],
                         self.sibling)

    def _pair_sum(self, p, j):
        cx, cy = self.chips[j]
        return self.copy(5 * p + 1 + j, self.pm[p].at[2 * cx + cy], self.b_buf[p].at[j], (cx, cy, self.c))

    def _store(self, p):
        _, _, _, axis, size = self.params[p]
        n = len(self.params)
        return pltpu.make_async_copy(self.r_buf[p], _half(self.g_out[p], axis, size, self.c),
                                     self.local_sems.at[n + p])

    def _finished(self, p):
        _, _, _, axis, size = self.params[p]
        return self.copy(5 * p + 4, self.r_buf[p], _half(self.g_out[p], axis, size, self.c), self.sibling)

    def start(self):
        for p in range(len(self.params)):
            self._load(p).start()
            self._to_sibling(p).start()

    def exchange(self):
        for p in range(len(self.params)):
            self._load(p).wait()
            self.copy(5 * p, self.a_buf[p], self.a_buf[p], self.me).wait_recv()
            for j, (cx, cy) in enumerate(self.chips):
                kj = 2 * cx + cy
                self.pm[p][kj] = (self.pm[p][kj].astype(F32) + self.a_buf[p][kj].astype(F32)).astype(BF16)
                self._pair_sum(p, j).start()
            self.r_buf[p][...] = self.pm[p][self.k].astype(F32) + self.a_buf[p][self.k].astype(F32)

    def finish(self):
        self.finish_sum()
        self.finish_wait()

    def finish_sum(self):
        for p in range(len(self.params)):
            for j in range(3):
                self.copy(5 * p + 1 + j, self.b_buf[p].at[j], self.b_buf[p].at[j], self.me).wait_recv()
                self.r_buf[p][...] = self.r_buf[p][...] + self.b_buf[p][j].astype(F32)
            self._store(p).start()
            self._finished(p).start()

    def finish_wait(self):
        for p, (_, _, _, axis, size) in enumerate(self.params):
            theirs = _half(self.g_out[p], axis, size, 1 - self.c)
            self.copy(5 * p + 4, theirs, theirs, self.me).wait_recv()
            self._store(p).wait()
            self._to_sibling(p).wait_send()
            for j in range(3):
                self._pair_sum(p, j).wait_send()
            self._finished(p).wait_send()


class _SmallSum:
    def __init__(self, rows):
        self.rows = rows
        self.scratch = [pltpu.VMEM((N_DEV, rows, LANES), F32),
                        pltpu.SemaphoreType.DMA((N_DEV - 1,)), pltpu.SemaphoreType.DMA((N_DEV - 1,))]
        self.out_shape = jax.ShapeDtypeStruct((rows, LANES), F32)
        self.spec_shape = (rows, LANES)

    def bind(self, src, dst, scratch):
        self.src, self.dst = src, dst
        self.buf, send_sems, recv_sems = scratch
        self.copy = _remote_copier(send_sems, recv_sems)
        self.x, self.y, self.c = lax.axis_index("x"), lax.axis_index("y"), lax.axis_index("c")

    def _send(self, f):
        fx, fy, fc = [(a, b, d) for a in (0, 1) for b in (0, 1) for d in (0, 1)][f]
        x, y, c = self.x, self.y, self.c
        peer = (1 - x if fx else x, 1 - y if fy else y, 1 - c if fc else c)
        return self.copy(f - 1, self.src, self.buf.at[f], peer)

    def start(self):
        for f in range(1, N_DEV):
            self._send(f).start()
        self.buf[0] = self.src[...]

    def finish(self):
        me = (self.x, self.y, self.c)
        for f in range(1, N_DEV):
            self.copy(f - 1, self.buf.at[f], self.buf.at[f], me).wait_recv()
        dev = 4 * self.x + 2 * self.y + self.c
        total = self.buf[dev]
        for d in range(1, N_DEV):
            total = total + self.buf[jnp.bitwise_xor(dev, d)]
        self.dst[...] = total
        for f in range(1, N_DEV):
            self._send(f).wait_send()


def _adamw_math(w, g, m, v):
    m = ADAM_B1 * m + (1.0 - ADAM_B1) * g
    v = ADAM_B2 * v + (1.0 - ADAM_B2) * (g * g)
    m_hat = m / (1.0 - ADAM_B1 ** ADAM_STEP)
    v_hat = v / (1.0 - ADAM_B2 ** ADAM_STEP)
    delta = -ADAM_LR * (m_hat / (jnp.sqrt(v_hat) + ADAM_EPS) + ADAM_WD * w)
    return delta, m, v


def _adamw_tiled(w, g, m, v, tm):
    rows, cols = w.shape

    def body(w_ref, g_ref, m_ref, v_ref, d_ref, nm_ref, nv_ref, g_out):
        g = g_ref[...]
        d_ref[...], nm_ref[...], nv_ref[...] = _adamw_math(w_ref[...], g, m_ref[...], v_ref[...])
        g_out[...] = g

    spec = _row_spec(tm, cols)
    return pl.pallas_call(
        body,
        name="adamw_w_in",
        grid=(rows // tm,),
        in_specs=[spec] * 4,
        out_specs=[spec] * 4,
        out_shape=[jax.ShapeDtypeStruct(w.shape, F32)] * 4,
        compiler_params=pltpu.CompilerParams(dimension_semantics=("parallel",), vmem_limit_bytes=VMEM_LIMIT),
    )(w, g, m, v)


def _adamw_many(ws, gs, ms, vs, sq_err):
    n = len(ws)
    gs = list(gs) + [sq_err]
    g_arrays, g_at = [], []
    for g in gs:
        arr, row = g if isinstance(g, tuple) else (g, None)
        k = next((j for j, a in enumerate(g_arrays) if a is arr), len(g_arrays))
        if k == len(g_arrays):
            g_arrays.append(arr)
        g_at.append((k, row))
    n_g = len(g_arrays)

    def body(*refs):
        w_refs, m_refs, v_refs, g_refs, outs = (refs[:n], refs[n:2 * n], refs[2 * n:3 * n], refs[3 * n:3 * n + n_g],
                                                refs[3 * n + n_g:])
        k, row = g_at[n]
        sq = g_refs[k][...] if row is None else g_refs[k][row:row + 8, :]
        outs[4 * n][...] = jnp.full((1, 1), 0.5 * jnp.sum(sq) / D_MODEL, F32)
        for i in range(n):
            k, row = g_at[i]
            if row is None and g_refs[k].ndim == 2 and w_refs[i].ndim == 3:
                cols = ws[i].shape[2]
                for h in range(ws[i].shape[1]):
                    outs[3 * n + i][:, h, :] = g_refs[k][:, h * cols:(h + 1) * cols]
                g = outs[3 * n + i][...]
            else:
                g = g_refs[k][...] if row is None else g_refs[k][row:row + ws[i].shape[0], :]
                outs[3 * n + i][...] = g
            d, nm, nv = _adamw_math(w_refs[i][...], g, m_refs[i][...], v_refs[i][...])
            outs[i][...] = d
            outs[n + i][...] = nm
            outs[2 * n + i][...] = nv

    vmem_spec = pl.BlockSpec(memory_space=pltpu.VMEM)
    shapes = [jax.ShapeDtypeStruct(w.shape, F32) for w in ws]
    out = pl.pallas_call(
        body,
        name="adamw_small",
        in_specs=[vmem_spec] * (3 * n + n_g),
        out_specs=[vmem_spec] * (4 * n + 1),
        out_shape=shapes * 4 + [jax.ShapeDtypeStruct((1, 1), F32)],
        compiler_params=pltpu.CompilerParams(vmem_limit_bytes=VMEM_LIMIT),
    )(*ws, *ms, *vs, *g_arrays)
    return out[:n], out[n:2 * n], out[2 * n:3 * n], out[3 * n:4 * n], out[4 * n]


def _rope_tables(s):
    half = QK_ROPE_DIM // 2
    inv_freq = np.float32(ROPE_THETA) ** (-np.arange(half, dtype=np.float32) / np.float32(half))
    ang = (np.arange(s, dtype=np.float32)[:, None] * inv_freq[None, :]).astype(np.float32)
    cos, sin = np.cos(ang.astype(np.float64)).astype(np.float32), np.sin(ang.astype(np.float64)).astype(np.float32)
    z16 = np.zeros((s, half), np.float32)
    z32 = np.zeros((s, HEAD_PAD - QK_NOPE_DIM - QK_ROPE_DIM), np.float32)
    z64 = np.zeros((s, QK_NOPE_DIM), np.float32)
    rc = np.concatenate([np.ones((s, QK_NOPE_DIM), np.float32), cos, cos, z32], axis=1)
    rsa = np.concatenate([z64, -sin, z16, z32], axis=1)
    rsb = np.concatenate([z64, z16, sin, z32], axis=1)
    return jnp.asarray(rc), jnp.asarray(rsa), jnp.asarray(rsb)


def kernel(x, norm_in, w_in, q_norm, w_uq, kv_norm, w_ukv, pool_w, pool_scale, w_branch_attn, w_branch_pool, w_out, norm_final, loss_target, m_norm_in, m_w_in, m_q_norm, m_w_uq, m_kv_norm, m_w_ukv, m_pool_w, m_pool_scale, m_w_branch_attn, m_w_branch_pool, m_w_out, m_norm_final, v_norm_in, v_w_in, v_q_norm, v_w_uq, v_kv_norm, v_w_ukv, v_pool_w, v_pool_scale, v_w_branch_attn, v_w_branch_pool, v_w_out, v_norm_final):
    s = x.shape[1]
    t_att, t_row = _tiles(s)
    x2 = x.reshape(s, D_MODEL)
    tgt = loss_target.reshape(s, D_MODEL)

    cx, cy = lax.axis_index("x"), lax.axis_index("y")
    others = [2 * ox + oy for ox, oy in _other_chips(cx, cy)]
    hn, z_sh, (w_in_t, w_uq_all, w_ukv_all) = _inproj_fwd(
        jnp.stack([2 * cx + cy, others[1], others[2], others[0]]).astype(jnp.int32), x2, norm_in.reshape(1, -1),
        w_in.T.astype(BF16), [w_uq, w_ukv], 4 * t_row)
    rc, rsa, rsb = _rope_tables(s)
    g_in = norm_in.reshape(1, -1)
    g_q = q_norm.reshape(-1, LANES)
    g_kv = kv_norm.reshape(1, -1)
    g_f = norm_final.reshape(1, -1)
    ps = pool_scale.reshape(1, -1)

    q, k, v, q_t, v_t, wuq_p, wk_p, wv, *late, pw_bf = _qkv_fwd(
        z_sh, g_q, g_kv, w_uq_all, w_ukv_all, rc, rsa, rsb, [w_branch_attn, w_branch_pool, w_out, pool_w], 2 * t_row)
    o, lse, (w_ba_all, w_bp_all, w_out_all) = _attn_fwd(q_t, k, v_t, late, t_att)
    w_out_f = w_out_all.reshape(D_MODEL, D_MODEL)

    do, delta, dgattn, dgpool, dgmerge, ddc, dh, gs_mid, d_w_out, d_w_ba, d_w_bp = _mid(o, z_sh, x2, tgt, pw_bf, ps, w_ba_all, w_bp_all, w_out_f, g_f, t_row)

    late_grads = [d_w_ba, d_w_bp, d_w_out.reshape(N_CHIPS, 256, D_MODEL)]
    dq, dk_t, dv_t, (g_w_ba, g_w_bp, g_w_out), g_small_mid = _attn_bwd(q, q_t, k, v, do, lse, delta, late_grads,
                                                                      gs_mid, t_att)
    dzq, dzkv, dzkr, d_w_uq, d_w_ukv, d_q_norm, d_kv_norm = _qkv_bwd(
        dq, dk_t, dv_t, z_sh, g_q, g_kv, wuq_p, wk_p, wv, rc, rsa, rsb, 2 * t_row)
    grad_x, gs, dz_sh = _inproj_bwd_x(dzq, dzkv, dzkr, dgattn, ddc, dgpool, dgmerge, x2, dh, g_in, d_q_norm, d_kv_norm,
                                      w_in_t, 2 * t_row)

    order = jnp.stack(others + [2 * cx + cy]).astype(jnp.int32)
    g_w_in_t, g_w_uq, g_w_ukv, g_small = _inproj_bwd_w(
        order, dz_sh, hn, d_w_uq, d_w_ukv, gs, 4 * t_row)

    dl_w_in, nm_w_in, nv_w_in, g_w_in = (a.T for a in _adamw_tiled(w_in.T, g_w_in_t, m_w_in.T, v_w_in.T, 152))

    packed = {n: (g_small_mid, r) for n, r in _first_rows(SMALL_MID).items() if n != "sq_err"}
    packed.update({n: (g_small, r) for n, r in _first_rows(SMALL_LATE).items()})

    def as_rows(n, a):
        return a.reshape(-1, LANES) if n in packed else a

    names = ["norm_in", "q_norm", "w_uq", "kv_norm", "w_ukv", "pool_w", "pool_scale", "w_branch_attn",
             "w_branch_pool", "w_out", "norm_final"]
    ws = dict(norm_in=norm_in, q_norm=q_norm, w_uq=w_uq, kv_norm=kv_norm, w_ukv=w_ukv, pool_w=pool_w,
              pool_scale=pool_scale, w_branch_attn=w_branch_attn, w_branch_pool=w_branch_pool, w_out=w_out,
              norm_final=norm_final)
    gsd = dict(packed, w_uq=g_w_uq, w_ukv=g_w_ukv, w_branch_attn=g_w_ba, w_branch_pool=g_w_bp, w_out=g_w_out)
    msd = dict(norm_in=m_norm_in, q_norm=m_q_norm, w_uq=m_w_uq, kv_norm=m_kv_norm, w_ukv=m_w_ukv, pool_w=m_pool_w,
               pool_scale=m_pool_scale, w_branch_attn=m_w_branch_attn, w_branch_pool=m_w_branch_pool, w_out=m_w_out,
               norm_final=m_norm_final)
    vsd = dict(norm_in=v_norm_in, q_norm=v_q_norm, w_uq=v_w_uq, kv_norm=v_kv_norm, w_ukv=v_w_ukv, pool_w=v_pool_w,
               pool_scale=v_pool_scale, w_branch_attn=v_w_branch_attn, w_branch_pool=v_w_branch_pool, w_out=v_w_out,
               norm_final=v_norm_final)
    dls, nms, nvs, g_outs, loss = _adamw_many(
        [as_rows(n, ws[n]) for n in names], [gsd[n] for n in names], [as_rows(n, msd[n]) for n in names],
        [as_rows(n, vsd[n]) for n in names], (g_small_mid, _first_rows(SMALL_MID)["sq_err"]))

    grads = dict(zip(names, g_outs))
    grads["w_in"] = g_w_in
    delta_w = {n: d.reshape(ws[n].shape) for n, d in zip(names, dls)}
    new_m = {n: d.reshape(ws[n].shape) for n, d in zip(names, nms)}
    new_v = {n: d.reshape(ws[n].shape) for n, d in zip(names, nvs)}
    delta_w["w_in"], new_m["w_in"], new_v["w_in"] = dl_w_in, nm_w_in, nv_w_in
    ws["w_in"] = w_in

    order = ["norm_in", "w_in", "q_norm", "w_uq", "kv_norm", "w_ukv", "pool_w", "pool_scale", "w_branch_attn",
             "w_branch_pool", "w_out", "norm_final"]
    return (loss.reshape(()), grad_x.reshape(x.shape),
            *[grads[n].reshape(ws[n].shape) for n in order],
            *[delta_w[n] for n in order], *[new_m[n] for n in order], *[new_v[n] for n in order])
```
